```python
import math
import jax
import jax.numpy as jnp
from jax import lax
import numpy as np


D_MODEL = 1024
BATCH = 16
SEQ = 2048
DEPTH = 2

N_A_LAYERS = DEPTH // 2
N_B_LAYERS = DEPTH - N_A_LAYERS

SSM_GROUP = 16
SSM_GROUPS = D_MODEL // SSM_GROUP
SSM_STATE = 64
DT_MIN = 0.001
DT_MAX = 0.1

HEAD_DIM = 64
HEADS_PER_GROUP = D_MODEL // HEAD_DIM
DILATED_GROUPS = ((128, 1), (512, 4), (2048, 16))
N_DIL = len(DILATED_GROUPS)
BAND = DILATED_GROUPS[0][0] // DILATED_GROUPS[0][1]
MAX_DIL = max(d for _, d in DILATED_GROUPS)
ATT_WIDTH = N_DIL * HEADS_PER_GROUP * HEAD_DIM
MERGED_WIDTH = HEADS_PER_GROUP * HEAD_DIM
NEG_BIG = -1e30

REL_BUCKETS = 32
REL_MAX_DIST = 2048

D_FF = 2816
CONV_WIDTH = 3

DN_ALPHA = (2.0 * DEPTH) ** 0.25
DN_BETA = (8.0 * DEPTH) ** -0.25
LN_EPS = 1e-5

kernel_name = 'yoco_s5_dilated_attn_deepnorm_trunk'


def layer_norm(x, gain, bias):
    xf = x.astype(jnp.float32)
    mu = jnp.mean(xf, axis=-1, keepdims=True)
    var = jnp.mean(jnp.square(xf - mu), axis=-1, keepdims=True)
    y = (xf - mu) * lax.rsqrt(var + LN_EPS) * gain.astype(jnp.float32) + bias.astype(jnp.float32)
    return y.astype(x.dtype)


def post_norm(x, f, gain, bias):
    return layer_norm(DN_ALPHA * x + f.astype(x.dtype), gain, bias)


def _complex_affine_combine(e1, e2):
    a1r, a1i, b1r, b1i = e1
    a2r, a2i, b2r, b2i = e2
    return (a2r * a1r - a2i * a1i,
            a2r * a1i + a2i * a1r,
            a2r * b1r - a2i * b1i + b2r,
            a2r * b1i + a2i * b1r + b2i)


def s5_mixer(x, lam_re, lam_im, log_dt, b_re, b_im, c_re, c_im, d_skip, w_glu, b_glu, w_out):
    f32 = jnp.float32
    bsz, seq, _ = x.shape
    u = x.astype(f32).reshape(bsz, seq, SSM_GROUPS, SSM_GROUP)
    lr, li = lam_re.astype(f32), lam_im.astype(f32)
    dt = jnp.exp(log_dt.astype(f32))[:, None]
    mag = jnp.exp(lr * dt)
    ab_r, ab_i = mag * jnp.cos(li * dt), mag * jnp.sin(li * dt)
    den = lr * lr + li * li
    nr = ab_r - 1.0
    co_r = (nr * lr + ab_i * li) / den
    co_i = (ab_i * lr - nr * li) / den
    br, bi = b_re.astype(f32), b_im.astype(f32)
    bb_r = co_r[..., None] * br - co_i[..., None] * bi
    bb_i = co_r[..., None] * bi + co_i[..., None] * br
    bu_r = jnp.einsum('blgh,gph->blgp', u, bb_r)
    bu_i = jnp.einsum('blgh,gph->blgp', u, bb_i)
    a_r = jnp.broadcast_to(ab_r, (1, seq, SSM_GROUPS, SSM_STATE))
    a_i = jnp.broadcast_to(ab_i, (1, seq, SSM_GROUPS, SSM_STATE))
    _, _, h_r, h_i = lax.associative_scan(_complex_affine_combine, (a_r, a_i, bu_r, bu_i), axis=1)
    y = (jnp.einsum('blgp,ghp->blgh', h_r, c_re.astype(f32))
         - jnp.einsum('blgp,ghp->blgh', h_i, c_im.astype(f32))
         + d_skip.astype(f32) * u)
    y = jax.nn.gelu(y.reshape(bsz, seq, D_MODEL))
    g = y * jax.nn.sigmoid(y @ w_glu.astype(f32) + b_glu.astype(f32))
    return (g @ w_out.astype(f32)).astype(x.dtype)


def conv_glu_ffn(x, w_up, conv_w, conv_b, w_down):
    seq = x.shape[1]
    hcat = x @ w_up
    hp = jnp.pad(hcat, ((0, 0), (CONV_WIDTH - 1, 0), (0, 0)))
    hcat = conv_b + sum(conv_w[k] * hp[:, CONV_WIDTH - 1 - k:CONV_WIDTH - 1 - k + seq]
                        for k in range(CONV_WIDTH))
    val, gate = jnp.split(hcat, 2, axis=-1)
    return (jax.nn.silu(gate) * val) @ w_down


def _padded_len(seq):
    span = BAND * MAX_DIL
    return -(-seq // span) * span


def _to_residue_blocks(t, dil):
    bsz, lp, h, e = t.shape
    m = lp // dil
    t = t.reshape(bsz, m, dil, h, e).transpose(0, 2, 1, 3, 4)
    return t.reshape(bsz, dil, m // BAND, BAND, h, e)


def _from_residue_blocks(t, dil):
    bsz = t.shape[0]
    rest = t.shape[4:]
    t = t.reshape((bsz, dil, -1) + rest)
    t = jnp.moveaxis(t, 1, 2)
    return t.reshape((bsz, -1) + rest)


def _with_previous_block(t):
    prev = jnp.concatenate([jnp.zeros_like(t[:, :, :1]), t[:, :, :-1]], axis=2)
    return jnp.concatenate([prev, t], axis=3)


def _t5_bucket(dist):
    exact = REL_BUCKETS // 2
    d = np.maximum(dist, 1).astype(np.float32)
    large = exact + (np.log(d / exact) / math.log(REL_MAX_DIST / exact)
                     * (REL_BUCKETS - exact)).astype(np.int64)
    large = np.minimum(large, REL_BUCKETS - 1)
    return np.where(dist < exact, dist, large).astype(np.int32)


def _group_bias_mask(rel_bias, g, dil, n_blocks):
    steps = np.arange(BAND)[:, None] + BAND - np.arange(2 * BAND)[None, :]
    bucket = _t5_bucket(np.maximum(steps, 0) * dil)
    cols = rel_bias[:, g * HEADS_PER_GROUP:(g + 1) * HEADS_PER_GROUP]
    bias = jnp.transpose(cols[bucket], (2, 0, 1)).astype(jnp.float32)
    in_band = (steps >= 0) & (steps <= BAND)
    has_prev = (np.arange(n_blocks)[:, None, None] > 0) | (np.arange(2 * BAND)[None, None, :] >= BAND)
    valid = jnp.asarray(in_band[None] & has_prev)
    return bias, valid


def shared_kv(h, w_kv):
    bsz, seq, _ = h.shape
    lp = _padded_len(seq)
    kv = (h @ w_kv).astype(jnp.float32)
    kv = jnp.pad(kv, ((0, 0), (0, lp - seq), (0, 0)))
    kv = kv.reshape(bsz, lp, 2, N_DIL, HEADS_PER_GROUP, HEAD_DIM)
    blocks = []
    for g, (_, dil) in enumerate(DILATED_GROUPS):
        blocks.append(_with_previous_block(_to_residue_blocks(kv[:, :, 0, g], dil)))
        blocks.append(_with_previous_block(_to_residue_blocks(kv[:, :, 1, g], dil)))
    return blocks


def dilated_attention(h, w_q, w_out, rel_bias, kv_blocks):
    bsz, seq, _ = h.shape
    lp = _padded_len(seq)
    q = (h @ w_q).astype(jnp.float32) * (HEAD_DIM ** -0.5)
    q = jnp.pad(q, ((0, 0), (0, lp - seq), (0, 0))).reshape(bsz, lp, N_DIL, HEADS_PER_GROUP, HEAD_DIM)
    outs, lses = [], []
    for g, (_, dil) in enumerate(DILATED_GROUPS):
        kb, vb = kv_blocks[2 * g], kv_blocks[2 * g + 1]
        qb = _to_residue_blocks(q[:, :, g], dil)
        bias, valid = _group_bias_mask(rel_bias, g, dil, qb.shape[2])
        s = jnp.einsum('brnqhe,brnkhe->brnhqk', qb, kb) + bias
        s = jnp.where(valid[:, None], s, NEG_BIG)
        lse = jax.nn.logsumexp(s, axis=-1)
        p = jnp.exp(s - lse[..., None])
        o = jnp.einsum('brnhqk,brnkhe->brnqhe', p, vb)
        outs.append(_from_residue_blocks(o, dil)[:, :seq])
        lses.append(_from_residue_blocks(jnp.swapaxes(lse, -1, -2), dil)[:, :seq])
    wts = jax.nn.softmax(jnp.stack(lses), axis=0)
    o = jnp.einsum('gblh,gblhe->blhe', wts, jnp.stack(outs))
    return o.reshape(bsz, seq, MERGED_WIDTH).astype(h.dtype) @ w_out


def _fwd_setup_inputs(seed: int = 0) -> dict:
    key = jax.random.key(seed)
    ks = jax.random.split(key, 24)
    f32 = jnp.float32
    na, nbl, g, p, gs = N_A_LAYERS, N_B_LAYERS, SSM_GROUPS, SSM_STATE, SSM_GROUP

    def nrm(k, shape, scale):
        return jax.random.normal(k, shape, f32) * scale

    x = nrm(ks[0], (BATCH, SEQ, D_MODEL), 1.0)
    s5_lam_re = -0.5 + nrm(ks[1], (na, g, p), 0.01)
    s5_lam_im = math.pi * jnp.arange(p, dtype=f32) + nrm(ks[2], (na, g, p), 0.01)
    s5_log_dt = jax.random.uniform(ks[3], (na, g), f32, math.log(DT_MIN), math.log(DT_MAX))
    s5_b_re = nrm(ks[4], (na, g, p, gs), (2.0 * gs) ** -0.5)
    s5_b_im = nrm(ks[5], (na, g, p, gs), (2.0 * gs) ** -0.5)
    s5_c_re = nrm(ks[6], (na, g, gs, p), p ** -0.5)
    s5_c_im = nrm(ks[7], (na, g, gs, p), p ** -0.5)
    s5_d = nrm(ks[8], (na, g, gs), 1.0)
    s5_w_glu = nrm(ks[9], (na, D_MODEL, D_MODEL), D_MODEL ** -0.5)
    s5_b_glu = nrm(ks[10], (na, D_MODEL), 0.01)
    s5_w_out = nrm(ks[11], (na, D_MODEL, D_MODEL), D_MODEL ** -0.5 * DN_BETA)
    w_k = nrm(ks[12], (D_MODEL, ATT_WIDTH), D_MODEL ** -0.5)
    w_v = nrm(ks[13], (D_MODEL, ATT_WIDTH), D_MODEL ** -0.5 * DN_BETA)
    attn_w_kv = jnp.concatenate([w_k, w_v], axis=1)
    attn_w_q = nrm(ks[14], (nbl, D_MODEL, ATT_WIDTH), D_MODEL ** -0.5)
    attn_w_out = nrm(ks[15], (nbl, MERGED_WIDTH, D_MODEL), MERGED_WIDTH ** -0.5 * DN_BETA)
    rel_bias = nrm(ks[16], (REL_BUCKETS, N_DIL * HEADS_PER_GROUP), 0.5)
    ffn_w_up = nrm(ks[17], (DEPTH, D_MODEL, 2 * D_FF), D_MODEL ** -0.5 * DN_BETA)
    ffn_conv_w = nrm(ks[18], (DEPTH, CONV_WIDTH, 2 * D_FF), CONV_WIDTH ** -0.5)
    ffn_conv_b = nrm(ks[19], (DEPTH, 2 * D_FF), 0.01)
    ffn_w_down = nrm(ks[20], (DEPTH, D_FF, D_MODEL), D_FF ** -0.5 * DN_BETA)
    ln_gain = 1.0 + nrm(ks[21], (DEPTH, 2, D_MODEL), 0.01)
    ln_bias = nrm(ks[22], (DEPTH, 2, D_MODEL), 0.01)
    return {'x': x, 's5_lam_re': s5_lam_re, 's5_lam_im': s5_lam_im, 's5_log_dt': s5_log_dt,
            's5_b_re': s5_b_re, 's5_b_im': s5_b_im, 's5_c_re': s5_c_re, 's5_c_im': s5_c_im,
            's5_d': s5_d, 's5_w_glu': s5_w_glu, 's5_b_glu': s5_b_glu, 's5_w_out': s5_w_out,
            'attn_w_kv': attn_w_kv, 'attn_w_q': attn_w_q, 'attn_w_out': attn_w_out,
            'rel_bias': rel_bias, 'ffn_w_up': ffn_w_up, 'ffn_conv_w': ffn_conv_w,
            'ffn_conv_b': ffn_conv_b, 'ffn_w_down': ffn_w_down, 'ln_gain': ln_gain, 'ln_bias': ln_bias}


def _fwd_reference(x, s5_lam_re, s5_lam_im, s5_log_dt, s5_b_re, s5_b_im, s5_c_re, s5_c_im,
              s5_d, s5_w_glu, s5_b_glu, s5_w_out, attn_w_kv, attn_w_q, attn_w_out,
              rel_bias, ffn_w_up, ffn_conv_w, ffn_conv_b, ffn_w_down, ln_gain, ln_bias):
    h = x
    kv_blocks = None
    for layer in range(DEPTH):
        if layer < N_A_LAYERS:
            i = layer
            mix = s5_mixer(h, s5_lam_re[i], s5_lam_im[i], s5_log_dt[i], s5_b_re[i], s5_b_im[i],
                           s5_c_re[i], s5_c_im[i], s5_d[i], s5_w_glu[i], s5_b_glu[i], s5_w_out[i])
        else:
            j = layer - N_A_LAYERS
            mix = dilated_attention(h, attn_w_q[j], attn_w_out[j], rel_bias, kv_blocks)
        h = post_norm(h, mix, ln_gain[layer, 0], ln_bias[layer, 0])
        ffn = conv_glu_ffn(h, ffn_w_up[layer], ffn_conv_w[layer], ffn_conv_b[layer], ffn_w_down[layer])
        h = post_norm(h, ffn, ln_gain[layer, 1], ln_bias[layer, 1])
        if layer == N_A_LAYERS - 1:
            kv_blocks = shared_kv(h, attn_w_kv)
    return h


import jax as _jax
import jax.numpy as _jnp

TWIN_FORMAT = 'train_step'
FWD_PARAMS = ['x', 's5_lam_re', 's5_lam_im', 's5_log_dt', 's5_b_re', 's5_b_im', 's5_c_re', 's5_c_im', 's5_d', 's5_w_glu', 's5_b_glu', 's5_w_out', 'attn_w_kv', 'attn_w_q', 'attn_w_out', 'rel_bias', 'ffn_w_up', 'ffn_conv_w', 'ffn_conv_b', 'ffn_w_down', 'ln_gain', 'ln_bias']
TWIN_WEIGHTS = ['s5_lam_re', 's5_lam_im', 's5_log_dt', 's5_b_re', 's5_b_im', 's5_c_re', 's5_c_im', 's5_d', 's5_w_glu', 's5_b_glu', 's5_w_out', 'attn_w_kv', 'attn_w_q', 'attn_w_out', 'rel_bias', 'ffn_w_up', 'ffn_conv_w', 'ffn_conv_b', 'ffn_w_down', 'ln_gain', 'ln_bias']
TWIN_DIFF_INPUT = 'x'
TWIN_INPUTS = ['x', 's5_lam_re', 's5_lam_im', 's5_log_dt', 's5_b_re', 's5_b_im', 's5_c_re', 's5_c_im', 's5_d', 's5_w_glu', 's5_b_glu', 's5_w_out', 'attn_w_kv', 'attn_w_q', 'attn_w_out', 'rel_bias', 'ffn_w_up', 'ffn_conv_w', 'ffn_conv_b', 'ffn_w_down', 'ln_gain', 'ln_bias', 'loss_target', 'm_s5_lam_re', 'm_s5_lam_im', 'm_s5_log_dt', 'm_s5_b_re', 'm_s5_b_im', 'm_s5_c_re', 'm_s5_c_im', 'm_s5_d', 'm_s5_w_glu', 'm_s5_b_glu', 'm_s5_w_out', 'm_attn_w_kv', 'm_attn_w_q', 'm_attn_w_out', 'm_rel_bias', 'm_ffn_w_up', 'm_ffn_conv_w', 'm_ffn_conv_b', 'm_ffn_w_down', 'm_ln_gain', 'm_ln_bias', 'v_s5_lam_re', 'v_s5_lam_im', 'v_s5_log_dt', 'v_s5_b_re', 'v_s5_b_im', 'v_s5_c_re', 'v_s5_c_im', 'v_s5_d', 'v_s5_w_glu', 'v_s5_b_glu', 'v_s5_w_out', 'v_attn_w_kv', 'v_attn_w_q', 'v_attn_w_out', 'v_rel_bias', 'v_ffn_w_up', 'v_ffn_conv_w', 'v_ffn_conv_b', 'v_ffn_w_down', 'v_ln_gain', 'v_ln_bias']
TWIN_OUTPUTS = ['loss', 'grad_x', 'grad_s5_lam_re', 'grad_s5_lam_im', 'grad_s5_log_dt', 'grad_s5_b_re', 'grad_s5_b_im', 'grad_s5_c_re', 'grad_s5_c_im', 'grad_s5_d', 'grad_s5_w_glu', 'grad_s5_b_glu', 'grad_s5_w_out', 'grad_attn_w_kv', 'grad_attn_w_q', 'grad_attn_w_out', 'grad_rel_bias', 'grad_ffn_w_up', 'grad_ffn_conv_w', 'grad_ffn_conv_b', 'grad_ffn_w_down', 'grad_ln_gain', 'grad_ln_bias', 'delta_s5_lam_re', 'delta_s5_lam_im', 'delta_s5_log_dt', 'delta_s5_b_re', 'delta_s5_b_im', 'delta_s5_c_re', 'delta_s5_c_im', 'delta_s5_d', 'delta_s5_w_glu', 'delta_s5_b_glu', 'delta_s5_w_out', 'delta_attn_w_kv', 'delta_attn_w_q', 'delta_attn_w_out', 'delta_rel_bias', 'delta_ffn_w_up', 'delta_ffn_conv_w', 'delta_ffn_conv_b', 'delta_ffn_w_down', 'delta_ln_gain', 'delta_ln_bias', 'new_m_s5_lam_re', 'new_m_s5_lam_im', 'new_m_s5_log_dt', 'new_m_s5_b_re', 'new_m_s5_b_im', 'new_m_s5_c_re', 'new_m_s5_c_im', 'new_m_s5_d', 'new_m_s5_w_glu', 'new_m_s5_b_glu', 'new_m_s5_w_out', 'new_m_attn_w_kv', 'new_m_attn_w_q', 'new_m_attn_w_out', 'new_m_rel_bias', 'new_m_ffn_w_up', 'new_m_ffn_conv_w', 'new_m_ffn_conv_b', 'new_m_ffn_w_down', 'new_m_ln_gain', 'new_m_ln_bias', 'new_v_s5_lam_re', 'new_v_s5_lam_im', 'new_v_s5_log_dt', 'new_v_s5_b_re', 'new_v_s5_b_im', 'new_v_s5_c_re', 'new_v_s5_c_im', 'new_v_s5_d', 'new_v_s5_w_glu', 'new_v_s5_b_glu', 'new_v_s5_w_out', 'new_v_attn_w_kv', 'new_v_attn_w_q', 'new_v_attn_w_out', 'new_v_rel_bias', 'new_v_ffn_w_up', 'new_v_ffn_conv_w', 'new_v_ffn_conv_b', 'new_v_ffn_w_down', 'new_v_ln_gain', 'new_v_ln_bias']
TWIN_LEAF_KINDS = {'loss': 'loss', 'grad_x': 'grad_x', 'grad_s5_lam_re': 'grad_w', 'grad_s5_lam_im': 'grad_w', 'grad_s5_log_dt': 'grad_w', 'grad_s5_b_re': 'grad_w', 'grad_s5_b_im': 'grad_w', 'grad_s5_c_re': 'grad_w', 'grad_s5_c_im': 'grad_w', 'grad_s5_d': 'grad_w', 'grad_s5_w_glu': 'grad_w', 'grad_s5_b_glu': 'grad_w', 'grad_s5_w_out': 'grad_w', 'grad_attn_w_kv': 'grad_w', 'grad_attn_w_q': 'grad_w', 'grad_attn_w_out': 'grad_w', 'grad_rel_bias': 'grad_w', 'grad_ffn_w_up': 'grad_w', 'grad_ffn_conv_w': 'grad_w', 'grad_ffn_conv_b': 'grad_w', 'grad_ffn_w_down': 'grad_w', 'grad_ln_gain': 'grad_w', 'grad_ln_bias': 'grad_w', 'delta_s5_lam_re': 'delta_w', 'delta_s5_lam_im': 'delta_w', 'delta_s5_log_dt': 'delta_w', 'delta_s5_b_re': 'delta_w', 'delta_s5_b_im': 'delta_w', 'delta_s5_c_re': 'delta_w', 'delta_s5_c_im': 'delta_w', 'delta_s5_d': 'delta_w', 'delta_s5_w_glu': 'delta_w', 'delta_s5_b_glu': 'delta_w', 'delta_s5_w_out': 'delta_w', 'delta_attn_w_kv': 'delta_w', 'delta_attn_w_q': 'delta_w', 'delta_attn_w_out': 'delta_w', 'delta_rel_bias': 'delta_w', 'delta_ffn_w_up': 'delta_w', 'delta_ffn_conv_w': 'delta_w', 'delta_ffn_conv_b': 'delta_w', 'delta_ffn_w_down': 'delta_w', 'delta_ln_gain': 'delta_w', 'delta_ln_bias': 'delta_w', 'new_m_s5_lam_re': 'new_m', 'new_m_s5_lam_im': 'new_m', 'new_m_s5_log_dt': 'new_m', 'new_m_s5_b_re': 'new_m', 'new_m_s5_b_im': 'new_m', 'new_m_s5_c_re': 'new_m', 'new_m_s5_c_im': 'new_m', 'new_m_s5_d': 'new_m', 'new_m_s5_w_glu': 'new_m', 'new_m_s5_b_glu': 'new_m', 'new_m_s5_w_out': 'new_m', 'new_m_attn_w_kv': 'new_m', 'new_m_attn_w_q': 'new_m', 'new_m_attn_w_out': 'new_m', 'new_m_rel_bias': 'new_m', 'new_m_ffn_w_up': 'new_m', 'new_m_ffn_conv_w': 'new_m', 'new_m_ffn_conv_b': 'new_m', 'new_m_ffn_w_down': 'new_m', 'new_m_ln_gain': 'new_m', 'new_m_ln_bias': 'new_m', 'new_v_s5_lam_re': 'new_v', 'new_v_s5_lam_im': 'new_v', 'new_v_s5_log_dt': 'new_v', 'new_v_s5_b_re': 'new_v', 'new_v_s5_b_im': 'new_v', 'new_v_s5_c_re': 'new_v', 'new_v_s5_c_im': 'new_v', 'new_v_s5_d': 'new_v', 'new_v_s5_w_glu': 'new_v', 'new_v_s5_b_glu': 'new_v', 'new_v_s5_w_out': 'new_v', 'new_v_attn_w_kv': 'new_v', 'new_v_attn_w_q': 'new_v', 'new_v_attn_w_out': 'new_v', 'new_v_rel_bias': 'new_v', 'new_v_ffn_w_up': 'new_v', 'new_v_ffn_conv_w': 'new_v', 'new_v_ffn_conv_b': 'new_v', 'new_v_ffn_w_down': 'new_v', 'new_v_ln_gain': 'new_v', 'new_v_ln_bias': 'new_v'}


def _forward(args):
    return _fwd_reference(*[args[k] for k in FWD_PARAMS])


def _output_shape():
    out = _jax.eval_shape(lambda: _forward(_fwd_setup_inputs(0)))
    return out.shape, out.dtype

N_MICROBATCH = 1
ADAM_LR = 0.001
ADAM_B1 = 0.9
ADAM_B2 = 0.999
ADAM_EPS = 1e-08
ADAM_WD = 0.01
ADAM_STEP = 10
PER_EXAMPLE_BATCH_AXIS = {'x': 0, 'loss_target': 0}
SHARED_INPUTS = []
_WEIGHT_DTYPES = {'s5_lam_re': _jnp.float32, 's5_lam_im': _jnp.float32, 's5_log_dt': _jnp.float32, 's5_b_re': _jnp.float32, 's5_b_im': _jnp.float32, 's5_c_re': _jnp.float32, 's5_c_im': _jnp.float32, 's5_d': _jnp.float32, 's5_w_glu': _jnp.float32, 's5_b_glu': _jnp.float32, 's5_w_out': _jnp.float32, 'attn_w_kv': _jnp.float32, 'attn_w_q': _jnp.float32, 'attn_w_out': _jnp.float32, 'rel_bias': _jnp.float32, 'ffn_w_up': _jnp.float32, 'ffn_conv_w': _jnp.float32, 'ffn_conv_b': _jnp.float32, 'ffn_w_down': _jnp.float32, 'ln_gain': _jnp.float32, 'ln_bias': _jnp.float32}
MOMENT_SCALE = {'s5_lam_re': 1.765542e-03, 's5_lam_im': 1.923623e-03, 's5_log_dt': 1.635434e+00, 's5_b_re': 1.079067e-03, 's5_b_im': 1.101834e-03, 's5_c_re': 1.548686e-03, 's5_c_im': 1.533539e-03, 's5_d': 2.832937e-02, 's5_w_glu': 6.412817e-03, 's5_b_glu': 1.162913e-02, 's5_w_out': 4.560649e-02, 'attn_w_kv': 5.819256e-03, 'attn_w_q': 3.200761e-03, 'attn_w_out': 1.308951e-02, 'rel_bias': 4.103565e-03, 'ffn_w_up': 1.033675e-02, 'ffn_conv_w': 5.246643e-03, 'ffn_conv_b': 1.027031e-02, 'ffn_w_down': 1.703920e-02, 'ln_gain': 1.599065e+01, 'ln_bias': 7.800077e-01}


def _to_microbatches(a, axis):
    t = _jnp.moveaxis(a, axis, 0)
    t = t.reshape((N_MICROBATCH, t.shape[0] // N_MICROBATCH) + t.shape[1:])
    return _jnp.moveaxis(t, 1, axis + 1)


def setup_inputs(seed: int = 0) -> dict:
    inp = _fwd_setup_inputs(seed)
    key = _jax.random.fold_in(_jax.random.key(seed), 7919)
    shape, _ = _output_shape()
    out = dict(inp)
    out["loss_target"] = _jax.random.normal(_jax.random.fold_in(key, 0), shape, _jnp.float32)
    for i, name in enumerate(TWIN_WEIGHTS):
        w = inp[name].astype(_jnp.float32)
        if MOMENT_SCALE is None:
            s = _jnp.sqrt(_jnp.mean(_jnp.square(w)) + 1e-30)
        else:
            s = MOMENT_SCALE[name]
        km, kv = _jax.random.split(_jax.random.fold_in(key, i + 1))
        out[name] = w
        out["m_" + name] = s * _jax.random.normal(km, w.shape, _jnp.float32)
        out["v_" + name] = (s * s) * _jax.random.uniform(kv, w.shape, _jnp.float32, 0.5, 1.5)
    if N_MICROBATCH > 1:
        for name, axis in PER_EXAMPLE_BATCH_AXIS.items():
            out[name] = _to_microbatches(out[name], axis)
    return {'x': out['x'], 's5_lam_re': out['s5_lam_re'], 's5_lam_im': out['s5_lam_im'], 's5_log_dt': out['s5_log_dt'], 's5_b_re': out['s5_b_re'], 's5_b_im': out['s5_b_im'], 's5_c_re': out['s5_c_re'], 's5_c_im': out['s5_c_im'], 's5_d': out['s5_d'], 's5_w_glu': out['s5_w_glu'], 's5_b_glu': out['s5_b_glu'], 's5_w_out': out['s5_w_out'], 'attn_w_kv': out['attn_w_kv'], 'attn_w_q': out['attn_w_q'], 'attn_w_out': out['attn_w_out'], 'rel_bias': out['rel_bias'], 'ffn_w_up': out['ffn_w_up'], 'ffn_conv_w': out['ffn_conv_w'], 'ffn_conv_b': out['ffn_conv_b'], 'ffn_w_down': out['ffn_w_down'], 'ln_gain': out['ln_gain'], 'ln_bias': out['ln_bias'], 'loss_target': out['loss_target'], 'm_s5_lam_re': out['m_s5_lam_re'], 'm_s5_lam_im': out['m_s5_lam_im'], 'm_s5_log_dt': out['m_s5_log_dt'], 'm_s5_b_re': out['m_s5_b_re'], 'm_s5_b_im': out['m_s5_b_im'], 'm_s5_c_re': out['m_s5_c_re'], 'm_s5_c_im': out['m_s5_c_im'], 'm_s5_d': out['m_s5_d'], 'm_s5_w_glu': out['m_s5_w_glu'], 'm_s5_b_glu': out['m_s5_b_glu'], 'm_s5_w_out': out['m_s5_w_out'], 'm_attn_w_kv': out['m_attn_w_kv'], 'm_attn_w_q': out['m_attn_w_q'], 'm_attn_w_out': out['m_attn_w_out'], 'm_rel_bias': out['m_rel_bias'], 'm_ffn_w_up': out['m_ffn_w_up'], 'm_ffn_conv_w': out['m_ffn_conv_w'], 'm_ffn_conv_b': out['m_ffn_conv_b'], 'm_ffn_w_down': out['m_ffn_w_down'], 'm_ln_gain': out['m_ln_gain'], 'm_ln_bias': out['m_ln_bias'], 'v_s5_lam_re': out['v_s5_lam_re'], 'v_s5_lam_im': out['v_s5_lam_im'], 'v_s5_log_dt': out['v_s5_log_dt'], 'v_s5_b_re': out['v_s5_b_re'], 'v_s5_b_im': out['v_s5_b_im'], 'v_s5_c_re': out['v_s5_c_re'], 'v_s5_c_im': out['v_s5_c_im'], 'v_s5_d': out['v_s5_d'], 'v_s5_w_glu': out['v_s5_w_glu'], 'v_s5_b_glu': out['v_s5_b_glu'], 'v_s5_w_out': out['v_s5_w_out'], 'v_attn_w_kv': out['v_attn_w_kv'], 'v_attn_w_q': out['v_attn_w_q'], 'v_attn_w_out': out['v_attn_w_out'], 'v_rel_bias': out['v_rel_bias'], 'v_ffn_w_up': out['v_ffn_w_up'], 'v_ffn_conv_w': out['v_ffn_conv_w'], 'v_ffn_conv_b': out['v_ffn_conv_b'], 'v_ffn_w_down': out['v_ffn_w_down'], 'v_ln_gain': out['v_ln_gain'], 'v_ln_bias': out['v_ln_bias']}


def _loss(weights, diff, rest, loss_target):
    with _jax.named_scope("forward"):
        args = {**rest, TWIN_DIFF_INPUT: diff, **{k: w.astype(_WEIGHT_DTYPES[k]) for k, w in weights.items()}}
        y = _forward(args)
    with _jax.named_scope("loss_head"):
        err = _jnp.square(y.astype(_jnp.float32) - loss_target)
        return 0.5 * _jnp.sum(_jnp.mean(err, axis=-1)) if err.ndim else 0.5 * err


def _adamw(w, g, m, v):
    m = ADAM_B1 * m + (1.0 - ADAM_B1) * g
    v = ADAM_B2 * v + (1.0 - ADAM_B2) * _jnp.square(g)
    m_hat = m / (1.0 - ADAM_B1 ** ADAM_STEP)
    v_hat = v / (1.0 - ADAM_B2 ** ADAM_STEP)
    delta = -ADAM_LR * (m_hat / (_jnp.sqrt(v_hat) + ADAM_EPS) + ADAM_WD * w)
    return delta, m, v


def reference(x, s5_lam_re, s5_lam_im, s5_log_dt, s5_b_re, s5_b_im, s5_c_re, s5_c_im, s5_d, s5_w_glu, s5_b_glu, s5_w_out, attn_w_kv, attn_w_q, attn_w_out, rel_bias, ffn_w_up, ffn_conv_w, ffn_conv_b, ffn_w_down, ln_gain, ln_bias, loss_target, m_s5_lam_re, m_s5_lam_im, m_s5_log_dt, m_s5_b_re, m_s5_b_im, m_s5_c_re, m_s5_c_im, m_s5_d, m_s5_w_glu, m_s5_b_glu, m_s5_w_out, m_attn_w_kv, m_attn_w_q, m_attn_w_out, m_rel_bias, m_ffn_w_up, m_ffn_conv_w, m_ffn_conv_b, m_ffn_w_down, m_ln_gain, m_ln_bias, v_s5_lam_re, v_s5_lam_im, v_s5_log_dt, v_s5_b_re, v_s5_b_im, v_s5_c_re, v_s5_c_im, v_s5_d, v_s5_w_glu, v_s5_b_glu, v_s5_w_out, v_attn_w_kv, v_attn_w_q, v_attn_w_out, v_rel_bias, v_ffn_w_up, v_ffn_conv_w, v_ffn_conv_b, v_ffn_w_down, v_ln_gain, v_ln_bias):
    given = dict(x=x, s5_lam_re=s5_lam_re, s5_lam_im=s5_lam_im, s5_log_dt=s5_log_dt, s5_b_re=s5_b_re, s5_b_im=s5_b_im, s5_c_re=s5_c_re, s5_c_im=s5_c_im, s5_d=s5_d, s5_w_glu=s5_w_glu, s5_b_glu=s5_b_glu, s5_w_out=s5_w_out, attn_w_kv=attn_w_kv, attn_w_q=attn_w_q, attn_w_out=attn_w_out, rel_bias=rel_bias, ffn_w_up=ffn_w_up, ffn_conv_w=ffn_conv_w, ffn_conv_b=ffn_conv_b, ffn_w_down=ffn_w_down, ln_gain=ln_gain, ln_bias=ln_bias, loss_target=loss_target, m_s5_lam_re=m_s5_lam_re, m_s5_lam_im=m_s5_lam_im, m_s5_log_dt=m_s5_log_dt, m_s5_b_re=m_s5_b_re, m_s5_b_im=m_s5_b_im, m_s5_c_re=m_s5_c_re, m_s5_c_im=m_s5_c_im, m_s5_d=m_s5_d, m_s5_w_glu=m_s5_w_glu, m_s5_b_glu=m_s5_b_glu, m_s5_w_out=m_s5_w_out, m_attn_w_kv=m_attn_w_kv, m_attn_w_q=m_attn_w_q, m_attn_w_out=m_attn_w_out, m_rel_bias=m_rel_bias, m_ffn_w_up=m_ffn_w_up, m_ffn_conv_w=m_ffn_conv_w, m_ffn_conv_b=m_ffn_conv_b, m_ffn_w_down=m_ffn_w_down, m_ln_gain=m_ln_gain, m_ln_bias=m_ln_bias, v_s5_lam_re=v_s5_lam_re, v_s5_lam_im=v_s5_lam_im, v_s5_log_dt=v_s5_log_dt, v_s5_b_re=v_s5_b_re, v_s5_b_im=v_s5_b_im, v_s5_c_re=v_s5_c_re, v_s5_c_im=v_s5_c_im, v_s5_d=v_s5_d, v_s5_w_glu=v_s5_w_glu, v_s5_b_glu=v_s5_b_glu, v_s5_w_out=v_s5_w_out, v_attn_w_kv=v_attn_w_kv, v_attn_w_q=v_attn_w_q, v_attn_w_out=v_attn_w_out, v_rel_bias=v_rel_bias, v_ffn_w_up=v_ffn_w_up, v_ffn_conv_w=v_ffn_conv_w, v_ffn_conv_b=v_ffn_conv_b, v_ffn_w_down=v_ffn_w_down, v_ln_gain=v_ln_gain, v_ln_bias=v_ln_bias)
    weights = {n: given[n] for n in TWIN_WEIGHTS}
    shared = {n: given[n] for n in SHARED_INPUTS}
    per_example = {n: given[n] for n in ['x']}
    grad_fn = _jax.value_and_grad(_loss, argnums=(0, 1))

    def one_microbatch(ex, loss_target):
        ex = dict(ex)
        diff = ex.pop(TWIN_DIFF_INPUT)
        return grad_fn(weights, diff, {**shared, **ex}, loss_target)

    if N_MICROBATCH == 1:
        loss, (grad_w, grad_x) = one_microbatch(per_example, given["loss_target"])
    else:
        def body(carry, xs):
            loss_sum, grad_sum = carry
            l_k, (gw_k, gx_k) = one_microbatch(xs[0], xs[1])
            with _jax.named_scope("update"):
                return (loss_sum + l_k, _jax.tree.map(_jnp.add, grad_sum, gw_k)), gx_k

        init = (_jnp.zeros((), _jnp.float32), _jax.tree.map(_jnp.zeros_like, weights))
        (loss, grad_w), grad_x = _jax.lax.scan(body, init, (per_example, given["loss_target"]))
    with _jax.named_scope("update"):
        delta_w, new_m, new_v = {}, {}, {}
        for n in TWIN_WEIGHTS:
            delta_w[n], new_m[n], new_v[n] = _adamw(weights[n], grad_w[n], given["m_" + n], given["v_" + n])
    return (loss, grad_x, *[grad_w[n] for n in TWIN_WEIGHTS], *[delta_w[n] for n in TWIN_WEIGHTS],
            *[new_m[n] for n in TWIN_WEIGHTS], *[new_v[n] for n in TWIN_WEIGHTS])
```

```python
import functools
import math

import numpy as np
import jax
import jax.numpy as jnp
from jax import lax
from jax.experimental import pallas as pl
from jax.experimental.pallas import tpu as pltpu

F32 = jnp.float32
BF16 = jnp.bfloat16
MXU_DTYPE = jnp.bfloat16
V7X_VMEM_LIMIT_BYTES = 52 << 20
MESH = pl.DeviceIdType.MESH

DEPTH = 2
SSM_GROUP = 16
SSM_STATE = 64
GROUPS_PER_CLUSTER = 16
CLUSTER_W = GROUPS_PER_CLUSTER * SSM_GROUP
HEAD_DIM = 64
DILATIONS = (1, 4, 16)
BAND = 128
NEG_BIG = -1e30
REL_BUCKETS = 32
REL_MAX_DIST = 2048
DN_ALPHA = (2.0 * DEPTH) ** 0.25
LN_EPS = 1e-5
ADAM_LR, ADAM_B1, ADAM_B2, ADAM_EPS, ADAM_WD, ADAM_STEP = 0.001, 0.9, 0.999, 1e-08, 0.01, 10
GELU_K = math.sqrt(2.0 / math.pi)
GELU_C = 0.044715


def _pallas(body, **kw):
    return pl.pallas_call(body, **kw)


def _params(sem=None):
    return pltpu.CompilerParams(dimension_semantics=sem, vmem_limit_bytes=V7X_VMEM_LIMIT_BYTES)


def _pick(n, cands):
    for c in cands:
        if n % c == 0:
            return c
    return n


def _sigmoid(z):
    return 1.0 / (1.0 + jnp.exp(-z))


def _gelu(y):
    return 0.5 * y * (1.0 + jnp.tanh(GELU_K * (y + GELU_C * y * y * y)))


def _gelu_grad(y):
    t = jnp.tanh(GELU_K * (y + GELU_C * y * y * y))
    return 0.5 * (1.0 + t) + 0.5 * y * (1.0 - t * t) * (GELU_K * (1.0 + 3.0 * GELU_C * y * y))


def _mm_nn(a, w, *, bias=None, out_dtype=F32, name):
    T, K = a.shape
    P, _, Np = w.shape
    tm = _pick(T, (512, 256, 128))
    tn = _pick(Np, (1408, 1024, 768, 512, 384, 256, 128))
    nj = Np // tn

    def body(*refs):
        if bias is None:
            a_ref, w_ref, o_ref = refs
        else:
            a_ref, w_ref, b_ref, o_ref = refs
        acc = jnp.dot(a_ref[...].astype(MXU_DTYPE), w_ref[...].astype(MXU_DTYPE), preferred_element_type=F32)
        if bias is not None:
            acc = acc + b_ref[...]
        o_ref[...] = acc.astype(o_ref.dtype)

    in_specs = [pl.BlockSpec((tm, K), lambda i, p, j: (i, 0)),
                pl.BlockSpec((None, K, tn), lambda i, p, j: (p, 0, j))]
    args = [a, w]
    if bias is not None:
        in_specs.append(pl.BlockSpec((1, tn), lambda i, p, j: (0, p * nj + j)))
        args.append(bias)
    return _pallas(
        body, name=name, grid=(T // tm, P, nj), in_specs=in_specs,
        out_specs=pl.BlockSpec((tm, tn), lambda i, p, j: (i, p * nj + j)),
        out_shape=jax.ShapeDtypeStruct((T, P * Np), out_dtype),
        compiler_params=_params(("parallel", "parallel", "parallel")),
    )(*args)


def _mm_nt(a, w, *, p0=0, pn=None, name):
    T = a.shape[0]
    _, K, Np = w.shape
    pn = w.shape[0] if pn is None else pn
    tm = _pick(T, (256, 128))
    tn = _pick(Np, (1536, 1408, 1024, 768, 512, 384, 256, 128))
    nj = Np // tn
    nred = pn * nj

    def body(a_ref, w_ref, o_ref, acc):
        r = pl.program_id(1)

        @pl.when(r == 0)
        def _():
            acc[...] = jnp.zeros_like(acc)

        acc[...] += lax.dot_general(a_ref[...].astype(MXU_DTYPE), w_ref[...].astype(MXU_DTYPE),
                                    (((1,), (1,)), ((), ())), preferred_element_type=F32)

        @pl.when(r == nred - 1)
        def _():
            o_ref[...] = acc[...]

    return _pallas(
        body, name=name, grid=(T // tm, nred),
        in_specs=[pl.BlockSpec((tm, tn), lambda i, r: (i, r)),
                  pl.BlockSpec((None, K, tn), lambda i, r: (p0 + r // nj, 0, r % nj))],
        out_specs=pl.BlockSpec((tm, K), lambda i, r: (i, 0)),
        out_shape=jax.ShapeDtypeStruct((T, K), F32),
        scratch_shapes=[pltpu.VMEM((tm, K), F32)],
        compiler_params=_params(("parallel", "arbitrary")),
    )(a, w)


def _tn(a, b, *, ptotal, np_cols, p0=0, prev=None, name):
    T, K = a.shape
    Np = np_cols
    pn = b.shape[1] // Np
    tt = _pick(T, (1024, 512, 256, 128))
    tk = _pick(K, (512, 256, 128))
    tn = _pick(Np, (1408, 768, 512, 256, 128))
    nj = Np // tn
    nt = T // tt

    def body(*refs):
        a_ref, b_ref = refs[0], refs[1]
        o_ref, acc = refs[-2], refs[-1]
        t = pl.program_id(3)

        @pl.when(t == 0)
        def _():
            acc[...] = jnp.zeros_like(acc)

        acc[...] += lax.dot_general(a_ref[...].astype(MXU_DTYPE), b_ref[...].astype(MXU_DTYPE),
                                    (((0,), (0,)), ((), ())), preferred_element_type=F32)

        @pl.when(t == nt - 1)
        def _():
            o_ref[...] = acc[...]

    in_specs = [pl.BlockSpec((tt, tk), lambda kb, p, j, t: (t, kb)),
                pl.BlockSpec((tt, tn), lambda kb, p, j, t: (t, p * nj + j))]
    args = [a, b]
    aliases = {}
    if prev is not None:
        in_specs.append(pl.BlockSpec(memory_space=pl.ANY))
        args.append(prev)
        aliases = {2: 0}
    return _pallas(
        body, name=name, grid=(K // tk, pn, nj, nt), in_specs=in_specs,
        out_specs=pl.BlockSpec((None, tk, tn), lambda kb, p, j, t: (p0 + p, kb, j)),
        out_shape=jax.ShapeDtypeStruct((ptotal, K, Np), F32),
        scratch_shapes=[pltpu.VMEM((tk, tn), F32)],
        input_output_aliases=aliases,
        compiler_params=_params(("parallel", "parallel", "parallel", "arbitrary")),
    )(*args)


def _rows(tm, f):
    return pl.BlockSpec((tm, f), lambda i: (i, 0))


def _whole(shape):
    nd = len(shape)
    return pl.BlockSpec(shape, lambda i: (0,) * nd)


def _ln_fwd(xres, f, gain, bias, *, name):
    T, D = xres.shape
    tm = _pick(T, (256, 128))

    def body(x_ref, f_ref, g_ref, b_ref, y_ref, yb_ref, xh_ref, rs_ref):
        z = DN_ALPHA * x_ref[...] + f_ref[...]
        mu = jnp.mean(z, axis=-1, keepdims=True)
        zc = z - mu
        var = jnp.mean(zc * zc, axis=-1, keepdims=True)
        rstd = lax.rsqrt(var + LN_EPS)
        xh = zc * rstd
        y = xh * g_ref[...] + b_ref[...]
        y_ref[...] = y
        yb_ref[...] = y.astype(yb_ref.dtype)
        xh_ref[...] = xh
        rs_ref[...] = rstd

    return _pallas(
        body, name=name, grid=(T // tm,),
        in_specs=[_rows(tm, D), _rows(tm, D), _whole((1, D)), _whole((1, D))],
        out_specs=[_rows(tm, D), _rows(tm, D), _rows(tm, D), _rows(tm, 1)],
        out_shape=[jax.ShapeDtypeStruct((T, D), F32), jax.ShapeDtypeStruct((T, D), MXU_DTYPE),
                   jax.ShapeDtypeStruct((T, D), F32), jax.ShapeDtypeStruct((T, 1), F32)],
        compiler_params=_params(("parallel",)),
    )(xres, f, gain, bias)


def _ln_bwd(addends, coefs, xhat, rstd, gain, *, name):
    T, D = xhat.shape
    tm = _pick(T, (256, 128))
    n = len(addends)

    def body(*refs):
        adds = refs[:n]
        xh_ref, rs_ref, g_ref, dz_ref, dzb_ref, dg_ref, db_ref = refs[n:]
        dy = coefs[0] * adds[0][...]
        for c, r in zip(coefs[1:], adds[1:]):
            dy = dy + c * r[...]
        xh = xh_ref[...]
        dxh = dy * g_ref[...]
        m1 = jnp.mean(dxh, axis=-1, keepdims=True)
        m2 = jnp.mean(dxh * xh, axis=-1, keepdims=True)
        dz = rs_ref[...] * (dxh - m1 - xh * m2)
        dz_ref[...] = dz
        dzb_ref[...] = dz.astype(dzb_ref.dtype)

        @pl.when(pl.program_id(0) == 0)
        def _():
            dg_ref[...] = jnp.zeros_like(dg_ref)
            db_ref[...] = jnp.zeros_like(db_ref)

        dg_ref[...] += jnp.sum(dy * xh, axis=0, keepdims=True)
        db_ref[...] += jnp.sum(dy, axis=0, keepdims=True)

    return _pallas(
        body, name=name, grid=(T // tm,),
        in_specs=[_rows(tm, D)] * n + [_rows(tm, D), _rows(tm, 1), _whole((1, D))],
        out_specs=[_rows(tm, D), _rows(tm, D), _whole((1, D)), _whole((1, D))],
        out_shape=[jax.ShapeDtypeStruct((T, D), F32), jax.ShapeDtypeStruct((T, D), MXU_DTYPE),
                   jax.ShapeDtypeStruct((1, D), F32), jax.ShapeDtypeStruct((1, D), F32)],
        compiler_params=_params(("arbitrary",)),
    )(*addends, xhat, rstd, gain)


def _loss_grad(y, tgt, *, name):
    T, D = y.shape
    tm = _pick(T, (256, 128))

    def body(y_ref, t_ref, dy_ref, l_ref):
        e = y_ref[...] - t_ref[...]
        dy_ref[...] = e * (1.0 / D)

        @pl.when(pl.program_id(0) == 0)
        def _():
            l_ref[...] = jnp.zeros_like(l_ref)

        l_ref[...] += jnp.zeros_like(l_ref) + jnp.sum(e * e) * (0.5 / D)

    return _pallas(
        body, name=name, grid=(T // tm,),
        in_specs=[_rows(tm, D), _rows(tm, D)],
        out_specs=[_rows(tm, D), _whole((1, 128))],
        out_shape=[jax.ShapeDtypeStruct((T, D), F32), jax.ShapeDtypeStruct((1, 128), F32)],
        compiler_params=_params(("arbitrary",)),
    )(y, tgt)


def _axpy(a, b, ca, *, name):
    T, D = a.shape
    tm = _pick(T, (256, 128))

    def body(a_ref, b_ref, o_ref):
        o_ref[...] = ca * a_ref[...] + b_ref[...]

    return _pallas(
        body, name=name, grid=(T // tm,), in_specs=[_rows(tm, D), _rows(tm, D)], out_specs=_rows(tm, D),
        out_shape=jax.ShapeDtypeStruct((T, D), F32), compiler_params=_params(("parallel",)),
    )(a, b)


def _glu_gate(y, z, *, name):
    T, D = y.shape
    tm = _pick(T, (256, 128))

    def body(y_ref, z_ref, g_ref):
        g_ref[...] = (_gelu(y_ref[...]) * _sigmoid(z_ref[...])).astype(g_ref.dtype)

    return _pallas(
        body, name=name, grid=(T // tm,), in_specs=[_rows(tm, D), _rows(tm, D)], out_specs=_rows(tm, D),
        out_shape=jax.ShapeDtypeStruct((T, D), MXU_DTYPE), compiler_params=_params(("parallel",)),
    )(y, z)


def _glu_bwd(y, z, dg, *, name):
    T, D = y.shape
    tm = _pick(T, (256, 128))

    def body(y_ref, z_ref, dg_ref, dzb_ref, dyg_ref, db_ref):
        s = _sigmoid(z_ref[...])
        dg = dg_ref[...]
        dz = dg * _gelu(y_ref[...]) * s * (1.0 - s)
        dzb_ref[...] = dz.astype(dzb_ref.dtype)
        dyg_ref[...] = dg * s

        @pl.when(pl.program_id(0) == 0)
        def _():
            db_ref[...] = jnp.zeros_like(db_ref)

        db_ref[...] += jnp.sum(dz, axis=0, keepdims=True)

    return _pallas(
        body, name=name, grid=(T // tm,), in_specs=[_rows(tm, D)] * 3,
        out_specs=[_rows(tm, D), _rows(tm, D), _whole((1, D))],
        out_shape=[jax.ShapeDtypeStruct((T, D), MXU_DTYPE), jax.ShapeDtypeStruct((T, D), F32),
                   jax.ShapeDtypeStruct((1, D), F32)],
        compiler_params=_params(("arbitrary",)),
    )(y, z, dg)


def _gelu_bwd(y, d1, d2, *, name):
    T, D = y.shape
    tm = _pick(T, (256, 128))

    def body(y_ref, a_ref, b_ref, o_ref):
        o_ref[...] = (a_ref[...] + b_ref[...]) * _gelu_grad(y_ref[...])

    return _pallas(
        body, name=name, grid=(T // tm,), in_specs=[_rows(tm, D)] * 3, out_specs=_rows(tm, D),
        out_shape=jax.ShapeDtypeStruct((T, D), F32), compiler_params=_params(("parallel",)),
    )(y, d1, d2)


CONV_ROWS = 128


def _shift_back(x, edge, at_start, tm):
    rows = lax.broadcasted_iota(jnp.int32, x.shape, 0)
    keep = jnp.where(at_start, 0.0, 1.0)
    e7 = edge[7:8, :] * keep
    e6 = edge[6:7, :] * keep
    r1 = pltpu.roll(x, 1, 0)
    r2 = pltpu.roll(x, 2, 0)
    x1 = jnp.where(rows == 0, e7, r1)
    x2 = jnp.where(rows == 0, e6, jnp.where(rows == 1, e7, r2))
    return x1, x2


def _conv_specs(T, F2, tm):
    return [_rows(tm, F2),
            pl.BlockSpec((8, F2), lambda i: (jnp.maximum(i * (tm // 8) - 1, 0), 0))]


def _conv_glu_fwd(hc, conv_w, conv_b, L, *, name):
    T, F2 = hc.shape
    F = F2 // 2
    tm = CONV_ROWS

    def body(x_ref, e_ref, w_ref, b_ref, a_ref):
        at_start = (pl.program_id(0) * tm) % L == 0
        x = x_ref[...]
        x1, x2 = _shift_back(x, e_ref[...], at_start, tm)
        c = b_ref[...] + w_ref[0:1, :] * x + w_ref[1:2, :] * x1 + w_ref[2:3, :] * x2
        val, gate = c[:, :F], c[:, F:]
        a_ref[...] = (gate * _sigmoid(gate) * val).astype(a_ref.dtype)

    return _pallas(
        body, name=name, grid=(T // tm,),
        in_specs=_conv_specs(T, F2, tm) + [_whole((3, F2)), _whole((1, F2))],
        out_specs=_rows(tm, F),
        out_shape=jax.ShapeDtypeStruct((T, F), MXU_DTYPE), compiler_params=_params(("parallel",)),
    )(hc, hc, conv_w, conv_b)


def _conv_glu_bwd(hc, da, conv_w, conv_b, L, *, name):
    T, F2 = hc.shape
    F = F2 // 2
    tm = CONV_ROWS

    def body(x_ref, e_ref, da_ref, w_ref, b_ref, dc_ref, dw_ref, db_ref):
        at_start = (pl.program_id(0) * tm) % L == 0
        x = x_ref[...]
        x1, x2 = _shift_back(x, e_ref[...], at_start, tm)
        c = b_ref[...] + w_ref[0:1, :] * x + w_ref[1:2, :] * x1 + w_ref[2:3, :] * x2
        val, gate = c[:, :F], c[:, F:]
        s = _sigmoid(gate)
        da = da_ref[...]
        dval = da * (gate * s)
        dgate = da * val * (s * (1.0 + gate * (1.0 - s)))
        dc_ref[:, :F] = dval
        dc_ref[:, F:] = dgate
        dc = dc_ref[...]

        @pl.when(pl.program_id(0) == 0)
        def _():
            dw_ref[...] = jnp.zeros_like(dw_ref)
            db_ref[...] = jnp.zeros_like(db_ref)

        dw_ref[0:1, :] += jnp.sum(dc * x, axis=0, keepdims=True)
        dw_ref[1:2, :] += jnp.sum(dc * x1, axis=0, keepdims=True)
        dw_ref[2:3, :] += jnp.sum(dc * x2, axis=0, keepdims=True)
        db_ref[...] += jnp.sum(dc, axis=0, keepdims=True)

    return _pallas(
        body, name=name, grid=(T // tm,),
        in_specs=_conv_specs(T, F2, tm) + [_rows(tm, F), _whole((3, F2)), _whole((1, F2))],
        out_specs=[_rows(tm, F2), _whole((3, F2)), _whole((1, F2))],
        out_shape=[jax.ShapeDtypeStruct((T, F2), F32), jax.ShapeDtypeStruct((3, F2), F32),
                   jax.ShapeDtypeStruct((1, F2), F32)],
        compiler_params=_params(("arbitrary",)),
    )(hc, hc, da, conv_w, conv_b)


def _conv_bwd_input(dc, conv_w, L, *, name):
    T, F2 = dc.shape
    tm = CONV_ROWS
    last_blk = T // 8 - 1

    def body(x_ref, e_ref, w_ref, o_ref):
        at_end = ((pl.program_id(0) + 1) * tm) % L == 0
        x = x_ref[...]
        rows = lax.broadcasted_iota(jnp.int32, x.shape, 0)
        keep = jnp.where(at_end, 0.0, 1.0)
        e0 = e_ref[0:1, :] * keep
        e1 = e_ref[1:2, :] * keep
        u1 = pltpu.roll(x, tm - 1, 0)
        u2 = pltpu.roll(x, tm - 2, 0)
        x1 = jnp.where(rows == tm - 1, e0, u1)
        x2 = jnp.where(rows == tm - 1, e1, jnp.where(rows == tm - 2, e0, u2))
        o_ref[...] = (w_ref[0:1, :] * x + w_ref[1:2, :] * x1 + w_ref[2:3, :] * x2).astype(o_ref.dtype)

    return _pallas(
        body, name=name, grid=(T // tm,),
        in_specs=[_rows(tm, F2),
                  pl.BlockSpec((8, F2), lambda i: (jnp.minimum((i + 1) * (tm // 8), last_blk), 0)),
                  _whole((3, F2))],
        out_specs=_rows(tm, F2),
        out_shape=jax.ShapeDtypeStruct((T, F2), MXU_DTYPE), compiler_params=_params(("parallel",)),
    )(dc, dc, conv_w)


S5_CHUNK = 128
LANES = 128


def _slab_rows(c, n, ncl):
    return pl.ds(c, n) if ncl == 1 else pl.ds(c, n, stride=ncl)


def _slab_put(ref, c, n, ncl, val):
    for s in range(val.shape[1] // LANES):
        ref[s, _slab_rows(c, n, ncl), :] = val[:, s * LANES:(s + 1) * LANES]


def _slab_get(ref, c, n, ncl):
    return jnp.concatenate([ref[s, _slab_rows(c, n, ncl), :] for s in range(ref.shape[0])], axis=-1)


def _slabs(n_slab, rows):
    return pl.BlockSpec((n_slab, rows, LANES), lambda i: (0, i, 0))


def _s5_fwd(xi, wb, wc, a_r, a_i, d_row, B, *, name):
    T, D = xi.shape
    ncl = wb.shape[0]
    cs = wb.shape[2] // 2
    ns = cs // LANES
    R = B * ncl
    Q = S5_CHUNK
    QR = Q * ncl
    nsteps = Q // B

    def body(x_ref, wb_ref, wc_ref, ar_ref, ai_ref, d_ref, y_ref, yg_ref, hr_ref, hi_ref, bur, bui, cr, ci):
        @pl.when(pl.program_id(0) == 0)
        def _():
            cr[...] = jnp.zeros_like(cr)
            ci[...] = jnp.zeros_like(ci)

        x = x_ref[...]
        xb = x.astype(MXU_DTYPE)
        for c in range(ncl):
            bu = jnp.dot(xb[:, c * CLUSTER_W:(c + 1) * CLUSTER_W], wb_ref[c], preferred_element_type=F32)
            _slab_put(bur, c, Q, ncl, bu[:, :cs])
            _slab_put(bui, c, Q, ncl, bu[:, cs:])
        ar = ar_ref[...]
        ai = ai_ref[...]

        def step(k, carry):
            hr, hi = carry
            sl = pl.ds(pl.multiple_of(k * R, R), R)
            nr = ar * hr - ai * hi + bur[:, sl, :]
            ni = ar * hi + ai * hr + bui[:, sl, :]
            hr_ref[:, sl, :] = nr
            hi_ref[:, sl, :] = ni
            return nr, ni

        hr, hi = lax.fori_loop(0, nsteps, step, (cr[...], ci[...]), unroll=4)
        cr[...] = hr
        ci[...] = hi
        parts = []
        for c in range(ncl):
            hrc = _slab_get(hr_ref, c, Q, ncl).astype(MXU_DTYPE)
            hic = _slab_get(hi_ref, c, Q, ncl).astype(MXU_DTYPE)
            parts.append(jnp.dot(hrc, wc_ref[c, :cs, :], preferred_element_type=F32)
                         + jnp.dot(hic, wc_ref[c, cs:, :], preferred_element_type=F32))
        y = d_ref[...] * x + (parts[0] if ncl == 1 else jnp.concatenate(parts, axis=-1))
        y_ref[...] = y
        yg_ref[...] = _gelu(y).astype(yg_ref.dtype)

    return _pallas(
        body, name=name, grid=(T // Q,),
        in_specs=[_rows(Q, D), _whole(wb.shape), _whole(wc.shape), _whole((ns, R, LANES)), _whole((ns, R, LANES)),
                  _whole((1, D))],
        out_specs=[_rows(Q, D), _rows(Q, D), _slabs(ns, QR), _slabs(ns, QR)],
        out_shape=[jax.ShapeDtypeStruct((T, D), F32), jax.ShapeDtypeStruct((T, D), MXU_DTYPE),
                   jax.ShapeDtypeStruct((ns, T * ncl, LANES), F32), jax.ShapeDtypeStruct((ns, T * ncl, LANES), F32)],
        scratch_shapes=[pltpu.VMEM((ns, QR, LANES), F32), pltpu.VMEM((ns, QR, LANES), F32),
                        pltpu.VMEM((ns, R, LANES), F32), pltpu.VMEM((ns, R, LANES), F32)],
        compiler_params=_params(("arbitrary",)),
    )(xi, wb, wc, a_r, a_i, d_row)


def _s5_bwd(dy, xi, h_r, h_i, wb, wc, a_r, a_i, d_row, B, *, name):
    T, D = dy.shape
    ncl = wb.shape[0]
    cs = wb.shape[2] // 2
    ns = cs // LANES
    R = B * ncl
    Q = S5_CHUNK
    nsteps = Q // B
    nchunk = T // Q
    QR = Q * ncl

    def rev(i):
        return nchunk - 1 - i

    def body(dy_ref, x_ref, hr_ref, hi_ref, pr_ref, pi_ref, wb_ref, wc_ref, ar_ref, ai_ref, d_ref,
             du_ref, gr_ref, gi_ref, dar_ref, dai_ref, dd_ref, dhr, dhi, cr, ci):
        i = pl.program_id(0)

        @pl.when(i == 0)
        def _():
            cr[...] = jnp.zeros_like(cr)
            ci[...] = jnp.zeros_like(ci)
            dar_ref[...] = jnp.zeros_like(dar_ref)
            dai_ref[...] = jnp.zeros_like(dai_ref)
            dd_ref[...] = jnp.zeros_like(dd_ref)

        dyv = dy_ref[...]
        dyb = dyv.astype(MXU_DTYPE)
        for c in range(ncl):
            dh = lax.dot_general(dyb[:, c * CLUSTER_W:(c + 1) * CLUSTER_W], wc_ref[c],
                                 (((1,), (1,)), ((), ())), preferred_element_type=F32)
            _slab_put(dhr, c, Q, ncl, dh[:, :cs])
            _slab_put(dhi, c, Q, ncl, dh[:, cs:])
        ar = ar_ref[...]
        ai = ai_ref[...]

        def step(j, carry):
            gr, gi, sar, sai = carry
            k = nsteps - 1 - j
            sl = pl.ds(pl.multiple_of(k * R, R), R)
            ngr = dhr[:, sl, :] + ar * gr + ai * gi
            ngi = dhi[:, sl, :] - ai * gr + ar * gi
            gr_ref[:, sl, :] = ngr
            gi_ref[:, sl, :] = ngi
            pv = pl.ds(pl.multiple_of((k - 1) * R, R), R)
            hpr = hr_ref[:, pv, :]
            hpi = hi_ref[:, pv, :]
            return ngr, ngi, sar + ngr * hpr + ngi * hpi, sai - ngr * hpi + ngi * hpr

        gr, gi, sar, sai = lax.fori_loop(0, nsteps - 1, step, (cr[...], ci[...], dar_ref[...], dai_ref[...]), unroll=4)
        sl0 = pl.ds(0, R)
        ngr = dhr[:, sl0, :] + ar * gr + ai * gi
        ngi = dhi[:, sl0, :] - ai * gr + ar * gi
        gr_ref[:, sl0, :] = ngr
        gi_ref[:, sl0, :] = ngi
        keep = jnp.where(i == nchunk - 1, 0.0, 1.0)
        hpr = pr_ref[:, 8 - R:8, :] * keep
        hpi = pi_ref[:, 8 - R:8, :] * keep
        dar_ref[...] = sar + ngr * hpr + ngi * hpi
        dai_ref[...] = sai - ngr * hpi + ngi * hpr
        cr[...] = ngr
        ci[...] = ngi
        parts = []
        for c in range(ncl):
            grc = _slab_get(gr_ref, c, Q, ncl).astype(MXU_DTYPE)
            gic = _slab_get(gi_ref, c, Q, ncl).astype(MXU_DTYPE)
            parts.append(lax.dot_general(grc, wb_ref[c, :, :cs], (((1,), (1,)), ((), ())), preferred_element_type=F32)
                         + lax.dot_general(gic, wb_ref[c, :, cs:], (((1,), (1,)), ((), ())), preferred_element_type=F32))
        du_ref[...] = d_ref[...] * dyv + (parts[0] if ncl == 1 else jnp.concatenate(parts, axis=-1))
        dd_ref[...] += jnp.sum(dyv * x_ref[...], axis=0, keepdims=True)

    tok = pl.BlockSpec((Q, D), lambda i: (rev(i), 0))
    st = pl.BlockSpec((ns, QR, LANES), lambda i: (0, rev(i), 0))
    before = pl.BlockSpec((ns, 8, LANES), lambda i: (0, jnp.maximum(rev(i) * (QR // 8) - 1, 0), 0))
    acc = _whole((ns, R, LANES))
    return _pallas(
        body, name=name, grid=(nchunk,),
        in_specs=[tok, tok, st, st, before, before, _whole(wb.shape), _whole(wc.shape), acc, acc, _whole((1, D))],
        out_specs=[tok, st, st, acc, acc, _whole((1, D))],
        out_shape=[jax.ShapeDtypeStruct((T, D), F32),
                   jax.ShapeDtypeStruct((ns, T * ncl, LANES), F32), jax.ShapeDtypeStruct((ns, T * ncl, LANES), F32),
                   jax.ShapeDtypeStruct((ns, R, LANES), F32), jax.ShapeDtypeStruct((ns, R, LANES), F32),
                   jax.ShapeDtypeStruct((1, D), F32)],
        scratch_shapes=[pltpu.VMEM((ns, QR, LANES), F32)] * 2 + [pltpu.VMEM((ns, R, LANES), F32)] * 2,
        compiler_params=_params(("arbitrary",)),
    )(dy, xi, h_r, h_i, h_r, h_i, wb, wc, a_r, a_i, d_row)


def _cluster_tn(tok, st, ncl, *, tok_left, name):
    T = tok.shape[0]
    ns = st.shape[0]
    cs = ns * LANES
    tt = _pick(T, (512, 256, 128))
    nt = T // tt
    oshape = (ncl, CLUSTER_W, cs) if tok_left else (ncl, cs, CLUSTER_W)

    def body(tok_ref, st_ref, o_ref, acc):
        t = pl.program_id(0)

        @pl.when(t == 0)
        def _():
            acc[...] = jnp.zeros_like(acc)

        tk = tok_ref[...].astype(MXU_DTYPE)
        for c in range(ncl):
            tc = tk[:, c * CLUSTER_W:(c + 1) * CLUSTER_W]
            sc = _slab_get(st_ref, c, tt, ncl).astype(MXU_DTYPE)
            lhs, rhs = (tc, sc) if tok_left else (sc, tc)
            acc[c] += lax.dot_general(lhs, rhs, (((0,), (0,)), ((), ())), preferred_element_type=F32)

        @pl.when(t == nt - 1)
        def _():
            o_ref[...] = acc[...]

    return _pallas(
        body, name=name, grid=(nt,),
        in_specs=[_rows(tt, tok.shape[1]), _slabs(ns, tt * ncl)],
        out_specs=_whole(oshape),
        out_shape=jax.ShapeDtypeStruct(oshape, F32),
        scratch_shapes=[pltpu.VMEM(oshape, F32)],
        compiler_params=_params(("arbitrary",)),
    )(tok, st)


def _s5_discretize(lam_re, lam_im, log_dt, b_re, b_im):
    dt = jnp.exp(log_dt)[:, None]
    mag = jnp.exp(lam_re * dt)
    ab_r, ab_i = mag * jnp.cos(lam_im * dt), mag * jnp.sin(lam_im * dt)
    den = lam_re * lam_re + lam_im * lam_im
    nr = ab_r - 1.0
    co_r = (nr * lam_re + ab_i * lam_im) / den
    co_i = (ab_i * lam_re - nr * lam_im) / den
    bb_r = co_r[..., None] * b_re - co_i[..., None] * b_im
    bb_i = co_r[..., None] * b_im + co_i[..., None] * b_re
    return ab_r, ab_i, bb_r, bb_i


def _blockdiag(m):
    G, r, k = m.shape
    ncl = G // GROUPS_PER_CLUSTER
    m4 = m.reshape(ncl, GROUPS_PER_CLUSTER, r, k)
    eye = jnp.eye(GROUPS_PER_CLUSTER, dtype=m.dtype)
    return jnp.einsum('cgrk,gh->cgrhk', m4, eye).reshape(ncl, GROUPS_PER_CLUSTER * r, GROUPS_PER_CLUSTER * k)


def _unblockdiag(m, r, k):
    ncl = m.shape[0]
    m5 = m.reshape(ncl, GROUPS_PER_CLUSTER, r, GROUPS_PER_CLUSTER, k)
    eye = jnp.eye(GROUPS_PER_CLUSTER, dtype=m.dtype)
    return jnp.einsum('cgrhk,gh->cgrk', m5, eye).reshape(ncl * GROUPS_PER_CLUSTER, r, k)


def _t5_bucket(dist):
    exact = REL_BUCKETS // 2
    d = np.maximum(dist, 1).astype(np.float32)
    large = exact + (np.log(d / exact) / math.log(REL_MAX_DIST / exact) * (REL_BUCKETS - exact)).astype(np.int64)
    large = np.minimum(large, REL_BUCKETS - 1)
    return np.where(dist < exact, dist, large).astype(np.int32)


def _band_tables(dil):
    steps = np.arange(BAND)[:, None] + BAND - np.arange(2 * BAND)[None, :]
    bucket = _t5_bucket(np.maximum(steps, 0) * dil)
    in_band = (steps >= 0) & (steps <= BAND)
    return bucket, in_band


def _attn_bias(rel_bias, hpg):
    out = []
    for g, dil in enumerate(DILATIONS):
        bucket, in_band = _band_tables(dil)
        cols = rel_bias[:, g * hpg:(g + 1) * hpg]
        bias = jnp.transpose(cols[bucket], (2, 0, 1)).astype(F32)
        out.append(jnp.where(jnp.asarray(in_band)[None], bias, NEG_BIG))
    return jnp.concatenate(out, axis=0)


def _attn_blocks(dil, L):
    M = L // dil
    return M, M // BAND


def _row_sel(r, M, dil):
    return pl.ds(r, M) if dil == 1 else pl.ds(r, M, stride=dil)


def _attn_fwd(q, kv, bias, L, hpg, *, name):
    T = q.shape[0]
    nb_ = T // L
    HP = hpg // 2
    W3 = 3 * hpg * HEAD_DIM
    mmax = L

    def group_body(dil, q_ref, k_ref, v_ref, b_ref, o_ref, l_ref, os, ls):
        M, NB = _attn_blocks(dil, L)
        for r in range(dil):
            rows = _row_sel(r, M, dil)
            qr = (q_ref[rows, :] * 0.125).astype(MXU_DTYPE)
            kr = k_ref[rows, :].astype(MXU_DTYPE)
            vr = v_ref[rows, :].astype(MXU_DTYPE)
            for n in range(NB):
                qs = slice(n * BAND, (n + 1) * BAND)
                ks = slice(0, BAND) if n == 0 else slice((n - 1) * BAND, (n + 1) * BAND)
                for hh in range(2):
                    ln = slice(hh * HEAD_DIM, (hh + 1) * HEAD_DIM)
                    bb = b_ref[hh, :, BAND:] if n == 0 else b_ref[hh]
                    s = lax.dot_general(qr[qs, ln], kr[ks, ln], (((1,), (1,)), ((), ())),
                                        preferred_element_type=F32) + bb
                    m = jnp.max(s, axis=-1, keepdims=True)
                    p = jnp.exp(s - m)
                    l = jnp.sum(p, axis=-1, keepdims=True)
                    pn = p / l
                    os[qs, ln] = jnp.dot(pn.astype(MXU_DTYPE), vr[ks, ln], preferred_element_type=F32)
                    ls[qs, ln] = jnp.broadcast_to(m + jnp.log(l), (BAND, HEAD_DIM))
            o_ref[rows, :] = os[0:M, :]
            l_ref[rows, :] = ls[0:M, :]

    def body(q_ref, k_ref, v_ref, b_ref, o_ref, l_ref, os, ls):
        g = pl.program_id(0)
        for gi, dil in enumerate(DILATIONS):
            pl.when(g == gi)(functools.partial(group_body, dil, q_ref, k_ref, v_ref, b_ref, o_ref, l_ref, os, ls))

    blk = (L, 2 * HEAD_DIM)
    return _pallas(
        body, name=name, grid=(3, nb_, HP),
        in_specs=[pl.BlockSpec(blk, lambda g, b, h: (b, g * HP + h)),
                  pl.BlockSpec(blk, lambda g, b, h: (b, g * HP + h)),
                  pl.BlockSpec(blk, lambda g, b, h: (b, 3 * HP + g * HP + h)),
                  pl.BlockSpec((2, BAND, 2 * BAND), lambda g, b, h: (g * HP + h, 0, 0))],
        out_specs=[pl.BlockSpec(blk, lambda g, b, h: (b, g * HP + h)),
                   pl.BlockSpec(blk, lambda g, b, h: (b, g * HP + h))],
        out_shape=[jax.ShapeDtypeStruct((T, W3), F32), jax.ShapeDtypeStruct((T, W3), F32)],
        scratch_shapes=[pltpu.VMEM((mmax, 2 * HEAD_DIM), F32), pltpu.VMEM((mmax, 2 * HEAD_DIM), F32)],
        compiler_params=_params(("arbitrary", "arbitrary", "arbitrary")),
    )(q, kv, kv, bias)


def _attn_merge(o3, l3, hw, *, name):
    T = o3.shape[0]
    tm = _pick(T, (256, 128))

    def body(o0, o1, o2, l0, l1, l2, o_ref, ob_ref, lse_ref):
        a0, a1, a2 = l0[...], l1[...], l2[...]
        m = jnp.maximum(jnp.maximum(a0, a1), a2)
        e0, e1, e2 = jnp.exp(a0 - m), jnp.exp(a1 - m), jnp.exp(a2 - m)
        z = e0 + e1 + e2
        o = (e0 * o0[...] + e1 * o1[...] + e2 * o2[...]) / z
        o_ref[...] = o
        ob_ref[...] = o.astype(ob_ref.dtype)
        lse_ref[...] = m + jnp.log(z)

    def col(g):
        return pl.BlockSpec((tm, hw), lambda i: (i, g))

    return _pallas(
        body, name=name, grid=(T // tm,),
        in_specs=[col(0), col(1), col(2), col(0), col(1), col(2)],
        out_specs=[_rows(tm, hw)] * 3,
        out_shape=[jax.ShapeDtypeStruct((T, hw), F32), jax.ShapeDtypeStruct((T, hw), MXU_DTYPE),
                   jax.ShapeDtypeStruct((T, hw), F32)],
        compiler_params=_params(("parallel",)),
    )(o3, o3, o3, l3, l3, l3)


def _attn_bwd(q, kv, do, o, lse, bias, L, hpg, *, name):
    T = q.shape[0]
    nb_ = T // L
    HP = hpg // 2
    W3 = 3 * hpg * HEAD_DIM
    mmax = L

    def group_body(dil, q_ref, k_ref, v_ref, do_ref, o_ref, l_ref, b_ref, dq_ref, dk_ref, dv_ref, ds_ref,
                   dqs, dks, dvs):
        M, NB = _attn_blocks(dil, L)
        for r in range(dil):
            rows = _row_sel(r, M, dil)
            qr = (q_ref[rows, :] * 0.125).astype(MXU_DTYPE)
            kr = k_ref[rows, :].astype(MXU_DTYPE)
            vr = v_ref[rows, :].astype(MXU_DTYPE)
            dor = do_ref[rows, :]
            orr = o_ref[rows, :]
            lr = l_ref[rows, :]
            dks[0:M, :] = jnp.zeros((M, 2 * HEAD_DIM), F32)
            dvs[0:M, :] = jnp.zeros((M, 2 * HEAD_DIM), F32)
            for n in range(NB):
                qs = slice(n * BAND, (n + 1) * BAND)
                ks = slice(0, BAND) if n == 0 else slice((n - 1) * BAND, (n + 1) * BAND)
                for hh in range(2):
                    ln = slice(hh * HEAD_DIM, (hh + 1) * HEAD_DIM)
                    bb = b_ref[hh, :, BAND:] if n == 0 else b_ref[hh]
                    qb, kb, vb = qr[qs, ln], kr[ks, ln], vr[ks, ln]
                    dob = dor[qs, ln]
                    s = lax.dot_general(qb, kb, (((1,), (1,)), ((), ())), preferred_element_type=F32) + bb
                    p = jnp.exp(s - lr[qs, hh * HEAD_DIM:hh * HEAD_DIM + 1])
                    dobm = dob.astype(MXU_DTYPE)
                    dp = lax.dot_general(dobm, vb, (((1,), (1,)), ((), ())), preferred_element_type=F32)
                    delta = jnp.sum(dob * orr[qs, ln], axis=-1, keepdims=True)
                    ds = p * (dp - delta)
                    if n == 0:
                        ds_ref[hh, :, BAND:] += ds
                    else:
                        ds_ref[hh] += ds
                    dsm = ds.astype(MXU_DTYPE)
                    dqs[qs, ln] = jnp.dot(dsm, kb, preferred_element_type=F32) * 0.125
                    dks[ks, ln] += lax.dot_general(dsm, qb, (((0,), (0,)), ((), ())), preferred_element_type=F32)
                    dvs[ks, ln] += lax.dot_general(p.astype(MXU_DTYPE), dobm, (((0,), (0,)), ((), ())),
                                                   preferred_element_type=F32)
            dq_ref[rows, :] = dqs[0:M, :]
            dk_ref[rows, :] = dks[0:M, :]
            dv_ref[rows, :] = dvs[0:M, :]

    def body(q_ref, k_ref, v_ref, do_ref, o_ref, l_ref, b_ref, dq_ref, dk_ref, dv_ref, ds_ref, dqs, dks, dvs):
        g = pl.program_id(0)

        @pl.when(pl.program_id(2) == 0)
        def _():
            ds_ref[...] = jnp.zeros_like(ds_ref)

        for gi, dil in enumerate(DILATIONS):
            pl.when(g == gi)(functools.partial(group_body, dil, q_ref, k_ref, v_ref, do_ref, o_ref, l_ref, b_ref,
                                               dq_ref, dk_ref, dv_ref, ds_ref, dqs, dks, dvs))

    blk = (L, 2 * HEAD_DIM)
    gcol = lambda g, h, b: (b, g * HP + h)
    hcol = lambda g, h, b: (b, h)
    return _pallas(
        body, name=name, grid=(3, HP, nb_),
        in_specs=[pl.BlockSpec(blk, gcol), pl.BlockSpec(blk, gcol),
                  pl.BlockSpec(blk, lambda g, h, b: (b, 3 * HP + g * HP + h)),
                  pl.BlockSpec(blk, hcol), pl.BlockSpec(blk, hcol), pl.BlockSpec(blk, hcol),
                  pl.BlockSpec((2, BAND, 2 * BAND), lambda g, h, b: (g * HP + h, 0, 0))],
        out_specs=[pl.BlockSpec(blk, gcol), pl.BlockSpec(blk, gcol), pl.BlockSpec(blk, gcol),
                   pl.BlockSpec((2, BAND, 2 * BAND), lambda g, h, b: (g * HP + h, 0, 0))],
        out_shape=[jax.ShapeDtypeStruct((T, W3), F32), jax.ShapeDtypeStruct((T, W3), F32),
                   jax.ShapeDtypeStruct((T, W3), F32), jax.ShapeDtypeStruct((3 * hpg, BAND, 2 * BAND), F32)],
        scratch_shapes=[pltpu.VMEM((mmax, 2 * HEAD_DIM), F32)] * 3,
        compiler_params=_params(("arbitrary", "arbitrary", "arbitrary")),
    )(q, kv, kv, do, o, lse, bias)


def _bias_grad(ds_sum, hpg, *, name):
    nh = ds_sum.shape[0]
    idx = np.stack([np.where(_band_tables(dil)[1], _band_tables(dil)[0], -1) for dil in DILATIONS]).astype(np.int32)

    def body(ds_ref, idx_ref, o_ref):
        d = ds_ref[...]
        ix = idx_ref[...]
        lane = lax.broadcasted_iota(jnp.int32, (8, 128), 1)
        row = jnp.zeros((8, 128), F32)
        for b in range(REL_BUCKETS):
            row = row + jnp.where(lane == b, jnp.sum(jnp.where(ix == b, d, 0.0)), 0.0)
        o_ref[...] = row

    out = _pallas(
        body, name=name, grid=(nh,),
        in_specs=[pl.BlockSpec((None, BAND, 2 * BAND), lambda h: (h, 0, 0)),
                  pl.BlockSpec((None, BAND, 2 * BAND), lambda h: (h // hpg, 0, 0))],
        out_specs=pl.BlockSpec((None, 8, 128), lambda h: (h, 0, 0)),
        out_shape=jax.ShapeDtypeStruct((nh, 8, 128), F32),
        compiler_params=_params(("parallel",)),
    )(ds_sum, jnp.asarray(idx))
    return out[:, 0, :REL_BUCKETS].T


def _adamw(w, g, m, v, *, name):
    Rw, C = w.shape
    tm = _pick(Rw, (512, 352, 256, 128, 64, 32, 16, 8))

    def body(w_ref, g_ref, m_ref, v_ref, d_ref, nm_ref, nv_ref):
        gg = g_ref[...]
        nm = ADAM_B1 * m_ref[...] + (1.0 - ADAM_B1) * gg
        nv = ADAM_B2 * v_ref[...] + (1.0 - ADAM_B2) * (gg * gg)
        m_hat = nm / (1.0 - ADAM_B1 ** ADAM_STEP)
        v_hat = nv / (1.0 - ADAM_B2 ** ADAM_STEP)
        d_ref[...] = -ADAM_LR * (m_hat / (jnp.sqrt(v_hat) + ADAM_EPS) + ADAM_WD * w_ref[...])
        nm_ref[...] = nm
        nv_ref[...] = nv

    return _pallas(
        body, name=name, grid=(Rw // tm,), in_specs=[_rows(tm, C)] * 4, out_specs=[_rows(tm, C)] * 3,
        out_shape=[jax.ShapeDtypeStruct((Rw, C), F32)] * 3, compiler_params=_params(("parallel",)),
    )(w, g, m, v)


def _add2(a, b, *, cast, name):
    Rw, C = a.shape
    tm = _pick(Rw, (512, 352, 320, 256, 128, 64, 32, 16, 8))

    def body(a_ref, b_ref, *outs):
        s = a_ref[...] + b_ref[...]
        outs[0][...] = s
        if cast:
            outs[1][...] = s.astype(BF16)

    shapes = [jax.ShapeDtypeStruct((Rw, C), F32)] + ([jax.ShapeDtypeStruct((Rw, C), BF16)] if cast else [])
    return _pallas(
        body, name=name, grid=(Rw // tm,), in_specs=[_rows(tm, C)] * 2, out_specs=[_rows(tm, C)] * len(shapes),
        out_shape=shapes, compiler_params=_params(("parallel",)),
    )(a, b)


def _add4(own, rc, *, name):
    Rw, C = own.shape
    tm = _pick(Rw, (512, 352, 320, 256, 128, 64, 32, 16, 8))

    def body(o_ref, r_ref, out_ref):
        s = o_ref[...]
        for k in range(3):
            s = s + r_ref[k].astype(F32)
        out_ref[...] = s

    return _pallas(
        body, name=name, grid=(Rw // tm,),
        in_specs=[_rows(tm, C), pl.BlockSpec((3, tm, C), lambda i: (0, i, 0))], out_specs=_rows(tm, C),
        out_shape=jax.ShapeDtypeStruct((Rw, C), F32), compiler_params=_params(("parallel",)),
    )(own, rc)


def _place():
    x, y, c = lax.axis_index("x"), lax.axis_index("y"), lax.axis_index("c")
    chips = [(1 - x, y), (x, 1 - y), (1 - x, 1 - y)]
    return x, y, c, chips


_ANY = pl.BlockSpec(memory_space=pl.ANY)


def _comm_call(body, ins, out_shapes, n_local, n_remote, *, name, aliases=None):
    sems = [pltpu.SemaphoreType.DMA((max(n_local, 1),))] + [pltpu.SemaphoreType.DMA((max(n, 1),)) for n in n_remote]
    return _pallas(
        body, name=name, in_specs=[_ANY] * len(ins), out_specs=[_ANY] * len(out_shapes), out_shape=out_shapes,
        scratch_shapes=sems, input_output_aliases=aliases or {},
        compiler_params=pltpu.CompilerParams(has_side_effects=True),
    )(*ins)


def _rcopy(src, dst, ssem, rsem, dev):
    return pltpu.make_async_remote_copy(src_ref=src, dst_ref=dst, send_sem=ssem, recv_sem=rsem,
                                        device_id=dev, device_id_type=MESH)


def _all_gather(shards, *, name):
    n = len(shards)

    def body(*refs):
        ins, outs = refs[:n], refs[n:2 * n]
        lsem, s_ici, r_ici, s_d2d, r_d2d = refs[2 * n:]
        x, y, c, chips = _place()
        me = 2 * x + y
        sib = (x, y, 1 - c)
        started = []
        for a in range(n):
            lc = pltpu.make_async_copy(ins[a], outs[a].at[me], lsem.at[a])
            lc.start()
            started.append(lc)
        sends = []
        for a in range(n):
            for k, (tx, ty) in enumerate(chips):
                cp = _rcopy(ins[a].at[c], outs[a].at[me, c], s_ici.at[3 * a + k], r_ici.at[3 * a + k], (tx, ty, c))
                cp.start()
                sends.append(cp)
        for a in range(n):
            for k, (tx, ty) in enumerate(chips):
                pk = 2 * tx + ty
                _rcopy(ins[a].at[c], outs[a].at[pk, c], s_ici.at[3 * a + k], r_ici.at[3 * a + k], (tx, ty, c)).wait_recv()
                fw = _rcopy(outs[a].at[pk, c], outs[a].at[pk, c], s_d2d.at[3 * a + k], r_d2d.at[3 * a + k], sib)
                fw.start()
                sends.append(fw)
        for a in range(n):
            for k, (tx, ty) in enumerate(chips):
                pk = 2 * tx + ty
                _rcopy(ins[a].at[c], outs[a].at[pk, 1 - c], s_d2d.at[3 * a + k], r_d2d.at[3 * a + k], sib).wait_recv()
        for cp in sends:
            cp.wait_send()
        for lc in started:
            lc.wait()

    shapes = [jax.ShapeDtypeStruct((4,) + s.shape, s.dtype) for s in shards]
    return _comm_call(body, shards, shapes, n, [3 * n] * 4, name=name)


def _pair_split(gs, *, name):
    n = len(gs)

    def body(*refs):
        ins, mine, theirs = refs[:n], refs[n:2 * n], refs[2 * n:3 * n]
        lsem, ssem, rsem = refs[3 * n:]
        x, y, c, _ = _place()
        sib = (x, y, 1 - c)
        cps = []
        for a in range(n):
            lc = pltpu.make_async_copy(ins[a].at[:, pl.ds(c, 1)], mine[a], lsem.at[a])
            lc.start()
            cp = _rcopy(ins[a].at[:, pl.ds(1 - c, 1)], theirs[a], ssem.at[a], rsem.at[a], sib)
            cp.start()
            cps.append((lc, cp))
        for lc, cp in cps:
            cp.wait_send()
            cp.wait_recv()
            lc.wait()

    shapes = [jax.ShapeDtypeStruct((4, 1) + g.shape[2:], g.dtype) for g in gs]
    outs = _comm_call(body, gs, shapes + shapes, n, [n, n], name=name)
    return outs[:n], outs[n:]


def _chip_exchange(hx, hf, *, name):
    n = len(hx)

    def body(*refs):
        hxr, hfr, own, got = refs[:n], refs[n:2 * n], refs[2 * n:3 * n], refs[3 * n:4 * n]
        lsem, ssem, rsem = refs[4 * n:]
        x, y, c, chips = _place()
        me = 2 * x + y
        cps, lcs = [], []
        for a in range(n):
            lc = pltpu.make_async_copy(hfr[a].at[me], own[a], lsem.at[a])
            lc.start()
            lcs.append(lc)
            for k, (tx, ty) in enumerate(chips):
                cp = _rcopy(hxr[a].at[2 * tx + ty], got[a].at[k], ssem.at[3 * a + k], rsem.at[3 * a + k], (tx, ty, c))
                cp.start()
                cps.append(cp)
        for cp in cps:
            cp.wait_send()
            cp.wait_recv()
        for lc in lcs:
            lc.wait()

    shapes = ([jax.ShapeDtypeStruct(h.shape[1:], F32) for h in hf]
              + [jax.ShapeDtypeStruct((3,) + h.shape[1:], h.dtype) for h in hx])
    outs = _comm_call(body, list(hx) + list(hf), shapes, n, [3 * n, 3 * n], name=name)
    return outs[:n], outs[n:]


def _pair_join(fs, *, name):
    n = len(fs)

    def body(*refs):
        ins, outs = refs[:n], refs[n:2 * n]
        lsem, ssem, rsem = refs[2 * n:]
        x, y, c, _ = _place()
        sib = (x, y, 1 - c)
        cps = []
        for a in range(n):
            lc = pltpu.make_async_copy(ins[a], outs[a].at[pl.ds(c, 1)], lsem.at[a])
            lc.start()
            cp = _rcopy(ins[a], outs[a].at[pl.ds(c, 1)], ssem.at[a], rsem.at[a], sib)
            cp.start()
            cps.append((lc, cp))
        for lc, cp in cps:
            cp.wait_send()
            cp.wait_recv()
            lc.wait()

    shapes = [jax.ShapeDtypeStruct((2,) + f.shape[1:], f.dtype) for f in fs]
    return _comm_call(body, fs, shapes, n, [n, n], name=name)


def _reduce_scatter(grads, exch_bf16, tag):
    n = len(grads)
    mine, theirs = _pair_split(grads, name=f"rs_pair_split_{tag}")
    hf, hx = [], []
    for a in range(n):
        r, c = grads[a].shape[2:]
        res = _add2(mine[a].reshape(4 * r, c), theirs[a].reshape(4 * r, c), cast=exch_bf16[a], name=f"rs_pair_sum_{tag}{a}")
        hf.append(res[0].reshape(4, 1, r, c))
        hx.append((res[1] if exch_bf16[a] else res[0]).reshape(4, 1, r, c))
    own, got = _chip_exchange(hx, hf, name=f"rs_chip_exchange_{tag}")
    fs = []
    for a in range(n):
        r, c = grads[a].shape[2:]
        f = _add4(own[a].reshape(r, c), got[a].reshape(3, r, c), name=f"rs_chip_sum_{tag}{a}")
        fs.append(f.reshape(1, r, c))
    return _pair_join(fs, name=f"rs_pair_join_{tag}")


def _interleave(a, B, L):
    return a.reshape(B, L, -1).transpose(1, 0, 2).reshape(B * L, -1)


def _deinterleave(a, B, L):
    return a.reshape(L, B, -1).transpose(1, 0, 2).reshape(B * L, -1)


def _local_step(x, tgt, W, S):
    B, L, D = x.shape
    T = B * L
    G = D // SSM_GROUP
    Pst = SSM_STATE
    hpg = D // HEAD_DIM
    HW = hpg * HEAD_DIM
    ncl = G // GROUPS_PER_CLUSTER
    x2 = x.reshape(T, D)
    tgt2 = tgt.reshape(T, D)

    disc = lambda *p: _s5_discretize(*p)
    (ab_r, ab_i, bb_r, bb_i), disc_vjp = jax.vjp(disc, S["lam_re"], S["lam_im"], S["log_dt"], S["b_re"], S["b_im"])
    wb = jnp.concatenate([_blockdiag(jnp.transpose(bb_r, (0, 2, 1))), _blockdiag(jnp.transpose(bb_i, (0, 2, 1)))],
                         axis=-1).astype(MXU_DTYPE)
    wc = jnp.concatenate([_blockdiag(jnp.transpose(S["c_re"], (0, 2, 1))), _blockdiag(-jnp.transpose(S["c_im"], (0, 2, 1)))],
                         axis=1).astype(MXU_DTYPE)
    cs = GROUPS_PER_CLUSTER * Pst
    slab = lambda ab: jnp.tile(jnp.transpose(ab.reshape(ncl, cs // LANES, LANES), (1, 0, 2)), (1, B, 1))
    a_r, a_i = slab(ab_r), slab(ab_i)
    d_row = S["d"].reshape(1, D)

    xi = _interleave(x2, B, L)
    y, yg, h_r, h_i = _s5_fwd(xi, wb, wc, a_r, a_i, d_row, B, name="s5_fwd")
    z = _mm_nn(yg, W["w_glu"], bias=S["b_glu"].reshape(1, D), name="glu_z")
    gate = _glu_gate(y, z, name="glu_gate")
    mix_i = _mm_nn(gate, W["w_out"], name="s5_out")
    mix = _deinterleave(mix_i, B, L)
    h1, h1b, xh1, rs1 = _ln_fwd(x2, mix, S["ln_gain"][0, 0][None], S["ln_bias"][0, 0][None], name="ln_fwd_0a")

    def ffn_fwd(hb, l):
        hc = _mm_nn(hb, W[f"w_up{l}"], name=f"ffn_up_{l}")
        a = _conv_glu_fwd(hc, S["conv_w"][l], S["conv_b"][l][None], L, name=f"ffn_conv_{l}")
        f = _mm_nn(a, W[f"w_down{l}"], name=f"ffn_down_{l}")
        return hc, a, f

    hc0, a0, f0 = ffn_fwd(h1b, 0)
    h2, h2b, xh2, rs2 = _ln_fwd(h1, f0, S["ln_gain"][0, 1][None], S["ln_bias"][0, 1][None], name="ln_fwd_0b")

    kv = _mm_nn(h2b, W["w_kv"], name="attn_kv")
    q = _mm_nn(h2b, W["w_q"], name="attn_q")
    bias = _attn_bias(S["rel_bias"], hpg)
    o3, l3 = _attn_fwd(q, kv, bias, L, hpg, name="attn_fwd")
    o, ob, lse = _attn_merge(o3, l3, HW, name="attn_merge")
    att = _mm_nn(ob, W["w_ao"], name="attn_out")
    h3, h3b, xh3, rs3 = _ln_fwd(h2, att, S["ln_gain"][1, 0][None], S["ln_bias"][1, 0][None], name="ln_fwd_1a")
    hc1, a1, f1 = ffn_fwd(h3b, 1)
    h4, _, xh4, rs4 = _ln_fwd(h3, f1, S["ln_gain"][1, 1][None], S["ln_bias"][1, 1][None], name="ln_fwd_1b")

    dh4, lrow = _loss_grad(h4, tgt2, name="loss")
    loss = lrow[0, 0]

    GW, GS = {}, {}

    def ffn_bwd(dzb, hb, hc, a, l):
        da = _mm_nt(dzb, W[f"w_down{l}"], name=f"ffn_down_bwd_x_{l}")
        GW[f"w_down{l}"] = _tn(a, dzb, ptotal=1, np_cols=D, name=f"ffn_down_bwd_w_{l}")
        dc, dcw, dcb = _conv_glu_bwd(hc, da, S["conv_w"][l], S["conv_b"][l][None], L, name=f"ffn_conv_bwd_{l}")
        dhc = _conv_bwd_input(dc, S["conv_w"][l], L, name=f"ffn_conv_bwd_x_{l}")
        dh = _mm_nt(dhc, W[f"w_up{l}"], name=f"ffn_up_bwd_x_{l}")
        GW[f"w_up{l}"] = _tn(hb, dhc, ptotal=W[f"w_up{l}"].shape[0], np_cols=W[f"w_up{l}"].shape[2], name=f"ffn_up_bwd_w_{l}")
        return dh, dcw, dcb

    dz4, dz4b, dg4, db4 = _ln_bwd([dh4], [1.0], xh4, rs4, S["ln_gain"][1, 1][None], name="ln_bwd_1b")
    dh3f, dcw1, dcb1 = ffn_bwd(dz4b, h3b, hc1, a1, 1)
    dz3, dz3b, dg3, db3 = _ln_bwd([dz4, dh3f], [DN_ALPHA, 1.0], xh3, rs3, S["ln_gain"][1, 0][None], name="ln_bwd_1a")
    do = _mm_nt(dz3b, W["w_ao"], name="attn_out_bwd_x")
    GW["w_ao"] = _tn(ob, dz3b, ptotal=1, np_cols=D, name="attn_out_bwd_w")
    dq, dk, dv, ds_sum = _attn_bwd(q, kv, do, o, lse, bias, L, hpg, name="attn_bwd")
    GS["rel_bias"] = _bias_grad(ds_sum, hpg, name="attn_bias_grad")
    GW["w_q"] = _tn(h2b, dq, ptotal=W["w_q"].shape[0], np_cols=W["w_q"].shape[2], name="attn_q_bwd_w")
    pkv, npkv = W["w_kv"].shape[0], W["w_kv"].shape[2]
    gkv = _tn(h2b, dk, ptotal=pkv, np_cols=npkv, p0=0, name="attn_k_bwd_w")
    GW["w_kv"] = _tn(h2b, dv, ptotal=pkv, np_cols=npkv, p0=pkv // 2, prev=gkv, name="attn_v_bwd_w")
    dh2q = _mm_nt(dq, W["w_q"], name="attn_q_bwd_x")
    dh2k = _mm_nt(dk, W["w_kv"], p0=0, pn=pkv // 2, name="attn_k_bwd_x")
    dh2v = _mm_nt(dv, W["w_kv"], p0=pkv // 2, pn=pkv // 2, name="attn_v_bwd_x")

    dz2, dz2b, dg2, db2 = _ln_bwd([dz3, dh2q, dh2k, dh2v], [DN_ALPHA, 1.0, 1.0, 1.0], xh2, rs2,
                                  S["ln_gain"][0, 1][None], name="ln_bwd_0b")
    dh1f, dcw0, dcb0 = ffn_bwd(dz2b, h1b, hc0, a0, 0)
    dz1, dz1b, dg1, db1 = _ln_bwd([dz2, dh1f], [DN_ALPHA, 1.0], xh1, rs1, S["ln_gain"][0, 0][None], name="ln_bwd_0a")
    dmix_i = _interleave(dz1b, B, L)
    dgate = _mm_nt(dmix_i, W["w_out"], name="s5_out_bwd_x")
    GW["w_out"] = _tn(gate, dmix_i, ptotal=1, np_cols=D, name="s5_out_bwd_w")
    dzg, dyg1, dbglu = _glu_bwd(y, z, dgate, name="glu_bwd")
    dyg2 = _mm_nt(dzg, W["w_glu"], name="glu_z_bwd_x")
    GW["w_glu"] = _tn(yg, dzg, ptotal=1, np_cols=D, name="glu_z_bwd_w")
    dy = _gelu_bwd(y, dyg1, dyg2, name="gelu_bwd")
    du_i, g_r, g_i, dar, dai, dd = _s5_bwd(dy, xi, h_r, h_i, wb, wc, a_r, a_i, d_row, B, name="s5_bwd")
    dwb_r = _cluster_tn(xi, g_r, ncl, tok_left=True, name="s5_b_grad_re")
    dwb_i = _cluster_tn(xi, g_i, ncl, tok_left=True, name="s5_b_grad_im")
    dwc_r = _cluster_tn(dy, h_r, ncl, tok_left=False, name="s5_c_grad_re")
    dwc_i = _cluster_tn(dy, h_i, ncl, tok_left=False, name="s5_c_grad_im")
    grad_x = _axpy(dz1, _deinterleave(du_i, B, L), DN_ALPHA, name="grad_x")

    dbb_r = jnp.transpose(_unblockdiag(dwb_r, SSM_GROUP, Pst), (0, 2, 1))
    dbb_i = jnp.transpose(_unblockdiag(dwb_i, SSM_GROUP, Pst), (0, 2, 1))
    unslab = lambda da: jnp.transpose(da.reshape(cs // LANES, B, ncl, LANES).sum(1), (1, 0, 2)).reshape(G, Pst)
    dab_r, dab_i = unslab(dar), unslab(dai)
    GS["lam_re"], GS["lam_im"], GS["log_dt"], GS["b_re"], GS["b_im"] = disc_vjp((dab_r, dab_i, dbb_r, dbb_i))
    GS["c_re"] = jnp.transpose(_unblockdiag(dwc_r, Pst, SSM_GROUP), (0, 2, 1))
    GS["c_im"] = -jnp.transpose(_unblockdiag(dwc_i, Pst, SSM_GROUP), (0, 2, 1))
    GS["d"] = dd.reshape(G, SSM_GROUP)
    GS["b_glu"] = dbglu.reshape(D)
    GS["conv_w"] = jnp.stack([dcw0, dcw1])
    GS["conv_b"] = jnp.stack([dcb0[0], dcb1[0]])
    GS["ln_gain"] = jnp.stack([jnp.stack([dg1[0], dg2[0]]), jnp.stack([dg3[0], dg4[0]])])
    GS["ln_bias"] = jnp.stack([jnp.stack([db1[0], db2[0]]), jnp.stack([db3[0], db4[0]])])
    return loss, grad_x.reshape(B, L, D), GW, GS


SMALL_REPLICATED = ("lam_re", "lam_im", "log_dt", "b_re", "b_im", "c_re", "c_im", "d", "rel_bias", "conv_b")
SMALL_SHARDED = ("b_glu", "conv_w", "ln_gain", "ln_bias")
SMALL_ORDER = SMALL_REPLICATED + SMALL_SHARDED


def _pack(arrs, lanes, row_mult):
    flat = jnp.concatenate([a.reshape(-1).astype(F32) for a in arrs])
    rows = -(-flat.shape[0] // lanes)
    rows = -(-rows // row_mult) * row_mult
    return jnp.pad(flat, (0, rows * lanes - flat.shape[0])).reshape(rows, lanes)


def _unpack(packed, shapes):
    flat = packed.reshape(-1)
    out, off = [], 0
    for s in shapes:
        n = int(np.prod(s))
        out.append(flat[off:off + n].reshape(s))
        off += n
    return out


def kernel(x, s5_lam_re, s5_lam_im, s5_log_dt, s5_b_re, s5_b_im, s5_c_re, s5_c_im, s5_d, s5_w_glu, s5_b_glu, s5_w_out, attn_w_kv, attn_w_q, attn_w_out, rel_bias, ffn_w_up, ffn_conv_w, ffn_conv_b, ffn_w_down, ln_gain, ln_bias, loss_target, m_s5_lam_re, m_s5_lam_im, m_s5_log_dt, m_s5_b_re, m_s5_b_im, m_s5_c_re, m_s5_c_im, m_s5_d, m_s5_w_glu, m_s5_b_glu, m_s5_w_out, m_attn_w_kv, m_attn_w_q, m_attn_w_out, m_rel_bias, m_ffn_w_up, m_ffn_conv_w, m_ffn_conv_b, m_ffn_w_down, m_ln_gain, m_ln_bias, v_s5_lam_re, v_s5_lam_im, v_s5_log_dt, v_s5_b_re, v_s5_b_im, v_s5_c_re, v_s5_c_im, v_s5_d, v_s5_w_glu, v_s5_b_glu, v_s5_w_out, v_attn_w_kv, v_attn_w_q, v_attn_w_out, v_rel_bias, v_ffn_w_up, v_ffn_conv_w, v_ffn_conv_b, v_ffn_w_down, v_ln_gain, v_ln_bias):
    names = ["s5_lam_re", "s5_lam_im", "s5_log_dt", "s5_b_re", "s5_b_im", "s5_c_re", "s5_c_im", "s5_d", "s5_w_glu",
             "s5_b_glu", "s5_w_out", "attn_w_kv", "attn_w_q", "attn_w_out", "rel_bias", "ffn_w_up", "ffn_conv_w",
             "ffn_conv_b", "ffn_w_down", "ln_gain", "ln_bias"]
    loc = locals()
    w_in = {n: loc[n] for n in names}
    m_in = {n: loc["m_" + n] for n in names}
    v_in = {n: loc["v_" + n] for n in names}
    px = lax.axis_index("x")
    py = lax.axis_index("y")
    chip = 2 * px + py

    big = [("w_glu", s5_w_glu[0], False), ("w_out", s5_w_out[0], False), ("w_ao", attn_w_out[0], False),
           ("w_kv", attn_w_kv, True), ("w_q", attn_w_q[0], True),
           ("w_up0", ffn_w_up[0], True), ("w_up1", ffn_w_up[1], True),
           ("w_down0", ffn_w_down[0], False), ("w_down1", ffn_w_down[1], False)]
    big_src = {"w_glu": ("s5_w_glu", 0), "w_out": ("s5_w_out", 0), "w_ao": ("attn_w_out", 0), "w_kv": ("attn_w_kv", None),
               "w_q": ("attn_w_q", 0), "w_up0": ("ffn_w_up", 0), "w_up1": ("ffn_w_up", 1),
               "w_down0": ("ffn_w_down", 0), "w_down1": ("ffn_w_down", 1)}

    small_sh = {"b_glu": s5_b_glu[0], "conv_w": ffn_conv_w, "ln_gain": ln_gain, "ln_bias": ln_bias}
    sh_shapes = [small_sh[k].shape for k in SMALL_SHARDED]
    sh_pack = _pack([small_sh[k] for k in SMALL_SHARDED], 128, 16)

    shards = [w.astype(MXU_DTYPE).reshape(2, w.shape[0] // 2, w.shape[1]) for _, w, _ in big]
    shards.append(sh_pack.reshape(2, sh_pack.shape[0] // 2, 128))
    gathered = _all_gather(shards, name="weights_all_gather")

    W = {}
    for (key, w, colsh), g in zip(big, gathered[:-1]):
        r, c = w.shape
        W[key] = g.reshape(4, r, c) if colsh else g.reshape(1, 4 * r, c)
    parts = [_unpack(gathered[-1][p], sh_shapes) for p in range(4)]
    S = {k: jnp.concatenate([parts[p][i] for p in range(4)], axis=-1) for i, k in enumerate(SMALL_SHARDED)}
    S.update(lam_re=s5_lam_re[0], lam_im=s5_lam_im[0], log_dt=s5_log_dt[0], b_re=s5_b_re[0], b_im=s5_b_im[0],
             c_re=s5_c_re[0], c_im=s5_c_im[0], d=s5_d[0], rel_bias=rel_bias, conv_b=ffn_conv_b)

    loss, grad_x, GW, GS = _local_step(x, loss_target, W, S)
    loss = lax.psum(loss, ("x", "y", "c"))

    gs_shapes = [GS[k].shape for k in SMALL_ORDER]
    gs_pack = _pack([GS[k] for k in SMALL_ORDER], 128, 64)
    rs = gs_pack.shape[0] // 8
    grads = []
    for key, w, colsh in big:
        r, c = w.shape
        grads.append(GW[key].reshape(4, 2, r // 2, c))
    grads.append(gs_pack.reshape(4, 2, rs, 128))
    red = _reduce_scatter(grads, [True] * len(big) + [False], "g")
    small_all = _all_gather([red[-1]], name="small_grads_all_gather")[0]
    gsmall = dict(zip(SMALL_ORDER, _unpack(small_all, gs_shapes)))

    out_g, out_d, out_m, out_v = {}, {}, {}, {}
    big_res = {}
    for (key, w, colsh), gr in zip(big, red[:-1]):
        r, c = w.shape
        src, l = big_src[key]
        mm = m_in[src] if l is None else m_in[src][l]
        vv = v_in[src] if l is None else v_in[src][l]
        g2 = gr.reshape(r, c)
        dlt, nm, nv = _adamw(w, g2, mm.reshape(r, c), vv.reshape(r, c), name=f"adamw_{key}")
        big_res[key] = (g2, dlt, nm, nv)

    def big_out(i):
        o = {}
        o["s5_w_glu"] = big_res["w_glu"][i][None]
        o["s5_w_out"] = big_res["w_out"][i][None]
        o["attn_w_out"] = big_res["w_ao"][i][None]
        o["attn_w_kv"] = big_res["w_kv"][i]
        o["attn_w_q"] = big_res["w_q"][i][None]
        o["ffn_w_up"] = jnp.stack([big_res["w_up0"][i], big_res["w_up1"][i]])
        o["ffn_w_down"] = jnp.stack([big_res["w_down0"][i], big_res["w_down1"][i]])
        return o

    small_w = {"lam_re": s5_lam_re, "lam_im": s5_lam_im, "log_dt": s5_log_dt, "b_re": s5_b_re, "b_im": s5_b_im,
               "c_re": s5_c_re, "c_im": s5_c_im, "d": s5_d, "rel_bias": rel_bias, "conv_b": ffn_conv_b,
               "b_glu": s5_b_glu, "conv_w": ffn_conv_w, "ln_gain": ln_gain, "ln_bias": ln_bias}
    small_name = {"lam_re": "s5_lam_re", "lam_im": "s5_lam_im", "log_dt": "s5_log_dt", "b_re": "s5_b_re", "b_im": "s5_b_im",
                  "c_re": "s5_c_re", "c_im": "s5_c_im", "d": "s5_d", "rel_bias": "rel_bias", "conv_b": "ffn_conv_b",
                  "b_glu": "s5_b_glu", "conv_w": "ffn_conv_w", "ln_gain": "ln_gain", "ln_bias": "ln_bias"}
    sg = {}
    for k in SMALL_ORDER:
        shp = small_w[k].shape
        g = gsmall[k]
        if k in SMALL_SHARDED:
            width = shp[-1]
            g = lax.dynamic_slice_in_dim(g, chip * width, width, axis=g.ndim - 1)
        sg[k] = g.reshape(shp)
    sshapes = [small_w[k].shape for k in SMALL_ORDER]
    pw = _pack([small_w[k] for k in SMALL_ORDER], 128, 8)
    pg = _pack([sg[k] for k in SMALL_ORDER], 128, 8)
    pm = _pack([m_in[small_name[k]] for k in SMALL_ORDER], 128, 8)
    pv = _pack([v_in[small_name[k]] for k in SMALL_ORDER], 128, 8)
    sd, snm, snv = _adamw(pw, pg, pm, pv, name="adamw_small")
    sd = dict(zip(SMALL_ORDER, _unpack(sd, sshapes)))
    snm = dict(zip(SMALL_ORDER, _unpack(snm, sshapes)))
    snv = dict(zip(SMALL_ORDER, _unpack(snv, sshapes)))

    res = [{}, {}, {}, {}]
    for i in range(4):
        res[i].update(big_out(i))
    for k in SMALL_ORDER:
        res[0][small_name[k]] = sg[k]
        res[1][small_name[k]] = sd[k]
        res[2][small_name[k]] = snm[k]
        res[3][small_name[k]] = snv[k]
    outs = [loss, grad_x]
    for i in range(4):
        outs += [res[i][n] for n in names]
    return tuple(outs)
```

```python
import functools
import math

import numpy as np
import jax
import jax.numpy as jnp
from jax import lax
from jax.experimental import pallas as pl
from jax.experimental.pallas import tpu as pltpu

F32 = jnp.float32
BF16 = jnp.bfloat16
MXU_DTYPE = jnp.bfloat16
V7X_VMEM_LIMIT_BYTES = 52 << 20
MESH = pl.DeviceIdType.MESH

DEPTH = 2
SSM_GROUP = 16
SSM_STATE = 64
GROUPS_PER_CLUSTER = 16
CLUSTER_W = GROUPS_PER_CLUSTER * SSM_GROUP
HEAD_DIM = 64
DILATIONS = (1, 4, 16)
BAND = 128
NEG_BIG = -1e30
REL_BUCKETS = 32
REL_MAX_DIST = 2048
DN_ALPHA = (2.0 * DEPTH) ** 0.25
LN_EPS = 1e-5
ADAM_LR, ADAM_B1, ADAM_B2, ADAM_EPS, ADAM_WD, ADAM_STEP = 0.001, 0.9, 0.999, 1e-08, 0.01, 10
GELU_K = math.sqrt(2.0 / math.pi)
GELU_C = 0.044715


def _pallas(body, **kw):
    return pl.pallas_call(body, **kw)


def _params(sem=None):
    return pltpu.CompilerParams(dimension_semantics=sem, vmem_limit_bytes=V7X_VMEM_LIMIT_BYTES)


def _pick(n, cands):
    for c in cands:
        if n % c == 0:
            return c
    return n


def _sigmoid(z):
    return 1.0 / (1.0 + jnp.exp(-z))


def _gelu(y):
    return 0.5 * y * (1.0 + jnp.tanh(GELU_K * (y + GELU_C * y * y * y)))


def _gelu_grad(y):
    t = jnp.tanh(GELU_K * (y + GELU_C * y * y * y))
    return 0.5 * (1.0 + t) + 0.5 * y * (1.0 - t * t) * (GELU_K * (1.0 + 3.0 * GELU_C * y * y))


def _mm_nn(a, w, *, bias=None, out_dtype=F32, name):
    T, K = a.shape
    P, _, Np = w.shape
    tm = _pick(T, (512, 256, 128))
    tn = _pick(Np, (1408, 1024, 768, 512, 384, 256, 128))
    nj = Np // tn

    def body(*refs):
        if bias is None:
            a_ref, w_ref, o_ref = refs
        else:
            a_ref, w_ref, b_ref, o_ref = refs
        acc = jnp.dot(a_ref[...].astype(MXU_DTYPE), w_ref[...].astype(MXU_DTYPE), preferred_element_type=F32)
        if bias is not None:
            acc = acc + b_ref[...]
        o_ref[...] = acc.astype(o_ref.dtype)

    in_specs = [pl.BlockSpec((tm, K), lambda i, p, j: (i, 0)),
                pl.BlockSpec((None, K, tn), lambda i, p, j: (p, 0, j))]
    args = [a, w]
    if bias is not None:
        in_specs.append(pl.BlockSpec((1, tn), lambda i, p, j: (0, p * nj + j)))
        args.append(bias)
    return _pallas(
        body, name=name, grid=(T // tm, P, nj), in_specs=in_specs,
        out_specs=pl.BlockSpec((tm, tn), lambda i, p, j: (i, p * nj + j)),
        out_shape=jax.ShapeDtypeStruct((T, P * Np), out_dtype),
        compiler_params=_params(("parallel", "parallel", "parallel")),
    )(*args)


def _mm_nt(a, w, *, p0=0, pn=None, name):
    T = a.shape[0]
    _, K, Np = w.shape
    pn = w.shape[0] if pn is None else pn
    tm = _pick(T, (256, 128))
    tn = _pick(Np, (1536, 1408, 1024, 768, 512, 384, 256, 128))
    nj = Np // tn
    nred = pn * nj

    def body(a_ref, w_ref, o_ref, acc):
        r = pl.program_id(1)

        @pl.when(r == 0)
        def _():
            acc[...] = jnp.zeros_like(acc)

        acc[...] += lax.dot_general(a_ref[...].astype(MXU_DTYPE), w_ref[...].astype(MXU_DTYPE),
                                    (((1,), (1,)), ((), ())), preferred_element_type=F32)

        @pl.when(r == nred - 1)
        def _():
            o_ref[...] = acc[...]

    return _pallas(
        body, name=name, grid=(T // tm, nred),
        in_specs=[pl.BlockSpec((tm, tn), lambda i, r: (i, r)),
                  pl.BlockSpec((None, K, tn), lambda i, r: (p0 + r // nj, 0, r % nj))],
        out_specs=pl.BlockSpec((tm, K), lambda i, r: (i, 0)),
        out_shape=jax.ShapeDtypeStruct((T, K), F32),
        scratch_shapes=[pltpu.VMEM((tm, K), F32)],
        compiler_params=_params(("parallel", "arbitrary")),
    )(a, w)


def _tn(a, b, *, ptotal, np_cols, p0=0, prev=None, name):
    T, K = a.shape
    Np = np_cols
    pn = b.shape[1] // Np
    tt = _pick(T, (1024, 512, 256, 128))
    tk = _pick(K, (512, 256, 128))
    tn = _pick(Np, (1408, 768, 512, 256, 128))
    nj = Np // tn
    nt = T // tt

    def body(*refs):
        a_ref, b_ref = refs[0], refs[1]
        o_ref, acc = refs[-2], refs[-1]
        t = pl.program_id(3)

        @pl.when(t == 0)
        def _():
            acc[...] = jnp.zeros_like(acc)

        acc[...] += lax.dot_general(a_ref[...].astype(MXU_DTYPE), b_ref[...].astype(MXU_DTYPE),
                                    (((0,), (0,)), ((), ())), preferred_element_type=F32)

        @pl.when(t == nt - 1)
        def _():
            o_ref[...] = acc[...]

    in_specs = [pl.BlockSpec((tt, tk), lambda kb, p, j, t: (t, kb)),
                pl.BlockSpec((tt, tn), lambda kb, p, j, t: (t, p * nj + j))]
    args = [a, b]
    aliases = {}
    if prev is not None:
        in_specs.append(pl.BlockSpec(memory_space=pl.ANY))
        args.append(prev)
        aliases = {2: 0}
    return _pallas(
        body, name=name, grid=(K // tk, pn, nj, nt), in_specs=in_specs,
        out_specs=pl.BlockSpec((None, tk, tn), lambda kb, p, j, t: (p0 + p, kb, j)),
        out_shape=jax.ShapeDtypeStruct((ptotal, K, Np), F32),
        scratch_shapes=[pltpu.VMEM((tk, tn), F32)],
        input_output_aliases=aliases,
        compiler_params=_params(("parallel", "parallel", "parallel", "arbitrary")),
    )(*args)


def _rows(tm, f):
    return pl.BlockSpec((tm, f), lambda i: (i, 0))


def _whole(shape):
    nd = len(shape)
    return pl.BlockSpec(shape, lambda i: (0,) * nd)


def _ln_fwd(xres, f, gain, bias, *, name):
    T, D = xres.shape
    tm = _pick(T, (256, 128))

    def body(x_ref, f_ref, g_ref, b_ref, y_ref, yb_ref, xh_ref, rs_ref):
        z = DN_ALPHA * x_ref[...] + f_ref[...]
        mu = jnp.mean(z, axis=-1, keepdims=True)
        zc = z - mu
        var = jnp.mean(zc * zc, axis=-1, keepdims=True)
        rstd = lax.rsqrt(var + LN_EPS)
        xh = zc * rstd
        y = xh * g_ref[...] + b_ref[...]
        y_ref[...] = y
        yb_ref[...] = y.astype(yb_ref.dtype)
        xh_ref[...] = xh
        rs_ref[...] = rstd

    return _pallas(
        body, name=name, grid=(T // tm,),
        in_specs=[_rows(tm, D), _rows(tm, D), _whole((1, D)), _whole((1, D))],
        out_specs=[_rows(tm, D), _rows(tm, D), _rows(tm, D), _rows(tm, 1)],
        out_shape=[jax.ShapeDtypeStruct((T, D), F32), jax.ShapeDtypeStruct((T, D), MXU_DTYPE),
                   jax.ShapeDtypeStruct((T, D), F32), jax.ShapeDtypeStruct((T, 1), F32)],
        compiler_params=_params(("parallel",)),
    )(xres, f, gain, bias)


def _ln_bwd(addends, coefs, xhat, rstd, gain, *, name):
    T, D = xhat.shape
    tm = _pick(T, (256, 128))
    n = len(addends)

    def body(*refs):
        adds = refs[:n]
        xh_ref, rs_ref, g_ref, dz_ref, dzb_ref, dg_ref, db_ref = refs[n:]
        dy = coefs[0] * adds[0][...]
        for c, r in zip(coefs[1:], adds[1:]):
            dy = dy + c * r[...]
        xh = xh_ref[...]
        dxh = dy * g_ref[...]
        m1 = jnp.mean(dxh, axis=-1, keepdims=True)
        m2 = jnp.mean(dxh * xh, axis=-1, keepdims=True)
        dz = rs_ref[...] * (dxh - m1 - xh * m2)
        dz_ref[...] = dz
        dzb_ref[...] = dz.astype(dzb_ref.dtype)

        @pl.when(pl.program_id(0) == 0)
        def _():
            dg_ref[...] = jnp.zeros_like(dg_ref)
            db_ref[...] = jnp.zeros_like(db_ref)

        dg_ref[...] += jnp.sum(dy * xh, axis=0, keepdims=True)
        db_ref[...] += jnp.sum(dy, axis=0, keepdims=True)

    return _pallas(
        body, name=name, grid=(T // tm,),
        in_specs=[_rows(tm, D)] * n + [_rows(tm, D), _rows(tm, 1), _whole((1, D))],
        out_specs=[_rows(tm, D), _rows(tm, D), _whole((1, D)), _whole((1, D))],
        out_shape=[jax.ShapeDtypeStruct((T, D), F32), jax.ShapeDtypeStruct((T, D), MXU_DTYPE),
                   jax.ShapeDtypeStruct((1, D), F32), jax.ShapeDtypeStruct((1, D), F32)],
        compiler_params=_params(("arbitrary",)),
    )(*addends, xhat, rstd, gain)


def _loss_grad(y, tgt, *, name):
    T, D = y.shape
    tm = _pick(T, (256, 128))

    def body(y_ref, t_ref, dy_ref, l_ref):
        e = y_ref[...] - t_ref[...]
        dy_ref[...] = e * (1.0 / D)

        @pl.when(pl.program_id(0) == 0)
        def _():
            l_ref[...] = jnp.zeros_like(l_ref)

        l_ref[...] += jnp.zeros_like(l_ref) + jnp.sum(e * e) * (0.5 / D)

    return _pallas(
        body, name=name, grid=(T // tm,),
        in_specs=[_rows(tm, D), _rows(tm, D)],
        out_specs=[_rows(tm, D), _whole((1, 128))],
        out_shape=[jax.ShapeDtypeStruct((T, D), F32), jax.ShapeDtypeStruct((1, 128), F32)],
        compiler_params=_params(("arbitrary",)),
    )(y, tgt)


def _axpy(a, b, ca, *, name):
    T, D = a.shape
    tm = _pick(T, (256, 128))

    def body(a_ref, b_ref, o_ref):
        o_ref[...] = ca * a_ref[...] + b_ref[...]

    return _pallas(
        body, name=name, grid=(T // tm,), in_specs=[_rows(tm, D), _rows(tm, D)], out_specs=_rows(tm, D),
        out_shape=jax.ShapeDtypeStruct((T, D), F32), compiler_params=_params(("parallel",)),
    )(a, b)


def _glu_gate(y, z, *, name):
    T, D = y.shape
    tm = _pick(T, (256, 128))

    def body(y_ref, z_ref, g_ref):
        g_ref[...] = (_gelu(y_ref[...]) * _sigmoid(z_ref[...])).astype(g_ref.dtype)

    return _pallas(
        body, name=name, grid=(T // tm,), in_specs=[_rows(tm, D), _rows(tm, D)], out_specs=_rows(tm, D),
        out_shape=jax.ShapeDtypeStruct((T, D), MXU_DTYPE), compiler_params=_params(("parallel",)),
    )(y, z)


def _glu_bwd(y, z, dg, *, name):
    T, D = y.shape
    tm = _pick(T, (256, 128))

    def body(y_ref, z_ref, dg_ref, dzb_ref, dyg_ref, db_ref):
        s = _sigmoid(z_ref[...])
        dg = dg_ref[...]
        dz = dg * _gelu(y_ref[...]) * s * (1.0 - s)
        dzb_ref[...] = dz.astype(dzb_ref.dtype)
        dyg_ref[...] = dg * s

        @pl.when(pl.program_id(0) == 0)
        def _():
            db_ref[...] = jnp.zeros_like(db_ref)

        db_ref[...] += jnp.sum(dz, axis=0, keepdims=True)

    return _pallas(
        body, name=name, grid=(T // tm,), in_specs=[_rows(tm, D)] * 3,
        out_specs=[_rows(tm, D), _rows(tm, D), _whole((1, D))],
        out_shape=[jax.ShapeDtypeStruct((T, D), MXU_DTYPE), jax.ShapeDtypeStruct((T, D), F32),
                   jax.ShapeDtypeStruct((1, D), F32)],
        compiler_params=_params(("arbitrary",)),
    )(y, z, dg)


def _gelu_bwd(y, d1, d2, *, name):
    T, D = y.shape
    tm = _pick(T, (256, 128))

    def body(y_ref, a_ref, b_ref, o_ref):
        o_ref[...] = (a_ref[...] + b_ref[...]) * _gelu_grad(y_ref[...])

    return _pallas(
        body, name=name, grid=(T // tm,), in_specs=[_rows(tm, D)] * 3, out_specs=_rows(tm, D),
        out_shape=jax.ShapeDtypeStruct((T, D), F32), compiler_params=_params(("parallel",)),
    )(y, d1, d2)


CONV_ROWS = 128


def _shift_back(x, edge, at_start, tm):
    rows = lax.broadcasted_iota(jnp.int32, x.shape, 0)
    keep = jnp.where(at_start, 0.0, 1.0)
    e7 = edge[7:8, :] * keep
    e6 = edge[6:7, :] * keep
    r1 = pltpu.roll(x, 1, 0)
    r2 = pltpu.roll(x, 2, 0)
    x1 = jnp.where(rows == 0, e7, r1)
    x2 = jnp.where(rows == 0, e6, jnp.where(rows == 1, e7, r2))
    return x1, x2


def _conv_specs(T, F2, tm):
    return [_rows(tm, F2),
            pl.BlockSpec((8, F2), lambda i: (jnp.maximum(i * (tm // 8) - 1, 0), 0))]


def _conv_glu_fwd(hc, conv_w, conv_b, L, *, name):
    T, F2 = hc.shape
    F = F2 // 2
    tm = CONV_ROWS

    def body(x_ref, e_ref, w_ref, b_ref, a_ref):
        at_start = (pl.program_id(0) * tm) % L == 0
        x = x_ref[...]
        x1, x2 = _shift_back(x, e_ref[...], at_start, tm)
        c = b_ref[...] + w_ref[0:1, :] * x + w_ref[1:2, :] * x1 + w_ref[2:3, :] * x2
        val, gate = c[:, :F], c[:, F:]
        a_ref[...] = (gate * _sigmoid(gate) * val).astype(a_ref.dtype)

    return _pallas(
        body, name=name, grid=(T // tm,),
        in_specs=_conv_specs(T, F2, tm) + [_whole((3, F2)), _whole((1, F2))],
        out_specs=_rows(tm, F),
        out_shape=jax.ShapeDtypeStruct((T, F), MXU_DTYPE), compiler_params=_params(("parallel",)),
    )(hc, hc, conv_w, conv_b)


def _conv_glu_bwd(hc, da, conv_w, conv_b, L, *, name):
    T, F2 = hc.shape
    F = F2 // 2
    tm = CONV_ROWS

    def body(x_ref, e_ref, da_ref, w_ref, b_ref, dc_ref, dw_ref, db_ref):
        at_start = (pl.program_id(0) * tm) % L == 0
        x = x_ref[...]
        x1, x2 = _shift_back(x, e_ref[...], at_start, tm)
        c = b_ref[...] + w_ref[0:1, :] * x + w_ref[1:2, :] * x1 + w_ref[2:3, :] * x2
        val, gate = c[:, :F], c[:, F:]
        s = _sigmoid(gate)
        da = da_ref[...]
        dval = da * (gate * s)
        dgate = da * val * (s * (1.0 + gate * (1.0 - s)))
        dc_ref[:, :F] = dval
        dc_ref[:, F:] = dgate
        dc = dc_ref[...]

        @pl.when(pl.program_id(0) == 0)
        def _():
            dw_ref[...] = jnp.zeros_like(dw_ref)
            db_ref[...] = jnp.zeros_like(db_ref)

        dw_ref[0:1, :] += jnp.sum(dc * x, axis=0, keepdims=True)
        dw_ref[1:2, :] += jnp.sum(dc * x1, axis=0, keepdims=True)
        dw_ref[2:3, :] += jnp.sum(dc * x2, axis=0, keepdims=True)
        db_ref[...] += jnp.sum(dc, axis=0, keepdims=True)

    return _pallas(
        body, name=name, grid=(T // tm,),
        in_specs=_conv_specs(T, F2, tm) + [_rows(tm, F), _whole((3, F2)), _whole((1, F2))],
        out_specs=[_rows(tm, F2), _whole((3, F2)), _whole((1, F2))],
        out_shape=[jax.ShapeDtypeStruct((T, F2), F32), jax.ShapeDtypeStruct((3, F2), F32),
                   jax.ShapeDtypeStruct((1, F2), F32)],
        compiler_params=_params(("arbitrary",)),
    )(hc, hc, da, conv_w, conv_b)


def _conv_bwd_input(dc, conv_w, L, *, name):
    T, F2 = dc.shape
    tm = CONV_ROWS
    last_blk = T // 8 - 1

    def body(x_ref, e_ref, w_ref, o_ref):
        at_end = ((pl.program_id(0) + 1) * tm) % L == 0
        x = x_ref[...]
        rows = lax.broadcasted_iota(jnp.int32, x.shape, 0)
        keep = jnp.where(at_end, 0.0, 1.0)
        e0 = e_ref[0:1, :] * keep
        e1 = e_ref[1:2, :] * keep
        u1 = pltpu.roll(x, tm - 1, 0)
        u2 = pltpu.roll(x, tm - 2, 0)
        x1 = jnp.where(rows == tm - 1, e0, u1)
        x2 = jnp.where(rows == tm - 1, e1, jnp.where(rows == tm - 2, e0, u2))
        o_ref[...] = (w_ref[0:1, :] * x + w_ref[1:2, :] * x1 + w_ref[2:3, :] * x2).astype(o_ref.dtype)

    return _pallas(
        body, name=name, grid=(T // tm,),
        in_specs=[_rows(tm, F2),
                  pl.BlockSpec((8, F2), lambda i: (jnp.minimum((i + 1) * (tm // 8), last_blk), 0)),
                  _whole((3, F2))],
        out_specs=_rows(tm, F2),
        out_shape=jax.ShapeDtypeStruct((T, F2), MXU_DTYPE), compiler_params=_params(("parallel",)),
    )(dc, dc, conv_w)


S5_CHUNK = 128
LANES = 128


def _slab_rows(c, n, ncl):
    return pl.ds(c, n) if ncl == 1 else pl.ds(c, n, stride=ncl)


def _slab_put(ref, c, n, ncl, val):
    for s in range(val.shape[1] // LANES):
        ref[s, _slab_rows(c, n, ncl), :] = val[:, s * LANES:(s + 1) * LANES]


def _slab_get(ref, c, n, ncl):
    return jnp.concatenate([ref[s, _slab_rows(c, n, ncl), :] for s in range(ref.shape[0])], axis=-1)


def _slabs(n_slab, rows):
    return pl.BlockSpec((n_slab, rows, LANES), lambda i: (0, i, 0))


def _s5_fwd(xi, wb, wc, a_r, a_i, d_row, B, *, name):
    T, D = xi.shape
    ncl = wb.shape[0]
    cs = wb.shape[2] // 2
    ns = cs // LANES
    R = B * ncl
    Q = S5_CHUNK
    QR = Q * ncl
    nsteps = Q // B

    def body(x_ref, wb_ref, wc_ref, ar_ref, ai_ref, d_ref, y_ref, yg_ref, hr_ref, hi_ref, bur, bui, cr, ci):
        @pl.when(pl.program_id(0) == 0)
        def _():
            cr[...] = jnp.zeros_like(cr)
            ci[...] = jnp.zeros_like(ci)

        x = x_ref[...]
        xb = x.astype(MXU_DTYPE)
        for c in range(ncl):
            bu = jnp.dot(xb[:, c * CLUSTER_W:(c + 1) * CLUSTER_W], wb_ref[c], preferred_element_type=F32)
            _slab_put(bur, c, Q, ncl, bu[:, :cs])
            _slab_put(bui, c, Q, ncl, bu[:, cs:])
        ar = ar_ref[...]
        ai = ai_ref[...]

        def step(k, carry):
            hr, hi = carry
            sl = pl.ds(pl.multiple_of(k * R, R), R)
            nr = ar * hr - ai * hi + bur[:, sl, :]
            ni = ar * hi + ai * hr + bui[:, sl, :]
            hr_ref[:, sl, :] = nr
            hi_ref[:, sl, :] = ni
            return nr, ni

        hr, hi = lax.fori_loop(0, nsteps, step, (cr[...], ci[...]), unroll=4)
        cr[...] = hr
        ci[...] = hi
        parts = []
        for c in range(ncl):
            hrc = _slab_get(hr_ref, c, Q, ncl).astype(MXU_DTYPE)
            hic = _slab_get(hi_ref, c, Q, ncl).astype(MXU_DTYPE)
            parts.append(jnp.dot(hrc, wc_ref[c, :cs, :], preferred_element_type=F32)
                         + jnp.dot(hic, wc_ref[c, cs:, :], preferred_element_type=F32))
        y = d_ref[...] * x + (parts[0] if ncl == 1 else jnp.concatenate(parts, axis=-1))
        y_ref[...] = y
        yg_ref[...] = _gelu(y).astype(yg_ref.dtype)

    return _pallas(
        body, name=name, grid=(T // Q,),
        in_specs=[_rows(Q, D), _whole(wb.shape), _whole(wc.shape), _whole((ns, R, LANES)), _whole((ns, R, LANES)),
                  _whole((1, D))],
        out_specs=[_rows(Q, D), _rows(Q, D), _slabs(ns, QR), _slabs(ns, QR)],
        out_shape=[jax.ShapeDtypeStruct((T, D), F32), jax.ShapeDtypeStruct((T, D), MXU_DTYPE),
                   jax.ShapeDtypeStruct((ns, T * ncl, LANES), F32), jax.ShapeDtypeStruct((ns, T * ncl, LANES), F32)],
        scratch_shapes=[pltpu.VMEM((ns, QR, LANES), F32), pltpu.VMEM((ns, QR, LANES), F32),
                        pltpu.VMEM((ns, R, LANES), F32), pltpu.VMEM((ns, R, LANES), F32)],
        compiler_params=_params(("arbitrary",)),
    )(xi, wb, wc, a_r, a_i, d_row)


def _s5_bwd(dy, xi, h_r, h_i, wb, wc, a_r, a_i, d_row, B, *, name):
    T, D = dy.shape
    ncl = wb.shape[0]
    cs = wb.shape[2] // 2
    ns = cs // LANES
    R = B * ncl
    Q = S5_CHUNK
    nsteps = Q // B
    nchunk = T // Q
    QR = Q * ncl

    def rev(i):
        return nchunk - 1 - i

    def body(dy_ref, x_ref, hr_ref, hi_ref, pr_ref, pi_ref, wb_ref, wc_ref, ar_ref, ai_ref, d_ref,
             du_ref, gr_ref, gi_ref, dar_ref, dai_ref, dd_ref, dhr, dhi, cr, ci):
        i = pl.program_id(0)

        @pl.when(i == 0)
        def _():
            cr[...] = jnp.zeros_like(cr)
            ci[...] = jnp.zeros_like(ci)
            dar_ref[...] = jnp.zeros_like(dar_ref)
            dai_ref[...] = jnp.zeros_like(dai_ref)
            dd_ref[...] = jnp.zeros_like(dd_ref)

        dyv = dy_ref[...]
        dyb = dyv.astype(MXU_DTYPE)
        for c in range(ncl):
            dh = lax.dot_general(dyb[:, c * CLUSTER_W:(c + 1) * CLUSTER_W], wc_ref[c],
                                 (((1,), (1,)), ((), ())), preferred_element_type=F32)
            _slab_put(dhr, c, Q, ncl, dh[:, :cs])
            _slab_put(dhi, c, Q, ncl, dh[:, cs:])
        ar = ar_ref[...]
        ai = ai_ref[...]

        def step(j, carry):
            gr, gi, sar, sai = carry
            k = nsteps - 1 - j
            sl = pl.ds(pl.multiple_of(k * R, R), R)
            ngr = dhr[:, sl, :] + ar * gr + ai * gi
            ngi = dhi[:, sl, :] - ai * gr + ar * gi
            gr_ref[:, sl, :] = ngr
            gi_ref[:, sl, :] = ngi
            pv = pl.ds(pl.multiple_of((k - 1) * R, R), R)
            hpr = hr_ref[:, pv, :]
            hpi = hi_ref[:, pv, :]
            return ngr, ngi, sar + ngr * hpr + ngi * hpi, sai - ngr * hpi + ngi * hpr

        gr, gi, sar, sai = lax.fori_loop(0, nsteps - 1, step, (cr[...], ci[...], dar_ref[...], dai_ref[...]), unroll=4)
        sl0 = pl.ds(0, R)
        ngr = dhr[:, sl0, :] + ar * gr + ai * gi
        ngi = dhi[:, sl0, :] - ai * gr + ar * gi
        gr_ref[:, sl0, :] = ngr
        gi_ref[:, sl0, :] = ngi
        keep = jnp.where(i == nchunk - 1, 0.0, 1.0)
        hpr = pr_ref[:, 8 - R:8, :] * keep
        hpi = pi_ref[:, 8 - R:8, :] * keep
        dar_ref[...] = sar + ngr * hpr + ngi * hpi
        dai_ref[...] = sai - ngr * hpi + ngi * hpr
        cr[...] = ngr
        ci[...] = ngi
        parts = []
        for c in range(ncl):
            grc = _slab_get(gr_ref, c, Q, ncl).astype(MXU_DTYPE)
            gic = _slab_get(gi_ref, c, Q, ncl).astype(MXU_DTYPE)
            parts.append(lax.dot_general(grc, wb_ref[c, :, :cs], (((1,), (1,)), ((), ())), preferred_element_type=F32)
                         + lax.dot_general(gic, wb_ref[c, :, cs:], (((1,), (1,)), ((), ())), preferred_element_type=F32))
        du_ref[...] = d_ref[...] * dyv + (parts[0] if ncl == 1 else jnp.concatenate(parts, axis=-1))
        dd_ref[...] += jnp.sum(dyv * x_ref[...], axis=0, keepdims=True)

    tok = pl.BlockSpec((Q, D), lambda i: (rev(i), 0))
    st = pl.BlockSpec((ns, QR, LANES), lambda i: (0, rev(i), 0))
    before = pl.BlockSpec((ns, 8, LANES), lambda i: (0, jnp.maximum(rev(i) * (QR // 8) - 1, 0), 0))
    acc = _whole((ns, R, LANES))
    return _pallas(
        body, name=name, grid=(nchunk,),
        in_specs=[tok, tok, st, st, before, before, _whole(wb.shape), _whole(wc.shape), acc, acc, _whole((1, D))],
        out_specs=[tok, st, st, acc, acc, _whole((1, D))],
        out_shape=[jax.ShapeDtypeStruct((T, D), F32),
                   jax.ShapeDtypeStruct((ns, T * ncl, LANES), F32), jax.ShapeDtypeStruct((ns, T * ncl, LANES), F32),
                   jax.ShapeDtypeStruct((ns, R, LANES), F32), jax.ShapeDtypeStruct((ns, R, LANES), F32),
                   jax.ShapeDtypeStruct((1, D), F32)],
        scratch_shapes=[pltpu.VMEM((ns, QR, LANES), F32)] * 2 + [pltpu.VMEM((ns, R, LANES), F32)] * 2,
        compiler_params=_params(("arbitrary",)),
    )(dy, xi, h_r, h_i, h_r, h_i, wb, wc, a_r, a_i, d_row)


def _cluster_tn(tok, st, ncl, *, tok_left, name):
    T = tok.shape[0]
    ns = st.shape[0]
    cs = ns * LANES
    tt = _pick(T, (512, 256, 128))
    nt = T // tt
    oshape = (ncl, CLUSTER_W, cs) if tok_left else (ncl, cs, CLUSTER_W)

    def body(tok_ref, st_ref, o_ref, acc):
        t = pl.program_id(0)

        @pl.when(t == 0)
        def _():
            acc[...] = jnp.zeros_like(acc)

        tk = tok_ref[...].astype(MXU_DTYPE)
        for c in range(ncl):
            tc = tk[:, c * CLUSTER_W:(c + 1) * CLUSTER_W]
            sc = _slab_get(st_ref, c, tt, ncl).astype(MXU_DTYPE)
            lhs, rhs = (tc, sc) if tok_left else (sc, tc)
            acc[c] += lax.dot_general(lhs, rhs, (((0,), (0,)), ((), ())), preferred_element_type=F32)

        @pl.when(t == nt - 1)
        def _():
            o_ref[...] = acc[...]

    return _pallas(
        body, name=name, grid=(nt,),
        in_specs=[_rows(tt, tok.shape[1]), _slabs(ns, tt * ncl)],
        out_specs=_whole(oshape),
        out_shape=jax.ShapeDtypeStruct(oshape, F32),
        scratch_shapes=[pltpu.VMEM(oshape, F32)],
        compiler_params=_params(("arbitrary",)),
    )(tok, st)


def _s5_discretize(lam_re, lam_im, log_dt, b_re, b_im):
    dt = jnp.exp(log_dt)[:, None]
    mag = jnp.exp(lam_re * dt)
    ab_r, ab_i = mag * jnp.cos(lam_im * dt), mag * jnp.sin(lam_im * dt)
    den = lam_re * lam_re + lam_im * lam_im
    nr = ab_r - 1.0
    co_r = (nr * lam_re + ab_i * lam_im) / den
    co_i = (ab_i * lam_re - nr * lam_im) / den
    bb_r = co_r[..., None] * b_re - co_i[..., None] * b_im
    bb_i = co_r[..., None] * b_im + co_i[..., None] * b_re
    return ab_r, ab_i, bb_r, bb_i


def _blockdiag(m):
    G, r, k = m.shape
    ncl = G // GROUPS_PER_CLUSTER
    m4 = m.reshape(ncl, GROUPS_PER_CLUSTER, r, k)
    eye = jnp.eye(GROUPS_PER_CLUSTER, dtype=m.dtype)
    return jnp.einsum('cgrk,gh->cgrhk', m4, eye).reshape(ncl, GROUPS_PER_CLUSTER * r, GROUPS_PER_CLUSTER * k)


def _unblockdiag(m, r, k):
    ncl = m.shape[0]
    m5 = m.reshape(ncl, GROUPS_PER_CLUSTER, r, GROUPS_PER_CLUSTER, k)
    eye = jnp.eye(GROUPS_PER_CLUSTER, dtype=m.dtype)
    return jnp.einsum('cgrhk,gh->cgrk', m5, eye).reshape(ncl * GROUPS_PER_CLUSTER, r, k)


def _t5_bucket(dist):
    exact = REL_BUCKETS // 2
    d = np.maximum(dist, 1).astype(np.float32)
    large = exact + (np.log(d / exact) / math.log(REL_MAX_DIST / exact) * (REL_BUCKETS - exact)).astype(np.int64)
    large = np.minimum(large, REL_BUCKETS - 1)
    return np.where(dist < exact, dist, large).astype(np.int32)


def _band_tables(dil):
    steps = np.arange(BAND)[:, None] + BAND - np.arange(2 * BAND)[None, :]
    bucket = _t5_bucket(np.maximum(steps, 0) * dil)
    in_band = (steps >= 0) & (steps <= BAND)
    return bucket, in_band


def _attn_bias(rel_bias, hpg):
    out = []
    for g, dil in enumerate(DILATIONS):
        bucket, in_band = _band_tables(dil)
        cols = rel_bias[:, g * hpg:(g + 1) * hpg].astype(F32)
        onehot = jnp.asarray((bucket.reshape(-1, 1) == np.arange(REL_BUCKETS)[None, :]).astype(np.float32))
        bias = jnp.dot(onehot, cols, precision=lax.Precision.HIGHEST).T.reshape(hpg, BAND, 2 * BAND)
        out.append(jnp.where(jnp.asarray(in_band)[None], bias, NEG_BIG))
    return jnp.concatenate(out, axis=0)


def _attn_blocks(dil, L):
    M = L // dil
    return M, M // BAND


def _row_sel(r, M, dil):
    return pl.ds(r, M) if dil == 1 else pl.ds(r, M, stride=dil)


def _attn_fwd(q, kv, bias, L, hpg, *, name):
    T = q.shape[0]
    nb_ = T // L
    HP = hpg // 2
    W3 = 3 * hpg * HEAD_DIM
    mmax = L

    def group_body(dil, q_ref, k_ref, v_ref, b_ref, o_ref, l_ref, os, ls):
        M, NB = _attn_blocks(dil, L)
        for r in range(dil):
            rows = _row_sel(r, M, dil)
            qr = (q_ref[rows, :] * 0.125).astype(MXU_DTYPE)
            kr = k_ref[rows, :].astype(MXU_DTYPE)
            vr = v_ref[rows, :].astype(MXU_DTYPE)
            for n in range(NB):
                qs = slice(n * BAND, (n + 1) * BAND)
                ks = slice(0, BAND) if n == 0 else slice((n - 1) * BAND, (n + 1) * BAND)
                for hh in range(2):
                    ln = slice(hh * HEAD_DIM, (hh + 1) * HEAD_DIM)
                    bb = b_ref[hh, :, BAND:] if n == 0 else b_ref[hh]
                    s = lax.dot_general(qr[qs, ln], kr[ks, ln], (((1,), (1,)), ((), ())),
                                        preferred_element_type=F32) + bb
                    m = jnp.max(s, axis=-1, keepdims=True)
                    p = jnp.exp(s - m)
                    l = jnp.sum(p, axis=-1, keepdims=True)
                    pn = p / l
                    os[qs, ln] = jnp.dot(pn.astype(MXU_DTYPE), vr[ks, ln], preferred_element_type=F32)
                    ls[qs, ln] = jnp.broadcast_to(m + jnp.log(l), (BAND, HEAD_DIM))
            o_ref[rows, :] = os[0:M, :]
            l_ref[rows, :] = ls[0:M, :]

    def body(q_ref, k_ref, v_ref, b_ref, o_ref, l_ref, os, ls):
        g = pl.program_id(0)
        for gi, dil in enumerate(DILATIONS):
            pl.when(g == gi)(functools.partial(group_body, dil, q_ref, k_ref, v_ref, b_ref, o_ref, l_ref, os, ls))

    blk = (L, 2 * HEAD_DIM)
    return _pallas(
        body, name=name, grid=(3, nb_, HP),
        in_specs=[pl.BlockSpec(blk, lambda g, b, h: (b, g * HP + h)),
                  pl.BlockSpec(blk, lambda g, b, h: (b, g * HP + h)),
                  pl.BlockSpec(blk, lambda g, b, h: (b, 3 * HP + g * HP + h)),
                  pl.BlockSpec((2, BAND, 2 * BAND), lambda g, b, h: (g * HP + h, 0, 0))],
        out_specs=[pl.BlockSpec(blk, lambda g, b, h: (b, g * HP + h)),
                   pl.BlockSpec(blk, lambda g, b, h: (b, g * HP + h))],
        out_shape=[jax.ShapeDtypeStruct((T, W3), F32), jax.ShapeDtypeStruct((T, W3), F32)],
        scratch_shapes=[pltpu.VMEM((mmax, 2 * HEAD_DIM), F32), pltpu.VMEM((mmax, 2 * HEAD_DIM), F32)],
        compiler_params=_params(("arbitrary", "arbitrary", "arbitrary")),
    )(q, kv, kv, bias)


def _attn_merge(o3, l3, hw, *, name):
    T = o3.shape[0]
    tm = _pick(T, (256, 128))

    def body(o0, o1, o2, l0, l1, l2, o_ref, ob_ref, lse_ref):
        a0, a1, a2 = l0[...], l1[...], l2[...]
        m = jnp.maximum(jnp.maximum(a0, a1), a2)
        e0, e1, e2 = jnp.exp(a0 - m), jnp.exp(a1 - m), jnp.exp(a2 - m)
        z = e0 + e1 + e2
        o = (e0 * o0[...] + e1 * o1[...] + e2 * o2[...]) / z
        o_ref[...] = o
        ob_ref[...] = o.astype(ob_ref.dtype)
        lse_ref[...] = m + jnp.log(z)

    def col(g):
        return pl.BlockSpec((tm, hw), lambda i: (i, g))

    return _pallas(
        body, name=name, grid=(T // tm,),
        in_specs=[col(0), col(1), col(2), col(0), col(1), col(2)],
        out_specs=[_rows(tm, hw)] * 3,
        out_shape=[jax.ShapeDtypeStruct((T, hw), F32), jax.ShapeDtypeStruct((T, hw), MXU_DTYPE),
                   jax.ShapeDtypeStruct((T, hw), F32)],
        compiler_params=_params(("parallel",)),
    )(o3, o3, o3, l3, l3, l3)


def _attn_bwd(q, kv, do, o, lse, bias, L, hpg, *, name):
    T = q.shape[0]
    nb_ = T // L
    HP = hpg // 2
    W3 = 3 * hpg * HEAD_DIM
    mmax = L

    def group_body(dil, q_ref, k_ref, v_ref, do_ref, o_ref, l_ref, b_ref, dq_ref, dk_ref, dv_ref, ds_ref,
                   dqs, dks, dvs):
        M, NB = _attn_blocks(dil, L)
        for r in range(dil):
            rows = _row_sel(r, M, dil)
            qr = (q_ref[rows, :] * 0.125).astype(MXU_DTYPE)
            kr = k_ref[rows, :].astype(MXU_DTYPE)
            vr = v_ref[rows, :].astype(MXU_DTYPE)
            dor = do_ref[rows, :]
            orr = o_ref[rows, :]
            lr = l_ref[rows, :]
            dks[0:M, :] = jnp.zeros((M, 2 * HEAD_DIM), F32)
            dvs[0:M, :] = jnp.zeros((M, 2 * HEAD_DIM), F32)
            for n in range(NB):
                qs = slice(n * BAND, (n + 1) * BAND)
                ks = slice(0, BAND) if n == 0 else slice((n - 1) * BAND, (n + 1) * BAND)
                for hh in range(2):
                    ln = slice(hh * HEAD_DIM, (hh + 1) * HEAD_DIM)
                    bb = b_ref[hh, :, BAND:] if n == 0 else b_ref[hh]
                    qb, kb, vb = qr[qs, ln], kr[ks, ln], vr[ks, ln]
                    dob = dor[qs, ln]
                    s = lax.dot_general(qb, kb, (((1,), (1,)), ((), ())), preferred_element_type=F32) + bb
                    p = jnp.exp(s - lr[qs, hh * HEAD_DIM:hh * HEAD_DIM + 1])
                    dobm = dob.astype(MXU_DTYPE)
                    dp = lax.dot_general(dobm, vb, (((1,), (1,)), ((), ())), preferred_element_type=F32)
                    delta = jnp.sum(dob * orr[qs, ln], axis=-1, keepdims=True)
                    ds = p * (dp - delta)
                    if n == 0:
                        ds_ref[hh, :, BAND:] += ds
                    else:
                        ds_ref[hh] += ds
                    dsm = ds.astype(MXU_DTYPE)
                    dqs[qs, ln] = jnp.dot(dsm, kb, preferred_element_type=F32) * 0.125
                    dks[ks, ln] += lax.dot_general(dsm, qb, (((0,), (0,)), ((), ())), preferred_element_type=F32)
                    dvs[ks, ln] += lax.dot_general(p.astype(MXU_DTYPE), dobm, (((0,), (0,)), ((), ())),
                                                   preferred_element_type=F32)
            dq_ref[rows, :] = dqs[0:M, :]
            dk_ref[rows, :] = dks[0:M, :]
            dv_ref[rows, :] = dvs[0:M, :]

    def body(q_ref, k_ref, v_ref, do_ref, o_ref, l_ref, b_ref, dq_ref, dk_ref, dv_ref, ds_ref, dqs, dks, dvs):
        g = pl.program_id(0)

        @pl.when(pl.program_id(2) == 0)
        def _():
            ds_ref[...] = jnp.zeros_like(ds_ref)

        for gi, dil in enumerate(DILATIONS):
            pl.when(g == gi)(functools.partial(group_body, dil, q_ref, k_ref, v_ref, do_ref, o_ref, l_ref, b_ref,
                                               dq_ref, dk_ref, dv_ref, ds_ref, dqs, dks, dvs))

    blk = (L, 2 * HEAD_DIM)
    gcol = lambda g, h, b: (b, g * HP + h)
    hcol = lambda g, h, b: (b, h)
    return _pallas(
        body, name=name, grid=(3, HP, nb_),
        in_specs=[pl.BlockSpec(blk, gcol), pl.BlockSpec(blk, gcol),
                  pl.BlockSpec(blk, lambda g, h, b: (b, 3 * HP + g * HP + h)),
                  pl.BlockSpec(blk, hcol), pl.BlockSpec(blk, hcol), pl.BlockSpec(blk, hcol),
                  pl.BlockSpec((2, BAND, 2 * BAND), lambda g, h, b: (g * HP + h, 0, 0))],
        out_specs=[pl.BlockSpec(blk, gcol), pl.BlockSpec(blk, gcol), pl.BlockSpec(blk, gcol),
                   pl.BlockSpec((2, BAND, 2 * BAND), lambda g, h, b: (g * HP + h, 0, 0))],
        out_shape=[jax.ShapeDtypeStruct((T, W3), F32), jax.ShapeDtypeStruct((T, W3), F32),
                   jax.ShapeDtypeStruct((T, W3), F32), jax.ShapeDtypeStruct((3 * hpg, BAND, 2 * BAND), F32)],
        scratch_shapes=[pltpu.VMEM((mmax, 2 * HEAD_DIM), F32)] * 3,
        compiler_params=_params(("arbitrary", "arbitrary", "arbitrary")),
    )(q, kv, kv, do, o, lse, bias)


def _bias_grad(ds_sum, hpg, *, name):
    nh = ds_sum.shape[0]
    idx = np.stack([np.where(_band_tables(dil)[1], _band_tables(dil)[0], -1) for dil in DILATIONS]).astype(np.int32)

    def body(ds_ref, idx_ref, o_ref):
        d = ds_ref[...]
        ix = idx_ref[...]
        lane = lax.broadcasted_iota(jnp.int32, (8, 128), 1)
        row = jnp.zeros((8, 128), F32)
        for b in range(REL_BUCKETS):
            row = row + jnp.where(lane == b, jnp.sum(jnp.where(ix == b, d, 0.0)), 0.0)
        o_ref[...] = row

    out = _pallas(
        body, name=name, grid=(nh,),
        in_specs=[pl.BlockSpec((None, BAND, 2 * BAND), lambda h: (h, 0, 0)),
                  pl.BlockSpec((None, BAND, 2 * BAND), lambda h: (h // hpg, 0, 0))],
        out_specs=pl.BlockSpec((None, 8, 128), lambda h: (h, 0, 0)),
        out_shape=jax.ShapeDtypeStruct((nh, 8, 128), F32),
        compiler_params=_params(("parallel",)),
    )(ds_sum, jnp.asarray(idx))
    return out[:, 0, :REL_BUCKETS].T


def _adamw(w, g, m, v, *, name):
    Rw, C = w.shape
    tm = _pick(Rw, (512, 352, 256, 128, 64, 32, 16, 8))

    def body(w_ref, g_ref, m_ref, v_ref, d_ref, nm_ref, nv_ref):
        gg = g_ref[...]
        nm = ADAM_B1 * m_ref[...] + (1.0 - ADAM_B1) * gg
        nv = ADAM_B2 * v_ref[...] + (1.0 - ADAM_B2) * (gg * gg)
        m_hat = nm / (1.0 - ADAM_B1 ** ADAM_STEP)
        v_hat = nv / (1.0 - ADAM_B2 ** ADAM_STEP)
        d_ref[...] = -ADAM_LR * (m_hat / (jnp.sqrt(v_hat) + ADAM_EPS) + ADAM_WD * w_ref[...])
        nm_ref[...] = nm
        nv_ref[...] = nv

    return _pallas(
        body, name=name, grid=(Rw // tm,), in_specs=[_rows(tm, C)] * 4, out_specs=[_rows(tm, C)] * 3,
        out_shape=[jax.ShapeDtypeStruct((Rw, C), F32)] * 3, compiler_params=_params(("parallel",)),
    )(w, g, m, v)


ROW_TILE_ELEMS = 256 * 1024


def _tile_rows(r, c):
    best = 8
    for t in range(8, r + 1, 8):
        if r % t == 0 and t * c <= ROW_TILE_ELEMS:
            best = t
    return best


def _adamw_halves(w, m, v, mine, other, cidx, *, name):
    _, r, c = w.shape
    tm = _tile_rows(r, c)

    def body(c_ref, w_ref, m_ref, v_ref, a_ref, b_ref, g_ref, d_ref, nm_ref, nv_ref):
        gg = jnp.where(pl.program_id(0) == c_ref[0], a_ref[...], b_ref[...])
        nm = ADAM_B1 * m_ref[...] + (1.0 - ADAM_B1) * gg
        nv = ADAM_B2 * v_ref[...] + (1.0 - ADAM_B2) * (gg * gg)
        m_hat = nm / (1.0 - ADAM_B1 ** ADAM_STEP)
        v_hat = nv / (1.0 - ADAM_B2 ** ADAM_STEP)
        g_ref[...] = gg
        d_ref[...] = -ADAM_LR * (m_hat / (jnp.sqrt(v_hat) + ADAM_EPS) + ADAM_WD * w_ref[...])
        nm_ref[...] = nm
        nv_ref[...] = nv

    half = pl.BlockSpec((None, tm, c), lambda h, i, cr: (h, i, 0))
    one = pl.BlockSpec((None, tm, c), lambda h, i, cr: (0, i, 0))
    spec = pltpu.PrefetchScalarGridSpec(num_scalar_prefetch=1, grid=(2, r // tm),
                                        in_specs=[half, half, half, one, one], out_specs=[half] * 4)
    return _pallas(
        body, name=name, grid_spec=spec, out_shape=[jax.ShapeDtypeStruct((2, r, c), F32)] * 4,
        compiler_params=_params(("parallel", "parallel")),
    )(cidx, w, m, v, mine, other)


def _pair_sum(g, theirs, cidx, *, cast, name):
    _, _, r, c = g.shape
    tm = _tile_rows(r, c)

    def body(c_ref, g_ref, t_ref, *outs):
        s = g_ref[...] + t_ref[...]
        outs[0][...] = s
        if cast:
            outs[1][...] = s.astype(BF16)

    blk = (None, None, tm, c)
    first = pl.BlockSpec(blk, lambda p, i, cr: (p, 0, i, 0))
    shapes = [jax.ShapeDtypeStruct((4, 1, r, c), F32)] + ([jax.ShapeDtypeStruct((4, 1, r, c), BF16)] if cast else [])
    spec = pltpu.PrefetchScalarGridSpec(
        num_scalar_prefetch=1, grid=(4, r // tm),
        in_specs=[pl.BlockSpec(blk, lambda p, i, cr: (p, cr[0], i, 0)), first], out_specs=[first] * len(shapes))
    return _pallas(body, name=name, grid_spec=spec, out_shape=shapes,
                   compiler_params=_params(("parallel", "parallel")))(cidx, g, theirs)


def _chip_sum(hf, got, chip_idx, *, name):
    _, _, r, c = hf.shape
    tm = _tile_rows(r, c)

    def body(p_ref, h_ref, r_ref, o_ref):
        s = h_ref[...]
        for k in range(3):
            s = s + r_ref[k].astype(F32)
        o_ref[...] = s

    spec = pltpu.PrefetchScalarGridSpec(
        num_scalar_prefetch=1, grid=(r // tm,),
        in_specs=[pl.BlockSpec((None, None, tm, c), lambda i, pr: (pr[0], 0, i, 0)),
                  pl.BlockSpec((3, None, tm, c), lambda i, pr: (0, 0, i, 0))],
        out_specs=pl.BlockSpec((None, tm, c), lambda i, pr: (0, i, 0)))
    return _pallas(body, name=name, grid_spec=spec, out_shape=jax.ShapeDtypeStruct((1, r, c), F32),
                   compiler_params=_params(("parallel",)))(chip_idx, hf, got)


def _place():
    x, y, c = lax.axis_index("x"), lax.axis_index("y"), lax.axis_index("c")
    chips = [(1 - x, y), (x, 1 - y), (1 - x, 1 - y)]
    return x, y, c, chips


_ANY = pl.BlockSpec(memory_space=pl.ANY)


def _comm_call(body, ins, out_shapes, n_remote, *, name):
    sems = [pltpu.SemaphoreType.DMA((n,)) for n in n_remote]
    return _pallas(
        body, name=name, in_specs=[_ANY] * len(ins), out_specs=[_ANY] * len(out_shapes), out_shape=out_shapes,
        scratch_shapes=sems, compiler_params=pltpu.CompilerParams(has_side_effects=True),
    )(*ins)


def _rcopy(src, dst, ssem, rsem, dev):
    return pltpu.make_async_remote_copy(src_ref=src, dst_ref=dst, send_sem=ssem, recv_sem=rsem,
                                        device_id=dev, device_id_type=MESH)


def _all_gather(shards, *, name):
    n = len(shards)

    def body(*refs):
        ins, outs = refs[:n], refs[n:2 * n]
        s_ici, r_ici, s_d2d, r_d2d = refs[2 * n:]
        x, y, c, chips = _place()
        me = 2 * x + y
        sib = (x, y, 1 - c)
        sends = []
        for a in range(n):
            for k, (tx, ty) in enumerate(chips):
                cp = _rcopy(ins[a].at[c], outs[a].at[me, c], s_ici.at[3 * a + k], r_ici.at[3 * a + k], (tx, ty, c))
                cp.start()
                sends.append(cp)
        for a in range(n):
            for k, (tx, ty) in enumerate(chips):
                pk = 2 * tx + ty
                _rcopy(ins[a].at[c], outs[a].at[pk, c], s_ici.at[3 * a + k], r_ici.at[3 * a + k], (tx, ty, c)).wait_recv()
                fw = _rcopy(outs[a].at[pk, c], outs[a].at[pk, c], s_d2d.at[3 * a + k], r_d2d.at[3 * a + k], sib)
                fw.start()
                sends.append(fw)
        for a in range(n):
            for k, (tx, ty) in enumerate(chips):
                pk = 2 * tx + ty
                _rcopy(ins[a].at[c], outs[a].at[pk, 1 - c], s_d2d.at[3 * a + k], r_d2d.at[3 * a + k], sib).wait_recv()
        for cp in sends:
            cp.wait_send()

    shapes = [jax.ShapeDtypeStruct((4,) + s.shape, s.dtype) for s in shards]
    return _comm_call(body, shards, shapes, [3 * n] * 4, name=name)


def _gather(shards, chip, *, name):
    outs = _all_gather(shards, name=name)
    return [lax.dynamic_update_slice(o, s[None], (chip, 0, 0, 0)) for o, s in zip(outs, shards)]


def _pair_send(gs, *, name):
    n = len(gs)

    def body(*refs):
        ins, theirs = refs[:n], refs[n:2 * n]
        ssem, rsem = refs[2 * n:]
        x, y, c, _ = _place()
        sib = (x, y, 1 - c)
        cps = []
        for a in range(n):
            cp = _rcopy(ins[a].at[:, pl.ds(1 - c, 1)], theirs[a], ssem.at[a], rsem.at[a], sib)
            cp.start()
            cps.append(cp)
        for cp in cps:
            cp.wait_send()
            cp.wait_recv()

    shapes = [jax.ShapeDtypeStruct((4, 1) + g.shape[2:], g.dtype) for g in gs]
    return _comm_call(body, gs, shapes, [n, n], name=name)


def _chip_exchange(hx, *, name):
    n = len(hx)

    def body(*refs):
        hxr, got = refs[:n], refs[n:2 * n]
        ssem, rsem = refs[2 * n:]
        x, y, c, chips = _place()
        cps = []
        for a in range(n):
            for k, (tx, ty) in enumerate(chips):
                cp = _rcopy(hxr[a].at[2 * tx + ty], got[a].at[k], ssem.at[3 * a + k], rsem.at[3 * a + k], (tx, ty, c))
                cp.start()
                cps.append(cp)
        for cp in cps:
            cp.wait_send()
            cp.wait_recv()

    shapes = [jax.ShapeDtypeStruct((3,) + h.shape[1:], h.dtype) for h in hx]
    return _comm_call(body, hx, shapes, [3 * n, 3 * n], name=name)


def _pair_swap(fs, *, name):
    n = len(fs)

    def body(*refs):
        ins, outs = refs[:n], refs[n:2 * n]
        ssem, rsem = refs[2 * n:]
        x, y, c, _ = _place()
        cps = []
        for a in range(n):
            cp = _rcopy(ins[a], outs[a], ssem.at[a], rsem.at[a], (x, y, 1 - c))
            cp.start()
            cps.append(cp)
        for cp in cps:
            cp.wait_send()
            cp.wait_recv()

    shapes = [jax.ShapeDtypeStruct(f.shape, f.dtype) for f in fs]
    return _comm_call(body, fs, shapes, [n, n], name=name)


def _reduce_scatter(grads, exch_bf16, cidx, chip_idx, tag):
    n = len(grads)
    theirs = _pair_send(grads, name=f"rs_pair_send_{tag}")
    hf, hx = [], []
    for a in range(n):
        res = _pair_sum(grads[a], theirs[a], cidx, cast=exch_bf16[a], name=f"rs_pair_sum_{tag}{a}")
        hf.append(res[0])
        hx.append(res[1] if exch_bf16[a] else res[0])
    got = _chip_exchange(hx, name=f"rs_chip_exchange_{tag}")
    mine = [_chip_sum(hf[a], got[a], chip_idx, name=f"rs_chip_sum_{tag}{a}") for a in range(n)]
    return mine, _pair_swap(mine, name=f"rs_pair_swap_{tag}")


def _interleave(a, B, L):
    return a.reshape(B, L, -1).transpose(1, 0, 2).reshape(B * L, -1)


def _deinterleave(a, B, L):
    return a.reshape(L, B, -1).transpose(1, 0, 2).reshape(B * L, -1)


def _local_step(x, tgt, W, S):
    B, L, D = x.shape
    T = B * L
    G = D // SSM_GROUP
    Pst = SSM_STATE
    hpg = D // HEAD_DIM
    HW = hpg * HEAD_DIM
    ncl = G // GROUPS_PER_CLUSTER
    x2 = x.reshape(T, D)
    tgt2 = tgt.reshape(T, D)

    disc = lambda *p: _s5_discretize(*p)
    (ab_r, ab_i, bb_r, bb_i), disc_vjp = jax.vjp(disc, S["lam_re"], S["lam_im"], S["log_dt"], S["b_re"], S["b_im"])
    wb = jnp.concatenate([_blockdiag(jnp.transpose(bb_r, (0, 2, 1))), _blockdiag(jnp.transpose(bb_i, (0, 2, 1)))],
                         axis=-1).astype(MXU_DTYPE)
    wc = jnp.concatenate([_blockdiag(jnp.transpose(S["c_re"], (0, 2, 1))), _blockdiag(-jnp.transpose(S["c_im"], (0, 2, 1)))],
                         axis=1).astype(MXU_DTYPE)
    cs = GROUPS_PER_CLUSTER * Pst
    slab = lambda ab: jnp.tile(jnp.transpose(ab.reshape(ncl, cs // LANES, LANES), (1, 0, 2)), (1, B, 1))
    a_r, a_i = slab(ab_r), slab(ab_i)
    d_row = S["d"].reshape(1, D)

    xi = _interleave(x2, B, L)
    y, yg, h_r, h_i = _s5_fwd(xi, wb, wc, a_r, a_i, d_row, B, name="s5_fwd")
    z = _mm_nn(yg, W["w_glu"], bias=S["b_glu"].reshape(1, D), name="glu_z")
    gate = _glu_gate(y, z, name="glu_gate")
    mix_i = _mm_nn(gate, W["w_out"], name="s5_out")
    mix = _deinterleave(mix_i, B, L)
    h1, h1b, xh1, rs1 = _ln_fwd(x2, mix, S["ln_gain"][0, 0][None], S["ln_bias"][0, 0][None], name="ln_fwd_0a")

    def ffn_fwd(hb, l):
        hc = _mm_nn(hb, W[f"w_up{l}"], name=f"ffn_up_{l}")
        a = _conv_glu_fwd(hc, S["conv_w"][l], S["conv_b"][l][None], L, name=f"ffn_conv_{l}")
        f = _mm_nn(a, W[f"w_down{l}"], name=f"ffn_down_{l}")
        return hc, a, f

    hc0, a0, f0 = ffn_fwd(h1b, 0)
    h2, h2b, xh2, rs2 = _ln_fwd(h1, f0, S["ln_gain"][0, 1][None], S["ln_bias"][0, 1][None], name="ln_fwd_0b")

    kv = _mm_nn(h2b, W["w_kv"], name="attn_kv")
    q = _mm_nn(h2b, W["w_q"], name="attn_q")
    bias = _attn_bias(S["rel_bias"], hpg)
    o3, l3 = _attn_fwd(q, kv, bias, L, hpg, name="attn_fwd")
    o, ob, lse = _attn_merge(o3, l3, HW, name="attn_merge")
    att = _mm_nn(ob, W["w_ao"], name="attn_out")
    h3, h3b, xh3, rs3 = _ln_fwd(h2, att, S["ln_gain"][1, 0][None], S["ln_bias"][1, 0][None], name="ln_fwd_1a")
    hc1, a1, f1 = ffn_fwd(h3b, 1)
    h4, _, xh4, rs4 = _ln_fwd(h3, f1, S["ln_gain"][1, 1][None], S["ln_bias"][1, 1][None], name="ln_fwd_1b")

    dh4, lrow = _loss_grad(h4, tgt2, name="loss")
    loss = lrow[0, 0]

    GW, GS = {}, {}

    def ffn_bwd(dzb, hb, hc, a, l):
        da = _mm_nt(dzb, W[f"w_down{l}"], name=f"ffn_down_bwd_x_{l}")
        GW[f"w_down{l}"] = _tn(a, dzb, ptotal=1, np_cols=D, name=f"ffn_down_bwd_w_{l}")
        dc, dcw, dcb = _conv_glu_bwd(hc, da, S["conv_w"][l], S["conv_b"][l][None], L, name=f"ffn_conv_bwd_{l}")
        dhc = _conv_bwd_input(dc, S["conv_w"][l], L, name=f"ffn_conv_bwd_x_{l}")
        dh = _mm_nt(dhc, W[f"w_up{l}"], name=f"ffn_up_bwd_x_{l}")
        GW[f"w_up{l}"] = _tn(hb, dhc, ptotal=W[f"w_up{l}"].shape[0], np_cols=W[f"w_up{l}"].shape[2], name=f"ffn_up_bwd_w_{l}")
        return dh, dcw, dcb

    dz4, dz4b, dg4, db4 = _ln_bwd([dh4], [1.0], xh4, rs4, S["ln_gain"][1, 1][None], name="ln_bwd_1b")
    dh3f, dcw1, dcb1 = ffn_bwd(dz4b, h3b, hc1, a1, 1)
    dz3, dz3b, dg3, db3 = _ln_bwd([dz4, dh3f], [DN_ALPHA, 1.0], xh3, rs3, S["ln_gain"][1, 0][None], name="ln_bwd_1a")
    do = _mm_nt(dz3b, W["w_ao"], name="attn_out_bwd_x")
    GW["w_ao"] = _tn(ob, dz3b, ptotal=1, np_cols=D, name="attn_out_bwd_w")
    dq, dk, dv, ds_sum = _attn_bwd(q, kv, do, o, lse, bias, L, hpg, name="attn_bwd")
    GS["rel_bias"] = _bias_grad(ds_sum, hpg, name="attn_bias_grad")
    GW["w_q"] = _tn(h2b, dq, ptotal=W["w_q"].shape[0], np_cols=W["w_q"].shape[2], name="attn_q_bwd_w")
    pkv, npkv = W["w_kv"].shape[0], W["w_kv"].shape[2]
    gkv = _tn(h2b, dk, ptotal=pkv, np_cols=npkv, p0=0, name="attn_k_bwd_w")
    GW["w_kv"] = _tn(h2b, dv, ptotal=pkv, np_cols=npkv, p0=pkv // 2, prev=gkv, name="attn_v_bwd_w")
    dh2q = _mm_nt(dq, W["w_q"], name="attn_q_bwd_x")
    dh2k = _mm_nt(dk, W["w_kv"], p0=0, pn=pkv // 2, name="attn_k_bwd_x")
    dh2v = _mm_nt(dv, W["w_kv"], p0=pkv // 2, pn=pkv // 2, name="attn_v_bwd_x")

    dz2, dz2b, dg2, db2 = _ln_bwd([dz3, dh2q, dh2k, dh2v], [DN_ALPHA, 1.0, 1.0, 1.0], xh2, rs2,
                                  S["ln_gain"][0, 1][None], name="ln_bwd_0b")
    dh1f, dcw0, dcb0 = ffn_bwd(dz2b, h1b, hc0, a0, 0)
    dz1, dz1b, dg1, db1 = _ln_bwd([dz2, dh1f], [DN_ALPHA, 1.0], xh1, rs1, S["ln_gain"][0, 0][None], name="ln_bwd_0a")
    dmix_i = _interleave(dz1b, B, L)
    dgate = _mm_nt(dmix_i, W["w_out"], name="s5_out_bwd_x")
    GW["w_out"] = _tn(gate, dmix_i, ptotal=1, np_cols=D, name="s5_out_bwd_w")
    dzg, dyg1, dbglu = _glu_bwd(y, z, dgate, name="glu_bwd")
    dyg2 = _mm_nt(dzg, W["w_glu"], name="glu_z_bwd_x")
    GW["w_glu"] = _tn(yg, dzg, ptotal=1, np_cols=D, name="glu_z_bwd_w")
    dy = _gelu_bwd(y, dyg1, dyg2, name="gelu_bwd")
    du_i, g_r, g_i, dar, dai, dd = _s5_bwd(dy, xi, h_r, h_i, wb, wc, a_r, a_i, d_row, B, name="s5_bwd")
    dwb_r = _cluster_tn(xi, g_r, ncl, tok_left=True, name="s5_b_grad_re")
    dwb_i = _cluster_tn(xi, g_i, ncl, tok_left=True, name="s5_b_grad_im")
    dwc_r = _cluster_tn(dy, h_r, ncl, tok_left=False, name="s5_c_grad_re")
    dwc_i = _cluster_tn(dy, h_i, ncl, tok_left=False, name="s5_c_grad_im")
    grad_x = _axpy(dz1, _deinterleave(du_i, B, L), DN_ALPHA, name="grad_x")

    dbb_r = jnp.transpose(_unblockdiag(dwb_r, SSM_GROUP, Pst), (0, 2, 1))
    dbb_i = jnp.transpose(_unblockdiag(dwb_i, SSM_GROUP, Pst), (0, 2, 1))
    unslab = lambda da: jnp.transpose(da.reshape(cs // LANES, B, ncl, LANES).sum(1), (1, 0, 2)).reshape(G, Pst)
    dab_r, dab_i = unslab(dar), unslab(dai)
    GS["lam_re"], GS["lam_im"], GS["log_dt"], GS["b_re"], GS["b_im"] = disc_vjp((dab_r, dab_i, dbb_r, dbb_i))
    GS["c_re"] = jnp.transpose(_unblockdiag(dwc_r, Pst, SSM_GROUP), (0, 2, 1))
    GS["c_im"] = -jnp.transpose(_unblockdiag(dwc_i, Pst, SSM_GROUP), (0, 2, 1))
    GS["d"] = dd.reshape(G, SSM_GROUP)
    GS["b_glu"] = dbglu.reshape(D)
    GS["conv_w"] = jnp.stack([dcw0, dcw1])
    GS["conv_b"] = jnp.stack([dcb0[0], dcb1[0]])
    GS["ln_gain"] = jnp.stack([jnp.stack([dg1[0], dg2[0]]), jnp.stack([dg3[0], dg4[0]])])
    GS["ln_bias"] = jnp.stack([jnp.stack([db1[0], db2[0]]), jnp.stack([db3[0], db4[0]])])
    return loss, grad_x.reshape(B, L, D), GW, GS


SMALL_REPLICATED = ("lam_re", "lam_im", "log_dt", "b_re", "b_im", "c_re", "c_im", "d", "rel_bias", "conv_b")
SMALL_SHARDED = ("b_glu", "conv_w", "ln_gain", "ln_bias")
SMALL_ORDER = SMALL_REPLICATED + SMALL_SHARDED


def _pack(arrs, lanes, row_mult):
    flat = jnp.concatenate([a.reshape(-1).astype(F32) for a in arrs])
    rows = -(-flat.shape[0] // lanes)
    rows = -(-rows // row_mult) * row_mult
    return jnp.pad(flat, (0, rows * lanes - flat.shape[0])).reshape(rows, lanes)


def _unpack(packed, shapes):
    flat = packed.reshape(-1)
    out, off = [], 0
    for s in shapes:
        n = int(np.prod(s))
        out.append(flat[off:off + n].reshape(s))
        off += n
    return out


def kernel(x, s5_lam_re, s5_lam_im, s5_log_dt, s5_b_re, s5_b_im, s5_c_re, s5_c_im, s5_d, s5_w_glu, s5_b_glu, s5_w_out, attn_w_kv, attn_w_q, attn_w_out, rel_bias, ffn_w_up, ffn_conv_w, ffn_conv_b, ffn_w_down, ln_gain, ln_bias, loss_target, m_s5_lam_re, m_s5_lam_im, m_s5_log_dt, m_s5_b_re, m_s5_b_im, m_s5_c_re, m_s5_c_im, m_s5_d, m_s5_w_glu, m_s5_b_glu, m_s5_w_out, m_attn_w_kv, m_attn_w_q, m_attn_w_out, m_rel_bias, m_ffn_w_up, m_ffn_conv_w, m_ffn_conv_b, m_ffn_w_down, m_ln_gain, m_ln_bias, v_s5_lam_re, v_s5_lam_im, v_s5_log_dt, v_s5_b_re, v_s5_b_im, v_s5_c_re, v_s5_c_im, v_s5_d, v_s5_w_glu, v_s5_b_glu, v_s5_w_out, v_attn_w_kv, v_attn_w_q, v_attn_w_out, v_rel_bias, v_ffn_w_up, v_ffn_conv_w, v_ffn_conv_b, v_ffn_w_down, v_ln_gain, v_ln_bias):
    names = ["s5_lam_re", "s5_lam_im", "s5_log_dt", "s5_b_re", "s5_b_im", "s5_c_re", "s5_c_im", "s5_d", "s5_w_glu",
             "s5_b_glu", "s5_w_out", "attn_w_kv", "attn_w_q", "attn_w_out", "rel_bias", "ffn_w_up", "ffn_conv_w",
             "ffn_conv_b", "ffn_w_down", "ln_gain", "ln_bias"]
    loc = locals()
    w_in = {n: loc[n] for n in names}
    m_in = {n: loc["m_" + n] for n in names}
    v_in = {n: loc["v_" + n] for n in names}
    chip = 2 * lax.axis_index("x") + lax.axis_index("y")
    core = lax.axis_index("c")
    chip_idx = jnp.reshape(chip, (1,)).astype(jnp.int32)
    cidx = jnp.reshape(core, (1,)).astype(jnp.int32)

    big = [("w_glu", s5_w_glu[0], False), ("w_out", s5_w_out[0], False), ("w_ao", attn_w_out[0], False),
           ("w_kv", attn_w_kv, True), ("w_q", attn_w_q[0], True),
           ("w_up0", ffn_w_up[0], True), ("w_up1", ffn_w_up[1], True),
           ("w_down0", ffn_w_down[0], False), ("w_down1", ffn_w_down[1], False)]
    big_src = {"w_glu": ("s5_w_glu", 0), "w_out": ("s5_w_out", 0), "w_ao": ("attn_w_out", 0), "w_kv": ("attn_w_kv", None),
               "w_q": ("attn_w_q", 0), "w_up0": ("ffn_w_up", 0), "w_up1": ("ffn_w_up", 1),
               "w_down0": ("ffn_w_down", 0), "w_down1": ("ffn_w_down", 1)}

    small_sh = {"b_glu": s5_b_glu[0], "conv_w": ffn_conv_w, "ln_gain": ln_gain, "ln_bias": ln_bias}
    sh_shapes = [small_sh[k].shape for k in SMALL_SHARDED]
    sh_pack = _pack([small_sh[k] for k in SMALL_SHARDED], 128, 16)

    shards = [w.astype(MXU_DTYPE).reshape(2, w.shape[0] // 2, w.shape[1]) for _, w, _ in big]
    shards.append(sh_pack.reshape(2, sh_pack.shape[0] // 2, 128))
    gathered = _gather(shards, chip, name="weights_all_gather")

    W = {}
    for (key, w, colsh), g in zip(big, gathered[:-1]):
        r, c = w.shape
        W[key] = g.reshape(4, r, c) if colsh else g.reshape(1, 4 * r, c)
    parts = [_unpack(gathered[-1][p], sh_shapes) for p in range(4)]
    S = {k: jnp.concatenate([parts[p][i] for p in range(4)], axis=-1) for i, k in enumerate(SMALL_SHARDED)}
    S.update(lam_re=s5_lam_re[0], lam_im=s5_lam_im[0], log_dt=s5_log_dt[0], b_re=s5_b_re[0], b_im=s5_b_im[0],
             c_re=s5_c_re[0], c_im=s5_c_im[0], d=s5_d[0], rel_bias=rel_bias, conv_b=ffn_conv_b)

    loss, grad_x, GW, GS = _local_step(x, loss_target, W, S)
    loss = lax.psum(loss, ("x", "y", "c"))

    gs_shapes = [GS[k].shape for k in SMALL_ORDER]
    gs_pack = _pack([GS[k] for k in SMALL_ORDER], 128, 64)
    rs = gs_pack.shape[0] // 8
    grads = []
    for key, w, colsh in big:
        r, c = w.shape
        grads.append(GW[key].reshape(4, 2, r // 2, c))
    grads.append(gs_pack.reshape(4, 2, rs, 128))
    mine, other = _reduce_scatter(grads, [True] * len(big) + [False], cidx, chip_idx, "g")
    small_halves = jnp.where(core == 0, jnp.concatenate([mine[-1], other[-1]]), jnp.concatenate([other[-1], mine[-1]]))
    small_all = _gather([small_halves], chip, name="small_grads_all_gather")[0]
    gsmall = dict(zip(SMALL_ORDER, _unpack(small_all, gs_shapes)))

    big_res = {}
    for (key, w, colsh), gm, go in zip(big, mine[:-1], other[:-1]):
        r, c = w.shape
        src, l = big_src[key]
        mm = m_in[src] if l is None else m_in[src][l]
        vv = v_in[src] if l is None else v_in[src][l]
        halves = lambda t: t.reshape(2, r // 2, c)
        res4 = _adamw_halves(halves(w), halves(mm), halves(vv), gm, go, cidx, name=f"adamw_{key}")
        big_res[key] = tuple(t.reshape(r, c) for t in res4)

    def big_out(i):
        o = {}
        o["s5_w_glu"] = big_res["w_glu"][i][None]
        o["s5_w_out"] = big_res["w_out"][i][None]
        o["attn_w_out"] = big_res["w_ao"][i][None]
        o["attn_w_kv"] = big_res["w_kv"][i]
        o["attn_w_q"] = big_res["w_q"][i][None]
        o["ffn_w_up"] = jnp.stack([big_res["w_up0"][i], big_res["w_up1"][i]])
        o["ffn_w_down"] = jnp.stack([big_res["w_down0"][i], big_res["w_down1"][i]])
        return o

    small_w = {"lam_re": s5_lam_re, "lam_im": s5_lam_im, "log_dt": s5_log_dt, "b_re": s5_b_re, "b_im": s5_b_im,
               "c_re": s5_c_re, "c_im": s5_c_im, "d": s5_d, "rel_bias": rel_bias, "conv_b": ffn_conv_b,
               "b_glu": s5_b_glu, "conv_w": ffn_conv_w, "ln_gain": ln_gain, "ln_bias": ln_bias}
    small_name = {"lam_re": "s5_lam_re", "lam_im": "s5_lam_im", "log_dt": "s5_log_dt", "b_re": "s5_b_re", "b_im": "s5_b_im",
                  "c_re": "s5_c_re", "c_im": "s5_c_im", "d": "s5_d", "rel_bias": "rel_bias", "conv_b": "ffn_conv_b",
                  "b_glu": "s5_b_glu", "conv_w": "ffn_conv_w", "ln_gain": "ln_gain", "ln_bias": "ln_bias"}
    sg = {}
    for k in SMALL_ORDER:
        shp = small_w[k].shape
        g = gsmall[k]
        if k in SMALL_SHARDED:
            width = shp[-1]
            g = lax.dynamic_slice_in_dim(g, chip * width, width, axis=g.ndim - 1)
        sg[k] = g.reshape(shp)
    sshapes = [small_w[k].shape for k in SMALL_ORDER]
    pw = _pack([small_w[k] for k in SMALL_ORDER], 128, 512)
    pg = _pack([sg[k] for k in SMALL_ORDER], 128, 512)
    pm = _pack([m_in[small_name[k]] for k in SMALL_ORDER], 128, 512)
    pv = _pack([v_in[small_name[k]] for k in SMALL_ORDER], 128, 512)
    sd, snm, snv = _adamw(pw, pg, pm, pv, name="adamw_small")
    sd = dict(zip(SMALL_ORDER, _unpack(sd, sshapes)))
    snm = dict(zip(SMALL_ORDER, _unpack(snm, sshapes)))
    snv = dict(zip(SMALL_ORDER, _unpack(snv, sshapes)))

    res = [{}, {}, {}, {}]
    for i in range(4):
        res[i].update(big_out(i))
    for k in SMALL_ORDER:
        res[0][small_name[k]] = sg[k]
        res[1][small_name[k]] = sd[k]
        res[2][small_name[k]] = snm[k]
        res[3][small_name[k]] = snv[k]
    outs = [loss, grad_x]
    for i in range(4):
        outs += [res[i][n] for n in names]
    return tuple(outs)
```

```python
import functools
import math

import numpy as np
import jax
import jax.numpy as jnp
from jax import lax
from jax.experimental import pallas as pl
from jax.experimental.pallas import tpu as pltpu

F32 = jnp.float32
BF16 = jnp.bfloat16
MXU_DTYPE = jnp.bfloat16
V7X_VMEM_LIMIT_BYTES = 52 << 20
MESH = pl.DeviceIdType.MESH

DEPTH = 2
SSM_GROUP = 16
SSM_STATE = 64
GROUPS_PER_CLUSTER = 16
CLUSTER_W = GROUPS_PER_CLUSTER * SSM_GROUP
HEAD_DIM = 64
DILATIONS = (1, 4, 16)
BAND = 128
NEG_BIG = -1e30
REL_BUCKETS = 32
REL_MAX_DIST = 2048
DN_ALPHA = (2.0 * DEPTH) ** 0.25
LN_EPS = 1e-5
ADAM_LR, ADAM_B1, ADAM_B2, ADAM_EPS, ADAM_WD, ADAM_STEP = 0.001, 0.9, 0.999, 1e-08, 0.01, 10
GELU_K = math.sqrt(2.0 / math.pi)
GELU_C = 0.044715


def _pallas(body, **kw):
    return pl.pallas_call(body, **kw)


def _params(sem=None):
    return pltpu.CompilerParams(dimension_semantics=sem, vmem_limit_bytes=V7X_VMEM_LIMIT_BYTES)


def _pick(n, cands):
    for c in cands:
        if n % c == 0:
            return c
    return n


def _sigmoid(z):
    return 1.0 / (1.0 + jnp.exp(-z))


def _gelu(y):
    return 0.5 * y * (1.0 + jnp.tanh(GELU_K * (y + GELU_C * y * y * y)))


def _gelu_grad(y):
    t = jnp.tanh(GELU_K * (y + GELU_C * y * y * y))
    return 0.5 * (1.0 + t) + 0.5 * y * (1.0 - t * t) * (GELU_K * (1.0 + 3.0 * GELU_C * y * y))


def _mm_nn(a, w, *, bias=None, out_dtype=F32, name):
    T, K = a.shape
    P, _, Np = w.shape
    tm = _pick(T, (1024, 512, 256, 128))
    tn = _pick(Np, (1408, 1024, 768, 512, 384, 256, 128))
    nj = Np // tn

    def body(*refs):
        if bias is None:
            a_ref, w_ref, o_ref = refs
        else:
            a_ref, w_ref, b_ref, o_ref = refs
        acc = jnp.dot(a_ref[...].astype(MXU_DTYPE), w_ref[...].astype(MXU_DTYPE), preferred_element_type=F32)
        if bias is not None:
            acc = acc + b_ref[...]
        o_ref[...] = acc.astype(o_ref.dtype)

    in_specs = [pl.BlockSpec((tm, K), lambda p, j, i: (i, 0)),
                pl.BlockSpec((None, K, tn), lambda p, j, i: (p, 0, j))]
    args = [a, w]
    if bias is not None:
        in_specs.append(pl.BlockSpec((1, tn), lambda p, j, i: (0, p * nj + j)))
        args.append(bias)
    return _pallas(
        body, name=name, grid=(P, nj, T // tm), in_specs=in_specs,
        out_specs=pl.BlockSpec((tm, tn), lambda p, j, i: (i, p * nj + j)),
        out_shape=jax.ShapeDtypeStruct((T, P * Np), out_dtype),
        compiler_params=_params(("parallel", "parallel", "parallel")),
    )(*args)


def _mm_nt(a, w, *, p0=0, pn=None, name):
    T = a.shape[0]
    _, K, Np = w.shape
    pn = w.shape[0] if pn is None else pn
    tm = _pick(T, (1024, 512, 256, 128) if K <= 1024 else (512, 256, 128))
    tn = _pick(Np, (1536, 1408, 1024, 768, 512, 384, 256, 128))
    nj = Np // tn
    nred = pn * nj

    def body(a_ref, w_ref, o_ref, acc):
        r = pl.program_id(1)

        @pl.when(r == 0)
        def _():
            acc[...] = jnp.zeros_like(acc)

        acc[...] += lax.dot_general(a_ref[...].astype(MXU_DTYPE), w_ref[...].astype(MXU_DTYPE),
                                    (((1,), (1,)), ((), ())), preferred_element_type=F32)

        @pl.when(r == nred - 1)
        def _():
            o_ref[...] = acc[...]

    return _pallas(
        body, name=name, grid=(T // tm, nred),
        in_specs=[pl.BlockSpec((tm, tn), lambda i, r: (i, r)),
                  pl.BlockSpec((None, K, tn), lambda i, r: (p0 + r // nj, 0, r % nj))],
        out_specs=pl.BlockSpec((tm, K), lambda i, r: (i, 0)),
        out_shape=jax.ShapeDtypeStruct((T, K), F32),
        scratch_shapes=[pltpu.VMEM((tm, K), F32)],
        compiler_params=_params(("parallel", "arbitrary")),
    )(a, w)


def _tn(a, b, *, ptotal, np_cols, p0=0, prev=None, name):
    T, K = a.shape
    Np = np_cols
    pn = b.shape[1] // Np
    tt = _pick(T, (1024, 512, 256, 128))
    tk = _pick(K, (1408, 1024, 512, 256, 128))
    tn = _pick(Np, (1408, 768, 512, 256, 128))
    if tk * tn > 1408 * 1024:
        tn = _pick(Np, (512, 256, 128))
    nj = Np // tn
    nt = T // tt

    def body(*refs):
        a_ref, b_ref = refs[0], refs[1]
        o_ref, acc = refs[-2], refs[-1]
        t = pl.program_id(3)

        @pl.when(t == 0)
        def _():
            acc[...] = jnp.zeros_like(acc)

        acc[...] += lax.dot_general(a_ref[...].astype(MXU_DTYPE), b_ref[...].astype(MXU_DTYPE),
                                    (((0,), (0,)), ((), ())), preferred_element_type=F32)

        @pl.when(t == nt - 1)
        def _():
            o_ref[...] = acc[...]

    in_specs = [pl.BlockSpec((tt, tk), lambda kb, p, j, t: (t, kb)),
                pl.BlockSpec((tt, tn), lambda kb, p, j, t: (t, p * nj + j))]
    args = [a, b]
    aliases = {}
    if prev is not None:
        in_specs.append(pl.BlockSpec(memory_space=pl.ANY))
        args.append(prev)
        aliases = {2: 0}
    return _pallas(
        body, name=name, grid=(K // tk, pn, nj, nt), in_specs=in_specs,
        out_specs=pl.BlockSpec((None, tk, tn), lambda kb, p, j, t: (p0 + p, kb, j)),
        out_shape=jax.ShapeDtypeStruct((ptotal, K, Np), F32),
        scratch_shapes=[pltpu.VMEM((tk, tn), F32)],
        input_output_aliases=aliases,
        compiler_params=_params(("parallel", "parallel", "parallel", "arbitrary")),
    )(*args)


def _rows(tm, f):
    return pl.BlockSpec((tm, f), lambda i: (i, 0))


def _whole(shape):
    nd = len(shape)
    return pl.BlockSpec(shape, lambda i: (0,) * nd)


def _ln_fwd(xres, f, gain, bias, *, name):
    T, D = xres.shape
    tm = _pick(T, (256, 128))

    def body(x_ref, f_ref, g_ref, b_ref, y_ref, yb_ref, xh_ref, rs_ref):
        z = DN_ALPHA * x_ref[...] + f_ref[...]
        mu = jnp.mean(z, axis=-1, keepdims=True)
        zc = z - mu
        var = jnp.mean(zc * zc, axis=-1, keepdims=True)
        rstd = lax.rsqrt(var + LN_EPS)
        xh = zc * rstd
        y = xh * g_ref[...] + b_ref[...]
        y_ref[...] = y
        yb_ref[...] = y.astype(yb_ref.dtype)
        xh_ref[...] = xh
        rs_ref[...] = rstd

    return _pallas(
        body, name=name, grid=(T // tm,),
        in_specs=[_rows(tm, D), _rows(tm, D), _whole((1, D)), _whole((1, D))],
        out_specs=[_rows(tm, D), _rows(tm, D), _rows(tm, D), _rows(tm, 1)],
        out_shape=[jax.ShapeDtypeStruct((T, D), F32), jax.ShapeDtypeStruct((T, D), MXU_DTYPE),
                   jax.ShapeDtypeStruct((T, D), F32), jax.ShapeDtypeStruct((T, 1), F32)],
        compiler_params=_params(("parallel",)),
    )(xres, f, gain, bias)


def _ln_bwd(addends, coefs, xhat, rstd, gain, *, name):
    T, D = xhat.shape
    tm = _pick(T, (256, 128))
    n = len(addends)

    def body(*refs):
        adds = refs[:n]
        xh_ref, rs_ref, g_ref, dz_ref, dzb_ref, dg_ref, db_ref = refs[n:]
        dy = coefs[0] * adds[0][...]
        for c, r in zip(coefs[1:], adds[1:]):
            dy = dy + c * r[...]
        xh = xh_ref[...]
        dxh = dy * g_ref[...]
        m1 = jnp.mean(dxh, axis=-1, keepdims=True)
        m2 = jnp.mean(dxh * xh, axis=-1, keepdims=True)
        dz = rs_ref[...] * (dxh - m1 - xh * m2)
        dz_ref[...] = dz
        dzb_ref[...] = dz.astype(dzb_ref.dtype)

        @pl.when(pl.program_id(0) == 0)
        def _():
            dg_ref[...] = jnp.zeros_like(dg_ref)
            db_ref[...] = jnp.zeros_like(db_ref)

        dg_ref[...] += jnp.sum(dy * xh, axis=0, keepdims=True)
        db_ref[...] += jnp.sum(dy, axis=0, keepdims=True)

    return _pallas(
        body, name=name, grid=(T // tm,),
        in_specs=[_rows(tm, D)] * n + [_rows(tm, D), _rows(tm, 1), _whole((1, D))],
        out_specs=[_rows(tm, D), _rows(tm, D), _whole((1, D)), _whole((1, D))],
        out_shape=[jax.ShapeDtypeStruct((T, D), F32), jax.ShapeDtypeStruct((T, D), MXU_DTYPE),
                   jax.ShapeDtypeStruct((1, D), F32), jax.ShapeDtypeStruct((1, D), F32)],
        compiler_params=_params(("arbitrary",)),
    )(*addends, xhat, rstd, gain)


def _loss_grad(y, tgt, *, name):
    T, D = y.shape
    tm = _pick(T, (256, 128))

    def body(y_ref, t_ref, dy_ref, l_ref):
        e = y_ref[...] - t_ref[...]
        dy_ref[...] = e * (1.0 / D)

        @pl.when(pl.program_id(0) == 0)
        def _():
            l_ref[...] = jnp.zeros_like(l_ref)

        l_ref[...] += jnp.zeros_like(l_ref) + jnp.sum(e * e) * (0.5 / D)

    return _pallas(
        body, name=name, grid=(T // tm,),
        in_specs=[_rows(tm, D), _rows(tm, D)],
        out_specs=[_rows(tm, D), _whole((1, 128))],
        out_shape=[jax.ShapeDtypeStruct((T, D), F32), jax.ShapeDtypeStruct((1, 128), F32)],
        compiler_params=_params(("arbitrary",)),
    )(y, tgt)


def _axpy(a, b, ca, *, name):
    T, D = a.shape
    tm = _pick(T, (256, 128))

    def body(a_ref, b_ref, o_ref):
        o_ref[...] = ca * a_ref[...] + b_ref[...]

    return _pallas(
        body, name=name, grid=(T // tm,), in_specs=[_rows(tm, D), _rows(tm, D)], out_specs=_rows(tm, D),
        out_shape=jax.ShapeDtypeStruct((T, D), F32), compiler_params=_params(("parallel",)),
    )(a, b)


def _glu_gate(y, z, *, name):
    T, D = y.shape
    tm = _pick(T, (256, 128))

    def body(y_ref, z_ref, g_ref):
        g_ref[...] = (_gelu(y_ref[...]) * _sigmoid(z_ref[...])).astype(g_ref.dtype)

    return _pallas(
        body, name=name, grid=(T // tm,), in_specs=[_rows(tm, D), _rows(tm, D)], out_specs=_rows(tm, D),
        out_shape=jax.ShapeDtypeStruct((T, D), MXU_DTYPE), compiler_params=_params(("parallel",)),
    )(y, z)


def _glu_bwd(y, z, dg, *, name):
    T, D = y.shape
    tm = _pick(T, (256, 128))

    def body(y_ref, z_ref, dg_ref, dzb_ref, dyg_ref, db_ref):
        s = _sigmoid(z_ref[...])
        dg = dg_ref[...]
        dz = dg * _gelu(y_ref[...]) * s * (1.0 - s)
        dzb_ref[...] = dz.astype(dzb_ref.dtype)
        dyg_ref[...] = dg * s

        @pl.when(pl.program_id(0) == 0)
        def _():
            db_ref[...] = jnp.zeros_like(db_ref)

        db_ref[...] += jnp.sum(dz, axis=0, keepdims=True)

    return _pallas(
        body, name=name, grid=(T // tm,), in_specs=[_rows(tm, D)] * 3,
        out_specs=[_rows(tm, D), _rows(tm, D), _whole((1, D))],
        out_shape=[jax.ShapeDtypeStruct((T, D), MXU_DTYPE), jax.ShapeDtypeStruct((T, D), F32),
                   jax.ShapeDtypeStruct((1, D), F32)],
        compiler_params=_params(("arbitrary",)),
    )(y, z, dg)


def _gelu_bwd(y, d1, d2, *, name):
    T, D = y.shape
    tm = _pick(T, (256, 128))

    def body(y_ref, a_ref, b_ref, o_ref):
        o_ref[...] = (a_ref[...] + b_ref[...]) * _gelu_grad(y_ref[...])

    return _pallas(
        body, name=name, grid=(T // tm,), in_specs=[_rows(tm, D)] * 3, out_specs=_rows(tm, D),
        out_shape=jax.ShapeDtypeStruct((T, D), F32), compiler_params=_params(("parallel",)),
    )(y, d1, d2)


CONV_ROWS = 128


def _shift_back(x, edge, at_start, tm):
    rows = lax.broadcasted_iota(jnp.int32, x.shape, 0)
    keep = jnp.where(at_start, 0.0, 1.0)
    e7 = edge[7:8, :] * keep
    e6 = edge[6:7, :] * keep
    r1 = pltpu.roll(x, 1, 0)
    r2 = pltpu.roll(x, 2, 0)
    x1 = jnp.where(rows == 0, e7, r1)
    x2 = jnp.where(rows == 0, e6, jnp.where(rows == 1, e7, r2))
    return x1, x2


def _conv_specs(T, F2, tm):
    return [_rows(tm, F2),
            pl.BlockSpec((8, F2), lambda i: (jnp.maximum(i * (tm // 8) - 1, 0), 0))]


def _conv_glu_fwd(hc, conv_w, conv_b, L, *, name):
    T, F2 = hc.shape
    F = F2 // 2
    tm = CONV_ROWS

    def body(x_ref, e_ref, w_ref, b_ref, a_ref):
        at_start = (pl.program_id(0) * tm) % L == 0
        x = x_ref[...]
        x1, x2 = _shift_back(x, e_ref[...], at_start, tm)
        c = b_ref[...] + w_ref[0:1, :] * x + w_ref[1:2, :] * x1 + w_ref[2:3, :] * x2
        val, gate = c[:, :F], c[:, F:]
        a_ref[...] = (gate * _sigmoid(gate) * val).astype(a_ref.dtype)

    return _pallas(
        body, name=name, grid=(T // tm,),
        in_specs=_conv_specs(T, F2, tm) + [_whole((3, F2)), _whole((1, F2))],
        out_specs=_rows(tm, F),
        out_shape=jax.ShapeDtypeStruct((T, F), MXU_DTYPE), compiler_params=_params(("parallel",)),
    )(hc, hc, conv_w, conv_b)


def _conv_glu_bwd(hc, da, conv_w, conv_b, L, *, name):
    T, F2 = hc.shape
    F = F2 // 2
    tm = CONV_ROWS

    def body(x_ref, e_ref, da_ref, w_ref, b_ref, dc_ref, dw_ref, db_ref):
        at_start = (pl.program_id(0) * tm) % L == 0
        x = x_ref[...]
        x1, x2 = _shift_back(x, e_ref[...], at_start, tm)
        c = b_ref[...] + w_ref[0:1, :] * x + w_ref[1:2, :] * x1 + w_ref[2:3, :] * x2
        val, gate = c[:, :F], c[:, F:]
        s = _sigmoid(gate)
        da = da_ref[...]
        dval = da * (gate * s)
        dgate = da * val * (s * (1.0 + gate * (1.0 - s)))
        dc = jnp.concatenate([dval, dgate], axis=-1)
        dc_ref[...] = dc.astype(dc_ref.dtype)

        @pl.when(pl.program_id(0) == 0)
        def _():
            dw_ref[...] = jnp.zeros_like(dw_ref)
            db_ref[...] = jnp.zeros_like(db_ref)

        dw_ref[0:1, :] += jnp.sum(dc * x, axis=0, keepdims=True)
        dw_ref[1:2, :] += jnp.sum(dc * x1, axis=0, keepdims=True)
        dw_ref[2:3, :] += jnp.sum(dc * x2, axis=0, keepdims=True)
        db_ref[...] += jnp.sum(dc, axis=0, keepdims=True)

    return _pallas(
        body, name=name, grid=(T // tm,),
        in_specs=_conv_specs(T, F2, tm) + [_rows(tm, F), _whole((3, F2)), _whole((1, F2))],
        out_specs=[_rows(tm, F2), _whole((3, F2)), _whole((1, F2))],
        out_shape=[jax.ShapeDtypeStruct((T, F2), MXU_DTYPE), jax.ShapeDtypeStruct((3, F2), F32),
                   jax.ShapeDtypeStruct((1, F2), F32)],
        compiler_params=_params(("arbitrary",)),
    )(hc, hc, da, conv_w, conv_b)


def _conv_bwd_input(dc, conv_w, L, *, name):
    T, F2 = dc.shape
    tm = CONV_ROWS
    edge = 16
    last_blk = T // edge - 1

    def body(x_ref, e_ref, w_ref, o_ref):
        at_end = ((pl.program_id(0) + 1) * tm) % L == 0
        x = x_ref[...].astype(F32)
        rows = lax.broadcasted_iota(jnp.int32, x.shape, 0)
        keep = jnp.where(at_end, 0.0, 1.0)
        ev = e_ref[...].astype(F32)
        e0 = ev[0:1, :] * keep
        e1 = ev[1:2, :] * keep
        u1 = pltpu.roll(x, tm - 1, 0)
        u2 = pltpu.roll(x, tm - 2, 0)
        x1 = jnp.where(rows == tm - 1, e0, u1)
        x2 = jnp.where(rows == tm - 1, e1, jnp.where(rows == tm - 2, e0, u2))
        o_ref[...] = (w_ref[0:1, :] * x + w_ref[1:2, :] * x1 + w_ref[2:3, :] * x2).astype(o_ref.dtype)

    return _pallas(
        body, name=name, grid=(T // tm,),
        in_specs=[_rows(tm, F2),
                  pl.BlockSpec((edge, F2), lambda i: (jnp.minimum((i + 1) * (tm // edge), last_blk), 0)),
                  _whole((3, F2))],
        out_specs=_rows(tm, F2),
        out_shape=jax.ShapeDtypeStruct((T, F2), MXU_DTYPE), compiler_params=_params(("parallel",)),
    )(dc, dc, conv_w)


S5_CHUNK = 128
LANES = 128


def _slab_rows(c, n, ncl):
    return pl.ds(c, n) if ncl == 1 else pl.ds(c, n, stride=ncl)


def _slab_put(ref, c, n, ncl, val):
    for s in range(val.shape[1] // LANES):
        ref[s, _slab_rows(c, n, ncl), :] = val[:, s * LANES:(s + 1) * LANES]


def _slab_get(ref, c, n, ncl):
    return jnp.concatenate([ref[s, _slab_rows(c, n, ncl), :] for s in range(ref.shape[0])], axis=-1)


def _slabs(n_slab, rows):
    return pl.BlockSpec((n_slab, rows, LANES), lambda i: (0, i, 0))


def _s5_fwd(xi, wb, wc, a_r, a_i, d_row, B, *, name):
    T, D = xi.shape
    ncl = wb.shape[0]
    cs = wb.shape[2] // 2
    ns = cs // LANES
    R = B * ncl
    Q = S5_CHUNK
    QR = Q * ncl
    nsteps = Q // B

    def body(x_ref, wb_ref, wc_ref, ar_ref, ai_ref, d_ref, y_ref, yg_ref, hr_ref, hi_ref, bur, bui, cr, ci):
        @pl.when(pl.program_id(0) == 0)
        def _():
            cr[...] = jnp.zeros_like(cr)
            ci[...] = jnp.zeros_like(ci)

        x = x_ref[...]
        xb = x.astype(MXU_DTYPE)
        for c in range(ncl):
            bu = jnp.dot(xb[:, c * CLUSTER_W:(c + 1) * CLUSTER_W], wb_ref[c], preferred_element_type=F32)
            _slab_put(bur, c, Q, ncl, bu[:, :cs])
            _slab_put(bui, c, Q, ncl, bu[:, cs:])
        ar = ar_ref[...]
        ai = ai_ref[...]

        def step(k, carry):
            hr, hi = carry
            sl = pl.ds(pl.multiple_of(k * R, R), R)
            nr = ar * hr - ai * hi + bur[:, sl, :]
            ni = ar * hi + ai * hr + bui[:, sl, :]
            hr_ref[:, sl, :] = nr
            hi_ref[:, sl, :] = ni
            return nr, ni

        hr, hi = lax.fori_loop(0, nsteps, step, (cr[...], ci[...]), unroll=4)
        cr[...] = hr
        ci[...] = hi
        parts = []
        for c in range(ncl):
            hrc = _slab_get(hr_ref, c, Q, ncl).astype(MXU_DTYPE)
            hic = _slab_get(hi_ref, c, Q, ncl).astype(MXU_DTYPE)
            parts.append(jnp.dot(hrc, wc_ref[c, :cs, :], preferred_element_type=F32)
                         + jnp.dot(hic, wc_ref[c, cs:, :], preferred_element_type=F32))
        y = d_ref[...] * x + (parts[0] if ncl == 1 else jnp.concatenate(parts, axis=-1))
        y_ref[...] = y
        yg_ref[...] = _gelu(y).astype(yg_ref.dtype)

    return _pallas(
        body, name=name, grid=(T // Q,),
        in_specs=[_rows(Q, D), _whole(wb.shape), _whole(wc.shape), _whole((ns, R, LANES)), _whole((ns, R, LANES)),
                  _whole((1, D))],
        out_specs=[_rows(Q, D), _rows(Q, D), _slabs(ns, QR), _slabs(ns, QR)],
        out_shape=[jax.ShapeDtypeStruct((T, D), F32), jax.ShapeDtypeStruct((T, D), MXU_DTYPE),
                   jax.ShapeDtypeStruct((ns, T * ncl, LANES), F32), jax.ShapeDtypeStruct((ns, T * ncl, LANES), F32)],
        scratch_shapes=[pltpu.VMEM((ns, QR, LANES), F32), pltpu.VMEM((ns, QR, LANES), F32),
                        pltpu.VMEM((ns, R, LANES), F32), pltpu.VMEM((ns, R, LANES), F32)],
        compiler_params=_params(("arbitrary",)),
    )(xi, wb, wc, a_r, a_i, d_row)


def _s5_bwd(dy, xi, h_r, h_i, wb, wc, a_r, a_i, d_row, B, *, name):
    T, D = dy.shape
    ncl = wb.shape[0]
    cs = wb.shape[2] // 2
    ns = cs // LANES
    R = B * ncl
    Q = S5_CHUNK
    nsteps = Q // B
    nchunk = T // Q
    QR = Q * ncl

    def rev(i):
        return nchunk - 1 - i

    def body(dy_ref, x_ref, hr_ref, hi_ref, pr_ref, pi_ref, wb_ref, wc_ref, ar_ref, ai_ref, d_ref,
             du_ref, gr_ref, gi_ref, dar_ref, dai_ref, dd_ref, dhr, dhi, cr, ci):
        i = pl.program_id(0)

        @pl.when(i == 0)
        def _():
            cr[...] = jnp.zeros_like(cr)
            ci[...] = jnp.zeros_like(ci)
            dar_ref[...] = jnp.zeros_like(dar_ref)
            dai_ref[...] = jnp.zeros_like(dai_ref)
            dd_ref[...] = jnp.zeros_like(dd_ref)

        dyv = dy_ref[...]
        dyb = dyv.astype(MXU_DTYPE)
        for c in range(ncl):
            dh = lax.dot_general(dyb[:, c * CLUSTER_W:(c + 1) * CLUSTER_W], wc_ref[c],
                                 (((1,), (1,)), ((), ())), preferred_element_type=F32)
            _slab_put(dhr, c, Q, ncl, dh[:, :cs])
            _slab_put(dhi, c, Q, ncl, dh[:, cs:])
        ar = ar_ref[...]
        ai = ai_ref[...]

        def step(j, carry):
            gr, gi, sar, sai = carry
            k = nsteps - 1 - j
            sl = pl.ds(pl.multiple_of(k * R, R), R)
            ngr = dhr[:, sl, :] + ar * gr + ai * gi
            ngi = dhi[:, sl, :] - ai * gr + ar * gi
            gr_ref[:, sl, :] = ngr
            gi_ref[:, sl, :] = ngi
            pv = pl.ds(pl.multiple_of((k - 1) * R, R), R)
            hpr = hr_ref[:, pv, :]
            hpi = hi_ref[:, pv, :]
            return ngr, ngi, sar + ngr * hpr + ngi * hpi, sai - ngr * hpi + ngi * hpr

        gr, gi, sar, sai = lax.fori_loop(0, nsteps - 1, step, (cr[...], ci[...], dar_ref[...], dai_ref[...]), unroll=4)
        sl0 = pl.ds(0, R)
        ngr = dhr[:, sl0, :] + ar * gr + ai * gi
        ngi = dhi[:, sl0, :] - ai * gr + ar * gi
        gr_ref[:, sl0, :] = ngr
        gi_ref[:, sl0, :] = ngi
        keep = jnp.where(i == nchunk - 1, 0.0, 1.0)
        hpr = pr_ref[:, 8 - R:8, :] * keep
        hpi = pi_ref[:, 8 - R:8, :] * keep
        dar_ref[...] = sar + ngr * hpr + ngi * hpi
        dai_ref[...] = sai - ngr * hpi + ngi * hpr
        cr[...] = ngr
        ci[...] = ngi
        parts = []
        for c in range(ncl):
            grc = _slab_get(gr_ref, c, Q, ncl).astype(MXU_DTYPE)
            gic = _slab_get(gi_ref, c, Q, ncl).astype(MXU_DTYPE)
            parts.append(lax.dot_general(grc, wb_ref[c, :, :cs], (((1,), (1,)), ((), ())), preferred_element_type=F32)
                         + lax.dot_general(gic, wb_ref[c, :, cs:], (((1,), (1,)), ((), ())), preferred_element_type=F32))
        du_ref[...] = d_ref[...] * dyv + (parts[0] if ncl == 1 else jnp.concatenate(parts, axis=-1))
        dd_ref[...] += jnp.sum(dyv * x_ref[...], axis=0, keepdims=True)

    tok = pl.BlockSpec((Q, D), lambda i: (rev(i), 0))
    st = pl.BlockSpec((ns, QR, LANES), lambda i: (0, rev(i), 0))
    before = pl.BlockSpec((ns, 8, LANES), lambda i: (0, jnp.maximum(rev(i) * (QR // 8) - 1, 0), 0))
    acc = _whole((ns, R, LANES))
    return _pallas(
        body, name=name, grid=(nchunk,),
        in_specs=[tok, tok, st, st, before, before, _whole(wb.shape), _whole(wc.shape), acc, acc, _whole((1, D))],
        out_specs=[tok, st, st, acc, acc, _whole((1, D))],
        out_shape=[jax.ShapeDtypeStruct((T, D), F32),
                   jax.ShapeDtypeStruct((ns, T * ncl, LANES), F32), jax.ShapeDtypeStruct((ns, T * ncl, LANES), F32),
                   jax.ShapeDtypeStruct((ns, R, LANES), F32), jax.ShapeDtypeStruct((ns, R, LANES), F32),
                   jax.ShapeDtypeStruct((1, D), F32)],
        scratch_shapes=[pltpu.VMEM((ns, QR, LANES), F32)] * 2 + [pltpu.VMEM((ns, R, LANES), F32)] * 2,
        compiler_params=_params(("arbitrary",)),
    )(dy, xi, h_r, h_i, h_r, h_i, wb, wc, a_r, a_i, d_row)


def _cluster_tn(tok, st, ncl, *, tok_left, name):
    T = tok.shape[0]
    ns = st.shape[0]
    cs = ns * LANES
    tt = _pick(T, (512, 256, 128))
    nt = T // tt
    oshape = (ncl, CLUSTER_W, cs) if tok_left else (ncl, cs, CLUSTER_W)

    def body(tok_ref, st_ref, o_ref, acc):
        t = pl.program_id(0)

        @pl.when(t == 0)
        def _():
            acc[...] = jnp.zeros_like(acc)

        tk = tok_ref[...].astype(MXU_DTYPE)
        for c in range(ncl):
            tc = tk[:, c * CLUSTER_W:(c + 1) * CLUSTER_W]
            sc = _slab_get(st_ref, c, tt, ncl).astype(MXU_DTYPE)
            lhs, rhs = (tc, sc) if tok_left else (sc, tc)
            acc[c] += lax.dot_general(lhs, rhs, (((0,), (0,)), ((), ())), preferred_element_type=F32)

        @pl.when(t == nt - 1)
        def _():
            o_ref[...] = acc[...]

    return _pallas(
        body, name=name, grid=(nt,),
        in_specs=[_rows(tt, tok.shape[1]), _slabs(ns, tt * ncl)],
        out_specs=_whole(oshape),
        out_shape=jax.ShapeDtypeStruct(oshape, F32),
        scratch_shapes=[pltpu.VMEM(oshape, F32)],
        compiler_params=_params(("arbitrary",)),
    )(tok, st)


def _s5_discretize(lam_re, lam_im, log_dt, b_re, b_im):
    dt = jnp.exp(log_dt)[:, None]
    mag = jnp.exp(lam_re * dt)
    ab_r, ab_i = mag * jnp.cos(lam_im * dt), mag * jnp.sin(lam_im * dt)
    den = lam_re * lam_re + lam_im * lam_im
    nr = ab_r - 1.0
    co_r = (nr * lam_re + ab_i * lam_im) / den
    co_i = (ab_i * lam_re - nr * lam_im) / den
    bb_r = co_r[..., None] * b_re - co_i[..., None] * b_im
    bb_i = co_r[..., None] * b_im + co_i[..., None] * b_re
    return ab_r, ab_i, bb_r, bb_i


def _blockdiag(m):
    G, r, k = m.shape
    ncl = G // GROUPS_PER_CLUSTER
    m4 = m.reshape(ncl, GROUPS_PER_CLUSTER, r, k)
    eye = jnp.eye(GROUPS_PER_CLUSTER, dtype=m.dtype)
    return jnp.einsum('cgrk,gh->cgrhk', m4, eye).reshape(ncl, GROUPS_PER_CLUSTER * r, GROUPS_PER_CLUSTER * k)


def _unblockdiag(m, r, k):
    ncl = m.shape[0]
    m5 = m.reshape(ncl, GROUPS_PER_CLUSTER, r, GROUPS_PER_CLUSTER, k)
    eye = jnp.eye(GROUPS_PER_CLUSTER, dtype=m.dtype)
    return jnp.einsum('cgrhk,gh->cgrk', m5, eye).reshape(ncl * GROUPS_PER_CLUSTER, r, k)


def _t5_bucket(dist):
    exact = REL_BUCKETS // 2
    d = np.maximum(dist, 1).astype(np.float32)
    large = exact + (np.log(d / exact) / math.log(REL_MAX_DIST / exact) * (REL_BUCKETS - exact)).astype(np.int64)
    large = np.minimum(large, REL_BUCKETS - 1)
    return np.where(dist < exact, dist, large).astype(np.int32)


def _band_tables(dil):
    steps = np.arange(BAND)[:, None] + BAND - np.arange(2 * BAND)[None, :]
    bucket = _t5_bucket(np.maximum(steps, 0) * dil)
    in_band = (steps >= 0) & (steps <= BAND)
    return bucket, in_band


def _attn_bias(rel_bias, hpg):
    out = []
    for g, dil in enumerate(DILATIONS):
        bucket, in_band = _band_tables(dil)
        cols = rel_bias[:, g * hpg:(g + 1) * hpg].astype(F32)
        onehot = jnp.asarray((bucket.reshape(-1, 1) == np.arange(REL_BUCKETS)[None, :]).astype(np.float32))
        bias = jnp.dot(onehot, cols, precision=lax.Precision.HIGHEST).T.reshape(hpg, BAND, 2 * BAND)
        out.append(jnp.where(jnp.asarray(in_band)[None], bias, NEG_BIG))
    return jnp.concatenate(out, axis=0)


def _attn_blocks(dil, L):
    M = L // dil
    return M, M // BAND


def _row_sel(r, M, dil):
    return pl.ds(r, M) if dil == 1 else pl.ds(r, M, stride=dil)


def _attn_fwd(q, kv, bias, L, hpg, *, name):
    T = q.shape[0]
    nb_ = T // L
    HP = hpg // 2
    W3 = 3 * hpg * HEAD_DIM
    mmax = L

    def group_body(dil, q_ref, k_ref, v_ref, b_ref, o_ref, l_ref, os, ls):
        M, NB = _attn_blocks(dil, L)
        for r in range(dil):
            rows = _row_sel(r, M, dil)
            qr = (q_ref[rows, :] * 0.125).astype(MXU_DTYPE)
            kr = k_ref[rows, :].astype(MXU_DTYPE)
            vr = v_ref[rows, :].astype(MXU_DTYPE)
            ones = jnp.ones((M, HEAD_DIM), MXU_DTYPE)
            va = [jnp.concatenate([vr[:, hh * HEAD_DIM:(hh + 1) * HEAD_DIM], ones], axis=-1) for hh in range(2)]
            for n in range(NB):
                qs = slice(n * BAND, (n + 1) * BAND)
                ks = slice(0, BAND) if n == 0 else slice((n - 1) * BAND, (n + 1) * BAND)
                for hh in range(2):
                    ln = slice(hh * HEAD_DIM, (hh + 1) * HEAD_DIM)
                    bb = b_ref[hh, :, BAND:] if n == 0 else b_ref[hh]
                    s = lax.dot_general(qr[qs, ln], kr[ks, ln], (((1,), (1,)), ((), ())),
                                        preferred_element_type=F32) + bb
                    m = jnp.max(s, axis=-1, keepdims=True)
                    p = jnp.exp(s - m)
                    pv = jnp.dot(p.astype(MXU_DTYPE), va[hh][ks, :], preferred_element_type=F32)
                    l = pv[:, HEAD_DIM:]
                    os[qs, ln] = pv[:, :HEAD_DIM] / l
                    ls[qs, ln] = m + jnp.log(l)
            o_ref[rows, :] = os[0:M, :]
            l_ref[rows, :] = ls[0:M, :]

    def body(q_ref, k_ref, v_ref, b_ref, o_ref, l_ref, os, ls):
        g = pl.program_id(0)
        for gi, dil in enumerate(DILATIONS):
            pl.when(g == gi)(functools.partial(group_body, dil, q_ref, k_ref, v_ref, b_ref, o_ref, l_ref, os, ls))

    blk = (L, 2 * HEAD_DIM)
    return _pallas(
        body, name=name, grid=(3, nb_, HP),
        in_specs=[pl.BlockSpec(blk, lambda g, b, h: (b, g * HP + h)),
                  pl.BlockSpec(blk, lambda g, b, h: (b, g * HP + h)),
                  pl.BlockSpec(blk, lambda g, b, h: (b, 3 * HP + g * HP + h)),
                  pl.BlockSpec((2, BAND, 2 * BAND), lambda g, b, h: (g * HP + h, 0, 0))],
        out_specs=[pl.BlockSpec(blk, lambda g, b, h: (b, g * HP + h)),
                   pl.BlockSpec(blk, lambda g, b, h: (b, g * HP + h))],
        out_shape=[jax.ShapeDtypeStruct((T, W3), F32), jax.ShapeDtypeStruct((T, W3), F32)],
        scratch_shapes=[pltpu.VMEM((mmax, 2 * HEAD_DIM), F32), pltpu.VMEM((mmax, 2 * HEAD_DIM), F32)],
        compiler_params=_params(("arbitrary", "arbitrary", "arbitrary")),
    )(q, kv, kv, bias)


def _attn_merge(o3, l3, hw, *, name):
    T = o3.shape[0]
    tm = _pick(T, (256, 128))

    def body(o0, o1, o2, l0, l1, l2, o_ref, ob_ref, lse_ref):
        a0, a1, a2 = l0[...], l1[...], l2[...]
        m = jnp.maximum(jnp.maximum(a0, a1), a2)
        e0, e1, e2 = jnp.exp(a0 - m), jnp.exp(a1 - m), jnp.exp(a2 - m)
        z = e0 + e1 + e2
        o = (e0 * o0[...] + e1 * o1[...] + e2 * o2[...]) / z
        o_ref[...] = o
        ob_ref[...] = o.astype(ob_ref.dtype)
        lse_ref[...] = m + jnp.log(z)

    def col(g):
        return pl.BlockSpec((tm, hw), lambda i: (i, g))

    return _pallas(
        body, name=name, grid=(T // tm,),
        in_specs=[col(0), col(1), col(2), col(0), col(1), col(2)],
        out_specs=[_rows(tm, hw)] * 3,
        out_shape=[jax.ShapeDtypeStruct((T, hw), F32), jax.ShapeDtypeStruct((T, hw), MXU_DTYPE),
                   jax.ShapeDtypeStruct((T, hw), F32)],
        compiler_params=_params(("parallel",)),
    )(o3, o3, o3, l3, l3, l3)


def _attn_bwd(q, kv, do, o, lse, bias, L, hpg, *, name):
    T = q.shape[0]
    nb_ = T // L
    HP = hpg // 2
    W3 = 3 * hpg * HEAD_DIM
    mmax = L

    def group_body(dil, q_ref, k_ref, v_ref, do_ref, o_ref, l_ref, b_ref, dq_ref, dk_ref, dv_ref, ds_ref,
                   dqs, dks, dvs):
        M, NB = _attn_blocks(dil, L)
        for r in range(dil):
            rows = _row_sel(r, M, dil)
            qr = (q_ref[rows, :] * 0.125).astype(MXU_DTYPE)
            kr = k_ref[rows, :].astype(MXU_DTYPE)
            vr = v_ref[rows, :].astype(MXU_DTYPE)
            dor = do_ref[rows, :]
            orr = o_ref[rows, :]
            lr = l_ref[rows, :]
            dks[0:M, :] = jnp.zeros((M, 2 * HEAD_DIM), F32)
            dvs[0:M, :] = jnp.zeros((M, 2 * HEAD_DIM), F32)
            for n in range(NB):
                qs = slice(n * BAND, (n + 1) * BAND)
                ks = slice(0, BAND) if n == 0 else slice((n - 1) * BAND, (n + 1) * BAND)
                for hh in range(2):
                    ln = slice(hh * HEAD_DIM, (hh + 1) * HEAD_DIM)
                    bb = b_ref[hh, :, BAND:] if n == 0 else b_ref[hh]
                    qb, kb, vb = qr[qs, ln], kr[ks, ln], vr[ks, ln]
                    dob = dor[qs, ln]
                    s = lax.dot_general(qb, kb, (((1,), (1,)), ((), ())), preferred_element_type=F32) + bb
                    p = jnp.exp(s - lr[qs, hh * HEAD_DIM:hh * HEAD_DIM + 1])
                    dobm = dob.astype(MXU_DTYPE)
                    dp = lax.dot_general(dobm, vb, (((1,), (1,)), ((), ())), preferred_element_type=F32)
                    delta = jnp.sum(dob * orr[qs, ln], axis=-1, keepdims=True)
                    ds = p * (dp - delta)
                    if n == 0:
                        ds_ref[hh, :, BAND:] += ds
                    else:
                        ds_ref[hh] += ds
                    dsm = ds.astype(MXU_DTYPE)
                    dqs[qs, ln] = jnp.dot(dsm, kb, preferred_element_type=F32) * 0.125
                    dks[ks, ln] += lax.dot_general(dsm, qb, (((0,), (0,)), ((), ())), preferred_element_type=F32)
                    dvs[ks, ln] += lax.dot_general(p.astype(MXU_DTYPE), dobm, (((0,), (0,)), ((), ())),
                                                   preferred_element_type=F32)
            dq_ref[rows, :] = dqs[0:M, :]
            dk_ref[rows, :] = dks[0:M, :]
            dv_ref[rows, :] = dvs[0:M, :]

    def body(q_ref, k_ref, v_ref, do_ref, o_ref, l_ref, b_ref, dq_ref, dk_ref, dv_ref, ds_ref, dqs, dks, dvs):
        g = pl.program_id(0)

        @pl.when(pl.program_id(2) == 0)
        def _():
            ds_ref[...] = jnp.zeros_like(ds_ref)

        for gi, dil in enumerate(DILATIONS):
            pl.when(g == gi)(functools.partial(group_body, dil, q_ref, k_ref, v_ref, do_ref, o_ref, l_ref, b_ref,
                                               dq_ref, dk_ref, dv_ref, ds_ref, dqs, dks, dvs))

    blk = (L, 2 * HEAD_DIM)
    gcol = lambda g, h, b: (b, g * HP + h)
    hcol = lambda g, h, b: (b, h)
    return _pallas(
        body, name=name, grid=(3, HP, nb_),
        in_specs=[pl.BlockSpec(blk, gcol), pl.BlockSpec(blk, gcol),
                  pl.BlockSpec(blk, lambda g, h, b: (b, 3 * HP + g * HP + h)),
                  pl.BlockSpec(blk, hcol), pl.BlockSpec(blk, hcol), pl.BlockSpec(blk, hcol),
                  pl.BlockSpec((2, BAND, 2 * BAND), lambda g, h, b: (g * HP + h, 0, 0))],
        out_specs=[pl.BlockSpec(blk, gcol), pl.BlockSpec(blk, gcol), pl.BlockSpec(blk, gcol),
                   pl.BlockSpec((2, BAND, 2 * BAND), lambda g, h, b: (g * HP + h, 0, 0))],
        out_shape=[jax.ShapeDtypeStruct((T, W3), F32), jax.ShapeDtypeStruct((T, W3), F32),
                   jax.ShapeDtypeStruct((T, W3), F32), jax.ShapeDtypeStruct((3 * hpg, BAND, 2 * BAND), F32)],
        scratch_shapes=[pltpu.VMEM((mmax, 2 * HEAD_DIM), F32)] * 3,
        compiler_params=_params(("arbitrary", "arbitrary", "arbitrary")),
    )(q, kv, kv, do, o, lse, bias)


def _bias_grad(ds_sum, hpg, *, name):
    nh = ds_sum.shape[0]
    idx = np.stack([np.where(_band_tables(dil)[1], _band_tables(dil)[0], -1) for dil in DILATIONS]).astype(np.int32)

    def body(ds_ref, idx_ref, o_ref):
        d = ds_ref[...]
        ix = idx_ref[...]
        lane = lax.broadcasted_iota(jnp.int32, (8, 128), 1)
        row = jnp.zeros((8, 128), F32)
        for b in range(REL_BUCKETS):
            row = row + jnp.where(lane == b, jnp.sum(jnp.where(ix == b, d, 0.0)), 0.0)
        o_ref[...] = row

    out = _pallas(
        body, name=name, grid=(nh,),
        in_specs=[pl.BlockSpec((None, BAND, 2 * BAND), lambda h: (h, 0, 0)),
                  pl.BlockSpec((None, BAND, 2 * BAND), lambda h: (h // hpg, 0, 0))],
        out_specs=pl.BlockSpec((None, 8, 128), lambda h: (h, 0, 0)),
        out_shape=jax.ShapeDtypeStruct((nh, 8, 128), F32),
        compiler_params=_params(("parallel",)),
    )(ds_sum, jnp.asarray(idx))
    return out[:, 0, :REL_BUCKETS].T


def _adamw(w, g, m, v, *, name):
    Rw, C = w.shape
    tm = _pick(Rw, (512, 352, 256, 128, 64, 32, 16, 8))

    def body(w_ref, g_ref, m_ref, v_ref, d_ref, nm_ref, nv_ref):
        gg = g_ref[...]
        nm = ADAM_B1 * m_ref[...] + (1.0 - ADAM_B1) * gg
        nv = ADAM_B2 * v_ref[...] + (1.0 - ADAM_B2) * (gg * gg)
        m_hat = nm / (1.0 - ADAM_B1 ** ADAM_STEP)
        v_hat = nv / (1.0 - ADAM_B2 ** ADAM_STEP)
        d_ref[...] = -ADAM_LR * (m_hat / (jnp.sqrt(v_hat) + ADAM_EPS) + ADAM_WD * w_ref[...])
        nm_ref[...] = nm
        nv_ref[...] = nv

    return _pallas(
        body, name=name, grid=(Rw // tm,), in_specs=[_rows(tm, C)] * 4, out_specs=[_rows(tm, C)] * 3,
        out_shape=[jax.ShapeDtypeStruct((Rw, C), F32)] * 3, compiler_params=_params(("parallel",)),
    )(w, g, m, v)


ROW_TILE_ELEMS = 256 * 1024


def _tile_rows(r, c):
    best = 8
    for t in range(8, r + 1, 8):
        if r % t == 0 and t * c <= ROW_TILE_ELEMS:
            best = t
    return best


def _adamw_halves(w, m, v, mine, other, cidx, *, name):
    _, r, c = w.shape
    tm = _tile_rows(r, c)

    def body(c_ref, w_ref, m_ref, v_ref, a_ref, b_ref, g_ref, d_ref, nm_ref, nv_ref):
        gg = jnp.where(pl.program_id(0) == c_ref[0], a_ref[...], b_ref[...])
        nm = ADAM_B1 * m_ref[...] + (1.0 - ADAM_B1) * gg
        nv = ADAM_B2 * v_ref[...] + (1.0 - ADAM_B2) * (gg * gg)
        m_hat = nm / (1.0 - ADAM_B1 ** ADAM_STEP)
        v_hat = nv / (1.0 - ADAM_B2 ** ADAM_STEP)
        g_ref[...] = gg
        d_ref[...] = -ADAM_LR * (m_hat / (jnp.sqrt(v_hat) + ADAM_EPS) + ADAM_WD * w_ref[...])
        nm_ref[...] = nm
        nv_ref[...] = nv

    half = pl.BlockSpec((None, tm, c), lambda h, i, cr: (h, i, 0))
    one = pl.BlockSpec((None, tm, c), lambda h, i, cr: (0, i, 0))
    spec = pltpu.PrefetchScalarGridSpec(num_scalar_prefetch=1, grid=(2, r // tm),
                                        in_specs=[half, half, half, one, one], out_specs=[half] * 4)
    return _pallas(
        body, name=name, grid_spec=spec, out_shape=[jax.ShapeDtypeStruct((2, r, c), F32)] * 4,
        compiler_params=_params(("parallel", "parallel")),
    )(cidx, w, m, v, mine, other)


def _pair_sum(g, theirs, cidx, *, cast, name):
    _, _, r, c = g.shape
    tm = _tile_rows(r, c)

    def body(c_ref, g_ref, t_ref, *outs):
        s = g_ref[...] + t_ref[...]
        outs[0][...] = s
        if cast:
            outs[1][...] = s.astype(BF16)

    blk = (None, None, tm, c)
    first = pl.BlockSpec(blk, lambda p, i, cr: (p, 0, i, 0))
    shapes = [jax.ShapeDtypeStruct((4, 1, r, c), F32)] + ([jax.ShapeDtypeStruct((4, 1, r, c), BF16)] if cast else [])
    spec = pltpu.PrefetchScalarGridSpec(
        num_scalar_prefetch=1, grid=(4, r // tm),
        in_specs=[pl.BlockSpec(blk, lambda p, i, cr: (p, cr[0], i, 0)), first], out_specs=[first] * len(shapes))
    return _pallas(body, name=name, grid_spec=spec, out_shape=shapes,
                   compiler_params=_params(("parallel", "parallel")))(cidx, g, theirs)


def _chip_sum(hf, got, chip_idx, *, name):
    _, _, r, c = hf.shape
    tm = _tile_rows(r, c)

    def body(p_ref, h_ref, r_ref, o_ref):
        s = h_ref[...]
        for k in range(3):
            s = s + r_ref[k].astype(F32)
        o_ref[...] = s

    spec = pltpu.PrefetchScalarGridSpec(
        num_scalar_prefetch=1, grid=(r // tm,),
        in_specs=[pl.BlockSpec((None, None, tm, c), lambda i, pr: (pr[0], 0, i, 0)),
                  pl.BlockSpec((3, None, tm, c), lambda i, pr: (0, 0, i, 0))],
        out_specs=pl.BlockSpec((None, tm, c), lambda i, pr: (0, i, 0)))
    return _pallas(body, name=name, grid_spec=spec, out_shape=jax.ShapeDtypeStruct((1, r, c), F32),
                   compiler_params=_params(("parallel",)))(chip_idx, hf, got)


def _place():
    x, y, c = lax.axis_index("x"), lax.axis_index("y"), lax.axis_index("c")
    chips = [(1 - x, y), (x, 1 - y), (1 - x, 1 - y)]
    return x, y, c, chips


_ANY = pl.BlockSpec(memory_space=pl.ANY)


def _comm_call(body, ins, out_shapes, n_remote, *, name):
    sems = [pltpu.SemaphoreType.DMA((n,)) for n in n_remote]
    return _pallas(
        body, name=name, in_specs=[_ANY] * len(ins), out_specs=[_ANY] * len(out_shapes), out_shape=out_shapes,
        scratch_shapes=sems, compiler_params=pltpu.CompilerParams(has_side_effects=True),
    )(*ins)


def _rcopy(src, dst, ssem, rsem, dev):
    return pltpu.make_async_remote_copy(src_ref=src, dst_ref=dst, send_sem=ssem, recv_sem=rsem,
                                        device_id=dev, device_id_type=MESH)


def _all_gather(shards, *, name):
    n = len(shards)

    def body(*refs):
        ins, outs = refs[:n], refs[n:2 * n]
        s_ici, r_ici, s_d2d, r_d2d = refs[2 * n:]
        x, y, c, chips = _place()
        me = 2 * x + y
        sib = (x, y, 1 - c)
        sends = []
        for a in range(n):
            for k, (tx, ty) in enumerate(chips):
                cp = _rcopy(ins[a].at[c], outs[a].at[me, c], s_ici.at[3 * a + k], r_ici.at[3 * a + k], (tx, ty, c))
                cp.start()
                sends.append(cp)
        for a in range(n):
            for k, (tx, ty) in enumerate(chips):
                pk = 2 * tx + ty
                _rcopy(ins[a].at[c], outs[a].at[pk, c], s_ici.at[3 * a + k], r_ici.at[3 * a + k], (tx, ty, c)).wait_recv()
                fw = _rcopy(outs[a].at[pk, c], outs[a].at[pk, c], s_d2d.at[3 * a + k], r_d2d.at[3 * a + k], sib)
                fw.start()
                sends.append(fw)
        for a in range(n):
            for k, (tx, ty) in enumerate(chips):
                pk = 2 * tx + ty
                _rcopy(ins[a].at[c], outs[a].at[pk, 1 - c], s_d2d.at[3 * a + k], r_d2d.at[3 * a + k], sib).wait_recv()
        for cp in sends:
            cp.wait_send()

    shapes = [jax.ShapeDtypeStruct((4,) + s.shape, s.dtype) for s in shards]
    return _comm_call(body, shards, shapes, [3 * n] * 4, name=name)


def _gather(shards, chip, *, name):
    outs = _all_gather(shards, name=name)
    return [lax.dynamic_update_slice(o, s[None], (chip, 0, 0, 0)) for o, s in zip(outs, shards)]


def _pair_send(gs, *, name):
    n = len(gs)

    def body(*refs):
        ins, theirs = refs[:n], refs[n:2 * n]
        ssem, rsem = refs[2 * n:]
        x, y, c, _ = _place()
        sib = (x, y, 1 - c)
        cps = []
        for a in range(n):
            cp = _rcopy(ins[a].at[:, pl.ds(1 - c, 1)], theirs[a], ssem.at[a], rsem.at[a], sib)
            cp.start()
            cps.append(cp)
        for cp in cps:
            cp.wait_send()
            cp.wait_recv()

    shapes = [jax.ShapeDtypeStruct((4, 1) + g.shape[2:], g.dtype) for g in gs]
    return _comm_call(body, gs, shapes, [n, n], name=name)


def _chip_exchange(hx, *, name):
    n = len(hx)

    def body(*refs):
        hxr, got = refs[:n], refs[n:2 * n]
        ssem, rsem = refs[2 * n:]
        x, y, c, chips = _place()
        cps = []
        for a in range(n):
            for k, (tx, ty) in enumerate(chips):
                cp = _rcopy(hxr[a].at[2 * tx + ty], got[a].at[k], ssem.at[3 * a + k], rsem.at[3 * a + k], (tx, ty, c))
                cp.start()
                cps.append(cp)
        for cp in cps:
            cp.wait_send()
            cp.wait_recv()

    shapes = [jax.ShapeDtypeStruct((3,) + h.shape[1:], h.dtype) for h in hx]
    return _comm_call(body, hx, shapes, [3 * n, 3 * n], name=name)


def _pair_swap(fs, *, name):
    n = len(fs)

    def body(*refs):
        ins, outs = refs[:n], refs[n:2 * n]
        ssem, rsem = refs[2 * n:]
        x, y, c, _ = _place()
        cps = []
        for a in range(n):
            cp = _rcopy(ins[a], outs[a], ssem.at[a], rsem.at[a], (x, y, 1 - c))
            cp.start()
            cps.append(cp)
        for cp in cps:
            cp.wait_send()
            cp.wait_recv()

    shapes = [jax.ShapeDtypeStruct(f.shape, f.dtype) for f in fs]
    return _comm_call(body, fs, shapes, [n, n], name=name)


def _reduce_scatter(grads, exch_bf16, cidx, chip_idx, tag):
    n = len(grads)
    theirs = _pair_send(grads, name=f"rs_pair_send_{tag}")
    hf, hx = [], []
    for a in range(n):
        res = _pair_sum(grads[a], theirs[a], cidx, cast=exch_bf16[a], name=f"rs_pair_sum_{tag}{a}")
        hf.append(res[0])
        hx.append(res[1] if exch_bf16[a] else res[0])
    got = _chip_exchange(hx, name=f"rs_chip_exchange_{tag}")
    mine = [_chip_sum(hf[a], got[a], chip_idx, name=f"rs_chip_sum_{tag}{a}") for a in range(n)]
    return mine, _pair_swap(mine, name=f"rs_pair_swap_{tag}")


def _interleave(a, B, L):
    return a.reshape(B, L, -1).transpose(1, 0, 2).reshape(B * L, -1)


def _deinterleave(a, B, L):
    return a.reshape(L, B, -1).transpose(1, 0, 2).reshape(B * L, -1)


def _local_step(x, tgt, W, S):
    B, L, D = x.shape
    T = B * L
    G = D // SSM_GROUP
    Pst = SSM_STATE
    hpg = D // HEAD_DIM
    HW = hpg * HEAD_DIM
    ncl = G // GROUPS_PER_CLUSTER
    x2 = x.reshape(T, D)
    tgt2 = tgt.reshape(T, D)

    disc = lambda *p: _s5_discretize(*p)
    (ab_r, ab_i, bb_r, bb_i), disc_vjp = jax.vjp(disc, S["lam_re"], S["lam_im"], S["log_dt"], S["b_re"], S["b_im"])
    wb = jnp.concatenate([_blockdiag(jnp.transpose(bb_r, (0, 2, 1))), _blockdiag(jnp.transpose(bb_i, (0, 2, 1)))],
                         axis=-1).astype(MXU_DTYPE)
    wc = jnp.concatenate([_blockdiag(jnp.transpose(S["c_re"], (0, 2, 1))), _blockdiag(-jnp.transpose(S["c_im"], (0, 2, 1)))],
                         axis=1).astype(MXU_DTYPE)
    cs = GROUPS_PER_CLUSTER * Pst
    slab = lambda ab: jnp.tile(jnp.transpose(ab.reshape(ncl, cs // LANES, LANES), (1, 0, 2)), (1, B, 1))
    a_r, a_i = slab(ab_r), slab(ab_i)
    d_row = S["d"].reshape(1, D)

    xi = _interleave(x2, B, L)
    y, yg, h_r, h_i = _s5_fwd(xi, wb, wc, a_r, a_i, d_row, B, name="s5_fwd")
    z = _mm_nn(yg, W["w_glu"], bias=S["b_glu"].reshape(1, D), name="glu_z")
    gate = _glu_gate(y, z, name="glu_gate")
    mix_i = _mm_nn(gate, W["w_out"], name="s5_out")
    mix = _deinterleave(mix_i, B, L)
    h1, h1b, xh1, rs1 = _ln_fwd(x2, mix, S["ln_gain"][0, 0][None], S["ln_bias"][0, 0][None], name="ln_fwd_0a")

    def ffn_fwd(hb, l):
        hc = _mm_nn(hb, W[f"w_up{l}"], name=f"ffn_up_{l}")
        a = _conv_glu_fwd(hc, S["conv_w"][l], S["conv_b"][l][None], L, name=f"ffn_conv_{l}")
        f = _mm_nn(a, W[f"w_down{l}"], name=f"ffn_down_{l}")
        return hc, a, f

    hc0, a0, f0 = ffn_fwd(h1b, 0)
    h2, h2b, xh2, rs2 = _ln_fwd(h1, f0, S["ln_gain"][0, 1][None], S["ln_bias"][0, 1][None], name="ln_fwd_0b")

    kv = _mm_nn(h2b, W["w_kv"], name="attn_kv")
    q = _mm_nn(h2b, W["w_q"], name="attn_q")
    bias = _attn_bias(S["rel_bias"], hpg)
    o3, l3 = _attn_fwd(q, kv, bias, L, hpg, name="attn_fwd")
    o, ob, lse = _attn_merge(o3, l3, HW, name="attn_merge")
    att = _mm_nn(ob, W["w_ao"], name="attn_out")
    h3, h3b, xh3, rs3 = _ln_fwd(h2, att, S["ln_gain"][1, 0][None], S["ln_bias"][1, 0][None], name="ln_fwd_1a")
    hc1, a1, f1 = ffn_fwd(h3b, 1)
    h4, _, xh4, rs4 = _ln_fwd(h3, f1, S["ln_gain"][1, 1][None], S["ln_bias"][1, 1][None], name="ln_fwd_1b")

    dh4, lrow = _loss_grad(h4, tgt2, name="loss")
    loss = lrow[0, 0]

    GW, GS = {}, {}

    def ffn_bwd(dzb, hb, hc, a, l):
        da = _mm_nt(dzb, W[f"w_down{l}"], name=f"ffn_down_bwd_x_{l}")
        GW[f"w_down{l}"] = _tn(a, dzb, ptotal=1, np_cols=D, name=f"ffn_down_bwd_w_{l}")
        dc, dcw, dcb = _conv_glu_bwd(hc, da, S["conv_w"][l], S["conv_b"][l][None], L, name=f"ffn_conv_bwd_{l}")
        dhc = _conv_bwd_input(dc, S["conv_w"][l], L, name=f"ffn_conv_bwd_x_{l}")
        dh = _mm_nt(dhc, W[f"w_up{l}"], name=f"ffn_up_bwd_x_{l}")
        GW[f"w_up{l}"] = _tn(hb, dhc, ptotal=W[f"w_up{l}"].shape[0], np_cols=W[f"w_up{l}"].shape[2], name=f"ffn_up_bwd_w_{l}")
        return dh, dcw, dcb

    dz4, dz4b, dg4, db4 = _ln_bwd([dh4], [1.0], xh4, rs4, S["ln_gain"][1, 1][None], name="ln_bwd_1b")
    dh3f, dcw1, dcb1 = ffn_bwd(dz4b, h3b, hc1, a1, 1)
    dz3, dz3b, dg3, db3 = _ln_bwd([dz4, dh3f], [DN_ALPHA, 1.0], xh3, rs3, S["ln_gain"][1, 0][None], name="ln_bwd_1a")
    do = _mm_nt(dz3b, W["w_ao"], name="attn_out_bwd_x")
    GW["w_ao"] = _tn(ob, dz3b, ptotal=1, np_cols=D, name="attn_out_bwd_w")
    dq, dk, dv, ds_sum = _attn_bwd(q, kv, do, o, lse, bias, L, hpg, name="attn_bwd")
    GS["rel_bias"] = _bias_grad(ds_sum, hpg, name="attn_bias_grad")
    GW["w_q"] = _tn(h2b, dq, ptotal=W["w_q"].shape[0], np_cols=W["w_q"].shape[2], name="attn_q_bwd_w")
    pkv, npkv = W["w_kv"].shape[0], W["w_kv"].shape[2]
    gkv = _tn(h2b, dk, ptotal=pkv, np_cols=npkv, p0=0, name="attn_k_bwd_w")
    GW["w_kv"] = _tn(h2b, dv, ptotal=pkv, np_cols=npkv, p0=pkv // 2, prev=gkv, name="attn_v_bwd_w")
    dh2q = _mm_nt(dq, W["w_q"], name="attn_q_bwd_x")
    dh2k = _mm_nt(dk, W["w_kv"], p0=0, pn=pkv // 2, name="attn_k_bwd_x")
    dh2v = _mm_nt(dv, W["w_kv"], p0=pkv // 2, pn=pkv // 2, name="attn_v_bwd_x")

    dz2, dz2b, dg2, db2 = _ln_bwd([dz3, dh2q, dh2k, dh2v], [DN_ALPHA, 1.0, 1.0, 1.0], xh2, rs2,
                                  S["ln_gain"][0, 1][None], name="ln_bwd_0b")
    dh1f, dcw0, dcb0 = ffn_bwd(dz2b, h1b, hc0, a0, 0)
    dz1, dz1b, dg1, db1 = _ln_bwd([dz2, dh1f], [DN_ALPHA, 1.0], xh1, rs1, S["ln_gain"][0, 0][None], name="ln_bwd_0a")
    dmix_i = _interleave(dz1b, B, L)
    dgate = _mm_nt(dmix_i, W["w_out"], name="s5_out_bwd_x")
    GW["w_out"] = _tn(gate, dmix_i, ptotal=1, np_cols=D, name="s5_out_bwd_w")
    dzg, dyg1, dbglu = _glu_bwd(y, z, dgate, name="glu_bwd")
    dyg2 = _mm_nt(dzg, W["w_glu"], name="glu_z_bwd_x")
    GW["w_glu"] = _tn(yg, dzg, ptotal=1, np_cols=D, name="glu_z_bwd_w")
    dy = _gelu_bwd(y, dyg1, dyg2, name="gelu_bwd")
    du_i, g_r, g_i, dar, dai, dd = _s5_bwd(dy, xi, h_r, h_i, wb, wc, a_r, a_i, d_row, B, name="s5_bwd")
    dwb_r = _cluster_tn(xi, g_r, ncl, tok_left=True, name="s5_b_grad_re")
    dwb_i = _cluster_tn(xi, g_i, ncl, tok_left=True, name="s5_b_grad_im")
    dwc_r = _cluster_tn(dy, h_r, ncl, tok_left=False, name="s5_c_grad_re")
    dwc_i = _cluster_tn(dy, h_i, ncl, tok_left=False, name="s5_c_grad_im")
    grad_x = _axpy(dz1, _deinterleave(du_i, B, L), DN_ALPHA, name="grad_x")

    dbb_r = jnp.transpose(_unblockdiag(dwb_r, SSM_GROUP, Pst), (0, 2, 1))
    dbb_i = jnp.transpose(_unblockdiag(dwb_i, SSM_GROUP, Pst), (0, 2, 1))
    unslab = lambda da: jnp.transpose(da.reshape(cs // LANES, B, ncl, LANES).sum(1), (1, 0, 2)).reshape(G, Pst)
    dab_r, dab_i = unslab(dar), unslab(dai)
    GS["lam_re"], GS["lam_im"], GS["log_dt"], GS["b_re"], GS["b_im"] = disc_vjp((dab_r, dab_i, dbb_r, dbb_i))
    GS["c_re"] = jnp.transpose(_unblockdiag(dwc_r, Pst, SSM_GROUP), (0, 2, 1))
    GS["c_im"] = -jnp.transpose(_unblockdiag(dwc_i, Pst, SSM_GROUP), (0, 2, 1))
    GS["d"] = dd.reshape(G, SSM_GROUP)
    GS["b_glu"] = dbglu.reshape(D)
    GS["conv_w"] = jnp.stack([dcw0, dcw1])
    GS["conv_b"] = jnp.stack([dcb0[0], dcb1[0]])
    GS["ln_gain"] = jnp.stack([jnp.stack([dg1[0], dg2[0]]), jnp.stack([dg3[0], dg4[0]])])
    GS["ln_bias"] = jnp.stack([jnp.stack([db1[0], db2[0]]), jnp.stack([db3[0], db4[0]])])
    return loss, grad_x.reshape(B, L, D), GW, GS


SMALL_REPLICATED = ("lam_re", "lam_im", "log_dt", "b_re", "b_im", "c_re", "c_im", "d", "rel_bias", "conv_b")
SMALL_SHARDED = ("b_glu", "conv_w", "ln_gain", "ln_bias")
SMALL_ORDER = SMALL_REPLICATED + SMALL_SHARDED


def _pack(arrs, lanes, row_mult):
    flat = jnp.concatenate([a.reshape(-1).astype(F32) for a in arrs])
    rows = -(-flat.shape[0] // lanes)
    rows = -(-rows // row_mult) * row_mult
    return jnp.pad(flat, (0, rows * lanes - flat.shape[0])).reshape(rows, lanes)


def _unpack(packed, shapes):
    flat = packed.reshape(-1)
    out, off = [], 0
    for s in shapes:
        n = int(np.prod(s))
        out.append(flat[off:off + n].reshape(s))
        off += n
    return out


def kernel(x, s5_lam_re, s5_lam_im, s5_log_dt, s5_b_re, s5_b_im, s5_c_re, s5_c_im, s5_d, s5_w_glu, s5_b_glu, s5_w_out, attn_w_kv, attn_w_q, attn_w_out, rel_bias, ffn_w_up, ffn_conv_w, ffn_conv_b, ffn_w_down, ln_gain, ln_bias, loss_target, m_s5_lam_re, m_s5_lam_im, m_s5_log_dt, m_s5_b_re, m_s5_b_im, m_s5_c_re, m_s5_c_im, m_s5_d, m_s5_w_glu, m_s5_b_glu, m_s5_w_out, m_attn_w_kv, m_attn_w_q, m_attn_w_out, m_rel_bias, m_ffn_w_up, m_ffn_conv_w, m_ffn_conv_b, m_ffn_w_down, m_ln_gain, m_ln_bias, v_s5_lam_re, v_s5_lam_im, v_s5_log_dt, v_s5_b_re, v_s5_b_im, v_s5_c_re, v_s5_c_im, v_s5_d, v_s5_w_glu, v_s5_b_glu, v_s5_w_out, v_attn_w_kv, v_attn_w_q, v_attn_w_out, v_rel_bias, v_ffn_w_up, v_ffn_conv_w, v_ffn_conv_b, v_ffn_w_down, v_ln_gain, v_ln_bias):
    names = ["s5_lam_re", "s5_lam_im", "s5_log_dt", "s5_b_re", "s5_b_im", "s5_c_re", "s5_c_im", "s5_d", "s5_w_glu",
             "s5_b_glu", "s5_w_out", "attn_w_kv", "attn_w_q", "attn_w_out", "rel_bias", "ffn_w_up", "ffn_conv_w",
             "ffn_conv_b", "ffn_w_down", "ln_gain", "ln_bias"]
    loc = locals()
    w_in = {n: loc[n] for n in names}
    m_in = {n: loc["m_" + n] for n in names}
    v_in = {n: loc["v_" + n] for n in names}
    chip = 2 * lax.axis_index("x") + lax.axis_index("y")
    core = lax.axis_index("c")
    chip_idx = jnp.reshape(chip, (1,)).astype(jnp.int32)
    cidx = jnp.reshape(core, (1,)).astype(jnp.int32)

    big = [("w_glu", s5_w_glu[0], False), ("w_out", s5_w_out[0], False), ("w_ao", attn_w_out[0], False),
           ("w_kv", attn_w_kv, True), ("w_q", attn_w_q[0], True),
           ("w_up0", ffn_w_up[0], True), ("w_up1", ffn_w_up[1], True),
           ("w_down0", ffn_w_down[0], False), ("w_down1", ffn_w_down[1], False)]
    big_src = {"w_glu": ("s5_w_glu", 0), "w_out": ("s5_w_out", 0), "w_ao": ("attn_w_out", 0), "w_kv": ("attn_w_kv", None),
               "w_q": ("attn_w_q", 0), "w_up0": ("ffn_w_up", 0), "w_up1": ("ffn_w_up", 1),
               "w_down0": ("ffn_w_down", 0), "w_down1": ("ffn_w_down", 1)}

    small_sh = {"b_glu": s5_b_glu[0], "conv_w": ffn_conv_w, "ln_gain": ln_gain, "ln_bias": ln_bias}
    sh_shapes = [small_sh[k].shape for k in SMALL_SHARDED]
    sh_pack = _pack([small_sh[k] for k in SMALL_SHARDED], 128, 16)

    shards = [w.astype(MXU_DTYPE).reshape(2, w.shape[0] // 2, w.shape[1]) for _, w, _ in big]
    shards.append(sh_pack.reshape(2, sh_pack.shape[0] // 2, 128))
    gathered = _gather(shards, chip, name="weights_all_gather")

    W = {}
    for (key, w, colsh), g in zip(big, gathered[:-1]):
        r, c = w.shape
        W[key] = g.reshape(4, r, c) if colsh else g.reshape(1, 4 * r, c)
    parts = [_unpack(gathered[-1][p], sh_shapes) for p in range(4)]
    S = {k: jnp.concatenate([parts[p][i] for p in range(4)], axis=-1) for i, k in enumerate(SMALL_SHARDED)}
    S.update(lam_re=s5_lam_re[0], lam_im=s5_lam_im[0], log_dt=s5_log_dt[0], b_re=s5_b_re[0], b_im=s5_b_im[0],
             c_re=s5_c_re[0], c_im=s5_c_im[0], d=s5_d[0], rel_bias=rel_bias, conv_b=ffn_conv_b)

    loss, grad_x, GW, GS = _local_step(x, loss_target, W, S)
    loss = lax.psum(loss, ("x", "y", "c"))

    gs_shapes = [GS[k].shape for k in SMALL_ORDER]
    gs_pack = _pack([GS[k] for k in SMALL_ORDER], 128, 64)
    rs = gs_pack.shape[0] // 8
    grads = []
    for key, w, colsh in big:
        r, c = w.shape
        grads.append(GW[key].reshape(4, 2, r // 2, c))
    grads.append(gs_pack.reshape(4, 2, rs, 128))
    mine, other = _reduce_scatter(grads, [True] * len(big) + [False], cidx, chip_idx, "g")
    small_halves = jnp.where(core == 0, jnp.concatenate([mine[-1], other[-1]]), jnp.concatenate([other[-1], mine[-1]]))
    small_all = _gather([small_halves], chip, name="small_grads_all_gather")[0]
    gsmall = dict(zip(SMALL_ORDER, _unpack(small_all, gs_shapes)))

    big_res = {}
    for (key, w, colsh), gm, go in zip(big, mine[:-1], other[:-1]):
        r, c = w.shape
        src, l = big_src[key]
        mm = m_in[src] if l is None else m_in[src][l]
        vv = v_in[src] if l is None else v_in[src][l]
        halves = lambda t: t.reshape(2, r // 2, c)
        res4 = _adamw_halves(halves(w), halves(mm), halves(vv), gm, go, cidx, name=f"adamw_{key}")
        big_res[key] = tuple(t.reshape(r, c) for t in res4)

    def big_out(i):
        o = {}
        o["s5_w_glu"] = big_res["w_glu"][i][None]
        o["s5_w_out"] = big_res["w_out"][i][None]
        o["attn_w_out"] = big_res["w_ao"][i][None]
        o["attn_w_kv"] = big_res["w_kv"][i]
        o["attn_w_q"] = big_res["w_q"][i][None]
        o["ffn_w_up"] = jnp.stack([big_res["w_up0"][i], big_res["w_up1"][i]])
        o["ffn_w_down"] = jnp.stack([big_res["w_down0"][i], big_res["w_down1"][i]])
        return o

    small_w = {"lam_re": s5_lam_re, "lam_im": s5_lam_im, "log_dt": s5_log_dt, "b_re": s5_b_re, "b_im": s5_b_im,
               "c_re": s5_c_re, "c_im": s5_c_im, "d": s5_d, "rel_bias": rel_bias, "conv_b": ffn_conv_b,
               "b_glu": s5_b_glu, "conv_w": ffn_conv_w, "ln_gain": ln_gain, "ln_bias": ln_bias}
    small_name = {"lam_re": "s5_lam_re", "lam_im": "s5_lam_im", "log_dt": "s5_log_dt", "b_re": "s5_b_re", "b_im": "s5_b_im",
                  "c_re": "s5_c_re", "c_im": "s5_c_im", "d": "s5_d", "rel_bias": "rel_bias", "conv_b": "ffn_conv_b",
                  "b_glu": "s5_b_glu", "conv_w": "ffn_conv_w", "ln_gain": "ln_gain", "ln_bias": "ln_bias"}
    sg = {}
    for k in SMALL_ORDER:
        shp = small_w[k].shape
        g = gsmall[k]
        if k in SMALL_SHARDED:
            width = shp[-1]
            g = lax.dynamic_slice_in_dim(g, chip * width, width, axis=g.ndim - 1)
        sg[k] = g.reshape(shp)
    sshapes = [small_w[k].shape for k in SMALL_ORDER]
    pw = _pack([small_w[k] for k in SMALL_ORDER], 128, 512)
    pg = _pack([sg[k] for k in SMALL_ORDER], 128, 512)
    pm = _pack([m_in[small_name[k]] for k in SMALL_ORDER], 128, 512)
    pv = _pack([v_in[small_name[k]] for k in SMALL_ORDER], 128, 512)
    sd, snm, snv = _adamw(pw, pg, pm, pv, name="adamw_small")
    sd = dict(zip(SMALL_ORDER, _unpack(sd, sshapes)))
    snm = dict(zip(SMALL_ORDER, _unpack(snm, sshapes)))
    snv = dict(zip(SMALL_ORDER, _unpack(snv, sshapes)))

    res = [{}, {}, {}, {}]
    for i in range(4):
        res[i].update(big_out(i))
    for k in SMALL_ORDER:
        res[0][small_name[k]] = sg[k]
        res[1][small_name[k]] = sd[k]
        res[2][small_name[k]] = snm[k]
        res[3][small_name[k]] = snv[k]
    outs = [loss, grad_x]
    for i in range(4):
        outs += [res[i][n] for n in names]
    return tuple(outs)
```

```python
import functools
import math

import numpy as np
import jax
import jax.numpy as jnp
from jax import lax
from jax.experimental import pallas as pl
from jax.experimental.pallas import tpu as pltpu

F32 = jnp.float32
BF16 = jnp.bfloat16
MXU_DTYPE = jnp.bfloat16
V7X_VMEM_LIMIT_BYTES = 52 << 20
MESH = pl.DeviceIdType.MESH

DEPTH = 2
SSM_GROUP = 16
SSM_STATE = 64
GROUPS_PER_CLUSTER = 16
CLUSTER_W = GROUPS_PER_CLUSTER * SSM_GROUP
HEAD_DIM = 64
DILATIONS = (1, 4, 16)
BAND = 128
NEG_BIG = -1e30
REL_BUCKETS = 32
REL_MAX_DIST = 2048
DN_ALPHA = (2.0 * DEPTH) ** 0.25
LN_EPS = 1e-5
ADAM_LR, ADAM_B1, ADAM_B2, ADAM_EPS, ADAM_WD, ADAM_STEP = 0.001, 0.9, 0.999, 1e-08, 0.01, 10
GELU_K = math.sqrt(2.0 / math.pi)
GELU_C = 0.044715


def _pallas(body, **kw):
    return pl.pallas_call(body, **kw)


def _params(sem=None):
    return pltpu.CompilerParams(dimension_semantics=sem, vmem_limit_bytes=V7X_VMEM_LIMIT_BYTES)


def _pick(n, cands):
    for c in cands:
        if n % c == 0:
            return c
    return n


def _sigmoid(z):
    return 1.0 / (1.0 + jnp.exp(-z))


def _gelu(y):
    return 0.5 * y * (1.0 + jnp.tanh(GELU_K * (y + GELU_C * y * y * y)))


def _gelu_grad(y):
    t = jnp.tanh(GELU_K * (y + GELU_C * y * y * y))
    return 0.5 * (1.0 + t) + 0.5 * y * (1.0 - t * t) * (GELU_K * (1.0 + 3.0 * GELU_C * y * y))


def _mm_nn(a, w, *, l=0, bias=None, out_dtype=F32, name):
    T, K = a.shape
    P, _, _, Np = w.shape
    tm = _pick(T, (1024, 512, 256, 128))
    tn = _pick(Np, (1408, 1024, 768, 512, 384, 256, 128))
    nj = Np // tn

    def body(*refs):
        if bias is None:
            a_ref, w_ref, o_ref = refs
        else:
            a_ref, w_ref, b_ref, o_ref = refs
        acc = jnp.dot(a_ref[...].astype(MXU_DTYPE), w_ref[...].astype(MXU_DTYPE), preferred_element_type=F32)
        if bias is not None:
            acc = acc + b_ref[...]
        o_ref[...] = acc.astype(o_ref.dtype)

    in_specs = [pl.BlockSpec((tm, K), lambda p, j, i: (i, 0)),
                pl.BlockSpec((None, None, K, tn), lambda p, j, i: (p, l, 0, j))]
    args = [a, w]
    if bias is not None:
        in_specs.append(pl.BlockSpec((1, tn), lambda p, j, i: (0, p * nj + j)))
        args.append(bias)
    return _pallas(
        body, name=name, grid=(P, nj, T // tm), in_specs=in_specs,
        out_specs=pl.BlockSpec((tm, tn), lambda p, j, i: (i, p * nj + j)),
        out_shape=jax.ShapeDtypeStruct((T, P * Np), out_dtype),
        compiler_params=_params(("parallel", "parallel", "parallel")),
    )(*args)


def _mm_nt(a, w, *, l=0, p0=0, pn=None, name):
    T = a.shape[0]
    _, _, K, Np = w.shape
    pn = w.shape[0] if pn is None else pn
    tm = _pick(T, (1024, 512, 256, 128) if K <= 1024 else (512, 256, 128))
    tn = _pick(Np, (1536, 1408, 1024, 768, 512, 384, 256, 128))
    nj = Np // tn
    nred = pn * nj

    def body(a_ref, w_ref, o_ref, acc):
        r = pl.program_id(1)

        @pl.when(r == 0)
        def _():
            acc[...] = jnp.zeros_like(acc)

        acc[...] += lax.dot_general(a_ref[...].astype(MXU_DTYPE), w_ref[...].astype(MXU_DTYPE),
                                    (((1,), (1,)), ((), ())), preferred_element_type=F32)

        @pl.when(r == nred - 1)
        def _():
            o_ref[...] = acc[...]

    return _pallas(
        body, name=name, grid=(T // tm, nred),
        in_specs=[pl.BlockSpec((tm, tn), lambda i, r: (i, r)),
                  pl.BlockSpec((None, None, K, tn), lambda i, r: (p0 + r // nj, l, 0, r % nj))],
        out_specs=pl.BlockSpec((tm, K), lambda i, r: (i, 0)),
        out_shape=jax.ShapeDtypeStruct((T, K), F32),
        scratch_shapes=[pltpu.VMEM((tm, K), F32)],
        compiler_params=_params(("parallel", "arbitrary")),
    )(a, w)


def _tn(a, b, *, ptotal, np_cols, nl=1, l=0, p0=0, prev=None, name):
    T, K = a.shape
    Np = np_cols
    pn = b.shape[1] // Np
    tt = _pick(T, (1024, 512, 256, 128))
    tk = _pick(K, (1408, 1024, 512, 256, 128))
    tn = _pick(Np, (1408, 768, 512, 256, 128))
    if tk * tn > 1408 * 1024:
        tn = _pick(Np, (512, 256, 128))
    nj = Np // tn
    nt = T // tt

    def body(*refs):
        a_ref, b_ref = refs[0], refs[1]
        o_ref, acc = refs[-2], refs[-1]
        t = pl.program_id(3)

        @pl.when(t == 0)
        def _():
            acc[...] = jnp.zeros_like(acc)

        acc[...] += lax.dot_general(a_ref[...].astype(MXU_DTYPE), b_ref[...].astype(MXU_DTYPE),
                                    (((0,), (0,)), ((), ())), preferred_element_type=F32)

        @pl.when(t == nt - 1)
        def _():
            o_ref[...] = acc[...]

    in_specs = [pl.BlockSpec((tt, tk), lambda kb, p, j, t: (t, kb)),
                pl.BlockSpec((tt, tn), lambda kb, p, j, t: (t, p * nj + j))]
    args = [a, b]
    aliases = {}
    if prev is not None:
        in_specs.append(pl.BlockSpec(memory_space=pl.ANY))
        args.append(prev)
        aliases = {2: 0}
    return _pallas(
        body, name=name, grid=(K // tk, pn, nj, nt), in_specs=in_specs,
        out_specs=pl.BlockSpec((None, None, tk, tn), lambda kb, p, j, t: (p0 + p, l, kb, j)),
        out_shape=jax.ShapeDtypeStruct((ptotal, nl, K, Np), F32),
        scratch_shapes=[pltpu.VMEM((tk, tn), F32)],
        input_output_aliases=aliases,
        compiler_params=_params(("parallel", "parallel", "parallel", "arbitrary")),
    )(*args)


def _rows(tm, f):
    return pl.BlockSpec((tm, f), lambda i: (i, 0))


def _whole(shape):
    nd = len(shape)
    return pl.BlockSpec(shape, lambda i: (0,) * nd)


def _ln_fwd(xres, f, gain, bias, *, name):
    T, D = xres.shape
    tm = _pick(T, (256, 128))

    def body(x_ref, f_ref, g_ref, b_ref, y_ref, yb_ref, xh_ref, rs_ref):
        z = DN_ALPHA * x_ref[...] + f_ref[...]
        mu = jnp.mean(z, axis=-1, keepdims=True)
        zc = z - mu
        var = jnp.mean(zc * zc, axis=-1, keepdims=True)
        rstd = lax.rsqrt(var + LN_EPS)
        xh = zc * rstd
        y = xh * g_ref[...] + b_ref[...]
        y_ref[...] = y
        yb_ref[...] = y.astype(yb_ref.dtype)
        xh_ref[...] = xh
        rs_ref[...] = rstd

    return _pallas(
        body, name=name, grid=(T // tm,),
        in_specs=[_rows(tm, D), _rows(tm, D), _whole((1, D)), _whole((1, D))],
        out_specs=[_rows(tm, D), _rows(tm, D), _rows(tm, D), _rows(tm, 1)],
        out_shape=[jax.ShapeDtypeStruct((T, D), F32), jax.ShapeDtypeStruct((T, D), MXU_DTYPE),
                   jax.ShapeDtypeStruct((T, D), F32), jax.ShapeDtypeStruct((T, 1), F32)],
        compiler_params=_params(("parallel",)),
    )(xres, f, gain, bias)


def _ln_bwd(addends, coefs, xhat, rstd, gain, *, name):
    T, D = xhat.shape
    tm = _pick(T, (256, 128))
    n = len(addends)

    def body(*refs):
        adds = refs[:n]
        xh_ref, rs_ref, g_ref, dz_ref, dzb_ref, dg_ref, db_ref = refs[n:]
        dy = coefs[0] * adds[0][...]
        for c, r in zip(coefs[1:], adds[1:]):
            dy = dy + c * r[...]
        xh = xh_ref[...]
        dxh = dy * g_ref[...]
        m1 = jnp.mean(dxh, axis=-1, keepdims=True)
        m2 = jnp.mean(dxh * xh, axis=-1, keepdims=True)
        dz = rs_ref[...] * (dxh - m1 - xh * m2)
        dz_ref[...] = dz
        dzb_ref[...] = dz.astype(dzb_ref.dtype)

        @pl.when(pl.program_id(0) == 0)
        def _():
            dg_ref[...] = jnp.zeros_like(dg_ref)
            db_ref[...] = jnp.zeros_like(db_ref)

        dg_ref[...] += jnp.sum(dy * xh, axis=0, keepdims=True)
        db_ref[...] += jnp.sum(dy, axis=0, keepdims=True)

    return _pallas(
        body, name=name, grid=(T // tm,),
        in_specs=[_rows(tm, D)] * n + [_rows(tm, D), _rows(tm, 1), _whole((1, D))],
        out_specs=[_rows(tm, D), _rows(tm, D), _whole((1, D)), _whole((1, D))],
        out_shape=[jax.ShapeDtypeStruct((T, D), F32), jax.ShapeDtypeStruct((T, D), MXU_DTYPE),
                   jax.ShapeDtypeStruct((1, D), F32), jax.ShapeDtypeStruct((1, D), F32)],
        compiler_params=_params(("arbitrary",)),
    )(*addends, xhat, rstd, gain)


def _loss_grad(y, tgt, *, name):
    T, D = y.shape
    tm = _pick(T, (256, 128))

    def body(y_ref, t_ref, dy_ref, l_ref):
        e = y_ref[...] - t_ref[...]
        dy_ref[...] = e * (1.0 / D)

        @pl.when(pl.program_id(0) == 0)
        def _():
            l_ref[...] = jnp.zeros_like(l_ref)

        l_ref[...] += jnp.zeros_like(l_ref) + jnp.sum(e * e) * (0.5 / D)

    return _pallas(
        body, name=name, grid=(T // tm,),
        in_specs=[_rows(tm, D), _rows(tm, D)],
        out_specs=[_rows(tm, D), _whole((1, 128))],
        out_shape=[jax.ShapeDtypeStruct((T, D), F32), jax.ShapeDtypeStruct((1, 128), F32)],
        compiler_params=_params(("arbitrary",)),
    )(y, tgt)


def _axpy(a, b, ca, *, name):
    T, D = a.shape
    tm = _pick(T, (256, 128))

    def body(a_ref, b_ref, o_ref):
        o_ref[...] = ca * a_ref[...] + b_ref[...]

    return _pallas(
        body, name=name, grid=(T // tm,), in_specs=[_rows(tm, D), _rows(tm, D)], out_specs=_rows(tm, D),
        out_shape=jax.ShapeDtypeStruct((T, D), F32), compiler_params=_params(("parallel",)),
    )(a, b)


def _glu_gate(y, z, *, name):
    T, D = y.shape
    tm = _pick(T, (256, 128))

    def body(y_ref, z_ref, g_ref):
        g_ref[...] = (_gelu(y_ref[...]) * _sigmoid(z_ref[...])).astype(g_ref.dtype)

    return _pallas(
        body, name=name, grid=(T // tm,), in_specs=[_rows(tm, D), _rows(tm, D)], out_specs=_rows(tm, D),
        out_shape=jax.ShapeDtypeStruct((T, D), MXU_DTYPE), compiler_params=_params(("parallel",)),
    )(y, z)


def _glu_bwd(y, z, dg, *, name):
    T, D = y.shape
    tm = _pick(T, (256, 128))

    def body(y_ref, z_ref, dg_ref, dzb_ref, dyg_ref, db_ref):
        s = _sigmoid(z_ref[...])
        dg = dg_ref[...]
        dz = dg * _gelu(y_ref[...]) * s * (1.0 - s)
        dzb_ref[...] = dz.astype(dzb_ref.dtype)
        dyg_ref[...] = dg * s

        @pl.when(pl.program_id(0) == 0)
        def _():
            db_ref[...] = jnp.zeros_like(db_ref)

        db_ref[...] += jnp.sum(dz, axis=0, keepdims=True)

    return _pallas(
        body, name=name, grid=(T // tm,), in_specs=[_rows(tm, D)] * 3,
        out_specs=[_rows(tm, D), _rows(tm, D), _whole((1, D))],
        out_shape=[jax.ShapeDtypeStruct((T, D), MXU_DTYPE), jax.ShapeDtypeStruct((T, D), F32),
                   jax.ShapeDtypeStruct((1, D), F32)],
        compiler_params=_params(("arbitrary",)),
    )(y, z, dg)


def _gelu_bwd(y, d1, d2, *, name):
    T, D = y.shape
    tm = _pick(T, (256, 128))

    def body(y_ref, a_ref, b_ref, o_ref):
        o_ref[...] = (a_ref[...] + b_ref[...]) * _gelu_grad(y_ref[...])

    return _pallas(
        body, name=name, grid=(T // tm,), in_specs=[_rows(tm, D)] * 3, out_specs=_rows(tm, D),
        out_shape=jax.ShapeDtypeStruct((T, D), F32), compiler_params=_params(("parallel",)),
    )(y, d1, d2)


CONV_ROWS = 128


def _shift_back(x, edge, at_start, tm):
    rows = lax.broadcasted_iota(jnp.int32, x.shape, 0)
    keep = jnp.where(at_start, 0.0, 1.0)
    e7 = edge[7:8, :] * keep
    e6 = edge[6:7, :] * keep
    r1 = pltpu.roll(x, 1, 0)
    r2 = pltpu.roll(x, 2, 0)
    x1 = jnp.where(rows == 0, e7, r1)
    x2 = jnp.where(rows == 0, e6, jnp.where(rows == 1, e7, r2))
    return x1, x2


def _conv_specs(T, F2, tm):
    return [_rows(tm, F2),
            pl.BlockSpec((8, F2), lambda i: (jnp.maximum(i * (tm // 8) - 1, 0), 0))]


def _conv_glu_fwd(hc, conv_w, conv_b, L, *, name):
    T, F2 = hc.shape
    F = F2 // 2
    tm = CONV_ROWS

    def body(x_ref, e_ref, w_ref, b_ref, a_ref):
        at_start = (pl.program_id(0) * tm) % L == 0
        x = x_ref[...]
        x1, x2 = _shift_back(x, e_ref[...], at_start, tm)
        c = b_ref[...] + w_ref[0:1, :] * x + w_ref[1:2, :] * x1 + w_ref[2:3, :] * x2
        val, gate = c[:, :F], c[:, F:]
        a_ref[...] = (gate * _sigmoid(gate) * val).astype(a_ref.dtype)

    return _pallas(
        body, name=name, grid=(T // tm,),
        in_specs=_conv_specs(T, F2, tm) + [_whole((3, F2)), _whole((1, F2))],
        out_specs=_rows(tm, F),
        out_shape=jax.ShapeDtypeStruct((T, F), MXU_DTYPE), compiler_params=_params(("parallel",)),
    )(hc, hc, conv_w, conv_b)


def _conv_glu_bwd(hc, da, conv_w, conv_b, L, *, name):
    T, F2 = hc.shape
    F = F2 // 2
    tm = CONV_ROWS

    def body(x_ref, e_ref, da_ref, w_ref, b_ref, dc_ref, dw_ref, db_ref):
        at_start = (pl.program_id(0) * tm) % L == 0
        x = x_ref[...]
        x1, x2 = _shift_back(x, e_ref[...], at_start, tm)
        c = b_ref[...] + w_ref[0:1, :] * x + w_ref[1:2, :] * x1 + w_ref[2:3, :] * x2
        val, gate = c[:, :F], c[:, F:]
        s = _sigmoid(gate)
        da = da_ref[...]
        dval = da * (gate * s)
        dgate = da * val * (s * (1.0 + gate * (1.0 - s)))
        dc = jnp.concatenate([dval, dgate], axis=-1)
        dc_ref[...] = dc.astype(dc_ref.dtype)

        @pl.when(pl.program_id(0) == 0)
        def _():
            dw_ref[...] = jnp.zeros_like(dw_ref)
            db_ref[...] = jnp.zeros_like(db_ref)

        dw_ref[0:1, :] += jnp.sum(dc * x, axis=0, keepdims=True)
        dw_ref[1:2, :] += jnp.sum(dc * x1, axis=0, keepdims=True)
        dw_ref[2:3, :] += jnp.sum(dc * x2, axis=0, keepdims=True)
        db_ref[...] += jnp.sum(dc, axis=0, keepdims=True)

    return _pallas(
        body, name=name, grid=(T // tm,),
        in_specs=_conv_specs(T, F2, tm) + [_rows(tm, F), _whole((3, F2)), _whole((1, F2))],
        out_specs=[_rows(tm, F2), _whole((3, F2)), _whole((1, F2))],
        out_shape=[jax.ShapeDtypeStruct((T, F2), MXU_DTYPE), jax.ShapeDtypeStruct((3, F2), F32),
                   jax.ShapeDtypeStruct((1, F2), F32)],
        compiler_params=_params(("arbitrary",)),
    )(hc, hc, da, conv_w, conv_b)


def _conv_bwd_input(dc, conv_w, L, *, name):
    T, F2 = dc.shape
    tm = CONV_ROWS
    edge = 16
    last_blk = T // edge - 1

    def body(x_ref, e_ref, w_ref, o_ref):
        at_end = ((pl.program_id(0) + 1) * tm) % L == 0
        x = x_ref[...].astype(F32)
        rows = lax.broadcasted_iota(jnp.int32, x.shape, 0)
        keep = jnp.where(at_end, 0.0, 1.0)
        ev = e_ref[...].astype(F32)
        e0 = ev[0:1, :] * keep
        e1 = ev[1:2, :] * keep
        u1 = pltpu.roll(x, tm - 1, 0)
        u2 = pltpu.roll(x, tm - 2, 0)
        x1 = jnp.where(rows == tm - 1, e0, u1)
        x2 = jnp.where(rows == tm - 1, e1, jnp.where(rows == tm - 2, e0, u2))
        o_ref[...] = (w_ref[0:1, :] * x + w_ref[1:2, :] * x1 + w_ref[2:3, :] * x2).astype(o_ref.dtype)

    return _pallas(
        body, name=name, grid=(T // tm,),
        in_specs=[_rows(tm, F2),
                  pl.BlockSpec((edge, F2), lambda i: (jnp.minimum((i + 1) * (tm // edge), last_blk), 0)),
                  _whole((3, F2))],
        out_specs=_rows(tm, F2),
        out_shape=jax.ShapeDtypeStruct((T, F2), MXU_DTYPE), compiler_params=_params(("parallel",)),
    )(dc, dc, conv_w)


S5_CHUNK = 128
LANES = 128


def _slab_rows(c, n, ncl):
    return pl.ds(c, n) if ncl == 1 else pl.ds(c, n, stride=ncl)


def _slab_put(ref, c, n, ncl, val):
    for s in range(val.shape[1] // LANES):
        ref[s, _slab_rows(c, n, ncl), :] = val[:, s * LANES:(s + 1) * LANES]


def _slab_get(ref, c, n, ncl):
    return jnp.concatenate([ref[s, _slab_rows(c, n, ncl), :] for s in range(ref.shape[0])], axis=-1)


def _slabs(n_slab, rows):
    return pl.BlockSpec((n_slab, rows, LANES), lambda i: (0, i, 0))


def _s5_fwd(x3, wb, wc, a_r, a_i, d_row, *, name):
    B, L, D = x3.shape
    ncl = wb.shape[0]
    cs = wb.shape[2] // 2
    ns = cs // LANES
    R = B * ncl
    nsteps = S5_CHUNK // B
    QR = nsteps * R

    def body(x_ref, wb_ref, wc_ref, ar_ref, ai_ref, d_ref, y_ref, yg_ref, hr_ref, hi_ref, bur, bui, cr, ci):
        @pl.when(pl.program_id(0) == 0)
        def _():
            cr[...] = jnp.zeros_like(cr)
            ci[...] = jnp.zeros_like(ci)

        for b in range(B):
            xb = x_ref[b].astype(MXU_DTYPE)
            for c in range(ncl):
                bu = jnp.dot(xb[:, c * CLUSTER_W:(c + 1) * CLUSTER_W], wb_ref[c], preferred_element_type=F32)
                _slab_put(bur, b * ncl + c, nsteps, R, bu[:, :cs])
                _slab_put(bui, b * ncl + c, nsteps, R, bu[:, cs:])
        ar = ar_ref[...]
        ai = ai_ref[...]

        def step(k, carry):
            hr, hi = carry
            sl = pl.ds(pl.multiple_of(k * R, R), R)
            nr = ar * hr - ai * hi + bur[:, sl, :]
            ni = ar * hi + ai * hr + bui[:, sl, :]
            hr_ref[:, sl, :] = nr
            hi_ref[:, sl, :] = ni
            return nr, ni

        hr, hi = lax.fori_loop(0, nsteps, step, (cr[...], ci[...]), unroll=4)
        cr[...] = hr
        ci[...] = hi
        for b in range(B):
            parts = []
            for c in range(ncl):
                hrc = _slab_get(hr_ref, b * ncl + c, nsteps, R).astype(MXU_DTYPE)
                hic = _slab_get(hi_ref, b * ncl + c, nsteps, R).astype(MXU_DTYPE)
                parts.append(jnp.dot(hrc, wc_ref[c, :cs, :], preferred_element_type=F32)
                             + jnp.dot(hic, wc_ref[c, cs:, :], preferred_element_type=F32))
            y = d_ref[...] * x_ref[b] + (parts[0] if ncl == 1 else jnp.concatenate(parts, axis=-1))
            y_ref[b] = y
            yg_ref[b] = _gelu(y).astype(yg_ref.dtype)

    tok = pl.BlockSpec((B, nsteps, D), lambda i: (0, i, 0))
    return _pallas(
        body, name=name, grid=(L // nsteps,),
        in_specs=[tok, _whole(wb.shape), _whole(wc.shape), _whole((ns, R, LANES)), _whole((ns, R, LANES)),
                  _whole((1, D))],
        out_specs=[tok, tok, _slabs(ns, QR), _slabs(ns, QR)],
        out_shape=[jax.ShapeDtypeStruct((B, L, D), F32), jax.ShapeDtypeStruct((B, L, D), MXU_DTYPE),
                   jax.ShapeDtypeStruct((ns, L * R, LANES), F32), jax.ShapeDtypeStruct((ns, L * R, LANES), F32)],
        scratch_shapes=[pltpu.VMEM((ns, QR, LANES), F32), pltpu.VMEM((ns, QR, LANES), F32),
                        pltpu.VMEM((ns, R, LANES), F32), pltpu.VMEM((ns, R, LANES), F32)],
        compiler_params=_params(("arbitrary",)),
    )(x3, wb, wc, a_r, a_i, d_row)


def _s5_bwd(dy3, x3, h_r, h_i, wb, wc, a_r, a_i, d_row, *, name):
    B, L, D = dy3.shape
    ncl = wb.shape[0]
    cs = wb.shape[2] // 2
    ns = cs // LANES
    R = B * ncl
    nsteps = S5_CHUNK // B
    nchunk = L // nsteps
    QR = nsteps * R

    def rev(i):
        return nchunk - 1 - i

    def body(dy_ref, x_ref, hr_ref, hi_ref, pr_ref, pi_ref, wb_ref, wc_ref, ar_ref, ai_ref, d_ref,
             du_ref, gr_ref, gi_ref, dar_ref, dai_ref, dd_ref, dhr, dhi, cr, ci):
        i = pl.program_id(0)

        @pl.when(i == 0)
        def _():
            cr[...] = jnp.zeros_like(cr)
            ci[...] = jnp.zeros_like(ci)
            dar_ref[...] = jnp.zeros_like(dar_ref)
            dai_ref[...] = jnp.zeros_like(dai_ref)
            dd_ref[...] = jnp.zeros_like(dd_ref)

        for b in range(B):
            dyb = dy_ref[b].astype(MXU_DTYPE)
            for c in range(ncl):
                dh = lax.dot_general(dyb[:, c * CLUSTER_W:(c + 1) * CLUSTER_W], wc_ref[c],
                                     (((1,), (1,)), ((), ())), preferred_element_type=F32)
                _slab_put(dhr, b * ncl + c, nsteps, R, dh[:, :cs])
                _slab_put(dhi, b * ncl + c, nsteps, R, dh[:, cs:])
        ar = ar_ref[...]
        ai = ai_ref[...]

        def step(j, carry):
            gr, gi, sar, sai = carry
            k = nsteps - 1 - j
            sl = pl.ds(pl.multiple_of(k * R, R), R)
            ngr = dhr[:, sl, :] + ar * gr + ai * gi
            ngi = dhi[:, sl, :] - ai * gr + ar * gi
            gr_ref[:, sl, :] = ngr
            gi_ref[:, sl, :] = ngi
            pv = pl.ds(pl.multiple_of((k - 1) * R, R), R)
            hpr = hr_ref[:, pv, :]
            hpi = hi_ref[:, pv, :]
            return ngr, ngi, sar + ngr * hpr + ngi * hpi, sai - ngr * hpi + ngi * hpr

        gr, gi, sar, sai = lax.fori_loop(0, nsteps - 1, step, (cr[...], ci[...], dar_ref[...], dai_ref[...]), unroll=4)
        sl0 = pl.ds(0, R)
        ngr = dhr[:, sl0, :] + ar * gr + ai * gi
        ngi = dhi[:, sl0, :] - ai * gr + ar * gi
        gr_ref[:, sl0, :] = ngr
        gi_ref[:, sl0, :] = ngi
        keep = jnp.where(i == nchunk - 1, 0.0, 1.0)
        hpr = pr_ref[:, 8 - R:8, :] * keep
        hpi = pi_ref[:, 8 - R:8, :] * keep
        dar_ref[...] = sar + ngr * hpr + ngi * hpi
        dai_ref[...] = sai - ngr * hpi + ngi * hpr
        cr[...] = ngr
        ci[...] = ngi
        dd = dd_ref[...]
        for b in range(B):
            parts = []
            for c in range(ncl):
                grc = _slab_get(gr_ref, b * ncl + c, nsteps, R).astype(MXU_DTYPE)
                gic = _slab_get(gi_ref, b * ncl + c, nsteps, R).astype(MXU_DTYPE)
                parts.append(
                    lax.dot_general(grc, wb_ref[c, :, :cs], (((1,), (1,)), ((), ())), preferred_element_type=F32)
                    + lax.dot_general(gic, wb_ref[c, :, cs:], (((1,), (1,)), ((), ())), preferred_element_type=F32))
            dyv = dy_ref[b]
            du_ref[b] = d_ref[...] * dyv + (parts[0] if ncl == 1 else jnp.concatenate(parts, axis=-1))
            dd = dd + jnp.sum(dyv * x_ref[b], axis=0, keepdims=True)
        dd_ref[...] = dd

    tok = pl.BlockSpec((B, nsteps, D), lambda i: (0, rev(i), 0))
    st = pl.BlockSpec((ns, QR, LANES), lambda i: (0, rev(i), 0))
    before = pl.BlockSpec((ns, 8, LANES), lambda i: (0, jnp.maximum(rev(i) * (QR // 8) - 1, 0), 0))
    acc = _whole((ns, R, LANES))
    return _pallas(
        body, name=name, grid=(nchunk,),
        in_specs=[tok, tok, st, st, before, before, _whole(wb.shape), _whole(wc.shape), acc, acc, _whole((1, D))],
        out_specs=[tok, st, st, acc, acc, _whole((1, D))],
        out_shape=[jax.ShapeDtypeStruct((B, L, D), F32),
                   jax.ShapeDtypeStruct((ns, L * R, LANES), F32), jax.ShapeDtypeStruct((ns, L * R, LANES), F32),
                   jax.ShapeDtypeStruct((ns, R, LANES), F32), jax.ShapeDtypeStruct((ns, R, LANES), F32),
                   jax.ShapeDtypeStruct((1, D), F32)],
        scratch_shapes=[pltpu.VMEM((ns, QR, LANES), F32)] * 2 + [pltpu.VMEM((ns, R, LANES), F32)] * 2,
        compiler_params=_params(("arbitrary",)),
    )(dy3, x3, h_r, h_i, h_r, h_i, wb, wc, a_r, a_i, d_row)


def _cluster_tn(tok3, st, ncl, *, tok_left, name):
    B, L, Dm = tok3.shape
    ns = st.shape[0]
    cs = ns * LANES
    R = B * ncl
    tt = _pick(L, (256, 128))
    nt = L // tt
    oshape = (ncl, CLUSTER_W, cs) if tok_left else (ncl, cs, CLUSTER_W)

    def body(tok_ref, st_ref, o_ref, acc):
        t = pl.program_id(0)

        @pl.when(t == 0)
        def _():
            acc[...] = jnp.zeros_like(acc)

        for b in range(B):
            tk = tok_ref[b].astype(MXU_DTYPE)
            for c in range(ncl):
                tc = tk[:, c * CLUSTER_W:(c + 1) * CLUSTER_W]
                sc = _slab_get(st_ref, b * ncl + c, tt, R).astype(MXU_DTYPE)
                lhs, rhs = (tc, sc) if tok_left else (sc, tc)
                acc[c] += lax.dot_general(lhs, rhs, (((0,), (0,)), ((), ())), preferred_element_type=F32)

        @pl.when(t == nt - 1)
        def _():
            o_ref[...] = acc[...]

    return _pallas(
        body, name=name, grid=(nt,),
        in_specs=[pl.BlockSpec((B, tt, Dm), lambda i: (0, i, 0)), _slabs(ns, tt * R)],
        out_specs=_whole(oshape),
        out_shape=jax.ShapeDtypeStruct(oshape, F32),
        scratch_shapes=[pltpu.VMEM(oshape, F32)],
        compiler_params=_params(("arbitrary",)),
    )(tok3, st)


def _s5_discretize(lam_re, lam_im, log_dt, b_re, b_im):
    dt = jnp.exp(log_dt)[:, None]
    mag = jnp.exp(lam_re * dt)
    ab_r, ab_i = mag * jnp.cos(lam_im * dt), mag * jnp.sin(lam_im * dt)
    den = lam_re * lam_re + lam_im * lam_im
    nr = ab_r - 1.0
    co_r = (nr * lam_re + ab_i * lam_im) / den
    co_i = (ab_i * lam_re - nr * lam_im) / den
    bb_r = co_r[..., None] * b_re - co_i[..., None] * b_im
    bb_i = co_r[..., None] * b_im + co_i[..., None] * b_re
    return ab_r, ab_i, bb_r, bb_i


def _blockdiag(m):
    G, r, k = m.shape
    ncl = G // GROUPS_PER_CLUSTER
    m4 = m.reshape(ncl, GROUPS_PER_CLUSTER, r, k)
    eye = jnp.eye(GROUPS_PER_CLUSTER, dtype=m.dtype)
    return jnp.einsum('cgrk,gh->cgrhk', m4, eye).reshape(ncl, GROUPS_PER_CLUSTER * r, GROUPS_PER_CLUSTER * k)


def _unblockdiag(m, r, k):
    ncl = m.shape[0]
    m5 = m.reshape(ncl, GROUPS_PER_CLUSTER, r, GROUPS_PER_CLUSTER, k)
    eye = jnp.eye(GROUPS_PER_CLUSTER, dtype=m.dtype)
    return jnp.einsum('cgrhk,gh->cgrk', m5, eye).reshape(ncl * GROUPS_PER_CLUSTER, r, k)


def _t5_bucket(dist):
    exact = REL_BUCKETS // 2
    d = np.maximum(dist, 1).astype(np.float32)
    large = exact + (np.log(d / exact) / math.log(REL_MAX_DIST / exact) * (REL_BUCKETS - exact)).astype(np.int64)
    large = np.minimum(large, REL_BUCKETS - 1)
    return np.where(dist < exact, dist, large).astype(np.int32)


def _band_tables(dil):
    steps = np.arange(BAND)[:, None] + BAND - np.arange(2 * BAND)[None, :]
    bucket = _t5_bucket(np.maximum(steps, 0) * dil)
    in_band = (steps >= 0) & (steps <= BAND)
    return bucket, in_band


def _attn_bias(rel_bias, hpg):
    out = []
    for g, dil in enumerate(DILATIONS):
        bucket, in_band = _band_tables(dil)
        cols = rel_bias[:, g * hpg:(g + 1) * hpg].astype(F32)
        onehot = jnp.asarray((bucket.reshape(-1, 1) == np.arange(REL_BUCKETS)[None, :]).astype(np.float32))
        bias = jnp.dot(onehot, cols, precision=lax.Precision.HIGHEST).T.reshape(hpg, BAND, 2 * BAND)
        out.append(jnp.where(jnp.asarray(in_band)[None], bias, NEG_BIG))
    return jnp.concatenate(out, axis=0)


def _attn_blocks(dil, L):
    M = L // dil
    return M, M // BAND


def _row_sel(r, M, dil):
    return pl.ds(r, M) if dil == 1 else pl.ds(r, M, stride=dil)


def _attn_fwd(q, kv, bias, L, hpg, *, name):
    T = q.shape[0]
    nb_ = T // L
    HP = hpg // 2
    W3 = 3 * hpg * HEAD_DIM
    mmax = L

    def group_body(dil, q_ref, k_ref, v_ref, b_ref, o_ref, l_ref, os, ls):
        M, NB = _attn_blocks(dil, L)
        for r in range(dil):
            rows = _row_sel(r, M, dil)
            qr = (q_ref[rows, :] * 0.125).astype(MXU_DTYPE)
            kr = k_ref[rows, :].astype(MXU_DTYPE)
            vr = v_ref[rows, :].astype(MXU_DTYPE)
            ones = jnp.ones((M, HEAD_DIM), MXU_DTYPE)
            va = [jnp.concatenate([vr[:, hh * HEAD_DIM:(hh + 1) * HEAD_DIM], ones], axis=-1) for hh in range(2)]
            for n in range(NB):
                qs = slice(n * BAND, (n + 1) * BAND)
                ks = slice(0, BAND) if n == 0 else slice((n - 1) * BAND, (n + 1) * BAND)
                for hh in range(2):
                    ln = slice(hh * HEAD_DIM, (hh + 1) * HEAD_DIM)
                    bb = b_ref[hh, :, BAND:] if n == 0 else b_ref[hh]
                    s = lax.dot_general(qr[qs, ln], kr[ks, ln], (((1,), (1,)), ((), ())),
                                        preferred_element_type=F32) + bb
                    m = jnp.max(s, axis=-1, keepdims=True)
                    p = jnp.exp(s - m)
                    pv = jnp.dot(p.astype(MXU_DTYPE), va[hh][ks, :], preferred_element_type=F32)
                    l = pv[:, HEAD_DIM:]
                    os[qs, ln] = pv[:, :HEAD_DIM] / l
                    ls[qs, ln] = m + jnp.log(l)
            o_ref[rows, :] = os[0:M, :]
            l_ref[rows, :] = ls[0:M, :]

    def body(q_ref, k_ref, v_ref, b_ref, o_ref, l_ref, os, ls):
        g = pl.program_id(0)
        for gi, dil in enumerate(DILATIONS):
            pl.when(g == gi)(functools.partial(group_body, dil, q_ref, k_ref, v_ref, b_ref, o_ref, l_ref, os, ls))

    blk = (L, 2 * HEAD_DIM)
    return _pallas(
        body, name=name, grid=(3, nb_, HP),
        in_specs=[pl.BlockSpec(blk, lambda g, b, h: (b, g * HP + h)),
                  pl.BlockSpec(blk, lambda g, b, h: (b, g * HP + h)),
                  pl.BlockSpec(blk, lambda g, b, h: (b, 3 * HP + g * HP + h)),
                  pl.BlockSpec((2, BAND, 2 * BAND), lambda g, b, h: (g * HP + h, 0, 0))],
        out_specs=[pl.BlockSpec(blk, lambda g, b, h: (b, g * HP + h)),
                   pl.BlockSpec(blk, lambda g, b, h: (b, g * HP + h))],
        out_shape=[jax.ShapeDtypeStruct((T, W3), F32), jax.ShapeDtypeStruct((T, W3), F32)],
        scratch_shapes=[pltpu.VMEM((mmax, 2 * HEAD_DIM), F32), pltpu.VMEM((mmax, 2 * HEAD_DIM), F32)],
        compiler_params=_params(("arbitrary", "arbitrary", "arbitrary")),
    )(q, kv, kv, bias)


def _attn_merge(o3, l3, hw, *, name):
    T = o3.shape[0]
    tm = _pick(T, (256, 128))

    def body(o0, o1, o2, l0, l1, l2, o_ref, ob_ref, lse_ref):
        a0, a1, a2 = l0[...], l1[...], l2[...]
        m = jnp.maximum(jnp.maximum(a0, a1), a2)
        e0, e1, e2 = jnp.exp(a0 - m), jnp.exp(a1 - m), jnp.exp(a2 - m)
        z = e0 + e1 + e2
        o = (e0 * o0[...] + e1 * o1[...] + e2 * o2[...]) / z
        o_ref[...] = o
        ob_ref[...] = o.astype(ob_ref.dtype)
        lse_ref[...] = m + jnp.log(z)

    def col(g):
        return pl.BlockSpec((tm, hw), lambda i: (i, g))

    return _pallas(
        body, name=name, grid=(T // tm,),
        in_specs=[col(0), col(1), col(2), col(0), col(1), col(2)],
        out_specs=[_rows(tm, hw)] * 3,
        out_shape=[jax.ShapeDtypeStruct((T, hw), F32), jax.ShapeDtypeStruct((T, hw), MXU_DTYPE),
                   jax.ShapeDtypeStruct((T, hw), F32)],
        compiler_params=_params(("parallel",)),
    )(o3, o3, o3, l3, l3, l3)


def _attn_bwd(q, kv, do, o, lse, bias, L, hpg, *, name):
    T = q.shape[0]
    nb_ = T // L
    HP = hpg // 2
    W3 = 3 * hpg * HEAD_DIM
    mmax = L

    def group_body(dil, q_ref, k_ref, v_ref, do_ref, o_ref, l_ref, b_ref, dq_ref, dk_ref, dv_ref, ds_ref,
                   dqs, dks, dvs):
        M, NB = _attn_blocks(dil, L)
        for r in range(dil):
            rows = _row_sel(r, M, dil)
            qr = (q_ref[rows, :] * 0.125).astype(MXU_DTYPE)
            kr = k_ref[rows, :].astype(MXU_DTYPE)
            vr = v_ref[rows, :].astype(MXU_DTYPE)
            dor = do_ref[rows, :]
            orr = o_ref[rows, :]
            lr = l_ref[rows, :]
            dks[0:M, :] = jnp.zeros((M, 2 * HEAD_DIM), F32)
            dvs[0:M, :] = jnp.zeros((M, 2 * HEAD_DIM), F32)
            for n in range(NB):
                qs = slice(n * BAND, (n + 1) * BAND)
                ks = slice(0, BAND) if n == 0 else slice((n - 1) * BAND, (n + 1) * BAND)
                for hh in range(2):
                    ln = slice(hh * HEAD_DIM, (hh + 1) * HEAD_DIM)
                    bb = b_ref[hh, :, BAND:] if n == 0 else b_ref[hh]
                    qb, kb, vb = qr[qs, ln], kr[ks, ln], vr[ks, ln]
                    dob = dor[qs, ln]
                    s = lax.dot_general(qb, kb, (((1,), (1,)), ((), ())), preferred_element_type=F32) + bb
                    p = jnp.exp(s - lr[qs, hh * HEAD_DIM:hh * HEAD_DIM + 1])
                    dobm = dob.astype(MXU_DTYPE)
                    dp = lax.dot_general(dobm, vb, (((1,), (1,)), ((), ())), preferred_element_type=F32)
                    delta = jnp.sum(dob * orr[qs, ln], axis=-1, keepdims=True)
                    ds = p * (dp - delta)
                    if n == 0:
                        ds_ref[hh, :, BAND:] += ds
                    else:
                        ds_ref[hh] += ds
                    dsm = ds.astype(MXU_DTYPE)
                    dqs[qs, ln] = jnp.dot(dsm, kb, preferred_element_type=F32) * 0.125
                    dks[ks, ln] += lax.dot_general(dsm, qb, (((0,), (0,)), ((), ())), preferred_element_type=F32)
                    dvs[ks, ln] += lax.dot_general(p.astype(MXU_DTYPE), dobm, (((0,), (0,)), ((), ())),
                                                   preferred_element_type=F32)
            dq_ref[rows, :] = dqs[0:M, :]
            dk_ref[rows, :] = dks[0:M, :]
            dv_ref[rows, :] = dvs[0:M, :]

    def body(q_ref, k_ref, v_ref, do_ref, o_ref, l_ref, b_ref, dq_ref, dk_ref, dv_ref, ds_ref, dqs, dks, dvs):
        g = pl.program_id(0)

        @pl.when(pl.program_id(2) == 0)
        def _():
            ds_ref[...] = jnp.zeros_like(ds_ref)

        for gi, dil in enumerate(DILATIONS):
            pl.when(g == gi)(functools.partial(group_body, dil, q_ref, k_ref, v_ref, do_ref, o_ref, l_ref, b_ref,
                                               dq_ref, dk_ref, dv_ref, ds_ref, dqs, dks, dvs))

    blk = (L, 2 * HEAD_DIM)
    gcol = lambda g, h, b: (b, g * HP + h)
    hcol = lambda g, h, b: (b, h)
    return _pallas(
        body, name=name, grid=(3, HP, nb_),
        in_specs=[pl.BlockSpec(blk, gcol), pl.BlockSpec(blk, gcol),
                  pl.BlockSpec(blk, lambda g, h, b: (b, 3 * HP + g * HP + h)),
                  pl.BlockSpec(blk, hcol), pl.BlockSpec(blk, hcol), pl.BlockSpec(blk, hcol),
                  pl.BlockSpec((2, BAND, 2 * BAND), lambda g, h, b: (g * HP + h, 0, 0))],
        out_specs=[pl.BlockSpec(blk, gcol), pl.BlockSpec(blk, gcol), pl.BlockSpec(blk, gcol),
                   pl.BlockSpec((2, BAND, 2 * BAND), lambda g, h, b: (g * HP + h, 0, 0))],
        out_shape=[jax.ShapeDtypeStruct((T, W3), F32), jax.ShapeDtypeStruct((T, W3), F32),
                   jax.ShapeDtypeStruct((T, W3), F32), jax.ShapeDtypeStruct((3 * hpg, BAND, 2 * BAND), F32)],
        scratch_shapes=[pltpu.VMEM((mmax, 2 * HEAD_DIM), F32)] * 3,
        compiler_params=_params(("arbitrary", "arbitrary", "arbitrary")),
    )(q, kv, kv, do, o, lse, bias)


def _bias_grad(ds_sum, hpg, *, name):
    nh = ds_sum.shape[0]
    idx = np.stack([np.where(_band_tables(dil)[1], _band_tables(dil)[0], -1) for dil in DILATIONS]).astype(np.int32)

    def body(ds_ref, idx_ref, o_ref):
        d = ds_ref[...]
        ix = idx_ref[...]
        lane = lax.broadcasted_iota(jnp.int32, (8, 128), 1)
        row = jnp.zeros((8, 128), F32)
        for b in range(REL_BUCKETS):
            row = row + jnp.where(lane == b, jnp.sum(jnp.where(ix == b, d, 0.0)), 0.0)
        o_ref[...] = row

    out = _pallas(
        body, name=name, grid=(nh,),
        in_specs=[pl.BlockSpec((None, BAND, 2 * BAND), lambda h: (h, 0, 0)),
                  pl.BlockSpec((None, BAND, 2 * BAND), lambda h: (h // hpg, 0, 0))],
        out_specs=pl.BlockSpec((None, 8, 128), lambda h: (h, 0, 0)),
        out_shape=jax.ShapeDtypeStruct((nh, 8, 128), F32),
        compiler_params=_params(("parallel",)),
    )(ds_sum, jnp.asarray(idx))
    return out[:, 0, :REL_BUCKETS].T


def _adamw(w, g, m, v, *, name):
    Rw, C = w.shape
    tm = _pick(Rw, (512, 352, 256, 128, 64, 32, 16, 8))

    def body(w_ref, g_ref, m_ref, v_ref, d_ref, nm_ref, nv_ref):
        gg = g_ref[...]
        nm = ADAM_B1 * m_ref[...] + (1.0 - ADAM_B1) * gg
        nv = ADAM_B2 * v_ref[...] + (1.0 - ADAM_B2) * (gg * gg)
        m_hat = nm / (1.0 - ADAM_B1 ** ADAM_STEP)
        v_hat = nv / (1.0 - ADAM_B2 ** ADAM_STEP)
        d_ref[...] = -ADAM_LR * (m_hat / (jnp.sqrt(v_hat) + ADAM_EPS) + ADAM_WD * w_ref[...])
        nm_ref[...] = nm
        nv_ref[...] = nv

    return _pallas(
        body, name=name, grid=(Rw // tm,), in_specs=[_rows(tm, C)] * 4, out_specs=[_rows(tm, C)] * 3,
        out_shape=[jax.ShapeDtypeStruct((Rw, C), F32)] * 3, compiler_params=_params(("parallel",)),
    )(w, g, m, v)


ROW_TILE_ELEMS = 256 * 1024


def _tile_rows(r, c):
    best = 8
    for t in range(8, r + 1, 8):
        if r % t == 0 and t * c <= ROW_TILE_ELEMS:
            best = t
    return best


def _adamw_halves(w, m, v, mine, other, cidx, *, name):
    _, r, c = w.shape
    tm = _tile_rows(r, c)

    def body(c_ref, w_ref, m_ref, v_ref, a_ref, b_ref, g_ref, d_ref, nm_ref, nv_ref):
        gg = jnp.where(pl.program_id(0) == c_ref[0], a_ref[...], b_ref[...])
        nm = ADAM_B1 * m_ref[...] + (1.0 - ADAM_B1) * gg
        nv = ADAM_B2 * v_ref[...] + (1.0 - ADAM_B2) * (gg * gg)
        m_hat = nm / (1.0 - ADAM_B1 ** ADAM_STEP)
        v_hat = nv / (1.0 - ADAM_B2 ** ADAM_STEP)
        g_ref[...] = gg
        d_ref[...] = -ADAM_LR * (m_hat / (jnp.sqrt(v_hat) + ADAM_EPS) + ADAM_WD * w_ref[...])
        nm_ref[...] = nm
        nv_ref[...] = nv

    half = pl.BlockSpec((None, tm, c), lambda h, i, cr: (h, i, 0))
    one = pl.BlockSpec((None, tm, c), lambda h, i, cr: (0, i, 0))
    spec = pltpu.PrefetchScalarGridSpec(num_scalar_prefetch=1, grid=(2, r // tm),
                                        in_specs=[half, half, half, one, one], out_specs=[half] * 4)
    return _pallas(
        body, name=name, grid_spec=spec, out_shape=[jax.ShapeDtypeStruct((2, r, c), F32)] * 4,
        compiler_params=_params(("parallel", "parallel")),
    )(cidx, w, m, v, mine, other)


def _pair_sum(g, theirs, cidx, *, cast, name):
    _, _, r, c = g.shape
    tm = _tile_rows(r, c)

    def body(c_ref, g_ref, t_ref, *outs):
        s = g_ref[...] + t_ref[...]
        outs[0][...] = s
        if cast:
            outs[1][...] = s.astype(BF16)

    blk = (None, None, tm, c)
    first = pl.BlockSpec(blk, lambda p, i, cr: (p, 0, i, 0))
    shapes = [jax.ShapeDtypeStruct((4, 1, r, c), F32)] + ([jax.ShapeDtypeStruct((4, 1, r, c), BF16)] if cast else [])
    spec = pltpu.PrefetchScalarGridSpec(
        num_scalar_prefetch=1, grid=(4, r // tm),
        in_specs=[pl.BlockSpec(blk, lambda p, i, cr: (p, cr[0], i, 0)), first], out_specs=[first] * len(shapes))
    return _pallas(body, name=name, grid_spec=spec, out_shape=shapes,
                   compiler_params=_params(("parallel", "parallel")))(cidx, g, theirs)


def _chip_sum(hf, got, chip_idx, *, name):
    _, _, r, c = hf.shape
    tm = _tile_rows(r, c)

    def body(p_ref, h_ref, r_ref, o_ref):
        s = h_ref[...]
        for k in range(3):
            s = s + r_ref[k].astype(F32)
        o_ref[...] = s

    spec = pltpu.PrefetchScalarGridSpec(
        num_scalar_prefetch=1, grid=(r // tm,),
        in_specs=[pl.BlockSpec((None, None, tm, c), lambda i, pr: (pr[0], 0, i, 0)),
                  pl.BlockSpec((3, None, tm, c), lambda i, pr: (0, 0, i, 0))],
        out_specs=pl.BlockSpec((None, tm, c), lambda i, pr: (0, i, 0)))
    return _pallas(body, name=name, grid_spec=spec, out_shape=jax.ShapeDtypeStruct((1, r, c), F32),
                   compiler_params=_params(("parallel",)))(chip_idx, hf, got)


def _place():
    x, y, c = lax.axis_index("x"), lax.axis_index("y"), lax.axis_index("c")
    chips = [(1 - x, y), (x, 1 - y), (1 - x, 1 - y)]
    return x, y, c, chips


_ANY = pl.BlockSpec(memory_space=pl.ANY)


def _comm_call(body, ins, out_shapes, n_remote, *, name):
    sems = [pltpu.SemaphoreType.DMA((n,)) for n in n_remote]
    return _pallas(
        body, name=name, in_specs=[_ANY] * len(ins), out_specs=[_ANY] * len(out_shapes), out_shape=out_shapes,
        scratch_shapes=sems, compiler_params=pltpu.CompilerParams(has_side_effects=True),
    )(*ins)


def _rcopy(src, dst, ssem, rsem, dev):
    return pltpu.make_async_remote_copy(src_ref=src, dst_ref=dst, send_sem=ssem, recv_sem=rsem,
                                        device_id=dev, device_id_type=MESH)


def _all_gather(shards, *, name):
    n = len(shards)

    def body(*refs):
        ins, outs = refs[:n], refs[n:2 * n]
        s_ici, r_ici, s_d2d, r_d2d = refs[2 * n:]
        x, y, c, chips = _place()
        me = 2 * x + y
        sib = (x, y, 1 - c)
        sends = []
        for a in range(n):
            for k, (tx, ty) in enumerate(chips):
                cp = _rcopy(ins[a].at[c], outs[a].at[me, c], s_ici.at[3 * a + k], r_ici.at[3 * a + k], (tx, ty, c))
                cp.start()
                sends.append(cp)
        for a in range(n):
            for k, (tx, ty) in enumerate(chips):
                pk = 2 * tx + ty
                _rcopy(ins[a].at[c], outs[a].at[pk, c], s_ici.at[3 * a + k], r_ici.at[3 * a + k], (tx, ty, c)).wait_recv()
                fw = _rcopy(outs[a].at[pk, c], outs[a].at[pk, c], s_d2d.at[3 * a + k], r_d2d.at[3 * a + k], sib)
                fw.start()
                sends.append(fw)
        for a in range(n):
            for k, (tx, ty) in enumerate(chips):
                pk = 2 * tx + ty
                _rcopy(ins[a].at[c], outs[a].at[pk, 1 - c], s_d2d.at[3 * a + k], r_d2d.at[3 * a + k], sib).wait_recv()
        for cp in sends:
            cp.wait_send()

    shapes = [jax.ShapeDtypeStruct((4,) + s.shape, s.dtype) for s in shards]
    return _comm_call(body, shards, shapes, [3 * n] * 4, name=name)


def _gather(shards, chip, *, name):
    outs = _all_gather(shards, name=name)
    return [lax.dynamic_update_slice(o, s[None], (chip, 0, 0, 0)) for o, s in zip(outs, shards)]


def _pair_send(gs, *, name):
    n = len(gs)

    def body(*refs):
        ins, theirs = refs[:n], refs[n:2 * n]
        ssem, rsem = refs[2 * n:]
        x, y, c, _ = _place()
        sib = (x, y, 1 - c)
        cps = []
        for a in range(n):
            cp = _rcopy(ins[a].at[:, pl.ds(1 - c, 1)], theirs[a], ssem.at[a], rsem.at[a], sib)
            cp.start()
            cps.append(cp)
        for cp in cps:
            cp.wait_send()
            cp.wait_recv()

    shapes = [jax.ShapeDtypeStruct((4, 1) + g.shape[2:], g.dtype) for g in gs]
    return _comm_call(body, gs, shapes, [n, n], name=name)


def _chip_exchange(hx, *, name):
    n = len(hx)

    def body(*refs):
        hxr, got = refs[:n], refs[n:2 * n]
        ssem, rsem = refs[2 * n:]
        x, y, c, chips = _place()
        cps = []
        for a in range(n):
            for k, (tx, ty) in enumerate(chips):
                cp = _rcopy(hxr[a].at[2 * tx + ty], got[a].at[k], ssem.at[3 * a + k], rsem.at[3 * a + k], (tx, ty, c))
                cp.start()
                cps.append(cp)
        for cp in cps:
            cp.wait_send()
            cp.wait_recv()

    shapes = [jax.ShapeDtypeStruct((3,) + h.shape[1:], h.dtype) for h in hx]
    return _comm_call(body, hx, shapes, [3 * n, 3 * n], name=name)


def _pair_swap(fs, *, name):
    n = len(fs)

    def body(*refs):
        ins, outs = refs[:n], refs[n:2 * n]
        ssem, rsem = refs[2 * n:]
        x, y, c, _ = _place()
        cps = []
        for a in range(n):
            cp = _rcopy(ins[a], outs[a], ssem.at[a], rsem.at[a], (x, y, 1 - c))
            cp.start()
            cps.append(cp)
        for cp in cps:
            cp.wait_send()
            cp.wait_recv()

    shapes = [jax.ShapeDtypeStruct(f.shape, f.dtype) for f in fs]
    return _comm_call(body, fs, shapes, [n, n], name=name)


def _reduce_scatter(grads, exch_bf16, cidx, chip_idx, tag):
    n = len(grads)
    theirs = _pair_send(grads, name=f"rs_pair_send_{tag}")
    hf, hx = [], []
    for a in range(n):
        res = _pair_sum(grads[a], theirs[a], cidx, cast=exch_bf16[a], name=f"rs_pair_sum_{tag}{a}")
        hf.append(res[0])
        hx.append(res[1] if exch_bf16[a] else res[0])
    got = _chip_exchange(hx, name=f"rs_chip_exchange_{tag}")
    mine = [_chip_sum(hf[a], got[a], chip_idx, name=f"rs_chip_sum_{tag}{a}") for a in range(n)]
    return mine, _pair_swap(mine, name=f"rs_pair_swap_{tag}")


def _local_step(x, tgt, W, S):
    B, L, D = x.shape
    T = B * L
    G = D // SSM_GROUP
    Pst = SSM_STATE
    hpg = D // HEAD_DIM
    HW = hpg * HEAD_DIM
    ncl = G // GROUPS_PER_CLUSTER
    x2 = x.reshape(T, D)
    tgt2 = tgt.reshape(T, D)

    disc = lambda *p: _s5_discretize(*p)
    (ab_r, ab_i, bb_r, bb_i), disc_vjp = jax.vjp(disc, S["lam_re"], S["lam_im"], S["log_dt"], S["b_re"], S["b_im"])
    wb = jnp.concatenate([_blockdiag(jnp.transpose(bb_r, (0, 2, 1))), _blockdiag(jnp.transpose(bb_i, (0, 2, 1)))],
                         axis=-1).astype(MXU_DTYPE)
    wc = jnp.concatenate([_blockdiag(jnp.transpose(S["c_re"], (0, 2, 1))), _blockdiag(-jnp.transpose(S["c_im"], (0, 2, 1)))],
                         axis=1).astype(MXU_DTYPE)
    cs = GROUPS_PER_CLUSTER * Pst
    slab = lambda ab: jnp.tile(jnp.transpose(ab.reshape(ncl, cs // LANES, LANES), (1, 0, 2)), (1, B, 1))
    a_r, a_i = slab(ab_r), slab(ab_i)
    d_row = S["d"].reshape(1, D)

    y3, yg3, h_r, h_i = _s5_fwd(x, wb, wc, a_r, a_i, d_row, name="s5_fwd")
    y, yg = y3.reshape(T, D), yg3.reshape(T, D)
    z = _mm_nn(yg, W["w_glu"], bias=S["b_glu"].reshape(1, D), name="glu_z")
    gate = _glu_gate(y, z, name="glu_gate")
    mix = _mm_nn(gate, W["w_out"], name="s5_out")
    h1, h1b, xh1, rs1 = _ln_fwd(x2, mix, S["ln_gain"][0, 0][None], S["ln_bias"][0, 0][None], name="ln_fwd_0a")

    def ffn_fwd(hb, l):
        hc = _mm_nn(hb, W["w_up"], l=l, name=f"ffn_up_{l}")
        a = _conv_glu_fwd(hc, S["conv_w"][l], S["conv_b"][l][None], L, name=f"ffn_conv_{l}")
        f = _mm_nn(a, W["w_down"], l=l, name=f"ffn_down_{l}")
        return hc, a, f

    hc0, a0, f0 = ffn_fwd(h1b, 0)
    h2, h2b, xh2, rs2 = _ln_fwd(h1, f0, S["ln_gain"][0, 1][None], S["ln_bias"][0, 1][None], name="ln_fwd_0b")

    kv = _mm_nn(h2b, W["w_kv"], name="attn_kv")
    q = _mm_nn(h2b, W["w_q"], name="attn_q")
    bias = _attn_bias(S["rel_bias"], hpg)
    o3, l3 = _attn_fwd(q, kv, bias, L, hpg, name="attn_fwd")
    o, ob, lse = _attn_merge(o3, l3, HW, name="attn_merge")
    att = _mm_nn(ob, W["w_ao"], name="attn_out")
    h3, h3b, xh3, rs3 = _ln_fwd(h2, att, S["ln_gain"][1, 0][None], S["ln_bias"][1, 0][None], name="ln_fwd_1a")
    hc1, a1, f1 = ffn_fwd(h3b, 1)
    h4, _, xh4, rs4 = _ln_fwd(h3, f1, S["ln_gain"][1, 1][None], S["ln_bias"][1, 1][None], name="ln_fwd_1b")

    dh4, lrow = _loss_grad(h4, tgt2, name="loss")
    loss = lrow[0, 0]

    GW, GS = {}, {}

    def ffn_bwd(dzb, hb, hc, a, l):
        da = _mm_nt(dzb, W["w_down"], l=l, name=f"ffn_down_bwd_x_{l}")
        GW["w_down"] = _tn(a, dzb, ptotal=1, nl=DEPTH, l=l, np_cols=D, prev=GW.get("w_down"), name=f"ffn_down_bwd_w_{l}")
        dc, dcw, dcb = _conv_glu_bwd(hc, da, S["conv_w"][l], S["conv_b"][l][None], L, name=f"ffn_conv_bwd_{l}")
        dhc = _conv_bwd_input(dc, S["conv_w"][l], L, name=f"ffn_conv_bwd_x_{l}")
        dh = _mm_nt(dhc, W["w_up"], l=l, name=f"ffn_up_bwd_x_{l}")
        GW["w_up"] = _tn(hb, dhc, ptotal=W["w_up"].shape[0], nl=DEPTH, l=l, np_cols=W["w_up"].shape[3],
                         prev=GW.get("w_up"), name=f"ffn_up_bwd_w_{l}")
        return dh, dcw, dcb

    dz4, dz4b, dg4, db4 = _ln_bwd([dh4], [1.0], xh4, rs4, S["ln_gain"][1, 1][None], name="ln_bwd_1b")
    dh3f, dcw1, dcb1 = ffn_bwd(dz4b, h3b, hc1, a1, 1)
    dz3, dz3b, dg3, db3 = _ln_bwd([dz4, dh3f], [DN_ALPHA, 1.0], xh3, rs3, S["ln_gain"][1, 0][None], name="ln_bwd_1a")
    do = _mm_nt(dz3b, W["w_ao"], name="attn_out_bwd_x")
    GW["w_ao"] = _tn(ob, dz3b, ptotal=1, np_cols=D, name="attn_out_bwd_w")
    dq, dk, dv, ds_sum = _attn_bwd(q, kv, do, o, lse, bias, L, hpg, name="attn_bwd")
    GS["rel_bias"] = _bias_grad(ds_sum, hpg, name="attn_bias_grad")
    GW["w_q"] = _tn(h2b, dq, ptotal=W["w_q"].shape[0], np_cols=W["w_q"].shape[3], name="attn_q_bwd_w")
    pkv, npkv = W["w_kv"].shape[0], W["w_kv"].shape[3]
    gkv = _tn(h2b, dk, ptotal=pkv, np_cols=npkv, p0=0, name="attn_k_bwd_w")
    GW["w_kv"] = _tn(h2b, dv, ptotal=pkv, np_cols=npkv, p0=pkv // 2, prev=gkv, name="attn_v_bwd_w")
    dh2q = _mm_nt(dq, W["w_q"], name="attn_q_bwd_x")
    dh2k = _mm_nt(dk, W["w_kv"], p0=0, pn=pkv // 2, name="attn_k_bwd_x")
    dh2v = _mm_nt(dv, W["w_kv"], p0=pkv // 2, pn=pkv // 2, name="attn_v_bwd_x")

    dz2, dz2b, dg2, db2 = _ln_bwd([dz3, dh2q, dh2k, dh2v], [DN_ALPHA, 1.0, 1.0, 1.0], xh2, rs2,
                                  S["ln_gain"][0, 1][None], name="ln_bwd_0b")
    dh1f, dcw0, dcb0 = ffn_bwd(dz2b, h1b, hc0, a0, 0)
    dz1, dz1b, dg1, db1 = _ln_bwd([dz2, dh1f], [DN_ALPHA, 1.0], xh1, rs1, S["ln_gain"][0, 0][None], name="ln_bwd_0a")
    dgate = _mm_nt(dz1b, W["w_out"], name="s5_out_bwd_x")
    GW["w_out"] = _tn(gate, dz1b, ptotal=1, np_cols=D, name="s5_out_bwd_w")
    dzg, dyg1, dbglu = _glu_bwd(y, z, dgate, name="glu_bwd")
    dyg2 = _mm_nt(dzg, W["w_glu"], name="glu_z_bwd_x")
    GW["w_glu"] = _tn(yg, dzg, ptotal=1, np_cols=D, name="glu_z_bwd_w")
    dy3 = _gelu_bwd(y, dyg1, dyg2, name="gelu_bwd").reshape(B, L, D)
    du3, g_r, g_i, dar, dai, dd = _s5_bwd(dy3, x, h_r, h_i, wb, wc, a_r, a_i, d_row, name="s5_bwd")
    dwb_r = _cluster_tn(x, g_r, ncl, tok_left=True, name="s5_b_grad_re")
    dwb_i = _cluster_tn(x, g_i, ncl, tok_left=True, name="s5_b_grad_im")
    dwc_r = _cluster_tn(dy3, h_r, ncl, tok_left=False, name="s5_c_grad_re")
    dwc_i = _cluster_tn(dy3, h_i, ncl, tok_left=False, name="s5_c_grad_im")
    grad_x = _axpy(dz1, du3.reshape(T, D), DN_ALPHA, name="grad_x")

    dbb_r = jnp.transpose(_unblockdiag(dwb_r, SSM_GROUP, Pst), (0, 2, 1))
    dbb_i = jnp.transpose(_unblockdiag(dwb_i, SSM_GROUP, Pst), (0, 2, 1))
    unslab = lambda da: jnp.transpose(da.reshape(cs // LANES, B, ncl, LANES).sum(1), (1, 0, 2)).reshape(G, Pst)
    dab_r, dab_i = unslab(dar), unslab(dai)
    GS["lam_re"], GS["lam_im"], GS["log_dt"], GS["b_re"], GS["b_im"] = disc_vjp((dab_r, dab_i, dbb_r, dbb_i))
    GS["c_re"] = jnp.transpose(_unblockdiag(dwc_r, Pst, SSM_GROUP), (0, 2, 1))
    GS["c_im"] = -jnp.transpose(_unblockdiag(dwc_i, Pst, SSM_GROUP), (0, 2, 1))
    GS["d"] = dd.reshape(G, SSM_GROUP)
    GS["b_glu"] = dbglu.reshape(D)
    GS["conv_w"] = jnp.stack([dcw0, dcw1])
    GS["conv_b"] = jnp.stack([dcb0[0], dcb1[0]])
    GS["ln_gain"] = jnp.stack([jnp.stack([dg1[0], dg2[0]]), jnp.stack([dg3[0], dg4[0]])])
    GS["ln_bias"] = jnp.stack([jnp.stack([db1[0], db2[0]]), jnp.stack([db3[0], db4[0]])])
    return loss, grad_x.reshape(B, L, D), GW, GS


SMALL_REPLICATED = ("lam_re", "lam_im", "log_dt", "b_re", "b_im", "c_re", "c_im", "d", "rel_bias", "conv_b")
SMALL_SHARDED = ("b_glu", "conv_w", "ln_gain", "ln_bias")
SMALL_ORDER = SMALL_REPLICATED + SMALL_SHARDED


def _pack(arrs, lanes, row_mult):
    flat = jnp.concatenate([a.reshape(-1).astype(F32) for a in arrs])
    rows = -(-flat.shape[0] // lanes)
    rows = -(-rows // row_mult) * row_mult
    return jnp.pad(flat, (0, rows * lanes - flat.shape[0])).reshape(rows, lanes)


def _unpack(packed, shapes):
    flat = packed.reshape(-1)
    out, off = [], 0
    for s in shapes:
        n = int(np.prod(s))
        out.append(flat[off:off + n].reshape(s))
        off += n
    return out


def kernel(x, s5_lam_re, s5_lam_im, s5_log_dt, s5_b_re, s5_b_im, s5_c_re, s5_c_im, s5_d, s5_w_glu, s5_b_glu, s5_w_out, attn_w_kv, attn_w_q, attn_w_out, rel_bias, ffn_w_up, ffn_conv_w, ffn_conv_b, ffn_w_down, ln_gain, ln_bias, loss_target, m_s5_lam_re, m_s5_lam_im, m_s5_log_dt, m_s5_b_re, m_s5_b_im, m_s5_c_re, m_s5_c_im, m_s5_d, m_s5_w_glu, m_s5_b_glu, m_s5_w_out, m_attn_w_kv, m_attn_w_q, m_attn_w_out, m_rel_bias, m_ffn_w_up, m_ffn_conv_w, m_ffn_conv_b, m_ffn_w_down, m_ln_gain, m_ln_bias, v_s5_lam_re, v_s5_lam_im, v_s5_log_dt, v_s5_b_re, v_s5_b_im, v_s5_c_re, v_s5_c_im, v_s5_d, v_s5_w_glu, v_s5_b_glu, v_s5_w_out, v_attn_w_kv, v_attn_w_q, v_attn_w_out, v_rel_bias, v_ffn_w_up, v_ffn_conv_w, v_ffn_conv_b, v_ffn_w_down, v_ln_gain, v_ln_bias):
    names = ["s5_lam_re", "s5_lam_im", "s5_log_dt", "s5_b_re", "s5_b_im", "s5_c_re", "s5_c_im", "s5_d", "s5_w_glu",
             "s5_b_glu", "s5_w_out", "attn_w_kv", "attn_w_q", "attn_w_out", "rel_bias", "ffn_w_up", "ffn_conv_w",
             "ffn_conv_b", "ffn_w_down", "ln_gain", "ln_bias"]
    loc = locals()
    w_in = {n: loc[n] for n in names}
    m_in = {n: loc["m_" + n] for n in names}
    v_in = {n: loc["v_" + n] for n in names}
    chip = 2 * lax.axis_index("x") + lax.axis_index("y")
    core = lax.axis_index("c")
    chip_idx = jnp.reshape(chip, (1,)).astype(jnp.int32)
    cidx = jnp.reshape(core, (1,)).astype(jnp.int32)

    big = [("w_glu", "s5_w_glu", "rows"), ("w_out", "s5_w_out", "rows"), ("w_ao", "attn_w_out", "rows"),
           ("w_kv", "attn_w_kv", "cols"), ("w_q", "attn_w_q", "cols"),
           ("w_up", "ffn_w_up", "layer_cols"), ("w_down", "ffn_w_down", "layer_rows")]

    def halves(t, kind):
        if kind.startswith("layer"):
            return t
        r, c = t.shape[-2:]
        return t.reshape(2, r // 2, c)

    def to_weight(g, kind):
        _, _, r, c = g.shape
        if kind == "rows":
            return g.reshape(1, 1, 8 * r, c)
        if kind == "cols":
            return g.reshape(4, 1, 2 * r, c)
        if kind == "layer_cols":
            return g
        return jnp.transpose(g, (1, 0, 2, 3)).reshape(1, 2, 4 * r, c)

    def from_weight_grad(gw, kind, r, c):
        if kind == "layer_rows":
            return jnp.transpose(gw.reshape(2, 4, r, c), (1, 0, 2, 3))
        return gw.reshape(4, 2, r, c)

    small_sh = {"b_glu": s5_b_glu[0], "conv_w": ffn_conv_w, "ln_gain": ln_gain, "ln_bias": ln_bias}
    sh_shapes = [small_sh[k].shape for k in SMALL_SHARDED]
    sh_pack = _pack([small_sh[k] for k in SMALL_SHARDED], 128, 16)

    shards = [halves(w_in[src].astype(MXU_DTYPE), kind) for _, src, kind in big]
    shards.append(sh_pack.reshape(2, sh_pack.shape[0] // 2, 128))
    gathered = _gather(shards, chip, name="weights_all_gather")

    W = {key: to_weight(g, kind) for (key, _, kind), g in zip(big, gathered[:-1])}
    parts = [_unpack(gathered[-1][p], sh_shapes) for p in range(4)]
    S = {k: jnp.concatenate([parts[p][i] for p in range(4)], axis=-1) for i, k in enumerate(SMALL_SHARDED)}
    S.update(lam_re=s5_lam_re[0], lam_im=s5_lam_im[0], log_dt=s5_log_dt[0], b_re=s5_b_re[0], b_im=s5_b_im[0],
             c_re=s5_c_re[0], c_im=s5_c_im[0], d=s5_d[0], rel_bias=rel_bias, conv_b=ffn_conv_b)

    loss, grad_x, GW, GS = _local_step(x, loss_target, W, S)
    loss = lax.psum(loss, ("x", "y", "c"))

    gs_shapes = [GS[k].shape for k in SMALL_ORDER]
    gs_pack = _pack([GS[k] for k in SMALL_ORDER], 128, 64)
    rs = gs_pack.shape[0] // 8
    grads = []
    for (key, _, kind), sh in zip(big, shards):
        grads.append(from_weight_grad(GW[key], kind, sh.shape[1], sh.shape[2]))
    grads.append(gs_pack.reshape(4, 2, rs, 128))
    mine, other = _reduce_scatter(grads, [True] * len(big) + [False], cidx, chip_idx, "g")
    small_halves = jnp.where(core == 0, jnp.concatenate([mine[-1], other[-1]]), jnp.concatenate([other[-1], mine[-1]]))
    small_all = _gather([small_halves], chip, name="small_grads_all_gather")[0]
    gsmall = dict(zip(SMALL_ORDER, _unpack(small_all, gs_shapes)))

    big_res = {}
    for (key, src, kind), gm, go in zip(big, mine[:-1], other[:-1]):
        res4 = _adamw_halves(halves(w_in[src], kind), halves(m_in[src], kind), halves(v_in[src], kind), gm, go, cidx,
                             name=f"adamw_{key}")
        big_res[src] = tuple(t.reshape(w_in[src].shape) for t in res4)

    def big_out(i):
        return {src: big_res[src][i] for _, src, _ in big}

    small_w = {"lam_re": s5_lam_re, "lam_im": s5_lam_im, "log_dt": s5_log_dt, "b_re": s5_b_re, "b_im": s5_b_im,
               "c_re": s5_c_re, "c_im": s5_c_im, "d": s5_d, "rel_bias": rel_bias, "conv_b": ffn_conv_b,
               "b_glu": s5_b_glu, "conv_w": ffn_conv_w, "ln_gain": ln_gain, "ln_bias": ln_bias}
    small_name = {"lam_re": "s5_lam_re", "lam_im": "s5_lam_im", "log_dt": "s5_log_dt", "b_re": "s5_b_re", "b_im": "s5_b_im",
                  "c_re": "s5_c_re", "c_im": "s5_c_im", "d": "s5_d", "rel_bias": "rel_bias", "conv_b": "ffn_conv_b",
                  "b_glu": "s5_b_glu", "conv_w": "ffn_conv_w", "ln_gain": "ln_gain", "ln_bias": "ln_bias"}
    sg = {}
    for k in SMALL_ORDER:
        shp = small_w[k].shape
        g = gsmall[k]
        if k in SMALL_SHARDED:
            width = shp[-1]
            g = lax.dynamic_slice_in_dim(g, chip * width, width, axis=g.ndim - 1)
        sg[k] = g.reshape(shp)
    sshapes = [small_w[k].shape for k in SMALL_ORDER]
    pw = _pack([small_w[k] for k in SMALL_ORDER], 128, 512)
    pg = _pack([sg[k] for k in SMALL_ORDER], 128, 512)
    pm = _pack([m_in[small_name[k]] for k in SMALL_ORDER], 128, 512)
    pv = _pack([v_in[small_name[k]] for k in SMALL_ORDER], 128, 512)
    sd, snm, snv = _adamw(pw, pg, pm, pv, name="adamw_small")
    sd = dict(zip(SMALL_ORDER, _unpack(sd, sshapes)))
    snm = dict(zip(SMALL_ORDER, _unpack(snm, sshapes)))
    snv = dict(zip(SMALL_ORDER, _unpack(snv, sshapes)))

    res = [{}, {}, {}, {}]
    for i in range(4):
        res[i].update(big_out(i))
    for k in SMALL_ORDER:
        res[0][small_name[k]] = sg[k]
        res[1][small_name[k]] = sd[k]
        res[2][small_name[k]] = snm[k]
        res[3][small_name[k]] = snv[k]
    outs = [loss, grad_x]
    for i in range(4):
        outs += [res[i][n] for n in names]
    return tuple(outs)
```

```python
import functools
import math

import numpy as np
import jax
import jax.numpy as jnp
from jax import lax
from jax.experimental import pallas as pl
from jax.experimental.pallas import tpu as pltpu

F32 = jnp.float32
BF16 = jnp.bfloat16
MXU_DTYPE = jnp.bfloat16
V7X_VMEM_LIMIT_BYTES = 52 << 20
MESH = pl.DeviceIdType.MESH

DEPTH = 2
SSM_GROUP = 16
SSM_STATE = 64
GROUPS_PER_CLUSTER = 16
CLUSTER_W = GROUPS_PER_CLUSTER * SSM_GROUP
HEAD_DIM = 64
DILATIONS = (1, 4, 16)
BAND = 128
NEG_BIG = -1e30
REL_BUCKETS = 32
REL_MAX_DIST = 2048
DN_ALPHA = (2.0 * DEPTH) ** 0.25
LN_EPS = 1e-5
ADAM_LR, ADAM_B1, ADAM_B2, ADAM_EPS, ADAM_WD, ADAM_STEP = 0.001, 0.9, 0.999, 1e-08, 0.01, 10
GELU_K = math.sqrt(2.0 / math.pi)
GELU_C = 0.044715


def _pallas(body, **kw):
    return pl.pallas_call(body, **kw)


def _params(sem=None):
    return pltpu.CompilerParams(dimension_semantics=sem, vmem_limit_bytes=V7X_VMEM_LIMIT_BYTES)


def _pick(n, cands):
    for c in cands:
        if n % c == 0:
            return c
    return n


def _sigmoid(z):
    return 1.0 / (1.0 + jnp.exp(-z))


def _gelu(y):
    return 0.5 * y * (1.0 + jnp.tanh(GELU_K * (y + GELU_C * y * y * y)))


def _gelu_grad(y):
    t = jnp.tanh(GELU_K * (y + GELU_C * y * y * y))
    return 0.5 * (1.0 + t) + 0.5 * y * (1.0 - t * t) * (GELU_K * (1.0 + 3.0 * GELU_C * y * y))


def _mm_nn(a, w, *, l=0, bias=None, out_dtype=F32, name):
    T, K = a.shape
    P, _, _, Np = w.shape
    tm = _pick(T, (1024, 512, 256, 128))
    tn = _pick(Np, (1408, 1024, 768, 512, 384, 256, 128))
    nj = Np // tn

    def body(*refs):
        if bias is None:
            a_ref, w_ref, o_ref = refs
        else:
            a_ref, w_ref, b_ref, o_ref = refs
        acc = jnp.dot(a_ref[...].astype(MXU_DTYPE), w_ref[...].astype(MXU_DTYPE), preferred_element_type=F32)
        if bias is not None:
            acc = acc + b_ref[...]
        o_ref[...] = acc.astype(o_ref.dtype)

    in_specs = [pl.BlockSpec((tm, K), lambda p, j, i: (i, 0)),
                pl.BlockSpec((None, None, K, tn), lambda p, j, i: (p, l, 0, j))]
    args = [a, w]
    if bias is not None:
        in_specs.append(pl.BlockSpec((1, tn), lambda p, j, i: (0, p * nj + j)))
        args.append(bias)
    return _pallas(
        body, name=name, grid=(P, nj, T // tm), in_specs=in_specs,
        out_specs=pl.BlockSpec((tm, tn), lambda p, j, i: (i, p * nj + j)),
        out_shape=jax.ShapeDtypeStruct((T, P * Np), out_dtype),
        compiler_params=_params(("parallel", "parallel", "parallel")),
    )(*args)


def _mm_nt(a, w, *, l=0, p0=0, pn=None, out_dtype=F32, name):
    T = a.shape[0]
    _, _, K, Np = w.shape
    pn = w.shape[0] if pn is None else pn
    tm = _pick(T, (1024, 512, 256, 128) if K <= 1024 else (512, 256, 128))
    tn = _pick(Np, (1536, 1408, 1024, 768, 512, 384, 256, 128))
    nj = Np // tn
    nred = pn * nj

    def body(a_ref, w_ref, o_ref, acc):
        r = pl.program_id(1)

        @pl.when(r == 0)
        def _():
            acc[...] = jnp.zeros_like(acc)

        acc[...] += lax.dot_general(a_ref[...].astype(MXU_DTYPE), w_ref[...].astype(MXU_DTYPE),
                                    (((1,), (1,)), ((), ())), preferred_element_type=F32)

        @pl.when(r == nred - 1)
        def _():
            o_ref[...] = acc[...].astype(o_ref.dtype)

    return _pallas(
        body, name=name, grid=(T // tm, nred),
        in_specs=[pl.BlockSpec((tm, tn), lambda i, r: (i, r)),
                  pl.BlockSpec((None, None, K, tn), lambda i, r: (p0 + r // nj, l, 0, r % nj))],
        out_specs=pl.BlockSpec((tm, K), lambda i, r: (i, 0)),
        out_shape=jax.ShapeDtypeStruct((T, K), out_dtype),
        scratch_shapes=[pltpu.VMEM((tm, K), F32)],
        compiler_params=_params(("parallel", "arbitrary")),
    )(a, w)


def _tn(a, b, *, ptotal, np_cols, nl=1, l=0, p0=0, prev=None, name):
    T, K = a.shape
    Np = np_cols
    pn = b.shape[1] // Np
    tt = _pick(T, (1024, 512, 256, 128))
    tk = _pick(K, (1408, 1024, 512, 256, 128))
    tn = _pick(Np, (1408, 768, 512, 256, 128))
    if tk * tn > 1408 * 1024:
        tn = _pick(Np, (512, 256, 128))
    nj = Np // tn
    nt = T // tt

    def body(*refs):
        a_ref, b_ref = refs[0], refs[1]
        o_ref, acc = refs[-2], refs[-1]
        t = pl.program_id(3)

        @pl.when(t == 0)
        def _():
            acc[...] = jnp.zeros_like(acc)

        acc[...] += lax.dot_general(a_ref[...].astype(MXU_DTYPE), b_ref[...].astype(MXU_DTYPE),
                                    (((0,), (0,)), ((), ())), preferred_element_type=F32)

        @pl.when(t == nt - 1)
        def _():
            o_ref[...] = acc[...]

    in_specs = [pl.BlockSpec((tt, tk), lambda kb, p, j, t: (t, kb)),
                pl.BlockSpec((tt, tn), lambda kb, p, j, t: (t, p * nj + j))]
    args = [a, b]
    aliases = {}
    if prev is not None:
        in_specs.append(pl.BlockSpec(memory_space=pl.ANY))
        args.append(prev)
        aliases = {2: 0}
    return _pallas(
        body, name=name, grid=(K // tk, pn, nj, nt), in_specs=in_specs,
        out_specs=pl.BlockSpec((None, None, tk, tn), lambda kb, p, j, t: (p0 + p, l, kb, j)),
        out_shape=jax.ShapeDtypeStruct((ptotal, nl, K, Np), F32),
        scratch_shapes=[pltpu.VMEM((tk, tn), F32)],
        input_output_aliases=aliases,
        compiler_params=_params(("parallel", "parallel", "parallel", "arbitrary")),
    )(*args)


def _rows(tm, f):
    return pl.BlockSpec((tm, f), lambda i: (i, 0))


def _whole(shape):
    nd = len(shape)
    return pl.BlockSpec(shape, lambda i: (0,) * nd)


def _ln_fwd(xres, f, gain, bias, *, name):
    T, D = xres.shape
    tm = _pick(T, (256, 128))

    def body(x_ref, f_ref, g_ref, b_ref, y_ref, yb_ref, xh_ref, rs_ref):
        z = DN_ALPHA * x_ref[...] + f_ref[...]
        mu = jnp.mean(z, axis=-1, keepdims=True)
        zc = z - mu
        var = jnp.mean(zc * zc, axis=-1, keepdims=True)
        rstd = lax.rsqrt(var + LN_EPS)
        xh = zc * rstd
        y = xh * g_ref[...] + b_ref[...]
        y_ref[...] = y
        yb_ref[...] = y.astype(yb_ref.dtype)
        xh_ref[...] = xh
        rs_ref[...] = rstd

    return _pallas(
        body, name=name, grid=(T // tm,),
        in_specs=[_rows(tm, D), _rows(tm, D), _whole((1, D)), _whole((1, D))],
        out_specs=[_rows(tm, D), _rows(tm, D), _rows(tm, D), _rows(tm, 1)],
        out_shape=[jax.ShapeDtypeStruct((T, D), F32), jax.ShapeDtypeStruct((T, D), MXU_DTYPE),
                   jax.ShapeDtypeStruct((T, D), F32), jax.ShapeDtypeStruct((T, 1), F32)],
        compiler_params=_params(("parallel",)),
    )(xres, f, gain, bias)


def _ln_bwd(addends, coefs, xhat, rstd, gain, *, name):
    T, D = xhat.shape
    tm = _pick(T, (256, 128))
    n = len(addends)

    def body(*refs):
        adds = refs[:n]
        xh_ref, rs_ref, g_ref, dz_ref, dzb_ref, dg_ref, db_ref = refs[n:]
        dy = coefs[0] * adds[0][...]
        for c, r in zip(coefs[1:], adds[1:]):
            dy = dy + c * r[...]
        xh = xh_ref[...]
        dxh = dy * g_ref[...]
        m1 = jnp.mean(dxh, axis=-1, keepdims=True)
        m2 = jnp.mean(dxh * xh, axis=-1, keepdims=True)
        dz = rs_ref[...] * (dxh - m1 - xh * m2)
        dz_ref[...] = dz
        dzb_ref[...] = dz.astype(dzb_ref.dtype)

        @pl.when(pl.program_id(0) == 0)
        def _():
            dg_ref[...] = jnp.zeros_like(dg_ref)
            db_ref[...] = jnp.zeros_like(db_ref)

        dg_ref[...] += jnp.sum(dy * xh, axis=0, keepdims=True)
        db_ref[...] += jnp.sum(dy, axis=0, keepdims=True)

    return _pallas(
        body, name=name, grid=(T // tm,),
        in_specs=[_rows(tm, D)] * n + [_rows(tm, D), _rows(tm, 1), _whole((1, D))],
        out_specs=[_rows(tm, D), _rows(tm, D), _whole((1, D)), _whole((1, D))],
        out_shape=[jax.ShapeDtypeStruct((T, D), F32), jax.ShapeDtypeStruct((T, D), MXU_DTYPE),
                   jax.ShapeDtypeStruct((1, D), F32), jax.ShapeDtypeStruct((1, D), F32)],
        compiler_params=_params(("arbitrary",)),
    )(*addends, xhat, rstd, gain)


def _loss_grad(y, tgt, *, name):
    T, D = y.shape
    tm = _pick(T, (256, 128))

    def body(y_ref, t_ref, dy_ref, l_ref):
        e = y_ref[...] - t_ref[...]
        dy_ref[...] = e * (1.0 / D)

        @pl.when(pl.program_id(0) == 0)
        def _():
            l_ref[...] = jnp.zeros_like(l_ref)

        l_ref[...] += jnp.zeros_like(l_ref) + jnp.sum(e * e) * (0.5 / D)

    return _pallas(
        body, name=name, grid=(T // tm,),
        in_specs=[_rows(tm, D), _rows(tm, D)],
        out_specs=[_rows(tm, D), _whole((1, 128))],
        out_shape=[jax.ShapeDtypeStruct((T, D), F32), jax.ShapeDtypeStruct((1, 128), F32)],
        compiler_params=_params(("arbitrary",)),
    )(y, tgt)


def _axpy(a, b, ca, *, name):
    T, D = a.shape
    tm = _pick(T, (256, 128))

    def body(a_ref, b_ref, o_ref):
        o_ref[...] = ca * a_ref[...] + b_ref[...]

    return _pallas(
        body, name=name, grid=(T // tm,), in_specs=[_rows(tm, D), _rows(tm, D)], out_specs=_rows(tm, D),
        out_shape=jax.ShapeDtypeStruct((T, D), F32), compiler_params=_params(("parallel",)),
    )(a, b)


def _glu_gate(y, z, *, name):
    T, D = y.shape
    tm = _pick(T, (256, 128))

    def body(y_ref, z_ref, g_ref):
        g_ref[...] = (_gelu(y_ref[...]) * _sigmoid(z_ref[...])).astype(g_ref.dtype)

    return _pallas(
        body, name=name, grid=(T // tm,), in_specs=[_rows(tm, D), _rows(tm, D)], out_specs=_rows(tm, D),
        out_shape=jax.ShapeDtypeStruct((T, D), MXU_DTYPE), compiler_params=_params(("parallel",)),
    )(y, z)


def _glu_bwd(y, z, dg, *, name):
    T, D = y.shape
    tm = _pick(T, (256, 128))

    def body(y_ref, z_ref, dg_ref, dzb_ref, dyg_ref, db_ref):
        s = _sigmoid(z_ref[...])
        dg = dg_ref[...]
        dz = dg * _gelu(y_ref[...]) * s * (1.0 - s)
        dzb_ref[...] = dz.astype(dzb_ref.dtype)
        dyg_ref[...] = dg * s

        @pl.when(pl.program_id(0) == 0)
        def _():
            db_ref[...] = jnp.zeros_like(db_ref)

        db_ref[...] += jnp.sum(dz, axis=0, keepdims=True)

    return _pallas(
        body, name=name, grid=(T // tm,), in_specs=[_rows(tm, D)] * 3,
        out_specs=[_rows(tm, D), _rows(tm, D), _whole((1, D))],
        out_shape=[jax.ShapeDtypeStruct((T, D), MXU_DTYPE), jax.ShapeDtypeStruct((T, D), F32),
                   jax.ShapeDtypeStruct((1, D), F32)],
        compiler_params=_params(("arbitrary",)),
    )(y, z, dg)


def _gelu_bwd(y, d1, d2, *, name):
    T, D = y.shape
    tm = _pick(T, (256, 128))

    def body(y_ref, a_ref, b_ref, o_ref):
        o_ref[...] = (a_ref[...] + b_ref[...]) * _gelu_grad(y_ref[...])

    return _pallas(
        body, name=name, grid=(T // tm,), in_specs=[_rows(tm, D)] * 3, out_specs=_rows(tm, D),
        out_shape=jax.ShapeDtypeStruct((T, D), F32), compiler_params=_params(("parallel",)),
    )(y, d1, d2)


CONV_ROWS = 128
CONV_EDGE = 16


def _shift_back(x, edge, at_start, tm):
    rows = lax.broadcasted_iota(jnp.int32, x.shape, 0)
    keep = jnp.where(at_start, 0.0, 1.0)
    e7 = edge[CONV_EDGE - 1:CONV_EDGE, :] * keep
    e6 = edge[CONV_EDGE - 2:CONV_EDGE - 1, :] * keep
    r1 = pltpu.roll(x, 1, 0)
    r2 = pltpu.roll(x, 2, 0)
    x1 = jnp.where(rows == 0, e7, r1)
    x2 = jnp.where(rows == 0, e6, jnp.where(rows == 1, e7, r2))
    return x1, x2


def _conv_specs(T, F2, tm):
    return [_rows(tm, F2),
            pl.BlockSpec((CONV_EDGE, F2), lambda i: (jnp.maximum(i * (tm // CONV_EDGE) - 1, 0), 0))]


def _conv_glu_fwd(hc, conv_w, conv_b, L, *, name):
    T, F2 = hc.shape
    F = F2 // 2
    tm = CONV_ROWS

    def body(x_ref, e_ref, w_ref, b_ref, a_ref):
        at_start = (pl.program_id(0) * tm) % L == 0
        x = x_ref[...].astype(F32)
        x1, x2 = _shift_back(x, e_ref[...].astype(F32), at_start, tm)
        c = b_ref[...] + w_ref[0:1, :] * x + w_ref[1:2, :] * x1 + w_ref[2:3, :] * x2
        val, gate = c[:, :F], c[:, F:]
        a_ref[...] = (gate * _sigmoid(gate) * val).astype(a_ref.dtype)

    return _pallas(
        body, name=name, grid=(T // tm,),
        in_specs=_conv_specs(T, F2, tm) + [_whole((3, F2)), _whole((1, F2))],
        out_specs=_rows(tm, F),
        out_shape=jax.ShapeDtypeStruct((T, F), MXU_DTYPE), compiler_params=_params(("parallel",)),
    )(hc, hc, conv_w, conv_b)


def _conv_glu_bwd(hc, da, conv_w, conv_b, L, *, name):
    T, F2 = hc.shape
    F = F2 // 2
    tm = CONV_ROWS

    def body(x_ref, e_ref, da_ref, w_ref, b_ref, dc_ref, dw_ref, db_ref):
        at_start = (pl.program_id(0) * tm) % L == 0
        x = x_ref[...].astype(F32)
        x1, x2 = _shift_back(x, e_ref[...].astype(F32), at_start, tm)
        c = b_ref[...] + w_ref[0:1, :] * x + w_ref[1:2, :] * x1 + w_ref[2:3, :] * x2
        val, gate = c[:, :F], c[:, F:]
        s = _sigmoid(gate)
        da = da_ref[...].astype(F32)
        dval = da * (gate * s)
        dgate = da * val * (s * (1.0 + gate * (1.0 - s)))
        dc = jnp.concatenate([dval, dgate], axis=-1)
        dc_ref[...] = dc.astype(dc_ref.dtype)

        @pl.when(pl.program_id(0) == 0)
        def _():
            dw_ref[...] = jnp.zeros_like(dw_ref)
            db_ref[...] = jnp.zeros_like(db_ref)

        dw_ref[0:1, :] += jnp.sum(dc * x, axis=0, keepdims=True)
        dw_ref[1:2, :] += jnp.sum(dc * x1, axis=0, keepdims=True)
        dw_ref[2:3, :] += jnp.sum(dc * x2, axis=0, keepdims=True)
        db_ref[...] += jnp.sum(dc, axis=0, keepdims=True)

    return _pallas(
        body, name=name, grid=(T // tm,),
        in_specs=_conv_specs(T, F2, tm) + [_rows(tm, F), _whole((3, F2)), _whole((1, F2))],
        out_specs=[_rows(tm, F2), _whole((3, F2)), _whole((1, F2))],
        out_shape=[jax.ShapeDtypeStruct((T, F2), MXU_DTYPE), jax.ShapeDtypeStruct((3, F2), F32),
                   jax.ShapeDtypeStruct((1, F2), F32)],
        compiler_params=_params(("arbitrary",)),
    )(hc, hc, da, conv_w, conv_b)


def _conv_bwd_input(dc, conv_w, L, *, name):
    T, F2 = dc.shape
    tm = CONV_ROWS
    edge = CONV_EDGE
    last_blk = T // edge - 1

    def body(x_ref, e_ref, w_ref, o_ref):
        at_end = ((pl.program_id(0) + 1) * tm) % L == 0
        x = x_ref[...].astype(F32)
        rows = lax.broadcasted_iota(jnp.int32, x.shape, 0)
        keep = jnp.where(at_end, 0.0, 1.0)
        ev = e_ref[...].astype(F32)
        e0 = ev[0:1, :] * keep
        e1 = ev[1:2, :] * keep
        u1 = pltpu.roll(x, tm - 1, 0)
        u2 = pltpu.roll(x, tm - 2, 0)
        x1 = jnp.where(rows == tm - 1, e0, u1)
        x2 = jnp.where(rows == tm - 1, e1, jnp.where(rows == tm - 2, e0, u2))
        o_ref[...] = (w_ref[0:1, :] * x + w_ref[1:2, :] * x1 + w_ref[2:3, :] * x2).astype(o_ref.dtype)

    return _pallas(
        body, name=name, grid=(T // tm,),
        in_specs=[_rows(tm, F2),
                  pl.BlockSpec((edge, F2), lambda i: (jnp.minimum((i + 1) * (tm // edge), last_blk), 0)),
                  _whole((3, F2))],
        out_specs=_rows(tm, F2),
        out_shape=jax.ShapeDtypeStruct((T, F2), MXU_DTYPE), compiler_params=_params(("parallel",)),
    )(dc, dc, conv_w)


S5_CHUNK = 128
LANES = 128


def _slab_rows(c, n, ncl):
    return pl.ds(c, n) if ncl == 1 else pl.ds(c, n, stride=ncl)


def _slab_put(ref, c, n, ncl, val):
    for s in range(val.shape[1] // LANES):
        ref[s, _slab_rows(c, n, ncl), :] = val[:, s * LANES:(s + 1) * LANES]


def _slab_get(ref, c, n, ncl):
    return jnp.concatenate([ref[s, _slab_rows(c, n, ncl), :] for s in range(ref.shape[0])], axis=-1)


def _slabs(n_slab, rows):
    return pl.BlockSpec((n_slab, rows, LANES), lambda i: (0, i, 0))


def _s5_fwd(xi, wb, wc, a_r, a_i, d_row, B, *, name):
    T, D = xi.shape
    ncl = wb.shape[0]
    cs = wb.shape[2] // 2
    ns = cs // LANES
    R = B * ncl
    Q = S5_CHUNK
    QR = Q * ncl
    nsteps = Q // B

    def body(x_ref, wb_ref, wc_ref, ar_ref, ai_ref, d_ref, y_ref, yg_ref, hr_ref, hi_ref, bur, bui, cr, ci):
        @pl.when(pl.program_id(0) == 0)
        def _():
            cr[...] = jnp.zeros_like(cr)
            ci[...] = jnp.zeros_like(ci)

        x = x_ref[...]
        xb = x.astype(MXU_DTYPE)
        for c in range(ncl):
            bu = jnp.dot(xb[:, c * CLUSTER_W:(c + 1) * CLUSTER_W], wb_ref[c], preferred_element_type=F32)
            _slab_put(bur, c, Q, ncl, bu[:, :cs])
            _slab_put(bui, c, Q, ncl, bu[:, cs:])
        ar = ar_ref[...]
        ai = ai_ref[...]

        def step(k, carry):
            hr, hi = carry
            sl = pl.ds(pl.multiple_of(k * R, R), R)
            nr = ar * hr - ai * hi + bur[:, sl, :]
            ni = ar * hi + ai * hr + bui[:, sl, :]
            hr_ref[:, sl, :] = nr
            hi_ref[:, sl, :] = ni
            return nr, ni

        hr, hi = lax.fori_loop(0, nsteps, step, (cr[...], ci[...]), unroll=4)
        cr[...] = hr
        ci[...] = hi
        parts = []
        for c in range(ncl):
            hrc = _slab_get(hr_ref, c, Q, ncl).astype(MXU_DTYPE)
            hic = _slab_get(hi_ref, c, Q, ncl).astype(MXU_DTYPE)
            parts.append(jnp.dot(hrc, wc_ref[c, :cs, :], preferred_element_type=F32)
                         + jnp.dot(hic, wc_ref[c, cs:, :], preferred_element_type=F32))
        y = d_ref[...] * x + (parts[0] if ncl == 1 else jnp.concatenate(parts, axis=-1))
        y_ref[...] = y
        yg_ref[...] = _gelu(y).astype(yg_ref.dtype)

    return _pallas(
        body, name=name, grid=(T // Q,),
        in_specs=[_rows(Q, D), _whole(wb.shape), _whole(wc.shape), _whole((ns, R, LANES)), _whole((ns, R, LANES)),
                  _whole((1, D))],
        out_specs=[_rows(Q, D), _rows(Q, D), _slabs(ns, QR), _slabs(ns, QR)],
        out_shape=[jax.ShapeDtypeStruct((T, D), F32), jax.ShapeDtypeStruct((T, D), MXU_DTYPE),
                   jax.ShapeDtypeStruct((ns, T * ncl, LANES), F32), jax.ShapeDtypeStruct((ns, T * ncl, LANES), F32)],
        scratch_shapes=[pltpu.VMEM((ns, QR, LANES), F32), pltpu.VMEM((ns, QR, LANES), F32),
                        pltpu.VMEM((ns, R, LANES), F32), pltpu.VMEM((ns, R, LANES), F32)],
        compiler_params=_params(("arbitrary",)),
    )(xi, wb, wc, a_r, a_i, d_row)


def _s5_bwd(dy, xi, h_r, h_i, wb, wc, a_r, a_i, d_row, B, *, name):
    T, D = dy.shape
    ncl = wb.shape[0]
    cs = wb.shape[2] // 2
    ns = cs // LANES
    R = B * ncl
    Q = S5_CHUNK
    nsteps = Q // B
    nchunk = T // Q
    QR = Q * ncl

    def rev(i):
        return nchunk - 1 - i

    def body(dy_ref, x_ref, hr_ref, hi_ref, pr_ref, pi_ref, wb_ref, wc_ref, ar_ref, ai_ref, d_ref,
             du_ref, gr_ref, gi_ref, dar_ref, dai_ref, dd_ref, dhr, dhi, cr, ci):
        i = pl.program_id(0)

        @pl.when(i == 0)
        def _():
            cr[...] = jnp.zeros_like(cr)
            ci[...] = jnp.zeros_like(ci)
            dar_ref[...] = jnp.zeros_like(dar_ref)
            dai_ref[...] = jnp.zeros_like(dai_ref)
            dd_ref[...] = jnp.zeros_like(dd_ref)

        dyv = dy_ref[...]
        dyb = dyv.astype(MXU_DTYPE)
        for c in range(ncl):
            dh = lax.dot_general(dyb[:, c * CLUSTER_W:(c + 1) * CLUSTER_W], wc_ref[c],
                                 (((1,), (1,)), ((), ())), preferred_element_type=F32)
            _slab_put(dhr, c, Q, ncl, dh[:, :cs])
            _slab_put(dhi, c, Q, ncl, dh[:, cs:])
        ar = ar_ref[...]
        ai = ai_ref[...]

        def step(j, carry):
            gr, gi, sar, sai = carry
            k = nsteps - 1 - j
            sl = pl.ds(pl.multiple_of(k * R, R), R)
            ngr = dhr[:, sl, :] + ar * gr + ai * gi
            ngi = dhi[:, sl, :] - ai * gr + ar * gi
            gr_ref[:, sl, :] = ngr
            gi_ref[:, sl, :] = ngi
            pv = pl.ds(pl.multiple_of((k - 1) * R, R), R)
            hpr = hr_ref[:, pv, :]
            hpi = hi_ref[:, pv, :]
            return ngr, ngi, sar + ngr * hpr + ngi * hpi, sai - ngr * hpi + ngi * hpr

        gr, gi, sar, sai = lax.fori_loop(0, nsteps - 1, step, (cr[...], ci[...], dar_ref[...], dai_ref[...]), unroll=4)
        sl0 = pl.ds(0, R)
        ngr = dhr[:, sl0, :] + ar * gr + ai * gi
        ngi = dhi[:, sl0, :] - ai * gr + ar * gi
        gr_ref[:, sl0, :] = ngr
        gi_ref[:, sl0, :] = ngi
        keep = jnp.where(i == nchunk - 1, 0.0, 1.0)
        hpr = pr_ref[:, 8 - R:8, :] * keep
        hpi = pi_ref[:, 8 - R:8, :] * keep
        dar_ref[...] = sar + ngr * hpr + ngi * hpi
        dai_ref[...] = sai - ngr * hpi + ngi * hpr
        cr[...] = ngr
        ci[...] = ngi
        parts = []
        for c in range(ncl):
            grc = _slab_get(gr_ref, c, Q, ncl).astype(MXU_DTYPE)
            gic = _slab_get(gi_ref, c, Q, ncl).astype(MXU_DTYPE)
            parts.append(lax.dot_general(grc, wb_ref[c, :, :cs], (((1,), (1,)), ((), ())), preferred_element_type=F32)
                         + lax.dot_general(gic, wb_ref[c, :, cs:], (((1,), (1,)), ((), ())), preferred_element_type=F32))
        du_ref[...] = d_ref[...] * dyv + (parts[0] if ncl == 1 else jnp.concatenate(parts, axis=-1))
        dd_ref[...] += jnp.sum(dyv * x_ref[...], axis=0, keepdims=True)

    tok = pl.BlockSpec((Q, D), lambda i: (rev(i), 0))
    st = pl.BlockSpec((ns, QR, LANES), lambda i: (0, rev(i), 0))
    before = pl.BlockSpec((ns, 8, LANES), lambda i: (0, jnp.maximum(rev(i) * (QR // 8) - 1, 0), 0))
    acc = _whole((ns, R, LANES))
    return _pallas(
        body, name=name, grid=(nchunk,),
        in_specs=[tok, tok, st, st, before, before, _whole(wb.shape), _whole(wc.shape), acc, acc, _whole((1, D))],
        out_specs=[tok, st, st, acc, acc, _whole((1, D))],
        out_shape=[jax.ShapeDtypeStruct((T, D), F32),
                   jax.ShapeDtypeStruct((ns, T * ncl, LANES), F32), jax.ShapeDtypeStruct((ns, T * ncl, LANES), F32),
                   jax.ShapeDtypeStruct((ns, R, LANES), F32), jax.ShapeDtypeStruct((ns, R, LANES), F32),
                   jax.ShapeDtypeStruct((1, D), F32)],
        scratch_shapes=[pltpu.VMEM((ns, QR, LANES), F32)] * 2 + [pltpu.VMEM((ns, R, LANES), F32)] * 2,
        compiler_params=_params(("arbitrary",)),
    )(dy, xi, h_r, h_i, h_r, h_i, wb, wc, a_r, a_i, d_row)


def _cluster_tn(tok, st, ncl, *, tok_left, name):
    T = tok.shape[0]
    ns = st.shape[0]
    cs = ns * LANES
    tt = _pick(T, (512, 256, 128))
    nt = T // tt
    oshape = (ncl, CLUSTER_W, cs) if tok_left else (ncl, cs, CLUSTER_W)

    def body(tok_ref, st_ref, o_ref, acc):
        t = pl.program_id(0)

        @pl.when(t == 0)
        def _():
            acc[...] = jnp.zeros_like(acc)

        tk = tok_ref[...].astype(MXU_DTYPE)
        for c in range(ncl):
            tc = tk[:, c * CLUSTER_W:(c + 1) * CLUSTER_W]
            sc = _slab_get(st_ref, c, tt, ncl).astype(MXU_DTYPE)
            lhs, rhs = (tc, sc) if tok_left else (sc, tc)
            acc[c] += lax.dot_general(lhs, rhs, (((0,), (0,)), ((), ())), preferred_element_type=F32)

        @pl.when(t == nt - 1)
        def _():
            o_ref[...] = acc[...]

    return _pallas(
        body, name=name, grid=(nt,),
        in_specs=[_rows(tt, tok.shape[1]), _slabs(ns, tt * ncl)],
        out_specs=_whole(oshape),
        out_shape=jax.ShapeDtypeStruct(oshape, F32),
        scratch_shapes=[pltpu.VMEM(oshape, F32)],
        compiler_params=_params(("arbitrary",)),
    )(tok, st)


def _s5_discretize(lam_re, lam_im, log_dt, b_re, b_im):
    dt = jnp.exp(log_dt)[:, None]
    mag = jnp.exp(lam_re * dt)
    ab_r, ab_i = mag * jnp.cos(lam_im * dt), mag * jnp.sin(lam_im * dt)
    den = lam_re * lam_re + lam_im * lam_im
    nr = ab_r - 1.0
    co_r = (nr * lam_re + ab_i * lam_im) / den
    co_i = (ab_i * lam_re - nr * lam_im) / den
    bb_r = co_r[..., None] * b_re - co_i[..., None] * b_im
    bb_i = co_r[..., None] * b_im + co_i[..., None] * b_re
    return ab_r, ab_i, bb_r, bb_i


def _blockdiag(m):
    G, r, k = m.shape
    ncl = G // GROUPS_PER_CLUSTER
    m4 = m.reshape(ncl, GROUPS_PER_CLUSTER, r, k)
    eye = jnp.eye(GROUPS_PER_CLUSTER, dtype=m.dtype)
    return jnp.einsum('cgrk,gh->cgrhk', m4, eye).reshape(ncl, GROUPS_PER_CLUSTER * r, GROUPS_PER_CLUSTER * k)


def _unblockdiag(m, r, k):
    ncl = m.shape[0]
    m5 = m.reshape(ncl, GROUPS_PER_CLUSTER, r, GROUPS_PER_CLUSTER, k)
    eye = jnp.eye(GROUPS_PER_CLUSTER, dtype=m.dtype)
    return jnp.einsum('cgrhk,gh->cgrk', m5, eye).reshape(ncl * GROUPS_PER_CLUSTER, r, k)


def _t5_bucket(dist):
    exact = REL_BUCKETS // 2
    d = np.maximum(dist, 1).astype(np.float32)
    large = exact + (np.log(d / exact) / math.log(REL_MAX_DIST / exact) * (REL_BUCKETS - exact)).astype(np.int64)
    large = np.minimum(large, REL_BUCKETS - 1)
    return np.where(dist < exact, dist, large).astype(np.int32)


def _band_tables(dil):
    steps = np.arange(BAND)[:, None] + BAND - np.arange(2 * BAND)[None, :]
    bucket = _t5_bucket(np.maximum(steps, 0) * dil)
    in_band = (steps >= 0) & (steps <= BAND)
    return bucket, in_band


def _attn_bias(rel_bias, hpg):
    out = []
    for g, dil in enumerate(DILATIONS):
        bucket, in_band = _band_tables(dil)
        cols = rel_bias[:, g * hpg:(g + 1) * hpg].astype(F32)
        onehot = jnp.asarray((bucket.reshape(-1, 1) == np.arange(REL_BUCKETS)[None, :]).astype(np.float32))
        bias = jnp.dot(onehot, cols, precision=lax.Precision.HIGHEST).T.reshape(hpg, BAND, 2 * BAND)
        out.append(jnp.where(jnp.asarray(in_band)[None], bias, NEG_BIG))
    return jnp.concatenate(out, axis=0)


def _attn_blocks(dil, L):
    M = L // dil
    return M, M // BAND


def _row_sel(r, M, dil):
    return pl.ds(r, M) if dil == 1 else pl.ds(r, M, stride=dil)


def _attn_fwd(q, kv, bias, L, hpg, *, name):
    T = q.shape[0]
    nb_ = T // L
    HP = hpg // 2
    W3 = 3 * hpg * HEAD_DIM
    mmax = L

    def group_body(dil, q_ref, k_ref, v_ref, b_ref, o_ref, l_ref, os, ls):
        M, NB = _attn_blocks(dil, L)
        for r in range(dil):
            rows = _row_sel(r, M, dil)
            qr = (q_ref[rows, :] * 0.125).astype(MXU_DTYPE)
            kr = k_ref[rows, :].astype(MXU_DTYPE)
            vr = v_ref[rows, :].astype(MXU_DTYPE)
            ones = jnp.ones((M, HEAD_DIM), MXU_DTYPE)
            va = [jnp.concatenate([vr[:, hh * HEAD_DIM:(hh + 1) * HEAD_DIM], ones], axis=-1) for hh in range(2)]
            for n in range(NB):
                qs = slice(n * BAND, (n + 1) * BAND)
                ks = slice(0, BAND) if n == 0 else slice((n - 1) * BAND, (n + 1) * BAND)
                for hh in range(2):
                    ln = slice(hh * HEAD_DIM, (hh + 1) * HEAD_DIM)
                    bb = b_ref[hh, :, BAND:] if n == 0 else b_ref[hh]
                    s = lax.dot_general(qr[qs, ln], kr[ks, ln], (((1,), (1,)), ((), ())),
                                        preferred_element_type=F32) + bb
                    m = jnp.max(s, axis=-1, keepdims=True)
                    p = jnp.exp(s - m)
                    pv = jnp.dot(p.astype(MXU_DTYPE), va[hh][ks, :], preferred_element_type=F32)
                    l = pv[:, HEAD_DIM:]
                    os[qs, ln] = pv[:, :HEAD_DIM] / l
                    ls[qs, ln] = m + jnp.log(l)
            o_ref[rows, :] = os[0:M, :]
            l_ref[rows, :] = ls[0:M, :]

    def body(q_ref, k_ref, v_ref, b_ref, o_ref, l_ref, os, ls):
        g = pl.program_id(0)
        for gi, dil in enumerate(DILATIONS):
            pl.when(g == gi)(functools.partial(group_body, dil, q_ref, k_ref, v_ref, b_ref, o_ref, l_ref, os, ls))

    blk = (L, 2 * HEAD_DIM)
    return _pallas(
        body, name=name, grid=(3, nb_, HP),
        in_specs=[pl.BlockSpec(blk, lambda g, b, h: (b, g * HP + h)),
                  pl.BlockSpec(blk, lambda g, b, h: (b, g * HP + h)),
                  pl.BlockSpec(blk, lambda g, b, h: (b, 3 * HP + g * HP + h)),
                  pl.BlockSpec((2, BAND, 2 * BAND), lambda g, b, h: (g * HP + h, 0, 0))],
        out_specs=[pl.BlockSpec(blk, lambda g, b, h: (b, g * HP + h)),
                   pl.BlockSpec(blk, lambda g, b, h: (b, g * HP + h))],
        out_shape=[jax.ShapeDtypeStruct((T, W3), F32), jax.ShapeDtypeStruct((T, W3), F32)],
        scratch_shapes=[pltpu.VMEM((mmax, 2 * HEAD_DIM), F32), pltpu.VMEM((mmax, 2 * HEAD_DIM), F32)],
        compiler_params=_params(("arbitrary", "arbitrary", "arbitrary")),
    )(q, kv, kv, bias)


def _attn_merge(o3, l3, hw, *, name):
    T = o3.shape[0]
    tm = _pick(T, (256, 128))

    def body(o0, o1, o2, l0, l1, l2, o_ref, ob_ref, lse_ref):
        a0, a1, a2 = l0[...], l1[...], l2[...]
        m = jnp.maximum(jnp.maximum(a0, a1), a2)
        e0, e1, e2 = jnp.exp(a0 - m), jnp.exp(a1 - m), jnp.exp(a2 - m)
        z = e0 + e1 + e2
        o = (e0 * o0[...] + e1 * o1[...] + e2 * o2[...]) / z
        o_ref[...] = o
        ob_ref[...] = o.astype(ob_ref.dtype)
        lse_ref[...] = m + jnp.log(z)

    def col(g):
        return pl.BlockSpec((tm, hw), lambda i: (i, g))

    return _pallas(
        body, name=name, grid=(T // tm,),
        in_specs=[col(0), col(1), col(2), col(0), col(1), col(2)],
        out_specs=[_rows(tm, hw)] * 3,
        out_shape=[jax.ShapeDtypeStruct((T, hw), F32), jax.ShapeDtypeStruct((T, hw), MXU_DTYPE),
                   jax.ShapeDtypeStruct((T, hw), F32)],
        compiler_params=_params(("parallel",)),
    )(o3, o3, o3, l3, l3, l3)


def _attn_bwd(q, kv, do, o, lse, bias, L, hpg, *, name):
    T = q.shape[0]
    nb_ = T // L
    HP = hpg // 2
    W3 = 3 * hpg * HEAD_DIM
    mmax = L

    def group_body(dil, q_ref, k_ref, v_ref, do_ref, o_ref, l_ref, b_ref, dq_ref, dk_ref, dv_ref, ds_ref,
                   dqs, dks, dvs):
        M, NB = _attn_blocks(dil, L)
        for r in range(dil):
            rows = _row_sel(r, M, dil)
            qr = (q_ref[rows, :] * 0.125).astype(MXU_DTYPE)
            kr = k_ref[rows, :].astype(MXU_DTYPE)
            vr = v_ref[rows, :].astype(MXU_DTYPE)
            dor = do_ref[rows, :]
            orr = o_ref[rows, :]
            lr = l_ref[rows, :]
            dks[0:M, :] = jnp.zeros((M, 2 * HEAD_DIM), F32)
            dvs[0:M, :] = jnp.zeros((M, 2 * HEAD_DIM), F32)
            for n in range(NB):
                qs = slice(n * BAND, (n + 1) * BAND)
                ks = slice(0, BAND) if n == 0 else slice((n - 1) * BAND, (n + 1) * BAND)
                for hh in range(2):
                    ln = slice(hh * HEAD_DIM, (hh + 1) * HEAD_DIM)
                    bb = b_ref[hh, :, BAND:] if n == 0 else b_ref[hh]
                    qb, kb, vb = qr[qs, ln], kr[ks, ln], vr[ks, ln]
                    dob = dor[qs, ln]
                    s = lax.dot_general(qb, kb, (((1,), (1,)), ((), ())), preferred_element_type=F32) + bb
                    p = jnp.exp(s - lr[qs, hh * HEAD_DIM:hh * HEAD_DIM + 1])
                    dobm = dob.astype(MXU_DTYPE)
                    dp = lax.dot_general(dobm, vb, (((1,), (1,)), ((), ())), preferred_element_type=F32)
                    delta = jnp.sum(dob * orr[qs, ln], axis=-1, keepdims=True)
                    ds = p * (dp - delta)
                    if n == 0:
                        ds_ref[hh, :, BAND:] += ds
                    else:
                        ds_ref[hh] += ds
                    dsm = ds.astype(MXU_DTYPE)
                    dqs[qs, ln] = jnp.dot(dsm, kb, preferred_element_type=F32) * 0.125
                    dks[ks, ln] += lax.dot_general(dsm, qb, (((0,), (0,)), ((), ())), preferred_element_type=F32)
                    dvs[ks, ln] += lax.dot_general(p.astype(MXU_DTYPE), dobm, (((0,), (0,)), ((), ())),
                                                   preferred_element_type=F32)
            dq_ref[rows, :] = dqs[0:M, :]
            dk_ref[rows, :] = dks[0:M, :]
            dv_ref[rows, :] = dvs[0:M, :]

    def body(q_ref, k_ref, v_ref, do_ref, o_ref, l_ref, b_ref, dq_ref, dk_ref, dv_ref, ds_ref, dqs, dks, dvs):
        g = pl.program_id(0)

        @pl.when(pl.program_id(2) == 0)
        def _():
            ds_ref[...] = jnp.zeros_like(ds_ref)

        for gi, dil in enumerate(DILATIONS):
            pl.when(g == gi)(functools.partial(group_body, dil, q_ref, k_ref, v_ref, do_ref, o_ref, l_ref, b_ref,
                                               dq_ref, dk_ref, dv_ref, ds_ref, dqs, dks, dvs))

    blk = (L, 2 * HEAD_DIM)
    gcol = lambda g, h, b: (b, g * HP + h)
    hcol = lambda g, h, b: (b, h)
    return _pallas(
        body, name=name, grid=(3, HP, nb_),
        in_specs=[pl.BlockSpec(blk, gcol), pl.BlockSpec(blk, gcol),
                  pl.BlockSpec(blk, lambda g, h, b: (b, 3 * HP + g * HP + h)),
                  pl.BlockSpec(blk, hcol), pl.BlockSpec(blk, hcol), pl.BlockSpec(blk, hcol),
                  pl.BlockSpec((2, BAND, 2 * BAND), lambda g, h, b: (g * HP + h, 0, 0))],
        out_specs=[pl.BlockSpec(blk, gcol), pl.BlockSpec(blk, gcol), pl.BlockSpec(blk, gcol),
                   pl.BlockSpec((2, BAND, 2 * BAND), lambda g, h, b: (g * HP + h, 0, 0))],
        out_shape=[jax.ShapeDtypeStruct((T, W3), F32), jax.ShapeDtypeStruct((T, W3), F32),
                   jax.ShapeDtypeStruct((T, W3), F32), jax.ShapeDtypeStruct((3 * hpg, BAND, 2 * BAND), F32)],
        scratch_shapes=[pltpu.VMEM((mmax, 2 * HEAD_DIM), F32)] * 3,
        compiler_params=_params(("arbitrary", "arbitrary", "arbitrary")),
    )(q, kv, kv, do, o, lse, bias)


def _bias_grad(ds_sum, hpg, *, name):
    nh = ds_sum.shape[0]
    idx = np.stack([np.where(_band_tables(dil)[1], _band_tables(dil)[0], -1) for dil in DILATIONS]).astype(np.int32)

    def body(ds_ref, idx_ref, o_ref):
        d = ds_ref[...]
        ix = idx_ref[...]
        lane = lax.broadcasted_iota(jnp.int32, (8, 128), 1)
        row = jnp.zeros((8, 128), F32)
        for b in range(REL_BUCKETS):
            row = row + jnp.where(lane == b, jnp.sum(jnp.where(ix == b, d, 0.0)), 0.0)
        o_ref[...] = row

    out = _pallas(
        body, name=name, grid=(nh,),
        in_specs=[pl.BlockSpec((None, BAND, 2 * BAND), lambda h: (h, 0, 0)),
                  pl.BlockSpec((None, BAND, 2 * BAND), lambda h: (h // hpg, 0, 0))],
        out_specs=pl.BlockSpec((None, 8, 128), lambda h: (h, 0, 0)),
        out_shape=jax.ShapeDtypeStruct((nh, 8, 128), F32),
        compiler_params=_params(("parallel",)),
    )(ds_sum, jnp.asarray(idx))
    return out[:, 0, :REL_BUCKETS].T


def _adamw(w, g, m, v, *, name):
    Rw, C = w.shape
    tm = _pick(Rw, (512, 352, 256, 128, 64, 32, 16, 8))

    def body(w_ref, g_ref, m_ref, v_ref, d_ref, nm_ref, nv_ref):
        gg = g_ref[...]
        nm = ADAM_B1 * m_ref[...] + (1.0 - ADAM_B1) * gg
        nv = ADAM_B2 * v_ref[...] + (1.0 - ADAM_B2) * (gg * gg)
        m_hat = nm / (1.0 - ADAM_B1 ** ADAM_STEP)
        v_hat = nv / (1.0 - ADAM_B2 ** ADAM_STEP)
        d_ref[...] = -ADAM_LR * (m_hat / (jnp.sqrt(v_hat) + ADAM_EPS) + ADAM_WD * w_ref[...])
        nm_ref[...] = nm
        nv_ref[...] = nv

    return _pallas(
        body, name=name, grid=(Rw // tm,), in_specs=[_rows(tm, C)] * 4, out_specs=[_rows(tm, C)] * 3,
        out_shape=[jax.ShapeDtypeStruct((Rw, C), F32)] * 3, compiler_params=_params(("parallel",)),
    )(w, g, m, v)


ROW_TILE_ELEMS = 256 * 1024


def _tile_rows(r, c):
    best = 8
    for t in range(8, r + 1, 8):
        if r % t == 0 and t * c <= ROW_TILE_ELEMS:
            best = t
    return best


def _adamw_halves(w, m, v, mine, other, cidx, *, name):
    _, r, c = w.shape
    tm = _tile_rows(r, c)

    def body(c_ref, w_ref, m_ref, v_ref, a_ref, b_ref, g_ref, d_ref, nm_ref, nv_ref):
        gg = jnp.where(pl.program_id(0) == c_ref[0], a_ref[...], b_ref[...])
        nm = ADAM_B1 * m_ref[...] + (1.0 - ADAM_B1) * gg
        nv = ADAM_B2 * v_ref[...] + (1.0 - ADAM_B2) * (gg * gg)
        m_hat = nm / (1.0 - ADAM_B1 ** ADAM_STEP)
        v_hat = nv / (1.0 - ADAM_B2 ** ADAM_STEP)
        g_ref[...] = gg
        d_ref[...] = -ADAM_LR * (m_hat / (jnp.sqrt(v_hat) + ADAM_EPS) + ADAM_WD * w_ref[...])
        nm_ref[...] = nm
        nv_ref[...] = nv

    half = pl.BlockSpec((None, tm, c), lambda h, i, cr: (h, i, 0))
    one = pl.BlockSpec((None, tm, c), lambda h, i, cr: (0, i, 0))
    spec = pltpu.PrefetchScalarGridSpec(num_scalar_prefetch=1, grid=(2, r // tm),
                                        in_specs=[half, half, half, one, one], out_specs=[half] * 4)
    return _pallas(
        body, name=name, grid_spec=spec, out_shape=[jax.ShapeDtypeStruct((2, r, c), F32)] * 4,
        compiler_params=_params(("parallel", "parallel")),
    )(cidx, w, m, v, mine, other)


def _pair_sum(g, theirs, cidx, *, cast, name):
    _, _, r, c = g.shape
    tm = _tile_rows(r, c)

    def body(c_ref, g_ref, t_ref, *outs):
        s = g_ref[...] + t_ref[...]
        outs[0][...] = s
        if cast:
            outs[1][...] = s.astype(BF16)

    blk = (None, None, tm, c)
    first = pl.BlockSpec(blk, lambda p, i, cr: (p, 0, i, 0))
    shapes = [jax.ShapeDtypeStruct((4, 1, r, c), F32)] + ([jax.ShapeDtypeStruct((4, 1, r, c), BF16)] if cast else [])
    spec = pltpu.PrefetchScalarGridSpec(
        num_scalar_prefetch=1, grid=(4, r // tm),
        in_specs=[pl.BlockSpec(blk, lambda p, i, cr: (p, cr[0], i, 0)), first], out_specs=[first] * len(shapes))
    return _pallas(body, name=name, grid_spec=spec, out_shape=shapes,
                   compiler_params=_params(("parallel", "parallel")))(cidx, g, theirs)


def _chip_sum(hf, got, chip_idx, *, name):
    _, _, r, c = hf.shape
    tm = _tile_rows(r, c)

    def body(p_ref, h_ref, r_ref, o_ref):
        s = h_ref[...]
        for k in range(3):
            s = s + r_ref[k].astype(F32)
        o_ref[...] = s

    spec = pltpu.PrefetchScalarGridSpec(
        num_scalar_prefetch=1, grid=(r // tm,),
        in_specs=[pl.BlockSpec((None, None, tm, c), lambda i, pr: (pr[0], 0, i, 0)),
                  pl.BlockSpec((3, None, tm, c), lambda i, pr: (0, 0, i, 0))],
        out_specs=pl.BlockSpec((None, tm, c), lambda i, pr: (0, i, 0)))
    return _pallas(body, name=name, grid_spec=spec, out_shape=jax.ShapeDtypeStruct((1, r, c), F32),
                   compiler_params=_params(("parallel",)))(chip_idx, hf, got)


def _place():
    x, y, c = lax.axis_index("x"), lax.axis_index("y"), lax.axis_index("c")
    chips = [(1 - x, y), (x, 1 - y), (1 - x, 1 - y)]
    return x, y, c, chips


_ANY = pl.BlockSpec(memory_space=pl.ANY)


def _comm_call(body, ins, out_shapes, n_remote, *, name):
    sems = [pltpu.SemaphoreType.DMA((n,)) for n in n_remote]
    return _pallas(
        body, name=name, in_specs=[_ANY] * len(ins), out_specs=[_ANY] * len(out_shapes), out_shape=out_shapes,
        scratch_shapes=sems, compiler_params=pltpu.CompilerParams(has_side_effects=True),
    )(*ins)


def _rcopy(src, dst, ssem, rsem, dev):
    return pltpu.make_async_remote_copy(src_ref=src, dst_ref=dst, send_sem=ssem, recv_sem=rsem,
                                        device_id=dev, device_id_type=MESH)


def _all_gather(shards, *, name):
    n = len(shards)

    def body(*refs):
        ins, outs = refs[:n], refs[n:2 * n]
        s_ici, r_ici, s_d2d, r_d2d = refs[2 * n:]
        x, y, c, chips = _place()
        me = 2 * x + y
        sib = (x, y, 1 - c)
        sends = []
        for a in range(n):
            for k, (tx, ty) in enumerate(chips):
                cp = _rcopy(ins[a].at[c], outs[a].at[me, c], s_ici.at[3 * a + k], r_ici.at[3 * a + k], (tx, ty, c))
                cp.start()
                sends.append(cp)
        for a in range(n):
            for k, (tx, ty) in enumerate(chips):
                pk = 2 * tx + ty
                _rcopy(ins[a].at[c], outs[a].at[pk, c], s_ici.at[3 * a + k], r_ici.at[3 * a + k], (tx, ty, c)).wait_recv()
                fw = _rcopy(outs[a].at[pk, c], outs[a].at[pk, c], s_d2d.at[3 * a + k], r_d2d.at[3 * a + k], sib)
                fw.start()
                sends.append(fw)
        for a in range(n):
            for k, (tx, ty) in enumerate(chips):
                pk = 2 * tx + ty
                _rcopy(ins[a].at[c], outs[a].at[pk, 1 - c], s_d2d.at[3 * a + k], r_d2d.at[3 * a + k], sib).wait_recv()
        for cp in sends:
            cp.wait_send()

    shapes = [jax.ShapeDtypeStruct((4,) + s.shape, s.dtype) for s in shards]
    return _comm_call(body, shards, shapes, [3 * n] * 4, name=name)


def _gather(shards, chip, *, name):
    outs = _all_gather(shards, name=name)
    return [lax.dynamic_update_slice(o, s[None], (chip, 0, 0, 0)) for o, s in zip(outs, shards)]


def _pair_send(gs, *, name):
    n = len(gs)

    def body(*refs):
        ins, theirs = refs[:n], refs[n:2 * n]
        ssem, rsem = refs[2 * n:]
        x, y, c, _ = _place()
        sib = (x, y, 1 - c)
        cps = []
        for a in range(n):
            cp = _rcopy(ins[a].at[:, pl.ds(1 - c, 1)], theirs[a], ssem.at[a], rsem.at[a], sib)
            cp.start()
            cps.append(cp)
        for cp in cps:
            cp.wait_send()
            cp.wait_recv()

    shapes = [jax.ShapeDtypeStruct((4, 1) + g.shape[2:], g.dtype) for g in gs]
    return _comm_call(body, gs, shapes, [n, n], name=name)


def _chip_exchange(hx, *, name):
    n = len(hx)

    def body(*refs):
        hxr, got = refs[:n], refs[n:2 * n]
        ssem, rsem = refs[2 * n:]
        x, y, c, chips = _place()
        cps = []
        for a in range(n):
            for k, (tx, ty) in enumerate(chips):
                cp = _rcopy(hxr[a].at[2 * tx + ty], got[a].at[k], ssem.at[3 * a + k], rsem.at[3 * a + k], (tx, ty, c))
                cp.start()
                cps.append(cp)
        for cp in cps:
            cp.wait_send()
            cp.wait_recv()

    shapes = [jax.ShapeDtypeStruct((3,) + h.shape[1:], h.dtype) for h in hx]
    return _comm_call(body, hx, shapes, [3 * n, 3 * n], name=name)


def _pair_swap(fs, *, name):
    n = len(fs)

    def body(*refs):
        ins, outs = refs[:n], refs[n:2 * n]
        ssem, rsem = refs[2 * n:]
        x, y, c, _ = _place()
        cps = []
        for a in range(n):
            cp = _rcopy(ins[a], outs[a], ssem.at[a], rsem.at[a], (x, y, 1 - c))
            cp.start()
            cps.append(cp)
        for cp in cps:
            cp.wait_send()
            cp.wait_recv()

    shapes = [jax.ShapeDtypeStruct(f.shape, f.dtype) for f in fs]
    return _comm_call(body, fs, shapes, [n, n], name=name)


def _reduce_scatter(grads, exch_bf16, cidx, chip_idx, tag):
    n = len(grads)
    theirs = _pair_send(grads, name=f"rs_pair_send_{tag}")
    hf, hx = [], []
    for a in range(n):
        res = _pair_sum(grads[a], theirs[a], cidx, cast=exch_bf16[a], name=f"rs_pair_sum_{tag}{a}")
        hf.append(res[0])
        hx.append(res[1] if exch_bf16[a] else res[0])
    got = _chip_exchange(hx, name=f"rs_chip_exchange_{tag}")
    mine = [_chip_sum(hf[a], got[a], chip_idx, name=f"rs_chip_sum_{tag}{a}") for a in range(n)]
    return mine, _pair_swap(mine, name=f"rs_pair_swap_{tag}")


def _interleave(a, B, L):
    return a.reshape(B, L, -1).transpose(1, 0, 2).reshape(B * L, -1)


def _deinterleave(a, B, L):
    return a.reshape(L, B, -1).transpose(1, 0, 2).reshape(B * L, -1)


def _local_step(x, tgt, W, S):
    B, L, D = x.shape
    T = B * L
    G = D // SSM_GROUP
    Pst = SSM_STATE
    hpg = D // HEAD_DIM
    HW = hpg * HEAD_DIM
    ncl = G // GROUPS_PER_CLUSTER
    x2 = x.reshape(T, D)
    tgt2 = tgt.reshape(T, D)

    disc = lambda *p: _s5_discretize(*p)
    (ab_r, ab_i, bb_r, bb_i), disc_vjp = jax.vjp(disc, S["lam_re"], S["lam_im"], S["log_dt"], S["b_re"], S["b_im"])
    wb = jnp.concatenate([_blockdiag(jnp.transpose(bb_r, (0, 2, 1))), _blockdiag(jnp.transpose(bb_i, (0, 2, 1)))],
                         axis=-1).astype(MXU_DTYPE)
    wc = jnp.concatenate([_blockdiag(jnp.transpose(S["c_re"], (0, 2, 1))), _blockdiag(-jnp.transpose(S["c_im"], (0, 2, 1)))],
                         axis=1).astype(MXU_DTYPE)
    cs = GROUPS_PER_CLUSTER * Pst
    slab = lambda ab: jnp.tile(jnp.transpose(ab.reshape(ncl, cs // LANES, LANES), (1, 0, 2)), (1, B, 1))
    a_r, a_i = slab(ab_r), slab(ab_i)
    d_row = S["d"].reshape(1, D)

    xi = _interleave(x2, B, L)
    y, yg, h_r, h_i = _s5_fwd(xi, wb, wc, a_r, a_i, d_row, B, name="s5_fwd")
    z = _mm_nn(yg, W["w_glu"], bias=S["b_glu"].reshape(1, D), name="glu_z")
    gate = _glu_gate(y, z, name="glu_gate")
    mix = _deinterleave(_mm_nn(gate, W["w_out"], name="s5_out"), B, L)
    h1, h1b, xh1, rs1 = _ln_fwd(x2, mix, S["ln_gain"][0, 0][None], S["ln_bias"][0, 0][None], name="ln_fwd_0a")

    def ffn_fwd(hb, l):
        hc = _mm_nn(hb, W["w_up"], l=l, out_dtype=MXU_DTYPE, name=f"ffn_up_{l}")
        a = _conv_glu_fwd(hc, S["conv_w"][l], S["conv_b"][l][None], L, name=f"ffn_conv_{l}")
        f = _mm_nn(a, W["w_down"], l=l, name=f"ffn_down_{l}")
        return hc, a, f

    hc0, a0, f0 = ffn_fwd(h1b, 0)
    h2, h2b, xh2, rs2 = _ln_fwd(h1, f0, S["ln_gain"][0, 1][None], S["ln_bias"][0, 1][None], name="ln_fwd_0b")

    kv = _mm_nn(h2b, W["w_kv"], name="attn_kv")
    q = _mm_nn(h2b, W["w_q"], name="attn_q")
    bias = _attn_bias(S["rel_bias"], hpg)
    o3, l3 = _attn_fwd(q, kv, bias, L, hpg, name="attn_fwd")
    o, ob, lse = _attn_merge(o3, l3, HW, name="attn_merge")
    att = _mm_nn(ob, W["w_ao"], name="attn_out")
    h3, h3b, xh3, rs3 = _ln_fwd(h2, att, S["ln_gain"][1, 0][None], S["ln_bias"][1, 0][None], name="ln_fwd_1a")
    hc1, a1, f1 = ffn_fwd(h3b, 1)
    h4, _, xh4, rs4 = _ln_fwd(h3, f1, S["ln_gain"][1, 1][None], S["ln_bias"][1, 1][None], name="ln_fwd_1b")

    dh4, lrow = _loss_grad(h4, tgt2, name="loss")
    loss = lrow[0, 0]

    GW, GS = {}, {}

    def ffn_bwd(dzb, hb, hc, a, l):
        da = _mm_nt(dzb, W["w_down"], l=l, out_dtype=MXU_DTYPE, name=f"ffn_down_bwd_x_{l}")
        GW["w_down"] = _tn(a, dzb, ptotal=1, nl=DEPTH, l=l, np_cols=D, prev=GW.get("w_down"), name=f"ffn_down_bwd_w_{l}")
        dc, dcw, dcb = _conv_glu_bwd(hc, da, S["conv_w"][l], S["conv_b"][l][None], L, name=f"ffn_conv_bwd_{l}")
        dhc = _conv_bwd_input(dc, S["conv_w"][l], L, name=f"ffn_conv_bwd_x_{l}")
        dh = _mm_nt(dhc, W["w_up"], l=l, name=f"ffn_up_bwd_x_{l}")
        GW["w_up"] = _tn(hb, dhc, ptotal=W["w_up"].shape[0], nl=DEPTH, l=l, np_cols=W["w_up"].shape[3],
                         prev=GW.get("w_up"), name=f"ffn_up_bwd_w_{l}")
        return dh, dcw, dcb

    dz4, dz4b, dg4, db4 = _ln_bwd([dh4], [1.0], xh4, rs4, S["ln_gain"][1, 1][None], name="ln_bwd_1b")
    dh3f, dcw1, dcb1 = ffn_bwd(dz4b, h3b, hc1, a1, 1)
    dz3, dz3b, dg3, db3 = _ln_bwd([dz4, dh3f], [DN_ALPHA, 1.0], xh3, rs3, S["ln_gain"][1, 0][None], name="ln_bwd_1a")
    do = _mm_nt(dz3b, W["w_ao"], name="attn_out_bwd_x")
    GW["w_ao"] = _tn(ob, dz3b, ptotal=1, np_cols=D, name="attn_out_bwd_w")
    dq, dk, dv, ds_sum = _attn_bwd(q, kv, do, o, lse, bias, L, hpg, name="attn_bwd")
    GS["rel_bias"] = _bias_grad(ds_sum, hpg, name="attn_bias_grad")
    GW["w_q"] = _tn(h2b, dq, ptotal=W["w_q"].shape[0], np_cols=W["w_q"].shape[3], name="attn_q_bwd_w")
    pkv, npkv = W["w_kv"].shape[0], W["w_kv"].shape[3]
    gkv = _tn(h2b, dk, ptotal=pkv, np_cols=npkv, p0=0, name="attn_k_bwd_w")
    GW["w_kv"] = _tn(h2b, dv, ptotal=pkv, np_cols=npkv, p0=pkv // 2, prev=gkv, name="attn_v_bwd_w")
    dh2q = _mm_nt(dq, W["w_q"], name="attn_q_bwd_x")
    dh2k = _mm_nt(dk, W["w_kv"], p0=0, pn=pkv // 2, name="attn_k_bwd_x")
    dh2v = _mm_nt(dv, W["w_kv"], p0=pkv // 2, pn=pkv // 2, name="attn_v_bwd_x")

    dz2, dz2b, dg2, db2 = _ln_bwd([dz3, dh2q, dh2k, dh2v], [DN_ALPHA, 1.0, 1.0, 1.0], xh2, rs2,
                                  S["ln_gain"][0, 1][None], name="ln_bwd_0b")
    dh1f, dcw0, dcb0 = ffn_bwd(dz2b, h1b, hc0, a0, 0)
    dz1, dz1b, dg1, db1 = _ln_bwd([dz2, dh1f], [DN_ALPHA, 1.0], xh1, rs1, S["ln_gain"][0, 0][None], name="ln_bwd_0a")
    dmix_i = _interleave(dz1b, B, L)
    dgate = _mm_nt(dmix_i, W["w_out"], name="s5_out_bwd_x")
    GW["w_out"] = _tn(gate, dmix_i, ptotal=1, np_cols=D, name="s5_out_bwd_w")
    dzg, dyg1, dbglu = _glu_bwd(y, z, dgate, name="glu_bwd")
    dyg2 = _mm_nt(dzg, W["w_glu"], name="glu_z_bwd_x")
    GW["w_glu"] = _tn(yg, dzg, ptotal=1, np_cols=D, name="glu_z_bwd_w")
    dy = _gelu_bwd(y, dyg1, dyg2, name="gelu_bwd")
    du_i, g_r, g_i, dar, dai, dd = _s5_bwd(dy, xi, h_r, h_i, wb, wc, a_r, a_i, d_row, B, name="s5_bwd")
    dwb_r = _cluster_tn(xi, g_r, ncl, tok_left=True, name="s5_b_grad_re")
    dwb_i = _cluster_tn(xi, g_i, ncl, tok_left=True, name="s5_b_grad_im")
    dwc_r = _cluster_tn(dy, h_r, ncl, tok_left=False, name="s5_c_grad_re")
    dwc_i = _cluster_tn(dy, h_i, ncl, tok_left=False, name="s5_c_grad_im")
    grad_x = _axpy(dz1, _deinterleave(du_i, B, L), DN_ALPHA, name="grad_x")

    dbb_r = jnp.transpose(_unblockdiag(dwb_r, SSM_GROUP, Pst), (0, 2, 1))
    dbb_i = jnp.transpose(_unblockdiag(dwb_i, SSM_GROUP, Pst), (0, 2, 1))
    unslab = lambda da: jnp.transpose(da.reshape(cs // LANES, B, ncl, LANES).sum(1), (1, 0, 2)).reshape(G, Pst)
    dab_r, dab_i = unslab(dar), unslab(dai)
    GS["lam_re"], GS["lam_im"], GS["log_dt"], GS["b_re"], GS["b_im"] = disc_vjp((dab_r, dab_i, dbb_r, dbb_i))
    GS["c_re"] = jnp.transpose(_unblockdiag(dwc_r, Pst, SSM_GROUP), (0, 2, 1))
    GS["c_im"] = -jnp.transpose(_unblockdiag(dwc_i, Pst, SSM_GROUP), (0, 2, 1))
    GS["d"] = dd.reshape(G, SSM_GROUP)
    GS["b_glu"] = dbglu.reshape(D)
    GS["conv_w"] = jnp.stack([dcw0, dcw1])
    GS["conv_b"] = jnp.stack([dcb0[0], dcb1[0]])
    GS["ln_gain"] = jnp.stack([jnp.stack([dg1[0], dg2[0]]), jnp.stack([dg3[0], dg4[0]])])
    GS["ln_bias"] = jnp.stack([jnp.stack([db1[0], db2[0]]), jnp.stack([db3[0], db4[0]])])
    return loss, grad_x.reshape(B, L, D), GW, GS


SMALL_REPLICATED = ("lam_re", "lam_im", "log_dt", "b_re", "b_im", "c_re", "c_im", "d", "rel_bias", "conv_b")
SMALL_SHARDED = ("b_glu", "conv_w", "ln_gain", "ln_bias")
SMALL_ORDER = SMALL_REPLICATED + SMALL_SHARDED


def _pack(arrs, lanes, row_mult):
    flat = jnp.concatenate([a.reshape(-1).astype(F32) for a in arrs])
    rows = -(-flat.shape[0] // lanes)
    rows = -(-rows // row_mult) * row_mult
    return jnp.pad(flat, (0, rows * lanes - flat.shape[0])).reshape(rows, lanes)


def _unpack(packed, shapes):
    flat = packed.reshape(-1)
    out, off = [], 0
    for s in shapes:
        n = int(np.prod(s))
        out.append(flat[off:off + n].reshape(s))
        off += n
    return out


def kernel(x, s5_lam_re, s5_lam_im, s5_log_dt, s5_b_re, s5_b_im, s5_c_re, s5_c_im, s5_d, s5_w_glu, s5_b_glu, s5_w_out, attn_w_kv, attn_w_q, attn_w_out, rel_bias, ffn_w_up, ffn_conv_w, ffn_conv_b, ffn_w_down, ln_gain, ln_bias, loss_target, m_s5_lam_re, m_s5_lam_im, m_s5_log_dt, m_s5_b_re, m_s5_b_im, m_s5_c_re, m_s5_c_im, m_s5_d, m_s5_w_glu, m_s5_b_glu, m_s5_w_out, m_attn_w_kv, m_attn_w_q, m_attn_w_out, m_rel_bias, m_ffn_w_up, m_ffn_conv_w, m_ffn_conv_b, m_ffn_w_down, m_ln_gain, m_ln_bias, v_s5_lam_re, v_s5_lam_im, v_s5_log_dt, v_s5_b_re, v_s5_b_im, v_s5_c_re, v_s5_c_im, v_s5_d, v_s5_w_glu, v_s5_b_glu, v_s5_w_out, v_attn_w_kv, v_attn_w_q, v_attn_w_out, v_rel_bias, v_ffn_w_up, v_ffn_conv_w, v_ffn_conv_b, v_ffn_w_down, v_ln_gain, v_ln_bias):
    names = ["s5_lam_re", "s5_lam_im", "s5_log_dt", "s5_b_re", "s5_b_im", "s5_c_re", "s5_c_im", "s5_d", "s5_w_glu",
             "s5_b_glu", "s5_w_out", "attn_w_kv", "attn_w_q", "attn_w_out", "rel_bias", "ffn_w_up", "ffn_conv_w",
             "ffn_conv_b", "ffn_w_down", "ln_gain", "ln_bias"]
    loc = locals()
    w_in = {n: loc[n] for n in names}
    m_in = {n: loc["m_" + n] for n in names}
    v_in = {n: loc["v_" + n] for n in names}
    chip = 2 * lax.axis_index("x") + lax.axis_index("y")
    core = lax.axis_index("c")
    chip_idx = jnp.reshape(chip, (1,)).astype(jnp.int32)
    cidx = jnp.reshape(core, (1,)).astype(jnp.int32)

    big = [("w_glu", "s5_w_glu", "rows"), ("w_out", "s5_w_out", "rows"), ("w_ao", "attn_w_out", "rows"),
           ("w_kv", "attn_w_kv", "cols"), ("w_q", "attn_w_q", "cols"),
           ("w_up", "ffn_w_up", "layer_cols"), ("w_down", "ffn_w_down", "layer_rows")]

    def halves(t, kind):
        if kind.startswith("layer"):
            return t
        r, c = t.shape[-2:]
        return t.reshape(2, r // 2, c)

    def to_weight(g, kind):
        _, _, r, c = g.shape
        if kind == "rows":
            return g.reshape(1, 1, 8 * r, c)
        if kind == "cols":
            return g.reshape(4, 1, 2 * r, c)
        if kind == "layer_cols":
            return g
        return jnp.transpose(g, (1, 0, 2, 3)).reshape(1, 2, 4 * r, c)

    def from_weight_grad(gw, kind, r, c):
        if kind == "layer_rows":
            return jnp.transpose(gw.reshape(2, 4, r, c), (1, 0, 2, 3))
        return gw.reshape(4, 2, r, c)

    small_sh = {"b_glu": s5_b_glu[0], "conv_w": ffn_conv_w, "ln_gain": ln_gain, "ln_bias": ln_bias}
    sh_shapes = [small_sh[k].shape for k in SMALL_SHARDED]
    sh_pack = _pack([small_sh[k] for k in SMALL_SHARDED], 128, 16)

    shards = [halves(w_in[src].astype(MXU_DTYPE), kind) for _, src, kind in big]
    shards.append(sh_pack.reshape(2, sh_pack.shape[0] // 2, 128))
    gathered = _gather(shards, chip, name="weights_all_gather")

    W = {key: to_weight(g, kind) for (key, _, kind), g in zip(big, gathered[:-1])}
    parts = [_unpack(gathered[-1][p], sh_shapes) for p in range(4)]
    S = {k: jnp.concatenate([parts[p][i] for p in range(4)], axis=-1) for i, k in enumerate(SMALL_SHARDED)}
    S.update(lam_re=s5_lam_re[0], lam_im=s5_lam_im[0], log_dt=s5_log_dt[0], b_re=s5_b_re[0], b_im=s5_b_im[0],
             c_re=s5_c_re[0], c_im=s5_c_im[0], d=s5_d[0], rel_bias=rel_bias, conv_b=ffn_conv_b)

    loss, grad_x, GW, GS = _local_step(x, loss_target, W, S)
    loss = lax.psum(loss, ("x", "y", "c"))

    gs_shapes = [GS[k].shape for k in SMALL_ORDER]
    gs_pack = _pack([GS[k] for k in SMALL_ORDER], 128, 64)
    rs = gs_pack.shape[0] // 8
    grads = []
    for (key, _, kind), sh in zip(big, shards):
        grads.append(from_weight_grad(GW[key], kind, sh.shape[1], sh.shape[2]))
    grads.append(gs_pack.reshape(4, 2, rs, 128))
    mine, other = _reduce_scatter(grads, [True] * len(big) + [False], cidx, chip_idx, "g")
    small_halves = jnp.where(core == 0, jnp.concatenate([mine[-1], other[-1]]), jnp.concatenate([other[-1], mine[-1]]))
    small_all = _gather([small_halves], chip, name="small_grads_all_gather")[0]
    gsmall = dict(zip(SMALL_ORDER, _unpack(small_all, gs_shapes)))

    big_res = {}
    for (key, src, kind), gm, go in zip(big, mine[:-1], other[:-1]):
        res4 = _adamw_halves(halves(w_in[src], kind), halves(m_in[src], kind), halves(v_in[src], kind), gm, go, cidx,
                             name=f"adamw_{key}")
        big_res[src] = tuple(t.reshape(w_in[src].shape) for t in res4)

    def big_out(i):
        return {src: big_res[src][i] for _, src, _ in big}

    small_w = {"lam_re": s5_lam_re, "lam_im": s5_lam_im, "log_dt": s5_log_dt, "b_re": s5_b_re, "b_im": s5_b_im,
               "c_re": s5_c_re, "c_im": s5_c_im, "d": s5_d, "rel_bias": rel_bias, "conv_b": ffn_conv_b,
               "b_glu": s5_b_glu, "conv_w": ffn_conv_w, "ln_gain": ln_gain, "ln_bias": ln_bias}
    small_name = {"lam_re": "s5_lam_re", "lam_im": "s5_lam_im", "log_dt": "s5_log_dt", "b_re": "s5_b_re", "b_im": "s5_b_im",
                  "c_re": "s5_c_re", "c_im": "s5_c_im", "d": "s5_d", "rel_bias": "rel_bias", "conv_b": "ffn_conv_b",
                  "b_glu": "s5_b_glu", "conv_w": "ffn_conv_w", "ln_gain": "ln_gain", "ln_bias": "ln_bias"}
    sg = {}
    for k in SMALL_ORDER:
        shp = small_w[k].shape
        g = gsmall[k]
        if k in SMALL_SHARDED:
            width = shp[-1]
            g = lax.dynamic_slice_in_dim(g, chip * width, width, axis=g.ndim - 1)
        sg[k] = g.reshape(shp)
    sshapes = [small_w[k].shape for k in SMALL_ORDER]
    pw = _pack([small_w[k] for k in SMALL_ORDER], 128, 512)
    pg = _pack([sg[k] for k in SMALL_ORDER], 128, 512)
    pm = _pack([m_in[small_name[k]] for k in SMALL_ORDER], 128, 512)
    pv = _pack([v_in[small_name[k]] for k in SMALL_ORDER], 128, 512)
    sd, snm, snv = _adamw(pw, pg, pm, pv, name="adamw_small")
    sd = dict(zip(SMALL_ORDER, _unpack(sd, sshapes)))
    snm = dict(zip(SMALL_ORDER, _unpack(snm, sshapes)))
    snv = dict(zip(SMALL_ORDER, _unpack(snv, sshapes)))

    res = [{}, {}, {}, {}]
    for i in range(4):
        res[i].update(big_out(i))
    for k in SMALL_ORDER:
        res[0][small_name[k]] = sg[k]
        res[1][small_name[k]] = sd[k]
        res[2][small_name[k]] = snm[k]
        res[3][small_name[k]] = snv[k]
    outs = [loss, grad_x]
    for i in range(4):
        outs += [res[i][n] for n in names]
    return tuple(outs)
```

```python
import functools
import math

import numpy as np
import jax
import jax.numpy as jnp
from jax import lax
from jax.experimental import pallas as pl
from jax.experimental.pallas import tpu as pltpu

F32 = jnp.float32
BF16 = jnp.bfloat16
MXU_DTYPE = jnp.bfloat16
V7X_VMEM_LIMIT_BYTES = 52 << 20
MESH = pl.DeviceIdType.MESH

DEPTH = 2
SSM_GROUP = 16
SSM_STATE = 64
GROUPS_PER_CLUSTER = 16
CLUSTER_W = GROUPS_PER_CLUSTER * SSM_GROUP
HEAD_DIM = 64
DILATIONS = (1, 4, 16)
BAND = 128
NEG_BIG = -1e30
REL_BUCKETS = 32
REL_MAX_DIST = 2048
DN_ALPHA = (2.0 * DEPTH) ** 0.25
LN_EPS = 1e-5
ADAM_LR, ADAM_B1, ADAM_B2, ADAM_EPS, ADAM_WD, ADAM_STEP = 0.001, 0.9, 0.999, 1e-08, 0.01, 10
GELU_K = math.sqrt(2.0 / math.pi)
GELU_C = 0.044715


def _pallas(body, **kw):
    return pl.pallas_call(body, **kw)


def _params(sem=None):
    return pltpu.CompilerParams(dimension_semantics=sem, vmem_limit_bytes=V7X_VMEM_LIMIT_BYTES)


def _pick(n, cands):
    for c in cands:
        if n % c == 0:
            return c
    return n


def _sigmoid(z):
    return 1.0 / (1.0 + jnp.exp(-z))


def _gelu(y):
    return 0.5 * y * (1.0 + jnp.tanh(GELU_K * (y + GELU_C * y * y * y)))


def _gelu_grad(y):
    t = jnp.tanh(GELU_K * (y + GELU_C * y * y * y))
    return 0.5 * (1.0 + t) + 0.5 * y * (1.0 - t * t) * (GELU_K * (1.0 + 3.0 * GELU_C * y * y))


def _mm_nn(a, w, *, l=0, bias=None, out_dtype=F32, name):
    T, K = a.shape
    P, _, _, Np = w.shape
    tm = _pick(T, (1024, 512, 256, 128))
    tn = _pick(Np, (1408, 1024, 768, 512, 384, 256, 128))
    nj = Np // tn

    def body(*refs):
        if bias is None:
            a_ref, w_ref, o_ref = refs
        else:
            a_ref, w_ref, b_ref, o_ref = refs
        acc = jnp.dot(a_ref[...].astype(MXU_DTYPE), w_ref[...].astype(MXU_DTYPE), preferred_element_type=F32)
        if bias is not None:
            acc = acc + b_ref[...]
        o_ref[...] = acc.astype(o_ref.dtype)

    in_specs = [pl.BlockSpec((tm, K), lambda p, j, i: (i, 0)),
                pl.BlockSpec((None, None, K, tn), lambda p, j, i: (p, l, 0, j))]
    args = [a, w]
    if bias is not None:
        in_specs.append(pl.BlockSpec((1, tn), lambda p, j, i: (0, p * nj + j)))
        args.append(bias)
    return _pallas(
        body, name=name, grid=(P, nj, T // tm), in_specs=in_specs,
        out_specs=pl.BlockSpec((tm, tn), lambda p, j, i: (i, p * nj + j)),
        out_shape=jax.ShapeDtypeStruct((T, P * Np), out_dtype),
        compiler_params=_params(("parallel", "parallel", "parallel")),
    )(*args)


def _mm_nt(a, w, *, l=0, p0=0, pn=None, out_dtype=F32, name):
    T = a.shape[0]
    _, _, K, Np = w.shape
    pn = w.shape[0] if pn is None else pn
    tm = _pick(T, (1024, 512, 256, 128) if K <= 1024 else (512, 256, 128))
    tn = _pick(Np, (1536, 1408, 1024, 768, 512, 384, 256, 128))
    nj = Np // tn
    nred = pn * nj

    def body(a_ref, w_ref, o_ref, acc):
        r = pl.program_id(1)

        @pl.when(r == 0)
        def _():
            acc[...] = jnp.zeros_like(acc)

        acc[...] += lax.dot_general(a_ref[...].astype(MXU_DTYPE), w_ref[...].astype(MXU_DTYPE),
                                    (((1,), (1,)), ((), ())), preferred_element_type=F32)

        @pl.when(r == nred - 1)
        def _():
            o_ref[...] = acc[...].astype(o_ref.dtype)

    return _pallas(
        body, name=name, grid=(T // tm, nred),
        in_specs=[pl.BlockSpec((tm, tn), lambda i, r: (i, r)),
                  pl.BlockSpec((None, None, K, tn), lambda i, r: (p0 + r // nj, l, 0, r % nj))],
        out_specs=pl.BlockSpec((tm, K), lambda i, r: (i, 0)),
        out_shape=jax.ShapeDtypeStruct((T, K), out_dtype),
        scratch_shapes=[pltpu.VMEM((tm, K), F32)],
        compiler_params=_params(("parallel", "arbitrary")),
    )(a, w)


def _tn(a, b, *, ptotal, np_cols, nl=1, l=0, p0=0, prev=None, name):
    T, K = a.shape
    Np = np_cols
    pn = b.shape[1] // Np
    tt = _pick(T, (1024, 512, 256, 128))
    tk = _pick(K, (1408, 1024, 512, 256, 128))
    tn = _pick(Np, (1408, 768, 512, 256, 128))
    if tk * tn > 1408 * 1024:
        tn = _pick(Np, (512, 256, 128))
    nj = Np // tn
    nt = T // tt

    def body(*refs):
        a_ref, b_ref = refs[0], refs[1]
        o_ref, acc = refs[-2], refs[-1]
        t = pl.program_id(3)

        @pl.when(t == 0)
        def _():
            acc[...] = jnp.zeros_like(acc)

        acc[...] += lax.dot_general(a_ref[...].astype(MXU_DTYPE), b_ref[...].astype(MXU_DTYPE),
                                    (((0,), (0,)), ((), ())), preferred_element_type=F32)

        @pl.when(t == nt - 1)
        def _():
            o_ref[...] = acc[...]

    in_specs = [pl.BlockSpec((tt, tk), lambda kb, p, j, t: (t, kb)),
                pl.BlockSpec((tt, tn), lambda kb, p, j, t: (t, p * nj + j))]
    args = [a, b]
    aliases = {}
    if prev is not None:
        in_specs.append(pl.BlockSpec(memory_space=pl.ANY))
        args.append(prev)
        aliases = {2: 0}
    return _pallas(
        body, name=name, grid=(K // tk, pn, nj, nt), in_specs=in_specs,
        out_specs=pl.BlockSpec((None, None, tk, tn), lambda kb, p, j, t: (p0 + p, l, kb, j)),
        out_shape=jax.ShapeDtypeStruct((ptotal, nl, K, Np), F32),
        scratch_shapes=[pltpu.VMEM((tk, tn), F32)],
        input_output_aliases=aliases,
        compiler_params=_params(("parallel", "parallel", "parallel", "arbitrary")),
    )(*args)


def _rows(tm, f):
    return pl.BlockSpec((tm, f), lambda i: (i, 0))


def _whole(shape):
    nd = len(shape)
    return pl.BlockSpec(shape, lambda i: (0,) * nd)


def _ln_fwd(xres, f, gain, bias, *, name):
    T, D = xres.shape
    tm = _pick(T, (256, 128))

    def body(x_ref, f_ref, g_ref, b_ref, y_ref, yb_ref, xh_ref, rs_ref):
        z = DN_ALPHA * x_ref[...] + f_ref[...]
        mu = jnp.mean(z, axis=-1, keepdims=True)
        zc = z - mu
        var = jnp.mean(zc * zc, axis=-1, keepdims=True)
        rstd = lax.rsqrt(var + LN_EPS)
        xh = zc * rstd
        y = xh * g_ref[...] + b_ref[...]
        y_ref[...] = y
        yb_ref[...] = y.astype(yb_ref.dtype)
        xh_ref[...] = xh
        rs_ref[...] = rstd

    return _pallas(
        body, name=name, grid=(T // tm,),
        in_specs=[_rows(tm, D), _rows(tm, D), _whole((1, D)), _whole((1, D))],
        out_specs=[_rows(tm, D), _rows(tm, D), _rows(tm, D), _rows(tm, 1)],
        out_shape=[jax.ShapeDtypeStruct((T, D), F32), jax.ShapeDtypeStruct((T, D), MXU_DTYPE),
                   jax.ShapeDtypeStruct((T, D), F32), jax.ShapeDtypeStruct((T, 1), F32)],
        compiler_params=_params(("parallel",)),
    )(xres, f, gain, bias)


def _ln_bwd(addends, coefs, xhat, rstd, gain, *, name):
    T, D = xhat.shape
    tm = _pick(T, (256, 128))
    n = len(addends)

    def body(*refs):
        adds = refs[:n]
        xh_ref, rs_ref, g_ref, dz_ref, dzb_ref, dg_ref, db_ref = refs[n:]
        dy = coefs[0] * adds[0][...]
        for c, r in zip(coefs[1:], adds[1:]):
            dy = dy + c * r[...]
        xh = xh_ref[...]
        dxh = dy * g_ref[...]
        m1 = jnp.mean(dxh, axis=-1, keepdims=True)
        m2 = jnp.mean(dxh * xh, axis=-1, keepdims=True)
        dz = rs_ref[...] * (dxh - m1 - xh * m2)
        dz_ref[...] = dz
        dzb_ref[...] = dz.astype(dzb_ref.dtype)

        @pl.when(pl.program_id(0) == 0)
        def _():
            dg_ref[...] = jnp.zeros_like(dg_ref)
            db_ref[...] = jnp.zeros_like(db_ref)

        dg_ref[...] += jnp.sum(dy * xh, axis=0, keepdims=True)
        db_ref[...] += jnp.sum(dy, axis=0, keepdims=True)

    return _pallas(
        body, name=name, grid=(T // tm,),
        in_specs=[_rows(tm, D)] * n + [_rows(tm, D), _rows(tm, 1), _whole((1, D))],
        out_specs=[_rows(tm, D), _rows(tm, D), _whole((1, D)), _whole((1, D))],
        out_shape=[jax.ShapeDtypeStruct((T, D), F32), jax.ShapeDtypeStruct((T, D), MXU_DTYPE),
                   jax.ShapeDtypeStruct((1, D), F32), jax.ShapeDtypeStruct((1, D), F32)],
        compiler_params=_params(("arbitrary",)),
    )(*addends, xhat, rstd, gain)


def _loss_grad(y, tgt, *, name):
    T, D = y.shape
    tm = _pick(T, (256, 128))

    def body(y_ref, t_ref, dy_ref, l_ref):
        e = y_ref[...] - t_ref[...]
        dy_ref[...] = e * (1.0 / D)

        @pl.when(pl.program_id(0) == 0)
        def _():
            l_ref[...] = jnp.zeros_like(l_ref)

        l_ref[...] += jnp.zeros_like(l_ref) + jnp.sum(e * e) * (0.5 / D)

    return _pallas(
        body, name=name, grid=(T // tm,),
        in_specs=[_rows(tm, D), _rows(tm, D)],
        out_specs=[_rows(tm, D), _whole((1, 128))],
        out_shape=[jax.ShapeDtypeStruct((T, D), F32), jax.ShapeDtypeStruct((1, 128), F32)],
        compiler_params=_params(("arbitrary",)),
    )(y, tgt)


def _axpy(a, b, ca, *, name):
    T, D = a.shape
    tm = _pick(T, (256, 128))

    def body(a_ref, b_ref, o_ref):
        o_ref[...] = ca * a_ref[...] + b_ref[...]

    return _pallas(
        body, name=name, grid=(T // tm,), in_specs=[_rows(tm, D), _rows(tm, D)], out_specs=_rows(tm, D),
        out_shape=jax.ShapeDtypeStruct((T, D), F32), compiler_params=_params(("parallel",)),
    )(a, b)


def _glu_gate(y, z, *, name):
    T, D = y.shape
    tm = _pick(T, (256, 128))

    def body(y_ref, z_ref, g_ref):
        g_ref[...] = (_gelu(y_ref[...]) * _sigmoid(z_ref[...])).astype(g_ref.dtype)

    return _pallas(
        body, name=name, grid=(T // tm,), in_specs=[_rows(tm, D), _rows(tm, D)], out_specs=_rows(tm, D),
        out_shape=jax.ShapeDtypeStruct((T, D), MXU_DTYPE), compiler_params=_params(("parallel",)),
    )(y, z)


def _glu_bwd(y, z, dg, *, name):
    T, D = y.shape
    tm = _pick(T, (256, 128))

    def body(y_ref, z_ref, dg_ref, dzb_ref, dyg_ref, db_ref):
        s = _sigmoid(z_ref[...])
        dg = dg_ref[...]
        dz = dg * _gelu(y_ref[...]) * s * (1.0 - s)
        dzb_ref[...] = dz.astype(dzb_ref.dtype)
        dyg_ref[...] = dg * s

        @pl.when(pl.program_id(0) == 0)
        def _():
            db_ref[...] = jnp.zeros_like(db_ref)

        db_ref[...] += jnp.sum(dz, axis=0, keepdims=True)

    return _pallas(
        body, name=name, grid=(T // tm,), in_specs=[_rows(tm, D)] * 3,
        out_specs=[_rows(tm, D), _rows(tm, D), _whole((1, D))],
        out_shape=[jax.ShapeDtypeStruct((T, D), MXU_DTYPE), jax.ShapeDtypeStruct((T, D), F32),
                   jax.ShapeDtypeStruct((1, D), F32)],
        compiler_params=_params(("arbitrary",)),
    )(y, z, dg)


def _gelu_bwd(y, d1, d2, *, name):
    T, D = y.shape
    tm = _pick(T, (256, 128))

    def body(y_ref, a_ref, b_ref, o_ref):
        o_ref[...] = (a_ref[...] + b_ref[...]) * _gelu_grad(y_ref[...])

    return _pallas(
        body, name=name, grid=(T // tm,), in_specs=[_rows(tm, D)] * 3, out_specs=_rows(tm, D),
        out_shape=jax.ShapeDtypeStruct((T, D), F32), compiler_params=_params(("parallel",)),
    )(y, d1, d2)


CONV_ROWS = 128
CONV_EDGE = 16


def _shift_back(x, edge, at_start, tm):
    rows = lax.broadcasted_iota(jnp.int32, x.shape, 0)
    keep = jnp.where(at_start, 0.0, 1.0)
    e7 = edge[CONV_EDGE - 1:CONV_EDGE, :] * keep
    e6 = edge[CONV_EDGE - 2:CONV_EDGE - 1, :] * keep
    r1 = pltpu.roll(x, 1, 0)
    r2 = pltpu.roll(x, 2, 0)
    x1 = jnp.where(rows == 0, e7, r1)
    x2 = jnp.where(rows == 0, e6, jnp.where(rows == 1, e7, r2))
    return x1, x2


def _conv_specs(T, F2, tm):
    return [_rows(tm, F2),
            pl.BlockSpec((CONV_EDGE, F2), lambda i: (jnp.maximum(i * (tm // CONV_EDGE) - 1, 0), 0))]


def _conv_glu_fwd(hc, conv_w, conv_b, L, *, name):
    T, F2 = hc.shape
    F = F2 // 2
    tm = CONV_ROWS

    def body(x_ref, e_ref, w_ref, b_ref, a_ref):
        at_start = (pl.program_id(0) * tm) % L == 0
        x = x_ref[...].astype(F32)
        x1, x2 = _shift_back(x, e_ref[...].astype(F32), at_start, tm)
        c = b_ref[...] + w_ref[0:1, :] * x + w_ref[1:2, :] * x1 + w_ref[2:3, :] * x2
        val, gate = c[:, :F], c[:, F:]
        a_ref[...] = (gate * _sigmoid(gate) * val).astype(a_ref.dtype)

    return _pallas(
        body, name=name, grid=(T // tm,),
        in_specs=_conv_specs(T, F2, tm) + [_whole((3, F2)), _whole((1, F2))],
        out_specs=_rows(tm, F),
        out_shape=jax.ShapeDtypeStruct((T, F), MXU_DTYPE), compiler_params=_params(("parallel",)),
    )(hc, hc, conv_w, conv_b)


def _conv_glu_bwd(hc, da, conv_w, conv_b, L, *, name):
    T, F2 = hc.shape
    F = F2 // 2
    tm = CONV_ROWS

    def body(x_ref, e_ref, da_ref, w_ref, b_ref, dc_ref, dw_ref, db_ref):
        at_start = (pl.program_id(0) * tm) % L == 0
        x = x_ref[...].astype(F32)
        x1, x2 = _shift_back(x, e_ref[...].astype(F32), at_start, tm)
        c = b_ref[...] + w_ref[0:1, :] * x + w_ref[1:2, :] * x1 + w_ref[2:3, :] * x2
        val, gate = c[:, :F], c[:, F:]
        s = _sigmoid(gate)
        da = da_ref[...].astype(F32)
        dval = da * (gate * s)
        dgate = da * val * (s * (1.0 + gate * (1.0 - s)))
        dc = jnp.concatenate([dval, dgate], axis=-1)
        dc_ref[...] = dc.astype(dc_ref.dtype)

        @pl.when(pl.program_id(0) == 0)
        def _():
            dw_ref[...] = jnp.zeros_like(dw_ref)
            db_ref[...] = jnp.zeros_like(db_ref)

        dw_ref[0:1, :] += jnp.sum(dc * x, axis=0, keepdims=True)
        dw_ref[1:2, :] += jnp.sum(dc * x1, axis=0, keepdims=True)
        dw_ref[2:3, :] += jnp.sum(dc * x2, axis=0, keepdims=True)
        db_ref[...] += jnp.sum(dc, axis=0, keepdims=True)

    return _pallas(
        body, name=name, grid=(T // tm,),
        in_specs=_conv_specs(T, F2, tm) + [_rows(tm, F), _whole((3, F2)), _whole((1, F2))],
        out_specs=[_rows(tm, F2), _whole((3, F2)), _whole((1, F2))],
        out_shape=[jax.ShapeDtypeStruct((T, F2), MXU_DTYPE), jax.ShapeDtypeStruct((3, F2), F32),
                   jax.ShapeDtypeStruct((1, F2), F32)],
        compiler_params=_params(("arbitrary",)),
    )(hc, hc, da, conv_w, conv_b)


def _conv_bwd_input(dc, conv_w, L, *, name):
    T, F2 = dc.shape
    tm = CONV_ROWS
    edge = CONV_EDGE
    last_blk = T // edge - 1

    def body(x_ref, e_ref, w_ref, o_ref):
        at_end = ((pl.program_id(0) + 1) * tm) % L == 0
        x = x_ref[...].astype(F32)
        rows = lax.broadcasted_iota(jnp.int32, x.shape, 0)
        keep = jnp.where(at_end, 0.0, 1.0)
        ev = e_ref[...].astype(F32)
        e0 = ev[0:1, :] * keep
        e1 = ev[1:2, :] * keep
        u1 = pltpu.roll(x, tm - 1, 0)
        u2 = pltpu.roll(x, tm - 2, 0)
        x1 = jnp.where(rows == tm - 1, e0, u1)
        x2 = jnp.where(rows == tm - 1, e1, jnp.where(rows == tm - 2, e0, u2))
        o_ref[...] = (w_ref[0:1, :] * x + w_ref[1:2, :] * x1 + w_ref[2:3, :] * x2).astype(o_ref.dtype)

    return _pallas(
        body, name=name, grid=(T // tm,),
        in_specs=[_rows(tm, F2),
                  pl.BlockSpec((edge, F2), lambda i: (jnp.minimum((i + 1) * (tm // edge), last_blk), 0)),
                  _whole((3, F2))],
        out_specs=_rows(tm, F2),
        out_shape=jax.ShapeDtypeStruct((T, F2), MXU_DTYPE), compiler_params=_params(("parallel",)),
    )(dc, dc, conv_w)


S5_CHUNK = 128
LANES = 128


def _slab_rows(c, n, ncl):
    return pl.ds(c, n) if ncl == 1 else pl.ds(c, n, stride=ncl)


def _slab_put(ref, c, n, ncl, val):
    for s in range(val.shape[1] // LANES):
        ref[s, _slab_rows(c, n, ncl), :] = val[:, s * LANES:(s + 1) * LANES]


def _slab_get(ref, c, n, ncl):
    return jnp.concatenate([ref[s, _slab_rows(c, n, ncl), :] for s in range(ref.shape[0])], axis=-1)


def _slabs(n_slab, rows):
    return pl.BlockSpec((n_slab, rows, LANES), lambda i: (0, i, 0))


def _s5_fwd(xi, wb, wc, a_r, a_i, d_row, B, *, name):
    T, D = xi.shape
    ncl = wb.shape[0]
    cs = wb.shape[2] // 2
    ns = cs // LANES
    R = B * ncl
    Q = S5_CHUNK
    QR = Q * ncl
    nsteps = Q // B

    def body(x_ref, wb_ref, wc_ref, ar_ref, ai_ref, d_ref, y_ref, yg_ref, hr_ref, hi_ref, bur, bui, cr, ci):
        @pl.when(pl.program_id(0) == 0)
        def _():
            cr[...] = jnp.zeros_like(cr)
            ci[...] = jnp.zeros_like(ci)

        x = x_ref[...]
        xb = x.astype(MXU_DTYPE)
        for c in range(ncl):
            bu = jnp.dot(xb[:, c * CLUSTER_W:(c + 1) * CLUSTER_W], wb_ref[c], preferred_element_type=F32)
            _slab_put(bur, c, Q, ncl, bu[:, :cs])
            _slab_put(bui, c, Q, ncl, bu[:, cs:])
        ar = ar_ref[...]
        ai = ai_ref[...]

        def step(k, carry):
            hr, hi = carry
            sl = pl.ds(pl.multiple_of(k * R, R), R)
            nr = ar * hr - ai * hi + bur[:, sl, :]
            ni = ar * hi + ai * hr + bui[:, sl, :]
            hr_ref[:, sl, :] = nr
            hi_ref[:, sl, :] = ni
            return nr, ni

        hr, hi = lax.fori_loop(0, nsteps, step, (cr[...], ci[...]), unroll=4)
        cr[...] = hr
        ci[...] = hi
        parts = []
        for c in range(ncl):
            hrc = _slab_get(hr_ref, c, Q, ncl).astype(MXU_DTYPE)
            hic = _slab_get(hi_ref, c, Q, ncl).astype(MXU_DTYPE)
            parts.append(jnp.dot(hrc, wc_ref[c, :cs, :], preferred_element_type=F32)
                         + jnp.dot(hic, wc_ref[c, cs:, :], preferred_element_type=F32))
        y = d_ref[...] * x + (parts[0] if ncl == 1 else jnp.concatenate(parts, axis=-1))
        y_ref[...] = y
        yg_ref[...] = _gelu(y).astype(yg_ref.dtype)

    return _pallas(
        body, name=name, grid=(T // Q,),
        in_specs=[_rows(Q, D), _whole(wb.shape), _whole(wc.shape), _whole((ns, R, LANES)), _whole((ns, R, LANES)),
                  _whole((1, D))],
        out_specs=[_rows(Q, D), _rows(Q, D), _slabs(ns, QR), _slabs(ns, QR)],
        out_shape=[jax.ShapeDtypeStruct((T, D), F32), jax.ShapeDtypeStruct((T, D), MXU_DTYPE),
                   jax.ShapeDtypeStruct((ns, T * ncl, LANES), F32), jax.ShapeDtypeStruct((ns, T * ncl, LANES), F32)],
        scratch_shapes=[pltpu.VMEM((ns, QR, LANES), F32), pltpu.VMEM((ns, QR, LANES), F32),
                        pltpu.VMEM((ns, R, LANES), F32), pltpu.VMEM((ns, R, LANES), F32)],
        compiler_params=_params(("arbitrary",)),
    )(xi, wb, wc, a_r, a_i, d_row)


def _s5_bwd(dy, xi, h_r, h_i, wb, wc, a_r, a_i, d_row, B, *, name):
    T, D = dy.shape
    ncl = wb.shape[0]
    cs = wb.shape[2] // 2
    ns = cs // LANES
    R = B * ncl
    Q = S5_CHUNK
    nsteps = Q // B
    nchunk = T // Q
    QR = Q * ncl

    def rev(i):
        return nchunk - 1 - i

    def body(dy_ref, x_ref, hr_ref, hi_ref, pr_ref, pi_ref, wb_ref, wc_ref, ar_ref, ai_ref, d_ref,
             du_ref, gr_ref, gi_ref, dar_ref, dai_ref, dd_ref, dhr, dhi, cr, ci):
        i = pl.program_id(0)

        @pl.when(i == 0)
        def _():
            cr[...] = jnp.zeros_like(cr)
            ci[...] = jnp.zeros_like(ci)
            dar_ref[...] = jnp.zeros_like(dar_ref)
            dai_ref[...] = jnp.zeros_like(dai_ref)
            dd_ref[...] = jnp.zeros_like(dd_ref)

        dyv = dy_ref[...]
        dyb = dyv.astype(MXU_DTYPE)
        for c in range(ncl):
            dh = lax.dot_general(dyb[:, c * CLUSTER_W:(c + 1) * CLUSTER_W], wc_ref[c],
                                 (((1,), (1,)), ((), ())), preferred_element_type=F32)
            _slab_put(dhr, c, Q, ncl, dh[:, :cs])
            _slab_put(dhi, c, Q, ncl, dh[:, cs:])
        ar = ar_ref[...]
        ai = ai_ref[...]

        def step(j, carry):
            gr, gi, sar, sai = carry
            k = nsteps - 1 - j
            sl = pl.ds(pl.multiple_of(k * R, R), R)
            ngr = dhr[:, sl, :] + ar * gr + ai * gi
            ngi = dhi[:, sl, :] - ai * gr + ar * gi
            gr_ref[:, sl, :] = ngr
            gi_ref[:, sl, :] = ngi
            pv = pl.ds(pl.multiple_of((k - 1) * R, R), R)
            hpr = hr_ref[:, pv, :]
            hpi = hi_ref[:, pv, :]
            return ngr, ngi, sar + ngr * hpr + ngi * hpi, sai - ngr * hpi + ngi * hpr

        gr, gi, sar, sai = lax.fori_loop(0, nsteps - 1, step, (cr[...], ci[...], dar_ref[...], dai_ref[...]), unroll=4)
        sl0 = pl.ds(0, R)
        ngr = dhr[:, sl0, :] + ar * gr + ai * gi
        ngi = dhi[:, sl0, :] - ai * gr + ar * gi
        gr_ref[:, sl0, :] = ngr
        gi_ref[:, sl0, :] = ngi
        keep = jnp.where(i == nchunk - 1, 0.0, 1.0)
        hpr = pr_ref[:, 8 - R:8, :] * keep
        hpi = pi_ref[:, 8 - R:8, :] * keep
        dar_ref[...] = sar + ngr * hpr + ngi * hpi
        dai_ref[...] = sai - ngr * hpi + ngi * hpr
        cr[...] = ngr
        ci[...] = ngi
        parts = []
        for c in range(ncl):
            grc = _slab_get(gr_ref, c, Q, ncl).astype(MXU_DTYPE)
            gic = _slab_get(gi_ref, c, Q, ncl).astype(MXU_DTYPE)
            parts.append(lax.dot_general(grc, wb_ref[c, :, :cs], (((1,), (1,)), ((), ())), preferred_element_type=F32)
                         + lax.dot_general(gic, wb_ref[c, :, cs:], (((1,), (1,)), ((), ())), preferred_element_type=F32))
        du_ref[...] = d_ref[...] * dyv + (parts[0] if ncl == 1 else jnp.concatenate(parts, axis=-1))
        dd_ref[...] += jnp.sum(dyv * x_ref[...], axis=0, keepdims=True)

    tok = pl.BlockSpec((Q, D), lambda i: (rev(i), 0))
    st = pl.BlockSpec((ns, QR, LANES), lambda i: (0, rev(i), 0))
    before = pl.BlockSpec((ns, 8, LANES), lambda i: (0, jnp.maximum(rev(i) * (QR // 8) - 1, 0), 0))
    acc = _whole((ns, R, LANES))
    return _pallas(
        body, name=name, grid=(nchunk,),
        in_specs=[tok, tok, st, st, before, before, _whole(wb.shape), _whole(wc.shape), acc, acc, _whole((1, D))],
        out_specs=[tok, st, st, acc, acc, _whole((1, D))],
        out_shape=[jax.ShapeDtypeStruct((T, D), F32),
                   jax.ShapeDtypeStruct((ns, T * ncl, LANES), F32), jax.ShapeDtypeStruct((ns, T * ncl, LANES), F32),
                   jax.ShapeDtypeStruct((ns, R, LANES), F32), jax.ShapeDtypeStruct((ns, R, LANES), F32),
                   jax.ShapeDtypeStruct((1, D), F32)],
        scratch_shapes=[pltpu.VMEM((ns, QR, LANES), F32)] * 2 + [pltpu.VMEM((ns, R, LANES), F32)] * 2,
        compiler_params=_params(("arbitrary",)),
    )(dy, xi, h_r, h_i, h_r, h_i, wb, wc, a_r, a_i, d_row)


def _cluster_tn(tok, st, ncl, *, tok_left, name):
    T = tok.shape[0]
    ns = st.shape[0]
    cs = ns * LANES
    tt = _pick(T, (512, 256, 128))
    nt = T // tt
    oshape = (ncl, CLUSTER_W, cs) if tok_left else (ncl, cs, CLUSTER_W)

    def body(tok_ref, st_ref, o_ref, acc):
        t = pl.program_id(0)

        @pl.when(t == 0)
        def _():
            acc[...] = jnp.zeros_like(acc)

        tk = tok_ref[...].astype(MXU_DTYPE)
        for c in range(ncl):
            tc = tk[:, c * CLUSTER_W:(c + 1) * CLUSTER_W]
            sc = _slab_get(st_ref, c, tt, ncl).astype(MXU_DTYPE)
            lhs, rhs = (tc, sc) if tok_left else (sc, tc)
            acc[c] += lax.dot_general(lhs, rhs, (((0,), (0,)), ((), ())), preferred_element_type=F32)

        @pl.when(t == nt - 1)
        def _():
            o_ref[...] = acc[...]

    return _pallas(
        body, name=name, grid=(nt,),
        in_specs=[_rows(tt, tok.shape[1]), _slabs(ns, tt * ncl)],
        out_specs=_whole(oshape),
        out_shape=jax.ShapeDtypeStruct(oshape, F32),
        scratch_shapes=[pltpu.VMEM(oshape, F32)],
        compiler_params=_params(("arbitrary",)),
    )(tok, st)


def _s5_discretize(lam_re, lam_im, log_dt, b_re, b_im):
    dt = jnp.exp(log_dt)[:, None]
    mag = jnp.exp(lam_re * dt)
    ab_r, ab_i = mag * jnp.cos(lam_im * dt), mag * jnp.sin(lam_im * dt)
    den = lam_re * lam_re + lam_im * lam_im
    nr = ab_r - 1.0
    co_r = (nr * lam_re + ab_i * lam_im) / den
    co_i = (ab_i * lam_re - nr * lam_im) / den
    bb_r = co_r[..., None] * b_re - co_i[..., None] * b_im
    bb_i = co_r[..., None] * b_im + co_i[..., None] * b_re
    return ab_r, ab_i, bb_r, bb_i


def _blockdiag(m):
    G, r, k = m.shape
    ncl = G // GROUPS_PER_CLUSTER
    m4 = m.reshape(ncl, GROUPS_PER_CLUSTER, r, k)
    eye = jnp.eye(GROUPS_PER_CLUSTER, dtype=m.dtype)
    return jnp.einsum('cgrk,gh->cgrhk', m4, eye).reshape(ncl, GROUPS_PER_CLUSTER * r, GROUPS_PER_CLUSTER * k)


def _unblockdiag(m, r, k):
    ncl = m.shape[0]
    m5 = m.reshape(ncl, GROUPS_PER_CLUSTER, r, GROUPS_PER_CLUSTER, k)
    eye = jnp.eye(GROUPS_PER_CLUSTER, dtype=m.dtype)
    return jnp.einsum('cgrhk,gh->cgrk', m5, eye).reshape(ncl * GROUPS_PER_CLUSTER, r, k)


def _t5_bucket(dist):
    exact = REL_BUCKETS // 2
    d = np.maximum(dist, 1).astype(np.float32)
    large = exact + (np.log(d / exact) / math.log(REL_MAX_DIST / exact) * (REL_BUCKETS - exact)).astype(np.int64)
    large = np.minimum(large, REL_BUCKETS - 1)
    return np.where(dist < exact, dist, large).astype(np.int32)


def _band_tables(dil):
    steps = np.arange(BAND)[:, None] + BAND - np.arange(2 * BAND)[None, :]
    bucket = _t5_bucket(np.maximum(steps, 0) * dil)
    in_band = (steps >= 0) & (steps <= BAND)
    return bucket, in_band


def _attn_bias(rel_bias, hpg):
    out = []
    for g, dil in enumerate(DILATIONS):
        bucket, in_band = _band_tables(dil)
        cols = rel_bias[:, g * hpg:(g + 1) * hpg].astype(F32)
        onehot = jnp.asarray((bucket.reshape(-1, 1) == np.arange(REL_BUCKETS)[None, :]).astype(np.float32))
        bias = jnp.dot(onehot, cols, precision=lax.Precision.HIGHEST).T.reshape(hpg, BAND, 2 * BAND)
        out.append(jnp.where(jnp.asarray(in_band)[None], bias, NEG_BIG))
    return jnp.concatenate(out, axis=0)


def _attn_blocks(dil, L):
    M = L // dil
    return M, M // BAND


def _row_sel(r, M, dil):
    return pl.ds(r, M) if dil == 1 else pl.ds(r, M, stride=dil)


def _attn_fwd(q, kv, bias, L, hpg, *, name):
    T = q.shape[0]
    nb_ = T // L
    HP = hpg // 2
    W3 = 3 * hpg * HEAD_DIM
    mmax = L

    def group_body(dil, q_ref, k_ref, v_ref, b_ref, o_ref, l_ref, os, ls):
        M, NB = _attn_blocks(dil, L)
        for r in range(dil):
            rows = _row_sel(r, M, dil)
            first = lax.broadcasted_iota(jnp.int32, (1, 2 * HEAD_DIM), 1) < HEAD_DIM
            qf = q_ref[rows, :] * 0.125
            qm = [jnp.where(first, qf, 0.0).astype(MXU_DTYPE), jnp.where(first, 0.0, qf).astype(MXU_DTYPE)]
            kr = k_ref[rows, :].astype(MXU_DTYPE)
            va = jnp.concatenate([v_ref[rows, :].astype(MXU_DTYPE), jnp.ones((M, 2 * HEAD_DIM), MXU_DTYPE)], axis=-1)
            for n in range(NB):
                qs = slice(n * BAND, (n + 1) * BAND)
                ks = slice(0, BAND) if n == 0 else slice((n - 1) * BAND, (n + 1) * BAND)
                o_h, l_h = [], []
                for hh in range(2):
                    bb = b_ref[hh, :, BAND:] if n == 0 else b_ref[hh]
                    s = lax.dot_general(qm[hh][qs, :], kr[ks, :], (((1,), (1,)), ((), ())),
                                        preferred_element_type=F32) + bb
                    m = jnp.max(s, axis=-1, keepdims=True)
                    p = jnp.exp(s - m)
                    pv = jnp.dot(p.astype(MXU_DTYPE), va[ks, :], preferred_element_type=F32)
                    l = pv[:, 2 * HEAD_DIM:]
                    o_h.append(pv[:, :2 * HEAD_DIM] / l)
                    l_h.append(m + jnp.log(l))
                os[qs, :] = jnp.where(first, o_h[0], o_h[1])
                ls[qs, :] = jnp.where(first, l_h[0], l_h[1])
            o_ref[rows, :] = os[0:M, :]
            l_ref[rows, :] = ls[0:M, :]

    def body(q_ref, k_ref, v_ref, b_ref, o_ref, l_ref, os, ls):
        g = pl.program_id(0)
        for gi, dil in enumerate(DILATIONS):
            pl.when(g == gi)(functools.partial(group_body, dil, q_ref, k_ref, v_ref, b_ref, o_ref, l_ref, os, ls))

    blk = (L, 2 * HEAD_DIM)
    return _pallas(
        body, name=name, grid=(3, nb_, HP),
        in_specs=[pl.BlockSpec(blk, lambda g, b, h: (b, g * HP + h)),
                  pl.BlockSpec(blk, lambda g, b, h: (b, g * HP + h)),
                  pl.BlockSpec(blk, lambda g, b, h: (b, 3 * HP + g * HP + h)),
                  pl.BlockSpec((2, BAND, 2 * BAND), lambda g, b, h: (g * HP + h, 0, 0))],
        out_specs=[pl.BlockSpec(blk, lambda g, b, h: (b, g * HP + h)),
                   pl.BlockSpec(blk, lambda g, b, h: (b, g * HP + h))],
        out_shape=[jax.ShapeDtypeStruct((T, W3), F32), jax.ShapeDtypeStruct((T, W3), F32)],
        scratch_shapes=[pltpu.VMEM((mmax, 2 * HEAD_DIM), F32), pltpu.VMEM((mmax, 2 * HEAD_DIM), F32)],
        compiler_params=_params(("arbitrary", "arbitrary", "arbitrary")),
    )(q, kv, kv, bias)


def _attn_merge(o3, l3, hw, *, name):
    T = o3.shape[0]
    tm = _pick(T, (256, 128))

    def body(o0, o1, o2, l0, l1, l2, o_ref, ob_ref, lse_ref):
        a0, a1, a2 = l0[...], l1[...], l2[...]
        m = jnp.maximum(jnp.maximum(a0, a1), a2)
        e0, e1, e2 = jnp.exp(a0 - m), jnp.exp(a1 - m), jnp.exp(a2 - m)
        z = e0 + e1 + e2
        o = (e0 * o0[...] + e1 * o1[...] + e2 * o2[...]) / z
        o_ref[...] = o
        ob_ref[...] = o.astype(ob_ref.dtype)
        lse_ref[...] = m + jnp.log(z)

    def col(g):
        return pl.BlockSpec((tm, hw), lambda i: (i, g))

    return _pallas(
        body, name=name, grid=(T // tm,),
        in_specs=[col(0), col(1), col(2), col(0), col(1), col(2)],
        out_specs=[_rows(tm, hw)] * 3,
        out_shape=[jax.ShapeDtypeStruct((T, hw), F32), jax.ShapeDtypeStruct((T, hw), MXU_DTYPE),
                   jax.ShapeDtypeStruct((T, hw), F32)],
        compiler_params=_params(("parallel",)),
    )(o3, o3, o3, l3, l3, l3)


def _attn_bwd(q, kv, do, o, lse, bias, L, hpg, *, name):
    T = q.shape[0]
    nb_ = T // L
    HP = hpg // 2
    W3 = 3 * hpg * HEAD_DIM
    mmax = L

    def group_body(dil, q_ref, k_ref, v_ref, do_ref, o_ref, l_ref, b_ref, dq_ref, dk_ref, dv_ref, ds_ref,
                   dqs, dks, dvs):
        M, NB = _attn_blocks(dil, L)
        for r in range(dil):
            rows = _row_sel(r, M, dil)
            first = lax.broadcasted_iota(jnp.int32, (1, 2 * HEAD_DIM), 1) < HEAD_DIM
            qf = q_ref[rows, :] * 0.125
            qm = [jnp.where(first, qf, 0.0).astype(MXU_DTYPE), jnp.where(first, 0.0, qf).astype(MXU_DTYPE)]
            kr = k_ref[rows, :].astype(MXU_DTYPE)
            vr = v_ref[rows, :].astype(MXU_DTYPE)
            dof = do_ref[rows, :]
            dom = [jnp.where(first, dof, 0.0).astype(MXU_DTYPE), jnp.where(first, 0.0, dof).astype(MXU_DTYPE)]
            dod = dof * o_ref[rows, :]
            delta = [jnp.sum(jnp.where(first, dod, 0.0), axis=-1, keepdims=True),
                     jnp.sum(jnp.where(first, 0.0, dod), axis=-1, keepdims=True)]
            lr = l_ref[rows, :]
            lse = [lr[:, 0:1], lr[:, HEAD_DIM:HEAD_DIM + 1]]
            dks[0:M, :] = jnp.zeros((M, 2 * HEAD_DIM), F32)
            dvs[0:M, :] = jnp.zeros((M, 2 * HEAD_DIM), F32)
            for n in range(NB):
                qs = slice(n * BAND, (n + 1) * BAND)
                ks = slice(0, BAND) if n == 0 else slice((n - 1) * BAND, (n + 1) * BAND)
                dq_h = []
                dkc = dvc = None
                for hh in range(2):
                    bb = b_ref[hh, :, BAND:] if n == 0 else b_ref[hh]
                    qb, dob = qm[hh][qs, :], dom[hh][qs, :]
                    s = lax.dot_general(qb, kr[ks, :], (((1,), (1,)), ((), ())), preferred_element_type=F32) + bb
                    p = jnp.exp(s - lse[hh][qs, :])
                    dp = lax.dot_general(dob, vr[ks, :], (((1,), (1,)), ((), ())), preferred_element_type=F32)
                    ds = p * (dp - delta[hh][qs, :])
                    if n == 0:
                        ds_ref[hh, :, BAND:] += ds
                    else:
                        ds_ref[hh] += ds
                    dsm = ds.astype(MXU_DTYPE)
                    dq_h.append(jnp.dot(dsm, kr[ks, :], preferred_element_type=F32))
                    dk1 = lax.dot_general(dsm, qb, (((0,), (0,)), ((), ())), preferred_element_type=F32)
                    dv1 = lax.dot_general(p.astype(MXU_DTYPE), dob, (((0,), (0,)), ((), ())), preferred_element_type=F32)
                    dkc = dk1 if dkc is None else dkc + dk1
                    dvc = dv1 if dvc is None else dvc + dv1
                dqs[qs, :] = jnp.where(first, dq_h[0], dq_h[1]) * 0.125
                dks[ks, :] += dkc
                dvs[ks, :] += dvc
            dq_ref[rows, :] = dqs[0:M, :]
            dk_ref[rows, :] = dks[0:M, :]
            dv_ref[rows, :] = dvs[0:M, :]

    def body(q_ref, k_ref, v_ref, do_ref, o_ref, l_ref, b_ref, dq_ref, dk_ref, dv_ref, ds_ref, dqs, dks, dvs):
        g = pl.program_id(0)

        @pl.when(pl.program_id(2) == 0)
        def _():
            ds_ref[...] = jnp.zeros_like(ds_ref)

        for gi, dil in enumerate(DILATIONS):
            pl.when(g == gi)(functools.partial(group_body, dil, q_ref, k_ref, v_ref, do_ref, o_ref, l_ref, b_ref,
                                               dq_ref, dk_ref, dv_ref, ds_ref, dqs, dks, dvs))

    blk = (L, 2 * HEAD_DIM)
    gcol = lambda g, h, b: (b, g * HP + h)
    hcol = lambda g, h, b: (b, h)
    return _pallas(
        body, name=name, grid=(3, HP, nb_),
        in_specs=[pl.BlockSpec(blk, gcol), pl.BlockSpec(blk, gcol),
                  pl.BlockSpec(blk, lambda g, h, b: (b, 3 * HP + g * HP + h)),
                  pl.BlockSpec(blk, hcol), pl.BlockSpec(blk, hcol), pl.BlockSpec(blk, hcol),
                  pl.BlockSpec((2, BAND, 2 * BAND), lambda g, h, b: (g * HP + h, 0, 0))],
        out_specs=[pl.BlockSpec(blk, gcol), pl.BlockSpec(blk, gcol), pl.BlockSpec(blk, gcol),
                   pl.BlockSpec((2, BAND, 2 * BAND), lambda g, h, b: (g * HP + h, 0, 0))],
        out_shape=[jax.ShapeDtypeStruct((T, W3), F32), jax.ShapeDtypeStruct((T, W3), F32),
                   jax.ShapeDtypeStruct((T, W3), F32), jax.ShapeDtypeStruct((3 * hpg, BAND, 2 * BAND), F32)],
        scratch_shapes=[pltpu.VMEM((mmax, 2 * HEAD_DIM), F32)] * 3,
        compiler_params=_params(("arbitrary", "arbitrary", "arbitrary")),
    )(q, kv, kv, do, o, lse, bias)


def _bias_grad(ds_sum, hpg, *, name):
    nh = ds_sum.shape[0]
    idx = np.stack([np.where(_band_tables(dil)[1], _band_tables(dil)[0], -1) for dil in DILATIONS]).astype(np.int32)

    def body(ds_ref, idx_ref, o_ref):
        d = ds_ref[...]
        ix = idx_ref[...]
        lane = lax.broadcasted_iota(jnp.int32, (8, 128), 1)
        row = jnp.zeros((8, 128), F32)
        for b in range(REL_BUCKETS):
            row = row + jnp.where(lane == b, jnp.sum(jnp.where(ix == b, d, 0.0)), 0.0)
        o_ref[...] = row

    out = _pallas(
        body, name=name, grid=(nh,),
        in_specs=[pl.BlockSpec((None, BAND, 2 * BAND), lambda h: (h, 0, 0)),
                  pl.BlockSpec((None, BAND, 2 * BAND), lambda h: (h // hpg, 0, 0))],
        out_specs=pl.BlockSpec((None, 8, 128), lambda h: (h, 0, 0)),
        out_shape=jax.ShapeDtypeStruct((nh, 8, 128), F32),
        compiler_params=_params(("parallel",)),
    )(ds_sum, jnp.asarray(idx))
    return out[:, 0, :REL_BUCKETS].T


def _adamw(w, g, m, v, *, name):
    Rw, C = w.shape
    tm = _pick(Rw, (512, 352, 256, 128, 64, 32, 16, 8))

    def body(w_ref, g_ref, m_ref, v_ref, d_ref, nm_ref, nv_ref):
        gg = g_ref[...]
        nm = ADAM_B1 * m_ref[...] + (1.0 - ADAM_B1) * gg
        nv = ADAM_B2 * v_ref[...] + (1.0 - ADAM_B2) * (gg * gg)
        m_hat = nm / (1.0 - ADAM_B1 ** ADAM_STEP)
        v_hat = nv / (1.0 - ADAM_B2 ** ADAM_STEP)
        d_ref[...] = -ADAM_LR * (m_hat / (jnp.sqrt(v_hat) + ADAM_EPS) + ADAM_WD * w_ref[...])
        nm_ref[...] = nm
        nv_ref[...] = nv

    return _pallas(
        body, name=name, grid=(Rw // tm,), in_specs=[_rows(tm, C)] * 4, out_specs=[_rows(tm, C)] * 3,
        out_shape=[jax.ShapeDtypeStruct((Rw, C), F32)] * 3, compiler_params=_params(("parallel",)),
    )(w, g, m, v)


ROW_TILE_ELEMS = 256 * 1024


def _tile_rows(r, c):
    best = 8
    for t in range(8, r + 1, 8):
        if r % t == 0 and t * c <= ROW_TILE_ELEMS:
            best = t
    return best


def _adamw_halves(w, m, v, mine, other, cidx, *, name):
    _, r, c = w.shape
    tm = _tile_rows(r, c)

    def body(c_ref, w_ref, m_ref, v_ref, a_ref, b_ref, g_ref, d_ref, nm_ref, nv_ref):
        gg = jnp.where(pl.program_id(0) == c_ref[0], a_ref[...], b_ref[...])
        nm = ADAM_B1 * m_ref[...] + (1.0 - ADAM_B1) * gg
        nv = ADAM_B2 * v_ref[...] + (1.0 - ADAM_B2) * (gg * gg)
        m_hat = nm / (1.0 - ADAM_B1 ** ADAM_STEP)
        v_hat = nv / (1.0 - ADAM_B2 ** ADAM_STEP)
        g_ref[...] = gg
        d_ref[...] = -ADAM_LR * (m_hat / (jnp.sqrt(v_hat) + ADAM_EPS) + ADAM_WD * w_ref[...])
        nm_ref[...] = nm
        nv_ref[...] = nv

    half = pl.BlockSpec((None, tm, c), lambda h, i, cr: (h, i, 0))
    one = pl.BlockSpec((None, tm, c), lambda h, i, cr: (0, i, 0))
    spec = pltpu.PrefetchScalarGridSpec(num_scalar_prefetch=1, grid=(2, r // tm),
                                        in_specs=[half, half, half, one, one], out_specs=[half] * 4)
    return _pallas(
        body, name=name, grid_spec=spec, out_shape=[jax.ShapeDtypeStruct((2, r, c), F32)] * 4,
        compiler_params=_params(("parallel", "parallel")),
    )(cidx, w, m, v, mine, other)


def _pair_sum(g, theirs, cidx, *, cast, name):
    _, _, r, c = g.shape
    tm = _tile_rows(r, c)

    def body(c_ref, g_ref, t_ref, *outs):
        s = g_ref[...] + t_ref[...]
        outs[0][...] = s
        if cast:
            outs[1][...] = s.astype(BF16)

    blk = (None, None, tm, c)
    first = pl.BlockSpec(blk, lambda p, i, cr: (p, 0, i, 0))
    shapes = [jax.ShapeDtypeStruct((4, 1, r, c), F32)] + ([jax.ShapeDtypeStruct((4, 1, r, c), BF16)] if cast else [])
    spec = pltpu.PrefetchScalarGridSpec(
        num_scalar_prefetch=1, grid=(4, r // tm),
        in_specs=[pl.BlockSpec(blk, lambda p, i, cr: (p, cr[0], i, 0)), first], out_specs=[first] * len(shapes))
    return _pallas(body, name=name, grid_spec=spec, out_shape=shapes,
                   compiler_params=_params(("parallel", "parallel")))(cidx, g, theirs)


def _chip_sum(hf, got, chip_idx, *, name):
    _, _, r, c = hf.shape
    tm = _tile_rows(r, c)

    def body(p_ref, h_ref, r_ref, o_ref):
        s = h_ref[...]
        for k in range(3):
            s = s + r_ref[k].astype(F32)
        o_ref[...] = s

    spec = pltpu.PrefetchScalarGridSpec(
        num_scalar_prefetch=1, grid=(r // tm,),
        in_specs=[pl.BlockSpec((None, None, tm, c), lambda i, pr: (pr[0], 0, i, 0)),
                  pl.BlockSpec((3, None, tm, c), lambda i, pr: (0, 0, i, 0))],
        out_specs=pl.BlockSpec((None, tm, c), lambda i, pr: (0, i, 0)))
    return _pallas(body, name=name, grid_spec=spec, out_shape=jax.ShapeDtypeStruct((1, r, c), F32),
                   compiler_params=_params(("parallel",)))(chip_idx, hf, got)


def _place():
    x, y, c = lax.axis_index("x"), lax.axis_index("y"), lax.axis_index("c")
    chips = [(1 - x, y), (x, 1 - y), (1 - x, 1 - y)]
    return x, y, c, chips


_ANY = pl.BlockSpec(memory_space=pl.ANY)


def _comm_call(body, ins, out_shapes, n_remote, *, name):
    sems = [pltpu.SemaphoreType.DMA((n,)) for n in n_remote]
    return _pallas(
        body, name=name, in_specs=[_ANY] * len(ins), out_specs=[_ANY] * len(out_shapes), out_shape=out_shapes,
        scratch_shapes=sems, compiler_params=pltpu.CompilerParams(has_side_effects=True),
    )(*ins)


def _rcopy(src, dst, ssem, rsem, dev):
    return pltpu.make_async_remote_copy(src_ref=src, dst_ref=dst, send_sem=ssem, recv_sem=rsem,
                                        device_id=dev, device_id_type=MESH)


def _all_gather(shards, *, name):
    n = len(shards)

    def body(*refs):
        ins, outs = refs[:n], refs[n:2 * n]
        s_ici, r_ici, s_d2d, r_d2d = refs[2 * n:]
        x, y, c, chips = _place()
        me = 2 * x + y
        sib = (x, y, 1 - c)
        sends = []
        for a in range(n):
            for k, (tx, ty) in enumerate(chips):
                cp = _rcopy(ins[a].at[c], outs[a].at[me, c], s_ici.at[3 * a + k], r_ici.at[3 * a + k], (tx, ty, c))
                cp.start()
                sends.append(cp)
        for a in range(n):
            for k, (tx, ty) in enumerate(chips):
                pk = 2 * tx + ty
                _rcopy(ins[a].at[c], outs[a].at[pk, c], s_ici.at[3 * a + k], r_ici.at[3 * a + k], (tx, ty, c)).wait_recv()
                fw = _rcopy(outs[a].at[pk, c], outs[a].at[pk, c], s_d2d.at[3 * a + k], r_d2d.at[3 * a + k], sib)
                fw.start()
                sends.append(fw)
        for a in range(n):
            for k, (tx, ty) in enumerate(chips):
                pk = 2 * tx + ty
                _rcopy(ins[a].at[c], outs[a].at[pk, 1 - c], s_d2d.at[3 * a + k], r_d2d.at[3 * a + k], sib).wait_recv()
        for cp in sends:
            cp.wait_send()

    shapes = [jax.ShapeDtypeStruct((4,) + s.shape, s.dtype) for s in shards]
    return _comm_call(body, shards, shapes, [3 * n] * 4, name=name)


def _gather(shards, chip, *, name):
    outs = _all_gather(shards, name=name)
    return [lax.dynamic_update_slice(o, s[None], (chip, 0, 0, 0)) for o, s in zip(outs, shards)]


def _pair_send(gs, *, name):
    n = len(gs)

    def body(*refs):
        ins, theirs = refs[:n], refs[n:2 * n]
        ssem, rsem = refs[2 * n:]
        x, y, c, _ = _place()
        sib = (x, y, 1 - c)
        cps = []
        for a in range(n):
            cp = _rcopy(ins[a].at[:, pl.ds(1 - c, 1)], theirs[a], ssem.at[a], rsem.at[a], sib)
            cp.start()
            cps.append(cp)
        for cp in cps:
            cp.wait_send()
            cp.wait_recv()

    shapes = [jax.ShapeDtypeStruct((4, 1) + g.shape[2:], g.dtype) for g in gs]
    return _comm_call(body, gs, shapes, [n, n], name=name)


def _chip_exchange(hx, *, name):
    n = len(hx)

    def body(*refs):
        hxr, got = refs[:n], refs[n:2 * n]
        ssem, rsem = refs[2 * n:]
        x, y, c, chips = _place()
        cps = []
        for a in range(n):
            for k, (tx, ty) in enumerate(chips):
                cp = _rcopy(hxr[a].at[2 * tx + ty], got[a].at[k], ssem.at[3 * a + k], rsem.at[3 * a + k], (tx, ty, c))
                cp.start()
                cps.append(cp)
        for cp in cps:
            cp.wait_send()
            cp.wait_recv()

    shapes = [jax.ShapeDtypeStruct((3,) + h.shape[1:], h.dtype) for h in hx]
    return _comm_call(body, hx, shapes, [3 * n, 3 * n], name=name)


def _pair_swap(fs, *, name):
    n = len(fs)

    def body(*refs):
        ins, outs = refs[:n], refs[n:2 * n]
        ssem, rsem = refs[2 * n:]
        x, y, c, _ = _place()
        cps = []
        for a in range(n):
            cp = _rcopy(ins[a], outs[a], ssem.at[a], rsem.at[a], (x, y, 1 - c))
            cp.start()
            cps.append(cp)
        for cp in cps:
            cp.wait_send()
            cp.wait_recv()

    shapes = [jax.ShapeDtypeStruct(f.shape, f.dtype) for f in fs]
    return _comm_call(body, fs, shapes, [n, n], name=name)


def _reduce_scatter(grads, exch_bf16, cidx, chip_idx, tag):
    n = len(grads)
    theirs = _pair_send(grads, name=f"rs_pair_send_{tag}")
    hf, hx = [], []
    for a in range(n):
        res = _pair_sum(grads[a], theirs[a], cidx, cast=exch_bf16[a], name=f"rs_pair_sum_{tag}{a}")
        hf.append(res[0])
        hx.append(res[1] if exch_bf16[a] else res[0])
    got = _chip_exchange(hx, name=f"rs_chip_exchange_{tag}")
    mine = [_chip_sum(hf[a], got[a], chip_idx, name=f"rs_chip_sum_{tag}{a}") for a in range(n)]
    return mine, _pair_swap(mine, name=f"rs_pair_swap_{tag}")


def _interleave(a, B, L):
    return a.reshape(B, L, -1).transpose(1, 0, 2).reshape(B * L, -1)


def _deinterleave(a, B, L):
    return a.reshape(L, B, -1).transpose(1, 0, 2).reshape(B * L, -1)


def _local_step(x, tgt, W, S):
    B, L, D = x.shape
    T = B * L
    G = D // SSM_GROUP
    Pst = SSM_STATE
    hpg = D // HEAD_DIM
    HW = hpg * HEAD_DIM
    ncl = G // GROUPS_PER_CLUSTER
    x2 = x.reshape(T, D)
    tgt2 = tgt.reshape(T, D)

    disc = lambda *p: _s5_discretize(*p)
    (ab_r, ab_i, bb_r, bb_i), disc_vjp = jax.vjp(disc, S["lam_re"], S["lam_im"], S["log_dt"], S["b_re"], S["b_im"])
    wb = jnp.concatenate([_blockdiag(jnp.transpose(bb_r, (0, 2, 1))), _blockdiag(jnp.transpose(bb_i, (0, 2, 1)))],
                         axis=-1).astype(MXU_DTYPE)
    wc = jnp.concatenate([_blockdiag(jnp.transpose(S["c_re"], (0, 2, 1))), _blockdiag(-jnp.transpose(S["c_im"], (0, 2, 1)))],
                         axis=1).astype(MXU_DTYPE)
    cs = GROUPS_PER_CLUSTER * Pst
    slab = lambda ab: jnp.tile(jnp.transpose(ab.reshape(ncl, cs // LANES, LANES), (1, 0, 2)), (1, B, 1))
    a_r, a_i = slab(ab_r), slab(ab_i)
    d_row = S["d"].reshape(1, D)

    xi = _interleave(x2, B, L)
    y, yg, h_r, h_i = _s5_fwd(xi, wb, wc, a_r, a_i, d_row, B, name="s5_fwd")
    z = _mm_nn(yg, W["w_glu"], bias=S["b_glu"].reshape(1, D), name="glu_z")
    gate = _glu_gate(y, z, name="glu_gate")
    mix = _deinterleave(_mm_nn(gate, W["w_out"], name="s5_out"), B, L)
    h1, h1b, xh1, rs1 = _ln_fwd(x2, mix, S["ln_gain"][0, 0][None], S["ln_bias"][0, 0][None], name="ln_fwd_0a")

    def ffn_fwd(hb, l):
        hc = _mm_nn(hb, W["w_up"], l=l, out_dtype=MXU_DTYPE, name=f"ffn_up_{l}")
        a = _conv_glu_fwd(hc, S["conv_w"][l], S["conv_b"][l][None], L, name=f"ffn_conv_{l}")
        f = _mm_nn(a, W["w_down"], l=l, name=f"ffn_down_{l}")
        return hc, a, f

    hc0, a0, f0 = ffn_fwd(h1b, 0)
    h2, h2b, xh2, rs2 = _ln_fwd(h1, f0, S["ln_gain"][0, 1][None], S["ln_bias"][0, 1][None], name="ln_fwd_0b")

    kv = _mm_nn(h2b, W["w_kv"], name="attn_kv")
    q = _mm_nn(h2b, W["w_q"], name="attn_q")
    bias = _attn_bias(S["rel_bias"], hpg)
    o3, l3 = _attn_fwd(q, kv, bias, L, hpg, name="attn_fwd")
    o, ob, lse = _attn_merge(o3, l3, HW, name="attn_merge")
    att = _mm_nn(ob, W["w_ao"], name="attn_out")
    h3, h3b, xh3, rs3 = _ln_fwd(h2, att, S["ln_gain"][1, 0][None], S["ln_bias"][1, 0][None], name="ln_fwd_1a")
    hc1, a1, f1 = ffn_fwd(h3b, 1)
    h4, _, xh4, rs4 = _ln_fwd(h3, f1, S["ln_gain"][1, 1][None], S["ln_bias"][1, 1][None], name="ln_fwd_1b")

    dh4, lrow = _loss_grad(h4, tgt2, name="loss")
    loss = lrow[0, 0]

    GW, GS = {}, {}

    def ffn_bwd(dzb, hb, hc, a, l):
        da = _mm_nt(dzb, W["w_down"], l=l, out_dtype=MXU_DTYPE, name=f"ffn_down_bwd_x_{l}")
        GW["w_down"] = _tn(a, dzb, ptotal=1, nl=DEPTH, l=l, np_cols=D, prev=GW.get("w_down"), name=f"ffn_down_bwd_w_{l}")
        dc, dcw, dcb = _conv_glu_bwd(hc, da, S["conv_w"][l], S["conv_b"][l][None], L, name=f"ffn_conv_bwd_{l}")
        dhc = _conv_bwd_input(dc, S["conv_w"][l], L, name=f"ffn_conv_bwd_x_{l}")
        dh = _mm_nt(dhc, W["w_up"], l=l, name=f"ffn_up_bwd_x_{l}")
        GW["w_up"] = _tn(hb, dhc, ptotal=W["w_up"].shape[0], nl=DEPTH, l=l, np_cols=W["w_up"].shape[3],
                         prev=GW.get("w_up"), name=f"ffn_up_bwd_w_{l}")
        return dh, dcw, dcb

    dz4, dz4b, dg4, db4 = _ln_bwd([dh4], [1.0], xh4, rs4, S["ln_gain"][1, 1][None], name="ln_bwd_1b")
    dh3f, dcw1, dcb1 = ffn_bwd(dz4b, h3b, hc1, a1, 1)
    dz3, dz3b, dg3, db3 = _ln_bwd([dz4, dh3f], [DN_ALPHA, 1.0], xh3, rs3, S["ln_gain"][1, 0][None], name="ln_bwd_1a")
    do = _mm_nt(dz3b, W["w_ao"], name="attn_out_bwd_x")
    GW["w_ao"] = _tn(ob, dz3b, ptotal=1, np_cols=D, name="attn_out_bwd_w")
    dq, dk, dv, ds_sum = _attn_bwd(q, kv, do, o, lse, bias, L, hpg, name="attn_bwd")
    GS["rel_bias"] = _bias_grad(ds_sum, hpg, name="attn_bias_grad")
    GW["w_q"] = _tn(h2b, dq, ptotal=W["w_q"].shape[0], np_cols=W["w_q"].shape[3], name="attn_q_bwd_w")
    pkv, npkv = W["w_kv"].shape[0], W["w_kv"].shape[3]
    gkv = _tn(h2b, dk, ptotal=pkv, np_cols=npkv, p0=0, name="attn_k_bwd_w")
    GW["w_kv"] = _tn(h2b, dv, ptotal=pkv, np_cols=npkv, p0=pkv // 2, prev=gkv, name="attn_v_bwd_w")
    dh2q = _mm_nt(dq, W["w_q"], name="attn_q_bwd_x")
    dh2k = _mm_nt(dk, W["w_kv"], p0=0, pn=pkv // 2, name="attn_k_bwd_x")
    dh2v = _mm_nt(dv, W["w_kv"], p0=pkv // 2, pn=pkv // 2, name="attn_v_bwd_x")

    dz2, dz2b, dg2, db2 = _ln_bwd([dz3, dh2q, dh2k, dh2v], [DN_ALPHA, 1.0, 1.0, 1.0], xh2, rs2,
                                  S["ln_gain"][0, 1][None], name="ln_bwd_0b")
    dh1f, dcw0, dcb0 = ffn_bwd(dz2b, h1b, hc0, a0, 0)
    dz1, dz1b, dg1, db1 = _ln_bwd([dz2, dh1f], [DN_ALPHA, 1.0], xh1, rs1, S["ln_gain"][0, 0][None], name="ln_bwd_0a")
    dmix_i = _interleave(dz1b, B, L)
    dgate = _mm_nt(dmix_i, W["w_out"], name="s5_out_bwd_x")
    GW["w_out"] = _tn(gate, dmix_i, ptotal=1, np_cols=D, name="s5_out_bwd_w")
    dzg, dyg1, dbglu = _glu_bwd(y, z, dgate, name="glu_bwd")
    dyg2 = _mm_nt(dzg, W["w_glu"], name="glu_z_bwd_x")
    GW["w_glu"] = _tn(yg, dzg, ptotal=1, np_cols=D, name="glu_z_bwd_w")
    dy = _gelu_bwd(y, dyg1, dyg2, name="gelu_bwd")
    du_i, g_r, g_i, dar, dai, dd = _s5_bwd(dy, xi, h_r, h_i, wb, wc, a_r, a_i, d_row, B, name="s5_bwd")
    dwb_r = _cluster_tn(xi, g_r, ncl, tok_left=True, name="s5_b_grad_re")
    dwb_i = _cluster_tn(xi, g_i, ncl, tok_left=True, name="s5_b_grad_im")
    dwc_r = _cluster_tn(dy, h_r, ncl, tok_left=False, name="s5_c_grad_re")
    dwc_i = _cluster_tn(dy, h_i, ncl, tok_left=False, name="s5_c_grad_im")
    grad_x = _axpy(dz1, _deinterleave(du_i, B, L), DN_ALPHA, name="grad_x")

    dbb_r = jnp.transpose(_unblockdiag(dwb_r, SSM_GROUP, Pst), (0, 2, 1))
    dbb_i = jnp.transpose(_unblockdiag(dwb_i, SSM_GROUP, Pst), (0, 2, 1))
    unslab = lambda da: jnp.transpose(da.reshape(cs // LANES, B, ncl, LANES).sum(1), (1, 0, 2)).reshape(G, Pst)
    dab_r, dab_i = unslab(dar), unslab(dai)
    GS["lam_re"], GS["lam_im"], GS["log_dt"], GS["b_re"], GS["b_im"] = disc_vjp((dab_r, dab_i, dbb_r, dbb_i))
    GS["c_re"] = jnp.transpose(_unblockdiag(dwc_r, Pst, SSM_GROUP), (0, 2, 1))
    GS["c_im"] = -jnp.transpose(_unblockdiag(dwc_i, Pst, SSM_GROUP), (0, 2, 1))
    GS["d"] = dd.reshape(G, SSM_GROUP)
    GS["b_glu"] = dbglu.reshape(D)
    GS["conv_w"] = jnp.stack([dcw0, dcw1])
    GS["conv_b"] = jnp.stack([dcb0[0], dcb1[0]])
    GS["ln_gain"] = jnp.stack([jnp.stack([dg1[0], dg2[0]]), jnp.stack([dg3[0], dg4[0]])])
    GS["ln_bias"] = jnp.stack([jnp.stack([db1[0], db2[0]]), jnp.stack([db3[0], db4[0]])])
    return loss, grad_x.reshape(B, L, D), GW, GS


SMALL_REPLICATED = ("lam_re", "lam_im", "log_dt", "b_re", "b_im", "c_re", "c_im", "d", "rel_bias", "conv_b")
SMALL_SHARDED = ("b_glu", "conv_w", "ln_gain", "ln_bias")
SMALL_ORDER = SMALL_REPLICATED + SMALL_SHARDED


def _pack(arrs, lanes, row_mult):
    flat = jnp.concatenate([a.reshape(-1).astype(F32) for a in arrs])
    rows = -(-flat.shape[0] // lanes)
    rows = -(-rows // row_mult) * row_mult
    return jnp.pad(flat, (0, rows * lanes - flat.shape[0])).reshape(rows, lanes)


def _unpack(packed, shapes):
    flat = packed.reshape(-1)
    out, off = [], 0
    for s in shapes:
        n = int(np.prod(s))
        out.append(flat[off:off + n].reshape(s))
        off += n
    return out


def kernel(x, s5_lam_re, s5_lam_im, s5_log_dt, s5_b_re, s5_b_im, s5_c_re, s5_c_im, s5_d, s5_w_glu, s5_b_glu, s5_w_out, attn_w_kv, attn_w_q, attn_w_out, rel_bias, ffn_w_up, ffn_conv_w, ffn_conv_b, ffn_w_down, ln_gain, ln_bias, loss_target, m_s5_lam_re, m_s5_lam_im, m_s5_log_dt, m_s5_b_re, m_s5_b_im, m_s5_c_re, m_s5_c_im, m_s5_d, m_s5_w_glu, m_s5_b_glu, m_s5_w_out, m_attn_w_kv, m_attn_w_q, m_attn_w_out, m_rel_bias, m_ffn_w_up, m_ffn_conv_w, m_ffn_conv_b, m_ffn_w_down, m_ln_gain, m_ln_bias, v_s5_lam_re, v_s5_lam_im, v_s5_log_dt, v_s5_b_re, v_s5_b_im, v_s5_c_re, v_s5_c_im, v_s5_d, v_s5_w_glu, v_s5_b_glu, v_s5_w_out, v_attn_w_kv, v_attn_w_q, v_attn_w_out, v_rel_bias, v_ffn_w_up, v_ffn_conv_w, v_ffn_conv_b, v_ffn_w_down, v_ln_gain, v_ln_bias):
    names = ["s5_lam_re", "s5_lam_im", "s5_log_dt", "s5_b_re", "s5_b_im", "s5_c_re", "s5_c_im", "s5_d", "s5_w_glu",
             "s5_b_glu", "s5_w_out", "attn_w_kv", "attn_w_q", "attn_w_out", "rel_bias", "ffn_w_up", "ffn_conv_w",
             "ffn_conv_b", "ffn_w_down", "ln_gain", "ln_bias"]
    loc = locals()
    w_in = {n: loc[n] for n in names}
    m_in = {n: loc["m_" + n] for n in names}
    v_in = {n: loc["v_" + n] for n in names}
    chip = 2 * lax.axis_index("x") + lax.axis_index("y")
    core = lax.axis_index("c")
    chip_idx = jnp.reshape(chip, (1,)).astype(jnp.int32)
    cidx = jnp.reshape(core, (1,)).astype(jnp.int32)

    big = [("w_glu", "s5_w_glu", "rows"), ("w_out", "s5_w_out", "rows"), ("w_ao", "attn_w_out", "rows"),
           ("w_kv", "attn_w_kv", "cols"), ("w_q", "attn_w_q", "cols"),
           ("w_up", "ffn_w_up", "layer_cols"), ("w_down", "ffn_w_down", "layer_rows")]

    def halves(t, kind):
        if kind.startswith("layer"):
            return t
        r, c = t.shape[-2:]
        return t.reshape(2, r // 2, c)

    def to_weight(g, kind):
        _, _, r, c = g.shape
        if kind == "rows":
            return g.reshape(1, 1, 8 * r, c)
        if kind == "cols":
            return g.reshape(4, 1, 2 * r, c)
        if kind == "layer_cols":
            return g
        return jnp.transpose(g, (1, 0, 2, 3)).reshape(1, 2, 4 * r, c)

    def from_weight_grad(gw, kind, r, c):
        if kind == "layer_rows":
            return jnp.transpose(gw.reshape(2, 4, r, c), (1, 0, 2, 3))
        return gw.reshape(4, 2, r, c)

    small_sh = {"b_glu": s5_b_glu[0], "conv_w": ffn_conv_w, "ln_gain": ln_gain, "ln_bias": ln_bias}
    sh_shapes = [small_sh[k].shape for k in SMALL_SHARDED]
    sh_pack = _pack([small_sh[k] for k in SMALL_SHARDED], 128, 16)

    shards = [halves(w_in[src].astype(MXU_DTYPE), kind) for _, src, kind in big]
    shards.append(sh_pack.reshape(2, sh_pack.shape[0] // 2, 128))
    gathered = _gather(shards, chip, name="weights_all_gather")

    W = {key: to_weight(g, kind) for (key, _, kind), g in zip(big, gathered[:-1])}
    parts = [_unpack(gathered[-1][p], sh_shapes) for p in range(4)]
    S = {k: jnp.concatenate([parts[p][i] for p in range(4)], axis=-1) for i, k in enumerate(SMALL_SHARDED)}
    S.update(lam_re=s5_lam_re[0], lam_im=s5_lam_im[0], log_dt=s5_log_dt[0], b_re=s5_b_re[0], b_im=s5_b_im[0],
             c_re=s5_c_re[0], c_im=s5_c_im[0], d=s5_d[0], rel_bias=rel_bias, conv_b=ffn_conv_b)

    loss, grad_x, GW, GS = _local_step(x, loss_target, W, S)
    loss = lax.psum(loss, ("x", "y", "c"))

    gs_shapes = [GS[k].shape for k in SMALL_ORDER]
    gs_pack = _pack([GS[k] for k in SMALL_ORDER], 128, 64)
    rs = gs_pack.shape[0] // 8
    grads = []
    for (key, _, kind), sh in zip(big, shards):
        grads.append(from_weight_grad(GW[key], kind, sh.shape[1], sh.shape[2]))
    grads.append(gs_pack.reshape(4, 2, rs, 128))
    mine, other = _reduce_scatter(grads, [True] * len(big) + [False], cidx, chip_idx, "g")
    small_halves = jnp.where(core == 0, jnp.concatenate([mine[-1], other[-1]]), jnp.concatenate([other[-1], mine[-1]]))
    small_all = _gather([small_halves], chip, name="small_grads_all_gather")[0]
    gsmall = dict(zip(SMALL_ORDER, _unpack(small_all, gs_shapes)))

    big_res = {}
    for (key, src, kind), gm, go in zip(big, mine[:-1], other[:-1]):
        res4 = _adamw_halves(halves(w_in[src], kind), halves(m_in[src], kind), halves(v_in[src], kind), gm, go, cidx,
                             name=f"adamw_{key}")
        big_res[src] = tuple(t.reshape(w_in[src].shape) for t in res4)

    def big_out(i):
        return {src: big_res[src][i] for _, src, _ in big}

    small_w = {"lam_re": s5_lam_re, "lam_im": s5_lam_im, "log_dt": s5_log_dt, "b_re": s5_b_re, "b_im": s5_b_im,
               "c_re": s5_c_re, "c_im": s5_c_im, "d": s5_d, "rel_bias": rel_bias, "conv_b": ffn_conv_b,
               "b_glu": s5_b_glu, "conv_w": ffn_conv_w, "ln_gain": ln_gain, "ln_bias": ln_bias}
    small_name = {"lam_re": "s5_lam_re", "lam_im": "s5_lam_im", "log_dt": "s5_log_dt", "b_re": "s5_b_re", "b_im": "s5_b_im",
                  "c_re": "s5_c_re", "c_im": "s5_c_im", "d": "s5_d", "rel_bias": "rel_bias", "conv_b": "ffn_conv_b",
                  "b_glu": "s5_b_glu", "conv_w": "ffn_conv_w", "ln_gain": "ln_gain", "ln_bias": "ln_bias"}
    sg = {}
    for k in SMALL_ORDER:
        shp = small_w[k].shape
        g = gsmall[k]
        if k in SMALL_SHARDED:
            width = shp[-1]
            g = lax.dynamic_slice_in_dim(g, chip * width, width, axis=g.ndim - 1)
        sg[k] = g.reshape(shp)
    sshapes = [small_w[k].shape for k in SMALL_ORDER]
    pw = _pack([small_w[k] for k in SMALL_ORDER], 128, 512)
    pg = _pack([sg[k] for k in SMALL_ORDER], 128, 512)
    pm = _pack([m_in[small_name[k]] for k in SMALL_ORDER], 128, 512)
    pv = _pack([v_in[small_name[k]] for k in SMALL_ORDER], 128, 512)
    sd, snm, snv = _adamw(pw, pg, pm, pv, name="adamw_small")
    sd = dict(zip(SMALL_ORDER, _unpack(sd, sshapes)))
    snm = dict(zip(SMALL_ORDER, _unpack(snm, sshapes)))
    snv = dict(zip(SMALL_ORDER, _unpack(snv, sshapes)))

    res = [{}, {}, {}, {}]
    for i in range(4):
        res[i].update(big_out(i))
    for k in SMALL_ORDER:
        res[0][small_name[k]] = sg[k]
        res[1][small_name[k]] = sd[k]
        res[2][small_name[k]] = snm[k]
        res[3][small_name[k]] = snv[k]
    outs = [loss, grad_x]
    for i in range(4):
        outs += [res[i][n] for n in names]
    return tuple(outs)
```

```python
import functools
import math

import numpy as np
import jax
import jax.numpy as jnp
from jax import lax
from jax.experimental import pallas as pl
from jax.experimental.pallas import tpu as pltpu

F32 = jnp.float32
BF16 = jnp.bfloat16
MXU_DTYPE = jnp.bfloat16
V7X_VMEM_LIMIT_BYTES = 52 << 20
MESH = pl.DeviceIdType.MESH

DEPTH = 2
SSM_GROUP = 16
SSM_STATE = 64
GROUPS_PER_CLUSTER = 16
CLUSTER_W = GROUPS_PER_CLUSTER * SSM_GROUP
HEAD_DIM = 64
DILATIONS = (1, 4, 16)
BAND = 128
NEG_BIG = -1e30
REL_BUCKETS = 32
REL_MAX_DIST = 2048
DN_ALPHA = (2.0 * DEPTH) ** 0.25
LN_EPS = 1e-5
ADAM_LR, ADAM_B1, ADAM_B2, ADAM_EPS, ADAM_WD, ADAM_STEP = 0.001, 0.9, 0.999, 1e-08, 0.01, 10
GELU_K = math.sqrt(2.0 / math.pi)
GELU_C = 0.044715


def _pallas(body, **kw):
    return pl.pallas_call(body, **kw)


def _params(sem=None):
    return pltpu.CompilerParams(dimension_semantics=sem, vmem_limit_bytes=V7X_VMEM_LIMIT_BYTES)


def _pick(n, cands):
    for c in cands:
        if n % c == 0:
            return c
    return n


def _sigmoid(z):
    return 1.0 / (1.0 + jnp.exp(-z))


def _gelu(y):
    return 0.5 * y * (1.0 + jnp.tanh(GELU_K * (y + GELU_C * y * y * y)))


def _gelu_grad(y):
    t = jnp.tanh(GELU_K * (y + GELU_C * y * y * y))
    return 0.5 * (1.0 + t) + 0.5 * y * (1.0 - t * t) * (GELU_K * (1.0 + 3.0 * GELU_C * y * y))


def _mm_nn(a, w, *, l=0, bias=None, out_dtype=F32, name):
    T, K = a.shape
    P, _, _, Np = w.shape
    tm = _pick(T, (1024, 512, 256, 128))
    tn = _pick(Np, (1408, 1024, 768, 512, 384, 256, 128))
    nj = Np // tn

    def body(*refs):
        if bias is None:
            a_ref, w_ref, o_ref = refs
        else:
            a_ref, w_ref, b_ref, o_ref = refs
        acc = jnp.dot(a_ref[...].astype(MXU_DTYPE), w_ref[...].astype(MXU_DTYPE), preferred_element_type=F32)
        if bias is not None:
            acc = acc + b_ref[...]
        o_ref[...] = acc.astype(o_ref.dtype)

    in_specs = [pl.BlockSpec((tm, K), lambda p, j, i: (i, 0)),
                pl.BlockSpec((None, None, K, tn), lambda p, j, i: (p, l, 0, j))]
    args = [a, w]
    if bias is not None:
        in_specs.append(pl.BlockSpec((1, tn), lambda p, j, i: (0, p * nj + j)))
        args.append(bias)
    return _pallas(
        body, name=name, grid=(P, nj, T // tm), in_specs=in_specs,
        out_specs=pl.BlockSpec((tm, tn), lambda p, j, i: (i, p * nj + j)),
        out_shape=jax.ShapeDtypeStruct((T, P * Np), out_dtype),
        compiler_params=_params(("parallel", "parallel", "parallel")),
    )(*args)


def _mm_nt(a, w, *, l=0, p0=0, pn=None, out_dtype=F32, name):
    T = a.shape[0]
    _, _, K, Np = w.shape
    pn = w.shape[0] if pn is None else pn
    tm = _pick(T, (1024, 512, 256, 128) if K <= 1024 else (512, 256, 128))
    tn = _pick(Np, (1536, 1408, 1024, 768, 512, 384, 256, 128))
    nj = Np // tn
    nred = pn * nj

    def body(a_ref, w_ref, o_ref, acc):
        r = pl.program_id(1)

        @pl.when(r == 0)
        def _():
            acc[...] = jnp.zeros_like(acc)

        acc[...] += lax.dot_general(a_ref[...].astype(MXU_DTYPE), w_ref[...].astype(MXU_DTYPE),
                                    (((1,), (1,)), ((), ())), preferred_element_type=F32)

        @pl.when(r == nred - 1)
        def _():
            o_ref[...] = acc[...].astype(o_ref.dtype)

    return _pallas(
        body, name=name, grid=(T // tm, nred),
        in_specs=[pl.BlockSpec((tm, tn), lambda i, r: (i, r)),
                  pl.BlockSpec((None, None, K, tn), lambda i, r: (p0 + r // nj, l, 0, r % nj))],
        out_specs=pl.BlockSpec((tm, K), lambda i, r: (i, 0)),
        out_shape=jax.ShapeDtypeStruct((T, K), out_dtype),
        scratch_shapes=[pltpu.VMEM((tm, K), F32)],
        compiler_params=_params(("parallel", "arbitrary")),
    )(a, w)


def _tn(a, b, *, ptotal, np_cols, nl=1, l=0, p0=0, prev=None, name):
    T, K = a.shape
    Np = np_cols
    pn = b.shape[1] // Np
    tt = _pick(T, (1024, 512, 256, 128))
    tk = _pick(K, (1408, 1024, 512, 256, 128))
    tn = _pick(Np, (1408, 768, 512, 256, 128))
    if tk * tn > 1408 * 1024:
        tn = _pick(Np, (512, 256, 128))
    nj = Np // tn
    nt = T // tt

    def body(*refs):
        a_ref, b_ref = refs[0], refs[1]
        o_ref, acc = refs[-2], refs[-1]
        t = pl.program_id(3)

        @pl.when(t == 0)
        def _():
            acc[...] = jnp.zeros_like(acc)

        acc[...] += lax.dot_general(a_ref[...].astype(MXU_DTYPE), b_ref[...].astype(MXU_DTYPE),
                                    (((0,), (0,)), ((), ())), preferred_element_type=F32)

        @pl.when(t == nt - 1)
        def _():
            o_ref[...] = acc[...]

    in_specs = [pl.BlockSpec((tt, tk), lambda kb, p, j, t: (t, kb)),
                pl.BlockSpec((tt, tn), lambda kb, p, j, t: (t, p * nj + j))]
    args = [a, b]
    aliases = {}
    if prev is not None:
        in_specs.append(pl.BlockSpec(memory_space=pl.ANY))
        args.append(prev)
        aliases = {2: 0}
    return _pallas(
        body, name=name, grid=(K // tk, pn, nj, nt), in_specs=in_specs,
        out_specs=pl.BlockSpec((None, None, tk, tn), lambda kb, p, j, t: (p0 + p, l, kb, j)),
        out_shape=jax.ShapeDtypeStruct((ptotal, nl, K, Np), F32),
        scratch_shapes=[pltpu.VMEM((tk, tn), F32)],
        input_output_aliases=aliases,
        compiler_params=_params(("parallel", "parallel", "parallel", "arbitrary")),
    )(*args)


def _rows(tm, f):
    return pl.BlockSpec((tm, f), lambda i: (i, 0))


def _whole(shape):
    nd = len(shape)
    return pl.BlockSpec(shape, lambda i: (0,) * nd)


def _ln_fwd(xres, f, gain, bias, *, name):
    T, D = xres.shape
    tm = _pick(T, (256, 128))

    def body(x_ref, f_ref, g_ref, b_ref, y_ref, yb_ref, xh_ref, rs_ref):
        z = DN_ALPHA * x_ref[...] + f_ref[...]
        mu = jnp.mean(z, axis=-1, keepdims=True)
        zc = z - mu
        var = jnp.mean(zc * zc, axis=-1, keepdims=True)
        rstd = lax.rsqrt(var + LN_EPS)
        xh = zc * rstd
        y = xh * g_ref[...] + b_ref[...]
        y_ref[...] = y
        yb_ref[...] = y.astype(yb_ref.dtype)
        xh_ref[...] = xh
        rs_ref[...] = rstd

    return _pallas(
        body, name=name, grid=(T // tm,),
        in_specs=[_rows(tm, D), _rows(tm, D), _whole((1, D)), _whole((1, D))],
        out_specs=[_rows(tm, D), _rows(tm, D), _rows(tm, D), _rows(tm, 1)],
        out_shape=[jax.ShapeDtypeStruct((T, D), F32), jax.ShapeDtypeStruct((T, D), MXU_DTYPE),
                   jax.ShapeDtypeStruct((T, D), F32), jax.ShapeDtypeStruct((T, 1), F32)],
        compiler_params=_params(("parallel",)),
    )(xres, f, gain, bias)


def _ln_bwd(addends, coefs, xhat, rstd, gain, *, name):
    T, D = xhat.shape
    tm = _pick(T, (256, 128))
    n = len(addends)

    def body(*refs):
        adds = refs[:n]
        xh_ref, rs_ref, g_ref, dz_ref, dzb_ref, dg_ref, db_ref = refs[n:]
        dy = coefs[0] * adds[0][...]
        for c, r in zip(coefs[1:], adds[1:]):
            dy = dy + c * r[...]
        xh = xh_ref[...]
        dxh = dy * g_ref[...]
        m1 = jnp.mean(dxh, axis=-1, keepdims=True)
        m2 = jnp.mean(dxh * xh, axis=-1, keepdims=True)
        dz = rs_ref[...] * (dxh - m1 - xh * m2)
        dz_ref[...] = dz
        dzb_ref[...] = dz.astype(dzb_ref.dtype)

        @pl.when(pl.program_id(0) == 0)
        def _():
            dg_ref[...] = jnp.zeros_like(dg_ref)
            db_ref[...] = jnp.zeros_like(db_ref)

        dg_ref[...] += jnp.sum(dy * xh, axis=0, keepdims=True)
        db_ref[...] += jnp.sum(dy, axis=0, keepdims=True)

    return _pallas(
        body, name=name, grid=(T // tm,),
        in_specs=[_rows(tm, D)] * n + [_rows(tm, D), _rows(tm, 1), _whole((1, D))],
        out_specs=[_rows(tm, D), _rows(tm, D), _whole((1, D)), _whole((1, D))],
        out_shape=[jax.ShapeDtypeStruct((T, D), F32), jax.ShapeDtypeStruct((T, D), MXU_DTYPE),
                   jax.ShapeDtypeStruct((1, D), F32), jax.ShapeDtypeStruct((1, D), F32)],
        compiler_params=_params(("arbitrary",)),
    )(*addends, xhat, rstd, gain)


def _loss_grad(y, tgt, *, name):
    T, D = y.shape
    tm = _pick(T, (256, 128))

    def body(y_ref, t_ref, dy_ref, l_ref):
        e = y_ref[...] - t_ref[...]
        dy_ref[...] = e * (1.0 / D)

        @pl.when(pl.program_id(0) == 0)
        def _():
            l_ref[...] = jnp.zeros_like(l_ref)

        l_ref[...] += jnp.zeros_like(l_ref) + jnp.sum(e * e) * (0.5 / D)

    return _pallas(
        body, name=name, grid=(T // tm,),
        in_specs=[_rows(tm, D), _rows(tm, D)],
        out_specs=[_rows(tm, D), _whole((1, 128))],
        out_shape=[jax.ShapeDtypeStruct((T, D), F32), jax.ShapeDtypeStruct((1, 128), F32)],
        compiler_params=_params(("arbitrary",)),
    )(y, tgt)


def _axpy(a, b, ca, *, name):
    T, D = a.shape
    tm = _pick(T, (256, 128))

    def body(a_ref, b_ref, o_ref):
        o_ref[...] = ca * a_ref[...] + b_ref[...]

    return _pallas(
        body, name=name, grid=(T // tm,), in_specs=[_rows(tm, D), _rows(tm, D)], out_specs=_rows(tm, D),
        out_shape=jax.ShapeDtypeStruct((T, D), F32), compiler_params=_params(("parallel",)),
    )(a, b)


def _glu_gate(y, z, *, name):
    T, D = y.shape
    tm = _pick(T, (256, 128))

    def body(y_ref, z_ref, g_ref):
        g_ref[...] = (_gelu(y_ref[...]) * _sigmoid(z_ref[...])).astype(g_ref.dtype)

    return _pallas(
        body, name=name, grid=(T // tm,), in_specs=[_rows(tm, D), _rows(tm, D)], out_specs=_rows(tm, D),
        out_shape=jax.ShapeDtypeStruct((T, D), MXU_DTYPE), compiler_params=_params(("parallel",)),
    )(y, z)


def _glu_bwd(y, z, dg, *, name):
    T, D = y.shape
    tm = _pick(T, (256, 128))

    def body(y_ref, z_ref, dg_ref, dzb_ref, dyg_ref, db_ref):
        s = _sigmoid(z_ref[...])
        dg = dg_ref[...]
        dz = dg * _gelu(y_ref[...]) * s * (1.0 - s)
        dzb_ref[...] = dz.astype(dzb_ref.dtype)
        dyg_ref[...] = dg * s

        @pl.when(pl.program_id(0) == 0)
        def _():
            db_ref[...] = jnp.zeros_like(db_ref)

        db_ref[...] += jnp.sum(dz, axis=0, keepdims=True)

    return _pallas(
        body, name=name, grid=(T // tm,), in_specs=[_rows(tm, D)] * 3,
        out_specs=[_rows(tm, D), _rows(tm, D), _whole((1, D))],
        out_shape=[jax.ShapeDtypeStruct((T, D), MXU_DTYPE), jax.ShapeDtypeStruct((T, D), F32),
                   jax.ShapeDtypeStruct((1, D), F32)],
        compiler_params=_params(("arbitrary",)),
    )(y, z, dg)


def _gelu_bwd(y, d1, d2, *, name):
    T, D = y.shape
    tm = _pick(T, (256, 128))

    def body(y_ref, a_ref, b_ref, o_ref):
        o_ref[...] = (a_ref[...] + b_ref[...]) * _gelu_grad(y_ref[...])

    return _pallas(
        body, name=name, grid=(T // tm,), in_specs=[_rows(tm, D)] * 3, out_specs=_rows(tm, D),
        out_shape=jax.ShapeDtypeStruct((T, D), F32), compiler_params=_params(("parallel",)),
    )(y, d1, d2)


CONV_ROWS = 128
CONV_EDGE = 16


def _shift_back(x, edge, at_start, tm):
    rows = lax.broadcasted_iota(jnp.int32, x.shape, 0)
    keep = jnp.where(at_start, 0.0, 1.0)
    e7 = edge[CONV_EDGE - 1:CONV_EDGE, :] * keep
    e6 = edge[CONV_EDGE - 2:CONV_EDGE - 1, :] * keep
    r1 = pltpu.roll(x, 1, 0)
    r2 = pltpu.roll(x, 2, 0)
    x1 = jnp.where(rows == 0, e7, r1)
    x2 = jnp.where(rows == 0, e6, jnp.where(rows == 1, e7, r2))
    return x1, x2


def _conv_specs(T, F2, tm):
    return [_rows(tm, F2),
            pl.BlockSpec((CONV_EDGE, F2), lambda i: (jnp.maximum(i * (tm // CONV_EDGE) - 1, 0), 0))]


def _conv_glu_fwd(hc, conv_w, conv_b, L, *, name):
    T, F2 = hc.shape
    F = F2 // 2
    tm = CONV_ROWS

    def body(x_ref, e_ref, w_ref, b_ref, a_ref):
        at_start = (pl.program_id(0) * tm) % L == 0
        x = x_ref[...].astype(F32)
        x1, x2 = _shift_back(x, e_ref[...].astype(F32), at_start, tm)
        c = b_ref[...] + w_ref[0:1, :] * x + w_ref[1:2, :] * x1 + w_ref[2:3, :] * x2
        val, gate = c[:, :F], c[:, F:]
        a_ref[...] = (gate * _sigmoid(gate) * val).astype(a_ref.dtype)

    return _pallas(
        body, name=name, grid=(T // tm,),
        in_specs=_conv_specs(T, F2, tm) + [_whole((3, F2)), _whole((1, F2))],
        out_specs=_rows(tm, F),
        out_shape=jax.ShapeDtypeStruct((T, F), MXU_DTYPE), compiler_params=_params(("parallel",)),
    )(hc, hc, conv_w, conv_b)


def _conv_glu_bwd(hc, da, conv_w, conv_b, L, *, name):
    T, F2 = hc.shape
    F = F2 // 2
    tm = CONV_ROWS

    def body(x_ref, e_ref, da_ref, w_ref, b_ref, dc_ref, dw_ref, db_ref):
        at_start = (pl.program_id(0) * tm) % L == 0
        x = x_ref[...].astype(F32)
        x1, x2 = _shift_back(x, e_ref[...].astype(F32), at_start, tm)
        c = b_ref[...] + w_ref[0:1, :] * x + w_ref[1:2, :] * x1 + w_ref[2:3, :] * x2
        val, gate = c[:, :F], c[:, F:]
        s = _sigmoid(gate)
        da = da_ref[...].astype(F32)
        dval = da * (gate * s)
        dgate = da * val * (s * (1.0 + gate * (1.0 - s)))
        dc = jnp.concatenate([dval, dgate], axis=-1)
        dc_ref[...] = dc.astype(dc_ref.dtype)

        @pl.when(pl.program_id(0) == 0)
        def _():
            dw_ref[...] = jnp.zeros_like(dw_ref)
            db_ref[...] = jnp.zeros_like(db_ref)

        dw_ref[0:1, :] += jnp.sum(dc * x, axis=0, keepdims=True)
        dw_ref[1:2, :] += jnp.sum(dc * x1, axis=0, keepdims=True)
        dw_ref[2:3, :] += jnp.sum(dc * x2, axis=0, keepdims=True)
        db_ref[...] += jnp.sum(dc, axis=0, keepdims=True)

    return _pallas(
        body, name=name, grid=(T // tm,),
        in_specs=_conv_specs(T, F2, tm) + [_rows(tm, F), _whole((3, F2)), _whole((1, F2))],
        out_specs=[_rows(tm, F2), _whole((3, F2)), _whole((1, F2))],
        out_shape=[jax.ShapeDtypeStruct((T, F2), MXU_DTYPE), jax.ShapeDtypeStruct((3, F2), F32),
                   jax.ShapeDtypeStruct((1, F2), F32)],
        compiler_params=_params(("arbitrary",)),
    )(hc, hc, da, conv_w, conv_b)


def _conv_bwd_input(dc, conv_w, L, *, name):
    T, F2 = dc.shape
    tm = CONV_ROWS
    edge = CONV_EDGE
    last_blk = T // edge - 1

    def body(x_ref, e_ref, w_ref, o_ref):
        at_end = ((pl.program_id(0) + 1) * tm) % L == 0
        x = x_ref[...].astype(F32)
        rows = lax.broadcasted_iota(jnp.int32, x.shape, 0)
        keep = jnp.where(at_end, 0.0, 1.0)
        ev = e_ref[...].astype(F32)
        e0 = ev[0:1, :] * keep
        e1 = ev[1:2, :] * keep
        u1 = pltpu.roll(x, tm - 1, 0)
        u2 = pltpu.roll(x, tm - 2, 0)
        x1 = jnp.where(rows == tm - 1, e0, u1)
        x2 = jnp.where(rows == tm - 1, e1, jnp.where(rows == tm - 2, e0, u2))
        o_ref[...] = (w_ref[0:1, :] * x + w_ref[1:2, :] * x1 + w_ref[2:3, :] * x2).astype(o_ref.dtype)

    return _pallas(
        body, name=name, grid=(T // tm,),
        in_specs=[_rows(tm, F2),
                  pl.BlockSpec((edge, F2), lambda i: (jnp.minimum((i + 1) * (tm // edge), last_blk), 0)),
                  _whole((3, F2))],
        out_specs=_rows(tm, F2),
        out_shape=jax.ShapeDtypeStruct((T, F2), MXU_DTYPE), compiler_params=_params(("parallel",)),
    )(dc, dc, conv_w)


S5_CHUNK = 128
LANES = 128


def _slab_rows(c, n, ncl):
    return pl.ds(c, n) if ncl == 1 else pl.ds(c, n, stride=ncl)


def _slab_put(ref, c, n, ncl, val):
    for s in range(val.shape[1] // LANES):
        ref[s, _slab_rows(c, n, ncl), :] = val[:, s * LANES:(s + 1) * LANES]


def _slab_get(ref, c, n, ncl):
    return jnp.concatenate([ref[s, _slab_rows(c, n, ncl), :] for s in range(ref.shape[0])], axis=-1)


def _slabs(n_slab, rows):
    return pl.BlockSpec((n_slab, rows, LANES), lambda i: (0, i, 0))


def _s5_fwd(xi, wb, wc, a_r, a_i, d_row, B, *, name):
    T, D = xi.shape
    ncl = wb.shape[0]
    cs = wb.shape[2] // 2
    ns = cs // LANES
    R = B * ncl
    Q = S5_CHUNK
    QR = Q * ncl
    nsteps = Q // B

    def body(x_ref, wb_ref, wc_ref, ar_ref, ai_ref, d_ref, y_ref, yg_ref, hr_ref, hi_ref, bur, bui, cr, ci):
        @pl.when(pl.program_id(0) == 0)
        def _():
            cr[...] = jnp.zeros_like(cr)
            ci[...] = jnp.zeros_like(ci)

        x = x_ref[...]
        xb = x.astype(MXU_DTYPE)
        for c in range(ncl):
            bu = jnp.dot(xb[:, c * CLUSTER_W:(c + 1) * CLUSTER_W], wb_ref[c], preferred_element_type=F32)
            _slab_put(bur, c, Q, ncl, bu[:, :cs])
            _slab_put(bui, c, Q, ncl, bu[:, cs:])
        ar = ar_ref[...]
        ai = ai_ref[...]

        def step(k, carry):
            hr, hi = carry
            sl = pl.ds(pl.multiple_of(k * R, R), R)
            nr = ar * hr - ai * hi + bur[:, sl, :]
            ni = ar * hi + ai * hr + bui[:, sl, :]
            hr_ref[:, sl, :] = nr
            hi_ref[:, sl, :] = ni
            return nr, ni

        hr, hi = lax.fori_loop(0, nsteps, step, (cr[...], ci[...]), unroll=4)
        cr[...] = hr
        ci[...] = hi
        parts = []
        for c in range(ncl):
            hrc = _slab_get(hr_ref, c, Q, ncl).astype(MXU_DTYPE)
            hic = _slab_get(hi_ref, c, Q, ncl).astype(MXU_DTYPE)
            parts.append(jnp.dot(hrc, wc_ref[c, :cs, :], preferred_element_type=F32)
                         + jnp.dot(hic, wc_ref[c, cs:, :], preferred_element_type=F32))
        y = d_ref[...] * x + (parts[0] if ncl == 1 else jnp.concatenate(parts, axis=-1))
        y_ref[...] = y
        yg_ref[...] = _gelu(y).astype(yg_ref.dtype)

    return _pallas(
        body, name=name, grid=(T // Q,),
        in_specs=[_rows(Q, D), _whole(wb.shape), _whole(wc.shape), _whole((ns, R, LANES)), _whole((ns, R, LANES)),
                  _whole((1, D))],
        out_specs=[_rows(Q, D), _rows(Q, D), _slabs(ns, QR), _slabs(ns, QR)],
        out_shape=[jax.ShapeDtypeStruct((T, D), F32), jax.ShapeDtypeStruct((T, D), MXU_DTYPE),
                   jax.ShapeDtypeStruct((ns, T * ncl, LANES), F32), jax.ShapeDtypeStruct((ns, T * ncl, LANES), F32)],
        scratch_shapes=[pltpu.VMEM((ns, QR, LANES), F32), pltpu.VMEM((ns, QR, LANES), F32),
                        pltpu.VMEM((ns, R, LANES), F32), pltpu.VMEM((ns, R, LANES), F32)],
        compiler_params=_params(("arbitrary",)),
    )(xi, wb, wc, a_r, a_i, d_row)


def _s5_bwd(dy, xi, h_r, h_i, wb, wc, a_r, a_i, d_row, B, *, name):
    T, D = dy.shape
    ncl = wb.shape[0]
    cs = wb.shape[2] // 2
    ns = cs // LANES
    R = B * ncl
    Q = S5_CHUNK
    nsteps = Q // B
    nchunk = T // Q
    QR = Q * ncl

    def rev(i):
        return nchunk - 1 - i

    def body(dy_ref, x_ref, hr_ref, hi_ref, pr_ref, pi_ref, wb_ref, wc_ref, ar_ref, ai_ref, d_ref,
             du_ref, gr_ref, gi_ref, dar_ref, dai_ref, dd_ref, dhr, dhi, cr, ci):
        i = pl.program_id(0)

        @pl.when(i == 0)
        def _():
            cr[...] = jnp.zeros_like(cr)
            ci[...] = jnp.zeros_like(ci)
            dar_ref[...] = jnp.zeros_like(dar_ref)
            dai_ref[...] = jnp.zeros_like(dai_ref)
            dd_ref[...] = jnp.zeros_like(dd_ref)

        dyv = dy_ref[...]
        dyb = dyv.astype(MXU_DTYPE)
        for c in range(ncl):
            dh = lax.dot_general(dyb[:, c * CLUSTER_W:(c + 1) * CLUSTER_W], wc_ref[c],
                                 (((1,), (1,)), ((), ())), preferred_element_type=F32)
            _slab_put(dhr, c, Q, ncl, dh[:, :cs])
            _slab_put(dhi, c, Q, ncl, dh[:, cs:])
        ar = ar_ref[...]
        ai = ai_ref[...]

        def step(j, carry):
            gr, gi, sar, sai = carry
            k = nsteps - 1 - j
            sl = pl.ds(pl.multiple_of(k * R, R), R)
            ngr = dhr[:, sl, :] + ar * gr + ai * gi
            ngi = dhi[:, sl, :] - ai * gr + ar * gi
            gr_ref[:, sl, :] = ngr
            gi_ref[:, sl, :] = ngi
            pv = pl.ds(pl.multiple_of((k - 1) * R, R), R)
            hpr = hr_ref[:, pv, :]
            hpi = hi_ref[:, pv, :]
            return ngr, ngi, sar + ngr * hpr + ngi * hpi, sai - ngr * hpi + ngi * hpr

        gr, gi, sar, sai = lax.fori_loop(0, nsteps - 1, step, (cr[...], ci[...], dar_ref[...], dai_ref[...]), unroll=4)
        sl0 = pl.ds(0, R)
        ngr = dhr[:, sl0, :] + ar * gr + ai * gi
        ngi = dhi[:, sl0, :] - ai * gr + ar * gi
        gr_ref[:, sl0, :] = ngr
        gi_ref[:, sl0, :] = ngi
        keep = jnp.where(i == nchunk - 1, 0.0, 1.0)
        hpr = pr_ref[:, 8 - R:8, :] * keep
        hpi = pi_ref[:, 8 - R:8, :] * keep
        dar_ref[...] = sar + ngr * hpr + ngi * hpi
        dai_ref[...] = sai - ngr * hpi + ngi * hpr
        cr[...] = ngr
        ci[...] = ngi
        parts = []
        for c in range(ncl):
            grc = _slab_get(gr_ref, c, Q, ncl).astype(MXU_DTYPE)
            gic = _slab_get(gi_ref, c, Q, ncl).astype(MXU_DTYPE)
            parts.append(lax.dot_general(grc, wb_ref[c, :, :cs], (((1,), (1,)), ((), ())), preferred_element_type=F32)
                         + lax.dot_general(gic, wb_ref[c, :, cs:], (((1,), (1,)), ((), ())), preferred_element_type=F32))
        du_ref[...] = d_ref[...] * dyv + (parts[0] if ncl == 1 else jnp.concatenate(parts, axis=-1))
        dd_ref[...] += jnp.sum(dyv * x_ref[...], axis=0, keepdims=True)

    tok = pl.BlockSpec((Q, D), lambda i: (rev(i), 0))
    st = pl.BlockSpec((ns, QR, LANES), lambda i: (0, rev(i), 0))
    before = pl.BlockSpec((ns, 8, LANES), lambda i: (0, jnp.maximum(rev(i) * (QR // 8) - 1, 0), 0))
    acc = _whole((ns, R, LANES))
    return _pallas(
        body, name=name, grid=(nchunk,),
        in_specs=[tok, tok, st, st, before, before, _whole(wb.shape), _whole(wc.shape), acc, acc, _whole((1, D))],
        out_specs=[tok, st, st, acc, acc, _whole((1, D))],
        out_shape=[jax.ShapeDtypeStruct((T, D), F32),
                   jax.ShapeDtypeStruct((ns, T * ncl, LANES), F32), jax.ShapeDtypeStruct((ns, T * ncl, LANES), F32),
                   jax.ShapeDtypeStruct((ns, R, LANES), F32), jax.ShapeDtypeStruct((ns, R, LANES), F32),
                   jax.ShapeDtypeStruct((1, D), F32)],
        scratch_shapes=[pltpu.VMEM((ns, QR, LANES), F32)] * 2 + [pltpu.VMEM((ns, R, LANES), F32)] * 2,
        compiler_params=_params(("arbitrary",)),
    )(dy, xi, h_r, h_i, h_r, h_i, wb, wc, a_r, a_i, d_row)


def _cluster_tn(tok, st, ncl, *, tok_left, name):
    T = tok.shape[0]
    ns = st.shape[0]
    cs = ns * LANES
    tt = _pick(T, (512, 256, 128))
    nt = T // tt
    oshape = (ncl, CLUSTER_W, cs) if tok_left else (ncl, cs, CLUSTER_W)

    def body(tok_ref, st_ref, o_ref, acc):
        t = pl.program_id(0)

        @pl.when(t == 0)
        def _():
            acc[...] = jnp.zeros_like(acc)

        tk = tok_ref[...].astype(MXU_DTYPE)
        for c in range(ncl):
            tc = tk[:, c * CLUSTER_W:(c + 1) * CLUSTER_W]
            sc = _slab_get(st_ref, c, tt, ncl).astype(MXU_DTYPE)
            lhs, rhs = (tc, sc) if tok_left else (sc, tc)
            acc[c] += lax.dot_general(lhs, rhs, (((0,), (0,)), ((), ())), preferred_element_type=F32)

        @pl.when(t == nt - 1)
        def _():
            o_ref[...] = acc[...]

    return _pallas(
        body, name=name, grid=(nt,),
        in_specs=[_rows(tt, tok.shape[1]), _slabs(ns, tt * ncl)],
        out_specs=_whole(oshape),
        out_shape=jax.ShapeDtypeStruct(oshape, F32),
        scratch_shapes=[pltpu.VMEM(oshape, F32)],
        compiler_params=_params(("arbitrary",)),
    )(tok, st)


def _s5_discretize(lam_re, lam_im, log_dt, b_re, b_im):
    dt = jnp.exp(log_dt)[:, None]
    mag = jnp.exp(lam_re * dt)
    ab_r, ab_i = mag * jnp.cos(lam_im * dt), mag * jnp.sin(lam_im * dt)
    den = lam_re * lam_re + lam_im * lam_im
    nr = ab_r - 1.0
    co_r = (nr * lam_re + ab_i * lam_im) / den
    co_i = (ab_i * lam_re - nr * lam_im) / den
    bb_r = co_r[..., None] * b_re - co_i[..., None] * b_im
    bb_i = co_r[..., None] * b_im + co_i[..., None] * b_re
    return ab_r, ab_i, bb_r, bb_i


def _blockdiag(m):
    G, r, k = m.shape
    ncl = G // GROUPS_PER_CLUSTER
    m4 = m.reshape(ncl, GROUPS_PER_CLUSTER, r, k)
    eye = jnp.eye(GROUPS_PER_CLUSTER, dtype=m.dtype)
    return jnp.einsum('cgrk,gh->cgrhk', m4, eye).reshape(ncl, GROUPS_PER_CLUSTER * r, GROUPS_PER_CLUSTER * k)


def _unblockdiag(m, r, k):
    ncl = m.shape[0]
    m5 = m.reshape(ncl, GROUPS_PER_CLUSTER, r, GROUPS_PER_CLUSTER, k)
    eye = jnp.eye(GROUPS_PER_CLUSTER, dtype=m.dtype)
    return jnp.einsum('cgrhk,gh->cgrk', m5, eye).reshape(ncl * GROUPS_PER_CLUSTER, r, k)


def _t5_bucket(dist):
    exact = REL_BUCKETS // 2
    d = np.maximum(dist, 1).astype(np.float32)
    large = exact + (np.log(d / exact) / math.log(REL_MAX_DIST / exact) * (REL_BUCKETS - exact)).astype(np.int64)
    large = np.minimum(large, REL_BUCKETS - 1)
    return np.where(dist < exact, dist, large).astype(np.int32)


def _band_tables(dil):
    steps = np.arange(BAND)[:, None] + BAND - np.arange(2 * BAND)[None, :]
    bucket = _t5_bucket(np.maximum(steps, 0) * dil)
    in_band = (steps >= 0) & (steps <= BAND)
    return bucket, in_band


def _attn_bias(rel_bias, hpg):
    out = []
    for g, dil in enumerate(DILATIONS):
        bucket, in_band = _band_tables(dil)
        cols = rel_bias[:, g * hpg:(g + 1) * hpg].astype(F32)
        onehot = jnp.asarray((bucket.reshape(-1, 1) == np.arange(REL_BUCKETS)[None, :]).astype(np.float32))
        bias = jnp.dot(onehot, cols, precision=lax.Precision.HIGHEST).T.reshape(hpg, BAND, 2 * BAND)
        out.append(jnp.where(jnp.asarray(in_band)[None], bias, NEG_BIG))
    return jnp.concatenate(out, axis=0)


def _attn_blocks(dil, L):
    M = L // dil
    return M, M // BAND


def _row_sel(r, M, dil):
    return pl.ds(r, M) if dil == 1 else pl.ds(r, M, stride=dil)


def _attn_fwd(q, kv, bias, L, hpg, *, name):
    T = q.shape[0]
    nb_ = T // L
    HP = hpg // 2
    W3 = 3 * hpg * HEAD_DIM
    mmax = L

    def group_body(dil, q_ref, k_ref, v_ref, b_ref, o_ref, l_ref, os, ls):
        M, NB = _attn_blocks(dil, L)
        for r in range(dil):
            rows = _row_sel(r, M, dil)
            first = lax.broadcasted_iota(jnp.int32, (1, 2 * HEAD_DIM), 1) < HEAD_DIM
            qf = q_ref[rows, :] * 0.125
            qm = [jnp.where(first, qf, 0.0).astype(MXU_DTYPE), jnp.where(first, 0.0, qf).astype(MXU_DTYPE)]
            kr = k_ref[rows, :].astype(MXU_DTYPE)
            va = jnp.concatenate([v_ref[rows, :].astype(MXU_DTYPE), jnp.ones((M, 2 * HEAD_DIM), MXU_DTYPE)], axis=-1)
            for n in range(NB):
                qs = slice(n * BAND, (n + 1) * BAND)
                ks = slice(0, BAND) if n == 0 else slice((n - 1) * BAND, (n + 1) * BAND)
                o_h, l_h = [], []
                for hh in range(2):
                    bb = b_ref[hh, :, BAND:] if n == 0 else b_ref[hh]
                    s = lax.dot_general(qm[hh][qs, :], kr[ks, :], (((1,), (1,)), ((), ())),
                                        preferred_element_type=F32) + bb
                    m = jnp.max(s, axis=-1, keepdims=True)
                    p = jnp.exp(s - m)
                    pv = jnp.dot(p.astype(MXU_DTYPE), va[ks, :], preferred_element_type=F32)
                    l = pv[:, 2 * HEAD_DIM:]
                    o_h.append(pv[:, :2 * HEAD_DIM] / l)
                    l_h.append(m + jnp.log(l))
                os[qs, :] = jnp.where(first, o_h[0], o_h[1])
                ls[qs, :] = jnp.where(first, l_h[0], l_h[1])
            o_ref[rows, :] = os[0:M, :]
            l_ref[rows, :] = ls[0:M, :]

    def body(q_ref, k_ref, v_ref, b_ref, o_ref, l_ref, os, ls):
        g = pl.program_id(0)
        for gi, dil in enumerate(DILATIONS):
            pl.when(g == gi)(functools.partial(group_body, dil, q_ref, k_ref, v_ref, b_ref, o_ref, l_ref, os, ls))

    blk = (L, 2 * HEAD_DIM)
    return _pallas(
        body, name=name, grid=(3, nb_, HP),
        in_specs=[pl.BlockSpec(blk, lambda g, b, h: (b, g * HP + h)),
                  pl.BlockSpec(blk, lambda g, b, h: (b, g * HP + h)),
                  pl.BlockSpec(blk, lambda g, b, h: (b, 3 * HP + g * HP + h)),
                  pl.BlockSpec((2, BAND, 2 * BAND), lambda g, b, h: (g * HP + h, 0, 0))],
        out_specs=[pl.BlockSpec(blk, lambda g, b, h: (b, g * HP + h)),
                   pl.BlockSpec(blk, lambda g, b, h: (b, g * HP + h))],
        out_shape=[jax.ShapeDtypeStruct((T, W3), F32), jax.ShapeDtypeStruct((T, W3), F32)],
        scratch_shapes=[pltpu.VMEM((mmax, 2 * HEAD_DIM), F32), pltpu.VMEM((mmax, 2 * HEAD_DIM), F32)],
        compiler_params=_params(("arbitrary", "arbitrary", "arbitrary")),
    )(q, kv, kv, bias)


def _attn_merge(o3, l3, hw, *, name):
    T = o3.shape[0]
    tm = _pick(T, (256, 128))

    def body(o0, o1, o2, l0, l1, l2, o_ref, ob_ref, lse_ref):
        a0, a1, a2 = l0[...], l1[...], l2[...]
        m = jnp.maximum(jnp.maximum(a0, a1), a2)
        e0, e1, e2 = jnp.exp(a0 - m), jnp.exp(a1 - m), jnp.exp(a2 - m)
        z = e0 + e1 + e2
        o = (e0 * o0[...] + e1 * o1[...] + e2 * o2[...]) / z
        o_ref[...] = o
        ob_ref[...] = o.astype(ob_ref.dtype)
        lse_ref[...] = m + jnp.log(z)

    def col(g):
        return pl.BlockSpec((tm, hw), lambda i: (i, g))

    return _pallas(
        body, name=name, grid=(T // tm,),
        in_specs=[col(0), col(1), col(2), col(0), col(1), col(2)],
        out_specs=[_rows(tm, hw)] * 3,
        out_shape=[jax.ShapeDtypeStruct((T, hw), F32), jax.ShapeDtypeStruct((T, hw), MXU_DTYPE),
                   jax.ShapeDtypeStruct((T, hw), F32)],
        compiler_params=_params(("parallel",)),
    )(o3, o3, o3, l3, l3, l3)


def _attn_bwd(q, kv, do, o, lse, bias, L, hpg, *, name):
    T = q.shape[0]
    nb_ = T // L
    HP = hpg // 2
    W3 = 3 * hpg * HEAD_DIM
    mmax = L

    def group_body(dil, q_ref, k_ref, v_ref, do_ref, o_ref, l_ref, b_ref, dq_ref, dk_ref, dv_ref, ds_ref,
                   dqs, dks, dvs):
        M, NB = _attn_blocks(dil, L)
        for r in range(dil):
            rows = _row_sel(r, M, dil)
            first = lax.broadcasted_iota(jnp.int32, (1, 2 * HEAD_DIM), 1) < HEAD_DIM
            qf = q_ref[rows, :] * 0.125
            qm = [jnp.where(first, qf, 0.0).astype(MXU_DTYPE), jnp.where(first, 0.0, qf).astype(MXU_DTYPE)]
            kr = k_ref[rows, :].astype(MXU_DTYPE)
            vr = v_ref[rows, :].astype(MXU_DTYPE)
            dof = do_ref[rows, :]
            dom = [jnp.where(first, dof, 0.0).astype(MXU_DTYPE), jnp.where(first, 0.0, dof).astype(MXU_DTYPE)]
            dod = dof * o_ref[rows, :]
            delta = [jnp.sum(jnp.where(first, dod, 0.0), axis=-1, keepdims=True),
                     jnp.sum(jnp.where(first, 0.0, dod), axis=-1, keepdims=True)]
            lr = l_ref[rows, :]
            lse = [lr[:, 0:1], lr[:, HEAD_DIM:HEAD_DIM + 1]]
            dks[0:M, :] = jnp.zeros((M, 2 * HEAD_DIM), F32)
            dvs[0:M, :] = jnp.zeros((M, 2 * HEAD_DIM), F32)
            for n in range(NB):
                qs = slice(n * BAND, (n + 1) * BAND)
                ks = slice(0, BAND) if n == 0 else slice((n - 1) * BAND, (n + 1) * BAND)
                dq_h = []
                dkc = dvc = None
                for hh in range(2):
                    bb = b_ref[hh, :, BAND:] if n == 0 else b_ref[hh]
                    qb, dob = qm[hh][qs, :], dom[hh][qs, :]
                    s = lax.dot_general(qb, kr[ks, :], (((1,), (1,)), ((), ())), preferred_element_type=F32) + bb
                    p = jnp.exp(s - lse[hh][qs, :])
                    dp = lax.dot_general(dob, vr[ks, :], (((1,), (1,)), ((), ())), preferred_element_type=F32)
                    ds = p * (dp - delta[hh][qs, :])
                    if n == 0:
                        ds_ref[hh, :, BAND:] += ds
                    else:
                        ds_ref[hh] += ds
                    dsm = ds.astype(MXU_DTYPE)
                    dq_h.append(jnp.dot(dsm, kr[ks, :], preferred_element_type=F32))
                    dk1 = lax.dot_general(dsm, qb, (((0,), (0,)), ((), ())), preferred_element_type=F32)
                    dv1 = lax.dot_general(p.astype(MXU_DTYPE), dob, (((0,), (0,)), ((), ())), preferred_element_type=F32)
                    dkc = dk1 if dkc is None else dkc + dk1
                    dvc = dv1 if dvc is None else dvc + dv1
                dqs[qs, :] = jnp.where(first, dq_h[0], dq_h[1]) * 0.125
                dks[ks, :] += dkc
                dvs[ks, :] += dvc
            dq_ref[rows, :] = dqs[0:M, :]
            dk_ref[rows, :] = dks[0:M, :]
            dv_ref[rows, :] = dvs[0:M, :]

    def body(q_ref, k_ref, v_ref, do_ref, o_ref, l_ref, b_ref, dq_ref, dk_ref, dv_ref, ds_ref, dqs, dks, dvs):
        g = pl.program_id(0)

        @pl.when(pl.program_id(2) == 0)
        def _():
            ds_ref[...] = jnp.zeros_like(ds_ref)

        for gi, dil in enumerate(DILATIONS):
            pl.when(g == gi)(functools.partial(group_body, dil, q_ref, k_ref, v_ref, do_ref, o_ref, l_ref, b_ref,
                                               dq_ref, dk_ref, dv_ref, ds_ref, dqs, dks, dvs))

    blk = (L, 2 * HEAD_DIM)
    gcol = lambda g, h, b: (b, g * HP + h)
    hcol = lambda g, h, b: (b, h)
    return _pallas(
        body, name=name, grid=(3, HP, nb_),
        in_specs=[pl.BlockSpec(blk, gcol), pl.BlockSpec(blk, gcol),
                  pl.BlockSpec(blk, lambda g, h, b: (b, 3 * HP + g * HP + h)),
                  pl.BlockSpec(blk, hcol), pl.BlockSpec(blk, hcol), pl.BlockSpec(blk, hcol),
                  pl.BlockSpec((2, BAND, 2 * BAND), lambda g, h, b: (g * HP + h, 0, 0))],
        out_specs=[pl.BlockSpec(blk, gcol), pl.BlockSpec(blk, gcol), pl.BlockSpec(blk, gcol),
                   pl.BlockSpec((2, BAND, 2 * BAND), lambda g, h, b: (g * HP + h, 0, 0))],
        out_shape=[jax.ShapeDtypeStruct((T, W3), F32), jax.ShapeDtypeStruct((T, W3), F32),
                   jax.ShapeDtypeStruct((T, W3), F32), jax.ShapeDtypeStruct((3 * hpg, BAND, 2 * BAND), F32)],
        scratch_shapes=[pltpu.VMEM((mmax, 2 * HEAD_DIM), F32)] * 3,
        compiler_params=_params(("arbitrary", "arbitrary", "arbitrary")),
    )(q, kv, kv, do, o, lse, bias)


def _bias_grad(ds_sum, hpg, *, name):
    nh = ds_sum.shape[0]
    idx = np.stack([np.where(_band_tables(dil)[1], _band_tables(dil)[0], -1) for dil in DILATIONS]).astype(np.int32)

    def body(ds_ref, idx_ref, o_ref):
        d = ds_ref[...]
        ix = idx_ref[...]
        lane = lax.broadcasted_iota(jnp.int32, (8, 128), 1)
        row = jnp.zeros((8, 128), F32)
        for b in range(REL_BUCKETS):
            row = row + jnp.where(lane == b, jnp.sum(jnp.where(ix == b, d, 0.0)), 0.0)
        o_ref[...] = row

    out = _pallas(
        body, name=name, grid=(nh,),
        in_specs=[pl.BlockSpec((None, BAND, 2 * BAND), lambda h: (h, 0, 0)),
                  pl.BlockSpec((None, BAND, 2 * BAND), lambda h: (h // hpg, 0, 0))],
        out_specs=pl.BlockSpec((None, 8, 128), lambda h: (h, 0, 0)),
        out_shape=jax.ShapeDtypeStruct((nh, 8, 128), F32),
        compiler_params=_params(("parallel",)),
    )(ds_sum, jnp.asarray(idx))
    return out[:, 0, :REL_BUCKETS].T


def _adamw(w, g, m, v, *, name):
    Rw, C = w.shape
    tm = _pick(Rw, (512, 352, 256, 128, 64, 32, 16, 8))

    def body(w_ref, g_ref, m_ref, v_ref, d_ref, nm_ref, nv_ref):
        gg = g_ref[...]
        nm = ADAM_B1 * m_ref[...] + (1.0 - ADAM_B1) * gg
        nv = ADAM_B2 * v_ref[...] + (1.0 - ADAM_B2) * (gg * gg)
        m_hat = nm / (1.0 - ADAM_B1 ** ADAM_STEP)
        v_hat = nv / (1.0 - ADAM_B2 ** ADAM_STEP)
        d_ref[...] = -ADAM_LR * (m_hat / (jnp.sqrt(v_hat) + ADAM_EPS) + ADAM_WD * w_ref[...])
        nm_ref[...] = nm
        nv_ref[...] = nv

    return _pallas(
        body, name=name, grid=(Rw // tm,), in_specs=[_rows(tm, C)] * 4, out_specs=[_rows(tm, C)] * 3,
        out_shape=[jax.ShapeDtypeStruct((Rw, C), F32)] * 3, compiler_params=_params(("parallel",)),
    )(w, g, m, v)


ROW_TILE_ELEMS = 256 * 1024


def _tile_rows(r, c):
    best = 8
    for t in range(8, r + 1, 8):
        if r % t == 0 and t * c <= ROW_TILE_ELEMS:
            best = t
    return best


def _adamw_halves(w, m, v, mine, other, cidx, *, name):
    _, r, c = w.shape
    tm = _tile_rows(r, c)

    def body(c_ref, w_ref, m_ref, v_ref, a_ref, b_ref, g_ref, d_ref, nm_ref, nv_ref):
        gg = jnp.where(pl.program_id(0) == c_ref[0], a_ref[...], b_ref[...])
        nm = ADAM_B1 * m_ref[...] + (1.0 - ADAM_B1) * gg
        nv = ADAM_B2 * v_ref[...] + (1.0 - ADAM_B2) * (gg * gg)
        m_hat = nm / (1.0 - ADAM_B1 ** ADAM_STEP)
        v_hat = nv / (1.0 - ADAM_B2 ** ADAM_STEP)
        g_ref[...] = gg
        d_ref[...] = -ADAM_LR * (m_hat / (jnp.sqrt(v_hat) + ADAM_EPS) + ADAM_WD * w_ref[...])
        nm_ref[...] = nm
        nv_ref[...] = nv

    half = pl.BlockSpec((None, tm, c), lambda h, i, cr: (h, i, 0))
    one = pl.BlockSpec((None, tm, c), lambda h, i, cr: (0, i, 0))
    spec = pltpu.PrefetchScalarGridSpec(num_scalar_prefetch=1, grid=(2, r // tm),
                                        in_specs=[half, half, half, one, one], out_specs=[half] * 4)
    return _pallas(
        body, name=name, grid_spec=spec, out_shape=[jax.ShapeDtypeStruct((2, r, c), F32)] * 4,
        compiler_params=_params(("parallel", "parallel")),
    )(cidx, w, m, v, mine, other)


def _pair_sum(g, theirs, cidx, *, cast, name):
    _, _, r, c = g.shape
    tm = _tile_rows(r, c)

    def body(c_ref, g_ref, t_ref, *outs):
        s = g_ref[...] + t_ref[...]
        outs[0][...] = s
        if cast:
            outs[1][...] = s.astype(BF16)

    blk = (None, None, tm, c)
    first = pl.BlockSpec(blk, lambda p, i, cr: (p, 0, i, 0))
    shapes = [jax.ShapeDtypeStruct((4, 1, r, c), F32)] + ([jax.ShapeDtypeStruct((4, 1, r, c), BF16)] if cast else [])
    spec = pltpu.PrefetchScalarGridSpec(
        num_scalar_prefetch=1, grid=(4, r // tm),
        in_specs=[pl.BlockSpec(blk, lambda p, i, cr: (p, cr[0], i, 0)), first], out_specs=[first] * len(shapes))
    return _pallas(body, name=name, grid_spec=spec, out_shape=shapes,
                   compiler_params=_params(("parallel", "parallel")))(cidx, g, theirs)


def _chip_sum(hf, got, chip_idx, *, name):
    _, _, r, c = hf.shape
    tm = _tile_rows(r, c)

    def body(p_ref, h_ref, r_ref, o_ref):
        s = h_ref[...]
        for k in range(3):
            s = s + r_ref[k].astype(F32)
        o_ref[...] = s

    spec = pltpu.PrefetchScalarGridSpec(
        num_scalar_prefetch=1, grid=(r // tm,),
        in_specs=[pl.BlockSpec((None, None, tm, c), lambda i, pr: (pr[0], 0, i, 0)),
                  pl.BlockSpec((3, None, tm, c), lambda i, pr: (0, 0, i, 0))],
        out_specs=pl.BlockSpec((None, tm, c), lambda i, pr: (0, i, 0)))
    return _pallas(body, name=name, grid_spec=spec, out_shape=jax.ShapeDtypeStruct((1, r, c), F32),
                   compiler_params=_params(("parallel",)))(chip_idx, hf, got)


def _place():
    x, y, c = lax.axis_index("x"), lax.axis_index("y"), lax.axis_index("c")
    chips = [(1 - x, y), (x, 1 - y), (1 - x, 1 - y)]
    return x, y, c, chips


_ANY = pl.BlockSpec(memory_space=pl.ANY)


def _comm_call(body, ins, out_shapes, n_remote, *, name, aliases=None):
    sems = [pltpu.SemaphoreType.DMA((n,)) for n in n_remote]
    return _pallas(
        body, name=name, in_specs=[_ANY] * len(ins), out_specs=[_ANY] * len(out_shapes), out_shape=out_shapes,
        scratch_shapes=sems, input_output_aliases=aliases or {},
        compiler_params=pltpu.CompilerParams(has_side_effects=True),
    )(*ins)


_HBM_SPEC = pl.BlockSpec(memory_space=pltpu.HBM)
_SEM_SPEC = pl.BlockSpec(memory_space=pltpu.SEMAPHORE)
_DATAFLOW = pltpu.SideEffectType.DATAFLOW_SIDE_EFFECTING


def _in_hbm(a):
    return pltpu.with_memory_space_constraint(a, pltpu.HBM)


def _gather_start(groups, *, name):
    flat = [s for g in groups for s in g]
    n, ng = len(flat), len(groups)

    def body(*refs):
        ins, lands = refs[:n], refs[n:2 * n]
        sems = refs[2 * n:2 * n + 2 * ng]
        token = refs[-1]
        x, y, c, chips = _place()
        me = 2 * x + y
        a = 0
        for gi, g in enumerate(groups):
            for j in range(len(g)):
                for k, (tx, ty) in enumerate(chips):
                    _rcopy(ins[a].at[c], lands[a].at[me, c], sems[2 * gi].at[3 * j + k], sems[2 * gi + 1].at[3 * j + k],
                           (tx, ty, c)).start()
                a += 1
        token[...] = jnp.zeros_like(token)

    land_shapes = [(4,) + s.shape for s in flat]
    out_shape = ([pltpu.SemaphoreType.DMA((3 * len(g),)) for g in groups for _ in range(2)]
                 + [pltpu.HBM(s.shape, s.dtype) for s in flat]
                 + [pltpu.HBM(ls, s.dtype) for ls, s in zip(land_shapes, flat)]
                 + [jax.ShapeDtypeStruct((8, 128), F32)])
    outs = _pallas(
        body, name=name, in_specs=[_HBM_SPEC] * (2 * n),
        out_specs=[_SEM_SPEC] * (2 * ng) + [_HBM_SPEC] * (2 * n) + [pl.BlockSpec(memory_space=pltpu.VMEM)],
        out_shape=out_shape, input_output_aliases={i: 2 * ng + i for i in range(2 * n)},
        compiler_params=pltpu.CompilerParams(has_side_effects=_DATAFLOW),
    )(*[_in_hbm(s) for s in flat], *[_in_hbm(lax.empty(ls, s.dtype)) for ls, s in zip(land_shapes, flat)])
    sems, thru, lands, token = outs[:2 * ng], outs[2 * ng:2 * ng + n], outs[2 * ng + n:2 * ng + 2 * n], outs[-1]
    res, a = [], 0
    for gi, g in enumerate(groups):
        res.append((sems[2 * gi], sems[2 * gi + 1], thru[a:a + len(g)], lands[a:a + len(g)]))
        a += len(g)
    return res, token


def _gather_wait(ssem, rsem, shards, lands, after, *, name):
    m = len(shards)

    def body(*refs):
        ins, lnd = refs[:m], refs[m:2 * m]
        ss, rs = refs[2 * m], refs[2 * m + 1]
        x, y, c, chips = _place()
        for j in range(m):
            for k, (tx, ty) in enumerate(chips):
                cp = _rcopy(ins[j].at[c], lnd[j].at[2 * tx + ty, c], ss.at[3 * j + k], rs.at[3 * j + k], (tx, ty, c))
                cp.wait_send()
                cp.wait_recv()

    outs = _pallas(
        body, name=name, in_specs=[_HBM_SPEC] * (2 * m) + [_SEM_SPEC, _SEM_SPEC, _ANY],
        out_specs=[_HBM_SPEC] * (2 * m),
        out_shape=[pltpu.HBM(s.shape, s.dtype) for s in shards] + [pltpu.HBM(l.shape, l.dtype) for l in lands],
        input_output_aliases={i: i for i in range(2 * m)},
        compiler_params=pltpu.CompilerParams(has_side_effects=_DATAFLOW),
    )(*shards, *lands, ssem, rsem, after)
    return outs[m:]


def _gather_forward(lands, *, name):
    n = len(lands)

    def body(*refs):
        outs = refs[n:2 * n]
        ssem, rsem = refs[2 * n:]
        x, y, c, chips = _place()
        sib = (x, y, 1 - c)
        cps = []
        for a in range(n):
            for k, (tx, ty) in enumerate(chips):
                pk = 2 * tx + ty
                cp = _rcopy(outs[a].at[pk, c], outs[a].at[pk, c], ssem.at[3 * a + k], rsem.at[3 * a + k], sib)
                cp.start()
                cps.append(cp)
        for a in range(n):
            for k, (tx, ty) in enumerate(chips):
                pk = 2 * tx + ty
                _rcopy(outs[a].at[pk, c], outs[a].at[pk, 1 - c], ssem.at[3 * a + k], rsem.at[3 * a + k], sib).wait_recv()
        for cp in cps:
            cp.wait_send()

    shapes = [jax.ShapeDtypeStruct(l.shape, l.dtype) for l in lands]
    return _comm_call(body, lands, shapes, [3 * n, 3 * n], name=name, aliases={i: i for i in range(n)})


class _Lazy:
    def __init__(self, group_of, make):
        self._group_of, self._make, self._done, self._anchor = group_of, make, {}, None

    def anchor(self, value):
        self._anchor = value

    def __getitem__(self, key):
        g = self._group_of[key]
        if g not in self._done:
            self._done[g] = self._make(g, self._anchor)
        return self._done[g][key]


def _anchor(mapping, value):
    if isinstance(mapping, _Lazy):
        mapping.anchor(value)


def _rcopy(src, dst, ssem, rsem, dev):
    return pltpu.make_async_remote_copy(src_ref=src, dst_ref=dst, send_sem=ssem, recv_sem=rsem,
                                        device_id=dev, device_id_type=MESH)


def _all_gather(shards, *, name):
    n = len(shards)

    def body(*refs):
        ins, outs = refs[:n], refs[n:2 * n]
        s_ici, r_ici, s_d2d, r_d2d = refs[2 * n:]
        x, y, c, chips = _place()
        me = 2 * x + y
        sib = (x, y, 1 - c)
        sends = []
        for a in range(n):
            for k, (tx, ty) in enumerate(chips):
                cp = _rcopy(ins[a].at[c], outs[a].at[me, c], s_ici.at[3 * a + k], r_ici.at[3 * a + k], (tx, ty, c))
                cp.start()
                sends.append(cp)
        for a in range(n):
            for k, (tx, ty) in enumerate(chips):
                pk = 2 * tx + ty
                _rcopy(ins[a].at[c], outs[a].at[pk, c], s_ici.at[3 * a + k], r_ici.at[3 * a + k], (tx, ty, c)).wait_recv()
                fw = _rcopy(outs[a].at[pk, c], outs[a].at[pk, c], s_d2d.at[3 * a + k], r_d2d.at[3 * a + k], sib)
                fw.start()
                sends.append(fw)
        for a in range(n):
            for k, (tx, ty) in enumerate(chips):
                pk = 2 * tx + ty
                _rcopy(ins[a].at[c], outs[a].at[pk, 1 - c], s_d2d.at[3 * a + k], r_d2d.at[3 * a + k], sib).wait_recv()
        for cp in sends:
            cp.wait_send()

    shapes = [jax.ShapeDtypeStruct((4,) + s.shape, s.dtype) for s in shards]
    return _comm_call(body, shards, shapes, [3 * n] * 4, name=name)


def _gather(shards, chip, *, name):
    outs = _all_gather(shards, name=name)
    return [lax.dynamic_update_slice(o, s[None], (chip, 0, 0, 0)) for o, s in zip(outs, shards)]


def _pair_send(gs, *, name):
    n = len(gs)

    def body(*refs):
        ins, theirs = refs[:n], refs[n:2 * n]
        ssem, rsem = refs[2 * n:]
        x, y, c, _ = _place()
        sib = (x, y, 1 - c)
        cps = []
        for a in range(n):
            cp = _rcopy(ins[a].at[:, pl.ds(1 - c, 1)], theirs[a], ssem.at[a], rsem.at[a], sib)
            cp.start()
            cps.append(cp)
        for cp in cps:
            cp.wait_send()
            cp.wait_recv()

    shapes = [jax.ShapeDtypeStruct((4, 1) + g.shape[2:], g.dtype) for g in gs]
    return _comm_call(body, gs, shapes, [n, n], name=name)


def _chip_exchange(hx, *, name):
    n = len(hx)

    def body(*refs):
        hxr, got = refs[:n], refs[n:2 * n]
        ssem, rsem = refs[2 * n:]
        x, y, c, chips = _place()
        cps = []
        for a in range(n):
            for k, (tx, ty) in enumerate(chips):
                cp = _rcopy(hxr[a].at[2 * tx + ty], got[a].at[k], ssem.at[3 * a + k], rsem.at[3 * a + k], (tx, ty, c))
                cp.start()
                cps.append(cp)
        for cp in cps:
            cp.wait_send()
            cp.wait_recv()

    shapes = [jax.ShapeDtypeStruct((3,) + h.shape[1:], h.dtype) for h in hx]
    return _comm_call(body, hx, shapes, [3 * n, 3 * n], name=name)


def _pair_swap(fs, *, name):
    n = len(fs)

    def body(*refs):
        ins, outs = refs[:n], refs[n:2 * n]
        ssem, rsem = refs[2 * n:]
        x, y, c, _ = _place()
        cps = []
        for a in range(n):
            cp = _rcopy(ins[a], outs[a], ssem.at[a], rsem.at[a], (x, y, 1 - c))
            cp.start()
            cps.append(cp)
        for cp in cps:
            cp.wait_send()
            cp.wait_recv()

    shapes = [jax.ShapeDtypeStruct(f.shape, f.dtype) for f in fs]
    return _comm_call(body, fs, shapes, [n, n], name=name)


def _reduce_scatter(grads, exch_bf16, cidx, chip_idx, tag):
    n = len(grads)
    theirs = _pair_send(grads, name=f"rs_pair_send_{tag}")
    hf, hx = [], []
    for a in range(n):
        res = _pair_sum(grads[a], theirs[a], cidx, cast=exch_bf16[a], name=f"rs_pair_sum_{tag}{a}")
        hf.append(res[0])
        hx.append(res[1] if exch_bf16[a] else res[0])
    got = _chip_exchange(hx, name=f"rs_chip_exchange_{tag}")
    mine = [_chip_sum(hf[a], got[a], chip_idx, name=f"rs_chip_sum_{tag}{a}") for a in range(n)]
    return mine, _pair_swap(mine, name=f"rs_pair_swap_{tag}")


def _interleave(a, B, L):
    return a.reshape(B, L, -1).transpose(1, 0, 2).reshape(B * L, -1)


def _deinterleave(a, B, L):
    return a.reshape(L, B, -1).transpose(1, 0, 2).reshape(B * L, -1)


def _local_step(x, tgt, W, S):
    B, L, D = x.shape
    T = B * L
    G = D // SSM_GROUP
    Pst = SSM_STATE
    hpg = D // HEAD_DIM
    HW = hpg * HEAD_DIM
    ncl = G // GROUPS_PER_CLUSTER
    x2 = x.reshape(T, D)
    tgt2 = tgt.reshape(T, D)

    disc = lambda *p: _s5_discretize(*p)
    (ab_r, ab_i, bb_r, bb_i), disc_vjp = jax.vjp(disc, S["lam_re"], S["lam_im"], S["log_dt"], S["b_re"], S["b_im"])
    wb = jnp.concatenate([_blockdiag(jnp.transpose(bb_r, (0, 2, 1))), _blockdiag(jnp.transpose(bb_i, (0, 2, 1)))],
                         axis=-1).astype(MXU_DTYPE)
    wc = jnp.concatenate([_blockdiag(jnp.transpose(S["c_re"], (0, 2, 1))), _blockdiag(-jnp.transpose(S["c_im"], (0, 2, 1)))],
                         axis=1).astype(MXU_DTYPE)
    cs = GROUPS_PER_CLUSTER * Pst
    slab = lambda ab: jnp.tile(jnp.transpose(ab.reshape(ncl, cs // LANES, LANES), (1, 0, 2)), (1, B, 1))
    a_r, a_i = slab(ab_r), slab(ab_i)
    d_row = S["d"].reshape(1, D)

    xi = _interleave(x2, B, L)
    y, yg, h_r, h_i = _s5_fwd(xi, wb, wc, a_r, a_i, d_row, B, name="s5_fwd")
    _anchor(W, yg)
    z = _mm_nn(yg, W["w_glu"], bias=S["b_glu"].reshape(1, D), name="glu_z")
    gate = _glu_gate(y, z, name="glu_gate")
    mix = _deinterleave(_mm_nn(gate, W["w_out"], name="s5_out"), B, L)
    h1, h1b, xh1, rs1 = _ln_fwd(x2, mix, S["ln_gain"][0, 0][None], S["ln_bias"][0, 0][None], name="ln_fwd_0a")

    def ffn_fwd(hb, l):
        hc = _mm_nn(hb, W["w_up"], l=l, out_dtype=MXU_DTYPE, name=f"ffn_up_{l}")
        a = _conv_glu_fwd(hc, S["conv_w"][l], S["conv_b"][l][None], L, name=f"ffn_conv_{l}")
        f = _mm_nn(a, W["w_down"], l=l, name=f"ffn_down_{l}")
        return hc, a, f

    _anchor(W, h1b)
    hc0, a0, f0 = ffn_fwd(h1b, 0)
    h2, h2b, xh2, rs2 = _ln_fwd(h1, f0, S["ln_gain"][0, 1][None], S["ln_bias"][0, 1][None], name="ln_fwd_0b")

    _anchor(W, h2b)
    kv = _mm_nn(h2b, W["w_kv"], name="attn_kv")
    q = _mm_nn(h2b, W["w_q"], name="attn_q")
    bias = _attn_bias(S["rel_bias"], hpg)
    o3, l3 = _attn_fwd(q, kv, bias, L, hpg, name="attn_fwd")
    o, ob, lse = _attn_merge(o3, l3, HW, name="attn_merge")
    att = _mm_nn(ob, W["w_ao"], name="attn_out")
    h3, h3b, xh3, rs3 = _ln_fwd(h2, att, S["ln_gain"][1, 0][None], S["ln_bias"][1, 0][None], name="ln_fwd_1a")
    hc1, a1, f1 = ffn_fwd(h3b, 1)
    h4, _, xh4, rs4 = _ln_fwd(h3, f1, S["ln_gain"][1, 1][None], S["ln_bias"][1, 1][None], name="ln_fwd_1b")

    dh4, lrow = _loss_grad(h4, tgt2, name="loss")
    loss = lrow[0, 0]

    GW, GS = {}, {}

    def ffn_bwd(dzb, hb, hc, a, l):
        da = _mm_nt(dzb, W["w_down"], l=l, out_dtype=MXU_DTYPE, name=f"ffn_down_bwd_x_{l}")
        GW["w_down"] = _tn(a, dzb, ptotal=1, nl=DEPTH, l=l, np_cols=D, prev=GW.get("w_down"), name=f"ffn_down_bwd_w_{l}")
        dc, dcw, dcb = _conv_glu_bwd(hc, da, S["conv_w"][l], S["conv_b"][l][None], L, name=f"ffn_conv_bwd_{l}")
        dhc = _conv_bwd_input(dc, S["conv_w"][l], L, name=f"ffn_conv_bwd_x_{l}")
        dh = _mm_nt(dhc, W["w_up"], l=l, name=f"ffn_up_bwd_x_{l}")
        GW["w_up"] = _tn(hb, dhc, ptotal=W["w_up"].shape[0], nl=DEPTH, l=l, np_cols=W["w_up"].shape[3],
                         prev=GW.get("w_up"), name=f"ffn_up_bwd_w_{l}")
        return dh, dcw, dcb

    dz4, dz4b, dg4, db4 = _ln_bwd([dh4], [1.0], xh4, rs4, S["ln_gain"][1, 1][None], name="ln_bwd_1b")
    dh3f, dcw1, dcb1 = ffn_bwd(dz4b, h3b, hc1, a1, 1)
    dz3, dz3b, dg3, db3 = _ln_bwd([dz4, dh3f], [DN_ALPHA, 1.0], xh3, rs3, S["ln_gain"][1, 0][None], name="ln_bwd_1a")
    do = _mm_nt(dz3b, W["w_ao"], name="attn_out_bwd_x")
    GW["w_ao"] = _tn(ob, dz3b, ptotal=1, np_cols=D, name="attn_out_bwd_w")
    dq, dk, dv, ds_sum = _attn_bwd(q, kv, do, o, lse, bias, L, hpg, name="attn_bwd")
    GS["rel_bias"] = _bias_grad(ds_sum, hpg, name="attn_bias_grad")
    GW["w_q"] = _tn(h2b, dq, ptotal=W["w_q"].shape[0], np_cols=W["w_q"].shape[3], name="attn_q_bwd_w")
    pkv, npkv = W["w_kv"].shape[0], W["w_kv"].shape[3]
    gkv = _tn(h2b, dk, ptotal=pkv, np_cols=npkv, p0=0, name="attn_k_bwd_w")
    GW["w_kv"] = _tn(h2b, dv, ptotal=pkv, np_cols=npkv, p0=pkv // 2, prev=gkv, name="attn_v_bwd_w")
    dh2q = _mm_nt(dq, W["w_q"], name="attn_q_bwd_x")
    dh2k = _mm_nt(dk, W["w_kv"], p0=0, pn=pkv // 2, name="attn_k_bwd_x")
    dh2v = _mm_nt(dv, W["w_kv"], p0=pkv // 2, pn=pkv // 2, name="attn_v_bwd_x")

    dz2, dz2b, dg2, db2 = _ln_bwd([dz3, dh2q, dh2k, dh2v], [DN_ALPHA, 1.0, 1.0, 1.0], xh2, rs2,
                                  S["ln_gain"][0, 1][None], name="ln_bwd_0b")
    dh1f, dcw0, dcb0 = ffn_bwd(dz2b, h1b, hc0, a0, 0)
    dz1, dz1b, dg1, db1 = _ln_bwd([dz2, dh1f], [DN_ALPHA, 1.0], xh1, rs1, S["ln_gain"][0, 0][None], name="ln_bwd_0a")
    dmix_i = _interleave(dz1b, B, L)
    dgate = _mm_nt(dmix_i, W["w_out"], name="s5_out_bwd_x")
    GW["w_out"] = _tn(gate, dmix_i, ptotal=1, np_cols=D, name="s5_out_bwd_w")
    dzg, dyg1, dbglu = _glu_bwd(y, z, dgate, name="glu_bwd")
    dyg2 = _mm_nt(dzg, W["w_glu"], name="glu_z_bwd_x")
    GW["w_glu"] = _tn(yg, dzg, ptotal=1, np_cols=D, name="glu_z_bwd_w")
    dy = _gelu_bwd(y, dyg1, dyg2, name="gelu_bwd")
    du_i, g_r, g_i, dar, dai, dd = _s5_bwd(dy, xi, h_r, h_i, wb, wc, a_r, a_i, d_row, B, name="s5_bwd")
    dwb_r = _cluster_tn(xi, g_r, ncl, tok_left=True, name="s5_b_grad_re")
    dwb_i = _cluster_tn(xi, g_i, ncl, tok_left=True, name="s5_b_grad_im")
    dwc_r = _cluster_tn(dy, h_r, ncl, tok_left=False, name="s5_c_grad_re")
    dwc_i = _cluster_tn(dy, h_i, ncl, tok_left=False, name="s5_c_grad_im")
    grad_x = _axpy(dz1, _deinterleave(du_i, B, L), DN_ALPHA, name="grad_x")

    dbb_r = jnp.transpose(_unblockdiag(dwb_r, SSM_GROUP, Pst), (0, 2, 1))
    dbb_i = jnp.transpose(_unblockdiag(dwb_i, SSM_GROUP, Pst), (0, 2, 1))
    unslab = lambda da: jnp.transpose(da.reshape(cs // LANES, B, ncl, LANES).sum(1), (1, 0, 2)).reshape(G, Pst)
    dab_r, dab_i = unslab(dar), unslab(dai)
    GS["lam_re"], GS["lam_im"], GS["log_dt"], GS["b_re"], GS["b_im"] = disc_vjp((dab_r, dab_i, dbb_r, dbb_i))
    GS["c_re"] = jnp.transpose(_unblockdiag(dwc_r, Pst, SSM_GROUP), (0, 2, 1))
    GS["c_im"] = -jnp.transpose(_unblockdiag(dwc_i, Pst, SSM_GROUP), (0, 2, 1))
    GS["d"] = dd.reshape(G, SSM_GROUP)
    GS["b_glu"] = dbglu.reshape(D)
    GS["conv_w"] = jnp.stack([dcw0, dcw1])
    GS["conv_b"] = jnp.stack([dcb0[0], dcb1[0]])
    GS["ln_gain"] = jnp.stack([jnp.stack([dg1[0], dg2[0]]), jnp.stack([dg3[0], dg4[0]])])
    GS["ln_bias"] = jnp.stack([jnp.stack([db1[0], db2[0]]), jnp.stack([db3[0], db4[0]])])
    return loss, grad_x.reshape(B, L, D), GW, GS


SMALL_REPLICATED = ("lam_re", "lam_im", "log_dt", "b_re", "b_im", "c_re", "c_im", "d", "rel_bias", "conv_b")
SMALL_SHARDED = ("b_glu", "conv_w", "ln_gain", "ln_bias")
SMALL_ORDER = SMALL_REPLICATED + SMALL_SHARDED


def _pack(arrs, lanes, row_mult):
    flat = jnp.concatenate([a.reshape(-1).astype(F32) for a in arrs])
    rows = -(-flat.shape[0] // lanes)
    rows = -(-rows // row_mult) * row_mult
    return jnp.pad(flat, (0, rows * lanes - flat.shape[0])).reshape(rows, lanes)


def _unpack(packed, shapes):
    flat = packed.reshape(-1)
    out, off = [], 0
    for s in shapes:
        n = int(np.prod(s))
        out.append(flat[off:off + n].reshape(s))
        off += n
    return out


def kernel(x, s5_lam_re, s5_lam_im, s5_log_dt, s5_b_re, s5_b_im, s5_c_re, s5_c_im, s5_d, s5_w_glu, s5_b_glu, s5_w_out, attn_w_kv, attn_w_q, attn_w_out, rel_bias, ffn_w_up, ffn_conv_w, ffn_conv_b, ffn_w_down, ln_gain, ln_bias, loss_target, m_s5_lam_re, m_s5_lam_im, m_s5_log_dt, m_s5_b_re, m_s5_b_im, m_s5_c_re, m_s5_c_im, m_s5_d, m_s5_w_glu, m_s5_b_glu, m_s5_w_out, m_attn_w_kv, m_attn_w_q, m_attn_w_out, m_rel_bias, m_ffn_w_up, m_ffn_conv_w, m_ffn_conv_b, m_ffn_w_down, m_ln_gain, m_ln_bias, v_s5_lam_re, v_s5_lam_im, v_s5_log_dt, v_s5_b_re, v_s5_b_im, v_s5_c_re, v_s5_c_im, v_s5_d, v_s5_w_glu, v_s5_b_glu, v_s5_w_out, v_attn_w_kv, v_attn_w_q, v_attn_w_out, v_rel_bias, v_ffn_w_up, v_ffn_conv_w, v_ffn_conv_b, v_ffn_w_down, v_ln_gain, v_ln_bias):
    names = ["s5_lam_re", "s5_lam_im", "s5_log_dt", "s5_b_re", "s5_b_im", "s5_c_re", "s5_c_im", "s5_d", "s5_w_glu",
             "s5_b_glu", "s5_w_out", "attn_w_kv", "attn_w_q", "attn_w_out", "rel_bias", "ffn_w_up", "ffn_conv_w",
             "ffn_conv_b", "ffn_w_down", "ln_gain", "ln_bias"]
    loc = locals()
    w_in = {n: loc[n] for n in names}
    m_in = {n: loc["m_" + n] for n in names}
    v_in = {n: loc["v_" + n] for n in names}
    chip = 2 * lax.axis_index("x") + lax.axis_index("y")
    core = lax.axis_index("c")
    chip_idx = jnp.reshape(chip, (1,)).astype(jnp.int32)
    cidx = jnp.reshape(core, (1,)).astype(jnp.int32)

    big = [("w_glu", "s5_w_glu", "rows"), ("w_out", "s5_w_out", "rows"), ("w_ao", "attn_w_out", "rows"),
           ("w_kv", "attn_w_kv", "cols"), ("w_q", "attn_w_q", "cols"),
           ("w_up", "ffn_w_up", "layer_cols"), ("w_down", "ffn_w_down", "layer_rows")]

    def halves(t, kind):
        if kind.startswith("layer"):
            return t
        r, c = t.shape[-2:]
        return t.reshape(2, r // 2, c)

    def to_weight(g, kind):
        _, _, r, c = g.shape
        if kind == "rows":
            return g.reshape(1, 1, 8 * r, c)
        if kind == "cols":
            return g.reshape(4, 1, 2 * r, c)
        if kind == "layer_cols":
            return g
        return jnp.transpose(g, (1, 0, 2, 3)).reshape(1, 2, 4 * r, c)

    def from_weight_grad(gw, kind, r, c):
        if kind == "layer_rows":
            return jnp.transpose(gw.reshape(2, 4, r, c), (1, 0, 2, 3))
        return gw.reshape(4, 2, r, c)

    small_sh = {"b_glu": s5_b_glu[0], "conv_w": ffn_conv_w, "ln_gain": ln_gain, "ln_bias": ln_bias}
    sh_shapes = [small_sh[k].shape for k in SMALL_SHARDED]
    sh_pack = _pack([small_sh[k] for k in SMALL_SHARDED], 128, 16)

    shards = [halves(w_in[src].astype(MXU_DTYPE), kind) for _, src, kind in big]
    shards.append(sh_pack.reshape(2, sh_pack.shape[0] // 2, 128))
    shard_of = {key: s for (key, _, _), s in zip(big, shards)}
    shard_of["small"] = shards[-1]
    kind_of = {key: kind for key, _, kind in big}

    group_keys = [["w_glu", "w_out", "small"], ["w_up", "w_down"], ["w_kv", "w_q", "w_ao"]]
    started, token = _gather_start([[shard_of[k] for k in g] for g in group_keys], name="weights_gather_start")

    def finish_group(gi, after):
        ssem, rsem, thru, lands = started[gi]
        lands = _gather_wait(ssem, rsem, thru, lands, after, name=f"weights_gather_wait_{gi}")
        lands = _gather_forward(lands, name=f"weights_gather_forward_{gi}")
        out = {}
        for key, land in zip(group_keys[gi], lands):
            full = lax.dynamic_update_slice(land, shard_of[key][None], (chip, 0, 0, 0))
            if key == "small":
                parts = [_unpack(full[p], sh_shapes) for p in range(4)]
                for i, k in enumerate(SMALL_SHARDED):
                    out[k] = jnp.concatenate([parts[p][i] for p in range(4)], axis=-1)
            else:
                out[key] = to_weight(full, kind_of[key])
        return out

    replicated = dict(lam_re=s5_lam_re[0], lam_im=s5_lam_im[0], log_dt=s5_log_dt[0], b_re=s5_b_re[0], b_im=s5_b_im[0],
                      c_re=s5_c_re[0], c_im=s5_c_im[0], rel_bias=rel_bias, conv_b=ffn_conv_b,
                      d=s5_d[0] + token[0, 0])
    group_of = {k: gi for gi, g in enumerate(group_keys) for k in g if k != "small"}
    group_of.update({k: 0 for k in SMALL_SHARDED})
    group_of.update({k: "replicated" for k in replicated})
    params = _Lazy(group_of, lambda g, after: replicated if g == "replicated" else finish_group(g, after))

    loss, grad_x, GW, GS = _local_step(x, loss_target, params, params)
    loss = lax.psum(loss, ("x", "y", "c"))

    gs_shapes = [GS[k].shape for k in SMALL_ORDER]
    gs_pack = _pack([GS[k] for k in SMALL_ORDER], 128, 64)
    rs = gs_pack.shape[0] // 8
    grads = []
    for (key, _, kind), sh in zip(big, shards):
        grads.append(from_weight_grad(GW[key], kind, sh.shape[1], sh.shape[2]))
    grads.append(gs_pack.reshape(4, 2, rs, 128))
    mine, other = _reduce_scatter(grads, [True] * len(big) + [False], cidx, chip_idx, "g")
    small_halves = jnp.where(core == 0, jnp.concatenate([mine[-1], other[-1]]), jnp.concatenate([other[-1], mine[-1]]))
    small_all = _gather([small_halves], chip, name="small_grads_all_gather")[0]
    gsmall = dict(zip(SMALL_ORDER, _unpack(small_all, gs_shapes)))

    big_res = {}
    for (key, src, kind), gm, go in zip(big, mine[:-1], other[:-1]):
        res4 = _adamw_halves(halves(w_in[src], kind), halves(m_in[src], kind), halves(v_in[src], kind), gm, go, cidx,
                             name=f"adamw_{key}")
        big_res[src] = tuple(t.reshape(w_in[src].shape) for t in res4)

    def big_out(i):
        return {src: big_res[src][i] for _, src, _ in big}

    small_w = {"lam_re": s5_lam_re, "lam_im": s5_lam_im, "log_dt": s5_log_dt, "b_re": s5_b_re, "b_im": s5_b_im,
               "c_re": s5_c_re, "c_im": s5_c_im, "d": s5_d, "rel_bias": rel_bias, "conv_b": ffn_conv_b,
               "b_glu": s5_b_glu, "conv_w": ffn_conv_w, "ln_gain": ln_gain, "ln_bias": ln_bias}
    small_name = {"lam_re": "s5_lam_re", "lam_im": "s5_lam_im", "log_dt": "s5_log_dt", "b_re": "s5_b_re", "b_im": "s5_b_im",
                  "c_re": "s5_c_re", "c_im": "s5_c_im", "d": "s5_d", "rel_bias": "rel_bias", "conv_b": "ffn_conv_b",
                  "b_glu": "s5_b_glu", "conv_w": "ffn_conv_w", "ln_gain": "ln_gain", "ln_bias": "ln_bias"}
    sg = {}
    for k in SMALL_ORDER:
        shp = small_w[k].shape
        g = gsmall[k]
        if k in SMALL_SHARDED:
            width = shp[-1]
            g = lax.dynamic_slice_in_dim(g, chip * width, width, axis=g.ndim - 1)
        sg[k] = g.reshape(shp)
    sshapes = [small_w[k].shape for k in SMALL_ORDER]
    pw = _pack([small_w[k] for k in SMALL_ORDER], 128, 512)
    pg = _pack([sg[k] for k in SMALL_ORDER], 128, 512)
    pm = _pack([m_in[small_name[k]] for k in SMALL_ORDER], 128, 512)
    pv = _pack([v_in[small_name[k]] for k in SMALL_ORDER], 128, 512)
    sd, snm, snv = _adamw(pw, pg, pm, pv, name="adamw_small")
    sd = dict(zip(SMALL_ORDER, _unpack(sd, sshapes)))
    snm = dict(zip(SMALL_ORDER, _unpack(snm, sshapes)))
    snv = dict(zip(SMALL_ORDER, _unpack(snv, sshapes)))

    res = [{}, {}, {}, {}]
    for i in range(4):
        res[i].update(big_out(i))
    for k in SMALL_ORDER:
        res[0][small_name[k]] = sg[k]
        res[1][small_name[k]] = sd[k]
        res[2][small_name[k]] = snm[k]
        res[3][small_name[k]] = snv[k]
    outs = [loss, grad_x]
    for i in range(4):
        outs += [res[i][n] for n in names]
    return tuple(outs)
```

```python
import functools
import math

import numpy as np
import jax
import jax.numpy as jnp
from jax import lax
from jax.experimental import pallas as pl
from jax.experimental.pallas import tpu as pltpu

F32 = jnp.float32
BF16 = jnp.bfloat16
MXU_DTYPE = jnp.bfloat16
V7X_VMEM_LIMIT_BYTES = 52 << 20
MESH = pl.DeviceIdType.MESH

DEPTH = 2
SSM_GROUP = 16
SSM_STATE = 64
GROUPS_PER_CLUSTER = 16
CLUSTER_W = GROUPS_PER_CLUSTER * SSM_GROUP
HEAD_DIM = 64
DILATIONS = (1, 4, 16)
BAND = 128
NEG_BIG = -1e30
REL_BUCKETS = 32
REL_MAX_DIST = 2048
DN_ALPHA = (2.0 * DEPTH) ** 0.25
LN_EPS = 1e-5
ADAM_LR, ADAM_B1, ADAM_B2, ADAM_EPS, ADAM_WD, ADAM_STEP = 0.001, 0.9, 0.999, 1e-08, 0.01, 10
GELU_K = math.sqrt(2.0 / math.pi)
GELU_C = 0.044715


def _pallas(body, **kw):
    return pl.pallas_call(body, **kw)


def _params(sem=None):
    return pltpu.CompilerParams(dimension_semantics=sem, vmem_limit_bytes=V7X_VMEM_LIMIT_BYTES)


def _pick(n, cands):
    for c in cands:
        if n % c == 0:
            return c
    return n


def _sigmoid(z):
    return 1.0 / (1.0 + jnp.exp(-z))


def _gelu(y):
    return 0.5 * y * (1.0 + jnp.tanh(GELU_K * (y + GELU_C * y * y * y)))


def _gelu_grad(y):
    t = jnp.tanh(GELU_K * (y + GELU_C * y * y * y))
    return 0.5 * (1.0 + t) + 0.5 * y * (1.0 - t * t) * (GELU_K * (1.0 + 3.0 * GELU_C * y * y))


def _mm_nn(a, w, *, l=0, bias=None, out_dtype=F32, name):
    T, K = a.shape
    P, _, _, Np = w.shape
    tm = _pick(T, (1024, 512, 256, 128))
    tn = _pick(Np, (1408, 1024, 768, 512, 384, 256, 128))
    nj = Np // tn

    def body(*refs):
        if bias is None:
            a_ref, w_ref, o_ref = refs
        else:
            a_ref, w_ref, b_ref, o_ref = refs
        acc = jnp.dot(a_ref[...].astype(MXU_DTYPE), w_ref[...].astype(MXU_DTYPE), preferred_element_type=F32)
        if bias is not None:
            acc = acc + b_ref[...]
        o_ref[...] = acc.astype(o_ref.dtype)

    in_specs = [pl.BlockSpec((tm, K), lambda p, j, i: (i, 0)),
                pl.BlockSpec((None, None, K, tn), lambda p, j, i: (p, l, 0, j))]
    args = [a, w]
    if bias is not None:
        in_specs.append(pl.BlockSpec((1, tn), lambda p, j, i: (0, p * nj + j)))
        args.append(bias)
    return _pallas(
        body, name=name, grid=(P, nj, T // tm), in_specs=in_specs,
        out_specs=pl.BlockSpec((tm, tn), lambda p, j, i: (i, p * nj + j)),
        out_shape=jax.ShapeDtypeStruct((T, P * Np), out_dtype),
        compiler_params=_params(("parallel", "parallel", "parallel")),
    )(*args)


def _mm_nt(a, w, *, l=0, p0=0, pn=None, out_dtype=F32, name):
    T = a.shape[0]
    _, _, K, Np = w.shape
    pn = w.shape[0] if pn is None else pn
    tm = _pick(T, (1024, 512, 256, 128) if K <= 1024 else (512, 256, 128))
    tn = _pick(Np, (1536, 1408, 1024, 768, 512, 384, 256, 128))
    nj = Np // tn
    nred = pn * nj

    def body(a_ref, w_ref, o_ref, acc):
        r = pl.program_id(1)

        @pl.when(r == 0)
        def _():
            acc[...] = jnp.zeros_like(acc)

        acc[...] += lax.dot_general(a_ref[...].astype(MXU_DTYPE), w_ref[...].astype(MXU_DTYPE),
                                    (((1,), (1,)), ((), ())), preferred_element_type=F32)

        @pl.when(r == nred - 1)
        def _():
            o_ref[...] = acc[...].astype(o_ref.dtype)

    return _pallas(
        body, name=name, grid=(T // tm, nred),
        in_specs=[pl.BlockSpec((tm, tn), lambda i, r: (i, r)),
                  pl.BlockSpec((None, None, K, tn), lambda i, r: (p0 + r // nj, l, 0, r % nj))],
        out_specs=pl.BlockSpec((tm, K), lambda i, r: (i, 0)),
        out_shape=jax.ShapeDtypeStruct((T, K), out_dtype),
        scratch_shapes=[pltpu.VMEM((tm, K), F32)],
        compiler_params=_params(("parallel", "arbitrary")),
    )(a, w)


def _tn(a, b, *, ptotal, np_cols, nl=1, l=0, p0=0, prev=None, name):
    T, K = a.shape
    Np = np_cols
    pn = b.shape[1] // Np
    tt = _pick(T, (1024, 512, 256, 128))
    tk = _pick(K, (1408, 1024, 512, 256, 128))
    tn = _pick(Np, (1408, 768, 512, 256, 128))
    if tk * tn > 1408 * 1024:
        tn = _pick(Np, (512, 256, 128))
    nj = Np // tn
    nt = T // tt

    def body(*refs):
        a_ref, b_ref = refs[0], refs[1]
        o_ref, acc = refs[-2], refs[-1]
        t = pl.program_id(3)

        @pl.when(t == 0)
        def _():
            acc[...] = jnp.zeros_like(acc)

        acc[...] += lax.dot_general(a_ref[...].astype(MXU_DTYPE), b_ref[...].astype(MXU_DTYPE),
                                    (((0,), (0,)), ((), ())), preferred_element_type=F32)

        @pl.when(t == nt - 1)
        def _():
            o_ref[...] = acc[...]

    in_specs = [pl.BlockSpec((tt, tk), lambda kb, p, j, t: (t, kb)),
                pl.BlockSpec((tt, tn), lambda kb, p, j, t: (t, p * nj + j))]
    args = [a, b]
    aliases = {}
    if prev is not None:
        in_specs.append(pl.BlockSpec(memory_space=pl.ANY))
        args.append(prev)
        aliases = {2: 0}
    return _pallas(
        body, name=name, grid=(K // tk, pn, nj, nt), in_specs=in_specs,
        out_specs=pl.BlockSpec((None, None, tk, tn), lambda kb, p, j, t: (p0 + p, l, kb, j)),
        out_shape=jax.ShapeDtypeStruct((ptotal, nl, K, Np), F32),
        scratch_shapes=[pltpu.VMEM((tk, tn), F32)],
        input_output_aliases=aliases,
        compiler_params=_params(("parallel", "parallel", "parallel", "arbitrary")),
    )(*args)


def _rows(tm, f):
    return pl.BlockSpec((tm, f), lambda i: (i, 0))


def _whole(shape):
    nd = len(shape)
    return pl.BlockSpec(shape, lambda i: (0,) * nd)


def _ln_fwd(xres, f, gain, bias, *, name):
    T, D = xres.shape
    tm = _pick(T, (256, 128))

    def body(x_ref, f_ref, g_ref, b_ref, y_ref, yb_ref, xh_ref, rs_ref):
        z = DN_ALPHA * x_ref[...] + f_ref[...]
        mu = jnp.mean(z, axis=-1, keepdims=True)
        zc = z - mu
        var = jnp.mean(zc * zc, axis=-1, keepdims=True)
        rstd = lax.rsqrt(var + LN_EPS)
        xh = zc * rstd
        y = xh * g_ref[...] + b_ref[...]
        y_ref[...] = y
        yb_ref[...] = y.astype(yb_ref.dtype)
        xh_ref[...] = xh
        rs_ref[...] = rstd

    return _pallas(
        body, name=name, grid=(T // tm,),
        in_specs=[_rows(tm, D), _rows(tm, D), _whole((1, D)), _whole((1, D))],
        out_specs=[_rows(tm, D), _rows(tm, D), _rows(tm, D), _rows(tm, 1)],
        out_shape=[jax.ShapeDtypeStruct((T, D), F32), jax.ShapeDtypeStruct((T, D), MXU_DTYPE),
                   jax.ShapeDtypeStruct((T, D), F32), jax.ShapeDtypeStruct((T, 1), F32)],
        compiler_params=_params(("parallel",)),
    )(xres, f, gain, bias)


def _ln_bwd(addends, coefs, xhat, rstd, gain, *, name):
    T, D = xhat.shape
    tm = _pick(T, (256, 128))
    n = len(addends)

    def body(*refs):
        adds = refs[:n]
        xh_ref, rs_ref, g_ref, dz_ref, dzb_ref, dg_ref, db_ref = refs[n:]
        dy = coefs[0] * adds[0][...]
        for c, r in zip(coefs[1:], adds[1:]):
            dy = dy + c * r[...]
        xh = xh_ref[...]
        dxh = dy * g_ref[...]
        m1 = jnp.mean(dxh, axis=-1, keepdims=True)
        m2 = jnp.mean(dxh * xh, axis=-1, keepdims=True)
        dz = rs_ref[...] * (dxh - m1 - xh * m2)
        dz_ref[...] = dz
        dzb_ref[...] = dz.astype(dzb_ref.dtype)

        @pl.when(pl.program_id(0) == 0)
        def _():
            dg_ref[...] = jnp.zeros_like(dg_ref)
            db_ref[...] = jnp.zeros_like(db_ref)

        dg_ref[...] += jnp.sum(dy * xh, axis=0, keepdims=True)
        db_ref[...] += jnp.sum(dy, axis=0, keepdims=True)

    return _pallas(
        body, name=name, grid=(T // tm,),
        in_specs=[_rows(tm, D)] * n + [_rows(tm, D), _rows(tm, 1), _whole((1, D))],
        out_specs=[_rows(tm, D), _rows(tm, D), _whole((1, D)), _whole((1, D))],
        out_shape=[jax.ShapeDtypeStruct((T, D), F32), jax.ShapeDtypeStruct((T, D), MXU_DTYPE),
                   jax.ShapeDtypeStruct((1, D), F32), jax.ShapeDtypeStruct((1, D), F32)],
        compiler_params=_params(("arbitrary",)),
    )(*addends, xhat, rstd, gain)


def _loss_grad(y, tgt, *, name):
    T, D = y.shape
    tm = _pick(T, (256, 128))

    def body(y_ref, t_ref, dy_ref, l_ref):
        e = y_ref[...] - t_ref[...]
        dy_ref[...] = e * (1.0 / D)

        @pl.when(pl.program_id(0) == 0)
        def _():
            l_ref[...] = jnp.zeros_like(l_ref)

        l_ref[...] += jnp.zeros_like(l_ref) + jnp.sum(e * e) * (0.5 / D)

    return _pallas(
        body, name=name, grid=(T // tm,),
        in_specs=[_rows(tm, D), _rows(tm, D)],
        out_specs=[_rows(tm, D), _whole((1, 128))],
        out_shape=[jax.ShapeDtypeStruct((T, D), F32), jax.ShapeDtypeStruct((1, 128), F32)],
        compiler_params=_params(("arbitrary",)),
    )(y, tgt)


def _axpy(a, b, ca, *, name):
    T, D = a.shape
    tm = _pick(T, (256, 128))

    def body(a_ref, b_ref, o_ref):
        o_ref[...] = ca * a_ref[...] + b_ref[...]

    return _pallas(
        body, name=name, grid=(T // tm,), in_specs=[_rows(tm, D), _rows(tm, D)], out_specs=_rows(tm, D),
        out_shape=jax.ShapeDtypeStruct((T, D), F32), compiler_params=_params(("parallel",)),
    )(a, b)


def _glu_gate(y, z, *, name):
    T, D = y.shape
    tm = _pick(T, (256, 128))

    def body(y_ref, z_ref, g_ref):
        g_ref[...] = (_gelu(y_ref[...]) * _sigmoid(z_ref[...])).astype(g_ref.dtype)

    return _pallas(
        body, name=name, grid=(T // tm,), in_specs=[_rows(tm, D), _rows(tm, D)], out_specs=_rows(tm, D),
        out_shape=jax.ShapeDtypeStruct((T, D), MXU_DTYPE), compiler_params=_params(("parallel",)),
    )(y, z)


def _glu_bwd(y, z, dg, *, name):
    T, D = y.shape
    tm = _pick(T, (256, 128))

    def body(y_ref, z_ref, dg_ref, dzb_ref, dyg_ref, db_ref):
        s = _sigmoid(z_ref[...])
        dg = dg_ref[...]
        dz = dg * _gelu(y_ref[...]) * s * (1.0 - s)
        dzb_ref[...] = dz.astype(dzb_ref.dtype)
        dyg_ref[...] = dg * s

        @pl.when(pl.program_id(0) == 0)
        def _():
            db_ref[...] = jnp.zeros_like(db_ref)

        db_ref[...] += jnp.sum(dz, axis=0, keepdims=True)

    return _pallas(
        body, name=name, grid=(T // tm,), in_specs=[_rows(tm, D)] * 3,
        out_specs=[_rows(tm, D), _rows(tm, D), _whole((1, D))],
        out_shape=[jax.ShapeDtypeStruct((T, D), MXU_DTYPE), jax.ShapeDtypeStruct((T, D), F32),
                   jax.ShapeDtypeStruct((1, D), F32)],
        compiler_params=_params(("arbitrary",)),
    )(y, z, dg)


def _gelu_bwd(y, d1, d2, *, name):
    T, D = y.shape
    tm = _pick(T, (256, 128))

    def body(y_ref, a_ref, b_ref, o_ref):
        o_ref[...] = (a_ref[...] + b_ref[...]) * _gelu_grad(y_ref[...])

    return _pallas(
        body, name=name, grid=(T // tm,), in_specs=[_rows(tm, D)] * 3, out_specs=_rows(tm, D),
        out_shape=jax.ShapeDtypeStruct((T, D), F32), compiler_params=_params(("parallel",)),
    )(y, d1, d2)


CONV_ROWS = 128
CONV_EDGE = 16


def _shift_back(x, edge, at_start, tm):
    rows = lax.broadcasted_iota(jnp.int32, x.shape, 0)
    keep = jnp.where(at_start, 0.0, 1.0)
    e7 = edge[CONV_EDGE - 1:CONV_EDGE, :] * keep
    e6 = edge[CONV_EDGE - 2:CONV_EDGE - 1, :] * keep
    r1 = pltpu.roll(x, 1, 0)
    r2 = pltpu.roll(x, 2, 0)
    x1 = jnp.where(rows == 0, e7, r1)
    x2 = jnp.where(rows == 0, e6, jnp.where(rows == 1, e7, r2))
    return x1, x2


def _conv_specs(T, F2, tm):
    return [_rows(tm, F2),
            pl.BlockSpec((CONV_EDGE, F2), lambda i: (jnp.maximum(i * (tm // CONV_EDGE) - 1, 0), 0))]


def _conv_glu_fwd(hc, conv_w, conv_b, L, *, name):
    T, F2 = hc.shape
    F = F2 // 2
    tm = CONV_ROWS

    def body(x_ref, e_ref, w_ref, b_ref, a_ref):
        at_start = (pl.program_id(0) * tm) % L == 0
        x = x_ref[...].astype(F32)
        x1, x2 = _shift_back(x, e_ref[...].astype(F32), at_start, tm)
        c = b_ref[...] + w_ref[0:1, :] * x + w_ref[1:2, :] * x1 + w_ref[2:3, :] * x2
        val, gate = c[:, :F], c[:, F:]
        a_ref[...] = (gate * _sigmoid(gate) * val).astype(a_ref.dtype)

    return _pallas(
        body, name=name, grid=(T // tm,),
        in_specs=_conv_specs(T, F2, tm) + [_whole((3, F2)), _whole((1, F2))],
        out_specs=_rows(tm, F),
        out_shape=jax.ShapeDtypeStruct((T, F), MXU_DTYPE), compiler_params=_params(("parallel",)),
    )(hc, hc, conv_w, conv_b)


def _conv_glu_bwd(hc, da, conv_w, conv_b, L, *, name):
    T, F2 = hc.shape
    F = F2 // 2
    tm = CONV_ROWS

    def body(x_ref, e_ref, da_ref, w_ref, b_ref, dc_ref, dw_ref, db_ref):
        at_start = (pl.program_id(0) * tm) % L == 0
        x = x_ref[...].astype(F32)
        x1, x2 = _shift_back(x, e_ref[...].astype(F32), at_start, tm)
        c = b_ref[...] + w_ref[0:1, :] * x + w_ref[1:2, :] * x1 + w_ref[2:3, :] * x2
        val, gate = c[:, :F], c[:, F:]
        s = _sigmoid(gate)
        da = da_ref[...].astype(F32)
        dval = da * (gate * s)
        dgate = da * val * (s * (1.0 + gate * (1.0 - s)))
        dc = jnp.concatenate([dval, dgate], axis=-1)
        dc_ref[...] = dc.astype(dc_ref.dtype)

        @pl.when(pl.program_id(0) == 0)
        def _():
            dw_ref[...] = jnp.zeros_like(dw_ref)
            db_ref[...] = jnp.zeros_like(db_ref)

        dw_ref[0:1, :] += jnp.sum(dc * x, axis=0, keepdims=True)
        dw_ref[1:2, :] += jnp.sum(dc * x1, axis=0, keepdims=True)
        dw_ref[2:3, :] += jnp.sum(dc * x2, axis=0, keepdims=True)
        db_ref[...] += jnp.sum(dc, axis=0, keepdims=True)

    return _pallas(
        body, name=name, grid=(T // tm,),
        in_specs=_conv_specs(T, F2, tm) + [_rows(tm, F), _whole((3, F2)), _whole((1, F2))],
        out_specs=[_rows(tm, F2), _whole((3, F2)), _whole((1, F2))],
        out_shape=[jax.ShapeDtypeStruct((T, F2), MXU_DTYPE), jax.ShapeDtypeStruct((3, F2), F32),
                   jax.ShapeDtypeStruct((1, F2), F32)],
        compiler_params=_params(("arbitrary",)),
    )(hc, hc, da, conv_w, conv_b)


def _conv_bwd_input(dc, conv_w, L, *, name):
    T, F2 = dc.shape
    tm = CONV_ROWS
    edge = CONV_EDGE
    last_blk = T // edge - 1

    def body(x_ref, e_ref, w_ref, o_ref):
        at_end = ((pl.program_id(0) + 1) * tm) % L == 0
        x = x_ref[...].astype(F32)
        rows = lax.broadcasted_iota(jnp.int32, x.shape, 0)
        keep = jnp.where(at_end, 0.0, 1.0)
        ev = e_ref[...].astype(F32)
        e0 = ev[0:1, :] * keep
        e1 = ev[1:2, :] * keep
        u1 = pltpu.roll(x, tm - 1, 0)
        u2 = pltpu.roll(x, tm - 2, 0)
        x1 = jnp.where(rows == tm - 1, e0, u1)
        x2 = jnp.where(rows == tm - 1, e1, jnp.where(rows == tm - 2, e0, u2))
        o_ref[...] = (w_ref[0:1, :] * x + w_ref[1:2, :] * x1 + w_ref[2:3, :] * x2).astype(o_ref.dtype)

    return _pallas(
        body, name=name, grid=(T // tm,),
        in_specs=[_rows(tm, F2),
                  pl.BlockSpec((edge, F2), lambda i: (jnp.minimum((i + 1) * (tm // edge), last_blk), 0)),
                  _whole((3, F2))],
        out_specs=_rows(tm, F2),
        out_shape=jax.ShapeDtypeStruct((T, F2), MXU_DTYPE), compiler_params=_params(("parallel",)),
    )(dc, dc, conv_w)


S5_CHUNK = 128
LANES = 128


def _slab_rows(c, n, ncl):
    return pl.ds(c, n) if ncl == 1 else pl.ds(c, n, stride=ncl)


def _slab_put(ref, c, n, ncl, val):
    for s in range(val.shape[1] // LANES):
        ref[s, _slab_rows(c, n, ncl), :] = val[:, s * LANES:(s + 1) * LANES]


def _slab_get(ref, c, n, ncl):
    return jnp.concatenate([ref[s, _slab_rows(c, n, ncl), :] for s in range(ref.shape[0])], axis=-1)


def _slabs(n_slab, rows):
    return pl.BlockSpec((n_slab, rows, LANES), lambda i: (0, i, 0))


def _s5_fwd(xi, wb, wc, a_r, a_i, d_row, B, *, name):
    T, D = xi.shape
    ncl = wb.shape[0]
    cs = wb.shape[2] // 2
    ns = cs // LANES
    R = B * ncl
    Q = S5_CHUNK
    QR = Q * ncl
    nsteps = Q // B

    def body(x_ref, wb_ref, wc_ref, ar_ref, ai_ref, d_ref, y_ref, yg_ref, hr_ref, hi_ref, bur, bui, cr, ci):
        @pl.when(pl.program_id(0) == 0)
        def _():
            cr[...] = jnp.zeros_like(cr)
            ci[...] = jnp.zeros_like(ci)

        x = x_ref[...]
        xb = x.astype(MXU_DTYPE)
        for c in range(ncl):
            bu = jnp.dot(xb[:, c * CLUSTER_W:(c + 1) * CLUSTER_W], wb_ref[c], preferred_element_type=F32)
            _slab_put(bur, c, Q, ncl, bu[:, :cs])
            _slab_put(bui, c, Q, ncl, bu[:, cs:])
        ar = ar_ref[...]
        ai = ai_ref[...]

        def step(k, carry):
            hr, hi = carry
            sl = pl.ds(pl.multiple_of(k * R, R), R)
            nr = ar * hr - ai * hi + bur[:, sl, :]
            ni = ar * hi + ai * hr + bui[:, sl, :]
            hr_ref[:, sl, :] = nr
            hi_ref[:, sl, :] = ni
            return nr, ni

        hr, hi = lax.fori_loop(0, nsteps, step, (cr[...], ci[...]), unroll=4)
        cr[...] = hr
        ci[...] = hi
        parts = []
        for c in range(ncl):
            hrc = _slab_get(hr_ref, c, Q, ncl).astype(MXU_DTYPE)
            hic = _slab_get(hi_ref, c, Q, ncl).astype(MXU_DTYPE)
            parts.append(jnp.dot(hrc, wc_ref[c, :cs, :], preferred_element_type=F32)
                         + jnp.dot(hic, wc_ref[c, cs:, :], preferred_element_type=F32))
        y = d_ref[...] * x + (parts[0] if ncl == 1 else jnp.concatenate(parts, axis=-1))
        y_ref[...] = y
        yg_ref[...] = _gelu(y).astype(yg_ref.dtype)

    return _pallas(
        body, name=name, grid=(T // Q,),
        in_specs=[_rows(Q, D), _whole(wb.shape), _whole(wc.shape), _whole((ns, R, LANES)), _whole((ns, R, LANES)),
                  _whole((1, D))],
        out_specs=[_rows(Q, D), _rows(Q, D), _slabs(ns, QR), _slabs(ns, QR)],
        out_shape=[jax.ShapeDtypeStruct((T, D), F32), jax.ShapeDtypeStruct((T, D), MXU_DTYPE),
                   jax.ShapeDtypeStruct((ns, T * ncl, LANES), F32), jax.ShapeDtypeStruct((ns, T * ncl, LANES), F32)],
        scratch_shapes=[pltpu.VMEM((ns, QR, LANES), F32), pltpu.VMEM((ns, QR, LANES), F32),
                        pltpu.VMEM((ns, R, LANES), F32), pltpu.VMEM((ns, R, LANES), F32)],
        compiler_params=_params(("arbitrary",)),
    )(xi, wb, wc, a_r, a_i, d_row)


def _s5_bwd(dy, xi, h_r, h_i, wb, wc, a_r, a_i, d_row, B, *, name):
    T, D = dy.shape
    ncl = wb.shape[0]
    cs = wb.shape[2] // 2
    ns = cs // LANES
    R = B * ncl
    Q = S5_CHUNK
    nsteps = Q // B
    nchunk = T // Q
    QR = Q * ncl

    def rev(i):
        return nchunk - 1 - i

    def body(dy_ref, x_ref, hr_ref, hi_ref, pr_ref, pi_ref, wb_ref, wc_ref, ar_ref, ai_ref, d_ref,
             du_ref, gr_ref, gi_ref, dar_ref, dai_ref, dd_ref, dhr, dhi, cr, ci):
        i = pl.program_id(0)

        @pl.when(i == 0)
        def _():
            cr[...] = jnp.zeros_like(cr)
            ci[...] = jnp.zeros_like(ci)
            dar_ref[...] = jnp.zeros_like(dar_ref)
            dai_ref[...] = jnp.zeros_like(dai_ref)
            dd_ref[...] = jnp.zeros_like(dd_ref)

        dyv = dy_ref[...]
        dyb = dyv.astype(MXU_DTYPE)
        for c in range(ncl):
            dh = lax.dot_general(dyb[:, c * CLUSTER_W:(c + 1) * CLUSTER_W], wc_ref[c],
                                 (((1,), (1,)), ((), ())), preferred_element_type=F32)
            _slab_put(dhr, c, Q, ncl, dh[:, :cs])
            _slab_put(dhi, c, Q, ncl, dh[:, cs:])
        ar = ar_ref[...]
        ai = ai_ref[...]

        def step(j, carry):
            gr, gi, sar, sai = carry
            k = nsteps - 1 - j
            sl = pl.ds(pl.multiple_of(k * R, R), R)
            ngr = dhr[:, sl, :] + ar * gr + ai * gi
            ngi = dhi[:, sl, :] - ai * gr + ar * gi
            gr_ref[:, sl, :] = ngr
            gi_ref[:, sl, :] = ngi
            pv = pl.ds(pl.multiple_of((k - 1) * R, R), R)
            hpr = hr_ref[:, pv, :]
            hpi = hi_ref[:, pv, :]
            return ngr, ngi, sar + ngr * hpr + ngi * hpi, sai - ngr * hpi + ngi * hpr

        gr, gi, sar, sai = lax.fori_loop(0, nsteps - 1, step, (cr[...], ci[...], dar_ref[...], dai_ref[...]), unroll=4)
        sl0 = pl.ds(0, R)
        ngr = dhr[:, sl0, :] + ar * gr + ai * gi
        ngi = dhi[:, sl0, :] - ai * gr + ar * gi
        gr_ref[:, sl0, :] = ngr
        gi_ref[:, sl0, :] = ngi
        keep = jnp.where(i == nchunk - 1, 0.0, 1.0)
        hpr = pr_ref[:, 8 - R:8, :] * keep
        hpi = pi_ref[:, 8 - R:8, :] * keep
        dar_ref[...] = sar + ngr * hpr + ngi * hpi
        dai_ref[...] = sai - ngr * hpi + ngi * hpr
        cr[...] = ngr
        ci[...] = ngi
        parts = []
        for c in range(ncl):
            grc = _slab_get(gr_ref, c, Q, ncl).astype(MXU_DTYPE)
            gic = _slab_get(gi_ref, c, Q, ncl).astype(MXU_DTYPE)
            parts.append(lax.dot_general(grc, wb_ref[c, :, :cs], (((1,), (1,)), ((), ())), preferred_element_type=F32)
                         + lax.dot_general(gic, wb_ref[c, :, cs:], (((1,), (1,)), ((), ())), preferred_element_type=F32))
        du_ref[...] = d_ref[...] * dyv + (parts[0] if ncl == 1 else jnp.concatenate(parts, axis=-1))
        dd_ref[...] += jnp.sum(dyv * x_ref[...], axis=0, keepdims=True)

    tok = pl.BlockSpec((Q, D), lambda i: (rev(i), 0))
    st = pl.BlockSpec((ns, QR, LANES), lambda i: (0, rev(i), 0))
    before = pl.BlockSpec((ns, 8, LANES), lambda i: (0, jnp.maximum(rev(i) * (QR // 8) - 1, 0), 0))
    acc = _whole((ns, R, LANES))
    return _pallas(
        body, name=name, grid=(nchunk,),
        in_specs=[tok, tok, st, st, before, before, _whole(wb.shape), _whole(wc.shape), acc, acc, _whole((1, D))],
        out_specs=[tok, st, st, acc, acc, _whole((1, D))],
        out_shape=[jax.ShapeDtypeStruct((T, D), F32),
                   jax.ShapeDtypeStruct((ns, T * ncl, LANES), F32), jax.ShapeDtypeStruct((ns, T * ncl, LANES), F32),
                   jax.ShapeDtypeStruct((ns, R, LANES), F32), jax.ShapeDtypeStruct((ns, R, LANES), F32),
                   jax.ShapeDtypeStruct((1, D), F32)],
        scratch_shapes=[pltpu.VMEM((ns, QR, LANES), F32)] * 2 + [pltpu.VMEM((ns, R, LANES), F32)] * 2,
        compiler_params=_params(("arbitrary",)),
    )(dy, xi, h_r, h_i, h_r, h_i, wb, wc, a_r, a_i, d_row)


def _cluster_tn(tok, st, ncl, *, tok_left, name):
    T = tok.shape[0]
    ns = st.shape[0]
    cs = ns * LANES
    tt = _pick(T, (512, 256, 128))
    nt = T // tt
    oshape = (ncl, CLUSTER_W, cs) if tok_left else (ncl, cs, CLUSTER_W)

    def body(tok_ref, st_ref, o_ref, acc):
        t = pl.program_id(0)

        @pl.when(t == 0)
        def _():
            acc[...] = jnp.zeros_like(acc)

        tk = tok_ref[...].astype(MXU_DTYPE)
        for c in range(ncl):
            tc = tk[:, c * CLUSTER_W:(c + 1) * CLUSTER_W]
            sc = _slab_get(st_ref, c, tt, ncl).astype(MXU_DTYPE)
            lhs, rhs = (tc, sc) if tok_left else (sc, tc)
            acc[c] += lax.dot_general(lhs, rhs, (((0,), (0,)), ((), ())), preferred_element_type=F32)

        @pl.when(t == nt - 1)
        def _():
            o_ref[...] = acc[...]

    return _pallas(
        body, name=name, grid=(nt,),
        in_specs=[_rows(tt, tok.shape[1]), _slabs(ns, tt * ncl)],
        out_specs=_whole(oshape),
        out_shape=jax.ShapeDtypeStruct(oshape, F32),
        scratch_shapes=[pltpu.VMEM(oshape, F32)],
        compiler_params=_params(("arbitrary",)),
    )(tok, st)


def _s5_discretize(lam_re, lam_im, log_dt, b_re, b_im):
    dt = jnp.exp(log_dt)[:, None]
    mag = jnp.exp(lam_re * dt)
    ab_r, ab_i = mag * jnp.cos(lam_im * dt), mag * jnp.sin(lam_im * dt)
    den = lam_re * lam_re + lam_im * lam_im
    nr = ab_r - 1.0
    co_r = (nr * lam_re + ab_i * lam_im) / den
    co_i = (ab_i * lam_re - nr * lam_im) / den
    bb_r = co_r[..., None] * b_re - co_i[..., None] * b_im
    bb_i = co_r[..., None] * b_im + co_i[..., None] * b_re
    return ab_r, ab_i, bb_r, bb_i


def _blockdiag(m):
    G, r, k = m.shape
    ncl = G // GROUPS_PER_CLUSTER
    m4 = m.reshape(ncl, GROUPS_PER_CLUSTER, r, k)
    eye = jnp.eye(GROUPS_PER_CLUSTER, dtype=m.dtype)
    return jnp.einsum('cgrk,gh->cgrhk', m4, eye).reshape(ncl, GROUPS_PER_CLUSTER * r, GROUPS_PER_CLUSTER * k)


def _unblockdiag(m, r, k):
    ncl = m.shape[0]
    m5 = m.reshape(ncl, GROUPS_PER_CLUSTER, r, GROUPS_PER_CLUSTER, k)
    eye = jnp.eye(GROUPS_PER_CLUSTER, dtype=m.dtype)
    return jnp.einsum('cgrhk,gh->cgrk', m5, eye).reshape(ncl * GROUPS_PER_CLUSTER, r, k)


def _t5_bucket(dist):
    exact = REL_BUCKETS // 2
    d = np.maximum(dist, 1).astype(np.float32)
    large = exact + (np.log(d / exact) / math.log(REL_MAX_DIST / exact) * (REL_BUCKETS - exact)).astype(np.int64)
    large = np.minimum(large, REL_BUCKETS - 1)
    return np.where(dist < exact, dist, large).astype(np.int32)


def _band_tables(dil):
    steps = np.arange(BAND)[:, None] + BAND - np.arange(2 * BAND)[None, :]
    bucket = _t5_bucket(np.maximum(steps, 0) * dil)
    in_band = (steps >= 0) & (steps <= BAND)
    return bucket, in_band


def _attn_bias(rel_bias, hpg):
    out = []
    for g, dil in enumerate(DILATIONS):
        bucket, in_band = _band_tables(dil)
        cols = rel_bias[:, g * hpg:(g + 1) * hpg].astype(F32)
        onehot = jnp.asarray((bucket.reshape(-1, 1) == np.arange(REL_BUCKETS)[None, :]).astype(np.float32))
        bias = jnp.dot(onehot, cols, precision=lax.Precision.HIGHEST).T.reshape(hpg, BAND, 2 * BAND)
        out.append(jnp.where(jnp.asarray(in_band)[None], bias, NEG_BIG))
    return jnp.concatenate(out, axis=0)


def _attn_blocks(dil, L):
    M = L // dil
    return M, M // BAND


def _row_sel(r, M, dil):
    return pl.ds(r, M) if dil == 1 else pl.ds(r, M, stride=dil)


def _attn_fwd(q, kv, bias, L, hpg, *, name):
    T = q.shape[0]
    nb_ = T // L
    HP = hpg // 2
    W3 = 3 * hpg * HEAD_DIM
    mmax = L

    def group_body(dil, q_ref, k_ref, v_ref, b_ref, o_ref, l_ref, os, ls):
        M, NB = _attn_blocks(dil, L)
        for r in range(dil):
            rows = _row_sel(r, M, dil)
            first = lax.broadcasted_iota(jnp.int32, (1, 2 * HEAD_DIM), 1) < HEAD_DIM
            qf = q_ref[rows, :] * 0.125
            qm = [jnp.where(first, qf, 0.0).astype(MXU_DTYPE), jnp.where(first, 0.0, qf).astype(MXU_DTYPE)]
            kr = k_ref[rows, :].astype(MXU_DTYPE)
            va = jnp.concatenate([v_ref[rows, :].astype(MXU_DTYPE), jnp.ones((M, 2 * HEAD_DIM), MXU_DTYPE)], axis=-1)
            for n in range(NB):
                qs = slice(n * BAND, (n + 1) * BAND)
                ks = slice(0, BAND) if n == 0 else slice((n - 1) * BAND, (n + 1) * BAND)
                o_h, l_h = [], []
                for hh in range(2):
                    bb = b_ref[hh, :, BAND:] if n == 0 else b_ref[hh]
                    s = lax.dot_general(qm[hh][qs, :], kr[ks, :], (((1,), (1,)), ((), ())),
                                        preferred_element_type=F32) + bb
                    m = jnp.max(s, axis=-1, keepdims=True)
                    p = jnp.exp(s - m)
                    pv = jnp.dot(p.astype(MXU_DTYPE), va[ks, :], preferred_element_type=F32)
                    l = pv[:, 2 * HEAD_DIM:]
                    o_h.append(pv[:, :2 * HEAD_DIM] / l)
                    l_h.append(m + jnp.log(l))
                os[qs, :] = jnp.where(first, o_h[0], o_h[1])
                ls[qs, :] = jnp.where(first, l_h[0], l_h[1])
            o_ref[rows, :] = os[0:M, :]
            l_ref[rows, :] = ls[0:M, :]

    def body(q_ref, k_ref, v_ref, b_ref, o_ref, l_ref, os, ls):
        g = pl.program_id(0)
        for gi, dil in enumerate(DILATIONS):
            pl.when(g == gi)(functools.partial(group_body, dil, q_ref, k_ref, v_ref, b_ref, o_ref, l_ref, os, ls))

    blk = (L, 2 * HEAD_DIM)
    return _pallas(
        body, name=name, grid=(3, nb_, HP),
        in_specs=[pl.BlockSpec(blk, lambda g, b, h: (b, g * HP + h)),
                  pl.BlockSpec(blk, lambda g, b, h: (b, g * HP + h)),
                  pl.BlockSpec(blk, lambda g, b, h: (b, 3 * HP + g * HP + h)),
                  pl.BlockSpec((2, BAND, 2 * BAND), lambda g, b, h: (g * HP + h, 0, 0))],
        out_specs=[pl.BlockSpec(blk, lambda g, b, h: (b, g * HP + h)),
                   pl.BlockSpec(blk, lambda g, b, h: (b, g * HP + h))],
        out_shape=[jax.ShapeDtypeStruct((T, W3), F32), jax.ShapeDtypeStruct((T, W3), F32)],
        scratch_shapes=[pltpu.VMEM((mmax, 2 * HEAD_DIM), F32), pltpu.VMEM((mmax, 2 * HEAD_DIM), F32)],
        compiler_params=_params(("arbitrary", "arbitrary", "arbitrary")),
    )(q, kv, kv, bias)


def _attn_merge(o3, l3, hw, *, name):
    T = o3.shape[0]
    tm = _pick(T, (256, 128))

    def body(o0, o1, o2, l0, l1, l2, o_ref, ob_ref, lse_ref):
        a0, a1, a2 = l0[...], l1[...], l2[...]
        m = jnp.maximum(jnp.maximum(a0, a1), a2)
        e0, e1, e2 = jnp.exp(a0 - m), jnp.exp(a1 - m), jnp.exp(a2 - m)
        z = e0 + e1 + e2
        o = (e0 * o0[...] + e1 * o1[...] + e2 * o2[...]) / z
        o_ref[...] = o
        ob_ref[...] = o.astype(ob_ref.dtype)
        lse_ref[...] = m + jnp.log(z)

    def col(g):
        return pl.BlockSpec((tm, hw), lambda i: (i, g))

    return _pallas(
        body, name=name, grid=(T // tm,),
        in_specs=[col(0), col(1), col(2), col(0), col(1), col(2)],
        out_specs=[_rows(tm, hw)] * 3,
        out_shape=[jax.ShapeDtypeStruct((T, hw), F32), jax.ShapeDtypeStruct((T, hw), MXU_DTYPE),
                   jax.ShapeDtypeStruct((T, hw), F32)],
        compiler_params=_params(("parallel",)),
    )(o3, o3, o3, l3, l3, l3)


def _attn_bwd(q, kv, do, o, lse, bias, L, hpg, *, name):
    T = q.shape[0]
    nb_ = T // L
    HP = hpg // 2
    W3 = 3 * hpg * HEAD_DIM
    mmax = L

    def group_body(dil, q_ref, k_ref, v_ref, do_ref, o_ref, l_ref, b_ref, dq_ref, dk_ref, dv_ref, ds_ref,
                   dqs, dks, dvs):
        M, NB = _attn_blocks(dil, L)
        for r in range(dil):
            rows = _row_sel(r, M, dil)
            first = lax.broadcasted_iota(jnp.int32, (1, 2 * HEAD_DIM), 1) < HEAD_DIM
            qf = q_ref[rows, :] * 0.125
            qm = [jnp.where(first, qf, 0.0).astype(MXU_DTYPE), jnp.where(first, 0.0, qf).astype(MXU_DTYPE)]
            kr = k_ref[rows, :].astype(MXU_DTYPE)
            vr = v_ref[rows, :].astype(MXU_DTYPE)
            dof = do_ref[rows, :]
            dom = [jnp.where(first, dof, 0.0).astype(MXU_DTYPE), jnp.where(first, 0.0, dof).astype(MXU_DTYPE)]
            dod = dof * o_ref[rows, :]
            delta = [jnp.sum(jnp.where(first, dod, 0.0), axis=-1, keepdims=True),
                     jnp.sum(jnp.where(first, 0.0, dod), axis=-1, keepdims=True)]
            lr = l_ref[rows, :]
            lse = [lr[:, 0:1], lr[:, HEAD_DIM:HEAD_DIM + 1]]
            dks[0:M, :] = jnp.zeros((M, 2 * HEAD_DIM), F32)
            dvs[0:M, :] = jnp.zeros((M, 2 * HEAD_DIM), F32)
            for n in range(NB):
                qs = slice(n * BAND, (n + 1) * BAND)
                ks = slice(0, BAND) if n == 0 else slice((n - 1) * BAND, (n + 1) * BAND)
                dq_h = []
                dkc = dvc = None
                for hh in range(2):
                    bb = b_ref[hh, :, BAND:] if n == 0 else b_ref[hh]
                    qb, dob = qm[hh][qs, :], dom[hh][qs, :]
                    s = lax.dot_general(qb, kr[ks, :], (((1,), (1,)), ((), ())), preferred_element_type=F32) + bb
                    p = jnp.exp(s - lse[hh][qs, :])
                    dp = lax.dot_general(dob, vr[ks, :], (((1,), (1,)), ((), ())), preferred_element_type=F32)
                    ds = p * (dp - delta[hh][qs, :])
                    if n == 0:
                        ds_ref[hh, :, BAND:] += ds
                    else:
                        ds_ref[hh] += ds
                    dsm = ds.astype(MXU_DTYPE)
                    dq_h.append(jnp.dot(dsm, kr[ks, :], preferred_element_type=F32))
                    dk1 = lax.dot_general(dsm, qb, (((0,), (0,)), ((), ())), preferred_element_type=F32)
                    dv1 = lax.dot_general(p.astype(MXU_DTYPE), dob, (((0,), (0,)), ((), ())), preferred_element_type=F32)
                    dkc = dk1 if dkc is None else dkc + dk1
                    dvc = dv1 if dvc is None else dvc + dv1
                dqs[qs, :] = jnp.where(first, dq_h[0], dq_h[1]) * 0.125
                dks[ks, :] += dkc
                dvs[ks, :] += dvc
            dq_ref[rows, :] = dqs[0:M, :]
            dk_ref[rows, :] = dks[0:M, :]
            dv_ref[rows, :] = dvs[0:M, :]

    def body(q_ref, k_ref, v_ref, do_ref, o_ref, l_ref, b_ref, dq_ref, dk_ref, dv_ref, ds_ref, dqs, dks, dvs):
        g = pl.program_id(0)

        @pl.when(pl.program_id(2) == 0)
        def _():
            ds_ref[...] = jnp.zeros_like(ds_ref)

        for gi, dil in enumerate(DILATIONS):
            pl.when(g == gi)(functools.partial(group_body, dil, q_ref, k_ref, v_ref, do_ref, o_ref, l_ref, b_ref,
                                               dq_ref, dk_ref, dv_ref, ds_ref, dqs, dks, dvs))

    blk = (L, 2 * HEAD_DIM)
    gcol = lambda g, h, b: (b, g * HP + h)
    hcol = lambda g, h, b: (b, h)
    return _pallas(
        body, name=name, grid=(3, HP, nb_),
        in_specs=[pl.BlockSpec(blk, gcol), pl.BlockSpec(blk, gcol),
                  pl.BlockSpec(blk, lambda g, h, b: (b, 3 * HP + g * HP + h)),
                  pl.BlockSpec(blk, hcol), pl.BlockSpec(blk, hcol), pl.BlockSpec(blk, hcol),
                  pl.BlockSpec((2, BAND, 2 * BAND), lambda g, h, b: (g * HP + h, 0, 0))],
        out_specs=[pl.BlockSpec(blk, gcol), pl.BlockSpec(blk, gcol), pl.BlockSpec(blk, gcol),
                   pl.BlockSpec((2, BAND, 2 * BAND), lambda g, h, b: (g * HP + h, 0, 0))],
        out_shape=[jax.ShapeDtypeStruct((T, W3), F32), jax.ShapeDtypeStruct((T, W3), F32),
                   jax.ShapeDtypeStruct((T, W3), F32), jax.ShapeDtypeStruct((3 * hpg, BAND, 2 * BAND), F32)],
        scratch_shapes=[pltpu.VMEM((mmax, 2 * HEAD_DIM), F32)] * 3,
        compiler_params=_params(("arbitrary", "arbitrary", "arbitrary")),
    )(q, kv, kv, do, o, lse, bias)


def _bias_grad(ds_sum, hpg, *, name):
    nh = ds_sum.shape[0]
    idx = np.stack([np.where(_band_tables(dil)[1], _band_tables(dil)[0], -1) for dil in DILATIONS]).astype(np.int32)

    def body(ds_ref, idx_ref, o_ref):
        d = ds_ref[...]
        ix = idx_ref[...]
        lane = lax.broadcasted_iota(jnp.int32, (8, 128), 1)
        row = jnp.zeros((8, 128), F32)
        for b in range(REL_BUCKETS):
            row = row + jnp.where(lane == b, jnp.sum(jnp.where(ix == b, d, 0.0)), 0.0)
        o_ref[...] = row

    out = _pallas(
        body, name=name, grid=(nh,),
        in_specs=[pl.BlockSpec((None, BAND, 2 * BAND), lambda h: (h, 0, 0)),
                  pl.BlockSpec((None, BAND, 2 * BAND), lambda h: (h // hpg, 0, 0))],
        out_specs=pl.BlockSpec((None, 8, 128), lambda h: (h, 0, 0)),
        out_shape=jax.ShapeDtypeStruct((nh, 8, 128), F32),
        compiler_params=_params(("parallel",)),
    )(ds_sum, jnp.asarray(idx))
    return out[:, 0, :REL_BUCKETS].T


def _adamw(w, g, m, v, *, name):
    Rw, C = w.shape
    tm = _pick(Rw, (512, 352, 256, 128, 64, 32, 16, 8))

    def body(w_ref, g_ref, m_ref, v_ref, d_ref, nm_ref, nv_ref):
        gg = g_ref[...]
        nm = ADAM_B1 * m_ref[...] + (1.0 - ADAM_B1) * gg
        nv = ADAM_B2 * v_ref[...] + (1.0 - ADAM_B2) * (gg * gg)
        m_hat = nm / (1.0 - ADAM_B1 ** ADAM_STEP)
        v_hat = nv / (1.0 - ADAM_B2 ** ADAM_STEP)
        d_ref[...] = -ADAM_LR * (m_hat / (jnp.sqrt(v_hat) + ADAM_EPS) + ADAM_WD * w_ref[...])
        nm_ref[...] = nm
        nv_ref[...] = nv

    return _pallas(
        body, name=name, grid=(Rw // tm,), in_specs=[_rows(tm, C)] * 4, out_specs=[_rows(tm, C)] * 3,
        out_shape=[jax.ShapeDtypeStruct((Rw, C), F32)] * 3, compiler_params=_params(("parallel",)),
    )(w, g, m, v)


ROW_TILE_ELEMS = 256 * 1024


def _tile_rows(r, c):
    best = 8
    for t in range(8, r + 1, 8):
        if r % t == 0 and t * c <= ROW_TILE_ELEMS:
            best = t
    return best


def _adamw_halves(w, m, v, mine, other, cidx, *, layer=0, prev=None, name):
    NL, _, r, c = w.shape
    tm = _tile_rows(r, c)

    def body(c_ref, w_ref, m_ref, v_ref, a_ref, b_ref, *rest):
        g_ref, d_ref, nm_ref, nv_ref = rest[-4:]
        gg = jnp.where(pl.program_id(0) == c_ref[0], a_ref[...], b_ref[...])
        nm = ADAM_B1 * m_ref[...] + (1.0 - ADAM_B1) * gg
        nv = ADAM_B2 * v_ref[...] + (1.0 - ADAM_B2) * (gg * gg)
        m_hat = nm / (1.0 - ADAM_B1 ** ADAM_STEP)
        v_hat = nv / (1.0 - ADAM_B2 ** ADAM_STEP)
        g_ref[...] = gg
        d_ref[...] = -ADAM_LR * (m_hat / (jnp.sqrt(v_hat) + ADAM_EPS) + ADAM_WD * w_ref[...])
        nm_ref[...] = nm
        nv_ref[...] = nv

    half = pl.BlockSpec((None, None, tm, c), lambda h, i, cr: (layer, h, i, 0))
    one = pl.BlockSpec((None, tm, c), lambda h, i, cr: (0, i, 0))
    in_specs = [half, half, half, one, one]
    args = [cidx, w, m, v, mine, other]
    aliases = {}
    if prev is not None:
        in_specs += [_ANY] * 4
        args += list(prev)
        aliases = {6 + k: k for k in range(4)}
    spec = pltpu.PrefetchScalarGridSpec(num_scalar_prefetch=1, grid=(2, r // tm), in_specs=in_specs, out_specs=[half] * 4)
    return _pallas(
        body, name=name, grid_spec=spec, out_shape=[jax.ShapeDtypeStruct((NL, 2, r, c), F32)] * 4,
        input_output_aliases=aliases, compiler_params=_params(("parallel", "parallel")),
    )(*args)


def _pair_sum(g, theirs, cidx, *, cast, name):
    _, _, r, c = g.shape
    tm = _tile_rows(r, c)

    def body(c_ref, g_ref, t_ref, *outs):
        s = g_ref[...] + t_ref[...]
        outs[0][...] = s
        if cast:
            outs[1][...] = s.astype(BF16)

    blk = (None, None, tm, c)
    first = pl.BlockSpec(blk, lambda p, i, cr: (p, 0, i, 0))
    shapes = [jax.ShapeDtypeStruct((4, 1, r, c), F32)] + ([jax.ShapeDtypeStruct((4, 1, r, c), BF16)] if cast else [])
    spec = pltpu.PrefetchScalarGridSpec(
        num_scalar_prefetch=1, grid=(4, r // tm),
        in_specs=[pl.BlockSpec(blk, lambda p, i, cr: (p, cr[0], i, 0)), first], out_specs=[first] * len(shapes))
    return _pallas(body, name=name, grid_spec=spec, out_shape=shapes,
                   compiler_params=_params(("parallel", "parallel")))(cidx, g, theirs)


def _chip_sum(hf, got, chip_idx, *, name):
    _, _, r, c = hf.shape
    tm = _tile_rows(r, c)

    def body(p_ref, h_ref, r_ref, o_ref):
        s = h_ref[...]
        for k in range(3):
            s = s + r_ref[k].astype(F32)
        o_ref[...] = s

    spec = pltpu.PrefetchScalarGridSpec(
        num_scalar_prefetch=1, grid=(r // tm,),
        in_specs=[pl.BlockSpec((None, None, tm, c), lambda i, pr: (pr[0], 0, i, 0)),
                  pl.BlockSpec((3, None, tm, c), lambda i, pr: (0, 0, i, 0))],
        out_specs=pl.BlockSpec((None, tm, c), lambda i, pr: (0, i, 0)))
    return _pallas(body, name=name, grid_spec=spec, out_shape=jax.ShapeDtypeStruct((1, r, c), F32),
                   compiler_params=_params(("parallel",)))(chip_idx, hf, got)


def _place():
    x, y, c = lax.axis_index("x"), lax.axis_index("y"), lax.axis_index("c")
    chips = [(1 - x, y), (x, 1 - y), (1 - x, 1 - y)]
    return x, y, c, chips


_ANY = pl.BlockSpec(memory_space=pl.ANY)


def _comm_call(body, ins, out_shapes, n_remote, *, name, aliases=None):
    sems = [pltpu.SemaphoreType.DMA((n,)) for n in n_remote]
    return _pallas(
        body, name=name, in_specs=[_ANY] * len(ins), out_specs=[_ANY] * len(out_shapes), out_shape=out_shapes,
        scratch_shapes=sems, input_output_aliases=aliases or {},
        compiler_params=pltpu.CompilerParams(has_side_effects=True),
    )(*ins)


_HBM_SPEC = pl.BlockSpec(memory_space=pltpu.HBM)
_SEM_SPEC = pl.BlockSpec(memory_space=pltpu.SEMAPHORE)
_DATAFLOW = pltpu.SideEffectType.DATAFLOW_SIDE_EFFECTING


def _in_hbm(a):
    return pltpu.with_memory_space_constraint(a, pltpu.HBM)


def _gather_start(groups, *, name):
    flat = [s for g in groups for s in g]
    n, ng = len(flat), len(groups)

    def body(*refs):
        ins, lands = refs[:n], refs[n:2 * n]
        sems = refs[2 * n:2 * n + 2 * ng]
        token = refs[-1]
        x, y, c, chips = _place()
        me = 2 * x + y
        a = 0
        for gi, g in enumerate(groups):
            for j in range(len(g)):
                for k, (tx, ty) in enumerate(chips):
                    _rcopy(ins[a].at[c], lands[a].at[me, c], sems[2 * gi].at[3 * j + k], sems[2 * gi + 1].at[3 * j + k],
                           (tx, ty, c)).start()
                a += 1
        token[...] = jnp.zeros_like(token)

    land_shapes = [(4,) + s.shape for s in flat]
    out_shape = ([pltpu.SemaphoreType.DMA((3 * len(g),)) for g in groups for _ in range(2)]
                 + [pltpu.HBM(s.shape, s.dtype) for s in flat]
                 + [pltpu.HBM(ls, s.dtype) for ls, s in zip(land_shapes, flat)]
                 + [jax.ShapeDtypeStruct((8, 128), F32)])
    outs = _pallas(
        body, name=name, in_specs=[_HBM_SPEC] * (2 * n),
        out_specs=[_SEM_SPEC] * (2 * ng) + [_HBM_SPEC] * (2 * n) + [pl.BlockSpec(memory_space=pltpu.VMEM)],
        out_shape=out_shape, input_output_aliases={i: 2 * ng + i for i in range(2 * n)},
        compiler_params=pltpu.CompilerParams(has_side_effects=_DATAFLOW),
    )(*[_in_hbm(s) for s in flat], *[_in_hbm(lax.empty(ls, s.dtype)) for ls, s in zip(land_shapes, flat)])
    sems, thru, lands, token = outs[:2 * ng], outs[2 * ng:2 * ng + n], outs[2 * ng + n:2 * ng + 2 * n], outs[-1]
    res, a = [], 0
    for gi, g in enumerate(groups):
        res.append((sems[2 * gi], sems[2 * gi + 1], thru[a:a + len(g)], lands[a:a + len(g)]))
        a += len(g)
    return res, token


def _gather_wait(ssem, rsem, shards, lands, after, *, name):
    m = len(shards)

    def body(*refs):
        ins, lnd = refs[:m], refs[m:2 * m]
        ss, rs = refs[2 * m], refs[2 * m + 1]
        x, y, c, chips = _place()
        for j in range(m):
            for k, (tx, ty) in enumerate(chips):
                cp = _rcopy(ins[j].at[c], lnd[j].at[2 * tx + ty, c], ss.at[3 * j + k], rs.at[3 * j + k], (tx, ty, c))
                cp.wait_send()
                cp.wait_recv()

    outs = _pallas(
        body, name=name, in_specs=[_HBM_SPEC] * (2 * m) + [_SEM_SPEC, _SEM_SPEC, _ANY],
        out_specs=[_HBM_SPEC] * (2 * m),
        out_shape=[pltpu.HBM(s.shape, s.dtype) for s in shards] + [pltpu.HBM(l.shape, l.dtype) for l in lands],
        input_output_aliases={i: i for i in range(2 * m)},
        compiler_params=pltpu.CompilerParams(has_side_effects=_DATAFLOW),
    )(*shards, *lands, ssem, rsem, after)
    return outs[m:]


def _gather_forward(lands, *, name):
    n = len(lands)

    def body(*refs):
        outs = refs[n:2 * n]
        ssem, rsem = refs[2 * n:]
        x, y, c, chips = _place()
        sib = (x, y, 1 - c)
        cps = []
        for a in range(n):
            for k, (tx, ty) in enumerate(chips):
                pk = 2 * tx + ty
                cp = _rcopy(outs[a].at[pk, c], outs[a].at[pk, c], ssem.at[3 * a + k], rsem.at[3 * a + k], sib)
                cp.start()
                cps.append(cp)
        for a in range(n):
            for k, (tx, ty) in enumerate(chips):
                pk = 2 * tx + ty
                _rcopy(outs[a].at[pk, c], outs[a].at[pk, 1 - c], ssem.at[3 * a + k], rsem.at[3 * a + k], sib).wait_recv()
        for cp in cps:
            cp.wait_send()

    shapes = [jax.ShapeDtypeStruct(l.shape, l.dtype) for l in lands]
    return _comm_call(body, lands, shapes, [3 * n, 3 * n], name=name, aliases={i: i for i in range(n)})


class _Lazy:
    def __init__(self, group_of, make):
        self._group_of, self._make, self._done, self._anchor = group_of, make, {}, None

    def anchor(self, value):
        self._anchor = value

    def __getitem__(self, key):
        g = self._group_of[key]
        if g not in self._done:
            self._done[g] = self._make(g, self._anchor)
        return self._done[g][key]


def _anchor(mapping, value):
    if isinstance(mapping, _Lazy):
        mapping.anchor(value)


def _rcopy(src, dst, ssem, rsem, dev):
    return pltpu.make_async_remote_copy(src_ref=src, dst_ref=dst, send_sem=ssem, recv_sem=rsem,
                                        device_id=dev, device_id_type=MESH)


def _all_gather(shards, *, name):
    n = len(shards)

    def body(*refs):
        ins, outs = refs[:n], refs[n:2 * n]
        s_ici, r_ici, s_d2d, r_d2d = refs[2 * n:]
        x, y, c, chips = _place()
        me = 2 * x + y
        sib = (x, y, 1 - c)
        sends = []
        for a in range(n):
            for k, (tx, ty) in enumerate(chips):
                cp = _rcopy(ins[a].at[c], outs[a].at[me, c], s_ici.at[3 * a + k], r_ici.at[3 * a + k], (tx, ty, c))
                cp.start()
                sends.append(cp)
        for a in range(n):
            for k, (tx, ty) in enumerate(chips):
                pk = 2 * tx + ty
                _rcopy(ins[a].at[c], outs[a].at[pk, c], s_ici.at[3 * a + k], r_ici.at[3 * a + k], (tx, ty, c)).wait_recv()
                fw = _rcopy(outs[a].at[pk, c], outs[a].at[pk, c], s_d2d.at[3 * a + k], r_d2d.at[3 * a + k], sib)
                fw.start()
                sends.append(fw)
        for a in range(n):
            for k, (tx, ty) in enumerate(chips):
                pk = 2 * tx + ty
                _rcopy(ins[a].at[c], outs[a].at[pk, 1 - c], s_d2d.at[3 * a + k], r_d2d.at[3 * a + k], sib).wait_recv()
        for cp in sends:
            cp.wait_send()

    shapes = [jax.ShapeDtypeStruct((4,) + s.shape, s.dtype) for s in shards]
    return _comm_call(body, shards, shapes, [3 * n] * 4, name=name)


def _gather(shards, chip, *, name):
    outs = _all_gather(shards, name=name)
    return [lax.dynamic_update_slice(o, s[None], (chip, 0, 0, 0)) for o, s in zip(outs, shards)]


def _pair_send(gs, *, name):
    n = len(gs)

    def body(*refs):
        ins, theirs = refs[:n], refs[n:2 * n]
        ssem, rsem = refs[2 * n:]
        x, y, c, _ = _place()
        sib = (x, y, 1 - c)
        cps = []
        for a in range(n):
            cp = _rcopy(ins[a].at[:, pl.ds(1 - c, 1)], theirs[a], ssem.at[a], rsem.at[a], sib)
            cp.start()
            cps.append(cp)
        for cp in cps:
            cp.wait_send()
            cp.wait_recv()

    shapes = [jax.ShapeDtypeStruct((4, 1) + g.shape[2:], g.dtype) for g in gs]
    return _comm_call(body, gs, shapes, [n, n], name=name)


def _chip_exchange(hx, *, name):
    n = len(hx)

    def body(*refs):
        hxr, got = refs[:n], refs[n:2 * n]
        ssem, rsem = refs[2 * n:]
        x, y, c, chips = _place()
        cps = []
        for a in range(n):
            for k, (tx, ty) in enumerate(chips):
                cp = _rcopy(hxr[a].at[2 * tx + ty], got[a].at[k], ssem.at[3 * a + k], rsem.at[3 * a + k], (tx, ty, c))
                cp.start()
                cps.append(cp)
        for cp in cps:
            cp.wait_send()
            cp.wait_recv()

    shapes = [jax.ShapeDtypeStruct((3,) + h.shape[1:], h.dtype) for h in hx]
    return _comm_call(body, hx, shapes, [3 * n, 3 * n], name=name)


def _pair_swap(fs, *, name):
    n = len(fs)

    def body(*refs):
        ins, outs = refs[:n], refs[n:2 * n]
        ssem, rsem = refs[2 * n:]
        x, y, c, _ = _place()
        cps = []
        for a in range(n):
            cp = _rcopy(ins[a], outs[a], ssem.at[a], rsem.at[a], (x, y, 1 - c))
            cp.start()
            cps.append(cp)
        for cp in cps:
            cp.wait_send()
            cp.wait_recv()

    shapes = [jax.ShapeDtypeStruct(f.shape, f.dtype) for f in fs]
    return _comm_call(body, fs, shapes, [n, n], name=name)


def _chip_exchange_start(hx, *, name):
    n = len(hx)

    def body(*refs):
        ins, gots = refs[:n], refs[n:2 * n]
        ssem, rsem = refs[2 * n], refs[2 * n + 1]
        token = refs[-1]
        x, y, c, chips = _place()
        for a in range(n):
            for k, (tx, ty) in enumerate(chips):
                _rcopy(ins[a].at[2 * tx + ty], gots[a].at[k], ssem.at[3 * a + k], rsem.at[3 * a + k], (tx, ty, c)).start()
        token[...] = jnp.zeros_like(token)

    got_shapes = [(3,) + h.shape[1:] for h in hx]
    outs = _pallas(
        body, name=name, in_specs=[_HBM_SPEC] * (2 * n),
        out_specs=[_SEM_SPEC] * 2 + [_HBM_SPEC] * (2 * n) + [pl.BlockSpec(memory_space=pltpu.VMEM)],
        out_shape=([pltpu.SemaphoreType.DMA((3 * n,))] * 2 + [pltpu.HBM(h.shape, h.dtype) for h in hx]
                   + [pltpu.HBM(gs, h.dtype) for gs, h in zip(got_shapes, hx)] + [jax.ShapeDtypeStruct((8, 128), F32)]),
        input_output_aliases={i: 2 + i for i in range(2 * n)},
        compiler_params=pltpu.CompilerParams(has_side_effects=_DATAFLOW),
    )(*[_in_hbm(h) for h in hx], *[_in_hbm(lax.empty(gs, h.dtype)) for gs, h in zip(got_shapes, hx)])
    return (outs[0], outs[1], outs[2:2 + n], outs[2 + n:2 + 2 * n]), outs[-1]


def _chip_exchange_wait(started, after, *, name):
    ssem, rsem, hx, gots = started
    n = len(hx)

    def body(*refs):
        ins, gts = refs[:n], refs[n:2 * n]
        ss, rs = refs[2 * n], refs[2 * n + 1]
        x, y, c, chips = _place()
        for a in range(n):
            for k, (tx, ty) in enumerate(chips):
                cp = _rcopy(ins[a].at[2 * tx + ty], gts[a].at[k], ss.at[3 * a + k], rs.at[3 * a + k], (tx, ty, c))
                cp.wait_send()
                cp.wait_recv()

    outs = _pallas(
        body, name=name, in_specs=[_HBM_SPEC] * (2 * n) + [_SEM_SPEC, _SEM_SPEC, _ANY],
        out_specs=[_HBM_SPEC] * (2 * n),
        out_shape=[pltpu.HBM(h.shape, h.dtype) for h in hx] + [pltpu.HBM(g.shape, g.dtype) for g in gots],
        input_output_aliases={i: i for i in range(2 * n)},
        compiler_params=pltpu.CompilerParams(has_side_effects=_DATAFLOW),
    )(*hx, *gots, ssem, rsem, after)
    return outs[n:]


def _pair_sums(grads, exch_bf16, cidx, tag):
    theirs = _pair_send(grads, name=f"rs_pair_send_{tag}")
    hf, hx = [], []
    for a in range(len(grads)):
        res = _pair_sum(grads[a], theirs[a], cidx, cast=exch_bf16[a], name=f"rs_pair_sum_{tag}{a}")
        hf.append(res[0])
        hx.append(res[1] if exch_bf16[a] else res[0])
    return hf, hx


def _chip_sums(hf, got, chip_idx, tag):
    return [_chip_sum(hf[a], got[a], chip_idx, name=f"rs_chip_sum_{tag}{a}") for a in range(len(hf))]


def _interleave(a, B, L):
    return a.reshape(B, L, -1).transpose(1, 0, 2).reshape(B * L, -1)


def _deinterleave(a, B, L):
    return a.reshape(L, B, -1).transpose(1, 0, 2).reshape(B * L, -1)


def _local_step(x, tgt, W, S, on_layer1_grads=None):
    B, L, D = x.shape
    T = B * L
    G = D // SSM_GROUP
    Pst = SSM_STATE
    hpg = D // HEAD_DIM
    HW = hpg * HEAD_DIM
    ncl = G // GROUPS_PER_CLUSTER
    x2 = x.reshape(T, D)
    tgt2 = tgt.reshape(T, D)

    disc = lambda *p: _s5_discretize(*p)
    (ab_r, ab_i, bb_r, bb_i), disc_vjp = jax.vjp(disc, S["lam_re"], S["lam_im"], S["log_dt"], S["b_re"], S["b_im"])
    wb = jnp.concatenate([_blockdiag(jnp.transpose(bb_r, (0, 2, 1))), _blockdiag(jnp.transpose(bb_i, (0, 2, 1)))],
                         axis=-1).astype(MXU_DTYPE)
    wc = jnp.concatenate([_blockdiag(jnp.transpose(S["c_re"], (0, 2, 1))), _blockdiag(-jnp.transpose(S["c_im"], (0, 2, 1)))],
                         axis=1).astype(MXU_DTYPE)
    cs = GROUPS_PER_CLUSTER * Pst
    slab = lambda ab: jnp.tile(jnp.transpose(ab.reshape(ncl, cs // LANES, LANES), (1, 0, 2)), (1, B, 1))
    a_r, a_i = slab(ab_r), slab(ab_i)
    d_row = S["d"].reshape(1, D)

    xi = _interleave(x2, B, L)
    y, yg, h_r, h_i = _s5_fwd(xi, wb, wc, a_r, a_i, d_row, B, name="s5_fwd")
    _anchor(W, yg)
    z = _mm_nn(yg, W["w_glu"], bias=S["b_glu"].reshape(1, D), name="glu_z")
    gate = _glu_gate(y, z, name="glu_gate")
    mix = _deinterleave(_mm_nn(gate, W["w_out"], name="s5_out"), B, L)
    h1, h1b, xh1, rs1 = _ln_fwd(x2, mix, S["ln_gain"][0, 0][None], S["ln_bias"][0, 0][None], name="ln_fwd_0a")

    def ffn_fwd(hb, l):
        hc = _mm_nn(hb, W["w_up"], l=l, out_dtype=MXU_DTYPE, name=f"ffn_up_{l}")
        a = _conv_glu_fwd(hc, S["conv_w"][l], S["conv_b"][l][None], L, name=f"ffn_conv_{l}")
        f = _mm_nn(a, W["w_down"], l=l, name=f"ffn_down_{l}")
        return hc, a, f

    _anchor(W, h1b)
    hc0, a0, f0 = ffn_fwd(h1b, 0)
    h2, h2b, xh2, rs2 = _ln_fwd(h1, f0, S["ln_gain"][0, 1][None], S["ln_bias"][0, 1][None], name="ln_fwd_0b")

    _anchor(W, h2b)
    kv = _mm_nn(h2b, W["w_kv"], name="attn_kv")
    q = _mm_nn(h2b, W["w_q"], name="attn_q")
    bias = _attn_bias(S["rel_bias"], hpg)
    o3, l3 = _attn_fwd(q, kv, bias, L, hpg, name="attn_fwd")
    o, ob, lse = _attn_merge(o3, l3, HW, name="attn_merge")
    att = _mm_nn(ob, W["w_ao"], name="attn_out")
    h3, h3b, xh3, rs3 = _ln_fwd(h2, att, S["ln_gain"][1, 0][None], S["ln_bias"][1, 0][None], name="ln_fwd_1a")
    hc1, a1, f1 = ffn_fwd(h3b, 1)
    h4, _, xh4, rs4 = _ln_fwd(h3, f1, S["ln_gain"][1, 1][None], S["ln_bias"][1, 1][None], name="ln_fwd_1b")

    dh4, lrow = _loss_grad(h4, tgt2, name="loss")
    loss = lrow[0, 0]

    GW, GS = {}, {}

    def ffn_bwd(dzb, hb, hc, a, l):
        da = _mm_nt(dzb, W["w_down"], l=l, out_dtype=MXU_DTYPE, name=f"ffn_down_bwd_x_{l}")
        GW[f"w_down{l}"] = _tn(a, dzb, ptotal=1, np_cols=D, name=f"ffn_down_bwd_w_{l}")
        dc, dcw, dcb = _conv_glu_bwd(hc, da, S["conv_w"][l], S["conv_b"][l][None], L, name=f"ffn_conv_bwd_{l}")
        dhc = _conv_bwd_input(dc, S["conv_w"][l], L, name=f"ffn_conv_bwd_x_{l}")
        dh = _mm_nt(dhc, W["w_up"], l=l, name=f"ffn_up_bwd_x_{l}")
        GW[f"w_up{l}"] = _tn(hb, dhc, ptotal=W["w_up"].shape[0], np_cols=W["w_up"].shape[3], name=f"ffn_up_bwd_w_{l}")
        return dh, dcw, dcb

    dz4, dz4b, dg4, db4 = _ln_bwd([dh4], [1.0], xh4, rs4, S["ln_gain"][1, 1][None], name="ln_bwd_1b")
    dh3f, dcw1, dcb1 = ffn_bwd(dz4b, h3b, hc1, a1, 1)
    dz3, dz3b, dg3, db3 = _ln_bwd([dz4, dh3f], [DN_ALPHA, 1.0], xh3, rs3, S["ln_gain"][1, 0][None], name="ln_bwd_1a")
    do = _mm_nt(dz3b, W["w_ao"], name="attn_out_bwd_x")
    GW["w_ao"] = _tn(ob, dz3b, ptotal=1, np_cols=D, name="attn_out_bwd_w")
    dq, dk, dv, ds_sum = _attn_bwd(q, kv, do, o, lse, bias, L, hpg, name="attn_bwd")
    GS["rel_bias"] = _bias_grad(ds_sum, hpg, name="attn_bias_grad")
    GW["w_q"] = _tn(h2b, dq, ptotal=W["w_q"].shape[0], np_cols=W["w_q"].shape[3], name="attn_q_bwd_w")
    pkv, npkv = W["w_kv"].shape[0], W["w_kv"].shape[3]
    gkv = _tn(h2b, dk, ptotal=pkv, np_cols=npkv, p0=0, name="attn_k_bwd_w")
    GW["w_kv"] = _tn(h2b, dv, ptotal=pkv, np_cols=npkv, p0=pkv // 2, prev=gkv, name="attn_v_bwd_w")
    dh2q = _mm_nt(dq, W["w_q"], name="attn_q_bwd_x")
    dh2k = _mm_nt(dk, W["w_kv"], p0=0, pn=pkv // 2, name="attn_k_bwd_x")
    dh2v = _mm_nt(dv, W["w_kv"], p0=pkv // 2, pn=pkv // 2, name="attn_v_bwd_x")

    gain_0b = S["ln_gain"][0, 1][None]
    if on_layer1_grads is not None:
        gain_0b = gain_0b + on_layer1_grads(GW)[0, 0]

    dz2, dz2b, dg2, db2 = _ln_bwd([dz3, dh2q, dh2k, dh2v], [DN_ALPHA, 1.0, 1.0, 1.0], xh2, rs2, gain_0b,
                                  name="ln_bwd_0b")
    dh1f, dcw0, dcb0 = ffn_bwd(dz2b, h1b, hc0, a0, 0)
    dz1, dz1b, dg1, db1 = _ln_bwd([dz2, dh1f], [DN_ALPHA, 1.0], xh1, rs1, S["ln_gain"][0, 0][None], name="ln_bwd_0a")
    dmix_i = _interleave(dz1b, B, L)
    dgate = _mm_nt(dmix_i, W["w_out"], name="s5_out_bwd_x")
    GW["w_out"] = _tn(gate, dmix_i, ptotal=1, np_cols=D, name="s5_out_bwd_w")
    dzg, dyg1, dbglu = _glu_bwd(y, z, dgate, name="glu_bwd")
    dyg2 = _mm_nt(dzg, W["w_glu"], name="glu_z_bwd_x")
    GW["w_glu"] = _tn(yg, dzg, ptotal=1, np_cols=D, name="glu_z_bwd_w")
    dy = _gelu_bwd(y, dyg1, dyg2, name="gelu_bwd")
    du_i, g_r, g_i, dar, dai, dd = _s5_bwd(dy, xi, h_r, h_i, wb, wc, a_r, a_i, d_row, B, name="s5_bwd")
    dwb_r = _cluster_tn(xi, g_r, ncl, tok_left=True, name="s5_b_grad_re")
    dwb_i = _cluster_tn(xi, g_i, ncl, tok_left=True, name="s5_b_grad_im")
    dwc_r = _cluster_tn(dy, h_r, ncl, tok_left=False, name="s5_c_grad_re")
    dwc_i = _cluster_tn(dy, h_i, ncl, tok_left=False, name="s5_c_grad_im")
    grad_x = _axpy(dz1, _deinterleave(du_i, B, L), DN_ALPHA, name="grad_x")

    dbb_r = jnp.transpose(_unblockdiag(dwb_r, SSM_GROUP, Pst), (0, 2, 1))
    dbb_i = jnp.transpose(_unblockdiag(dwb_i, SSM_GROUP, Pst), (0, 2, 1))
    unslab = lambda da: jnp.transpose(da.reshape(cs // LANES, B, ncl, LANES).sum(1), (1, 0, 2)).reshape(G, Pst)
    dab_r, dab_i = unslab(dar), unslab(dai)
    GS["lam_re"], GS["lam_im"], GS["log_dt"], GS["b_re"], GS["b_im"] = disc_vjp((dab_r, dab_i, dbb_r, dbb_i))
    GS["c_re"] = jnp.transpose(_unblockdiag(dwc_r, Pst, SSM_GROUP), (0, 2, 1))
    GS["c_im"] = -jnp.transpose(_unblockdiag(dwc_i, Pst, SSM_GROUP), (0, 2, 1))
    GS["d"] = dd.reshape(G, SSM_GROUP)
    GS["b_glu"] = dbglu.reshape(D)
    GS["conv_w"] = jnp.stack([dcw0, dcw1])
    GS["conv_b"] = jnp.stack([dcb0[0], dcb1[0]])
    GS["ln_gain"] = jnp.stack([jnp.stack([dg1[0], dg2[0]]), jnp.stack([dg3[0], dg4[0]])])
    GS["ln_bias"] = jnp.stack([jnp.stack([db1[0], db2[0]]), jnp.stack([db3[0], db4[0]])])
    return loss, grad_x.reshape(B, L, D), GW, GS


SMALL_REPLICATED = ("lam_re", "lam_im", "log_dt", "b_re", "b_im", "c_re", "c_im", "d", "rel_bias", "conv_b")
SMALL_SHARDED = ("b_glu", "conv_w", "ln_gain", "ln_bias")
SMALL_ORDER = SMALL_REPLICATED + SMALL_SHARDED


def _pack(arrs, lanes, row_mult):
    flat = jnp.concatenate([a.reshape(-1).astype(F32) for a in arrs])
    rows = -(-flat.shape[0] // lanes)
    rows = -(-rows // row_mult) * row_mult
    return jnp.pad(flat, (0, rows * lanes - flat.shape[0])).reshape(rows, lanes)


def _unpack(packed, shapes):
    flat = packed.reshape(-1)
    out, off = [], 0
    for s in shapes:
        n = int(np.prod(s))
        out.append(flat[off:off + n].reshape(s))
        off += n
    return out


def kernel(x, s5_lam_re, s5_lam_im, s5_log_dt, s5_b_re, s5_b_im, s5_c_re, s5_c_im, s5_d, s5_w_glu, s5_b_glu, s5_w_out, attn_w_kv, attn_w_q, attn_w_out, rel_bias, ffn_w_up, ffn_conv_w, ffn_conv_b, ffn_w_down, ln_gain, ln_bias, loss_target, m_s5_lam_re, m_s5_lam_im, m_s5_log_dt, m_s5_b_re, m_s5_b_im, m_s5_c_re, m_s5_c_im, m_s5_d, m_s5_w_glu, m_s5_b_glu, m_s5_w_out, m_attn_w_kv, m_attn_w_q, m_attn_w_out, m_rel_bias, m_ffn_w_up, m_ffn_conv_w, m_ffn_conv_b, m_ffn_w_down, m_ln_gain, m_ln_bias, v_s5_lam_re, v_s5_lam_im, v_s5_log_dt, v_s5_b_re, v_s5_b_im, v_s5_c_re, v_s5_c_im, v_s5_d, v_s5_w_glu, v_s5_b_glu, v_s5_w_out, v_attn_w_kv, v_attn_w_q, v_attn_w_out, v_rel_bias, v_ffn_w_up, v_ffn_conv_w, v_ffn_conv_b, v_ffn_w_down, v_ln_gain, v_ln_bias):
    names = ["s5_lam_re", "s5_lam_im", "s5_log_dt", "s5_b_re", "s5_b_im", "s5_c_re", "s5_c_im", "s5_d", "s5_w_glu",
             "s5_b_glu", "s5_w_out", "attn_w_kv", "attn_w_q", "attn_w_out", "rel_bias", "ffn_w_up", "ffn_conv_w",
             "ffn_conv_b", "ffn_w_down", "ln_gain", "ln_bias"]
    loc = locals()
    w_in = {n: loc[n] for n in names}
    m_in = {n: loc["m_" + n] for n in names}
    v_in = {n: loc["v_" + n] for n in names}
    chip = 2 * lax.axis_index("x") + lax.axis_index("y")
    core = lax.axis_index("c")
    chip_idx = jnp.reshape(chip, (1,)).astype(jnp.int32)
    cidx = jnp.reshape(core, (1,)).astype(jnp.int32)

    big = [("w_glu", "s5_w_glu", "rows"), ("w_out", "s5_w_out", "rows"), ("w_ao", "attn_w_out", "rows"),
           ("w_kv", "attn_w_kv", "cols"), ("w_q", "attn_w_q", "cols"),
           ("w_up", "ffn_w_up", "layer_cols"), ("w_down", "ffn_w_down", "layer_rows")]

    def halves(t, kind):
        if kind.startswith("layer"):
            return t
        r, c = t.shape[-2:]
        return t.reshape(2, r // 2, c)

    def to_weight(g, kind):
        _, _, r, c = g.shape
        if kind == "rows":
            return g.reshape(1, 1, 8 * r, c)
        if kind == "cols":
            return g.reshape(4, 1, 2 * r, c)
        if kind == "layer_cols":
            return g
        return jnp.transpose(g, (1, 0, 2, 3)).reshape(1, 2, 4 * r, c)

    small_sh = {"b_glu": s5_b_glu[0], "conv_w": ffn_conv_w, "ln_gain": ln_gain, "ln_bias": ln_bias}
    sh_shapes = [small_sh[k].shape for k in SMALL_SHARDED]
    sh_pack = _pack([small_sh[k] for k in SMALL_SHARDED], 128, 16)

    shards = [halves(w_in[src].astype(MXU_DTYPE), kind) for _, src, kind in big]
    shards.append(sh_pack.reshape(2, sh_pack.shape[0] // 2, 128))
    shard_of = {key: s for (key, _, _), s in zip(big, shards)}
    shard_of["small"] = shards[-1]
    kind_of = {key: kind for key, _, kind in big}

    group_keys = [["w_glu", "w_out", "small"], ["w_up", "w_down"], ["w_kv", "w_q", "w_ao"]]
    started, token = _gather_start([[shard_of[k] for k in g] for g in group_keys], name="weights_gather_start")

    def finish_group(gi, after):
        ssem, rsem, thru, lands = started[gi]
        lands = _gather_wait(ssem, rsem, thru, lands, after, name=f"weights_gather_wait_{gi}")
        lands = _gather_forward(lands, name=f"weights_gather_forward_{gi}")
        out = {}
        for key, land in zip(group_keys[gi], lands):
            full = lax.dynamic_update_slice(land, shard_of[key][None], (chip, 0, 0, 0))
            if key == "small":
                parts = [_unpack(full[p], sh_shapes) for p in range(4)]
                for i, k in enumerate(SMALL_SHARDED):
                    out[k] = jnp.concatenate([parts[p][i] for p in range(4)], axis=-1)
            else:
                out[key] = to_weight(full, kind_of[key])
        return out

    replicated = dict(lam_re=s5_lam_re[0], lam_im=s5_lam_im[0], log_dt=s5_log_dt[0], b_re=s5_b_re[0], b_im=s5_b_im[0],
                      c_re=s5_c_re[0], c_im=s5_c_im[0], rel_bias=rel_bias, conv_b=ffn_conv_b,
                      d=s5_d[0] + token[0, 0])
    group_of = {k: gi for gi, g in enumerate(group_keys) for k in g if k != "small"}
    group_of.update({k: 0 for k in SMALL_SHARDED})
    group_of.update({k: "replicated" for k in replicated})
    params = _Lazy(group_of, lambda g, after: replicated if g == "replicated" else finish_group(g, after))

    red = [("w_up1", "ffn_w_up", 1), ("w_down1", "ffn_w_down", 1), ("w_ao", "attn_w_out", 0), ("w_kv", "attn_w_kv", 0),
           ("w_q", "attn_w_q", 0), ("w_up0", "ffn_w_up", 0), ("w_down0", "ffn_w_down", 0), ("w_out", "s5_w_out", 0),
           ("w_glu", "s5_w_glu", 0)]
    n_early = 5

    def grad_halves(gw, key, src):
        r, c = w_in[src].shape[-2:]
        return gw[key].reshape(4, 2, r // 2, c)

    early = {}

    def on_layer1_grads(gw):
        ga = [grad_halves(gw, key, src) for key, src, _ in red[:n_early]]
        hf, hx = _pair_sums(ga, [True] * n_early, cidx, "a")
        early["hf"] = hf
        early["started"], tok = _chip_exchange_start(hx, name="rs_chip_exchange_start_a")
        return tok

    loss, grad_x, GW, GS = _local_step(x, loss_target, params, params, on_layer1_grads)
    loss = lax.psum(loss, ("x", "y", "c"))

    gs_shapes = [GS[k].shape for k in SMALL_ORDER]
    gs_pack = _pack([GS[k] for k in SMALL_ORDER], 128, 64)
    rs = gs_pack.shape[0] // 8
    gb = [grad_halves(GW, key, src) for key, src, _ in red[n_early:]] + [gs_pack.reshape(4, 2, rs, 128)]
    n_late = len(gb)
    hf_b, hx_b = _pair_sums(gb, [True] * (n_late - 1) + [False], cidx, "b")
    got_a = _chip_exchange_wait(early["started"], grad_x, name="rs_chip_exchange_wait_a")
    got_b = _chip_exchange(hx_b, name="rs_chip_exchange_b")
    mine = _chip_sums(early["hf"], got_a, chip_idx, "a") + _chip_sums(hf_b, got_b, chip_idx, "b")
    other = _pair_swap(mine, name="rs_pair_swap")
    small_halves = jnp.where(core == 0, jnp.concatenate([mine[-1], other[-1]]), jnp.concatenate([other[-1], mine[-1]]))
    small_all = _gather([small_halves], chip, name="small_grads_all_gather")[0]
    gsmall = dict(zip(SMALL_ORDER, _unpack(small_all, gs_shapes)))

    big_res = {}
    for (key, src, layer), gm, go in zip(red, mine[:-1], other[:-1]):
        nl = w_in[src].shape[0] if src in ("ffn_w_up", "ffn_w_down") else 1
        r, c = w_in[src].shape[-2:]
        view = lambda t: t.reshape(nl, 2, r // 2, c)
        res4 = _adamw_halves(view(w_in[src]), view(m_in[src]), view(v_in[src]), gm, go, cidx, layer=layer,
                             prev=big_res.get(src), name=f"adamw_{key}")
        big_res[src] = res4
    big_res = {src: tuple(t.reshape(w_in[src].shape) for t in res4) for src, res4 in big_res.items()}

    def big_out(i):
        return {src: big_res[src][i] for _, src, _ in big}

    small_w = {"lam_re": s5_lam_re, "lam_im": s5_lam_im, "log_dt": s5_log_dt, "b_re": s5_b_re, "b_im": s5_b_im,
               "c_re": s5_c_re, "c_im": s5_c_im, "d": s5_d, "rel_bias": rel_bias, "conv_b": ffn_conv_b,
               "b_glu": s5_b_glu, "conv_w": ffn_conv_w, "ln_gain": ln_gain, "ln_bias": ln_bias}
    small_name = {"lam_re": "s5_lam_re", "lam_im": "s5_lam_im", "log_dt": "s5_log_dt", "b_re": "s5_b_re", "b_im": "s5_b_im",
                  "c_re": "s5_c_re", "c_im": "s5_c_im", "d": "s5_d", "rel_bias": "rel_bias", "conv_b": "ffn_conv_b",
                  "b_glu": "s5_b_glu", "conv_w": "ffn_conv_w", "ln_gain": "ln_gain", "ln_bias": "ln_bias"}
    sg = {}
    for k in SMALL_ORDER:
        shp = small_w[k].shape
        g = gsmall[k]
        if k in SMALL_SHARDED:
            width = shp[-1]
            g = lax.dynamic_slice_in_dim(g, chip * width, width, axis=g.ndim - 1)
        sg[k] = g.reshape(shp)
    sshapes = [small_w[k].shape for k in SMALL_ORDER]
    pw = _pack([small_w[k] for k in SMALL_ORDER], 128, 512)
    pg = _pack([sg[k] for k in SMALL_ORDER], 128, 512)
    pm = _pack([m_in[small_name[k]] for k in SMALL_ORDER], 128, 512)
    pv = _pack([v_in[small_name[k]] for k in SMALL_ORDER], 128, 512)
    sd, snm, snv = _adamw(pw, pg, pm, pv, name="adamw_small")
    sd = dict(zip(SMALL_ORDER, _unpack(sd, sshapes)))
    snm = dict(zip(SMALL_ORDER, _unpack(snm, sshapes)))
    snv = dict(zip(SMALL_ORDER, _unpack(snv, sshapes)))

    res = [{}, {}, {}, {}]
    for i in range(4):
        res[i].update(big_out(i))
    for k in SMALL_ORDER:
        res[0][small_name[k]] = sg[k]
        res[1][small_name[k]] = sd[k]
        res[2][small_name[k]] = snm[k]
        res[3][small_name[k]] = snv[k]
    outs = [loss, grad_x]
    for i in range(4):
        outs += [res[i][n] for n in names]
    return tuple(outs)
```

```python
import functools
import math

import numpy as np
import jax
import jax.numpy as jnp
from jax import lax
from jax.experimental import pallas as pl
from jax.experimental.pallas import tpu as pltpu

F32 = jnp.float32
BF16 = jnp.bfloat16
MXU_DTYPE = jnp.bfloat16
V7X_VMEM_LIMIT_BYTES = 52 << 20
MESH = pl.DeviceIdType.MESH

DEPTH = 2
SSM_GROUP = 16
SSM_STATE = 64
GROUPS_PER_CLUSTER = 16
CLUSTER_W = GROUPS_PER_CLUSTER * SSM_GROUP
HEAD_DIM = 64
DILATIONS = (1, 4, 16)
BAND = 128
NEG_BIG = -1e30
REL_BUCKETS = 32
REL_MAX_DIST = 2048
DN_ALPHA = (2.0 * DEPTH) ** 0.25
LN_EPS = 1e-5
ADAM_LR, ADAM_B1, ADAM_B2, ADAM_EPS, ADAM_WD, ADAM_STEP = 0.001, 0.9, 0.999, 1e-08, 0.01, 10
GELU_K = math.sqrt(2.0 / math.pi)
GELU_C = 0.044715


def _pallas(body, **kw):
    return pl.pallas_call(body, **kw)


def _params(sem=None):
    return pltpu.CompilerParams(dimension_semantics=sem, vmem_limit_bytes=V7X_VMEM_LIMIT_BYTES)


def _pick(n, cands):
    for c in cands:
        if n % c == 0:
            return c
    return n


def _sigmoid(z):
    return 1.0 / (1.0 + jnp.exp(-z))


def _gelu(y):
    return 0.5 * y * (1.0 + jnp.tanh(GELU_K * (y + GELU_C * y * y * y)))


def _gelu_grad(y):
    t = jnp.tanh(GELU_K * (y + GELU_C * y * y * y))
    return 0.5 * (1.0 + t) + 0.5 * y * (1.0 - t * t) * (GELU_K * (1.0 + 3.0 * GELU_C * y * y))


def _mm_nn(a, w, *, l=0, bias=None, out_dtype=F32, name):
    T, K = a.shape
    P, _, _, Np = w.shape
    tm = _pick(T, (1024, 512, 256, 128))
    tn = _pick(Np, (1408, 1024, 768, 512, 384, 256, 128))
    nj = Np // tn

    def body(*refs):
        if bias is None:
            a_ref, w_ref, o_ref = refs
        else:
            a_ref, w_ref, b_ref, o_ref = refs
        acc = jnp.dot(a_ref[...].astype(MXU_DTYPE), w_ref[...].astype(MXU_DTYPE), preferred_element_type=F32)
        if bias is not None:
            acc = acc + b_ref[...]
        o_ref[...] = acc.astype(o_ref.dtype)

    in_specs = [pl.BlockSpec((tm, K), lambda p, j, i: (i, 0)),
                pl.BlockSpec((None, None, K, tn), lambda p, j, i: (p, l, 0, j))]
    args = [a, w]
    if bias is not None:
        in_specs.append(pl.BlockSpec((1, tn), lambda p, j, i: (0, p * nj + j)))
        args.append(bias)
    return _pallas(
        body, name=name, grid=(P, nj, T // tm), in_specs=in_specs,
        out_specs=pl.BlockSpec((tm, tn), lambda p, j, i: (i, p * nj + j)),
        out_shape=jax.ShapeDtypeStruct((T, P * Np), out_dtype),
        compiler_params=_params(("parallel", "parallel", "parallel")),
    )(*args)


def _mm_nt(a, w, *, l=0, p0=0, pn=None, out_dtype=F32, name):
    T = a.shape[0]
    _, _, K, Np = w.shape
    pn = w.shape[0] if pn is None else pn
    tm = _pick(T, (1024, 512, 256, 128) if K <= 1024 else (512, 256, 128))
    tn = _pick(Np, (1536, 1408, 1024, 768, 512, 384, 256, 128))
    nj = Np // tn
    nred = pn * nj

    def body(a_ref, w_ref, o_ref, acc):
        r = pl.program_id(1)

        @pl.when(r == 0)
        def _():
            acc[...] = jnp.zeros_like(acc)

        acc[...] += lax.dot_general(a_ref[...].astype(MXU_DTYPE), w_ref[...].astype(MXU_DTYPE),
                                    (((1,), (1,)), ((), ())), preferred_element_type=F32)

        @pl.when(r == nred - 1)
        def _():
            o_ref[...] = acc[...].astype(o_ref.dtype)

    return _pallas(
        body, name=name, grid=(T // tm, nred),
        in_specs=[pl.BlockSpec((tm, tn), lambda i, r: (i, r)),
                  pl.BlockSpec((None, None, K, tn), lambda i, r: (p0 + r // nj, l, 0, r % nj))],
        out_specs=pl.BlockSpec((tm, K), lambda i, r: (i, 0)),
        out_shape=jax.ShapeDtypeStruct((T, K), out_dtype),
        scratch_shapes=[pltpu.VMEM((tm, K), F32)],
        compiler_params=_params(("parallel", "arbitrary")),
    )(a, w)


def _tn(a, b, *, ptotal, np_cols, nl=1, l=0, p0=0, prev=None, name):
    T, K = a.shape
    Np = np_cols
    pn = b.shape[1] // Np
    tt = _pick(T, (1024, 512, 256, 128))
    tk = _pick(K, (1408, 1024, 512, 256, 128))
    tn = _pick(Np, (1408, 768, 512, 256, 128))
    if tk * tn > 1408 * 1024:
        tn = _pick(Np, (512, 256, 128))
    nj = Np // tn
    nt = T // tt

    def body(*refs):
        a_ref, b_ref = refs[0], refs[1]
        o_ref, acc = refs[-2], refs[-1]
        t = pl.program_id(3)

        @pl.when(t == 0)
        def _():
            acc[...] = jnp.zeros_like(acc)

        acc[...] += lax.dot_general(a_ref[...].astype(MXU_DTYPE), b_ref[...].astype(MXU_DTYPE),
                                    (((0,), (0,)), ((), ())), preferred_element_type=F32)

        @pl.when(t == nt - 1)
        def _():
            o_ref[...] = acc[...]

    in_specs = [pl.BlockSpec((tt, tk), lambda kb, p, j, t: (t, kb)),
                pl.BlockSpec((tt, tn), lambda kb, p, j, t: (t, p * nj + j))]
    args = [a, b]
    aliases = {}
    if prev is not None:
        in_specs.append(pl.BlockSpec(memory_space=pl.ANY))
        args.append(prev)
        aliases = {2: 0}
    return _pallas(
        body, name=name, grid=(K // tk, pn, nj, nt), in_specs=in_specs,
        out_specs=pl.BlockSpec((None, None, tk, tn), lambda kb, p, j, t: (p0 + p, l, kb, j)),
        out_shape=jax.ShapeDtypeStruct((ptotal, nl, K, Np), F32),
        scratch_shapes=[pltpu.VMEM((tk, tn), F32)],
        input_output_aliases=aliases,
        compiler_params=_params(("parallel", "parallel", "parallel", "arbitrary")),
    )(*args)


def _rows(tm, f):
    return pl.BlockSpec((tm, f), lambda i: (i, 0))


def _whole(shape):
    nd = len(shape)
    return pl.BlockSpec(shape, lambda i: (0,) * nd)


def _ln_fwd(xres, f, gain, bias, *, name):
    T, D = xres.shape
    tm = _pick(T, (256, 128))

    def body(x_ref, f_ref, g_ref, b_ref, y_ref, yb_ref, xh_ref, rs_ref):
        z = DN_ALPHA * x_ref[...] + f_ref[...]
        mu = jnp.mean(z, axis=-1, keepdims=True)
        zc = z - mu
        var = jnp.mean(zc * zc, axis=-1, keepdims=True)
        rstd = lax.rsqrt(var + LN_EPS)
        xh = zc * rstd
        y = xh * g_ref[...] + b_ref[...]
        y_ref[...] = y
        yb_ref[...] = y.astype(yb_ref.dtype)
        xh_ref[...] = xh
        rs_ref[...] = rstd

    return _pallas(
        body, name=name, grid=(T // tm,),
        in_specs=[_rows(tm, D), _rows(tm, D), _whole((1, D)), _whole((1, D))],
        out_specs=[_rows(tm, D), _rows(tm, D), _rows(tm, D), _rows(tm, 1)],
        out_shape=[jax.ShapeDtypeStruct((T, D), F32), jax.ShapeDtypeStruct((T, D), MXU_DTYPE),
                   jax.ShapeDtypeStruct((T, D), F32), jax.ShapeDtypeStruct((T, 1), F32)],
        compiler_params=_params(("parallel",)),
    )(xres, f, gain, bias)


def _ln_bwd(addends, coefs, xhat, rstd, gain, *, name):
    T, D = xhat.shape
    tm = _pick(T, (256, 128))
    n = len(addends)

    def body(*refs):
        adds = refs[:n]
        xh_ref, rs_ref, g_ref, dz_ref, dzb_ref, dg_ref, db_ref = refs[n:]
        dy = coefs[0] * adds[0][...]
        for c, r in zip(coefs[1:], adds[1:]):
            dy = dy + c * r[...]
        xh = xh_ref[...]
        dxh = dy * g_ref[...]
        m1 = jnp.mean(dxh, axis=-1, keepdims=True)
        m2 = jnp.mean(dxh * xh, axis=-1, keepdims=True)
        dz = rs_ref[...] * (dxh - m1 - xh * m2)
        dz_ref[...] = dz
        dzb_ref[...] = dz.astype(dzb_ref.dtype)

        @pl.when(pl.program_id(0) == 0)
        def _():
            dg_ref[...] = jnp.zeros_like(dg_ref)
            db_ref[...] = jnp.zeros_like(db_ref)

        dg_ref[...] += jnp.sum(dy * xh, axis=0, keepdims=True)
        db_ref[...] += jnp.sum(dy, axis=0, keepdims=True)

    return _pallas(
        body, name=name, grid=(T // tm,),
        in_specs=[_rows(tm, D)] * n + [_rows(tm, D), _rows(tm, 1), _whole((1, D))],
        out_specs=[_rows(tm, D), _rows(tm, D), _whole((1, D)), _whole((1, D))],
        out_shape=[jax.ShapeDtypeStruct((T, D), F32), jax.ShapeDtypeStruct((T, D), MXU_DTYPE),
                   jax.ShapeDtypeStruct((1, D), F32), jax.ShapeDtypeStruct((1, D), F32)],
        compiler_params=_params(("arbitrary",)),
    )(*addends, xhat, rstd, gain)


def _loss_grad(y, tgt, *, name):
    T, D = y.shape
    tm = _pick(T, (256, 128))

    def body(y_ref, t_ref, dy_ref, l_ref):
        e = y_ref[...] - t_ref[...]
        dy_ref[...] = e * (1.0 / D)

        @pl.when(pl.program_id(0) == 0)
        def _():
            l_ref[...] = jnp.zeros_like(l_ref)

        l_ref[...] += jnp.zeros_like(l_ref) + jnp.sum(e * e) * (0.5 / D)

    return _pallas(
        body, name=name, grid=(T // tm,),
        in_specs=[_rows(tm, D), _rows(tm, D)],
        out_specs=[_rows(tm, D), _whole((1, 128))],
        out_shape=[jax.ShapeDtypeStruct((T, D), F32), jax.ShapeDtypeStruct((1, 128), F32)],
        compiler_params=_params(("arbitrary",)),
    )(y, tgt)


def _axpy(a, b, ca, *, name):
    T, D = a.shape
    tm = _pick(T, (256, 128))

    def body(a_ref, b_ref, o_ref):
        o_ref[...] = ca * a_ref[...] + b_ref[...]

    return _pallas(
        body, name=name, grid=(T // tm,), in_specs=[_rows(tm, D), _rows(tm, D)], out_specs=_rows(tm, D),
        out_shape=jax.ShapeDtypeStruct((T, D), F32), compiler_params=_params(("parallel",)),
    )(a, b)


def _glu_gate(y, z, *, name):
    T, D = y.shape
    tm = _pick(T, (256, 128))

    def body(y_ref, z_ref, g_ref):
        g_ref[...] = (_gelu(y_ref[...]) * _sigmoid(z_ref[...])).astype(g_ref.dtype)

    return _pallas(
        body, name=name, grid=(T // tm,), in_specs=[_rows(tm, D), _rows(tm, D)], out_specs=_rows(tm, D),
        out_shape=jax.ShapeDtypeStruct((T, D), MXU_DTYPE), compiler_params=_params(("parallel",)),
    )(y, z)


def _glu_bwd(y, z, dg, *, name):
    T, D = y.shape
    tm = _pick(T, (256, 128))

    def body(y_ref, z_ref, dg_ref, dzb_ref, dyg_ref, db_ref):
        s = _sigmoid(z_ref[...])
        dg = dg_ref[...]
        dz = dg * _gelu(y_ref[...]) * s * (1.0 - s)
        dzb_ref[...] = dz.astype(dzb_ref.dtype)
        dyg_ref[...] = dg * s

        @pl.when(pl.program_id(0) == 0)
        def _():
            db_ref[...] = jnp.zeros_like(db_ref)

        db_ref[...] += jnp.sum(dz, axis=0, keepdims=True)

    return _pallas(
        body, name=name, grid=(T // tm,), in_specs=[_rows(tm, D)] * 3,
        out_specs=[_rows(tm, D), _rows(tm, D), _whole((1, D))],
        out_shape=[jax.ShapeDtypeStruct((T, D), MXU_DTYPE), jax.ShapeDtypeStruct((T, D), F32),
                   jax.ShapeDtypeStruct((1, D), F32)],
        compiler_params=_params(("arbitrary",)),
    )(y, z, dg)


def _gelu_bwd(y, d1, d2, *, name):
    T, D = y.shape
    tm = _pick(T, (256, 128))

    def body(y_ref, a_ref, b_ref, o_ref):
        o_ref[...] = (a_ref[...] + b_ref[...]) * _gelu_grad(y_ref[...])

    return _pallas(
        body, name=name, grid=(T // tm,), in_specs=[_rows(tm, D)] * 3, out_specs=_rows(tm, D),
        out_shape=jax.ShapeDtypeStruct((T, D), F32), compiler_params=_params(("parallel",)),
    )(y, d1, d2)


CONV_ROWS = 128
CONV_EDGE = 16


def _shift_back(x, edge, at_start, tm):
    rows = lax.broadcasted_iota(jnp.int32, x.shape, 0)
    keep = jnp.where(at_start, 0.0, 1.0)
    e7 = edge[CONV_EDGE - 1:CONV_EDGE, :] * keep
    e6 = edge[CONV_EDGE - 2:CONV_EDGE - 1, :] * keep
    r1 = pltpu.roll(x, 1, 0)
    r2 = pltpu.roll(x, 2, 0)
    x1 = jnp.where(rows == 0, e7, r1)
    x2 = jnp.where(rows == 0, e6, jnp.where(rows == 1, e7, r2))
    return x1, x2


def _conv_specs(T, F2, tm):
    return [_rows(tm, F2),
            pl.BlockSpec((CONV_EDGE, F2), lambda i: (jnp.maximum(i * (tm // CONV_EDGE) - 1, 0), 0))]


def _conv_glu_fwd(hc, conv_w, conv_b, L, *, name):
    T, F2 = hc.shape
    F = F2 // 2
    tm = CONV_ROWS

    def body(x_ref, e_ref, w_ref, b_ref, a_ref):
        at_start = (pl.program_id(0) * tm) % L == 0
        x = x_ref[...].astype(F32)
        x1, x2 = _shift_back(x, e_ref[...].astype(F32), at_start, tm)
        c = b_ref[...] + w_ref[0:1, :] * x + w_ref[1:2, :] * x1 + w_ref[2:3, :] * x2
        val, gate = c[:, :F], c[:, F:]
        a_ref[...] = (gate * _sigmoid(gate) * val).astype(a_ref.dtype)

    return _pallas(
        body, name=name, grid=(T // tm,),
        in_specs=_conv_specs(T, F2, tm) + [_whole((3, F2)), _whole((1, F2))],
        out_specs=_rows(tm, F),
        out_shape=jax.ShapeDtypeStruct((T, F), MXU_DTYPE), compiler_params=_params(("parallel",)),
    )(hc, hc, conv_w, conv_b)


def _conv_glu_bwd(hc, da, conv_w, conv_b, L, *, name):
    T, F2 = hc.shape
    F = F2 // 2
    tm = CONV_ROWS

    def body(x_ref, e_ref, da_ref, w_ref, b_ref, dc_ref, dw_ref, db_ref):
        at_start = (pl.program_id(0) * tm) % L == 0
        x = x_ref[...].astype(F32)
        x1, x2 = _shift_back(x, e_ref[...].astype(F32), at_start, tm)
        c = b_ref[...] + w_ref[0:1, :] * x + w_ref[1:2, :] * x1 + w_ref[2:3, :] * x2
        val, gate = c[:, :F], c[:, F:]
        s = _sigmoid(gate)
        da = da_ref[...].astype(F32)
        dval = da * (gate * s)
        dgate = da * val * (s * (1.0 + gate * (1.0 - s)))
        dc = jnp.concatenate([dval, dgate], axis=-1)
        dc_ref[...] = dc.astype(dc_ref.dtype)

        @pl.when(pl.program_id(0) == 0)
        def _():
            dw_ref[...] = jnp.zeros_like(dw_ref)
            db_ref[...] = jnp.zeros_like(db_ref)

        dw_ref[0:1, :] += jnp.sum(dc * x, axis=0, keepdims=True)
        dw_ref[1:2, :] += jnp.sum(dc * x1, axis=0, keepdims=True)
        dw_ref[2:3, :] += jnp.sum(dc * x2, axis=0, keepdims=True)
        db_ref[...] += jnp.sum(dc, axis=0, keepdims=True)

    return _pallas(
        body, name=name, grid=(T // tm,),
        in_specs=_conv_specs(T, F2, tm) + [_rows(tm, F), _whole((3, F2)), _whole((1, F2))],
        out_specs=[_rows(tm, F2), _whole((3, F2)), _whole((1, F2))],
        out_shape=[jax.ShapeDtypeStruct((T, F2), MXU_DTYPE), jax.ShapeDtypeStruct((3, F2), F32),
                   jax.ShapeDtypeStruct((1, F2), F32)],
        compiler_params=_params(("arbitrary",)),
    )(hc, hc, da, conv_w, conv_b)


def _conv_bwd_input(dc, conv_w, L, *, name):
    T, F2 = dc.shape
    tm = CONV_ROWS
    edge = CONV_EDGE
    last_blk = T // edge - 1

    def body(x_ref, e_ref, w_ref, o_ref):
        at_end = ((pl.program_id(0) + 1) * tm) % L == 0
        x = x_ref[...].astype(F32)
        rows = lax.broadcasted_iota(jnp.int32, x.shape, 0)
        keep = jnp.where(at_end, 0.0, 1.0)
        ev = e_ref[...].astype(F32)
        e0 = ev[0:1, :] * keep
        e1 = ev[1:2, :] * keep
        u1 = pltpu.roll(x, tm - 1, 0)
        u2 = pltpu.roll(x, tm - 2, 0)
        x1 = jnp.where(rows == tm - 1, e0, u1)
        x2 = jnp.where(rows == tm - 1, e1, jnp.where(rows == tm - 2, e0, u2))
        o_ref[...] = (w_ref[0:1, :] * x + w_ref[1:2, :] * x1 + w_ref[2:3, :] * x2).astype(o_ref.dtype)

    return _pallas(
        body, name=name, grid=(T // tm,),
        in_specs=[_rows(tm, F2),
                  pl.BlockSpec((edge, F2), lambda i: (jnp.minimum((i + 1) * (tm // edge), last_blk), 0)),
                  _whole((3, F2))],
        out_specs=_rows(tm, F2),
        out_shape=jax.ShapeDtypeStruct((T, F2), MXU_DTYPE), compiler_params=_params(("parallel",)),
    )(dc, dc, conv_w)


S5_CHUNK = 128
LANES = 128


def _slab_rows(c, n, ncl):
    return pl.ds(c, n) if ncl == 1 else pl.ds(c, n, stride=ncl)


def _slab_put(ref, c, n, ncl, val):
    for s in range(val.shape[1] // LANES):
        ref[s, _slab_rows(c, n, ncl), :] = val[:, s * LANES:(s + 1) * LANES]


def _slab_get(ref, c, n, ncl):
    return jnp.concatenate([ref[s, _slab_rows(c, n, ncl), :] for s in range(ref.shape[0])], axis=-1)


def _slabs(n_slab, rows):
    return pl.BlockSpec((n_slab, rows, LANES), lambda i: (0, i, 0))


def _s5_fwd(xi, wb, wc, a_r, a_i, d_row, B, *, name):
    T, D = xi.shape
    ncl = wb.shape[0]
    cs = wb.shape[2] // 2
    ns = cs // LANES
    R = B * ncl
    Q = S5_CHUNK
    QR = Q * ncl
    nsteps = Q // B

    def body(x_ref, wb_ref, wc_ref, ar_ref, ai_ref, d_ref, y_ref, yg_ref, hr_ref, hi_ref, bur, bui, cr, ci):
        @pl.when(pl.program_id(0) == 0)
        def _():
            cr[...] = jnp.zeros_like(cr)
            ci[...] = jnp.zeros_like(ci)

        x = x_ref[...]
        xb = x.astype(MXU_DTYPE)
        for c in range(ncl):
            bu = jnp.dot(xb[:, c * CLUSTER_W:(c + 1) * CLUSTER_W], wb_ref[c], preferred_element_type=F32)
            _slab_put(bur, c, Q, ncl, bu[:, :cs])
            _slab_put(bui, c, Q, ncl, bu[:, cs:])
        ar = ar_ref[...]
        ai = ai_ref[...]

        def step(k, carry):
            hr, hi = carry
            sl = pl.ds(pl.multiple_of(k * R, R), R)
            nr = ar * hr - ai * hi + bur[:, sl, :]
            ni = ar * hi + ai * hr + bui[:, sl, :]
            hr_ref[:, sl, :] = nr
            hi_ref[:, sl, :] = ni
            return nr, ni

        hr, hi = lax.fori_loop(0, nsteps, step, (cr[...], ci[...]), unroll=4)
        cr[...] = hr
        ci[...] = hi
        parts = []
        for c in range(ncl):
            hrc = _slab_get(hr_ref, c, Q, ncl).astype(MXU_DTYPE)
            hic = _slab_get(hi_ref, c, Q, ncl).astype(MXU_DTYPE)
            parts.append(jnp.dot(hrc, wc_ref[c, :cs, :], preferred_element_type=F32)
                         + jnp.dot(hic, wc_ref[c, cs:, :], preferred_element_type=F32))
        y = d_ref[...] * x + (parts[0] if ncl == 1 else jnp.concatenate(parts, axis=-1))
        y_ref[...] = y
        yg_ref[...] = _gelu(y).astype(yg_ref.dtype)

    return _pallas(
        body, name=name, grid=(T // Q,),
        in_specs=[_rows(Q, D), _whole(wb.shape), _whole(wc.shape), _whole((ns, R, LANES)), _whole((ns, R, LANES)),
                  _whole((1, D))],
        out_specs=[_rows(Q, D), _rows(Q, D), _slabs(ns, QR), _slabs(ns, QR)],
        out_shape=[jax.ShapeDtypeStruct((T, D), F32), jax.ShapeDtypeStruct((T, D), MXU_DTYPE),
                   jax.ShapeDtypeStruct((ns, T * ncl, LANES), F32), jax.ShapeDtypeStruct((ns, T * ncl, LANES), F32)],
        scratch_shapes=[pltpu.VMEM((ns, QR, LANES), F32), pltpu.VMEM((ns, QR, LANES), F32),
                        pltpu.VMEM((ns, R, LANES), F32), pltpu.VMEM((ns, R, LANES), F32)],
        compiler_params=_params(("arbitrary",)),
    )(xi, wb, wc, a_r, a_i, d_row)


def _s5_bwd(dy, xi, h_r, h_i, wb, wc, a_r, a_i, d_row, B, *, name):
    T, D = dy.shape
    ncl = wb.shape[0]
    cs = wb.shape[2] // 2
    ns = cs // LANES
    R = B * ncl
    Q = S5_CHUNK
    nsteps = Q // B
    nchunk = T // Q
    QR = Q * ncl

    def rev(i):
        return nchunk - 1 - i

    def body(dy_ref, x_ref, hr_ref, hi_ref, pr_ref, pi_ref, wb_ref, wc_ref, ar_ref, ai_ref, d_ref,
             du_ref, gr_ref, gi_ref, dar_ref, dai_ref, dd_ref, dhr, dhi, cr, ci):
        i = pl.program_id(0)

        @pl.when(i == 0)
        def _():
            cr[...] = jnp.zeros_like(cr)
            ci[...] = jnp.zeros_like(ci)
            dar_ref[...] = jnp.zeros_like(dar_ref)
            dai_ref[...] = jnp.zeros_like(dai_ref)
            dd_ref[...] = jnp.zeros_like(dd_ref)

        dyv = dy_ref[...]
        dyb = dyv.astype(MXU_DTYPE)
        for c in range(ncl):
            dh = lax.dot_general(dyb[:, c * CLUSTER_W:(c + 1) * CLUSTER_W], wc_ref[c],
                                 (((1,), (1,)), ((), ())), preferred_element_type=F32)
            _slab_put(dhr, c, Q, ncl, dh[:, :cs])
            _slab_put(dhi, c, Q, ncl, dh[:, cs:])
        ar = ar_ref[...]
        ai = ai_ref[...]

        def step(j, carry):
            gr, gi, sar, sai = carry
            k = nsteps - 1 - j
            sl = pl.ds(pl.multiple_of(k * R, R), R)
            ngr = dhr[:, sl, :] + ar * gr + ai * gi
            ngi = dhi[:, sl, :] - ai * gr + ar * gi
            gr_ref[:, sl, :] = ngr
            gi_ref[:, sl, :] = ngi
            pv = pl.ds(pl.multiple_of((k - 1) * R, R), R)
            hpr = hr_ref[:, pv, :]
            hpi = hi_ref[:, pv, :]
            return ngr, ngi, sar + ngr * hpr + ngi * hpi, sai - ngr * hpi + ngi * hpr

        gr, gi, sar, sai = lax.fori_loop(0, nsteps - 1, step, (cr[...], ci[...], dar_ref[...], dai_ref[...]), unroll=4)
        sl0 = pl.ds(0, R)
        ngr = dhr[:, sl0, :] + ar * gr + ai * gi
        ngi = dhi[:, sl0, :] - ai * gr + ar * gi
        gr_ref[:, sl0, :] = ngr
        gi_ref[:, sl0, :] = ngi
        keep = jnp.where(i == nchunk - 1, 0.0, 1.0)
        hpr = pr_ref[:, 8 - R:8, :] * keep
        hpi = pi_ref[:, 8 - R:8, :] * keep
        dar_ref[...] = sar + ngr * hpr + ngi * hpi
        dai_ref[...] = sai - ngr * hpi + ngi * hpr
        cr[...] = ngr
        ci[...] = ngi
        parts = []
        for c in range(ncl):
            grc = _slab_get(gr_ref, c, Q, ncl).astype(MXU_DTYPE)
            gic = _slab_get(gi_ref, c, Q, ncl).astype(MXU_DTYPE)
            parts.append(lax.dot_general(grc, wb_ref[c, :, :cs], (((1,), (1,)), ((), ())), preferred_element_type=F32)
                         + lax.dot_general(gic, wb_ref[c, :, cs:], (((1,), (1,)), ((), ())), preferred_element_type=F32))
        du_ref[...] = d_ref[...] * dyv + (parts[0] if ncl == 1 else jnp.concatenate(parts, axis=-1))
        dd_ref[...] += jnp.sum(dyv * x_ref[...], axis=0, keepdims=True)

    tok = pl.BlockSpec((Q, D), lambda i: (rev(i), 0))
    st = pl.BlockSpec((ns, QR, LANES), lambda i: (0, rev(i), 0))
    before = pl.BlockSpec((ns, 8, LANES), lambda i: (0, jnp.maximum(rev(i) * (QR // 8) - 1, 0), 0))
    acc = _whole((ns, R, LANES))
    return _pallas(
        body, name=name, grid=(nchunk,),
        in_specs=[tok, tok, st, st, before, before, _whole(wb.shape), _whole(wc.shape), acc, acc, _whole((1, D))],
        out_specs=[tok, st, st, acc, acc, _whole((1, D))],
        out_shape=[jax.ShapeDtypeStruct((T, D), F32),
                   jax.ShapeDtypeStruct((ns, T * ncl, LANES), F32), jax.ShapeDtypeStruct((ns, T * ncl, LANES), F32),
                   jax.ShapeDtypeStruct((ns, R, LANES), F32), jax.ShapeDtypeStruct((ns, R, LANES), F32),
                   jax.ShapeDtypeStruct((1, D), F32)],
        scratch_shapes=[pltpu.VMEM((ns, QR, LANES), F32)] * 2 + [pltpu.VMEM((ns, R, LANES), F32)] * 2,
        compiler_params=_params(("arbitrary",)),
    )(dy, xi, h_r, h_i, h_r, h_i, wb, wc, a_r, a_i, d_row)


def _cluster_tn(tok, st, ncl, *, tok_left, name):
    T = tok.shape[0]
    ns = st.shape[0]
    cs = ns * LANES
    tt = _pick(T, (512, 256, 128))
    nt = T // tt
    oshape = (ncl, CLUSTER_W, cs) if tok_left else (ncl, cs, CLUSTER_W)

    def body(tok_ref, st_ref, o_ref, acc):
        t = pl.program_id(0)

        @pl.when(t == 0)
        def _():
            acc[...] = jnp.zeros_like(acc)

        tk = tok_ref[...].astype(MXU_DTYPE)
        for c in range(ncl):
            tc = tk[:, c * CLUSTER_W:(c + 1) * CLUSTER_W]
            sc = _slab_get(st_ref, c, tt, ncl).astype(MXU_DTYPE)
            lhs, rhs = (tc, sc) if tok_left else (sc, tc)
            acc[c] += lax.dot_general(lhs, rhs, (((0,), (0,)), ((), ())), preferred_element_type=F32)

        @pl.when(t == nt - 1)
        def _():
            o_ref[...] = acc[...]

    return _pallas(
        body, name=name, grid=(nt,),
        in_specs=[_rows(tt, tok.shape[1]), _slabs(ns, tt * ncl)],
        out_specs=_whole(oshape),
        out_shape=jax.ShapeDtypeStruct(oshape, F32),
        scratch_shapes=[pltpu.VMEM(oshape, F32)],
        compiler_params=_params(("arbitrary",)),
    )(tok, st)


def _s5_discretize(lam_re, lam_im, log_dt, b_re, b_im):
    dt = jnp.exp(log_dt)[:, None]
    mag = jnp.exp(lam_re * dt)
    ab_r, ab_i = mag * jnp.cos(lam_im * dt), mag * jnp.sin(lam_im * dt)
    den = lam_re * lam_re + lam_im * lam_im
    nr = ab_r - 1.0
    co_r = (nr * lam_re + ab_i * lam_im) / den
    co_i = (ab_i * lam_re - nr * lam_im) / den
    bb_r = co_r[..., None] * b_re - co_i[..., None] * b_im
    bb_i = co_r[..., None] * b_im + co_i[..., None] * b_re
    return ab_r, ab_i, bb_r, bb_i


def _blockdiag(m):
    G, r, k = m.shape
    ncl = G // GROUPS_PER_CLUSTER
    m4 = m.reshape(ncl, GROUPS_PER_CLUSTER, r, k)
    eye = jnp.eye(GROUPS_PER_CLUSTER, dtype=m.dtype)
    return jnp.einsum('cgrk,gh->cgrhk', m4, eye).reshape(ncl, GROUPS_PER_CLUSTER * r, GROUPS_PER_CLUSTER * k)


def _unblockdiag(m, r, k):
    ncl = m.shape[0]
    m5 = m.reshape(ncl, GROUPS_PER_CLUSTER, r, GROUPS_PER_CLUSTER, k)
    eye = jnp.eye(GROUPS_PER_CLUSTER, dtype=m.dtype)
    return jnp.einsum('cgrhk,gh->cgrk', m5, eye).reshape(ncl * GROUPS_PER_CLUSTER, r, k)


def _t5_bucket(dist):
    exact = REL_BUCKETS // 2
    d = np.maximum(dist, 1).astype(np.float32)
    large = exact + (np.log(d / exact) / math.log(REL_MAX_DIST / exact) * (REL_BUCKETS - exact)).astype(np.int64)
    large = np.minimum(large, REL_BUCKETS - 1)
    return np.where(dist < exact, dist, large).astype(np.int32)


def _band_tables(dil):
    steps = np.arange(BAND)[:, None] + BAND - np.arange(2 * BAND)[None, :]
    bucket = _t5_bucket(np.maximum(steps, 0) * dil)
    in_band = (steps >= 0) & (steps <= BAND)
    return bucket, in_band


def _attn_bias(rel_bias, hpg):
    out = []
    for g, dil in enumerate(DILATIONS):
        bucket, in_band = _band_tables(dil)
        cols = rel_bias[:, g * hpg:(g + 1) * hpg].astype(F32)
        onehot = jnp.asarray((bucket.reshape(-1, 1) == np.arange(REL_BUCKETS)[None, :]).astype(np.float32))
        bias = jnp.dot(onehot, cols, precision=lax.Precision.HIGHEST).T.reshape(hpg, BAND, 2 * BAND)
        out.append(jnp.where(jnp.asarray(in_band)[None], bias, NEG_BIG))
    return jnp.concatenate(out, axis=0)


def _attn_blocks(dil, L):
    M = L // dil
    return M, M // BAND


def _row_sel(r, M, dil):
    return pl.ds(r, M) if dil == 1 else pl.ds(r, M, stride=dil)


def _attn_fwd(q, kv, bias, L, hpg, *, name):
    T = q.shape[0]
    nb_ = T // L
    HP = hpg // 2
    W3 = 3 * hpg * HEAD_DIM
    mmax = L

    def group_body(dil, q_ref, k_ref, v_ref, b_ref, o_ref, l_ref, os, ls):
        M, NB = _attn_blocks(dil, L)
        for r in range(dil):
            rows = _row_sel(r, M, dil)
            first = lax.broadcasted_iota(jnp.int32, (1, 2 * HEAD_DIM), 1) < HEAD_DIM
            qf = q_ref[rows, :] * 0.125
            qm = [jnp.where(first, qf, 0.0).astype(MXU_DTYPE), jnp.where(first, 0.0, qf).astype(MXU_DTYPE)]
            kr = k_ref[rows, :].astype(MXU_DTYPE)
            va = jnp.concatenate([v_ref[rows, :].astype(MXU_DTYPE), jnp.ones((M, 2 * HEAD_DIM), MXU_DTYPE)], axis=-1)
            for n in range(NB):
                qs = slice(n * BAND, (n + 1) * BAND)
                ks = slice(0, BAND) if n == 0 else slice((n - 1) * BAND, (n + 1) * BAND)
                o_h, l_h = [], []
                for hh in range(2):
                    bb = b_ref[hh, :, BAND:] if n == 0 else b_ref[hh]
                    s = lax.dot_general(qm[hh][qs, :], kr[ks, :], (((1,), (1,)), ((), ())),
                                        preferred_element_type=F32) + bb
                    m = jnp.max(s, axis=-1, keepdims=True)
                    p = jnp.exp(s - m)
                    pv = jnp.dot(p.astype(MXU_DTYPE), va[ks, :], preferred_element_type=F32)
                    l = pv[:, 2 * HEAD_DIM:]
                    o_h.append(pv[:, :2 * HEAD_DIM] / l)
                    l_h.append(m + jnp.log(l))
                os[qs, :] = jnp.where(first, o_h[0], o_h[1])
                ls[qs, :] = jnp.where(first, l_h[0], l_h[1])
            o_ref[rows, :] = os[0:M, :]
            l_ref[rows, :] = ls[0:M, :]

    def body(q_ref, k_ref, v_ref, b_ref, o_ref, l_ref, os, ls):
        g = pl.program_id(0)
        for gi, dil in enumerate(DILATIONS):
            pl.when(g == gi)(functools.partial(group_body, dil, q_ref, k_ref, v_ref, b_ref, o_ref, l_ref, os, ls))

    blk = (L, 2 * HEAD_DIM)
    return _pallas(
        body, name=name, grid=(3, nb_, HP),
        in_specs=[pl.BlockSpec(blk, lambda g, b, h: (b, g * HP + h)),
                  pl.BlockSpec(blk, lambda g, b, h: (b, g * HP + h)),
                  pl.BlockSpec(blk, lambda g, b, h: (b, 3 * HP + g * HP + h)),
                  pl.BlockSpec((2, BAND, 2 * BAND), lambda g, b, h: (g * HP + h, 0, 0))],
        out_specs=[pl.BlockSpec(blk, lambda g, b, h: (b, g * HP + h)),
                   pl.BlockSpec(blk, lambda g, b, h: (b, g * HP + h))],
        out_shape=[jax.ShapeDtypeStruct((T, W3), F32), jax.ShapeDtypeStruct((T, W3), F32)],
        scratch_shapes=[pltpu.VMEM((mmax, 2 * HEAD_DIM), F32), pltpu.VMEM((mmax, 2 * HEAD_DIM), F32)],
        compiler_params=_params(("arbitrary", "arbitrary", "arbitrary")),
    )(q, kv, kv, bias)


def _attn_merge(o3, l3, hw, *, name):
    T = o3.shape[0]
    tm = _pick(T, (256, 128))

    def body(o0, o1, o2, l0, l1, l2, o_ref, ob_ref, lse_ref):
        a0, a1, a2 = l0[...], l1[...], l2[...]
        m = jnp.maximum(jnp.maximum(a0, a1), a2)
        e0, e1, e2 = jnp.exp(a0 - m), jnp.exp(a1 - m), jnp.exp(a2 - m)
        z = e0 + e1 + e2
        o = (e0 * o0[...] + e1 * o1[...] + e2 * o2[...]) / z
        o_ref[...] = o
        ob_ref[...] = o.astype(ob_ref.dtype)
        lse_ref[...] = m + jnp.log(z)

    def col(g):
        return pl.BlockSpec((tm, hw), lambda i: (i, g))

    return _pallas(
        body, name=name, grid=(T // tm,),
        in_specs=[col(0), col(1), col(2), col(0), col(1), col(2)],
        out_specs=[_rows(tm, hw)] * 3,
        out_shape=[jax.ShapeDtypeStruct((T, hw), F32), jax.ShapeDtypeStruct((T, hw), MXU_DTYPE),
                   jax.ShapeDtypeStruct((T, hw), F32)],
        compiler_params=_params(("parallel",)),
    )(o3, o3, o3, l3, l3, l3)


def _attn_bwd(q, kv, do, o, lse, bias, L, hpg, *, name):
    T = q.shape[0]
    nb_ = T // L
    HP = hpg // 2
    W3 = 3 * hpg * HEAD_DIM
    mmax = L

    def group_body(dil, q_ref, k_ref, v_ref, do_ref, o_ref, l_ref, b_ref, dq_ref, dk_ref, dv_ref, ds_ref,
                   dqs, dks, dvs):
        M, NB = _attn_blocks(dil, L)
        for r in range(dil):
            rows = _row_sel(r, M, dil)
            first = lax.broadcasted_iota(jnp.int32, (1, 2 * HEAD_DIM), 1) < HEAD_DIM
            qf = q_ref[rows, :] * 0.125
            qm = [jnp.where(first, qf, 0.0).astype(MXU_DTYPE), jnp.where(first, 0.0, qf).astype(MXU_DTYPE)]
            kr = k_ref[rows, :].astype(MXU_DTYPE)
            vr = v_ref[rows, :].astype(MXU_DTYPE)
            dof = do_ref[rows, :]
            dom = [jnp.where(first, dof, 0.0).astype(MXU_DTYPE), jnp.where(first, 0.0, dof).astype(MXU_DTYPE)]
            dod = dof * o_ref[rows, :]
            delta = [jnp.sum(jnp.where(first, dod, 0.0), axis=-1, keepdims=True),
                     jnp.sum(jnp.where(first, 0.0, dod), axis=-1, keepdims=True)]
            lr = l_ref[rows, :]
            lse = [lr[:, 0:1], lr[:, HEAD_DIM:HEAD_DIM + 1]]
            dks[0:M, :] = jnp.zeros((M, 2 * HEAD_DIM), F32)
            dvs[0:M, :] = jnp.zeros((M, 2 * HEAD_DIM), F32)
            for n in range(NB):
                qs = slice(n * BAND, (n + 1) * BAND)
                ks = slice(0, BAND) if n == 0 else slice((n - 1) * BAND, (n + 1) * BAND)
                dq_h = []
                dkc = dvc = None
                for hh in range(2):
                    bb = b_ref[hh, :, BAND:] if n == 0 else b_ref[hh]
                    qb, dob = qm[hh][qs, :], dom[hh][qs, :]
                    s = lax.dot_general(qb, kr[ks, :], (((1,), (1,)), ((), ())), preferred_element_type=F32) + bb
                    p = jnp.exp(s - lse[hh][qs, :])
                    dp = lax.dot_general(dob, vr[ks, :], (((1,), (1,)), ((), ())), preferred_element_type=F32)
                    ds = p * (dp - delta[hh][qs, :])
                    if n == 0:
                        ds_ref[hh, :, BAND:] += ds
                    else:
                        ds_ref[hh] += ds
                    dsm = ds.astype(MXU_DTYPE)
                    dq_h.append(jnp.dot(dsm, kr[ks, :], preferred_element_type=F32))
                    dk1 = lax.dot_general(dsm, qb, (((0,), (0,)), ((), ())), preferred_element_type=F32)
                    dv1 = lax.dot_general(p.astype(MXU_DTYPE), dob, (((0,), (0,)), ((), ())), preferred_element_type=F32)
                    dkc = dk1 if dkc is None else dkc + dk1
                    dvc = dv1 if dvc is None else dvc + dv1
                dqs[qs, :] = jnp.where(first, dq_h[0], dq_h[1]) * 0.125
                dks[ks, :] += dkc
                dvs[ks, :] += dvc
            dq_ref[rows, :] = dqs[0:M, :]
            dk_ref[rows, :] = dks[0:M, :]
            dv_ref[rows, :] = dvs[0:M, :]

    def body(q_ref, k_ref, v_ref, do_ref, o_ref, l_ref, b_ref, dq_ref, dk_ref, dv_ref, ds_ref, dqs, dks, dvs):
        g = pl.program_id(0)

        @pl.when(pl.program_id(2) == 0)
        def _():
            ds_ref[...] = jnp.zeros_like(ds_ref)

        for gi, dil in enumerate(DILATIONS):
            pl.when(g == gi)(functools.partial(group_body, dil, q_ref, k_ref, v_ref, do_ref, o_ref, l_ref, b_ref,
                                               dq_ref, dk_ref, dv_ref, ds_ref, dqs, dks, dvs))

    blk = (L, 2 * HEAD_DIM)
    gcol = lambda g, h, b: (b, g * HP + h)
    hcol = lambda g, h, b: (b, h)
    return _pallas(
        body, name=name, grid=(3, HP, nb_),
        in_specs=[pl.BlockSpec(blk, gcol), pl.BlockSpec(blk, gcol),
                  pl.BlockSpec(blk, lambda g, h, b: (b, 3 * HP + g * HP + h)),
                  pl.BlockSpec(blk, hcol), pl.BlockSpec(blk, hcol), pl.BlockSpec(blk, hcol),
                  pl.BlockSpec((2, BAND, 2 * BAND), lambda g, h, b: (g * HP + h, 0, 0))],
        out_specs=[pl.BlockSpec(blk, gcol), pl.BlockSpec(blk, gcol), pl.BlockSpec(blk, gcol),
                   pl.BlockSpec((2, BAND, 2 * BAND), lambda g, h, b: (g * HP + h, 0, 0))],
        out_shape=[jax.ShapeDtypeStruct((T, W3), F32), jax.ShapeDtypeStruct((T, W3), F32),
                   jax.ShapeDtypeStruct((T, W3), F32), jax.ShapeDtypeStruct((3 * hpg, BAND, 2 * BAND), F32)],
        scratch_shapes=[pltpu.VMEM((mmax, 2 * HEAD_DIM), F32)] * 3,
        compiler_params=_params(("arbitrary", "arbitrary", "arbitrary")),
    )(q, kv, kv, do, o, lse, bias)


def _bias_grad(ds_sum, hpg, *, name):
    nh = ds_sum.shape[0]
    idx = np.stack([np.where(_band_tables(dil)[1], _band_tables(dil)[0], -1) for dil in DILATIONS]).astype(np.int32)

    def body(ds_ref, idx_ref, o_ref):
        d = ds_ref[...]
        ix = idx_ref[...]
        lane = lax.broadcasted_iota(jnp.int32, (8, 128), 1)
        row = jnp.zeros((8, 128), F32)
        for b in range(REL_BUCKETS):
            row = row + jnp.where(lane == b, jnp.sum(jnp.where(ix == b, d, 0.0)), 0.0)
        o_ref[...] = row

    out = _pallas(
        body, name=name, grid=(nh,),
        in_specs=[pl.BlockSpec((None, BAND, 2 * BAND), lambda h: (h, 0, 0)),
                  pl.BlockSpec((None, BAND, 2 * BAND), lambda h: (h // hpg, 0, 0))],
        out_specs=pl.BlockSpec((None, 8, 128), lambda h: (h, 0, 0)),
        out_shape=jax.ShapeDtypeStruct((nh, 8, 128), F32),
        compiler_params=_params(("parallel",)),
    )(ds_sum, jnp.asarray(idx))
    return out[:, 0, :REL_BUCKETS].T


def _adamw(w, g, m, v, *, name):
    Rw, C = w.shape
    tm = _pick(Rw, (512, 352, 256, 128, 64, 32, 16, 8))

    def body(w_ref, g_ref, m_ref, v_ref, d_ref, nm_ref, nv_ref):
        gg = g_ref[...]
        nm = ADAM_B1 * m_ref[...] + (1.0 - ADAM_B1) * gg
        nv = ADAM_B2 * v_ref[...] + (1.0 - ADAM_B2) * (gg * gg)
        m_hat = nm / (1.0 - ADAM_B1 ** ADAM_STEP)
        v_hat = nv / (1.0 - ADAM_B2 ** ADAM_STEP)
        d_ref[...] = -ADAM_LR * (m_hat / (jnp.sqrt(v_hat) + ADAM_EPS) + ADAM_WD * w_ref[...])
        nm_ref[...] = nm
        nv_ref[...] = nv

    return _pallas(
        body, name=name, grid=(Rw // tm,), in_specs=[_rows(tm, C)] * 4, out_specs=[_rows(tm, C)] * 3,
        out_shape=[jax.ShapeDtypeStruct((Rw, C), F32)] * 3, compiler_params=_params(("parallel",)),
    )(w, g, m, v)


ROW_TILE_ELEMS = 256 * 1024


def _tile_rows(r, c):
    best = 8
    for t in range(8, r + 1, 8):
        if r % t == 0 and t * c <= ROW_TILE_ELEMS:
            best = t
    return best


def _adamw_halves(w, m, v, mine, other, cidx, *, layer=0, prev=None, name):
    NL, _, r, c = w.shape
    tm = _tile_rows(r, c)

    def body(c_ref, w_ref, m_ref, v_ref, a_ref, b_ref, *rest):
        g_ref, d_ref, nm_ref, nv_ref = rest[-4:]
        gg = jnp.where(pl.program_id(0) == c_ref[0], a_ref[...], b_ref[...])
        nm = ADAM_B1 * m_ref[...] + (1.0 - ADAM_B1) * gg
        nv = ADAM_B2 * v_ref[...] + (1.0 - ADAM_B2) * (gg * gg)
        m_hat = nm / (1.0 - ADAM_B1 ** ADAM_STEP)
        v_hat = nv / (1.0 - ADAM_B2 ** ADAM_STEP)
        g_ref[...] = gg
        d_ref[...] = -ADAM_LR * (m_hat / (jnp.sqrt(v_hat) + ADAM_EPS) + ADAM_WD * w_ref[...])
        nm_ref[...] = nm
        nv_ref[...] = nv

    half = pl.BlockSpec((None, None, tm, c), lambda h, i, cr: (layer, h, i, 0))
    one = pl.BlockSpec((None, tm, c), lambda h, i, cr: (0, i, 0))
    in_specs = [half, half, half, one, one]
    args = [cidx, w, m, v, mine, other]
    aliases = {}
    if prev is not None:
        in_specs += [_ANY] * 4
        args += list(prev)
        aliases = {6 + k: k for k in range(4)}
    spec = pltpu.PrefetchScalarGridSpec(num_scalar_prefetch=1, grid=(2, r // tm), in_specs=in_specs, out_specs=[half] * 4)
    return _pallas(
        body, name=name, grid_spec=spec, out_shape=[jax.ShapeDtypeStruct((NL, 2, r, c), F32)] * 4,
        input_output_aliases=aliases, compiler_params=_params(("parallel", "parallel")),
    )(*args)


def _pair_sum(g, theirs, cidx, *, cast, name):
    _, _, r, c = g.shape
    tm = _tile_rows(r, c)

    def body(c_ref, g_ref, t_ref, *outs):
        s = g_ref[...] + t_ref[...]
        outs[0][...] = s
        if cast:
            outs[1][...] = s.astype(BF16)

    blk = (None, None, tm, c)
    first = pl.BlockSpec(blk, lambda p, i, cr: (p, 0, i, 0))
    shapes = [jax.ShapeDtypeStruct((4, 1, r, c), F32)] + ([jax.ShapeDtypeStruct((4, 1, r, c), BF16)] if cast else [])
    spec = pltpu.PrefetchScalarGridSpec(
        num_scalar_prefetch=1, grid=(4, r // tm),
        in_specs=[pl.BlockSpec(blk, lambda p, i, cr: (p, cr[0], i, 0)), first], out_specs=[first] * len(shapes))
    return _pallas(body, name=name, grid_spec=spec, out_shape=shapes,
                   compiler_params=_params(("parallel", "parallel")))(cidx, g, theirs)


def _chip_sum(hf, got, chip_idx, *, name):
    _, _, r, c = hf.shape
    tm = _tile_rows(r, c)

    def body(p_ref, h_ref, r_ref, o_ref):
        s = h_ref[...]
        for k in range(3):
            s = s + r_ref[k].astype(F32)
        o_ref[...] = s

    spec = pltpu.PrefetchScalarGridSpec(
        num_scalar_prefetch=1, grid=(r // tm,),
        in_specs=[pl.BlockSpec((None, None, tm, c), lambda i, pr: (pr[0], 0, i, 0)),
                  pl.BlockSpec((3, None, tm, c), lambda i, pr: (0, 0, i, 0))],
        out_specs=pl.BlockSpec((None, tm, c), lambda i, pr: (0, i, 0)))
    return _pallas(body, name=name, grid_spec=spec, out_shape=jax.ShapeDtypeStruct((1, r, c), F32),
                   compiler_params=_params(("parallel",)))(chip_idx, hf, got)


def _place():
    x, y, c = lax.axis_index("x"), lax.axis_index("y"), lax.axis_index("c")
    chips = [(1 - x, y), (x, 1 - y), (1 - x, 1 - y)]
    return x, y, c, chips


_ANY = pl.BlockSpec(memory_space=pl.ANY)


def _comm_call(body, ins, out_shapes, n_remote, *, name, aliases=None):
    sems = [pltpu.SemaphoreType.DMA((n,)) for n in n_remote]
    return _pallas(
        body, name=name, in_specs=[_ANY] * len(ins), out_specs=[_ANY] * len(out_shapes), out_shape=out_shapes,
        scratch_shapes=sems, input_output_aliases=aliases or {},
        compiler_params=pltpu.CompilerParams(has_side_effects=True),
    )(*ins)


_HBM_SPEC = pl.BlockSpec(memory_space=pltpu.HBM)
_SEM_SPEC = pl.BlockSpec(memory_space=pltpu.SEMAPHORE)
_DATAFLOW = pltpu.SideEffectType.DATAFLOW_SIDE_EFFECTING


def _in_hbm(a):
    return pltpu.with_memory_space_constraint(a, pltpu.HBM)


def _gather_start(groups, *, name):
    flat = [s for g in groups for s in g]
    n, ng = len(flat), len(groups)

    def body(*refs):
        ins, lands = refs[:n], refs[n:2 * n]
        sems = refs[2 * n:2 * n + 2 * ng]
        token = refs[-1]
        x, y, c, chips = _place()
        me = 2 * x + y
        a = 0
        for gi, g in enumerate(groups):
            for j in range(len(g)):
                for k, (tx, ty) in enumerate(chips):
                    _rcopy(ins[a].at[c], lands[a].at[me, c], sems[2 * gi].at[3 * j + k], sems[2 * gi + 1].at[3 * j + k],
                           (tx, ty, c)).start()
                a += 1
        token[...] = jnp.zeros_like(token)

    land_shapes = [(4,) + s.shape for s in flat]
    out_shape = ([pltpu.SemaphoreType.DMA((3 * len(g),)) for g in groups for _ in range(2)]
                 + [pltpu.HBM(s.shape, s.dtype) for s in flat]
                 + [pltpu.HBM(ls, s.dtype) for ls, s in zip(land_shapes, flat)]
                 + [jax.ShapeDtypeStruct((8, 128), F32)])
    outs = _pallas(
        body, name=name, in_specs=[_HBM_SPEC] * (2 * n),
        out_specs=[_SEM_SPEC] * (2 * ng) + [_HBM_SPEC] * (2 * n) + [pl.BlockSpec(memory_space=pltpu.VMEM)],
        out_shape=out_shape, input_output_aliases={i: 2 * ng + i for i in range(2 * n)},
        compiler_params=pltpu.CompilerParams(has_side_effects=_DATAFLOW),
    )(*[_in_hbm(s) for s in flat], *[_in_hbm(lax.empty(ls, s.dtype)) for ls, s in zip(land_shapes, flat)])
    sems, thru, lands, token = outs[:2 * ng], outs[2 * ng:2 * ng + n], outs[2 * ng + n:2 * ng + 2 * n], outs[-1]
    res, a = [], 0
    for gi, g in enumerate(groups):
        res.append((sems[2 * gi], sems[2 * gi + 1], thru[a:a + len(g)], lands[a:a + len(g)]))
        a += len(g)
    return res, token


def _gather_wait(ssem, rsem, shards, lands, after, *, name):
    m = len(shards)

    def body(*refs):
        ins, lnd = refs[:m], refs[m:2 * m]
        ss, rs = refs[2 * m], refs[2 * m + 1]
        x, y, c, chips = _place()
        for j in range(m):
            for k, (tx, ty) in enumerate(chips):
                cp = _rcopy(ins[j].at[c], lnd[j].at[2 * tx + ty, c], ss.at[3 * j + k], rs.at[3 * j + k], (tx, ty, c))
                cp.wait_send()
                cp.wait_recv()

    outs = _pallas(
        body, name=name, in_specs=[_HBM_SPEC] * (2 * m) + [_SEM_SPEC, _SEM_SPEC, _ANY],
        out_specs=[_HBM_SPEC] * (2 * m),
        out_shape=[pltpu.HBM(s.shape, s.dtype) for s in shards] + [pltpu.HBM(l.shape, l.dtype) for l in lands],
        input_output_aliases={i: i for i in range(2 * m)},
        compiler_params=pltpu.CompilerParams(has_side_effects=_DATAFLOW),
    )(*shards, *lands, ssem, rsem, after)
    return outs[m:]


def _gather_forward(lands, *, name):
    n = len(lands)

    def body(*refs):
        outs = refs[n:2 * n]
        ssem, rsem = refs[2 * n:]
        x, y, c, chips = _place()
        sib = (x, y, 1 - c)
        cps = []
        for a in range(n):
            for k, (tx, ty) in enumerate(chips):
                pk = 2 * tx + ty
                cp = _rcopy(outs[a].at[pk, c], outs[a].at[pk, c], ssem.at[3 * a + k], rsem.at[3 * a + k], sib)
                cp.start()
                cps.append(cp)
        for a in range(n):
            for k, (tx, ty) in enumerate(chips):
                pk = 2 * tx + ty
                _rcopy(outs[a].at[pk, c], outs[a].at[pk, 1 - c], ssem.at[3 * a + k], rsem.at[3 * a + k], sib).wait_recv()
        for cp in cps:
            cp.wait_send()

    shapes = [jax.ShapeDtypeStruct(l.shape, l.dtype) for l in lands]
    return _comm_call(body, lands, shapes, [3 * n, 3 * n], name=name, aliases={i: i for i in range(n)})


class _Lazy:
    def __init__(self, group_of, make):
        self._group_of, self._make, self._done, self._anchor = group_of, make, {}, None

    def anchor(self, value):
        self._anchor = value

    def __getitem__(self, key):
        g = self._group_of[key]
        if g not in self._done:
            self._done[g] = self._make(g, self._anchor)
        return self._done[g][key]


def _anchor(mapping, value):
    if isinstance(mapping, _Lazy):
        mapping.anchor(value)


def _rcopy(src, dst, ssem, rsem, dev):
    return pltpu.make_async_remote_copy(src_ref=src, dst_ref=dst, send_sem=ssem, recv_sem=rsem,
                                        device_id=dev, device_id_type=MESH)


def _all_gather(shards, *, name):
    n = len(shards)

    def body(*refs):
        ins, outs = refs[:n], refs[n:2 * n]
        s_ici, r_ici, s_d2d, r_d2d = refs[2 * n:]
        x, y, c, chips = _place()
        me = 2 * x + y
        sib = (x, y, 1 - c)
        sends = []
        for a in range(n):
            for k, (tx, ty) in enumerate(chips):
                cp = _rcopy(ins[a].at[c], outs[a].at[me, c], s_ici.at[3 * a + k], r_ici.at[3 * a + k], (tx, ty, c))
                cp.start()
                sends.append(cp)
        for a in range(n):
            for k, (tx, ty) in enumerate(chips):
                pk = 2 * tx + ty
                _rcopy(ins[a].at[c], outs[a].at[pk, c], s_ici.at[3 * a + k], r_ici.at[3 * a + k], (tx, ty, c)).wait_recv()
                fw = _rcopy(outs[a].at[pk, c], outs[a].at[pk, c], s_d2d.at[3 * a + k], r_d2d.at[3 * a + k], sib)
                fw.start()
                sends.append(fw)
        for a in range(n):
            for k, (tx, ty) in enumerate(chips):
                pk = 2 * tx + ty
                _rcopy(ins[a].at[c], outs[a].at[pk, 1 - c], s_d2d.at[3 * a + k], r_d2d.at[3 * a + k], sib).wait_recv()
        for cp in sends:
            cp.wait_send()

    shapes = [jax.ShapeDtypeStruct((4,) + s.shape, s.dtype) for s in shards]
    return _comm_call(body, shards, shapes, [3 * n] * 4, name=name)


def _gather(shards, chip, *, name):
    outs = _all_gather(shards, name=name)
    return [lax.dynamic_update_slice(o, s[None], (chip, 0, 0, 0)) for o, s in zip(outs, shards)]


def _pair_send(gs, *, name):
    n = len(gs)

    def body(*refs):
        ins, theirs = refs[:n], refs[n:2 * n]
        ssem, rsem = refs[2 * n:]
        x, y, c, _ = _place()
        sib = (x, y, 1 - c)
        cps = []
        for a in range(n):
            cp = _rcopy(ins[a].at[:, pl.ds(1 - c, 1)], theirs[a], ssem.at[a], rsem.at[a], sib)
            cp.start()
            cps.append(cp)
        for cp in cps:
            cp.wait_send()
            cp.wait_recv()

    shapes = [jax.ShapeDtypeStruct((4, 1) + g.shape[2:], g.dtype) for g in gs]
    return _comm_call(body, gs, shapes, [n, n], name=name)


def _chip_exchange(hx, *, name):
    n = len(hx)

    def body(*refs):
        hxr, got = refs[:n], refs[n:2 * n]
        ssem, rsem = refs[2 * n:]
        x, y, c, chips = _place()
        cps = []
        for a in range(n):
            for k, (tx, ty) in enumerate(chips):
                cp = _rcopy(hxr[a].at[2 * tx + ty], got[a].at[k], ssem.at[3 * a + k], rsem.at[3 * a + k], (tx, ty, c))
                cp.start()
                cps.append(cp)
        for cp in cps:
            cp.wait_send()
            cp.wait_recv()

    shapes = [jax.ShapeDtypeStruct((3,) + h.shape[1:], h.dtype) for h in hx]
    return _comm_call(body, hx, shapes, [3 * n, 3 * n], name=name)


def _pair_swap(fs, *, name):
    n = len(fs)

    def body(*refs):
        ins, outs = refs[:n], refs[n:2 * n]
        ssem, rsem = refs[2 * n:]
        x, y, c, _ = _place()
        cps = []
        for a in range(n):
            cp = _rcopy(ins[a], outs[a], ssem.at[a], rsem.at[a], (x, y, 1 - c))
            cp.start()
            cps.append(cp)
        for cp in cps:
            cp.wait_send()
            cp.wait_recv()

    shapes = [jax.ShapeDtypeStruct(f.shape, f.dtype) for f in fs]
    return _comm_call(body, fs, shapes, [n, n], name=name)


def _chip_exchange_start(hx, *, name):
    n = len(hx)

    def body(*refs):
        ins, gots = refs[:n], refs[n:2 * n]
        ssem, rsem = refs[2 * n], refs[2 * n + 1]
        token = refs[-1]
        x, y, c, chips = _place()
        for a in range(n):
            for k, (tx, ty) in enumerate(chips):
                _rcopy(ins[a].at[2 * tx + ty], gots[a].at[k], ssem.at[3 * a + k], rsem.at[3 * a + k], (tx, ty, c)).start()
        token[...] = jnp.zeros_like(token)

    got_shapes = [(3,) + h.shape[1:] for h in hx]
    outs = _pallas(
        body, name=name, in_specs=[_HBM_SPEC] * (2 * n),
        out_specs=[_SEM_SPEC] * 2 + [_HBM_SPEC] * (2 * n) + [pl.BlockSpec(memory_space=pltpu.VMEM)],
        out_shape=([pltpu.SemaphoreType.DMA((3 * n,))] * 2 + [pltpu.HBM(h.shape, h.dtype) for h in hx]
                   + [pltpu.HBM(gs, h.dtype) for gs, h in zip(got_shapes, hx)] + [jax.ShapeDtypeStruct((8, 128), F32)]),
        input_output_aliases={i: 2 + i for i in range(2 * n)},
        compiler_params=pltpu.CompilerParams(has_side_effects=_DATAFLOW),
    )(*[_in_hbm(h) for h in hx], *[_in_hbm(lax.empty(gs, h.dtype)) for gs, h in zip(got_shapes, hx)])
    return (outs[0], outs[1], outs[2:2 + n], outs[2 + n:2 + 2 * n]), outs[-1]


def _chip_exchange_wait(started, after, *, name):
    ssem, rsem, hx, gots = started
    n = len(hx)

    def body(*refs):
        ins, gts = refs[:n], refs[n:2 * n]
        ss, rs = refs[2 * n], refs[2 * n + 1]
        x, y, c, chips = _place()
        for a in range(n):
            for k, (tx, ty) in enumerate(chips):
                cp = _rcopy(ins[a].at[2 * tx + ty], gts[a].at[k], ss.at[3 * a + k], rs.at[3 * a + k], (tx, ty, c))
                cp.wait_send()
                cp.wait_recv()

    outs = _pallas(
        body, name=name, in_specs=[_HBM_SPEC] * (2 * n) + [_SEM_SPEC, _SEM_SPEC, _ANY],
        out_specs=[_HBM_SPEC] * (2 * n),
        out_shape=[pltpu.HBM(h.shape, h.dtype) for h in hx] + [pltpu.HBM(g.shape, g.dtype) for g in gots],
        input_output_aliases={i: i for i in range(2 * n)},
        compiler_params=pltpu.CompilerParams(has_side_effects=_DATAFLOW),
    )(*hx, *gots, ssem, rsem, after)
    return outs[n:]


def _pair_sums(grads, exch_bf16, cidx, tag):
    theirs = _pair_send(grads, name=f"rs_pair_send_{tag}")
    hf, hx = [], []
    for a in range(len(grads)):
        res = _pair_sum(grads[a], theirs[a], cidx, cast=exch_bf16[a], name=f"rs_pair_sum_{tag}{a}")
        hf.append(res[0])
        hx.append(res[1] if exch_bf16[a] else res[0])
    return hf, hx


def _chip_sums(hf, got, chip_idx, tag):
    return [_chip_sum(hf[a], got[a], chip_idx, name=f"rs_chip_sum_{tag}{a}") for a in range(len(hf))]


def _interleave(a, B, L):
    return a.reshape(B, L, -1).transpose(1, 0, 2).reshape(B * L, -1)


def _deinterleave(a, B, L):
    return a.reshape(L, B, -1).transpose(1, 0, 2).reshape(B * L, -1)


def _local_step(x, tgt, W, S, on_grads=None):
    B, L, D = x.shape
    T = B * L
    G = D // SSM_GROUP
    Pst = SSM_STATE
    hpg = D // HEAD_DIM
    HW = hpg * HEAD_DIM
    ncl = G // GROUPS_PER_CLUSTER
    x2 = x.reshape(T, D)
    tgt2 = tgt.reshape(T, D)

    disc = lambda *p: _s5_discretize(*p)
    (ab_r, ab_i, bb_r, bb_i), disc_vjp = jax.vjp(disc, S["lam_re"], S["lam_im"], S["log_dt"], S["b_re"], S["b_im"])
    wb = jnp.concatenate([_blockdiag(jnp.transpose(bb_r, (0, 2, 1))), _blockdiag(jnp.transpose(bb_i, (0, 2, 1)))],
                         axis=-1).astype(MXU_DTYPE)
    wc = jnp.concatenate([_blockdiag(jnp.transpose(S["c_re"], (0, 2, 1))), _blockdiag(-jnp.transpose(S["c_im"], (0, 2, 1)))],
                         axis=1).astype(MXU_DTYPE)
    cs = GROUPS_PER_CLUSTER * Pst
    slab = lambda ab: jnp.tile(jnp.transpose(ab.reshape(ncl, cs // LANES, LANES), (1, 0, 2)), (1, B, 1))
    a_r, a_i = slab(ab_r), slab(ab_i)
    d_row = S["d"].reshape(1, D)

    xi = _interleave(x2, B, L)
    y, yg, h_r, h_i = _s5_fwd(xi, wb, wc, a_r, a_i, d_row, B, name="s5_fwd")
    _anchor(W, yg)
    z = _mm_nn(yg, W["w_glu"], bias=S["b_glu"].reshape(1, D), name="glu_z")
    gate = _glu_gate(y, z, name="glu_gate")
    mix = _deinterleave(_mm_nn(gate, W["w_out"], name="s5_out"), B, L)
    h1, h1b, xh1, rs1 = _ln_fwd(x2, mix, S["ln_gain"][0, 0][None], S["ln_bias"][0, 0][None], name="ln_fwd_0a")

    def ffn_fwd(hb, l):
        hc = _mm_nn(hb, W["w_up"], l=l, out_dtype=MXU_DTYPE, name=f"ffn_up_{l}")
        a = _conv_glu_fwd(hc, S["conv_w"][l], S["conv_b"][l][None], L, name=f"ffn_conv_{l}")
        f = _mm_nn(a, W["w_down"], l=l, name=f"ffn_down_{l}")
        return hc, a, f

    _anchor(W, h1b)
    hc0, a0, f0 = ffn_fwd(h1b, 0)
    h2, h2b, xh2, rs2 = _ln_fwd(h1, f0, S["ln_gain"][0, 1][None], S["ln_bias"][0, 1][None], name="ln_fwd_0b")

    _anchor(W, h2b)
    kv = _mm_nn(h2b, W["w_kv"], name="attn_kv")
    q = _mm_nn(h2b, W["w_q"], name="attn_q")
    bias = _attn_bias(S["rel_bias"], hpg)
    o3, l3 = _attn_fwd(q, kv, bias, L, hpg, name="attn_fwd")
    o, ob, lse = _attn_merge(o3, l3, HW, name="attn_merge")
    att = _mm_nn(ob, W["w_ao"], name="attn_out")
    h3, h3b, xh3, rs3 = _ln_fwd(h2, att, S["ln_gain"][1, 0][None], S["ln_bias"][1, 0][None], name="ln_fwd_1a")
    hc1, a1, f1 = ffn_fwd(h3b, 1)
    h4, _, xh4, rs4 = _ln_fwd(h3, f1, S["ln_gain"][1, 1][None], S["ln_bias"][1, 1][None], name="ln_fwd_1b")

    dh4, lrow = _loss_grad(h4, tgt2, name="loss")
    loss = lrow[0, 0]

    GW, GS = {}, {}

    def ffn_bwd(dzb, hb, hc, a, l):
        da = _mm_nt(dzb, W["w_down"], l=l, out_dtype=MXU_DTYPE, name=f"ffn_down_bwd_x_{l}")
        GW[f"w_down{l}"] = _tn(a, dzb, ptotal=1, np_cols=D, name=f"ffn_down_bwd_w_{l}")
        dc, dcw, dcb = _conv_glu_bwd(hc, da, S["conv_w"][l], S["conv_b"][l][None], L, name=f"ffn_conv_bwd_{l}")
        dhc = _conv_bwd_input(dc, S["conv_w"][l], L, name=f"ffn_conv_bwd_x_{l}")
        dh = _mm_nt(dhc, W["w_up"], l=l, name=f"ffn_up_bwd_x_{l}")
        GW[f"w_up{l}"] = _tn(hb, dhc, ptotal=W["w_up"].shape[0], np_cols=W["w_up"].shape[3], name=f"ffn_up_bwd_w_{l}")
        return dh, dcw, dcb

    dz4, dz4b, dg4, db4 = _ln_bwd([dh4], [1.0], xh4, rs4, S["ln_gain"][1, 1][None], name="ln_bwd_1b")
    dh3f, dcw1, dcb1 = ffn_bwd(dz4b, h3b, hc1, a1, 1)
    dz3, dz3b, dg3, db3 = _ln_bwd([dz4, dh3f], [DN_ALPHA, 1.0], xh3, rs3, S["ln_gain"][1, 0][None], name="ln_bwd_1a")
    do = _mm_nt(dz3b, W["w_ao"], name="attn_out_bwd_x")
    GW["w_ao"] = _tn(ob, dz3b, ptotal=1, np_cols=D, name="attn_out_bwd_w")
    dq, dk, dv, ds_sum = _attn_bwd(q, kv, do, o, lse, bias, L, hpg, name="attn_bwd")
    GS["rel_bias"] = _bias_grad(ds_sum, hpg, name="attn_bias_grad")
    GW["w_q"] = _tn(h2b, dq, ptotal=W["w_q"].shape[0], np_cols=W["w_q"].shape[3], name="attn_q_bwd_w")
    pkv, npkv = W["w_kv"].shape[0], W["w_kv"].shape[3]
    gkv = _tn(h2b, dk, ptotal=pkv, np_cols=npkv, p0=0, name="attn_k_bwd_w")
    GW["w_kv"] = _tn(h2b, dv, ptotal=pkv, np_cols=npkv, p0=pkv // 2, prev=gkv, name="attn_v_bwd_w")
    dh2q = _mm_nt(dq, W["w_q"], name="attn_q_bwd_x")
    dh2k = _mm_nt(dk, W["w_kv"], p0=0, pn=pkv // 2, name="attn_k_bwd_x")
    dh2v = _mm_nt(dv, W["w_kv"], p0=pkv // 2, pn=pkv // 2, name="attn_v_bwd_x")

    gain_0b = S["ln_gain"][0, 1][None]
    if on_grads is not None:
        gain_0b = gain_0b + on_grads(0, GW)[0, 0]

    dz2, dz2b, dg2, db2 = _ln_bwd([dz3, dh2q, dh2k, dh2v], [DN_ALPHA, 1.0, 1.0, 1.0], xh2, rs2, gain_0b,
                                  name="ln_bwd_0b")
    dh1f, dcw0, dcb0 = ffn_bwd(dz2b, h1b, hc0, a0, 0)
    gain_0a = S["ln_gain"][0, 0][None]
    if on_grads is not None:
        gain_0a = gain_0a + on_grads(1, GW)[0, 0]
    dz1, dz1b, dg1, db1 = _ln_bwd([dz2, dh1f], [DN_ALPHA, 1.0], xh1, rs1, gain_0a, name="ln_bwd_0a")
    dmix_i = _interleave(dz1b, B, L)
    dgate = _mm_nt(dmix_i, W["w_out"], name="s5_out_bwd_x")
    GW["w_out"] = _tn(gate, dmix_i, ptotal=1, np_cols=D, name="s5_out_bwd_w")
    dzg, dyg1, dbglu = _glu_bwd(y, z, dgate, name="glu_bwd")
    dyg2 = _mm_nt(dzg, W["w_glu"], name="glu_z_bwd_x")
    GW["w_glu"] = _tn(yg, dzg, ptotal=1, np_cols=D, name="glu_z_bwd_w")
    dy = _gelu_bwd(y, dyg1, dyg2, name="gelu_bwd")
    du_i, g_r, g_i, dar, dai, dd = _s5_bwd(dy, xi, h_r, h_i, wb, wc, a_r, a_i, d_row, B, name="s5_bwd")
    dwb_r = _cluster_tn(xi, g_r, ncl, tok_left=True, name="s5_b_grad_re")
    dwb_i = _cluster_tn(xi, g_i, ncl, tok_left=True, name="s5_b_grad_im")
    dwc_r = _cluster_tn(dy, h_r, ncl, tok_left=False, name="s5_c_grad_re")
    dwc_i = _cluster_tn(dy, h_i, ncl, tok_left=False, name="s5_c_grad_im")
    grad_x = _axpy(dz1, _deinterleave(du_i, B, L), DN_ALPHA, name="grad_x")

    dbb_r = jnp.transpose(_unblockdiag(dwb_r, SSM_GROUP, Pst), (0, 2, 1))
    dbb_i = jnp.transpose(_unblockdiag(dwb_i, SSM_GROUP, Pst), (0, 2, 1))
    unslab = lambda da: jnp.transpose(da.reshape(cs // LANES, B, ncl, LANES).sum(1), (1, 0, 2)).reshape(G, Pst)
    dab_r, dab_i = unslab(dar), unslab(dai)
    GS["lam_re"], GS["lam_im"], GS["log_dt"], GS["b_re"], GS["b_im"] = disc_vjp((dab_r, dab_i, dbb_r, dbb_i))
    GS["c_re"] = jnp.transpose(_unblockdiag(dwc_r, Pst, SSM_GROUP), (0, 2, 1))
    GS["c_im"] = -jnp.transpose(_unblockdiag(dwc_i, Pst, SSM_GROUP), (0, 2, 1))
    GS["d"] = dd.reshape(G, SSM_GROUP)
    GS["b_glu"] = dbglu.reshape(D)
    GS["conv_w"] = jnp.stack([dcw0, dcw1])
    GS["conv_b"] = jnp.stack([dcb0[0], dcb1[0]])
    GS["ln_gain"] = jnp.stack([jnp.stack([dg1[0], dg2[0]]), jnp.stack([dg3[0], dg4[0]])])
    GS["ln_bias"] = jnp.stack([jnp.stack([db1[0], db2[0]]), jnp.stack([db3[0], db4[0]])])
    return loss, grad_x.reshape(B, L, D), GW, GS


SMALL_REPLICATED = ("lam_re", "lam_im", "log_dt", "b_re", "b_im", "c_re", "c_im", "d", "rel_bias", "conv_b")
SMALL_SHARDED = ("b_glu", "conv_w", "ln_gain", "ln_bias")
SMALL_ORDER = SMALL_REPLICATED + SMALL_SHARDED


def _pack(arrs, lanes, row_mult):
    flat = jnp.concatenate([a.reshape(-1).astype(F32) for a in arrs])
    rows = -(-flat.shape[0] // lanes)
    rows = -(-rows // row_mult) * row_mult
    return jnp.pad(flat, (0, rows * lanes - flat.shape[0])).reshape(rows, lanes)


def _unpack(packed, shapes):
    flat = packed.reshape(-1)
    out, off = [], 0
    for s in shapes:
        n = int(np.prod(s))
        out.append(flat[off:off + n].reshape(s))
        off += n
    return out


def kernel(x, s5_lam_re, s5_lam_im, s5_log_dt, s5_b_re, s5_b_im, s5_c_re, s5_c_im, s5_d, s5_w_glu, s5_b_glu, s5_w_out, attn_w_kv, attn_w_q, attn_w_out, rel_bias, ffn_w_up, ffn_conv_w, ffn_conv_b, ffn_w_down, ln_gain, ln_bias, loss_target, m_s5_lam_re, m_s5_lam_im, m_s5_log_dt, m_s5_b_re, m_s5_b_im, m_s5_c_re, m_s5_c_im, m_s5_d, m_s5_w_glu, m_s5_b_glu, m_s5_w_out, m_attn_w_kv, m_attn_w_q, m_attn_w_out, m_rel_bias, m_ffn_w_up, m_ffn_conv_w, m_ffn_conv_b, m_ffn_w_down, m_ln_gain, m_ln_bias, v_s5_lam_re, v_s5_lam_im, v_s5_log_dt, v_s5_b_re, v_s5_b_im, v_s5_c_re, v_s5_c_im, v_s5_d, v_s5_w_glu, v_s5_b_glu, v_s5_w_out, v_attn_w_kv, v_attn_w_q, v_attn_w_out, v_rel_bias, v_ffn_w_up, v_ffn_conv_w, v_ffn_conv_b, v_ffn_w_down, v_ln_gain, v_ln_bias):
    names = ["s5_lam_re", "s5_lam_im", "s5_log_dt", "s5_b_re", "s5_b_im", "s5_c_re", "s5_c_im", "s5_d", "s5_w_glu",
             "s5_b_glu", "s5_w_out", "attn_w_kv", "attn_w_q", "attn_w_out", "rel_bias", "ffn_w_up", "ffn_conv_w",
             "ffn_conv_b", "ffn_w_down", "ln_gain", "ln_bias"]
    loc = locals()
    w_in = {n: loc[n] for n in names}
    m_in = {n: loc["m_" + n] for n in names}
    v_in = {n: loc["v_" + n] for n in names}
    chip = 2 * lax.axis_index("x") + lax.axis_index("y")
    core = lax.axis_index("c")
    chip_idx = jnp.reshape(chip, (1,)).astype(jnp.int32)
    cidx = jnp.reshape(core, (1,)).astype(jnp.int32)

    big = [("w_glu", "s5_w_glu", "rows"), ("w_out", "s5_w_out", "rows"), ("w_ao", "attn_w_out", "rows"),
           ("w_kv", "attn_w_kv", "cols"), ("w_q", "attn_w_q", "cols"),
           ("w_up", "ffn_w_up", "layer_cols"), ("w_down", "ffn_w_down", "layer_rows")]

    def halves(t, kind):
        if kind.startswith("layer"):
            return t
        r, c = t.shape[-2:]
        return t.reshape(2, r // 2, c)

    def to_weight(g, kind):
        _, _, r, c = g.shape
        if kind == "rows":
            return g.reshape(1, 1, 8 * r, c)
        if kind == "cols":
            return g.reshape(4, 1, 2 * r, c)
        if kind == "layer_cols":
            return g
        return jnp.transpose(g, (1, 0, 2, 3)).reshape(1, 2, 4 * r, c)

    small_sh = {"b_glu": s5_b_glu[0], "conv_w": ffn_conv_w, "ln_gain": ln_gain, "ln_bias": ln_bias}
    sh_shapes = [small_sh[k].shape for k in SMALL_SHARDED]
    sh_pack = _pack([small_sh[k] for k in SMALL_SHARDED], 128, 16)

    shards = [halves(w_in[src].astype(MXU_DTYPE), kind) for _, src, kind in big]
    shards.append(sh_pack.reshape(2, sh_pack.shape[0] // 2, 128))
    shard_of = {key: s for (key, _, _), s in zip(big, shards)}
    shard_of["small"] = shards[-1]
    kind_of = {key: kind for key, _, kind in big}

    group_keys = [["w_glu", "w_out", "small"], ["w_up", "w_down"], ["w_kv", "w_q", "w_ao"]]
    started, token = _gather_start([[shard_of[k] for k in g] for g in group_keys], name="weights_gather_start")

    def finish_group(gi, after):
        ssem, rsem, thru, lands = started[gi]
        lands = _gather_wait(ssem, rsem, thru, lands, after, name=f"weights_gather_wait_{gi}")
        lands = _gather_forward(lands, name=f"weights_gather_forward_{gi}")
        out = {}
        for key, land in zip(group_keys[gi], lands):
            full = lax.dynamic_update_slice(land, shard_of[key][None], (chip, 0, 0, 0))
            if key == "small":
                parts = [_unpack(full[p], sh_shapes) for p in range(4)]
                for i, k in enumerate(SMALL_SHARDED):
                    out[k] = jnp.concatenate([parts[p][i] for p in range(4)], axis=-1)
            else:
                out[key] = to_weight(full, kind_of[key])
        return out

    replicated = dict(lam_re=s5_lam_re[0], lam_im=s5_lam_im[0], log_dt=s5_log_dt[0], b_re=s5_b_re[0], b_im=s5_b_im[0],
                      c_re=s5_c_re[0], c_im=s5_c_im[0], rel_bias=rel_bias, conv_b=ffn_conv_b,
                      d=s5_d[0] + token[0, 0])
    group_of = {k: gi for gi, g in enumerate(group_keys) for k in g if k != "small"}
    group_of.update({k: 0 for k in SMALL_SHARDED})
    group_of.update({k: "replicated" for k in replicated})
    params = _Lazy(group_of, lambda g, after: replicated if g == "replicated" else finish_group(g, after))

    red = [("w_up1", "ffn_w_up", 1), ("w_down1", "ffn_w_down", 1), ("w_ao", "attn_w_out", 0), ("w_kv", "attn_w_kv", 0),
           ("w_q", "attn_w_q", 0), ("w_up0", "ffn_w_up", 0), ("w_down0", "ffn_w_down", 0), ("w_out", "s5_w_out", 0),
           ("w_glu", "s5_w_glu", 0)]
    early_stages = [red[:5], red[5:7]]
    n_early = 7

    def grad_halves(gw, key, src):
        r, c = w_in[src].shape[-2:]
        return gw[key].reshape(4, 2, r // 2, c)

    early = []

    def on_grads(stage, gw):
        tag = "ab"[stage]
        ga = [grad_halves(gw, key, src) for key, src, _ in early_stages[stage]]
        hf, hx = _pair_sums(ga, [True] * len(ga), cidx, tag)
        started, tok = _chip_exchange_start(hx, name=f"rs_chip_exchange_start_{tag}")
        early.append((hf, started, tag))
        return tok

    loss, grad_x, GW, GS = _local_step(x, loss_target, params, params, on_grads)
    loss = lax.psum(loss, ("x", "y", "c"))

    gs_shapes = [GS[k].shape for k in SMALL_ORDER]
    gs_pack = _pack([GS[k] for k in SMALL_ORDER], 128, 64)
    rs = gs_pack.shape[0] // 8
    gl = [grad_halves(GW, key, src) for key, src, _ in red[n_early:]] + [gs_pack.reshape(4, 2, rs, 128)]
    hf_l, hx_l = _pair_sums(gl, [True] * (len(gl) - 1) + [False], cidx, "c")
    mine = []
    for hf, started, tag in early:
        got = _chip_exchange_wait(started, grad_x, name=f"rs_chip_exchange_wait_{tag}")
        mine += _chip_sums(hf, got, chip_idx, tag)
    mine += _chip_sums(hf_l, _chip_exchange(hx_l, name="rs_chip_exchange_c"), chip_idx, "c")
    other = _pair_swap(mine, name="rs_pair_swap")
    small_halves = jnp.where(core == 0, jnp.concatenate([mine[-1], other[-1]]), jnp.concatenate([other[-1], mine[-1]]))
    small_all = _gather([small_halves], chip, name="small_grads_all_gather")[0]
    gsmall = dict(zip(SMALL_ORDER, _unpack(small_all, gs_shapes)))

    big_res = {}
    for (key, src, layer), gm, go in zip(red, mine[:-1], other[:-1]):
        nl = w_in[src].shape[0] if src in ("ffn_w_up", "ffn_w_down") else 1
        r, c = w_in[src].shape[-2:]
        view = lambda t: t.reshape(nl, 2, r // 2, c)
        res4 = _adamw_halves(view(w_in[src]), view(m_in[src]), view(v_in[src]), gm, go, cidx, layer=layer,
                             prev=big_res.get(src), name=f"adamw_{key}")
        big_res[src] = res4
    big_res = {src: tuple(t.reshape(w_in[src].shape) for t in res4) for src, res4 in big_res.items()}

    def big_out(i):
        return {src: big_res[src][i] for _, src, _ in big}

    small_w = {"lam_re": s5_lam_re, "lam_im": s5_lam_im, "log_dt": s5_log_dt, "b_re": s5_b_re, "b_im": s5_b_im,
               "c_re": s5_c_re, "c_im": s5_c_im, "d": s5_d, "rel_bias": rel_bias, "conv_b": ffn_conv_b,
               "b_glu": s5_b_glu, "conv_w": ffn_conv_w, "ln_gain": ln_gain, "ln_bias": ln_bias}
    small_name = {"lam_re": "s5_lam_re", "lam_im": "s5_lam_im", "log_dt": "s5_log_dt", "b_re": "s5_b_re", "b_im": "s5_b_im",
                  "c_re": "s5_c_re", "c_im": "s5_c_im", "d": "s5_d", "rel_bias": "rel_bias", "conv_b": "ffn_conv_b",
                  "b_glu": "s5_b_glu", "conv_w": "ffn_conv_w", "ln_gain": "ln_gain", "ln_bias": "ln_bias"}
    sg = {}
    for k in SMALL_ORDER:
        shp = small_w[k].shape
        g = gsmall[k]
        if k in SMALL_SHARDED:
            width = shp[-1]
            g = lax.dynamic_slice_in_dim(g, chip * width, width, axis=g.ndim - 1)
        sg[k] = g.reshape(shp)
    sd, snm, snv = {}, {}, {}
    for k in SMALL_ORDER:
        shp = small_w[k].shape
        flat = lambda t: t.reshape(-1, shp[-1])
        r3 = _adamw(flat(small_w[k]), flat(sg[k]), flat(m_in[small_name[k]]), flat(v_in[small_name[k]]),
                    name=f"adamw_{k}")
        sd[k], snm[k], snv[k] = (t.reshape(shp) for t in r3)

    res = [{}, {}, {}, {}]
    for i in range(4):
        res[i].update(big_out(i))
    for k in SMALL_ORDER:
        res[0][small_name[k]] = sg[k]
        res[1][small_name[k]] = sd[k]
        res[2][small_name[k]] = snm[k]
        res[3][small_name[k]] = snv[k]
    outs = [loss, grad_x]
    for i in range(4):
        outs += [res[i][n] for n in names]
    return tuple(outs)
```

```python
import functools
import math

import numpy as np
import jax
import jax.numpy as jnp
from jax import lax
from jax.experimental import pallas as pl
from jax.experimental.pallas import tpu as pltpu

F32 = jnp.float32
BF16 = jnp.bfloat16
MXU_DTYPE = jnp.bfloat16
V7X_VMEM_LIMIT_BYTES = 52 << 20
MESH = pl.DeviceIdType.MESH

DEPTH = 2
SSM_GROUP = 16
SSM_STATE = 64
GROUPS_PER_CLUSTER = 16
CLUSTER_W = GROUPS_PER_CLUSTER * SSM_GROUP
HEAD_DIM = 64
DILATIONS = (1, 4, 16)
BAND = 128
NEG_BIG = -1e30
REL_BUCKETS = 32
REL_MAX_DIST = 2048
DN_ALPHA = (2.0 * DEPTH) ** 0.25
LN_EPS = 1e-5
ADAM_LR, ADAM_B1, ADAM_B2, ADAM_EPS, ADAM_WD, ADAM_STEP = 0.001, 0.9, 0.999, 1e-08, 0.01, 10
GELU_K = math.sqrt(2.0 / math.pi)
GELU_C = 0.044715


def _pallas(body, **kw):
    return pl.pallas_call(body, **kw)


def _params(sem=None):
    return pltpu.CompilerParams(dimension_semantics=sem, vmem_limit_bytes=V7X_VMEM_LIMIT_BYTES)


def _pick(n, cands):
    for c in cands:
        if n % c == 0:
            return c
    return n


def _sigmoid(z):
    return 1.0 / (1.0 + jnp.exp(-z))


def _gelu(y):
    return 0.5 * y * (1.0 + jnp.tanh(GELU_K * (y + GELU_C * y * y * y)))


def _gelu_grad(y):
    t = jnp.tanh(GELU_K * (y + GELU_C * y * y * y))
    return 0.5 * (1.0 + t) + 0.5 * y * (1.0 - t * t) * (GELU_K * (1.0 + 3.0 * GELU_C * y * y))


def _mm_nn(a, w, *, l=0, bias=None, out_dtype=F32, name):
    T, K = a.shape
    P, _, _, Np = w.shape
    tm = _pick(T, (1024, 512, 256, 128))
    tn = _pick(Np, (1408, 1024, 768, 512, 384, 256, 128))
    nj = Np // tn

    def body(*refs):
        if bias is None:
            a_ref, w_ref, o_ref = refs
        else:
            a_ref, w_ref, b_ref, o_ref = refs
        acc = jnp.dot(a_ref[...].astype(MXU_DTYPE), w_ref[...].astype(MXU_DTYPE), preferred_element_type=F32)
        if bias is not None:
            acc = acc + b_ref[...]
        o_ref[...] = acc.astype(o_ref.dtype)

    in_specs = [pl.BlockSpec((tm, K), lambda p, j, i: (i, 0)),
                pl.BlockSpec((None, None, K, tn), lambda p, j, i: (p, l, 0, j))]
    args = [a, w]
    if bias is not None:
        in_specs.append(pl.BlockSpec((1, tn), lambda p, j, i: (0, p * nj + j)))
        args.append(bias)
    return _pallas(
        body, name=name, grid=(P, nj, T // tm), in_specs=in_specs,
        out_specs=pl.BlockSpec((tm, tn), lambda p, j, i: (i, p * nj + j)),
        out_shape=jax.ShapeDtypeStruct((T, P * Np), out_dtype),
        compiler_params=_params(("parallel", "parallel", "parallel")),
    )(*args)


def _mm_nt(a, w, *, l=0, p0=0, pn=None, out_dtype=F32, name):
    T = a.shape[0]
    _, _, K, Np = w.shape
    pn = w.shape[0] if pn is None else pn
    tm = _pick(T, (1024, 512, 256, 128) if K <= 1024 else (512, 256, 128))
    tn = _pick(Np, (1536, 1408, 1024, 768, 512, 384, 256, 128))
    nj = Np // tn
    nred = pn * nj

    def body(a_ref, w_ref, o_ref, acc):
        r = pl.program_id(1)

        @pl.when(r == 0)
        def _():
            acc[...] = jnp.zeros_like(acc)

        acc[...] += lax.dot_general(a_ref[...].astype(MXU_DTYPE), w_ref[...].astype(MXU_DTYPE),
                                    (((1,), (1,)), ((), ())), preferred_element_type=F32)

        @pl.when(r == nred - 1)
        def _():
            o_ref[...] = acc[...].astype(o_ref.dtype)

    return _pallas(
        body, name=name, grid=(T // tm, nred),
        in_specs=[pl.BlockSpec((tm, tn), lambda i, r: (i, r)),
                  pl.BlockSpec((None, None, K, tn), lambda i, r: (p0 + r // nj, l, 0, r % nj))],
        out_specs=pl.BlockSpec((tm, K), lambda i, r: (i, 0)),
        out_shape=jax.ShapeDtypeStruct((T, K), out_dtype),
        scratch_shapes=[pltpu.VMEM((tm, K), F32)],
        compiler_params=_params(("parallel", "arbitrary")),
    )(a, w)


def _tn(a, b, *, ptotal, np_cols, nl=1, l=0, p0=0, prev=None, name):
    T, K = a.shape
    Np = np_cols
    pn = b.shape[1] // Np
    tt = _pick(T, (1024, 512, 256, 128))
    tk = _pick(K, (1408, 1024, 512, 256, 128))
    tn = _pick(Np, (1408, 768, 512, 256, 128))
    if tk * tn > 1408 * 1024:
        tn = _pick(Np, (512, 256, 128))
    nj = Np // tn
    nt = T // tt

    def body(*refs):
        a_ref, b_ref = refs[0], refs[1]
        o_ref, acc = refs[-2], refs[-1]
        t = pl.program_id(3)

        @pl.when(t == 0)
        def _():
            acc[...] = jnp.zeros_like(acc)

        acc[...] += lax.dot_general(a_ref[...].astype(MXU_DTYPE), b_ref[...].astype(MXU_DTYPE),
                                    (((0,), (0,)), ((), ())), preferred_element_type=F32)

        @pl.when(t == nt - 1)
        def _():
            o_ref[...] = acc[...]

    in_specs = [pl.BlockSpec((tt, tk), lambda kb, p, j, t: (t, kb)),
                pl.BlockSpec((tt, tn), lambda kb, p, j, t: (t, p * nj + j))]
    args = [a, b]
    aliases = {}
    if prev is not None:
        in_specs.append(pl.BlockSpec(memory_space=pl.ANY))
        args.append(prev)
        aliases = {2: 0}
    return _pallas(
        body, name=name, grid=(K // tk, pn, nj, nt), in_specs=in_specs,
        out_specs=pl.BlockSpec((None, None, tk, tn), lambda kb, p, j, t: (p0 + p, l, kb, j)),
        out_shape=jax.ShapeDtypeStruct((ptotal, nl, K, Np), F32),
        scratch_shapes=[pltpu.VMEM((tk, tn), F32)],
        input_output_aliases=aliases,
        compiler_params=_params(("parallel", "parallel", "parallel", "arbitrary")),
    )(*args)


def _rows(tm, f):
    return pl.BlockSpec((tm, f), lambda i: (i, 0))


def _whole(shape):
    nd = len(shape)
    return pl.BlockSpec(shape, lambda i: (0,) * nd)


def _ln_fwd(xres, f, gain, bias, *, name):
    T, D = xres.shape
    tm = _pick(T, (256, 128))

    def body(x_ref, f_ref, g_ref, b_ref, y_ref, yb_ref, xh_ref, rs_ref):
        z = DN_ALPHA * x_ref[...] + f_ref[...]
        mu = jnp.mean(z, axis=-1, keepdims=True)
        zc = z - mu
        var = jnp.mean(zc * zc, axis=-1, keepdims=True)
        rstd = lax.rsqrt(var + LN_EPS)
        xh = zc * rstd
        y = xh * g_ref[...] + b_ref[...]
        y_ref[...] = y
        yb_ref[...] = y.astype(yb_ref.dtype)
        xh_ref[...] = xh
        rs_ref[...] = rstd

    return _pallas(
        body, name=name, grid=(T // tm,),
        in_specs=[_rows(tm, D), _rows(tm, D), _whole((1, D)), _whole((1, D))],
        out_specs=[_rows(tm, D), _rows(tm, D), _rows(tm, D), _rows(tm, 1)],
        out_shape=[jax.ShapeDtypeStruct((T, D), F32), jax.ShapeDtypeStruct((T, D), MXU_DTYPE),
                   jax.ShapeDtypeStruct((T, D), F32), jax.ShapeDtypeStruct((T, 1), F32)],
        compiler_params=_params(("parallel",)),
    )(xres, f, gain, bias)


def _ln_bwd(addends, coefs, xhat, rstd, gain, *, name):
    T, D = xhat.shape
    tm = _pick(T, (256, 128))
    n = len(addends)

    def body(*refs):
        adds = refs[:n]
        xh_ref, rs_ref, g_ref, dz_ref, dzb_ref, dg_ref, db_ref = refs[n:]
        dy = coefs[0] * adds[0][...]
        for c, r in zip(coefs[1:], adds[1:]):
            dy = dy + c * r[...]
        xh = xh_ref[...]
        dxh = dy * g_ref[...]
        m1 = jnp.mean(dxh, axis=-1, keepdims=True)
        m2 = jnp.mean(dxh * xh, axis=-1, keepdims=True)
        dz = rs_ref[...] * (dxh - m1 - xh * m2)
        dz_ref[...] = dz
        dzb_ref[...] = dz.astype(dzb_ref.dtype)

        @pl.when(pl.program_id(0) == 0)
        def _():
            dg_ref[...] = jnp.zeros_like(dg_ref)
            db_ref[...] = jnp.zeros_like(db_ref)

        dg_ref[...] += jnp.sum(dy * xh, axis=0, keepdims=True)
        db_ref[...] += jnp.sum(dy, axis=0, keepdims=True)

    return _pallas(
        body, name=name, grid=(T // tm,),
        in_specs=[_rows(tm, D)] * n + [_rows(tm, D), _rows(tm, 1), _whole((1, D))],
        out_specs=[_rows(tm, D), _rows(tm, D), _whole((1, D)), _whole((1, D))],
        out_shape=[jax.ShapeDtypeStruct((T, D), F32), jax.ShapeDtypeStruct((T, D), MXU_DTYPE),
                   jax.ShapeDtypeStruct((1, D), F32), jax.ShapeDtypeStruct((1, D), F32)],
        compiler_params=_params(("arbitrary",)),
    )(*addends, xhat, rstd, gain)


def _loss_grad(y, tgt, *, name):
    T, D = y.shape
    tm = _pick(T, (256, 128))

    def body(y_ref, t_ref, dy_ref, l_ref):
        e = y_ref[...] - t_ref[...]
        dy_ref[...] = e * (1.0 / D)

        @pl.when(pl.program_id(0) == 0)
        def _():
            l_ref[...] = jnp.zeros_like(l_ref)

        l_ref[...] += jnp.zeros_like(l_ref) + jnp.sum(e * e) * (0.5 / D)

    return _pallas(
        body, name=name, grid=(T // tm,),
        in_specs=[_rows(tm, D), _rows(tm, D)],
        out_specs=[_rows(tm, D), _whole((1, 128))],
        out_shape=[jax.ShapeDtypeStruct((T, D), F32), jax.ShapeDtypeStruct((1, 128), F32)],
        compiler_params=_params(("arbitrary",)),
    )(y, tgt)


def _axpy(a, b, ca, *, name):
    T, D = a.shape
    tm = _pick(T, (256, 128))

    def body(a_ref, b_ref, o_ref):
        o_ref[...] = ca * a_ref[...] + b_ref[...]

    return _pallas(
        body, name=name, grid=(T // tm,), in_specs=[_rows(tm, D), _rows(tm, D)], out_specs=_rows(tm, D),
        out_shape=jax.ShapeDtypeStruct((T, D), F32), compiler_params=_params(("parallel",)),
    )(a, b)


def _glu_gate(y, z, *, name):
    T, D = y.shape
    tm = _pick(T, (256, 128))

    def body(y_ref, z_ref, g_ref):
        g_ref[...] = (_gelu(y_ref[...]) * _sigmoid(z_ref[...])).astype(g_ref.dtype)

    return _pallas(
        body, name=name, grid=(T // tm,), in_specs=[_rows(tm, D), _rows(tm, D)], out_specs=_rows(tm, D),
        out_shape=jax.ShapeDtypeStruct((T, D), MXU_DTYPE), compiler_params=_params(("parallel",)),
    )(y, z)


def _glu_bwd(y, z, dg, *, name):
    T, D = y.shape
    tm = _pick(T, (256, 128))

    def body(y_ref, z_ref, dg_ref, dzb_ref, dyg_ref, db_ref):
        s = _sigmoid(z_ref[...])
        dg = dg_ref[...]
        dz = dg * _gelu(y_ref[...]) * s * (1.0 - s)
        dzb_ref[...] = dz.astype(dzb_ref.dtype)
        dyg_ref[...] = dg * s

        @pl.when(pl.program_id(0) == 0)
        def _():
            db_ref[...] = jnp.zeros_like(db_ref)

        db_ref[...] += jnp.sum(dz, axis=0, keepdims=True)

    return _pallas(
        body, name=name, grid=(T // tm,), in_specs=[_rows(tm, D)] * 3,
        out_specs=[_rows(tm, D), _rows(tm, D), _whole((1, D))],
        out_shape=[jax.ShapeDtypeStruct((T, D), MXU_DTYPE), jax.ShapeDtypeStruct((T, D), F32),
                   jax.ShapeDtypeStruct((1, D), F32)],
        compiler_params=_params(("arbitrary",)),
    )(y, z, dg)


def _gelu_bwd(y, d1, d2, *, name):
    T, D = y.shape
    tm = _pick(T, (256, 128))

    def body(y_ref, a_ref, b_ref, o_ref):
        o_ref[...] = (a_ref[...] + b_ref[...]) * _gelu_grad(y_ref[...])

    return _pallas(
        body, name=name, grid=(T // tm,), in_specs=[_rows(tm, D)] * 3, out_specs=_rows(tm, D),
        out_shape=jax.ShapeDtypeStruct((T, D), F32), compiler_params=_params(("parallel",)),
    )(y, d1, d2)


CONV_ROWS = 128
CONV_EDGE = 16


def _row_shifts(x, edge, drop_edge, tm, back):
    keep = jnp.where(drop_edge, 0.0, 1.0).astype(edge.dtype)
    ext = jnp.concatenate([edge * keep, x] if back else [x, edge * keep], axis=0)
    row = lax.broadcasted_iota(jnp.int32, (tm, tm + CONV_EDGE), 0)
    col = lax.broadcasted_iota(jnp.int32, (tm, tm + CONV_EDGE), 1)
    base = row + CONV_EDGE if back else row
    out = []
    for k in (1, 2):
        pick = (col == (base - k if back else base + k)).astype(x.dtype)
        out.append(jnp.dot(pick, ext, preferred_element_type=F32))
    return out


def _conv_specs(T, F2, tm):
    return [_rows(tm, F2),
            pl.BlockSpec((CONV_EDGE, F2), lambda i: (jnp.maximum(i * (tm // CONV_EDGE) - 1, 0), 0))]


def _conv_glu_fwd(hc, conv_w, conv_b, L, *, name):
    T, F2 = hc.shape
    F = F2 // 2
    tm = CONV_ROWS

    def body(x_ref, e_ref, w_ref, b_ref, a_ref):
        at_start = (pl.program_id(0) * tm) % L == 0
        x1, x2 = _row_shifts(x_ref[...], e_ref[...], at_start, tm, True)
        x = x_ref[...].astype(F32)
        c = b_ref[...] + w_ref[0:1, :] * x + w_ref[1:2, :] * x1 + w_ref[2:3, :] * x2
        val, gate = c[:, :F], c[:, F:]
        a_ref[...] = (gate * _sigmoid(gate) * val).astype(a_ref.dtype)

    return _pallas(
        body, name=name, grid=(T // tm,),
        in_specs=_conv_specs(T, F2, tm) + [_whole((3, F2)), _whole((1, F2))],
        out_specs=_rows(tm, F),
        out_shape=jax.ShapeDtypeStruct((T, F), MXU_DTYPE), compiler_params=_params(("parallel",)),
    )(hc, hc, conv_w, conv_b)


def _conv_glu_bwd(hc, da, conv_w, conv_b, L, *, name):
    T, F2 = hc.shape
    F = F2 // 2
    tm = CONV_ROWS

    def body(x_ref, e_ref, da_ref, w_ref, b_ref, dc_ref, dw_ref, db_ref):
        at_start = (pl.program_id(0) * tm) % L == 0
        x1, x2 = _row_shifts(x_ref[...], e_ref[...], at_start, tm, True)
        x = x_ref[...].astype(F32)
        c = b_ref[...] + w_ref[0:1, :] * x + w_ref[1:2, :] * x1 + w_ref[2:3, :] * x2
        val, gate = c[:, :F], c[:, F:]
        s = _sigmoid(gate)
        da = da_ref[...].astype(F32)
        dval = da * (gate * s)
        dgate = da * val * (s * (1.0 + gate * (1.0 - s)))
        dc = jnp.concatenate([dval, dgate], axis=-1)
        dc_ref[...] = dc.astype(dc_ref.dtype)

        @pl.when(pl.program_id(0) == 0)
        def _():
            dw_ref[...] = jnp.zeros_like(dw_ref)
            db_ref[...] = jnp.zeros_like(db_ref)

        dw_ref[0:1, :] += jnp.sum(dc * x, axis=0, keepdims=True)
        dw_ref[1:2, :] += jnp.sum(dc * x1, axis=0, keepdims=True)
        dw_ref[2:3, :] += jnp.sum(dc * x2, axis=0, keepdims=True)
        db_ref[...] += jnp.sum(dc, axis=0, keepdims=True)

    return _pallas(
        body, name=name, grid=(T // tm,),
        in_specs=_conv_specs(T, F2, tm) + [_rows(tm, F), _whole((3, F2)), _whole((1, F2))],
        out_specs=[_rows(tm, F2), _whole((3, F2)), _whole((1, F2))],
        out_shape=[jax.ShapeDtypeStruct((T, F2), MXU_DTYPE), jax.ShapeDtypeStruct((3, F2), F32),
                   jax.ShapeDtypeStruct((1, F2), F32)],
        compiler_params=_params(("arbitrary",)),
    )(hc, hc, da, conv_w, conv_b)


def _conv_bwd_input(dc, conv_w, L, *, name):
    T, F2 = dc.shape
    tm = CONV_ROWS
    edge = CONV_EDGE
    last_blk = T // edge - 1

    def body(x_ref, e_ref, w_ref, o_ref):
        at_end = ((pl.program_id(0) + 1) * tm) % L == 0
        x1, x2 = _row_shifts(x_ref[...], e_ref[...], at_end, tm, False)
        x = x_ref[...].astype(F32)
        o_ref[...] = (w_ref[0:1, :] * x + w_ref[1:2, :] * x1 + w_ref[2:3, :] * x2).astype(o_ref.dtype)

    return _pallas(
        body, name=name, grid=(T // tm,),
        in_specs=[_rows(tm, F2),
                  pl.BlockSpec((edge, F2), lambda i: (jnp.minimum((i + 1) * (tm // edge), last_blk), 0)),
                  _whole((3, F2))],
        out_specs=_rows(tm, F2),
        out_shape=jax.ShapeDtypeStruct((T, F2), MXU_DTYPE), compiler_params=_params(("parallel",)),
    )(dc, dc, conv_w)


S5_CHUNK = 128
LANES = 128


def _slab_rows(c, n, ncl):
    return pl.ds(c, n) if ncl == 1 else pl.ds(c, n, stride=ncl)


def _slab_put(ref, c, n, ncl, val):
    for s in range(val.shape[1] // LANES):
        ref[s, _slab_rows(c, n, ncl), :] = val[:, s * LANES:(s + 1) * LANES]


def _slab_get(ref, c, n, ncl):
    return jnp.concatenate([ref[s, _slab_rows(c, n, ncl), :] for s in range(ref.shape[0])], axis=-1)


def _slabs(n_slab, rows):
    return pl.BlockSpec((n_slab, rows, LANES), lambda i: (0, i, 0))


def _s5_fwd(xi, wb, wc, a_r, a_i, d_row, B, *, name):
    T, D = xi.shape
    ncl = wb.shape[0]
    cs = wb.shape[2] // 2
    ns = cs // LANES
    R = B * ncl
    Q = S5_CHUNK
    QR = Q * ncl
    nsteps = Q // B

    def body(x_ref, wb_ref, wc_ref, ar_ref, ai_ref, d_ref, y_ref, yg_ref, hr_ref, hi_ref, bur, bui, cr, ci):
        @pl.when(pl.program_id(0) == 0)
        def _():
            cr[...] = jnp.zeros_like(cr)
            ci[...] = jnp.zeros_like(ci)

        x = x_ref[...]
        xb = x.astype(MXU_DTYPE)
        for c in range(ncl):
            bu = jnp.dot(xb[:, c * CLUSTER_W:(c + 1) * CLUSTER_W], wb_ref[c], preferred_element_type=F32)
            _slab_put(bur, c, Q, ncl, bu[:, :cs])
            _slab_put(bui, c, Q, ncl, bu[:, cs:])
        ar = ar_ref[...]
        ai = ai_ref[...]

        def step(k, carry):
            hr, hi = carry
            sl = pl.ds(pl.multiple_of(k * R, R), R)
            nr = ar * hr - ai * hi + bur[:, sl, :]
            ni = ar * hi + ai * hr + bui[:, sl, :]
            hr_ref[:, sl, :] = nr
            hi_ref[:, sl, :] = ni
            return nr, ni

        hr, hi = lax.fori_loop(0, nsteps, step, (cr[...], ci[...]), unroll=4)
        cr[...] = hr
        ci[...] = hi
        parts = []
        for c in range(ncl):
            hrc = _slab_get(hr_ref, c, Q, ncl).astype(MXU_DTYPE)
            hic = _slab_get(hi_ref, c, Q, ncl).astype(MXU_DTYPE)
            parts.append(jnp.dot(hrc, wc_ref[c, :cs, :], preferred_element_type=F32)
                         + jnp.dot(hic, wc_ref[c, cs:, :], preferred_element_type=F32))
        y = d_ref[...] * x + (parts[0] if ncl == 1 else jnp.concatenate(parts, axis=-1))
        y_ref[...] = y
        yg_ref[...] = _gelu(y).astype(yg_ref.dtype)

    return _pallas(
        body, name=name, grid=(T // Q,),
        in_specs=[_rows(Q, D), _whole(wb.shape), _whole(wc.shape), _whole((ns, R, LANES)), _whole((ns, R, LANES)),
                  _whole((1, D))],
        out_specs=[_rows(Q, D), _rows(Q, D), _slabs(ns, QR), _slabs(ns, QR)],
        out_shape=[jax.ShapeDtypeStruct((T, D), F32), jax.ShapeDtypeStruct((T, D), MXU_DTYPE),
                   jax.ShapeDtypeStruct((ns, T * ncl, LANES), F32), jax.ShapeDtypeStruct((ns, T * ncl, LANES), F32)],
        scratch_shapes=[pltpu.VMEM((ns, QR, LANES), F32), pltpu.VMEM((ns, QR, LANES), F32),
                        pltpu.VMEM((ns, R, LANES), F32), pltpu.VMEM((ns, R, LANES), F32)],
        compiler_params=_params(("arbitrary",)),
    )(xi, wb, wc, a_r, a_i, d_row)


def _s5_bwd(dy, xi, h_r, h_i, wb, wc, a_r, a_i, d_row, B, *, name):
    T, D = dy.shape
    ncl = wb.shape[0]
    cs = wb.shape[2] // 2
    ns = cs // LANES
    R = B * ncl
    Q = S5_CHUNK
    nsteps = Q // B
    nchunk = T // Q
    QR = Q * ncl

    def rev(i):
        return nchunk - 1 - i

    def body(dy_ref, x_ref, hr_ref, hi_ref, pr_ref, pi_ref, wb_ref, wc_ref, ar_ref, ai_ref, d_ref,
             du_ref, gr_ref, gi_ref, dar_ref, dai_ref, dd_ref, dhr, dhi, cr, ci):
        i = pl.program_id(0)

        @pl.when(i == 0)
        def _():
            cr[...] = jnp.zeros_like(cr)
            ci[...] = jnp.zeros_like(ci)
            dar_ref[...] = jnp.zeros_like(dar_ref)
            dai_ref[...] = jnp.zeros_like(dai_ref)
            dd_ref[...] = jnp.zeros_like(dd_ref)

        dyv = dy_ref[...]
        dyb = dyv.astype(MXU_DTYPE)
        for c in range(ncl):
            dh = lax.dot_general(dyb[:, c * CLUSTER_W:(c + 1) * CLUSTER_W], wc_ref[c],
                                 (((1,), (1,)), ((), ())), preferred_element_type=F32)
            _slab_put(dhr, c, Q, ncl, dh[:, :cs])
            _slab_put(dhi, c, Q, ncl, dh[:, cs:])
        ar = ar_ref[...]
        ai = ai_ref[...]

        def step(j, carry):
            gr, gi, sar, sai = carry
            k = nsteps - 1 - j
            sl = pl.ds(pl.multiple_of(k * R, R), R)
            ngr = dhr[:, sl, :] + ar * gr + ai * gi
            ngi = dhi[:, sl, :] - ai * gr + ar * gi
            gr_ref[:, sl, :] = ngr
            gi_ref[:, sl, :] = ngi
            pv = pl.ds(pl.multiple_of((k - 1) * R, R), R)
            hpr = hr_ref[:, pv, :]
            hpi = hi_ref[:, pv, :]
            return ngr, ngi, sar + ngr * hpr + ngi * hpi, sai - ngr * hpi + ngi * hpr

        gr, gi, sar, sai = lax.fori_loop(0, nsteps - 1, step, (cr[...], ci[...], dar_ref[...], dai_ref[...]), unroll=4)
        sl0 = pl.ds(0, R)
        ngr = dhr[:, sl0, :] + ar * gr + ai * gi
        ngi = dhi[:, sl0, :] - ai * gr + ar * gi
        gr_ref[:, sl0, :] = ngr
        gi_ref[:, sl0, :] = ngi
        keep = jnp.where(i == nchunk - 1, 0.0, 1.0)
        hpr = pr_ref[:, 8 - R:8, :] * keep
        hpi = pi_ref[:, 8 - R:8, :] * keep
        dar_ref[...] = sar + ngr * hpr + ngi * hpi
        dai_ref[...] = sai - ngr * hpi + ngi * hpr
        cr[...] = ngr
        ci[...] = ngi
        parts = []
        for c in range(ncl):
            grc = _slab_get(gr_ref, c, Q, ncl).astype(MXU_DTYPE)
            gic = _slab_get(gi_ref, c, Q, ncl).astype(MXU_DTYPE)
            parts.append(lax.dot_general(grc, wb_ref[c, :, :cs], (((1,), (1,)), ((), ())), preferred_element_type=F32)
                         + lax.dot_general(gic, wb_ref[c, :, cs:], (((1,), (1,)), ((), ())), preferred_element_type=F32))
        du_ref[...] = d_ref[...] * dyv + (parts[0] if ncl == 1 else jnp.concatenate(parts, axis=-1))
        dd_ref[...] += jnp.sum(dyv * x_ref[...], axis=0, keepdims=True)

    tok = pl.BlockSpec((Q, D), lambda i: (rev(i), 0))
    st = pl.BlockSpec((ns, QR, LANES), lambda i: (0, rev(i), 0))
    before = pl.BlockSpec((ns, 8, LANES), lambda i: (0, jnp.maximum(rev(i) * (QR // 8) - 1, 0), 0))
    acc = _whole((ns, R, LANES))
    return _pallas(
        body, name=name, grid=(nchunk,),
        in_specs=[tok, tok, st, st, before, before, _whole(wb.shape), _whole(wc.shape), acc, acc, _whole((1, D))],
        out_specs=[tok, st, st, acc, acc, _whole((1, D))],
        out_shape=[jax.ShapeDtypeStruct((T, D), F32),
                   jax.ShapeDtypeStruct((ns, T * ncl, LANES), F32), jax.ShapeDtypeStruct((ns, T * ncl, LANES), F32),
                   jax.ShapeDtypeStruct((ns, R, LANES), F32), jax.ShapeDtypeStruct((ns, R, LANES), F32),
                   jax.ShapeDtypeStruct((1, D), F32)],
        scratch_shapes=[pltpu.VMEM((ns, QR, LANES), F32)] * 2 + [pltpu.VMEM((ns, R, LANES), F32)] * 2,
        compiler_params=_params(("arbitrary",)),
    )(dy, xi, h_r, h_i, h_r, h_i, wb, wc, a_r, a_i, d_row)


def _cluster_tn(tok, st, ncl, *, tok_left, name):
    T = tok.shape[0]
    ns = st.shape[0]
    cs = ns * LANES
    tt = _pick(T, (512, 256, 128))
    nt = T // tt
    oshape = (ncl, CLUSTER_W, cs) if tok_left else (ncl, cs, CLUSTER_W)

    def body(tok_ref, st_ref, o_ref, acc):
        t = pl.program_id(0)

        @pl.when(t == 0)
        def _():
            acc[...] = jnp.zeros_like(acc)

        tk = tok_ref[...].astype(MXU_DTYPE)
        for c in range(ncl):
            tc = tk[:, c * CLUSTER_W:(c + 1) * CLUSTER_W]
            sc = _slab_get(st_ref, c, tt, ncl).astype(MXU_DTYPE)
            lhs, rhs = (tc, sc) if tok_left else (sc, tc)
            acc[c] += lax.dot_general(lhs, rhs, (((0,), (0,)), ((), ())), preferred_element_type=F32)

        @pl.when(t == nt - 1)
        def _():
            o_ref[...] = acc[...]

    return _pallas(
        body, name=name, grid=(nt,),
        in_specs=[_rows(tt, tok.shape[1]), _slabs(ns, tt * ncl)],
        out_specs=_whole(oshape),
        out_shape=jax.ShapeDtypeStruct(oshape, F32),
        scratch_shapes=[pltpu.VMEM(oshape, F32)],
        compiler_params=_params(("arbitrary",)),
    )(tok, st)


def _s5_discretize(lam_re, lam_im, log_dt, b_re, b_im):
    dt = jnp.exp(log_dt)[:, None]
    mag = jnp.exp(lam_re * dt)
    ab_r, ab_i = mag * jnp.cos(lam_im * dt), mag * jnp.sin(lam_im * dt)
    den = lam_re * lam_re + lam_im * lam_im
    nr = ab_r - 1.0
    co_r = (nr * lam_re + ab_i * lam_im) / den
    co_i = (ab_i * lam_re - nr * lam_im) / den
    bb_r = co_r[..., None] * b_re - co_i[..., None] * b_im
    bb_i = co_r[..., None] * b_im + co_i[..., None] * b_re
    return ab_r, ab_i, bb_r, bb_i


def _blockdiag(m):
    G, r, k = m.shape
    ncl = G // GROUPS_PER_CLUSTER
    m4 = m.reshape(ncl, GROUPS_PER_CLUSTER, r, k)
    eye = jnp.eye(GROUPS_PER_CLUSTER, dtype=m.dtype)
    return jnp.einsum('cgrk,gh->cgrhk', m4, eye).reshape(ncl, GROUPS_PER_CLUSTER * r, GROUPS_PER_CLUSTER * k)


def _unblockdiag(m, r, k):
    ncl = m.shape[0]
    m5 = m.reshape(ncl, GROUPS_PER_CLUSTER, r, GROUPS_PER_CLUSTER, k)
    eye = jnp.eye(GROUPS_PER_CLUSTER, dtype=m.dtype)
    return jnp.einsum('cgrhk,gh->cgrk', m5, eye).reshape(ncl * GROUPS_PER_CLUSTER, r, k)


def _t5_bucket(dist):
    exact = REL_BUCKETS // 2
    d = np.maximum(dist, 1).astype(np.float32)
    large = exact + (np.log(d / exact) / math.log(REL_MAX_DIST / exact) * (REL_BUCKETS - exact)).astype(np.int64)
    large = np.minimum(large, REL_BUCKETS - 1)
    return np.where(dist < exact, dist, large).astype(np.int32)


def _band_tables(dil):
    steps = np.arange(BAND)[:, None] + BAND - np.arange(2 * BAND)[None, :]
    bucket = _t5_bucket(np.maximum(steps, 0) * dil)
    in_band = (steps >= 0) & (steps <= BAND)
    return bucket, in_band


def _attn_bias(rel_bias, hpg):
    out = []
    for g, dil in enumerate(DILATIONS):
        bucket, in_band = _band_tables(dil)
        cols = rel_bias[:, g * hpg:(g + 1) * hpg].astype(F32)
        onehot = jnp.asarray((bucket.reshape(-1, 1) == np.arange(REL_BUCKETS)[None, :]).astype(np.float32))
        bias = jnp.dot(onehot, cols, precision=lax.Precision.HIGHEST).T.reshape(hpg, BAND, 2 * BAND)
        out.append(jnp.where(jnp.asarray(in_band)[None], bias, NEG_BIG))
    return jnp.concatenate(out, axis=0)


def _attn_blocks(dil, L):
    M = L // dil
    return M, M // BAND


def _row_sel(r, M, dil):
    return pl.ds(r, M) if dil == 1 else pl.ds(r, M, stride=dil)


def _attn_fwd(q, kv, bias, L, hpg, *, name):
    T = q.shape[0]
    nb_ = T // L
    HP = hpg // 2
    W3 = 3 * hpg * HEAD_DIM
    mmax = L

    def group_body(dil, q_ref, k_ref, v_ref, b_ref, o_ref, l_ref, os, ls):
        M, NB = _attn_blocks(dil, L)
        for r in range(dil):
            rows = _row_sel(r, M, dil)
            first = lax.broadcasted_iota(jnp.int32, (1, 2 * HEAD_DIM), 1) < HEAD_DIM
            qf = q_ref[rows, :] * 0.125
            qm = [jnp.where(first, qf, 0.0).astype(MXU_DTYPE), jnp.where(first, 0.0, qf).astype(MXU_DTYPE)]
            kr = k_ref[rows, :].astype(MXU_DTYPE)
            va = jnp.concatenate([v_ref[rows, :].astype(MXU_DTYPE), jnp.ones((M, 2 * HEAD_DIM), MXU_DTYPE)], axis=-1)
            for n in range(NB):
                qs = slice(n * BAND, (n + 1) * BAND)
                ks = slice(0, BAND) if n == 0 else slice((n - 1) * BAND, (n + 1) * BAND)
                o_h, l_h = [], []
                for hh in range(2):
                    bb = b_ref[hh, :, BAND:] if n == 0 else b_ref[hh]
                    s = lax.dot_general(qm[hh][qs, :], kr[ks, :], (((1,), (1,)), ((), ())),
                                        preferred_element_type=F32) + bb
                    m = jnp.max(s, axis=-1, keepdims=True)
                    p = jnp.exp(s - m)
                    pv = jnp.dot(p.astype(MXU_DTYPE), va[ks, :], preferred_element_type=F32)
                    l = pv[:, 2 * HEAD_DIM:]
                    o_h.append(pv[:, :2 * HEAD_DIM] / l)
                    l_h.append(m + jnp.log(l))
                os[qs, :] = jnp.where(first, o_h[0], o_h[1])
                ls[qs, :] = jnp.where(first, l_h[0], l_h[1])
            o_ref[rows, :] = os[0:M, :]
            l_ref[rows, :] = ls[0:M, :]

    def body(q_ref, k_ref, v_ref, b_ref, o_ref, l_ref, os, ls):
        g = pl.program_id(0)
        for gi, dil in enumerate(DILATIONS):
            pl.when(g == gi)(functools.partial(group_body, dil, q_ref, k_ref, v_ref, b_ref, o_ref, l_ref, os, ls))

    blk = (L, 2 * HEAD_DIM)
    return _pallas(
        body, name=name, grid=(3, nb_, HP),
        in_specs=[pl.BlockSpec(blk, lambda g, b, h: (b, g * HP + h)),
                  pl.BlockSpec(blk, lambda g, b, h: (b, g * HP + h)),
                  pl.BlockSpec(blk, lambda g, b, h: (b, 3 * HP + g * HP + h)),
                  pl.BlockSpec((2, BAND, 2 * BAND), lambda g, b, h: (g * HP + h, 0, 0))],
        out_specs=[pl.BlockSpec(blk, lambda g, b, h: (b, g * HP + h)),
                   pl.BlockSpec(blk, lambda g, b, h: (b, g * HP + h))],
        out_shape=[jax.ShapeDtypeStruct((T, W3), F32), jax.ShapeDtypeStruct((T, W3), F32)],
        scratch_shapes=[pltpu.VMEM((mmax, 2 * HEAD_DIM), F32), pltpu.VMEM((mmax, 2 * HEAD_DIM), F32)],
        compiler_params=_params(("arbitrary", "arbitrary", "arbitrary")),
    )(q, kv, kv, bias)


def _attn_merge(o3, l3, hw, *, name):
    T = o3.shape[0]
    tm = _pick(T, (256, 128))

    def body(o0, o1, o2, l0, l1, l2, o_ref, ob_ref, lse_ref):
        a0, a1, a2 = l0[...], l1[...], l2[...]
        m = jnp.maximum(jnp.maximum(a0, a1), a2)
        e0, e1, e2 = jnp.exp(a0 - m), jnp.exp(a1 - m), jnp.exp(a2 - m)
        z = e0 + e1 + e2
        o = (e0 * o0[...] + e1 * o1[...] + e2 * o2[...]) / z
        o_ref[...] = o
        ob_ref[...] = o.astype(ob_ref.dtype)
        lse_ref[...] = m + jnp.log(z)

    def col(g):
        return pl.BlockSpec((tm, hw), lambda i: (i, g))

    return _pallas(
        body, name=name, grid=(T // tm,),
        in_specs=[col(0), col(1), col(2), col(0), col(1), col(2)],
        out_specs=[_rows(tm, hw)] * 3,
        out_shape=[jax.ShapeDtypeStruct((T, hw), F32), jax.ShapeDtypeStruct((T, hw), MXU_DTYPE),
                   jax.ShapeDtypeStruct((T, hw), F32)],
        compiler_params=_params(("parallel",)),
    )(o3, o3, o3, l3, l3, l3)


def _attn_bwd(q, kv, do, o, lse, bias, L, hpg, *, name):
    T = q.shape[0]
    nb_ = T // L
    HP = hpg // 2
    W3 = 3 * hpg * HEAD_DIM
    mmax = L

    def group_body(dil, q_ref, k_ref, v_ref, do_ref, o_ref, l_ref, b_ref, dq_ref, dk_ref, dv_ref, ds_ref,
                   dqs, dks, dvs):
        M, NB = _attn_blocks(dil, L)
        for r in range(dil):
            rows = _row_sel(r, M, dil)
            first = lax.broadcasted_iota(jnp.int32, (1, 2 * HEAD_DIM), 1) < HEAD_DIM
            qf = q_ref[rows, :] * 0.125
            qm = [jnp.where(first, qf, 0.0).astype(MXU_DTYPE), jnp.where(first, 0.0, qf).astype(MXU_DTYPE)]
            kr = k_ref[rows, :].astype(MXU_DTYPE)
            vr = v_ref[rows, :].astype(MXU_DTYPE)
            dof = do_ref[rows, :]
            dom = [jnp.where(first, dof, 0.0).astype(MXU_DTYPE), jnp.where(first, 0.0, dof).astype(MXU_DTYPE)]
            dod = dof * o_ref[rows, :]
            delta = [jnp.sum(jnp.where(first, dod, 0.0), axis=-1, keepdims=True),
                     jnp.sum(jnp.where(first, 0.0, dod), axis=-1, keepdims=True)]
            lr = l_ref[rows, :]
            lse = [lr[:, 0:1], lr[:, HEAD_DIM:HEAD_DIM + 1]]
            dks[0:M, :] = jnp.zeros((M, 2 * HEAD_DIM), F32)
            dvs[0:M, :] = jnp.zeros((M, 2 * HEAD_DIM), F32)
            for n in range(NB):
                qs = slice(n * BAND, (n + 1) * BAND)
                ks = slice(0, BAND) if n == 0 else slice((n - 1) * BAND, (n + 1) * BAND)
                dq_h = []
                dkc = dvc = None
                for hh in range(2):
                    bb = b_ref[hh, :, BAND:] if n == 0 else b_ref[hh]
                    qb, dob = qm[hh][qs, :], dom[hh][qs, :]
                    s = lax.dot_general(qb, kr[ks, :], (((1,), (1,)), ((), ())), preferred_element_type=F32) + bb
                    p = jnp.exp(s - lse[hh][qs, :])
                    dp = lax.dot_general(dob, vr[ks, :], (((1,), (1,)), ((), ())), preferred_element_type=F32)
                    ds = p * (dp - delta[hh][qs, :])
                    if n == 0:
                        ds_ref[hh, :, BAND:] += ds
                    else:
                        ds_ref[hh] += ds
                    dsm = ds.astype(MXU_DTYPE)
                    dq_h.append(jnp.dot(dsm, kr[ks, :], preferred_element_type=F32))
                    dk1 = lax.dot_general(dsm, qb, (((0,), (0,)), ((), ())), preferred_element_type=F32)
                    dv1 = lax.dot_general(p.astype(MXU_DTYPE), dob, (((0,), (0,)), ((), ())), preferred_element_type=F32)
                    dkc = dk1 if dkc is None else dkc + dk1
                    dvc = dv1 if dvc is None else dvc + dv1
                dqs[qs, :] = jnp.where(first, dq_h[0], dq_h[1]) * 0.125
                dks[ks, :] += dkc
                dvs[ks, :] += dvc
            dq_ref[rows, :] = dqs[0:M, :]
            dk_ref[rows, :] = dks[0:M, :]
            dv_ref[rows, :] = dvs[0:M, :]

    def body(q_ref, k_ref, v_ref, do_ref, o_ref, l_ref, b_ref, dq_ref, dk_ref, dv_ref, ds_ref, dqs, dks, dvs):
        g = pl.program_id(0)

        @pl.when(pl.program_id(2) == 0)
        def _():
            ds_ref[...] = jnp.zeros_like(ds_ref)

        for gi, dil in enumerate(DILATIONS):
            pl.when(g == gi)(functools.partial(group_body, dil, q_ref, k_ref, v_ref, do_ref, o_ref, l_ref, b_ref,
                                               dq_ref, dk_ref, dv_ref, ds_ref, dqs, dks, dvs))

    blk = (L, 2 * HEAD_DIM)
    gcol = lambda g, h, b: (b, g * HP + h)
    hcol = lambda g, h, b: (b, h)
    return _pallas(
        body, name=name, grid=(3, HP, nb_),
        in_specs=[pl.BlockSpec(blk, gcol), pl.BlockSpec(blk, gcol),
                  pl.BlockSpec(blk, lambda g, h, b: (b, 3 * HP + g * HP + h)),
                  pl.BlockSpec(blk, hcol), pl.BlockSpec(blk, hcol), pl.BlockSpec(blk, hcol),
                  pl.BlockSpec((2, BAND, 2 * BAND), lambda g, h, b: (g * HP + h, 0, 0))],
        out_specs=[pl.BlockSpec(blk, gcol), pl.BlockSpec(blk, gcol), pl.BlockSpec(blk, gcol),
                   pl.BlockSpec((2, BAND, 2 * BAND), lambda g, h, b: (g * HP + h, 0, 0))],
        out_shape=[jax.ShapeDtypeStruct((T, W3), F32), jax.ShapeDtypeStruct((T, W3), F32),
                   jax.ShapeDtypeStruct((T, W3), F32), jax.ShapeDtypeStruct((3 * hpg, BAND, 2 * BAND), F32)],
        scratch_shapes=[pltpu.VMEM((mmax, 2 * HEAD_DIM), F32)] * 3,
        compiler_params=_params(("arbitrary", "arbitrary", "arbitrary")),
    )(q, kv, kv, do, o, lse, bias)


def _bias_grad(ds_sum, hpg, *, name):
    nh = ds_sum.shape[0]
    idx = np.stack([np.where(_band_tables(dil)[1], _band_tables(dil)[0], -1) for dil in DILATIONS]).astype(np.int32)

    def body(ds_ref, idx_ref, o_ref):
        d = ds_ref[...]
        ix = idx_ref[...]
        lane = lax.broadcasted_iota(jnp.int32, (8, 128), 1)
        row = jnp.zeros((8, 128), F32)
        for b in range(REL_BUCKETS):
            row = row + jnp.where(lane == b, jnp.sum(jnp.where(ix == b, d, 0.0)), 0.0)
        o_ref[...] = row

    out = _pallas(
        body, name=name, grid=(nh,),
        in_specs=[pl.BlockSpec((None, BAND, 2 * BAND), lambda h: (h, 0, 0)),
                  pl.BlockSpec((None, BAND, 2 * BAND), lambda h: (h // hpg, 0, 0))],
        out_specs=pl.BlockSpec((None, 8, 128), lambda h: (h, 0, 0)),
        out_shape=jax.ShapeDtypeStruct((nh, 8, 128), F32),
        compiler_params=_params(("parallel",)),
    )(ds_sum, jnp.asarray(idx))
    return out[:, 0, :REL_BUCKETS].T


def _adamw(w, g, m, v, *, name):
    Rw, C = w.shape
    tm = _pick(Rw, (512, 352, 256, 128, 64, 32, 16, 8))

    def body(w_ref, g_ref, m_ref, v_ref, d_ref, nm_ref, nv_ref):
        gg = g_ref[...]
        nm = ADAM_B1 * m_ref[...] + (1.0 - ADAM_B1) * gg
        nv = ADAM_B2 * v_ref[...] + (1.0 - ADAM_B2) * (gg * gg)
        m_hat = nm / (1.0 - ADAM_B1 ** ADAM_STEP)
        v_hat = nv / (1.0 - ADAM_B2 ** ADAM_STEP)
        d_ref[...] = -ADAM_LR * (m_hat / (jnp.sqrt(v_hat) + ADAM_EPS) + ADAM_WD * w_ref[...])
        nm_ref[...] = nm
        nv_ref[...] = nv

    return _pallas(
        body, name=name, grid=(Rw // tm,), in_specs=[_rows(tm, C)] * 4, out_specs=[_rows(tm, C)] * 3,
        out_shape=[jax.ShapeDtypeStruct((Rw, C), F32)] * 3, compiler_params=_params(("parallel",)),
    )(w, g, m, v)


ROW_TILE_ELEMS = 256 * 1024


def _tile_rows(r, c):
    best = 8
    for t in range(8, r + 1, 8):
        if r % t == 0 and t * c <= ROW_TILE_ELEMS:
            best = t
    return best


def _adamw_halves(w, m, v, mine, other, cidx, *, layer=0, prev=None, name):
    NL, _, r, c = w.shape
    tm = _tile_rows(r, c)

    def body(c_ref, w_ref, m_ref, v_ref, a_ref, b_ref, *rest):
        g_ref, d_ref, nm_ref, nv_ref = rest[-4:]
        gg = jnp.where(pl.program_id(0) == c_ref[0], a_ref[...], b_ref[...])
        nm = ADAM_B1 * m_ref[...] + (1.0 - ADAM_B1) * gg
        nv = ADAM_B2 * v_ref[...] + (1.0 - ADAM_B2) * (gg * gg)
        m_hat = nm / (1.0 - ADAM_B1 ** ADAM_STEP)
        v_hat = nv / (1.0 - ADAM_B2 ** ADAM_STEP)
        g_ref[...] = gg
        d_ref[...] = -ADAM_LR * (m_hat / (jnp.sqrt(v_hat) + ADAM_EPS) + ADAM_WD * w_ref[...])
        nm_ref[...] = nm
        nv_ref[...] = nv

    half = pl.BlockSpec((None, None, tm, c), lambda h, i, cr: (layer, h, i, 0))
    one = pl.BlockSpec((None, tm, c), lambda h, i, cr: (0, i, 0))
    in_specs = [half, half, half, one, one]
    args = [cidx, w, m, v, mine, other]
    aliases = {}
    if prev is not None:
        in_specs += [_ANY] * 4
        args += list(prev)
        aliases = {6 + k: k for k in range(4)}
    spec = pltpu.PrefetchScalarGridSpec(num_scalar_prefetch=1, grid=(2, r // tm), in_specs=in_specs, out_specs=[half] * 4)
    return _pallas(
        body, name=name, grid_spec=spec, out_shape=[jax.ShapeDtypeStruct((NL, 2, r, c), F32)] * 4,
        input_output_aliases=aliases, compiler_params=_params(("parallel", "parallel")),
    )(*args)


def _pair_sum(g, theirs, cidx, *, cast, name):
    _, _, r, c = g.shape
    tm = _tile_rows(r, c)

    def body(c_ref, g_ref, t_ref, *outs):
        s = g_ref[...] + t_ref[...]
        outs[0][...] = s
        if cast:
            outs[1][...] = s.astype(BF16)

    blk = (None, None, tm, c)
    first = pl.BlockSpec(blk, lambda p, i, cr: (p, 0, i, 0))
    shapes = [jax.ShapeDtypeStruct((4, 1, r, c), F32)] + ([jax.ShapeDtypeStruct((4, 1, r, c), BF16)] if cast else [])
    spec = pltpu.PrefetchScalarGridSpec(
        num_scalar_prefetch=1, grid=(4, r // tm),
        in_specs=[pl.BlockSpec(blk, lambda p, i, cr: (p, cr[0], i, 0)), first], out_specs=[first] * len(shapes))
    return _pallas(body, name=name, grid_spec=spec, out_shape=shapes,
                   compiler_params=_params(("parallel", "parallel")))(cidx, g, theirs)


def _chip_sum(hf, got, chip_idx, *, name):
    _, _, r, c = hf.shape
    tm = _tile_rows(r, c)

    def body(p_ref, h_ref, r_ref, o_ref):
        s = h_ref[...]
        for k in range(3):
            s = s + r_ref[k].astype(F32)
        o_ref[...] = s

    spec = pltpu.PrefetchScalarGridSpec(
        num_scalar_prefetch=1, grid=(r // tm,),
        in_specs=[pl.BlockSpec((None, None, tm, c), lambda i, pr: (pr[0], 0, i, 0)),
                  pl.BlockSpec((3, None, tm, c), lambda i, pr: (0, 0, i, 0))],
        out_specs=pl.BlockSpec((None, tm, c), lambda i, pr: (0, i, 0)))
    return _pallas(body, name=name, grid_spec=spec, out_shape=jax.ShapeDtypeStruct((1, r, c), F32),
                   compiler_params=_params(("parallel",)))(chip_idx, hf, got)


def _place():
    x, y, c = lax.axis_index("x"), lax.axis_index("y"), lax.axis_index("c")
    chips = [(1 - x, y), (x, 1 - y), (1 - x, 1 - y)]
    return x, y, c, chips


_ANY = pl.BlockSpec(memory_space=pl.ANY)


def _comm_call(body, ins, out_shapes, n_remote, *, name, aliases=None):
    sems = [pltpu.SemaphoreType.DMA((n,)) for n in n_remote]
    return _pallas(
        body, name=name, in_specs=[_ANY] * len(ins), out_specs=[_ANY] * len(out_shapes), out_shape=out_shapes,
        scratch_shapes=sems, input_output_aliases=aliases or {},
        compiler_params=pltpu.CompilerParams(has_side_effects=True),
    )(*ins)


_HBM_SPEC = pl.BlockSpec(memory_space=pltpu.HBM)
_SEM_SPEC = pl.BlockSpec(memory_space=pltpu.SEMAPHORE)
_DATAFLOW = pltpu.SideEffectType.DATAFLOW_SIDE_EFFECTING


def _in_hbm(a):
    return pltpu.with_memory_space_constraint(a, pltpu.HBM)


def _gather_start(groups, *, name):
    flat = [s for g in groups for s in g]
    n, ng = len(flat), len(groups)

    def body(*refs):
        ins, lands = refs[:n], refs[n:2 * n]
        sems = refs[2 * n:2 * n + 2 * ng]
        token = refs[-1]
        x, y, c, chips = _place()
        me = 2 * x + y
        a = 0
        for gi, g in enumerate(groups):
            for j in range(len(g)):
                for k, (tx, ty) in enumerate(chips):
                    _rcopy(ins[a].at[c], lands[a].at[me, c], sems[2 * gi].at[3 * j + k], sems[2 * gi + 1].at[3 * j + k],
                           (tx, ty, c)).start()
                a += 1
        token[...] = jnp.zeros_like(token)

    land_shapes = [(4,) + s.shape for s in flat]
    out_shape = ([pltpu.SemaphoreType.DMA((3 * len(g),)) for g in groups for _ in range(2)]
                 + [pltpu.HBM(s.shape, s.dtype) for s in flat]
                 + [pltpu.HBM(ls, s.dtype) for ls, s in zip(land_shapes, flat)]
                 + [jax.ShapeDtypeStruct((8, 128), F32)])
    outs = _pallas(
        body, name=name, in_specs=[_HBM_SPEC] * (2 * n),
        out_specs=[_SEM_SPEC] * (2 * ng) + [_HBM_SPEC] * (2 * n) + [pl.BlockSpec(memory_space=pltpu.VMEM)],
        out_shape=out_shape, input_output_aliases={i: 2 * ng + i for i in range(2 * n)},
        compiler_params=pltpu.CompilerParams(has_side_effects=_DATAFLOW),
    )(*[_in_hbm(s) for s in flat], *[_in_hbm(lax.empty(ls, s.dtype)) for ls, s in zip(land_shapes, flat)])
    sems, thru, lands, token = outs[:2 * ng], outs[2 * ng:2 * ng + n], outs[2 * ng + n:2 * ng + 2 * n], outs[-1]
    res, a = [], 0
    for gi, g in enumerate(groups):
        res.append((sems[2 * gi], sems[2 * gi + 1], thru[a:a + len(g)], lands[a:a + len(g)]))
        a += len(g)
    return res, token


def _gather_wait(ssem, rsem, shards, lands, after, *, name):
    m = len(shards)

    def body(*refs):
        ins, lnd = refs[:m], refs[m:2 * m]
        ss, rs = refs[2 * m], refs[2 * m + 1]
        x, y, c, chips = _place()
        for j in range(m):
            for k, (tx, ty) in enumerate(chips):
                cp = _rcopy(ins[j].at[c], lnd[j].at[2 * tx + ty, c], ss.at[3 * j + k], rs.at[3 * j + k], (tx, ty, c))
                cp.wait_send()
                cp.wait_recv()

    outs = _pallas(
        body, name=name, in_specs=[_HBM_SPEC] * (2 * m) + [_SEM_SPEC, _SEM_SPEC, _ANY],
        out_specs=[_HBM_SPEC] * (2 * m),
        out_shape=[pltpu.HBM(s.shape, s.dtype) for s in shards] + [pltpu.HBM(l.shape, l.dtype) for l in lands],
        input_output_aliases={i: i for i in range(2 * m)},
        compiler_params=pltpu.CompilerParams(has_side_effects=_DATAFLOW),
    )(*shards, *lands, ssem, rsem, after)
    return outs[m:]


def _gather_forward(lands, *, name):
    n = len(lands)

    def body(*refs):
        outs = refs[n:2 * n]
        ssem, rsem = refs[2 * n:]
        x, y, c, chips = _place()
        sib = (x, y, 1 - c)
        cps = []
        for a in range(n):
            for k, (tx, ty) in enumerate(chips):
                pk = 2 * tx + ty
                cp = _rcopy(outs[a].at[pk, c], outs[a].at[pk, c], ssem.at[3 * a + k], rsem.at[3 * a + k], sib)
                cp.start()
                cps.append(cp)
        for a in range(n):
            for k, (tx, ty) in enumerate(chips):
                pk = 2 * tx + ty
                _rcopy(outs[a].at[pk, c], outs[a].at[pk, 1 - c], ssem.at[3 * a + k], rsem.at[3 * a + k], sib).wait_recv()
        for cp in cps:
            cp.wait_send()

    shapes = [jax.ShapeDtypeStruct(l.shape, l.dtype) for l in lands]
    return _comm_call(body, lands, shapes, [3 * n, 3 * n], name=name, aliases={i: i for i in range(n)})


class _Lazy:
    def __init__(self, group_of, make):
        self._group_of, self._make, self._done, self._anchor = group_of, make, {}, None

    def anchor(self, value):
        self._anchor = value

    def __getitem__(self, key):
        g = self._group_of[key]
        if g not in self._done:
            self._done[g] = self._make(g, self._anchor)
        return self._done[g][key]


def _anchor(mapping, value):
    if isinstance(mapping, _Lazy):
        mapping.anchor(value)


def _rcopy(src, dst, ssem, rsem, dev):
    return pltpu.make_async_remote_copy(src_ref=src, dst_ref=dst, send_sem=ssem, recv_sem=rsem,
                                        device_id=dev, device_id_type=MESH)


def _all_gather(shards, *, name):
    n = len(shards)

    def body(*refs):
        ins, outs = refs[:n], refs[n:2 * n]
        s_ici, r_ici, s_d2d, r_d2d = refs[2 * n:]
        x, y, c, chips = _place()
        me = 2 * x + y
        sib = (x, y, 1 - c)
        sends = []
        for a in range(n):
            for k, (tx, ty) in enumerate(chips):
                cp = _rcopy(ins[a].at[c], outs[a].at[me, c], s_ici.at[3 * a + k], r_ici.at[3 * a + k], (tx, ty, c))
                cp.start()
                sends.append(cp)
        for a in range(n):
            for k, (tx, ty) in enumerate(chips):
                pk = 2 * tx + ty
                _rcopy(ins[a].at[c], outs[a].at[pk, c], s_ici.at[3 * a + k], r_ici.at[3 * a + k], (tx, ty, c)).wait_recv()
                fw = _rcopy(outs[a].at[pk, c], outs[a].at[pk, c], s_d2d.at[3 * a + k], r_d2d.at[3 * a + k], sib)
                fw.start()
                sends.append(fw)
        for a in range(n):
            for k, (tx, ty) in enumerate(chips):
                pk = 2 * tx + ty
                _rcopy(ins[a].at[c], outs[a].at[pk, 1 - c], s_d2d.at[3 * a + k], r_d2d.at[3 * a + k], sib).wait_recv()
        for cp in sends:
            cp.wait_send()

    shapes = [jax.ShapeDtypeStruct((4,) + s.shape, s.dtype) for s in shards]
    return _comm_call(body, shards, shapes, [3 * n] * 4, name=name)


def _gather(shards, chip, *, name):
    outs = _all_gather(shards, name=name)
    return [lax.dynamic_update_slice(o, s[None], (chip, 0, 0, 0)) for o, s in zip(outs, shards)]


def _pair_send(gs, *, name):
    n = len(gs)

    def body(*refs):
        ins, theirs = refs[:n], refs[n:2 * n]
        ssem, rsem = refs[2 * n:]
        x, y, c, _ = _place()
        sib = (x, y, 1 - c)
        cps = []
        for a in range(n):
            cp = _rcopy(ins[a].at[:, pl.ds(1 - c, 1)], theirs[a], ssem.at[a], rsem.at[a], sib)
            cp.start()
            cps.append(cp)
        for cp in cps:
            cp.wait_send()
            cp.wait_recv()

    shapes = [jax.ShapeDtypeStruct((4, 1) + g.shape[2:], g.dtype) for g in gs]
    return _comm_call(body, gs, shapes, [n, n], name=name)


def _chip_exchange(hx, *, name):
    n = len(hx)

    def body(*refs):
        hxr, got = refs[:n], refs[n:2 * n]
        ssem, rsem = refs[2 * n:]
        x, y, c, chips = _place()
        cps = []
        for a in range(n):
            for k, (tx, ty) in enumerate(chips):
                cp = _rcopy(hxr[a].at[2 * tx + ty], got[a].at[k], ssem.at[3 * a + k], rsem.at[3 * a + k], (tx, ty, c))
                cp.start()
                cps.append(cp)
        for cp in cps:
            cp.wait_send()
            cp.wait_recv()

    shapes = [jax.ShapeDtypeStruct((3,) + h.shape[1:], h.dtype) for h in hx]
    return _comm_call(body, hx, shapes, [3 * n, 3 * n], name=name)


def _pair_swap(fs, *, name):
    n = len(fs)

    def body(*refs):
        ins, outs = refs[:n], refs[n:2 * n]
        ssem, rsem = refs[2 * n:]
        x, y, c, _ = _place()
        cps = []
        for a in range(n):
            cp = _rcopy(ins[a], outs[a], ssem.at[a], rsem.at[a], (x, y, 1 - c))
            cp.start()
            cps.append(cp)
        for cp in cps:
            cp.wait_send()
            cp.wait_recv()

    shapes = [jax.ShapeDtypeStruct(f.shape, f.dtype) for f in fs]
    return _comm_call(body, fs, shapes, [n, n], name=name)


def _chip_exchange_start(hx, *, name):
    n = len(hx)

    def body(*refs):
        ins, gots = refs[:n], refs[n:2 * n]
        ssem, rsem = refs[2 * n], refs[2 * n + 1]
        token = refs[-1]
        x, y, c, chips = _place()
        for a in range(n):
            for k, (tx, ty) in enumerate(chips):
                _rcopy(ins[a].at[2 * tx + ty], gots[a].at[k], ssem.at[3 * a + k], rsem.at[3 * a + k], (tx, ty, c)).start()
        token[...] = jnp.zeros_like(token)

    got_shapes = [(3,) + h.shape[1:] for h in hx]
    outs = _pallas(
        body, name=name, in_specs=[_HBM_SPEC] * (2 * n),
        out_specs=[_SEM_SPEC] * 2 + [_HBM_SPEC] * (2 * n) + [pl.BlockSpec(memory_space=pltpu.VMEM)],
        out_shape=([pltpu.SemaphoreType.DMA((3 * n,))] * 2 + [pltpu.HBM(h.shape, h.dtype) for h in hx]
                   + [pltpu.HBM(gs, h.dtype) for gs, h in zip(got_shapes, hx)] + [jax.ShapeDtypeStruct((8, 128), F32)]),
        input_output_aliases={i: 2 + i for i in range(2 * n)},
        compiler_params=pltpu.CompilerParams(has_side_effects=_DATAFLOW),
    )(*[_in_hbm(h) for h in hx], *[_in_hbm(lax.empty(gs, h.dtype)) for gs, h in zip(got_shapes, hx)])
    return (outs[0], outs[1], outs[2:2 + n], outs[2 + n:2 + 2 * n]), outs[-1]


def _chip_exchange_wait(started, after, *, name):
    ssem, rsem, hx, gots = started
    n = len(hx)

    def body(*refs):
        ins, gts = refs[:n], refs[n:2 * n]
        ss, rs = refs[2 * n], refs[2 * n + 1]
        x, y, c, chips = _place()
        for a in range(n):
            for k, (tx, ty) in enumerate(chips):
                cp = _rcopy(ins[a].at[2 * tx + ty], gts[a].at[k], ss.at[3 * a + k], rs.at[3 * a + k], (tx, ty, c))
                cp.wait_send()
                cp.wait_recv()

    outs = _pallas(
        body, name=name, in_specs=[_HBM_SPEC] * (2 * n) + [_SEM_SPEC, _SEM_SPEC, _ANY],
        out_specs=[_HBM_SPEC] * (2 * n),
        out_shape=[pltpu.HBM(h.shape, h.dtype) for h in hx] + [pltpu.HBM(g.shape, g.dtype) for g in gots],
        input_output_aliases={i: i for i in range(2 * n)},
        compiler_params=pltpu.CompilerParams(has_side_effects=_DATAFLOW),
    )(*hx, *gots, ssem, rsem, after)
    return outs[n:]


def _pair_sums(grads, exch_bf16, cidx, tag):
    theirs = _pair_send(grads, name=f"rs_pair_send_{tag}")
    hf, hx = [], []
    for a in range(len(grads)):
        res = _pair_sum(grads[a], theirs[a], cidx, cast=exch_bf16[a], name=f"rs_pair_sum_{tag}{a}")
        hf.append(res[0])
        hx.append(res[1] if exch_bf16[a] else res[0])
    return hf, hx


def _chip_sums(hf, got, chip_idx, tag):
    return [_chip_sum(hf[a], got[a], chip_idx, name=f"rs_chip_sum_{tag}{a}") for a in range(len(hf))]


def _interleave(a, B, L):
    return a.reshape(B, L, -1).transpose(1, 0, 2).reshape(B * L, -1)


def _deinterleave(a, B, L):
    return a.reshape(L, B, -1).transpose(1, 0, 2).reshape(B * L, -1)


def _local_step(x, tgt, W, S, on_grads=None):
    B, L, D = x.shape
    T = B * L
    G = D // SSM_GROUP
    Pst = SSM_STATE
    hpg = D // HEAD_DIM
    HW = hpg * HEAD_DIM
    ncl = G // GROUPS_PER_CLUSTER
    x2 = x.reshape(T, D)
    tgt2 = tgt.reshape(T, D)

    disc = lambda *p: _s5_discretize(*p)
    (ab_r, ab_i, bb_r, bb_i), disc_vjp = jax.vjp(disc, S["lam_re"], S["lam_im"], S["log_dt"], S["b_re"], S["b_im"])
    wb = jnp.concatenate([_blockdiag(jnp.transpose(bb_r, (0, 2, 1))), _blockdiag(jnp.transpose(bb_i, (0, 2, 1)))],
                         axis=-1).astype(MXU_DTYPE)
    wc = jnp.concatenate([_blockdiag(jnp.transpose(S["c_re"], (0, 2, 1))), _blockdiag(-jnp.transpose(S["c_im"], (0, 2, 1)))],
                         axis=1).astype(MXU_DTYPE)
    cs = GROUPS_PER_CLUSTER * Pst
    slab = lambda ab: jnp.tile(jnp.transpose(ab.reshape(ncl, cs // LANES, LANES), (1, 0, 2)), (1, B, 1))
    a_r, a_i = slab(ab_r), slab(ab_i)
    d_row = S["d"].reshape(1, D)

    xi = _interleave(x2, B, L)
    y, yg, h_r, h_i = _s5_fwd(xi, wb, wc, a_r, a_i, d_row, B, name="s5_fwd")
    _anchor(W, yg)
    z = _mm_nn(yg, W["w_glu"], bias=S["b_glu"].reshape(1, D), name="glu_z")
    gate = _glu_gate(y, z, name="glu_gate")
    mix = _deinterleave(_mm_nn(gate, W["w_out"], name="s5_out"), B, L)
    h1, h1b, xh1, rs1 = _ln_fwd(x2, mix, S["ln_gain"][0, 0][None], S["ln_bias"][0, 0][None], name="ln_fwd_0a")

    def ffn_fwd(hb, l):
        hc = _mm_nn(hb, W["w_up"], l=l, out_dtype=MXU_DTYPE, name=f"ffn_up_{l}")
        a = _conv_glu_fwd(hc, S["conv_w"][l], S["conv_b"][l][None], L, name=f"ffn_conv_{l}")
        f = _mm_nn(a, W["w_down"], l=l, name=f"ffn_down_{l}")
        return hc, a, f

    _anchor(W, h1b)
    hc0, a0, f0 = ffn_fwd(h1b, 0)
    h2, h2b, xh2, rs2 = _ln_fwd(h1, f0, S["ln_gain"][0, 1][None], S["ln_bias"][0, 1][None], name="ln_fwd_0b")

    _anchor(W, h2b)
    kv = _mm_nn(h2b, W["w_kv"], name="attn_kv")
    q = _mm_nn(h2b, W["w_q"], name="attn_q")
    bias = _attn_bias(S["rel_bias"], hpg)
    o3, l3 = _attn_fwd(q, kv, bias, L, hpg, name="attn_fwd")
    o, ob, lse = _attn_merge(o3, l3, HW, name="attn_merge")
    att = _mm_nn(ob, W["w_ao"], name="attn_out")
    h3, h3b, xh3, rs3 = _ln_fwd(h2, att, S["ln_gain"][1, 0][None], S["ln_bias"][1, 0][None], name="ln_fwd_1a")
    hc1, a1, f1 = ffn_fwd(h3b, 1)
    h4, _, xh4, rs4 = _ln_fwd(h3, f1, S["ln_gain"][1, 1][None], S["ln_bias"][1, 1][None], name="ln_fwd_1b")

    dh4, lrow = _loss_grad(h4, tgt2, name="loss")
    loss = lrow[0, 0]

    GW, GS = {}, {}

    def ffn_bwd(dzb, hb, hc, a, l):
        da = _mm_nt(dzb, W["w_down"], l=l, out_dtype=MXU_DTYPE, name=f"ffn_down_bwd_x_{l}")
        GW[f"w_down{l}"] = _tn(a, dzb, ptotal=1, np_cols=D, name=f"ffn_down_bwd_w_{l}")
        dc, dcw, dcb = _conv_glu_bwd(hc, da, S["conv_w"][l], S["conv_b"][l][None], L, name=f"ffn_conv_bwd_{l}")
        dhc = _conv_bwd_input(dc, S["conv_w"][l], L, name=f"ffn_conv_bwd_x_{l}")
        dh = _mm_nt(dhc, W["w_up"], l=l, name=f"ffn_up_bwd_x_{l}")
        GW[f"w_up{l}"] = _tn(hb, dhc, ptotal=W["w_up"].shape[0], np_cols=W["w_up"].shape[3], name=f"ffn_up_bwd_w_{l}")
        return dh, dcw, dcb

    dz4, dz4b, dg4, db4 = _ln_bwd([dh4], [1.0], xh4, rs4, S["ln_gain"][1, 1][None], name="ln_bwd_1b")
    dh3f, dcw1, dcb1 = ffn_bwd(dz4b, h3b, hc1, a1, 1)
    dz3, dz3b, dg3, db3 = _ln_bwd([dz4, dh3f], [DN_ALPHA, 1.0], xh3, rs3, S["ln_gain"][1, 0][None], name="ln_bwd_1a")
    do = _mm_nt(dz3b, W["w_ao"], name="attn_out_bwd_x")
    GW["w_ao"] = _tn(ob, dz3b, ptotal=1, np_cols=D, name="attn_out_bwd_w")
    dq, dk, dv, ds_sum = _attn_bwd(q, kv, do, o, lse, bias, L, hpg, name="attn_bwd")
    GS["rel_bias"] = _bias_grad(ds_sum, hpg, name="attn_bias_grad")
    GW["w_q"] = _tn(h2b, dq, ptotal=W["w_q"].shape[0], np_cols=W["w_q"].shape[3], name="attn_q_bwd_w")
    pkv, npkv = W["w_kv"].shape[0], W["w_kv"].shape[3]
    gkv = _tn(h2b, dk, ptotal=pkv, np_cols=npkv, p0=0, name="attn_k_bwd_w")
    GW["w_kv"] = _tn(h2b, dv, ptotal=pkv, np_cols=npkv, p0=pkv // 2, prev=gkv, name="attn_v_bwd_w")
    dh2q = _mm_nt(dq, W["w_q"], name="attn_q_bwd_x")
    dh2k = _mm_nt(dk, W["w_kv"], p0=0, pn=pkv // 2, name="attn_k_bwd_x")
    dh2v = _mm_nt(dv, W["w_kv"], p0=pkv // 2, pn=pkv // 2, name="attn_v_bwd_x")

    gain_0b = S["ln_gain"][0, 1][None]
    if on_grads is not None:
        gain_0b = gain_0b + on_grads(0, GW)[0, 0]

    dz2, dz2b, dg2, db2 = _ln_bwd([dz3, dh2q, dh2k, dh2v], [DN_ALPHA, 1.0, 1.0, 1.0], xh2, rs2, gain_0b,
                                  name="ln_bwd_0b")
    dh1f, dcw0, dcb0 = ffn_bwd(dz2b, h1b, hc0, a0, 0)
    gain_0a = S["ln_gain"][0, 0][None]
    if on_grads is not None:
        gain_0a = gain_0a + on_grads(1, GW)[0, 0]
    dz1, dz1b, dg1, db1 = _ln_bwd([dz2, dh1f], [DN_ALPHA, 1.0], xh1, rs1, gain_0a, name="ln_bwd_0a")
    dmix_i = _interleave(dz1b, B, L)
    dgate = _mm_nt(dmix_i, W["w_out"], name="s5_out_bwd_x")
    GW["w_out"] = _tn(gate, dmix_i, ptotal=1, np_cols=D, name="s5_out_bwd_w")
    dzg, dyg1, dbglu = _glu_bwd(y, z, dgate, name="glu_bwd")
    dyg2 = _mm_nt(dzg, W["w_glu"], name="glu_z_bwd_x")
    GW["w_glu"] = _tn(yg, dzg, ptotal=1, np_cols=D, name="glu_z_bwd_w")
    dy = _gelu_bwd(y, dyg1, dyg2, name="gelu_bwd")
    du_i, g_r, g_i, dar, dai, dd = _s5_bwd(dy, xi, h_r, h_i, wb, wc, a_r, a_i, d_row, B, name="s5_bwd")
    dwb_r = _cluster_tn(xi, g_r, ncl, tok_left=True, name="s5_b_grad_re")
    dwb_i = _cluster_tn(xi, g_i, ncl, tok_left=True, name="s5_b_grad_im")
    dwc_r = _cluster_tn(dy, h_r, ncl, tok_left=False, name="s5_c_grad_re")
    dwc_i = _cluster_tn(dy, h_i, ncl, tok_left=False, name="s5_c_grad_im")
    grad_x = _axpy(dz1, _deinterleave(du_i, B, L), DN_ALPHA, name="grad_x")

    dbb_r = jnp.transpose(_unblockdiag(dwb_r, SSM_GROUP, Pst), (0, 2, 1))
    dbb_i = jnp.transpose(_unblockdiag(dwb_i, SSM_GROUP, Pst), (0, 2, 1))
    unslab = lambda da: jnp.transpose(da.reshape(cs // LANES, B, ncl, LANES).sum(1), (1, 0, 2)).reshape(G, Pst)
    dab_r, dab_i = unslab(dar), unslab(dai)
    GS["lam_re"], GS["lam_im"], GS["log_dt"], GS["b_re"], GS["b_im"] = disc_vjp((dab_r, dab_i, dbb_r, dbb_i))
    GS["c_re"] = jnp.transpose(_unblockdiag(dwc_r, Pst, SSM_GROUP), (0, 2, 1))
    GS["c_im"] = -jnp.transpose(_unblockdiag(dwc_i, Pst, SSM_GROUP), (0, 2, 1))
    GS["d"] = dd.reshape(G, SSM_GROUP)
    GS["b_glu"] = dbglu.reshape(D)
    GS["conv_w"] = jnp.stack([dcw0, dcw1])
    GS["conv_b"] = jnp.stack([dcb0[0], dcb1[0]])
    GS["ln_gain"] = jnp.stack([jnp.stack([dg1[0], dg2[0]]), jnp.stack([dg3[0], dg4[0]])])
    GS["ln_bias"] = jnp.stack([jnp.stack([db1[0], db2[0]]), jnp.stack([db3[0], db4[0]])])
    return loss, grad_x.reshape(B, L, D), GW, GS


SMALL_REPLICATED = ("lam_re", "lam_im", "log_dt", "b_re", "b_im", "c_re", "c_im", "d", "rel_bias", "conv_b")
SMALL_SHARDED = ("b_glu", "conv_w", "ln_gain", "ln_bias")
SMALL_ORDER = SMALL_REPLICATED + SMALL_SHARDED


def _pack(arrs, lanes, row_mult):
    flat = jnp.concatenate([a.reshape(-1).astype(F32) for a in arrs])
    rows = -(-flat.shape[0] // lanes)
    rows = -(-rows // row_mult) * row_mult
    return jnp.pad(flat, (0, rows * lanes - flat.shape[0])).reshape(rows, lanes)


def _unpack(packed, shapes):
    flat = packed.reshape(-1)
    out, off = [], 0
    for s in shapes:
        n = int(np.prod(s))
        out.append(flat[off:off + n].reshape(s))
        off += n
    return out


def kernel(x, s5_lam_re, s5_lam_im, s5_log_dt, s5_b_re, s5_b_im, s5_c_re, s5_c_im, s5_d, s5_w_glu, s5_b_glu, s5_w_out, attn_w_kv, attn_w_q, attn_w_out, rel_bias, ffn_w_up, ffn_conv_w, ffn_conv_b, ffn_w_down, ln_gain, ln_bias, loss_target, m_s5_lam_re, m_s5_lam_im, m_s5_log_dt, m_s5_b_re, m_s5_b_im, m_s5_c_re, m_s5_c_im, m_s5_d, m_s5_w_glu, m_s5_b_glu, m_s5_w_out, m_attn_w_kv, m_attn_w_q, m_attn_w_out, m_rel_bias, m_ffn_w_up, m_ffn_conv_w, m_ffn_conv_b, m_ffn_w_down, m_ln_gain, m_ln_bias, v_s5_lam_re, v_s5_lam_im, v_s5_log_dt, v_s5_b_re, v_s5_b_im, v_s5_c_re, v_s5_c_im, v_s5_d, v_s5_w_glu, v_s5_b_glu, v_s5_w_out, v_attn_w_kv, v_attn_w_q, v_attn_w_out, v_rel_bias, v_ffn_w_up, v_ffn_conv_w, v_ffn_conv_b, v_ffn_w_down, v_ln_gain, v_ln_bias):
    names = ["s5_lam_re", "s5_lam_im", "s5_log_dt", "s5_b_re", "s5_b_im", "s5_c_re", "s5_c_im", "s5_d", "s5_w_glu",
             "s5_b_glu", "s5_w_out", "attn_w_kv", "attn_w_q", "attn_w_out", "rel_bias", "ffn_w_up", "ffn_conv_w",
             "ffn_conv_b", "ffn_w_down", "ln_gain", "ln_bias"]
    loc = locals()
    w_in = {n: loc[n] for n in names}
    m_in = {n: loc["m_" + n] for n in names}
    v_in = {n: loc["v_" + n] for n in names}
    chip = 2 * lax.axis_index("x") + lax.axis_index("y")
    core = lax.axis_index("c")
    chip_idx = jnp.reshape(chip, (1,)).astype(jnp.int32)
    cidx = jnp.reshape(core, (1,)).astype(jnp.int32)

    big = [("w_glu", "s5_w_glu", "rows"), ("w_out", "s5_w_out", "rows"), ("w_ao", "attn_w_out", "rows"),
           ("w_kv", "attn_w_kv", "cols"), ("w_q", "attn_w_q", "cols"),
           ("w_up", "ffn_w_up", "layer_cols"), ("w_down", "ffn_w_down", "layer_rows")]

    def halves(t, kind):
        if kind.startswith("layer"):
            return t
        r, c = t.shape[-2:]
        return t.reshape(2, r // 2, c)

    def to_weight(g, kind):
        _, _, r, c = g.shape
        if kind == "rows":
            return g.reshape(1, 1, 8 * r, c)
        if kind == "cols":
            return g.reshape(4, 1, 2 * r, c)
        if kind == "layer_cols":
            return g
        return jnp.transpose(g, (1, 0, 2, 3)).reshape(1, 2, 4 * r, c)

    small_sh = {"b_glu": s5_b_glu[0], "conv_w": ffn_conv_w, "ln_gain": ln_gain, "ln_bias": ln_bias}
    sh_shapes = [small_sh[k].shape for k in SMALL_SHARDED]
    sh_pack = _pack([small_sh[k] for k in SMALL_SHARDED], 128, 16)

    shards = [halves(w_in[src].astype(MXU_DTYPE), kind) for _, src, kind in big]
    shards.append(sh_pack.reshape(2, sh_pack.shape[0] // 2, 128))
    shard_of = {key: s for (key, _, _), s in zip(big, shards)}
    shard_of["small"] = shards[-1]
    kind_of = {key: kind for key, _, kind in big}

    group_keys = [["w_glu", "w_out", "small"], ["w_up", "w_down"], ["w_kv", "w_q", "w_ao"]]
    started, token = _gather_start([[shard_of[k] for k in g] for g in group_keys], name="weights_gather_start")

    def finish_group(gi, after):
        ssem, rsem, thru, lands = started[gi]
        lands = _gather_wait(ssem, rsem, thru, lands, after, name=f"weights_gather_wait_{gi}")
        lands = _gather_forward(lands, name=f"weights_gather_forward_{gi}")
        out = {}
        for key, land in zip(group_keys[gi], lands):
            full = lax.dynamic_update_slice(land, shard_of[key][None], (chip, 0, 0, 0))
            if key == "small":
                parts = [_unpack(full[p], sh_shapes) for p in range(4)]
                for i, k in enumerate(SMALL_SHARDED):
                    out[k] = jnp.concatenate([parts[p][i] for p in range(4)], axis=-1)
            else:
                out[key] = to_weight(full, kind_of[key])
        return out

    replicated = dict(lam_re=s5_lam_re[0], lam_im=s5_lam_im[0], log_dt=s5_log_dt[0], b_re=s5_b_re[0], b_im=s5_b_im[0],
                      c_re=s5_c_re[0], c_im=s5_c_im[0], rel_bias=rel_bias, conv_b=ffn_conv_b,
                      d=s5_d[0] + token[0, 0])
    group_of = {k: gi for gi, g in enumerate(group_keys) for k in g if k != "small"}
    group_of.update({k: 0 for k in SMALL_SHARDED})
    group_of.update({k: "replicated" for k in replicated})
    params = _Lazy(group_of, lambda g, after: replicated if g == "replicated" else finish_group(g, after))

    red = [("w_up1", "ffn_w_up", 1), ("w_down1", "ffn_w_down", 1), ("w_ao", "attn_w_out", 0), ("w_kv", "attn_w_kv", 0),
           ("w_q", "attn_w_q", 0), ("w_up0", "ffn_w_up", 0), ("w_down0", "ffn_w_down", 0), ("w_out", "s5_w_out", 0),
           ("w_glu", "s5_w_glu", 0)]
    early_stages = [red[:5], red[5:7]]
    n_early = 7

    def grad_halves(gw, key, src):
        r, c = w_in[src].shape[-2:]
        return gw[key].reshape(4, 2, r // 2, c)

    early = []

    def on_grads(stage, gw):
        tag = "ab"[stage]
        ga = [grad_halves(gw, key, src) for key, src, _ in early_stages[stage]]
        hf, hx = _pair_sums(ga, [True] * len(ga), cidx, tag)
        started, tok = _chip_exchange_start(hx, name=f"rs_chip_exchange_start_{tag}")
        early.append((hf, started, tag))
        return tok

    loss, grad_x, GW, GS = _local_step(x, loss_target, params, params, on_grads)
    loss = lax.psum(loss, ("x", "y", "c"))

    gs_shapes = [GS[k].shape for k in SMALL_ORDER]
    gs_pack = _pack([GS[k] for k in SMALL_ORDER], 128, 64)
    rs = gs_pack.shape[0] // 8
    gl = [grad_halves(GW, key, src) for key, src, _ in red[n_early:]] + [gs_pack.reshape(4, 2, rs, 128)]
    hf_l, hx_l = _pair_sums(gl, [True] * (len(gl) - 1) + [False], cidx, "c")
    mine = []
    for hf, started, tag in early:
        got = _chip_exchange_wait(started, grad_x, name=f"rs_chip_exchange_wait_{tag}")
        mine += _chip_sums(hf, got, chip_idx, tag)
    mine += _chip_sums(hf_l, _chip_exchange(hx_l, name="rs_chip_exchange_c"), chip_idx, "c")
    other = _pair_swap(mine, name="rs_pair_swap")
    small_halves = jnp.where(core == 0, jnp.concatenate([mine[-1], other[-1]]), jnp.concatenate([other[-1], mine[-1]]))
    small_all = _gather([small_halves], chip, name="small_grads_all_gather")[0]
    gsmall = dict(zip(SMALL_ORDER, _unpack(small_all, gs_shapes)))

    big_res = {}
    for (key, src, layer), gm, go in zip(red, mine[:-1], other[:-1]):
        nl = w_in[src].shape[0] if src in ("ffn_w_up", "ffn_w_down") else 1
        r, c = w_in[src].shape[-2:]
        view = lambda t: t.reshape(nl, 2, r // 2, c)
        res4 = _adamw_halves(view(w_in[src]), view(m_in[src]), view(v_in[src]), gm, go, cidx, layer=layer,
                             prev=big_res.get(src), name=f"adamw_{key}")
        big_res[src] = res4
    big_res = {src: tuple(t.reshape(w_in[src].shape) for t in res4) for src, res4 in big_res.items()}

    def big_out(i):
        return {src: big_res[src][i] for _, src, _ in big}

    small_w = {"lam_re": s5_lam_re, "lam_im": s5_lam_im, "log_dt": s5_log_dt, "b_re": s5_b_re, "b_im": s5_b_im,
               "c_re": s5_c_re, "c_im": s5_c_im, "d": s5_d, "rel_bias": rel_bias, "conv_b": ffn_conv_b,
               "b_glu": s5_b_glu, "conv_w": ffn_conv_w, "ln_gain": ln_gain, "ln_bias": ln_bias}
    small_name = {"lam_re": "s5_lam_re", "lam_im": "s5_lam_im", "log_dt": "s5_log_dt", "b_re": "s5_b_re", "b_im": "s5_b_im",
                  "c_re": "s5_c_re", "c_im": "s5_c_im", "d": "s5_d", "rel_bias": "rel_bias", "conv_b": "ffn_conv_b",
                  "b_glu": "s5_b_glu", "conv_w": "ffn_conv_w", "ln_gain": "ln_gain", "ln_bias": "ln_bias"}
    sg = {}
    for k in SMALL_ORDER:
        shp = small_w[k].shape
        g = gsmall[k]
        if k in SMALL_SHARDED:
            width = shp[-1]
            g = lax.dynamic_slice_in_dim(g, chip * width, width, axis=g.ndim - 1)
        sg[k] = g.reshape(shp)
    sd, snm, snv = {}, {}, {}
    for k in SMALL_ORDER:
        shp = small_w[k].shape
        flat = lambda t: t.reshape(-1, shp[-1])
        r3 = _adamw(flat(small_w[k]), flat(sg[k]), flat(m_in[small_name[k]]), flat(v_in[small_name[k]]),
                    name=f"adamw_{k}")
        sd[k], snm[k], snv[k] = (t.reshape(shp) for t in r3)

    res = [{}, {}, {}, {}]
    for i in range(4):
        res[i].update(big_out(i))
    for k in SMALL_ORDER:
        res[0][small_name[k]] = sg[k]
        res[1][small_name[k]] = sd[k]
        res[2][small_name[k]] = snm[k]
        res[3][small_name[k]] = snv[k]
    outs = [loss, grad_x]
    for i in range(4):
        outs += [res[i][n] for n in names]
    return tuple(outs)
```

```python
import functools
import math

import numpy as np
import jax
import jax.numpy as jnp
from jax import lax
from jax.experimental import pallas as pl
from jax.experimental.pallas import tpu as pltpu

F32 = jnp.float32
BF16 = jnp.bfloat16
MXU_DTYPE = jnp.bfloat16
V7X_VMEM_LIMIT_BYTES = 52 << 20
MESH = pl.DeviceIdType.MESH

DEPTH = 2
SSM_GROUP = 16
SSM_STATE = 64
GROUPS_PER_CLUSTER = 16
CLUSTER_W = GROUPS_PER_CLUSTER * SSM_GROUP
HEAD_DIM = 64
DILATIONS = (1, 4, 16)
BAND = 128
NEG_BIG = -1e30
REL_BUCKETS = 32
REL_MAX_DIST = 2048
DN_ALPHA = (2.0 * DEPTH) ** 0.25
LN_EPS = 1e-5
ADAM_LR, ADAM_B1, ADAM_B2, ADAM_EPS, ADAM_WD, ADAM_STEP = 0.001, 0.9, 0.999, 1e-08, 0.01, 10
GELU_K = math.sqrt(2.0 / math.pi)
GELU_C = 0.044715


def _pallas(body, **kw):
    return pl.pallas_call(body, **kw)


def _params(sem=None):
    return pltpu.CompilerParams(dimension_semantics=sem, vmem_limit_bytes=V7X_VMEM_LIMIT_BYTES)


def _pick(n, cands):
    for c in cands:
        if n % c == 0:
            return c
    return n


def _sigmoid(z):
    return 1.0 / (1.0 + jnp.exp(-z))


def _gelu(y):
    return 0.5 * y * (1.0 + jnp.tanh(GELU_K * (y + GELU_C * y * y * y)))


def _gelu_grad(y):
    t = jnp.tanh(GELU_K * (y + GELU_C * y * y * y))
    return 0.5 * (1.0 + t) + 0.5 * y * (1.0 - t * t) * (GELU_K * (1.0 + 3.0 * GELU_C * y * y))


def _mm_nn(a, w, *, l=0, bias=None, out_dtype=F32, name):
    T, K = a.shape
    P, _, _, Np = w.shape
    tm = _pick(T, (1024, 512, 256, 128))
    tn = _pick(Np, (1408, 1024, 768, 512, 384, 256, 128))
    nj = Np // tn

    def body(*refs):
        if bias is None:
            a_ref, w_ref, o_ref = refs
        else:
            a_ref, w_ref, b_ref, o_ref = refs
        acc = jnp.dot(a_ref[...].astype(MXU_DTYPE), w_ref[...].astype(MXU_DTYPE), preferred_element_type=F32)
        if bias is not None:
            acc = acc + b_ref[...]
        o_ref[...] = acc.astype(o_ref.dtype)

    in_specs = [pl.BlockSpec((tm, K), lambda p, j, i: (i, 0)),
                pl.BlockSpec((None, None, K, tn), lambda p, j, i: (p, l, 0, j))]
    args = [a, w]
    if bias is not None:
        in_specs.append(pl.BlockSpec((1, tn), lambda p, j, i: (0, p * nj + j)))
        args.append(bias)
    return _pallas(
        body, name=name, grid=(P, nj, T // tm), in_specs=in_specs,
        out_specs=pl.BlockSpec((tm, tn), lambda p, j, i: (i, p * nj + j)),
        out_shape=jax.ShapeDtypeStruct((T, P * Np), out_dtype),
        compiler_params=_params(("parallel", "parallel", "parallel")),
    )(*args)


def _mm_nt(a, w, *, l=0, p0=0, pn=None, out_dtype=F32, name):
    T = a.shape[0]
    _, _, K, Np = w.shape
    pn = w.shape[0] if pn is None else pn
    tm = _pick(T, (1024, 512, 256, 128) if K <= 1024 else (512, 256, 128))
    tn = _pick(Np, (1536, 1408, 1024, 768, 512, 384, 256, 128))
    nj = Np // tn
    nred = pn * nj

    def body(a_ref, w_ref, o_ref, acc):
        r = pl.program_id(1)

        @pl.when(r == 0)
        def _():
            acc[...] = jnp.zeros_like(acc)

        acc[...] += lax.dot_general(a_ref[...].astype(MXU_DTYPE), w_ref[...].astype(MXU_DTYPE),
                                    (((1,), (1,)), ((), ())), preferred_element_type=F32)

        @pl.when(r == nred - 1)
        def _():
            o_ref[...] = acc[...].astype(o_ref.dtype)

    return _pallas(
        body, name=name, grid=(T // tm, nred),
        in_specs=[pl.BlockSpec((tm, tn), lambda i, r: (i, r)),
                  pl.BlockSpec((None, None, K, tn), lambda i, r: (p0 + r // nj, l, 0, r % nj))],
        out_specs=pl.BlockSpec((tm, K), lambda i, r: (i, 0)),
        out_shape=jax.ShapeDtypeStruct((T, K), out_dtype),
        scratch_shapes=[pltpu.VMEM((tm, K), F32)],
        compiler_params=_params(("parallel", "arbitrary")),
    )(a, w)


def _tn(a, b, *, ptotal, np_cols, nl=1, l=0, p0=0, prev=None, name):
    T, K = a.shape
    Np = np_cols
    pn = b.shape[1] // Np
    tt = _pick(T, (1024, 512, 256, 128))
    tk = _pick(K, (1408, 1024, 512, 256, 128))
    tn = _pick(Np, (1408, 768, 512, 256, 128))
    if tk * tn > 1408 * 1024:
        tn = _pick(Np, (512, 256, 128))
    nj = Np // tn
    nt = T // tt

    def body(*refs):
        a_ref, b_ref = refs[0], refs[1]
        o_ref, acc = refs[-2], refs[-1]
        t = pl.program_id(3)

        @pl.when(t == 0)
        def _():
            acc[...] = jnp.zeros_like(acc)

        acc[...] += lax.dot_general(a_ref[...].astype(MXU_DTYPE), b_ref[...].astype(MXU_DTYPE),
                                    (((0,), (0,)), ((), ())), preferred_element_type=F32)

        @pl.when(t == nt - 1)
        def _():
            o_ref[...] = acc[...]

    in_specs = [pl.BlockSpec((tt, tk), lambda kb, p, j, t: (t, kb)),
                pl.BlockSpec((tt, tn), lambda kb, p, j, t: (t, p * nj + j))]
    args = [a, b]
    aliases = {}
    if prev is not None:
        in_specs.append(pl.BlockSpec(memory_space=pl.ANY))
        args.append(prev)
        aliases = {2: 0}
    return _pallas(
        body, name=name, grid=(K // tk, pn, nj, nt), in_specs=in_specs,
        out_specs=pl.BlockSpec((None, None, tk, tn), lambda kb, p, j, t: (p0 + p, l, kb, j)),
        out_shape=jax.ShapeDtypeStruct((ptotal, nl, K, Np), F32),
        scratch_shapes=[pltpu.VMEM((tk, tn), F32)],
        input_output_aliases=aliases,
        compiler_params=_params(("parallel", "parallel", "parallel", "arbitrary")),
    )(*args)


def _rows(tm, f):
    return pl.BlockSpec((tm, f), lambda i: (i, 0))


def _whole(shape):
    nd = len(shape)
    return pl.BlockSpec(shape, lambda i: (0,) * nd)


def _ln_fwd(xres, f, gain, bias, *, name):
    T, D = xres.shape
    tm = _pick(T, (256, 128))

    def body(x_ref, f_ref, g_ref, b_ref, y_ref, yb_ref, xh_ref, rs_ref):
        z = DN_ALPHA * x_ref[...] + f_ref[...]
        mu = jnp.mean(z, axis=-1, keepdims=True)
        zc = z - mu
        var = jnp.mean(zc * zc, axis=-1, keepdims=True)
        rstd = lax.rsqrt(var + LN_EPS)
        xh = zc * rstd
        y = xh * g_ref[...] + b_ref[...]
        y_ref[...] = y
        yb_ref[...] = y.astype(yb_ref.dtype)
        xh_ref[...] = xh
        rs_ref[...] = rstd

    return _pallas(
        body, name=name, grid=(T // tm,),
        in_specs=[_rows(tm, D), _rows(tm, D), _whole((1, D)), _whole((1, D))],
        out_specs=[_rows(tm, D), _rows(tm, D), _rows(tm, D), _rows(tm, 1)],
        out_shape=[jax.ShapeDtypeStruct((T, D), F32), jax.ShapeDtypeStruct((T, D), MXU_DTYPE),
                   jax.ShapeDtypeStruct((T, D), F32), jax.ShapeDtypeStruct((T, 1), F32)],
        compiler_params=_params(("parallel",)),
    )(xres, f, gain, bias)


def _ln_bwd(addends, coefs, xhat, rstd, gain, *, name):
    T, D = xhat.shape
    tm = _pick(T, (256, 128))
    n = len(addends)

    def body(*refs):
        adds = refs[:n]
        xh_ref, rs_ref, g_ref, dz_ref, dzb_ref, dg_ref, db_ref = refs[n:]
        dy = coefs[0] * adds[0][...]
        for c, r in zip(coefs[1:], adds[1:]):
            dy = dy + c * r[...]
        xh = xh_ref[...]
        dxh = dy * g_ref[...]
        m1 = jnp.mean(dxh, axis=-1, keepdims=True)
        m2 = jnp.mean(dxh * xh, axis=-1, keepdims=True)
        dz = rs_ref[...] * (dxh - m1 - xh * m2)
        dz_ref[...] = dz
        dzb_ref[...] = dz.astype(dzb_ref.dtype)

        @pl.when(pl.program_id(0) == 0)
        def _():
            dg_ref[...] = jnp.zeros_like(dg_ref)
            db_ref[...] = jnp.zeros_like(db_ref)

        dg_ref[...] += jnp.sum(dy * xh, axis=0, keepdims=True)
        db_ref[...] += jnp.sum(dy, axis=0, keepdims=True)

    return _pallas(
        body, name=name, grid=(T // tm,),
        in_specs=[_rows(tm, D)] * n + [_rows(tm, D), _rows(tm, 1), _whole((1, D))],
        out_specs=[_rows(tm, D), _rows(tm, D), _whole((1, D)), _whole((1, D))],
        out_shape=[jax.ShapeDtypeStruct((T, D), F32), jax.ShapeDtypeStruct((T, D), MXU_DTYPE),
                   jax.ShapeDtypeStruct((1, D), F32), jax.ShapeDtypeStruct((1, D), F32)],
        compiler_params=_params(("arbitrary",)),
    )(*addends, xhat, rstd, gain)


def _loss_grad(y, tgt, *, name):
    T, D = y.shape
    tm = _pick(T, (256, 128))

    def body(y_ref, t_ref, dy_ref, l_ref):
        e = y_ref[...] - t_ref[...]
        dy_ref[...] = e * (1.0 / D)

        @pl.when(pl.program_id(0) == 0)
        def _():
            l_ref[...] = jnp.zeros_like(l_ref)

        l_ref[...] += jnp.zeros_like(l_ref) + jnp.sum(e * e) * (0.5 / D)

    return _pallas(
        body, name=name, grid=(T // tm,),
        in_specs=[_rows(tm, D), _rows(tm, D)],
        out_specs=[_rows(tm, D), _whole((1, 128))],
        out_shape=[jax.ShapeDtypeStruct((T, D), F32), jax.ShapeDtypeStruct((1, 128), F32)],
        compiler_params=_params(("arbitrary",)),
    )(y, tgt)


def _axpy(a, b, ca, *, name):
    T, D = a.shape
    tm = _pick(T, (256, 128))

    def body(a_ref, b_ref, o_ref):
        o_ref[...] = ca * a_ref[...] + b_ref[...]

    return _pallas(
        body, name=name, grid=(T // tm,), in_specs=[_rows(tm, D), _rows(tm, D)], out_specs=_rows(tm, D),
        out_shape=jax.ShapeDtypeStruct((T, D), F32), compiler_params=_params(("parallel",)),
    )(a, b)


def _glu_gate(y, z, *, name):
    T, D = y.shape
    tm = _pick(T, (256, 128))

    def body(y_ref, z_ref, g_ref):
        g_ref[...] = (_gelu(y_ref[...]) * _sigmoid(z_ref[...])).astype(g_ref.dtype)

    return _pallas(
        body, name=name, grid=(T // tm,), in_specs=[_rows(tm, D), _rows(tm, D)], out_specs=_rows(tm, D),
        out_shape=jax.ShapeDtypeStruct((T, D), MXU_DTYPE), compiler_params=_params(("parallel",)),
    )(y, z)


def _glu_bwd(y, z, dg, *, name):
    T, D = y.shape
    tm = _pick(T, (256, 128))

    def body(y_ref, z_ref, dg_ref, dzb_ref, dyg_ref, db_ref):
        s = _sigmoid(z_ref[...])
        dg = dg_ref[...]
        dz = dg * _gelu(y_ref[...]) * s * (1.0 - s)
        dzb_ref[...] = dz.astype(dzb_ref.dtype)
        dyg_ref[...] = dg * s

        @pl.when(pl.program_id(0) == 0)
        def _():
            db_ref[...] = jnp.zeros_like(db_ref)

        db_ref[...] += jnp.sum(dz, axis=0, keepdims=True)

    return _pallas(
        body, name=name, grid=(T // tm,), in_specs=[_rows(tm, D)] * 3,
        out_specs=[_rows(tm, D), _rows(tm, D), _whole((1, D))],
        out_shape=[jax.ShapeDtypeStruct((T, D), MXU_DTYPE), jax.ShapeDtypeStruct((T, D), F32),
                   jax.ShapeDtypeStruct((1, D), F32)],
        compiler_params=_params(("arbitrary",)),
    )(y, z, dg)


def _gelu_bwd(y, d1, d2, *, name):
    T, D = y.shape
    tm = _pick(T, (256, 128))

    def body(y_ref, a_ref, b_ref, o_ref):
        o_ref[...] = (a_ref[...] + b_ref[...]) * _gelu_grad(y_ref[...])

    return _pallas(
        body, name=name, grid=(T // tm,), in_specs=[_rows(tm, D)] * 3, out_specs=_rows(tm, D),
        out_shape=jax.ShapeDtypeStruct((T, D), F32), compiler_params=_params(("parallel",)),
    )(y, d1, d2)


CONV_ROWS = 128
CONV_EDGE = 16


def _row_shifts(x, edge, drop_edge, tm, back):
    keep = jnp.where(drop_edge, 0.0, 1.0).astype(edge.dtype)
    ext = jnp.concatenate([edge * keep, x] if back else [x, edge * keep], axis=0)
    row = lax.broadcasted_iota(jnp.int32, (tm, tm + CONV_EDGE), 0)
    col = lax.broadcasted_iota(jnp.int32, (tm, tm + CONV_EDGE), 1)
    base = row + CONV_EDGE if back else row
    out = []
    for k in (1, 2):
        pick = (col == (base - k if back else base + k)).astype(x.dtype)
        out.append(jnp.dot(pick, ext, preferred_element_type=F32))
    return out


def _conv_specs(T, F2, tm):
    return [_rows(tm, F2),
            pl.BlockSpec((CONV_EDGE, F2), lambda i: (jnp.maximum(i * (tm // CONV_EDGE) - 1, 0), 0))]


def _conv_glu_fwd(hc, conv_w, conv_b, L, *, name):
    T, F2 = hc.shape
    F = F2 // 2
    tm = CONV_ROWS

    def body(x_ref, e_ref, w_ref, b_ref, a_ref):
        at_start = (pl.program_id(0) * tm) % L == 0
        x1, x2 = _row_shifts(x_ref[...], e_ref[...], at_start, tm, True)
        x = x_ref[...].astype(F32)
        c = b_ref[...] + w_ref[0:1, :] * x + w_ref[1:2, :] * x1 + w_ref[2:3, :] * x2
        val, gate = c[:, :F], c[:, F:]
        a_ref[...] = (gate * _sigmoid(gate) * val).astype(a_ref.dtype)

    return _pallas(
        body, name=name, grid=(T // tm,),
        in_specs=_conv_specs(T, F2, tm) + [_whole((3, F2)), _whole((1, F2))],
        out_specs=_rows(tm, F),
        out_shape=jax.ShapeDtypeStruct((T, F), MXU_DTYPE), compiler_params=_params(("parallel",)),
    )(hc, hc, conv_w, conv_b)


def _conv_glu_bwd(hc, da, conv_w, conv_b, L, *, name):
    T, F2 = hc.shape
    F = F2 // 2
    tm = CONV_ROWS

    def body(x_ref, e_ref, da_ref, w_ref, b_ref, dc_ref, dw_ref, db_ref):
        at_start = (pl.program_id(0) * tm) % L == 0
        x1, x2 = _row_shifts(x_ref[...], e_ref[...], at_start, tm, True)
        x = x_ref[...].astype(F32)
        c = b_ref[...] + w_ref[0:1, :] * x + w_ref[1:2, :] * x1 + w_ref[2:3, :] * x2
        val, gate = c[:, :F], c[:, F:]
        s = _sigmoid(gate)
        da = da_ref[...].astype(F32)
        dval = da * (gate * s)
        dgate = da * val * (s * (1.0 + gate * (1.0 - s)))
        dc = jnp.concatenate([dval, dgate], axis=-1)
        dc_ref[...] = dc.astype(dc_ref.dtype)

        @pl.when(pl.program_id(0) == 0)
        def _():
            dw_ref[...] = jnp.zeros_like(dw_ref)
            db_ref[...] = jnp.zeros_like(db_ref)

        dw_ref[0:1, :] += jnp.sum(dc * x, axis=0, keepdims=True)
        dw_ref[1:2, :] += jnp.sum(dc * x1, axis=0, keepdims=True)
        dw_ref[2:3, :] += jnp.sum(dc * x2, axis=0, keepdims=True)
        db_ref[...] += jnp.sum(dc, axis=0, keepdims=True)

    return _pallas(
        body, name=name, grid=(T // tm,),
        in_specs=_conv_specs(T, F2, tm) + [_rows(tm, F), _whole((3, F2)), _whole((1, F2))],
        out_specs=[_rows(tm, F2), _whole((3, F2)), _whole((1, F2))],
        out_shape=[jax.ShapeDtypeStruct((T, F2), MXU_DTYPE), jax.ShapeDtypeStruct((3, F2), F32),
                   jax.ShapeDtypeStruct((1, F2), F32)],
        compiler_params=_params(("arbitrary",)),
    )(hc, hc, da, conv_w, conv_b)


def _conv_bwd_input(dc, conv_w, L, *, name):
    T, F2 = dc.shape
    tm = CONV_ROWS
    edge = CONV_EDGE
    last_blk = T // edge - 1

    def body(x_ref, e_ref, w_ref, o_ref):
        at_end = ((pl.program_id(0) + 1) * tm) % L == 0
        x1, x2 = _row_shifts(x_ref[...], e_ref[...], at_end, tm, False)
        x = x_ref[...].astype(F32)
        o_ref[...] = (w_ref[0:1, :] * x + w_ref[1:2, :] * x1 + w_ref[2:3, :] * x2).astype(o_ref.dtype)

    return _pallas(
        body, name=name, grid=(T // tm,),
        in_specs=[_rows(tm, F2),
                  pl.BlockSpec((edge, F2), lambda i: (jnp.minimum((i + 1) * (tm // edge), last_blk), 0)),
                  _whole((3, F2))],
        out_specs=_rows(tm, F2),
        out_shape=jax.ShapeDtypeStruct((T, F2), MXU_DTYPE), compiler_params=_params(("parallel",)),
    )(dc, dc, conv_w)


S5_CHUNK = 128
LANES = 128


def _slab_rows(c, n, ncl):
    return pl.ds(c, n) if ncl == 1 else pl.ds(c, n, stride=ncl)


def _slab_put(ref, c, n, ncl, val):
    for s in range(val.shape[1] // LANES):
        ref[s, _slab_rows(c, n, ncl), :] = val[:, s * LANES:(s + 1) * LANES]


def _slab_get(ref, c, n, ncl):
    return jnp.concatenate([ref[s, _slab_rows(c, n, ncl), :] for s in range(ref.shape[0])], axis=-1)


def _slabs(n_slab, rows):
    return pl.BlockSpec((n_slab, rows, LANES), lambda i: (0, i, 0))


def _s5_fwd(xi, wb, wc, a_r, a_i, d_row, B, *, name):
    T, D = xi.shape
    ncl = wb.shape[0]
    cs = wb.shape[2] // 2
    ns = cs // LANES
    R = B * ncl
    Q = S5_CHUNK
    QR = Q * ncl
    nsteps = Q // B

    def body(x_ref, wb_ref, wc_ref, ar_ref, ai_ref, d_ref, y_ref, yg_ref, hr_ref, hi_ref, bur, bui, cr, ci):
        @pl.when(pl.program_id(0) == 0)
        def _():
            cr[...] = jnp.zeros_like(cr)
            ci[...] = jnp.zeros_like(ci)

        x = x_ref[...]
        xb = x.astype(MXU_DTYPE)
        for c in range(ncl):
            bu = jnp.dot(xb[:, c * CLUSTER_W:(c + 1) * CLUSTER_W], wb_ref[c], preferred_element_type=F32)
            _slab_put(bur, c, Q, ncl, bu[:, :cs])
            _slab_put(bui, c, Q, ncl, bu[:, cs:])
        ar = ar_ref[...]
        ai = ai_ref[...]

        def step(k, carry):
            hr, hi = carry
            sl = pl.ds(pl.multiple_of(k * R, R), R)
            nr = ar * hr - ai * hi + bur[:, sl, :]
            ni = ar * hi + ai * hr + bui[:, sl, :]
            hr_ref[:, sl, :] = nr
            hi_ref[:, sl, :] = ni
            return nr, ni

        hr, hi = lax.fori_loop(0, nsteps, step, (cr[...], ci[...]), unroll=4)
        cr[...] = hr
        ci[...] = hi
        parts = []
        for c in range(ncl):
            hrc = _slab_get(hr_ref, c, Q, ncl).astype(MXU_DTYPE)
            hic = _slab_get(hi_ref, c, Q, ncl).astype(MXU_DTYPE)
            parts.append(jnp.dot(hrc, wc_ref[c, :cs, :], preferred_element_type=F32)
                         + jnp.dot(hic, wc_ref[c, cs:, :], preferred_element_type=F32))
        y = d_ref[...] * x + (parts[0] if ncl == 1 else jnp.concatenate(parts, axis=-1))
        y_ref[...] = y
        yg_ref[...] = _gelu(y).astype(yg_ref.dtype)

    return _pallas(
        body, name=name, grid=(T // Q,),
        in_specs=[_rows(Q, D), _whole(wb.shape), _whole(wc.shape), _whole((ns, R, LANES)), _whole((ns, R, LANES)),
                  _whole((1, D))],
        out_specs=[_rows(Q, D), _rows(Q, D), _slabs(ns, QR), _slabs(ns, QR)],
        out_shape=[jax.ShapeDtypeStruct((T, D), F32), jax.ShapeDtypeStruct((T, D), MXU_DTYPE),
                   jax.ShapeDtypeStruct((ns, T * ncl, LANES), F32), jax.ShapeDtypeStruct((ns, T * ncl, LANES), F32)],
        scratch_shapes=[pltpu.VMEM((ns, QR, LANES), F32), pltpu.VMEM((ns, QR, LANES), F32),
                        pltpu.VMEM((ns, R, LANES), F32), pltpu.VMEM((ns, R, LANES), F32)],
        compiler_params=_params(("arbitrary",)),
    )(xi, wb, wc, a_r, a_i, d_row)


def _s5_bwd(dy, xi, h_r, h_i, wb, wc, a_r, a_i, d_row, B, *, name):
    T, D = dy.shape
    ncl = wb.shape[0]
    cs = wb.shape[2] // 2
    ns = cs // LANES
    R = B * ncl
    Q = S5_CHUNK
    nsteps = Q // B
    nchunk = T // Q
    QR = Q * ncl

    def rev(i):
        return nchunk - 1 - i

    def body(dy_ref, x_ref, hr_ref, hi_ref, pr_ref, pi_ref, wb_ref, wc_ref, ar_ref, ai_ref, d_ref,
             du_ref, gr_ref, gi_ref, dar_ref, dai_ref, dd_ref, dhr, dhi, cr, ci):
        i = pl.program_id(0)

        @pl.when(i == 0)
        def _():
            cr[...] = jnp.zeros_like(cr)
            ci[...] = jnp.zeros_like(ci)
            dar_ref[...] = jnp.zeros_like(dar_ref)
            dai_ref[...] = jnp.zeros_like(dai_ref)
            dd_ref[...] = jnp.zeros_like(dd_ref)

        dyv = dy_ref[...]
        dyb = dyv.astype(MXU_DTYPE)
        for c in range(ncl):
            dh = lax.dot_general(dyb[:, c * CLUSTER_W:(c + 1) * CLUSTER_W], wc_ref[c],
                                 (((1,), (1,)), ((), ())), preferred_element_type=F32)
            _slab_put(dhr, c, Q, ncl, dh[:, :cs])
            _slab_put(dhi, c, Q, ncl, dh[:, cs:])
        ar = ar_ref[...]
        ai = ai_ref[...]

        def step(j, carry):
            gr, gi = carry
            k = nsteps - 1 - j
            sl = pl.ds(pl.multiple_of(k * R, R), R)
            ngr = dhr[:, sl, :] + ar * gr + ai * gi
            ngi = dhi[:, sl, :] - ai * gr + ar * gi
            gr_ref[:, sl, :] = ngr
            gi_ref[:, sl, :] = ngi
            return ngr, ngi

        gr, gi = lax.fori_loop(0, nsteps, step, (cr[...], ci[...]), unroll=4)
        cr[...] = gr
        ci[...] = gi
        keep = jnp.where(i == nchunk - 1, 0.0, 1.0)
        hpr = jnp.concatenate([pr_ref[:, 8 - R:8, :] * keep, hr_ref[:, 0:QR - R, :]], axis=1)
        hpi = jnp.concatenate([pi_ref[:, 8 - R:8, :] * keep, hi_ref[:, 0:QR - R, :]], axis=1)
        gra, gia = gr_ref[...], gi_ref[...]
        steps = lambda t: jnp.sum(t.reshape(ns, nsteps, R, LANES), axis=1)
        dar_ref[...] += steps(gra * hpr + gia * hpi)
        dai_ref[...] += steps(gia * hpr - gra * hpi)
        parts = []
        for c in range(ncl):
            grc = _slab_get(gr_ref, c, Q, ncl).astype(MXU_DTYPE)
            gic = _slab_get(gi_ref, c, Q, ncl).astype(MXU_DTYPE)
            parts.append(lax.dot_general(grc, wb_ref[c, :, :cs], (((1,), (1,)), ((), ())), preferred_element_type=F32)
                         + lax.dot_general(gic, wb_ref[c, :, cs:], (((1,), (1,)), ((), ())), preferred_element_type=F32))
        du_ref[...] = d_ref[...] * dyv + (parts[0] if ncl == 1 else jnp.concatenate(parts, axis=-1))
        dd_ref[...] += jnp.sum(dyv * x_ref[...], axis=0, keepdims=True)

    tok = pl.BlockSpec((Q, D), lambda i: (rev(i), 0))
    st = pl.BlockSpec((ns, QR, LANES), lambda i: (0, rev(i), 0))
    before = pl.BlockSpec((ns, 8, LANES), lambda i: (0, jnp.maximum(rev(i) * (QR // 8) - 1, 0), 0))
    acc = _whole((ns, R, LANES))
    return _pallas(
        body, name=name, grid=(nchunk,),
        in_specs=[tok, tok, st, st, before, before, _whole(wb.shape), _whole(wc.shape), acc, acc, _whole((1, D))],
        out_specs=[tok, st, st, acc, acc, _whole((1, D))],
        out_shape=[jax.ShapeDtypeStruct((T, D), F32),
                   jax.ShapeDtypeStruct((ns, T * ncl, LANES), F32), jax.ShapeDtypeStruct((ns, T * ncl, LANES), F32),
                   jax.ShapeDtypeStruct((ns, R, LANES), F32), jax.ShapeDtypeStruct((ns, R, LANES), F32),
                   jax.ShapeDtypeStruct((1, D), F32)],
        scratch_shapes=[pltpu.VMEM((ns, QR, LANES), F32)] * 2 + [pltpu.VMEM((ns, R, LANES), F32)] * 2,
        compiler_params=_params(("arbitrary",)),
    )(dy, xi, h_r, h_i, h_r, h_i, wb, wc, a_r, a_i, d_row)


def _cluster_tn(tok, st, ncl, *, tok_left, name):
    T = tok.shape[0]
    ns = st.shape[0]
    cs = ns * LANES
    tt = _pick(T, (512, 256, 128))
    nt = T // tt
    oshape = (ncl, CLUSTER_W, cs) if tok_left else (ncl, cs, CLUSTER_W)

    def body(tok_ref, st_ref, o_ref, acc):
        t = pl.program_id(0)

        @pl.when(t == 0)
        def _():
            acc[...] = jnp.zeros_like(acc)

        tk = tok_ref[...].astype(MXU_DTYPE)
        for c in range(ncl):
            tc = tk[:, c * CLUSTER_W:(c + 1) * CLUSTER_W]
            sc = _slab_get(st_ref, c, tt, ncl).astype(MXU_DTYPE)
            lhs, rhs = (tc, sc) if tok_left else (sc, tc)
            acc[c] += lax.dot_general(lhs, rhs, (((0,), (0,)), ((), ())), preferred_element_type=F32)

        @pl.when(t == nt - 1)
        def _():
            o_ref[...] = acc[...]

    return _pallas(
        body, name=name, grid=(nt,),
        in_specs=[_rows(tt, tok.shape[1]), _slabs(ns, tt * ncl)],
        out_specs=_whole(oshape),
        out_shape=jax.ShapeDtypeStruct(oshape, F32),
        scratch_shapes=[pltpu.VMEM(oshape, F32)],
        compiler_params=_params(("arbitrary",)),
    )(tok, st)


def _s5_discretize(lam_re, lam_im, log_dt, b_re, b_im):
    dt = jnp.exp(log_dt)[:, None]
    mag = jnp.exp(lam_re * dt)
    ab_r, ab_i = mag * jnp.cos(lam_im * dt), mag * jnp.sin(lam_im * dt)
    den = lam_re * lam_re + lam_im * lam_im
    nr = ab_r - 1.0
    co_r = (nr * lam_re + ab_i * lam_im) / den
    co_i = (ab_i * lam_re - nr * lam_im) / den
    bb_r = co_r[..., None] * b_re - co_i[..., None] * b_im
    bb_i = co_r[..., None] * b_im + co_i[..., None] * b_re
    return ab_r, ab_i, bb_r, bb_i


def _blockdiag(m):
    G, r, k = m.shape
    ncl = G // GROUPS_PER_CLUSTER
    m4 = m.reshape(ncl, GROUPS_PER_CLUSTER, r, k)
    eye = jnp.eye(GROUPS_PER_CLUSTER, dtype=m.dtype)
    return jnp.einsum('cgrk,gh->cgrhk', m4, eye).reshape(ncl, GROUPS_PER_CLUSTER * r, GROUPS_PER_CLUSTER * k)


def _unblockdiag(m, r, k):
    ncl = m.shape[0]
    m5 = m.reshape(ncl, GROUPS_PER_CLUSTER, r, GROUPS_PER_CLUSTER, k)
    eye = jnp.eye(GROUPS_PER_CLUSTER, dtype=m.dtype)
    return jnp.einsum('cgrhk,gh->cgrk', m5, eye).reshape(ncl * GROUPS_PER_CLUSTER, r, k)


def _t5_bucket(dist):
    exact = REL_BUCKETS // 2
    d = np.maximum(dist, 1).astype(np.float32)
    large = exact + (np.log(d / exact) / math.log(REL_MAX_DIST / exact) * (REL_BUCKETS - exact)).astype(np.int64)
    large = np.minimum(large, REL_BUCKETS - 1)
    return np.where(dist < exact, dist, large).astype(np.int32)


def _band_tables(dil):
    steps = np.arange(BAND)[:, None] + BAND - np.arange(2 * BAND)[None, :]
    bucket = _t5_bucket(np.maximum(steps, 0) * dil)
    in_band = (steps >= 0) & (steps <= BAND)
    return bucket, in_band


def _attn_bias(rel_bias, hpg):
    out = []
    for g, dil in enumerate(DILATIONS):
        bucket, in_band = _band_tables(dil)
        cols = rel_bias[:, g * hpg:(g + 1) * hpg].astype(F32)
        onehot = jnp.asarray((bucket.reshape(-1, 1) == np.arange(REL_BUCKETS)[None, :]).astype(np.float32))
        bias = jnp.dot(onehot, cols, precision=lax.Precision.HIGHEST).T.reshape(hpg, BAND, 2 * BAND)
        out.append(jnp.where(jnp.asarray(in_band)[None], bias, NEG_BIG))
    return jnp.concatenate(out, axis=0)


def _attn_blocks(dil, L):
    M = L // dil
    return M, M // BAND


def _row_sel(r, M, dil):
    return pl.ds(r, M) if dil == 1 else pl.ds(r, M, stride=dil)


def _attn_fwd(q, kv, bias, L, hpg, *, name):
    T = q.shape[0]
    nb_ = T // L
    HP = hpg // 2
    W3 = 3 * hpg * HEAD_DIM
    mmax = L

    def group_body(dil, q_ref, k_ref, v_ref, b_ref, o_ref, l_ref, os, ls):
        M, NB = _attn_blocks(dil, L)
        for r in range(dil):
            rows = _row_sel(r, M, dil)
            first = lax.broadcasted_iota(jnp.int32, (1, 2 * HEAD_DIM), 1) < HEAD_DIM
            qf = q_ref[rows, :] * 0.125
            qm = [jnp.where(first, qf, 0.0).astype(MXU_DTYPE), jnp.where(first, 0.0, qf).astype(MXU_DTYPE)]
            kr = k_ref[rows, :].astype(MXU_DTYPE)
            va = jnp.concatenate([v_ref[rows, :].astype(MXU_DTYPE), jnp.ones((M, 2 * HEAD_DIM), MXU_DTYPE)], axis=-1)
            for n in range(NB):
                qs = slice(n * BAND, (n + 1) * BAND)
                ks = slice(0, BAND) if n == 0 else slice((n - 1) * BAND, (n + 1) * BAND)
                o_h, l_h = [], []
                for hh in range(2):
                    bb = b_ref[hh, :, BAND:] if n == 0 else b_ref[hh]
                    s = lax.dot_general(qm[hh][qs, :], kr[ks, :], (((1,), (1,)), ((), ())),
                                        preferred_element_type=F32) + bb
                    m = jnp.max(s, axis=-1, keepdims=True)
                    p = jnp.exp(s - m)
                    pv = jnp.dot(p.astype(MXU_DTYPE), va[ks, :], preferred_element_type=F32)
                    l = pv[:, 2 * HEAD_DIM:]
                    o_h.append(pv[:, :2 * HEAD_DIM] / l)
                    l_h.append(m + jnp.log(l))
                os[qs, :] = jnp.where(first, o_h[0], o_h[1])
                ls[qs, :] = jnp.where(first, l_h[0], l_h[1])
            o_ref[rows, :] = os[0:M, :]
            l_ref[rows, :] = ls[0:M, :]

    def body(q_ref, k_ref, v_ref, b_ref, o_ref, l_ref, os, ls):
        g = pl.program_id(0)
        for gi, dil in enumerate(DILATIONS):
            pl.when(g == gi)(functools.partial(group_body, dil, q_ref, k_ref, v_ref, b_ref, o_ref, l_ref, os, ls))

    blk = (L, 2 * HEAD_DIM)
    return _pallas(
        body, name=name, grid=(3, nb_, HP),
        in_specs=[pl.BlockSpec(blk, lambda g, b, h: (b, g * HP + h)),
                  pl.BlockSpec(blk, lambda g, b, h: (b, g * HP + h)),
                  pl.BlockSpec(blk, lambda g, b, h: (b, 3 * HP + g * HP + h)),
                  pl.BlockSpec((2, BAND, 2 * BAND), lambda g, b, h: (g * HP + h, 0, 0))],
        out_specs=[pl.BlockSpec(blk, lambda g, b, h: (b, g * HP + h)),
                   pl.BlockSpec(blk, lambda g, b, h: (b, g * HP + h))],
        out_shape=[jax.ShapeDtypeStruct((T, W3), F32), jax.ShapeDtypeStruct((T, W3), F32)],
        scratch_shapes=[pltpu.VMEM((mmax, 2 * HEAD_DIM), F32), pltpu.VMEM((mmax, 2 * HEAD_DIM), F32)],
        compiler_params=_params(("arbitrary", "arbitrary", "arbitrary")),
    )(q, kv, kv, bias)


def _attn_merge(o3, l3, hw, *, name):
    T = o3.shape[0]
    tm = _pick(T, (256, 128))

    def body(o0, o1, o2, l0, l1, l2, o_ref, ob_ref, lse_ref):
        a0, a1, a2 = l0[...], l1[...], l2[...]
        m = jnp.maximum(jnp.maximum(a0, a1), a2)
        e0, e1, e2 = jnp.exp(a0 - m), jnp.exp(a1 - m), jnp.exp(a2 - m)
        z = e0 + e1 + e2
        o = (e0 * o0[...] + e1 * o1[...] + e2 * o2[...]) / z
        o_ref[...] = o
        ob_ref[...] = o.astype(ob_ref.dtype)
        lse_ref[...] = m + jnp.log(z)

    def col(g):
        return pl.BlockSpec((tm, hw), lambda i: (i, g))

    return _pallas(
        body, name=name, grid=(T // tm,),
        in_specs=[col(0), col(1), col(2), col(0), col(1), col(2)],
        out_specs=[_rows(tm, hw)] * 3,
        out_shape=[jax.ShapeDtypeStruct((T, hw), F32), jax.ShapeDtypeStruct((T, hw), MXU_DTYPE),
                   jax.ShapeDtypeStruct((T, hw), F32)],
        compiler_params=_params(("parallel",)),
    )(o3, o3, o3, l3, l3, l3)


def _attn_bwd(q, kv, do, o, lse, bias, L, hpg, *, name):
    T = q.shape[0]
    nb_ = T // L
    HP = hpg // 2
    W3 = 3 * hpg * HEAD_DIM
    mmax = L

    def group_body(dil, q_ref, k_ref, v_ref, do_ref, o_ref, l_ref, b_ref, dq_ref, dk_ref, dv_ref, ds_ref,
                   dqs, dks, dvs):
        M, NB = _attn_blocks(dil, L)
        for r in range(dil):
            rows = _row_sel(r, M, dil)
            first = lax.broadcasted_iota(jnp.int32, (1, 2 * HEAD_DIM), 1) < HEAD_DIM
            qf = q_ref[rows, :] * 0.125
            qm = [jnp.where(first, qf, 0.0).astype(MXU_DTYPE), jnp.where(first, 0.0, qf).astype(MXU_DTYPE)]
            kr = k_ref[rows, :].astype(MXU_DTYPE)
            vr = v_ref[rows, :].astype(MXU_DTYPE)
            dof = do_ref[rows, :]
            dom = [jnp.where(first, dof, 0.0).astype(MXU_DTYPE), jnp.where(first, 0.0, dof).astype(MXU_DTYPE)]
            dod = dof * o_ref[rows, :]
            delta = [jnp.sum(jnp.where(first, dod, 0.0), axis=-1, keepdims=True),
                     jnp.sum(jnp.where(first, 0.0, dod), axis=-1, keepdims=True)]
            lr = l_ref[rows, :]
            lse = [lr[:, 0:1], lr[:, HEAD_DIM:HEAD_DIM + 1]]
            dks[0:M, :] = jnp.zeros((M, 2 * HEAD_DIM), F32)
            dvs[0:M, :] = jnp.zeros((M, 2 * HEAD_DIM), F32)
            for n in range(NB):
                qs = slice(n * BAND, (n + 1) * BAND)
                ks = slice(0, BAND) if n == 0 else slice((n - 1) * BAND, (n + 1) * BAND)
                dq_h = []
                dkc = dvc = None
                for hh in range(2):
                    bb = b_ref[hh, :, BAND:] if n == 0 else b_ref[hh]
                    qb, dob = qm[hh][qs, :], dom[hh][qs, :]
                    s = lax.dot_general(qb, kr[ks, :], (((1,), (1,)), ((), ())), preferred_element_type=F32) + bb
                    p = jnp.exp(s - lse[hh][qs, :])
                    dp = lax.dot_general(dob, vr[ks, :], (((1,), (1,)), ((), ())), preferred_element_type=F32)
                    ds = p * (dp - delta[hh][qs, :])
                    if n == 0:
                        ds_ref[hh, :, BAND:] += ds
                    else:
                        ds_ref[hh] += ds
                    dsm = ds.astype(MXU_DTYPE)
                    dq_h.append(jnp.dot(dsm, kr[ks, :], preferred_element_type=F32))
                    dk1 = lax.dot_general(dsm, qb, (((0,), (0,)), ((), ())), preferred_element_type=F32)
                    dv1 = lax.dot_general(p.astype(MXU_DTYPE), dob, (((0,), (0,)), ((), ())), preferred_element_type=F32)
                    dkc = dk1 if dkc is None else dkc + dk1
                    dvc = dv1 if dvc is None else dvc + dv1
                dqs[qs, :] = jnp.where(first, dq_h[0], dq_h[1]) * 0.125
                dks[ks, :] += dkc
                dvs[ks, :] += dvc
            dq_ref[rows, :] = dqs[0:M, :]
            dk_ref[rows, :] = dks[0:M, :]
            dv_ref[rows, :] = dvs[0:M, :]

    def body(q_ref, k_ref, v_ref, do_ref, o_ref, l_ref, b_ref, dq_ref, dk_ref, dv_ref, ds_ref, dqs, dks, dvs):
        g = pl.program_id(0)

        @pl.when(pl.program_id(2) == 0)
        def _():
            ds_ref[...] = jnp.zeros_like(ds_ref)

        for gi, dil in enumerate(DILATIONS):
            pl.when(g == gi)(functools.partial(group_body, dil, q_ref, k_ref, v_ref, do_ref, o_ref, l_ref, b_ref,
                                               dq_ref, dk_ref, dv_ref, ds_ref, dqs, dks, dvs))

    blk = (L, 2 * HEAD_DIM)
    gcol = lambda g, h, b: (b, g * HP + h)
    hcol = lambda g, h, b: (b, h)
    return _pallas(
        body, name=name, grid=(3, HP, nb_),
        in_specs=[pl.BlockSpec(blk, gcol), pl.BlockSpec(blk, gcol),
                  pl.BlockSpec(blk, lambda g, h, b: (b, 3 * HP + g * HP + h)),
                  pl.BlockSpec(blk, hcol), pl.BlockSpec(blk, hcol), pl.BlockSpec(blk, hcol),
                  pl.BlockSpec((2, BAND, 2 * BAND), lambda g, h, b: (g * HP + h, 0, 0))],
        out_specs=[pl.BlockSpec(blk, gcol), pl.BlockSpec(blk, gcol), pl.BlockSpec(blk, gcol),
                   pl.BlockSpec((2, BAND, 2 * BAND), lambda g, h, b: (g * HP + h, 0, 0))],
        out_shape=[jax.ShapeDtypeStruct((T, W3), F32), jax.ShapeDtypeStruct((T, W3), F32),
                   jax.ShapeDtypeStruct((T, W3), F32), jax.ShapeDtypeStruct((3 * hpg, BAND, 2 * BAND), F32)],
        scratch_shapes=[pltpu.VMEM((mmax, 2 * HEAD_DIM), F32)] * 3,
        compiler_params=_params(("arbitrary", "arbitrary", "arbitrary")),
    )(q, kv, kv, do, o, lse, bias)


def _bias_grad(ds_sum, hpg, *, name):
    nh = ds_sum.shape[0]
    idx = np.stack([np.where(_band_tables(dil)[1], _band_tables(dil)[0], -1) for dil in DILATIONS]).astype(np.int32)

    def body(ds_ref, idx_ref, o_ref):
        d = ds_ref[...]
        ix = idx_ref[...]
        lane = lax.broadcasted_iota(jnp.int32, (8, 128), 1)
        row = jnp.zeros((8, 128), F32)
        for b in range(REL_BUCKETS):
            row = row + jnp.where(lane == b, jnp.sum(jnp.where(ix == b, d, 0.0)), 0.0)
        o_ref[...] = row

    out = _pallas(
        body, name=name, grid=(nh,),
        in_specs=[pl.BlockSpec((None, BAND, 2 * BAND), lambda h: (h, 0, 0)),
                  pl.BlockSpec((None, BAND, 2 * BAND), lambda h: (h // hpg, 0, 0))],
        out_specs=pl.BlockSpec((None, 8, 128), lambda h: (h, 0, 0)),
        out_shape=jax.ShapeDtypeStruct((nh, 8, 128), F32),
        compiler_params=_params(("parallel",)),
    )(ds_sum, jnp.asarray(idx))
    return out[:, 0, :REL_BUCKETS].T


def _adamw(w, g, m, v, *, name):
    Rw, C = w.shape
    tm = _pick(Rw, (512, 352, 256, 128, 64, 32, 16, 8))

    def body(w_ref, g_ref, m_ref, v_ref, d_ref, nm_ref, nv_ref):
        gg = g_ref[...]
        nm = ADAM_B1 * m_ref[...] + (1.0 - ADAM_B1) * gg
        nv = ADAM_B2 * v_ref[...] + (1.0 - ADAM_B2) * (gg * gg)
        m_hat = nm / (1.0 - ADAM_B1 ** ADAM_STEP)
        v_hat = nv / (1.0 - ADAM_B2 ** ADAM_STEP)
        d_ref[...] = -ADAM_LR * (m_hat / (jnp.sqrt(v_hat) + ADAM_EPS) + ADAM_WD * w_ref[...])
        nm_ref[...] = nm
        nv_ref[...] = nv

    return _pallas(
        body, name=name, grid=(Rw // tm,), in_specs=[_rows(tm, C)] * 4, out_specs=[_rows(tm, C)] * 3,
        out_shape=[jax.ShapeDtypeStruct((Rw, C), F32)] * 3, compiler_params=_params(("parallel",)),
    )(w, g, m, v)


ROW_TILE_ELEMS = 256 * 1024


def _tile_rows(r, c):
    best = 8
    for t in range(8, r + 1, 8):
        if r % t == 0 and t * c <= ROW_TILE_ELEMS:
            best = t
    return best


def _adamw_halves(w, m, v, mine, other, cidx, *, layer=0, prev=None, name):
    NL, _, r, c = w.shape
    tm = _tile_rows(r, c)

    def body(c_ref, w_ref, m_ref, v_ref, a_ref, b_ref, *rest):
        g_ref, d_ref, nm_ref, nv_ref = rest[-4:]
        gg = jnp.where(pl.program_id(0) == c_ref[0], a_ref[...], b_ref[...])
        nm = ADAM_B1 * m_ref[...] + (1.0 - ADAM_B1) * gg
        nv = ADAM_B2 * v_ref[...] + (1.0 - ADAM_B2) * (gg * gg)
        m_hat = nm / (1.0 - ADAM_B1 ** ADAM_STEP)
        v_hat = nv / (1.0 - ADAM_B2 ** ADAM_STEP)
        g_ref[...] = gg
        d_ref[...] = -ADAM_LR * (m_hat / (jnp.sqrt(v_hat) + ADAM_EPS) + ADAM_WD * w_ref[...])
        nm_ref[...] = nm
        nv_ref[...] = nv

    half = pl.BlockSpec((None, None, tm, c), lambda h, i, cr: (layer, h, i, 0))
    one = pl.BlockSpec((None, tm, c), lambda h, i, cr: (0, i, 0))
    in_specs = [half, half, half, one, one]
    args = [cidx, w, m, v, mine, other]
    aliases = {}
    if prev is not None:
        in_specs += [_ANY] * 4
        args += list(prev)
        aliases = {6 + k: k for k in range(4)}
    spec = pltpu.PrefetchScalarGridSpec(num_scalar_prefetch=1, grid=(2, r // tm), in_specs=in_specs, out_specs=[half] * 4)
    return _pallas(
        body, name=name, grid_spec=spec, out_shape=[jax.ShapeDtypeStruct((NL, 2, r, c), F32)] * 4,
        input_output_aliases=aliases, compiler_params=_params(("parallel", "parallel")),
    )(*args)


def _pair_sum(g, theirs, cidx, *, cast, name):
    _, _, r, c = g.shape
    tm = _tile_rows(r, c)

    def body(c_ref, g_ref, t_ref, *outs):
        s = g_ref[...] + t_ref[...]
        outs[0][...] = s
        if cast:
            outs[1][...] = s.astype(BF16)

    blk = (None, None, tm, c)
    first = pl.BlockSpec(blk, lambda p, i, cr: (p, 0, i, 0))
    shapes = [jax.ShapeDtypeStruct((4, 1, r, c), F32)] + ([jax.ShapeDtypeStruct((4, 1, r, c), BF16)] if cast else [])
    spec = pltpu.PrefetchScalarGridSpec(
        num_scalar_prefetch=1, grid=(4, r // tm),
        in_specs=[pl.BlockSpec(blk, lambda p, i, cr: (p, cr[0], i, 0)), first], out_specs=[first] * len(shapes))
    return _pallas(body, name=name, grid_spec=spec, out_shape=shapes,
                   compiler_params=_params(("parallel", "parallel")))(cidx, g, theirs)


def _chip_sum(hf, got, chip_idx, *, name):
    _, _, r, c = hf.shape
    tm = _tile_rows(r, c)

    def body(p_ref, h_ref, r_ref, o_ref):
        s = h_ref[...]
        for k in range(3):
            s = s + r_ref[k].astype(F32)
        o_ref[...] = s

    spec = pltpu.PrefetchScalarGridSpec(
        num_scalar_prefetch=1, grid=(r // tm,),
        in_specs=[pl.BlockSpec((None, None, tm, c), lambda i, pr: (pr[0], 0, i, 0)),
                  pl.BlockSpec((3, None, tm, c), lambda i, pr: (0, 0, i, 0))],
        out_specs=pl.BlockSpec((None, tm, c), lambda i, pr: (0, i, 0)))
    return _pallas(body, name=name, grid_spec=spec, out_shape=jax.ShapeDtypeStruct((1, r, c), F32),
                   compiler_params=_params(("parallel",)))(chip_idx, hf, got)


def _place():
    x, y, c = lax.axis_index("x"), lax.axis_index("y"), lax.axis_index("c")
    chips = [(1 - x, y), (x, 1 - y), (1 - x, 1 - y)]
    return x, y, c, chips


_ANY = pl.BlockSpec(memory_space=pl.ANY)


def _comm_call(body, ins, out_shapes, n_remote, *, name, aliases=None):
    sems = [pltpu.SemaphoreType.DMA((n,)) for n in n_remote]
    return _pallas(
        body, name=name, in_specs=[_ANY] * len(ins), out_specs=[_ANY] * len(out_shapes), out_shape=out_shapes,
        scratch_shapes=sems, input_output_aliases=aliases or {},
        compiler_params=pltpu.CompilerParams(has_side_effects=True),
    )(*ins)


_HBM_SPEC = pl.BlockSpec(memory_space=pltpu.HBM)
_SEM_SPEC = pl.BlockSpec(memory_space=pltpu.SEMAPHORE)
_DATAFLOW = pltpu.SideEffectType.DATAFLOW_SIDE_EFFECTING


def _in_hbm(a):
    return pltpu.with_memory_space_constraint(a, pltpu.HBM)


def _gather_start(groups, *, name):
    flat = [s for g in groups for s in g]
    n, ng = len(flat), len(groups)

    def body(*refs):
        ins, lands = refs[:n], refs[n:2 * n]
        sems = refs[2 * n:2 * n + 2 * ng]
        token = refs[-1]
        x, y, c, chips = _place()
        me = 2 * x + y
        a = 0
        for gi, g in enumerate(groups):
            for j in range(len(g)):
                for k, (tx, ty) in enumerate(chips):
                    _rcopy(ins[a].at[c], lands[a].at[me, c], sems[2 * gi].at[3 * j + k], sems[2 * gi + 1].at[3 * j + k],
                           (tx, ty, c)).start()
                a += 1
        token[...] = jnp.zeros_like(token)

    land_shapes = [(4,) + s.shape for s in flat]
    out_shape = ([pltpu.SemaphoreType.DMA((3 * len(g),)) for g in groups for _ in range(2)]
                 + [pltpu.HBM(s.shape, s.dtype) for s in flat]
                 + [pltpu.HBM(ls, s.dtype) for ls, s in zip(land_shapes, flat)]
                 + [jax.ShapeDtypeStruct((8, 128), F32)])
    outs = _pallas(
        body, name=name, in_specs=[_HBM_SPEC] * (2 * n),
        out_specs=[_SEM_SPEC] * (2 * ng) + [_HBM_SPEC] * (2 * n) + [pl.BlockSpec(memory_space=pltpu.VMEM)],
        out_shape=out_shape, input_output_aliases={i: 2 * ng + i for i in range(2 * n)},
        compiler_params=pltpu.CompilerParams(has_side_effects=_DATAFLOW),
    )(*[_in_hbm(s) for s in flat], *[_in_hbm(lax.empty(ls, s.dtype)) for ls, s in zip(land_shapes, flat)])
    sems, thru, lands, token = outs[:2 * ng], outs[2 * ng:2 * ng + n], outs[2 * ng + n:2 * ng + 2 * n], outs[-1]
    res, a = [], 0
    for gi, g in enumerate(groups):
        res.append((sems[2 * gi], sems[2 * gi + 1], thru[a:a + len(g)], lands[a:a + len(g)]))
        a += len(g)
    return res, token


def _gather_wait(ssem, rsem, shards, lands, after, *, name):
    m = len(shards)

    def body(*refs):
        ins, lnd = refs[:m], refs[m:2 * m]
        ss, rs = refs[2 * m], refs[2 * m + 1]
        x, y, c, chips = _place()
        for j in range(m):
            for k, (tx, ty) in enumerate(chips):
                cp = _rcopy(ins[j].at[c], lnd[j].at[2 * tx + ty, c], ss.at[3 * j + k], rs.at[3 * j + k], (tx, ty, c))
                cp.wait_send()
                cp.wait_recv()

    outs = _pallas(
        body, name=name, in_specs=[_HBM_SPEC] * (2 * m) + [_SEM_SPEC, _SEM_SPEC, _ANY],
        out_specs=[_HBM_SPEC] * (2 * m),
        out_shape=[pltpu.HBM(s.shape, s.dtype) for s in shards] + [pltpu.HBM(l.shape, l.dtype) for l in lands],
        input_output_aliases={i: i for i in range(2 * m)},
        compiler_params=pltpu.CompilerParams(has_side_effects=_DATAFLOW),
    )(*shards, *lands, ssem, rsem, after)
    return outs[m:]


def _gather_forward(lands, *, name):
    n = len(lands)

    def body(*refs):
        outs = refs[n:2 * n]
        ssem, rsem = refs[2 * n:]
        x, y, c, chips = _place()
        sib = (x, y, 1 - c)
        cps = []
        for a in range(n):
            for k, (tx, ty) in enumerate(chips):
                pk = 2 * tx + ty
                cp = _rcopy(outs[a].at[pk, c], outs[a].at[pk, c], ssem.at[3 * a + k], rsem.at[3 * a + k], sib)
                cp.start()
                cps.append(cp)
        for a in range(n):
            for k, (tx, ty) in enumerate(chips):
                pk = 2 * tx + ty
                _rcopy(outs[a].at[pk, c], outs[a].at[pk, 1 - c], ssem.at[3 * a + k], rsem.at[3 * a + k], sib).wait_recv()
        for cp in cps:
            cp.wait_send()

    shapes = [jax.ShapeDtypeStruct(l.shape, l.dtype) for l in lands]
    return _comm_call(body, lands, shapes, [3 * n, 3 * n], name=name, aliases={i: i for i in range(n)})


class _Lazy:
    def __init__(self, group_of, make):
        self._group_of, self._make, self._done, self._anchor = group_of, make, {}, None

    def anchor(self, value):
        self._anchor = value

    def __getitem__(self, key):
        g = self._group_of[key]
        if g not in self._done:
            self._done[g] = self._make(g, self._anchor)
        return self._done[g][key]


def _anchor(mapping, value):
    if isinstance(mapping, _Lazy):
        mapping.anchor(value)


def _rcopy(src, dst, ssem, rsem, dev):
    return pltpu.make_async_remote_copy(src_ref=src, dst_ref=dst, send_sem=ssem, recv_sem=rsem,
                                        device_id=dev, device_id_type=MESH)


def _all_gather(shards, *, name):
    n = len(shards)

    def body(*refs):
        ins, outs = refs[:n], refs[n:2 * n]
        s_ici, r_ici, s_d2d, r_d2d = refs[2 * n:]
        x, y, c, chips = _place()
        me = 2 * x + y
        sib = (x, y, 1 - c)
        sends = []
        for a in range(n):
            for k, (tx, ty) in enumerate(chips):
                cp = _rcopy(ins[a].at[c], outs[a].at[me, c], s_ici.at[3 * a + k], r_ici.at[3 * a + k], (tx, ty, c))
                cp.start()
                sends.append(cp)
        for a in range(n):
            for k, (tx, ty) in enumerate(chips):
                pk = 2 * tx + ty
                _rcopy(ins[a].at[c], outs[a].at[pk, c], s_ici.at[3 * a + k], r_ici.at[3 * a + k], (tx, ty, c)).wait_recv()
                fw = _rcopy(outs[a].at[pk, c], outs[a].at[pk, c], s_d2d.at[3 * a + k], r_d2d.at[3 * a + k], sib)
                fw.start()
                sends.append(fw)
        for a in range(n):
            for k, (tx, ty) in enumerate(chips):
                pk = 2 * tx + ty
                _rcopy(ins[a].at[c], outs[a].at[pk, 1 - c], s_d2d.at[3 * a + k], r_d2d.at[3 * a + k], sib).wait_recv()
        for cp in sends:
            cp.wait_send()

    shapes = [jax.ShapeDtypeStruct((4,) + s.shape, s.dtype) for s in shards]
    return _comm_call(body, shards, shapes, [3 * n] * 4, name=name)


def _gather(shards, chip, *, name):
    outs = _all_gather(shards, name=name)
    return [lax.dynamic_update_slice(o, s[None], (chip, 0, 0, 0)) for o, s in zip(outs, shards)]


def _pair_send(gs, *, name):
    n = len(gs)

    def body(*refs):
        ins, theirs = refs[:n], refs[n:2 * n]
        ssem, rsem = refs[2 * n:]
        x, y, c, _ = _place()
        sib = (x, y, 1 - c)
        cps = []
        for a in range(n):
            cp = _rcopy(ins[a].at[:, pl.ds(1 - c, 1)], theirs[a], ssem.at[a], rsem.at[a], sib)
            cp.start()
            cps.append(cp)
        for cp in cps:
            cp.wait_send()
            cp.wait_recv()

    shapes = [jax.ShapeDtypeStruct((4, 1) + g.shape[2:], g.dtype) for g in gs]
    return _comm_call(body, gs, shapes, [n, n], name=name)


def _chip_exchange(hx, *, name):
    n = len(hx)

    def body(*refs):
        hxr, got = refs[:n], refs[n:2 * n]
        ssem, rsem = refs[2 * n:]
        x, y, c, chips = _place()
        cps = []
        for a in range(n):
            for k, (tx, ty) in enumerate(chips):
                cp = _rcopy(hxr[a].at[2 * tx + ty], got[a].at[k], ssem.at[3 * a + k], rsem.at[3 * a + k], (tx, ty, c))
                cp.start()
                cps.append(cp)
        for cp in cps:
            cp.wait_send()
            cp.wait_recv()

    shapes = [jax.ShapeDtypeStruct((3,) + h.shape[1:], h.dtype) for h in hx]
    return _comm_call(body, hx, shapes, [3 * n, 3 * n], name=name)


def _pair_swap(fs, *, name):
    n = len(fs)

    def body(*refs):
        ins, outs = refs[:n], refs[n:2 * n]
        ssem, rsem = refs[2 * n:]
        x, y, c, _ = _place()
        cps = []
        for a in range(n):
            cp = _rcopy(ins[a], outs[a], ssem.at[a], rsem.at[a], (x, y, 1 - c))
            cp.start()
            cps.append(cp)
        for cp in cps:
            cp.wait_send()
            cp.wait_recv()

    shapes = [jax.ShapeDtypeStruct(f.shape, f.dtype) for f in fs]
    return _comm_call(body, fs, shapes, [n, n], name=name)


def _chip_exchange_start(hx, *, name):
    n = len(hx)

    def body(*refs):
        ins, gots = refs[:n], refs[n:2 * n]
        ssem, rsem = refs[2 * n], refs[2 * n + 1]
        token = refs[-1]
        x, y, c, chips = _place()
        for a in range(n):
            for k, (tx, ty) in enumerate(chips):
                _rcopy(ins[a].at[2 * tx + ty], gots[a].at[k], ssem.at[3 * a + k], rsem.at[3 * a + k], (tx, ty, c)).start()
        token[...] = jnp.zeros_like(token)

    got_shapes = [(3,) + h.shape[1:] for h in hx]
    outs = _pallas(
        body, name=name, in_specs=[_HBM_SPEC] * (2 * n),
        out_specs=[_SEM_SPEC] * 2 + [_HBM_SPEC] * (2 * n) + [pl.BlockSpec(memory_space=pltpu.VMEM)],
        out_shape=([pltpu.SemaphoreType.DMA((3 * n,))] * 2 + [pltpu.HBM(h.shape, h.dtype) for h in hx]
                   + [pltpu.HBM(gs, h.dtype) for gs, h in zip(got_shapes, hx)] + [jax.ShapeDtypeStruct((8, 128), F32)]),
        input_output_aliases={i: 2 + i for i in range(2 * n)},
        compiler_params=pltpu.CompilerParams(has_side_effects=_DATAFLOW),
    )(*[_in_hbm(h) for h in hx], *[_in_hbm(lax.empty(gs, h.dtype)) for gs, h in zip(got_shapes, hx)])
    return (outs[0], outs[1], outs[2:2 + n], outs[2 + n:2 + 2 * n]), outs[-1]


def _chip_exchange_wait(started, after, *, name):
    ssem, rsem, hx, gots = started
    n = len(hx)

    def body(*refs):
        ins, gts = refs[:n], refs[n:2 * n]
        ss, rs = refs[2 * n], refs[2 * n + 1]
        x, y, c, chips = _place()
        for a in range(n):
            for k, (tx, ty) in enumerate(chips):
                cp = _rcopy(ins[a].at[2 * tx + ty], gts[a].at[k], ss.at[3 * a + k], rs.at[3 * a + k], (tx, ty, c))
                cp.wait_send()
                cp.wait_recv()

    outs = _pallas(
        body, name=name, in_specs=[_HBM_SPEC] * (2 * n) + [_SEM_SPEC, _SEM_SPEC, _ANY],
        out_specs=[_HBM_SPEC] * (2 * n),
        out_shape=[pltpu.HBM(h.shape, h.dtype) for h in hx] + [pltpu.HBM(g.shape, g.dtype) for g in gots],
        input_output_aliases={i: i for i in range(2 * n)},
        compiler_params=pltpu.CompilerParams(has_side_effects=_DATAFLOW),
    )(*hx, *gots, ssem, rsem, after)
    return outs[n:]


def _pair_send_start(gs, *, name):
    n = len(gs)

    def body(*refs):
        ins, lands = refs[:n], refs[n:2 * n]
        ssem, rsem = refs[2 * n], refs[2 * n + 1]
        token = refs[-1]
        x, y, c, _ = _place()
        for a in range(n):
            _rcopy(ins[a].at[:, pl.ds(1 - c, 1)], lands[a], ssem.at[a], rsem.at[a], (x, y, 1 - c)).start()
        token[...] = jnp.zeros_like(token)

    land_shapes = [(4, 1) + g.shape[2:] for g in gs]
    outs = _pallas(
        body, name=name, in_specs=[_HBM_SPEC] * (2 * n),
        out_specs=[_SEM_SPEC] * 2 + [_HBM_SPEC] * (2 * n) + [pl.BlockSpec(memory_space=pltpu.VMEM)],
        out_shape=([pltpu.SemaphoreType.DMA((n,))] * 2 + [pltpu.HBM(g.shape, g.dtype) for g in gs]
                   + [pltpu.HBM(ls, g.dtype) for ls, g in zip(land_shapes, gs)] + [jax.ShapeDtypeStruct((8, 128), F32)]),
        input_output_aliases={i: 2 + i for i in range(2 * n)},
        compiler_params=pltpu.CompilerParams(has_side_effects=_DATAFLOW),
    )(*[_in_hbm(g) for g in gs], *[_in_hbm(lax.empty(ls, g.dtype)) for ls, g in zip(land_shapes, gs)])
    return (outs[0], outs[1], outs[2:2 + n], outs[2 + n:2 + 2 * n]), outs[-1]


def _pair_send_wait(started, after, *, name):
    ssem, rsem, gs, lands = started
    n = len(gs)

    def body(*refs):
        ins, lnd = refs[:n], refs[n:2 * n]
        ss, rs = refs[2 * n], refs[2 * n + 1]
        x, y, c, _ = _place()
        for a in range(n):
            cp = _rcopy(ins[a].at[:, pl.ds(1 - c, 1)], lnd[a], ss.at[a], rs.at[a], (x, y, 1 - c))
            cp.wait_send()
            cp.wait_recv()

    outs = _pallas(
        body, name=name, in_specs=[_HBM_SPEC] * (2 * n) + [_SEM_SPEC, _SEM_SPEC, _ANY],
        out_specs=[_HBM_SPEC] * (2 * n),
        out_shape=[pltpu.HBM(g.shape, g.dtype) for g in gs] + [pltpu.HBM(l.shape, l.dtype) for l in lands],
        input_output_aliases={i: i for i in range(2 * n)},
        compiler_params=pltpu.CompilerParams(has_side_effects=_DATAFLOW),
    )(*gs, *lands, ssem, rsem, after)
    return list(outs[:n]), list(outs[n:])


def _pair_sums(grads, exch_bf16, cidx, tag, theirs=None):
    if theirs is None:
        theirs = _pair_send(grads, name=f"rs_pair_send_{tag}")
    hf, hx = [], []
    for a in range(len(grads)):
        res = _pair_sum(grads[a], theirs[a], cidx, cast=exch_bf16[a], name=f"rs_pair_sum_{tag}{a}")
        hf.append(res[0])
        hx.append(res[1] if exch_bf16[a] else res[0])
    return hf, hx


def _chip_sums(hf, got, chip_idx, tag):
    return [_chip_sum(hf[a], got[a], chip_idx, name=f"rs_chip_sum_{tag}{a}") for a in range(len(hf))]


def _interleave(a, B, L):
    return a.reshape(B, L, -1).transpose(1, 0, 2).reshape(B * L, -1)


def _deinterleave(a, B, L):
    return a.reshape(L, B, -1).transpose(1, 0, 2).reshape(B * L, -1)


def _local_step(x, tgt, W, S, on_grads=None):
    B, L, D = x.shape
    T = B * L
    G = D // SSM_GROUP
    Pst = SSM_STATE
    hpg = D // HEAD_DIM
    HW = hpg * HEAD_DIM
    ncl = G // GROUPS_PER_CLUSTER
    x2 = x.reshape(T, D)
    tgt2 = tgt.reshape(T, D)

    disc = lambda *p: _s5_discretize(*p)
    (ab_r, ab_i, bb_r, bb_i), disc_vjp = jax.vjp(disc, S["lam_re"], S["lam_im"], S["log_dt"], S["b_re"], S["b_im"])
    wb = jnp.concatenate([_blockdiag(jnp.transpose(bb_r, (0, 2, 1))), _blockdiag(jnp.transpose(bb_i, (0, 2, 1)))],
                         axis=-1).astype(MXU_DTYPE)
    wc = jnp.concatenate([_blockdiag(jnp.transpose(S["c_re"], (0, 2, 1))), _blockdiag(-jnp.transpose(S["c_im"], (0, 2, 1)))],
                         axis=1).astype(MXU_DTYPE)
    cs = GROUPS_PER_CLUSTER * Pst
    slab = lambda ab: jnp.tile(jnp.transpose(ab.reshape(ncl, cs // LANES, LANES), (1, 0, 2)), (1, B, 1))
    a_r, a_i = slab(ab_r), slab(ab_i)
    d_row = S["d"].reshape(1, D)

    xi = _interleave(x2, B, L)
    y, yg, h_r, h_i = _s5_fwd(xi, wb, wc, a_r, a_i, d_row, B, name="s5_fwd")
    _anchor(W, yg)
    z = _mm_nn(yg, W["w_glu"], bias=S["b_glu"].reshape(1, D), name="glu_z")
    gate = _glu_gate(y, z, name="glu_gate")
    mix = _deinterleave(_mm_nn(gate, W["w_out"], name="s5_out"), B, L)
    h1, h1b, xh1, rs1 = _ln_fwd(x2, mix, S["ln_gain"][0, 0][None], S["ln_bias"][0, 0][None], name="ln_fwd_0a")

    def ffn_fwd(hb, l):
        hc = _mm_nn(hb, W["w_up"], l=l, out_dtype=MXU_DTYPE, name=f"ffn_up_{l}")
        a = _conv_glu_fwd(hc, S["conv_w"][l], S["conv_b"][l][None], L, name=f"ffn_conv_{l}")
        f = _mm_nn(a, W["w_down"], l=l, name=f"ffn_down_{l}")
        return hc, a, f

    _anchor(W, h1b)
    hc0, a0, f0 = ffn_fwd(h1b, 0)
    h2, h2b, xh2, rs2 = _ln_fwd(h1, f0, S["ln_gain"][0, 1][None], S["ln_bias"][0, 1][None], name="ln_fwd_0b")

    _anchor(W, h2b)
    kv = _mm_nn(h2b, W["w_kv"], name="attn_kv")
    q = _mm_nn(h2b, W["w_q"], name="attn_q")
    bias = _attn_bias(S["rel_bias"], hpg)
    o3, l3 = _attn_fwd(q, kv, bias, L, hpg, name="attn_fwd")
    o, ob, lse = _attn_merge(o3, l3, HW, name="attn_merge")
    att = _mm_nn(ob, W["w_ao"], name="attn_out")
    h3, h3b, xh3, rs3 = _ln_fwd(h2, att, S["ln_gain"][1, 0][None], S["ln_bias"][1, 0][None], name="ln_fwd_1a")
    hc1, a1, f1 = ffn_fwd(h3b, 1)
    h4, _, xh4, rs4 = _ln_fwd(h3, f1, S["ln_gain"][1, 1][None], S["ln_bias"][1, 1][None], name="ln_fwd_1b")

    dh4, lrow = _loss_grad(h4, tgt2, name="loss")
    loss = lrow[0, 0]

    GW, GS = {}, {}

    def ffn_bwd(dzb, hb, hc, a, l):
        da = _mm_nt(dzb, W["w_down"], l=l, out_dtype=MXU_DTYPE, name=f"ffn_down_bwd_x_{l}")
        GW[f"w_down{l}"] = _tn(a, dzb, ptotal=1, np_cols=D, name=f"ffn_down_bwd_w_{l}")
        dc, dcw, dcb = _conv_glu_bwd(hc, da, S["conv_w"][l], S["conv_b"][l][None], L, name=f"ffn_conv_bwd_{l}")
        dhc = _conv_bwd_input(dc, S["conv_w"][l], L, name=f"ffn_conv_bwd_x_{l}")
        dh = _mm_nt(dhc, W["w_up"], l=l, name=f"ffn_up_bwd_x_{l}")
        GW[f"w_up{l}"] = _tn(hb, dhc, ptotal=W["w_up"].shape[0], np_cols=W["w_up"].shape[3], name=f"ffn_up_bwd_w_{l}")
        return dh, dcw, dcb

    dz4, dz4b, dg4, db4 = _ln_bwd([dh4], [1.0], xh4, rs4, S["ln_gain"][1, 1][None], name="ln_bwd_1b")
    dh3f, dcw1, dcb1 = ffn_bwd(dz4b, h3b, hc1, a1, 1)
    dz3, dz3b, dg3, db3 = _ln_bwd([dz4, dh3f], [DN_ALPHA, 1.0], xh3, rs3, S["ln_gain"][1, 0][None], name="ln_bwd_1a")
    do = _mm_nt(dz3b, W["w_ao"], name="attn_out_bwd_x")
    GW["w_ao"] = _tn(ob, dz3b, ptotal=1, np_cols=D, name="attn_out_bwd_w")
    dq, dk, dv, ds_sum = _attn_bwd(q, kv, do, o, lse, bias, L, hpg, name="attn_bwd")
    GS["rel_bias"] = _bias_grad(ds_sum, hpg, name="attn_bias_grad")
    GW["w_q"] = _tn(h2b, dq, ptotal=W["w_q"].shape[0], np_cols=W["w_q"].shape[3], name="attn_q_bwd_w")
    pkv, npkv = W["w_kv"].shape[0], W["w_kv"].shape[3]
    gkv = _tn(h2b, dk, ptotal=pkv, np_cols=npkv, p0=0, name="attn_k_bwd_w")
    GW["w_kv"] = _tn(h2b, dv, ptotal=pkv, np_cols=npkv, p0=pkv // 2, prev=gkv, name="attn_v_bwd_w")
    dh2q = _mm_nt(dq, W["w_q"], name="attn_q_bwd_x")
    dh2k = _mm_nt(dk, W["w_kv"], p0=0, pn=pkv // 2, name="attn_k_bwd_x")
    dh2v = _mm_nt(dv, W["w_kv"], p0=pkv // 2, pn=pkv // 2, name="attn_v_bwd_x")

    gain_0b = S["ln_gain"][0, 1][None]
    if on_grads is not None:
        gain_0b = gain_0b + on_grads(0, GW)[0, 0]

    dz2, dz2b, dg2, db2 = _ln_bwd([dz3, dh2q, dh2k, dh2v], [DN_ALPHA, 1.0, 1.0, 1.0], xh2, rs2, gain_0b,
                                  name="ln_bwd_0b")
    dh1f, dcw0, dcb0 = ffn_bwd(dz2b, h1b, hc0, a0, 0)
    gain_0a = S["ln_gain"][0, 0][None]
    if on_grads is not None:
        gain_0a = gain_0a + on_grads(1, GW)[0, 0]
    dz1, dz1b, dg1, db1 = _ln_bwd([dz2, dh1f], [DN_ALPHA, 1.0], xh1, rs1, gain_0a, name="ln_bwd_0a")
    dmix_i = _interleave(dz1b, B, L)
    dgate = _mm_nt(dmix_i, W["w_out"], name="s5_out_bwd_x")
    GW["w_out"] = _tn(gate, dmix_i, ptotal=1, np_cols=D, name="s5_out_bwd_w")
    dzg, dyg1, dbglu = _glu_bwd(y, z, dgate, name="glu_bwd")
    dyg2 = _mm_nt(dzg, W["w_glu"], name="glu_z_bwd_x")
    GW["w_glu"] = _tn(yg, dzg, ptotal=1, np_cols=D, name="glu_z_bwd_w")
    dy = _gelu_bwd(y, dyg1, dyg2, name="gelu_bwd")
    if on_grads is not None:
        d_row = d_row + on_grads(2, GW)[0, 0]
    du_i, g_r, g_i, dar, dai, dd = _s5_bwd(dy, xi, h_r, h_i, wb, wc, a_r, a_i, d_row, B, name="s5_bwd")
    dwb_r = _cluster_tn(xi, g_r, ncl, tok_left=True, name="s5_b_grad_re")
    dwb_i = _cluster_tn(xi, g_i, ncl, tok_left=True, name="s5_b_grad_im")
    dwc_r = _cluster_tn(dy, h_r, ncl, tok_left=False, name="s5_c_grad_re")
    dwc_i = _cluster_tn(dy, h_i, ncl, tok_left=False, name="s5_c_grad_im")
    grad_x = _axpy(dz1, _deinterleave(du_i, B, L), DN_ALPHA, name="grad_x")

    dbb_r = jnp.transpose(_unblockdiag(dwb_r, SSM_GROUP, Pst), (0, 2, 1))
    dbb_i = jnp.transpose(_unblockdiag(dwb_i, SSM_GROUP, Pst), (0, 2, 1))
    unslab = lambda da: jnp.transpose(da.reshape(cs // LANES, B, ncl, LANES).sum(1), (1, 0, 2)).reshape(G, Pst)
    dab_r, dab_i = unslab(dar), unslab(dai)
    GS["lam_re"], GS["lam_im"], GS["log_dt"], GS["b_re"], GS["b_im"] = disc_vjp((dab_r, dab_i, dbb_r, dbb_i))
    GS["c_re"] = jnp.transpose(_unblockdiag(dwc_r, Pst, SSM_GROUP), (0, 2, 1))
    GS["c_im"] = -jnp.transpose(_unblockdiag(dwc_i, Pst, SSM_GROUP), (0, 2, 1))
    GS["d"] = dd.reshape(G, SSM_GROUP)
    GS["b_glu"] = dbglu.reshape(D)
    GS["conv_w"] = jnp.stack([dcw0, dcw1])
    GS["conv_b"] = jnp.stack([dcb0[0], dcb1[0]])
    GS["ln_gain"] = jnp.stack([jnp.stack([dg1[0], dg2[0]]), jnp.stack([dg3[0], dg4[0]])])
    GS["ln_bias"] = jnp.stack([jnp.stack([db1[0], db2[0]]), jnp.stack([db3[0], db4[0]])])
    return loss, grad_x.reshape(B, L, D), GW, GS


SMALL_REPLICATED = ("lam_re", "lam_im", "log_dt", "b_re", "b_im", "c_re", "c_im", "d", "rel_bias", "conv_b")
SMALL_SHARDED = ("b_glu", "conv_w", "ln_gain", "ln_bias")
SMALL_ORDER = SMALL_REPLICATED + SMALL_SHARDED


def _pack(arrs, lanes, row_mult):
    flat = jnp.concatenate([a.reshape(-1).astype(F32) for a in arrs])
    rows = -(-flat.shape[0] // lanes)
    rows = -(-rows // row_mult) * row_mult
    return jnp.pad(flat, (0, rows * lanes - flat.shape[0])).reshape(rows, lanes)


def _unpack(packed, shapes):
    flat = packed.reshape(-1)
    out, off = [], 0
    for s in shapes:
        n = int(np.prod(s))
        out.append(flat[off:off + n].reshape(s))
        off += n
    return out


def kernel(x, s5_lam_re, s5_lam_im, s5_log_dt, s5_b_re, s5_b_im, s5_c_re, s5_c_im, s5_d, s5_w_glu, s5_b_glu, s5_w_out, attn_w_kv, attn_w_q, attn_w_out, rel_bias, ffn_w_up, ffn_conv_w, ffn_conv_b, ffn_w_down, ln_gain, ln_bias, loss_target, m_s5_lam_re, m_s5_lam_im, m_s5_log_dt, m_s5_b_re, m_s5_b_im, m_s5_c_re, m_s5_c_im, m_s5_d, m_s5_w_glu, m_s5_b_glu, m_s5_w_out, m_attn_w_kv, m_attn_w_q, m_attn_w_out, m_rel_bias, m_ffn_w_up, m_ffn_conv_w, m_ffn_conv_b, m_ffn_w_down, m_ln_gain, m_ln_bias, v_s5_lam_re, v_s5_lam_im, v_s5_log_dt, v_s5_b_re, v_s5_b_im, v_s5_c_re, v_s5_c_im, v_s5_d, v_s5_w_glu, v_s5_b_glu, v_s5_w_out, v_attn_w_kv, v_attn_w_q, v_attn_w_out, v_rel_bias, v_ffn_w_up, v_ffn_conv_w, v_ffn_conv_b, v_ffn_w_down, v_ln_gain, v_ln_bias):
    names = ["s5_lam_re", "s5_lam_im", "s5_log_dt", "s5_b_re", "s5_b_im", "s5_c_re", "s5_c_im", "s5_d", "s5_w_glu",
             "s5_b_glu", "s5_w_out", "attn_w_kv", "attn_w_q", "attn_w_out", "rel_bias", "ffn_w_up", "ffn_conv_w",
             "ffn_conv_b", "ffn_w_down", "ln_gain", "ln_bias"]
    loc = locals()
    w_in = {n: loc[n] for n in names}
    m_in = {n: loc["m_" + n] for n in names}
    v_in = {n: loc["v_" + n] for n in names}
    chip = 2 * lax.axis_index("x") + lax.axis_index("y")
    core = lax.axis_index("c")
    chip_idx = jnp.reshape(chip, (1,)).astype(jnp.int32)
    cidx = jnp.reshape(core, (1,)).astype(jnp.int32)

    big = [("w_glu", "s5_w_glu", "rows"), ("w_out", "s5_w_out", "rows"), ("w_ao", "attn_w_out", "rows"),
           ("w_kv", "attn_w_kv", "cols"), ("w_q", "attn_w_q", "cols"),
           ("w_up", "ffn_w_up", "layer_cols"), ("w_down", "ffn_w_down", "layer_rows")]

    def halves(t, kind):
        if kind.startswith("layer"):
            return t
        r, c = t.shape[-2:]
        return t.reshape(2, r // 2, c)

    def to_weight(g, kind):
        _, _, r, c = g.shape
        if kind == "rows":
            return g.reshape(1, 1, 8 * r, c)
        if kind == "cols":
            return g.reshape(4, 1, 2 * r, c)
        if kind == "layer_cols":
            return g
        return jnp.transpose(g, (1, 0, 2, 3)).reshape(1, 2, 4 * r, c)

    small_sh = {"b_glu": s5_b_glu[0], "conv_w": ffn_conv_w, "ln_gain": ln_gain, "ln_bias": ln_bias}
    sh_shapes = [small_sh[k].shape for k in SMALL_SHARDED]
    sh_pack = _pack([small_sh[k] for k in SMALL_SHARDED], 128, 16)

    shards = [halves(w_in[src].astype(MXU_DTYPE), kind) for _, src, kind in big]
    shards.append(sh_pack.reshape(2, sh_pack.shape[0] // 2, 128))
    shard_of = {key: s for (key, _, _), s in zip(big, shards)}
    shard_of["small"] = shards[-1]
    kind_of = {key: kind for key, _, kind in big}

    group_keys = [["w_glu", "w_out", "small"], ["w_up", "w_down"], ["w_kv", "w_q", "w_ao"]]
    started, token = _gather_start([[shard_of[k] for k in g] for g in group_keys], name="weights_gather_start")

    def finish_group(gi, after):
        ssem, rsem, thru, lands = started[gi]
        lands = _gather_wait(ssem, rsem, thru, lands, after, name=f"weights_gather_wait_{gi}")
        lands = _gather_forward(lands, name=f"weights_gather_forward_{gi}")
        out = {}
        for key, land in zip(group_keys[gi], lands):
            full = lax.dynamic_update_slice(land, shard_of[key][None], (chip, 0, 0, 0))
            if key == "small":
                parts = [_unpack(full[p], sh_shapes) for p in range(4)]
                for i, k in enumerate(SMALL_SHARDED):
                    out[k] = jnp.concatenate([parts[p][i] for p in range(4)], axis=-1)
            else:
                out[key] = to_weight(full, kind_of[key])
        return out

    replicated = dict(lam_re=s5_lam_re[0], lam_im=s5_lam_im[0], log_dt=s5_log_dt[0], b_re=s5_b_re[0], b_im=s5_b_im[0],
                      c_re=s5_c_re[0], c_im=s5_c_im[0], rel_bias=rel_bias, conv_b=ffn_conv_b,
                      d=s5_d[0] + token[0, 0])
    group_of = {k: gi for gi, g in enumerate(group_keys) for k in g if k != "small"}
    group_of.update({k: 0 for k in SMALL_SHARDED})
    group_of.update({k: "replicated" for k in replicated})
    params = _Lazy(group_of, lambda g, after: replicated if g == "replicated" else finish_group(g, after))

    red = [("w_up1", "ffn_w_up", 1), ("w_down1", "ffn_w_down", 1), ("w_ao", "attn_w_out", 0), ("w_kv", "attn_w_kv", 0),
           ("w_q", "attn_w_q", 0), ("w_down0", "ffn_w_down", 0), ("w_up0", "ffn_w_up", 0), ("w_out", "s5_w_out", 0),
           ("w_glu", "s5_w_glu", 0)]
    stages = [red[:5], red[5:7], red[7:]]

    def grad_halves(gw, key, src):
        r, c = w_in[src].shape[-2:]
        return gw[key].reshape(4, 2, r // 2, c)

    sent, early = {}, []

    def on_grads(stage, gw):
        tokens = []
        if stage > 0:
            tag = "abc"[stage - 1]
            ga, theirs = _pair_send_wait(sent.pop(stage - 1), gw[stages[stage][-1][0]], name=f"rs_pair_send_wait_{tag}")
            hf, hx = _pair_sums(ga, [True] * len(ga), cidx, tag, theirs)
            started, tok = _chip_exchange_start(hx, name=f"rs_chip_exchange_start_{tag}")
            early.append((hf, started, tag))
            tokens.append(tok)
        ga = [grad_halves(gw, key, src) for key, src, _ in stages[stage]]
        sent[stage], tok = _pair_send_start(ga, name=f"rs_pair_send_start_{'abc'[stage]}")
        return sum(tokens, tok)

    loss, grad_x, GW, GS = _local_step(x, loss_target, params, params, on_grads)
    loss = lax.psum(loss, ("x", "y", "c"))

    gs_shapes = [GS[k].shape for k in SMALL_ORDER]
    gs_pack = _pack([GS[k] for k in SMALL_ORDER], 128, 64)
    rs = gs_pack.shape[0] // 8
    gs_halves = [gs_pack.reshape(4, 2, rs, 128)]
    gl, theirs_l = _pair_send_wait(sent.pop(2), grad_x, name="rs_pair_send_wait_c")
    theirs_l += _pair_send(gs_halves, name="rs_pair_send_small")
    gl += gs_halves
    hf_l, hx_l = _pair_sums(gl, [True] * (len(gl) - 1) + [False], cidx, "c", theirs_l)
    mine = []
    for hf, started, tag in early:
        got = _chip_exchange_wait(started, grad_x, name=f"rs_chip_exchange_wait_{tag}")
        mine += _chip_sums(hf, got, chip_idx, tag)
    mine += _chip_sums(hf_l, _chip_exchange(hx_l, name="rs_chip_exchange_c"), chip_idx, "c")
    other = _pair_swap(mine, name="rs_pair_swap")
    small_halves = jnp.where(core == 0, jnp.concatenate([mine[-1], other[-1]]), jnp.concatenate([other[-1], mine[-1]]))
    small_all = _gather([small_halves], chip, name="small_grads_all_gather")[0]
    gsmall = dict(zip(SMALL_ORDER, _unpack(small_all, gs_shapes)))

    big_res = {}
    for (key, src, layer), gm, go in zip(red, mine[:-1], other[:-1]):
        nl = w_in[src].shape[0] if src in ("ffn_w_up", "ffn_w_down") else 1
        r, c = w_in[src].shape[-2:]
        view = lambda t: t.reshape(nl, 2, r // 2, c)
        res4 = _adamw_halves(view(w_in[src]), view(m_in[src]), view(v_in[src]), gm, go, cidx, layer=layer,
                             prev=big_res.get(src), name=f"adamw_{key}")
        big_res[src] = res4
    big_res = {src: tuple(t.reshape(w_in[src].shape) for t in res4) for src, res4 in big_res.items()}

    def big_out(i):
        return {src: big_res[src][i] for _, src, _ in big}

    small_w = {"lam_re": s5_lam_re, "lam_im": s5_lam_im, "log_dt": s5_log_dt, "b_re": s5_b_re, "b_im": s5_b_im,
               "c_re": s5_c_re, "c_im": s5_c_im, "d": s5_d, "rel_bias": rel_bias, "conv_b": ffn_conv_b,
               "b_glu": s5_b_glu, "conv_w": ffn_conv_w, "ln_gain": ln_gain, "ln_bias": ln_bias}
    small_name = {"lam_re": "s5_lam_re", "lam_im": "s5_lam_im", "log_dt": "s5_log_dt", "b_re": "s5_b_re", "b_im": "s5_b_im",
                  "c_re": "s5_c_re", "c_im": "s5_c_im", "d": "s5_d", "rel_bias": "rel_bias", "conv_b": "ffn_conv_b",
                  "b_glu": "s5_b_glu", "conv_w": "ffn_conv_w", "ln_gain": "ln_gain", "ln_bias": "ln_bias"}
    sg = {}
    for k in SMALL_ORDER:
        shp = small_w[k].shape
        g = gsmall[k]
        if k in SMALL_SHARDED:
            width = shp[-1]
            g = lax.dynamic_slice_in_dim(g, chip * width, width, axis=g.ndim - 1)
        sg[k] = g.reshape(shp)
    sd, snm, snv = {}, {}, {}
    for k in SMALL_ORDER:
        shp = small_w[k].shape
        flat = lambda t: t.reshape(-1, shp[-1])
        r3 = _adamw(flat(small_w[k]), flat(sg[k]), flat(m_in[small_name[k]]), flat(v_in[small_name[k]]),
                    name=f"adamw_{k}")
        sd[k], snm[k], snv[k] = (t.reshape(shp) for t in r3)

    res = [{}, {}, {}, {}]
    for i in range(4):
        res[i].update(big_out(i))
    for k in SMALL_ORDER:
        res[0][small_name[k]] = sg[k]
        res[1][small_name[k]] = sd[k]
        res[2][small_name[k]] = snm[k]
        res[3][small_name[k]] = snv[k]
    outs = [loss, grad_x]
    for i in range(4):
        outs += [res[i][n] for n in names]
    return tuple(outs)
```

```python
import functools
import math

import numpy as np
import jax
import jax.numpy as jnp
from jax import lax
from jax.experimental import pallas as pl
from jax.experimental.pallas import tpu as pltpu

F32 = jnp.float32
BF16 = jnp.bfloat16
MXU_DTYPE = jnp.bfloat16
V7X_VMEM_LIMIT_BYTES = 52 << 20
MESH = pl.DeviceIdType.MESH

DEPTH = 2
SSM_GROUP = 16
SSM_STATE = 64
GROUPS_PER_CLUSTER = 16
CLUSTER_W = GROUPS_PER_CLUSTER * SSM_GROUP
HEAD_DIM = 64
DILATIONS = (1, 4, 16)
BAND = 128
NEG_BIG = -1e30
REL_BUCKETS = 32
REL_MAX_DIST = 2048
DN_ALPHA = (2.0 * DEPTH) ** 0.25
LN_EPS = 1e-5
ADAM_LR, ADAM_B1, ADAM_B2, ADAM_EPS, ADAM_WD, ADAM_STEP = 0.001, 0.9, 0.999, 1e-08, 0.01, 10
GELU_K = math.sqrt(2.0 / math.pi)
GELU_C = 0.044715


def _pallas(body, **kw):
    return pl.pallas_call(body, **kw)


def _params(sem=None):
    return pltpu.CompilerParams(dimension_semantics=sem, vmem_limit_bytes=V7X_VMEM_LIMIT_BYTES)


def _pick(n, cands):
    for c in cands:
        if n % c == 0:
            return c
    return n


def _sigmoid(z):
    return 1.0 / (1.0 + jnp.exp(-z))


def _gelu(y):
    return 0.5 * y * (1.0 + jnp.tanh(GELU_K * (y + GELU_C * y * y * y)))


def _gelu_grad(y):
    t = jnp.tanh(GELU_K * (y + GELU_C * y * y * y))
    return 0.5 * (1.0 + t) + 0.5 * y * (1.0 - t * t) * (GELU_K * (1.0 + 3.0 * GELU_C * y * y))


def _mm_nn(a, w, *, l=0, bias=None, out_dtype=F32, name):
    T, K = a.shape
    P, _, _, Np = w.shape
    tm = _pick(T, (1024, 512, 256, 128))
    tn = _pick(Np, (1408, 1024, 768, 512, 384, 256, 128))
    nj = Np // tn

    def body(*refs):
        if bias is None:
            a_ref, w_ref, o_ref = refs
        else:
            a_ref, w_ref, b_ref, o_ref = refs
        acc = jnp.dot(a_ref[...].astype(MXU_DTYPE), w_ref[...].astype(MXU_DTYPE), preferred_element_type=F32)
        if bias is not None:
            acc = acc + b_ref[...]
        o_ref[...] = acc.astype(o_ref.dtype)

    in_specs = [pl.BlockSpec((tm, K), lambda p, j, i: (i, 0)),
                pl.BlockSpec((None, None, K, tn), lambda p, j, i: (p, l, 0, j))]
    args = [a, w]
    if bias is not None:
        in_specs.append(pl.BlockSpec((1, tn), lambda p, j, i: (0, p * nj + j)))
        args.append(bias)
    return _pallas(
        body, name=name, grid=(P, nj, T // tm), in_specs=in_specs,
        out_specs=pl.BlockSpec((tm, tn), lambda p, j, i: (i, p * nj + j)),
        out_shape=jax.ShapeDtypeStruct((T, P * Np), out_dtype),
        compiler_params=_params(("parallel", "parallel", "parallel")),
    )(*args)


def _mm_nt(a, w, *, l=0, p0=0, pn=None, out_dtype=F32, name):
    T = a.shape[0]
    _, _, K, Np = w.shape
    pn = w.shape[0] if pn is None else pn
    tm = _pick(T, (1024, 512, 256, 128) if K <= 1024 else (512, 256, 128))
    tn = _pick(Np, (1536, 1408, 1024, 768, 512, 384, 256, 128))
    nj = Np // tn
    nred = pn * nj

    def body(a_ref, w_ref, o_ref, acc):
        r = pl.program_id(1)

        @pl.when(r == 0)
        def _():
            acc[...] = jnp.zeros_like(acc)

        acc[...] += lax.dot_general(a_ref[...].astype(MXU_DTYPE), w_ref[...].astype(MXU_DTYPE),
                                    (((1,), (1,)), ((), ())), preferred_element_type=F32)

        @pl.when(r == nred - 1)
        def _():
            o_ref[...] = acc[...].astype(o_ref.dtype)

    return _pallas(
        body, name=name, grid=(T // tm, nred),
        in_specs=[pl.BlockSpec((tm, tn), lambda i, r: (i, r)),
                  pl.BlockSpec((None, None, K, tn), lambda i, r: (p0 + r // nj, l, 0, r % nj))],
        out_specs=pl.BlockSpec((tm, K), lambda i, r: (i, 0)),
        out_shape=jax.ShapeDtypeStruct((T, K), out_dtype),
        scratch_shapes=[pltpu.VMEM((tm, K), F32)],
        compiler_params=_params(("parallel", "arbitrary")),
    )(a, w)


def _tn(a, b, *, ptotal, np_cols, nl=1, l=0, p0=0, prev=None, name):
    T, K = a.shape
    Np = np_cols
    pn = b.shape[1] // Np
    tt = _pick(T, (1024, 512, 256, 128))
    tk = _pick(K, (1408, 1024, 512, 256, 128))
    tn = _pick(Np, (1408, 768, 512, 256, 128))
    if tk * tn > 1408 * 1024:
        tn = _pick(Np, (512, 256, 128))
    nj = Np // tn
    nt = T // tt

    def body(*refs):
        a_ref, b_ref = refs[0], refs[1]
        o_ref, acc = refs[-2], refs[-1]
        t = pl.program_id(3)

        @pl.when(t == 0)
        def _():
            acc[...] = jnp.zeros_like(acc)

        acc[...] += lax.dot_general(a_ref[...].astype(MXU_DTYPE), b_ref[...].astype(MXU_DTYPE),
                                    (((0,), (0,)), ((), ())), preferred_element_type=F32)

        @pl.when(t == nt - 1)
        def _():
            o_ref[...] = acc[...]

    in_specs = [pl.BlockSpec((tt, tk), lambda kb, p, j, t: (t, kb)),
                pl.BlockSpec((tt, tn), lambda kb, p, j, t: (t, p * nj + j))]
    args = [a, b]
    aliases = {}
    if prev is not None:
        in_specs.append(pl.BlockSpec(memory_space=pl.ANY))
        args.append(prev)
        aliases = {2: 0}
    return _pallas(
        body, name=name, grid=(K // tk, pn, nj, nt), in_specs=in_specs,
        out_specs=pl.BlockSpec((None, None, tk, tn), lambda kb, p, j, t: (p0 + p, l, kb, j)),
        out_shape=jax.ShapeDtypeStruct((ptotal, nl, K, Np), F32),
        scratch_shapes=[pltpu.VMEM((tk, tn), F32)],
        input_output_aliases=aliases,
        compiler_params=_params(("parallel", "parallel", "parallel", "arbitrary")),
    )(*args)


def _rows(tm, f):
    return pl.BlockSpec((tm, f), lambda i: (i, 0))


def _whole(shape):
    nd = len(shape)
    return pl.BlockSpec(shape, lambda i: (0,) * nd)


def _ln_fwd(xres, f, gain, bias, *, name):
    T, D = xres.shape
    tm = _pick(T, (256, 128))

    def body(x_ref, f_ref, g_ref, b_ref, y_ref, yb_ref, xh_ref, rs_ref):
        z = DN_ALPHA * x_ref[...] + f_ref[...]
        mu = jnp.mean(z, axis=-1, keepdims=True)
        zc = z - mu
        var = jnp.mean(zc * zc, axis=-1, keepdims=True)
        rstd = lax.rsqrt(var + LN_EPS)
        xh = zc * rstd
        y = xh * g_ref[...] + b_ref[...]
        y_ref[...] = y
        yb_ref[...] = y.astype(yb_ref.dtype)
        xh_ref[...] = xh
        rs_ref[...] = rstd

    return _pallas(
        body, name=name, grid=(T // tm,),
        in_specs=[_rows(tm, D), _rows(tm, D), _whole((1, D)), _whole((1, D))],
        out_specs=[_rows(tm, D), _rows(tm, D), _rows(tm, D), _rows(tm, 1)],
        out_shape=[jax.ShapeDtypeStruct((T, D), F32), jax.ShapeDtypeStruct((T, D), MXU_DTYPE),
                   jax.ShapeDtypeStruct((T, D), F32), jax.ShapeDtypeStruct((T, 1), F32)],
        compiler_params=_params(("parallel",)),
    )(xres, f, gain, bias)


def _ln_bwd(addends, coefs, xhat, rstd, gain, *, name):
    T, D = xhat.shape
    tm = _pick(T, (256, 128))
    n = len(addends)

    def body(*refs):
        adds = refs[:n]
        xh_ref, rs_ref, g_ref, dz_ref, dzb_ref, dg_ref, db_ref = refs[n:]
        dy = coefs[0] * adds[0][...]
        for c, r in zip(coefs[1:], adds[1:]):
            dy = dy + c * r[...]
        xh = xh_ref[...]
        dxh = dy * g_ref[...]
        m1 = jnp.mean(dxh, axis=-1, keepdims=True)
        m2 = jnp.mean(dxh * xh, axis=-1, keepdims=True)
        dz = rs_ref[...] * (dxh - m1 - xh * m2)
        dz_ref[...] = dz
        dzb_ref[...] = dz.astype(dzb_ref.dtype)

        @pl.when(pl.program_id(0) == 0)
        def _():
            dg_ref[...] = jnp.zeros_like(dg_ref)
            db_ref[...] = jnp.zeros_like(db_ref)

        dg_ref[...] += jnp.sum(dy * xh, axis=0, keepdims=True)
        db_ref[...] += jnp.sum(dy, axis=0, keepdims=True)

    return _pallas(
        body, name=name, grid=(T // tm,),
        in_specs=[_rows(tm, D)] * n + [_rows(tm, D), _rows(tm, 1), _whole((1, D))],
        out_specs=[_rows(tm, D), _rows(tm, D), _whole((1, D)), _whole((1, D))],
        out_shape=[jax.ShapeDtypeStruct((T, D), F32), jax.ShapeDtypeStruct((T, D), MXU_DTYPE),
                   jax.ShapeDtypeStruct((1, D), F32), jax.ShapeDtypeStruct((1, D), F32)],
        compiler_params=_params(("arbitrary",)),
    )(*addends, xhat, rstd, gain)


def _loss_grad(y, tgt, *, name):
    T, D = y.shape
    tm = _pick(T, (256, 128))

    def body(y_ref, t_ref, dy_ref, l_ref):
        e = y_ref[...] - t_ref[...]
        dy_ref[...] = e * (1.0 / D)

        @pl.when(pl.program_id(0) == 0)
        def _():
            l_ref[...] = jnp.zeros_like(l_ref)

        l_ref[...] += jnp.zeros_like(l_ref) + jnp.sum(e * e) * (0.5 / D)

    return _pallas(
        body, name=name, grid=(T // tm,),
        in_specs=[_rows(tm, D), _rows(tm, D)],
        out_specs=[_rows(tm, D), _whole((1, 128))],
        out_shape=[jax.ShapeDtypeStruct((T, D), F32), jax.ShapeDtypeStruct((1, 128), F32)],
        compiler_params=_params(("arbitrary",)),
    )(y, tgt)


def _axpy(a, b, ca, *, name):
    T, D = a.shape
    tm = _pick(T, (256, 128))

    def body(a_ref, b_ref, o_ref):
        o_ref[...] = ca * a_ref[...] + b_ref[...]

    return _pallas(
        body, name=name, grid=(T // tm,), in_specs=[_rows(tm, D), _rows(tm, D)], out_specs=_rows(tm, D),
        out_shape=jax.ShapeDtypeStruct((T, D), F32), compiler_params=_params(("parallel",)),
    )(a, b)


def _glu_gate(y, z, *, name):
    T, D = y.shape
    tm = _pick(T, (256, 128))

    def body(y_ref, z_ref, g_ref):
        g_ref[...] = (_gelu(y_ref[...]) * _sigmoid(z_ref[...])).astype(g_ref.dtype)

    return _pallas(
        body, name=name, grid=(T // tm,), in_specs=[_rows(tm, D), _rows(tm, D)], out_specs=_rows(tm, D),
        out_shape=jax.ShapeDtypeStruct((T, D), MXU_DTYPE), compiler_params=_params(("parallel",)),
    )(y, z)


def _glu_bwd(y, z, dg, *, name):
    T, D = y.shape
    tm = _pick(T, (256, 128))

    def body(y_ref, z_ref, dg_ref, dzb_ref, dyg_ref, db_ref):
        s = _sigmoid(z_ref[...])
        dg = dg_ref[...]
        dz = dg * _gelu(y_ref[...]) * s * (1.0 - s)
        dzb_ref[...] = dz.astype(dzb_ref.dtype)
        dyg_ref[...] = dg * s

        @pl.when(pl.program_id(0) == 0)
        def _():
            db_ref[...] = jnp.zeros_like(db_ref)

        db_ref[...] += jnp.sum(dz, axis=0, keepdims=True)

    return _pallas(
        body, name=name, grid=(T // tm,), in_specs=[_rows(tm, D)] * 3,
        out_specs=[_rows(tm, D), _rows(tm, D), _whole((1, D))],
        out_shape=[jax.ShapeDtypeStruct((T, D), MXU_DTYPE), jax.ShapeDtypeStruct((T, D), F32),
                   jax.ShapeDtypeStruct((1, D), F32)],
        compiler_params=_params(("arbitrary",)),
    )(y, z, dg)


def _gelu_bwd(y, d1, d2, *, name):
    T, D = y.shape
    tm = _pick(T, (256, 128))

    def body(y_ref, a_ref, b_ref, o_ref):
        o_ref[...] = (a_ref[...] + b_ref[...]) * _gelu_grad(y_ref[...])

    return _pallas(
        body, name=name, grid=(T // tm,), in_specs=[_rows(tm, D)] * 3, out_specs=_rows(tm, D),
        out_shape=jax.ShapeDtypeStruct((T, D), F32), compiler_params=_params(("parallel",)),
    )(y, d1, d2)


CONV_ROWS = 128
CONV_EDGE = 16


def _row_shifts(x, edge, drop_edge, tm, back):
    keep = jnp.where(drop_edge, 0.0, 1.0).astype(edge.dtype)
    ext = jnp.concatenate([edge * keep, x] if back else [x, edge * keep], axis=0)
    row = lax.broadcasted_iota(jnp.int32, (tm, tm + CONV_EDGE), 0)
    col = lax.broadcasted_iota(jnp.int32, (tm, tm + CONV_EDGE), 1)
    base = row + CONV_EDGE if back else row
    out = []
    for k in (1, 2):
        pick = (col == (base - k if back else base + k)).astype(x.dtype)
        out.append(jnp.dot(pick, ext, preferred_element_type=F32))
    return out


def _conv_specs(T, F2, tm):
    return [_rows(tm, F2),
            pl.BlockSpec((CONV_EDGE, F2), lambda i: (jnp.maximum(i * (tm // CONV_EDGE) - 1, 0), 0))]


def _conv_glu_fwd(hc, conv_w, conv_b, L, *, name):
    T, F2 = hc.shape
    F = F2 // 2
    tm = CONV_ROWS

    def body(x_ref, e_ref, w_ref, b_ref, a_ref):
        at_start = (pl.program_id(0) * tm) % L == 0
        x1, x2 = _row_shifts(x_ref[...], e_ref[...], at_start, tm, True)
        x = x_ref[...].astype(F32)
        c = b_ref[...] + w_ref[0:1, :] * x + w_ref[1:2, :] * x1 + w_ref[2:3, :] * x2
        val, gate = c[:, :F], c[:, F:]
        a_ref[...] = (gate * _sigmoid(gate) * val).astype(a_ref.dtype)

    return _pallas(
        body, name=name, grid=(T // tm,),
        in_specs=_conv_specs(T, F2, tm) + [_whole((3, F2)), _whole((1, F2))],
        out_specs=_rows(tm, F),
        out_shape=jax.ShapeDtypeStruct((T, F), MXU_DTYPE), compiler_params=_params(("parallel",)),
    )(hc, hc, conv_w, conv_b)


def _conv_glu_bwd(hc, da, conv_w, conv_b, L, *, name):
    T, F2 = hc.shape
    F = F2 // 2
    tm = CONV_ROWS

    def body(x_ref, e_ref, da_ref, w_ref, b_ref, dc_ref, dw_ref, db_ref):
        at_start = (pl.program_id(0) * tm) % L == 0
        x1, x2 = _row_shifts(x_ref[...], e_ref[...], at_start, tm, True)
        x = x_ref[...].astype(F32)
        c = b_ref[...] + w_ref[0:1, :] * x + w_ref[1:2, :] * x1 + w_ref[2:3, :] * x2
        val, gate = c[:, :F], c[:, F:]
        s = _sigmoid(gate)
        da = da_ref[...].astype(F32)
        dval = da * (gate * s)
        dgate = da * val * (s * (1.0 + gate * (1.0 - s)))
        dc = jnp.concatenate([dval, dgate], axis=-1)
        dc_ref[...] = dc.astype(dc_ref.dtype)

        @pl.when(pl.program_id(0) == 0)
        def _():
            dw_ref[...] = jnp.zeros_like(dw_ref)
            db_ref[...] = jnp.zeros_like(db_ref)

        dw_ref[0:1, :] += jnp.sum(dc * x, axis=0, keepdims=True)
        dw_ref[1:2, :] += jnp.sum(dc * x1, axis=0, keepdims=True)
        dw_ref[2:3, :] += jnp.sum(dc * x2, axis=0, keepdims=True)
        db_ref[...] += jnp.sum(dc, axis=0, keepdims=True)

    return _pallas(
        body, name=name, grid=(T // tm,),
        in_specs=_conv_specs(T, F2, tm) + [_rows(tm, F), _whole((3, F2)), _whole((1, F2))],
        out_specs=[_rows(tm, F2), _whole((3, F2)), _whole((1, F2))],
        out_shape=[jax.ShapeDtypeStruct((T, F2), MXU_DTYPE), jax.ShapeDtypeStruct((3, F2), F32),
                   jax.ShapeDtypeStruct((1, F2), F32)],
        compiler_params=_params(("arbitrary",)),
    )(hc, hc, da, conv_w, conv_b)


def _conv_bwd_input(dc, conv_w, L, *, name):
    T, F2 = dc.shape
    tm = CONV_ROWS
    edge = CONV_EDGE
    last_blk = T // edge - 1

    def body(x_ref, e_ref, w_ref, o_ref):
        at_end = ((pl.program_id(0) + 1) * tm) % L == 0
        x1, x2 = _row_shifts(x_ref[...], e_ref[...], at_end, tm, False)
        x = x_ref[...].astype(F32)
        o_ref[...] = (w_ref[0:1, :] * x + w_ref[1:2, :] * x1 + w_ref[2:3, :] * x2).astype(o_ref.dtype)

    return _pallas(
        body, name=name, grid=(T // tm,),
        in_specs=[_rows(tm, F2),
                  pl.BlockSpec((edge, F2), lambda i: (jnp.minimum((i + 1) * (tm // edge), last_blk), 0)),
                  _whole((3, F2))],
        out_specs=_rows(tm, F2),
        out_shape=jax.ShapeDtypeStruct((T, F2), MXU_DTYPE), compiler_params=_params(("parallel",)),
    )(dc, dc, conv_w)


S5_CHUNK = 128
LANES = 128


def _slab_rows(c, n, ncl):
    return pl.ds(c, n) if ncl == 1 else pl.ds(c, n, stride=ncl)


def _slab_put(ref, c, n, ncl, val):
    for s in range(val.shape[1] // LANES):
        ref[s, _slab_rows(c, n, ncl), :] = val[:, s * LANES:(s + 1) * LANES]


def _slab_get(ref, c, n, ncl):
    return jnp.concatenate([ref[s, _slab_rows(c, n, ncl), :] for s in range(ref.shape[0])], axis=-1)


def _slabs(n_slab, rows):
    return pl.BlockSpec((n_slab, rows, LANES), lambda i: (0, i, 0))


def _s5_fwd(xi, wb, wc, a_r, a_i, d_row, B, *, name):
    T, D = xi.shape
    ncl = wb.shape[0]
    cs = wb.shape[2] // 2
    ns = cs // LANES
    R = B * ncl
    Q = S5_CHUNK
    QR = Q * ncl
    nsteps = Q // B

    def body(x_ref, wb_ref, wc_ref, ar_ref, ai_ref, d_ref, y_ref, yg_ref, hr_ref, hi_ref, bur, bui, cr, ci):
        @pl.when(pl.program_id(0) == 0)
        def _():
            cr[...] = jnp.zeros_like(cr)
            ci[...] = jnp.zeros_like(ci)

        x = x_ref[...]
        xb = x.astype(MXU_DTYPE)
        for c in range(ncl):
            bu = jnp.dot(xb[:, c * CLUSTER_W:(c + 1) * CLUSTER_W], wb_ref[c], preferred_element_type=F32)
            _slab_put(bur, c, Q, ncl, bu[:, :cs])
            _slab_put(bui, c, Q, ncl, bu[:, cs:])
        ar = ar_ref[...]
        ai = ai_ref[...]

        def step(k, carry):
            hr, hi = carry
            sl = pl.ds(pl.multiple_of(k * R, R), R)
            nr = ar * hr - ai * hi + bur[:, sl, :]
            ni = ar * hi + ai * hr + bui[:, sl, :]
            hr_ref[:, sl, :] = nr
            hi_ref[:, sl, :] = ni
            return nr, ni

        hr, hi = lax.fori_loop(0, nsteps, step, (cr[...], ci[...]), unroll=4)
        cr[...] = hr
        ci[...] = hi
        parts = []
        for c in range(ncl):
            hrc = _slab_get(hr_ref, c, Q, ncl).astype(MXU_DTYPE)
            hic = _slab_get(hi_ref, c, Q, ncl).astype(MXU_DTYPE)
            parts.append(jnp.dot(hrc, wc_ref[c, :cs, :], preferred_element_type=F32)
                         + jnp.dot(hic, wc_ref[c, cs:, :], preferred_element_type=F32))
        y = d_ref[...] * x + (parts[0] if ncl == 1 else jnp.concatenate(parts, axis=-1))
        y_ref[...] = y
        yg_ref[...] = _gelu(y).astype(yg_ref.dtype)

    return _pallas(
        body, name=name, grid=(T // Q,),
        in_specs=[_rows(Q, D), _whole(wb.shape), _whole(wc.shape), _whole((ns, R, LANES)), _whole((ns, R, LANES)),
                  _whole((1, D))],
        out_specs=[_rows(Q, D), _rows(Q, D), _slabs(ns, QR), _slabs(ns, QR)],
        out_shape=[jax.ShapeDtypeStruct((T, D), F32), jax.ShapeDtypeStruct((T, D), MXU_DTYPE),
                   jax.ShapeDtypeStruct((ns, T * ncl, LANES), F32), jax.ShapeDtypeStruct((ns, T * ncl, LANES), F32)],
        scratch_shapes=[pltpu.VMEM((ns, QR, LANES), F32), pltpu.VMEM((ns, QR, LANES), F32),
                        pltpu.VMEM((ns, R, LANES), F32), pltpu.VMEM((ns, R, LANES), F32)],
        compiler_params=_params(("arbitrary",)),
    )(xi, wb, wc, a_r, a_i, d_row)


def _s5_bwd(dy, xi, h_r, h_i, wb, wc, a_r, a_i, d_row, B, *, name):
    T, D = dy.shape
    ncl = wb.shape[0]
    cs = wb.shape[2] // 2
    ns = cs // LANES
    R = B * ncl
    Q = S5_CHUNK
    nsteps = Q // B
    nchunk = T // Q
    QR = Q * ncl

    def rev(i):
        return nchunk - 1 - i

    def body(dy_ref, x_ref, hr_ref, hi_ref, pr_ref, pi_ref, wb_ref, wc_ref, ar_ref, ai_ref, d_ref,
             du_ref, gr_ref, gi_ref, dar_ref, dai_ref, dd_ref, dhr, dhi, cr, ci):
        i = pl.program_id(0)

        @pl.when(i == 0)
        def _():
            cr[...] = jnp.zeros_like(cr)
            ci[...] = jnp.zeros_like(ci)
            dar_ref[...] = jnp.zeros_like(dar_ref)
            dai_ref[...] = jnp.zeros_like(dai_ref)
            dd_ref[...] = jnp.zeros_like(dd_ref)

        dyv = dy_ref[...]
        dyb = dyv.astype(MXU_DTYPE)
        for c in range(ncl):
            dh = lax.dot_general(dyb[:, c * CLUSTER_W:(c + 1) * CLUSTER_W], wc_ref[c],
                                 (((1,), (1,)), ((), ())), preferred_element_type=F32)
            _slab_put(dhr, c, Q, ncl, dh[:, :cs])
            _slab_put(dhi, c, Q, ncl, dh[:, cs:])
        ar = ar_ref[...]
        ai = ai_ref[...]

        def step(j, carry):
            gr, gi = carry
            k = nsteps - 1 - j
            sl = pl.ds(pl.multiple_of(k * R, R), R)
            ngr = dhr[:, sl, :] + ar * gr + ai * gi
            ngi = dhi[:, sl, :] - ai * gr + ar * gi
            gr_ref[:, sl, :] = ngr
            gi_ref[:, sl, :] = ngi
            return ngr, ngi

        gr, gi = lax.fori_loop(0, nsteps, step, (cr[...], ci[...]), unroll=4)
        cr[...] = gr
        ci[...] = gi
        keep = jnp.where(i == nchunk - 1, 0.0, 1.0)
        hpr = jnp.concatenate([pr_ref[:, 8 - R:8, :] * keep, hr_ref[:, 0:QR - R, :]], axis=1)
        hpi = jnp.concatenate([pi_ref[:, 8 - R:8, :] * keep, hi_ref[:, 0:QR - R, :]], axis=1)
        gra, gia = gr_ref[...], gi_ref[...]
        steps = lambda t: jnp.sum(t.reshape(ns, nsteps, R, LANES), axis=1)
        dar_ref[...] += steps(gra * hpr + gia * hpi)
        dai_ref[...] += steps(gia * hpr - gra * hpi)
        parts = []
        for c in range(ncl):
            grc = _slab_get(gr_ref, c, Q, ncl).astype(MXU_DTYPE)
            gic = _slab_get(gi_ref, c, Q, ncl).astype(MXU_DTYPE)
            parts.append(lax.dot_general(grc, wb_ref[c, :, :cs], (((1,), (1,)), ((), ())), preferred_element_type=F32)
                         + lax.dot_general(gic, wb_ref[c, :, cs:], (((1,), (1,)), ((), ())), preferred_element_type=F32))
        du_ref[...] = d_ref[...] * dyv + (parts[0] if ncl == 1 else jnp.concatenate(parts, axis=-1))
        dd_ref[...] += jnp.sum(dyv * x_ref[...], axis=0, keepdims=True)

    tok = pl.BlockSpec((Q, D), lambda i: (rev(i), 0))
    st = pl.BlockSpec((ns, QR, LANES), lambda i: (0, rev(i), 0))
    before = pl.BlockSpec((ns, 8, LANES), lambda i: (0, jnp.maximum(rev(i) * (QR // 8) - 1, 0), 0))
    acc = _whole((ns, R, LANES))
    return _pallas(
        body, name=name, grid=(nchunk,),
        in_specs=[tok, tok, st, st, before, before, _whole(wb.shape), _whole(wc.shape), acc, acc, _whole((1, D))],
        out_specs=[tok, st, st, acc, acc, _whole((1, D))],
        out_shape=[jax.ShapeDtypeStruct((T, D), F32),
                   jax.ShapeDtypeStruct((ns, T * ncl, LANES), F32), jax.ShapeDtypeStruct((ns, T * ncl, LANES), F32),
                   jax.ShapeDtypeStruct((ns, R, LANES), F32), jax.ShapeDtypeStruct((ns, R, LANES), F32),
                   jax.ShapeDtypeStruct((1, D), F32)],
        scratch_shapes=[pltpu.VMEM((ns, QR, LANES), F32)] * 2 + [pltpu.VMEM((ns, R, LANES), F32)] * 2,
        compiler_params=_params(("arbitrary",)),
    )(dy, xi, h_r, h_i, h_r, h_i, wb, wc, a_r, a_i, d_row)


def _cluster_tn(tok, st, ncl, *, tok_left, name):
    T = tok.shape[0]
    ns = st.shape[0]
    cs = ns * LANES
    tt = _pick(T, (512, 256, 128))
    nt = T // tt
    oshape = (ncl, CLUSTER_W, cs) if tok_left else (ncl, cs, CLUSTER_W)

    def body(tok_ref, st_ref, o_ref, acc):
        t = pl.program_id(0)

        @pl.when(t == 0)
        def _():
            acc[...] = jnp.zeros_like(acc)

        tk = tok_ref[...].astype(MXU_DTYPE)
        for c in range(ncl):
            tc = tk[:, c * CLUSTER_W:(c + 1) * CLUSTER_W]
            sc = _slab_get(st_ref, c, tt, ncl).astype(MXU_DTYPE)
            lhs, rhs = (tc, sc) if tok_left else (sc, tc)
            acc[c] += lax.dot_general(lhs, rhs, (((0,), (0,)), ((), ())), preferred_element_type=F32)

        @pl.when(t == nt - 1)
        def _():
            o_ref[...] = acc[...]

    return _pallas(
        body, name=name, grid=(nt,),
        in_specs=[_rows(tt, tok.shape[1]), _slabs(ns, tt * ncl)],
        out_specs=_whole(oshape),
        out_shape=jax.ShapeDtypeStruct(oshape, F32),
        scratch_shapes=[pltpu.VMEM(oshape, F32)],
        compiler_params=_params(("arbitrary",)),
    )(tok, st)


def _s5_discretize(lam_re, lam_im, log_dt, b_re, b_im):
    dt = jnp.exp(log_dt)[:, None]
    mag = jnp.exp(lam_re * dt)
    ab_r, ab_i = mag * jnp.cos(lam_im * dt), mag * jnp.sin(lam_im * dt)
    den = lam_re * lam_re + lam_im * lam_im
    nr = ab_r - 1.0
    co_r = (nr * lam_re + ab_i * lam_im) / den
    co_i = (ab_i * lam_re - nr * lam_im) / den
    bb_r = co_r[..., None] * b_re - co_i[..., None] * b_im
    bb_i = co_r[..., None] * b_im + co_i[..., None] * b_re
    return ab_r, ab_i, bb_r, bb_i


def _blockdiag(m):
    G, r, k = m.shape
    ncl = G // GROUPS_PER_CLUSTER
    m4 = m.reshape(ncl, GROUPS_PER_CLUSTER, r, k)
    eye = jnp.eye(GROUPS_PER_CLUSTER, dtype=m.dtype)
    return jnp.einsum('cgrk,gh->cgrhk', m4, eye).reshape(ncl, GROUPS_PER_CLUSTER * r, GROUPS_PER_CLUSTER * k)


def _unblockdiag(m, r, k):
    ncl = m.shape[0]
    m5 = m.reshape(ncl, GROUPS_PER_CLUSTER, r, GROUPS_PER_CLUSTER, k)
    eye = jnp.eye(GROUPS_PER_CLUSTER, dtype=m.dtype)
    return jnp.einsum('cgrhk,gh->cgrk', m5, eye).reshape(ncl * GROUPS_PER_CLUSTER, r, k)


def _t5_bucket(dist):
    exact = REL_BUCKETS // 2
    d = np.maximum(dist, 1).astype(np.float32)
    large = exact + (np.log(d / exact) / math.log(REL_MAX_DIST / exact) * (REL_BUCKETS - exact)).astype(np.int64)
    large = np.minimum(large, REL_BUCKETS - 1)
    return np.where(dist < exact, dist, large).astype(np.int32)


def _band_tables(dil):
    steps = np.arange(BAND)[:, None] + BAND - np.arange(2 * BAND)[None, :]
    bucket = _t5_bucket(np.maximum(steps, 0) * dil)
    in_band = (steps >= 0) & (steps <= BAND)
    return bucket, in_band


def _attn_bias(rel_bias, hpg):
    out = []
    for g, dil in enumerate(DILATIONS):
        bucket, in_band = _band_tables(dil)
        cols = rel_bias[:, g * hpg:(g + 1) * hpg].astype(F32)
        onehot = jnp.asarray((bucket.reshape(-1, 1) == np.arange(REL_BUCKETS)[None, :]).astype(np.float32))
        bias = jnp.dot(onehot, cols, precision=lax.Precision.HIGHEST).T.reshape(hpg, BAND, 2 * BAND)
        out.append(jnp.where(jnp.asarray(in_band)[None], bias, NEG_BIG))
    return jnp.concatenate(out, axis=0)


def _attn_blocks(dil, L):
    M = L // dil
    return M, M // BAND


def _row_sel(r, M, dil):
    return pl.ds(r, M) if dil == 1 else pl.ds(r, M, stride=dil)


def _attn_fwd(q, kv, bias, L, hpg, *, name):
    T = q.shape[0]
    nb_ = T // L
    HP = hpg // 2
    W3 = 3 * hpg * HEAD_DIM
    mmax = L

    def group_body(dil, q_ref, k_ref, v_ref, b_ref, o_ref, l_ref, os, ls):
        M, NB = _attn_blocks(dil, L)
        for r in range(dil):
            rows = _row_sel(r, M, dil)
            first = lax.broadcasted_iota(jnp.int32, (1, 2 * HEAD_DIM), 1) < HEAD_DIM
            qf = q_ref[rows, :] * 0.125
            qm = [jnp.where(first, qf, 0.0).astype(MXU_DTYPE), jnp.where(first, 0.0, qf).astype(MXU_DTYPE)]
            kr = k_ref[rows, :].astype(MXU_DTYPE)
            va = jnp.concatenate([v_ref[rows, :].astype(MXU_DTYPE), jnp.ones((M, 2 * HEAD_DIM), MXU_DTYPE)], axis=-1)
            for n in range(NB):
                qs = slice(n * BAND, (n + 1) * BAND)
                ks = slice(0, BAND) if n == 0 else slice((n - 1) * BAND, (n + 1) * BAND)
                o_h, l_h = [], []
                for hh in range(2):
                    bb = b_ref[hh, :, BAND:] if n == 0 else b_ref[hh]
                    s = lax.dot_general(qm[hh][qs, :], kr[ks, :], (((1,), (1,)), ((), ())),
                                        preferred_element_type=F32) + bb
                    m = jnp.max(s, axis=-1, keepdims=True)
                    p = jnp.exp(s - m)
                    pv = jnp.dot(p.astype(MXU_DTYPE), va[ks, :], preferred_element_type=F32)
                    l = pv[:, 2 * HEAD_DIM:]
                    o_h.append(pv[:, :2 * HEAD_DIM] / l)
                    l_h.append(m + jnp.log(l))
                os[qs, :] = jnp.where(first, o_h[0], o_h[1])
                ls[qs, :] = jnp.where(first, l_h[0], l_h[1])
            o_ref[rows, :] = os[0:M, :]
            l_ref[rows, :] = ls[0:M, :]

    def body(q_ref, k_ref, v_ref, b_ref, o_ref, l_ref, os, ls):
        g = pl.program_id(0)
        for gi, dil in enumerate(DILATIONS):
            pl.when(g == gi)(functools.partial(group_body, dil, q_ref, k_ref, v_ref, b_ref, o_ref, l_ref, os, ls))

    blk = (L, 2 * HEAD_DIM)
    return _pallas(
        body, name=name, grid=(3, nb_, HP),
        in_specs=[pl.BlockSpec(blk, lambda g, b, h: (b, g * HP + h)),
                  pl.BlockSpec(blk, lambda g, b, h: (b, g * HP + h)),
                  pl.BlockSpec(blk, lambda g, b, h: (b, 3 * HP + g * HP + h)),
                  pl.BlockSpec((2, BAND, 2 * BAND), lambda g, b, h: (g * HP + h, 0, 0))],
        out_specs=[pl.BlockSpec(blk, lambda g, b, h: (b, g * HP + h)),
                   pl.BlockSpec(blk, lambda g, b, h: (b, g * HP + h))],
        out_shape=[jax.ShapeDtypeStruct((T, W3), F32), jax.ShapeDtypeStruct((T, W3), F32)],
        scratch_shapes=[pltpu.VMEM((mmax, 2 * HEAD_DIM), F32), pltpu.VMEM((mmax, 2 * HEAD_DIM), F32)],
        compiler_params=_params(("arbitrary", "arbitrary", "arbitrary")),
    )(q, kv, kv, bias)


def _attn_merge(o3, l3, hw, *, name):
    T = o3.shape[0]
    tm = _pick(T, (256, 128))

    def body(o0, o1, o2, l0, l1, l2, o_ref, ob_ref, lse_ref):
        a0, a1, a2 = l0[...], l1[...], l2[...]
        m = jnp.maximum(jnp.maximum(a0, a1), a2)
        e0, e1, e2 = jnp.exp(a0 - m), jnp.exp(a1 - m), jnp.exp(a2 - m)
        z = e0 + e1 + e2
        o = (e0 * o0[...] + e1 * o1[...] + e2 * o2[...]) / z
        o_ref[...] = o
        ob_ref[...] = o.astype(ob_ref.dtype)
        lse_ref[...] = m + jnp.log(z)

    def col(g):
        return pl.BlockSpec((tm, hw), lambda i: (i, g))

    return _pallas(
        body, name=name, grid=(T // tm,),
        in_specs=[col(0), col(1), col(2), col(0), col(1), col(2)],
        out_specs=[_rows(tm, hw)] * 3,
        out_shape=[jax.ShapeDtypeStruct((T, hw), F32), jax.ShapeDtypeStruct((T, hw), MXU_DTYPE),
                   jax.ShapeDtypeStruct((T, hw), F32)],
        compiler_params=_params(("parallel",)),
    )(o3, o3, o3, l3, l3, l3)


def _attn_bwd(q, kv, do, o, lse, bias, L, hpg, *, name):
    T = q.shape[0]
    nb_ = T // L
    HP = hpg // 2
    W3 = 3 * hpg * HEAD_DIM
    mmax = L

    def group_body(dil, q_ref, k_ref, v_ref, do_ref, o_ref, l_ref, b_ref, dq_ref, dk_ref, dv_ref, ds_ref,
                   dqs, dks, dvs):
        M, NB = _attn_blocks(dil, L)
        for r in range(dil):
            rows = _row_sel(r, M, dil)
            first = lax.broadcasted_iota(jnp.int32, (1, 2 * HEAD_DIM), 1) < HEAD_DIM
            qf = q_ref[rows, :] * 0.125
            qm = [jnp.where(first, qf, 0.0).astype(MXU_DTYPE), jnp.where(first, 0.0, qf).astype(MXU_DTYPE)]
            kr = k_ref[rows, :].astype(MXU_DTYPE)
            vr = v_ref[rows, :].astype(MXU_DTYPE)
            dof = do_ref[rows, :]
            dom = [jnp.where(first, dof, 0.0).astype(MXU_DTYPE), jnp.where(first, 0.0, dof).astype(MXU_DTYPE)]
            dod = dof * o_ref[rows, :]
            delta = [jnp.sum(jnp.where(first, dod, 0.0), axis=-1, keepdims=True),
                     jnp.sum(jnp.where(first, 0.0, dod), axis=-1, keepdims=True)]
            lr = l_ref[rows, :]
            lse = [lr[:, 0:1], lr[:, HEAD_DIM:HEAD_DIM + 1]]
            dks[0:M, :] = jnp.zeros((M, 2 * HEAD_DIM), F32)
            dvs[0:M, :] = jnp.zeros((M, 2 * HEAD_DIM), F32)
            for n in range(NB):
                qs = slice(n * BAND, (n + 1) * BAND)
                ks = slice(0, BAND) if n == 0 else slice((n - 1) * BAND, (n + 1) * BAND)
                dq_h = []
                dkc = dvc = None
                for hh in range(2):
                    bb = b_ref[hh, :, BAND:] if n == 0 else b_ref[hh]
                    qb, dob = qm[hh][qs, :], dom[hh][qs, :]
                    s = lax.dot_general(qb, kr[ks, :], (((1,), (1,)), ((), ())), preferred_element_type=F32) + bb
                    p = jnp.exp(s - lse[hh][qs, :])
                    dp = lax.dot_general(dob, vr[ks, :], (((1,), (1,)), ((), ())), preferred_element_type=F32)
                    ds = p * (dp - delta[hh][qs, :])
                    if n == 0:
                        ds_ref[hh, :, BAND:] += ds
                    else:
                        ds_ref[hh] += ds
                    dsm = ds.astype(MXU_DTYPE)
                    dq_h.append(jnp.dot(dsm, kr[ks, :], preferred_element_type=F32))
                    dk1 = lax.dot_general(dsm, qb, (((0,), (0,)), ((), ())), preferred_element_type=F32)
                    dv1 = lax.dot_general(p.astype(MXU_DTYPE), dob, (((0,), (0,)), ((), ())), preferred_element_type=F32)
                    dkc = dk1 if dkc is None else dkc + dk1
                    dvc = dv1 if dvc is None else dvc + dv1
                dqs[qs, :] = jnp.where(first, dq_h[0], dq_h[1]) * 0.125
                dks[ks, :] += dkc
                dvs[ks, :] += dvc
            dq_ref[rows, :] = dqs[0:M, :]
            dk_ref[rows, :] = dks[0:M, :]
            dv_ref[rows, :] = dvs[0:M, :]

    def body(q_ref, k_ref, v_ref, do_ref, o_ref, l_ref, b_ref, dq_ref, dk_ref, dv_ref, ds_ref, dqs, dks, dvs):
        g = pl.program_id(0)

        @pl.when(pl.program_id(2) == 0)
        def _():
            ds_ref[...] = jnp.zeros_like(ds_ref)

        for gi, dil in enumerate(DILATIONS):
            pl.when(g == gi)(functools.partial(group_body, dil, q_ref, k_ref, v_ref, do_ref, o_ref, l_ref, b_ref,
                                               dq_ref, dk_ref, dv_ref, ds_ref, dqs, dks, dvs))

    blk = (L, 2 * HEAD_DIM)
    gcol = lambda g, h, b: (b, g * HP + h)
    hcol = lambda g, h, b: (b, h)
    return _pallas(
        body, name=name, grid=(3, HP, nb_),
        in_specs=[pl.BlockSpec(blk, gcol), pl.BlockSpec(blk, gcol),
                  pl.BlockSpec(blk, lambda g, h, b: (b, 3 * HP + g * HP + h)),
                  pl.BlockSpec(blk, hcol), pl.BlockSpec(blk, hcol), pl.BlockSpec(blk, hcol),
                  pl.BlockSpec((2, BAND, 2 * BAND), lambda g, h, b: (g * HP + h, 0, 0))],
        out_specs=[pl.BlockSpec(blk, gcol), pl.BlockSpec(blk, gcol), pl.BlockSpec(blk, gcol),
                   pl.BlockSpec((2, BAND, 2 * BAND), lambda g, h, b: (g * HP + h, 0, 0))],
        out_shape=[jax.ShapeDtypeStruct((T, W3), F32), jax.ShapeDtypeStruct((T, W3), F32),
                   jax.ShapeDtypeStruct((T, W3), F32), jax.ShapeDtypeStruct((3 * hpg, BAND, 2 * BAND), F32)],
        scratch_shapes=[pltpu.VMEM((mmax, 2 * HEAD_DIM), F32)] * 3,
        compiler_params=_params(("arbitrary", "arbitrary", "arbitrary")),
    )(q, kv, kv, do, o, lse, bias)


def _bias_grad(ds_sum, hpg, *, name):
    nh = ds_sum.shape[0]
    idx = np.stack([np.where(_band_tables(dil)[1], _band_tables(dil)[0], -1) for dil in DILATIONS]).astype(np.int32)

    def body(ds_ref, idx_ref, o_ref):
        d = ds_ref[...]
        ix = idx_ref[...]
        lane = lax.broadcasted_iota(jnp.int32, (8, 128), 1)
        row = jnp.zeros((8, 128), F32)
        for b in range(REL_BUCKETS):
            row = row + jnp.where(lane == b, jnp.sum(jnp.where(ix == b, d, 0.0)), 0.0)
        o_ref[...] = row

    out = _pallas(
        body, name=name, grid=(nh,),
        in_specs=[pl.BlockSpec((None, BAND, 2 * BAND), lambda h: (h, 0, 0)),
                  pl.BlockSpec((None, BAND, 2 * BAND), lambda h: (h // hpg, 0, 0))],
        out_specs=pl.BlockSpec((None, 8, 128), lambda h: (h, 0, 0)),
        out_shape=jax.ShapeDtypeStruct((nh, 8, 128), F32),
        compiler_params=_params(("parallel",)),
    )(ds_sum, jnp.asarray(idx))
    return out[:, 0, :REL_BUCKETS].T


def _adamw(w, g, m, v, *, name):
    Rw, C = w.shape
    tm = _pick(Rw, (512, 352, 256, 128, 64, 32, 16, 8))

    def body(w_ref, g_ref, m_ref, v_ref, d_ref, nm_ref, nv_ref):
        gg = g_ref[...]
        nm = ADAM_B1 * m_ref[...] + (1.0 - ADAM_B1) * gg
        nv = ADAM_B2 * v_ref[...] + (1.0 - ADAM_B2) * (gg * gg)
        m_hat = nm / (1.0 - ADAM_B1 ** ADAM_STEP)
        v_hat = nv / (1.0 - ADAM_B2 ** ADAM_STEP)
        d_ref[...] = -ADAM_LR * (m_hat / (jnp.sqrt(v_hat) + ADAM_EPS) + ADAM_WD * w_ref[...])
        nm_ref[...] = nm
        nv_ref[...] = nv

    return _pallas(
        body, name=name, grid=(Rw // tm,), in_specs=[_rows(tm, C)] * 4, out_specs=[_rows(tm, C)] * 3,
        out_shape=[jax.ShapeDtypeStruct((Rw, C), F32)] * 3, compiler_params=_params(("parallel",)),
    )(w, g, m, v)


ROW_TILE_ELEMS = 256 * 1024


def _tile_rows(r, c):
    best = 8
    for t in range(8, r + 1, 8):
        if r % t == 0 and t * c <= ROW_TILE_ELEMS:
            best = t
    return best


def _adamw_halves(w, m, v, mine, other, cidx, *, layer=0, prev=None, name):
    NL, _, r, c = w.shape
    tm = _tile_rows(r, c)

    def body(c_ref, w_ref, m_ref, v_ref, a_ref, b_ref, *rest):
        g_ref, d_ref, nm_ref, nv_ref = rest[-4:]
        gg = jnp.where(pl.program_id(0) == c_ref[0], a_ref[...], b_ref[...])
        nm = ADAM_B1 * m_ref[...] + (1.0 - ADAM_B1) * gg
        nv = ADAM_B2 * v_ref[...] + (1.0 - ADAM_B2) * (gg * gg)
        m_hat = nm / (1.0 - ADAM_B1 ** ADAM_STEP)
        v_hat = nv / (1.0 - ADAM_B2 ** ADAM_STEP)
        g_ref[...] = gg
        d_ref[...] = -ADAM_LR * (m_hat / (jnp.sqrt(v_hat) + ADAM_EPS) + ADAM_WD * w_ref[...])
        nm_ref[...] = nm
        nv_ref[...] = nv

    half = pl.BlockSpec((None, None, tm, c), lambda h, i, cr: (layer, h, i, 0))
    one = pl.BlockSpec((None, tm, c), lambda h, i, cr: (0, i, 0))
    in_specs = [half, half, half, one, one]
    args = [cidx, w, m, v, mine, other]
    aliases = {}
    if prev is not None:
        in_specs += [_ANY] * 4
        args += list(prev)
        aliases = {6 + k: k for k in range(4)}
    spec = pltpu.PrefetchScalarGridSpec(num_scalar_prefetch=1, grid=(2, r // tm), in_specs=in_specs, out_specs=[half] * 4)
    return _pallas(
        body, name=name, grid_spec=spec, out_shape=[jax.ShapeDtypeStruct((NL, 2, r, c), F32)] * 4,
        input_output_aliases=aliases, compiler_params=_params(("parallel", "parallel")),
    )(*args)


def _pair_sum(g, theirs, cidx, *, cast, name):
    _, _, r, c = g.shape
    tm = _tile_rows(r, c)

    def body(c_ref, g_ref, t_ref, *outs):
        s = g_ref[...] + t_ref[...]
        outs[0][...] = s
        if cast:
            outs[1][...] = s.astype(BF16)

    blk = (None, None, tm, c)
    first = pl.BlockSpec(blk, lambda p, i, cr: (p, 0, i, 0))
    shapes = [jax.ShapeDtypeStruct((4, 1, r, c), F32)] + ([jax.ShapeDtypeStruct((4, 1, r, c), BF16)] if cast else [])
    spec = pltpu.PrefetchScalarGridSpec(
        num_scalar_prefetch=1, grid=(4, r // tm),
        in_specs=[pl.BlockSpec(blk, lambda p, i, cr: (p, cr[0], i, 0)), first], out_specs=[first] * len(shapes))
    return _pallas(body, name=name, grid_spec=spec, out_shape=shapes,
                   compiler_params=_params(("parallel", "parallel")))(cidx, g, theirs)


def _chip_sum(hf, got, chip_idx, *, name):
    _, _, r, c = hf.shape
    tm = _tile_rows(r, c)

    def body(p_ref, h_ref, r_ref, o_ref):
        s = h_ref[...]
        for k in range(3):
            s = s + r_ref[k].astype(F32)
        o_ref[...] = s

    spec = pltpu.PrefetchScalarGridSpec(
        num_scalar_prefetch=1, grid=(r // tm,),
        in_specs=[pl.BlockSpec((None, None, tm, c), lambda i, pr: (pr[0], 0, i, 0)),
                  pl.BlockSpec((3, None, tm, c), lambda i, pr: (0, 0, i, 0))],
        out_specs=pl.BlockSpec((None, tm, c), lambda i, pr: (0, i, 0)))
    return _pallas(body, name=name, grid_spec=spec, out_shape=jax.ShapeDtypeStruct((1, r, c), F32),
                   compiler_params=_params(("parallel",)))(chip_idx, hf, got)


def _place():
    x, y, c = lax.axis_index("x"), lax.axis_index("y"), lax.axis_index("c")
    chips = [(1 - x, y), (x, 1 - y), (1 - x, 1 - y)]
    return x, y, c, chips


_ANY = pl.BlockSpec(memory_space=pl.ANY)


def _comm_call(body, ins, out_shapes, n_remote, *, name, aliases=None):
    sems = [pltpu.SemaphoreType.DMA((n,)) for n in n_remote]
    return _pallas(
        body, name=name, in_specs=[_ANY] * len(ins), out_specs=[_ANY] * len(out_shapes), out_shape=out_shapes,
        scratch_shapes=sems, input_output_aliases=aliases or {},
        compiler_params=pltpu.CompilerParams(has_side_effects=True),
    )(*ins)


_HBM_SPEC = pl.BlockSpec(memory_space=pltpu.HBM)
_SEM_SPEC = pl.BlockSpec(memory_space=pltpu.SEMAPHORE)
_DATAFLOW = pltpu.SideEffectType.DATAFLOW_SIDE_EFFECTING


def _in_hbm(a):
    return pltpu.with_memory_space_constraint(a, pltpu.HBM)


def _gather_start(groups, *, name):
    flat = [s for g in groups for s in g]
    n, ng = len(flat), len(groups)

    def body(*refs):
        ins, lands = refs[:n], refs[n:2 * n]
        sems = refs[2 * n:2 * n + 2 * ng]
        token = refs[-1]
        x, y, c, chips = _place()
        me = 2 * x + y
        a = 0
        for gi, g in enumerate(groups):
            for j in range(len(g)):
                for k, (tx, ty) in enumerate(chips):
                    _rcopy(ins[a].at[c], lands[a].at[me, c], sems[2 * gi].at[3 * j + k], sems[2 * gi + 1].at[3 * j + k],
                           (tx, ty, c)).start()
                a += 1
        token[...] = jnp.zeros_like(token)

    land_shapes = [(4,) + s.shape for s in flat]
    out_shape = ([pltpu.SemaphoreType.DMA((3 * len(g),)) for g in groups for _ in range(2)]
                 + [pltpu.HBM(s.shape, s.dtype) for s in flat]
                 + [pltpu.HBM(ls, s.dtype) for ls, s in zip(land_shapes, flat)]
                 + [jax.ShapeDtypeStruct((8, 128), F32)])
    outs = _pallas(
        body, name=name, in_specs=[_HBM_SPEC] * (2 * n),
        out_specs=[_SEM_SPEC] * (2 * ng) + [_HBM_SPEC] * (2 * n) + [pl.BlockSpec(memory_space=pltpu.VMEM)],
        out_shape=out_shape, input_output_aliases={i: 2 * ng + i for i in range(2 * n)},
        compiler_params=pltpu.CompilerParams(has_side_effects=_DATAFLOW),
    )(*[_in_hbm(s) for s in flat], *[_in_hbm(lax.empty(ls, s.dtype)) for ls, s in zip(land_shapes, flat)])
    sems, thru, lands, token = outs[:2 * ng], outs[2 * ng:2 * ng + n], outs[2 * ng + n:2 * ng + 2 * n], outs[-1]
    res, a = [], 0
    for gi, g in enumerate(groups):
        res.append((sems[2 * gi], sems[2 * gi + 1], thru[a:a + len(g)], lands[a:a + len(g)]))
        a += len(g)
    return res, token


def _gather_wait(ssem, rsem, shards, lands, after, *, name):
    m = len(shards)

    def body(*refs):
        ins, lnd = refs[:m], refs[m:2 * m]
        ss, rs = refs[2 * m], refs[2 * m + 1]
        x, y, c, chips = _place()
        for j in range(m):
            for k, (tx, ty) in enumerate(chips):
                cp = _rcopy(ins[j].at[c], lnd[j].at[2 * tx + ty, c], ss.at[3 * j + k], rs.at[3 * j + k], (tx, ty, c))
                cp.wait_send()
                cp.wait_recv()

    outs = _pallas(
        body, name=name, in_specs=[_HBM_SPEC] * (2 * m) + [_SEM_SPEC, _SEM_SPEC, _ANY],
        out_specs=[_HBM_SPEC] * (2 * m),
        out_shape=[pltpu.HBM(s.shape, s.dtype) for s in shards] + [pltpu.HBM(l.shape, l.dtype) for l in lands],
        input_output_aliases={i: i for i in range(2 * m)},
        compiler_params=pltpu.CompilerParams(has_side_effects=_DATAFLOW),
    )(*shards, *lands, ssem, rsem, after)
    return outs[m:]


def _gather_forward(lands, *, name):
    n = len(lands)

    def body(*refs):
        outs = refs[n:2 * n]
        ssem, rsem = refs[2 * n:]
        x, y, c, chips = _place()
        sib = (x, y, 1 - c)
        cps = []
        for a in range(n):
            for k, (tx, ty) in enumerate(chips):
                pk = 2 * tx + ty
                cp = _rcopy(outs[a].at[pk, c], outs[a].at[pk, c], ssem.at[3 * a + k], rsem.at[3 * a + k], sib)
                cp.start()
                cps.append(cp)
        for a in range(n):
            for k, (tx, ty) in enumerate(chips):
                pk = 2 * tx + ty
                _rcopy(outs[a].at[pk, c], outs[a].at[pk, 1 - c], ssem.at[3 * a + k], rsem.at[3 * a + k], sib).wait_recv()
        for cp in cps:
            cp.wait_send()

    shapes = [jax.ShapeDtypeStruct(l.shape, l.dtype) for l in lands]
    return _comm_call(body, lands, shapes, [3 * n, 3 * n], name=name, aliases={i: i for i in range(n)})


class _Lazy:
    def __init__(self, group_of, make):
        self._group_of, self._make, self._done, self._anchor = group_of, make, {}, None

    def anchor(self, value):
        self._anchor = value

    def __getitem__(self, key):
        g = self._group_of[key]
        if g not in self._done:
            self._done[g] = self._make(g, self._anchor)
        return self._done[g][key]


def _anchor(mapping, value):
    if isinstance(mapping, _Lazy):
        mapping.anchor(value)


def _rcopy(src, dst, ssem, rsem, dev):
    return pltpu.make_async_remote_copy(src_ref=src, dst_ref=dst, send_sem=ssem, recv_sem=rsem,
                                        device_id=dev, device_id_type=MESH)


def _all_gather(shards, *, name):
    n = len(shards)

    def body(*refs):
        ins, outs = refs[:n], refs[n:2 * n]
        s_ici, r_ici, s_d2d, r_d2d = refs[2 * n:]
        x, y, c, chips = _place()
        me = 2 * x + y
        sib = (x, y, 1 - c)
        sends = []
        for a in range(n):
            for k, (tx, ty) in enumerate(chips):
                cp = _rcopy(ins[a].at[c], outs[a].at[me, c], s_ici.at[3 * a + k], r_ici.at[3 * a + k], (tx, ty, c))
                cp.start()
                sends.append(cp)
        for a in range(n):
            for k, (tx, ty) in enumerate(chips):
                pk = 2 * tx + ty
                _rcopy(ins[a].at[c], outs[a].at[pk, c], s_ici.at[3 * a + k], r_ici.at[3 * a + k], (tx, ty, c)).wait_recv()
                fw = _rcopy(outs[a].at[pk, c], outs[a].at[pk, c], s_d2d.at[3 * a + k], r_d2d.at[3 * a + k], sib)
                fw.start()
                sends.append(fw)
        for a in range(n):
            for k, (tx, ty) in enumerate(chips):
                pk = 2 * tx + ty
                _rcopy(ins[a].at[c], outs[a].at[pk, 1 - c], s_d2d.at[3 * a + k], r_d2d.at[3 * a + k], sib).wait_recv()
        for cp in sends:
            cp.wait_send()

    shapes = [jax.ShapeDtypeStruct((4,) + s.shape, s.dtype) for s in shards]
    return _comm_call(body, shards, shapes, [3 * n] * 4, name=name)


def _gather(shards, chip, *, name):
    outs = _all_gather(shards, name=name)
    return [lax.dynamic_update_slice(o, s[None], (chip, 0, 0, 0)) for o, s in zip(outs, shards)]


def _pair_send(gs, *, name):
    n = len(gs)

    def body(*refs):
        ins, theirs = refs[:n], refs[n:2 * n]
        ssem, rsem = refs[2 * n:]
        x, y, c, _ = _place()
        sib = (x, y, 1 - c)
        cps = []
        for a in range(n):
            cp = _rcopy(ins[a].at[:, pl.ds(1 - c, 1)], theirs[a], ssem.at[a], rsem.at[a], sib)
            cp.start()
            cps.append(cp)
        for cp in cps:
            cp.wait_send()
            cp.wait_recv()

    shapes = [jax.ShapeDtypeStruct((4, 1) + g.shape[2:], g.dtype) for g in gs]
    return _comm_call(body, gs, shapes, [n, n], name=name)


def _chip_exchange(hx, *, name):
    n = len(hx)

    def body(*refs):
        hxr, got = refs[:n], refs[n:2 * n]
        ssem, rsem = refs[2 * n:]
        x, y, c, chips = _place()
        cps = []
        for a in range(n):
            for k, (tx, ty) in enumerate(chips):
                cp = _rcopy(hxr[a].at[2 * tx + ty], got[a].at[k], ssem.at[3 * a + k], rsem.at[3 * a + k], (tx, ty, c))
                cp.start()
                cps.append(cp)
        for cp in cps:
            cp.wait_send()
            cp.wait_recv()

    shapes = [jax.ShapeDtypeStruct((3,) + h.shape[1:], h.dtype) for h in hx]
    return _comm_call(body, hx, shapes, [3 * n, 3 * n], name=name)


def _pair_swap(fs, *, name):
    n = len(fs)

    def body(*refs):
        ins, outs = refs[:n], refs[n:2 * n]
        ssem, rsem = refs[2 * n:]
        x, y, c, _ = _place()
        cps = []
        for a in range(n):
            cp = _rcopy(ins[a], outs[a], ssem.at[a], rsem.at[a], (x, y, 1 - c))
            cp.start()
            cps.append(cp)
        for cp in cps:
            cp.wait_send()
            cp.wait_recv()

    shapes = [jax.ShapeDtypeStruct(f.shape, f.dtype) for f in fs]
    return _comm_call(body, fs, shapes, [n, n], name=name)


def _chip_exchange_start(hx, *, name):
    n = len(hx)

    def body(*refs):
        ins, gots = refs[:n], refs[n:2 * n]
        ssem, rsem = refs[2 * n], refs[2 * n + 1]
        token = refs[-1]
        x, y, c, chips = _place()
        for a in range(n):
            for k, (tx, ty) in enumerate(chips):
                _rcopy(ins[a].at[2 * tx + ty], gots[a].at[k], ssem.at[3 * a + k], rsem.at[3 * a + k], (tx, ty, c)).start()
        token[...] = jnp.zeros_like(token)

    got_shapes = [(3,) + h.shape[1:] for h in hx]
    outs = _pallas(
        body, name=name, in_specs=[_HBM_SPEC] * (2 * n),
        out_specs=[_SEM_SPEC] * 2 + [_HBM_SPEC] * (2 * n) + [pl.BlockSpec(memory_space=pltpu.VMEM)],
        out_shape=([pltpu.SemaphoreType.DMA((3 * n,))] * 2 + [pltpu.HBM(h.shape, h.dtype) for h in hx]
                   + [pltpu.HBM(gs, h.dtype) for gs, h in zip(got_shapes, hx)] + [jax.ShapeDtypeStruct((8, 128), F32)]),
        input_output_aliases={i: 2 + i for i in range(2 * n)},
        compiler_params=pltpu.CompilerParams(has_side_effects=_DATAFLOW),
    )(*[_in_hbm(h) for h in hx], *[_in_hbm(lax.empty(gs, h.dtype)) for gs, h in zip(got_shapes, hx)])
    return (outs[0], outs[1], outs[2:2 + n], outs[2 + n:2 + 2 * n]), outs[-1]


def _chip_exchange_wait(started, after, *, name):
    ssem, rsem, hx, gots = started
    n = len(hx)

    def body(*refs):
        ins, gts = refs[:n], refs[n:2 * n]
        ss, rs = refs[2 * n], refs[2 * n + 1]
        x, y, c, chips = _place()
        for a in range(n):
            for k, (tx, ty) in enumerate(chips):
                cp = _rcopy(ins[a].at[2 * tx + ty], gts[a].at[k], ss.at[3 * a + k], rs.at[3 * a + k], (tx, ty, c))
                cp.wait_send()
                cp.wait_recv()

    outs = _pallas(
        body, name=name, in_specs=[_HBM_SPEC] * (2 * n) + [_SEM_SPEC, _SEM_SPEC, _ANY],
        out_specs=[_HBM_SPEC] * (2 * n),
        out_shape=[pltpu.HBM(h.shape, h.dtype) for h in hx] + [pltpu.HBM(g.shape, g.dtype) for g in gots],
        input_output_aliases={i: i for i in range(2 * n)},
        compiler_params=pltpu.CompilerParams(has_side_effects=_DATAFLOW),
    )(*hx, *gots, ssem, rsem, after)
    return outs[n:]


def _pair_send_start(gs, *, name):
    n = len(gs)

    def body(*refs):
        ins, lands = refs[:n], refs[n:2 * n]
        ssem, rsem = refs[2 * n], refs[2 * n + 1]
        token = refs[-1]
        x, y, c, _ = _place()
        for a in range(n):
            _rcopy(ins[a].at[:, pl.ds(1 - c, 1)], lands[a], ssem.at[a], rsem.at[a], (x, y, 1 - c)).start()
        token[...] = jnp.zeros_like(token)

    land_shapes = [(4, 1) + g.shape[2:] for g in gs]
    outs = _pallas(
        body, name=name, in_specs=[_HBM_SPEC] * (2 * n),
        out_specs=[_SEM_SPEC] * 2 + [_HBM_SPEC] * (2 * n) + [pl.BlockSpec(memory_space=pltpu.VMEM)],
        out_shape=([pltpu.SemaphoreType.DMA((n,))] * 2 + [pltpu.HBM(g.shape, g.dtype) for g in gs]
                   + [pltpu.HBM(ls, g.dtype) for ls, g in zip(land_shapes, gs)] + [jax.ShapeDtypeStruct((8, 128), F32)]),
        input_output_aliases={i: 2 + i for i in range(2 * n)},
        compiler_params=pltpu.CompilerParams(has_side_effects=_DATAFLOW),
    )(*[_in_hbm(g) for g in gs], *[_in_hbm(lax.empty(ls, g.dtype)) for ls, g in zip(land_shapes, gs)])
    return (outs[0], outs[1], outs[2:2 + n], outs[2 + n:2 + 2 * n]), outs[-1]


def _pair_send_wait(started, after, *, name):
    ssem, rsem, gs, lands = started
    n = len(gs)

    def body(*refs):
        ins, lnd = refs[:n], refs[n:2 * n]
        ss, rs = refs[2 * n], refs[2 * n + 1]
        x, y, c, _ = _place()
        for a in range(n):
            cp = _rcopy(ins[a].at[:, pl.ds(1 - c, 1)], lnd[a], ss.at[a], rs.at[a], (x, y, 1 - c))
            cp.wait_send()
            cp.wait_recv()

    outs = _pallas(
        body, name=name, in_specs=[_HBM_SPEC] * (2 * n) + [_SEM_SPEC, _SEM_SPEC, _ANY],
        out_specs=[_HBM_SPEC] * (2 * n),
        out_shape=[pltpu.HBM(g.shape, g.dtype) for g in gs] + [pltpu.HBM(l.shape, l.dtype) for l in lands],
        input_output_aliases={i: i for i in range(2 * n)},
        compiler_params=pltpu.CompilerParams(has_side_effects=_DATAFLOW),
    )(*gs, *lands, ssem, rsem, after)
    return list(outs[:n]), list(outs[n:])


def _pair_sums(grads, exch_bf16, cidx, tag, theirs=None):
    if theirs is None:
        theirs = _pair_send(grads, name=f"rs_pair_send_{tag}")
    hf, hx = [], []
    for a in range(len(grads)):
        res = _pair_sum(grads[a], theirs[a], cidx, cast=exch_bf16[a], name=f"rs_pair_sum_{tag}{a}")
        hf.append(res[0])
        hx.append(res[1] if exch_bf16[a] else res[0])
    return hf, hx


def _chip_sums(hf, got, chip_idx, tag):
    return [_chip_sum(hf[a], got[a], chip_idx, name=f"rs_chip_sum_{tag}{a}") for a in range(len(hf))]


def _interleave(a, B, L):
    return a.reshape(B, L, -1).transpose(1, 0, 2).reshape(B * L, -1)


def _deinterleave(a, B, L):
    return a.reshape(L, B, -1).transpose(1, 0, 2).reshape(B * L, -1)


def _local_step(x, tgt, W, S, on_grads=None):
    B, L, D = x.shape
    T = B * L
    G = D // SSM_GROUP
    Pst = SSM_STATE
    hpg = D // HEAD_DIM
    HW = hpg * HEAD_DIM
    ncl = G // GROUPS_PER_CLUSTER
    x2 = x.reshape(T, D)
    tgt2 = tgt.reshape(T, D)

    disc = lambda *p: _s5_discretize(*p)
    (ab_r, ab_i, bb_r, bb_i), disc_vjp = jax.vjp(disc, S["lam_re"], S["lam_im"], S["log_dt"], S["b_re"], S["b_im"])
    wb = jnp.concatenate([_blockdiag(jnp.transpose(bb_r, (0, 2, 1))), _blockdiag(jnp.transpose(bb_i, (0, 2, 1)))],
                         axis=-1).astype(MXU_DTYPE)
    wc = jnp.concatenate([_blockdiag(jnp.transpose(S["c_re"], (0, 2, 1))), _blockdiag(-jnp.transpose(S["c_im"], (0, 2, 1)))],
                         axis=1).astype(MXU_DTYPE)
    cs = GROUPS_PER_CLUSTER * Pst
    slab = lambda ab: jnp.tile(jnp.transpose(ab.reshape(ncl, cs // LANES, LANES), (1, 0, 2)), (1, B, 1))
    a_r, a_i = slab(ab_r), slab(ab_i)
    d_row = S["d"].reshape(1, D)

    xi = _interleave(x2, B, L)
    y, yg, h_r, h_i = _s5_fwd(xi, wb, wc, a_r, a_i, d_row, B, name="s5_fwd")
    _anchor(W, yg)
    z = _mm_nn(yg, W["w_glu"], bias=S["b_glu"].reshape(1, D), name="glu_z")
    gate = _glu_gate(y, z, name="glu_gate")
    mix = _deinterleave(_mm_nn(gate, W["w_out"], name="s5_out"), B, L)
    h1, h1b, xh1, rs1 = _ln_fwd(x2, mix, S["ln_gain"][0, 0][None], S["ln_bias"][0, 0][None], name="ln_fwd_0a")

    def ffn_fwd(hb, l):
        hc = _mm_nn(hb, W["w_up"], l=l, out_dtype=MXU_DTYPE, name=f"ffn_up_{l}")
        a = _conv_glu_fwd(hc, S["conv_w"][l], S["conv_b"][l][None], L, name=f"ffn_conv_{l}")
        f = _mm_nn(a, W["w_down"], l=l, name=f"ffn_down_{l}")
        return hc, a, f

    _anchor(W, h1b)
    hc0, a0, f0 = ffn_fwd(h1b, 0)
    h2, h2b, xh2, rs2 = _ln_fwd(h1, f0, S["ln_gain"][0, 1][None], S["ln_bias"][0, 1][None], name="ln_fwd_0b")

    _anchor(W, h2b)
    kv = _mm_nn(h2b, W["w_kv"], name="attn_kv")
    q = _mm_nn(h2b, W["w_q"], name="attn_q")
    bias = _attn_bias(S["rel_bias"], hpg)
    o3, l3 = _attn_fwd(q, kv, bias, L, hpg, name="attn_fwd")
    o, ob, lse = _attn_merge(o3, l3, HW, name="attn_merge")
    att = _mm_nn(ob, W["w_ao"], name="attn_out")
    h3, h3b, xh3, rs3 = _ln_fwd(h2, att, S["ln_gain"][1, 0][None], S["ln_bias"][1, 0][None], name="ln_fwd_1a")
    hc1, a1, f1 = ffn_fwd(h3b, 1)
    h4, _, xh4, rs4 = _ln_fwd(h3, f1, S["ln_gain"][1, 1][None], S["ln_bias"][1, 1][None], name="ln_fwd_1b")

    dh4, lrow = _loss_grad(h4, tgt2, name="loss")
    loss = lrow[0, 0]

    GW, GS = {}, {}

    def ffn_bwd(dzb, hb, hc, a, l):
        da = _mm_nt(dzb, W["w_down"], l=l, out_dtype=MXU_DTYPE, name=f"ffn_down_bwd_x_{l}")
        GW[f"w_down{l}"] = _tn(a, dzb, ptotal=1, np_cols=D, name=f"ffn_down_bwd_w_{l}")
        dc, dcw, dcb = _conv_glu_bwd(hc, da, S["conv_w"][l], S["conv_b"][l][None], L, name=f"ffn_conv_bwd_{l}")
        dhc = _conv_bwd_input(dc, S["conv_w"][l], L, name=f"ffn_conv_bwd_x_{l}")
        dh = _mm_nt(dhc, W["w_up"], l=l, name=f"ffn_up_bwd_x_{l}")
        GW[f"w_up{l}"] = _tn(hb, dhc, ptotal=W["w_up"].shape[0], np_cols=W["w_up"].shape[3], name=f"ffn_up_bwd_w_{l}")
        return dh, dcw, dcb

    dz4, dz4b, dg4, db4 = _ln_bwd([dh4], [1.0], xh4, rs4, S["ln_gain"][1, 1][None], name="ln_bwd_1b")
    dh3f, dcw1, dcb1 = ffn_bwd(dz4b, h3b, hc1, a1, 1)
    dz3, dz3b, dg3, db3 = _ln_bwd([dz4, dh3f], [DN_ALPHA, 1.0], xh3, rs3, S["ln_gain"][1, 0][None], name="ln_bwd_1a")
    do = _mm_nt(dz3b, W["w_ao"], name="attn_out_bwd_x")
    GW["w_ao"] = _tn(ob, dz3b, ptotal=1, np_cols=D, name="attn_out_bwd_w")
    dq, dk, dv, ds_sum = _attn_bwd(q, kv, do, o, lse, bias, L, hpg, name="attn_bwd")
    GS["rel_bias"] = _bias_grad(ds_sum, hpg, name="attn_bias_grad")
    GW["w_q"] = _tn(h2b, dq, ptotal=W["w_q"].shape[0], np_cols=W["w_q"].shape[3], name="attn_q_bwd_w")
    pkv, npkv = W["w_kv"].shape[0], W["w_kv"].shape[3]
    gkv = _tn(h2b, dk, ptotal=pkv, np_cols=npkv, p0=0, name="attn_k_bwd_w")
    GW["w_kv"] = _tn(h2b, dv, ptotal=pkv, np_cols=npkv, p0=pkv // 2, prev=gkv, name="attn_v_bwd_w")
    dh2q = _mm_nt(dq, W["w_q"], name="attn_q_bwd_x")
    dh2k = _mm_nt(dk, W["w_kv"], p0=0, pn=pkv // 2, name="attn_k_bwd_x")
    dh2v = _mm_nt(dv, W["w_kv"], p0=pkv // 2, pn=pkv // 2, name="attn_v_bwd_x")

    gain_0b = S["ln_gain"][0, 1][None]
    if on_grads is not None:
        gain_0b = gain_0b + on_grads(0, GW)[0, 0]

    dz2, dz2b, dg2, db2 = _ln_bwd([dz3, dh2q, dh2k, dh2v], [DN_ALPHA, 1.0, 1.0, 1.0], xh2, rs2, gain_0b,
                                  name="ln_bwd_0b")
    dh1f, dcw0, dcb0 = ffn_bwd(dz2b, h1b, hc0, a0, 0)
    gain_0a = S["ln_gain"][0, 0][None]
    if on_grads is not None:
        gain_0a = gain_0a + on_grads(1, GW)[0, 0]
    dz1, dz1b, dg1, db1 = _ln_bwd([dz2, dh1f], [DN_ALPHA, 1.0], xh1, rs1, gain_0a, name="ln_bwd_0a")
    dmix_i = _interleave(dz1b, B, L)
    dgate = _mm_nt(dmix_i, W["w_out"], name="s5_out_bwd_x")
    GW["w_out"] = _tn(gate, dmix_i, ptotal=1, np_cols=D, name="s5_out_bwd_w")
    dzg, dyg1, dbglu = _glu_bwd(y, z, dgate, name="glu_bwd")
    dyg2 = _mm_nt(dzg, W["w_glu"], name="glu_z_bwd_x")
    GW["w_glu"] = _tn(yg, dzg, ptotal=1, np_cols=D, name="glu_z_bwd_w")
    dy = _gelu_bwd(y, dyg1, dyg2, name="gelu_bwd")
    if on_grads is not None:
        d_row = d_row + on_grads(2, GW)[0, 0]
    du_i, g_r, g_i, dar, dai, dd = _s5_bwd(dy, xi, h_r, h_i, wb, wc, a_r, a_i, d_row, B, name="s5_bwd")
    dwb_r = _cluster_tn(xi, g_r, ncl, tok_left=True, name="s5_b_grad_re")
    dwb_i = _cluster_tn(xi, g_i, ncl, tok_left=True, name="s5_b_grad_im")
    dwc_r = _cluster_tn(dy, h_r, ncl, tok_left=False, name="s5_c_grad_re")
    dwc_i = _cluster_tn(dy, h_i, ncl, tok_left=False, name="s5_c_grad_im")
    grad_x = _axpy(dz1, _deinterleave(du_i, B, L), DN_ALPHA, name="grad_x")

    dbb_r = jnp.transpose(_unblockdiag(dwb_r, SSM_GROUP, Pst), (0, 2, 1))
    dbb_i = jnp.transpose(_unblockdiag(dwb_i, SSM_GROUP, Pst), (0, 2, 1))
    unslab = lambda da: jnp.transpose(da.reshape(cs // LANES, B, ncl, LANES).sum(1), (1, 0, 2)).reshape(G, Pst)
    dab_r, dab_i = unslab(dar), unslab(dai)
    GS["lam_re"], GS["lam_im"], GS["log_dt"], GS["b_re"], GS["b_im"] = disc_vjp((dab_r, dab_i, dbb_r, dbb_i))
    GS["c_re"] = jnp.transpose(_unblockdiag(dwc_r, Pst, SSM_GROUP), (0, 2, 1))
    GS["c_im"] = -jnp.transpose(_unblockdiag(dwc_i, Pst, SSM_GROUP), (0, 2, 1))
    GS["d"] = dd.reshape(G, SSM_GROUP)
    GS["b_glu"] = dbglu.reshape(D)
    GS["conv_w"] = jnp.stack([dcw0, dcw1])
    GS["conv_b"] = jnp.stack([dcb0[0], dcb1[0]])
    GS["ln_gain"] = jnp.stack([jnp.stack([dg1[0], dg2[0]]), jnp.stack([dg3[0], dg4[0]])])
    GS["ln_bias"] = jnp.stack([jnp.stack([db1[0], db2[0]]), jnp.stack([db3[0], db4[0]])])
    return loss, grad_x.reshape(B, L, D), GW, GS


SMALL_REPLICATED = ("lam_re", "lam_im", "log_dt", "b_re", "b_im", "c_re", "c_im", "d", "rel_bias", "conv_b")
SMALL_SHARDED = ("b_glu", "conv_w", "ln_gain", "ln_bias")
SMALL_ORDER = SMALL_REPLICATED + SMALL_SHARDED


def _pack(arrs, lanes, row_mult):
    flat = jnp.concatenate([a.reshape(-1).astype(F32) for a in arrs])
    rows = -(-flat.shape[0] // lanes)
    rows = -(-rows // row_mult) * row_mult
    return jnp.pad(flat, (0, rows * lanes - flat.shape[0])).reshape(rows, lanes)


def _unpack(packed, shapes):
    flat = packed.reshape(-1)
    out, off = [], 0
    for s in shapes:
        n = int(np.prod(s))
        out.append(flat[off:off + n].reshape(s))
        off += n
    return out


def kernel(x, s5_lam_re, s5_lam_im, s5_log_dt, s5_b_re, s5_b_im, s5_c_re, s5_c_im, s5_d, s5_w_glu, s5_b_glu, s5_w_out, attn_w_kv, attn_w_q, attn_w_out, rel_bias, ffn_w_up, ffn_conv_w, ffn_conv_b, ffn_w_down, ln_gain, ln_bias, loss_target, m_s5_lam_re, m_s5_lam_im, m_s5_log_dt, m_s5_b_re, m_s5_b_im, m_s5_c_re, m_s5_c_im, m_s5_d, m_s5_w_glu, m_s5_b_glu, m_s5_w_out, m_attn_w_kv, m_attn_w_q, m_attn_w_out, m_rel_bias, m_ffn_w_up, m_ffn_conv_w, m_ffn_conv_b, m_ffn_w_down, m_ln_gain, m_ln_bias, v_s5_lam_re, v_s5_lam_im, v_s5_log_dt, v_s5_b_re, v_s5_b_im, v_s5_c_re, v_s5_c_im, v_s5_d, v_s5_w_glu, v_s5_b_glu, v_s5_w_out, v_attn_w_kv, v_attn_w_q, v_attn_w_out, v_rel_bias, v_ffn_w_up, v_ffn_conv_w, v_ffn_conv_b, v_ffn_w_down, v_ln_gain, v_ln_bias):
    names = ["s5_lam_re", "s5_lam_im", "s5_log_dt", "s5_b_re", "s5_b_im", "s5_c_re", "s5_c_im", "s5_d", "s5_w_glu",
             "s5_b_glu", "s5_w_out", "attn_w_kv", "attn_w_q", "attn_w_out", "rel_bias", "ffn_w_up", "ffn_conv_w",
             "ffn_conv_b", "ffn_w_down", "ln_gain", "ln_bias"]
    loc = locals()
    w_in = {n: loc[n] for n in names}
    m_in = {n: loc["m_" + n] for n in names}
    v_in = {n: loc["v_" + n] for n in names}
    chip = 2 * lax.axis_index("x") + lax.axis_index("y")
    core = lax.axis_index("c")
    chip_idx = jnp.reshape(chip, (1,)).astype(jnp.int32)
    cidx = jnp.reshape(core, (1,)).astype(jnp.int32)

    big = [("w_glu", "s5_w_glu", "rows"), ("w_out", "s5_w_out", "rows"), ("w_ao", "attn_w_out", "rows"),
           ("w_kv", "attn_w_kv", "cols"), ("w_q", "attn_w_q", "cols"),
           ("w_up", "ffn_w_up", "layer_cols"), ("w_down", "ffn_w_down", "layer_rows")]

    def halves(t, kind):
        if kind.startswith("layer"):
            return t
        r, c = t.shape[-2:]
        return t.reshape(2, r // 2, c)

    def to_weight(g, kind):
        _, _, r, c = g.shape
        if kind == "rows":
            return g.reshape(1, 1, 8 * r, c)
        if kind == "cols":
            return g.reshape(4, 1, 2 * r, c)
        if kind == "layer_cols":
            return g
        return jnp.transpose(g, (1, 0, 2, 3)).reshape(1, 2, 4 * r, c)

    small_sh = {"b_glu": s5_b_glu[0], "conv_w": ffn_conv_w, "ln_gain": ln_gain, "ln_bias": ln_bias}
    sh_shapes = [small_sh[k].shape for k in SMALL_SHARDED]
    sh_pack = _pack([small_sh[k] for k in SMALL_SHARDED], 128, 16)

    shards = [halves(w_in[src].astype(MXU_DTYPE), kind) for _, src, kind in big]
    shards.append(sh_pack.reshape(2, sh_pack.shape[0] // 2, 128))
    shard_of = {key: s for (key, _, _), s in zip(big, shards)}
    shard_of["small"] = shards[-1]
    kind_of = {key: kind for key, _, kind in big}

    group_keys = [["w_glu", "w_out", "small"], ["w_up", "w_down"], ["w_kv", "w_q", "w_ao"]]
    started, token = _gather_start([[shard_of[k] for k in g] for g in group_keys], name="weights_gather_start")

    def finish_group(gi, after):
        ssem, rsem, thru, lands = started[gi]
        lands = _gather_wait(ssem, rsem, thru, lands, after, name=f"weights_gather_wait_{gi}")
        lands = _gather_forward(lands, name=f"weights_gather_forward_{gi}")
        out = {}
        for key, land in zip(group_keys[gi], lands):
            full = lax.dynamic_update_slice(land, shard_of[key][None], (chip, 0, 0, 0))
            if key == "small":
                parts = [_unpack(full[p], sh_shapes) for p in range(4)]
                for i, k in enumerate(SMALL_SHARDED):
                    out[k] = jnp.concatenate([parts[p][i] for p in range(4)], axis=-1)
            else:
                out[key] = to_weight(full, kind_of[key])
        return out

    replicated = dict(lam_re=s5_lam_re[0], lam_im=s5_lam_im[0], log_dt=s5_log_dt[0], b_re=s5_b_re[0], b_im=s5_b_im[0],
                      c_re=s5_c_re[0], c_im=s5_c_im[0], rel_bias=rel_bias, conv_b=ffn_conv_b,
                      d=s5_d[0] + token[0, 0])
    group_of = {k: gi for gi, g in enumerate(group_keys) for k in g if k != "small"}
    group_of.update({k: 0 for k in SMALL_SHARDED})
    group_of.update({k: "replicated" for k in replicated})
    params = _Lazy(group_of, lambda g, after: replicated if g == "replicated" else finish_group(g, after))

    red = [("w_up1", "ffn_w_up", 1), ("w_down1", "ffn_w_down", 1), ("w_ao", "attn_w_out", 0), ("w_kv", "attn_w_kv", 0),
           ("w_q", "attn_w_q", 0), ("w_down0", "ffn_w_down", 0), ("w_up0", "ffn_w_up", 0), ("w_out", "s5_w_out", 0),
           ("w_glu", "s5_w_glu", 0)]
    stages = [red[:5], red[5:7], red[7:]]

    def grad_halves(gw, key, src):
        r, c = w_in[src].shape[-2:]
        return gw[key].reshape(4, 2, r // 2, c)

    sent, early = {}, []

    def on_grads(stage, gw):
        tokens = []
        if stage > 0:
            tag = "abc"[stage - 1]
            ga, theirs = _pair_send_wait(sent.pop(stage - 1), gw[stages[stage][-1][0]], name=f"rs_pair_send_wait_{tag}")
            hf, hx = _pair_sums(ga, [True] * len(ga), cidx, tag, theirs)
            started, tok = _chip_exchange_start(hx, name=f"rs_chip_exchange_start_{tag}")
            early.append((hf, started, tag))
            tokens.append(tok)
        ga = [grad_halves(gw, key, src) for key, src, _ in stages[stage]]
        sent[stage], tok = _pair_send_start(ga, name=f"rs_pair_send_start_{'abc'[stage]}")
        return sum(tokens, tok)

    loss, grad_x, GW, GS = _local_step(x, loss_target, params, params, on_grads)

    gs_shapes = [GS[k].shape for k in SMALL_ORDER] + [(1,)]
    gs_pack = _pack([GS[k] for k in SMALL_ORDER] + [loss.reshape(1)], 128, 64)
    rs = gs_pack.shape[0] // 8
    gs_halves = [gs_pack.reshape(4, 2, rs, 128)]
    gl, theirs_l = _pair_send_wait(sent.pop(2), grad_x, name="rs_pair_send_wait_c")
    theirs_l += _pair_send(gs_halves, name="rs_pair_send_small")
    gl += gs_halves
    hf_l, hx_l = _pair_sums(gl, [True] * (len(gl) - 1) + [False], cidx, "c", theirs_l)
    mine = []
    for hf, started, tag in early:
        got = _chip_exchange_wait(started, grad_x, name=f"rs_chip_exchange_wait_{tag}")
        mine += _chip_sums(hf, got, chip_idx, tag)
    mine += _chip_sums(hf_l, _chip_exchange(hx_l, name="rs_chip_exchange_c"), chip_idx, "c")
    other = _pair_swap(mine, name="rs_pair_swap")
    small_halves = jnp.where(core == 0, jnp.concatenate([mine[-1], other[-1]]), jnp.concatenate([other[-1], mine[-1]]))
    small_all = _gather([small_halves], chip, name="small_grads_all_gather")[0]
    totals = _unpack(small_all, gs_shapes)
    gsmall = dict(zip(SMALL_ORDER, totals))
    loss = totals[-1][0]

    big_res = {}
    for (key, src, layer), gm, go in zip(red, mine[:-1], other[:-1]):
        nl = w_in[src].shape[0] if src in ("ffn_w_up", "ffn_w_down") else 1
        r, c = w_in[src].shape[-2:]
        view = lambda t: t.reshape(nl, 2, r // 2, c)
        res4 = _adamw_halves(view(w_in[src]), view(m_in[src]), view(v_in[src]), gm, go, cidx, layer=layer,
                             prev=big_res.get(src), name=f"adamw_{key}")
        big_res[src] = res4
    big_res = {src: tuple(t.reshape(w_in[src].shape) for t in res4) for src, res4 in big_res.items()}

    def big_out(i):
        return {src: big_res[src][i] for _, src, _ in big}

    small_w = {"lam_re": s5_lam_re, "lam_im": s5_lam_im, "log_dt": s5_log_dt, "b_re": s5_b_re, "b_im": s5_b_im,
               "c_re": s5_c_re, "c_im": s5_c_im, "d": s5_d, "rel_bias": rel_bias, "conv_b": ffn_conv_b,
               "b_glu": s5_b_glu, "conv_w": ffn_conv_w, "ln_gain": ln_gain, "ln_bias": ln_bias}
    small_name = {"lam_re": "s5_lam_re", "lam_im": "s5_lam_im", "log_dt": "s5_log_dt", "b_re": "s5_b_re", "b_im": "s5_b_im",
                  "c_re": "s5_c_re", "c_im": "s5_c_im", "d": "s5_d", "rel_bias": "rel_bias", "conv_b": "ffn_conv_b",
                  "b_glu": "s5_b_glu", "conv_w": "ffn_conv_w", "ln_gain": "ln_gain", "ln_bias": "ln_bias"}
    sg = {}
    for k in SMALL_ORDER:
        shp = small_w[k].shape
        g = gsmall[k]
        if k in SMALL_SHARDED:
            width = shp[-1]
            g = lax.dynamic_slice_in_dim(g, chip * width, width, axis=g.ndim - 1)
        sg[k] = g.reshape(shp)
    sd, snm, snv = {}, {}, {}
    for k in SMALL_ORDER:
        shp = small_w[k].shape
        flat = lambda t: t.reshape(-1, shp[-1])
        r3 = _adamw(flat(small_w[k]), flat(sg[k]), flat(m_in[small_name[k]]), flat(v_in[small_name[k]]),
                    name=f"adamw_{k}")
        sd[k], snm[k], snv[k] = (t.reshape(shp) for t in r3)

    res = [{}, {}, {}, {}]
    for i in range(4):
        res[i].update(big_out(i))
    for k in SMALL_ORDER:
        res[0][small_name[k]] = sg[k]
        res[1][small_name[k]] = sd[k]
        res[2][small_name[k]] = snm[k]
        res[3][small_name[k]] = snv[k]
    outs = [loss, grad_x]
    for i in range(4):
        outs += [res[i][n] for n in names]
    return tuple(outs)
```

```python
import functools
import math

import numpy as np
import jax
import jax.numpy as jnp
from jax import lax
from jax.experimental import pallas as pl
from jax.experimental.pallas import tpu as pltpu

F32 = jnp.float32
BF16 = jnp.bfloat16
MXU_DTYPE = jnp.bfloat16
V7X_VMEM_LIMIT_BYTES = 52 << 20
MESH = pl.DeviceIdType.MESH

DEPTH = 2
SSM_GROUP = 16
SSM_STATE = 64
GROUPS_PER_CLUSTER = 16
CLUSTER_W = GROUPS_PER_CLUSTER * SSM_GROUP
HEAD_DIM = 64
DILATIONS = (1, 4, 16)
BAND = 128
NEG_BIG = -1e30
REL_BUCKETS = 32
REL_MAX_DIST = 2048
DN_ALPHA = (2.0 * DEPTH) ** 0.25
LN_EPS = 1e-5
ADAM_LR, ADAM_B1, ADAM_B2, ADAM_EPS, ADAM_WD, ADAM_STEP = 0.001, 0.9, 0.999, 1e-08, 0.01, 10
GELU_K = math.sqrt(2.0 / math.pi)
GELU_C = 0.044715


def _pallas(body, **kw):
    return pl.pallas_call(body, **kw)


def _params(sem=None):
    return pltpu.CompilerParams(dimension_semantics=sem, vmem_limit_bytes=V7X_VMEM_LIMIT_BYTES)


def _pick(n, cands):
    for c in cands:
        if n % c == 0:
            return c
    return n


def _sigmoid(z):
    return 1.0 / (1.0 + jnp.exp(-z))


def _gelu(y):
    return 0.5 * y * (1.0 + jnp.tanh(GELU_K * (y + GELU_C * y * y * y)))


def _gelu_grad(y):
    t = jnp.tanh(GELU_K * (y + GELU_C * y * y * y))
    return 0.5 * (1.0 + t) + 0.5 * y * (1.0 - t * t) * (GELU_K * (1.0 + 3.0 * GELU_C * y * y))


def _mm_nn(a, w, *, l=0, bias=None, out_dtype=F32, name):
    T, K = a.shape
    P, _, _, Np = w.shape
    tm = _pick(T, (1024, 512, 256, 128))
    tn = _pick(Np, (1408, 1024, 768, 512, 384, 256, 128))
    nj = Np // tn

    def body(*refs):
        if bias is None:
            a_ref, w_ref, o_ref = refs
        else:
            a_ref, w_ref, b_ref, o_ref = refs
        acc = jnp.dot(a_ref[...].astype(MXU_DTYPE), w_ref[...].astype(MXU_DTYPE), preferred_element_type=F32)
        if bias is not None:
            acc = acc + b_ref[...]
        o_ref[...] = acc.astype(o_ref.dtype)

    in_specs = [pl.BlockSpec((tm, K), lambda p, j, i: (i, 0)),
                pl.BlockSpec((None, None, K, tn), lambda p, j, i: (p, l, 0, j))]
    args = [a, w]
    if bias is not None:
        in_specs.append(pl.BlockSpec((1, tn), lambda p, j, i: (0, p * nj + j)))
        args.append(bias)
    return _pallas(
        body, name=name, grid=(P, nj, T // tm), in_specs=in_specs,
        out_specs=pl.BlockSpec((tm, tn), lambda p, j, i: (i, p * nj + j)),
        out_shape=jax.ShapeDtypeStruct((T, P * Np), out_dtype),
        compiler_params=_params(("parallel", "parallel", "parallel")),
    )(*args)


def _mm_nt(a, w, *, l=0, p0=0, pn=None, out_dtype=F32, name):
    T = a.shape[0]
    _, _, K, Np = w.shape
    pn = w.shape[0] if pn is None else pn
    tm = _pick(T, (1024, 512, 256, 128) if K <= 1024 else (512, 256, 128))
    tn = _pick(Np, (1536, 1408, 1024, 768, 512, 384, 256, 128))
    nj = Np // tn
    nred = pn * nj

    def body(a_ref, w_ref, o_ref, acc):
        r = pl.program_id(1)

        @pl.when(r == 0)
        def _():
            acc[...] = jnp.zeros_like(acc)

        acc[...] += lax.dot_general(a_ref[...].astype(MXU_DTYPE), w_ref[...].astype(MXU_DTYPE),
                                    (((1,), (1,)), ((), ())), preferred_element_type=F32)

        @pl.when(r == nred - 1)
        def _():
            o_ref[...] = acc[...].astype(o_ref.dtype)

    return _pallas(
        body, name=name, grid=(T // tm, nred),
        in_specs=[pl.BlockSpec((tm, tn), lambda i, r: (i, r)),
                  pl.BlockSpec((None, None, K, tn), lambda i, r: (p0 + r // nj, l, 0, r % nj))],
        out_specs=pl.BlockSpec((tm, K), lambda i, r: (i, 0)),
        out_shape=jax.ShapeDtypeStruct((T, K), out_dtype),
        scratch_shapes=[pltpu.VMEM((tm, K), F32)],
        compiler_params=_params(("parallel", "arbitrary")),
    )(a, w)


def _tn(a, b, *, ptotal, np_cols, nl=1, l=0, p0=0, prev=None, name):
    T, K = a.shape
    Np = np_cols
    pn = b.shape[1] // Np
    tt = _pick(T, (1024, 512, 256, 128))
    tk = _pick(K, (1408, 1024, 512, 256, 128))
    tn = _pick(Np, (1408, 768, 512, 256, 128))
    if tk * tn > 1408 * 1024:
        tn = _pick(Np, (512, 256, 128))
    nj = Np // tn
    nt = T // tt

    def body(*refs):
        a_ref, b_ref = refs[0], refs[1]
        o_ref, acc = refs[-2], refs[-1]
        t = pl.program_id(3)

        @pl.when(t == 0)
        def _():
            acc[...] = jnp.zeros_like(acc)

        acc[...] += lax.dot_general(a_ref[...].astype(MXU_DTYPE), b_ref[...].astype(MXU_DTYPE),
                                    (((0,), (0,)), ((), ())), preferred_element_type=F32)

        @pl.when(t == nt - 1)
        def _():
            o_ref[...] = acc[...]

    in_specs = [pl.BlockSpec((tt, tk), lambda kb, p, j, t: (t, kb)),
                pl.BlockSpec((tt, tn), lambda kb, p, j, t: (t, p * nj + j))]
    args = [a, b]
    aliases = {}
    if prev is not None:
        in_specs.append(pl.BlockSpec(memory_space=pl.ANY))
        args.append(prev)
        aliases = {2: 0}
    return _pallas(
        body, name=name, grid=(K // tk, pn, nj, nt), in_specs=in_specs,
        out_specs=pl.BlockSpec((None, None, tk, tn), lambda kb, p, j, t: (p0 + p, l, kb, j)),
        out_shape=jax.ShapeDtypeStruct((ptotal, nl, K, Np), F32),
        scratch_shapes=[pltpu.VMEM((tk, tn), F32)],
        input_output_aliases=aliases,
        compiler_params=_params(("parallel", "parallel", "parallel", "arbitrary")),
    )(*args)


def _rows(tm, f):
    return pl.BlockSpec((tm, f), lambda i: (i, 0))


def _whole(shape):
    nd = len(shape)
    return pl.BlockSpec(shape, lambda i: (0,) * nd)


def _ln_fwd(xres, f, gain, bias, *, name):
    T, D = xres.shape
    tm = _pick(T, (256, 128))

    def body(x_ref, f_ref, g_ref, b_ref, y_ref, yb_ref, xh_ref, rs_ref):
        z = DN_ALPHA * x_ref[...] + f_ref[...]
        mu = jnp.mean(z, axis=-1, keepdims=True)
        zc = z - mu
        var = jnp.mean(zc * zc, axis=-1, keepdims=True)
        rstd = lax.rsqrt(var + LN_EPS)
        xh = zc * rstd
        y = xh * g_ref[...] + b_ref[...]
        y_ref[...] = y
        yb_ref[...] = y.astype(yb_ref.dtype)
        xh_ref[...] = xh
        rs_ref[...] = rstd

    return _pallas(
        body, name=name, grid=(T // tm,),
        in_specs=[_rows(tm, D), _rows(tm, D), _whole((1, D)), _whole((1, D))],
        out_specs=[_rows(tm, D), _rows(tm, D), _rows(tm, D), _rows(tm, 1)],
        out_shape=[jax.ShapeDtypeStruct((T, D), F32), jax.ShapeDtypeStruct((T, D), MXU_DTYPE),
                   jax.ShapeDtypeStruct((T, D), F32), jax.ShapeDtypeStruct((T, 1), F32)],
        compiler_params=_params(("parallel",)),
    )(xres, f, gain, bias)


def _ln_bwd(addends, coefs, xhat, rstd, gain, *, name):
    T, D = xhat.shape
    tm = _pick(T, (256, 128))
    n = len(addends)

    def body(*refs):
        adds = refs[:n]
        xh_ref, rs_ref, g_ref, dz_ref, dzb_ref, dg_ref, db_ref = refs[n:]
        dy = coefs[0] * adds[0][...]
        for c, r in zip(coefs[1:], adds[1:]):
            dy = dy + c * r[...]
        xh = xh_ref[...]
        dxh = dy * g_ref[...]
        m1 = jnp.mean(dxh, axis=-1, keepdims=True)
        m2 = jnp.mean(dxh * xh, axis=-1, keepdims=True)
        dz = rs_ref[...] * (dxh - m1 - xh * m2)
        dz_ref[...] = dz
        dzb_ref[...] = dz.astype(dzb_ref.dtype)

        @pl.when(pl.program_id(0) == 0)
        def _():
            dg_ref[...] = jnp.zeros_like(dg_ref)
            db_ref[...] = jnp.zeros_like(db_ref)

        dg_ref[...] += jnp.sum(dy * xh, axis=0, keepdims=True)
        db_ref[...] += jnp.sum(dy, axis=0, keepdims=True)

    return _pallas(
        body, name=name, grid=(T // tm,),
        in_specs=[_rows(tm, D)] * n + [_rows(tm, D), _rows(tm, 1), _whole((1, D))],
        out_specs=[_rows(tm, D), _rows(tm, D), _whole((1, D)), _whole((1, D))],
        out_shape=[jax.ShapeDtypeStruct((T, D), F32), jax.ShapeDtypeStruct((T, D), MXU_DTYPE),
                   jax.ShapeDtypeStruct((1, D), F32), jax.ShapeDtypeStruct((1, D), F32)],
        compiler_params=_params(("arbitrary",)),
    )(*addends, xhat, rstd, gain)


def _loss_grad(y, tgt, *, name):
    T, D = y.shape
    tm = _pick(T, (256, 128))

    def body(y_ref, t_ref, dy_ref, l_ref):
        e = y_ref[...] - t_ref[...]
        dy_ref[...] = e * (1.0 / D)

        @pl.when(pl.program_id(0) == 0)
        def _():
            l_ref[...] = jnp.zeros_like(l_ref)

        l_ref[...] += jnp.zeros_like(l_ref) + jnp.sum(e * e) * (0.5 / D)

    return _pallas(
        body, name=name, grid=(T // tm,),
        in_specs=[_rows(tm, D), _rows(tm, D)],
        out_specs=[_rows(tm, D), _whole((1, 128))],
        out_shape=[jax.ShapeDtypeStruct((T, D), F32), jax.ShapeDtypeStruct((1, 128), F32)],
        compiler_params=_params(("arbitrary",)),
    )(y, tgt)


def _axpy(a, b, ca, *, name):
    T, D = a.shape
    tm = _pick(T, (256, 128))

    def body(a_ref, b_ref, o_ref):
        o_ref[...] = ca * a_ref[...] + b_ref[...]

    return _pallas(
        body, name=name, grid=(T // tm,), in_specs=[_rows(tm, D), _rows(tm, D)], out_specs=_rows(tm, D),
        out_shape=jax.ShapeDtypeStruct((T, D), F32), compiler_params=_params(("parallel",)),
    )(a, b)


def _glu_gate(y, z, *, name):
    T, D = y.shape
    tm = _pick(T, (256, 128))

    def body(y_ref, z_ref, g_ref):
        g_ref[...] = (_gelu(y_ref[...]) * _sigmoid(z_ref[...])).astype(g_ref.dtype)

    return _pallas(
        body, name=name, grid=(T // tm,), in_specs=[_rows(tm, D), _rows(tm, D)], out_specs=_rows(tm, D),
        out_shape=jax.ShapeDtypeStruct((T, D), MXU_DTYPE), compiler_params=_params(("parallel",)),
    )(y, z)


def _glu_bwd(y, z, dg, *, name):
    T, D = y.shape
    tm = _pick(T, (256, 128))

    def body(y_ref, z_ref, dg_ref, dzb_ref, dyg_ref, db_ref):
        s = _sigmoid(z_ref[...])
        dg = dg_ref[...]
        dz = dg * _gelu(y_ref[...]) * s * (1.0 - s)
        dzb_ref[...] = dz.astype(dzb_ref.dtype)
        dyg_ref[...] = dg * s

        @pl.when(pl.program_id(0) == 0)
        def _():
            db_ref[...] = jnp.zeros_like(db_ref)

        db_ref[...] += jnp.sum(dz, axis=0, keepdims=True)

    return _pallas(
        body, name=name, grid=(T // tm,), in_specs=[_rows(tm, D)] * 3,
        out_specs=[_rows(tm, D), _rows(tm, D), _whole((1, D))],
        out_shape=[jax.ShapeDtypeStruct((T, D), MXU_DTYPE), jax.ShapeDtypeStruct((T, D), F32),
                   jax.ShapeDtypeStruct((1, D), F32)],
        compiler_params=_params(("arbitrary",)),
    )(y, z, dg)


def _gelu_bwd(y, d1, d2, *, name):
    T, D = y.shape
    tm = _pick(T, (256, 128))

    def body(y_ref, a_ref, b_ref, o_ref):
        o_ref[...] = (a_ref[...] + b_ref[...]) * _gelu_grad(y_ref[...])

    return _pallas(
        body, name=name, grid=(T // tm,), in_specs=[_rows(tm, D)] * 3, out_specs=_rows(tm, D),
        out_shape=jax.ShapeDtypeStruct((T, D), F32), compiler_params=_params(("parallel",)),
    )(y, d1, d2)


CONV_ROWS = 128
CONV_EDGE = 16


def _row_shifts(x, edge, drop_edge, tm, back):
    keep = jnp.where(drop_edge, 0.0, 1.0).astype(edge.dtype)
    ext = jnp.concatenate([edge * keep, x] if back else [x, edge * keep], axis=0)
    row = lax.broadcasted_iota(jnp.int32, (tm, tm + CONV_EDGE), 0)
    col = lax.broadcasted_iota(jnp.int32, (tm, tm + CONV_EDGE), 1)
    base = row + CONV_EDGE if back else row
    out = []
    for k in (1, 2):
        pick = (col == (base - k if back else base + k)).astype(x.dtype)
        out.append(jnp.dot(pick, ext, preferred_element_type=F32))
    return out


def _conv_specs(T, F2, tm):
    return [_rows(tm, F2),
            pl.BlockSpec((CONV_EDGE, F2), lambda i: (jnp.maximum(i * (tm // CONV_EDGE) - 1, 0), 0))]


def _conv_glu_fwd(hc, conv_w, conv_b, L, *, name):
    T, F2 = hc.shape
    F = F2 // 2
    tm = CONV_ROWS

    def body(x_ref, e_ref, w_ref, b_ref, a_ref):
        at_start = (pl.program_id(0) * tm) % L == 0
        x1, x2 = _row_shifts(x_ref[...], e_ref[...], at_start, tm, True)
        x = x_ref[...].astype(F32)
        c = b_ref[...] + w_ref[0:1, :] * x + w_ref[1:2, :] * x1 + w_ref[2:3, :] * x2
        val, gate = c[:, :F], c[:, F:]
        a_ref[...] = (gate * _sigmoid(gate) * val).astype(a_ref.dtype)

    return _pallas(
        body, name=name, grid=(T // tm,),
        in_specs=_conv_specs(T, F2, tm) + [_whole((3, F2)), _whole((1, F2))],
        out_specs=_rows(tm, F),
        out_shape=jax.ShapeDtypeStruct((T, F), MXU_DTYPE), compiler_params=_params(("parallel",)),
    )(hc, hc, conv_w, conv_b)


def _conv_glu_bwd(hc, da, conv_w, conv_b, L, *, name):
    T, F2 = hc.shape
    F = F2 // 2
    tm = CONV_ROWS

    def body(x_ref, e_ref, da_ref, w_ref, b_ref, dc_ref, dw_ref, db_ref):
        at_start = (pl.program_id(0) * tm) % L == 0
        x1, x2 = _row_shifts(x_ref[...], e_ref[...], at_start, tm, True)
        x = x_ref[...].astype(F32)
        c = b_ref[...] + w_ref[0:1, :] * x + w_ref[1:2, :] * x1 + w_ref[2:3, :] * x2
        val, gate = c[:, :F], c[:, F:]
        s = _sigmoid(gate)
        da = da_ref[...].astype(F32)
        dval = da * (gate * s)
        dgate = da * val * (s * (1.0 + gate * (1.0 - s)))
        dc = jnp.concatenate([dval, dgate], axis=-1)
        dc_ref[...] = dc.astype(dc_ref.dtype)

        @pl.when(pl.program_id(0) == 0)
        def _():
            dw_ref[...] = jnp.zeros_like(dw_ref)
            db_ref[...] = jnp.zeros_like(db_ref)

        dw_ref[0:1, :] += jnp.sum(dc * x, axis=0, keepdims=True)
        dw_ref[1:2, :] += jnp.sum(dc * x1, axis=0, keepdims=True)
        dw_ref[2:3, :] += jnp.sum(dc * x2, axis=0, keepdims=True)
        db_ref[...] += jnp.sum(dc, axis=0, keepdims=True)

    return _pallas(
        body, name=name, grid=(T // tm,),
        in_specs=_conv_specs(T, F2, tm) + [_rows(tm, F), _whole((3, F2)), _whole((1, F2))],
        out_specs=[_rows(tm, F2), _whole((3, F2)), _whole((1, F2))],
        out_shape=[jax.ShapeDtypeStruct((T, F2), MXU_DTYPE), jax.ShapeDtypeStruct((3, F2), F32),
                   jax.ShapeDtypeStruct((1, F2), F32)],
        compiler_params=_params(("arbitrary",)),
    )(hc, hc, da, conv_w, conv_b)


def _conv_bwd_input(dc, conv_w, L, *, name):
    T, F2 = dc.shape
    tm = CONV_ROWS
    edge = CONV_EDGE
    last_blk = T // edge - 1

    def body(x_ref, e_ref, w_ref, o_ref):
        at_end = ((pl.program_id(0) + 1) * tm) % L == 0
        x1, x2 = _row_shifts(x_ref[...], e_ref[...], at_end, tm, False)
        x = x_ref[...].astype(F32)
        o_ref[...] = (w_ref[0:1, :] * x + w_ref[1:2, :] * x1 + w_ref[2:3, :] * x2).astype(o_ref.dtype)

    return _pallas(
        body, name=name, grid=(T // tm,),
        in_specs=[_rows(tm, F2),
                  pl.BlockSpec((edge, F2), lambda i: (jnp.minimum((i + 1) * (tm // edge), last_blk), 0)),
                  _whole((3, F2))],
        out_specs=_rows(tm, F2),
        out_shape=jax.ShapeDtypeStruct((T, F2), MXU_DTYPE), compiler_params=_params(("parallel",)),
    )(dc, dc, conv_w)


S5_CHUNK = 128
LANES = 128


def _slab_rows(c, n, ncl):
    return pl.ds(c, n) if ncl == 1 else pl.ds(c, n, stride=ncl)


def _slab_put(ref, c, n, ncl, val):
    for s in range(val.shape[1] // LANES):
        ref[s, _slab_rows(c, n, ncl), :] = val[:, s * LANES:(s + 1) * LANES]


def _slab_get(ref, c, n, ncl):
    return jnp.concatenate([ref[s, _slab_rows(c, n, ncl), :] for s in range(ref.shape[0])], axis=-1)


def _slabs(n_slab, rows):
    return pl.BlockSpec((n_slab, rows, LANES), lambda i: (0, i, 0))


def _s5_fwd(xi, wb, wc, a_r, a_i, d_row, B, *, name):
    T, D = xi.shape
    ncl = wb.shape[0]
    cs = wb.shape[2] // 2
    ns = cs // LANES
    R = B * ncl
    Q = S5_CHUNK
    QR = Q * ncl
    nsteps = Q // B

    def body(x_ref, wb_ref, wc_ref, ar_ref, ai_ref, d_ref, y_ref, yg_ref, hr_ref, hi_ref, bur, bui, cr, ci):
        @pl.when(pl.program_id(0) == 0)
        def _():
            cr[...] = jnp.zeros_like(cr)
            ci[...] = jnp.zeros_like(ci)

        x = x_ref[...]
        xb = x.astype(MXU_DTYPE)
        for c in range(ncl):
            bu = jnp.dot(xb[:, c * CLUSTER_W:(c + 1) * CLUSTER_W], wb_ref[c], preferred_element_type=F32)
            _slab_put(bur, c, Q, ncl, bu[:, :cs])
            _slab_put(bui, c, Q, ncl, bu[:, cs:])
        ar = ar_ref[...]
        ai = ai_ref[...]

        def step(k, carry):
            hr, hi = carry
            sl = pl.ds(pl.multiple_of(k * R, R), R)
            nr = ar * hr - ai * hi + bur[:, sl, :]
            ni = ar * hi + ai * hr + bui[:, sl, :]
            hr_ref[:, sl, :] = nr
            hi_ref[:, sl, :] = ni
            return nr, ni

        hr, hi = lax.fori_loop(0, nsteps, step, (cr[...], ci[...]), unroll=4)
        cr[...] = hr
        ci[...] = hi
        parts = []
        for c in range(ncl):
            hrc = _slab_get(hr_ref, c, Q, ncl).astype(MXU_DTYPE)
            hic = _slab_get(hi_ref, c, Q, ncl).astype(MXU_DTYPE)
            parts.append(jnp.dot(hrc, wc_ref[c, :cs, :], preferred_element_type=F32)
                         + jnp.dot(hic, wc_ref[c, cs:, :], preferred_element_type=F32))
        y = d_ref[...] * x + (parts[0] if ncl == 1 else jnp.concatenate(parts, axis=-1))
        y_ref[...] = y
        yg_ref[...] = _gelu(y).astype(yg_ref.dtype)

    return _pallas(
        body, name=name, grid=(T // Q,),
        in_specs=[_rows(Q, D), _whole(wb.shape), _whole(wc.shape), _whole((ns, R, LANES)), _whole((ns, R, LANES)),
                  _whole((1, D))],
        out_specs=[_rows(Q, D), _rows(Q, D), _slabs(ns, QR), _slabs(ns, QR)],
        out_shape=[jax.ShapeDtypeStruct((T, D), F32), jax.ShapeDtypeStruct((T, D), MXU_DTYPE),
                   jax.ShapeDtypeStruct((ns, T * ncl, LANES), F32), jax.ShapeDtypeStruct((ns, T * ncl, LANES), F32)],
        scratch_shapes=[pltpu.VMEM((ns, QR, LANES), F32), pltpu.VMEM((ns, QR, LANES), F32),
                        pltpu.VMEM((ns, R, LANES), F32), pltpu.VMEM((ns, R, LANES), F32)],
        compiler_params=_params(("arbitrary",)),
    )(xi, wb, wc, a_r, a_i, d_row)


def _s5_bwd(dy, xi, h_r, h_i, wb, wc, a_r, a_i, d_row, B, *, name):
    T, D = dy.shape
    ncl = wb.shape[0]
    cs = wb.shape[2] // 2
    ns = cs // LANES
    R = B * ncl
    Q = S5_CHUNK
    nsteps = Q // B
    nchunk = T // Q
    QR = Q * ncl

    def rev(i):
        return nchunk - 1 - i

    def body(dy_ref, x_ref, hr_ref, hi_ref, pr_ref, pi_ref, wb_ref, wc_ref, ar_ref, ai_ref, d_ref,
             du_ref, gr_ref, gi_ref, dar_ref, dai_ref, dd_ref, dhr, dhi, cr, ci):
        i = pl.program_id(0)

        @pl.when(i == 0)
        def _():
            cr[...] = jnp.zeros_like(cr)
            ci[...] = jnp.zeros_like(ci)
            dar_ref[...] = jnp.zeros_like(dar_ref)
            dai_ref[...] = jnp.zeros_like(dai_ref)
            dd_ref[...] = jnp.zeros_like(dd_ref)

        dyv = dy_ref[...]
        dyb = dyv.astype(MXU_DTYPE)
        for c in range(ncl):
            dh = lax.dot_general(dyb[:, c * CLUSTER_W:(c + 1) * CLUSTER_W], wc_ref[c],
                                 (((1,), (1,)), ((), ())), preferred_element_type=F32)
            _slab_put(dhr, c, Q, ncl, dh[:, :cs])
            _slab_put(dhi, c, Q, ncl, dh[:, cs:])
        ar = ar_ref[...]
        ai = ai_ref[...]

        def step(j, carry):
            gr, gi = carry
            k = nsteps - 1 - j
            sl = pl.ds(pl.multiple_of(k * R, R), R)
            ngr = dhr[:, sl, :] + ar * gr + ai * gi
            ngi = dhi[:, sl, :] - ai * gr + ar * gi
            gr_ref[:, sl, :] = ngr
            gi_ref[:, sl, :] = ngi
            return ngr, ngi

        gr, gi = lax.fori_loop(0, nsteps, step, (cr[...], ci[...]), unroll=4)
        cr[...] = gr
        ci[...] = gi
        keep = jnp.where(i == nchunk - 1, 0.0, 1.0)
        hpr = jnp.concatenate([pr_ref[:, 8 - R:8, :] * keep, hr_ref[:, 0:QR - R, :]], axis=1)
        hpi = jnp.concatenate([pi_ref[:, 8 - R:8, :] * keep, hi_ref[:, 0:QR - R, :]], axis=1)
        gra, gia = gr_ref[...], gi_ref[...]
        steps = lambda t: jnp.sum(t.reshape(ns, nsteps, R, LANES), axis=1)
        dar_ref[...] += steps(gra * hpr + gia * hpi)
        dai_ref[...] += steps(gia * hpr - gra * hpi)
        parts = []
        for c in range(ncl):
            grc = _slab_get(gr_ref, c, Q, ncl).astype(MXU_DTYPE)
            gic = _slab_get(gi_ref, c, Q, ncl).astype(MXU_DTYPE)
            parts.append(lax.dot_general(grc, wb_ref[c, :, :cs], (((1,), (1,)), ((), ())), preferred_element_type=F32)
                         + lax.dot_general(gic, wb_ref[c, :, cs:], (((1,), (1,)), ((), ())), preferred_element_type=F32))
        du_ref[...] = d_ref[...] * dyv + (parts[0] if ncl == 1 else jnp.concatenate(parts, axis=-1))
        dd_ref[...] += jnp.sum(dyv * x_ref[...], axis=0, keepdims=True)

    tok = pl.BlockSpec((Q, D), lambda i: (rev(i), 0))
    st = pl.BlockSpec((ns, QR, LANES), lambda i: (0, rev(i), 0))
    before = pl.BlockSpec((ns, 8, LANES), lambda i: (0, jnp.maximum(rev(i) * (QR // 8) - 1, 0), 0))
    acc = _whole((ns, R, LANES))
    return _pallas(
        body, name=name, grid=(nchunk,),
        in_specs=[tok, tok, st, st, before, before, _whole(wb.shape), _whole(wc.shape), acc, acc, _whole((1, D))],
        out_specs=[tok, st, st, acc, acc, _whole((1, D))],
        out_shape=[jax.ShapeDtypeStruct((T, D), F32),
                   jax.ShapeDtypeStruct((ns, T * ncl, LANES), F32), jax.ShapeDtypeStruct((ns, T * ncl, LANES), F32),
                   jax.ShapeDtypeStruct((ns, R, LANES), F32), jax.ShapeDtypeStruct((ns, R, LANES), F32),
                   jax.ShapeDtypeStruct((1, D), F32)],
        scratch_shapes=[pltpu.VMEM((ns, QR, LANES), F32)] * 2 + [pltpu.VMEM((ns, R, LANES), F32)] * 2,
        compiler_params=_params(("arbitrary",)),
    )(dy, xi, h_r, h_i, h_r, h_i, wb, wc, a_r, a_i, d_row)


def _cluster_tn(tok, st, ncl, *, tok_left, name):
    T = tok.shape[0]
    ns = st.shape[0]
    cs = ns * LANES
    tt = _pick(T, (512, 256, 128))
    nt = T // tt
    oshape = (ncl, CLUSTER_W, cs) if tok_left else (ncl, cs, CLUSTER_W)

    def body(tok_ref, st_ref, o_ref, acc):
        t = pl.program_id(0)

        @pl.when(t == 0)
        def _():
            acc[...] = jnp.zeros_like(acc)

        tk = tok_ref[...].astype(MXU_DTYPE)
        for c in range(ncl):
            tc = tk[:, c * CLUSTER_W:(c + 1) * CLUSTER_W]
            sc = _slab_get(st_ref, c, tt, ncl).astype(MXU_DTYPE)
            lhs, rhs = (tc, sc) if tok_left else (sc, tc)
            acc[c] += lax.dot_general(lhs, rhs, (((0,), (0,)), ((), ())), preferred_element_type=F32)

        @pl.when(t == nt - 1)
        def _():
            o_ref[...] = acc[...]

    return _pallas(
        body, name=name, grid=(nt,),
        in_specs=[_rows(tt, tok.shape[1]), _slabs(ns, tt * ncl)],
        out_specs=_whole(oshape),
        out_shape=jax.ShapeDtypeStruct(oshape, F32),
        scratch_shapes=[pltpu.VMEM(oshape, F32)],
        compiler_params=_params(("arbitrary",)),
    )(tok, st)


def _s5_discretize(lam_re, lam_im, log_dt, b_re, b_im):
    dt = jnp.exp(log_dt)[:, None]
    mag = jnp.exp(lam_re * dt)
    ab_r, ab_i = mag * jnp.cos(lam_im * dt), mag * jnp.sin(lam_im * dt)
    den = lam_re * lam_re + lam_im * lam_im
    nr = ab_r - 1.0
    co_r = (nr * lam_re + ab_i * lam_im) / den
    co_i = (ab_i * lam_re - nr * lam_im) / den
    bb_r = co_r[..., None] * b_re - co_i[..., None] * b_im
    bb_i = co_r[..., None] * b_im + co_i[..., None] * b_re
    return ab_r, ab_i, bb_r, bb_i


def _blockdiag(m):
    G, r, k = m.shape
    ncl = G // GROUPS_PER_CLUSTER
    m4 = m.reshape(ncl, GROUPS_PER_CLUSTER, r, k)
    eye = jnp.eye(GROUPS_PER_CLUSTER, dtype=m.dtype)
    return jnp.einsum('cgrk,gh->cgrhk', m4, eye).reshape(ncl, GROUPS_PER_CLUSTER * r, GROUPS_PER_CLUSTER * k)


def _unblockdiag(m, r, k):
    ncl = m.shape[0]
    m5 = m.reshape(ncl, GROUPS_PER_CLUSTER, r, GROUPS_PER_CLUSTER, k)
    eye = jnp.eye(GROUPS_PER_CLUSTER, dtype=m.dtype)
    return jnp.einsum('cgrhk,gh->cgrk', m5, eye).reshape(ncl * GROUPS_PER_CLUSTER, r, k)


def _t5_bucket(dist):
    exact = REL_BUCKETS // 2
    d = np.maximum(dist, 1).astype(np.float32)
    large = exact + (np.log(d / exact) / math.log(REL_MAX_DIST / exact) * (REL_BUCKETS - exact)).astype(np.int64)
    large = np.minimum(large, REL_BUCKETS - 1)
    return np.where(dist < exact, dist, large).astype(np.int32)


def _band_tables(dil):
    steps = np.arange(BAND)[:, None] + BAND - np.arange(2 * BAND)[None, :]
    bucket = _t5_bucket(np.maximum(steps, 0) * dil)
    in_band = (steps >= 0) & (steps <= BAND)
    return bucket, in_band


def _attn_bias(rel_bias, hpg):
    out = []
    for g, dil in enumerate(DILATIONS):
        bucket, in_band = _band_tables(dil)
        cols = rel_bias[:, g * hpg:(g + 1) * hpg].astype(F32)
        onehot = jnp.asarray((bucket.reshape(-1, 1) == np.arange(REL_BUCKETS)[None, :]).astype(np.float32))
        bias = jnp.dot(onehot, cols, precision=lax.Precision.HIGHEST).T.reshape(hpg, BAND, 2 * BAND)
        out.append(jnp.where(jnp.asarray(in_band)[None], bias, NEG_BIG))
    return jnp.concatenate(out, axis=0)


def _attn_blocks(dil, L):
    M = L // dil
    return M, M // BAND


def _row_sel(r, M, dil):
    return pl.ds(r, M) if dil == 1 else pl.ds(r, M, stride=dil)


def _attn_fwd(q, kv, bias, L, hpg, *, name):
    T = q.shape[0]
    nb_ = T // L
    HP = hpg // 2
    W3 = 3 * hpg * HEAD_DIM
    mmax = L

    def group_body(dil, q_ref, k_ref, v_ref, b_ref, o_ref, l_ref, os, ls):
        M, NB = _attn_blocks(dil, L)
        for r in range(dil):
            rows = _row_sel(r, M, dil)
            first = lax.broadcasted_iota(jnp.int32, (1, 2 * HEAD_DIM), 1) < HEAD_DIM
            qf = q_ref[rows, :] * 0.125
            qm = [jnp.where(first, qf, 0.0).astype(MXU_DTYPE), jnp.where(first, 0.0, qf).astype(MXU_DTYPE)]
            kr = k_ref[rows, :].astype(MXU_DTYPE)
            va = jnp.concatenate([v_ref[rows, :].astype(MXU_DTYPE), jnp.ones((M, 2 * HEAD_DIM), MXU_DTYPE)], axis=-1)
            for n in range(NB):
                qs = slice(n * BAND, (n + 1) * BAND)
                ks = slice(0, BAND) if n == 0 else slice((n - 1) * BAND, (n + 1) * BAND)
                o_h, l_h = [], []
                for hh in range(2):
                    bb = b_ref[hh, :, BAND:] if n == 0 else b_ref[hh]
                    s = lax.dot_general(qm[hh][qs, :], kr[ks, :], (((1,), (1,)), ((), ())),
                                        preferred_element_type=F32) + bb
                    m = jnp.max(s, axis=-1, keepdims=True)
                    p = jnp.exp(s - m)
                    pv = jnp.dot(p.astype(MXU_DTYPE), va[ks, :], preferred_element_type=F32)
                    l = pv[:, 2 * HEAD_DIM:]
                    o_h.append(pv[:, :2 * HEAD_DIM] / l)
                    l_h.append(m + jnp.log(l))
                os[qs, :] = jnp.where(first, o_h[0], o_h[1])
                ls[qs, :] = jnp.where(first, l_h[0], l_h[1])
            o_ref[rows, :] = os[0:M, :]
            l_ref[rows, :] = ls[0:M, :]

    def body(q_ref, k_ref, v_ref, b_ref, o_ref, l_ref, os, ls):
        g = pl.program_id(0)
        for gi, dil in enumerate(DILATIONS):
            pl.when(g == gi)(functools.partial(group_body, dil, q_ref, k_ref, v_ref, b_ref, o_ref, l_ref, os, ls))

    blk = (L, 2 * HEAD_DIM)
    return _pallas(
        body, name=name, grid=(3, nb_, HP),
        in_specs=[pl.BlockSpec(blk, lambda g, b, h: (b, g * HP + h)),
                  pl.BlockSpec(blk, lambda g, b, h: (b, g * HP + h)),
                  pl.BlockSpec(blk, lambda g, b, h: (b, 3 * HP + g * HP + h)),
                  pl.BlockSpec((2, BAND, 2 * BAND), lambda g, b, h: (g * HP + h, 0, 0))],
        out_specs=[pl.BlockSpec(blk, lambda g, b, h: (b, g * HP + h)),
                   pl.BlockSpec(blk, lambda g, b, h: (b, g * HP + h))],
        out_shape=[jax.ShapeDtypeStruct((T, W3), F32), jax.ShapeDtypeStruct((T, W3), F32)],
        scratch_shapes=[pltpu.VMEM((mmax, 2 * HEAD_DIM), F32), pltpu.VMEM((mmax, 2 * HEAD_DIM), F32)],
        compiler_params=_params(("arbitrary", "arbitrary", "arbitrary")),
    )(q, kv, kv, bias)


def _attn_merge(o3, l3, hw, *, name):
    T = o3.shape[0]
    tm = _pick(T, (256, 128))

    def body(o0, o1, o2, l0, l1, l2, o_ref, ob_ref, lse_ref):
        a0, a1, a2 = l0[...], l1[...], l2[...]
        m = jnp.maximum(jnp.maximum(a0, a1), a2)
        e0, e1, e2 = jnp.exp(a0 - m), jnp.exp(a1 - m), jnp.exp(a2 - m)
        z = e0 + e1 + e2
        o = (e0 * o0[...] + e1 * o1[...] + e2 * o2[...]) / z
        o_ref[...] = o
        ob_ref[...] = o.astype(ob_ref.dtype)
        lse_ref[...] = m + jnp.log(z)

    def col(g):
        return pl.BlockSpec((tm, hw), lambda i: (i, g))

    return _pallas(
        body, name=name, grid=(T // tm,),
        in_specs=[col(0), col(1), col(2), col(0), col(1), col(2)],
        out_specs=[_rows(tm, hw)] * 3,
        out_shape=[jax.ShapeDtypeStruct((T, hw), F32), jax.ShapeDtypeStruct((T, hw), MXU_DTYPE),
                   jax.ShapeDtypeStruct((T, hw), F32)],
        compiler_params=_params(("parallel",)),
    )(o3, o3, o3, l3, l3, l3)


def _attn_bwd(q, kv, do, o, lse, bias, L, hpg, *, name):
    T = q.shape[0]
    nb_ = T // L
    HP = hpg // 2
    W3 = 3 * hpg * HEAD_DIM
    mmax = L

    def group_body(dil, q_ref, k_ref, v_ref, do_ref, o_ref, l_ref, b_ref, dq_ref, dk_ref, dv_ref, ds_ref,
                   dqs, dks, dvs):
        M, NB = _attn_blocks(dil, L)
        for r in range(dil):
            rows = _row_sel(r, M, dil)
            first = lax.broadcasted_iota(jnp.int32, (1, 2 * HEAD_DIM), 1) < HEAD_DIM
            qf = q_ref[rows, :] * 0.125
            qm = [jnp.where(first, qf, 0.0).astype(MXU_DTYPE), jnp.where(first, 0.0, qf).astype(MXU_DTYPE)]
            kr = k_ref[rows, :].astype(MXU_DTYPE)
            vr = v_ref[rows, :].astype(MXU_DTYPE)
            dof = do_ref[rows, :]
            dom = [jnp.where(first, dof, 0.0).astype(MXU_DTYPE), jnp.where(first, 0.0, dof).astype(MXU_DTYPE)]
            dod = dof * o_ref[rows, :]
            delta = [jnp.sum(jnp.where(first, dod, 0.0), axis=-1, keepdims=True),
                     jnp.sum(jnp.where(first, 0.0, dod), axis=-1, keepdims=True)]
            lr = l_ref[rows, :]
            lse = [lr[:, 0:1], lr[:, HEAD_DIM:HEAD_DIM + 1]]
            dks[0:M, :] = jnp.zeros((M, 2 * HEAD_DIM), F32)
            dvs[0:M, :] = jnp.zeros((M, 2 * HEAD_DIM), F32)
            for n in range(NB):
                qs = slice(n * BAND, (n + 1) * BAND)
                ks = slice(0, BAND) if n == 0 else slice((n - 1) * BAND, (n + 1) * BAND)
                dq_h = []
                dkc = dvc = None
                for hh in range(2):
                    bb = b_ref[hh, :, BAND:] if n == 0 else b_ref[hh]
                    qb, dob = qm[hh][qs, :], dom[hh][qs, :]
                    s = lax.dot_general(qb, kr[ks, :], (((1,), (1,)), ((), ())), preferred_element_type=F32) + bb
                    p = jnp.exp(s - lse[hh][qs, :])
                    dp = lax.dot_general(dob, vr[ks, :], (((1,), (1,)), ((), ())), preferred_element_type=F32)
                    ds = p * (dp - delta[hh][qs, :])
                    if n == 0:
                        ds_ref[hh, :, BAND:] += ds
                    else:
                        ds_ref[hh] += ds
                    dsm = ds.astype(MXU_DTYPE)
                    dq_h.append(jnp.dot(dsm, kr[ks, :], preferred_element_type=F32))
                    dk1 = lax.dot_general(dsm, qb, (((0,), (0,)), ((), ())), preferred_element_type=F32)
                    dv1 = lax.dot_general(p.astype(MXU_DTYPE), dob, (((0,), (0,)), ((), ())), preferred_element_type=F32)
                    dkc = dk1 if dkc is None else dkc + dk1
                    dvc = dv1 if dvc is None else dvc + dv1
                dqs[qs, :] = jnp.where(first, dq_h[0], dq_h[1]) * 0.125
                dks[ks, :] += dkc
                dvs[ks, :] += dvc
            dq_ref[rows, :] = dqs[0:M, :]
            dk_ref[rows, :] = dks[0:M, :]
            dv_ref[rows, :] = dvs[0:M, :]

    def body(q_ref, k_ref, v_ref, do_ref, o_ref, l_ref, b_ref, dq_ref, dk_ref, dv_ref, ds_ref, dqs, dks, dvs):
        g = pl.program_id(0)

        @pl.when(pl.program_id(2) == 0)
        def _():
            ds_ref[...] = jnp.zeros_like(ds_ref)

        for gi, dil in enumerate(DILATIONS):
            pl.when(g == gi)(functools.partial(group_body, dil, q_ref, k_ref, v_ref, do_ref, o_ref, l_ref, b_ref,
                                               dq_ref, dk_ref, dv_ref, ds_ref, dqs, dks, dvs))

    blk = (L, 2 * HEAD_DIM)
    gcol = lambda g, h, b: (b, g * HP + h)
    hcol = lambda g, h, b: (b, h)
    return _pallas(
        body, name=name, grid=(3, HP, nb_),
        in_specs=[pl.BlockSpec(blk, gcol), pl.BlockSpec(blk, gcol),
                  pl.BlockSpec(blk, lambda g, h, b: (b, 3 * HP + g * HP + h)),
                  pl.BlockSpec(blk, hcol), pl.BlockSpec(blk, hcol), pl.BlockSpec(blk, hcol),
                  pl.BlockSpec((2, BAND, 2 * BAND), lambda g, h, b: (g * HP + h, 0, 0))],
        out_specs=[pl.BlockSpec(blk, gcol), pl.BlockSpec(blk, gcol), pl.BlockSpec(blk, gcol),
                   pl.BlockSpec((2, BAND, 2 * BAND), lambda g, h, b: (g * HP + h, 0, 0))],
        out_shape=[jax.ShapeDtypeStruct((T, W3), F32), jax.ShapeDtypeStruct((T, W3), F32),
                   jax.ShapeDtypeStruct((T, W3), F32), jax.ShapeDtypeStruct((3 * hpg, BAND, 2 * BAND), F32)],
        scratch_shapes=[pltpu.VMEM((mmax, 2 * HEAD_DIM), F32)] * 3,
        compiler_params=_params(("arbitrary", "arbitrary", "arbitrary")),
    )(q, kv, kv, do, o, lse, bias)


def _bias_grad(ds_sum, hpg, *, name):
    nh = ds_sum.shape[0]
    idx = np.stack([np.where(_band_tables(dil)[1], _band_tables(dil)[0], -1) for dil in DILATIONS]).astype(np.int32)

    def body(ds_ref, idx_ref, o_ref):
        d = ds_ref[...]
        ix = idx_ref[...]
        lane = lax.broadcasted_iota(jnp.int32, (8, 128), 1)
        row = jnp.zeros((8, 128), F32)
        for b in range(REL_BUCKETS):
            row = row + jnp.where(lane == b, jnp.sum(jnp.where(ix == b, d, 0.0)), 0.0)
        o_ref[...] = row

    out = _pallas(
        body, name=name, grid=(nh,),
        in_specs=[pl.BlockSpec((None, BAND, 2 * BAND), lambda h: (h, 0, 0)),
                  pl.BlockSpec((None, BAND, 2 * BAND), lambda h: (h // hpg, 0, 0))],
        out_specs=pl.BlockSpec((None, 8, 128), lambda h: (h, 0, 0)),
        out_shape=jax.ShapeDtypeStruct((nh, 8, 128), F32),
        compiler_params=_params(("parallel",)),
    )(ds_sum, jnp.asarray(idx))
    return out[:, 0, :REL_BUCKETS].T


def _adamw(w, g, m, v, *, name):
    Rw, C = w.shape
    tm = _pick(Rw, (512, 352, 256, 128, 64, 32, 16, 8))

    def body(w_ref, g_ref, m_ref, v_ref, d_ref, nm_ref, nv_ref):
        gg = g_ref[...]
        nm = ADAM_B1 * m_ref[...] + (1.0 - ADAM_B1) * gg
        nv = ADAM_B2 * v_ref[...] + (1.0 - ADAM_B2) * (gg * gg)
        m_hat = nm / (1.0 - ADAM_B1 ** ADAM_STEP)
        v_hat = nv / (1.0 - ADAM_B2 ** ADAM_STEP)
        d_ref[...] = -ADAM_LR * (m_hat / (jnp.sqrt(v_hat) + ADAM_EPS) + ADAM_WD * w_ref[...])
        nm_ref[...] = nm
        nv_ref[...] = nv

    return _pallas(
        body, name=name, grid=(Rw // tm,), in_specs=[_rows(tm, C)] * 4, out_specs=[_rows(tm, C)] * 3,
        out_shape=[jax.ShapeDtypeStruct((Rw, C), F32)] * 3, compiler_params=_params(("parallel",)),
    )(w, g, m, v)


ROW_TILE_ELEMS = 256 * 1024


def _tile_rows(r, c):
    best = 8
    for t in range(8, r + 1, 8):
        if r % t == 0 and t * c <= ROW_TILE_ELEMS:
            best = t
    return best


def _adamw_halves(w, m, v, mine, other, cidx, *, layer=0, prev=None, name):
    NL, _, r, c = w.shape
    tm = _tile_rows(r, c)

    def body(c_ref, w_ref, m_ref, v_ref, a_ref, b_ref, *rest):
        g_ref, d_ref, nm_ref, nv_ref = rest[-4:]
        gg = jnp.where(pl.program_id(0) == c_ref[0], a_ref[...], b_ref[...])
        nm = ADAM_B1 * m_ref[...] + (1.0 - ADAM_B1) * gg
        nv = ADAM_B2 * v_ref[...] + (1.0 - ADAM_B2) * (gg * gg)
        m_hat = nm / (1.0 - ADAM_B1 ** ADAM_STEP)
        v_hat = nv / (1.0 - ADAM_B2 ** ADAM_STEP)
        g_ref[...] = gg
        d_ref[...] = -ADAM_LR * (m_hat / (jnp.sqrt(v_hat) + ADAM_EPS) + ADAM_WD * w_ref[...])
        nm_ref[...] = nm
        nv_ref[...] = nv

    half = pl.BlockSpec((None, None, tm, c), lambda h, i, cr: (layer, h, i, 0))
    one = pl.BlockSpec((None, tm, c), lambda h, i, cr: (0, i, 0))
    in_specs = [half, half, half, one, one]
    args = [cidx, w, m, v, mine, other]
    aliases = {}
    if prev is not None:
        in_specs += [_ANY] * 4
        args += list(prev)
        aliases = {6 + k: k for k in range(4)}
    spec = pltpu.PrefetchScalarGridSpec(num_scalar_prefetch=1, grid=(2, r // tm), in_specs=in_specs, out_specs=[half] * 4)
    return _pallas(
        body, name=name, grid_spec=spec, out_shape=[jax.ShapeDtypeStruct((NL, 2, r, c), F32)] * 4,
        input_output_aliases=aliases, compiler_params=_params(("parallel", "parallel")),
    )(*args)


def _pair_sum(g, theirs, cidx, *, cast, name):
    _, _, r, c = g.shape
    tm = _tile_rows(r, c)

    def body(c_ref, g_ref, t_ref, *outs):
        s = g_ref[...] + t_ref[...]
        outs[0][...] = s
        if cast:
            outs[1][...] = s.astype(BF16)

    blk = (None, None, tm, c)
    first = pl.BlockSpec(blk, lambda p, i, cr: (p, 0, i, 0))
    shapes = [jax.ShapeDtypeStruct((4, 1, r, c), F32)] + ([jax.ShapeDtypeStruct((4, 1, r, c), BF16)] if cast else [])
    spec = pltpu.PrefetchScalarGridSpec(
        num_scalar_prefetch=1, grid=(4, r // tm),
        in_specs=[pl.BlockSpec(blk, lambda p, i, cr: (p, cr[0], i, 0)), first], out_specs=[first] * len(shapes))
    return _pallas(body, name=name, grid_spec=spec, out_shape=shapes,
                   compiler_params=_params(("parallel", "parallel")))(cidx, g, theirs)


def _chip_sum(hf, got, chip_idx, *, name):
    _, _, r, c = hf.shape
    tm = _tile_rows(r, c)

    def body(p_ref, h_ref, r_ref, o_ref):
        s = h_ref[...]
        for k in range(3):
            s = s + r_ref[k].astype(F32)
        o_ref[...] = s

    spec = pltpu.PrefetchScalarGridSpec(
        num_scalar_prefetch=1, grid=(r // tm,),
        in_specs=[pl.BlockSpec((None, None, tm, c), lambda i, pr: (pr[0], 0, i, 0)),
                  pl.BlockSpec((3, None, tm, c), lambda i, pr: (0, 0, i, 0))],
        out_specs=pl.BlockSpec((None, tm, c), lambda i, pr: (0, i, 0)))
    return _pallas(body, name=name, grid_spec=spec, out_shape=jax.ShapeDtypeStruct((1, r, c), F32),
                   compiler_params=_params(("parallel",)))(chip_idx, hf, got)


def _place():
    x, y, c = lax.axis_index("x"), lax.axis_index("y"), lax.axis_index("c")
    chips = [(1 - x, y), (x, 1 - y), (1 - x, 1 - y)]
    return x, y, c, chips


_ANY = pl.BlockSpec(memory_space=pl.ANY)


def _comm_call(body, ins, out_shapes, n_remote, *, name, aliases=None):
    sems = [pltpu.SemaphoreType.DMA((n,)) for n in n_remote]
    return _pallas(
        body, name=name, in_specs=[_ANY] * len(ins), out_specs=[_ANY] * len(out_shapes), out_shape=out_shapes,
        scratch_shapes=sems, input_output_aliases=aliases or {},
        compiler_params=pltpu.CompilerParams(has_side_effects=True),
    )(*ins)


_HBM_SPEC = pl.BlockSpec(memory_space=pltpu.HBM)
_SEM_SPEC = pl.BlockSpec(memory_space=pltpu.SEMAPHORE)
_DATAFLOW = pltpu.SideEffectType.DATAFLOW_SIDE_EFFECTING


def _in_hbm(a):
    return pltpu.with_memory_space_constraint(a, pltpu.HBM)


def _gather_start(groups, *, name):
    flat = [s for g in groups for s in g]
    n, ng = len(flat), len(groups)

    def body(*refs):
        ins, lands = refs[:n], refs[n:2 * n]
        sems = refs[2 * n:2 * n + 2 * ng]
        token = refs[-1]
        x, y, c, chips = _place()
        me = 2 * x + y
        a = 0
        for gi, g in enumerate(groups):
            for j in range(len(g)):
                for k, (tx, ty) in enumerate(chips):
                    _rcopy(ins[a].at[c], lands[a].at[me, c], sems[2 * gi].at[3 * j + k], sems[2 * gi + 1].at[3 * j + k],
                           (tx, ty, c)).start()
                a += 1
        token[...] = jnp.zeros_like(token)

    land_shapes = [(4,) + s.shape for s in flat]
    out_shape = ([pltpu.SemaphoreType.DMA((3 * len(g),)) for g in groups for _ in range(2)]
                 + [pltpu.HBM(s.shape, s.dtype) for s in flat]
                 + [pltpu.HBM(ls, s.dtype) for ls, s in zip(land_shapes, flat)]
                 + [jax.ShapeDtypeStruct((8, 128), F32)])
    outs = _pallas(
        body, name=name, in_specs=[_HBM_SPEC] * (2 * n),
        out_specs=[_SEM_SPEC] * (2 * ng) + [_HBM_SPEC] * (2 * n) + [pl.BlockSpec(memory_space=pltpu.VMEM)],
        out_shape=out_shape, input_output_aliases={i: 2 * ng + i for i in range(2 * n)},
        compiler_params=pltpu.CompilerParams(has_side_effects=_DATAFLOW),
    )(*[_in_hbm(s) for s in flat], *[_in_hbm(lax.empty(ls, s.dtype)) for ls, s in zip(land_shapes, flat)])
    sems, thru, lands, token = outs[:2 * ng], outs[2 * ng:2 * ng + n], outs[2 * ng + n:2 * ng + 2 * n], outs[-1]
    res, a = [], 0
    for gi, g in enumerate(groups):
        res.append((sems[2 * gi], sems[2 * gi + 1], thru[a:a + len(g)], lands[a:a + len(g)]))
        a += len(g)
    return res, token


def _gather_wait(ssem, rsem, shards, lands, after, *, name):
    m = len(shards)

    def body(*refs):
        ins, lnd = refs[:m], refs[m:2 * m]
        ss, rs = refs[2 * m], refs[2 * m + 1]
        x, y, c, chips = _place()
        for j in range(m):
            for k, (tx, ty) in enumerate(chips):
                cp = _rcopy(ins[j].at[c], lnd[j].at[2 * tx + ty, c], ss.at[3 * j + k], rs.at[3 * j + k], (tx, ty, c))
                cp.wait_send()
                cp.wait_recv()

    outs = _pallas(
        body, name=name, in_specs=[_HBM_SPEC] * (2 * m) + [_SEM_SPEC, _SEM_SPEC, _ANY],
        out_specs=[_HBM_SPEC] * (2 * m),
        out_shape=[pltpu.HBM(s.shape, s.dtype) for s in shards] + [pltpu.HBM(l.shape, l.dtype) for l in lands],
        input_output_aliases={i: i for i in range(2 * m)},
        compiler_params=pltpu.CompilerParams(has_side_effects=_DATAFLOW),
    )(*shards, *lands, ssem, rsem, after)
    return outs[m:]


def _gather_forward(lands, *, name):
    n = len(lands)

    def body(*refs):
        outs = refs[n:2 * n]
        ssem, rsem = refs[2 * n:]
        x, y, c, chips = _place()
        sib = (x, y, 1 - c)
        cps = []
        for a in range(n):
            for k, (tx, ty) in enumerate(chips):
                pk = 2 * tx + ty
                cp = _rcopy(outs[a].at[pk, c], outs[a].at[pk, c], ssem.at[3 * a + k], rsem.at[3 * a + k], sib)
                cp.start()
                cps.append(cp)
        for a in range(n):
            for k, (tx, ty) in enumerate(chips):
                pk = 2 * tx + ty
                _rcopy(outs[a].at[pk, c], outs[a].at[pk, 1 - c], ssem.at[3 * a + k], rsem.at[3 * a + k], sib).wait_recv()
        for cp in cps:
            cp.wait_send()

    shapes = [jax.ShapeDtypeStruct(l.shape, l.dtype) for l in lands]
    return _comm_call(body, lands, shapes, [3 * n, 3 * n], name=name, aliases={i: i for i in range(n)})


def _gather_forward_start(lands, *, name):
    n = len(lands)

    def body(*refs):
        ssem, rsem = refs[n], refs[n + 1]
        outs = refs[n + 2:2 * n + 2]
        token = refs[-1]
        x, y, c, chips = _place()
        for a in range(n):
            for k, (tx, ty) in enumerate(chips):
                pk = 2 * tx + ty
                _rcopy(outs[a].at[pk, c], outs[a].at[pk, c], ssem.at[3 * a + k], rsem.at[3 * a + k], (x, y, 1 - c)).start()
        token[...] = jnp.zeros_like(token)

    outs = _pallas(
        body, name=name, in_specs=[_HBM_SPEC] * n,
        out_specs=[_SEM_SPEC] * 2 + [_HBM_SPEC] * n + [pl.BlockSpec(memory_space=pltpu.VMEM)],
        out_shape=([pltpu.SemaphoreType.DMA((3 * n,))] * 2 + [pltpu.HBM(l.shape, l.dtype) for l in lands]
                   + [jax.ShapeDtypeStruct((8, 128), F32)]),
        input_output_aliases={i: 2 + i for i in range(n)},
        compiler_params=pltpu.CompilerParams(has_side_effects=_DATAFLOW),
    )(*lands)
    return (outs[0], outs[1], outs[2:2 + n]), outs[-1]


def _gather_forward_wait(started, after, *, name):
    ssem, rsem, lands = started
    n = len(lands)

    def body(*refs):
        lnd = refs[:n]
        ss, rs = refs[n], refs[n + 1]
        x, y, c, chips = _place()
        sib = (x, y, 1 - c)
        for a in range(n):
            for k, (tx, ty) in enumerate(chips):
                pk = 2 * tx + ty
                _rcopy(lnd[a].at[pk, c], lnd[a].at[pk, 1 - c], ss.at[3 * a + k], rs.at[3 * a + k], sib).wait_recv()
                _rcopy(lnd[a].at[pk, c], lnd[a].at[pk, c], ss.at[3 * a + k], rs.at[3 * a + k], sib).wait_send()

    return _pallas(
        body, name=name, in_specs=[_HBM_SPEC] * n + [_SEM_SPEC, _SEM_SPEC, _ANY], out_specs=[_HBM_SPEC] * n,
        out_shape=[pltpu.HBM(l.shape, l.dtype) for l in lands], input_output_aliases={i: i for i in range(n)},
        compiler_params=pltpu.CompilerParams(has_side_effects=_DATAFLOW),
    )(*lands, ssem, rsem, after)


class _Lazy:
    def __init__(self, group_of, make, prepare):
        self._group_of, self._make, self._prepare, self._done, self._anchor = group_of, make, prepare, {}, None

    def anchor(self, value):
        self._anchor = value

    def prepare(self, key, value):
        return self._prepare(self._group_of[key], value)

    def __getitem__(self, key):
        g = self._group_of[key]
        if g not in self._done:
            self._done[g] = self._make(g, self._anchor)
        return self._done[g][key]


def _anchor(mapping, value):
    if isinstance(mapping, _Lazy):
        mapping.anchor(value)


def _prepare(mapping, key, value):
    return mapping.prepare(key, value)[0, 0] if isinstance(mapping, _Lazy) else 0.0


def _rcopy(src, dst, ssem, rsem, dev):
    return pltpu.make_async_remote_copy(src_ref=src, dst_ref=dst, send_sem=ssem, recv_sem=rsem,
                                        device_id=dev, device_id_type=MESH)


def _all_gather(shards, *, name):
    n = len(shards)

    def body(*refs):
        ins, outs = refs[:n], refs[n:2 * n]
        s_ici, r_ici, s_d2d, r_d2d = refs[2 * n:]
        x, y, c, chips = _place()
        me = 2 * x + y
        sib = (x, y, 1 - c)
        sends = []
        for a in range(n):
            for k, (tx, ty) in enumerate(chips):
                cp = _rcopy(ins[a].at[c], outs[a].at[me, c], s_ici.at[3 * a + k], r_ici.at[3 * a + k], (tx, ty, c))
                cp.start()
                sends.append(cp)
        for a in range(n):
            for k, (tx, ty) in enumerate(chips):
                pk = 2 * tx + ty
                _rcopy(ins[a].at[c], outs[a].at[pk, c], s_ici.at[3 * a + k], r_ici.at[3 * a + k], (tx, ty, c)).wait_recv()
                fw = _rcopy(outs[a].at[pk, c], outs[a].at[pk, c], s_d2d.at[3 * a + k], r_d2d.at[3 * a + k], sib)
                fw.start()
                sends.append(fw)
        for a in range(n):
            for k, (tx, ty) in enumerate(chips):
                pk = 2 * tx + ty
                _rcopy(ins[a].at[c], outs[a].at[pk, 1 - c], s_d2d.at[3 * a + k], r_d2d.at[3 * a + k], sib).wait_recv()
        for cp in sends:
            cp.wait_send()

    shapes = [jax.ShapeDtypeStruct((4,) + s.shape, s.dtype) for s in shards]
    return _comm_call(body, shards, shapes, [3 * n] * 4, name=name)


def _gather(shards, chip, *, name):
    outs = _all_gather(shards, name=name)
    return [lax.dynamic_update_slice(o, s[None], (chip, 0, 0, 0)) for o, s in zip(outs, shards)]


def _pair_send(gs, *, name):
    n = len(gs)

    def body(*refs):
        ins, theirs = refs[:n], refs[n:2 * n]
        ssem, rsem = refs[2 * n:]
        x, y, c, _ = _place()
        sib = (x, y, 1 - c)
        cps = []
        for a in range(n):
            cp = _rcopy(ins[a].at[:, pl.ds(1 - c, 1)], theirs[a], ssem.at[a], rsem.at[a], sib)
            cp.start()
            cps.append(cp)
        for cp in cps:
            cp.wait_send()
            cp.wait_recv()

    shapes = [jax.ShapeDtypeStruct((4, 1) + g.shape[2:], g.dtype) for g in gs]
    return _comm_call(body, gs, shapes, [n, n], name=name)


def _chip_exchange(hx, *, name):
    n = len(hx)

    def body(*refs):
        hxr, got = refs[:n], refs[n:2 * n]
        ssem, rsem = refs[2 * n:]
        x, y, c, chips = _place()
        cps = []
        for a in range(n):
            for k, (tx, ty) in enumerate(chips):
                cp = _rcopy(hxr[a].at[2 * tx + ty], got[a].at[k], ssem.at[3 * a + k], rsem.at[3 * a + k], (tx, ty, c))
                cp.start()
                cps.append(cp)
        for cp in cps:
            cp.wait_send()
            cp.wait_recv()

    shapes = [jax.ShapeDtypeStruct((3,) + h.shape[1:], h.dtype) for h in hx]
    return _comm_call(body, hx, shapes, [3 * n, 3 * n], name=name)


def _pair_swap(fs, *, name):
    n = len(fs)

    def body(*refs):
        ins, outs = refs[:n], refs[n:2 * n]
        ssem, rsem = refs[2 * n:]
        x, y, c, _ = _place()
        cps = []
        for a in range(n):
            cp = _rcopy(ins[a], outs[a], ssem.at[a], rsem.at[a], (x, y, 1 - c))
            cp.start()
            cps.append(cp)
        for cp in cps:
            cp.wait_send()
            cp.wait_recv()

    shapes = [jax.ShapeDtypeStruct(f.shape, f.dtype) for f in fs]
    return _comm_call(body, fs, shapes, [n, n], name=name)


def _chip_exchange_start(hx, *, name):
    n = len(hx)

    def body(*refs):
        ins, gots = refs[:n], refs[n:2 * n]
        ssem, rsem = refs[2 * n], refs[2 * n + 1]
        token = refs[-1]
        x, y, c, chips = _place()
        for a in range(n):
            for k, (tx, ty) in enumerate(chips):
                _rcopy(ins[a].at[2 * tx + ty], gots[a].at[k], ssem.at[3 * a + k], rsem.at[3 * a + k], (tx, ty, c)).start()
        token[...] = jnp.zeros_like(token)

    got_shapes = [(3,) + h.shape[1:] for h in hx]
    outs = _pallas(
        body, name=name, in_specs=[_HBM_SPEC] * (2 * n),
        out_specs=[_SEM_SPEC] * 2 + [_HBM_SPEC] * (2 * n) + [pl.BlockSpec(memory_space=pltpu.VMEM)],
        out_shape=([pltpu.SemaphoreType.DMA((3 * n,))] * 2 + [pltpu.HBM(h.shape, h.dtype) for h in hx]
                   + [pltpu.HBM(gs, h.dtype) for gs, h in zip(got_shapes, hx)] + [jax.ShapeDtypeStruct((8, 128), F32)]),
        input_output_aliases={i: 2 + i for i in range(2 * n)},
        compiler_params=pltpu.CompilerParams(has_side_effects=_DATAFLOW),
    )(*[_in_hbm(h) for h in hx], *[_in_hbm(lax.empty(gs, h.dtype)) for gs, h in zip(got_shapes, hx)])
    return (outs[0], outs[1], outs[2:2 + n], outs[2 + n:2 + 2 * n]), outs[-1]


def _chip_exchange_wait(started, after, *, name):
    ssem, rsem, hx, gots = started
    n = len(hx)

    def body(*refs):
        ins, gts = refs[:n], refs[n:2 * n]
        ss, rs = refs[2 * n], refs[2 * n + 1]
        x, y, c, chips = _place()
        for a in range(n):
            for k, (tx, ty) in enumerate(chips):
                cp = _rcopy(ins[a].at[2 * tx + ty], gts[a].at[k], ss.at[3 * a + k], rs.at[3 * a + k], (tx, ty, c))
                cp.wait_send()
                cp.wait_recv()

    outs = _pallas(
        body, name=name, in_specs=[_HBM_SPEC] * (2 * n) + [_SEM_SPEC, _SEM_SPEC, _ANY],
        out_specs=[_HBM_SPEC] * (2 * n),
        out_shape=[pltpu.HBM(h.shape, h.dtype) for h in hx] + [pltpu.HBM(g.shape, g.dtype) for g in gots],
        input_output_aliases={i: i for i in range(2 * n)},
        compiler_params=pltpu.CompilerParams(has_side_effects=_DATAFLOW),
    )(*hx, *gots, ssem, rsem, after)
    return outs[n:]


def _pair_send_start(gs, *, name):
    n = len(gs)

    def body(*refs):
        ins, lands = refs[:n], refs[n:2 * n]
        ssem, rsem = refs[2 * n], refs[2 * n + 1]
        token = refs[-1]
        x, y, c, _ = _place()
        for a in range(n):
            _rcopy(ins[a].at[:, pl.ds(1 - c, 1)], lands[a], ssem.at[a], rsem.at[a], (x, y, 1 - c)).start()
        token[...] = jnp.zeros_like(token)

    land_shapes = [(4, 1) + g.shape[2:] for g in gs]
    outs = _pallas(
        body, name=name, in_specs=[_HBM_SPEC] * (2 * n),
        out_specs=[_SEM_SPEC] * 2 + [_HBM_SPEC] * (2 * n) + [pl.BlockSpec(memory_space=pltpu.VMEM)],
        out_shape=([pltpu.SemaphoreType.DMA((n,))] * 2 + [pltpu.HBM(g.shape, g.dtype) for g in gs]
                   + [pltpu.HBM(ls, g.dtype) for ls, g in zip(land_shapes, gs)] + [jax.ShapeDtypeStruct((8, 128), F32)]),
        input_output_aliases={i: 2 + i for i in range(2 * n)},
        compiler_params=pltpu.CompilerParams(has_side_effects=_DATAFLOW),
    )(*[_in_hbm(g) for g in gs], *[_in_hbm(lax.empty(ls, g.dtype)) for ls, g in zip(land_shapes, gs)])
    return (outs[0], outs[1], outs[2:2 + n], outs[2 + n:2 + 2 * n]), outs[-1]


def _pair_send_wait(started, after, *, name):
    ssem, rsem, gs, lands = started
    n = len(gs)

    def body(*refs):
        ins, lnd = refs[:n], refs[n:2 * n]
        ss, rs = refs[2 * n], refs[2 * n + 1]
        x, y, c, _ = _place()
        for a in range(n):
            cp = _rcopy(ins[a].at[:, pl.ds(1 - c, 1)], lnd[a], ss.at[a], rs.at[a], (x, y, 1 - c))
            cp.wait_send()
            cp.wait_recv()

    outs = _pallas(
        body, name=name, in_specs=[_HBM_SPEC] * (2 * n) + [_SEM_SPEC, _SEM_SPEC, _ANY],
        out_specs=[_HBM_SPEC] * (2 * n),
        out_shape=[pltpu.HBM(g.shape, g.dtype) for g in gs] + [pltpu.HBM(l.shape, l.dtype) for l in lands],
        input_output_aliases={i: i for i in range(2 * n)},
        compiler_params=pltpu.CompilerParams(has_side_effects=_DATAFLOW),
    )(*gs, *lands, ssem, rsem, after)
    return list(outs[:n]), list(outs[n:])


def _pair_sums(grads, exch_bf16, cidx, tag, theirs=None):
    if theirs is None:
        theirs = _pair_send(grads, name=f"rs_pair_send_{tag}")
    hf, hx = [], []
    for a in range(len(grads)):
        res = _pair_sum(grads[a], theirs[a], cidx, cast=exch_bf16[a], name=f"rs_pair_sum_{tag}{a}")
        hf.append(res[0])
        hx.append(res[1] if exch_bf16[a] else res[0])
    return hf, hx


def _chip_sums(hf, got, chip_idx, tag):
    return [_chip_sum(hf[a], got[a], chip_idx, name=f"rs_chip_sum_{tag}{a}") for a in range(len(hf))]


def _interleave(a, B, L):
    return a.reshape(B, L, -1).transpose(1, 0, 2).reshape(B * L, -1)


def _deinterleave(a, B, L):
    return a.reshape(L, B, -1).transpose(1, 0, 2).reshape(B * L, -1)


def _local_step(x, tgt, W, S, on_grads=None):
    B, L, D = x.shape
    T = B * L
    G = D // SSM_GROUP
    Pst = SSM_STATE
    hpg = D // HEAD_DIM
    HW = hpg * HEAD_DIM
    ncl = G // GROUPS_PER_CLUSTER
    x2 = x.reshape(T, D)
    tgt2 = tgt.reshape(T, D)

    disc = lambda *p: _s5_discretize(*p)
    (ab_r, ab_i, bb_r, bb_i), disc_vjp = jax.vjp(disc, S["lam_re"], S["lam_im"], S["log_dt"], S["b_re"], S["b_im"])
    wb = jnp.concatenate([_blockdiag(jnp.transpose(bb_r, (0, 2, 1))), _blockdiag(jnp.transpose(bb_i, (0, 2, 1)))],
                         axis=-1).astype(MXU_DTYPE)
    wc = jnp.concatenate([_blockdiag(jnp.transpose(S["c_re"], (0, 2, 1))), _blockdiag(-jnp.transpose(S["c_im"], (0, 2, 1)))],
                         axis=1).astype(MXU_DTYPE)
    cs = GROUPS_PER_CLUSTER * Pst
    slab = lambda ab: jnp.tile(jnp.transpose(ab.reshape(ncl, cs // LANES, LANES), (1, 0, 2)), (1, B, 1))
    a_r, a_i = slab(ab_r), slab(ab_i)
    d_row = S["d"].reshape(1, D)

    xi = _interleave(x2, B, L)
    y, yg, h_r, h_i = _s5_fwd(xi, wb, wc, a_r, a_i, d_row, B, name="s5_fwd")
    _anchor(W, yg)
    z = _mm_nn(yg, W["w_glu"], bias=S["b_glu"].reshape(1, D), name="glu_z")
    gate = _glu_gate(y, z, name="glu_gate")
    mix_i = _mm_nn(gate, W["w_out"], name="s5_out")
    tok = _prepare(W, "w_up", mix_i)
    mix = _deinterleave(mix_i, B, L)
    h1, h1b, xh1, rs1 = _ln_fwd(x2, mix, S["ln_gain"][0, 0][None] + tok, S["ln_bias"][0, 0][None], name="ln_fwd_0a")

    def ffn_fwd(hb, l, prepare=None):
        hc = _mm_nn(hb, W["w_up"], l=l, out_dtype=MXU_DTYPE, name=f"ffn_up_{l}")
        tok = _prepare(W, prepare, hc) if prepare else 0.0
        a = _conv_glu_fwd(hc, S["conv_w"][l], S["conv_b"][l][None] + tok, L, name=f"ffn_conv_{l}")
        f = _mm_nn(a, W["w_down"], l=l, name=f"ffn_down_{l}")
        return hc, a, f

    _anchor(W, h1b)
    hc0, a0, f0 = ffn_fwd(h1b, 0, prepare="w_kv")
    h2, h2b, xh2, rs2 = _ln_fwd(h1, f0, S["ln_gain"][0, 1][None], S["ln_bias"][0, 1][None], name="ln_fwd_0b")

    _anchor(W, h2b)
    kv = _mm_nn(h2b, W["w_kv"], name="attn_kv")
    q = _mm_nn(h2b, W["w_q"], name="attn_q")
    bias = _attn_bias(S["rel_bias"], hpg)
    o3, l3 = _attn_fwd(q, kv, bias, L, hpg, name="attn_fwd")
    o, ob, lse = _attn_merge(o3, l3, HW, name="attn_merge")
    att = _mm_nn(ob, W["w_ao"], name="attn_out")
    h3, h3b, xh3, rs3 = _ln_fwd(h2, att, S["ln_gain"][1, 0][None], S["ln_bias"][1, 0][None], name="ln_fwd_1a")
    hc1, a1, f1 = ffn_fwd(h3b, 1)
    h4, _, xh4, rs4 = _ln_fwd(h3, f1, S["ln_gain"][1, 1][None], S["ln_bias"][1, 1][None], name="ln_fwd_1b")

    dh4, lrow = _loss_grad(h4, tgt2, name="loss")
    loss = lrow[0, 0]

    GW, GS = {}, {}

    def ffn_bwd(dzb, hb, hc, a, l):
        da = _mm_nt(dzb, W["w_down"], l=l, out_dtype=MXU_DTYPE, name=f"ffn_down_bwd_x_{l}")
        GW[f"w_down{l}"] = _tn(a, dzb, ptotal=1, np_cols=D, name=f"ffn_down_bwd_w_{l}")
        dc, dcw, dcb = _conv_glu_bwd(hc, da, S["conv_w"][l], S["conv_b"][l][None], L, name=f"ffn_conv_bwd_{l}")
        dhc = _conv_bwd_input(dc, S["conv_w"][l], L, name=f"ffn_conv_bwd_x_{l}")
        dh = _mm_nt(dhc, W["w_up"], l=l, name=f"ffn_up_bwd_x_{l}")
        GW[f"w_up{l}"] = _tn(hb, dhc, ptotal=W["w_up"].shape[0], np_cols=W["w_up"].shape[3], name=f"ffn_up_bwd_w_{l}")
        return dh, dcw, dcb

    dz4, dz4b, dg4, db4 = _ln_bwd([dh4], [1.0], xh4, rs4, S["ln_gain"][1, 1][None], name="ln_bwd_1b")
    dh3f, dcw1, dcb1 = ffn_bwd(dz4b, h3b, hc1, a1, 1)
    dz3, dz3b, dg3, db3 = _ln_bwd([dz4, dh3f], [DN_ALPHA, 1.0], xh3, rs3, S["ln_gain"][1, 0][None], name="ln_bwd_1a")
    do = _mm_nt(dz3b, W["w_ao"], name="attn_out_bwd_x")
    GW["w_ao"] = _tn(ob, dz3b, ptotal=1, np_cols=D, name="attn_out_bwd_w")
    dq, dk, dv, ds_sum = _attn_bwd(q, kv, do, o, lse, bias, L, hpg, name="attn_bwd")
    GS["rel_bias"] = _bias_grad(ds_sum, hpg, name="attn_bias_grad")
    GW["w_q"] = _tn(h2b, dq, ptotal=W["w_q"].shape[0], np_cols=W["w_q"].shape[3], name="attn_q_bwd_w")
    pkv, npkv = W["w_kv"].shape[0], W["w_kv"].shape[3]
    gkv = _tn(h2b, dk, ptotal=pkv, np_cols=npkv, p0=0, name="attn_k_bwd_w")
    GW["w_kv"] = _tn(h2b, dv, ptotal=pkv, np_cols=npkv, p0=pkv // 2, prev=gkv, name="attn_v_bwd_w")
    dh2q = _mm_nt(dq, W["w_q"], name="attn_q_bwd_x")
    dh2k = _mm_nt(dk, W["w_kv"], p0=0, pn=pkv // 2, name="attn_k_bwd_x")
    dh2v = _mm_nt(dv, W["w_kv"], p0=pkv // 2, pn=pkv // 2, name="attn_v_bwd_x")

    gain_0b = S["ln_gain"][0, 1][None]
    if on_grads is not None:
        gain_0b = gain_0b + on_grads(0, GW)[0, 0]

    dz2, dz2b, dg2, db2 = _ln_bwd([dz3, dh2q, dh2k, dh2v], [DN_ALPHA, 1.0, 1.0, 1.0], xh2, rs2, gain_0b,
                                  name="ln_bwd_0b")
    dh1f, dcw0, dcb0 = ffn_bwd(dz2b, h1b, hc0, a0, 0)
    gain_0a = S["ln_gain"][0, 0][None]
    if on_grads is not None:
        gain_0a = gain_0a + on_grads(1, GW)[0, 0]
    dz1, dz1b, dg1, db1 = _ln_bwd([dz2, dh1f], [DN_ALPHA, 1.0], xh1, rs1, gain_0a, name="ln_bwd_0a")
    dmix_i = _interleave(dz1b, B, L)
    dgate = _mm_nt(dmix_i, W["w_out"], name="s5_out_bwd_x")
    GW["w_out"] = _tn(gate, dmix_i, ptotal=1, np_cols=D, name="s5_out_bwd_w")
    dzg, dyg1, dbglu = _glu_bwd(y, z, dgate, name="glu_bwd")
    dyg2 = _mm_nt(dzg, W["w_glu"], name="glu_z_bwd_x")
    GW["w_glu"] = _tn(yg, dzg, ptotal=1, np_cols=D, name="glu_z_bwd_w")
    dy = _gelu_bwd(y, dyg1, dyg2, name="gelu_bwd")
    if on_grads is not None:
        d_row = d_row + on_grads(2, GW)[0, 0]
    du_i, g_r, g_i, dar, dai, dd = _s5_bwd(dy, xi, h_r, h_i, wb, wc, a_r, a_i, d_row, B, name="s5_bwd")
    dwb_r = _cluster_tn(xi, g_r, ncl, tok_left=True, name="s5_b_grad_re")
    dwb_i = _cluster_tn(xi, g_i, ncl, tok_left=True, name="s5_b_grad_im")
    dwc_r = _cluster_tn(dy, h_r, ncl, tok_left=False, name="s5_c_grad_re")
    dwc_i = _cluster_tn(dy, h_i, ncl, tok_left=False, name="s5_c_grad_im")
    grad_x = _axpy(dz1, _deinterleave(du_i, B, L), DN_ALPHA, name="grad_x")

    dbb_r = jnp.transpose(_unblockdiag(dwb_r, SSM_GROUP, Pst), (0, 2, 1))
    dbb_i = jnp.transpose(_unblockdiag(dwb_i, SSM_GROUP, Pst), (0, 2, 1))
    unslab = lambda da: jnp.transpose(da.reshape(cs // LANES, B, ncl, LANES).sum(1), (1, 0, 2)).reshape(G, Pst)
    dab_r, dab_i = unslab(dar), unslab(dai)
    GS["lam_re"], GS["lam_im"], GS["log_dt"], GS["b_re"], GS["b_im"] = disc_vjp((dab_r, dab_i, dbb_r, dbb_i))
    GS["c_re"] = jnp.transpose(_unblockdiag(dwc_r, Pst, SSM_GROUP), (0, 2, 1))
    GS["c_im"] = -jnp.transpose(_unblockdiag(dwc_i, Pst, SSM_GROUP), (0, 2, 1))
    GS["d"] = dd.reshape(G, SSM_GROUP)
    GS["b_glu"] = dbglu.reshape(D)
    GS["conv_w"] = jnp.stack([dcw0, dcw1])
    GS["conv_b"] = jnp.stack([dcb0[0], dcb1[0]])
    GS["ln_gain"] = jnp.stack([jnp.stack([dg1[0], dg2[0]]), jnp.stack([dg3[0], dg4[0]])])
    GS["ln_bias"] = jnp.stack([jnp.stack([db1[0], db2[0]]), jnp.stack([db3[0], db4[0]])])
    return loss, grad_x.reshape(B, L, D), GW, GS


SMALL_REPLICATED = ("lam_re", "lam_im", "log_dt", "b_re", "b_im", "c_re", "c_im", "d", "rel_bias", "conv_b")
SMALL_SHARDED = ("b_glu", "conv_w", "ln_gain", "ln_bias")
SMALL_ORDER = SMALL_REPLICATED + SMALL_SHARDED


def _pack(arrs, lanes, row_mult):
    flat = jnp.concatenate([a.reshape(-1).astype(F32) for a in arrs])
    rows = -(-flat.shape[0] // lanes)
    rows = -(-rows // row_mult) * row_mult
    return jnp.pad(flat, (0, rows * lanes - flat.shape[0])).reshape(rows, lanes)


def _unpack(packed, shapes):
    flat = packed.reshape(-1)
    out, off = [], 0
    for s in shapes:
        n = int(np.prod(s))
        out.append(flat[off:off + n].reshape(s))
        off += n
    return out


def kernel(x, s5_lam_re, s5_lam_im, s5_log_dt, s5_b_re, s5_b_im, s5_c_re, s5_c_im, s5_d, s5_w_glu, s5_b_glu, s5_w_out, attn_w_kv, attn_w_q, attn_w_out, rel_bias, ffn_w_up, ffn_conv_w, ffn_conv_b, ffn_w_down, ln_gain, ln_bias, loss_target, m_s5_lam_re, m_s5_lam_im, m_s5_log_dt, m_s5_b_re, m_s5_b_im, m_s5_c_re, m_s5_c_im, m_s5_d, m_s5_w_glu, m_s5_b_glu, m_s5_w_out, m_attn_w_kv, m_attn_w_q, m_attn_w_out, m_rel_bias, m_ffn_w_up, m_ffn_conv_w, m_ffn_conv_b, m_ffn_w_down, m_ln_gain, m_ln_bias, v_s5_lam_re, v_s5_lam_im, v_s5_log_dt, v_s5_b_re, v_s5_b_im, v_s5_c_re, v_s5_c_im, v_s5_d, v_s5_w_glu, v_s5_b_glu, v_s5_w_out, v_attn_w_kv, v_attn_w_q, v_attn_w_out, v_rel_bias, v_ffn_w_up, v_ffn_conv_w, v_ffn_conv_b, v_ffn_w_down, v_ln_gain, v_ln_bias):
    names = ["s5_lam_re", "s5_lam_im", "s5_log_dt", "s5_b_re", "s5_b_im", "s5_c_re", "s5_c_im", "s5_d", "s5_w_glu",
             "s5_b_glu", "s5_w_out", "attn_w_kv", "attn_w_q", "attn_w_out", "rel_bias", "ffn_w_up", "ffn_conv_w",
             "ffn_conv_b", "ffn_w_down", "ln_gain", "ln_bias"]
    loc = locals()
    w_in = {n: loc[n] for n in names}
    m_in = {n: loc["m_" + n] for n in names}
    v_in = {n: loc["v_" + n] for n in names}
    chip = 2 * lax.axis_index("x") + lax.axis_index("y")
    core = lax.axis_index("c")
    chip_idx = jnp.reshape(chip, (1,)).astype(jnp.int32)
    cidx = jnp.reshape(core, (1,)).astype(jnp.int32)

    big = [("w_glu", "s5_w_glu", "rows"), ("w_out", "s5_w_out", "rows"), ("w_ao", "attn_w_out", "rows"),
           ("w_kv", "attn_w_kv", "cols"), ("w_q", "attn_w_q", "cols"),
           ("w_up", "ffn_w_up", "layer_cols"), ("w_down", "ffn_w_down", "layer_rows")]

    def halves(t, kind):
        if kind.startswith("layer"):
            return t
        r, c = t.shape[-2:]
        return t.reshape(2, r // 2, c)

    def to_weight(g, kind):
        _, _, r, c = g.shape
        if kind == "rows":
            return g.reshape(1, 1, 8 * r, c)
        if kind == "cols":
            return g.reshape(4, 1, 2 * r, c)
        if kind == "layer_cols":
            return g
        return jnp.transpose(g, (1, 0, 2, 3)).reshape(1, 2, 4 * r, c)

    small_sh = {"b_glu": s5_b_glu[0], "conv_w": ffn_conv_w, "ln_gain": ln_gain, "ln_bias": ln_bias}
    sh_shapes = [small_sh[k].shape for k in SMALL_SHARDED]
    sh_pack = _pack([small_sh[k] for k in SMALL_SHARDED], 128, 16)

    shards = [halves(w_in[src].astype(MXU_DTYPE), kind) for _, src, kind in big]
    shards.append(sh_pack.reshape(2, sh_pack.shape[0] // 2, 128))
    shard_of = {key: s for (key, _, _), s in zip(big, shards)}
    shard_of["small"] = shards[-1]
    kind_of = {key: kind for key, _, kind in big}

    group_keys = [["w_glu", "w_out", "small"], ["w_up", "w_down"], ["w_kv", "w_q", "w_ao"]]
    started, token = _gather_start([[shard_of[k] for k in g] for g in group_keys], name="weights_gather_start")

    forwarding = {}

    def prepare_group(gi, after):
        ssem, rsem, thru, lands = started[gi]
        lands = _gather_wait(ssem, rsem, thru, lands, after, name=f"weights_gather_wait_{gi}")
        forwarding[gi], tok = _gather_forward_start(lands, name=f"weights_gather_forward_start_{gi}")
        return tok

    def finish_group(gi, after):
        if gi in forwarding:
            lands = _gather_forward_wait(forwarding.pop(gi), after, name=f"weights_gather_forward_wait_{gi}")
        else:
            ssem, rsem, thru, lands = started[gi]
            lands = _gather_wait(ssem, rsem, thru, lands, after, name=f"weights_gather_wait_{gi}")
            lands = _gather_forward(lands, name=f"weights_gather_forward_{gi}")
        out = {}
        for key, land in zip(group_keys[gi], lands):
            full = lax.dynamic_update_slice(land, shard_of[key][None], (chip, 0, 0, 0))
            if key == "small":
                parts = [_unpack(full[p], sh_shapes) for p in range(4)]
                for i, k in enumerate(SMALL_SHARDED):
                    out[k] = jnp.concatenate([parts[p][i] for p in range(4)], axis=-1)
            else:
                out[key] = to_weight(full, kind_of[key])
        return out

    replicated = dict(lam_re=s5_lam_re[0], lam_im=s5_lam_im[0], log_dt=s5_log_dt[0], b_re=s5_b_re[0], b_im=s5_b_im[0],
                      c_re=s5_c_re[0], c_im=s5_c_im[0], rel_bias=rel_bias, conv_b=ffn_conv_b,
                      d=s5_d[0] + token[0, 0])
    group_of = {k: gi for gi, g in enumerate(group_keys) for k in g if k != "small"}
    group_of.update({k: 0 for k in SMALL_SHARDED})
    group_of.update({k: "replicated" for k in replicated})
    params = _Lazy(group_of, lambda g, after: replicated if g == "replicated" else finish_group(g, after), prepare_group)

    red = [("w_up1", "ffn_w_up", 1), ("w_down1", "ffn_w_down", 1), ("w_ao", "attn_w_out", 0), ("w_kv", "attn_w_kv", 0),
           ("w_q", "attn_w_q", 0), ("w_down0", "ffn_w_down", 0), ("w_up0", "ffn_w_up", 0), ("w_out", "s5_w_out", 0),
           ("w_glu", "s5_w_glu", 0)]
    stages = [red[:5], red[5:7], red[7:]]

    def grad_halves(gw, key, src):
        r, c = w_in[src].shape[-2:]
        return gw[key].reshape(4, 2, r // 2, c)

    sent, early = {}, []

    def on_grads(stage, gw):
        tokens = []
        if stage > 0:
            tag = "abc"[stage - 1]
            ga, theirs = _pair_send_wait(sent.pop(stage - 1), gw[stages[stage][-1][0]], name=f"rs_pair_send_wait_{tag}")
            hf, hx = _pair_sums(ga, [True] * len(ga), cidx, tag, theirs)
            started, tok = _chip_exchange_start(hx, name=f"rs_chip_exchange_start_{tag}")
            early.append((hf, started, tag))
            tokens.append(tok)
        ga = [grad_halves(gw, key, src) for key, src, _ in stages[stage]]
        sent[stage], tok = _pair_send_start(ga, name=f"rs_pair_send_start_{'abc'[stage]}")
        return sum(tokens, tok)

    loss, grad_x, GW, GS = _local_step(x, loss_target, params, params, on_grads)

    gs_shapes = [GS[k].shape for k in SMALL_ORDER] + [(1,)]
    gs_pack = _pack([GS[k] for k in SMALL_ORDER] + [loss.reshape(1)], 128, 64)
    rs = gs_pack.shape[0] // 8
    gs_halves = [gs_pack.reshape(4, 2, rs, 128)]
    gl, theirs_l = _pair_send_wait(sent.pop(2), grad_x, name="rs_pair_send_wait_c")
    theirs_l += _pair_send(gs_halves, name="rs_pair_send_small")
    gl += gs_halves
    hf_l, hx_l = _pair_sums(gl, [True] * (len(gl) - 1) + [False], cidx, "c", theirs_l)
    started_l, after = _chip_exchange_start(hx_l, name="rs_chip_exchange_start_c")
    mine = []
    for hf, started, tag in early:
        got = _chip_exchange_wait(started, after, name=f"rs_chip_exchange_wait_{tag}")
        mine += _chip_sums(hf, got, chip_idx, tag)
        after = mine[-1]
    mine += _chip_sums(hf_l, _chip_exchange_wait(started_l, after, name="rs_chip_exchange_wait_c"), chip_idx, "c")
    other = _pair_swap(mine, name="rs_pair_swap")
    small_halves = jnp.where(core == 0, jnp.concatenate([mine[-1], other[-1]]), jnp.concatenate([other[-1], mine[-1]]))
    small_all = _gather([small_halves], chip, name="small_grads_all_gather")[0]
    totals = _unpack(small_all, gs_shapes)
    gsmall = dict(zip(SMALL_ORDER, totals))
    loss = totals[-1][0]

    big_res = {}
    for (key, src, layer), gm, go in zip(red, mine[:-1], other[:-1]):
        nl = w_in[src].shape[0] if src in ("ffn_w_up", "ffn_w_down") else 1
        r, c = w_in[src].shape[-2:]
        view = lambda t: t.reshape(nl, 2, r // 2, c)
        res4 = _adamw_halves(view(w_in[src]), view(m_in[src]), view(v_in[src]), gm, go, cidx, layer=layer,
                             prev=big_res.get(src), name=f"adamw_{key}")
        big_res[src] = res4
    big_res = {src: tuple(t.reshape(w_in[src].shape) for t in res4) for src, res4 in big_res.items()}

    def big_out(i):
        return {src: big_res[src][i] for _, src, _ in big}

    small_w = {"lam_re": s5_lam_re, "lam_im": s5_lam_im, "log_dt": s5_log_dt, "b_re": s5_b_re, "b_im": s5_b_im,
               "c_re": s5_c_re, "c_im": s5_c_im, "d": s5_d, "rel_bias": rel_bias, "conv_b": ffn_conv_b,
               "b_glu": s5_b_glu, "conv_w": ffn_conv_w, "ln_gain": ln_gain, "ln_bias": ln_bias}
    small_name = {"lam_re": "s5_lam_re", "lam_im": "s5_lam_im", "log_dt": "s5_log_dt", "b_re": "s5_b_re", "b_im": "s5_b_im",
                  "c_re": "s5_c_re", "c_im": "s5_c_im", "d": "s5_d", "rel_bias": "rel_bias", "conv_b": "ffn_conv_b",
                  "b_glu": "s5_b_glu", "conv_w": "ffn_conv_w", "ln_gain": "ln_gain", "ln_bias": "ln_bias"}
    sg = {}
    for k in SMALL_ORDER:
        shp = small_w[k].shape
        g = gsmall[k]
        if k in SMALL_SHARDED:
            width = shp[-1]
            g = lax.dynamic_slice_in_dim(g, chip * width, width, axis=g.ndim - 1)
        sg[k] = g.reshape(shp)
    sd, snm, snv = {}, {}, {}
    for k in SMALL_ORDER:
        shp = small_w[k].shape
        flat = lambda t: t.reshape(-1, shp[-1])
        r3 = _adamw(flat(small_w[k]), flat(sg[k]), flat(m_in[small_name[k]]), flat(v_in[small_name[k]]),
                    name=f"adamw_{k}")
        sd[k], snm[k], snv[k] = (t.reshape(shp) for t in r3)

    res = [{}, {}, {}, {}]
    for i in range(4):
        res[i].update(big_out(i))
    for k in SMALL_ORDER:
        res[0][small_name[k]] = sg[k]
        res[1][small_name[k]] = sd[k]
        res[2][small_name[k]] = snm[k]
        res[3][small_name[k]] = snv[k]
    outs = [loss, grad_x]
    for i in range(4):
        outs += [res[i][n] for n in names]
    return tuple(outs)
```

```python
import functools
import math

import numpy as np
import jax
import jax.numpy as jnp
from jax import lax
from jax.experimental import pallas as pl
from jax.experimental.pallas import tpu as pltpu

F32 = jnp.float32
BF16 = jnp.bfloat16
MXU_DTYPE = jnp.bfloat16
V7X_VMEM_LIMIT_BYTES = 52 << 20
MESH = pl.DeviceIdType.MESH

DEPTH = 2
SSM_GROUP = 16
SSM_STATE = 64
GROUPS_PER_CLUSTER = 16
CLUSTER_W = GROUPS_PER_CLUSTER * SSM_GROUP
HEAD_DIM = 64
DILATIONS = (1, 4, 16)
BAND = 128
ATTN_BATCH = (4, 8)
NEG_BIG = -1e30
REL_BUCKETS = 32
REL_MAX_DIST = 2048
DN_ALPHA = (2.0 * DEPTH) ** 0.25
LN_EPS = 1e-5
ADAM_LR, ADAM_B1, ADAM_B2, ADAM_EPS, ADAM_WD, ADAM_STEP = 0.001, 0.9, 0.999, 1e-08, 0.01, 10
GELU_K = math.sqrt(2.0 / math.pi)
GELU_C = 0.044715


def _pallas(body, **kw):
    return pl.pallas_call(body, **kw)


def _params(sem=None):
    return pltpu.CompilerParams(dimension_semantics=sem, vmem_limit_bytes=V7X_VMEM_LIMIT_BYTES)


def _pick(n, cands):
    for c in cands:
        if n % c == 0:
            return c
    return n


def _sigmoid(z):
    return 1.0 / (1.0 + jnp.exp(-z))


def _gelu(y):
    return 0.5 * y * (1.0 + jnp.tanh(GELU_K * (y + GELU_C * y * y * y)))


def _gelu_grad(y):
    t = jnp.tanh(GELU_K * (y + GELU_C * y * y * y))
    return 0.5 * (1.0 + t) + 0.5 * y * (1.0 - t * t) * (GELU_K * (1.0 + 3.0 * GELU_C * y * y))


def _mm_nn(a, w, *, l=0, bias=None, out_dtype=F32, name):
    T, K = a.shape
    P, _, _, Np = w.shape
    tm = _pick(T, (1024, 512, 256, 128))
    tn = _pick(Np, (1408, 1024, 768, 512, 384, 256, 128))
    nj = Np // tn

    def body(*refs):
        if bias is None:
            a_ref, w_ref, o_ref = refs
        else:
            a_ref, w_ref, b_ref, o_ref = refs
        acc = jnp.dot(a_ref[...].astype(MXU_DTYPE), w_ref[...].astype(MXU_DTYPE), preferred_element_type=F32)
        if bias is not None:
            acc = acc + b_ref[...]
        o_ref[...] = acc.astype(o_ref.dtype)

    in_specs = [pl.BlockSpec((tm, K), lambda p, j, i: (i, 0)),
                pl.BlockSpec((None, None, K, tn), lambda p, j, i: (p, l, 0, j))]
    args = [a, w]
    if bias is not None:
        in_specs.append(pl.BlockSpec((1, tn), lambda p, j, i: (0, p * nj + j)))
        args.append(bias)
    return _pallas(
        body, name=name, grid=(P, nj, T // tm), in_specs=in_specs,
        out_specs=pl.BlockSpec((tm, tn), lambda p, j, i: (i, p * nj + j)),
        out_shape=jax.ShapeDtypeStruct((T, P * Np), out_dtype),
        compiler_params=_params(("parallel", "parallel", "parallel")),
    )(*args)


def _mm_nt(a, w, *, l=0, p0=0, pn=None, out_dtype=F32, name):
    T = a.shape[0]
    _, _, K, Np = w.shape
    pn = w.shape[0] if pn is None else pn
    tm = _pick(T, (1024, 512, 256, 128) if K <= 1024 else (512, 256, 128))
    tn = _pick(Np, (1536, 1408, 1024, 768, 512, 384, 256, 128))
    nj = Np // tn
    nred = pn * nj

    def body(a_ref, w_ref, o_ref, acc):
        r = pl.program_id(1)

        @pl.when(r == 0)
        def _():
            acc[...] = jnp.zeros_like(acc)

        acc[...] += lax.dot_general(a_ref[...].astype(MXU_DTYPE), w_ref[...].astype(MXU_DTYPE),
                                    (((1,), (1,)), ((), ())), preferred_element_type=F32)

        @pl.when(r == nred - 1)
        def _():
            o_ref[...] = acc[...].astype(o_ref.dtype)

    return _pallas(
        body, name=name, grid=(T // tm, nred),
        in_specs=[pl.BlockSpec((tm, tn), lambda i, r: (i, r)),
                  pl.BlockSpec((None, None, K, tn), lambda i, r: (p0 + r // nj, l, 0, r % nj))],
        out_specs=pl.BlockSpec((tm, K), lambda i, r: (i, 0)),
        out_shape=jax.ShapeDtypeStruct((T, K), out_dtype),
        scratch_shapes=[pltpu.VMEM((tm, K), F32)],
        compiler_params=_params(("parallel", "arbitrary")),
    )(a, w)


def _tn(a, b, *, ptotal, np_cols, nl=1, l=0, p0=0, prev=None, name):
    T, K = a.shape
    Np = np_cols
    pn = b.shape[1] // Np
    tt = _pick(T, (1024, 512, 256, 128))
    tk = _pick(K, (1408, 1024, 512, 256, 128))
    tn = _pick(Np, (1408, 768, 512, 256, 128))
    if tk * tn > 1408 * 1024:
        tn = _pick(Np, (512, 256, 128))
    nj = Np // tn
    nt = T // tt

    def body(*refs):
        a_ref, b_ref = refs[0], refs[1]
        o_ref, acc = refs[-2], refs[-1]
        t = pl.program_id(3)

        @pl.when(t == 0)
        def _():
            acc[...] = jnp.zeros_like(acc)

        acc[...] += lax.dot_general(a_ref[...].astype(MXU_DTYPE), b_ref[...].astype(MXU_DTYPE),
                                    (((0,), (0,)), ((), ())), preferred_element_type=F32)

        @pl.when(t == nt - 1)
        def _():
            o_ref[...] = acc[...]

    in_specs = [pl.BlockSpec((tt, tk), lambda kb, p, j, t: (t, kb)),
                pl.BlockSpec((tt, tn), lambda kb, p, j, t: (t, p * nj + j))]
    args = [a, b]
    aliases = {}
    if prev is not None:
        in_specs.append(pl.BlockSpec(memory_space=pl.ANY))
        args.append(prev)
        aliases = {2: 0}
    return _pallas(
        body, name=name, grid=(K // tk, pn, nj, nt), in_specs=in_specs,
        out_specs=pl.BlockSpec((None, None, tk, tn), lambda kb, p, j, t: (p0 + p, l, kb, j)),
        out_shape=jax.ShapeDtypeStruct((ptotal, nl, K, Np), F32),
        scratch_shapes=[pltpu.VMEM((tk, tn), F32)],
        input_output_aliases=aliases,
        compiler_params=_params(("parallel", "parallel", "parallel", "arbitrary")),
    )(*args)


def _rows(tm, f):
    return pl.BlockSpec((tm, f), lambda i: (i, 0))


def _whole(shape):
    nd = len(shape)
    return pl.BlockSpec(shape, lambda i: (0,) * nd)


def _ln_fwd(xres, f, gain, bias, *, name):
    T, D = xres.shape
    tm = _pick(T, (256, 128))

    def body(x_ref, f_ref, g_ref, b_ref, y_ref, yb_ref, xh_ref, rs_ref):
        z = DN_ALPHA * x_ref[...] + f_ref[...]
        mu = jnp.mean(z, axis=-1, keepdims=True)
        zc = z - mu
        var = jnp.mean(zc * zc, axis=-1, keepdims=True)
        rstd = lax.rsqrt(var + LN_EPS)
        xh = zc * rstd
        y = xh * g_ref[...] + b_ref[...]
        y_ref[...] = y
        yb_ref[...] = y.astype(yb_ref.dtype)
        xh_ref[...] = xh
        rs_ref[...] = rstd

    return _pallas(
        body, name=name, grid=(T // tm,),
        in_specs=[_rows(tm, D), _rows(tm, D), _whole((1, D)), _whole((1, D))],
        out_specs=[_rows(tm, D), _rows(tm, D), _rows(tm, D), _rows(tm, 1)],
        out_shape=[jax.ShapeDtypeStruct((T, D), F32), jax.ShapeDtypeStruct((T, D), MXU_DTYPE),
                   jax.ShapeDtypeStruct((T, D), F32), jax.ShapeDtypeStruct((T, 1), F32)],
        compiler_params=_params(("parallel",)),
    )(xres, f, gain, bias)


def _ln_bwd(addends, coefs, xhat, rstd, gain, *, name):
    T, D = xhat.shape
    tm = _pick(T, (256, 128))
    n = len(addends)

    def body(*refs):
        adds = refs[:n]
        xh_ref, rs_ref, g_ref, dz_ref, dzb_ref, dg_ref, db_ref = refs[n:]
        dy = coefs[0] * adds[0][...]
        for c, r in zip(coefs[1:], adds[1:]):
            dy = dy + c * r[...]
        xh = xh_ref[...]
        dxh = dy * g_ref[...]
        m1 = jnp.mean(dxh, axis=-1, keepdims=True)
        m2 = jnp.mean(dxh * xh, axis=-1, keepdims=True)
        dz = rs_ref[...] * (dxh - m1 - xh * m2)
        dz_ref[...] = dz
        dzb_ref[...] = dz.astype(dzb_ref.dtype)

        @pl.when(pl.program_id(0) == 0)
        def _():
            dg_ref[...] = jnp.zeros_like(dg_ref)
            db_ref[...] = jnp.zeros_like(db_ref)

        dg_ref[...] += jnp.sum(dy * xh, axis=0, keepdims=True)
        db_ref[...] += jnp.sum(dy, axis=0, keepdims=True)

    return _pallas(
        body, name=name, grid=(T // tm,),
        in_specs=[_rows(tm, D)] * n + [_rows(tm, D), _rows(tm, 1), _whole((1, D))],
        out_specs=[_rows(tm, D), _rows(tm, D), _whole((1, D)), _whole((1, D))],
        out_shape=[jax.ShapeDtypeStruct((T, D), F32), jax.ShapeDtypeStruct((T, D), MXU_DTYPE),
                   jax.ShapeDtypeStruct((1, D), F32), jax.ShapeDtypeStruct((1, D), F32)],
        compiler_params=_params(("arbitrary",)),
    )(*addends, xhat, rstd, gain)


def _loss_grad(y, tgt, *, name):
    T, D = y.shape
    tm = _pick(T, (256, 128))

    def body(y_ref, t_ref, dy_ref, l_ref):
        e = y_ref[...] - t_ref[...]
        dy_ref[...] = e * (1.0 / D)

        @pl.when(pl.program_id(0) == 0)
        def _():
            l_ref[...] = jnp.zeros_like(l_ref)

        l_ref[...] += jnp.zeros_like(l_ref) + jnp.sum(e * e) * (0.5 / D)

    return _pallas(
        body, name=name, grid=(T // tm,),
        in_specs=[_rows(tm, D), _rows(tm, D)],
        out_specs=[_rows(tm, D), _whole((1, 128))],
        out_shape=[jax.ShapeDtypeStruct((T, D), F32), jax.ShapeDtypeStruct((1, 128), F32)],
        compiler_params=_params(("arbitrary",)),
    )(y, tgt)


def _axpy(a, b, ca, *, name):
    T, D = a.shape
    tm = _pick(T, (256, 128))

    def body(a_ref, b_ref, o_ref):
        o_ref[...] = ca * a_ref[...] + b_ref[...]

    return _pallas(
        body, name=name, grid=(T // tm,), in_specs=[_rows(tm, D), _rows(tm, D)], out_specs=_rows(tm, D),
        out_shape=jax.ShapeDtypeStruct((T, D), F32), compiler_params=_params(("parallel",)),
    )(a, b)


def _glu_gate(y, z, *, name):
    T, D = y.shape
    tm = _pick(T, (256, 128))

    def body(y_ref, z_ref, g_ref):
        g_ref[...] = (_gelu(y_ref[...]) * _sigmoid(z_ref[...])).astype(g_ref.dtype)

    return _pallas(
        body, name=name, grid=(T // tm,), in_specs=[_rows(tm, D), _rows(tm, D)], out_specs=_rows(tm, D),
        out_shape=jax.ShapeDtypeStruct((T, D), MXU_DTYPE), compiler_params=_params(("parallel",)),
    )(y, z)


def _glu_bwd(y, z, dg, *, name):
    T, D = y.shape
    tm = _pick(T, (256, 128))

    def body(y_ref, z_ref, dg_ref, dzb_ref, dyg_ref, db_ref):
        s = _sigmoid(z_ref[...])
        dg = dg_ref[...]
        dz = dg * _gelu(y_ref[...]) * s * (1.0 - s)
        dzb_ref[...] = dz.astype(dzb_ref.dtype)
        dyg_ref[...] = dg * s

        @pl.when(pl.program_id(0) == 0)
        def _():
            db_ref[...] = jnp.zeros_like(db_ref)

        db_ref[...] += jnp.sum(dz, axis=0, keepdims=True)

    return _pallas(
        body, name=name, grid=(T // tm,), in_specs=[_rows(tm, D)] * 3,
        out_specs=[_rows(tm, D), _rows(tm, D), _whole((1, D))],
        out_shape=[jax.ShapeDtypeStruct((T, D), MXU_DTYPE), jax.ShapeDtypeStruct((T, D), F32),
                   jax.ShapeDtypeStruct((1, D), F32)],
        compiler_params=_params(("arbitrary",)),
    )(y, z, dg)


def _gelu_bwd(y, d1, d2, *, name):
    T, D = y.shape
    tm = _pick(T, (256, 128))

    def body(y_ref, a_ref, b_ref, o_ref):
        o_ref[...] = (a_ref[...] + b_ref[...]) * _gelu_grad(y_ref[...])

    return _pallas(
        body, name=name, grid=(T // tm,), in_specs=[_rows(tm, D)] * 3, out_specs=_rows(tm, D),
        out_shape=jax.ShapeDtypeStruct((T, D), F32), compiler_params=_params(("parallel",)),
    )(y, d1, d2)


CONV_ROWS = 128
CONV_EDGE = 16


def _row_shifts(x, edge, drop_edge, tm, back):
    keep = jnp.where(drop_edge, 0.0, 1.0).astype(edge.dtype)
    ext = jnp.concatenate([edge * keep, x] if back else [x, edge * keep], axis=0)
    row = lax.broadcasted_iota(jnp.int32, (tm, tm + CONV_EDGE), 0)
    col = lax.broadcasted_iota(jnp.int32, (tm, tm + CONV_EDGE), 1)
    base = row + CONV_EDGE if back else row
    out = []
    for k in (1, 2):
        pick = (col == (base - k if back else base + k)).astype(x.dtype)
        out.append(jnp.dot(pick, ext, preferred_element_type=F32))
    return out


def _conv_specs(T, F2, tm):
    return [_rows(tm, F2),
            pl.BlockSpec((CONV_EDGE, F2), lambda i: (jnp.maximum(i * (tm // CONV_EDGE) - 1, 0), 0))]


def _conv_glu_fwd(hc, conv_w, conv_b, L, *, name):
    T, F2 = hc.shape
    F = F2 // 2
    tm = CONV_ROWS

    def body(x_ref, e_ref, w_ref, b_ref, a_ref):
        at_start = (pl.program_id(0) * tm) % L == 0
        x1, x2 = _row_shifts(x_ref[...], e_ref[...], at_start, tm, True)
        x = x_ref[...].astype(F32)
        c = b_ref[...] + w_ref[0:1, :] * x + w_ref[1:2, :] * x1 + w_ref[2:3, :] * x2
        val, gate = c[:, :F], c[:, F:]
        a_ref[...] = (gate * _sigmoid(gate) * val).astype(a_ref.dtype)

    return _pallas(
        body, name=name, grid=(T // tm,),
        in_specs=_conv_specs(T, F2, tm) + [_whole((3, F2)), _whole((1, F2))],
        out_specs=_rows(tm, F),
        out_shape=jax.ShapeDtypeStruct((T, F), MXU_DTYPE), compiler_params=_params(("parallel",)),
    )(hc, hc, conv_w, conv_b)


def _conv_glu_bwd(hc, da, conv_w, conv_b, L, *, name):
    T, F2 = hc.shape
    F = F2 // 2
    tm = CONV_ROWS

    def body(x_ref, e_ref, da_ref, w_ref, b_ref, dc_ref, dw_ref, db_ref):
        at_start = (pl.program_id(0) * tm) % L == 0
        x1, x2 = _row_shifts(x_ref[...], e_ref[...], at_start, tm, True)
        x = x_ref[...].astype(F32)
        c = b_ref[...] + w_ref[0:1, :] * x + w_ref[1:2, :] * x1 + w_ref[2:3, :] * x2
        val, gate = c[:, :F], c[:, F:]
        s = _sigmoid(gate)
        da = da_ref[...].astype(F32)
        dval = da * (gate * s)
        dgate = da * val * (s * (1.0 + gate * (1.0 - s)))
        dc = jnp.concatenate([dval, dgate], axis=-1)
        dc_ref[...] = dc.astype(dc_ref.dtype)

        @pl.when(pl.program_id(0) == 0)
        def _():
            dw_ref[...] = jnp.zeros_like(dw_ref)
            db_ref[...] = jnp.zeros_like(db_ref)

        dw_ref[0:1, :] += jnp.sum(dc * x, axis=0, keepdims=True)
        dw_ref[1:2, :] += jnp.sum(dc * x1, axis=0, keepdims=True)
        dw_ref[2:3, :] += jnp.sum(dc * x2, axis=0, keepdims=True)
        db_ref[...] += jnp.sum(dc, axis=0, keepdims=True)

    return _pallas(
        body, name=name, grid=(T // tm,),
        in_specs=_conv_specs(T, F2, tm) + [_rows(tm, F), _whole((3, F2)), _whole((1, F2))],
        out_specs=[_rows(tm, F2), _whole((3, F2)), _whole((1, F2))],
        out_shape=[jax.ShapeDtypeStruct((T, F2), MXU_DTYPE), jax.ShapeDtypeStruct((3, F2), F32),
                   jax.ShapeDtypeStruct((1, F2), F32)],
        compiler_params=_params(("arbitrary",)),
    )(hc, hc, da, conv_w, conv_b)


def _conv_bwd_input(dc, conv_w, L, *, name):
    T, F2 = dc.shape
    tm = CONV_ROWS
    edge = CONV_EDGE
    last_blk = T // edge - 1

    def body(x_ref, e_ref, w_ref, o_ref):
        at_end = ((pl.program_id(0) + 1) * tm) % L == 0
        x1, x2 = _row_shifts(x_ref[...], e_ref[...], at_end, tm, False)
        x = x_ref[...].astype(F32)
        o_ref[...] = (w_ref[0:1, :] * x + w_ref[1:2, :] * x1 + w_ref[2:3, :] * x2).astype(o_ref.dtype)

    return _pallas(
        body, name=name, grid=(T // tm,),
        in_specs=[_rows(tm, F2),
                  pl.BlockSpec((edge, F2), lambda i: (jnp.minimum((i + 1) * (tm // edge), last_blk), 0)),
                  _whole((3, F2))],
        out_specs=_rows(tm, F2),
        out_shape=jax.ShapeDtypeStruct((T, F2), MXU_DTYPE), compiler_params=_params(("parallel",)),
    )(dc, dc, conv_w)


S5_CHUNK = 128
LANES = 128


def _slab_rows(c, n, ncl):
    return pl.ds(c, n) if ncl == 1 else pl.ds(c, n, stride=ncl)


def _slab_put(ref, c, n, ncl, val):
    for s in range(val.shape[1] // LANES):
        ref[s, _slab_rows(c, n, ncl), :] = val[:, s * LANES:(s + 1) * LANES]


def _slab_get(ref, c, n, ncl):
    return jnp.concatenate([ref[s, _slab_rows(c, n, ncl), :] for s in range(ref.shape[0])], axis=-1)


def _slabs(n_slab, rows):
    return pl.BlockSpec((n_slab, rows, LANES), lambda i: (0, i, 0))


def _s5_fwd(xi, wb, wc, a_r, a_i, d_row, B, *, name):
    T, D = xi.shape
    ncl = wb.shape[0]
    cs = wb.shape[2] // 2
    ns = cs // LANES
    R = B * ncl
    Q = S5_CHUNK
    QR = Q * ncl
    nsteps = Q // B

    def body(x_ref, wb_ref, wc_ref, ar_ref, ai_ref, d_ref, y_ref, yg_ref, hr_ref, hi_ref, bur, bui, cr, ci):
        @pl.when(pl.program_id(0) == 0)
        def _():
            cr[...] = jnp.zeros_like(cr)
            ci[...] = jnp.zeros_like(ci)

        x = x_ref[...]
        xb = x.astype(MXU_DTYPE)
        for c in range(ncl):
            bu = jnp.dot(xb[:, c * CLUSTER_W:(c + 1) * CLUSTER_W], wb_ref[c], preferred_element_type=F32)
            _slab_put(bur, c, Q, ncl, bu[:, :cs])
            _slab_put(bui, c, Q, ncl, bu[:, cs:])
        ar = ar_ref[...]
        ai = ai_ref[...]

        def step(k, carry):
            hr, hi = carry
            sl = pl.ds(pl.multiple_of(k * R, R), R)
            nr = ar * hr - ai * hi + bur[:, sl, :]
            ni = ar * hi + ai * hr + bui[:, sl, :]
            hr_ref[:, sl, :] = nr
            hi_ref[:, sl, :] = ni
            return nr, ni

        hr, hi = lax.fori_loop(0, nsteps, step, (cr[...], ci[...]), unroll=4)
        cr[...] = hr
        ci[...] = hi
        parts = []
        for c in range(ncl):
            hrc = _slab_get(hr_ref, c, Q, ncl).astype(MXU_DTYPE)
            hic = _slab_get(hi_ref, c, Q, ncl).astype(MXU_DTYPE)
            parts.append(jnp.dot(hrc, wc_ref[c, :cs, :], preferred_element_type=F32)
                         + jnp.dot(hic, wc_ref[c, cs:, :], preferred_element_type=F32))
        y = d_ref[...] * x + (parts[0] if ncl == 1 else jnp.concatenate(parts, axis=-1))
        y_ref[...] = y
        yg_ref[...] = _gelu(y).astype(yg_ref.dtype)

    return _pallas(
        body, name=name, grid=(T // Q,),
        in_specs=[_rows(Q, D), _whole(wb.shape), _whole(wc.shape), _whole((ns, R, LANES)), _whole((ns, R, LANES)),
                  _whole((1, D))],
        out_specs=[_rows(Q, D), _rows(Q, D), _slabs(ns, QR), _slabs(ns, QR)],
        out_shape=[jax.ShapeDtypeStruct((T, D), F32), jax.ShapeDtypeStruct((T, D), MXU_DTYPE),
                   jax.ShapeDtypeStruct((ns, T * ncl, LANES), F32), jax.ShapeDtypeStruct((ns, T * ncl, LANES), F32)],
        scratch_shapes=[pltpu.VMEM((ns, QR, LANES), F32), pltpu.VMEM((ns, QR, LANES), F32),
                        pltpu.VMEM((ns, R, LANES), F32), pltpu.VMEM((ns, R, LANES), F32)],
        compiler_params=_params(("arbitrary",)),
    )(xi, wb, wc, a_r, a_i, d_row)


def _s5_bwd(dy, xi, h_r, h_i, wb, wc, a_r, a_i, d_row, B, *, name):
    T, D = dy.shape
    ncl = wb.shape[0]
    cs = wb.shape[2] // 2
    ns = cs // LANES
    R = B * ncl
    Q = S5_CHUNK
    nsteps = Q // B
    nchunk = T // Q
    QR = Q * ncl

    def rev(i):
        return nchunk - 1 - i

    def body(dy_ref, x_ref, hr_ref, hi_ref, pr_ref, pi_ref, wb_ref, wc_ref, ar_ref, ai_ref, d_ref,
             du_ref, gr_ref, gi_ref, dar_ref, dai_ref, dd_ref, dhr, dhi, cr, ci):
        i = pl.program_id(0)

        @pl.when(i == 0)
        def _():
            cr[...] = jnp.zeros_like(cr)
            ci[...] = jnp.zeros_like(ci)
            dar_ref[...] = jnp.zeros_like(dar_ref)
            dai_ref[...] = jnp.zeros_like(dai_ref)
            dd_ref[...] = jnp.zeros_like(dd_ref)

        dyv = dy_ref[...]
        dyb = dyv.astype(MXU_DTYPE)
        for c in range(ncl):
            dh = lax.dot_general(dyb[:, c * CLUSTER_W:(c + 1) * CLUSTER_W], wc_ref[c],
                                 (((1,), (1,)), ((), ())), preferred_element_type=F32)
            _slab_put(dhr, c, Q, ncl, dh[:, :cs])
            _slab_put(dhi, c, Q, ncl, dh[:, cs:])
        ar = ar_ref[...]
        ai = ai_ref[...]

        def step(j, carry):
            gr, gi = carry
            k = nsteps - 1 - j
            sl = pl.ds(pl.multiple_of(k * R, R), R)
            ngr = dhr[:, sl, :] + ar * gr + ai * gi
            ngi = dhi[:, sl, :] - ai * gr + ar * gi
            gr_ref[:, sl, :] = ngr
            gi_ref[:, sl, :] = ngi
            return ngr, ngi

        gr, gi = lax.fori_loop(0, nsteps, step, (cr[...], ci[...]), unroll=4)
        cr[...] = gr
        ci[...] = gi
        keep = jnp.where(i == nchunk - 1, 0.0, 1.0)
        hpr = jnp.concatenate([pr_ref[:, 8 - R:8, :] * keep, hr_ref[:, 0:QR - R, :]], axis=1)
        hpi = jnp.concatenate([pi_ref[:, 8 - R:8, :] * keep, hi_ref[:, 0:QR - R, :]], axis=1)
        gra, gia = gr_ref[...], gi_ref[...]
        steps = lambda t: jnp.sum(t.reshape(ns, nsteps, R, LANES), axis=1)
        dar_ref[...] += steps(gra * hpr + gia * hpi)
        dai_ref[...] += steps(gia * hpr - gra * hpi)
        parts = []
        for c in range(ncl):
            grc = _slab_get(gr_ref, c, Q, ncl).astype(MXU_DTYPE)
            gic = _slab_get(gi_ref, c, Q, ncl).astype(MXU_DTYPE)
            parts.append(lax.dot_general(grc, wb_ref[c, :, :cs], (((1,), (1,)), ((), ())), preferred_element_type=F32)
                         + lax.dot_general(gic, wb_ref[c, :, cs:], (((1,), (1,)), ((), ())), preferred_element_type=F32))
        du_ref[...] = d_ref[...] * dyv + (parts[0] if ncl == 1 else jnp.concatenate(parts, axis=-1))
        dd_ref[...] += jnp.sum(dyv * x_ref[...], axis=0, keepdims=True)

    tok = pl.BlockSpec((Q, D), lambda i: (rev(i), 0))
    st = pl.BlockSpec((ns, QR, LANES), lambda i: (0, rev(i), 0))
    before = pl.BlockSpec((ns, 8, LANES), lambda i: (0, jnp.maximum(rev(i) * (QR // 8) - 1, 0), 0))
    acc = _whole((ns, R, LANES))
    return _pallas(
        body, name=name, grid=(nchunk,),
        in_specs=[tok, tok, st, st, before, before, _whole(wb.shape), _whole(wc.shape), acc, acc, _whole((1, D))],
        out_specs=[tok, st, st, acc, acc, _whole((1, D))],
        out_shape=[jax.ShapeDtypeStruct((T, D), F32),
                   jax.ShapeDtypeStruct((ns, T * ncl, LANES), F32), jax.ShapeDtypeStruct((ns, T * ncl, LANES), F32),
                   jax.ShapeDtypeStruct((ns, R, LANES), F32), jax.ShapeDtypeStruct((ns, R, LANES), F32),
                   jax.ShapeDtypeStruct((1, D), F32)],
        scratch_shapes=[pltpu.VMEM((ns, QR, LANES), F32)] * 2 + [pltpu.VMEM((ns, R, LANES), F32)] * 2,
        compiler_params=_params(("arbitrary",)),
    )(dy, xi, h_r, h_i, h_r, h_i, wb, wc, a_r, a_i, d_row)


def _cluster_tn(tok, st, ncl, *, tok_left, name):
    T = tok.shape[0]
    ns = st.shape[0]
    cs = ns * LANES
    tt = _pick(T, (512, 256, 128))
    nt = T // tt
    oshape = (ncl, CLUSTER_W, cs) if tok_left else (ncl, cs, CLUSTER_W)

    def body(tok_ref, st_ref, o_ref, acc):
        t = pl.program_id(0)

        @pl.when(t == 0)
        def _():
            acc[...] = jnp.zeros_like(acc)

        tk = tok_ref[...].astype(MXU_DTYPE)
        for c in range(ncl):
            tc = tk[:, c * CLUSTER_W:(c + 1) * CLUSTER_W]
            sc = _slab_get(st_ref, c, tt, ncl).astype(MXU_DTYPE)
            lhs, rhs = (tc, sc) if tok_left else (sc, tc)
            acc[c] += lax.dot_general(lhs, rhs, (((0,), (0,)), ((), ())), preferred_element_type=F32)

        @pl.when(t == nt - 1)
        def _():
            o_ref[...] = acc[...]

    return _pallas(
        body, name=name, grid=(nt,),
        in_specs=[_rows(tt, tok.shape[1]), _slabs(ns, tt * ncl)],
        out_specs=_whole(oshape),
        out_shape=jax.ShapeDtypeStruct(oshape, F32),
        scratch_shapes=[pltpu.VMEM(oshape, F32)],
        compiler_params=_params(("arbitrary",)),
    )(tok, st)


def _s5_discretize(lam_re, lam_im, log_dt, b_re, b_im):
    dt = jnp.exp(log_dt)[:, None]
    mag = jnp.exp(lam_re * dt)
    ab_r, ab_i = mag * jnp.cos(lam_im * dt), mag * jnp.sin(lam_im * dt)
    den = lam_re * lam_re + lam_im * lam_im
    nr = ab_r - 1.0
    co_r = (nr * lam_re + ab_i * lam_im) / den
    co_i = (ab_i * lam_re - nr * lam_im) / den
    bb_r = co_r[..., None] * b_re - co_i[..., None] * b_im
    bb_i = co_r[..., None] * b_im + co_i[..., None] * b_re
    return ab_r, ab_i, bb_r, bb_i


def _blockdiag(m):
    G, r, k = m.shape
    ncl = G // GROUPS_PER_CLUSTER
    m4 = m.reshape(ncl, GROUPS_PER_CLUSTER, r, k)
    eye = jnp.eye(GROUPS_PER_CLUSTER, dtype=m.dtype)
    return jnp.einsum('cgrk,gh->cgrhk', m4, eye).reshape(ncl, GROUPS_PER_CLUSTER * r, GROUPS_PER_CLUSTER * k)


def _unblockdiag(m, r, k):
    ncl = m.shape[0]
    m5 = m.reshape(ncl, GROUPS_PER_CLUSTER, r, GROUPS_PER_CLUSTER, k)
    eye = jnp.eye(GROUPS_PER_CLUSTER, dtype=m.dtype)
    return jnp.einsum('cgrhk,gh->cgrk', m5, eye).reshape(ncl * GROUPS_PER_CLUSTER, r, k)


def _t5_bucket(dist):
    exact = REL_BUCKETS // 2
    d = np.maximum(dist, 1).astype(np.float32)
    large = exact + (np.log(d / exact) / math.log(REL_MAX_DIST / exact) * (REL_BUCKETS - exact)).astype(np.int64)
    large = np.minimum(large, REL_BUCKETS - 1)
    return np.where(dist < exact, dist, large).astype(np.int32)


def _band_tables(dil):
    steps = np.arange(BAND)[:, None] + BAND - np.arange(2 * BAND)[None, :]
    bucket = _t5_bucket(np.maximum(steps, 0) * dil)
    in_band = (steps >= 0) & (steps <= BAND)
    return bucket, in_band


def _attn_bias(rel_bias, hpg):
    out = []
    for g, dil in enumerate(DILATIONS):
        bucket, in_band = _band_tables(dil)
        cols = rel_bias[:, g * hpg:(g + 1) * hpg].astype(F32)
        onehot = jnp.asarray((bucket.reshape(-1, 1) == np.arange(REL_BUCKETS)[None, :]).astype(np.float32))
        bias = jnp.dot(onehot, cols, precision=lax.Precision.HIGHEST).T.reshape(hpg, BAND, 2 * BAND)
        out.append(jnp.where(jnp.asarray(in_band)[None], bias, NEG_BIG))
    return jnp.concatenate(out, axis=0)


def _attn_blocks(dil, L):
    M = L // dil
    return M, M // BAND


def _row_sel(r, M, dil):
    return pl.ds(r, M) if dil == 1 else pl.ds(r, M, stride=dil)


def _attn_fwd(q, kv, bias, L, hpg, *, name):
    T = q.shape[0]
    nb_ = T // L
    HP = hpg // 2
    W3 = 3 * hpg * HEAD_DIM
    mmax = L

    def group_body(dil, q_ref, k_ref, v_ref, b_ref, o_ref, l_ref, os, ls):
        M, NB = _attn_blocks(dil, L)
        first = lax.broadcasted_iota(jnp.int32, (1, 2 * HEAD_DIM), 1) < HEAD_DIM
        loaded = {}

        def load(r):
            rows = _row_sel(r, M, dil)
            qf = q_ref[rows, :] * 0.125
            qm = [jnp.where(first, qf, 0.0).astype(MXU_DTYPE), jnp.where(first, 0.0, qf).astype(MXU_DTYPE)]
            kr = k_ref[rows, :].astype(MXU_DTYPE)
            va = jnp.concatenate([v_ref[rows, :].astype(MXU_DTYPE), jnp.ones((M, 2 * HEAD_DIM), MXU_DTYPE)], axis=-1)
            return qm, kr, va

        tasks = [(r, n) for r in range(dil) for n in range(NB)]
        nbatch = ATTN_BATCH[NB == 1]
        for t0 in range(0, len(tasks), nbatch):
            batch = tasks[t0:t0 + nbatch]
            for r, _ in batch:
                if r not in loaded:
                    loaded[r] = load(r)
            chains = [(r, n, hh) for r, n in batch for hh in range(2)]
            ks = lambda n: slice(0, BAND) if n == 0 else slice((n - 1) * BAND, (n + 1) * BAND)
            s = [lax.dot_general(loaded[r][0][hh][n * BAND:(n + 1) * BAND, :], loaded[r][1][ks(n), :],
                                 (((1,), (1,)), ((), ())), preferred_element_type=F32)
                 + (b_ref[hh, :, BAND:] if n == 0 else b_ref[hh]) for r, n, hh in chains]
            m = [jnp.max(t, axis=-1, keepdims=True) for t in s]
            p = [jnp.exp(t - mm) for t, mm in zip(s, m)]
            pv = [jnp.dot(t.astype(MXU_DTYPE), loaded[r][2][ks(n), :], preferred_element_type=F32)
                  for t, (r, n, hh) in zip(p, chains)]
            l = [t[:, 2 * HEAD_DIM:] for t in pv]
            o_h = [t[:, :2 * HEAD_DIM] / ll for t, ll in zip(pv, l)]
            l_h = [mm + jnp.log(ll) for mm, ll in zip(m, l)]
            for i, (r, n) in enumerate(batch):
                os[r * M + n * BAND:r * M + (n + 1) * BAND, :] = jnp.where(first, o_h[2 * i], o_h[2 * i + 1])
                ls[r * M + n * BAND:r * M + (n + 1) * BAND, :] = jnp.where(first, l_h[2 * i], l_h[2 * i + 1])
                if n == NB - 1:
                    rows = _row_sel(r, M, dil)
                    o_ref[rows, :] = os[r * M:(r + 1) * M, :]
                    l_ref[rows, :] = ls[r * M:(r + 1) * M, :]

    def body(q_ref, k_ref, v_ref, b_ref, o_ref, l_ref, os, ls):
        g = pl.program_id(0)
        for gi, dil in enumerate(DILATIONS):
            pl.when(g == gi)(functools.partial(group_body, dil, q_ref, k_ref, v_ref, b_ref, o_ref, l_ref, os, ls))

    blk = (L, 2 * HEAD_DIM)
    return _pallas(
        body, name=name, grid=(3, nb_, HP),
        in_specs=[pl.BlockSpec(blk, lambda g, b, h: (b, g * HP + h)),
                  pl.BlockSpec(blk, lambda g, b, h: (b, g * HP + h)),
                  pl.BlockSpec(blk, lambda g, b, h: (b, 3 * HP + g * HP + h)),
                  pl.BlockSpec((2, BAND, 2 * BAND), lambda g, b, h: (g * HP + h, 0, 0))],
        out_specs=[pl.BlockSpec(blk, lambda g, b, h: (b, g * HP + h)),
                   pl.BlockSpec(blk, lambda g, b, h: (b, g * HP + h))],
        out_shape=[jax.ShapeDtypeStruct((T, W3), F32), jax.ShapeDtypeStruct((T, W3), F32)],
        scratch_shapes=[pltpu.VMEM((mmax, 2 * HEAD_DIM), F32), pltpu.VMEM((mmax, 2 * HEAD_DIM), F32)],
        compiler_params=_params(("arbitrary", "arbitrary", "arbitrary")),
    )(q, kv, kv, bias)


def _attn_merge(o3, l3, hw, *, name):
    T = o3.shape[0]
    tm = _pick(T, (256, 128))

    def body(o0, o1, o2, l0, l1, l2, o_ref, ob_ref, lse_ref):
        a0, a1, a2 = l0[...], l1[...], l2[...]
        m = jnp.maximum(jnp.maximum(a0, a1), a2)
        e0, e1, e2 = jnp.exp(a0 - m), jnp.exp(a1 - m), jnp.exp(a2 - m)
        z = e0 + e1 + e2
        o = (e0 * o0[...] + e1 * o1[...] + e2 * o2[...]) / z
        o_ref[...] = o
        ob_ref[...] = o.astype(ob_ref.dtype)
        lse_ref[...] = m + jnp.log(z)

    def col(g):
        return pl.BlockSpec((tm, hw), lambda i: (i, g))

    return _pallas(
        body, name=name, grid=(T // tm,),
        in_specs=[col(0), col(1), col(2), col(0), col(1), col(2)],
        out_specs=[_rows(tm, hw)] * 3,
        out_shape=[jax.ShapeDtypeStruct((T, hw), F32), jax.ShapeDtypeStruct((T, hw), MXU_DTYPE),
                   jax.ShapeDtypeStruct((T, hw), F32)],
        compiler_params=_params(("parallel",)),
    )(o3, o3, o3, l3, l3, l3)


def _attn_bwd(q, kv, do, o, lse, bias, L, hpg, *, name):
    T = q.shape[0]
    nb_ = T // L
    HP = hpg // 2
    W3 = 3 * hpg * HEAD_DIM
    mmax = L

    def group_body(dil, q_ref, k_ref, v_ref, do_ref, o_ref, l_ref, b_ref, dq_ref, dk_ref, dv_ref, ds_ref,
                   dqs, dks, dvs):
        M, NB = _attn_blocks(dil, L)
        first = lax.broadcasted_iota(jnp.int32, (1, 2 * HEAD_DIM), 1) < HEAD_DIM
        loaded = {}

        def load(r):
            rows = _row_sel(r, M, dil)
            qf = q_ref[rows, :] * 0.125
            qm = [jnp.where(first, qf, 0.0).astype(MXU_DTYPE), jnp.where(first, 0.0, qf).astype(MXU_DTYPE)]
            kr = k_ref[rows, :].astype(MXU_DTYPE)
            vr = v_ref[rows, :].astype(MXU_DTYPE)
            dof = do_ref[rows, :]
            dom = [jnp.where(first, dof, 0.0).astype(MXU_DTYPE), jnp.where(first, 0.0, dof).astype(MXU_DTYPE)]
            dod = dof * o_ref[rows, :]
            delta = [jnp.sum(jnp.where(first, dod, 0.0), axis=-1, keepdims=True),
                     jnp.sum(jnp.where(first, 0.0, dod), axis=-1, keepdims=True)]
            lr = l_ref[rows, :]
            lse = [lr[:, 0:1], lr[:, HEAD_DIM:HEAD_DIM + 1]]
            if NB > 1:
                dks[r * M:(r + 1) * M, :] = jnp.zeros((M, 2 * HEAD_DIM), F32)
                dvs[r * M:(r + 1) * M, :] = jnp.zeros((M, 2 * HEAD_DIM), F32)
            return qm, kr, vr, dom, delta, lse

        nt = (((1,), (1,)), ((), ()))
        tn = (((0,), (0,)), ((), ()))
        tasks = [(r, n) for r in range(dil) for n in range(NB)]
        nbatch = ATTN_BATCH[NB == 1]
        for t0 in range(0, len(tasks), nbatch):
            batch = tasks[t0:t0 + nbatch]
            for r, _ in batch:
                if r not in loaded:
                    loaded[r] = load(r)
            chains = [(r, n, hh) for r, n in batch for hh in range(2)]
            qs = lambda n: slice(n * BAND, (n + 1) * BAND)
            ks = lambda n: slice(0, BAND) if n == 0 else slice((n - 1) * BAND, (n + 1) * BAND)
            qb = [loaded[r][0][hh][qs(n), :] for r, n, hh in chains]
            dob = [loaded[r][3][hh][qs(n), :] for r, n, hh in chains]
            kb = [loaded[r][1][ks(n), :] for r, n, hh in chains]
            s = [lax.dot_general(a, b, nt, preferred_element_type=F32) + (b_ref[hh, :, BAND:] if n == 0 else b_ref[hh])
                 for a, b, (r, n, hh) in zip(qb, kb, chains)]
            dp = [lax.dot_general(a, loaded[r][2][ks(n), :], nt, preferred_element_type=F32)
                  for a, (r, n, hh) in zip(dob, chains)]
            p = [jnp.exp(t - loaded[r][5][hh][qs(n), :]) for t, (r, n, hh) in zip(s, chains)]
            ds = [a * (b - loaded[r][4][hh][qs(n), :]) for a, b, (r, n, hh) in zip(p, dp, chains)]
            for t, (r, n, hh) in zip(ds, chains):
                if n == 0:
                    ds_ref[hh, :, BAND:] += t
                else:
                    ds_ref[hh] += t
            dsm = [t.astype(MXU_DTYPE) for t in ds]
            dq = [jnp.dot(a, b, preferred_element_type=F32) for a, b in zip(dsm, kb)]
            dk = [lax.dot_general(a, b, tn, preferred_element_type=F32) for a, b in zip(dsm, qb)]
            dv = [lax.dot_general(a.astype(MXU_DTYPE), b, tn, preferred_element_type=F32) for a, b in zip(p, dob)]
            for i, (r, n) in enumerate(batch):
                dqs[r * M + n * BAND:r * M + (n + 1) * BAND, :] = jnp.where(first, dq[2 * i], dq[2 * i + 1]) * 0.125
                ksm = slice(r * M + ks(n).start, r * M + ks(n).stop)
                if NB > 1:
                    dks[ksm, :] += dk[2 * i] + dk[2 * i + 1]
                    dvs[ksm, :] += dv[2 * i] + dv[2 * i + 1]
                else:
                    dks[ksm, :] = dk[2 * i] + dk[2 * i + 1]
                    dvs[ksm, :] = dv[2 * i] + dv[2 * i + 1]
                if n == NB - 1:
                    rows = _row_sel(r, M, dil)
                    dq_ref[rows, :] = dqs[r * M:(r + 1) * M, :]
                    dk_ref[rows, :] = dks[r * M:(r + 1) * M, :]
                    dv_ref[rows, :] = dvs[r * M:(r + 1) * M, :]

    def body(q_ref, k_ref, v_ref, do_ref, o_ref, l_ref, b_ref, dq_ref, dk_ref, dv_ref, ds_ref, dqs, dks, dvs):
        g = pl.program_id(0)

        @pl.when(pl.program_id(2) == 0)
        def _():
            ds_ref[...] = jnp.zeros_like(ds_ref)

        for gi, dil in enumerate(DILATIONS):
            pl.when(g == gi)(functools.partial(group_body, dil, q_ref, k_ref, v_ref, do_ref, o_ref, l_ref, b_ref,
                                               dq_ref, dk_ref, dv_ref, ds_ref, dqs, dks, dvs))

    blk = (L, 2 * HEAD_DIM)
    gcol = lambda g, h, b: (b, g * HP + h)
    hcol = lambda g, h, b: (b, h)
    return _pallas(
        body, name=name, grid=(3, HP, nb_),
        in_specs=[pl.BlockSpec(blk, gcol), pl.BlockSpec(blk, gcol),
                  pl.BlockSpec(blk, lambda g, h, b: (b, 3 * HP + g * HP + h)),
                  pl.BlockSpec(blk, hcol), pl.BlockSpec(blk, hcol), pl.BlockSpec(blk, hcol),
                  pl.BlockSpec((2, BAND, 2 * BAND), lambda g, h, b: (g * HP + h, 0, 0))],
        out_specs=[pl.BlockSpec(blk, gcol), pl.BlockSpec(blk, gcol), pl.BlockSpec(blk, gcol),
                   pl.BlockSpec((2, BAND, 2 * BAND), lambda g, h, b: (g * HP + h, 0, 0))],
        out_shape=[jax.ShapeDtypeStruct((T, W3), F32), jax.ShapeDtypeStruct((T, W3), F32),
                   jax.ShapeDtypeStruct((T, W3), F32), jax.ShapeDtypeStruct((3 * hpg, BAND, 2 * BAND), F32)],
        scratch_shapes=[pltpu.VMEM((mmax, 2 * HEAD_DIM), F32)] * 3,
        compiler_params=_params(("arbitrary", "arbitrary", "arbitrary")),
    )(q, kv, kv, do, o, lse, bias)


def _bias_grad(ds_sum, hpg, *, name):
    nh = ds_sum.shape[0]
    idx = np.stack([np.where(_band_tables(dil)[1], _band_tables(dil)[0], -1) for dil in DILATIONS]).astype(np.int32)

    def body(ds_ref, idx_ref, o_ref):
        d = ds_ref[...]
        ix = idx_ref[...]
        lane = lax.broadcasted_iota(jnp.int32, (8, 128), 1)
        row = jnp.zeros((8, 128), F32)
        for b in range(REL_BUCKETS):
            row = row + jnp.where(lane == b, jnp.sum(jnp.where(ix == b, d, 0.0)), 0.0)
        o_ref[...] = row

    out = _pallas(
        body, name=name, grid=(nh,),
        in_specs=[pl.BlockSpec((None, BAND, 2 * BAND), lambda h: (h, 0, 0)),
                  pl.BlockSpec((None, BAND, 2 * BAND), lambda h: (h // hpg, 0, 0))],
        out_specs=pl.BlockSpec((None, 8, 128), lambda h: (h, 0, 0)),
        out_shape=jax.ShapeDtypeStruct((nh, 8, 128), F32),
        compiler_params=_params(("parallel",)),
    )(ds_sum, jnp.asarray(idx))
    return out[:, 0, :REL_BUCKETS].T


def _adamw(w, g, m, v, *, name):
    Rw, C = w.shape
    tm = _pick(Rw, (512, 352, 256, 128, 64, 32, 16, 8))

    def body(w_ref, g_ref, m_ref, v_ref, d_ref, nm_ref, nv_ref):
        gg = g_ref[...]
        nm = ADAM_B1 * m_ref[...] + (1.0 - ADAM_B1) * gg
        nv = ADAM_B2 * v_ref[...] + (1.0 - ADAM_B2) * (gg * gg)
        m_hat = nm / (1.0 - ADAM_B1 ** ADAM_STEP)
        v_hat = nv / (1.0 - ADAM_B2 ** ADAM_STEP)
        d_ref[...] = -ADAM_LR * (m_hat / (jnp.sqrt(v_hat) + ADAM_EPS) + ADAM_WD * w_ref[...])
        nm_ref[...] = nm
        nv_ref[...] = nv

    return _pallas(
        body, name=name, grid=(Rw // tm,), in_specs=[_rows(tm, C)] * 4, out_specs=[_rows(tm, C)] * 3,
        out_shape=[jax.ShapeDtypeStruct((Rw, C), F32)] * 3, compiler_params=_params(("parallel",)),
    )(w, g, m, v)


ROW_TILE_ELEMS = 256 * 1024


def _tile_rows(r, c):
    best = 8
    for t in range(8, r + 1, 8):
        if r % t == 0 and t * c <= ROW_TILE_ELEMS:
            best = t
    return best


def _adamw_halves(w, m, v, mine, other, cidx, *, layer=0, prev=None, name):
    NL, _, r, c = w.shape
    tm = _tile_rows(r, c)

    def body(c_ref, w_ref, m_ref, v_ref, a_ref, b_ref, *rest):
        g_ref, d_ref, nm_ref, nv_ref = rest[-4:]
        gg = jnp.where(pl.program_id(0) == c_ref[0], a_ref[...], b_ref[...])
        nm = ADAM_B1 * m_ref[...] + (1.0 - ADAM_B1) * gg
        nv = ADAM_B2 * v_ref[...] + (1.0 - ADAM_B2) * (gg * gg)
        m_hat = nm / (1.0 - ADAM_B1 ** ADAM_STEP)
        v_hat = nv / (1.0 - ADAM_B2 ** ADAM_STEP)
        g_ref[...] = gg
        d_ref[...] = -ADAM_LR * (m_hat / (jnp.sqrt(v_hat) + ADAM_EPS) + ADAM_WD * w_ref[...])
        nm_ref[...] = nm
        nv_ref[...] = nv

    half = pl.BlockSpec((None, None, tm, c), lambda h, i, cr: (layer, h, i, 0))
    one = pl.BlockSpec((None, tm, c), lambda h, i, cr: (0, i, 0))
    in_specs = [half, half, half, one, one]
    args = [cidx, w, m, v, mine, other]
    aliases = {}
    if prev is not None:
        in_specs += [_ANY] * 4
        args += list(prev)
        aliases = {6 + k: k for k in range(4)}
    spec = pltpu.PrefetchScalarGridSpec(num_scalar_prefetch=1, grid=(2, r // tm), in_specs=in_specs, out_specs=[half] * 4)
    return _pallas(
        body, name=name, grid_spec=spec, out_shape=[jax.ShapeDtypeStruct((NL, 2, r, c), F32)] * 4,
        input_output_aliases=aliases, compiler_params=_params(("parallel", "parallel")),
    )(*args)


def _pair_sum(g, theirs, cidx, *, cast, name):
    _, _, r, c = g.shape
    tm = _tile_rows(r, c)

    def body(c_ref, g_ref, t_ref, *outs):
        s = g_ref[...] + t_ref[...]
        outs[0][...] = s
        if cast:
            outs[1][...] = s.astype(BF16)

    blk = (None, None, tm, c)
    first = pl.BlockSpec(blk, lambda p, i, cr: (p, 0, i, 0))
    shapes = [jax.ShapeDtypeStruct((4, 1, r, c), F32)] + ([jax.ShapeDtypeStruct((4, 1, r, c), BF16)] if cast else [])
    spec = pltpu.PrefetchScalarGridSpec(
        num_scalar_prefetch=1, grid=(4, r // tm),
        in_specs=[pl.BlockSpec(blk, lambda p, i, cr: (p, cr[0], i, 0)), first], out_specs=[first] * len(shapes))
    return _pallas(body, name=name, grid_spec=spec, out_shape=shapes,
                   compiler_params=_params(("parallel", "parallel")))(cidx, g, theirs)


def _chip_sum(hf, got, chip_idx, *, name):
    _, _, r, c = hf.shape
    tm = _tile_rows(r, c)

    def body(p_ref, h_ref, r_ref, o_ref):
        s = h_ref[...]
        for k in range(3):
            s = s + r_ref[k].astype(F32)
        o_ref[...] = s

    spec = pltpu.PrefetchScalarGridSpec(
        num_scalar_prefetch=1, grid=(r // tm,),
        in_specs=[pl.BlockSpec((None, None, tm, c), lambda i, pr: (pr[0], 0, i, 0)),
                  pl.BlockSpec((3, None, tm, c), lambda i, pr: (0, 0, i, 0))],
        out_specs=pl.BlockSpec((None, tm, c), lambda i, pr: (0, i, 0)))
    return _pallas(body, name=name, grid_spec=spec, out_shape=jax.ShapeDtypeStruct((1, r, c), F32),
                   compiler_params=_params(("parallel",)))(chip_idx, hf, got)


def _place():
    x, y, c = lax.axis_index("x"), lax.axis_index("y"), lax.axis_index("c")
    chips = [(1 - x, y), (x, 1 - y), (1 - x, 1 - y)]
    return x, y, c, chips


_ANY = pl.BlockSpec(memory_space=pl.ANY)


def _comm_call(body, ins, out_shapes, n_remote, *, name, aliases=None):
    sems = [pltpu.SemaphoreType.DMA((n,)) for n in n_remote]
    return _pallas(
        body, name=name, in_specs=[_ANY] * len(ins), out_specs=[_ANY] * len(out_shapes), out_shape=out_shapes,
        scratch_shapes=sems, input_output_aliases=aliases or {},
        compiler_params=pltpu.CompilerParams(has_side_effects=True),
    )(*ins)


_HBM_SPEC = pl.BlockSpec(memory_space=pltpu.HBM)
_SEM_SPEC = pl.BlockSpec(memory_space=pltpu.SEMAPHORE)
_DATAFLOW = pltpu.SideEffectType.DATAFLOW_SIDE_EFFECTING


def _in_hbm(a):
    return pltpu.with_memory_space_constraint(a, pltpu.HBM)


def _gather_start(groups, *, name):
    flat = [s for g in groups for s in g]
    n, ng = len(flat), len(groups)

    def body(*refs):
        ins, lands = refs[:n], refs[n:2 * n]
        sems = refs[2 * n:2 * n + 2 * ng]
        token = refs[-1]
        x, y, c, chips = _place()
        me = 2 * x + y
        a = 0
        for gi, g in enumerate(groups):
            for j in range(len(g)):
                for k, (tx, ty) in enumerate(chips):
                    _rcopy(ins[a].at[c], lands[a].at[me, c], sems[2 * gi].at[3 * j + k], sems[2 * gi + 1].at[3 * j + k],
                           (tx, ty, c)).start()
                a += 1
        token[...] = jnp.zeros_like(token)

    land_shapes = [(4,) + s.shape for s in flat]
    out_shape = ([pltpu.SemaphoreType.DMA((3 * len(g),)) for g in groups for _ in range(2)]
                 + [pltpu.HBM(s.shape, s.dtype) for s in flat]
                 + [pltpu.HBM(ls, s.dtype) for ls, s in zip(land_shapes, flat)]
                 + [jax.ShapeDtypeStruct((8, 128), F32)])
    outs = _pallas(
        body, name=name, in_specs=[_HBM_SPEC] * (2 * n),
        out_specs=[_SEM_SPEC] * (2 * ng) + [_HBM_SPEC] * (2 * n) + [pl.BlockSpec(memory_space=pltpu.VMEM)],
        out_shape=out_shape, input_output_aliases={i: 2 * ng + i for i in range(2 * n)},
        compiler_params=pltpu.CompilerParams(has_side_effects=_DATAFLOW),
    )(*[_in_hbm(s) for s in flat], *[_in_hbm(lax.empty(ls, s.dtype)) for ls, s in zip(land_shapes, flat)])
    sems, thru, lands, token = outs[:2 * ng], outs[2 * ng:2 * ng + n], outs[2 * ng + n:2 * ng + 2 * n], outs[-1]
    res, a = [], 0
    for gi, g in enumerate(groups):
        res.append((sems[2 * gi], sems[2 * gi + 1], thru[a:a + len(g)], lands[a:a + len(g)]))
        a += len(g)
    return res, token


def _gather_wait(ssem, rsem, shards, lands, after, *, name):
    m = len(shards)

    def body(*refs):
        ins, lnd = refs[:m], refs[m:2 * m]
        ss, rs = refs[2 * m], refs[2 * m + 1]
        x, y, c, chips = _place()
        for j in range(m):
            for k, (tx, ty) in enumerate(chips):
                cp = _rcopy(ins[j].at[c], lnd[j].at[2 * tx + ty, c], ss.at[3 * j + k], rs.at[3 * j + k], (tx, ty, c))
                cp.wait_send()
                cp.wait_recv()

    outs = _pallas(
        body, name=name, in_specs=[_HBM_SPEC] * (2 * m) + [_SEM_SPEC, _SEM_SPEC, _ANY],
        out_specs=[_HBM_SPEC] * (2 * m),
        out_shape=[pltpu.HBM(s.shape, s.dtype) for s in shards] + [pltpu.HBM(l.shape, l.dtype) for l in lands],
        input_output_aliases={i: i for i in range(2 * m)},
        compiler_params=pltpu.CompilerParams(has_side_effects=_DATAFLOW),
    )(*shards, *lands, ssem, rsem, after)
    return outs[m:]


def _gather_forward(lands, *, name):
    n = len(lands)

    def body(*refs):
        outs = refs[n:2 * n]
        ssem, rsem = refs[2 * n:]
        x, y, c, chips = _place()
        sib = (x, y, 1 - c)
        cps = []
        for a in range(n):
            for k, (tx, ty) in enumerate(chips):
                pk = 2 * tx + ty
                cp = _rcopy(outs[a].at[pk, c], outs[a].at[pk, c], ssem.at[3 * a + k], rsem.at[3 * a + k], sib)
                cp.start()
                cps.append(cp)
        for a in range(n):
            for k, (tx, ty) in enumerate(chips):
                pk = 2 * tx + ty
                _rcopy(outs[a].at[pk, c], outs[a].at[pk, 1 - c], ssem.at[3 * a + k], rsem.at[3 * a + k], sib).wait_recv()
        for cp in cps:
            cp.wait_send()

    shapes = [jax.ShapeDtypeStruct(l.shape, l.dtype) for l in lands]
    return _comm_call(body, lands, shapes, [3 * n, 3 * n], name=name, aliases={i: i for i in range(n)})


def _gather_forward_start(lands, *, name):
    n = len(lands)

    def body(*refs):
        ssem, rsem = refs[n], refs[n + 1]
        outs = refs[n + 2:2 * n + 2]
        token = refs[-1]
        x, y, c, chips = _place()
        for a in range(n):
            for k, (tx, ty) in enumerate(chips):
                pk = 2 * tx + ty
                _rcopy(outs[a].at[pk, c], outs[a].at[pk, c], ssem.at[3 * a + k], rsem.at[3 * a + k], (x, y, 1 - c)).start()
        token[...] = jnp.zeros_like(token)

    outs = _pallas(
        body, name=name, in_specs=[_HBM_SPEC] * n,
        out_specs=[_SEM_SPEC] * 2 + [_HBM_SPEC] * n + [pl.BlockSpec(memory_space=pltpu.VMEM)],
        out_shape=([pltpu.SemaphoreType.DMA((3 * n,))] * 2 + [pltpu.HBM(l.shape, l.dtype) for l in lands]
                   + [jax.ShapeDtypeStruct((8, 128), F32)]),
        input_output_aliases={i: 2 + i for i in range(n)},
        compiler_params=pltpu.CompilerParams(has_side_effects=_DATAFLOW),
    )(*lands)
    return (outs[0], outs[1], outs[2:2 + n]), outs[-1]


def _gather_forward_wait(started, after, *, name):
    ssem, rsem, lands = started
    n = len(lands)

    def body(*refs):
        lnd = refs[:n]
        ss, rs = refs[n], refs[n + 1]
        x, y, c, chips = _place()
        sib = (x, y, 1 - c)
        for a in range(n):
            for k, (tx, ty) in enumerate(chips):
                pk = 2 * tx + ty
                _rcopy(lnd[a].at[pk, c], lnd[a].at[pk, 1 - c], ss.at[3 * a + k], rs.at[3 * a + k], sib).wait_recv()
                _rcopy(lnd[a].at[pk, c], lnd[a].at[pk, c], ss.at[3 * a + k], rs.at[3 * a + k], sib).wait_send()

    return _pallas(
        body, name=name, in_specs=[_HBM_SPEC] * n + [_SEM_SPEC, _SEM_SPEC, _ANY], out_specs=[_HBM_SPEC] * n,
        out_shape=[pltpu.HBM(l.shape, l.dtype) for l in lands], input_output_aliases={i: i for i in range(n)},
        compiler_params=pltpu.CompilerParams(has_side_effects=_DATAFLOW),
    )(*lands, ssem, rsem, after)


class _Lazy:
    def __init__(self, group_of, make, prepare):
        self._group_of, self._make, self._prepare, self._done, self._anchor = group_of, make, prepare, {}, None

    def anchor(self, value):
        self._anchor = value

    def prepare(self, key, value):
        return self._prepare(self._group_of[key], value)

    def __getitem__(self, key):
        g = self._group_of[key]
        if g not in self._done:
            self._done[g] = self._make(g, self._anchor)
        return self._done[g][key]


def _anchor(mapping, value):
    if isinstance(mapping, _Lazy):
        mapping.anchor(value)


def _prepare(mapping, key, value):
    return mapping.prepare(key, value)[0, 0] if isinstance(mapping, _Lazy) else 0.0


def _rcopy(src, dst, ssem, rsem, dev):
    return pltpu.make_async_remote_copy(src_ref=src, dst_ref=dst, send_sem=ssem, recv_sem=rsem,
                                        device_id=dev, device_id_type=MESH)


def _all_gather(shards, *, name):
    n = len(shards)

    def body(*refs):
        ins, outs = refs[:n], refs[n:2 * n]
        s_ici, r_ici, s_d2d, r_d2d = refs[2 * n:]
        x, y, c, chips = _place()
        me = 2 * x + y
        sib = (x, y, 1 - c)
        sends = []
        for a in range(n):
            for k, (tx, ty) in enumerate(chips):
                cp = _rcopy(ins[a].at[c], outs[a].at[me, c], s_ici.at[3 * a + k], r_ici.at[3 * a + k], (tx, ty, c))
                cp.start()
                sends.append(cp)
        for a in range(n):
            for k, (tx, ty) in enumerate(chips):
                pk = 2 * tx + ty
                _rcopy(ins[a].at[c], outs[a].at[pk, c], s_ici.at[3 * a + k], r_ici.at[3 * a + k], (tx, ty, c)).wait_recv()
                fw = _rcopy(outs[a].at[pk, c], outs[a].at[pk, c], s_d2d.at[3 * a + k], r_d2d.at[3 * a + k], sib)
                fw.start()
                sends.append(fw)
        for a in range(n):
            for k, (tx, ty) in enumerate(chips):
                pk = 2 * tx + ty
                _rcopy(ins[a].at[c], outs[a].at[pk, 1 - c], s_d2d.at[3 * a + k], r_d2d.at[3 * a + k], sib).wait_recv()
        for cp in sends:
            cp.wait_send()

    shapes = [jax.ShapeDtypeStruct((4,) + s.shape, s.dtype) for s in shards]
    return _comm_call(body, shards, shapes, [3 * n] * 4, name=name)


def _gather(shards, chip, *, name):
    outs = _all_gather(shards, name=name)
    return [lax.dynamic_update_slice(o, s[None], (chip, 0, 0, 0)) for o, s in zip(outs, shards)]


def _pair_send(gs, *, name):
    n = len(gs)

    def body(*refs):
        ins, theirs = refs[:n], refs[n:2 * n]
        ssem, rsem = refs[2 * n:]
        x, y, c, _ = _place()
        sib = (x, y, 1 - c)
        cps = []
        for a in range(n):
            cp = _rcopy(ins[a].at[:, pl.ds(1 - c, 1)], theirs[a], ssem.at[a], rsem.at[a], sib)
            cp.start()
            cps.append(cp)
        for cp in cps:
            cp.wait_send()
            cp.wait_recv()

    shapes = [jax.ShapeDtypeStruct((4, 1) + g.shape[2:], g.dtype) for g in gs]
    return _comm_call(body, gs, shapes, [n, n], name=name)


def _chip_exchange(hx, *, name):
    n = len(hx)

    def body(*refs):
        hxr, got = refs[:n], refs[n:2 * n]
        ssem, rsem = refs[2 * n:]
        x, y, c, chips = _place()
        cps = []
        for a in range(n):
            for k, (tx, ty) in enumerate(chips):
                cp = _rcopy(hxr[a].at[2 * tx + ty], got[a].at[k], ssem.at[3 * a + k], rsem.at[3 * a + k], (tx, ty, c))
                cp.start()
                cps.append(cp)
        for cp in cps:
            cp.wait_send()
            cp.wait_recv()

    shapes = [jax.ShapeDtypeStruct((3,) + h.shape[1:], h.dtype) for h in hx]
    return _comm_call(body, hx, shapes, [3 * n, 3 * n], name=name)


def _pair_swap(fs, *, name):
    n = len(fs)

    def body(*refs):
        ins, outs = refs[:n], refs[n:2 * n]
        ssem, rsem = refs[2 * n:]
        x, y, c, _ = _place()
        cps = []
        for a in range(n):
            cp = _rcopy(ins[a], outs[a], ssem.at[a], rsem.at[a], (x, y, 1 - c))
            cp.start()
            cps.append(cp)
        for cp in cps:
            cp.wait_send()
            cp.wait_recv()

    shapes = [jax.ShapeDtypeStruct(f.shape, f.dtype) for f in fs]
    return _comm_call(body, fs, shapes, [n, n], name=name)


def _chip_exchange_start(hx, *, name):
    n = len(hx)

    def body(*refs):
        ins, gots = refs[:n], refs[n:2 * n]
        ssem, rsem = refs[2 * n], refs[2 * n + 1]
        token = refs[-1]
        x, y, c, chips = _place()
        for a in range(n):
            for k, (tx, ty) in enumerate(chips):
                _rcopy(ins[a].at[2 * tx + ty], gots[a].at[k], ssem.at[3 * a + k], rsem.at[3 * a + k], (tx, ty, c)).start()
        token[...] = jnp.zeros_like(token)

    got_shapes = [(3,) + h.shape[1:] for h in hx]
    outs = _pallas(
        body, name=name, in_specs=[_HBM_SPEC] * (2 * n),
        out_specs=[_SEM_SPEC] * 2 + [_HBM_SPEC] * (2 * n) + [pl.BlockSpec(memory_space=pltpu.VMEM)],
        out_shape=([pltpu.SemaphoreType.DMA((3 * n,))] * 2 + [pltpu.HBM(h.shape, h.dtype) for h in hx]
                   + [pltpu.HBM(gs, h.dtype) for gs, h in zip(got_shapes, hx)] + [jax.ShapeDtypeStruct((8, 128), F32)]),
        input_output_aliases={i: 2 + i for i in range(2 * n)},
        compiler_params=pltpu.CompilerParams(has_side_effects=_DATAFLOW),
    )(*[_in_hbm(h) for h in hx], *[_in_hbm(lax.empty(gs, h.dtype)) for gs, h in zip(got_shapes, hx)])
    return (outs[0], outs[1], outs[2:2 + n], outs[2 + n:2 + 2 * n]), outs[-1]


def _chip_exchange_wait(started, after, *, name):
    ssem, rsem, hx, gots = started
    n = len(hx)

    def body(*refs):
        ins, gts = refs[:n], refs[n:2 * n]
        ss, rs = refs[2 * n], refs[2 * n + 1]
        x, y, c, chips = _place()
        for a in range(n):
            for k, (tx, ty) in enumerate(chips):
                cp = _rcopy(ins[a].at[2 * tx + ty], gts[a].at[k], ss.at[3 * a + k], rs.at[3 * a + k], (tx, ty, c))
                cp.wait_send()
                cp.wait_recv()

    outs = _pallas(
        body, name=name, in_specs=[_HBM_SPEC] * (2 * n) + [_SEM_SPEC, _SEM_SPEC, _ANY],
        out_specs=[_HBM_SPEC] * (2 * n),
        out_shape=[pltpu.HBM(h.shape, h.dtype) for h in hx] + [pltpu.HBM(g.shape, g.dtype) for g in gots],
        input_output_aliases={i: i for i in range(2 * n)},
        compiler_params=pltpu.CompilerParams(has_side_effects=_DATAFLOW),
    )(*hx, *gots, ssem, rsem, after)
    return outs[n:]


def _pair_send_start(gs, *, name):
    n = len(gs)

    def body(*refs):
        ins, lands = refs[:n], refs[n:2 * n]
        ssem, rsem = refs[2 * n], refs[2 * n + 1]
        token = refs[-1]
        x, y, c, _ = _place()
        for a in range(n):
            _rcopy(ins[a].at[:, pl.ds(1 - c, 1)], lands[a], ssem.at[a], rsem.at[a], (x, y, 1 - c)).start()
        token[...] = jnp.zeros_like(token)

    land_shapes = [(4, 1) + g.shape[2:] for g in gs]
    outs = _pallas(
        body, name=name, in_specs=[_HBM_SPEC] * (2 * n),
        out_specs=[_SEM_SPEC] * 2 + [_HBM_SPEC] * (2 * n) + [pl.BlockSpec(memory_space=pltpu.VMEM)],
        out_shape=([pltpu.SemaphoreType.DMA((n,))] * 2 + [pltpu.HBM(g.shape, g.dtype) for g in gs]
                   + [pltpu.HBM(ls, g.dtype) for ls, g in zip(land_shapes, gs)] + [jax.ShapeDtypeStruct((8, 128), F32)]),
        input_output_aliases={i: 2 + i for i in range(2 * n)},
        compiler_params=pltpu.CompilerParams(has_side_effects=_DATAFLOW),
    )(*[_in_hbm(g) for g in gs], *[_in_hbm(lax.empty(ls, g.dtype)) for ls, g in zip(land_shapes, gs)])
    return (outs[0], outs[1], outs[2:2 + n], outs[2 + n:2 + 2 * n]), outs[-1]


def _pair_send_wait(started, after, *, name):
    ssem, rsem, gs, lands = started
    n = len(gs)

    def body(*refs):
        ins, lnd = refs[:n], refs[n:2 * n]
        ss, rs = refs[2 * n], refs[2 * n + 1]
        x, y, c, _ = _place()
        for a in range(n):
            cp = _rcopy(ins[a].at[:, pl.ds(1 - c, 1)], lnd[a], ss.at[a], rs.at[a], (x, y, 1 - c))
            cp.wait_send()
            cp.wait_recv()

    outs = _pallas(
        body, name=name, in_specs=[_HBM_SPEC] * (2 * n) + [_SEM_SPEC, _SEM_SPEC, _ANY],
        out_specs=[_HBM_SPEC] * (2 * n),
        out_shape=[pltpu.HBM(g.shape, g.dtype) for g in gs] + [pltpu.HBM(l.shape, l.dtype) for l in lands],
        input_output_aliases={i: i for i in range(2 * n)},
        compiler_params=pltpu.CompilerParams(has_side_effects=_DATAFLOW),
    )(*gs, *lands, ssem, rsem, after)
    return list(outs[:n]), list(outs[n:])


def _pair_sums(grads, exch_bf16, cidx, tag, theirs=None):
    if theirs is None:
        theirs = _pair_send(grads, name=f"rs_pair_send_{tag}")
    hf, hx = [], []
    for a in range(len(grads)):
        res = _pair_sum(grads[a], theirs[a], cidx, cast=exch_bf16[a], name=f"rs_pair_sum_{tag}{a}")
        hf.append(res[0])
        hx.append(res[1] if exch_bf16[a] else res[0])
    return hf, hx


def _chip_sums(hf, got, chip_idx, tag):
    return [_chip_sum(hf[a], got[a], chip_idx, name=f"rs_chip_sum_{tag}{a}") for a in range(len(hf))]


def _interleave(a, B, L):
    return a.reshape(B, L, -1).transpose(1, 0, 2).reshape(B * L, -1)


def _deinterleave(a, B, L):
    return a.reshape(L, B, -1).transpose(1, 0, 2).reshape(B * L, -1)


def _local_step(x, tgt, W, S, on_grads=None):
    B, L, D = x.shape
    T = B * L
    G = D // SSM_GROUP
    Pst = SSM_STATE
    hpg = D // HEAD_DIM
    HW = hpg * HEAD_DIM
    ncl = G // GROUPS_PER_CLUSTER
    x2 = x.reshape(T, D)
    tgt2 = tgt.reshape(T, D)

    disc = lambda *p: _s5_discretize(*p)
    (ab_r, ab_i, bb_r, bb_i), disc_vjp = jax.vjp(disc, S["lam_re"], S["lam_im"], S["log_dt"], S["b_re"], S["b_im"])
    wb = jnp.concatenate([_blockdiag(jnp.transpose(bb_r, (0, 2, 1))), _blockdiag(jnp.transpose(bb_i, (0, 2, 1)))],
                         axis=-1).astype(MXU_DTYPE)
    wc = jnp.concatenate([_blockdiag(jnp.transpose(S["c_re"], (0, 2, 1))), _blockdiag(-jnp.transpose(S["c_im"], (0, 2, 1)))],
                         axis=1).astype(MXU_DTYPE)
    cs = GROUPS_PER_CLUSTER * Pst
    slab = lambda ab: jnp.tile(jnp.transpose(ab.reshape(ncl, cs // LANES, LANES), (1, 0, 2)), (1, B, 1))
    a_r, a_i = slab(ab_r), slab(ab_i)
    d_row = S["d"].reshape(1, D)

    xi = _interleave(x2, B, L)
    y, yg, h_r, h_i = _s5_fwd(xi, wb, wc, a_r, a_i, d_row, B, name="s5_fwd")
    _anchor(W, yg)
    z = _mm_nn(yg, W["w_glu"], bias=S["b_glu"].reshape(1, D), name="glu_z")
    gate = _glu_gate(y, z, name="glu_gate")
    mix_i = _mm_nn(gate, W["w_out"], name="s5_out")
    tok = _prepare(W, "w_up", mix_i)
    mix = _deinterleave(mix_i, B, L)
    h1, h1b, xh1, rs1 = _ln_fwd(x2, mix, S["ln_gain"][0, 0][None] + tok, S["ln_bias"][0, 0][None], name="ln_fwd_0a")

    def ffn_fwd(hb, l, prepare=None):
        hc = _mm_nn(hb, W["w_up"], l=l, out_dtype=MXU_DTYPE, name=f"ffn_up_{l}")
        tok = _prepare(W, prepare, hc) if prepare else 0.0
        a = _conv_glu_fwd(hc, S["conv_w"][l], S["conv_b"][l][None] + tok, L, name=f"ffn_conv_{l}")
        f = _mm_nn(a, W["w_down"], l=l, name=f"ffn_down_{l}")
        return hc, a, f

    _anchor(W, h1b)
    hc0, a0, f0 = ffn_fwd(h1b, 0, prepare="w_kv")
    h2, h2b, xh2, rs2 = _ln_fwd(h1, f0, S["ln_gain"][0, 1][None], S["ln_bias"][0, 1][None], name="ln_fwd_0b")

    _anchor(W, h2b)
    kv = _mm_nn(h2b, W["w_kv"], name="attn_kv")
    q = _mm_nn(h2b, W["w_q"], name="attn_q")
    bias = _attn_bias(S["rel_bias"], hpg)
    o3, l3 = _attn_fwd(q, kv, bias, L, hpg, name="attn_fwd")
    o, ob, lse = _attn_merge(o3, l3, HW, name="attn_merge")
    att = _mm_nn(ob, W["w_ao"], name="attn_out")
    h3, h3b, xh3, rs3 = _ln_fwd(h2, att, S["ln_gain"][1, 0][None], S["ln_bias"][1, 0][None], name="ln_fwd_1a")
    hc1, a1, f1 = ffn_fwd(h3b, 1)
    h4, _, xh4, rs4 = _ln_fwd(h3, f1, S["ln_gain"][1, 1][None], S["ln_bias"][1, 1][None], name="ln_fwd_1b")

    dh4, lrow = _loss_grad(h4, tgt2, name="loss")
    loss = lrow[0, 0]

    GW, GS = {}, {}

    def ffn_bwd(dzb, hb, hc, a, l):
        da = _mm_nt(dzb, W["w_down"], l=l, out_dtype=MXU_DTYPE, name=f"ffn_down_bwd_x_{l}")
        GW[f"w_down{l}"] = _tn(a, dzb, ptotal=1, np_cols=D, name=f"ffn_down_bwd_w_{l}")
        dc, dcw, dcb = _conv_glu_bwd(hc, da, S["conv_w"][l], S["conv_b"][l][None], L, name=f"ffn_conv_bwd_{l}")
        dhc = _conv_bwd_input(dc, S["conv_w"][l], L, name=f"ffn_conv_bwd_x_{l}")
        dh = _mm_nt(dhc, W["w_up"], l=l, name=f"ffn_up_bwd_x_{l}")
        GW[f"w_up{l}"] = _tn(hb, dhc, ptotal=W["w_up"].shape[0], np_cols=W["w_up"].shape[3], name=f"ffn_up_bwd_w_{l}")
        return dh, dcw, dcb

    dz4, dz4b, dg4, db4 = _ln_bwd([dh4], [1.0], xh4, rs4, S["ln_gain"][1, 1][None], name="ln_bwd_1b")
    dh3f, dcw1, dcb1 = ffn_bwd(dz4b, h3b, hc1, a1, 1)
    dz3, dz3b, dg3, db3 = _ln_bwd([dz4, dh3f], [DN_ALPHA, 1.0], xh3, rs3, S["ln_gain"][1, 0][None], name="ln_bwd_1a")
    do = _mm_nt(dz3b, W["w_ao"], name="attn_out_bwd_x")
    GW["w_ao"] = _tn(ob, dz3b, ptotal=1, np_cols=D, name="attn_out_bwd_w")
    dq, dk, dv, ds_sum = _attn_bwd(q, kv, do, o, lse, bias, L, hpg, name="attn_bwd")
    GS["rel_bias"] = _bias_grad(ds_sum, hpg, name="attn_bias_grad")
    GW["w_q"] = _tn(h2b, dq, ptotal=W["w_q"].shape[0], np_cols=W["w_q"].shape[3], name="attn_q_bwd_w")
    pkv, npkv = W["w_kv"].shape[0], W["w_kv"].shape[3]
    gkv = _tn(h2b, dk, ptotal=pkv, np_cols=npkv, p0=0, name="attn_k_bwd_w")
    GW["w_kv"] = _tn(h2b, dv, ptotal=pkv, np_cols=npkv, p0=pkv // 2, prev=gkv, name="attn_v_bwd_w")
    dh2q = _mm_nt(dq, W["w_q"], name="attn_q_bwd_x")
    dh2k = _mm_nt(dk, W["w_kv"], p0=0, pn=pkv // 2, name="attn_k_bwd_x")
    dh2v = _mm_nt(dv, W["w_kv"], p0=pkv // 2, pn=pkv // 2, name="attn_v_bwd_x")

    gain_0b = S["ln_gain"][0, 1][None]
    if on_grads is not None:
        gain_0b = gain_0b + on_grads(0, GW)[0, 0]

    dz2, dz2b, dg2, db2 = _ln_bwd([dz3, dh2q, dh2k, dh2v], [DN_ALPHA, 1.0, 1.0, 1.0], xh2, rs2, gain_0b,
                                  name="ln_bwd_0b")
    dh1f, dcw0, dcb0 = ffn_bwd(dz2b, h1b, hc0, a0, 0)
    gain_0a = S["ln_gain"][0, 0][None]
    if on_grads is not None:
        gain_0a = gain_0a + on_grads(1, GW)[0, 0]
    dz1, dz1b, dg1, db1 = _ln_bwd([dz2, dh1f], [DN_ALPHA, 1.0], xh1, rs1, gain_0a, name="ln_bwd_0a")
    dmix_i = _interleave(dz1b, B, L)
    dgate = _mm_nt(dmix_i, W["w_out"], name="s5_out_bwd_x")
    GW["w_out"] = _tn(gate, dmix_i, ptotal=1, np_cols=D, name="s5_out_bwd_w")
    dzg, dyg1, dbglu = _glu_bwd(y, z, dgate, name="glu_bwd")
    dyg2 = _mm_nt(dzg, W["w_glu"], name="glu_z_bwd_x")
    GW["w_glu"] = _tn(yg, dzg, ptotal=1, np_cols=D, name="glu_z_bwd_w")
    dy = _gelu_bwd(y, dyg1, dyg2, name="gelu_bwd")
    if on_grads is not None:
        d_row = d_row + on_grads(2, GW)[0, 0]
    du_i, g_r, g_i, dar, dai, dd = _s5_bwd(dy, xi, h_r, h_i, wb, wc, a_r, a_i, d_row, B, name="s5_bwd")
    dwb_r = _cluster_tn(xi, g_r, ncl, tok_left=True, name="s5_b_grad_re")
    dwb_i = _cluster_tn(xi, g_i, ncl, tok_left=True, name="s5_b_grad_im")
    dwc_r = _cluster_tn(dy, h_r, ncl, tok_left=False, name="s5_c_grad_re")
    dwc_i = _cluster_tn(dy, h_i, ncl, tok_left=False, name="s5_c_grad_im")
    grad_x = _axpy(dz1, _deinterleave(du_i, B, L), DN_ALPHA, name="grad_x")

    dbb_r = jnp.transpose(_unblockdiag(dwb_r, SSM_GROUP, Pst), (0, 2, 1))
    dbb_i = jnp.transpose(_unblockdiag(dwb_i, SSM_GROUP, Pst), (0, 2, 1))
    unslab = lambda da: jnp.transpose(da.reshape(cs // LANES, B, ncl, LANES).sum(1), (1, 0, 2)).reshape(G, Pst)
    dab_r, dab_i = unslab(dar), unslab(dai)
    GS["lam_re"], GS["lam_im"], GS["log_dt"], GS["b_re"], GS["b_im"] = disc_vjp((dab_r, dab_i, dbb_r, dbb_i))
    GS["c_re"] = jnp.transpose(_unblockdiag(dwc_r, Pst, SSM_GROUP), (0, 2, 1))
    GS["c_im"] = -jnp.transpose(_unblockdiag(dwc_i, Pst, SSM_GROUP), (0, 2, 1))
    GS["d"] = dd.reshape(G, SSM_GROUP)
    GS["b_glu"] = dbglu.reshape(D)
    GS["conv_w"] = jnp.stack([dcw0, dcw1])
    GS["conv_b"] = jnp.stack([dcb0[0], dcb1[0]])
    GS["ln_gain"] = jnp.stack([jnp.stack([dg1[0], dg2[0]]), jnp.stack([dg3[0], dg4[0]])])
    GS["ln_bias"] = jnp.stack([jnp.stack([db1[0], db2[0]]), jnp.stack([db3[0], db4[0]])])
    return loss, grad_x.reshape(B, L, D), GW, GS


SMALL_REPLICATED = ("lam_re", "lam_im", "log_dt", "b_re", "b_im", "c_re", "c_im", "d", "rel_bias", "conv_b")
SMALL_SHARDED = ("b_glu", "conv_w", "ln_gain", "ln_bias")
SMALL_ORDER = SMALL_REPLICATED + SMALL_SHARDED


def _pack(arrs, lanes, row_mult):
    flat = jnp.concatenate([a.reshape(-1).astype(F32) for a in arrs])
    rows = -(-flat.shape[0] // lanes)
    rows = -(-rows // row_mult) * row_mult
    return jnp.pad(flat, (0, rows * lanes - flat.shape[0])).reshape(rows, lanes)


def _unpack(packed, shapes):
    flat = packed.reshape(-1)
    out, off = [], 0
    for s in shapes:
        n = int(np.prod(s))
        out.append(flat[off:off + n].reshape(s))
        off += n
    return out


def kernel(x, s5_lam_re, s5_lam_im, s5_log_dt, s5_b_re, s5_b_im, s5_c_re, s5_c_im, s5_d, s5_w_glu, s5_b_glu, s5_w_out, attn_w_kv, attn_w_q, attn_w_out, rel_bias, ffn_w_up, ffn_conv_w, ffn_conv_b, ffn_w_down, ln_gain, ln_bias, loss_target, m_s5_lam_re, m_s5_lam_im, m_s5_log_dt, m_s5_b_re, m_s5_b_im, m_s5_c_re, m_s5_c_im, m_s5_d, m_s5_w_glu, m_s5_b_glu, m_s5_w_out, m_attn_w_kv, m_attn_w_q, m_attn_w_out, m_rel_bias, m_ffn_w_up, m_ffn_conv_w, m_ffn_conv_b, m_ffn_w_down, m_ln_gain, m_ln_bias, v_s5_lam_re, v_s5_lam_im, v_s5_log_dt, v_s5_b_re, v_s5_b_im, v_s5_c_re, v_s5_c_im, v_s5_d, v_s5_w_glu, v_s5_b_glu, v_s5_w_out, v_attn_w_kv, v_attn_w_q, v_attn_w_out, v_rel_bias, v_ffn_w_up, v_ffn_conv_w, v_ffn_conv_b, v_ffn_w_down, v_ln_gain, v_ln_bias):
    names = ["s5_lam_re", "s5_lam_im", "s5_log_dt", "s5_b_re", "s5_b_im", "s5_c_re", "s5_c_im", "s5_d", "s5_w_glu",
             "s5_b_glu", "s5_w_out", "attn_w_kv", "attn_w_q", "attn_w_out", "rel_bias", "ffn_w_up", "ffn_conv_w",
             "ffn_conv_b", "ffn_w_down", "ln_gain", "ln_bias"]
    loc = locals()
    w_in = {n: loc[n] for n in names}
    m_in = {n: loc["m_" + n] for n in names}
    v_in = {n: loc["v_" + n] for n in names}
    chip = 2 * lax.axis_index("x") + lax.axis_index("y")
    core = lax.axis_index("c")
    chip_idx = jnp.reshape(chip, (1,)).astype(jnp.int32)
    cidx = jnp.reshape(core, (1,)).astype(jnp.int32)

    big = [("w_glu", "s5_w_glu", "rows"), ("w_out", "s5_w_out", "rows"), ("w_ao", "attn_w_out", "rows"),
           ("w_kv", "attn_w_kv", "cols"), ("w_q", "attn_w_q", "cols"),
           ("w_up", "ffn_w_up", "layer_cols"), ("w_down", "ffn_w_down", "layer_rows")]

    def halves(t, kind):
        if kind.startswith("layer"):
            return t
        r, c = t.shape[-2:]
        return t.reshape(2, r // 2, c)

    def to_weight(g, kind):
        _, _, r, c = g.shape
        if kind == "rows":
            return g.reshape(1, 1, 8 * r, c)
        if kind == "cols":
            return g.reshape(4, 1, 2 * r, c)
        if kind == "layer_cols":
            return g
        return jnp.transpose(g, (1, 0, 2, 3)).reshape(1, 2, 4 * r, c)

    small_sh = {"b_glu": s5_b_glu[0], "conv_w": ffn_conv_w, "ln_gain": ln_gain, "ln_bias": ln_bias}
    sh_shapes = [small_sh[k].shape for k in SMALL_SHARDED]
    sh_pack = _pack([small_sh[k] for k in SMALL_SHARDED], 128, 16)

    shards = [halves(w_in[src].astype(MXU_DTYPE), kind) for _, src, kind in big]
    shards.append(sh_pack.reshape(2, sh_pack.shape[0] // 2, 128))
    shard_of = {key: s for (key, _, _), s in zip(big, shards)}
    shard_of["small"] = shards[-1]
    kind_of = {key: kind for key, _, kind in big}

    group_keys = [["w_glu", "w_out", "small"], ["w_up", "w_down"], ["w_kv", "w_q", "w_ao"]]
    started, token = _gather_start([[shard_of[k] for k in g] for g in group_keys], name="weights_gather_start")

    forwarding = {}

    def prepare_group(gi, after):
        ssem, rsem, thru, lands = started[gi]
        lands = _gather_wait(ssem, rsem, thru, lands, after, name=f"weights_gather_wait_{gi}")
        forwarding[gi], tok = _gather_forward_start(lands, name=f"weights_gather_forward_start_{gi}")
        return tok

    def finish_group(gi, after):
        if gi in forwarding:
            lands = _gather_forward_wait(forwarding.pop(gi), after, name=f"weights_gather_forward_wait_{gi}")
        else:
            ssem, rsem, thru, lands = started[gi]
            lands = _gather_wait(ssem, rsem, thru, lands, after, name=f"weights_gather_wait_{gi}")
            lands = _gather_forward(lands, name=f"weights_gather_forward_{gi}")
        out = {}
        for key, land in zip(group_keys[gi], lands):
            full = lax.dynamic_update_slice(land, shard_of[key][None], (chip, 0, 0, 0))
            if key == "small":
                parts = [_unpack(full[p], sh_shapes) for p in range(4)]
                for i, k in enumerate(SMALL_SHARDED):
                    out[k] = jnp.concatenate([parts[p][i] for p in range(4)], axis=-1)
            else:
                out[key] = to_weight(full, kind_of[key])
        return out

    replicated = dict(lam_re=s5_lam_re[0], lam_im=s5_lam_im[0], log_dt=s5_log_dt[0], b_re=s5_b_re[0], b_im=s5_b_im[0],
                      c_re=s5_c_re[0], c_im=s5_c_im[0], rel_bias=rel_bias, conv_b=ffn_conv_b,
                      d=s5_d[0] + token[0, 0])
    group_of = {k: gi for gi, g in enumerate(group_keys) for k in g if k != "small"}
    group_of.update({k: 0 for k in SMALL_SHARDED})
    group_of.update({k: "replicated" for k in replicated})
    params = _Lazy(group_of, lambda g, after: replicated if g == "replicated" else finish_group(g, after), prepare_group)

    red = [("w_up1", "ffn_w_up", 1), ("w_down1", "ffn_w_down", 1), ("w_ao", "attn_w_out", 0), ("w_kv", "attn_w_kv", 0),
           ("w_q", "attn_w_q", 0), ("w_down0", "ffn_w_down", 0), ("w_up0", "ffn_w_up", 0), ("w_out", "s5_w_out", 0),
           ("w_glu", "s5_w_glu", 0)]
    stages = [red[:5], red[5:7], red[7:]]

    def grad_halves(gw, key, src):
        r, c = w_in[src].shape[-2:]
        return gw[key].reshape(4, 2, r // 2, c)

    sent, early = {}, []

    def on_grads(stage, gw):
        tokens = []
        if stage > 0:
            tag = "abc"[stage - 1]
            ga, theirs = _pair_send_wait(sent.pop(stage - 1), gw[stages[stage][-1][0]], name=f"rs_pair_send_wait_{tag}")
            hf, hx = _pair_sums(ga, [True] * len(ga), cidx, tag, theirs)
            started, tok = _chip_exchange_start(hx, name=f"rs_chip_exchange_start_{tag}")
            early.append((hf, started, tag))
            tokens.append(tok)
        ga = [grad_halves(gw, key, src) for key, src, _ in stages[stage]]
        sent[stage], tok = _pair_send_start(ga, name=f"rs_pair_send_start_{'abc'[stage]}")
        return sum(tokens, tok)

    loss, grad_x, GW, GS = _local_step(x, loss_target, params, params, on_grads)

    gs_shapes = [GS[k].shape for k in SMALL_ORDER] + [(1,)]
    gs_pack = _pack([GS[k] for k in SMALL_ORDER] + [loss.reshape(1)], 128, 64)
    rs = gs_pack.shape[0] // 8
    gs_halves = [gs_pack.reshape(4, 2, rs, 128)]
    gl, theirs_l = _pair_send_wait(sent.pop(2), grad_x, name="rs_pair_send_wait_c")
    theirs_l += _pair_send(gs_halves, name="rs_pair_send_small")
    gl += gs_halves
    hf_l, hx_l = _pair_sums(gl, [True] * (len(gl) - 1) + [False], cidx, "c", theirs_l)
    started_l, after = _chip_exchange_start(hx_l, name="rs_chip_exchange_start_c")
    mine = []
    for hf, started, tag in early:
        got = _chip_exchange_wait(started, after, name=f"rs_chip_exchange_wait_{tag}")
        mine += _chip_sums(hf, got, chip_idx, tag)
        after = mine[-1]
    mine += _chip_sums(hf_l, _chip_exchange_wait(started_l, after, name="rs_chip_exchange_wait_c"), chip_idx, "c")
    other = _pair_swap(mine, name="rs_pair_swap")
    small_halves = jnp.where(core == 0, jnp.concatenate([mine[-1], other[-1]]), jnp.concatenate([other[-1], mine[-1]]))
    small_all = _gather([small_halves], chip, name="small_grads_all_gather")[0]
    totals = _unpack(small_all, gs_shapes)
    gsmall = dict(zip(SMALL_ORDER, totals))
    loss = totals[-1][0]

    big_res = {}
    for (key, src, layer), gm, go in zip(red, mine[:-1], other[:-1]):
        nl = w_in[src].shape[0] if src in ("ffn_w_up", "ffn_w_down") else 1
        r, c = w_in[src].shape[-2:]
        view = lambda t: t.reshape(nl, 2, r // 2, c)
        res4 = _adamw_halves(view(w_in[src]), view(m_in[src]), view(v_in[src]), gm, go, cidx, layer=layer,
                             prev=big_res.get(src), name=f"adamw_{key}")
        big_res[src] = res4
    big_res = {src: tuple(t.reshape(w_in[src].shape) for t in res4) for src, res4 in big_res.items()}

    def big_out(i):
        return {src: big_res[src][i] for _, src, _ in big}

    small_w = {"lam_re": s5_lam_re, "lam_im": s5_lam_im, "log_dt": s5_log_dt, "b_re": s5_b_re, "b_im": s5_b_im,
               "c_re": s5_c_re, "c_im": s5_c_im, "d": s5_d, "rel_bias": rel_bias, "conv_b": ffn_conv_b,
               "b_glu": s5_b_glu, "conv_w": ffn_conv_w, "ln_gain": ln_gain, "ln_bias": ln_bias}
    small_name = {"lam_re": "s5_lam_re", "lam_im": "s5_lam_im", "log_dt": "s5_log_dt", "b_re": "s5_b_re", "b_im": "s5_b_im",
                  "c_re": "s5_c_re", "c_im": "s5_c_im", "d": "s5_d", "rel_bias": "rel_bias", "conv_b": "ffn_conv_b",
                  "b_glu": "s5_b_glu", "conv_w": "ffn_conv_w", "ln_gain": "ln_gain", "ln_bias": "ln_bias"}
    sg = {}
    for k in SMALL_ORDER:
        shp = small_w[k].shape
        g = gsmall[k]
        if k in SMALL_SHARDED:
            width = shp[-1]
            g = lax.dynamic_slice_in_dim(g, chip * width, width, axis=g.ndim - 1)
        sg[k] = g.reshape(shp)
    sd, snm, snv = {}, {}, {}
    for k in SMALL_ORDER:
        shp = small_w[k].shape
        flat = lambda t: t.reshape(-1, shp[-1])
        r3 = _adamw(flat(small_w[k]), flat(sg[k]), flat(m_in[small_name[k]]), flat(v_in[small_name[k]]),
                    name=f"adamw_{k}")
        sd[k], snm[k], snv[k] = (t.reshape(shp) for t in r3)

    res = [{}, {}, {}, {}]
    for i in range(4):
        res[i].update(big_out(i))
    for k in SMALL_ORDER:
        res[0][small_name[k]] = sg[k]
        res[1][small_name[k]] = sd[k]
        res[2][small_name[k]] = snm[k]
        res[3][small_name[k]] = snv[k]
    outs = [loss, grad_x]
    for i in range(4):
        outs += [res[i][n] for n in names]
    return tuple(outs)
```

```python
import functools
import math

import numpy as np
import jax
import jax.numpy as jnp
from jax import lax
from jax.experimental import pallas as pl
from jax.experimental.pallas import tpu as pltpu

F32 = jnp.float32
BF16 = jnp.bfloat16
MXU_DTYPE = jnp.bfloat16
V7X_VMEM_LIMIT_BYTES = 52 << 20
MESH = pl.DeviceIdType.MESH

DEPTH = 2
SSM_GROUP = 16
SSM_STATE = 64
GROUPS_PER_CLUSTER = 16
CLUSTER_W = GROUPS_PER_CLUSTER * SSM_GROUP
HEAD_DIM = 64
DILATIONS = (1, 4, 16)
BAND = 128
ATTN_BATCH = (4, 8)
NEG_BIG = -1e30
REL_BUCKETS = 32
REL_MAX_DIST = 2048
DN_ALPHA = (2.0 * DEPTH) ** 0.25
LN_EPS = 1e-5
ADAM_LR, ADAM_B1, ADAM_B2, ADAM_EPS, ADAM_WD, ADAM_STEP = 0.001, 0.9, 0.999, 1e-08, 0.01, 10
GELU_K = math.sqrt(2.0 / math.pi)
GELU_C = 0.044715


def _pallas(body, **kw):
    return pl.pallas_call(body, **kw)


def _params(sem=None):
    return pltpu.CompilerParams(dimension_semantics=sem, vmem_limit_bytes=V7X_VMEM_LIMIT_BYTES)


def _pick(n, cands):
    for c in cands:
        if n % c == 0:
            return c
    return n


def _sigmoid(z):
    return 1.0 / (1.0 + jnp.exp(-z))


def _gelu(y):
    return 0.5 * y * (1.0 + jnp.tanh(GELU_K * (y + GELU_C * y * y * y)))


def _gelu_grad(y):
    t = jnp.tanh(GELU_K * (y + GELU_C * y * y * y))
    return 0.5 * (1.0 + t) + 0.5 * y * (1.0 - t * t) * (GELU_K * (1.0 + 3.0 * GELU_C * y * y))


def _mm_nn(a, w, *, l=0, bias=None, out_dtype=F32, name):
    T, K = a.shape
    P, _, _, Np = w.shape
    tm = _pick(T, (1024, 512, 256, 128))
    tn = _pick(Np, (1408, 1024, 768, 512, 384, 256, 128))
    nj = Np // tn

    def body(*refs):
        if bias is None:
            a_ref, w_ref, o_ref = refs
        else:
            a_ref, w_ref, b_ref, o_ref = refs
        acc = jnp.dot(a_ref[...].astype(MXU_DTYPE), w_ref[...].astype(MXU_DTYPE), preferred_element_type=F32)
        if bias is not None:
            acc = acc + b_ref[...]
        o_ref[...] = acc.astype(o_ref.dtype)

    in_specs = [pl.BlockSpec((tm, K), lambda p, j, i: (i, 0)),
                pl.BlockSpec((None, None, K, tn), lambda p, j, i: (p, l, 0, j))]
    args = [a, w]
    if bias is not None:
        in_specs.append(pl.BlockSpec((1, tn), lambda p, j, i: (0, p * nj + j)))
        args.append(bias)
    return _pallas(
        body, name=name, grid=(P, nj, T // tm), in_specs=in_specs,
        out_specs=pl.BlockSpec((tm, tn), lambda p, j, i: (i, p * nj + j)),
        out_shape=jax.ShapeDtypeStruct((T, P * Np), out_dtype),
        compiler_params=_params(("parallel", "parallel", "parallel")),
    )(*args)


def _mm_nt(a, w, *, l=0, p0=0, pn=None, out_dtype=F32, name):
    T = a.shape[0]
    _, _, K, Np = w.shape
    pn = w.shape[0] if pn is None else pn
    tm = _pick(T, (1024, 512, 256, 128) if K <= 1024 else (512, 256, 128))
    tn = _pick(Np, (1536, 1408, 1024, 768, 512, 384, 256, 128))
    nj = Np // tn
    nred = pn * nj

    def body(a_ref, w_ref, o_ref, acc):
        r = pl.program_id(1)

        @pl.when(r == 0)
        def _():
            acc[...] = jnp.zeros_like(acc)

        acc[...] += lax.dot_general(a_ref[...].astype(MXU_DTYPE), w_ref[...].astype(MXU_DTYPE),
                                    (((1,), (1,)), ((), ())), preferred_element_type=F32)

        @pl.when(r == nred - 1)
        def _():
            o_ref[...] = acc[...].astype(o_ref.dtype)

    return _pallas(
        body, name=name, grid=(T // tm, nred),
        in_specs=[pl.BlockSpec((tm, tn), lambda i, r: (i, r)),
                  pl.BlockSpec((None, None, K, tn), lambda i, r: (p0 + r // nj, l, 0, r % nj))],
        out_specs=pl.BlockSpec((tm, K), lambda i, r: (i, 0)),
        out_shape=jax.ShapeDtypeStruct((T, K), out_dtype),
        scratch_shapes=[pltpu.VMEM((tm, K), F32)],
        compiler_params=_params(("parallel", "arbitrary")),
    )(a, w)


def _tn(a, b, *, ptotal, np_cols, nl=1, l=0, p0=0, prev=None, name):
    T, K = a.shape
    Np = np_cols
    pn = b.shape[1] // Np
    tt = _pick(T, (1024, 512, 256, 128))
    tk = _pick(K, (1408, 1024, 512, 256, 128))
    tn = _pick(Np, (1408, 768, 512, 256, 128))
    if tk * tn > 1408 * 1024:
        tn = _pick(Np, (512, 256, 128))
    nj = Np // tn
    nt = T // tt

    def body(*refs):
        a_ref, b_ref = refs[0], refs[1]
        o_ref, acc = refs[-2], refs[-1]
        t = pl.program_id(3)

        @pl.when(t == 0)
        def _():
            acc[...] = jnp.zeros_like(acc)

        acc[...] += lax.dot_general(a_ref[...].astype(MXU_DTYPE), b_ref[...].astype(MXU_DTYPE),
                                    (((0,), (0,)), ((), ())), preferred_element_type=F32)

        @pl.when(t == nt - 1)
        def _():
            o_ref[...] = acc[...]

    in_specs = [pl.BlockSpec((tt, tk), lambda kb, p, j, t: (t, kb)),
                pl.BlockSpec((tt, tn), lambda kb, p, j, t: (t, p * nj + j))]
    args = [a, b]
    aliases = {}
    if prev is not None:
        in_specs.append(pl.BlockSpec(memory_space=pl.ANY))
        args.append(prev)
        aliases = {2: 0}
    return _pallas(
        body, name=name, grid=(K // tk, pn, nj, nt), in_specs=in_specs,
        out_specs=pl.BlockSpec((None, None, tk, tn), lambda kb, p, j, t: (p0 + p, l, kb, j)),
        out_shape=jax.ShapeDtypeStruct((ptotal, nl, K, Np), F32),
        scratch_shapes=[pltpu.VMEM((tk, tn), F32)],
        input_output_aliases=aliases,
        compiler_params=_params(("parallel", "parallel", "parallel", "arbitrary")),
    )(*args)


def _rows(tm, f):
    return pl.BlockSpec((tm, f), lambda i: (i, 0))


def _whole(shape):
    nd = len(shape)
    return pl.BlockSpec(shape, lambda i: (0,) * nd)


def _ln_fwd(xres, f, gain, bias, *, name):
    T, D = xres.shape
    tm = _pick(T, (256, 128))

    def body(x_ref, f_ref, g_ref, b_ref, y_ref, yb_ref, xh_ref, rs_ref):
        z = DN_ALPHA * x_ref[...] + f_ref[...]
        mu = jnp.mean(z, axis=-1, keepdims=True)
        zc = z - mu
        var = jnp.mean(zc * zc, axis=-1, keepdims=True)
        rstd = lax.rsqrt(var + LN_EPS)
        xh = zc * rstd
        y = xh * g_ref[...] + b_ref[...]
        y_ref[...] = y
        yb_ref[...] = y.astype(yb_ref.dtype)
        xh_ref[...] = xh
        rs_ref[...] = rstd

    return _pallas(
        body, name=name, grid=(T // tm,),
        in_specs=[_rows(tm, D), _rows(tm, D), _whole((1, D)), _whole((1, D))],
        out_specs=[_rows(tm, D), _rows(tm, D), _rows(tm, D), _rows(tm, 1)],
        out_shape=[jax.ShapeDtypeStruct((T, D), F32), jax.ShapeDtypeStruct((T, D), MXU_DTYPE),
                   jax.ShapeDtypeStruct((T, D), F32), jax.ShapeDtypeStruct((T, 1), F32)],
        compiler_params=_params(("parallel",)),
    )(xres, f, gain, bias)


def _ln_bwd(addends, coefs, xhat, rstd, gain, *, name):
    T, D = xhat.shape
    tm = _pick(T, (256, 128))
    n = len(addends)

    def body(*refs):
        adds = refs[:n]
        xh_ref, rs_ref, g_ref, dz_ref, dzb_ref, dg_ref, db_ref = refs[n:]
        dy = coefs[0] * adds[0][...]
        for c, r in zip(coefs[1:], adds[1:]):
            dy = dy + c * r[...]
        xh = xh_ref[...]
        dxh = dy * g_ref[...]
        m1 = jnp.mean(dxh, axis=-1, keepdims=True)
        m2 = jnp.mean(dxh * xh, axis=-1, keepdims=True)
        dz = rs_ref[...] * (dxh - m1 - xh * m2)
        dz_ref[...] = dz
        dzb_ref[...] = dz.astype(dzb_ref.dtype)

        @pl.when(pl.program_id(0) == 0)
        def _():
            dg_ref[...] = jnp.zeros_like(dg_ref)
            db_ref[...] = jnp.zeros_like(db_ref)

        dg_ref[...] += jnp.sum(dy * xh, axis=0, keepdims=True)
        db_ref[...] += jnp.sum(dy, axis=0, keepdims=True)

    return _pallas(
        body, name=name, grid=(T // tm,),
        in_specs=[_rows(tm, D)] * n + [_rows(tm, D), _rows(tm, 1), _whole((1, D))],
        out_specs=[_rows(tm, D), _rows(tm, D), _whole((1, D)), _whole((1, D))],
        out_shape=[jax.ShapeDtypeStruct((T, D), F32), jax.ShapeDtypeStruct((T, D), MXU_DTYPE),
                   jax.ShapeDtypeStruct((1, D), F32), jax.ShapeDtypeStruct((1, D), F32)],
        compiler_params=_params(("arbitrary",)),
    )(*addends, xhat, rstd, gain)


def _loss_grad(y, tgt, *, name):
    T, D = y.shape
    tm = _pick(T, (256, 128))

    def body(y_ref, t_ref, dy_ref, l_ref):
        e = y_ref[...] - t_ref[...]
        dy_ref[...] = e * (1.0 / D)

        @pl.when(pl.program_id(0) == 0)
        def _():
            l_ref[...] = jnp.zeros_like(l_ref)

        l_ref[...] += jnp.zeros_like(l_ref) + jnp.sum(e * e) * (0.5 / D)

    return _pallas(
        body, name=name, grid=(T // tm,),
        in_specs=[_rows(tm, D), _rows(tm, D)],
        out_specs=[_rows(tm, D), _whole((1, 128))],
        out_shape=[jax.ShapeDtypeStruct((T, D), F32), jax.ShapeDtypeStruct((1, 128), F32)],
        compiler_params=_params(("arbitrary",)),
    )(y, tgt)


def _axpy(a, b, ca, *, name):
    T, D = a.shape
    tm = _pick(T, (256, 128))

    def body(a_ref, b_ref, o_ref):
        o_ref[...] = ca * a_ref[...] + b_ref[...]

    return _pallas(
        body, name=name, grid=(T // tm,), in_specs=[_rows(tm, D), _rows(tm, D)], out_specs=_rows(tm, D),
        out_shape=jax.ShapeDtypeStruct((T, D), F32), compiler_params=_params(("parallel",)),
    )(a, b)


def _glu_gate(y, z, *, name):
    T, D = y.shape
    tm = _pick(T, (256, 128))

    def body(y_ref, z_ref, g_ref):
        g_ref[...] = (_gelu(y_ref[...]) * _sigmoid(z_ref[...])).astype(g_ref.dtype)

    return _pallas(
        body, name=name, grid=(T // tm,), in_specs=[_rows(tm, D), _rows(tm, D)], out_specs=_rows(tm, D),
        out_shape=jax.ShapeDtypeStruct((T, D), MXU_DTYPE), compiler_params=_params(("parallel",)),
    )(y, z)


def _glu_bwd(y, z, dg, *, name):
    T, D = y.shape
    tm = _pick(T, (256, 128))

    def body(y_ref, z_ref, dg_ref, dzb_ref, dyg_ref, db_ref):
        s = _sigmoid(z_ref[...])
        dg = dg_ref[...]
        dz = dg * _gelu(y_ref[...]) * s * (1.0 - s)
        dzb_ref[...] = dz.astype(dzb_ref.dtype)
        dyg_ref[...] = dg * s

        @pl.when(pl.program_id(0) == 0)
        def _():
            db_ref[...] = jnp.zeros_like(db_ref)

        db_ref[...] += jnp.sum(dz, axis=0, keepdims=True)

    return _pallas(
        body, name=name, grid=(T // tm,), in_specs=[_rows(tm, D)] * 3,
        out_specs=[_rows(tm, D), _rows(tm, D), _whole((1, D))],
        out_shape=[jax.ShapeDtypeStruct((T, D), MXU_DTYPE), jax.ShapeDtypeStruct((T, D), F32),
                   jax.ShapeDtypeStruct((1, D), F32)],
        compiler_params=_params(("arbitrary",)),
    )(y, z, dg)


def _gelu_bwd(y, d1, d2, *, name):
    T, D = y.shape
    tm = _pick(T, (256, 128))

    def body(y_ref, a_ref, b_ref, o_ref):
        o_ref[...] = (a_ref[...] + b_ref[...]) * _gelu_grad(y_ref[...])

    return _pallas(
        body, name=name, grid=(T // tm,), in_specs=[_rows(tm, D)] * 3, out_specs=_rows(tm, D),
        out_shape=jax.ShapeDtypeStruct((T, D), F32), compiler_params=_params(("parallel",)),
    )(y, d1, d2)


CONV_ROWS = 128
CONV_EDGE = 16


def _row_shifts(x, edge, drop_edge, tm, back):
    keep = jnp.where(drop_edge, 0.0, 1.0).astype(edge.dtype)
    ext = jnp.concatenate([edge * keep, x] if back else [x, edge * keep], axis=0)
    row = lax.broadcasted_iota(jnp.int32, (tm, tm + CONV_EDGE), 0)
    col = lax.broadcasted_iota(jnp.int32, (tm, tm + CONV_EDGE), 1)
    base = row + CONV_EDGE if back else row
    out = []
    for k in (1, 2):
        pick = (col == (base - k if back else base + k)).astype(x.dtype)
        out.append(jnp.dot(pick, ext, preferred_element_type=F32))
    return out


def _conv_specs(T, F2, tm):
    return [_rows(tm, F2),
            pl.BlockSpec((CONV_EDGE, F2), lambda i: (jnp.maximum(i * (tm // CONV_EDGE) - 1, 0), 0))]


def _conv_glu_fwd(hc, conv_w, conv_b, L, *, name):
    T, F2 = hc.shape
    F = F2 // 2
    tm = CONV_ROWS

    def body(x_ref, e_ref, w_ref, b_ref, a_ref):
        at_start = (pl.program_id(0) * tm) % L == 0
        x1, x2 = _row_shifts(x_ref[...], e_ref[...], at_start, tm, True)
        x = x_ref[...].astype(F32)
        c = b_ref[...] + w_ref[0:1, :] * x + w_ref[1:2, :] * x1 + w_ref[2:3, :] * x2
        val, gate = c[:, :F], c[:, F:]
        a_ref[...] = (gate * _sigmoid(gate) * val).astype(a_ref.dtype)

    return _pallas(
        body, name=name, grid=(T // tm,),
        in_specs=_conv_specs(T, F2, tm) + [_whole((3, F2)), _whole((1, F2))],
        out_specs=_rows(tm, F),
        out_shape=jax.ShapeDtypeStruct((T, F), MXU_DTYPE), compiler_params=_params(("parallel",)),
    )(hc, hc, conv_w, conv_b)


def _conv_glu_bwd(hc, da, conv_w, conv_b, L, *, name):
    T, F2 = hc.shape
    F = F2 // 2
    tm = CONV_ROWS

    def body(x_ref, e_ref, da_ref, w_ref, b_ref, dc_ref, dw_ref, db_ref):
        at_start = (pl.program_id(0) * tm) % L == 0
        x1, x2 = _row_shifts(x_ref[...], e_ref[...], at_start, tm, True)
        x = x_ref[...].astype(F32)
        c = b_ref[...] + w_ref[0:1, :] * x + w_ref[1:2, :] * x1 + w_ref[2:3, :] * x2
        val, gate = c[:, :F], c[:, F:]
        s = _sigmoid(gate)
        da = da_ref[...].astype(F32)
        dval = da * (gate * s)
        dgate = da * val * (s * (1.0 + gate * (1.0 - s)))
        dc = jnp.concatenate([dval, dgate], axis=-1)
        dc_ref[...] = dc.astype(dc_ref.dtype)

        @pl.when(pl.program_id(0) == 0)
        def _():
            dw_ref[...] = jnp.zeros_like(dw_ref)
            db_ref[...] = jnp.zeros_like(db_ref)

        dw_ref[0:1, :] += jnp.sum(dc * x, axis=0, keepdims=True)
        dw_ref[1:2, :] += jnp.sum(dc * x1, axis=0, keepdims=True)
        dw_ref[2:3, :] += jnp.sum(dc * x2, axis=0, keepdims=True)
        db_ref[...] += jnp.sum(dc, axis=0, keepdims=True)

    return _pallas(
        body, name=name, grid=(T // tm,),
        in_specs=_conv_specs(T, F2, tm) + [_rows(tm, F), _whole((3, F2)), _whole((1, F2))],
        out_specs=[_rows(tm, F2), _whole((3, F2)), _whole((1, F2))],
        out_shape=[jax.ShapeDtypeStruct((T, F2), MXU_DTYPE), jax.ShapeDtypeStruct((3, F2), F32),
                   jax.ShapeDtypeStruct((1, F2), F32)],
        compiler_params=_params(("arbitrary",)),
    )(hc, hc, da, conv_w, conv_b)


def _conv_bwd_input(dc, conv_w, L, *, name):
    T, F2 = dc.shape
    tm = CONV_ROWS
    edge = CONV_EDGE
    last_blk = T // edge - 1

    def body(x_ref, e_ref, w_ref, o_ref):
        at_end = ((pl.program_id(0) + 1) * tm) % L == 0
        x1, x2 = _row_shifts(x_ref[...], e_ref[...], at_end, tm, False)
        x = x_ref[...].astype(F32)
        o_ref[...] = (w_ref[0:1, :] * x + w_ref[1:2, :] * x1 + w_ref[2:3, :] * x2).astype(o_ref.dtype)

    return _pallas(
        body, name=name, grid=(T // tm,),
        in_specs=[_rows(tm, F2),
                  pl.BlockSpec((edge, F2), lambda i: (jnp.minimum((i + 1) * (tm // edge), last_blk), 0)),
                  _whole((3, F2))],
        out_specs=_rows(tm, F2),
        out_shape=jax.ShapeDtypeStruct((T, F2), MXU_DTYPE), compiler_params=_params(("parallel",)),
    )(dc, dc, conv_w)


S5_CHUNK = 128
LANES = 128


def _slab_rows(c, n, ncl):
    return pl.ds(c, n) if ncl == 1 else pl.ds(c, n, stride=ncl)


def _slab_put(ref, c, n, ncl, val):
    for s in range(val.shape[1] // LANES):
        ref[s, _slab_rows(c, n, ncl), :] = val[:, s * LANES:(s + 1) * LANES]


def _slab_get(ref, c, n, ncl):
    return jnp.concatenate([ref[s, _slab_rows(c, n, ncl), :] for s in range(ref.shape[0])], axis=-1)


def _slabs(n_slab, rows):
    return pl.BlockSpec((n_slab, rows, LANES), lambda i: (0, i, 0))


def _s5_fwd(xi, wb, wc, a_r, a_i, d_row, B, *, name):
    T, D = xi.shape
    ncl = wb.shape[0]
    cs = wb.shape[2] // 2
    ns = cs // LANES
    R = B * ncl
    Q = S5_CHUNK
    QR = Q * ncl
    nsteps = Q // B

    def body(x_ref, wb_ref, wc_ref, ar_ref, ai_ref, d_ref, y_ref, yg_ref, hr_ref, hi_ref, bur, bui, cr, ci):
        @pl.when(pl.program_id(0) == 0)
        def _():
            cr[...] = jnp.zeros_like(cr)
            ci[...] = jnp.zeros_like(ci)

        x = x_ref[...]
        xb = x.astype(MXU_DTYPE)
        for c in range(ncl):
            bu = jnp.dot(xb[:, c * CLUSTER_W:(c + 1) * CLUSTER_W], wb_ref[c], preferred_element_type=F32)
            _slab_put(bur, c, Q, ncl, bu[:, :cs])
            _slab_put(bui, c, Q, ncl, bu[:, cs:])
        ar = ar_ref[...]
        ai = ai_ref[...]

        def step(k, carry):
            hr, hi = carry
            sl = pl.ds(pl.multiple_of(k * R, R), R)
            nr = ar * hr - ai * hi + bur[:, sl, :]
            ni = ar * hi + ai * hr + bui[:, sl, :]
            hr_ref[:, sl, :] = nr
            hi_ref[:, sl, :] = ni
            return nr, ni

        hr, hi = lax.fori_loop(0, nsteps, step, (cr[...], ci[...]), unroll=4)
        cr[...] = hr
        ci[...] = hi
        parts = []
        for c in range(ncl):
            hrc = _slab_get(hr_ref, c, Q, ncl).astype(MXU_DTYPE)
            hic = _slab_get(hi_ref, c, Q, ncl).astype(MXU_DTYPE)
            parts.append(jnp.dot(hrc, wc_ref[c, :cs, :], preferred_element_type=F32)
                         + jnp.dot(hic, wc_ref[c, cs:, :], preferred_element_type=F32))
        y = d_ref[...] * x + (parts[0] if ncl == 1 else jnp.concatenate(parts, axis=-1))
        y_ref[...] = y
        yg_ref[...] = _gelu(y).astype(yg_ref.dtype)

    return _pallas(
        body, name=name, grid=(T // Q,),
        in_specs=[_rows(Q, D), _whole(wb.shape), _whole(wc.shape), _whole((ns, R, LANES)), _whole((ns, R, LANES)),
                  _whole((1, D))],
        out_specs=[_rows(Q, D), _rows(Q, D), _slabs(ns, QR), _slabs(ns, QR)],
        out_shape=[jax.ShapeDtypeStruct((T, D), F32), jax.ShapeDtypeStruct((T, D), MXU_DTYPE),
                   jax.ShapeDtypeStruct((ns, T * ncl, LANES), F32), jax.ShapeDtypeStruct((ns, T * ncl, LANES), F32)],
        scratch_shapes=[pltpu.VMEM((ns, QR, LANES), F32), pltpu.VMEM((ns, QR, LANES), F32),
                        pltpu.VMEM((ns, R, LANES), F32), pltpu.VMEM((ns, R, LANES), F32)],
        compiler_params=_params(("arbitrary",)),
    )(xi, wb, wc, a_r, a_i, d_row)


def _s5_bwd(dy, xi, h_r, h_i, wb, wc, a_r, a_i, d_row, B, *, name):
    T, D = dy.shape
    ncl = wb.shape[0]
    cs = wb.shape[2] // 2
    ns = cs // LANES
    R = B * ncl
    Q = S5_CHUNK
    nsteps = Q // B
    nchunk = T // Q
    QR = Q * ncl

    def rev(i):
        return nchunk - 1 - i

    def body(dy_ref, x_ref, hr_ref, hi_ref, pr_ref, pi_ref, wb_ref, wc_ref, ar_ref, ai_ref, d_ref,
             du_ref, gr_ref, gi_ref, dar_ref, dai_ref, dd_ref, dhr, dhi, cr, ci):
        i = pl.program_id(0)

        @pl.when(i == 0)
        def _():
            cr[...] = jnp.zeros_like(cr)
            ci[...] = jnp.zeros_like(ci)
            dar_ref[...] = jnp.zeros_like(dar_ref)
            dai_ref[...] = jnp.zeros_like(dai_ref)
            dd_ref[...] = jnp.zeros_like(dd_ref)

        dyv = dy_ref[...]
        dyb = dyv.astype(MXU_DTYPE)
        for c in range(ncl):
            dh = lax.dot_general(dyb[:, c * CLUSTER_W:(c + 1) * CLUSTER_W], wc_ref[c],
                                 (((1,), (1,)), ((), ())), preferred_element_type=F32)
            _slab_put(dhr, c, Q, ncl, dh[:, :cs])
            _slab_put(dhi, c, Q, ncl, dh[:, cs:])
        ar = ar_ref[...]
        ai = ai_ref[...]

        def step(j, carry):
            gr, gi = carry
            k = nsteps - 1 - j
            sl = pl.ds(pl.multiple_of(k * R, R), R)
            ngr = dhr[:, sl, :] + ar * gr + ai * gi
            ngi = dhi[:, sl, :] - ai * gr + ar * gi
            gr_ref[:, sl, :] = ngr
            gi_ref[:, sl, :] = ngi
            return ngr, ngi

        gr, gi = lax.fori_loop(0, nsteps, step, (cr[...], ci[...]), unroll=4)
        cr[...] = gr
        ci[...] = gi
        keep = jnp.where(i == nchunk - 1, 0.0, 1.0)
        hpr = jnp.concatenate([pr_ref[:, 8 - R:8, :] * keep, hr_ref[:, 0:QR - R, :]], axis=1)
        hpi = jnp.concatenate([pi_ref[:, 8 - R:8, :] * keep, hi_ref[:, 0:QR - R, :]], axis=1)
        gra, gia = gr_ref[...], gi_ref[...]
        steps = lambda t: jnp.sum(t.reshape(ns, nsteps, R, LANES), axis=1)
        dar_ref[...] += steps(gra * hpr + gia * hpi)
        dai_ref[...] += steps(gia * hpr - gra * hpi)
        parts = []
        for c in range(ncl):
            grc = _slab_get(gr_ref, c, Q, ncl).astype(MXU_DTYPE)
            gic = _slab_get(gi_ref, c, Q, ncl).astype(MXU_DTYPE)
            parts.append(lax.dot_general(grc, wb_ref[c, :, :cs], (((1,), (1,)), ((), ())), preferred_element_type=F32)
                         + lax.dot_general(gic, wb_ref[c, :, cs:], (((1,), (1,)), ((), ())), preferred_element_type=F32))
        du_ref[...] = d_ref[...] * dyv + (parts[0] if ncl == 1 else jnp.concatenate(parts, axis=-1))
        dd_ref[...] += jnp.sum(dyv * x_ref[...], axis=0, keepdims=True)

    tok = pl.BlockSpec((Q, D), lambda i: (rev(i), 0))
    st = pl.BlockSpec((ns, QR, LANES), lambda i: (0, rev(i), 0))
    before = pl.BlockSpec((ns, 8, LANES), lambda i: (0, jnp.maximum(rev(i) * (QR // 8) - 1, 0), 0))
    acc = _whole((ns, R, LANES))
    return _pallas(
        body, name=name, grid=(nchunk,),
        in_specs=[tok, tok, st, st, before, before, _whole(wb.shape), _whole(wc.shape), acc, acc, _whole((1, D))],
        out_specs=[tok, st, st, acc, acc, _whole((1, D))],
        out_shape=[jax.ShapeDtypeStruct((T, D), F32),
                   jax.ShapeDtypeStruct((ns, T * ncl, LANES), F32), jax.ShapeDtypeStruct((ns, T * ncl, LANES), F32),
                   jax.ShapeDtypeStruct((ns, R, LANES), F32), jax.ShapeDtypeStruct((ns, R, LANES), F32),
                   jax.ShapeDtypeStruct((1, D), F32)],
        scratch_shapes=[pltpu.VMEM((ns, QR, LANES), F32)] * 2 + [pltpu.VMEM((ns, R, LANES), F32)] * 2,
        compiler_params=_params(("arbitrary",)),
    )(dy, xi, h_r, h_i, h_r, h_i, wb, wc, a_r, a_i, d_row)


def _cluster_tn(tok, st, ncl, *, tok_left, name):
    T = tok.shape[0]
    ns = st.shape[0]
    cs = ns * LANES
    tt = _pick(T, (512, 256, 128))
    nt = T // tt
    oshape = (ncl, CLUSTER_W, cs) if tok_left else (ncl, cs, CLUSTER_W)

    def body(tok_ref, st_ref, o_ref, acc):
        t = pl.program_id(0)

        @pl.when(t == 0)
        def _():
            acc[...] = jnp.zeros_like(acc)

        tk = tok_ref[...].astype(MXU_DTYPE)
        for c in range(ncl):
            tc = tk[:, c * CLUSTER_W:(c + 1) * CLUSTER_W]
            sc = _slab_get(st_ref, c, tt, ncl).astype(MXU_DTYPE)
            lhs, rhs = (tc, sc) if tok_left else (sc, tc)
            acc[c] += lax.dot_general(lhs, rhs, (((0,), (0,)), ((), ())), preferred_element_type=F32)

        @pl.when(t == nt - 1)
        def _():
            o_ref[...] = acc[...]

    return _pallas(
        body, name=name, grid=(nt,),
        in_specs=[_rows(tt, tok.shape[1]), _slabs(ns, tt * ncl)],
        out_specs=_whole(oshape),
        out_shape=jax.ShapeDtypeStruct(oshape, F32),
        scratch_shapes=[pltpu.VMEM(oshape, F32)],
        compiler_params=_params(("arbitrary",)),
    )(tok, st)


def _s5_discretize(lam_re, lam_im, log_dt, b_re, b_im):
    dt = jnp.exp(log_dt)[:, None]
    mag = jnp.exp(lam_re * dt)
    ab_r, ab_i = mag * jnp.cos(lam_im * dt), mag * jnp.sin(lam_im * dt)
    den = lam_re * lam_re + lam_im * lam_im
    nr = ab_r - 1.0
    co_r = (nr * lam_re + ab_i * lam_im) / den
    co_i = (ab_i * lam_re - nr * lam_im) / den
    bb_r = co_r[..., None] * b_re - co_i[..., None] * b_im
    bb_i = co_r[..., None] * b_im + co_i[..., None] * b_re
    return ab_r, ab_i, bb_r, bb_i


def _blockdiag(m):
    G, r, k = m.shape
    ncl = G // GROUPS_PER_CLUSTER
    m4 = m.reshape(ncl, GROUPS_PER_CLUSTER, r, k)
    eye = jnp.eye(GROUPS_PER_CLUSTER, dtype=m.dtype)
    return jnp.einsum('cgrk,gh->cgrhk', m4, eye).reshape(ncl, GROUPS_PER_CLUSTER * r, GROUPS_PER_CLUSTER * k)


def _unblockdiag(m, r, k):
    ncl = m.shape[0]
    m5 = m.reshape(ncl, GROUPS_PER_CLUSTER, r, GROUPS_PER_CLUSTER, k)
    eye = jnp.eye(GROUPS_PER_CLUSTER, dtype=m.dtype)
    return jnp.einsum('cgrhk,gh->cgrk', m5, eye).reshape(ncl * GROUPS_PER_CLUSTER, r, k)


def _t5_bucket(dist):
    exact = REL_BUCKETS // 2
    d = np.maximum(dist, 1).astype(np.float32)
    large = exact + (np.log(d / exact) / math.log(REL_MAX_DIST / exact) * (REL_BUCKETS - exact)).astype(np.int64)
    large = np.minimum(large, REL_BUCKETS - 1)
    return np.where(dist < exact, dist, large).astype(np.int32)


def _band_tables(dil):
    steps = np.arange(BAND)[:, None] + BAND - np.arange(2 * BAND)[None, :]
    bucket = _t5_bucket(np.maximum(steps, 0) * dil)
    in_band = (steps >= 0) & (steps <= BAND)
    return bucket, in_band


def _attn_bias(rel_bias, hpg):
    out = []
    for g, dil in enumerate(DILATIONS):
        bucket, in_band = _band_tables(dil)
        cols = rel_bias[:, g * hpg:(g + 1) * hpg].astype(F32)
        onehot = jnp.asarray((bucket.reshape(-1, 1) == np.arange(REL_BUCKETS)[None, :]).astype(np.float32))
        bias = jnp.dot(onehot, cols, precision=lax.Precision.HIGHEST).T.reshape(hpg, BAND, 2 * BAND)
        out.append(jnp.where(jnp.asarray(in_band)[None], bias, NEG_BIG))
    return jnp.concatenate(out, axis=0)


def _attn_blocks(dil, L):
    M = L // dil
    return M, M // BAND


def _row_sel(r, M, dil):
    return pl.ds(r, M) if dil == 1 else pl.ds(r, M, stride=dil)


def _attn_fwd(q, kv, bias, L, hpg, *, name):
    T = q.shape[0]
    nb_ = T // L
    HP = hpg // 2
    W3 = 3 * hpg * HEAD_DIM
    mmax = L

    def group_body(dil, q_ref, k_ref, v_ref, b_ref, o_ref, l_ref, os, ls):
        M, NB = _attn_blocks(dil, L)
        first = lax.broadcasted_iota(jnp.int32, (1, 2 * HEAD_DIM), 1) < HEAD_DIM
        loaded = {}

        def load(r):
            rows = _row_sel(r, M, dil)
            qf = q_ref[rows, :] * 0.125
            qm = [jnp.where(first, qf, 0.0).astype(MXU_DTYPE), jnp.where(first, 0.0, qf).astype(MXU_DTYPE)]
            kr = k_ref[rows, :].astype(MXU_DTYPE)
            va = jnp.concatenate([v_ref[rows, :].astype(MXU_DTYPE), jnp.ones((M, 2 * HEAD_DIM), MXU_DTYPE)], axis=-1)
            return qm, kr, va

        tasks = [(r, n) for r in range(dil) for n in range(NB)]
        nbatch = ATTN_BATCH[NB == 1]
        for t0 in range(0, len(tasks), nbatch):
            batch = tasks[t0:t0 + nbatch]
            for r, _ in batch:
                if r not in loaded:
                    loaded[r] = load(r)
            chains = [(r, n, hh) for r, n in batch for hh in range(2)]
            ks = lambda n: slice(0, BAND) if n == 0 else slice((n - 1) * BAND, (n + 1) * BAND)
            s = [lax.dot_general(loaded[r][0][hh][n * BAND:(n + 1) * BAND, :], loaded[r][1][ks(n), :],
                                 (((1,), (1,)), ((), ())), preferred_element_type=F32)
                 + (b_ref[hh, :, BAND:] if n == 0 else b_ref[hh]) for r, n, hh in chains]
            m = [jnp.max(t, axis=-1, keepdims=True) for t in s]
            p = [jnp.exp(t - mm) for t, mm in zip(s, m)]
            pv = [jnp.dot(t.astype(MXU_DTYPE), loaded[r][2][ks(n), :], preferred_element_type=F32)
                  for t, (r, n, hh) in zip(p, chains)]
            l = [t[:, 2 * HEAD_DIM:] for t in pv]
            o_h = [t[:, :2 * HEAD_DIM] / ll for t, ll in zip(pv, l)]
            l_h = [mm + jnp.log(ll) for mm, ll in zip(m, l)]
            for i, (r, n) in enumerate(batch):
                os[r * M + n * BAND:r * M + (n + 1) * BAND, :] = jnp.where(first, o_h[2 * i], o_h[2 * i + 1])
                ls[r * M + n * BAND:r * M + (n + 1) * BAND, :] = jnp.where(first, l_h[2 * i], l_h[2 * i + 1])
                if n == NB - 1:
                    rows = _row_sel(r, M, dil)
                    o_ref[rows, :] = os[r * M:(r + 1) * M, :]
                    l_ref[rows, :] = ls[r * M:(r + 1) * M, :]

    def body(q_ref, k_ref, v_ref, b_ref, o_ref, l_ref, os, ls):
        g = pl.program_id(0)
        for gi, dil in enumerate(DILATIONS):
            pl.when(g == gi)(functools.partial(group_body, dil, q_ref, k_ref, v_ref, b_ref, o_ref, l_ref, os, ls))

    blk = (L, 2 * HEAD_DIM)
    return _pallas(
        body, name=name, grid=(3, nb_, HP),
        in_specs=[pl.BlockSpec(blk, lambda g, b, h: (b, g * HP + h)),
                  pl.BlockSpec(blk, lambda g, b, h: (b, g * HP + h)),
                  pl.BlockSpec(blk, lambda g, b, h: (b, 3 * HP + g * HP + h)),
                  pl.BlockSpec((2, BAND, 2 * BAND), lambda g, b, h: (g * HP + h, 0, 0))],
        out_specs=[pl.BlockSpec(blk, lambda g, b, h: (b, g * HP + h)),
                   pl.BlockSpec(blk, lambda g, b, h: (b, g * HP + h))],
        out_shape=[jax.ShapeDtypeStruct((T, W3), F32), jax.ShapeDtypeStruct((T, W3), F32)],
        scratch_shapes=[pltpu.VMEM((mmax, 2 * HEAD_DIM), F32), pltpu.VMEM((mmax, 2 * HEAD_DIM), F32)],
        compiler_params=_params(("arbitrary", "arbitrary", "arbitrary")),
    )(q, kv, kv, bias)


def _attn_merge(o3, l3, hw, *, name):
    T = o3.shape[0]
    tm = _pick(T, (256, 128))

    def body(o0, o1, o2, l0, l1, l2, o_ref, ob_ref, lse_ref):
        a0, a1, a2 = l0[...], l1[...], l2[...]
        m = jnp.maximum(jnp.maximum(a0, a1), a2)
        e0, e1, e2 = jnp.exp(a0 - m), jnp.exp(a1 - m), jnp.exp(a2 - m)
        z = e0 + e1 + e2
        o = (e0 * o0[...] + e1 * o1[...] + e2 * o2[...]) / z
        o_ref[...] = o
        ob_ref[...] = o.astype(ob_ref.dtype)
        lse_ref[...] = m + jnp.log(z)

    def col(g):
        return pl.BlockSpec((tm, hw), lambda i: (i, g))

    return _pallas(
        body, name=name, grid=(T // tm,),
        in_specs=[col(0), col(1), col(2), col(0), col(1), col(2)],
        out_specs=[_rows(tm, hw)] * 3,
        out_shape=[jax.ShapeDtypeStruct((T, hw), F32), jax.ShapeDtypeStruct((T, hw), MXU_DTYPE),
                   jax.ShapeDtypeStruct((T, hw), F32)],
        compiler_params=_params(("parallel",)),
    )(o3, o3, o3, l3, l3, l3)


def _attn_bwd(q, kv, do, o, lse, bias, L, hpg, *, name):
    T = q.shape[0]
    nb_ = T // L
    HP = hpg // 2
    W3 = 3 * hpg * HEAD_DIM
    mmax = L

    def group_body(dil, q_ref, k_ref, v_ref, do_ref, o_ref, l_ref, b_ref, dq_ref, dk_ref, dv_ref, ds_ref,
                   dqs, dks, dvs):
        M, NB = _attn_blocks(dil, L)
        first = lax.broadcasted_iota(jnp.int32, (1, 2 * HEAD_DIM), 1) < HEAD_DIM
        loaded = {}

        def load(r):
            rows = _row_sel(r, M, dil)
            qf = q_ref[rows, :] * 0.125
            qm = [jnp.where(first, qf, 0.0).astype(MXU_DTYPE), jnp.where(first, 0.0, qf).astype(MXU_DTYPE)]
            kr = k_ref[rows, :].astype(MXU_DTYPE)
            vr = v_ref[rows, :].astype(MXU_DTYPE)
            dof = do_ref[rows, :]
            dom = [jnp.where(first, dof, 0.0).astype(MXU_DTYPE), jnp.where(first, 0.0, dof).astype(MXU_DTYPE)]
            dod = dof * o_ref[rows, :]
            delta = [jnp.sum(jnp.where(first, dod, 0.0), axis=-1, keepdims=True),
                     jnp.sum(jnp.where(first, 0.0, dod), axis=-1, keepdims=True)]
            lr = l_ref[rows, :]
            lse = [lr[:, 0:1], lr[:, HEAD_DIM:HEAD_DIM + 1]]
            if NB > 1:
                dks[r * M:(r + 1) * M, :] = jnp.zeros((M, 2 * HEAD_DIM), F32)
                dvs[r * M:(r + 1) * M, :] = jnp.zeros((M, 2 * HEAD_DIM), F32)
            return qm, kr, vr, dom, delta, lse

        nt = (((1,), (1,)), ((), ()))
        tn = (((0,), (0,)), ((), ()))
        tasks = [(r, n) for r in range(dil) for n in range(NB)]
        nbatch = ATTN_BATCH[NB == 1]
        for t0 in range(0, len(tasks), nbatch):
            batch = tasks[t0:t0 + nbatch]
            for r, _ in batch:
                if r not in loaded:
                    loaded[r] = load(r)
            chains = [(r, n, hh) for r, n in batch for hh in range(2)]
            qs = lambda n: slice(n * BAND, (n + 1) * BAND)
            ks = lambda n: slice(0, BAND) if n == 0 else slice((n - 1) * BAND, (n + 1) * BAND)
            qb = [loaded[r][0][hh][qs(n), :] for r, n, hh in chains]
            dob = [loaded[r][3][hh][qs(n), :] for r, n, hh in chains]
            kb = [loaded[r][1][ks(n), :] for r, n, hh in chains]
            s = [lax.dot_general(a, b, nt, preferred_element_type=F32) + (b_ref[hh, :, BAND:] if n == 0 else b_ref[hh])
                 for a, b, (r, n, hh) in zip(qb, kb, chains)]
            dp = [lax.dot_general(a, loaded[r][2][ks(n), :], nt, preferred_element_type=F32)
                  for a, (r, n, hh) in zip(dob, chains)]
            p = [jnp.exp(t - loaded[r][5][hh][qs(n), :]) for t, (r, n, hh) in zip(s, chains)]
            ds = [a * (b - loaded[r][4][hh][qs(n), :]) for a, b, (r, n, hh) in zip(p, dp, chains)]
            for t, (r, n, hh) in zip(ds, chains):
                if n == 0:
                    ds_ref[hh, :, BAND:] += t
                else:
                    ds_ref[hh] += t
            dsm = [t.astype(MXU_DTYPE) for t in ds]
            dq = [jnp.dot(a, b, preferred_element_type=F32) for a, b in zip(dsm, kb)]
            dk = [lax.dot_general(a, b, tn, preferred_element_type=F32) for a, b in zip(dsm, qb)]
            dv = [lax.dot_general(a.astype(MXU_DTYPE), b, tn, preferred_element_type=F32) for a, b in zip(p, dob)]
            for i, (r, n) in enumerate(batch):
                dqs[r * M + n * BAND:r * M + (n + 1) * BAND, :] = jnp.where(first, dq[2 * i], dq[2 * i + 1]) * 0.125
                ksm = slice(r * M + ks(n).start, r * M + ks(n).stop)
                if NB > 1:
                    dks[ksm, :] += dk[2 * i] + dk[2 * i + 1]
                    dvs[ksm, :] += dv[2 * i] + dv[2 * i + 1]
                else:
                    dks[ksm, :] = dk[2 * i] + dk[2 * i + 1]
                    dvs[ksm, :] = dv[2 * i] + dv[2 * i + 1]
                if n == NB - 1:
                    rows = _row_sel(r, M, dil)
                    dq_ref[rows, :] = dqs[r * M:(r + 1) * M, :]
                    dk_ref[rows, :] = dks[r * M:(r + 1) * M, :]
                    dv_ref[rows, :] = dvs[r * M:(r + 1) * M, :]

    def body(q_ref, k_ref, v_ref, do_ref, o_ref, l_ref, b_ref, dq_ref, dk_ref, dv_ref, ds_ref, dqs, dks, dvs):
        g = pl.program_id(0)

        @pl.when(pl.program_id(2) == 0)
        def _():
            ds_ref[...] = jnp.zeros_like(ds_ref)

        for gi, dil in enumerate(DILATIONS):
            pl.when(g == gi)(functools.partial(group_body, dil, q_ref, k_ref, v_ref, do_ref, o_ref, l_ref, b_ref,
                                               dq_ref, dk_ref, dv_ref, ds_ref, dqs, dks, dvs))

    blk = (L, 2 * HEAD_DIM)
    gcol = lambda g, h, b: (b, g * HP + h)
    hcol = lambda g, h, b: (b, h)
    return _pallas(
        body, name=name, grid=(3, HP, nb_),
        in_specs=[pl.BlockSpec(blk, gcol), pl.BlockSpec(blk, gcol),
                  pl.BlockSpec(blk, lambda g, h, b: (b, 3 * HP + g * HP + h)),
                  pl.BlockSpec(blk, hcol), pl.BlockSpec(blk, hcol), pl.BlockSpec(blk, hcol),
                  pl.BlockSpec((2, BAND, 2 * BAND), lambda g, h, b: (g * HP + h, 0, 0))],
        out_specs=[pl.BlockSpec(blk, gcol), pl.BlockSpec(blk, gcol), pl.BlockSpec(blk, gcol),
                   pl.BlockSpec((2, BAND, 2 * BAND), lambda g, h, b: (g * HP + h, 0, 0))],
        out_shape=[jax.ShapeDtypeStruct((T, W3), F32), jax.ShapeDtypeStruct((T, W3), F32),
                   jax.ShapeDtypeStruct((T, W3), F32), jax.ShapeDtypeStruct((3 * hpg, BAND, 2 * BAND), F32)],
        scratch_shapes=[pltpu.VMEM((mmax, 2 * HEAD_DIM), F32)] * 3,
        compiler_params=_params(("arbitrary", "arbitrary", "arbitrary")),
    )(q, kv, kv, do, o, lse, bias)


def _bias_grad(ds_sum, hpg, *, name):
    nh = ds_sum.shape[0]
    idx = np.stack([np.where(_band_tables(dil)[1], _band_tables(dil)[0], -1) for dil in DILATIONS]).astype(np.int32)

    def body(ds_ref, idx_ref, o_ref):
        d = ds_ref[...]
        ix = idx_ref[...]
        lane = lax.broadcasted_iota(jnp.int32, (8, 128), 1)
        row = jnp.zeros((8, 128), F32)
        for b in range(REL_BUCKETS):
            row = row + jnp.where(lane == b, jnp.sum(jnp.where(ix == b, d, 0.0)), 0.0)
        o_ref[...] = row

    out = _pallas(
        body, name=name, grid=(nh,),
        in_specs=[pl.BlockSpec((None, BAND, 2 * BAND), lambda h: (h, 0, 0)),
                  pl.BlockSpec((None, BAND, 2 * BAND), lambda h: (h // hpg, 0, 0))],
        out_specs=pl.BlockSpec((None, 8, 128), lambda h: (h, 0, 0)),
        out_shape=jax.ShapeDtypeStruct((nh, 8, 128), F32),
        compiler_params=_params(("parallel",)),
    )(ds_sum, jnp.asarray(idx))
    return out[:, 0, :REL_BUCKETS].T


def _adamw(w, g, m, v, *, name):
    Rw, C = w.shape
    tm = _pick(Rw, (512, 352, 256, 128, 64, 32, 16, 8))

    def body(w_ref, g_ref, m_ref, v_ref, d_ref, nm_ref, nv_ref):
        gg = g_ref[...]
        nm = ADAM_B1 * m_ref[...] + (1.0 - ADAM_B1) * gg
        nv = ADAM_B2 * v_ref[...] + (1.0 - ADAM_B2) * (gg * gg)
        m_hat = nm / (1.0 - ADAM_B1 ** ADAM_STEP)
        v_hat = nv / (1.0 - ADAM_B2 ** ADAM_STEP)
        d_ref[...] = -ADAM_LR * (m_hat / (jnp.sqrt(v_hat) + ADAM_EPS) + ADAM_WD * w_ref[...])
        nm_ref[...] = nm
        nv_ref[...] = nv

    return _pallas(
        body, name=name, grid=(Rw // tm,), in_specs=[_rows(tm, C)] * 4, out_specs=[_rows(tm, C)] * 3,
        out_shape=[jax.ShapeDtypeStruct((Rw, C), F32)] * 3, compiler_params=_params(("parallel",)),
    )(w, g, m, v)


ROW_TILE_ELEMS = 256 * 1024


def _tile_rows(r, c):
    best = 8
    for t in range(8, r + 1, 8):
        if r % t == 0 and t * c <= ROW_TILE_ELEMS:
            best = t
    return best


def _adamw_halves(w, m, v, mine, other, cidx, *, layer=0, prev=None, name):
    NL, _, r, c = w.shape
    tm = _tile_rows(r, c)

    def body(c_ref, w_ref, m_ref, v_ref, a_ref, b_ref, *rest):
        g_ref, d_ref, nm_ref, nv_ref = rest[-4:]
        gg = jnp.where(pl.program_id(0) == c_ref[0], a_ref[...], b_ref[...])
        nm = ADAM_B1 * m_ref[...] + (1.0 - ADAM_B1) * gg
        nv = ADAM_B2 * v_ref[...] + (1.0 - ADAM_B2) * (gg * gg)
        m_hat = nm / (1.0 - ADAM_B1 ** ADAM_STEP)
        v_hat = nv / (1.0 - ADAM_B2 ** ADAM_STEP)
        g_ref[...] = gg
        d_ref[...] = -ADAM_LR * (m_hat / (jnp.sqrt(v_hat) + ADAM_EPS) + ADAM_WD * w_ref[...])
        nm_ref[...] = nm
        nv_ref[...] = nv

    half = pl.BlockSpec((None, None, tm, c), lambda h, i, cr: (layer, h, i, 0))
    one = pl.BlockSpec((None, tm, c), lambda h, i, cr: (0, i, 0))
    in_specs = [half, half, half, one, one]
    args = [cidx, w, m, v, mine, other]
    aliases = {}
    if prev is not None:
        in_specs += [_ANY] * 4
        args += list(prev)
        aliases = {6 + k: k for k in range(4)}
    spec = pltpu.PrefetchScalarGridSpec(num_scalar_prefetch=1, grid=(2, r // tm), in_specs=in_specs, out_specs=[half] * 4)
    return _pallas(
        body, name=name, grid_spec=spec, out_shape=[jax.ShapeDtypeStruct((NL, 2, r, c), F32)] * 4,
        input_output_aliases=aliases, compiler_params=_params(("parallel", "parallel")),
    )(*args)


def _pair_sum(g, theirs, cidx, *, cast, name):
    _, _, r, c = g.shape
    tm = _tile_rows(r, c)

    def body(c_ref, g_ref, t_ref, *outs):
        s = g_ref[...] + t_ref[...]
        outs[0][...] = s
        if cast:
            outs[1][...] = s.astype(BF16)

    blk = (None, None, tm, c)
    first = pl.BlockSpec(blk, lambda p, i, cr: (p, 0, i, 0))
    shapes = [jax.ShapeDtypeStruct((4, 1, r, c), F32)] + ([jax.ShapeDtypeStruct((4, 1, r, c), BF16)] if cast else [])
    spec = pltpu.PrefetchScalarGridSpec(
        num_scalar_prefetch=1, grid=(4, r // tm),
        in_specs=[pl.BlockSpec(blk, lambda p, i, cr: (p, cr[0], i, 0)), first], out_specs=[first] * len(shapes))
    return _pallas(body, name=name, grid_spec=spec, out_shape=shapes,
                   compiler_params=_params(("parallel", "parallel")))(cidx, g, theirs)


def _chip_sum(hf, got, chip_idx, *, name):
    _, _, r, c = hf.shape
    tm = _tile_rows(r, c)

    def body(p_ref, h_ref, r_ref, o_ref):
        s = h_ref[...]
        for k in range(3):
            s = s + r_ref[k].astype(F32)
        o_ref[...] = s

    spec = pltpu.PrefetchScalarGridSpec(
        num_scalar_prefetch=1, grid=(r // tm,),
        in_specs=[pl.BlockSpec((None, None, tm, c), lambda i, pr: (pr[0], 0, i, 0)),
                  pl.BlockSpec((3, None, tm, c), lambda i, pr: (0, 0, i, 0))],
        out_specs=pl.BlockSpec((None, tm, c), lambda i, pr: (0, i, 0)))
    return _pallas(body, name=name, grid_spec=spec, out_shape=jax.ShapeDtypeStruct((1, r, c), F32),
                   compiler_params=_params(("parallel",)))(chip_idx, hf, got)


def _place():
    x, y, c = lax.axis_index("x"), lax.axis_index("y"), lax.axis_index("c")
    chips = [(1 - x, y), (x, 1 - y), (1 - x, 1 - y)]
    return x, y, c, chips


_ANY = pl.BlockSpec(memory_space=pl.ANY)


def _comm_call(body, ins, out_shapes, n_remote, *, name, aliases=None):
    sems = [pltpu.SemaphoreType.DMA((n,)) for n in n_remote]
    return _pallas(
        body, name=name, in_specs=[_ANY] * len(ins), out_specs=[_ANY] * len(out_shapes), out_shape=out_shapes,
        scratch_shapes=sems, input_output_aliases=aliases or {},
        compiler_params=pltpu.CompilerParams(has_side_effects=True),
    )(*ins)


_HBM_SPEC = pl.BlockSpec(memory_space=pltpu.HBM)
_SEM_SPEC = pl.BlockSpec(memory_space=pltpu.SEMAPHORE)
_DATAFLOW = pltpu.SideEffectType.DATAFLOW_SIDE_EFFECTING


def _in_hbm(a):
    return pltpu.with_memory_space_constraint(a, pltpu.HBM)


def _gather_start(groups, *, name):
    flat = [s for g in groups for s in g]
    n, ng = len(flat), len(groups)

    def body(*refs):
        ins, lands = refs[:n], refs[n:2 * n]
        sems = refs[2 * n:2 * n + 2 * ng]
        token = refs[-1]
        x, y, c, chips = _place()
        me = 2 * x + y
        a = 0
        for gi, g in enumerate(groups):
            for j in range(len(g)):
                for k, (tx, ty) in enumerate(chips):
                    _rcopy(ins[a].at[c], lands[a].at[me, c], sems[2 * gi].at[3 * j + k], sems[2 * gi + 1].at[3 * j + k],
                           (tx, ty, c)).start()
                a += 1
        token[...] = jnp.zeros_like(token)

    land_shapes = [(4,) + s.shape for s in flat]
    out_shape = ([pltpu.SemaphoreType.DMA((3 * len(g),)) for g in groups for _ in range(2)]
                 + [pltpu.HBM(s.shape, s.dtype) for s in flat]
                 + [pltpu.HBM(ls, s.dtype) for ls, s in zip(land_shapes, flat)]
                 + [jax.ShapeDtypeStruct((8, 128), F32)])
    outs = _pallas(
        body, name=name, in_specs=[_HBM_SPEC] * (2 * n),
        out_specs=[_SEM_SPEC] * (2 * ng) + [_HBM_SPEC] * (2 * n) + [pl.BlockSpec(memory_space=pltpu.VMEM)],
        out_shape=out_shape, input_output_aliases={i: 2 * ng + i for i in range(2 * n)},
        compiler_params=pltpu.CompilerParams(has_side_effects=_DATAFLOW),
    )(*[_in_hbm(s) for s in flat], *[_in_hbm(lax.empty(ls, s.dtype)) for ls, s in zip(land_shapes, flat)])
    sems, thru, lands, token = outs[:2 * ng], outs[2 * ng:2 * ng + n], outs[2 * ng + n:2 * ng + 2 * n], outs[-1]
    res, a = [], 0
    for gi, g in enumerate(groups):
        res.append((sems[2 * gi], sems[2 * gi + 1], thru[a:a + len(g)], lands[a:a + len(g)]))
        a += len(g)
    return res, token


def _gather_wait(ssem, rsem, shards, lands, after, *, name):
    m = len(shards)

    def body(*refs):
        ins, lnd = refs[:m], refs[m:2 * m]
        ss, rs = refs[2 * m], refs[2 * m + 1]
        x, y, c, chips = _place()
        for j in range(m):
            for k, (tx, ty) in enumerate(chips):
                cp = _rcopy(ins[j].at[c], lnd[j].at[2 * tx + ty, c], ss.at[3 * j + k], rs.at[3 * j + k], (tx, ty, c))
                cp.wait_send()
                cp.wait_recv()

    outs = _pallas(
        body, name=name, in_specs=[_HBM_SPEC] * (2 * m) + [_SEM_SPEC, _SEM_SPEC, _ANY],
        out_specs=[_HBM_SPEC] * (2 * m),
        out_shape=[pltpu.HBM(s.shape, s.dtype) for s in shards] + [pltpu.HBM(l.shape, l.dtype) for l in lands],
        input_output_aliases={i: i for i in range(2 * m)},
        compiler_params=pltpu.CompilerParams(has_side_effects=_DATAFLOW),
    )(*shards, *lands, ssem, rsem, after)
    return outs[m:]


def _gather_forward(lands, *, name):
    n = len(lands)

    def body(*refs):
        outs = refs[n:2 * n]
        ssem, rsem = refs[2 * n:]
        x, y, c, chips = _place()
        sib = (x, y, 1 - c)
        cps = []
        for a in range(n):
            for k, (tx, ty) in enumerate(chips):
                pk = 2 * tx + ty
                cp = _rcopy(outs[a].at[pk, c], outs[a].at[pk, c], ssem.at[3 * a + k], rsem.at[3 * a + k], sib)
                cp.start()
                cps.append(cp)
        for a in range(n):
            for k, (tx, ty) in enumerate(chips):
                pk = 2 * tx + ty
                _rcopy(outs[a].at[pk, c], outs[a].at[pk, 1 - c], ssem.at[3 * a + k], rsem.at[3 * a + k], sib).wait_recv()
        for cp in cps:
            cp.wait_send()

    shapes = [jax.ShapeDtypeStruct(l.shape, l.dtype) for l in lands]
    return _comm_call(body, lands, shapes, [3 * n, 3 * n], name=name, aliases={i: i for i in range(n)})


def _gather_forward_start(lands, *, name):
    n = len(lands)

    def body(*refs):
        ssem, rsem = refs[n], refs[n + 1]
        outs = refs[n + 2:2 * n + 2]
        token = refs[-1]
        x, y, c, chips = _place()
        for a in range(n):
            for k, (tx, ty) in enumerate(chips):
                pk = 2 * tx + ty
                _rcopy(outs[a].at[pk, c], outs[a].at[pk, c], ssem.at[3 * a + k], rsem.at[3 * a + k], (x, y, 1 - c)).start()
        token[...] = jnp.zeros_like(token)

    outs = _pallas(
        body, name=name, in_specs=[_HBM_SPEC] * n,
        out_specs=[_SEM_SPEC] * 2 + [_HBM_SPEC] * n + [pl.BlockSpec(memory_space=pltpu.VMEM)],
        out_shape=([pltpu.SemaphoreType.DMA((3 * n,))] * 2 + [pltpu.HBM(l.shape, l.dtype) for l in lands]
                   + [jax.ShapeDtypeStruct((8, 128), F32)]),
        input_output_aliases={i: 2 + i for i in range(n)},
        compiler_params=pltpu.CompilerParams(has_side_effects=_DATAFLOW),
    )(*lands)
    return (outs[0], outs[1], outs[2:2 + n]), outs[-1]


def _gather_forward_wait(started, after, *, name):
    ssem, rsem, lands = started
    n = len(lands)

    def body(*refs):
        lnd = refs[:n]
        ss, rs = refs[n], refs[n + 1]
        x, y, c, chips = _place()
        sib = (x, y, 1 - c)
        for a in range(n):
            for k, (tx, ty) in enumerate(chips):
                pk = 2 * tx + ty
                _rcopy(lnd[a].at[pk, c], lnd[a].at[pk, 1 - c], ss.at[3 * a + k], rs.at[3 * a + k], sib).wait_recv()
                _rcopy(lnd[a].at[pk, c], lnd[a].at[pk, c], ss.at[3 * a + k], rs.at[3 * a + k], sib).wait_send()

    return _pallas(
        body, name=name, in_specs=[_HBM_SPEC] * n + [_SEM_SPEC, _SEM_SPEC, _ANY], out_specs=[_HBM_SPEC] * n,
        out_shape=[pltpu.HBM(l.shape, l.dtype) for l in lands], input_output_aliases={i: i for i in range(n)},
        compiler_params=pltpu.CompilerParams(has_side_effects=_DATAFLOW),
    )(*lands, ssem, rsem, after)


class _Lazy:
    def __init__(self, group_of, make, prepare):
        self._group_of, self._make, self._prepare, self._done, self._anchor = group_of, make, prepare, {}, None

    def anchor(self, value):
        self._anchor = value

    def prepare(self, key, value):
        return self._prepare(self._group_of[key], value)

    def __getitem__(self, key):
        g = self._group_of[key]
        if g not in self._done:
            self._done[g] = self._make(g, self._anchor)
        return self._done[g][key]


def _anchor(mapping, value):
    if isinstance(mapping, _Lazy):
        mapping.anchor(value)


def _prepare(mapping, key, value):
    return mapping.prepare(key, value)[0, 0] if isinstance(mapping, _Lazy) else 0.0


def _rcopy(src, dst, ssem, rsem, dev):
    return pltpu.make_async_remote_copy(src_ref=src, dst_ref=dst, send_sem=ssem, recv_sem=rsem,
                                        device_id=dev, device_id_type=MESH)


def _all_gather(shards, *, name):
    n = len(shards)

    def body(*refs):
        ins, outs = refs[:n], refs[n:2 * n]
        s_ici, r_ici, s_d2d, r_d2d = refs[2 * n:]
        x, y, c, chips = _place()
        me = 2 * x + y
        sib = (x, y, 1 - c)
        sends = []
        for a in range(n):
            for k, (tx, ty) in enumerate(chips):
                cp = _rcopy(ins[a].at[c], outs[a].at[me, c], s_ici.at[3 * a + k], r_ici.at[3 * a + k], (tx, ty, c))
                cp.start()
                sends.append(cp)
        for a in range(n):
            for k, (tx, ty) in enumerate(chips):
                pk = 2 * tx + ty
                _rcopy(ins[a].at[c], outs[a].at[pk, c], s_ici.at[3 * a + k], r_ici.at[3 * a + k], (tx, ty, c)).wait_recv()
                fw = _rcopy(outs[a].at[pk, c], outs[a].at[pk, c], s_d2d.at[3 * a + k], r_d2d.at[3 * a + k], sib)
                fw.start()
                sends.append(fw)
        for a in range(n):
            for k, (tx, ty) in enumerate(chips):
                pk = 2 * tx + ty
                _rcopy(ins[a].at[c], outs[a].at[pk, 1 - c], s_d2d.at[3 * a + k], r_d2d.at[3 * a + k], sib).wait_recv()
        for cp in sends:
            cp.wait_send()

    shapes = [jax.ShapeDtypeStruct((4,) + s.shape, s.dtype) for s in shards]
    return _comm_call(body, shards, shapes, [3 * n] * 4, name=name)


def _gather(shards, chip, *, name):
    outs = _all_gather(shards, name=name)
    return [lax.dynamic_update_slice(o, s[None], (chip, 0, 0, 0)) for o, s in zip(outs, shards)]


def _pair_send(gs, *, name):
    n = len(gs)

    def body(*refs):
        ins, theirs = refs[:n], refs[n:2 * n]
        ssem, rsem = refs[2 * n:]
        x, y, c, _ = _place()
        sib = (x, y, 1 - c)
        cps = []
        for a in range(n):
            cp = _rcopy(ins[a].at[:, pl.ds(1 - c, 1)], theirs[a], ssem.at[a], rsem.at[a], sib)
            cp.start()
            cps.append(cp)
        for cp in cps:
            cp.wait_send()
            cp.wait_recv()

    shapes = [jax.ShapeDtypeStruct((4, 1) + g.shape[2:], g.dtype) for g in gs]
    return _comm_call(body, gs, shapes, [n, n], name=name)


def _chip_exchange(hx, *, name):
    n = len(hx)

    def body(*refs):
        hxr, got = refs[:n], refs[n:2 * n]
        ssem, rsem = refs[2 * n:]
        x, y, c, chips = _place()
        cps = []
        for a in range(n):
            for k, (tx, ty) in enumerate(chips):
                cp = _rcopy(hxr[a].at[2 * tx + ty], got[a].at[k], ssem.at[3 * a + k], rsem.at[3 * a + k], (tx, ty, c))
                cp.start()
                cps.append(cp)
        for cp in cps:
            cp.wait_send()
            cp.wait_recv()

    shapes = [jax.ShapeDtypeStruct((3,) + h.shape[1:], h.dtype) for h in hx]
    return _comm_call(body, hx, shapes, [3 * n, 3 * n], name=name)


def _pair_swap(fs, *, name):
    n = len(fs)

    def body(*refs):
        ins, outs = refs[:n], refs[n:2 * n]
        ssem, rsem = refs[2 * n:]
        x, y, c, _ = _place()
        cps = []
        for a in range(n):
            cp = _rcopy(ins[a], outs[a], ssem.at[a], rsem.at[a], (x, y, 1 - c))
            cp.start()
            cps.append(cp)
        for cp in cps:
            cp.wait_send()
            cp.wait_recv()

    shapes = [jax.ShapeDtypeStruct(f.shape, f.dtype) for f in fs]
    return _comm_call(body, fs, shapes, [n, n], name=name)


def _chip_exchange_start(hx, *, name):
    n = len(hx)

    def body(*refs):
        ins, gots = refs[:n], refs[n:2 * n]
        ssem, rsem = refs[2 * n], refs[2 * n + 1]
        token = refs[-1]
        x, y, c, chips = _place()
        for a in range(n):
            for k, (tx, ty) in enumerate(chips):
                _rcopy(ins[a].at[2 * tx + ty], gots[a].at[k], ssem.at[3 * a + k], rsem.at[3 * a + k], (tx, ty, c)).start()
        token[...] = jnp.zeros_like(token)

    got_shapes = [(3,) + h.shape[1:] for h in hx]
    outs = _pallas(
        body, name=name, in_specs=[_HBM_SPEC] * (2 * n),
        out_specs=[_SEM_SPEC] * 2 + [_HBM_SPEC] * (2 * n) + [pl.BlockSpec(memory_space=pltpu.VMEM)],
        out_shape=([pltpu.SemaphoreType.DMA((3 * n,))] * 2 + [pltpu.HBM(h.shape, h.dtype) for h in hx]
                   + [pltpu.HBM(gs, h.dtype) for gs, h in zip(got_shapes, hx)] + [jax.ShapeDtypeStruct((8, 128), F32)]),
        input_output_aliases={i: 2 + i for i in range(2 * n)},
        compiler_params=pltpu.CompilerParams(has_side_effects=_DATAFLOW),
    )(*[_in_hbm(h) for h in hx], *[_in_hbm(lax.empty(gs, h.dtype)) for gs, h in zip(got_shapes, hx)])
    return (outs[0], outs[1], outs[2:2 + n], outs[2 + n:2 + 2 * n]), outs[-1]


def _chip_exchange_wait(started, after, *, name):
    ssem, rsem, hx, gots = started
    n = len(hx)

    def body(*refs):
        ins, gts = refs[:n], refs[n:2 * n]
        ss, rs = refs[2 * n], refs[2 * n + 1]
        x, y, c, chips = _place()
        for a in range(n):
            for k, (tx, ty) in enumerate(chips):
                cp = _rcopy(ins[a].at[2 * tx + ty], gts[a].at[k], ss.at[3 * a + k], rs.at[3 * a + k], (tx, ty, c))
                cp.wait_send()
                cp.wait_recv()

    outs = _pallas(
        body, name=name, in_specs=[_HBM_SPEC] * (2 * n) + [_SEM_SPEC, _SEM_SPEC, _ANY],
        out_specs=[_HBM_SPEC] * (2 * n),
        out_shape=[pltpu.HBM(h.shape, h.dtype) for h in hx] + [pltpu.HBM(g.shape, g.dtype) for g in gots],
        input_output_aliases={i: i for i in range(2 * n)},
        compiler_params=pltpu.CompilerParams(has_side_effects=_DATAFLOW),
    )(*hx, *gots, ssem, rsem, after)
    return outs[n:]


def _sent_part(ref, c, whole):
    return ref if whole else ref.at[:, pl.ds(1 - c, 1)]


def _pair_send_start(gs, *, name, whole=False):
    n = len(gs)

    def body(*refs):
        ins, lands = refs[:n], refs[n:2 * n]
        ssem, rsem = refs[2 * n], refs[2 * n + 1]
        token = refs[-1]
        x, y, c, _ = _place()
        for a in range(n):
            _rcopy(_sent_part(ins[a], c, whole), lands[a], ssem.at[a], rsem.at[a], (x, y, 1 - c)).start()
        token[...] = jnp.zeros_like(token)

    land_shapes = [g.shape if whole else (4, 1) + g.shape[2:] for g in gs]
    outs = _pallas(
        body, name=name, in_specs=[_HBM_SPEC] * (2 * n),
        out_specs=[_SEM_SPEC] * 2 + [_HBM_SPEC] * (2 * n) + [pl.BlockSpec(memory_space=pltpu.VMEM)],
        out_shape=([pltpu.SemaphoreType.DMA((n,))] * 2 + [pltpu.HBM(g.shape, g.dtype) for g in gs]
                   + [pltpu.HBM(ls, g.dtype) for ls, g in zip(land_shapes, gs)] + [jax.ShapeDtypeStruct((8, 128), F32)]),
        input_output_aliases={i: 2 + i for i in range(2 * n)},
        compiler_params=pltpu.CompilerParams(has_side_effects=_DATAFLOW),
    )(*[_in_hbm(g) for g in gs], *[_in_hbm(lax.empty(ls, g.dtype)) for ls, g in zip(land_shapes, gs)])
    return (outs[0], outs[1], outs[2:2 + n], outs[2 + n:2 + 2 * n]), outs[-1]


def _pair_send_wait(started, after, *, name, whole=False):
    ssem, rsem, gs, lands = started
    n = len(gs)

    def body(*refs):
        ins, lnd = refs[:n], refs[n:2 * n]
        ss, rs = refs[2 * n], refs[2 * n + 1]
        x, y, c, _ = _place()
        for a in range(n):
            cp = _rcopy(_sent_part(ins[a], c, whole), lnd[a], ss.at[a], rs.at[a], (x, y, 1 - c))
            cp.wait_send()
            cp.wait_recv()

    outs = _pallas(
        body, name=name, in_specs=[_HBM_SPEC] * (2 * n) + [_SEM_SPEC, _SEM_SPEC, _ANY],
        out_specs=[_HBM_SPEC] * (2 * n),
        out_shape=[pltpu.HBM(g.shape, g.dtype) for g in gs] + [pltpu.HBM(l.shape, l.dtype) for l in lands],
        input_output_aliases={i: i for i in range(2 * n)},
        compiler_params=pltpu.CompilerParams(has_side_effects=_DATAFLOW),
    )(*gs, *lands, ssem, rsem, after)
    return list(outs[:n]), list(outs[n:])


def _pair_sums(grads, exch_bf16, cidx, tag, theirs=None):
    if theirs is None:
        theirs = _pair_send(grads, name=f"rs_pair_send_{tag}")
    hf, hx = [], []
    for a in range(len(grads)):
        res = _pair_sum(grads[a], theirs[a], cidx, cast=exch_bf16[a], name=f"rs_pair_sum_{tag}{a}")
        hf.append(res[0])
        hx.append(res[1] if exch_bf16[a] else res[0])
    return hf, hx


def _chip_sums(hf, got, chip_idx, tag):
    return [_chip_sum(hf[a], got[a], chip_idx, name=f"rs_chip_sum_{tag}{a}") for a in range(len(hf))]


def _interleave(a, B, L):
    return a.reshape(B, L, -1).transpose(1, 0, 2).reshape(B * L, -1)


def _deinterleave(a, B, L):
    return a.reshape(L, B, -1).transpose(1, 0, 2).reshape(B * L, -1)


def _local_step(x, tgt, W, S, on_grads=None):
    B, L, D = x.shape
    T = B * L
    G = D // SSM_GROUP
    Pst = SSM_STATE
    hpg = D // HEAD_DIM
    HW = hpg * HEAD_DIM
    ncl = G // GROUPS_PER_CLUSTER
    x2 = x.reshape(T, D)
    tgt2 = tgt.reshape(T, D)

    disc = lambda *p: _s5_discretize(*p)
    (ab_r, ab_i, bb_r, bb_i), disc_vjp = jax.vjp(disc, S["lam_re"], S["lam_im"], S["log_dt"], S["b_re"], S["b_im"])
    wb = jnp.concatenate([_blockdiag(jnp.transpose(bb_r, (0, 2, 1))), _blockdiag(jnp.transpose(bb_i, (0, 2, 1)))],
                         axis=-1).astype(MXU_DTYPE)
    wc = jnp.concatenate([_blockdiag(jnp.transpose(S["c_re"], (0, 2, 1))), _blockdiag(-jnp.transpose(S["c_im"], (0, 2, 1)))],
                         axis=1).astype(MXU_DTYPE)
    cs = GROUPS_PER_CLUSTER * Pst
    slab = lambda ab: jnp.tile(jnp.transpose(ab.reshape(ncl, cs // LANES, LANES), (1, 0, 2)), (1, B, 1))
    a_r, a_i = slab(ab_r), slab(ab_i)
    d_row = S["d"].reshape(1, D)

    xi = _interleave(x2, B, L)
    y, yg, h_r, h_i = _s5_fwd(xi, wb, wc, a_r, a_i, d_row, B, name="s5_fwd")
    _anchor(W, yg)
    z = _mm_nn(yg, W["w_glu"], bias=S["b_glu"].reshape(1, D), name="glu_z")
    gate = _glu_gate(y, z, name="glu_gate")
    mix_i = _mm_nn(gate, W["w_out"], name="s5_out")
    tok = _prepare(W, "w_up", mix_i)
    mix = _deinterleave(mix_i, B, L)
    h1, h1b, xh1, rs1 = _ln_fwd(x2, mix, S["ln_gain"][0, 0][None] + tok, S["ln_bias"][0, 0][None], name="ln_fwd_0a")

    def ffn_fwd(hb, l, prepare=None):
        hc = _mm_nn(hb, W["w_up"], l=l, out_dtype=MXU_DTYPE, name=f"ffn_up_{l}")
        tok = _prepare(W, prepare, hc) if prepare else 0.0
        a = _conv_glu_fwd(hc, S["conv_w"][l], S["conv_b"][l][None] + tok, L, name=f"ffn_conv_{l}")
        f = _mm_nn(a, W["w_down"], l=l, name=f"ffn_down_{l}")
        return hc, a, f

    _anchor(W, h1b)
    hc0, a0, f0 = ffn_fwd(h1b, 0, prepare="w_kv")
    h2, h2b, xh2, rs2 = _ln_fwd(h1, f0, S["ln_gain"][0, 1][None], S["ln_bias"][0, 1][None], name="ln_fwd_0b")

    _anchor(W, h2b)
    kv = _mm_nn(h2b, W["w_kv"], name="attn_kv")
    q = _mm_nn(h2b, W["w_q"], name="attn_q")
    bias = _attn_bias(S["rel_bias"], hpg)
    o3, l3 = _attn_fwd(q, kv, bias, L, hpg, name="attn_fwd")
    o, ob, lse = _attn_merge(o3, l3, HW, name="attn_merge")
    att = _mm_nn(ob, W["w_ao"], name="attn_out")
    h3, h3b, xh3, rs3 = _ln_fwd(h2, att, S["ln_gain"][1, 0][None], S["ln_bias"][1, 0][None], name="ln_fwd_1a")
    hc1, a1, f1 = ffn_fwd(h3b, 1)
    h4, _, xh4, rs4 = _ln_fwd(h3, f1, S["ln_gain"][1, 1][None], S["ln_bias"][1, 1][None], name="ln_fwd_1b")

    dh4, lrow = _loss_grad(h4, tgt2, name="loss")
    loss = lrow[0, 0]

    GW, GS = {}, {}

    def ffn_bwd(dzb, hb, hc, a, l):
        da = _mm_nt(dzb, W["w_down"], l=l, out_dtype=MXU_DTYPE, name=f"ffn_down_bwd_x_{l}")
        GW[f"w_down{l}"] = _tn(a, dzb, ptotal=1, np_cols=D, name=f"ffn_down_bwd_w_{l}")
        dc, dcw, dcb = _conv_glu_bwd(hc, da, S["conv_w"][l], S["conv_b"][l][None], L, name=f"ffn_conv_bwd_{l}")
        dhc = _conv_bwd_input(dc, S["conv_w"][l], L, name=f"ffn_conv_bwd_x_{l}")
        dh = _mm_nt(dhc, W["w_up"], l=l, name=f"ffn_up_bwd_x_{l}")
        GW[f"w_up{l}"] = _tn(hb, dhc, ptotal=W["w_up"].shape[0], np_cols=W["w_up"].shape[3], name=f"ffn_up_bwd_w_{l}")
        return dh, dcw, dcb

    dz4, dz4b, dg4, db4 = _ln_bwd([dh4], [1.0], xh4, rs4, S["ln_gain"][1, 1][None], name="ln_bwd_1b")
    dh3f, dcw1, dcb1 = ffn_bwd(dz4b, h3b, hc1, a1, 1)
    dz3, dz3b, dg3, db3 = _ln_bwd([dz4, dh3f], [DN_ALPHA, 1.0], xh3, rs3, S["ln_gain"][1, 0][None], name="ln_bwd_1a")
    do = _mm_nt(dz3b, W["w_ao"], name="attn_out_bwd_x")
    GW["w_ao"] = _tn(ob, dz3b, ptotal=1, np_cols=D, name="attn_out_bwd_w")
    dq, dk, dv, ds_sum = _attn_bwd(q, kv, do, o, lse, bias, L, hpg, name="attn_bwd")
    GS["rel_bias"] = _bias_grad(ds_sum, hpg, name="attn_bias_grad")
    GW["w_q"] = _tn(h2b, dq, ptotal=W["w_q"].shape[0], np_cols=W["w_q"].shape[3], name="attn_q_bwd_w")
    pkv, npkv = W["w_kv"].shape[0], W["w_kv"].shape[3]
    gkv = _tn(h2b, dk, ptotal=pkv, np_cols=npkv, p0=0, name="attn_k_bwd_w")
    GW["w_kv"] = _tn(h2b, dv, ptotal=pkv, np_cols=npkv, p0=pkv // 2, prev=gkv, name="attn_v_bwd_w")
    dh2q = _mm_nt(dq, W["w_q"], name="attn_q_bwd_x")
    dh2k = _mm_nt(dk, W["w_kv"], p0=0, pn=pkv // 2, name="attn_k_bwd_x")
    dh2v = _mm_nt(dv, W["w_kv"], p0=pkv // 2, pn=pkv // 2, name="attn_v_bwd_x")

    gain_0b = S["ln_gain"][0, 1][None]
    if on_grads is not None:
        gain_0b = gain_0b + on_grads(0, GW, dh2v)[0, 0]

    dz2, dz2b, dg2, db2 = _ln_bwd([dz3, dh2q, dh2k, dh2v], [DN_ALPHA, 1.0, 1.0, 1.0], xh2, rs2, gain_0b,
                                  name="ln_bwd_0b")
    dh1f, dcw0, dcb0 = ffn_bwd(dz2b, h1b, hc0, a0, 0)
    gain_0a = S["ln_gain"][0, 0][None]
    if on_grads is not None:
        gain_0a = gain_0a + on_grads(1, GW, GW["w_up0"])[0, 0]
    dz1, dz1b, dg1, db1 = _ln_bwd([dz2, dh1f], [DN_ALPHA, 1.0], xh1, rs1, gain_0a, name="ln_bwd_0a")
    dmix_i = _interleave(dz1b, B, L)
    dgate = _mm_nt(dmix_i, W["w_out"], name="s5_out_bwd_x")
    GW["w_out"] = _tn(gate, dmix_i, ptotal=1, np_cols=D, name="s5_out_bwd_w")
    dzg, dyg1, dbglu = _glu_bwd(y, z, dgate, name="glu_bwd")
    dyg2 = _mm_nt(dzg, W["w_glu"], name="glu_z_bwd_x")
    GW["w_glu"] = _tn(yg, dzg, ptotal=1, np_cols=D, name="glu_z_bwd_w")
    dy = _gelu_bwd(y, dyg1, dyg2, name="gelu_bwd")
    if on_grads is not None:
        d_row = d_row + on_grads(2, GW, GW["w_glu"])[0, 0]
    du_i, g_r, g_i, dar, dai, dd = _s5_bwd(dy, xi, h_r, h_i, wb, wc, a_r, a_i, d_row, B, name="s5_bwd")
    if on_grads is not None:
        on_grads(3, GW, du_i)
    dwb_r = _cluster_tn(xi, g_r, ncl, tok_left=True, name="s5_b_grad_re")
    dwb_i = _cluster_tn(xi, g_i, ncl, tok_left=True, name="s5_b_grad_im")
    dwc_r = _cluster_tn(dy, h_r, ncl, tok_left=False, name="s5_c_grad_re")
    dwc_i = _cluster_tn(dy, h_i, ncl, tok_left=False, name="s5_c_grad_im")
    grad_x = _axpy(dz1, _deinterleave(du_i, B, L), DN_ALPHA, name="grad_x")

    dbb_r = jnp.transpose(_unblockdiag(dwb_r, SSM_GROUP, Pst), (0, 2, 1))
    dbb_i = jnp.transpose(_unblockdiag(dwb_i, SSM_GROUP, Pst), (0, 2, 1))
    unslab = lambda da: jnp.transpose(da.reshape(cs // LANES, B, ncl, LANES).sum(1), (1, 0, 2)).reshape(G, Pst)
    dab_r, dab_i = unslab(dar), unslab(dai)
    GS["lam_re"], GS["lam_im"], GS["log_dt"], GS["b_re"], GS["b_im"] = disc_vjp((dab_r, dab_i, dbb_r, dbb_i))
    GS["c_re"] = jnp.transpose(_unblockdiag(dwc_r, Pst, SSM_GROUP), (0, 2, 1))
    GS["c_im"] = -jnp.transpose(_unblockdiag(dwc_i, Pst, SSM_GROUP), (0, 2, 1))
    GS["d"] = dd.reshape(G, SSM_GROUP)
    GS["b_glu"] = dbglu.reshape(D)
    GS["conv_w"] = jnp.stack([dcw0, dcw1])
    GS["conv_b"] = jnp.stack([dcb0[0], dcb1[0]])
    GS["ln_gain"] = jnp.stack([jnp.stack([dg1[0], dg2[0]]), jnp.stack([dg3[0], dg4[0]])])
    GS["ln_bias"] = jnp.stack([jnp.stack([db1[0], db2[0]]), jnp.stack([db3[0], db4[0]])])
    return loss, grad_x.reshape(B, L, D), GW, GS


SMALL_REPLICATED = ("lam_re", "lam_im", "log_dt", "b_re", "b_im", "c_re", "c_im", "d", "rel_bias", "conv_b")
SMALL_SHARDED = ("b_glu", "conv_w", "ln_gain", "ln_bias")
SMALL_ORDER = SMALL_REPLICATED + SMALL_SHARDED


def _pack(arrs, lanes, row_mult):
    flat = jnp.concatenate([a.reshape(-1).astype(F32) for a in arrs])
    rows = -(-flat.shape[0] // lanes)
    rows = -(-rows // row_mult) * row_mult
    return jnp.pad(flat, (0, rows * lanes - flat.shape[0])).reshape(rows, lanes)


def _unpack(packed, shapes):
    flat = packed.reshape(-1)
    out, off = [], 0
    for s in shapes:
        n = int(np.prod(s))
        out.append(flat[off:off + n].reshape(s))
        off += n
    return out


def kernel(x, s5_lam_re, s5_lam_im, s5_log_dt, s5_b_re, s5_b_im, s5_c_re, s5_c_im, s5_d, s5_w_glu, s5_b_glu, s5_w_out, attn_w_kv, attn_w_q, attn_w_out, rel_bias, ffn_w_up, ffn_conv_w, ffn_conv_b, ffn_w_down, ln_gain, ln_bias, loss_target, m_s5_lam_re, m_s5_lam_im, m_s5_log_dt, m_s5_b_re, m_s5_b_im, m_s5_c_re, m_s5_c_im, m_s5_d, m_s5_w_glu, m_s5_b_glu, m_s5_w_out, m_attn_w_kv, m_attn_w_q, m_attn_w_out, m_rel_bias, m_ffn_w_up, m_ffn_conv_w, m_ffn_conv_b, m_ffn_w_down, m_ln_gain, m_ln_bias, v_s5_lam_re, v_s5_lam_im, v_s5_log_dt, v_s5_b_re, v_s5_b_im, v_s5_c_re, v_s5_c_im, v_s5_d, v_s5_w_glu, v_s5_b_glu, v_s5_w_out, v_attn_w_kv, v_attn_w_q, v_attn_w_out, v_rel_bias, v_ffn_w_up, v_ffn_conv_w, v_ffn_conv_b, v_ffn_w_down, v_ln_gain, v_ln_bias):
    names = ["s5_lam_re", "s5_lam_im", "s5_log_dt", "s5_b_re", "s5_b_im", "s5_c_re", "s5_c_im", "s5_d", "s5_w_glu",
             "s5_b_glu", "s5_w_out", "attn_w_kv", "attn_w_q", "attn_w_out", "rel_bias", "ffn_w_up", "ffn_conv_w",
             "ffn_conv_b", "ffn_w_down", "ln_gain", "ln_bias"]
    loc = locals()
    w_in = {n: loc[n] for n in names}
    m_in = {n: loc["m_" + n] for n in names}
    v_in = {n: loc["v_" + n] for n in names}
    chip = 2 * lax.axis_index("x") + lax.axis_index("y")
    core = lax.axis_index("c")
    chip_idx = jnp.reshape(chip, (1,)).astype(jnp.int32)
    cidx = jnp.reshape(core, (1,)).astype(jnp.int32)

    big = [("w_glu", "s5_w_glu", "rows"), ("w_out", "s5_w_out", "rows"), ("w_ao", "attn_w_out", "rows"),
           ("w_kv", "attn_w_kv", "cols"), ("w_q", "attn_w_q", "cols"),
           ("w_up", "ffn_w_up", "layer_cols"), ("w_down", "ffn_w_down", "layer_rows")]

    def halves(t, kind):
        if kind.startswith("layer"):
            return t
        r, c = t.shape[-2:]
        return t.reshape(2, r // 2, c)

    def to_weight(g, kind):
        _, _, r, c = g.shape
        if kind == "rows":
            return g.reshape(1, 1, 8 * r, c)
        if kind == "cols":
            return g.reshape(4, 1, 2 * r, c)
        if kind == "layer_cols":
            return g
        return jnp.transpose(g, (1, 0, 2, 3)).reshape(1, 2, 4 * r, c)

    small_sh = {"b_glu": s5_b_glu[0], "conv_w": ffn_conv_w, "ln_gain": ln_gain, "ln_bias": ln_bias}
    sh_shapes = [small_sh[k].shape for k in SMALL_SHARDED]
    sh_pack = _pack([small_sh[k] for k in SMALL_SHARDED], 128, 16)

    shards = [halves(w_in[src].astype(MXU_DTYPE), kind) for _, src, kind in big]
    shards.append(sh_pack.reshape(2, sh_pack.shape[0] // 2, 128))
    shard_of = {key: s for (key, _, _), s in zip(big, shards)}
    shard_of["small"] = shards[-1]
    kind_of = {key: kind for key, _, kind in big}

    group_keys = [["w_glu", "w_out", "small"], ["w_up", "w_down"], ["w_kv", "w_q", "w_ao"]]
    started, token = _gather_start([[shard_of[k] for k in g] for g in group_keys], name="weights_gather_start")

    forwarding = {}

    def prepare_group(gi, after):
        ssem, rsem, thru, lands = started[gi]
        lands = _gather_wait(ssem, rsem, thru, lands, after, name=f"weights_gather_wait_{gi}")
        forwarding[gi], tok = _gather_forward_start(lands, name=f"weights_gather_forward_start_{gi}")
        return tok

    def finish_group(gi, after):
        if gi in forwarding:
            lands = _gather_forward_wait(forwarding.pop(gi), after, name=f"weights_gather_forward_wait_{gi}")
        else:
            ssem, rsem, thru, lands = started[gi]
            lands = _gather_wait(ssem, rsem, thru, lands, after, name=f"weights_gather_wait_{gi}")
            lands = _gather_forward(lands, name=f"weights_gather_forward_{gi}")
        out = {}
        for key, land in zip(group_keys[gi], lands):
            full = lax.dynamic_update_slice(land, shard_of[key][None], (chip, 0, 0, 0))
            if key == "small":
                parts = [_unpack(full[p], sh_shapes) for p in range(4)]
                for i, k in enumerate(SMALL_SHARDED):
                    out[k] = jnp.concatenate([parts[p][i] for p in range(4)], axis=-1)
            else:
                out[key] = to_weight(full, kind_of[key])
        return out

    replicated = dict(lam_re=s5_lam_re[0], lam_im=s5_lam_im[0], log_dt=s5_log_dt[0], b_re=s5_b_re[0], b_im=s5_b_im[0],
                      c_re=s5_c_re[0], c_im=s5_c_im[0], rel_bias=rel_bias, conv_b=ffn_conv_b,
                      d=s5_d[0] + token[0, 0])
    group_of = {k: gi for gi, g in enumerate(group_keys) for k in g if k != "small"}
    group_of.update({k: 0 for k in SMALL_SHARDED})
    group_of.update({k: "replicated" for k in replicated})
    params = _Lazy(group_of, lambda g, after: replicated if g == "replicated" else finish_group(g, after), prepare_group)

    red = [("w_up1", "ffn_w_up", 1), ("w_down1", "ffn_w_down", 1), ("w_ao", "attn_w_out", 0), ("w_kv", "attn_w_kv", 0),
           ("w_q", "attn_w_q", 0), ("w_down0", "ffn_w_down", 0), ("w_up0", "ffn_w_up", 0), ("w_out", "s5_w_out", 0),
           ("w_glu", "s5_w_glu", 0)]
    stages = [red[:5], red[5:7], red[7:]]

    def grad_halves(gw, key, src):
        r, c = w_in[src].shape[-2:]
        return gw[key].reshape(4, 2, r // 2, c)

    sent, early = {}, []

    def on_grads(stage, gw, latest):
        tokens = []
        if stage > 0:
            tag = "abc"[stage - 1]
            ga, theirs = _pair_send_wait(sent.pop(stage - 1), latest, name=f"rs_pair_send_wait_{tag}")
            hf, hx = _pair_sums(ga, [True] * len(ga), cidx, tag, theirs)
            started, tok = _chip_exchange_start(hx, name=f"rs_chip_exchange_start_{tag}")
            early.append((hf, started, tag))
            tokens.append(tok)
        if stage < len(stages):
            ga = [grad_halves(gw, key, src) for key, src, _ in stages[stage]]
            sent[stage], tok = _pair_send_start(ga, name=f"rs_pair_send_start_{'abc'[stage]}")
            tokens.append(tok)
        return sum(tokens[1:], tokens[0])

    loss, grad_x, GW, GS = _local_step(x, loss_target, params, params, on_grads)

    gs_shapes = [GS[k].shape for k in SMALL_ORDER] + [(1,)]
    gs_pack = _pack([GS[k] for k in SMALL_ORDER] + [loss.reshape(1)], 128, 64)
    rs = gs_pack.shape[0] // 8
    gs_halves = [gs_pack.reshape(4, 2, rs, 128)]
    hf_s, hx_s = _pair_sums(gs_halves, [False], cidx, "s")
    started_s, after = _chip_exchange_start(hx_s, name="rs_chip_exchange_start_s")
    mine = []
    for hf, started, tag in early:
        got = _chip_exchange_wait(started, after, name=f"rs_chip_exchange_wait_{tag}")
        mine += _chip_sums(hf, got, chip_idx, tag)
        after = mine[-1]
    swapping, _ = _pair_send_start(mine, name="rs_pair_swap_start", whole=True)
    mine_s = _chip_sums(hf_s, _chip_exchange_wait(started_s, after, name="rs_chip_exchange_wait_s"), chip_idx, "s")[0]
    other_s = _pair_swap([mine_s], name="rs_pair_swap_small")[0]
    small_halves = jnp.where(core == 0, jnp.concatenate([mine_s, other_s]), jnp.concatenate([other_s, mine_s]))
    small_all = _gather([small_halves], chip, name="small_grads_all_gather")[0]
    totals = _unpack(small_all, gs_shapes)
    gsmall = dict(zip(SMALL_ORDER, totals))
    loss = totals[-1][0]

    small_w = {"lam_re": s5_lam_re, "lam_im": s5_lam_im, "log_dt": s5_log_dt, "b_re": s5_b_re, "b_im": s5_b_im,
               "c_re": s5_c_re, "c_im": s5_c_im, "d": s5_d, "rel_bias": rel_bias, "conv_b": ffn_conv_b,
               "b_glu": s5_b_glu, "conv_w": ffn_conv_w, "ln_gain": ln_gain, "ln_bias": ln_bias}
    small_name = {"lam_re": "s5_lam_re", "lam_im": "s5_lam_im", "log_dt": "s5_log_dt", "b_re": "s5_b_re", "b_im": "s5_b_im",
                  "c_re": "s5_c_re", "c_im": "s5_c_im", "d": "s5_d", "rel_bias": "rel_bias", "conv_b": "ffn_conv_b",
                  "b_glu": "s5_b_glu", "conv_w": "ffn_conv_w", "ln_gain": "ln_gain", "ln_bias": "ln_bias"}
    sg = {}
    for k in SMALL_ORDER:
        shp = small_w[k].shape
        g = gsmall[k]
        if k in SMALL_SHARDED:
            width = shp[-1]
            g = lax.dynamic_slice_in_dim(g, chip * width, width, axis=g.ndim - 1)
        sg[k] = g.reshape(shp)
    sd, snm, snv = {}, {}, {}
    for k in SMALL_ORDER:
        shp = small_w[k].shape
        flat = lambda t: t.reshape(-1, shp[-1])
        r3 = _adamw(flat(small_w[k]), flat(sg[k]), flat(m_in[small_name[k]]), flat(v_in[small_name[k]]),
                    name=f"adamw_{k}")
        sd[k], snm[k], snv[k] = (t.reshape(shp) for t in r3)

    mine, other = _pair_send_wait(swapping, sd[SMALL_ORDER[-1]], name="rs_pair_swap_wait", whole=True)
    big_res = {}
    for (key, src, layer), gm, go in zip(red, mine, other):
        nl = w_in[src].shape[0] if src in ("ffn_w_up", "ffn_w_down") else 1
        r, c = w_in[src].shape[-2:]
        view = lambda t: t.reshape(nl, 2, r // 2, c)
        res4 = _adamw_halves(view(w_in[src]), view(m_in[src]), view(v_in[src]), gm, go, cidx, layer=layer,
                             prev=big_res.get(src), name=f"adamw_{key}")
        big_res[src] = res4
    big_res = {src: tuple(t.reshape(w_in[src].shape) for t in res4) for src, res4 in big_res.items()}

    def big_out(i):
        return {src: big_res[src][i] for _, src, _ in big}

    res = [{}, {}, {}, {}]
    for i in range(4):
        res[i].update(big_out(i))
    for k in SMALL_ORDER:
        res[0][small_name[k]] = sg[k]
        res[1][small_name[k]] = sd[k]
        res[2][small_name[k]] = snm[k]
        res[3][small_name[k]] = snv[k]
    outs = [loss, grad_x]
    for i in range(4):
        outs += [res[i][n] for n in names]
    return tuple(outs)
```

```python
import functools
import math

import numpy as np
import jax
import jax.numpy as jnp
from jax import lax
from jax.experimental import pallas as pl
from jax.experimental.pallas import tpu as pltpu

F32 = jnp.float32
BF16 = jnp.bfloat16
MXU_DTYPE = jnp.bfloat16
V7X_VMEM_LIMIT_BYTES = 52 << 20
MESH = pl.DeviceIdType.MESH

DEPTH = 2
SSM_GROUP = 16
SSM_STATE = 64
GROUPS_PER_CLUSTER = 16
CLUSTER_W = GROUPS_PER_CLUSTER * SSM_GROUP
HEAD_DIM = 64
DILATIONS = (1, 4, 16)
BAND = 128
ATTN_BATCH = (4, 8)
NEG_BIG = -1e30
REL_BUCKETS = 32
REL_MAX_DIST = 2048
DN_ALPHA = (2.0 * DEPTH) ** 0.25
LN_EPS = 1e-5
ADAM_LR, ADAM_B1, ADAM_B2, ADAM_EPS, ADAM_WD, ADAM_STEP = 0.001, 0.9, 0.999, 1e-08, 0.01, 10
GELU_K = math.sqrt(2.0 / math.pi)
GELU_C = 0.044715


def _pallas(body, **kw):
    return pl.pallas_call(body, **kw)


def _params(sem=None):
    return pltpu.CompilerParams(dimension_semantics=sem, vmem_limit_bytes=V7X_VMEM_LIMIT_BYTES)


def _pick(n, cands):
    for c in cands:
        if n % c == 0:
            return c
    return n


def _sigmoid(z):
    return 1.0 / (1.0 + jnp.exp(-z))


def _gelu(y):
    return 0.5 * y * (1.0 + jnp.tanh(GELU_K * (y + GELU_C * y * y * y)))


def _gelu_grad(y):
    t = jnp.tanh(GELU_K * (y + GELU_C * y * y * y))
    return 0.5 * (1.0 + t) + 0.5 * y * (1.0 - t * t) * (GELU_K * (1.0 + 3.0 * GELU_C * y * y))


def _mm_nn(a, w, *, l=0, bias=None, out_dtype=F32, name):
    T, K = a.shape
    P, _, _, Np = w.shape
    tm = _pick(T, (1024, 512, 256, 128))
    tn = _pick(Np, (1408, 1024, 768, 512, 384, 256, 128))
    nj = Np // tn

    def body(*refs):
        if bias is None:
            a_ref, w_ref, o_ref = refs
        else:
            a_ref, w_ref, b_ref, o_ref = refs
        acc = jnp.dot(a_ref[...].astype(MXU_DTYPE), w_ref[...].astype(MXU_DTYPE), preferred_element_type=F32)
        if bias is not None:
            acc = acc + b_ref[...]
        o_ref[...] = acc.astype(o_ref.dtype)

    in_specs = [pl.BlockSpec((tm, K), lambda p, j, i: (i, 0)),
                pl.BlockSpec((None, None, K, tn), lambda p, j, i: (p, l, 0, j))]
    args = [a, w]
    if bias is not None:
        in_specs.append(pl.BlockSpec((1, tn), lambda p, j, i: (0, p * nj + j)))
        args.append(bias)
    return _pallas(
        body, name=name, grid=(P, nj, T // tm), in_specs=in_specs,
        out_specs=pl.BlockSpec((tm, tn), lambda p, j, i: (i, p * nj + j)),
        out_shape=jax.ShapeDtypeStruct((T, P * Np), out_dtype),
        compiler_params=_params(("parallel", "parallel", "parallel")),
    )(*args)


def _mm_nt(a, w, *, l=0, p0=0, pn=None, out_dtype=F32, name):
    T = a.shape[0]
    _, _, K, Np = w.shape
    pn = w.shape[0] if pn is None else pn
    tm = _pick(T, (1024, 512, 256, 128) if K <= 1024 else (512, 256, 128))
    tn = _pick(Np, (1536, 1408, 1024, 768, 512, 384, 256, 128))
    nj = Np // tn
    nred = pn * nj

    def body(a_ref, w_ref, o_ref, acc):
        r = pl.program_id(1)

        @pl.when(r == 0)
        def _():
            acc[...] = jnp.zeros_like(acc)

        acc[...] += lax.dot_general(a_ref[...].astype(MXU_DTYPE), w_ref[...].astype(MXU_DTYPE),
                                    (((1,), (1,)), ((), ())), preferred_element_type=F32)

        @pl.when(r == nred - 1)
        def _():
            o_ref[...] = acc[...].astype(o_ref.dtype)

    return _pallas(
        body, name=name, grid=(T // tm, nred),
        in_specs=[pl.BlockSpec((tm, tn), lambda i, r: (i, r)),
                  pl.BlockSpec((None, None, K, tn), lambda i, r: (p0 + r // nj, l, 0, r % nj))],
        out_specs=pl.BlockSpec((tm, K), lambda i, r: (i, 0)),
        out_shape=jax.ShapeDtypeStruct((T, K), out_dtype),
        scratch_shapes=[pltpu.VMEM((tm, K), F32)],
        compiler_params=_params(("parallel", "arbitrary")),
    )(a, w)


def _tn(a, b, *, ptotal, np_cols, nl=1, l=0, p0=0, prev=None, name):
    T, K = a.shape
    Np = np_cols
    pn = b.shape[1] // Np
    tt = _pick(T, (1024, 512, 256, 128))
    tk = _pick(K, (1408, 1024, 512, 256, 128))
    tn = _pick(Np, (1408, 768, 512, 256, 128))
    if tk * tn > 1408 * 1024:
        tn = _pick(Np, (512, 256, 128))
    nj = Np // tn
    nt = T // tt

    def body(*refs):
        a_ref, b_ref = refs[0], refs[1]
        o_ref, acc = refs[-2], refs[-1]
        t = pl.program_id(3)

        @pl.when(t == 0)
        def _():
            acc[...] = jnp.zeros_like(acc)

        acc[...] += lax.dot_general(a_ref[...].astype(MXU_DTYPE), b_ref[...].astype(MXU_DTYPE),
                                    (((0,), (0,)), ((), ())), preferred_element_type=F32)

        @pl.when(t == nt - 1)
        def _():
            o_ref[...] = acc[...]

    in_specs = [pl.BlockSpec((tt, tk), lambda kb, p, j, t: (t, kb)),
                pl.BlockSpec((tt, tn), lambda kb, p, j, t: (t, p * nj + j))]
    args = [a, b]
    aliases = {}
    if prev is not None:
        in_specs.append(pl.BlockSpec(memory_space=pl.ANY))
        args.append(prev)
        aliases = {2: 0}
    return _pallas(
        body, name=name, grid=(K // tk, pn, nj, nt), in_specs=in_specs,
        out_specs=pl.BlockSpec((None, None, tk, tn), lambda kb, p, j, t: (p0 + p, l, kb, j)),
        out_shape=jax.ShapeDtypeStruct((ptotal, nl, K, Np), F32),
        scratch_shapes=[pltpu.VMEM((tk, tn), F32)],
        input_output_aliases=aliases,
        compiler_params=_params(("parallel", "parallel", "parallel", "arbitrary")),
    )(*args)


def _rows(tm, f):
    return pl.BlockSpec((tm, f), lambda i: (i, 0))


def _whole(shape):
    nd = len(shape)
    return pl.BlockSpec(shape, lambda i: (0,) * nd)


def _ln_fwd(xres, f, gain, bias, *, name):
    T, D = xres.shape
    tm = _pick(T, (256, 128))

    def body(x_ref, f_ref, g_ref, b_ref, y_ref, yb_ref, xh_ref, rs_ref):
        z = DN_ALPHA * x_ref[...] + f_ref[...]
        mu = jnp.mean(z, axis=-1, keepdims=True)
        zc = z - mu
        var = jnp.mean(zc * zc, axis=-1, keepdims=True)
        rstd = lax.rsqrt(var + LN_EPS)
        xh = zc * rstd
        y = xh * g_ref[...] + b_ref[...]
        y_ref[...] = y
        yb_ref[...] = y.astype(yb_ref.dtype)
        xh_ref[...] = xh
        rs_ref[...] = rstd

    return _pallas(
        body, name=name, grid=(T // tm,),
        in_specs=[_rows(tm, D), _rows(tm, D), _whole((1, D)), _whole((1, D))],
        out_specs=[_rows(tm, D), _rows(tm, D), _rows(tm, D), _rows(tm, 1)],
        out_shape=[jax.ShapeDtypeStruct((T, D), F32), jax.ShapeDtypeStruct((T, D), MXU_DTYPE),
                   jax.ShapeDtypeStruct((T, D), F32), jax.ShapeDtypeStruct((T, 1), F32)],
        compiler_params=_params(("parallel",)),
    )(xres, f, gain, bias)


def _ln_bwd(addends, coefs, xhat, rstd, gain, *, name):
    T, D = xhat.shape
    tm = _pick(T, (256, 128))
    n = len(addends)

    def body(*refs):
        adds = refs[:n]
        xh_ref, rs_ref, g_ref, dz_ref, dzb_ref, dg_ref, db_ref = refs[n:]
        dy = coefs[0] * adds[0][...]
        for c, r in zip(coefs[1:], adds[1:]):
            dy = dy + c * r[...]
        xh = xh_ref[...]
        dxh = dy * g_ref[...]
        m1 = jnp.mean(dxh, axis=-1, keepdims=True)
        m2 = jnp.mean(dxh * xh, axis=-1, keepdims=True)
        dz = rs_ref[...] * (dxh - m1 - xh * m2)
        dz_ref[...] = dz
        dzb_ref[...] = dz.astype(dzb_ref.dtype)

        @pl.when(pl.program_id(0) == 0)
        def _():
            dg_ref[...] = jnp.zeros_like(dg_ref)
            db_ref[...] = jnp.zeros_like(db_ref)

        dg_ref[...] += jnp.sum(dy * xh, axis=0, keepdims=True)
        db_ref[...] += jnp.sum(dy, axis=0, keepdims=True)

    return _pallas(
        body, name=name, grid=(T // tm,),
        in_specs=[_rows(tm, D)] * n + [_rows(tm, D), _rows(tm, 1), _whole((1, D))],
        out_specs=[_rows(tm, D), _rows(tm, D), _whole((1, D)), _whole((1, D))],
        out_shape=[jax.ShapeDtypeStruct((T, D), F32), jax.ShapeDtypeStruct((T, D), MXU_DTYPE),
                   jax.ShapeDtypeStruct((1, D), F32), jax.ShapeDtypeStruct((1, D), F32)],
        compiler_params=_params(("arbitrary",)),
    )(*addends, xhat, rstd, gain)


def _loss_grad(y, tgt, *, name):
    T, D = y.shape
    tm = _pick(T, (256, 128))

    def body(y_ref, t_ref, dy_ref, l_ref):
        e = y_ref[...] - t_ref[...]
        dy_ref[...] = e * (1.0 / D)

        @pl.when(pl.program_id(0) == 0)
        def _():
            l_ref[...] = jnp.zeros_like(l_ref)

        l_ref[...] += jnp.zeros_like(l_ref) + jnp.sum(e * e) * (0.5 / D)

    return _pallas(
        body, name=name, grid=(T // tm,),
        in_specs=[_rows(tm, D), _rows(tm, D)],
        out_specs=[_rows(tm, D), _whole((1, 128))],
        out_shape=[jax.ShapeDtypeStruct((T, D), F32), jax.ShapeDtypeStruct((1, 128), F32)],
        compiler_params=_params(("arbitrary",)),
    )(y, tgt)


def _axpy(a, b, ca, *, name):
    T, D = a.shape
    tm = _pick(T, (256, 128))

    def body(a_ref, b_ref, o_ref):
        o_ref[...] = ca * a_ref[...] + b_ref[...]

    return _pallas(
        body, name=name, grid=(T // tm,), in_specs=[_rows(tm, D), _rows(tm, D)], out_specs=_rows(tm, D),
        out_shape=jax.ShapeDtypeStruct((T, D), F32), compiler_params=_params(("parallel",)),
    )(a, b)


def _glu_gate(y, z, *, name):
    T, D = y.shape
    tm = _pick(T, (256, 128))

    def body(y_ref, z_ref, g_ref):
        g_ref[...] = (_gelu(y_ref[...]) * _sigmoid(z_ref[...])).astype(g_ref.dtype)

    return _pallas(
        body, name=name, grid=(T // tm,), in_specs=[_rows(tm, D), _rows(tm, D)], out_specs=_rows(tm, D),
        out_shape=jax.ShapeDtypeStruct((T, D), MXU_DTYPE), compiler_params=_params(("parallel",)),
    )(y, z)


def _glu_bwd(y, z, dg, *, name):
    T, D = y.shape
    tm = _pick(T, (256, 128))

    def body(y_ref, z_ref, dg_ref, dzb_ref, dyg_ref, db_ref):
        s = _sigmoid(z_ref[...])
        dg = dg_ref[...]
        dz = dg * _gelu(y_ref[...]) * s * (1.0 - s)
        dzb_ref[...] = dz.astype(dzb_ref.dtype)
        dyg_ref[...] = dg * s

        @pl.when(pl.program_id(0) == 0)
        def _():
            db_ref[...] = jnp.zeros_like(db_ref)

        db_ref[...] += jnp.sum(dz, axis=0, keepdims=True)

    return _pallas(
        body, name=name, grid=(T // tm,), in_specs=[_rows(tm, D)] * 3,
        out_specs=[_rows(tm, D), _rows(tm, D), _whole((1, D))],
        out_shape=[jax.ShapeDtypeStruct((T, D), MXU_DTYPE), jax.ShapeDtypeStruct((T, D), F32),
                   jax.ShapeDtypeStruct((1, D), F32)],
        compiler_params=_params(("arbitrary",)),
    )(y, z, dg)


def _gelu_bwd(y, d1, d2, *, name):
    T, D = y.shape
    tm = _pick(T, (256, 128))

    def body(y_ref, a_ref, b_ref, o_ref):
        o_ref[...] = (a_ref[...] + b_ref[...]) * _gelu_grad(y_ref[...])

    return _pallas(
        body, name=name, grid=(T // tm,), in_specs=[_rows(tm, D)] * 3, out_specs=_rows(tm, D),
        out_shape=jax.ShapeDtypeStruct((T, D), F32), compiler_params=_params(("parallel",)),
    )(y, d1, d2)


CONV_ROWS = 128
CONV_EDGE = 16


def _row_shifts(x, edge, drop_edge, tm, back):
    keep = jnp.where(drop_edge, 0.0, 1.0).astype(edge.dtype)
    ext = jnp.concatenate([edge * keep, x] if back else [x, edge * keep], axis=0)
    row = lax.broadcasted_iota(jnp.int32, (tm, tm + CONV_EDGE), 0)
    col = lax.broadcasted_iota(jnp.int32, (tm, tm + CONV_EDGE), 1)
    base = row + CONV_EDGE if back else row
    out = []
    for k in (1, 2):
        pick = (col == (base - k if back else base + k)).astype(x.dtype)
        out.append(jnp.dot(pick, ext, preferred_element_type=F32))
    return out


def _conv_specs(T, F2, tm):
    return [_rows(tm, F2),
            pl.BlockSpec((CONV_EDGE, F2), lambda i: (jnp.maximum(i * (tm // CONV_EDGE) - 1, 0), 0))]


def _conv_glu_fwd(hc, conv_w, conv_b, L, *, name):
    T, F2 = hc.shape
    F = F2 // 2
    tm = CONV_ROWS

    def body(x_ref, e_ref, w_ref, b_ref, a_ref):
        at_start = (pl.program_id(0) * tm) % L == 0
        x1, x2 = _row_shifts(x_ref[...], e_ref[...], at_start, tm, True)
        x = x_ref[...].astype(F32)
        c = b_ref[...] + w_ref[0:1, :] * x + w_ref[1:2, :] * x1 + w_ref[2:3, :] * x2
        val, gate = c[:, :F], c[:, F:]
        a_ref[...] = (gate * _sigmoid(gate) * val).astype(a_ref.dtype)

    return _pallas(
        body, name=name, grid=(T // tm,),
        in_specs=_conv_specs(T, F2, tm) + [_whole((3, F2)), _whole((1, F2))],
        out_specs=_rows(tm, F),
        out_shape=jax.ShapeDtypeStruct((T, F), MXU_DTYPE), compiler_params=_params(("parallel",)),
    )(hc, hc, conv_w, conv_b)


def _conv_glu_bwd(hc, da, conv_w, conv_b, L, *, name):
    T, F2 = hc.shape
    F = F2 // 2
    tm = CONV_ROWS

    def body(x_ref, e_ref, da_ref, w_ref, b_ref, dc_ref, dw_ref, db_ref):
        at_start = (pl.program_id(0) * tm) % L == 0
        x1, x2 = _row_shifts(x_ref[...], e_ref[...], at_start, tm, True)
        x = x_ref[...].astype(F32)
        c = b_ref[...] + w_ref[0:1, :] * x + w_ref[1:2, :] * x1 + w_ref[2:3, :] * x2
        val, gate = c[:, :F], c[:, F:]
        s = _sigmoid(gate)
        da = da_ref[...].astype(F32)
        dval = da * (gate * s)
        dgate = da * val * (s * (1.0 + gate * (1.0 - s)))
        dc = jnp.concatenate([dval, dgate], axis=-1)
        dc_ref[...] = dc.astype(dc_ref.dtype)

        @pl.when(pl.program_id(0) == 0)
        def _():
            dw_ref[...] = jnp.zeros_like(dw_ref)
            db_ref[...] = jnp.zeros_like(db_ref)

        dw_ref[0:1, :] += jnp.sum(dc * x, axis=0, keepdims=True)
        dw_ref[1:2, :] += jnp.sum(dc * x1, axis=0, keepdims=True)
        dw_ref[2:3, :] += jnp.sum(dc * x2, axis=0, keepdims=True)
        db_ref[...] += jnp.sum(dc, axis=0, keepdims=True)

    return _pallas(
        body, name=name, grid=(T // tm,),
        in_specs=_conv_specs(T, F2, tm) + [_rows(tm, F), _whole((3, F2)), _whole((1, F2))],
        out_specs=[_rows(tm, F2), _whole((3, F2)), _whole((1, F2))],
        out_shape=[jax.ShapeDtypeStruct((T, F2), MXU_DTYPE), jax.ShapeDtypeStruct((3, F2), F32),
                   jax.ShapeDtypeStruct((1, F2), F32)],
        compiler_params=_params(("arbitrary",)),
    )(hc, hc, da, conv_w, conv_b)


def _conv_bwd_input(dc, conv_w, L, *, name):
    T, F2 = dc.shape
    tm = CONV_ROWS
    edge = CONV_EDGE
    last_blk = T // edge - 1

    def body(x_ref, e_ref, w_ref, o_ref):
        at_end = ((pl.program_id(0) + 1) * tm) % L == 0
        x1, x2 = _row_shifts(x_ref[...], e_ref[...], at_end, tm, False)
        x = x_ref[...].astype(F32)
        o_ref[...] = (w_ref[0:1, :] * x + w_ref[1:2, :] * x1 + w_ref[2:3, :] * x2).astype(o_ref.dtype)

    return _pallas(
        body, name=name, grid=(T // tm,),
        in_specs=[_rows(tm, F2),
                  pl.BlockSpec((edge, F2), lambda i: (jnp.minimum((i + 1) * (tm // edge), last_blk), 0)),
                  _whole((3, F2))],
        out_specs=_rows(tm, F2),
        out_shape=jax.ShapeDtypeStruct((T, F2), MXU_DTYPE), compiler_params=_params(("parallel",)),
    )(dc, dc, conv_w)


S5_CHUNK = 128
LANES = 128


def _slab_rows(c, n, ncl):
    return pl.ds(c, n) if ncl == 1 else pl.ds(c, n, stride=ncl)


def _slab_put(ref, c, n, ncl, val):
    for s in range(val.shape[1] // LANES):
        ref[s, _slab_rows(c, n, ncl), :] = val[:, s * LANES:(s + 1) * LANES]


def _slab_get(ref, c, n, ncl):
    return jnp.concatenate([ref[s, _slab_rows(c, n, ncl), :] for s in range(ref.shape[0])], axis=-1)


def _slabs(n_slab, rows):
    return pl.BlockSpec((n_slab, rows, LANES), lambda i: (0, i, 0))


def _s5_fwd(xi, wb, wc, a_r, a_i, d_row, B, *, name):
    T, D = xi.shape
    ncl = wb.shape[0]
    cs = wb.shape[2] // 2
    ns = cs // LANES
    R = B * ncl
    Q = S5_CHUNK
    QR = Q * ncl
    nsteps = Q // B

    def body(x_ref, wb_ref, wc_ref, ar_ref, ai_ref, d_ref, y_ref, yg_ref, hr_ref, hi_ref, bur, bui, cr, ci):
        @pl.when(pl.program_id(0) == 0)
        def _():
            cr[...] = jnp.zeros_like(cr)
            ci[...] = jnp.zeros_like(ci)

        x = x_ref[...]
        xb = x.astype(MXU_DTYPE)
        for c in range(ncl):
            bu = jnp.dot(xb[:, c * CLUSTER_W:(c + 1) * CLUSTER_W], wb_ref[c], preferred_element_type=F32)
            _slab_put(bur, c, Q, ncl, bu[:, :cs])
            _slab_put(bui, c, Q, ncl, bu[:, cs:])
        ar = ar_ref[...]
        ai = ai_ref[...]

        def step(k, carry):
            hr, hi = carry
            sl = pl.ds(pl.multiple_of(k * R, R), R)
            nr = ar * hr - ai * hi + bur[:, sl, :]
            ni = ar * hi + ai * hr + bui[:, sl, :]
            hr_ref[:, sl, :] = nr
            hi_ref[:, sl, :] = ni
            return nr, ni

        hr, hi = lax.fori_loop(0, nsteps, step, (cr[...], ci[...]), unroll=4)
        cr[...] = hr
        ci[...] = hi
        parts = []
        for c in range(ncl):
            hrc = _slab_get(hr_ref, c, Q, ncl).astype(MXU_DTYPE)
            hic = _slab_get(hi_ref, c, Q, ncl).astype(MXU_DTYPE)
            parts.append(jnp.dot(hrc, wc_ref[c, :cs, :], preferred_element_type=F32)
                         + jnp.dot(hic, wc_ref[c, cs:, :], preferred_element_type=F32))
        y = d_ref[...] * x + (parts[0] if ncl == 1 else jnp.concatenate(parts, axis=-1))
        y_ref[...] = y
        yg_ref[...] = _gelu(y).astype(yg_ref.dtype)

    return _pallas(
        body, name=name, grid=(T // Q,),
        in_specs=[_rows(Q, D), _whole(wb.shape), _whole(wc.shape), _whole((ns, R, LANES)), _whole((ns, R, LANES)),
                  _whole((1, D))],
        out_specs=[_rows(Q, D), _rows(Q, D), _slabs(ns, QR), _slabs(ns, QR)],
        out_shape=[jax.ShapeDtypeStruct((T, D), F32), jax.ShapeDtypeStruct((T, D), MXU_DTYPE),
                   jax.ShapeDtypeStruct((ns, T * ncl, LANES), F32), jax.ShapeDtypeStruct((ns, T * ncl, LANES), F32)],
        scratch_shapes=[pltpu.VMEM((ns, QR, LANES), F32), pltpu.VMEM((ns, QR, LANES), F32),
                        pltpu.VMEM((ns, R, LANES), F32), pltpu.VMEM((ns, R, LANES), F32)],
        compiler_params=_params(("arbitrary",)),
    )(xi, wb, wc, a_r, a_i, d_row)


def _s5_bwd(dy, xi, h_r, h_i, wb, wc, a_r, a_i, d_row, B, *, name):
    T, D = dy.shape
    ncl = wb.shape[0]
    cs = wb.shape[2] // 2
    ns = cs // LANES
    R = B * ncl
    Q = S5_CHUNK
    nsteps = Q // B
    nchunk = T // Q
    QR = Q * ncl

    def rev(i):
        return nchunk - 1 - i

    def body(dy_ref, x_ref, hr_ref, hi_ref, pr_ref, pi_ref, wb_ref, wc_ref, ar_ref, ai_ref, d_ref,
             du_ref, gr_ref, gi_ref, dar_ref, dai_ref, dd_ref, dhr, dhi, cr, ci):
        i = pl.program_id(0)

        @pl.when(i == 0)
        def _():
            cr[...] = jnp.zeros_like(cr)
            ci[...] = jnp.zeros_like(ci)
            dar_ref[...] = jnp.zeros_like(dar_ref)
            dai_ref[...] = jnp.zeros_like(dai_ref)
            dd_ref[...] = jnp.zeros_like(dd_ref)

        dyv = dy_ref[...]
        dyb = dyv.astype(MXU_DTYPE)
        for c in range(ncl):
            dh = lax.dot_general(dyb[:, c * CLUSTER_W:(c + 1) * CLUSTER_W], wc_ref[c],
                                 (((1,), (1,)), ((), ())), preferred_element_type=F32)
            _slab_put(dhr, c, Q, ncl, dh[:, :cs])
            _slab_put(dhi, c, Q, ncl, dh[:, cs:])
        ar = ar_ref[...]
        ai = ai_ref[...]

        def step(j, carry):
            gr, gi = carry
            k = nsteps - 1 - j
            sl = pl.ds(pl.multiple_of(k * R, R), R)
            ngr = dhr[:, sl, :] + ar * gr + ai * gi
            ngi = dhi[:, sl, :] - ai * gr + ar * gi
            gr_ref[:, sl, :] = ngr
            gi_ref[:, sl, :] = ngi
            return ngr, ngi

        gr, gi = lax.fori_loop(0, nsteps, step, (cr[...], ci[...]), unroll=4)
        cr[...] = gr
        ci[...] = gi
        keep = jnp.where(i == nchunk - 1, 0.0, 1.0)
        hpr = jnp.concatenate([pr_ref[:, 8 - R:8, :] * keep, hr_ref[:, 0:QR - R, :]], axis=1)
        hpi = jnp.concatenate([pi_ref[:, 8 - R:8, :] * keep, hi_ref[:, 0:QR - R, :]], axis=1)
        gra, gia = gr_ref[...], gi_ref[...]
        steps = lambda t: jnp.sum(t.reshape(ns, nsteps, R, LANES), axis=1)
        dar_ref[...] += steps(gra * hpr + gia * hpi)
        dai_ref[...] += steps(gia * hpr - gra * hpi)
        parts = []
        for c in range(ncl):
            grc = _slab_get(gr_ref, c, Q, ncl).astype(MXU_DTYPE)
            gic = _slab_get(gi_ref, c, Q, ncl).astype(MXU_DTYPE)
            parts.append(lax.dot_general(grc, wb_ref[c, :, :cs], (((1,), (1,)), ((), ())), preferred_element_type=F32)
                         + lax.dot_general(gic, wb_ref[c, :, cs:], (((1,), (1,)), ((), ())), preferred_element_type=F32))
        du_ref[...] = d_ref[...] * dyv + (parts[0] if ncl == 1 else jnp.concatenate(parts, axis=-1))
        dd_ref[...] += jnp.sum(dyv * x_ref[...], axis=0, keepdims=True)

    tok = pl.BlockSpec((Q, D), lambda i: (rev(i), 0))
    st = pl.BlockSpec((ns, QR, LANES), lambda i: (0, rev(i), 0))
    before = pl.BlockSpec((ns, 8, LANES), lambda i: (0, jnp.maximum(rev(i) * (QR // 8) - 1, 0), 0))
    acc = _whole((ns, R, LANES))
    return _pallas(
        body, name=name, grid=(nchunk,),
        in_specs=[tok, tok, st, st, before, before, _whole(wb.shape), _whole(wc.shape), acc, acc, _whole((1, D))],
        out_specs=[tok, st, st, acc, acc, _whole((1, D))],
        out_shape=[jax.ShapeDtypeStruct((T, D), F32),
                   jax.ShapeDtypeStruct((ns, T * ncl, LANES), F32), jax.ShapeDtypeStruct((ns, T * ncl, LANES), F32),
                   jax.ShapeDtypeStruct((ns, R, LANES), F32), jax.ShapeDtypeStruct((ns, R, LANES), F32),
                   jax.ShapeDtypeStruct((1, D), F32)],
        scratch_shapes=[pltpu.VMEM((ns, QR, LANES), F32)] * 2 + [pltpu.VMEM((ns, R, LANES), F32)] * 2,
        compiler_params=_params(("arbitrary",)),
    )(dy, xi, h_r, h_i, h_r, h_i, wb, wc, a_r, a_i, d_row)


def _cluster_tn(tok, st, ncl, *, tok_left, name, after=None):
    T = tok.shape[0]
    ns = st.shape[0]
    cs = ns * LANES
    tt = _pick(T, (512, 256, 128))
    nt = T // tt
    oshape = (ncl, CLUSTER_W, cs) if tok_left else (ncl, cs, CLUSTER_W)

    def body(tok_ref, st_ref, *rest):
        o_ref, acc = rest[-2:]
        t = pl.program_id(0)

        @pl.when(t == 0)
        def _():
            acc[...] = jnp.zeros_like(acc)

        tk = tok_ref[...].astype(MXU_DTYPE)
        for c in range(ncl):
            tc = tk[:, c * CLUSTER_W:(c + 1) * CLUSTER_W]
            sc = _slab_get(st_ref, c, tt, ncl).astype(MXU_DTYPE)
            lhs, rhs = (tc, sc) if tok_left else (sc, tc)
            acc[c] += lax.dot_general(lhs, rhs, (((0,), (0,)), ((), ())), preferred_element_type=F32)

        @pl.when(t == nt - 1)
        def _():
            o_ref[...] = acc[...]

    return _pallas(
        body, name=name, grid=(nt,),
        in_specs=[_rows(tt, tok.shape[1]), _slabs(ns, tt * ncl)] + ([] if after is None else [_ANY]),
        out_specs=_whole(oshape),
        out_shape=jax.ShapeDtypeStruct(oshape, F32),
        scratch_shapes=[pltpu.VMEM(oshape, F32)],
        compiler_params=_params(("arbitrary",)),
    )(tok, st, *([] if after is None else [after]))


def _s5_discretize(lam_re, lam_im, log_dt, b_re, b_im):
    dt = jnp.exp(log_dt)[:, None]
    mag = jnp.exp(lam_re * dt)
    ab_r, ab_i = mag * jnp.cos(lam_im * dt), mag * jnp.sin(lam_im * dt)
    den = lam_re * lam_re + lam_im * lam_im
    nr = ab_r - 1.0
    co_r = (nr * lam_re + ab_i * lam_im) / den
    co_i = (ab_i * lam_re - nr * lam_im) / den
    bb_r = co_r[..., None] * b_re - co_i[..., None] * b_im
    bb_i = co_r[..., None] * b_im + co_i[..., None] * b_re
    return ab_r, ab_i, bb_r, bb_i


def _blockdiag(m):
    G, r, k = m.shape
    ncl = G // GROUPS_PER_CLUSTER
    m4 = m.reshape(ncl, GROUPS_PER_CLUSTER, r, k)
    eye = jnp.eye(GROUPS_PER_CLUSTER, dtype=m.dtype)
    return jnp.einsum('cgrk,gh->cgrhk', m4, eye).reshape(ncl, GROUPS_PER_CLUSTER * r, GROUPS_PER_CLUSTER * k)


def _unblockdiag(m, r, k):
    ncl = m.shape[0]
    m5 = m.reshape(ncl, GROUPS_PER_CLUSTER, r, GROUPS_PER_CLUSTER, k)
    eye = jnp.eye(GROUPS_PER_CLUSTER, dtype=m.dtype)
    return jnp.einsum('cgrhk,gh->cgrk', m5, eye).reshape(ncl * GROUPS_PER_CLUSTER, r, k)


def _t5_bucket(dist):
    exact = REL_BUCKETS // 2
    d = np.maximum(dist, 1).astype(np.float32)
    large = exact + (np.log(d / exact) / math.log(REL_MAX_DIST / exact) * (REL_BUCKETS - exact)).astype(np.int64)
    large = np.minimum(large, REL_BUCKETS - 1)
    return np.where(dist < exact, dist, large).astype(np.int32)


def _band_tables(dil):
    steps = np.arange(BAND)[:, None] + BAND - np.arange(2 * BAND)[None, :]
    bucket = _t5_bucket(np.maximum(steps, 0) * dil)
    in_band = (steps >= 0) & (steps <= BAND)
    return bucket, in_band


def _attn_bias(rel_bias, hpg):
    out = []
    for g, dil in enumerate(DILATIONS):
        bucket, in_band = _band_tables(dil)
        cols = rel_bias[:, g * hpg:(g + 1) * hpg].astype(F32)
        onehot = jnp.asarray((bucket.reshape(-1, 1) == np.arange(REL_BUCKETS)[None, :]).astype(np.float32))
        bias = jnp.dot(onehot, cols, precision=lax.Precision.HIGHEST).T.reshape(hpg, BAND, 2 * BAND)
        out.append(jnp.where(jnp.asarray(in_band)[None], bias, NEG_BIG))
    return jnp.concatenate(out, axis=0)


def _attn_blocks(dil, L):
    M = L // dil
    return M, M // BAND


def _row_sel(r, M, dil):
    return pl.ds(r, M) if dil == 1 else pl.ds(r, M, stride=dil)


def _attn_fwd(q, kv, bias, L, hpg, *, name):
    T = q.shape[0]
    nb_ = T // L
    HP = hpg // 2
    W3 = 3 * hpg * HEAD_DIM
    mmax = L

    def group_body(dil, q_ref, k_ref, v_ref, b_ref, o_ref, l_ref, os, ls):
        M, NB = _attn_blocks(dil, L)
        first = lax.broadcasted_iota(jnp.int32, (1, 2 * HEAD_DIM), 1) < HEAD_DIM
        loaded = {}

        def load(r):
            rows = _row_sel(r, M, dil)
            qf = q_ref[rows, :] * 0.125
            qm = [jnp.where(first, qf, 0.0).astype(MXU_DTYPE), jnp.where(first, 0.0, qf).astype(MXU_DTYPE)]
            kr = k_ref[rows, :].astype(MXU_DTYPE)
            va = jnp.concatenate([v_ref[rows, :].astype(MXU_DTYPE), jnp.ones((M, 2 * HEAD_DIM), MXU_DTYPE)], axis=-1)
            return qm, kr, va

        tasks = [(r, n) for r in range(dil) for n in range(NB)]
        nbatch = ATTN_BATCH[NB == 1]
        for t0 in range(0, len(tasks), nbatch):
            batch = tasks[t0:t0 + nbatch]
            for r, _ in batch:
                if r not in loaded:
                    loaded[r] = load(r)
            chains = [(r, n, hh) for r, n in batch for hh in range(2)]
            ks = lambda n: slice(0, BAND) if n == 0 else slice((n - 1) * BAND, (n + 1) * BAND)
            s = [lax.dot_general(loaded[r][0][hh][n * BAND:(n + 1) * BAND, :], loaded[r][1][ks(n), :],
                                 (((1,), (1,)), ((), ())), preferred_element_type=F32)
                 + (b_ref[hh, :, BAND:] if n == 0 else b_ref[hh]) for r, n, hh in chains]
            m = [jnp.max(t, axis=-1, keepdims=True) for t in s]
            p = [jnp.exp(t - mm) for t, mm in zip(s, m)]
            pv = [jnp.dot(t.astype(MXU_DTYPE), loaded[r][2][ks(n), :], preferred_element_type=F32)
                  for t, (r, n, hh) in zip(p, chains)]
            l = [t[:, 2 * HEAD_DIM:] for t in pv]
            o_h = [t[:, :2 * HEAD_DIM] / ll for t, ll in zip(pv, l)]
            l_h = [mm + jnp.log(ll) for mm, ll in zip(m, l)]
            for i, (r, n) in enumerate(batch):
                os[r * M + n * BAND:r * M + (n + 1) * BAND, :] = jnp.where(first, o_h[2 * i], o_h[2 * i + 1])
                ls[r * M + n * BAND:r * M + (n + 1) * BAND, :] = jnp.where(first, l_h[2 * i], l_h[2 * i + 1])
                if n == NB - 1:
                    rows = _row_sel(r, M, dil)
                    o_ref[rows, :] = os[r * M:(r + 1) * M, :]
                    l_ref[rows, :] = ls[r * M:(r + 1) * M, :]

    def body(q_ref, k_ref, v_ref, b_ref, o_ref, l_ref, os, ls):
        g = pl.program_id(0)
        for gi, dil in enumerate(DILATIONS):
            pl.when(g == gi)(functools.partial(group_body, dil, q_ref, k_ref, v_ref, b_ref, o_ref, l_ref, os, ls))

    blk = (L, 2 * HEAD_DIM)
    return _pallas(
        body, name=name, grid=(3, nb_, HP),
        in_specs=[pl.BlockSpec(blk, lambda g, b, h: (b, g * HP + h)),
                  pl.BlockSpec(blk, lambda g, b, h: (b, g * HP + h)),
                  pl.BlockSpec(blk, lambda g, b, h: (b, 3 * HP + g * HP + h)),
                  pl.BlockSpec((2, BAND, 2 * BAND), lambda g, b, h: (g * HP + h, 0, 0))],
        out_specs=[pl.BlockSpec(blk, lambda g, b, h: (b, g * HP + h)),
                   pl.BlockSpec(blk, lambda g, b, h: (b, g * HP + h))],
        out_shape=[jax.ShapeDtypeStruct((T, W3), F32), jax.ShapeDtypeStruct((T, W3), F32)],
        scratch_shapes=[pltpu.VMEM((mmax, 2 * HEAD_DIM), F32), pltpu.VMEM((mmax, 2 * HEAD_DIM), F32)],
        compiler_params=_params(("arbitrary", "arbitrary", "arbitrary")),
    )(q, kv, kv, bias)


def _attn_merge(o3, l3, hw, *, name):
    T = o3.shape[0]
    tm = _pick(T, (256, 128))

    def body(o0, o1, o2, l0, l1, l2, o_ref, ob_ref, lse_ref):
        a0, a1, a2 = l0[...], l1[...], l2[...]
        m = jnp.maximum(jnp.maximum(a0, a1), a2)
        e0, e1, e2 = jnp.exp(a0 - m), jnp.exp(a1 - m), jnp.exp(a2 - m)
        z = e0 + e1 + e2
        o = (e0 * o0[...] + e1 * o1[...] + e2 * o2[...]) / z
        o_ref[...] = o
        ob_ref[...] = o.astype(ob_ref.dtype)
        lse_ref[...] = m + jnp.log(z)

    def col(g):
        return pl.BlockSpec((tm, hw), lambda i: (i, g))

    return _pallas(
        body, name=name, grid=(T // tm,),
        in_specs=[col(0), col(1), col(2), col(0), col(1), col(2)],
        out_specs=[_rows(tm, hw)] * 3,
        out_shape=[jax.ShapeDtypeStruct((T, hw), F32), jax.ShapeDtypeStruct((T, hw), MXU_DTYPE),
                   jax.ShapeDtypeStruct((T, hw), F32)],
        compiler_params=_params(("parallel",)),
    )(o3, o3, o3, l3, l3, l3)


def _attn_bwd(q, kv, do, o, lse, bias, L, hpg, *, name):
    T = q.shape[0]
    nb_ = T // L
    HP = hpg // 2
    W3 = 3 * hpg * HEAD_DIM
    mmax = L

    def group_body(dil, q_ref, k_ref, v_ref, do_ref, o_ref, l_ref, b_ref, dq_ref, dk_ref, dv_ref, ds_ref,
                   dqs, dks, dvs):
        M, NB = _attn_blocks(dil, L)
        first = lax.broadcasted_iota(jnp.int32, (1, 2 * HEAD_DIM), 1) < HEAD_DIM
        loaded = {}

        def load(r):
            rows = _row_sel(r, M, dil)
            qf = q_ref[rows, :] * 0.125
            qm = [jnp.where(first, qf, 0.0).astype(MXU_DTYPE), jnp.where(first, 0.0, qf).astype(MXU_DTYPE)]
            kr = k_ref[rows, :].astype(MXU_DTYPE)
            vr = v_ref[rows, :].astype(MXU_DTYPE)
            dof = do_ref[rows, :]
            dom = [jnp.where(first, dof, 0.0).astype(MXU_DTYPE), jnp.where(first, 0.0, dof).astype(MXU_DTYPE)]
            dod = dof * o_ref[rows, :]
            delta = [jnp.sum(jnp.where(first, dod, 0.0), axis=-1, keepdims=True),
                     jnp.sum(jnp.where(first, 0.0, dod), axis=-1, keepdims=True)]
            lr = l_ref[rows, :]
            lse = [lr[:, 0:1], lr[:, HEAD_DIM:HEAD_DIM + 1]]
            if NB > 1:
                dks[r * M:(r + 1) * M, :] = jnp.zeros((M, 2 * HEAD_DIM), F32)
                dvs[r * M:(r + 1) * M, :] = jnp.zeros((M, 2 * HEAD_DIM), F32)
            return qm, kr, vr, dom, delta, lse

        nt = (((1,), (1,)), ((), ()))
        tn = (((0,), (0,)), ((), ()))
        tasks = [(r, n) for r in range(dil) for n in range(NB)]
        nbatch = ATTN_BATCH[NB == 1]
        for t0 in range(0, len(tasks), nbatch):
            batch = tasks[t0:t0 + nbatch]
            for r, _ in batch:
                if r not in loaded:
                    loaded[r] = load(r)
            chains = [(r, n, hh) for r, n in batch for hh in range(2)]
            qs = lambda n: slice(n * BAND, (n + 1) * BAND)
            ks = lambda n: slice(0, BAND) if n == 0 else slice((n - 1) * BAND, (n + 1) * BAND)
            qb = [loaded[r][0][hh][qs(n), :] for r, n, hh in chains]
            dob = [loaded[r][3][hh][qs(n), :] for r, n, hh in chains]
            kb = [loaded[r][1][ks(n), :] for r, n, hh in chains]
            s = [lax.dot_general(a, b, nt, preferred_element_type=F32) + (b_ref[hh, :, BAND:] if n == 0 else b_ref[hh])
                 for a, b, (r, n, hh) in zip(qb, kb, chains)]
            dp = [lax.dot_general(a, loaded[r][2][ks(n), :], nt, preferred_element_type=F32)
                  for a, (r, n, hh) in zip(dob, chains)]
            p = [jnp.exp(t - loaded[r][5][hh][qs(n), :]) for t, (r, n, hh) in zip(s, chains)]
            ds = [a * (b - loaded[r][4][hh][qs(n), :]) for a, b, (r, n, hh) in zip(p, dp, chains)]
            for t, (r, n, hh) in zip(ds, chains):
                if n == 0:
                    ds_ref[hh, :, BAND:] += t
                else:
                    ds_ref[hh] += t
            dsm = [t.astype(MXU_DTYPE) for t in ds]
            dq = [jnp.dot(a, b, preferred_element_type=F32) for a, b in zip(dsm, kb)]
            dk = [lax.dot_general(a, b, tn, preferred_element_type=F32) for a, b in zip(dsm, qb)]
            dv = [lax.dot_general(a.astype(MXU_DTYPE), b, tn, preferred_element_type=F32) for a, b in zip(p, dob)]
            for i, (r, n) in enumerate(batch):
                dqs[r * M + n * BAND:r * M + (n + 1) * BAND, :] = jnp.where(first, dq[2 * i], dq[2 * i + 1]) * 0.125
                ksm = slice(r * M + ks(n).start, r * M + ks(n).stop)
                if NB > 1:
                    dks[ksm, :] += dk[2 * i] + dk[2 * i + 1]
                    dvs[ksm, :] += dv[2 * i] + dv[2 * i + 1]
                else:
                    dks[ksm, :] = dk[2 * i] + dk[2 * i + 1]
                    dvs[ksm, :] = dv[2 * i] + dv[2 * i + 1]
                if n == NB - 1:
                    rows = _row_sel(r, M, dil)
                    dq_ref[rows, :] = dqs[r * M:(r + 1) * M, :]
                    dk_ref[rows, :] = dks[r * M:(r + 1) * M, :]
                    dv_ref[rows, :] = dvs[r * M:(r + 1) * M, :]

    def body(q_ref, k_ref, v_ref, do_ref, o_ref, l_ref, b_ref, dq_ref, dk_ref, dv_ref, ds_ref, dqs, dks, dvs):
        g = pl.program_id(0)

        @pl.when(pl.program_id(2) == 0)
        def _():
            ds_ref[...] = jnp.zeros_like(ds_ref)

        for gi, dil in enumerate(DILATIONS):
            pl.when(g == gi)(functools.partial(group_body, dil, q_ref, k_ref, v_ref, do_ref, o_ref, l_ref, b_ref,
                                               dq_ref, dk_ref, dv_ref, ds_ref, dqs, dks, dvs))

    blk = (L, 2 * HEAD_DIM)
    gcol = lambda g, h, b: (b, g * HP + h)
    hcol = lambda g, h, b: (b, h)
    return _pallas(
        body, name=name, grid=(3, HP, nb_),
        in_specs=[pl.BlockSpec(blk, gcol), pl.BlockSpec(blk, gcol),
                  pl.BlockSpec(blk, lambda g, h, b: (b, 3 * HP + g * HP + h)),
                  pl.BlockSpec(blk, hcol), pl.BlockSpec(blk, hcol), pl.BlockSpec(blk, hcol),
                  pl.BlockSpec((2, BAND, 2 * BAND), lambda g, h, b: (g * HP + h, 0, 0))],
        out_specs=[pl.BlockSpec(blk, gcol), pl.BlockSpec(blk, gcol), pl.BlockSpec(blk, gcol),
                   pl.BlockSpec((2, BAND, 2 * BAND), lambda g, h, b: (g * HP + h, 0, 0))],
        out_shape=[jax.ShapeDtypeStruct((T, W3), F32), jax.ShapeDtypeStruct((T, W3), F32),
                   jax.ShapeDtypeStruct((T, W3), F32), jax.ShapeDtypeStruct((3 * hpg, BAND, 2 * BAND), F32)],
        scratch_shapes=[pltpu.VMEM((mmax, 2 * HEAD_DIM), F32)] * 3,
        compiler_params=_params(("arbitrary", "arbitrary", "arbitrary")),
    )(q, kv, kv, do, o, lse, bias)


def _bias_grad(ds_sum, hpg, *, name):
    nh = ds_sum.shape[0]
    idx = np.stack([np.where(_band_tables(dil)[1], _band_tables(dil)[0], -1) for dil in DILATIONS]).astype(np.int32)

    def body(ds_ref, idx_ref, o_ref):
        d = ds_ref[...]
        ix = idx_ref[...]
        lane = lax.broadcasted_iota(jnp.int32, (8, 128), 1)
        row = jnp.zeros((8, 128), F32)
        for b in range(REL_BUCKETS):
            row = row + jnp.where(lane == b, jnp.sum(jnp.where(ix == b, d, 0.0)), 0.0)
        o_ref[...] = row

    out = _pallas(
        body, name=name, grid=(nh,),
        in_specs=[pl.BlockSpec((None, BAND, 2 * BAND), lambda h: (h, 0, 0)),
                  pl.BlockSpec((None, BAND, 2 * BAND), lambda h: (h // hpg, 0, 0))],
        out_specs=pl.BlockSpec((None, 8, 128), lambda h: (h, 0, 0)),
        out_shape=jax.ShapeDtypeStruct((nh, 8, 128), F32),
        compiler_params=_params(("parallel",)),
    )(ds_sum, jnp.asarray(idx))
    return out[:, 0, :REL_BUCKETS].T


def _adamw(w, g, m, v, *, name):
    Rw, C = w.shape
    tm = _pick(Rw, (512, 352, 256, 128, 64, 32, 16, 8))

    def body(w_ref, g_ref, m_ref, v_ref, d_ref, nm_ref, nv_ref):
        gg = g_ref[...]
        nm = ADAM_B1 * m_ref[...] + (1.0 - ADAM_B1) * gg
        nv = ADAM_B2 * v_ref[...] + (1.0 - ADAM_B2) * (gg * gg)
        m_hat = nm / (1.0 - ADAM_B1 ** ADAM_STEP)
        v_hat = nv / (1.0 - ADAM_B2 ** ADAM_STEP)
        d_ref[...] = -ADAM_LR * (m_hat / (jnp.sqrt(v_hat) + ADAM_EPS) + ADAM_WD * w_ref[...])
        nm_ref[...] = nm
        nv_ref[...] = nv

    return _pallas(
        body, name=name, grid=(Rw // tm,), in_specs=[_rows(tm, C)] * 4, out_specs=[_rows(tm, C)] * 3,
        out_shape=[jax.ShapeDtypeStruct((Rw, C), F32)] * 3, compiler_params=_params(("parallel",)),
    )(w, g, m, v)


ROW_TILE_ELEMS = 256 * 1024


def _tile_rows(r, c):
    best = 8
    for t in range(8, r + 1, 8):
        if r % t == 0 and t * c <= ROW_TILE_ELEMS:
            best = t
    return best


def _adamw_halves(w, m, v, mine, other, cidx, *, layer=0, prev=None, name):
    NL, _, r, c = w.shape
    tm = _tile_rows(r, c)

    def body(c_ref, w_ref, m_ref, v_ref, a_ref, b_ref, *rest):
        g_ref, d_ref, nm_ref, nv_ref = rest[-4:]
        gg = jnp.where(pl.program_id(0) == c_ref[0], a_ref[...], b_ref[...])
        nm = ADAM_B1 * m_ref[...] + (1.0 - ADAM_B1) * gg
        nv = ADAM_B2 * v_ref[...] + (1.0 - ADAM_B2) * (gg * gg)
        m_hat = nm / (1.0 - ADAM_B1 ** ADAM_STEP)
        v_hat = nv / (1.0 - ADAM_B2 ** ADAM_STEP)
        g_ref[...] = gg
        d_ref[...] = -ADAM_LR * (m_hat / (jnp.sqrt(v_hat) + ADAM_EPS) + ADAM_WD * w_ref[...])
        nm_ref[...] = nm
        nv_ref[...] = nv

    half = pl.BlockSpec((None, None, tm, c), lambda h, i, cr: (layer, h, i, 0))
    one = pl.BlockSpec((None, tm, c), lambda h, i, cr: (0, i, 0))
    in_specs = [half, half, half, one, one]
    args = [cidx, w, m, v, mine, other]
    aliases = {}
    if prev is not None:
        in_specs += [_ANY] * 4
        args += list(prev)
        aliases = {6 + k: k for k in range(4)}
    spec = pltpu.PrefetchScalarGridSpec(num_scalar_prefetch=1, grid=(2, r // tm), in_specs=in_specs, out_specs=[half] * 4)
    return _pallas(
        body, name=name, grid_spec=spec, out_shape=[jax.ShapeDtypeStruct((NL, 2, r, c), F32)] * 4,
        input_output_aliases=aliases, compiler_params=_params(("parallel", "parallel")),
    )(*args)


def _pair_sum(g, theirs, cidx, *, cast, name):
    _, _, r, c = g.shape
    tm = _tile_rows(r, c)

    def body(c_ref, g_ref, t_ref, *outs):
        s = g_ref[...] + t_ref[...]
        outs[0][...] = s
        if cast:
            outs[1][...] = s.astype(BF16)

    blk = (None, None, tm, c)
    first = pl.BlockSpec(blk, lambda p, i, cr: (p, 0, i, 0))
    shapes = [jax.ShapeDtypeStruct((4, 1, r, c), F32)] + ([jax.ShapeDtypeStruct((4, 1, r, c), BF16)] if cast else [])
    spec = pltpu.PrefetchScalarGridSpec(
        num_scalar_prefetch=1, grid=(4, r // tm),
        in_specs=[pl.BlockSpec(blk, lambda p, i, cr: (p, cr[0], i, 0)), first], out_specs=[first] * len(shapes))
    return _pallas(body, name=name, grid_spec=spec, out_shape=shapes,
                   compiler_params=_params(("parallel", "parallel")))(cidx, g, theirs)


def _chip_sum(hf, got, chip_idx, *, name):
    _, _, r, c = hf.shape
    tm = _tile_rows(r, c)

    def body(p_ref, h_ref, r_ref, o_ref):
        s = h_ref[...]
        for k in range(3):
            s = s + r_ref[k].astype(F32)
        o_ref[...] = s

    spec = pltpu.PrefetchScalarGridSpec(
        num_scalar_prefetch=1, grid=(r // tm,),
        in_specs=[pl.BlockSpec((None, None, tm, c), lambda i, pr: (pr[0], 0, i, 0)),
                  pl.BlockSpec((3, None, tm, c), lambda i, pr: (0, 0, i, 0))],
        out_specs=pl.BlockSpec((None, tm, c), lambda i, pr: (0, i, 0)))
    return _pallas(body, name=name, grid_spec=spec, out_shape=jax.ShapeDtypeStruct((1, r, c), F32),
                   compiler_params=_params(("parallel",)))(chip_idx, hf, got)


def _place():
    x, y, c = lax.axis_index("x"), lax.axis_index("y"), lax.axis_index("c")
    chips = [(1 - x, y), (x, 1 - y), (1 - x, 1 - y)]
    return x, y, c, chips


_ANY = pl.BlockSpec(memory_space=pl.ANY)


def _comm_call(body, ins, out_shapes, n_remote, *, name, aliases=None):
    sems = [pltpu.SemaphoreType.DMA((n,)) for n in n_remote]
    return _pallas(
        body, name=name, in_specs=[_ANY] * len(ins), out_specs=[_ANY] * len(out_shapes), out_shape=out_shapes,
        scratch_shapes=sems, input_output_aliases=aliases or {},
        compiler_params=pltpu.CompilerParams(has_side_effects=True),
    )(*ins)


_HBM_SPEC = pl.BlockSpec(memory_space=pltpu.HBM)
_SEM_SPEC = pl.BlockSpec(memory_space=pltpu.SEMAPHORE)
_DATAFLOW = pltpu.SideEffectType.DATAFLOW_SIDE_EFFECTING


def _in_hbm(a):
    return pltpu.with_memory_space_constraint(a, pltpu.HBM)


def _gather_start(groups, *, name):
    flat = [s for g in groups for s in g]
    n, ng = len(flat), len(groups)

    def body(*refs):
        ins, lands = refs[:n], refs[n:2 * n]
        sems = refs[2 * n:2 * n + 2 * ng]
        token = refs[-1]
        x, y, c, chips = _place()
        me = 2 * x + y
        a = 0
        for gi, g in enumerate(groups):
            for j in range(len(g)):
                for k, (tx, ty) in enumerate(chips):
                    _rcopy(ins[a].at[c], lands[a].at[me, c], sems[2 * gi].at[3 * j + k], sems[2 * gi + 1].at[3 * j + k],
                           (tx, ty, c)).start()
                a += 1
        token[...] = jnp.zeros_like(token)

    land_shapes = [(4,) + s.shape for s in flat]
    out_shape = ([pltpu.SemaphoreType.DMA((3 * len(g),)) for g in groups for _ in range(2)]
                 + [pltpu.HBM(s.shape, s.dtype) for s in flat]
                 + [pltpu.HBM(ls, s.dtype) for ls, s in zip(land_shapes, flat)]
                 + [jax.ShapeDtypeStruct((8, 128), F32)])
    outs = _pallas(
        body, name=name, in_specs=[_HBM_SPEC] * (2 * n),
        out_specs=[_SEM_SPEC] * (2 * ng) + [_HBM_SPEC] * (2 * n) + [pl.BlockSpec(memory_space=pltpu.VMEM)],
        out_shape=out_shape, input_output_aliases={i: 2 * ng + i for i in range(2 * n)},
        compiler_params=pltpu.CompilerParams(has_side_effects=_DATAFLOW),
    )(*[_in_hbm(s) for s in flat], *[_in_hbm(lax.empty(ls, s.dtype)) for ls, s in zip(land_shapes, flat)])
    sems, thru, lands, token = outs[:2 * ng], outs[2 * ng:2 * ng + n], outs[2 * ng + n:2 * ng + 2 * n], outs[-1]
    res, a = [], 0
    for gi, g in enumerate(groups):
        res.append((sems[2 * gi], sems[2 * gi + 1], thru[a:a + len(g)], lands[a:a + len(g)]))
        a += len(g)
    return res, token


def _gather_wait(ssem, rsem, shards, lands, after, *, name):
    m = len(shards)

    def body(*refs):
        ins, lnd = refs[:m], refs[m:2 * m]
        ss, rs = refs[2 * m], refs[2 * m + 1]
        x, y, c, chips = _place()
        for j in range(m):
            for k, (tx, ty) in enumerate(chips):
                cp = _rcopy(ins[j].at[c], lnd[j].at[2 * tx + ty, c], ss.at[3 * j + k], rs.at[3 * j + k], (tx, ty, c))
                cp.wait_send()
                cp.wait_recv()

    outs = _pallas(
        body, name=name, in_specs=[_HBM_SPEC] * (2 * m) + [_SEM_SPEC, _SEM_SPEC, _ANY],
        out_specs=[_HBM_SPEC] * (2 * m),
        out_shape=[pltpu.HBM(s.shape, s.dtype) for s in shards] + [pltpu.HBM(l.shape, l.dtype) for l in lands],
        input_output_aliases={i: i for i in range(2 * m)},
        compiler_params=pltpu.CompilerParams(has_side_effects=_DATAFLOW),
    )(*shards, *lands, ssem, rsem, after)
    return outs[m:]


def _gather_forward(lands, *, name):
    n = len(lands)

    def body(*refs):
        outs = refs[n:2 * n]
        ssem, rsem = refs[2 * n:]
        x, y, c, chips = _place()
        sib = (x, y, 1 - c)
        cps = []
        for a in range(n):
            for k, (tx, ty) in enumerate(chips):
                pk = 2 * tx + ty
                cp = _rcopy(outs[a].at[pk, c], outs[a].at[pk, c], ssem.at[3 * a + k], rsem.at[3 * a + k], sib)
                cp.start()
                cps.append(cp)
        for a in range(n):
            for k, (tx, ty) in enumerate(chips):
                pk = 2 * tx + ty
                _rcopy(outs[a].at[pk, c], outs[a].at[pk, 1 - c], ssem.at[3 * a + k], rsem.at[3 * a + k], sib).wait_recv()
        for cp in cps:
            cp.wait_send()

    shapes = [jax.ShapeDtypeStruct(l.shape, l.dtype) for l in lands]
    return _comm_call(body, lands, shapes, [3 * n, 3 * n], name=name, aliases={i: i for i in range(n)})


def _gather_forward_start(lands, *, name):
    n = len(lands)

    def body(*refs):
        ssem, rsem = refs[n], refs[n + 1]
        outs = refs[n + 2:2 * n + 2]
        token = refs[-1]
        x, y, c, chips = _place()
        for a in range(n):
            for k, (tx, ty) in enumerate(chips):
                pk = 2 * tx + ty
                _rcopy(outs[a].at[pk, c], outs[a].at[pk, c], ssem.at[3 * a + k], rsem.at[3 * a + k], (x, y, 1 - c)).start()
        token[...] = jnp.zeros_like(token)

    outs = _pallas(
        body, name=name, in_specs=[_HBM_SPEC] * n,
        out_specs=[_SEM_SPEC] * 2 + [_HBM_SPEC] * n + [pl.BlockSpec(memory_space=pltpu.VMEM)],
        out_shape=([pltpu.SemaphoreType.DMA((3 * n,))] * 2 + [pltpu.HBM(l.shape, l.dtype) for l in lands]
                   + [jax.ShapeDtypeStruct((8, 128), F32)]),
        input_output_aliases={i: 2 + i for i in range(n)},
        compiler_params=pltpu.CompilerParams(has_side_effects=_DATAFLOW),
    )(*lands)
    return (outs[0], outs[1], outs[2:2 + n]), outs[-1]


def _gather_forward_wait(started, after, *, name):
    ssem, rsem, lands = started
    n = len(lands)

    def body(*refs):
        lnd = refs[:n]
        ss, rs = refs[n], refs[n + 1]
        x, y, c, chips = _place()
        sib = (x, y, 1 - c)
        for a in range(n):
            for k, (tx, ty) in enumerate(chips):
                pk = 2 * tx + ty
                _rcopy(lnd[a].at[pk, c], lnd[a].at[pk, 1 - c], ss.at[3 * a + k], rs.at[3 * a + k], sib).wait_recv()
                _rcopy(lnd[a].at[pk, c], lnd[a].at[pk, c], ss.at[3 * a + k], rs.at[3 * a + k], sib).wait_send()

    return _pallas(
        body, name=name, in_specs=[_HBM_SPEC] * n + [_SEM_SPEC, _SEM_SPEC, _ANY], out_specs=[_HBM_SPEC] * n,
        out_shape=[pltpu.HBM(l.shape, l.dtype) for l in lands], input_output_aliases={i: i for i in range(n)},
        compiler_params=pltpu.CompilerParams(has_side_effects=_DATAFLOW),
    )(*lands, ssem, rsem, after)


class _Lazy:
    def __init__(self, group_of, make, prepare):
        self._group_of, self._make, self._prepare, self._done, self._anchor = group_of, make, prepare, {}, None

    def anchor(self, value):
        self._anchor = value

    def prepare(self, key, value):
        return self._prepare(self._group_of[key], value)

    def __getitem__(self, key):
        g = self._group_of[key]
        if g not in self._done:
            self._done[g] = self._make(g, self._anchor)
        return self._done[g][key]


def _anchor(mapping, value):
    if isinstance(mapping, _Lazy):
        mapping.anchor(value)


def _prepare(mapping, key, value):
    return mapping.prepare(key, value)[0, 0] if isinstance(mapping, _Lazy) else 0.0


def _rcopy(src, dst, ssem, rsem, dev):
    return pltpu.make_async_remote_copy(src_ref=src, dst_ref=dst, send_sem=ssem, recv_sem=rsem,
                                        device_id=dev, device_id_type=MESH)


def _all_gather(shards, *, name):
    n = len(shards)

    def body(*refs):
        ins, outs = refs[:n], refs[n:2 * n]
        s_ici, r_ici, s_d2d, r_d2d = refs[2 * n:]
        x, y, c, chips = _place()
        me = 2 * x + y
        sib = (x, y, 1 - c)
        sends = []
        for a in range(n):
            for k, (tx, ty) in enumerate(chips):
                cp = _rcopy(ins[a].at[c], outs[a].at[me, c], s_ici.at[3 * a + k], r_ici.at[3 * a + k], (tx, ty, c))
                cp.start()
                sends.append(cp)
        for a in range(n):
            for k, (tx, ty) in enumerate(chips):
                pk = 2 * tx + ty
                _rcopy(ins[a].at[c], outs[a].at[pk, c], s_ici.at[3 * a + k], r_ici.at[3 * a + k], (tx, ty, c)).wait_recv()
                fw = _rcopy(outs[a].at[pk, c], outs[a].at[pk, c], s_d2d.at[3 * a + k], r_d2d.at[3 * a + k], sib)
                fw.start()
                sends.append(fw)
        for a in range(n):
            for k, (tx, ty) in enumerate(chips):
                pk = 2 * tx + ty
                _rcopy(ins[a].at[c], outs[a].at[pk, 1 - c], s_d2d.at[3 * a + k], r_d2d.at[3 * a + k], sib).wait_recv()
        for cp in sends:
            cp.wait_send()

    shapes = [jax.ShapeDtypeStruct((4,) + s.shape, s.dtype) for s in shards]
    return _comm_call(body, shards, shapes, [3 * n] * 4, name=name)


def _gather(shards, chip, *, name):
    outs = _all_gather(shards, name=name)
    return [lax.dynamic_update_slice(o, s[None], (chip, 0, 0, 0)) for o, s in zip(outs, shards)]


def _pair_send(gs, *, name):
    n = len(gs)

    def body(*refs):
        ins, theirs = refs[:n], refs[n:2 * n]
        ssem, rsem = refs[2 * n:]
        x, y, c, _ = _place()
        sib = (x, y, 1 - c)
        cps = []
        for a in range(n):
            cp = _rcopy(ins[a].at[:, pl.ds(1 - c, 1)], theirs[a], ssem.at[a], rsem.at[a], sib)
            cp.start()
            cps.append(cp)
        for cp in cps:
            cp.wait_send()
            cp.wait_recv()

    shapes = [jax.ShapeDtypeStruct((4, 1) + g.shape[2:], g.dtype) for g in gs]
    return _comm_call(body, gs, shapes, [n, n], name=name)


def _chip_exchange(hx, *, name):
    n = len(hx)

    def body(*refs):
        hxr, got = refs[:n], refs[n:2 * n]
        ssem, rsem = refs[2 * n:]
        x, y, c, chips = _place()
        cps = []
        for a in range(n):
            for k, (tx, ty) in enumerate(chips):
                cp = _rcopy(hxr[a].at[2 * tx + ty], got[a].at[k], ssem.at[3 * a + k], rsem.at[3 * a + k], (tx, ty, c))
                cp.start()
                cps.append(cp)
        for cp in cps:
            cp.wait_send()
            cp.wait_recv()

    shapes = [jax.ShapeDtypeStruct((3,) + h.shape[1:], h.dtype) for h in hx]
    return _comm_call(body, hx, shapes, [3 * n, 3 * n], name=name)


def _pair_swap(fs, *, name):
    n = len(fs)

    def body(*refs):
        ins, outs = refs[:n], refs[n:2 * n]
        ssem, rsem = refs[2 * n:]
        x, y, c, _ = _place()
        cps = []
        for a in range(n):
            cp = _rcopy(ins[a], outs[a], ssem.at[a], rsem.at[a], (x, y, 1 - c))
            cp.start()
            cps.append(cp)
        for cp in cps:
            cp.wait_send()
            cp.wait_recv()

    shapes = [jax.ShapeDtypeStruct(f.shape, f.dtype) for f in fs]
    return _comm_call(body, fs, shapes, [n, n], name=name)


def _chip_exchange_start(hx, *, name):
    n = len(hx)

    def body(*refs):
        ins, gots = refs[:n], refs[n:2 * n]
        ssem, rsem = refs[2 * n], refs[2 * n + 1]
        token = refs[-1]
        x, y, c, chips = _place()
        for a in range(n):
            for k, (tx, ty) in enumerate(chips):
                _rcopy(ins[a].at[2 * tx + ty], gots[a].at[k], ssem.at[3 * a + k], rsem.at[3 * a + k], (tx, ty, c)).start()
        token[...] = jnp.zeros_like(token)

    got_shapes = [(3,) + h.shape[1:] for h in hx]
    outs = _pallas(
        body, name=name, in_specs=[_HBM_SPEC] * (2 * n),
        out_specs=[_SEM_SPEC] * 2 + [_HBM_SPEC] * (2 * n) + [pl.BlockSpec(memory_space=pltpu.VMEM)],
        out_shape=([pltpu.SemaphoreType.DMA((3 * n,))] * 2 + [pltpu.HBM(h.shape, h.dtype) for h in hx]
                   + [pltpu.HBM(gs, h.dtype) for gs, h in zip(got_shapes, hx)] + [jax.ShapeDtypeStruct((8, 128), F32)]),
        input_output_aliases={i: 2 + i for i in range(2 * n)},
        compiler_params=pltpu.CompilerParams(has_side_effects=_DATAFLOW),
    )(*[_in_hbm(h) for h in hx], *[_in_hbm(lax.empty(gs, h.dtype)) for gs, h in zip(got_shapes, hx)])
    return (outs[0], outs[1], outs[2:2 + n], outs[2 + n:2 + 2 * n]), outs[-1]


def _chip_exchange_wait(started, after, *, name):
    ssem, rsem, hx, gots = started
    n = len(hx)

    def body(*refs):
        ins, gts = refs[:n], refs[n:2 * n]
        ss, rs = refs[2 * n], refs[2 * n + 1]
        x, y, c, chips = _place()
        for a in range(n):
            for k, (tx, ty) in enumerate(chips):
                cp = _rcopy(ins[a].at[2 * tx + ty], gts[a].at[k], ss.at[3 * a + k], rs.at[3 * a + k], (tx, ty, c))
                cp.wait_send()
                cp.wait_recv()

    outs = _pallas(
        body, name=name, in_specs=[_HBM_SPEC] * (2 * n) + [_SEM_SPEC, _SEM_SPEC, _ANY],
        out_specs=[_HBM_SPEC] * (2 * n),
        out_shape=[pltpu.HBM(h.shape, h.dtype) for h in hx] + [pltpu.HBM(g.shape, g.dtype) for g in gots],
        input_output_aliases={i: i for i in range(2 * n)},
        compiler_params=pltpu.CompilerParams(has_side_effects=_DATAFLOW),
    )(*hx, *gots, ssem, rsem, after)
    return outs[n:]


def _sent_part(ref, c, whole):
    return ref if whole else ref.at[:, pl.ds(1 - c, 1)]


def _pair_send_start(gs, *, name, whole=False):
    n = len(gs)

    def body(*refs):
        ins, lands = refs[:n], refs[n:2 * n]
        ssem, rsem = refs[2 * n], refs[2 * n + 1]
        token = refs[-1]
        x, y, c, _ = _place()
        for a in range(n):
            _rcopy(_sent_part(ins[a], c, whole), lands[a], ssem.at[a], rsem.at[a], (x, y, 1 - c)).start()
        token[...] = jnp.zeros_like(token)

    land_shapes = [g.shape if whole else (4, 1) + g.shape[2:] for g in gs]
    outs = _pallas(
        body, name=name, in_specs=[_HBM_SPEC] * (2 * n),
        out_specs=[_SEM_SPEC] * 2 + [_HBM_SPEC] * (2 * n) + [pl.BlockSpec(memory_space=pltpu.VMEM)],
        out_shape=([pltpu.SemaphoreType.DMA((n,))] * 2 + [pltpu.HBM(g.shape, g.dtype) for g in gs]
                   + [pltpu.HBM(ls, g.dtype) for ls, g in zip(land_shapes, gs)] + [jax.ShapeDtypeStruct((8, 128), F32)]),
        input_output_aliases={i: 2 + i for i in range(2 * n)},
        compiler_params=pltpu.CompilerParams(has_side_effects=_DATAFLOW),
    )(*[_in_hbm(g) for g in gs], *[_in_hbm(lax.empty(ls, g.dtype)) for ls, g in zip(land_shapes, gs)])
    return (outs[0], outs[1], outs[2:2 + n], outs[2 + n:2 + 2 * n]), outs[-1]


def _pair_send_wait(started, after, *, name, whole=False):
    ssem, rsem, gs, lands = started
    n = len(gs)
    afters = list(after) if isinstance(after, (list, tuple)) else [after]

    def body(*refs):
        ins, lnd = refs[:n], refs[n:2 * n]
        ss, rs = refs[2 * n], refs[2 * n + 1]
        x, y, c, _ = _place()
        for a in range(n):
            cp = _rcopy(_sent_part(ins[a], c, whole), lnd[a], ss.at[a], rs.at[a], (x, y, 1 - c))
            cp.wait_send()
            cp.wait_recv()

    outs = _pallas(
        body, name=name, in_specs=[_HBM_SPEC] * (2 * n) + [_SEM_SPEC, _SEM_SPEC] + [_ANY] * len(afters),
        out_specs=[_HBM_SPEC] * (2 * n),
        out_shape=[pltpu.HBM(g.shape, g.dtype) for g in gs] + [pltpu.HBM(l.shape, l.dtype) for l in lands],
        input_output_aliases={i: i for i in range(2 * n)},
        compiler_params=pltpu.CompilerParams(has_side_effects=_DATAFLOW),
    )(*gs, *lands, ssem, rsem, *afters)
    return list(outs[:n]), list(outs[n:])


def _pair_sums(grads, exch_bf16, cidx, tag, theirs=None):
    if theirs is None:
        theirs = _pair_send(grads, name=f"rs_pair_send_{tag}")
    hf, hx = [], []
    for a in range(len(grads)):
        res = _pair_sum(grads[a], theirs[a], cidx, cast=exch_bf16[a], name=f"rs_pair_sum_{tag}{a}")
        hf.append(res[0])
        hx.append(res[1] if exch_bf16[a] else res[0])
    return hf, hx


def _chip_sums(hf, got, chip_idx, tag):
    return [_chip_sum(hf[a], got[a], chip_idx, name=f"rs_chip_sum_{tag}{a}") for a in range(len(hf))]


def _interleave(a, B, L):
    return a.reshape(B, L, -1).transpose(1, 0, 2).reshape(B * L, -1)


def _deinterleave(a, B, L):
    return a.reshape(L, B, -1).transpose(1, 0, 2).reshape(B * L, -1)


def _local_step(x, tgt, W, S, on_grads=None):
    B, L, D = x.shape
    T = B * L
    G = D // SSM_GROUP
    Pst = SSM_STATE
    hpg = D // HEAD_DIM
    HW = hpg * HEAD_DIM
    ncl = G // GROUPS_PER_CLUSTER
    x2 = x.reshape(T, D)
    tgt2 = tgt.reshape(T, D)

    disc = lambda *p: _s5_discretize(*p)
    (ab_r, ab_i, bb_r, bb_i), disc_vjp = jax.vjp(disc, S["lam_re"], S["lam_im"], S["log_dt"], S["b_re"], S["b_im"])
    wb = jnp.concatenate([_blockdiag(jnp.transpose(bb_r, (0, 2, 1))), _blockdiag(jnp.transpose(bb_i, (0, 2, 1)))],
                         axis=-1).astype(MXU_DTYPE)
    wc = jnp.concatenate([_blockdiag(jnp.transpose(S["c_re"], (0, 2, 1))), _blockdiag(-jnp.transpose(S["c_im"], (0, 2, 1)))],
                         axis=1).astype(MXU_DTYPE)
    cs = GROUPS_PER_CLUSTER * Pst
    slab = lambda ab: jnp.tile(jnp.transpose(ab.reshape(ncl, cs // LANES, LANES), (1, 0, 2)), (1, B, 1))
    a_r, a_i = slab(ab_r), slab(ab_i)
    d_row = S["d"].reshape(1, D)

    xi = _interleave(x2, B, L)
    y, yg, h_r, h_i = _s5_fwd(xi, wb, wc, a_r, a_i, d_row, B, name="s5_fwd")
    _anchor(W, yg)
    z = _mm_nn(yg, W["w_glu"], bias=S["b_glu"].reshape(1, D), name="glu_z")
    gate = _glu_gate(y, z, name="glu_gate")
    mix_i = _mm_nn(gate, W["w_out"], name="s5_out")
    tok = _prepare(W, "w_up", mix_i)
    mix = _deinterleave(mix_i, B, L)
    h1, h1b, xh1, rs1 = _ln_fwd(x2, mix, S["ln_gain"][0, 0][None] + tok, S["ln_bias"][0, 0][None], name="ln_fwd_0a")

    def ffn_fwd(hb, l, prepare=None):
        hc = _mm_nn(hb, W["w_up"], l=l, out_dtype=MXU_DTYPE, name=f"ffn_up_{l}")
        tok = _prepare(W, prepare, hc) if prepare else 0.0
        a = _conv_glu_fwd(hc, S["conv_w"][l], S["conv_b"][l][None] + tok, L, name=f"ffn_conv_{l}")
        f = _mm_nn(a, W["w_down"], l=l, name=f"ffn_down_{l}")
        return hc, a, f

    _anchor(W, h1b)
    hc0, a0, f0 = ffn_fwd(h1b, 0, prepare="w_kv")
    h2, h2b, xh2, rs2 = _ln_fwd(h1, f0, S["ln_gain"][0, 1][None], S["ln_bias"][0, 1][None], name="ln_fwd_0b")

    _anchor(W, h2b)
    kv = _mm_nn(h2b, W["w_kv"], name="attn_kv")
    q = _mm_nn(h2b, W["w_q"], name="attn_q")
    bias = _attn_bias(S["rel_bias"], hpg)
    o3, l3 = _attn_fwd(q, kv, bias, L, hpg, name="attn_fwd")
    o, ob, lse = _attn_merge(o3, l3, HW, name="attn_merge")
    att = _mm_nn(ob, W["w_ao"], name="attn_out")
    h3, h3b, xh3, rs3 = _ln_fwd(h2, att, S["ln_gain"][1, 0][None], S["ln_bias"][1, 0][None], name="ln_fwd_1a")
    hc1, a1, f1 = ffn_fwd(h3b, 1)
    h4, _, xh4, rs4 = _ln_fwd(h3, f1, S["ln_gain"][1, 1][None], S["ln_bias"][1, 1][None], name="ln_fwd_1b")

    dh4, lrow = _loss_grad(h4, tgt2, name="loss")
    loss = lrow[0, 0]

    GW, GS = {}, {}

    def ffn_bwd(dzb, hb, hc, a, l):
        da = _mm_nt(dzb, W["w_down"], l=l, out_dtype=MXU_DTYPE, name=f"ffn_down_bwd_x_{l}")
        GW[f"w_down{l}"] = _tn(a, dzb, ptotal=1, np_cols=D, name=f"ffn_down_bwd_w_{l}")
        dc, dcw, dcb = _conv_glu_bwd(hc, da, S["conv_w"][l], S["conv_b"][l][None], L, name=f"ffn_conv_bwd_{l}")
        dhc = _conv_bwd_input(dc, S["conv_w"][l], L, name=f"ffn_conv_bwd_x_{l}")
        dh = _mm_nt(dhc, W["w_up"], l=l, name=f"ffn_up_bwd_x_{l}")
        GW[f"w_up{l}"] = _tn(hb, dhc, ptotal=W["w_up"].shape[0], np_cols=W["w_up"].shape[3], name=f"ffn_up_bwd_w_{l}")
        return dh, dcw, dcb

    dz4, dz4b, dg4, db4 = _ln_bwd([dh4], [1.0], xh4, rs4, S["ln_gain"][1, 1][None], name="ln_bwd_1b")
    dh3f, dcw1, dcb1 = ffn_bwd(dz4b, h3b, hc1, a1, 1)
    dz3, dz3b, dg3, db3 = _ln_bwd([dz4, dh3f], [DN_ALPHA, 1.0], xh3, rs3, S["ln_gain"][1, 0][None], name="ln_bwd_1a")
    do = _mm_nt(dz3b, W["w_ao"], name="attn_out_bwd_x")
    GW["w_ao"] = _tn(ob, dz3b, ptotal=1, np_cols=D, name="attn_out_bwd_w")
    dq, dk, dv, ds_sum = _attn_bwd(q, kv, do, o, lse, bias, L, hpg, name="attn_bwd")
    GS["rel_bias"] = _bias_grad(ds_sum, hpg, name="attn_bias_grad")
    GW["w_q"] = _tn(h2b, dq, ptotal=W["w_q"].shape[0], np_cols=W["w_q"].shape[3], name="attn_q_bwd_w")
    pkv, npkv = W["w_kv"].shape[0], W["w_kv"].shape[3]
    gkv = _tn(h2b, dk, ptotal=pkv, np_cols=npkv, p0=0, name="attn_k_bwd_w")
    GW["w_kv"] = _tn(h2b, dv, ptotal=pkv, np_cols=npkv, p0=pkv // 2, prev=gkv, name="attn_v_bwd_w")
    dh2q = _mm_nt(dq, W["w_q"], name="attn_q_bwd_x")
    dh2k = _mm_nt(dk, W["w_kv"], p0=0, pn=pkv // 2, name="attn_k_bwd_x")
    dh2v = _mm_nt(dv, W["w_kv"], p0=pkv // 2, pn=pkv // 2, name="attn_v_bwd_x")

    gain_0b = S["ln_gain"][0, 1][None]
    if on_grads is not None:
        gain_0b = gain_0b + on_grads(0, GW, dh2v)[0, 0]

    dz2, dz2b, dg2, db2 = _ln_bwd([dz3, dh2q, dh2k, dh2v], [DN_ALPHA, 1.0, 1.0, 1.0], xh2, rs2, gain_0b,
                                  name="ln_bwd_0b")
    dh1f, dcw0, dcb0 = ffn_bwd(dz2b, h1b, hc0, a0, 0)
    gain_0a = S["ln_gain"][0, 0][None]
    if on_grads is not None:
        gain_0a = gain_0a + on_grads(1, GW, GW["w_up0"])[0, 0]
    dz1, dz1b, dg1, db1 = _ln_bwd([dz2, dh1f], [DN_ALPHA, 1.0], xh1, rs1, gain_0a, name="ln_bwd_0a")
    dmix_i = _interleave(dz1b, B, L)
    dgate = _mm_nt(dmix_i, W["w_out"], name="s5_out_bwd_x")
    GW["w_out"] = _tn(gate, dmix_i, ptotal=1, np_cols=D, name="s5_out_bwd_w")
    dzg, dyg1, dbglu = _glu_bwd(y, z, dgate, name="glu_bwd")
    dyg2 = _mm_nt(dzg, W["w_glu"], name="glu_z_bwd_x")
    GW["w_glu"] = _tn(yg, dzg, ptotal=1, np_cols=D, name="glu_z_bwd_w")
    dy = _gelu_bwd(y, dyg1, dyg2, name="gelu_bwd")
    if on_grads is not None:
        d_row = d_row + on_grads(2, GW, GW["w_glu"])[0, 0]
    du_i, g_r, g_i, dar, dai, dd = _s5_bwd(dy, xi, h_r, h_i, wb, wc, a_r, a_i, d_row, B, name="s5_bwd")
    started = on_grads(3, GW, du_i) if on_grads is not None else None
    dwb_r = _cluster_tn(xi, g_r, ncl, tok_left=True, name="s5_b_grad_re", after=started)
    dwb_i = _cluster_tn(xi, g_i, ncl, tok_left=True, name="s5_b_grad_im")
    dwc_r = _cluster_tn(dy, h_r, ncl, tok_left=False, name="s5_c_grad_re")
    dwc_i = _cluster_tn(dy, h_i, ncl, tok_left=False, name="s5_c_grad_im")
    grad_x = _axpy(dz1, _deinterleave(du_i, B, L), DN_ALPHA, name="grad_x")

    dbb_r = jnp.transpose(_unblockdiag(dwb_r, SSM_GROUP, Pst), (0, 2, 1))
    dbb_i = jnp.transpose(_unblockdiag(dwb_i, SSM_GROUP, Pst), (0, 2, 1))
    unslab = lambda da: jnp.transpose(da.reshape(cs // LANES, B, ncl, LANES).sum(1), (1, 0, 2)).reshape(G, Pst)
    dab_r, dab_i = unslab(dar), unslab(dai)
    GS["lam_re"], GS["lam_im"], GS["log_dt"], GS["b_re"], GS["b_im"] = disc_vjp((dab_r, dab_i, dbb_r, dbb_i))
    GS["c_re"] = jnp.transpose(_unblockdiag(dwc_r, Pst, SSM_GROUP), (0, 2, 1))
    GS["c_im"] = -jnp.transpose(_unblockdiag(dwc_i, Pst, SSM_GROUP), (0, 2, 1))
    GS["d"] = dd.reshape(G, SSM_GROUP)
    GS["b_glu"] = dbglu.reshape(D)
    GS["conv_w"] = jnp.stack([dcw0, dcw1])
    GS["conv_b"] = jnp.stack([dcb0[0], dcb1[0]])
    GS["ln_gain"] = jnp.stack([jnp.stack([dg1[0], dg2[0]]), jnp.stack([dg3[0], dg4[0]])])
    GS["ln_bias"] = jnp.stack([jnp.stack([db1[0], db2[0]]), jnp.stack([db3[0], db4[0]])])
    return loss, grad_x.reshape(B, L, D), GW, GS


SMALL_REPLICATED = ("lam_re", "lam_im", "log_dt", "b_re", "b_im", "c_re", "c_im", "d", "rel_bias", "conv_b")
SMALL_SHARDED = ("b_glu", "conv_w", "ln_gain", "ln_bias")
SMALL_ORDER = SMALL_REPLICATED + SMALL_SHARDED


def _pack(arrs, lanes, row_mult):
    flat = jnp.concatenate([a.reshape(-1).astype(F32) for a in arrs])
    rows = -(-flat.shape[0] // lanes)
    rows = -(-rows // row_mult) * row_mult
    return jnp.pad(flat, (0, rows * lanes - flat.shape[0])).reshape(rows, lanes)


def _unpack(packed, shapes):
    flat = packed.reshape(-1)
    out, off = [], 0
    for s in shapes:
        n = int(np.prod(s))
        out.append(flat[off:off + n].reshape(s))
        off += n
    return out


def kernel(x, s5_lam_re, s5_lam_im, s5_log_dt, s5_b_re, s5_b_im, s5_c_re, s5_c_im, s5_d, s5_w_glu, s5_b_glu, s5_w_out, attn_w_kv, attn_w_q, attn_w_out, rel_bias, ffn_w_up, ffn_conv_w, ffn_conv_b, ffn_w_down, ln_gain, ln_bias, loss_target, m_s5_lam_re, m_s5_lam_im, m_s5_log_dt, m_s5_b_re, m_s5_b_im, m_s5_c_re, m_s5_c_im, m_s5_d, m_s5_w_glu, m_s5_b_glu, m_s5_w_out, m_attn_w_kv, m_attn_w_q, m_attn_w_out, m_rel_bias, m_ffn_w_up, m_ffn_conv_w, m_ffn_conv_b, m_ffn_w_down, m_ln_gain, m_ln_bias, v_s5_lam_re, v_s5_lam_im, v_s5_log_dt, v_s5_b_re, v_s5_b_im, v_s5_c_re, v_s5_c_im, v_s5_d, v_s5_w_glu, v_s5_b_glu, v_s5_w_out, v_attn_w_kv, v_attn_w_q, v_attn_w_out, v_rel_bias, v_ffn_w_up, v_ffn_conv_w, v_ffn_conv_b, v_ffn_w_down, v_ln_gain, v_ln_bias):
    names = ["s5_lam_re", "s5_lam_im", "s5_log_dt", "s5_b_re", "s5_b_im", "s5_c_re", "s5_c_im", "s5_d", "s5_w_glu",
             "s5_b_glu", "s5_w_out", "attn_w_kv", "attn_w_q", "attn_w_out", "rel_bias", "ffn_w_up", "ffn_conv_w",
             "ffn_conv_b", "ffn_w_down", "ln_gain", "ln_bias"]
    loc = locals()
    w_in = {n: loc[n] for n in names}
    m_in = {n: loc["m_" + n] for n in names}
    v_in = {n: loc["v_" + n] for n in names}
    chip = 2 * lax.axis_index("x") + lax.axis_index("y")
    core = lax.axis_index("c")
    chip_idx = jnp.reshape(chip, (1,)).astype(jnp.int32)
    cidx = jnp.reshape(core, (1,)).astype(jnp.int32)

    big = [("w_glu", "s5_w_glu", "rows"), ("w_out", "s5_w_out", "rows"), ("w_ao", "attn_w_out", "rows"),
           ("w_kv", "attn_w_kv", "cols"), ("w_q", "attn_w_q", "cols"),
           ("w_up", "ffn_w_up", "layer_cols"), ("w_down", "ffn_w_down", "layer_rows")]

    def halves(t, kind):
        if kind.startswith("layer"):
            return t
        r, c = t.shape[-2:]
        return t.reshape(2, r // 2, c)

    def to_weight(g, kind):
        _, _, r, c = g.shape
        if kind == "rows":
            return g.reshape(1, 1, 8 * r, c)
        if kind == "cols":
            return g.reshape(4, 1, 2 * r, c)
        if kind == "layer_cols":
            return g
        return jnp.transpose(g, (1, 0, 2, 3)).reshape(1, 2, 4 * r, c)

    small_sh = {"b_glu": s5_b_glu[0], "conv_w": ffn_conv_w, "ln_gain": ln_gain, "ln_bias": ln_bias}
    sh_shapes = [small_sh[k].shape for k in SMALL_SHARDED]
    sh_pack = _pack([small_sh[k] for k in SMALL_SHARDED], 128, 16)

    shards = [halves(w_in[src].astype(MXU_DTYPE), kind) for _, src, kind in big]
    shards.append(sh_pack.reshape(2, sh_pack.shape[0] // 2, 128))
    shard_of = {key: s for (key, _, _), s in zip(big, shards)}
    shard_of["small"] = shards[-1]
    kind_of = {key: kind for key, _, kind in big}

    group_keys = [["w_glu", "w_out", "small"], ["w_up", "w_down"], ["w_kv", "w_q", "w_ao"]]
    started, token = _gather_start([[shard_of[k] for k in g] for g in group_keys], name="weights_gather_start")

    forwarding = {}

    def prepare_group(gi, after):
        ssem, rsem, thru, lands = started[gi]
        lands = _gather_wait(ssem, rsem, thru, lands, after, name=f"weights_gather_wait_{gi}")
        forwarding[gi], tok = _gather_forward_start(lands, name=f"weights_gather_forward_start_{gi}")
        return tok

    def finish_group(gi, after):
        if gi in forwarding:
            lands = _gather_forward_wait(forwarding.pop(gi), after, name=f"weights_gather_forward_wait_{gi}")
        else:
            ssem, rsem, thru, lands = started[gi]
            lands = _gather_wait(ssem, rsem, thru, lands, after, name=f"weights_gather_wait_{gi}")
            lands = _gather_forward(lands, name=f"weights_gather_forward_{gi}")
        out = {}
        for key, land in zip(group_keys[gi], lands):
            full = lax.dynamic_update_slice(land, shard_of[key][None], (chip, 0, 0, 0))
            if key == "small":
                parts = [_unpack(full[p], sh_shapes) for p in range(4)]
                for i, k in enumerate(SMALL_SHARDED):
                    out[k] = jnp.concatenate([parts[p][i] for p in range(4)], axis=-1)
            else:
                out[key] = to_weight(full, kind_of[key])
        return out

    replicated = dict(lam_re=s5_lam_re[0], lam_im=s5_lam_im[0], log_dt=s5_log_dt[0], b_re=s5_b_re[0], b_im=s5_b_im[0],
                      c_re=s5_c_re[0], c_im=s5_c_im[0], rel_bias=rel_bias, conv_b=ffn_conv_b,
                      d=s5_d[0] + token[0, 0])
    group_of = {k: gi for gi, g in enumerate(group_keys) for k in g if k != "small"}
    group_of.update({k: 0 for k in SMALL_SHARDED})
    group_of.update({k: "replicated" for k in replicated})
    params = _Lazy(group_of, lambda g, after: replicated if g == "replicated" else finish_group(g, after), prepare_group)

    red = [("w_up1", "ffn_w_up", 1), ("w_down1", "ffn_w_down", 1), ("w_ao", "attn_w_out", 0), ("w_kv", "attn_w_kv", 0),
           ("w_q", "attn_w_q", 0), ("w_down0", "ffn_w_down", 0), ("w_up0", "ffn_w_up", 0), ("w_out", "s5_w_out", 0),
           ("w_glu", "s5_w_glu", 0)]
    stages = [red[:5], red[5:7], red[7:]]

    def grad_halves(gw, key, src):
        r, c = w_in[src].shape[-2:]
        return gw[key].reshape(4, 2, r // 2, c)

    sent, early = {}, []

    def on_grads(stage, gw, latest):
        tokens = []
        if stage > 0:
            tag = "abc"[stage - 1]
            ga, theirs = _pair_send_wait(sent.pop(stage - 1), latest, name=f"rs_pair_send_wait_{tag}")
            hf, hx = _pair_sums(ga, [True] * len(ga), cidx, tag, theirs)
            started, tok = _chip_exchange_start(hx, name=f"rs_chip_exchange_start_{tag}")
            early.append((hf, started, tag))
            tokens.append(tok)
        if stage < len(stages):
            ga = [grad_halves(gw, key, src) for key, src, _ in stages[stage]]
            sent[stage], tok = _pair_send_start(ga, name=f"rs_pair_send_start_{'abc'[stage]}")
            tokens.append(tok)
        return sum(tokens[1:], tokens[0])

    loss, grad_x, GW, GS = _local_step(x, loss_target, params, params, on_grads)

    gs_shapes = [GS[k].shape for k in SMALL_ORDER] + [(1,)]
    gs_pack = _pack([GS[k] for k in SMALL_ORDER] + [loss.reshape(1)], 128, 64)
    rs = gs_pack.shape[0] // 8
    gs_halves = [gs_pack.reshape(4, 2, rs, 128)]
    hf_s, hx_s = _pair_sums(gs_halves, [False], cidx, "s")
    started_s, after = _chip_exchange_start(hx_s, name="rs_chip_exchange_start_s")
    mine = []
    for hf, started, tag in early:
        got = _chip_exchange_wait(started, after, name=f"rs_chip_exchange_wait_{tag}")
        mine += _chip_sums(hf, got, chip_idx, tag)
        after = mine[-1]
    swapping, after = _pair_send_start(mine, name="rs_pair_swap_start", whole=True)
    mine_s = _chip_sums(hf_s, _chip_exchange_wait(started_s, after, name="rs_chip_exchange_wait_s"), chip_idx, "s")[0]
    other_s = _pair_swap([mine_s], name="rs_pair_swap_small")[0]
    small_halves = jnp.where(core == 0, jnp.concatenate([mine_s, other_s]), jnp.concatenate([other_s, mine_s]))
    small_all = _gather([small_halves], chip, name="small_grads_all_gather")[0]
    totals = _unpack(small_all, gs_shapes)
    gsmall = dict(zip(SMALL_ORDER, totals))
    loss = totals[-1][0]

    small_w = {"lam_re": s5_lam_re, "lam_im": s5_lam_im, "log_dt": s5_log_dt, "b_re": s5_b_re, "b_im": s5_b_im,
               "c_re": s5_c_re, "c_im": s5_c_im, "d": s5_d, "rel_bias": rel_bias, "conv_b": ffn_conv_b,
               "b_glu": s5_b_glu, "conv_w": ffn_conv_w, "ln_gain": ln_gain, "ln_bias": ln_bias}
    small_name = {"lam_re": "s5_lam_re", "lam_im": "s5_lam_im", "log_dt": "s5_log_dt", "b_re": "s5_b_re", "b_im": "s5_b_im",
                  "c_re": "s5_c_re", "c_im": "s5_c_im", "d": "s5_d", "rel_bias": "rel_bias", "conv_b": "ffn_conv_b",
                  "b_glu": "s5_b_glu", "conv_w": "ffn_conv_w", "ln_gain": "ln_gain", "ln_bias": "ln_bias"}
    sg = {}
    for k in SMALL_ORDER:
        shp = small_w[k].shape
        g = gsmall[k]
        if k in SMALL_SHARDED:
            width = shp[-1]
            g = lax.dynamic_slice_in_dim(g, chip * width, width, axis=g.ndim - 1)
        sg[k] = g.reshape(shp)
    sd, snm, snv = {}, {}, {}
    for k in SMALL_ORDER:
        shp = small_w[k].shape
        flat = lambda t: t.reshape(-1, shp[-1])
        r3 = _adamw(flat(small_w[k]), flat(sg[k]), flat(m_in[small_name[k]]), flat(v_in[small_name[k]]),
                    name=f"adamw_{k}")
        sd[k], snm[k], snv[k] = (t.reshape(shp) for t in r3)

    mine, other = _pair_send_wait(swapping, [sd[k] for k in SMALL_ORDER], name="rs_pair_swap_wait", whole=True)
    big_res = {}
    for (key, src, layer), gm, go in zip(red, mine, other):
        nl = w_in[src].shape[0] if src in ("ffn_w_up", "ffn_w_down") else 1
        r, c = w_in[src].shape[-2:]
        view = lambda t: t.reshape(nl, 2, r // 2, c)
        res4 = _adamw_halves(view(w_in[src]), view(m_in[src]), view(v_in[src]), gm, go, cidx, layer=layer,
                             prev=big_res.get(src), name=f"adamw_{key}")
        big_res[src] = res4
    big_res = {src: tuple(t.reshape(w_in[src].shape) for t in res4) for src, res4 in big_res.items()}

    def big_out(i):
        return {src: big_res[src][i] for _, src, _ in big}

    res = [{}, {}, {}, {}]
    for i in range(4):
        res[i].update(big_out(i))
    for k in SMALL_ORDER:
        res[0][small_name[k]] = sg[k]
        res[1][small_name[k]] = sd[k]
        res[2][small_name[k]] = snm[k]
        res[3][small_name[k]] = snv[k]
    outs = [loss, grad_x]
    for i in range(4):
        outs += [res[i][n] for n in names]
    return tuple(outs)
```

```python
import functools
import math

import numpy as np
import jax
import jax.numpy as jnp
from jax import lax
from jax.experimental import pallas as pl
from jax.experimental.pallas import tpu as pltpu

F32 = jnp.float32
BF16 = jnp.bfloat16
MXU_DTYPE = jnp.bfloat16
V7X_VMEM_LIMIT_BYTES = 52 << 20
MESH = pl.DeviceIdType.MESH

DEPTH = 2
SSM_GROUP = 16
SSM_STATE = 64
GROUPS_PER_CLUSTER = 16
CLUSTER_W = GROUPS_PER_CLUSTER * SSM_GROUP
HEAD_DIM = 64
DILATIONS = (1, 4, 16)
BAND = 128
ATTN_BATCH = (4, 8)
NEG_BIG = -1e30
REL_BUCKETS = 32
REL_MAX_DIST = 2048
DN_ALPHA = (2.0 * DEPTH) ** 0.25
LN_EPS = 1e-5
ADAM_LR, ADAM_B1, ADAM_B2, ADAM_EPS, ADAM_WD, ADAM_STEP = 0.001, 0.9, 0.999, 1e-08, 0.01, 10
GELU_K = math.sqrt(2.0 / math.pi)
GELU_C = 0.044715


def _pallas(body, **kw):
    return pl.pallas_call(body, **kw)


def _params(sem=None):
    return pltpu.CompilerParams(dimension_semantics=sem, vmem_limit_bytes=V7X_VMEM_LIMIT_BYTES)


def _pick(n, cands):
    for c in cands:
        if n % c == 0:
            return c
    return n


def _sigmoid(z):
    return 1.0 / (1.0 + jnp.exp(-z))


def _gelu(y):
    return 0.5 * y * (1.0 + jnp.tanh(GELU_K * (y + GELU_C * y * y * y)))


def _gelu_grad(y):
    t = jnp.tanh(GELU_K * (y + GELU_C * y * y * y))
    return 0.5 * (1.0 + t) + 0.5 * y * (1.0 - t * t) * (GELU_K * (1.0 + 3.0 * GELU_C * y * y))


def _mm_nn(a, w, *, l=0, bias=None, out_dtype=F32, name):
    T, K = a.shape
    P, _, _, Np = w.shape
    tm = _pick(T, (1024, 512, 256, 128))
    tn = _pick(Np, (1408, 1024, 768, 512, 384, 256, 128))
    nj = Np // tn

    def body(*refs):
        if bias is None:
            a_ref, w_ref, o_ref = refs
        else:
            a_ref, w_ref, b_ref, o_ref = refs
        acc = jnp.dot(a_ref[...].astype(MXU_DTYPE), w_ref[...].astype(MXU_DTYPE), preferred_element_type=F32)
        if bias is not None:
            acc = acc + b_ref[...]
        o_ref[...] = acc.astype(o_ref.dtype)

    in_specs = [pl.BlockSpec((tm, K), lambda p, j, i: (i, 0)),
                pl.BlockSpec((None, None, K, tn), lambda p, j, i: (p, l, 0, j))]
    args = [a, w]
    if bias is not None:
        in_specs.append(pl.BlockSpec((1, tn), lambda p, j, i: (0, p * nj + j)))
        args.append(bias)
    return _pallas(
        body, name=name, grid=(P, nj, T // tm), in_specs=in_specs,
        out_specs=pl.BlockSpec((tm, tn), lambda p, j, i: (i, p * nj + j)),
        out_shape=jax.ShapeDtypeStruct((T, P * Np), out_dtype),
        compiler_params=_params(("parallel", "parallel", "parallel")),
    )(*args)


def _mm_nt(a, w, *, l=0, p0=0, pn=None, out_dtype=F32, name):
    T = a.shape[0]
    _, _, K, Np = w.shape
    pn = w.shape[0] if pn is None else pn
    tm = _pick(T, (1024, 512, 256, 128) if K <= 1024 else (512, 256, 128))
    tn = _pick(Np, (1536, 1408, 1024, 768, 512, 384, 256, 128))
    nj = Np // tn
    nred = pn * nj

    def body(a_ref, w_ref, o_ref, acc):
        r = pl.program_id(1)

        @pl.when(r == 0)
        def _():
            acc[...] = jnp.zeros_like(acc)

        acc[...] += lax.dot_general(a_ref[...].astype(MXU_DTYPE), w_ref[...].astype(MXU_DTYPE),
                                    (((1,), (1,)), ((), ())), preferred_element_type=F32)

        @pl.when(r == nred - 1)
        def _():
            o_ref[...] = acc[...].astype(o_ref.dtype)

    return _pallas(
        body, name=name, grid=(T // tm, nred),
        in_specs=[pl.BlockSpec((tm, tn), lambda i, r: (i, r)),
                  pl.BlockSpec((None, None, K, tn), lambda i, r: (p0 + r // nj, l, 0, r % nj))],
        out_specs=pl.BlockSpec((tm, K), lambda i, r: (i, 0)),
        out_shape=jax.ShapeDtypeStruct((T, K), out_dtype),
        scratch_shapes=[pltpu.VMEM((tm, K), F32)],
        compiler_params=_params(("parallel", "arbitrary")),
    )(a, w)


def _tn(a, b, *, ptotal, np_cols, nl=1, l=0, p0=0, prev=None, name):
    T, K = a.shape
    Np = np_cols
    pn = b.shape[1] // Np
    tt = _pick(T, (1024, 512, 256, 128))
    tk = _pick(K, (1408, 1024, 512, 256, 128))
    tn = _pick(Np, (1408, 768, 512, 256, 128))
    if tk * tn > 1408 * 1024:
        tn = _pick(Np, (512, 256, 128))
    nj = Np // tn
    nt = T // tt

    def body(*refs):
        a_ref, b_ref = refs[0], refs[1]
        o_ref, acc = refs[-2], refs[-1]
        t = pl.program_id(3)

        @pl.when(t == 0)
        def _():
            acc[...] = jnp.zeros_like(acc)

        acc[...] += lax.dot_general(a_ref[...].astype(MXU_DTYPE), b_ref[...].astype(MXU_DTYPE),
                                    (((0,), (0,)), ((), ())), preferred_element_type=F32)

        @pl.when(t == nt - 1)
        def _():
            o_ref[...] = acc[...]

    in_specs = [pl.BlockSpec((tt, tk), lambda kb, p, j, t: (t, kb)),
                pl.BlockSpec((tt, tn), lambda kb, p, j, t: (t, p * nj + j))]
    args = [a, b]
    aliases = {}
    if prev is not None:
        in_specs.append(pl.BlockSpec(memory_space=pl.ANY))
        args.append(prev)
        aliases = {2: 0}
    return _pallas(
        body, name=name, grid=(K // tk, pn, nj, nt), in_specs=in_specs,
        out_specs=pl.BlockSpec((None, None, tk, tn), lambda kb, p, j, t: (p0 + p, l, kb, j)),
        out_shape=jax.ShapeDtypeStruct((ptotal, nl, K, Np), F32),
        scratch_shapes=[pltpu.VMEM((tk, tn), F32)],
        input_output_aliases=aliases,
        compiler_params=_params(("parallel", "parallel", "parallel", "arbitrary")),
    )(*args)


def _rows(tm, f):
    return pl.BlockSpec((tm, f), lambda i: (i, 0))


def _whole(shape):
    nd = len(shape)
    return pl.BlockSpec(shape, lambda i: (0,) * nd)


def _ln_fwd(xres, f, gain, bias, *, name):
    T, D = xres.shape
    tm = _pick(T, (256, 128))

    def body(x_ref, f_ref, g_ref, b_ref, y_ref, yb_ref, xh_ref, rs_ref):
        z = DN_ALPHA * x_ref[...] + f_ref[...]
        mu = jnp.mean(z, axis=-1, keepdims=True)
        zc = z - mu
        var = jnp.mean(zc * zc, axis=-1, keepdims=True)
        rstd = lax.rsqrt(var + LN_EPS)
        xh = zc * rstd
        y = xh * g_ref[...] + b_ref[...]
        y_ref[...] = y
        yb_ref[...] = y.astype(yb_ref.dtype)
        xh_ref[...] = xh
        rs_ref[...] = rstd

    return _pallas(
        body, name=name, grid=(T // tm,),
        in_specs=[_rows(tm, D), _rows(tm, D), _whole((1, D)), _whole((1, D))],
        out_specs=[_rows(tm, D), _rows(tm, D), _rows(tm, D), _rows(tm, 1)],
        out_shape=[jax.ShapeDtypeStruct((T, D), F32), jax.ShapeDtypeStruct((T, D), MXU_DTYPE),
                   jax.ShapeDtypeStruct((T, D), F32), jax.ShapeDtypeStruct((T, 1), F32)],
        compiler_params=_params(("parallel",)),
    )(xres, f, gain, bias)


def _ln_bwd(addends, coefs, xhat, rstd, gain, *, name):
    T, D = xhat.shape
    tm = _pick(T, (256, 128))
    n = len(addends)

    def body(*refs):
        adds = refs[:n]
        xh_ref, rs_ref, g_ref, dz_ref, dzb_ref, dg_ref, db_ref = refs[n:]
        dy = coefs[0] * adds[0][...]
        for c, r in zip(coefs[1:], adds[1:]):
            dy = dy + c * r[...]
        xh = xh_ref[...]
        dxh = dy * g_ref[...]
        m1 = jnp.mean(dxh, axis=-1, keepdims=True)
        m2 = jnp.mean(dxh * xh, axis=-1, keepdims=True)
        dz = rs_ref[...] * (dxh - m1 - xh * m2)
        dz_ref[...] = dz
        dzb_ref[...] = dz.astype(dzb_ref.dtype)

        @pl.when(pl.program_id(0) == 0)
        def _():
            dg_ref[...] = jnp.zeros_like(dg_ref)
            db_ref[...] = jnp.zeros_like(db_ref)

        dg_ref[...] += jnp.sum(dy * xh, axis=0, keepdims=True)
        db_ref[...] += jnp.sum(dy, axis=0, keepdims=True)

    return _pallas(
        body, name=name, grid=(T // tm,),
        in_specs=[_rows(tm, D)] * n + [_rows(tm, D), _rows(tm, 1), _whole((1, D))],
        out_specs=[_rows(tm, D), _rows(tm, D), _whole((1, D)), _whole((1, D))],
        out_shape=[jax.ShapeDtypeStruct((T, D), F32), jax.ShapeDtypeStruct((T, D), MXU_DTYPE),
                   jax.ShapeDtypeStruct((1, D), F32), jax.ShapeDtypeStruct((1, D), F32)],
        compiler_params=_params(("arbitrary",)),
    )(*addends, xhat, rstd, gain)


def _loss_grad(y, tgt, *, name):
    T, D = y.shape
    tm = _pick(T, (256, 128))

    def body(y_ref, t_ref, dy_ref, l_ref):
        e = y_ref[...] - t_ref[...]
        dy_ref[...] = e * (1.0 / D)

        @pl.when(pl.program_id(0) == 0)
        def _():
            l_ref[...] = jnp.zeros_like(l_ref)

        l_ref[...] += jnp.zeros_like(l_ref) + jnp.sum(e * e) * (0.5 / D)

    return _pallas(
        body, name=name, grid=(T // tm,),
        in_specs=[_rows(tm, D), _rows(tm, D)],
        out_specs=[_rows(tm, D), _whole((1, 128))],
        out_shape=[jax.ShapeDtypeStruct((T, D), F32), jax.ShapeDtypeStruct((1, 128), F32)],
        compiler_params=_params(("arbitrary",)),
    )(y, tgt)


def _axpy(a, b, ca, *, name):
    T, D = a.shape
    tm = _pick(T, (256, 128))

    def body(a_ref, b_ref, o_ref):
        o_ref[...] = ca * a_ref[...] + b_ref[...]

    return _pallas(
        body, name=name, grid=(T // tm,), in_specs=[_rows(tm, D), _rows(tm, D)], out_specs=_rows(tm, D),
        out_shape=jax.ShapeDtypeStruct((T, D), F32), compiler_params=_params(("parallel",)),
    )(a, b)


def _glu_gate(y, z, *, name):
    T, D = y.shape
    tm = _pick(T, (256, 128))

    def body(y_ref, z_ref, g_ref):
        g_ref[...] = (_gelu(y_ref[...]) * _sigmoid(z_ref[...])).astype(g_ref.dtype)

    return _pallas(
        body, name=name, grid=(T // tm,), in_specs=[_rows(tm, D), _rows(tm, D)], out_specs=_rows(tm, D),
        out_shape=jax.ShapeDtypeStruct((T, D), MXU_DTYPE), compiler_params=_params(("parallel",)),
    )(y, z)


def _glu_bwd(y, z, dg, *, name):
    T, D = y.shape
    tm = _pick(T, (256, 128))

    def body(y_ref, z_ref, dg_ref, dzb_ref, dyg_ref, db_ref):
        s = _sigmoid(z_ref[...])
        dg = dg_ref[...]
        dz = dg * _gelu(y_ref[...]) * s * (1.0 - s)
        dzb_ref[...] = dz.astype(dzb_ref.dtype)
        dyg_ref[...] = dg * s

        @pl.when(pl.program_id(0) == 0)
        def _():
            db_ref[...] = jnp.zeros_like(db_ref)

        db_ref[...] += jnp.sum(dz, axis=0, keepdims=True)

    return _pallas(
        body, name=name, grid=(T // tm,), in_specs=[_rows(tm, D)] * 3,
        out_specs=[_rows(tm, D), _rows(tm, D), _whole((1, D))],
        out_shape=[jax.ShapeDtypeStruct((T, D), MXU_DTYPE), jax.ShapeDtypeStruct((T, D), F32),
                   jax.ShapeDtypeStruct((1, D), F32)],
        compiler_params=_params(("arbitrary",)),
    )(y, z, dg)


def _gelu_bwd(y, d1, d2, *, name):
    T, D = y.shape
    tm = _pick(T, (256, 128))

    def body(y_ref, a_ref, b_ref, o_ref):
        o_ref[...] = (a_ref[...] + b_ref[...]) * _gelu_grad(y_ref[...])

    return _pallas(
        body, name=name, grid=(T // tm,), in_specs=[_rows(tm, D)] * 3, out_specs=_rows(tm, D),
        out_shape=jax.ShapeDtypeStruct((T, D), F32), compiler_params=_params(("parallel",)),
    )(y, d1, d2)


CONV_ROWS = 128
CONV_EDGE = 16


def _row_shifts(x, edge, drop_edge, tm, back):
    keep = jnp.where(drop_edge, 0.0, 1.0).astype(edge.dtype)
    ext = jnp.concatenate([edge * keep, x] if back else [x, edge * keep], axis=0)
    row = lax.broadcasted_iota(jnp.int32, (tm, tm + CONV_EDGE), 0)
    col = lax.broadcasted_iota(jnp.int32, (tm, tm + CONV_EDGE), 1)
    base = row + CONV_EDGE if back else row
    out = []
    for k in (1, 2):
        pick = (col == (base - k if back else base + k)).astype(x.dtype)
        out.append(jnp.dot(pick, ext, preferred_element_type=F32))
    return out


def _conv_specs(T, F2, tm):
    return [_rows(tm, F2),
            pl.BlockSpec((CONV_EDGE, F2), lambda i: (jnp.maximum(i * (tm // CONV_EDGE) - 1, 0), 0))]


def _conv_glu_fwd(hc, conv_w, conv_b, L, *, name):
    T, F2 = hc.shape
    F = F2 // 2
    tm = CONV_ROWS

    def body(x_ref, e_ref, w_ref, b_ref, a_ref):
        at_start = (pl.program_id(0) * tm) % L == 0
        x1, x2 = _row_shifts(x_ref[...], e_ref[...], at_start, tm, True)
        x = x_ref[...].astype(F32)
        c = b_ref[...] + w_ref[0:1, :] * x + w_ref[1:2, :] * x1 + w_ref[2:3, :] * x2
        val, gate = c[:, :F], c[:, F:]
        a_ref[...] = (gate * _sigmoid(gate) * val).astype(a_ref.dtype)

    return _pallas(
        body, name=name, grid=(T // tm,),
        in_specs=_conv_specs(T, F2, tm) + [_whole((3, F2)), _whole((1, F2))],
        out_specs=_rows(tm, F),
        out_shape=jax.ShapeDtypeStruct((T, F), MXU_DTYPE), compiler_params=_params(("parallel",)),
    )(hc, hc, conv_w, conv_b)


def _conv_glu_bwd(hc, da, conv_w, conv_b, L, *, name):
    T, F2 = hc.shape
    F = F2 // 2
    tm = CONV_ROWS

    def body(x_ref, e_ref, da_ref, w_ref, b_ref, dc_ref, dw_ref, db_ref):
        at_start = (pl.program_id(0) * tm) % L == 0
        x1, x2 = _row_shifts(x_ref[...], e_ref[...], at_start, tm, True)
        x = x_ref[...].astype(F32)
        c = b_ref[...] + w_ref[0:1, :] * x + w_ref[1:2, :] * x1 + w_ref[2:3, :] * x2
        val, gate = c[:, :F], c[:, F:]
        s = _sigmoid(gate)
        da = da_ref[...].astype(F32)
        dval = da * (gate * s)
        dgate = da * val * (s * (1.0 + gate * (1.0 - s)))
        dc = jnp.concatenate([dval, dgate], axis=-1)
        dc_ref[...] = dc.astype(dc_ref.dtype)

        @pl.when(pl.program_id(0) == 0)
        def _():
            dw_ref[...] = jnp.zeros_like(dw_ref)
            db_ref[...] = jnp.zeros_like(db_ref)

        dw_ref[0:1, :] += jnp.sum(dc * x, axis=0, keepdims=True)
        dw_ref[1:2, :] += jnp.sum(dc * x1, axis=0, keepdims=True)
        dw_ref[2:3, :] += jnp.sum(dc * x2, axis=0, keepdims=True)
        db_ref[...] += jnp.sum(dc, axis=0, keepdims=True)

    return _pallas(
        body, name=name, grid=(T // tm,),
        in_specs=_conv_specs(T, F2, tm) + [_rows(tm, F), _whole((3, F2)), _whole((1, F2))],
        out_specs=[_rows(tm, F2), _whole((3, F2)), _whole((1, F2))],
        out_shape=[jax.ShapeDtypeStruct((T, F2), MXU_DTYPE), jax.ShapeDtypeStruct((3, F2), F32),
                   jax.ShapeDtypeStruct((1, F2), F32)],
        compiler_params=_params(("arbitrary",)),
    )(hc, hc, da, conv_w, conv_b)


def _conv_bwd_input(dc, conv_w, L, *, name):
    T, F2 = dc.shape
    tm = CONV_ROWS
    edge = CONV_EDGE
    last_blk = T // edge - 1

    def body(x_ref, e_ref, w_ref, o_ref):
        at_end = ((pl.program_id(0) + 1) * tm) % L == 0
        x1, x2 = _row_shifts(x_ref[...], e_ref[...], at_end, tm, False)
        x = x_ref[...].astype(F32)
        o_ref[...] = (w_ref[0:1, :] * x + w_ref[1:2, :] * x1 + w_ref[2:3, :] * x2).astype(o_ref.dtype)

    return _pallas(
        body, name=name, grid=(T // tm,),
        in_specs=[_rows(tm, F2),
                  pl.BlockSpec((edge, F2), lambda i: (jnp.minimum((i + 1) * (tm // edge), last_blk), 0)),
                  _whole((3, F2))],
        out_specs=_rows(tm, F2),
        out_shape=jax.ShapeDtypeStruct((T, F2), MXU_DTYPE), compiler_params=_params(("parallel",)),
    )(dc, dc, conv_w)


S5_CHUNK = 128
LANES = 128


def _slab_rows(c, n, ncl):
    return pl.ds(c, n) if ncl == 1 else pl.ds(c, n, stride=ncl)


def _slab_put(ref, c, n, ncl, val):
    for s in range(val.shape[1] // LANES):
        ref[s, _slab_rows(c, n, ncl), :] = val[:, s * LANES:(s + 1) * LANES]


def _slab_get(ref, c, n, ncl):
    return jnp.concatenate([ref[s, _slab_rows(c, n, ncl), :] for s in range(ref.shape[0])], axis=-1)


def _slabs(n_slab, rows):
    return pl.BlockSpec((n_slab, rows, LANES), lambda i: (0, i, 0))


def _s5_fwd(xi, wb, wc, a_r, a_i, d_row, B, *, name):
    T, D = xi.shape
    ncl = wb.shape[0]
    cs = wb.shape[2] // 2
    ns = cs // LANES
    R = B * ncl
    Q = S5_CHUNK
    QR = Q * ncl
    nsteps = Q // B

    def body(x_ref, wb_ref, wc_ref, ar_ref, ai_ref, d_ref, y_ref, yg_ref, hr_ref, hi_ref, bur, bui, cr, ci):
        @pl.when(pl.program_id(0) == 0)
        def _():
            cr[...] = jnp.zeros_like(cr)
            ci[...] = jnp.zeros_like(ci)

        x = x_ref[...]
        xb = x.astype(MXU_DTYPE)
        for c in range(ncl):
            bu = jnp.dot(xb[:, c * CLUSTER_W:(c + 1) * CLUSTER_W], wb_ref[c], preferred_element_type=F32)
            _slab_put(bur, c, Q, ncl, bu[:, :cs])
            _slab_put(bui, c, Q, ncl, bu[:, cs:])
        ar = ar_ref[...]
        ai = ai_ref[...]

        def step(k, carry):
            hr, hi = carry
            sl = pl.ds(pl.multiple_of(k * R, R), R)
            nr = ar * hr - ai * hi + bur[:, sl, :]
            ni = ar * hi + ai * hr + bui[:, sl, :]
            hr_ref[:, sl, :] = nr
            hi_ref[:, sl, :] = ni
            return nr, ni

        hr, hi = lax.fori_loop(0, nsteps, step, (cr[...], ci[...]), unroll=4)
        cr[...] = hr
        ci[...] = hi
        parts = []
        for c in range(ncl):
            hrc = _slab_get(hr_ref, c, Q, ncl).astype(MXU_DTYPE)
            hic = _slab_get(hi_ref, c, Q, ncl).astype(MXU_DTYPE)
            parts.append(jnp.dot(hrc, wc_ref[c, :cs, :], preferred_element_type=F32)
                         + jnp.dot(hic, wc_ref[c, cs:, :], preferred_element_type=F32))
        y = d_ref[...] * x + (parts[0] if ncl == 1 else jnp.concatenate(parts, axis=-1))
        y_ref[...] = y
        yg_ref[...] = _gelu(y).astype(yg_ref.dtype)

    return _pallas(
        body, name=name, grid=(T // Q,),
        in_specs=[_rows(Q, D), _whole(wb.shape), _whole(wc.shape), _whole((ns, R, LANES)), _whole((ns, R, LANES)),
                  _whole((1, D))],
        out_specs=[_rows(Q, D), _rows(Q, D), _slabs(ns, QR), _slabs(ns, QR)],
        out_shape=[jax.ShapeDtypeStruct((T, D), F32), jax.ShapeDtypeStruct((T, D), MXU_DTYPE),
                   jax.ShapeDtypeStruct((ns, T * ncl, LANES), F32), jax.ShapeDtypeStruct((ns, T * ncl, LANES), F32)],
        scratch_shapes=[pltpu.VMEM((ns, QR, LANES), F32), pltpu.VMEM((ns, QR, LANES), F32),
                        pltpu.VMEM((ns, R, LANES), F32), pltpu.VMEM((ns, R, LANES), F32)],
        compiler_params=_params(("arbitrary",)),
    )(xi, wb, wc, a_r, a_i, d_row)


def _s5_bwd(dy, xi, h_r, h_i, wb, wc, a_r, a_i, d_row, B, *, name):
    T, D = dy.shape
    ncl = wb.shape[0]
    cs = wb.shape[2] // 2
    ns = cs // LANES
    R = B * ncl
    Q = S5_CHUNK
    nsteps = Q // B
    nchunk = T // Q
    QR = Q * ncl

    def rev(i):
        return nchunk - 1 - i

    def body(dy_ref, x_ref, hr_ref, hi_ref, pr_ref, pi_ref, wb_ref, wc_ref, ar_ref, ai_ref, d_ref,
             du_ref, gr_ref, gi_ref, dar_ref, dai_ref, dd_ref, dhr, dhi, cr, ci):
        i = pl.program_id(0)

        @pl.when(i == 0)
        def _():
            cr[...] = jnp.zeros_like(cr)
            ci[...] = jnp.zeros_like(ci)
            dar_ref[...] = jnp.zeros_like(dar_ref)
            dai_ref[...] = jnp.zeros_like(dai_ref)
            dd_ref[...] = jnp.zeros_like(dd_ref)

        dyv = dy_ref[...]
        dyb = dyv.astype(MXU_DTYPE)
        for c in range(ncl):
            dh = lax.dot_general(dyb[:, c * CLUSTER_W:(c + 1) * CLUSTER_W], wc_ref[c],
                                 (((1,), (1,)), ((), ())), preferred_element_type=F32)
            _slab_put(dhr, c, Q, ncl, dh[:, :cs])
            _slab_put(dhi, c, Q, ncl, dh[:, cs:])
        ar = ar_ref[...]
        ai = ai_ref[...]

        def step(j, carry):
            gr, gi = carry
            k = nsteps - 1 - j
            sl = pl.ds(pl.multiple_of(k * R, R), R)
            ngr = dhr[:, sl, :] + ar * gr + ai * gi
            ngi = dhi[:, sl, :] - ai * gr + ar * gi
            gr_ref[:, sl, :] = ngr
            gi_ref[:, sl, :] = ngi
            return ngr, ngi

        gr, gi = lax.fori_loop(0, nsteps, step, (cr[...], ci[...]), unroll=4)
        cr[...] = gr
        ci[...] = gi
        keep = jnp.where(i == nchunk - 1, 0.0, 1.0)
        hpr = jnp.concatenate([pr_ref[:, 8 - R:8, :] * keep, hr_ref[:, 0:QR - R, :]], axis=1)
        hpi = jnp.concatenate([pi_ref[:, 8 - R:8, :] * keep, hi_ref[:, 0:QR - R, :]], axis=1)
        gra, gia = gr_ref[...], gi_ref[...]
        steps = lambda t: jnp.sum(t.reshape(ns, nsteps, R, LANES), axis=1)
        dar_ref[...] += steps(gra * hpr + gia * hpi)
        dai_ref[...] += steps(gia * hpr - gra * hpi)
        parts = []
        for c in range(ncl):
            grc = _slab_get(gr_ref, c, Q, ncl).astype(MXU_DTYPE)
            gic = _slab_get(gi_ref, c, Q, ncl).astype(MXU_DTYPE)
            parts.append(lax.dot_general(grc, wb_ref[c, :, :cs], (((1,), (1,)), ((), ())), preferred_element_type=F32)
                         + lax.dot_general(gic, wb_ref[c, :, cs:], (((1,), (1,)), ((), ())), preferred_element_type=F32))
        du_ref[...] = d_ref[...] * dyv + (parts[0] if ncl == 1 else jnp.concatenate(parts, axis=-1))
        dd_ref[...] += jnp.sum(dyv * x_ref[...], axis=0, keepdims=True)

    tok = pl.BlockSpec((Q, D), lambda i: (rev(i), 0))
    st = pl.BlockSpec((ns, QR, LANES), lambda i: (0, rev(i), 0))
    before = pl.BlockSpec((ns, 8, LANES), lambda i: (0, jnp.maximum(rev(i) * (QR // 8) - 1, 0), 0))
    acc = _whole((ns, R, LANES))
    return _pallas(
        body, name=name, grid=(nchunk,),
        in_specs=[tok, tok, st, st, before, before, _whole(wb.shape), _whole(wc.shape), acc, acc, _whole((1, D))],
        out_specs=[tok, st, st, acc, acc, _whole((1, D))],
        out_shape=[jax.ShapeDtypeStruct((T, D), F32),
                   jax.ShapeDtypeStruct((ns, T * ncl, LANES), F32), jax.ShapeDtypeStruct((ns, T * ncl, LANES), F32),
                   jax.ShapeDtypeStruct((ns, R, LANES), F32), jax.ShapeDtypeStruct((ns, R, LANES), F32),
                   jax.ShapeDtypeStruct((1, D), F32)],
        scratch_shapes=[pltpu.VMEM((ns, QR, LANES), F32)] * 2 + [pltpu.VMEM((ns, R, LANES), F32)] * 2,
        compiler_params=_params(("arbitrary",)),
    )(dy, xi, h_r, h_i, h_r, h_i, wb, wc, a_r, a_i, d_row)


def _cluster_tn(tok, st, ncl, *, tok_left, name, after=None):
    T = tok.shape[0]
    ns = st.shape[0]
    cs = ns * LANES
    tt = _pick(T, (512, 256, 128))
    nt = T // tt
    oshape = (ncl, CLUSTER_W, cs) if tok_left else (ncl, cs, CLUSTER_W)

    def body(tok_ref, st_ref, *rest):
        o_ref, acc = rest[-2:]
        t = pl.program_id(0)

        @pl.when(t == 0)
        def _():
            acc[...] = jnp.zeros_like(acc)

        tk = tok_ref[...].astype(MXU_DTYPE)
        for c in range(ncl):
            tc = tk[:, c * CLUSTER_W:(c + 1) * CLUSTER_W]
            sc = _slab_get(st_ref, c, tt, ncl).astype(MXU_DTYPE)
            lhs, rhs = (tc, sc) if tok_left else (sc, tc)
            acc[c] += lax.dot_general(lhs, rhs, (((0,), (0,)), ((), ())), preferred_element_type=F32)

        @pl.when(t == nt - 1)
        def _():
            o_ref[...] = acc[...]

    return _pallas(
        body, name=name, grid=(nt,),
        in_specs=[_rows(tt, tok.shape[1]), _slabs(ns, tt * ncl)] + ([] if after is None else [_ANY]),
        out_specs=_whole(oshape),
        out_shape=jax.ShapeDtypeStruct(oshape, F32),
        scratch_shapes=[pltpu.VMEM(oshape, F32)],
        compiler_params=_params(("arbitrary",)),
    )(tok, st, *([] if after is None else [after]))


def _s5_discretize(lam_re, lam_im, log_dt, b_re, b_im):
    dt = jnp.exp(log_dt)[:, None]
    mag = jnp.exp(lam_re * dt)
    ab_r, ab_i = mag * jnp.cos(lam_im * dt), mag * jnp.sin(lam_im * dt)
    den = lam_re * lam_re + lam_im * lam_im
    nr = ab_r - 1.0
    co_r = (nr * lam_re + ab_i * lam_im) / den
    co_i = (ab_i * lam_re - nr * lam_im) / den
    bb_r = co_r[..., None] * b_re - co_i[..., None] * b_im
    bb_i = co_r[..., None] * b_im + co_i[..., None] * b_re
    return ab_r, ab_i, bb_r, bb_i


def _blockdiag(m):
    G, r, k = m.shape
    ncl = G // GROUPS_PER_CLUSTER
    m4 = m.reshape(ncl, GROUPS_PER_CLUSTER, r, k)
    eye = jnp.eye(GROUPS_PER_CLUSTER, dtype=m.dtype)
    return jnp.einsum('cgrk,gh->cgrhk', m4, eye).reshape(ncl, GROUPS_PER_CLUSTER * r, GROUPS_PER_CLUSTER * k)


def _unblockdiag(m, r, k):
    ncl = m.shape[0]
    m5 = m.reshape(ncl, GROUPS_PER_CLUSTER, r, GROUPS_PER_CLUSTER, k)
    eye = jnp.eye(GROUPS_PER_CLUSTER, dtype=m.dtype)
    return jnp.einsum('cgrhk,gh->cgrk', m5, eye).reshape(ncl * GROUPS_PER_CLUSTER, r, k)


def _t5_bucket(dist):
    exact = REL_BUCKETS // 2
    d = np.maximum(dist, 1).astype(np.float32)
    large = exact + (np.log(d / exact) / math.log(REL_MAX_DIST / exact) * (REL_BUCKETS - exact)).astype(np.int64)
    large = np.minimum(large, REL_BUCKETS - 1)
    return np.where(dist < exact, dist, large).astype(np.int32)


def _band_tables(dil):
    steps = np.arange(BAND)[:, None] + BAND - np.arange(2 * BAND)[None, :]
    bucket = _t5_bucket(np.maximum(steps, 0) * dil)
    in_band = (steps >= 0) & (steps <= BAND)
    return bucket, in_band


def _attn_bias(rel_bias, hpg):
    out = []
    for g, dil in enumerate(DILATIONS):
        bucket, in_band = _band_tables(dil)
        cols = rel_bias[:, g * hpg:(g + 1) * hpg].astype(F32)
        onehot = jnp.asarray((bucket.reshape(-1, 1) == np.arange(REL_BUCKETS)[None, :]).astype(np.float32))
        bias = jnp.dot(onehot, cols, precision=lax.Precision.HIGHEST).T.reshape(hpg, BAND, 2 * BAND)
        out.append(jnp.where(jnp.asarray(in_band)[None], bias, NEG_BIG))
    return jnp.concatenate(out, axis=0)


def _attn_blocks(dil, L):
    M = L // dil
    return M, M // BAND


def _row_sel(r, M, dil):
    return pl.ds(r, M) if dil == 1 else pl.ds(r, M, stride=dil)


def _attn_fwd(q, kv, bias, L, hpg, *, name):
    T = q.shape[0]
    nb_ = T // L
    HP = hpg // 2
    W3 = 3 * hpg * HEAD_DIM
    mmax = L

    def group_body(dil, q_ref, k_ref, v_ref, b_ref, o_ref, l_ref, os, ls):
        M, NB = _attn_blocks(dil, L)
        first = lax.broadcasted_iota(jnp.int32, (1, 2 * HEAD_DIM), 1) < HEAD_DIM
        loaded = {}

        def load(r):
            rows = _row_sel(r, M, dil)
            qf = q_ref[rows, :] * 0.125
            qm = [jnp.where(first, qf, 0.0).astype(MXU_DTYPE), jnp.where(first, 0.0, qf).astype(MXU_DTYPE)]
            kr = k_ref[rows, :].astype(MXU_DTYPE)
            va = jnp.concatenate([v_ref[rows, :].astype(MXU_DTYPE), jnp.ones((M, 2 * HEAD_DIM), MXU_DTYPE)], axis=-1)
            return qm, kr, va

        tasks = [(r, n) for r in range(dil) for n in range(NB)]
        nbatch = ATTN_BATCH[NB == 1]
        for t0 in range(0, len(tasks), nbatch):
            batch = tasks[t0:t0 + nbatch]
            for r, _ in batch:
                if r not in loaded:
                    loaded[r] = load(r)
            chains = [(r, n, hh) for r, n in batch for hh in range(2)]
            ks = lambda n: slice(0, BAND) if n == 0 else slice((n - 1) * BAND, (n + 1) * BAND)
            s = [lax.dot_general(loaded[r][0][hh][n * BAND:(n + 1) * BAND, :], loaded[r][1][ks(n), :],
                                 (((1,), (1,)), ((), ())), preferred_element_type=F32)
                 + (b_ref[hh, :, BAND:] if n == 0 else b_ref[hh]) for r, n, hh in chains]
            m = [jnp.max(t, axis=-1, keepdims=True) for t in s]
            p = [jnp.exp(t - mm) for t, mm in zip(s, m)]
            pv = [jnp.dot(t.astype(MXU_DTYPE), loaded[r][2][ks(n), :], preferred_element_type=F32)
                  for t, (r, n, hh) in zip(p, chains)]
            l = [t[:, 2 * HEAD_DIM:] for t in pv]
            o_h = [t[:, :2 * HEAD_DIM] / ll for t, ll in zip(pv, l)]
            l_h = [mm + jnp.log(ll) for mm, ll in zip(m, l)]
            for i, (r, n) in enumerate(batch):
                os[r * M + n * BAND:r * M + (n + 1) * BAND, :] = jnp.where(first, o_h[2 * i], o_h[2 * i + 1])
                ls[r * M + n * BAND:r * M + (n + 1) * BAND, :] = jnp.where(first, l_h[2 * i], l_h[2 * i + 1])
                if n == NB - 1:
                    rows = _row_sel(r, M, dil)
                    o_ref[rows, :] = os[r * M:(r + 1) * M, :]
                    l_ref[rows, :] = ls[r * M:(r + 1) * M, :]

    def body(q_ref, k_ref, v_ref, b_ref, o_ref, l_ref, os, ls):
        g = pl.program_id(0)
        for gi, dil in enumerate(DILATIONS):
            pl.when(g == gi)(functools.partial(group_body, dil, q_ref, k_ref, v_ref, b_ref, o_ref, l_ref, os, ls))

    blk = (L, 2 * HEAD_DIM)
    return _pallas(
        body, name=name, grid=(3, nb_, HP),
        in_specs=[pl.BlockSpec(blk, lambda g, b, h: (b, g * HP + h)),
                  pl.BlockSpec(blk, lambda g, b, h: (b, g * HP + h)),
                  pl.BlockSpec(blk, lambda g, b, h: (b, 3 * HP + g * HP + h)),
                  pl.BlockSpec((2, BAND, 2 * BAND), lambda g, b, h: (g * HP + h, 0, 0))],
        out_specs=[pl.BlockSpec(blk, lambda g, b, h: (b, g * HP + h)),
                   pl.BlockSpec(blk, lambda g, b, h: (b, g * HP + h))],
        out_shape=[jax.ShapeDtypeStruct((T, W3), F32), jax.ShapeDtypeStruct((T, W3), F32)],
        scratch_shapes=[pltpu.VMEM((mmax, 2 * HEAD_DIM), F32), pltpu.VMEM((mmax, 2 * HEAD_DIM), F32)],
        compiler_params=_params(("arbitrary", "arbitrary", "arbitrary")),
    )(q, kv, kv, bias)


def _attn_merge(o3, l3, hw, *, name):
    T = o3.shape[0]
    tm = _pick(T, (256, 128))

    def body(o0, o1, o2, l0, l1, l2, o_ref, ob_ref, lse_ref):
        a0, a1, a2 = l0[...], l1[...], l2[...]
        m = jnp.maximum(jnp.maximum(a0, a1), a2)
        e0, e1, e2 = jnp.exp(a0 - m), jnp.exp(a1 - m), jnp.exp(a2 - m)
        z = e0 + e1 + e2
        o = (e0 * o0[...] + e1 * o1[...] + e2 * o2[...]) / z
        o_ref[...] = o
        ob_ref[...] = o.astype(ob_ref.dtype)
        lse_ref[...] = m + jnp.log(z)

    def col(g):
        return pl.BlockSpec((tm, hw), lambda i: (i, g))

    return _pallas(
        body, name=name, grid=(T // tm,),
        in_specs=[col(0), col(1), col(2), col(0), col(1), col(2)],
        out_specs=[_rows(tm, hw)] * 3,
        out_shape=[jax.ShapeDtypeStruct((T, hw), F32), jax.ShapeDtypeStruct((T, hw), MXU_DTYPE),
                   jax.ShapeDtypeStruct((T, hw), F32)],
        compiler_params=_params(("parallel",)),
    )(o3, o3, o3, l3, l3, l3)


def _attn_bwd(q, kv, do, o, lse, bias, L, hpg, *, name):
    T = q.shape[0]
    nb_ = T // L
    HP = hpg // 2
    W3 = 3 * hpg * HEAD_DIM
    mmax = L

    def group_body(dil, q_ref, k_ref, v_ref, do_ref, o_ref, l_ref, b_ref, dq_ref, dk_ref, dv_ref, ds_ref,
                   dqs, dks, dvs):
        M, NB = _attn_blocks(dil, L)
        first = lax.broadcasted_iota(jnp.int32, (1, 2 * HEAD_DIM), 1) < HEAD_DIM
        loaded = {}

        def load(r):
            rows = _row_sel(r, M, dil)
            qf = q_ref[rows, :] * 0.125
            qm = [jnp.where(first, qf, 0.0).astype(MXU_DTYPE), jnp.where(first, 0.0, qf).astype(MXU_DTYPE)]
            kr = k_ref[rows, :].astype(MXU_DTYPE)
            vr = v_ref[rows, :].astype(MXU_DTYPE)
            dof = do_ref[rows, :]
            dom = [jnp.where(first, dof, 0.0).astype(MXU_DTYPE), jnp.where(first, 0.0, dof).astype(MXU_DTYPE)]
            dod = dof * o_ref[rows, :]
            delta = [jnp.sum(jnp.where(first, dod, 0.0), axis=-1, keepdims=True),
                     jnp.sum(jnp.where(first, 0.0, dod), axis=-1, keepdims=True)]
            lr = l_ref[rows, :]
            lse = [lr[:, 0:1], lr[:, HEAD_DIM:HEAD_DIM + 1]]
            if NB > 1:
                dks[r * M:(r + 1) * M, :] = jnp.zeros((M, 2 * HEAD_DIM), F32)
                dvs[r * M:(r + 1) * M, :] = jnp.zeros((M, 2 * HEAD_DIM), F32)
            return qm, kr, vr, dom, delta, lse

        nt = (((1,), (1,)), ((), ()))
        tn = (((0,), (0,)), ((), ()))
        tasks = [(r, n) for r in range(dil) for n in range(NB)]
        nbatch = ATTN_BATCH[NB == 1]
        for t0 in range(0, len(tasks), nbatch):
            batch = tasks[t0:t0 + nbatch]
            for r, _ in batch:
                if r not in loaded:
                    loaded[r] = load(r)
            chains = [(r, n, hh) for r, n in batch for hh in range(2)]
            qs = lambda n: slice(n * BAND, (n + 1) * BAND)
            ks = lambda n: slice(0, BAND) if n == 0 else slice((n - 1) * BAND, (n + 1) * BAND)
            qb = [loaded[r][0][hh][qs(n), :] for r, n, hh in chains]
            dob = [loaded[r][3][hh][qs(n), :] for r, n, hh in chains]
            kb = [loaded[r][1][ks(n), :] for r, n, hh in chains]
            s = [lax.dot_general(a, b, nt, preferred_element_type=F32) + (b_ref[hh, :, BAND:] if n == 0 else b_ref[hh])
                 for a, b, (r, n, hh) in zip(qb, kb, chains)]
            dp = [lax.dot_general(a, loaded[r][2][ks(n), :], nt, preferred_element_type=F32)
                  for a, (r, n, hh) in zip(dob, chains)]
            p = [jnp.exp(t - loaded[r][5][hh][qs(n), :]) for t, (r, n, hh) in zip(s, chains)]
            ds = [a * (b - loaded[r][4][hh][qs(n), :]) for a, b, (r, n, hh) in zip(p, dp, chains)]
            for t, (r, n, hh) in zip(ds, chains):
                if n == 0:
                    ds_ref[hh, :, BAND:] += t
                else:
                    ds_ref[hh] += t
            dsm = [t.astype(MXU_DTYPE) for t in ds]
            dq = [jnp.dot(a, b, preferred_element_type=F32) for a, b in zip(dsm, kb)]
            dk = [lax.dot_general(a, b, tn, preferred_element_type=F32) for a, b in zip(dsm, qb)]
            dv = [lax.dot_general(a.astype(MXU_DTYPE), b, tn, preferred_element_type=F32) for a, b in zip(p, dob)]
            for i, (r, n) in enumerate(batch):
                dqs[r * M + n * BAND:r * M + (n + 1) * BAND, :] = jnp.where(first, dq[2 * i], dq[2 * i + 1]) * 0.125
                ksm = slice(r * M + ks(n).start, r * M + ks(n).stop)
                if NB > 1:
                    dks[ksm, :] += dk[2 * i] + dk[2 * i + 1]
                    dvs[ksm, :] += dv[2 * i] + dv[2 * i + 1]
                else:
                    dks[ksm, :] = dk[2 * i] + dk[2 * i + 1]
                    dvs[ksm, :] = dv[2 * i] + dv[2 * i + 1]
                if n == NB - 1:
                    rows = _row_sel(r, M, dil)
                    dq_ref[rows, :] = dqs[r * M:(r + 1) * M, :]
                    dk_ref[rows, :] = dks[r * M:(r + 1) * M, :]
                    dv_ref[rows, :] = dvs[r * M:(r + 1) * M, :]

    def body(q_ref, k_ref, v_ref, do_ref, o_ref, l_ref, b_ref, dq_ref, dk_ref, dv_ref, ds_ref, dqs, dks, dvs):
        g = pl.program_id(0)

        @pl.when(pl.program_id(2) == 0)
        def _():
            ds_ref[...] = jnp.zeros_like(ds_ref)

        for gi, dil in enumerate(DILATIONS):
            pl.when(g == gi)(functools.partial(group_body, dil, q_ref, k_ref, v_ref, do_ref, o_ref, l_ref, b_ref,
                                               dq_ref, dk_ref, dv_ref, ds_ref, dqs, dks, dvs))

    blk = (L, 2 * HEAD_DIM)
    gcol = lambda g, h, b: (b, g * HP + h)
    hcol = lambda g, h, b: (b, h)
    return _pallas(
        body, name=name, grid=(3, HP, nb_),
        in_specs=[pl.BlockSpec(blk, gcol), pl.BlockSpec(blk, gcol),
                  pl.BlockSpec(blk, lambda g, h, b: (b, 3 * HP + g * HP + h)),
                  pl.BlockSpec(blk, hcol), pl.BlockSpec(blk, hcol), pl.BlockSpec(blk, hcol),
                  pl.BlockSpec((2, BAND, 2 * BAND), lambda g, h, b: (g * HP + h, 0, 0))],
        out_specs=[pl.BlockSpec(blk, gcol), pl.BlockSpec(blk, gcol), pl.BlockSpec(blk, gcol),
                   pl.BlockSpec((2, BAND, 2 * BAND), lambda g, h, b: (g * HP + h, 0, 0))],
        out_shape=[jax.ShapeDtypeStruct((T, W3), F32), jax.ShapeDtypeStruct((T, W3), F32),
                   jax.ShapeDtypeStruct((T, W3), F32), jax.ShapeDtypeStruct((3 * hpg, BAND, 2 * BAND), F32)],
        scratch_shapes=[pltpu.VMEM((mmax, 2 * HEAD_DIM), F32)] * 3,
        compiler_params=_params(("arbitrary", "arbitrary", "arbitrary")),
    )(q, kv, kv, do, o, lse, bias)


def _bias_grad(ds_sum, hpg, *, name):
    nh = ds_sum.shape[0]
    idx = np.stack([np.where(_band_tables(dil)[1], _band_tables(dil)[0], -1) for dil in DILATIONS]).astype(np.int32)

    def body(ds_ref, idx_ref, o_ref):
        d = ds_ref[...]
        ix = idx_ref[...]
        lane = lax.broadcasted_iota(jnp.int32, (8, 128), 1)
        row = jnp.zeros((8, 128), F32)
        for b in range(REL_BUCKETS):
            row = row + jnp.where(lane == b, jnp.sum(jnp.where(ix == b, d, 0.0)), 0.0)
        o_ref[...] = row

    out = _pallas(
        body, name=name, grid=(nh,),
        in_specs=[pl.BlockSpec((None, BAND, 2 * BAND), lambda h: (h, 0, 0)),
                  pl.BlockSpec((None, BAND, 2 * BAND), lambda h: (h // hpg, 0, 0))],
        out_specs=pl.BlockSpec((None, 8, 128), lambda h: (h, 0, 0)),
        out_shape=jax.ShapeDtypeStruct((nh, 8, 128), F32),
        compiler_params=_params(("parallel",)),
    )(ds_sum, jnp.asarray(idx))
    return out[:, 0, :REL_BUCKETS].T


def _adamw(w, g, m, v, *, name):
    Rw, C = w.shape
    tm = _pick(Rw, (512, 352, 256, 128, 64, 32, 16, 8))

    def body(w_ref, g_ref, m_ref, v_ref, d_ref, nm_ref, nv_ref):
        gg = g_ref[...]
        nm = ADAM_B1 * m_ref[...] + (1.0 - ADAM_B1) * gg
        nv = ADAM_B2 * v_ref[...] + (1.0 - ADAM_B2) * (gg * gg)
        m_hat = nm / (1.0 - ADAM_B1 ** ADAM_STEP)
        v_hat = nv / (1.0 - ADAM_B2 ** ADAM_STEP)
        d_ref[...] = -ADAM_LR * (m_hat / (jnp.sqrt(v_hat) + ADAM_EPS) + ADAM_WD * w_ref[...])
        nm_ref[...] = nm
        nv_ref[...] = nv

    return _pallas(
        body, name=name, grid=(Rw // tm,), in_specs=[_rows(tm, C)] * 4, out_specs=[_rows(tm, C)] * 3,
        out_shape=[jax.ShapeDtypeStruct((Rw, C), F32)] * 3, compiler_params=_params(("parallel",)),
    )(w, g, m, v)


ROW_TILE_ELEMS = 256 * 1024


def _tile_rows(r, c):
    best = 8
    for t in range(8, r + 1, 8):
        if r % t == 0 and t * c <= ROW_TILE_ELEMS:
            best = t
    return best


def _adamw_halves(w, m, v, mine, other, cidx, *, layer=0, prev=None, name):
    NL, _, r, c = w.shape
    tm = _tile_rows(r, c)

    def body(c_ref, w_ref, m_ref, v_ref, a_ref, b_ref, *rest):
        g_ref, d_ref, nm_ref, nv_ref = rest[-4:]
        gg = jnp.where(pl.program_id(0) == c_ref[0], a_ref[...], b_ref[...])
        nm = ADAM_B1 * m_ref[...] + (1.0 - ADAM_B1) * gg
        nv = ADAM_B2 * v_ref[...] + (1.0 - ADAM_B2) * (gg * gg)
        m_hat = nm / (1.0 - ADAM_B1 ** ADAM_STEP)
        v_hat = nv / (1.0 - ADAM_B2 ** ADAM_STEP)
        g_ref[...] = gg
        d_ref[...] = -ADAM_LR * (m_hat / (jnp.sqrt(v_hat) + ADAM_EPS) + ADAM_WD * w_ref[...])
        nm_ref[...] = nm
        nv_ref[...] = nv

    half = pl.BlockSpec((None, None, tm, c), lambda h, i, cr: (layer, h, i, 0))
    one = pl.BlockSpec((None, tm, c), lambda h, i, cr: (0, i, 0))
    in_specs = [half, half, half, one, one]
    args = [cidx, w, m, v, mine, other]
    aliases = {}
    if prev is not None:
        in_specs += [_ANY] * 4
        args += list(prev)
        aliases = {6 + k: k for k in range(4)}
    spec = pltpu.PrefetchScalarGridSpec(num_scalar_prefetch=1, grid=(2, r // tm), in_specs=in_specs, out_specs=[half] * 4)
    return _pallas(
        body, name=name, grid_spec=spec, out_shape=[jax.ShapeDtypeStruct((NL, 2, r, c), F32)] * 4,
        input_output_aliases=aliases, compiler_params=_params(("parallel", "parallel")),
    )(*args)


def _pair_sum(g, theirs, cidx, *, cast, name):
    _, _, r, c = g.shape
    tm = _tile_rows(r, c)

    def body(c_ref, g_ref, t_ref, *outs):
        s = g_ref[...] + t_ref[...]
        outs[0][...] = s
        if cast:
            outs[1][...] = s.astype(BF16)

    blk = (None, None, tm, c)
    first = pl.BlockSpec(blk, lambda p, i, cr: (p, 0, i, 0))
    shapes = [jax.ShapeDtypeStruct((4, 1, r, c), F32)] + ([jax.ShapeDtypeStruct((4, 1, r, c), BF16)] if cast else [])
    spec = pltpu.PrefetchScalarGridSpec(
        num_scalar_prefetch=1, grid=(4, r // tm),
        in_specs=[pl.BlockSpec(blk, lambda p, i, cr: (p, cr[0], i, 0)), first], out_specs=[first] * len(shapes))
    return _pallas(body, name=name, grid_spec=spec, out_shape=shapes,
                   compiler_params=_params(("parallel", "parallel")))(cidx, g, theirs)


def _chip_sum(hf, got, chip_idx, *, name):
    _, _, r, c = hf.shape
    tm = _tile_rows(r, c)

    def body(p_ref, h_ref, r_ref, o_ref):
        s = h_ref[...]
        for k in range(3):
            s = s + r_ref[k].astype(F32)
        o_ref[...] = s

    spec = pltpu.PrefetchScalarGridSpec(
        num_scalar_prefetch=1, grid=(r // tm,),
        in_specs=[pl.BlockSpec((None, None, tm, c), lambda i, pr: (pr[0], 0, i, 0)),
                  pl.BlockSpec((3, None, tm, c), lambda i, pr: (0, 0, i, 0))],
        out_specs=pl.BlockSpec((None, tm, c), lambda i, pr: (0, i, 0)))
    return _pallas(body, name=name, grid_spec=spec, out_shape=jax.ShapeDtypeStruct((1, r, c), F32),
                   compiler_params=_params(("parallel",)))(chip_idx, hf, got)


def _place():
    x, y, c = lax.axis_index("x"), lax.axis_index("y"), lax.axis_index("c")
    chips = [(1 - x, y), (x, 1 - y), (1 - x, 1 - y)]
    return x, y, c, chips


_ANY = pl.BlockSpec(memory_space=pl.ANY)


def _comm_call(body, ins, out_shapes, n_remote, *, name, aliases=None):
    sems = [pltpu.SemaphoreType.DMA((n,)) for n in n_remote]
    return _pallas(
        body, name=name, in_specs=[_ANY] * len(ins), out_specs=[_ANY] * len(out_shapes), out_shape=out_shapes,
        scratch_shapes=sems, input_output_aliases=aliases or {},
        compiler_params=pltpu.CompilerParams(has_side_effects=True),
    )(*ins)


_HBM_SPEC = pl.BlockSpec(memory_space=pltpu.HBM)
_SEM_SPEC = pl.BlockSpec(memory_space=pltpu.SEMAPHORE)
_DATAFLOW = pltpu.SideEffectType.DATAFLOW_SIDE_EFFECTING


def _in_hbm(a):
    return pltpu.with_memory_space_constraint(a, pltpu.HBM)


def _gather_start(groups, *, name):
    flat = [s for g in groups for s in g]
    n, ng = len(flat), len(groups)

    def body(*refs):
        ins, lands = refs[:n], refs[n:2 * n]
        sems = refs[2 * n:2 * n + 2 * ng]
        token = refs[-1]
        x, y, c, chips = _place()
        me = 2 * x + y
        a = 0
        for gi, g in enumerate(groups):
            for j in range(len(g)):
                for k, (tx, ty) in enumerate(chips):
                    _rcopy(ins[a].at[c], lands[a].at[me, c], sems[2 * gi].at[3 * j + k], sems[2 * gi + 1].at[3 * j + k],
                           (tx, ty, c)).start()
                a += 1
        token[...] = jnp.zeros_like(token)

    land_shapes = [(4,) + s.shape for s in flat]
    out_shape = ([pltpu.SemaphoreType.DMA((3 * len(g),)) for g in groups for _ in range(2)]
                 + [pltpu.HBM(s.shape, s.dtype) for s in flat]
                 + [pltpu.HBM(ls, s.dtype) for ls, s in zip(land_shapes, flat)]
                 + [jax.ShapeDtypeStruct((8, 128), F32)])
    outs = _pallas(
        body, name=name, in_specs=[_HBM_SPEC] * (2 * n),
        out_specs=[_SEM_SPEC] * (2 * ng) + [_HBM_SPEC] * (2 * n) + [pl.BlockSpec(memory_space=pltpu.VMEM)],
        out_shape=out_shape, input_output_aliases={i: 2 * ng + i for i in range(2 * n)},
        compiler_params=pltpu.CompilerParams(has_side_effects=_DATAFLOW),
    )(*[_in_hbm(s) for s in flat], *[_in_hbm(lax.empty(ls, s.dtype)) for ls, s in zip(land_shapes, flat)])
    sems, thru, lands, token = outs[:2 * ng], outs[2 * ng:2 * ng + n], outs[2 * ng + n:2 * ng + 2 * n], outs[-1]
    res, a = [], 0
    for gi, g in enumerate(groups):
        res.append((sems[2 * gi], sems[2 * gi + 1], thru[a:a + len(g)], lands[a:a + len(g)]))
        a += len(g)
    return res, token


def _gather_wait(ssem, rsem, shards, lands, after, *, name):
    m = len(shards)

    def body(*refs):
        ins, lnd = refs[:m], refs[m:2 * m]
        ss, rs = refs[2 * m], refs[2 * m + 1]
        x, y, c, chips = _place()
        for j in range(m):
            for k, (tx, ty) in enumerate(chips):
                cp = _rcopy(ins[j].at[c], lnd[j].at[2 * tx + ty, c], ss.at[3 * j + k], rs.at[3 * j + k], (tx, ty, c))
                cp.wait_send()
                cp.wait_recv()

    outs = _pallas(
        body, name=name, in_specs=[_HBM_SPEC] * (2 * m) + [_SEM_SPEC, _SEM_SPEC, _ANY],
        out_specs=[_HBM_SPEC] * (2 * m),
        out_shape=[pltpu.HBM(s.shape, s.dtype) for s in shards] + [pltpu.HBM(l.shape, l.dtype) for l in lands],
        input_output_aliases={i: i for i in range(2 * m)},
        compiler_params=pltpu.CompilerParams(has_side_effects=_DATAFLOW),
    )(*shards, *lands, ssem, rsem, after)
    return outs[m:]


def _gather_forward(lands, *, name):
    n = len(lands)

    def body(*refs):
        outs = refs[n:2 * n]
        ssem, rsem = refs[2 * n:]
        x, y, c, chips = _place()
        sib = (x, y, 1 - c)
        cps = []
        for a in range(n):
            for k, (tx, ty) in enumerate(chips):
                pk = 2 * tx + ty
                cp = _rcopy(outs[a].at[pk, c], outs[a].at[pk, c], ssem.at[3 * a + k], rsem.at[3 * a + k], sib)
                cp.start()
                cps.append(cp)
        for a in range(n):
            for k, (tx, ty) in enumerate(chips):
                pk = 2 * tx + ty
                _rcopy(outs[a].at[pk, c], outs[a].at[pk, 1 - c], ssem.at[3 * a + k], rsem.at[3 * a + k], sib).wait_recv()
        for cp in cps:
            cp.wait_send()

    shapes = [jax.ShapeDtypeStruct(l.shape, l.dtype) for l in lands]
    return _comm_call(body, lands, shapes, [3 * n, 3 * n], name=name, aliases={i: i for i in range(n)})


def _gather_forward_start(lands, *, name):
    n = len(lands)

    def body(*refs):
        ssem, rsem = refs[n], refs[n + 1]
        outs = refs[n + 2:2 * n + 2]
        token = refs[-1]
        x, y, c, chips = _place()
        for a in range(n):
            for k, (tx, ty) in enumerate(chips):
                pk = 2 * tx + ty
                _rcopy(outs[a].at[pk, c], outs[a].at[pk, c], ssem.at[3 * a + k], rsem.at[3 * a + k], (x, y, 1 - c)).start()
        token[...] = jnp.zeros_like(token)

    outs = _pallas(
        body, name=name, in_specs=[_HBM_SPEC] * n,
        out_specs=[_SEM_SPEC] * 2 + [_HBM_SPEC] * n + [pl.BlockSpec(memory_space=pltpu.VMEM)],
        out_shape=([pltpu.SemaphoreType.DMA((3 * n,))] * 2 + [pltpu.HBM(l.shape, l.dtype) for l in lands]
                   + [jax.ShapeDtypeStruct((8, 128), F32)]),
        input_output_aliases={i: 2 + i for i in range(n)},
        compiler_params=pltpu.CompilerParams(has_side_effects=_DATAFLOW),
    )(*lands)
    return (outs[0], outs[1], outs[2:2 + n]), outs[-1]


def _gather_forward_wait(started, after, *, name):
    ssem, rsem, lands = started
    n = len(lands)

    def body(*refs):
        lnd = refs[:n]
        ss, rs = refs[n], refs[n + 1]
        x, y, c, chips = _place()
        sib = (x, y, 1 - c)
        for a in range(n):
            for k, (tx, ty) in enumerate(chips):
                pk = 2 * tx + ty
                _rcopy(lnd[a].at[pk, c], lnd[a].at[pk, 1 - c], ss.at[3 * a + k], rs.at[3 * a + k], sib).wait_recv()
                _rcopy(lnd[a].at[pk, c], lnd[a].at[pk, c], ss.at[3 * a + k], rs.at[3 * a + k], sib).wait_send()

    return _pallas(
        body, name=name, in_specs=[_HBM_SPEC] * n + [_SEM_SPEC, _SEM_SPEC, _ANY], out_specs=[_HBM_SPEC] * n,
        out_shape=[pltpu.HBM(l.shape, l.dtype) for l in lands], input_output_aliases={i: i for i in range(n)},
        compiler_params=pltpu.CompilerParams(has_side_effects=_DATAFLOW),
    )(*lands, ssem, rsem, after)


class _Lazy:
    def __init__(self, group_of, make, prepare):
        self._group_of, self._make, self._prepare, self._done, self._anchor = group_of, make, prepare, {}, None

    def anchor(self, value):
        self._anchor = value

    def prepare(self, key, value):
        return self._prepare(self._group_of[key], value)

    def __getitem__(self, key):
        g = self._group_of[key]
        if g not in self._done:
            self._done[g] = self._make(g, self._anchor)
        return self._done[g][key]


def _anchor(mapping, value):
    if isinstance(mapping, _Lazy):
        mapping.anchor(value)


def _prepare(mapping, key, value):
    return mapping.prepare(key, value)[0, 0] if isinstance(mapping, _Lazy) else 0.0


def _rcopy(src, dst, ssem, rsem, dev):
    return pltpu.make_async_remote_copy(src_ref=src, dst_ref=dst, send_sem=ssem, recv_sem=rsem,
                                        device_id=dev, device_id_type=MESH)


def _all_gather(shards, *, name):
    n = len(shards)

    def body(*refs):
        ins, outs = refs[:n], refs[n:2 * n]
        s_ici, r_ici, s_d2d, r_d2d = refs[2 * n:]
        x, y, c, chips = _place()
        me = 2 * x + y
        sib = (x, y, 1 - c)
        sends = []
        for a in range(n):
            for k, (tx, ty) in enumerate(chips):
                cp = _rcopy(ins[a].at[c], outs[a].at[me, c], s_ici.at[3 * a + k], r_ici.at[3 * a + k], (tx, ty, c))
                cp.start()
                sends.append(cp)
        for a in range(n):
            for k, (tx, ty) in enumerate(chips):
                pk = 2 * tx + ty
                _rcopy(ins[a].at[c], outs[a].at[pk, c], s_ici.at[3 * a + k], r_ici.at[3 * a + k], (tx, ty, c)).wait_recv()
                fw = _rcopy(outs[a].at[pk, c], outs[a].at[pk, c], s_d2d.at[3 * a + k], r_d2d.at[3 * a + k], sib)
                fw.start()
                sends.append(fw)
        for a in range(n):
            for k, (tx, ty) in enumerate(chips):
                pk = 2 * tx + ty
                _rcopy(ins[a].at[c], outs[a].at[pk, 1 - c], s_d2d.at[3 * a + k], r_d2d.at[3 * a + k], sib).wait_recv()
        for cp in sends:
            cp.wait_send()

    shapes = [jax.ShapeDtypeStruct((4,) + s.shape, s.dtype) for s in shards]
    return _comm_call(body, shards, shapes, [3 * n] * 4, name=name)


def _gather(shards, chip, *, name):
    outs = _all_gather(shards, name=name)
    return [lax.dynamic_update_slice(o, s[None], (chip, 0, 0, 0)) for o, s in zip(outs, shards)]


def _pair_send(gs, *, name):
    n = len(gs)

    def body(*refs):
        ins, theirs = refs[:n], refs[n:2 * n]
        ssem, rsem = refs[2 * n:]
        x, y, c, _ = _place()
        sib = (x, y, 1 - c)
        cps = []
        for a in range(n):
            cp = _rcopy(ins[a].at[:, pl.ds(1 - c, 1)], theirs[a], ssem.at[a], rsem.at[a], sib)
            cp.start()
            cps.append(cp)
        for cp in cps:
            cp.wait_send()
            cp.wait_recv()

    shapes = [jax.ShapeDtypeStruct((4, 1) + g.shape[2:], g.dtype) for g in gs]
    return _comm_call(body, gs, shapes, [n, n], name=name)


def _chip_exchange(hx, *, name):
    n = len(hx)

    def body(*refs):
        hxr, got = refs[:n], refs[n:2 * n]
        ssem, rsem = refs[2 * n:]
        x, y, c, chips = _place()
        cps = []
        for a in range(n):
            for k, (tx, ty) in enumerate(chips):
                cp = _rcopy(hxr[a].at[2 * tx + ty], got[a].at[k], ssem.at[3 * a + k], rsem.at[3 * a + k], (tx, ty, c))
                cp.start()
                cps.append(cp)
        for cp in cps:
            cp.wait_send()
            cp.wait_recv()

    shapes = [jax.ShapeDtypeStruct((3,) + h.shape[1:], h.dtype) for h in hx]
    return _comm_call(body, hx, shapes, [3 * n, 3 * n], name=name)


def _pair_swap(fs, *, name):
    n = len(fs)

    def body(*refs):
        ins, outs = refs[:n], refs[n:2 * n]
        ssem, rsem = refs[2 * n:]
        x, y, c, _ = _place()
        cps = []
        for a in range(n):
            cp = _rcopy(ins[a], outs[a], ssem.at[a], rsem.at[a], (x, y, 1 - c))
            cp.start()
            cps.append(cp)
        for cp in cps:
            cp.wait_send()
            cp.wait_recv()

    shapes = [jax.ShapeDtypeStruct(f.shape, f.dtype) for f in fs]
    return _comm_call(body, fs, shapes, [n, n], name=name)


def _chip_exchange_start(hx, *, name):
    n = len(hx)

    def body(*refs):
        ins, gots = refs[:n], refs[n:2 * n]
        ssem, rsem = refs[2 * n], refs[2 * n + 1]
        token = refs[-1]
        x, y, c, chips = _place()
        for a in range(n):
            for k, (tx, ty) in enumerate(chips):
                _rcopy(ins[a].at[2 * tx + ty], gots[a].at[k], ssem.at[3 * a + k], rsem.at[3 * a + k], (tx, ty, c)).start()
        token[...] = jnp.zeros_like(token)

    got_shapes = [(3,) + h.shape[1:] for h in hx]
    outs = _pallas(
        body, name=name, in_specs=[_HBM_SPEC] * (2 * n),
        out_specs=[_SEM_SPEC] * 2 + [_HBM_SPEC] * (2 * n) + [pl.BlockSpec(memory_space=pltpu.VMEM)],
        out_shape=([pltpu.SemaphoreType.DMA((3 * n,))] * 2 + [pltpu.HBM(h.shape, h.dtype) for h in hx]
                   + [pltpu.HBM(gs, h.dtype) for gs, h in zip(got_shapes, hx)] + [jax.ShapeDtypeStruct((8, 128), F32)]),
        input_output_aliases={i: 2 + i for i in range(2 * n)},
        compiler_params=pltpu.CompilerParams(has_side_effects=_DATAFLOW),
    )(*[_in_hbm(h) for h in hx], *[_in_hbm(lax.empty(gs, h.dtype)) for gs, h in zip(got_shapes, hx)])
    return (outs[0], outs[1], outs[2:2 + n], outs[2 + n:2 + 2 * n]), outs[-1]


def _chip_exchange_wait(started, after, *, name):
    ssem, rsem, hx, gots = started
    n = len(hx)

    def body(*refs):
        ins, gts = refs[:n], refs[n:2 * n]
        ss, rs = refs[2 * n], refs[2 * n + 1]
        x, y, c, chips = _place()
        for a in range(n):
            for k, (tx, ty) in enumerate(chips):
                cp = _rcopy(ins[a].at[2 * tx + ty], gts[a].at[k], ss.at[3 * a + k], rs.at[3 * a + k], (tx, ty, c))
                cp.wait_send()
                cp.wait_recv()

    outs = _pallas(
        body, name=name, in_specs=[_HBM_SPEC] * (2 * n) + [_SEM_SPEC, _SEM_SPEC, _ANY],
        out_specs=[_HBM_SPEC] * (2 * n),
        out_shape=[pltpu.HBM(h.shape, h.dtype) for h in hx] + [pltpu.HBM(g.shape, g.dtype) for g in gots],
        input_output_aliases={i: i for i in range(2 * n)},
        compiler_params=pltpu.CompilerParams(has_side_effects=_DATAFLOW),
    )(*hx, *gots, ssem, rsem, after)
    return outs[n:]


def _sent_part(ref, c, whole):
    return ref if whole else ref.at[:, pl.ds(1 - c, 1)]


def _pair_send_start(gs, *, name, whole=False, after=None):
    n = len(gs)
    afters = [] if after is None else [after]

    def body(*refs):
        ins, lands = refs[:n], refs[n:2 * n]
        ssem, rsem = refs[2 * n + len(afters)], refs[2 * n + len(afters) + 1]
        token = refs[-1]
        x, y, c, _ = _place()
        for a in range(n):
            _rcopy(_sent_part(ins[a], c, whole), lands[a], ssem.at[a], rsem.at[a], (x, y, 1 - c)).start()
        token[...] = jnp.zeros_like(token)

    land_shapes = [g.shape if whole else (4, 1) + g.shape[2:] for g in gs]
    outs = _pallas(
        body, name=name, in_specs=[_HBM_SPEC] * (2 * n) + [_ANY] * len(afters),
        out_specs=[_SEM_SPEC] * 2 + [_HBM_SPEC] * (2 * n) + [pl.BlockSpec(memory_space=pltpu.VMEM)],
        out_shape=([pltpu.SemaphoreType.DMA((n,))] * 2 + [pltpu.HBM(g.shape, g.dtype) for g in gs]
                   + [pltpu.HBM(ls, g.dtype) for ls, g in zip(land_shapes, gs)] + [jax.ShapeDtypeStruct((8, 128), F32)]),
        input_output_aliases={i: 2 + i for i in range(2 * n)},
        compiler_params=pltpu.CompilerParams(has_side_effects=_DATAFLOW),
    )(*[_in_hbm(g) for g in gs], *[_in_hbm(lax.empty(ls, g.dtype)) for ls, g in zip(land_shapes, gs)], *afters)
    return (outs[0], outs[1], outs[2:2 + n], outs[2 + n:2 + 2 * n]), outs[-1]


def _pair_send_wait(started, after, *, name, whole=False):
    ssem, rsem, gs, lands = started
    n = len(gs)
    afters = list(after) if isinstance(after, (list, tuple)) else [after]

    def body(*refs):
        ins, lnd = refs[:n], refs[n:2 * n]
        ss, rs = refs[2 * n], refs[2 * n + 1]
        x, y, c, _ = _place()
        for a in range(n):
            cp = _rcopy(_sent_part(ins[a], c, whole), lnd[a], ss.at[a], rs.at[a], (x, y, 1 - c))
            cp.wait_send()
            cp.wait_recv()

    outs = _pallas(
        body, name=name, in_specs=[_HBM_SPEC] * (2 * n) + [_SEM_SPEC, _SEM_SPEC] + [_ANY] * len(afters),
        out_specs=[_HBM_SPEC] * (2 * n),
        out_shape=[pltpu.HBM(g.shape, g.dtype) for g in gs] + [pltpu.HBM(l.shape, l.dtype) for l in lands],
        input_output_aliases={i: i for i in range(2 * n)},
        compiler_params=pltpu.CompilerParams(has_side_effects=_DATAFLOW),
    )(*gs, *lands, ssem, rsem, *afters)
    return list(outs[:n]), list(outs[n:])


def _pair_sums(grads, exch_bf16, cidx, tag, theirs=None):
    if theirs is None:
        theirs = _pair_send(grads, name=f"rs_pair_send_{tag}")
    hf, hx = [], []
    for a in range(len(grads)):
        res = _pair_sum(grads[a], theirs[a], cidx, cast=exch_bf16[a], name=f"rs_pair_sum_{tag}{a}")
        hf.append(res[0])
        hx.append(res[1] if exch_bf16[a] else res[0])
    return hf, hx


def _chip_sums(hf, got, chip_idx, tag):
    return [_chip_sum(hf[a], got[a], chip_idx, name=f"rs_chip_sum_{tag}{a}") for a in range(len(hf))]


def _interleave(a, B, L):
    return a.reshape(B, L, -1).transpose(1, 0, 2).reshape(B * L, -1)


def _deinterleave(a, B, L):
    return a.reshape(L, B, -1).transpose(1, 0, 2).reshape(B * L, -1)


def _local_step(x, tgt, W, S, on_grads=None):
    B, L, D = x.shape
    T = B * L
    G = D // SSM_GROUP
    Pst = SSM_STATE
    hpg = D // HEAD_DIM
    HW = hpg * HEAD_DIM
    ncl = G // GROUPS_PER_CLUSTER
    x2 = x.reshape(T, D)
    tgt2 = tgt.reshape(T, D)

    disc = lambda *p: _s5_discretize(*p)
    (ab_r, ab_i, bb_r, bb_i), disc_vjp = jax.vjp(disc, S["lam_re"], S["lam_im"], S["log_dt"], S["b_re"], S["b_im"])
    wb = jnp.concatenate([_blockdiag(jnp.transpose(bb_r, (0, 2, 1))), _blockdiag(jnp.transpose(bb_i, (0, 2, 1)))],
                         axis=-1).astype(MXU_DTYPE)
    wc = jnp.concatenate([_blockdiag(jnp.transpose(S["c_re"], (0, 2, 1))), _blockdiag(-jnp.transpose(S["c_im"], (0, 2, 1)))],
                         axis=1).astype(MXU_DTYPE)
    cs = GROUPS_PER_CLUSTER * Pst
    slab = lambda ab: jnp.tile(jnp.transpose(ab.reshape(ncl, cs // LANES, LANES), (1, 0, 2)), (1, B, 1))
    a_r, a_i = slab(ab_r), slab(ab_i)
    d_row = S["d"].reshape(1, D)

    xi = _interleave(x2, B, L)
    y, yg, h_r, h_i = _s5_fwd(xi, wb, wc, a_r, a_i, d_row, B, name="s5_fwd")
    _anchor(W, yg)
    z = _mm_nn(yg, W["w_glu"], bias=S["b_glu"].reshape(1, D), name="glu_z")
    gate = _glu_gate(y, z, name="glu_gate")
    mix_i = _mm_nn(gate, W["w_out"], name="s5_out")
    tok = _prepare(W, "w_up", mix_i)
    mix = _deinterleave(mix_i, B, L)
    h1, h1b, xh1, rs1 = _ln_fwd(x2, mix, S["ln_gain"][0, 0][None] + tok, S["ln_bias"][0, 0][None], name="ln_fwd_0a")

    def ffn_fwd(hb, l, prepare=None):
        hc = _mm_nn(hb, W["w_up"], l=l, out_dtype=MXU_DTYPE, name=f"ffn_up_{l}")
        tok = _prepare(W, prepare, hc) if prepare else 0.0
        a = _conv_glu_fwd(hc, S["conv_w"][l], S["conv_b"][l][None] + tok, L, name=f"ffn_conv_{l}")
        f = _mm_nn(a, W["w_down"], l=l, name=f"ffn_down_{l}")
        return hc, a, f

    _anchor(W, h1b)
    hc0, a0, f0 = ffn_fwd(h1b, 0, prepare="w_kv")
    h2, h2b, xh2, rs2 = _ln_fwd(h1, f0, S["ln_gain"][0, 1][None], S["ln_bias"][0, 1][None], name="ln_fwd_0b")

    _anchor(W, h2b)
    kv = _mm_nn(h2b, W["w_kv"], name="attn_kv")
    q = _mm_nn(h2b, W["w_q"], name="attn_q")
    bias = _attn_bias(S["rel_bias"], hpg)
    o3, l3 = _attn_fwd(q, kv, bias, L, hpg, name="attn_fwd")
    o, ob, lse = _attn_merge(o3, l3, HW, name="attn_merge")
    att = _mm_nn(ob, W["w_ao"], name="attn_out")
    h3, h3b, xh3, rs3 = _ln_fwd(h2, att, S["ln_gain"][1, 0][None], S["ln_bias"][1, 0][None], name="ln_fwd_1a")
    hc1, a1, f1 = ffn_fwd(h3b, 1)
    h4, _, xh4, rs4 = _ln_fwd(h3, f1, S["ln_gain"][1, 1][None], S["ln_bias"][1, 1][None], name="ln_fwd_1b")

    dh4, lrow = _loss_grad(h4, tgt2, name="loss")
    loss = lrow[0, 0]

    GW, GS = {}, {}

    def ffn_bwd(dzb, hb, hc, a, l):
        da = _mm_nt(dzb, W["w_down"], l=l, out_dtype=MXU_DTYPE, name=f"ffn_down_bwd_x_{l}")
        GW[f"w_down{l}"] = _tn(a, dzb, ptotal=1, np_cols=D, name=f"ffn_down_bwd_w_{l}")
        dc, dcw, dcb = _conv_glu_bwd(hc, da, S["conv_w"][l], S["conv_b"][l][None], L, name=f"ffn_conv_bwd_{l}")
        dhc = _conv_bwd_input(dc, S["conv_w"][l], L, name=f"ffn_conv_bwd_x_{l}")
        dh = _mm_nt(dhc, W["w_up"], l=l, name=f"ffn_up_bwd_x_{l}")
        GW[f"w_up{l}"] = _tn(hb, dhc, ptotal=W["w_up"].shape[0], np_cols=W["w_up"].shape[3], name=f"ffn_up_bwd_w_{l}")
        return dh, dcw, dcb

    dz4, dz4b, dg4, db4 = _ln_bwd([dh4], [1.0], xh4, rs4, S["ln_gain"][1, 1][None], name="ln_bwd_1b")
    dh3f, dcw1, dcb1 = ffn_bwd(dz4b, h3b, hc1, a1, 1)
    dz3, dz3b, dg3, db3 = _ln_bwd([dz4, dh3f], [DN_ALPHA, 1.0], xh3, rs3, S["ln_gain"][1, 0][None], name="ln_bwd_1a")
    do = _mm_nt(dz3b, W["w_ao"], name="attn_out_bwd_x")
    GW["w_ao"] = _tn(ob, dz3b, ptotal=1, np_cols=D, name="attn_out_bwd_w")
    dq, dk, dv, ds_sum = _attn_bwd(q, kv, do, o, lse, bias, L, hpg, name="attn_bwd")
    GS["rel_bias"] = _bias_grad(ds_sum, hpg, name="attn_bias_grad")
    GW["w_q"] = _tn(h2b, dq, ptotal=W["w_q"].shape[0], np_cols=W["w_q"].shape[3], name="attn_q_bwd_w")
    pkv, npkv = W["w_kv"].shape[0], W["w_kv"].shape[3]
    gkv = _tn(h2b, dk, ptotal=pkv, np_cols=npkv, p0=0, name="attn_k_bwd_w")
    GW["w_kv"] = _tn(h2b, dv, ptotal=pkv, np_cols=npkv, p0=pkv // 2, prev=gkv, name="attn_v_bwd_w")
    dh2q = _mm_nt(dq, W["w_q"], name="attn_q_bwd_x")
    dh2k = _mm_nt(dk, W["w_kv"], p0=0, pn=pkv // 2, name="attn_k_bwd_x")
    dh2v = _mm_nt(dv, W["w_kv"], p0=pkv // 2, pn=pkv // 2, name="attn_v_bwd_x")

    gain_0b = S["ln_gain"][0, 1][None]
    if on_grads is not None:
        gain_0b = gain_0b + on_grads(0, GW, dh2v)[0, 0]

    dz2, dz2b, dg2, db2 = _ln_bwd([dz3, dh2q, dh2k, dh2v], [DN_ALPHA, 1.0, 1.0, 1.0], xh2, rs2, gain_0b,
                                  name="ln_bwd_0b")
    dh1f, dcw0, dcb0 = ffn_bwd(dz2b, h1b, hc0, a0, 0)
    gain_0a = S["ln_gain"][0, 0][None]
    if on_grads is not None:
        gain_0a = gain_0a + on_grads(1, GW, GW["w_up0"])[0, 0]
    dz1, dz1b, dg1, db1 = _ln_bwd([dz2, dh1f], [DN_ALPHA, 1.0], xh1, rs1, gain_0a, name="ln_bwd_0a")
    dmix_i = _interleave(dz1b, B, L)
    dgate = _mm_nt(dmix_i, W["w_out"], name="s5_out_bwd_x")
    GW["w_out"] = _tn(gate, dmix_i, ptotal=1, np_cols=D, name="s5_out_bwd_w")
    dzg, dyg1, dbglu = _glu_bwd(y, z, dgate, name="glu_bwd")
    dyg2 = _mm_nt(dzg, W["w_glu"], name="glu_z_bwd_x")
    GW["w_glu"] = _tn(yg, dzg, ptotal=1, np_cols=D, name="glu_z_bwd_w")
    dy = _gelu_bwd(y, dyg1, dyg2, name="gelu_bwd")
    if on_grads is not None:
        d_row = d_row + on_grads(2, GW, GW["w_glu"])[0, 0]
    du_i, g_r, g_i, dar, dai, dd = _s5_bwd(dy, xi, h_r, h_i, wb, wc, a_r, a_i, d_row, B, name="s5_bwd")
    started = on_grads(3, GW, du_i) if on_grads is not None else None
    dwb_r = _cluster_tn(xi, g_r, ncl, tok_left=True, name="s5_b_grad_re", after=started)
    dwb_i = _cluster_tn(xi, g_i, ncl, tok_left=True, name="s5_b_grad_im")
    dwc_r = _cluster_tn(dy, h_r, ncl, tok_left=False, name="s5_c_grad_re")
    dwc_i = _cluster_tn(dy, h_i, ncl, tok_left=False, name="s5_c_grad_im")
    grad_x = _axpy(dz1, _deinterleave(du_i, B, L), DN_ALPHA, name="grad_x")

    dbb_r = jnp.transpose(_unblockdiag(dwb_r, SSM_GROUP, Pst), (0, 2, 1))
    dbb_i = jnp.transpose(_unblockdiag(dwb_i, SSM_GROUP, Pst), (0, 2, 1))
    unslab = lambda da: jnp.transpose(da.reshape(cs // LANES, B, ncl, LANES).sum(1), (1, 0, 2)).reshape(G, Pst)
    dab_r, dab_i = unslab(dar), unslab(dai)
    GS["lam_re"], GS["lam_im"], GS["log_dt"], GS["b_re"], GS["b_im"] = disc_vjp((dab_r, dab_i, dbb_r, dbb_i))
    GS["c_re"] = jnp.transpose(_unblockdiag(dwc_r, Pst, SSM_GROUP), (0, 2, 1))
    GS["c_im"] = -jnp.transpose(_unblockdiag(dwc_i, Pst, SSM_GROUP), (0, 2, 1))
    GS["d"] = dd.reshape(G, SSM_GROUP)
    GS["b_glu"] = dbglu.reshape(D)
    GS["conv_w"] = jnp.stack([dcw0, dcw1])
    GS["conv_b"] = jnp.stack([dcb0[0], dcb1[0]])
    GS["ln_gain"] = jnp.stack([jnp.stack([dg1[0], dg2[0]]), jnp.stack([dg3[0], dg4[0]])])
    GS["ln_bias"] = jnp.stack([jnp.stack([db1[0], db2[0]]), jnp.stack([db3[0], db4[0]])])
    return loss, grad_x.reshape(B, L, D), GW, GS


SMALL_REPLICATED = ("lam_re", "lam_im", "log_dt", "b_re", "b_im", "c_re", "c_im", "d", "rel_bias", "conv_b")
SMALL_SHARDED = ("b_glu", "conv_w", "ln_gain", "ln_bias")
SMALL_ORDER = SMALL_REPLICATED + SMALL_SHARDED


def _pack(arrs, lanes, row_mult):
    flat = jnp.concatenate([a.reshape(-1).astype(F32) for a in arrs])
    rows = -(-flat.shape[0] // lanes)
    rows = -(-rows // row_mult) * row_mult
    return jnp.pad(flat, (0, rows * lanes - flat.shape[0])).reshape(rows, lanes)


def _unpack(packed, shapes):
    flat = packed.reshape(-1)
    out, off = [], 0
    for s in shapes:
        n = int(np.prod(s))
        out.append(flat[off:off + n].reshape(s))
        off += n
    return out


def kernel(x, s5_lam_re, s5_lam_im, s5_log_dt, s5_b_re, s5_b_im, s5_c_re, s5_c_im, s5_d, s5_w_glu, s5_b_glu, s5_w_out, attn_w_kv, attn_w_q, attn_w_out, rel_bias, ffn_w_up, ffn_conv_w, ffn_conv_b, ffn_w_down, ln_gain, ln_bias, loss_target, m_s5_lam_re, m_s5_lam_im, m_s5_log_dt, m_s5_b_re, m_s5_b_im, m_s5_c_re, m_s5_c_im, m_s5_d, m_s5_w_glu, m_s5_b_glu, m_s5_w_out, m_attn_w_kv, m_attn_w_q, m_attn_w_out, m_rel_bias, m_ffn_w_up, m_ffn_conv_w, m_ffn_conv_b, m_ffn_w_down, m_ln_gain, m_ln_bias, v_s5_lam_re, v_s5_lam_im, v_s5_log_dt, v_s5_b_re, v_s5_b_im, v_s5_c_re, v_s5_c_im, v_s5_d, v_s5_w_glu, v_s5_b_glu, v_s5_w_out, v_attn_w_kv, v_attn_w_q, v_attn_w_out, v_rel_bias, v_ffn_w_up, v_ffn_conv_w, v_ffn_conv_b, v_ffn_w_down, v_ln_gain, v_ln_bias):
    names = ["s5_lam_re", "s5_lam_im", "s5_log_dt", "s5_b_re", "s5_b_im", "s5_c_re", "s5_c_im", "s5_d", "s5_w_glu",
             "s5_b_glu", "s5_w_out", "attn_w_kv", "attn_w_q", "attn_w_out", "rel_bias", "ffn_w_up", "ffn_conv_w",
             "ffn_conv_b", "ffn_w_down", "ln_gain", "ln_bias"]
    loc = locals()
    w_in = {n: loc[n] for n in names}
    m_in = {n: loc["m_" + n] for n in names}
    v_in = {n: loc["v_" + n] for n in names}
    chip = 2 * lax.axis_index("x") + lax.axis_index("y")
    core = lax.axis_index("c")
    chip_idx = jnp.reshape(chip, (1,)).astype(jnp.int32)
    cidx = jnp.reshape(core, (1,)).astype(jnp.int32)

    big = [("w_glu", "s5_w_glu", "rows"), ("w_out", "s5_w_out", "rows"), ("w_ao", "attn_w_out", "rows"),
           ("w_kv", "attn_w_kv", "cols"), ("w_q", "attn_w_q", "cols"),
           ("w_up", "ffn_w_up", "layer_cols"), ("w_down", "ffn_w_down", "layer_rows")]

    def halves(t, kind):
        if kind.startswith("layer"):
            return t
        r, c = t.shape[-2:]
        return t.reshape(2, r // 2, c)

    def to_weight(g, kind):
        _, _, r, c = g.shape
        if kind == "rows":
            return g.reshape(1, 1, 8 * r, c)
        if kind == "cols":
            return g.reshape(4, 1, 2 * r, c)
        if kind == "layer_cols":
            return g
        return jnp.transpose(g, (1, 0, 2, 3)).reshape(1, 2, 4 * r, c)

    small_sh = {"b_glu": s5_b_glu[0], "conv_w": ffn_conv_w, "ln_gain": ln_gain, "ln_bias": ln_bias}
    sh_shapes = [small_sh[k].shape for k in SMALL_SHARDED]
    sh_pack = _pack([small_sh[k] for k in SMALL_SHARDED], 128, 16)

    shards = [halves(w_in[src].astype(MXU_DTYPE), kind) for _, src, kind in big]
    shards.append(sh_pack.reshape(2, sh_pack.shape[0] // 2, 128))
    shard_of = {key: s for (key, _, _), s in zip(big, shards)}
    shard_of["small"] = shards[-1]
    kind_of = {key: kind for key, _, kind in big}

    group_keys = [["w_glu", "w_out", "small"], ["w_up", "w_down"], ["w_kv", "w_q", "w_ao"]]
    started, token = _gather_start([[shard_of[k] for k in g] for g in group_keys], name="weights_gather_start")

    forwarding = {}

    def prepare_group(gi, after):
        ssem, rsem, thru, lands = started[gi]
        lands = _gather_wait(ssem, rsem, thru, lands, after, name=f"weights_gather_wait_{gi}")
        forwarding[gi], tok = _gather_forward_start(lands, name=f"weights_gather_forward_start_{gi}")
        return tok

    def finish_group(gi, after):
        if gi in forwarding:
            lands = _gather_forward_wait(forwarding.pop(gi), after, name=f"weights_gather_forward_wait_{gi}")
        else:
            ssem, rsem, thru, lands = started[gi]
            lands = _gather_wait(ssem, rsem, thru, lands, after, name=f"weights_gather_wait_{gi}")
            lands = _gather_forward(lands, name=f"weights_gather_forward_{gi}")
        out = {}
        for key, land in zip(group_keys[gi], lands):
            full = lax.dynamic_update_slice(land, shard_of[key][None], (chip, 0, 0, 0))
            if key == "small":
                parts = [_unpack(full[p], sh_shapes) for p in range(4)]
                for i, k in enumerate(SMALL_SHARDED):
                    out[k] = jnp.concatenate([parts[p][i] for p in range(4)], axis=-1)
            else:
                out[key] = to_weight(full, kind_of[key])
        return out

    replicated = dict(lam_re=s5_lam_re[0], lam_im=s5_lam_im[0], log_dt=s5_log_dt[0], b_re=s5_b_re[0], b_im=s5_b_im[0],
                      c_re=s5_c_re[0], c_im=s5_c_im[0], rel_bias=rel_bias, conv_b=ffn_conv_b,
                      d=s5_d[0] + token[0, 0])
    group_of = {k: gi for gi, g in enumerate(group_keys) for k in g if k != "small"}
    group_of.update({k: 0 for k in SMALL_SHARDED})
    group_of.update({k: "replicated" for k in replicated})
    params = _Lazy(group_of, lambda g, after: replicated if g == "replicated" else finish_group(g, after), prepare_group)

    red = [("w_up1", "ffn_w_up", 1), ("w_down1", "ffn_w_down", 1), ("w_ao", "attn_w_out", 0), ("w_kv", "attn_w_kv", 0),
           ("w_q", "attn_w_q", 0), ("w_down0", "ffn_w_down", 0), ("w_up0", "ffn_w_up", 0), ("w_out", "s5_w_out", 0),
           ("w_glu", "s5_w_glu", 0)]
    stages = [red[:5], red[5:7], red[7:]]

    def grad_halves(gw, key, src):
        r, c = w_in[src].shape[-2:]
        return gw[key].reshape(4, 2, r // 2, c)

    sent, early = {}, []

    def on_grads(stage, gw, latest):
        tokens = []
        if stage > 0:
            tag = "abc"[stage - 1]
            ga, theirs = _pair_send_wait(sent.pop(stage - 1), latest, name=f"rs_pair_send_wait_{tag}")
            hf, hx = _pair_sums(ga, [True] * len(ga), cidx, tag, theirs)
            started, tok = _chip_exchange_start(hx, name=f"rs_chip_exchange_start_{tag}")
            early.append((hf, started, tag))
            tokens.append(tok)
        if stage < len(stages):
            ga = [grad_halves(gw, key, src) for key, src, _ in stages[stage]]
            sent[stage], tok = _pair_send_start(ga, name=f"rs_pair_send_start_{'abc'[stage]}")
            tokens.append(tok)
        return sum(tokens[1:], tokens[0])

    loss, grad_x, GW, GS = _local_step(x, loss_target, params, params, on_grads)

    gs_shapes = [GS[k].shape for k in SMALL_ORDER] + [(1,)]
    gs_pack = _pack([GS[k] for k in SMALL_ORDER] + [loss.reshape(1)], 128, 64)
    rs = gs_pack.shape[0] // 8
    gs_halves = [gs_pack.reshape(4, 2, rs, 128)]
    hf_s, hx_s = _pair_sums(gs_halves, [False], cidx, "s")
    started_s, after = _chip_exchange_start(hx_s, name="rs_chip_exchange_start_s")
    mine = []
    for hf, started, tag in early:
        got = _chip_exchange_wait(started, after, name=f"rs_chip_exchange_wait_{tag}")
        mine += _chip_sums(hf, got, chip_idx, tag)
        after = mine[-1]
    mine_s = _chip_sums(hf_s, _chip_exchange_wait(started_s, after, name="rs_chip_exchange_wait_s"), chip_idx, "s")[0]
    other_s = _pair_swap([mine_s], name="rs_pair_swap_small")[0]
    swapping, tok = _pair_send_start(mine, name="rs_pair_swap_start", whole=True, after=other_s)
    small_halves = jnp.where(core == 0, jnp.concatenate([mine_s, other_s]), jnp.concatenate([other_s, mine_s]))
    small_halves = small_halves + tok[0, 0]
    small_all = _gather([small_halves], chip, name="small_grads_all_gather")[0]
    totals = _unpack(small_all, gs_shapes)
    gsmall = dict(zip(SMALL_ORDER, totals))
    loss = totals[-1][0]

    small_w = {"lam_re": s5_lam_re, "lam_im": s5_lam_im, "log_dt": s5_log_dt, "b_re": s5_b_re, "b_im": s5_b_im,
               "c_re": s5_c_re, "c_im": s5_c_im, "d": s5_d, "rel_bias": rel_bias, "conv_b": ffn_conv_b,
               "b_glu": s5_b_glu, "conv_w": ffn_conv_w, "ln_gain": ln_gain, "ln_bias": ln_bias}
    small_name = {"lam_re": "s5_lam_re", "lam_im": "s5_lam_im", "log_dt": "s5_log_dt", "b_re": "s5_b_re", "b_im": "s5_b_im",
                  "c_re": "s5_c_re", "c_im": "s5_c_im", "d": "s5_d", "rel_bias": "rel_bias", "conv_b": "ffn_conv_b",
                  "b_glu": "s5_b_glu", "conv_w": "ffn_conv_w", "ln_gain": "ln_gain", "ln_bias": "ln_bias"}
    sg = {}
    for k in SMALL_ORDER:
        shp = small_w[k].shape
        g = gsmall[k]
        if k in SMALL_SHARDED:
            width = shp[-1]
            g = lax.dynamic_slice_in_dim(g, chip * width, width, axis=g.ndim - 1)
        sg[k] = g.reshape(shp)
    sd, snm, snv = {}, {}, {}
    for k in SMALL_ORDER:
        shp = small_w[k].shape
        flat = lambda t: t.reshape(-1, shp[-1])
        r3 = _adamw(flat(small_w[k]), flat(sg[k]), flat(m_in[small_name[k]]), flat(v_in[small_name[k]]),
                    name=f"adamw_{k}")
        sd[k], snm[k], snv[k] = (t.reshape(shp) for t in r3)

    mine, other = _pair_send_wait(swapping, [sd[k] for k in SMALL_ORDER], name="rs_pair_swap_wait", whole=True)
    big_res = {}
    for (key, src, layer), gm, go in zip(red, mine, other):
        nl = w_in[src].shape[0] if src in ("ffn_w_up", "ffn_w_down") else 1
        r, c = w_in[src].shape[-2:]
        view = lambda t: t.reshape(nl, 2, r // 2, c)
        res4 = _adamw_halves(view(w_in[src]), view(m_in[src]), view(v_in[src]), gm, go, cidx, layer=layer,
                             prev=big_res.get(src), name=f"adamw_{key}")
        big_res[src] = res4
    big_res = {src: tuple(t.reshape(w_in[src].shape) for t in res4) for src, res4 in big_res.items()}

    def big_out(i):
        return {src: big_res[src][i] for _, src, _ in big}

    res = [{}, {}, {}, {}]
    for i in range(4):
        res[i].update(big_out(i))
    for k in SMALL_ORDER:
        res[0][small_name[k]] = sg[k]
        res[1][small_name[k]] = sd[k]
        res[2][small_name[k]] = snm[k]
        res[3][small_name[k]] = snv[k]
    outs = [loss, grad_x]
    for i in range(4):
        outs += [res[i][n] for n in names]
    return tuple(outs)
```

```python
import functools
import math

import numpy as np
import jax
import jax.numpy as jnp
from jax import lax
from jax.experimental import pallas as pl
from jax.experimental.pallas import tpu as pltpu

F32 = jnp.float32
BF16 = jnp.bfloat16
MXU_DTYPE = jnp.bfloat16
V7X_VMEM_LIMIT_BYTES = 52 << 20
MESH = pl.DeviceIdType.MESH

DEPTH = 2
SSM_GROUP = 16
SSM_STATE = 64
GROUPS_PER_CLUSTER = 16
CLUSTER_W = GROUPS_PER_CLUSTER * SSM_GROUP
HEAD_DIM = 64
DILATIONS = (1, 4, 16)
BAND = 128
ATTN_BATCH = (4, 8)
DEEP_BUFFERS = 3
NEG_BIG = -1e30
REL_BUCKETS = 32
REL_MAX_DIST = 2048
DN_ALPHA = (2.0 * DEPTH) ** 0.25
LN_EPS = 1e-5
ADAM_LR, ADAM_B1, ADAM_B2, ADAM_EPS, ADAM_WD, ADAM_STEP = 0.001, 0.9, 0.999, 1e-08, 0.01, 10
GELU_K = math.sqrt(2.0 / math.pi)
GELU_C = 0.044715


def _pallas(body, **kw):
    return pl.pallas_call(body, **kw)


def _params(sem=None):
    return pltpu.CompilerParams(dimension_semantics=sem, vmem_limit_bytes=V7X_VMEM_LIMIT_BYTES)


def _pick(n, cands):
    for c in cands:
        if n % c == 0:
            return c
    return n


def _sigmoid(z):
    return 1.0 / (1.0 + jnp.exp(-z))


def _gelu(y):
    return 0.5 * y * (1.0 + jnp.tanh(GELU_K * (y + GELU_C * y * y * y)))


def _gelu_grad(y):
    t = jnp.tanh(GELU_K * (y + GELU_C * y * y * y))
    return 0.5 * (1.0 + t) + 0.5 * y * (1.0 - t * t) * (GELU_K * (1.0 + 3.0 * GELU_C * y * y))


def _mm_nn(a, w, *, l=0, bias=None, out_dtype=F32, name):
    T, K = a.shape
    P, _, _, Np = w.shape
    tm = _pick(T, (1024, 512, 256, 128))
    tn = _pick(Np, (1408, 1024, 768, 512, 384, 256, 128))
    nj = Np // tn

    def body(*refs):
        if bias is None:
            a_ref, w_ref, o_ref = refs
        else:
            a_ref, w_ref, b_ref, o_ref = refs
        acc = jnp.dot(a_ref[...].astype(MXU_DTYPE), w_ref[...].astype(MXU_DTYPE), preferred_element_type=F32)
        if bias is not None:
            acc = acc + b_ref[...]
        o_ref[...] = acc.astype(o_ref.dtype)

    in_specs = [pl.BlockSpec((tm, K), lambda p, j, i: (i, 0)),
                pl.BlockSpec((None, None, K, tn), lambda p, j, i: (p, l, 0, j))]
    args = [a, w]
    if bias is not None:
        in_specs.append(pl.BlockSpec((1, tn), lambda p, j, i: (0, p * nj + j)))
        args.append(bias)
    return _pallas(
        body, name=name, grid=(P, nj, T // tm), in_specs=in_specs,
        out_specs=pl.BlockSpec((tm, tn), lambda p, j, i: (i, p * nj + j)),
        out_shape=jax.ShapeDtypeStruct((T, P * Np), out_dtype),
        compiler_params=_params(("parallel", "parallel", "parallel")),
    )(*args)


def _mm_nt(a, w, *, l=0, p0=0, pn=None, out_dtype=F32, name):
    T = a.shape[0]
    _, _, K, Np = w.shape
    pn = w.shape[0] if pn is None else pn
    tm = _pick(T, (1024, 512, 256, 128) if K <= 1024 else (512, 256, 128))
    tn = _pick(Np, (1536, 1408, 1024, 768, 512, 384, 256, 128))
    nj = Np // tn
    nred = pn * nj

    def body(a_ref, w_ref, o_ref, acc):
        r = pl.program_id(1)

        @pl.when(r == 0)
        def _():
            acc[...] = jnp.zeros_like(acc)

        acc[...] += lax.dot_general(a_ref[...].astype(MXU_DTYPE), w_ref[...].astype(MXU_DTYPE),
                                    (((1,), (1,)), ((), ())), preferred_element_type=F32)

        @pl.when(r == nred - 1)
        def _():
            o_ref[...] = acc[...].astype(o_ref.dtype)

    return _pallas(
        body, name=name, grid=(T // tm, nred),
        in_specs=[pl.BlockSpec((tm, tn), lambda i, r: (i, r)),
                  pl.BlockSpec((None, None, K, tn), lambda i, r: (p0 + r // nj, l, 0, r % nj))],
        out_specs=pl.BlockSpec((tm, K), lambda i, r: (i, 0)),
        out_shape=jax.ShapeDtypeStruct((T, K), out_dtype),
        scratch_shapes=[pltpu.VMEM((tm, K), F32)],
        compiler_params=_params(("parallel", "arbitrary")),
    )(a, w)


def _tn(a, b, *, ptotal, np_cols, nl=1, l=0, p0=0, prev=None, name):
    T, K = a.shape
    Np = np_cols
    pn = b.shape[1] // Np
    tt = _pick(T, (1024, 512, 256, 128))
    tk = _pick(K, (1408, 1024, 512, 256, 128))
    tn = _pick(Np, (1408, 768, 512, 256, 128))
    if tk * tn > 1408 * 1024:
        tn = _pick(Np, (512, 256, 128))
    nj = Np // tn
    nt = T // tt

    def body(*refs):
        a_ref, b_ref = refs[0], refs[1]
        o_ref, acc = refs[-2], refs[-1]
        t = pl.program_id(3)

        @pl.when(t == 0)
        def _():
            acc[...] = jnp.zeros_like(acc)

        acc[...] += lax.dot_general(a_ref[...].astype(MXU_DTYPE), b_ref[...].astype(MXU_DTYPE),
                                    (((0,), (0,)), ((), ())), preferred_element_type=F32)

        @pl.when(t == nt - 1)
        def _():
            o_ref[...] = acc[...]

    in_specs = [pl.BlockSpec((tt, tk), lambda kb, p, j, t: (t, kb)),
                pl.BlockSpec((tt, tn), lambda kb, p, j, t: (t, p * nj + j))]
    args = [a, b]
    aliases = {}
    if prev is not None:
        in_specs.append(pl.BlockSpec(memory_space=pl.ANY))
        args.append(prev)
        aliases = {2: 0}
    return _pallas(
        body, name=name, grid=(K // tk, pn, nj, nt), in_specs=in_specs,
        out_specs=pl.BlockSpec((None, None, tk, tn), lambda kb, p, j, t: (p0 + p, l, kb, j)),
        out_shape=jax.ShapeDtypeStruct((ptotal, nl, K, Np), F32),
        scratch_shapes=[pltpu.VMEM((tk, tn), F32)],
        input_output_aliases=aliases,
        compiler_params=_params(("parallel", "parallel", "parallel", "arbitrary")),
    )(*args)


def _rows(tm, f):
    return pl.BlockSpec((tm, f), lambda i: (i, 0))


def _whole(shape):
    nd = len(shape)
    return pl.BlockSpec(shape, lambda i: (0,) * nd)


def _ln_fwd(xres, f, gain, bias, *, name):
    T, D = xres.shape
    tm = _pick(T, (256, 128))

    def body(x_ref, f_ref, g_ref, b_ref, y_ref, yb_ref, xh_ref, rs_ref):
        z = DN_ALPHA * x_ref[...] + f_ref[...]
        mu = jnp.mean(z, axis=-1, keepdims=True)
        zc = z - mu
        var = jnp.mean(zc * zc, axis=-1, keepdims=True)
        rstd = lax.rsqrt(var + LN_EPS)
        xh = zc * rstd
        y = xh * g_ref[...] + b_ref[...]
        y_ref[...] = y
        yb_ref[...] = y.astype(yb_ref.dtype)
        xh_ref[...] = xh
        rs_ref[...] = rstd

    return _pallas(
        body, name=name, grid=(T // tm,),
        in_specs=[_rows(tm, D), _rows(tm, D), _whole((1, D)), _whole((1, D))],
        out_specs=[_rows(tm, D), _rows(tm, D), _rows(tm, D), _rows(tm, 1)],
        out_shape=[jax.ShapeDtypeStruct((T, D), F32), jax.ShapeDtypeStruct((T, D), MXU_DTYPE),
                   jax.ShapeDtypeStruct((T, D), F32), jax.ShapeDtypeStruct((T, 1), F32)],
        compiler_params=_params(("parallel",)),
    )(xres, f, gain, bias)


def _ln_bwd(addends, coefs, xhat, rstd, gain, *, name):
    T, D = xhat.shape
    tm = _pick(T, (256, 128))
    n = len(addends)

    def body(*refs):
        adds = refs[:n]
        xh_ref, rs_ref, g_ref, dz_ref, dzb_ref, dg_ref, db_ref = refs[n:]
        dy = coefs[0] * adds[0][...]
        for c, r in zip(coefs[1:], adds[1:]):
            dy = dy + c * r[...]
        xh = xh_ref[...]
        dxh = dy * g_ref[...]
        m1 = jnp.mean(dxh, axis=-1, keepdims=True)
        m2 = jnp.mean(dxh * xh, axis=-1, keepdims=True)
        dz = rs_ref[...] * (dxh - m1 - xh * m2)
        dz_ref[...] = dz
        dzb_ref[...] = dz.astype(dzb_ref.dtype)

        @pl.when(pl.program_id(0) == 0)
        def _():
            dg_ref[...] = jnp.zeros_like(dg_ref)
            db_ref[...] = jnp.zeros_like(db_ref)

        dg_ref[...] += jnp.sum(dy * xh, axis=0, keepdims=True)
        db_ref[...] += jnp.sum(dy, axis=0, keepdims=True)

    return _pallas(
        body, name=name, grid=(T // tm,),
        in_specs=[_rows(tm, D)] * n + [_rows(tm, D), _rows(tm, 1), _whole((1, D))],
        out_specs=[_rows(tm, D), _rows(tm, D), _whole((1, D)), _whole((1, D))],
        out_shape=[jax.ShapeDtypeStruct((T, D), F32), jax.ShapeDtypeStruct((T, D), MXU_DTYPE),
                   jax.ShapeDtypeStruct((1, D), F32), jax.ShapeDtypeStruct((1, D), F32)],
        compiler_params=_params(("arbitrary",)),
    )(*addends, xhat, rstd, gain)


def _loss_grad(y, tgt, *, name):
    T, D = y.shape
    tm = _pick(T, (256, 128))

    def body(y_ref, t_ref, dy_ref, l_ref):
        e = y_ref[...] - t_ref[...]
        dy_ref[...] = e * (1.0 / D)

        @pl.when(pl.program_id(0) == 0)
        def _():
            l_ref[...] = jnp.zeros_like(l_ref)

        l_ref[...] += jnp.zeros_like(l_ref) + jnp.sum(e * e) * (0.5 / D)

    return _pallas(
        body, name=name, grid=(T // tm,),
        in_specs=[_rows(tm, D), _rows(tm, D)],
        out_specs=[_rows(tm, D), _whole((1, 128))],
        out_shape=[jax.ShapeDtypeStruct((T, D), F32), jax.ShapeDtypeStruct((1, 128), F32)],
        compiler_params=_params(("arbitrary",)),
    )(y, tgt)


def _axpy(a, b, ca, *, name):
    T, D = a.shape
    tm = _pick(T, (256, 128))

    def body(a_ref, b_ref, o_ref):
        o_ref[...] = ca * a_ref[...] + b_ref[...]

    return _pallas(
        body, name=name, grid=(T // tm,), in_specs=[_rows(tm, D), _rows(tm, D)], out_specs=_rows(tm, D),
        out_shape=jax.ShapeDtypeStruct((T, D), F32), compiler_params=_params(("parallel",)),
    )(a, b)


def _glu_gate(y, z, *, name):
    T, D = y.shape
    tm = _pick(T, (256, 128))

    def body(y_ref, z_ref, g_ref):
        g_ref[...] = (_gelu(y_ref[...]) * _sigmoid(z_ref[...])).astype(g_ref.dtype)

    return _pallas(
        body, name=name, grid=(T // tm,), in_specs=[_rows(tm, D), _rows(tm, D)], out_specs=_rows(tm, D),
        out_shape=jax.ShapeDtypeStruct((T, D), MXU_DTYPE), compiler_params=_params(("parallel",)),
    )(y, z)


def _glu_bwd(y, z, dg, *, name):
    T, D = y.shape
    tm = _pick(T, (256, 128))

    def body(y_ref, z_ref, dg_ref, dzb_ref, dyg_ref, db_ref):
        s = _sigmoid(z_ref[...])
        dg = dg_ref[...]
        dz = dg * _gelu(y_ref[...]) * s * (1.0 - s)
        dzb_ref[...] = dz.astype(dzb_ref.dtype)
        dyg_ref[...] = dg * s

        @pl.when(pl.program_id(0) == 0)
        def _():
            db_ref[...] = jnp.zeros_like(db_ref)

        db_ref[...] += jnp.sum(dz, axis=0, keepdims=True)

    return _pallas(
        body, name=name, grid=(T // tm,), in_specs=[_rows(tm, D)] * 3,
        out_specs=[_rows(tm, D), _rows(tm, D), _whole((1, D))],
        out_shape=[jax.ShapeDtypeStruct((T, D), MXU_DTYPE), jax.ShapeDtypeStruct((T, D), F32),
                   jax.ShapeDtypeStruct((1, D), F32)],
        compiler_params=_params(("arbitrary",)),
    )(y, z, dg)


def _gelu_bwd(y, d1, d2, *, name):
    T, D = y.shape
    tm = _pick(T, (256, 128))

    def body(y_ref, a_ref, b_ref, o_ref):
        o_ref[...] = (a_ref[...] + b_ref[...]) * _gelu_grad(y_ref[...])

    return _pallas(
        body, name=name, grid=(T // tm,), in_specs=[_rows(tm, D)] * 3, out_specs=_rows(tm, D),
        out_shape=jax.ShapeDtypeStruct((T, D), F32), compiler_params=_params(("parallel",)),
    )(y, d1, d2)


CONV_ROWS = 128
CONV_EDGE = 16


def _row_shifts(x, edge, drop_edge, tm, back):
    keep = jnp.where(drop_edge, 0.0, 1.0).astype(edge.dtype)
    ext = jnp.concatenate([edge * keep, x] if back else [x, edge * keep], axis=0)
    row = lax.broadcasted_iota(jnp.int32, (tm, tm + CONV_EDGE), 0)
    col = lax.broadcasted_iota(jnp.int32, (tm, tm + CONV_EDGE), 1)
    base = row + CONV_EDGE if back else row
    out = []
    for k in (1, 2):
        pick = (col == (base - k if back else base + k)).astype(x.dtype)
        out.append(jnp.dot(pick, ext, preferred_element_type=F32))
    return out


def _conv_specs(T, F2, tm):
    return [_rows(tm, F2),
            pl.BlockSpec((CONV_EDGE, F2), lambda i: (jnp.maximum(i * (tm // CONV_EDGE) - 1, 0), 0))]


def _conv_glu_fwd(hc, conv_w, conv_b, L, *, name):
    T, F2 = hc.shape
    F = F2 // 2
    tm = CONV_ROWS

    def body(x_ref, e_ref, w_ref, b_ref, a_ref):
        at_start = (pl.program_id(0) * tm) % L == 0
        x1, x2 = _row_shifts(x_ref[...], e_ref[...], at_start, tm, True)
        x = x_ref[...].astype(F32)
        c = b_ref[...] + w_ref[0:1, :] * x + w_ref[1:2, :] * x1 + w_ref[2:3, :] * x2
        val, gate = c[:, :F], c[:, F:]
        a_ref[...] = (gate * _sigmoid(gate) * val).astype(a_ref.dtype)

    return _pallas(
        body, name=name, grid=(T // tm,),
        in_specs=_conv_specs(T, F2, tm) + [_whole((3, F2)), _whole((1, F2))],
        out_specs=_rows(tm, F),
        out_shape=jax.ShapeDtypeStruct((T, F), MXU_DTYPE), compiler_params=_params(("parallel",)),
    )(hc, hc, conv_w, conv_b)


def _conv_glu_bwd(hc, da, conv_w, conv_b, L, *, name):
    T, F2 = hc.shape
    F = F2 // 2
    tm = CONV_ROWS

    def body(x_ref, e_ref, da_ref, w_ref, b_ref, dc_ref, dw_ref, db_ref):
        at_start = (pl.program_id(0) * tm) % L == 0
        x1, x2 = _row_shifts(x_ref[...], e_ref[...], at_start, tm, True)
        x = x_ref[...].astype(F32)
        c = b_ref[...] + w_ref[0:1, :] * x + w_ref[1:2, :] * x1 + w_ref[2:3, :] * x2
        val, gate = c[:, :F], c[:, F:]
        s = _sigmoid(gate)
        da = da_ref[...].astype(F32)
        dval = da * (gate * s)
        dgate = da * val * (s * (1.0 + gate * (1.0 - s)))
        dc = jnp.concatenate([dval, dgate], axis=-1)
        dc_ref[...] = dc.astype(dc_ref.dtype)

        @pl.when(pl.program_id(0) == 0)
        def _():
            dw_ref[...] = jnp.zeros_like(dw_ref)
            db_ref[...] = jnp.zeros_like(db_ref)

        dw_ref[0:1, :] += jnp.sum(dc * x, axis=0, keepdims=True)
        dw_ref[1:2, :] += jnp.sum(dc * x1, axis=0, keepdims=True)
        dw_ref[2:3, :] += jnp.sum(dc * x2, axis=0, keepdims=True)
        db_ref[...] += jnp.sum(dc, axis=0, keepdims=True)

    return _pallas(
        body, name=name, grid=(T // tm,),
        in_specs=_conv_specs(T, F2, tm) + [_rows(tm, F), _whole((3, F2)), _whole((1, F2))],
        out_specs=[_rows(tm, F2), _whole((3, F2)), _whole((1, F2))],
        out_shape=[jax.ShapeDtypeStruct((T, F2), MXU_DTYPE), jax.ShapeDtypeStruct((3, F2), F32),
                   jax.ShapeDtypeStruct((1, F2), F32)],
        compiler_params=_params(("arbitrary",)),
    )(hc, hc, da, conv_w, conv_b)


def _conv_bwd_input(dc, conv_w, L, *, name):
    T, F2 = dc.shape
    tm = CONV_ROWS
    edge = CONV_EDGE
    last_blk = T // edge - 1

    def body(x_ref, e_ref, w_ref, o_ref):
        at_end = ((pl.program_id(0) + 1) * tm) % L == 0
        x1, x2 = _row_shifts(x_ref[...], e_ref[...], at_end, tm, False)
        x = x_ref[...].astype(F32)
        o_ref[...] = (w_ref[0:1, :] * x + w_ref[1:2, :] * x1 + w_ref[2:3, :] * x2).astype(o_ref.dtype)

    return _pallas(
        body, name=name, grid=(T // tm,),
        in_specs=[_rows(tm, F2),
                  pl.BlockSpec((edge, F2), lambda i: (jnp.minimum((i + 1) * (tm // edge), last_blk), 0)),
                  _whole((3, F2))],
        out_specs=_rows(tm, F2),
        out_shape=jax.ShapeDtypeStruct((T, F2), MXU_DTYPE), compiler_params=_params(("parallel",)),
    )(dc, dc, conv_w)


S5_CHUNK = 128
LANES = 128


def _slab_rows(c, n, ncl):
    return pl.ds(c, n) if ncl == 1 else pl.ds(c, n, stride=ncl)


def _slab_put(ref, c, n, ncl, val):
    for s in range(val.shape[1] // LANES):
        ref[s, _slab_rows(c, n, ncl), :] = val[:, s * LANES:(s + 1) * LANES]


def _slab_get(ref, c, n, ncl):
    return jnp.concatenate([ref[s, _slab_rows(c, n, ncl), :] for s in range(ref.shape[0])], axis=-1)


def _slabs(n_slab, rows):
    return pl.BlockSpec((n_slab, rows, LANES), lambda i: (0, i, 0))


def _s5_fwd(xi, wb, wc, a_r, a_i, d_row, B, *, name):
    T, D = xi.shape
    ncl = wb.shape[0]
    cs = wb.shape[2] // 2
    ns = cs // LANES
    R = B * ncl
    Q = S5_CHUNK
    QR = Q * ncl
    nsteps = Q // B

    def body(x_ref, wb_ref, wc_ref, ar_ref, ai_ref, d_ref, y_ref, yg_ref, hr_ref, hi_ref, bur, bui, cr, ci):
        @pl.when(pl.program_id(0) == 0)
        def _():
            cr[...] = jnp.zeros_like(cr)
            ci[...] = jnp.zeros_like(ci)

        x = x_ref[...]
        xb = x.astype(MXU_DTYPE)
        for c in range(ncl):
            bu = jnp.dot(xb[:, c * CLUSTER_W:(c + 1) * CLUSTER_W], wb_ref[c], preferred_element_type=F32)
            _slab_put(bur, c, Q, ncl, bu[:, :cs])
            _slab_put(bui, c, Q, ncl, bu[:, cs:])
        ar = ar_ref[...]
        ai = ai_ref[...]

        def step(k, carry):
            hr, hi = carry
            sl = pl.ds(pl.multiple_of(k * R, R), R)
            nr = ar * hr - ai * hi + bur[:, sl, :]
            ni = ar * hi + ai * hr + bui[:, sl, :]
            hr_ref[:, sl, :] = nr
            hi_ref[:, sl, :] = ni
            return nr, ni

        hr, hi = lax.fori_loop(0, nsteps, step, (cr[...], ci[...]), unroll=4)
        cr[...] = hr
        ci[...] = hi
        parts = []
        for c in range(ncl):
            hrc = _slab_get(hr_ref, c, Q, ncl).astype(MXU_DTYPE)
            hic = _slab_get(hi_ref, c, Q, ncl).astype(MXU_DTYPE)
            parts.append(jnp.dot(hrc, wc_ref[c, :cs, :], preferred_element_type=F32)
                         + jnp.dot(hic, wc_ref[c, cs:, :], preferred_element_type=F32))
        y = d_ref[...] * x + (parts[0] if ncl == 1 else jnp.concatenate(parts, axis=-1))
        y_ref[...] = y
        yg_ref[...] = _gelu(y).astype(yg_ref.dtype)

    return _pallas(
        body, name=name, grid=(T // Q,),
        in_specs=[_rows(Q, D), _whole(wb.shape), _whole(wc.shape), _whole((ns, R, LANES)), _whole((ns, R, LANES)),
                  _whole((1, D))],
        out_specs=[_rows(Q, D), _rows(Q, D), _slabs(ns, QR), _slabs(ns, QR)],
        out_shape=[jax.ShapeDtypeStruct((T, D), F32), jax.ShapeDtypeStruct((T, D), MXU_DTYPE),
                   jax.ShapeDtypeStruct((ns, T * ncl, LANES), F32), jax.ShapeDtypeStruct((ns, T * ncl, LANES), F32)],
        scratch_shapes=[pltpu.VMEM((ns, QR, LANES), F32), pltpu.VMEM((ns, QR, LANES), F32),
                        pltpu.VMEM((ns, R, LANES), F32), pltpu.VMEM((ns, R, LANES), F32)],
        compiler_params=_params(("arbitrary",)),
    )(xi, wb, wc, a_r, a_i, d_row)


def _s5_bwd(dy, xi, h_r, h_i, wb, wc, a_r, a_i, d_row, B, *, name):
    T, D = dy.shape
    ncl = wb.shape[0]
    cs = wb.shape[2] // 2
    ns = cs // LANES
    R = B * ncl
    Q = S5_CHUNK
    nsteps = Q // B
    nchunk = T // Q
    QR = Q * ncl

    def rev(i):
        return nchunk - 1 - i

    def body(dy_ref, x_ref, hr_hbm, hi_hbm, pr_ref, pi_ref, wb_ref, wc_ref, ar_ref, ai_ref, d_ref,
             du_ref, gr_ref, gi_ref, dar_ref, dai_ref, dd_ref, dhr, dhi, cr, ci, hr_buf, hi_buf, sem):
        i = pl.program_id(0)

        def fetch(step, slot):
            rows = pl.ds(pl.multiple_of(rev(step) * QR, QR), QR)
            return (pltpu.make_async_copy(hr_hbm.at[:, rows, :], hr_buf.at[slot], sem.at[0, slot]),
                    pltpu.make_async_copy(hi_hbm.at[:, rows, :], hi_buf.at[slot], sem.at[1, slot]))

        @pl.when(i == 0)
        def _():
            cr[...] = jnp.zeros_like(cr)
            ci[...] = jnp.zeros_like(ci)
            dar_ref[...] = jnp.zeros_like(dar_ref)
            dai_ref[...] = jnp.zeros_like(dai_ref)
            dd_ref[...] = jnp.zeros_like(dd_ref)
            for s in range(DEEP_BUFFERS - 1):
                for cp in fetch(s, s):
                    cp.start()

        ahead = i + DEEP_BUFFERS - 1

        @pl.when(ahead < nchunk)
        def _():
            for cp in fetch(ahead, ahead % DEEP_BUFFERS):
                cp.start()

        slot = i % DEEP_BUFFERS
        for cp in fetch(i, slot):
            cp.wait()
        hr_ref, hi_ref = hr_buf.at[slot], hi_buf.at[slot]

        dyv = dy_ref[...]
        dyb = dyv.astype(MXU_DTYPE)
        for c in range(ncl):
            dh = lax.dot_general(dyb[:, c * CLUSTER_W:(c + 1) * CLUSTER_W], wc_ref[c],
                                 (((1,), (1,)), ((), ())), preferred_element_type=F32)
            _slab_put(dhr, c, Q, ncl, dh[:, :cs])
            _slab_put(dhi, c, Q, ncl, dh[:, cs:])
        ar = ar_ref[...]
        ai = ai_ref[...]

        def step(j, carry):
            gr, gi = carry
            k = nsteps - 1 - j
            sl = pl.ds(pl.multiple_of(k * R, R), R)
            ngr = dhr[:, sl, :] + ar * gr + ai * gi
            ngi = dhi[:, sl, :] - ai * gr + ar * gi
            gr_ref[:, sl, :] = ngr
            gi_ref[:, sl, :] = ngi
            return ngr, ngi

        gr, gi = lax.fori_loop(0, nsteps, step, (cr[...], ci[...]), unroll=4)
        cr[...] = gr
        ci[...] = gi
        keep = jnp.where(i == nchunk - 1, 0.0, 1.0)
        hpr = jnp.concatenate([pr_ref[:, 8 - R:8, :] * keep, hr_ref[:, 0:QR - R, :]], axis=1)
        hpi = jnp.concatenate([pi_ref[:, 8 - R:8, :] * keep, hi_ref[:, 0:QR - R, :]], axis=1)
        gra, gia = gr_ref[...], gi_ref[...]
        steps = lambda t: jnp.sum(t.reshape(ns, nsteps, R, LANES), axis=1)
        dar_ref[...] += steps(gra * hpr + gia * hpi)
        dai_ref[...] += steps(gia * hpr - gra * hpi)
        parts = []
        for c in range(ncl):
            grc = _slab_get(gr_ref, c, Q, ncl).astype(MXU_DTYPE)
            gic = _slab_get(gi_ref, c, Q, ncl).astype(MXU_DTYPE)
            parts.append(lax.dot_general(grc, wb_ref[c, :, :cs], (((1,), (1,)), ((), ())), preferred_element_type=F32)
                         + lax.dot_general(gic, wb_ref[c, :, cs:], (((1,), (1,)), ((), ())), preferred_element_type=F32))
        du_ref[...] = d_ref[...] * dyv + (parts[0] if ncl == 1 else jnp.concatenate(parts, axis=-1))
        dd_ref[...] += jnp.sum(dyv * x_ref[...], axis=0, keepdims=True)

    tok = pl.BlockSpec((Q, D), lambda i: (rev(i), 0))
    st = pl.BlockSpec((ns, QR, LANES), lambda i: (0, rev(i), 0))
    st_in = pl.BlockSpec(memory_space=pl.ANY)
    before = pl.BlockSpec((ns, 8, LANES), lambda i: (0, jnp.maximum(rev(i) * (QR // 8) - 1, 0), 0))
    acc = _whole((ns, R, LANES))
    return _pallas(
        body, name=name, grid=(nchunk,),
        in_specs=[tok, tok, st_in, st_in, before, before, _whole(wb.shape), _whole(wc.shape), acc, acc, _whole((1, D))],
        out_specs=[tok, st, st, acc, acc, _whole((1, D))],
        out_shape=[jax.ShapeDtypeStruct((T, D), F32),
                   jax.ShapeDtypeStruct((ns, T * ncl, LANES), F32), jax.ShapeDtypeStruct((ns, T * ncl, LANES), F32),
                   jax.ShapeDtypeStruct((ns, R, LANES), F32), jax.ShapeDtypeStruct((ns, R, LANES), F32),
                   jax.ShapeDtypeStruct((1, D), F32)],
        scratch_shapes=([pltpu.VMEM((ns, QR, LANES), F32)] * 2 + [pltpu.VMEM((ns, R, LANES), F32)] * 2
                        + [pltpu.VMEM((DEEP_BUFFERS, ns, QR, LANES), F32)] * 2
                        + [pltpu.SemaphoreType.DMA((2, DEEP_BUFFERS))]),
        compiler_params=_params(("arbitrary",)),
    )(dy, xi, h_r, h_i, h_r, h_i, wb, wc, a_r, a_i, d_row)


def _cluster_tn(tok, st, ncl, *, tok_left, name, after=None):
    T = tok.shape[0]
    ns = st.shape[0]
    cs = ns * LANES
    tt = _pick(T, (512, 256, 128))
    nt = T // tt
    oshape = (ncl, CLUSTER_W, cs) if tok_left else (ncl, cs, CLUSTER_W)

    def body(tok_ref, st_ref, *rest):
        o_ref, acc = rest[-2:]
        t = pl.program_id(0)

        @pl.when(t == 0)
        def _():
            acc[...] = jnp.zeros_like(acc)

        tk = tok_ref[...].astype(MXU_DTYPE)
        for c in range(ncl):
            tc = tk[:, c * CLUSTER_W:(c + 1) * CLUSTER_W]
            sc = _slab_get(st_ref, c, tt, ncl).astype(MXU_DTYPE)
            lhs, rhs = (tc, sc) if tok_left else (sc, tc)
            acc[c] += lax.dot_general(lhs, rhs, (((0,), (0,)), ((), ())), preferred_element_type=F32)

        @pl.when(t == nt - 1)
        def _():
            o_ref[...] = acc[...]

    return _pallas(
        body, name=name, grid=(nt,),
        in_specs=[_rows(tt, tok.shape[1]), _slabs(ns, tt * ncl)] + ([] if after is None else [_ANY]),
        out_specs=_whole(oshape),
        out_shape=jax.ShapeDtypeStruct(oshape, F32),
        scratch_shapes=[pltpu.VMEM(oshape, F32)],
        compiler_params=_params(("arbitrary",)),
    )(tok, st, *([] if after is None else [after]))


def _s5_discretize(lam_re, lam_im, log_dt, b_re, b_im):
    dt = jnp.exp(log_dt)[:, None]
    mag = jnp.exp(lam_re * dt)
    ab_r, ab_i = mag * jnp.cos(lam_im * dt), mag * jnp.sin(lam_im * dt)
    den = lam_re * lam_re + lam_im * lam_im
    nr = ab_r - 1.0
    co_r = (nr * lam_re + ab_i * lam_im) / den
    co_i = (ab_i * lam_re - nr * lam_im) / den
    bb_r = co_r[..., None] * b_re - co_i[..., None] * b_im
    bb_i = co_r[..., None] * b_im + co_i[..., None] * b_re
    return ab_r, ab_i, bb_r, bb_i


def _blockdiag(m):
    G, r, k = m.shape
    ncl = G // GROUPS_PER_CLUSTER
    m4 = m.reshape(ncl, GROUPS_PER_CLUSTER, r, k)
    eye = jnp.eye(GROUPS_PER_CLUSTER, dtype=m.dtype)
    return jnp.einsum('cgrk,gh->cgrhk', m4, eye).reshape(ncl, GROUPS_PER_CLUSTER * r, GROUPS_PER_CLUSTER * k)


def _unblockdiag(m, r, k):
    ncl = m.shape[0]
    m5 = m.reshape(ncl, GROUPS_PER_CLUSTER, r, GROUPS_PER_CLUSTER, k)
    eye = jnp.eye(GROUPS_PER_CLUSTER, dtype=m.dtype)
    return jnp.einsum('cgrhk,gh->cgrk', m5, eye).reshape(ncl * GROUPS_PER_CLUSTER, r, k)


def _t5_bucket(dist):
    exact = REL_BUCKETS // 2
    d = np.maximum(dist, 1).astype(np.float32)
    large = exact + (np.log(d / exact) / math.log(REL_MAX_DIST / exact) * (REL_BUCKETS - exact)).astype(np.int64)
    large = np.minimum(large, REL_BUCKETS - 1)
    return np.where(dist < exact, dist, large).astype(np.int32)


def _band_tables(dil):
    steps = np.arange(BAND)[:, None] + BAND - np.arange(2 * BAND)[None, :]
    bucket = _t5_bucket(np.maximum(steps, 0) * dil)
    in_band = (steps >= 0) & (steps <= BAND)
    return bucket, in_band


def _attn_bias(rel_bias, hpg):
    out = []
    for g, dil in enumerate(DILATIONS):
        bucket, in_band = _band_tables(dil)
        cols = rel_bias[:, g * hpg:(g + 1) * hpg].astype(F32)
        onehot = jnp.asarray((bucket.reshape(-1, 1) == np.arange(REL_BUCKETS)[None, :]).astype(np.float32))
        bias = jnp.dot(onehot, cols, precision=lax.Precision.HIGHEST).T.reshape(hpg, BAND, 2 * BAND)
        out.append(jnp.where(jnp.asarray(in_band)[None], bias, NEG_BIG))
    return jnp.concatenate(out, axis=0)


def _attn_blocks(dil, L):
    M = L // dil
    return M, M // BAND


def _row_sel(r, M, dil):
    return pl.ds(r, M) if dil == 1 else pl.ds(r, M, stride=dil)


def _attn_fwd(q, kv, bias, L, hpg, *, name):
    T = q.shape[0]
    nb_ = T // L
    HP = hpg // 2
    W3 = 3 * hpg * HEAD_DIM
    mmax = L

    def group_body(dil, q_ref, k_ref, v_ref, b_ref, o_ref, l_ref, os, ls):
        M, NB = _attn_blocks(dil, L)
        first = lax.broadcasted_iota(jnp.int32, (1, 2 * HEAD_DIM), 1) < HEAD_DIM
        loaded = {}

        def load(r):
            rows = _row_sel(r, M, dil)
            qf = q_ref[rows, :] * 0.125
            qm = [jnp.where(first, qf, 0.0).astype(MXU_DTYPE), jnp.where(first, 0.0, qf).astype(MXU_DTYPE)]
            kr = k_ref[rows, :].astype(MXU_DTYPE)
            va = jnp.concatenate([v_ref[rows, :].astype(MXU_DTYPE), jnp.ones((M, 2 * HEAD_DIM), MXU_DTYPE)], axis=-1)
            return qm, kr, va

        tasks = [(r, n) for r in range(dil) for n in range(NB)]
        nbatch = ATTN_BATCH[NB == 1]
        for t0 in range(0, len(tasks), nbatch):
            batch = tasks[t0:t0 + nbatch]
            for r, _ in batch:
                if r not in loaded:
                    loaded[r] = load(r)
            chains = [(r, n, hh) for r, n in batch for hh in range(2)]
            ks = lambda n: slice(0, BAND) if n == 0 else slice((n - 1) * BAND, (n + 1) * BAND)
            s = [lax.dot_general(loaded[r][0][hh][n * BAND:(n + 1) * BAND, :], loaded[r][1][ks(n), :],
                                 (((1,), (1,)), ((), ())), preferred_element_type=F32)
                 + (b_ref[hh, :, BAND:] if n == 0 else b_ref[hh]) for r, n, hh in chains]
            m = [jnp.max(t, axis=-1, keepdims=True) for t in s]
            p = [jnp.exp(t - mm) for t, mm in zip(s, m)]
            pv = [jnp.dot(t.astype(MXU_DTYPE), loaded[r][2][ks(n), :], preferred_element_type=F32)
                  for t, (r, n, hh) in zip(p, chains)]
            l = [t[:, 2 * HEAD_DIM:] for t in pv]
            o_h = [t[:, :2 * HEAD_DIM] / ll for t, ll in zip(pv, l)]
            l_h = [mm + jnp.log(ll) for mm, ll in zip(m, l)]
            for i, (r, n) in enumerate(batch):
                os[r * M + n * BAND:r * M + (n + 1) * BAND, :] = jnp.where(first, o_h[2 * i], o_h[2 * i + 1])
                ls[r * M + n * BAND:r * M + (n + 1) * BAND, :] = jnp.where(first, l_h[2 * i], l_h[2 * i + 1])
                if n == NB - 1:
                    rows = _row_sel(r, M, dil)
                    o_ref[rows, :] = os[r * M:(r + 1) * M, :]
                    l_ref[rows, :] = ls[r * M:(r + 1) * M, :]

    def body(q_ref, k_ref, v_ref, b_ref, o_ref, l_ref, os, ls):
        g = pl.program_id(0)
        for gi, dil in enumerate(DILATIONS):
            pl.when(g == gi)(functools.partial(group_body, dil, q_ref, k_ref, v_ref, b_ref, o_ref, l_ref, os, ls))

    blk = (L, 2 * HEAD_DIM)
    return _pallas(
        body, name=name, grid=(3, nb_, HP),
        in_specs=[pl.BlockSpec(blk, lambda g, b, h: (b, g * HP + h)),
                  pl.BlockSpec(blk, lambda g, b, h: (b, g * HP + h)),
                  pl.BlockSpec(blk, lambda g, b, h: (b, 3 * HP + g * HP + h)),
                  pl.BlockSpec((2, BAND, 2 * BAND), lambda g, b, h: (g * HP + h, 0, 0))],
        out_specs=[pl.BlockSpec(blk, lambda g, b, h: (b, g * HP + h)),
                   pl.BlockSpec(blk, lambda g, b, h: (b, g * HP + h))],
        out_shape=[jax.ShapeDtypeStruct((T, W3), F32), jax.ShapeDtypeStruct((T, W3), F32)],
        scratch_shapes=[pltpu.VMEM((mmax, 2 * HEAD_DIM), F32), pltpu.VMEM((mmax, 2 * HEAD_DIM), F32)],
        compiler_params=_params(("arbitrary", "arbitrary", "arbitrary")),
    )(q, kv, kv, bias)


def _attn_merge(o3, l3, hw, *, name):
    T = o3.shape[0]
    tm = _pick(T, (256, 128))

    def body(o0, o1, o2, l0, l1, l2, o_ref, ob_ref, lse_ref):
        a0, a1, a2 = l0[...], l1[...], l2[...]
        m = jnp.maximum(jnp.maximum(a0, a1), a2)
        e0, e1, e2 = jnp.exp(a0 - m), jnp.exp(a1 - m), jnp.exp(a2 - m)
        z = e0 + e1 + e2
        o = (e0 * o0[...] + e1 * o1[...] + e2 * o2[...]) / z
        o_ref[...] = o
        ob_ref[...] = o.astype(ob_ref.dtype)
        lse_ref[...] = m + jnp.log(z)

    def col(g):
        return pl.BlockSpec((tm, hw), lambda i: (i, g))

    return _pallas(
        body, name=name, grid=(T // tm,),
        in_specs=[col(0), col(1), col(2), col(0), col(1), col(2)],
        out_specs=[_rows(tm, hw)] * 3,
        out_shape=[jax.ShapeDtypeStruct((T, hw), F32), jax.ShapeDtypeStruct((T, hw), MXU_DTYPE),
                   jax.ShapeDtypeStruct((T, hw), F32)],
        compiler_params=_params(("parallel",)),
    )(o3, o3, o3, l3, l3, l3)


def _attn_bwd(q, kv, do, o, lse, bias, L, hpg, *, name):
    T = q.shape[0]
    nb_ = T // L
    HP = hpg // 2
    W3 = 3 * hpg * HEAD_DIM
    mmax = L

    def group_body(dil, q_ref, k_ref, v_ref, do_ref, o_ref, l_ref, b_ref, dq_ref, dk_ref, dv_ref, ds_ref,
                   dqs, dks, dvs):
        M, NB = _attn_blocks(dil, L)
        first = lax.broadcasted_iota(jnp.int32, (1, 2 * HEAD_DIM), 1) < HEAD_DIM
        loaded = {}

        def load(r):
            rows = _row_sel(r, M, dil)
            qf = q_ref[rows, :] * 0.125
            qm = [jnp.where(first, qf, 0.0).astype(MXU_DTYPE), jnp.where(first, 0.0, qf).astype(MXU_DTYPE)]
            kr = k_ref[rows, :].astype(MXU_DTYPE)
            vr = v_ref[rows, :].astype(MXU_DTYPE)
            dof = do_ref[rows, :]
            dom = [jnp.where(first, dof, 0.0).astype(MXU_DTYPE), jnp.where(first, 0.0, dof).astype(MXU_DTYPE)]
            dod = dof * o_ref[rows, :]
            delta = [jnp.sum(jnp.where(first, dod, 0.0), axis=-1, keepdims=True),
                     jnp.sum(jnp.where(first, 0.0, dod), axis=-1, keepdims=True)]
            lr = l_ref[rows, :]
            lse = [lr[:, 0:1], lr[:, HEAD_DIM:HEAD_DIM + 1]]
            if NB > 1:
                dks[r * M:(r + 1) * M, :] = jnp.zeros((M, 2 * HEAD_DIM), F32)
                dvs[r * M:(r + 1) * M, :] = jnp.zeros((M, 2 * HEAD_DIM), F32)
            return qm, kr, vr, dom, delta, lse

        nt = (((1,), (1,)), ((), ()))
        tn = (((0,), (0,)), ((), ()))
        tasks = [(r, n) for r in range(dil) for n in range(NB)]
        nbatch = ATTN_BATCH[NB == 1]
        for t0 in range(0, len(tasks), nbatch):
            batch = tasks[t0:t0 + nbatch]
            for r, _ in batch:
                if r not in loaded:
                    loaded[r] = load(r)
            chains = [(r, n, hh) for r, n in batch for hh in range(2)]
            qs = lambda n: slice(n * BAND, (n + 1) * BAND)
            ks = lambda n: slice(0, BAND) if n == 0 else slice((n - 1) * BAND, (n + 1) * BAND)
            qb = [loaded[r][0][hh][qs(n), :] for r, n, hh in chains]
            dob = [loaded[r][3][hh][qs(n), :] for r, n, hh in chains]
            kb = [loaded[r][1][ks(n), :] for r, n, hh in chains]
            s = [lax.dot_general(a, b, nt, preferred_element_type=F32) + (b_ref[hh, :, BAND:] if n == 0 else b_ref[hh])
                 for a, b, (r, n, hh) in zip(qb, kb, chains)]
            dp = [lax.dot_general(a, loaded[r][2][ks(n), :], nt, preferred_element_type=F32)
                  for a, (r, n, hh) in zip(dob, chains)]
            p = [jnp.exp(t - loaded[r][5][hh][qs(n), :]) for t, (r, n, hh) in zip(s, chains)]
            ds = [a * (b - loaded[r][4][hh][qs(n), :]) for a, b, (r, n, hh) in zip(p, dp, chains)]
            for t, (r, n, hh) in zip(ds, chains):
                if n == 0:
                    ds_ref[hh, :, BAND:] += t
                else:
                    ds_ref[hh] += t
            dsm = [t.astype(MXU_DTYPE) for t in ds]
            dq = [jnp.dot(a, b, preferred_element_type=F32) for a, b in zip(dsm, kb)]
            dk = [lax.dot_general(a, b, tn, preferred_element_type=F32) for a, b in zip(dsm, qb)]
            dv = [lax.dot_general(a.astype(MXU_DTYPE), b, tn, preferred_element_type=F32) for a, b in zip(p, dob)]
            for i, (r, n) in enumerate(batch):
                dqs[r * M + n * BAND:r * M + (n + 1) * BAND, :] = jnp.where(first, dq[2 * i], dq[2 * i + 1]) * 0.125
                ksm = slice(r * M + ks(n).start, r * M + ks(n).stop)
                if NB > 1:
                    dks[ksm, :] += dk[2 * i] + dk[2 * i + 1]
                    dvs[ksm, :] += dv[2 * i] + dv[2 * i + 1]
                else:
                    dks[ksm, :] = dk[2 * i] + dk[2 * i + 1]
                    dvs[ksm, :] = dv[2 * i] + dv[2 * i + 1]
                if n == NB - 1:
                    rows = _row_sel(r, M, dil)
                    dq_ref[rows, :] = dqs[r * M:(r + 1) * M, :]
                    dk_ref[rows, :] = dks[r * M:(r + 1) * M, :]
                    dv_ref[rows, :] = dvs[r * M:(r + 1) * M, :]

    def body(q_ref, k_ref, v_ref, do_ref, o_ref, l_ref, b_ref, dq_ref, dk_ref, dv_ref, ds_ref, dqs, dks, dvs):
        g = pl.program_id(0)

        @pl.when(pl.program_id(2) == 0)
        def _():
            ds_ref[...] = jnp.zeros_like(ds_ref)

        for gi, dil in enumerate(DILATIONS):
            pl.when(g == gi)(functools.partial(group_body, dil, q_ref, k_ref, v_ref, do_ref, o_ref, l_ref, b_ref,
                                               dq_ref, dk_ref, dv_ref, ds_ref, dqs, dks, dvs))

    blk = (L, 2 * HEAD_DIM)
    gcol = lambda g, h, b: (b, g * HP + h)
    hcol = lambda g, h, b: (b, h)
    return _pallas(
        body, name=name, grid=(3, HP, nb_),
        in_specs=[pl.BlockSpec(blk, gcol), pl.BlockSpec(blk, gcol),
                  pl.BlockSpec(blk, lambda g, h, b: (b, 3 * HP + g * HP + h)),
                  pl.BlockSpec(blk, hcol), pl.BlockSpec(blk, hcol), pl.BlockSpec(blk, hcol),
                  pl.BlockSpec((2, BAND, 2 * BAND), lambda g, h, b: (g * HP + h, 0, 0))],
        out_specs=[pl.BlockSpec(blk, gcol), pl.BlockSpec(blk, gcol), pl.BlockSpec(blk, gcol),
                   pl.BlockSpec((2, BAND, 2 * BAND), lambda g, h, b: (g * HP + h, 0, 0))],
        out_shape=[jax.ShapeDtypeStruct((T, W3), F32), jax.ShapeDtypeStruct((T, W3), F32),
                   jax.ShapeDtypeStruct((T, W3), F32), jax.ShapeDtypeStruct((3 * hpg, BAND, 2 * BAND), F32)],
        scratch_shapes=[pltpu.VMEM((mmax, 2 * HEAD_DIM), F32)] * 3,
        compiler_params=_params(("arbitrary", "arbitrary", "arbitrary")),
    )(q, kv, kv, do, o, lse, bias)


def _bias_grad(ds_sum, hpg, *, name):
    nh = ds_sum.shape[0]
    idx = np.stack([np.where(_band_tables(dil)[1], _band_tables(dil)[0], -1) for dil in DILATIONS]).astype(np.int32)

    def body(ds_ref, idx_ref, o_ref):
        d = ds_ref[...]
        ix = idx_ref[...]
        lane = lax.broadcasted_iota(jnp.int32, (8, 128), 1)
        row = jnp.zeros((8, 128), F32)
        for b in range(REL_BUCKETS):
            row = row + jnp.where(lane == b, jnp.sum(jnp.where(ix == b, d, 0.0)), 0.0)
        o_ref[...] = row

    out = _pallas(
        body, name=name, grid=(nh,),
        in_specs=[pl.BlockSpec((None, BAND, 2 * BAND), lambda h: (h, 0, 0)),
                  pl.BlockSpec((None, BAND, 2 * BAND), lambda h: (h // hpg, 0, 0))],
        out_specs=pl.BlockSpec((None, 8, 128), lambda h: (h, 0, 0)),
        out_shape=jax.ShapeDtypeStruct((nh, 8, 128), F32),
        compiler_params=_params(("parallel",)),
    )(ds_sum, jnp.asarray(idx))
    return out[:, 0, :REL_BUCKETS].T


def _adamw(w, g, m, v, *, name):
    Rw, C = w.shape
    tm = _pick(Rw, (512, 352, 256, 128, 64, 32, 16, 8))

    def body(w_ref, g_ref, m_ref, v_ref, d_ref, nm_ref, nv_ref):
        gg = g_ref[...]
        nm = ADAM_B1 * m_ref[...] + (1.0 - ADAM_B1) * gg
        nv = ADAM_B2 * v_ref[...] + (1.0 - ADAM_B2) * (gg * gg)
        m_hat = nm / (1.0 - ADAM_B1 ** ADAM_STEP)
        v_hat = nv / (1.0 - ADAM_B2 ** ADAM_STEP)
        d_ref[...] = -ADAM_LR * (m_hat / (jnp.sqrt(v_hat) + ADAM_EPS) + ADAM_WD * w_ref[...])
        nm_ref[...] = nm
        nv_ref[...] = nv

    return _pallas(
        body, name=name, grid=(Rw // tm,), in_specs=[_rows(tm, C)] * 4, out_specs=[_rows(tm, C)] * 3,
        out_shape=[jax.ShapeDtypeStruct((Rw, C), F32)] * 3, compiler_params=_params(("parallel",)),
    )(w, g, m, v)


ROW_TILE_ELEMS = 256 * 1024


def _tile_rows(r, c):
    best = 8
    for t in range(8, r + 1, 8):
        if r % t == 0 and t * c <= ROW_TILE_ELEMS:
            best = t
    return best


def _adamw_halves(w, m, v, mine, other, cidx, *, layer=0, prev=None, name):
    NL, _, r, c = w.shape
    tm = _tile_rows(r, c)

    def body(c_ref, w_ref, m_ref, v_ref, a_ref, b_ref, *rest):
        g_ref, d_ref, nm_ref, nv_ref = rest[-4:]
        gg = jnp.where(pl.program_id(0) == c_ref[0], a_ref[...], b_ref[...])
        nm = ADAM_B1 * m_ref[...] + (1.0 - ADAM_B1) * gg
        nv = ADAM_B2 * v_ref[...] + (1.0 - ADAM_B2) * (gg * gg)
        m_hat = nm / (1.0 - ADAM_B1 ** ADAM_STEP)
        v_hat = nv / (1.0 - ADAM_B2 ** ADAM_STEP)
        g_ref[...] = gg
        d_ref[...] = -ADAM_LR * (m_hat / (jnp.sqrt(v_hat) + ADAM_EPS) + ADAM_WD * w_ref[...])
        nm_ref[...] = nm
        nv_ref[...] = nv

    half = pl.BlockSpec((None, None, tm, c), lambda h, i, cr: (layer, h, i, 0))
    one = pl.BlockSpec((None, tm, c), lambda h, i, cr: (0, i, 0))
    in_specs = [half, half, half, one, one]
    args = [cidx, w, m, v, mine, other]
    aliases = {}
    if prev is not None:
        in_specs += [_ANY] * 4
        args += list(prev)
        aliases = {6 + k: k for k in range(4)}
    spec = pltpu.PrefetchScalarGridSpec(num_scalar_prefetch=1, grid=(2, r // tm), in_specs=in_specs, out_specs=[half] * 4)
    return _pallas(
        body, name=name, grid_spec=spec, out_shape=[jax.ShapeDtypeStruct((NL, 2, r, c), F32)] * 4,
        input_output_aliases=aliases, compiler_params=_params(("parallel", "parallel")),
    )(*args)


def _pair_sum(g, theirs, cidx, *, cast, name):
    _, _, r, c = g.shape
    tm = _tile_rows(r, c)

    def body(c_ref, g_ref, t_ref, *outs):
        s = g_ref[...] + t_ref[...]
        outs[0][...] = s
        if cast:
            outs[1][...] = s.astype(BF16)

    blk = (None, None, tm, c)
    first = pl.BlockSpec(blk, lambda p, i, cr: (p, 0, i, 0))
    shapes = [jax.ShapeDtypeStruct((4, 1, r, c), F32)] + ([jax.ShapeDtypeStruct((4, 1, r, c), BF16)] if cast else [])
    spec = pltpu.PrefetchScalarGridSpec(
        num_scalar_prefetch=1, grid=(4, r // tm),
        in_specs=[pl.BlockSpec(blk, lambda p, i, cr: (p, cr[0], i, 0)), first], out_specs=[first] * len(shapes))
    return _pallas(body, name=name, grid_spec=spec, out_shape=shapes,
                   compiler_params=_params(("parallel", "parallel")))(cidx, g, theirs)


def _chip_sum(hf, got, chip_idx, *, name):
    _, _, r, c = hf.shape
    tm = _tile_rows(r, c)

    def body(p_ref, h_ref, r_ref, o_ref):
        s = h_ref[...]
        for k in range(3):
            s = s + r_ref[k].astype(F32)
        o_ref[...] = s

    spec = pltpu.PrefetchScalarGridSpec(
        num_scalar_prefetch=1, grid=(r // tm,),
        in_specs=[pl.BlockSpec((None, None, tm, c), lambda i, pr: (pr[0], 0, i, 0)),
                  pl.BlockSpec((3, None, tm, c), lambda i, pr: (0, 0, i, 0))],
        out_specs=pl.BlockSpec((None, tm, c), lambda i, pr: (0, i, 0)))
    return _pallas(body, name=name, grid_spec=spec, out_shape=jax.ShapeDtypeStruct((1, r, c), F32),
                   compiler_params=_params(("parallel",)))(chip_idx, hf, got)


def _place():
    x, y, c = lax.axis_index("x"), lax.axis_index("y"), lax.axis_index("c")
    chips = [(1 - x, y), (x, 1 - y), (1 - x, 1 - y)]
    return x, y, c, chips


_ANY = pl.BlockSpec(memory_space=pl.ANY)


def _comm_call(body, ins, out_shapes, n_remote, *, name, aliases=None):
    sems = [pltpu.SemaphoreType.DMA((n,)) for n in n_remote]
    return _pallas(
        body, name=name, in_specs=[_ANY] * len(ins), out_specs=[_ANY] * len(out_shapes), out_shape=out_shapes,
        scratch_shapes=sems, input_output_aliases=aliases or {},
        compiler_params=pltpu.CompilerParams(has_side_effects=True),
    )(*ins)


_HBM_SPEC = pl.BlockSpec(memory_space=pltpu.HBM)
_SEM_SPEC = pl.BlockSpec(memory_space=pltpu.SEMAPHORE)
_DATAFLOW = pltpu.SideEffectType.DATAFLOW_SIDE_EFFECTING


def _in_hbm(a):
    return pltpu.with_memory_space_constraint(a, pltpu.HBM)


def _gather_start(groups, *, name):
    flat = [s for g in groups for s in g]
    n, ng = len(flat), len(groups)

    def body(*refs):
        ins, lands = refs[:n], refs[n:2 * n]
        sems = refs[2 * n:2 * n + 2 * ng]
        token = refs[-1]
        x, y, c, chips = _place()
        me = 2 * x + y
        a = 0
        for gi, g in enumerate(groups):
            for j in range(len(g)):
                for k, (tx, ty) in enumerate(chips):
                    _rcopy(ins[a].at[c], lands[a].at[me, c], sems[2 * gi].at[3 * j + k], sems[2 * gi + 1].at[3 * j + k],
                           (tx, ty, c)).start()
                a += 1
        token[...] = jnp.zeros_like(token)

    land_shapes = [(4,) + s.shape for s in flat]
    out_shape = ([pltpu.SemaphoreType.DMA((3 * len(g),)) for g in groups for _ in range(2)]
                 + [pltpu.HBM(s.shape, s.dtype) for s in flat]
                 + [pltpu.HBM(ls, s.dtype) for ls, s in zip(land_shapes, flat)]
                 + [jax.ShapeDtypeStruct((8, 128), F32)])
    outs = _pallas(
        body, name=name, in_specs=[_HBM_SPEC] * (2 * n),
        out_specs=[_SEM_SPEC] * (2 * ng) + [_HBM_SPEC] * (2 * n) + [pl.BlockSpec(memory_space=pltpu.VMEM)],
        out_shape=out_shape, input_output_aliases={i: 2 * ng + i for i in range(2 * n)},
        compiler_params=pltpu.CompilerParams(has_side_effects=_DATAFLOW),
    )(*[_in_hbm(s) for s in flat], *[_in_hbm(lax.empty(ls, s.dtype)) for ls, s in zip(land_shapes, flat)])
    sems, thru, lands, token = outs[:2 * ng], outs[2 * ng:2 * ng + n], outs[2 * ng + n:2 * ng + 2 * n], outs[-1]
    res, a = [], 0
    for gi, g in enumerate(groups):
        res.append((sems[2 * gi], sems[2 * gi + 1], thru[a:a + len(g)], lands[a:a + len(g)]))
        a += len(g)
    return res, token


def _gather_wait(ssem, rsem, shards, lands, after, *, name):
    m = len(shards)

    def body(*refs):
        ins, lnd = refs[:m], refs[m:2 * m]
        ss, rs = refs[2 * m], refs[2 * m + 1]
        x, y, c, chips = _place()
        for j in range(m):
            for k, (tx, ty) in enumerate(chips):
                cp = _rcopy(ins[j].at[c], lnd[j].at[2 * tx + ty, c], ss.at[3 * j + k], rs.at[3 * j + k], (tx, ty, c))
                cp.wait_send()
                cp.wait_recv()

    outs = _pallas(
        body, name=name, in_specs=[_HBM_SPEC] * (2 * m) + [_SEM_SPEC, _SEM_SPEC, _ANY],
        out_specs=[_HBM_SPEC] * (2 * m),
        out_shape=[pltpu.HBM(s.shape, s.dtype) for s in shards] + [pltpu.HBM(l.shape, l.dtype) for l in lands],
        input_output_aliases={i: i for i in range(2 * m)},
        compiler_params=pltpu.CompilerParams(has_side_effects=_DATAFLOW),
    )(*shards, *lands, ssem, rsem, after)
    return outs[m:]


def _gather_forward(lands, *, name):
    n = len(lands)

    def body(*refs):
        outs = refs[n:2 * n]
        ssem, rsem = refs[2 * n:]
        x, y, c, chips = _place()
        sib = (x, y, 1 - c)
        cps = []
        for a in range(n):
            for k, (tx, ty) in enumerate(chips):
                pk = 2 * tx + ty
                cp = _rcopy(outs[a].at[pk, c], outs[a].at[pk, c], ssem.at[3 * a + k], rsem.at[3 * a + k], sib)
                cp.start()
                cps.append(cp)
        for a in range(n):
            for k, (tx, ty) in enumerate(chips):
                pk = 2 * tx + ty
                _rcopy(outs[a].at[pk, c], outs[a].at[pk, 1 - c], ssem.at[3 * a + k], rsem.at[3 * a + k], sib).wait_recv()
        for cp in cps:
            cp.wait_send()

    shapes = [jax.ShapeDtypeStruct(l.shape, l.dtype) for l in lands]
    return _comm_call(body, lands, shapes, [3 * n, 3 * n], name=name, aliases={i: i for i in range(n)})


def _gather_forward_start(lands, *, name):
    n = len(lands)

    def body(*refs):
        ssem, rsem = refs[n], refs[n + 1]
        outs = refs[n + 2:2 * n + 2]
        token = refs[-1]
        x, y, c, chips = _place()
        for a in range(n):
            for k, (tx, ty) in enumerate(chips):
                pk = 2 * tx + ty
                _rcopy(outs[a].at[pk, c], outs[a].at[pk, c], ssem.at[3 * a + k], rsem.at[3 * a + k], (x, y, 1 - c)).start()
        token[...] = jnp.zeros_like(token)

    outs = _pallas(
        body, name=name, in_specs=[_HBM_SPEC] * n,
        out_specs=[_SEM_SPEC] * 2 + [_HBM_SPEC] * n + [pl.BlockSpec(memory_space=pltpu.VMEM)],
        out_shape=([pltpu.SemaphoreType.DMA((3 * n,))] * 2 + [pltpu.HBM(l.shape, l.dtype) for l in lands]
                   + [jax.ShapeDtypeStruct((8, 128), F32)]),
        input_output_aliases={i: 2 + i for i in range(n)},
        compiler_params=pltpu.CompilerParams(has_side_effects=_DATAFLOW),
    )(*lands)
    return (outs[0], outs[1], outs[2:2 + n]), outs[-1]


def _gather_forward_wait(started, after, *, name):
    ssem, rsem, lands = started
    n = len(lands)

    def body(*refs):
        lnd = refs[:n]
        ss, rs = refs[n], refs[n + 1]
        x, y, c, chips = _place()
        sib = (x, y, 1 - c)
        for a in range(n):
            for k, (tx, ty) in enumerate(chips):
                pk = 2 * tx + ty
                _rcopy(lnd[a].at[pk, c], lnd[a].at[pk, 1 - c], ss.at[3 * a + k], rs.at[3 * a + k], sib).wait_recv()
                _rcopy(lnd[a].at[pk, c], lnd[a].at[pk, c], ss.at[3 * a + k], rs.at[3 * a + k], sib).wait_send()

    return _pallas(
        body, name=name, in_specs=[_HBM_SPEC] * n + [_SEM_SPEC, _SEM_SPEC, _ANY], out_specs=[_HBM_SPEC] * n,
        out_shape=[pltpu.HBM(l.shape, l.dtype) for l in lands], input_output_aliases={i: i for i in range(n)},
        compiler_params=pltpu.CompilerParams(has_side_effects=_DATAFLOW),
    )(*lands, ssem, rsem, after)


class _Lazy:
    def __init__(self, group_of, make, prepare):
        self._group_of, self._make, self._prepare, self._done, self._anchor = group_of, make, prepare, {}, None

    def anchor(self, value):
        self._anchor = value

    def prepare(self, key, value):
        return self._prepare(self._group_of[key], value)

    def __getitem__(self, key):
        g = self._group_of[key]
        if g not in self._done:
            self._done[g] = self._make(g, self._anchor)
        return self._done[g][key]


def _anchor(mapping, value):
    if isinstance(mapping, _Lazy):
        mapping.anchor(value)


def _prepare(mapping, key, value):
    return mapping.prepare(key, value)[0, 0] if isinstance(mapping, _Lazy) else 0.0


def _rcopy(src, dst, ssem, rsem, dev):
    return pltpu.make_async_remote_copy(src_ref=src, dst_ref=dst, send_sem=ssem, recv_sem=rsem,
                                        device_id=dev, device_id_type=MESH)


def _all_gather(shards, *, name):
    n = len(shards)

    def body(*refs):
        ins, outs = refs[:n], refs[n:2 * n]
        s_ici, r_ici, s_d2d, r_d2d = refs[2 * n:]
        x, y, c, chips = _place()
        me = 2 * x + y
        sib = (x, y, 1 - c)
        sends = []
        for a in range(n):
            for k, (tx, ty) in enumerate(chips):
                cp = _rcopy(ins[a].at[c], outs[a].at[me, c], s_ici.at[3 * a + k], r_ici.at[3 * a + k], (tx, ty, c))
                cp.start()
                sends.append(cp)
        for a in range(n):
            for k, (tx, ty) in enumerate(chips):
                pk = 2 * tx + ty
                _rcopy(ins[a].at[c], outs[a].at[pk, c], s_ici.at[3 * a + k], r_ici.at[3 * a + k], (tx, ty, c)).wait_recv()
                fw = _rcopy(outs[a].at[pk, c], outs[a].at[pk, c], s_d2d.at[3 * a + k], r_d2d.at[3 * a + k], sib)
                fw.start()
                sends.append(fw)
        for a in range(n):
            for k, (tx, ty) in enumerate(chips):
                pk = 2 * tx + ty
                _rcopy(ins[a].at[c], outs[a].at[pk, 1 - c], s_d2d.at[3 * a + k], r_d2d.at[3 * a + k], sib).wait_recv()
        for cp in sends:
            cp.wait_send()

    shapes = [jax.ShapeDtypeStruct((4,) + s.shape, s.dtype) for s in shards]
    return _comm_call(body, shards, shapes, [3 * n] * 4, name=name)


def _gather(shards, chip, *, name):
    outs = _all_gather(shards, name=name)
    return [lax.dynamic_update_slice(o, s[None], (chip, 0, 0, 0)) for o, s in zip(outs, shards)]


def _pair_send(gs, *, name):
    n = len(gs)

    def body(*refs):
        ins, theirs = refs[:n], refs[n:2 * n]
        ssem, rsem = refs[2 * n:]
        x, y, c, _ = _place()
        sib = (x, y, 1 - c)
        cps = []
        for a in range(n):
            cp = _rcopy(ins[a].at[:, pl.ds(1 - c, 1)], theirs[a], ssem.at[a], rsem.at[a], sib)
            cp.start()
            cps.append(cp)
        for cp in cps:
            cp.wait_send()
            cp.wait_recv()

    shapes = [jax.ShapeDtypeStruct((4, 1) + g.shape[2:], g.dtype) for g in gs]
    return _comm_call(body, gs, shapes, [n, n], name=name)


def _chip_exchange(hx, *, name):
    n = len(hx)

    def body(*refs):
        hxr, got = refs[:n], refs[n:2 * n]
        ssem, rsem = refs[2 * n:]
        x, y, c, chips = _place()
        cps = []
        for a in range(n):
            for k, (tx, ty) in enumerate(chips):
                cp = _rcopy(hxr[a].at[2 * tx + ty], got[a].at[k], ssem.at[3 * a + k], rsem.at[3 * a + k], (tx, ty, c))
                cp.start()
                cps.append(cp)
        for cp in cps:
            cp.wait_send()
            cp.wait_recv()

    shapes = [jax.ShapeDtypeStruct((3,) + h.shape[1:], h.dtype) for h in hx]
    return _comm_call(body, hx, shapes, [3 * n, 3 * n], name=name)


def _pair_swap(fs, *, name):
    n = len(fs)

    def body(*refs):
        ins, outs = refs[:n], refs[n:2 * n]
        ssem, rsem = refs[2 * n:]
        x, y, c, _ = _place()
        cps = []
        for a in range(n):
            cp = _rcopy(ins[a], outs[a], ssem.at[a], rsem.at[a], (x, y, 1 - c))
            cp.start()
            cps.append(cp)
        for cp in cps:
            cp.wait_send()
            cp.wait_recv()

    shapes = [jax.ShapeDtypeStruct(f.shape, f.dtype) for f in fs]
    return _comm_call(body, fs, shapes, [n, n], name=name)


def _chip_exchange_start(hx, *, name):
    n = len(hx)

    def body(*refs):
        ins, gots = refs[:n], refs[n:2 * n]
        ssem, rsem = refs[2 * n], refs[2 * n + 1]
        token = refs[-1]
        x, y, c, chips = _place()
        for a in range(n):
            for k, (tx, ty) in enumerate(chips):
                _rcopy(ins[a].at[2 * tx + ty], gots[a].at[k], ssem.at[3 * a + k], rsem.at[3 * a + k], (tx, ty, c)).start()
        token[...] = jnp.zeros_like(token)

    got_shapes = [(3,) + h.shape[1:] for h in hx]
    outs = _pallas(
        body, name=name, in_specs=[_HBM_SPEC] * (2 * n),
        out_specs=[_SEM_SPEC] * 2 + [_HBM_SPEC] * (2 * n) + [pl.BlockSpec(memory_space=pltpu.VMEM)],
        out_shape=([pltpu.SemaphoreType.DMA((3 * n,))] * 2 + [pltpu.HBM(h.shape, h.dtype) for h in hx]
                   + [pltpu.HBM(gs, h.dtype) for gs, h in zip(got_shapes, hx)] + [jax.ShapeDtypeStruct((8, 128), F32)]),
        input_output_aliases={i: 2 + i for i in range(2 * n)},
        compiler_params=pltpu.CompilerParams(has_side_effects=_DATAFLOW),
    )(*[_in_hbm(h) for h in hx], *[_in_hbm(lax.empty(gs, h.dtype)) for gs, h in zip(got_shapes, hx)])
    return (outs[0], outs[1], outs[2:2 + n], outs[2 + n:2 + 2 * n]), outs[-1]


def _chip_exchange_wait(started, after, *, name):
    ssem, rsem, hx, gots = started
    n = len(hx)

    def body(*refs):
        ins, gts = refs[:n], refs[n:2 * n]
        ss, rs = refs[2 * n], refs[2 * n + 1]
        x, y, c, chips = _place()
        for a in range(n):
            for k, (tx, ty) in enumerate(chips):
                cp = _rcopy(ins[a].at[2 * tx + ty], gts[a].at[k], ss.at[3 * a + k], rs.at[3 * a + k], (tx, ty, c))
                cp.wait_send()
                cp.wait_recv()

    outs = _pallas(
        body, name=name, in_specs=[_HBM_SPEC] * (2 * n) + [_SEM_SPEC, _SEM_SPEC, _ANY],
        out_specs=[_HBM_SPEC] * (2 * n),
        out_shape=[pltpu.HBM(h.shape, h.dtype) for h in hx] + [pltpu.HBM(g.shape, g.dtype) for g in gots],
        input_output_aliases={i: i for i in range(2 * n)},
        compiler_params=pltpu.CompilerParams(has_side_effects=_DATAFLOW),
    )(*hx, *gots, ssem, rsem, after)
    return outs[n:]


def _sent_part(ref, c, whole):
    return ref if whole else ref.at[:, pl.ds(1 - c, 1)]


def _pair_send_start(gs, *, name, whole=False, after=None):
    n = len(gs)
    afters = [] if after is None else [after]

    def body(*refs):
        ins, lands = refs[:n], refs[n:2 * n]
        ssem, rsem = refs[2 * n + len(afters)], refs[2 * n + len(afters) + 1]
        token = refs[-1]
        x, y, c, _ = _place()
        for a in range(n):
            _rcopy(_sent_part(ins[a], c, whole), lands[a], ssem.at[a], rsem.at[a], (x, y, 1 - c)).start()
        token[...] = jnp.zeros_like(token)

    land_shapes = [g.shape if whole else (4, 1) + g.shape[2:] for g in gs]
    outs = _pallas(
        body, name=name, in_specs=[_HBM_SPEC] * (2 * n) + [_ANY] * len(afters),
        out_specs=[_SEM_SPEC] * 2 + [_HBM_SPEC] * (2 * n) + [pl.BlockSpec(memory_space=pltpu.VMEM)],
        out_shape=([pltpu.SemaphoreType.DMA((n,))] * 2 + [pltpu.HBM(g.shape, g.dtype) for g in gs]
                   + [pltpu.HBM(ls, g.dtype) for ls, g in zip(land_shapes, gs)] + [jax.ShapeDtypeStruct((8, 128), F32)]),
        input_output_aliases={i: 2 + i for i in range(2 * n)},
        compiler_params=pltpu.CompilerParams(has_side_effects=_DATAFLOW),
    )(*[_in_hbm(g) for g in gs], *[_in_hbm(lax.empty(ls, g.dtype)) for ls, g in zip(land_shapes, gs)], *afters)
    return (outs[0], outs[1], outs[2:2 + n], outs[2 + n:2 + 2 * n]), outs[-1]


def _pair_send_wait(started, after, *, name, whole=False):
    ssem, rsem, gs, lands = started
    n = len(gs)
    afters = list(after) if isinstance(after, (list, tuple)) else [after]

    def body(*refs):
        ins, lnd = refs[:n], refs[n:2 * n]
        ss, rs = refs[2 * n], refs[2 * n + 1]
        x, y, c, _ = _place()
        for a in range(n):
            cp = _rcopy(_sent_part(ins[a], c, whole), lnd[a], ss.at[a], rs.at[a], (x, y, 1 - c))
            cp.wait_send()
            cp.wait_recv()

    outs = _pallas(
        body, name=name, in_specs=[_HBM_SPEC] * (2 * n) + [_SEM_SPEC, _SEM_SPEC] + [_ANY] * len(afters),
        out_specs=[_HBM_SPEC] * (2 * n),
        out_shape=[pltpu.HBM(g.shape, g.dtype) for g in gs] + [pltpu.HBM(l.shape, l.dtype) for l in lands],
        input_output_aliases={i: i for i in range(2 * n)},
        compiler_params=pltpu.CompilerParams(has_side_effects=_DATAFLOW),
    )(*gs, *lands, ssem, rsem, *afters)
    return list(outs[:n]), list(outs[n:])


def _pair_sums(grads, exch_bf16, cidx, tag, theirs=None):
    if theirs is None:
        theirs = _pair_send(grads, name=f"rs_pair_send_{tag}")
    hf, hx = [], []
    for a in range(len(grads)):
        res = _pair_sum(grads[a], theirs[a], cidx, cast=exch_bf16[a], name=f"rs_pair_sum_{tag}{a}")
        hf.append(res[0])
        hx.append(res[1] if exch_bf16[a] else res[0])
    return hf, hx


def _chip_sums(hf, got, chip_idx, tag):
    return [_chip_sum(hf[a], got[a], chip_idx, name=f"rs_chip_sum_{tag}{a}") for a in range(len(hf))]


def _interleave(a, B, L):
    return a.reshape(B, L, -1).transpose(1, 0, 2).reshape(B * L, -1)


def _deinterleave(a, B, L):
    return a.reshape(L, B, -1).transpose(1, 0, 2).reshape(B * L, -1)


def _local_step(x, tgt, W, S, on_grads=None):
    B, L, D = x.shape
    T = B * L
    G = D // SSM_GROUP
    Pst = SSM_STATE
    hpg = D // HEAD_DIM
    HW = hpg * HEAD_DIM
    ncl = G // GROUPS_PER_CLUSTER
    x2 = x.reshape(T, D)
    tgt2 = tgt.reshape(T, D)

    disc = lambda *p: _s5_discretize(*p)
    (ab_r, ab_i, bb_r, bb_i), disc_vjp = jax.vjp(disc, S["lam_re"], S["lam_im"], S["log_dt"], S["b_re"], S["b_im"])
    wb = jnp.concatenate([_blockdiag(jnp.transpose(bb_r, (0, 2, 1))), _blockdiag(jnp.transpose(bb_i, (0, 2, 1)))],
                         axis=-1).astype(MXU_DTYPE)
    wc = jnp.concatenate([_blockdiag(jnp.transpose(S["c_re"], (0, 2, 1))), _blockdiag(-jnp.transpose(S["c_im"], (0, 2, 1)))],
                         axis=1).astype(MXU_DTYPE)
    cs = GROUPS_PER_CLUSTER * Pst
    slab = lambda ab: jnp.tile(jnp.transpose(ab.reshape(ncl, cs // LANES, LANES), (1, 0, 2)), (1, B, 1))
    a_r, a_i = slab(ab_r), slab(ab_i)
    d_row = S["d"].reshape(1, D)

    xi = _interleave(x2, B, L)
    y, yg, h_r, h_i = _s5_fwd(xi, wb, wc, a_r, a_i, d_row, B, name="s5_fwd")
    _anchor(W, yg)
    z = _mm_nn(yg, W["w_glu"], bias=S["b_glu"].reshape(1, D), name="glu_z")
    gate = _glu_gate(y, z, name="glu_gate")
    mix_i = _mm_nn(gate, W["w_out"], name="s5_out")
    tok = _prepare(W, "w_up", mix_i)
    mix = _deinterleave(mix_i, B, L)
    h1, h1b, xh1, rs1 = _ln_fwd(x2, mix, S["ln_gain"][0, 0][None] + tok, S["ln_bias"][0, 0][None], name="ln_fwd_0a")

    def ffn_fwd(hb, l, prepare=None):
        hc = _mm_nn(hb, W["w_up"], l=l, out_dtype=MXU_DTYPE, name=f"ffn_up_{l}")
        tok = _prepare(W, prepare, hc) if prepare else 0.0
        a = _conv_glu_fwd(hc, S["conv_w"][l], S["conv_b"][l][None] + tok, L, name=f"ffn_conv_{l}")
        f = _mm_nn(a, W["w_down"], l=l, name=f"ffn_down_{l}")
        return hc, a, f

    _anchor(W, h1b)
    hc0, a0, f0 = ffn_fwd(h1b, 0, prepare="w_kv")
    h2, h2b, xh2, rs2 = _ln_fwd(h1, f0, S["ln_gain"][0, 1][None], S["ln_bias"][0, 1][None], name="ln_fwd_0b")

    _anchor(W, h2b)
    kv = _mm_nn(h2b, W["w_kv"], name="attn_kv")
    q = _mm_nn(h2b, W["w_q"], name="attn_q")
    bias = _attn_bias(S["rel_bias"], hpg)
    o3, l3 = _attn_fwd(q, kv, bias, L, hpg, name="attn_fwd")
    o, ob, lse = _attn_merge(o3, l3, HW, name="attn_merge")
    att = _mm_nn(ob, W["w_ao"], name="attn_out")
    h3, h3b, xh3, rs3 = _ln_fwd(h2, att, S["ln_gain"][1, 0][None], S["ln_bias"][1, 0][None], name="ln_fwd_1a")
    hc1, a1, f1 = ffn_fwd(h3b, 1)
    h4, _, xh4, rs4 = _ln_fwd(h3, f1, S["ln_gain"][1, 1][None], S["ln_bias"][1, 1][None], name="ln_fwd_1b")

    dh4, lrow = _loss_grad(h4, tgt2, name="loss")
    loss = lrow[0, 0]

    GW, GS = {}, {}

    def ffn_bwd(dzb, hb, hc, a, l):
        da = _mm_nt(dzb, W["w_down"], l=l, out_dtype=MXU_DTYPE, name=f"ffn_down_bwd_x_{l}")
        GW[f"w_down{l}"] = _tn(a, dzb, ptotal=1, np_cols=D, name=f"ffn_down_bwd_w_{l}")
        dc, dcw, dcb = _conv_glu_bwd(hc, da, S["conv_w"][l], S["conv_b"][l][None], L, name=f"ffn_conv_bwd_{l}")
        dhc = _conv_bwd_input(dc, S["conv_w"][l], L, name=f"ffn_conv_bwd_x_{l}")
        dh = _mm_nt(dhc, W["w_up"], l=l, name=f"ffn_up_bwd_x_{l}")
        GW[f"w_up{l}"] = _tn(hb, dhc, ptotal=W["w_up"].shape[0], np_cols=W["w_up"].shape[3], name=f"ffn_up_bwd_w_{l}")
        return dh, dcw, dcb

    dz4, dz4b, dg4, db4 = _ln_bwd([dh4], [1.0], xh4, rs4, S["ln_gain"][1, 1][None], name="ln_bwd_1b")
    dh3f, dcw1, dcb1 = ffn_bwd(dz4b, h3b, hc1, a1, 1)
    dz3, dz3b, dg3, db3 = _ln_bwd([dz4, dh3f], [DN_ALPHA, 1.0], xh3, rs3, S["ln_gain"][1, 0][None], name="ln_bwd_1a")
    do = _mm_nt(dz3b, W["w_ao"], name="attn_out_bwd_x")
    GW["w_ao"] = _tn(ob, dz3b, ptotal=1, np_cols=D, name="attn_out_bwd_w")
    dq, dk, dv, ds_sum = _attn_bwd(q, kv, do, o, lse, bias, L, hpg, name="attn_bwd")
    GS["rel_bias"] = _bias_grad(ds_sum, hpg, name="attn_bias_grad")
    GW["w_q"] = _tn(h2b, dq, ptotal=W["w_q"].shape[0], np_cols=W["w_q"].shape[3], name="attn_q_bwd_w")
    pkv, npkv = W["w_kv"].shape[0], W["w_kv"].shape[3]
    gkv = _tn(h2b, dk, ptotal=pkv, np_cols=npkv, p0=0, name="attn_k_bwd_w")
    GW["w_kv"] = _tn(h2b, dv, ptotal=pkv, np_cols=npkv, p0=pkv // 2, prev=gkv, name="attn_v_bwd_w")
    dh2q = _mm_nt(dq, W["w_q"], name="attn_q_bwd_x")
    dh2k = _mm_nt(dk, W["w_kv"], p0=0, pn=pkv // 2, name="attn_k_bwd_x")
    dh2v = _mm_nt(dv, W["w_kv"], p0=pkv // 2, pn=pkv // 2, name="attn_v_bwd_x")

    gain_0b = S["ln_gain"][0, 1][None]
    if on_grads is not None:
        gain_0b = gain_0b + on_grads(0, GW, dh2v)[0, 0]

    dz2, dz2b, dg2, db2 = _ln_bwd([dz3, dh2q, dh2k, dh2v], [DN_ALPHA, 1.0, 1.0, 1.0], xh2, rs2, gain_0b,
                                  name="ln_bwd_0b")
    dh1f, dcw0, dcb0 = ffn_bwd(dz2b, h1b, hc0, a0, 0)
    gain_0a = S["ln_gain"][0, 0][None]
    if on_grads is not None:
        gain_0a = gain_0a + on_grads(1, GW, GW["w_up0"])[0, 0]
    dz1, dz1b, dg1, db1 = _ln_bwd([dz2, dh1f], [DN_ALPHA, 1.0], xh1, rs1, gain_0a, name="ln_bwd_0a")
    dmix_i = _interleave(dz1b, B, L)
    dgate = _mm_nt(dmix_i, W["w_out"], name="s5_out_bwd_x")
    GW["w_out"] = _tn(gate, dmix_i, ptotal=1, np_cols=D, name="s5_out_bwd_w")
    dzg, dyg1, dbglu = _glu_bwd(y, z, dgate, name="glu_bwd")
    dyg2 = _mm_nt(dzg, W["w_glu"], name="glu_z_bwd_x")
    GW["w_glu"] = _tn(yg, dzg, ptotal=1, np_cols=D, name="glu_z_bwd_w")
    dy = _gelu_bwd(y, dyg1, dyg2, name="gelu_bwd")
    if on_grads is not None:
        d_row = d_row + on_grads(2, GW, GW["w_glu"])[0, 0]
    du_i, g_r, g_i, dar, dai, dd = _s5_bwd(dy, xi, h_r, h_i, wb, wc, a_r, a_i, d_row, B, name="s5_bwd")
    started = on_grads(3, GW, du_i) if on_grads is not None else None
    dwb_r = _cluster_tn(xi, g_r, ncl, tok_left=True, name="s5_b_grad_re", after=started)
    dwb_i = _cluster_tn(xi, g_i, ncl, tok_left=True, name="s5_b_grad_im")
    dwc_r = _cluster_tn(dy, h_r, ncl, tok_left=False, name="s5_c_grad_re")
    dwc_i = _cluster_tn(dy, h_i, ncl, tok_left=False, name="s5_c_grad_im")
    grad_x = _axpy(dz1, _deinterleave(du_i, B, L), DN_ALPHA, name="grad_x")

    dbb_r = jnp.transpose(_unblockdiag(dwb_r, SSM_GROUP, Pst), (0, 2, 1))
    dbb_i = jnp.transpose(_unblockdiag(dwb_i, SSM_GROUP, Pst), (0, 2, 1))
    unslab = lambda da: jnp.transpose(da.reshape(cs // LANES, B, ncl, LANES).sum(1), (1, 0, 2)).reshape(G, Pst)
    dab_r, dab_i = unslab(dar), unslab(dai)
    GS["lam_re"], GS["lam_im"], GS["log_dt"], GS["b_re"], GS["b_im"] = disc_vjp((dab_r, dab_i, dbb_r, dbb_i))
    GS["c_re"] = jnp.transpose(_unblockdiag(dwc_r, Pst, SSM_GROUP), (0, 2, 1))
    GS["c_im"] = -jnp.transpose(_unblockdiag(dwc_i, Pst, SSM_GROUP), (0, 2, 1))
    GS["d"] = dd.reshape(G, SSM_GROUP)
    GS["b_glu"] = dbglu.reshape(D)
    GS["conv_w"] = jnp.stack([dcw0, dcw1])
    GS["conv_b"] = jnp.stack([dcb0[0], dcb1[0]])
    GS["ln_gain"] = jnp.stack([jnp.stack([dg1[0], dg2[0]]), jnp.stack([dg3[0], dg4[0]])])
    GS["ln_bias"] = jnp.stack([jnp.stack([db1[0], db2[0]]), jnp.stack([db3[0], db4[0]])])
    return loss, grad_x.reshape(B, L, D), GW, GS


SMALL_REPLICATED = ("lam_re", "lam_im", "log_dt", "b_re", "b_im", "c_re", "c_im", "d", "rel_bias", "conv_b")
SMALL_SHARDED = ("b_glu", "conv_w", "ln_gain", "ln_bias")
SMALL_ORDER = SMALL_REPLICATED + SMALL_SHARDED


def _pack(arrs, lanes, row_mult):
    flat = jnp.concatenate([a.reshape(-1).astype(F32) for a in arrs])
    rows = -(-flat.shape[0] // lanes)
    rows = -(-rows // row_mult) * row_mult
    return jnp.pad(flat, (0, rows * lanes - flat.shape[0])).reshape(rows, lanes)


def _unpack(packed, shapes):
    flat = packed.reshape(-1)
    out, off = [], 0
    for s in shapes:
        n = int(np.prod(s))
        out.append(flat[off:off + n].reshape(s))
        off += n
    return out


def kernel(x, s5_lam_re, s5_lam_im, s5_log_dt, s5_b_re, s5_b_im, s5_c_re, s5_c_im, s5_d, s5_w_glu, s5_b_glu, s5_w_out, attn_w_kv, attn_w_q, attn_w_out, rel_bias, ffn_w_up, ffn_conv_w, ffn_conv_b, ffn_w_down, ln_gain, ln_bias, loss_target, m_s5_lam_re, m_s5_lam_im, m_s5_log_dt, m_s5_b_re, m_s5_b_im, m_s5_c_re, m_s5_c_im, m_s5_d, m_s5_w_glu, m_s5_b_glu, m_s5_w_out, m_attn_w_kv, m_attn_w_q, m_attn_w_out, m_rel_bias, m_ffn_w_up, m_ffn_conv_w, m_ffn_conv_b, m_ffn_w_down, m_ln_gain, m_ln_bias, v_s5_lam_re, v_s5_lam_im, v_s5_log_dt, v_s5_b_re, v_s5_b_im, v_s5_c_re, v_s5_c_im, v_s5_d, v_s5_w_glu, v_s5_b_glu, v_s5_w_out, v_attn_w_kv, v_attn_w_q, v_attn_w_out, v_rel_bias, v_ffn_w_up, v_ffn_conv_w, v_ffn_conv_b, v_ffn_w_down, v_ln_gain, v_ln_bias):
    names = ["s5_lam_re", "s5_lam_im", "s5_log_dt", "s5_b_re", "s5_b_im", "s5_c_re", "s5_c_im", "s5_d", "s5_w_glu",
             "s5_b_glu", "s5_w_out", "attn_w_kv", "attn_w_q", "attn_w_out", "rel_bias", "ffn_w_up", "ffn_conv_w",
             "ffn_conv_b", "ffn_w_down", "ln_gain", "ln_bias"]
    loc = locals()
    w_in = {n: loc[n] for n in names}
    m_in = {n: loc["m_" + n] for n in names}
    v_in = {n: loc["v_" + n] for n in names}
    chip = 2 * lax.axis_index("x") + lax.axis_index("y")
    core = lax.axis_index("c")
    chip_idx = jnp.reshape(chip, (1,)).astype(jnp.int32)
    cidx = jnp.reshape(core, (1,)).astype(jnp.int32)

    big = [("w_glu", "s5_w_glu", "rows"), ("w_out", "s5_w_out", "rows"), ("w_ao", "attn_w_out", "rows"),
           ("w_kv", "attn_w_kv", "cols"), ("w_q", "attn_w_q", "cols"),
           ("w_up", "ffn_w_up", "layer_cols"), ("w_down", "ffn_w_down", "layer_rows")]

    def halves(t, kind):
        if kind.startswith("layer"):
            return t
        r, c = t.shape[-2:]
        return t.reshape(2, r // 2, c)

    def to_weight(g, kind):
        _, _, r, c = g.shape
        if kind == "rows":
            return g.reshape(1, 1, 8 * r, c)
        if kind == "cols":
            return g.reshape(4, 1, 2 * r, c)
        if kind == "layer_cols":
            return g
        return jnp.transpose(g, (1, 0, 2, 3)).reshape(1, 2, 4 * r, c)

    small_sh = {"b_glu": s5_b_glu[0], "conv_w": ffn_conv_w, "ln_gain": ln_gain, "ln_bias": ln_bias}
    sh_shapes = [small_sh[k].shape for k in SMALL_SHARDED]
    sh_pack = _pack([small_sh[k] for k in SMALL_SHARDED], 128, 16)

    shards = [halves(w_in[src].astype(MXU_DTYPE), kind) for _, src, kind in big]
    shards.append(sh_pack.reshape(2, sh_pack.shape[0] // 2, 128))
    shard_of = {key: s for (key, _, _), s in zip(big, shards)}
    shard_of["small"] = shards[-1]
    kind_of = {key: kind for key, _, kind in big}

    group_keys = [["w_glu", "w_out", "small"], ["w_up", "w_down"], ["w_kv", "w_q", "w_ao"]]
    started, token = _gather_start([[shard_of[k] for k in g] for g in group_keys], name="weights_gather_start")

    forwarding = {}

    def prepare_group(gi, after):
        ssem, rsem, thru, lands = started[gi]
        lands = _gather_wait(ssem, rsem, thru, lands, after, name=f"weights_gather_wait_{gi}")
        forwarding[gi], tok = _gather_forward_start(lands, name=f"weights_gather_forward_start_{gi}")
        return tok

    def finish_group(gi, after):
        if gi in forwarding:
            lands = _gather_forward_wait(forwarding.pop(gi), after, name=f"weights_gather_forward_wait_{gi}")
        else:
            ssem, rsem, thru, lands = started[gi]
            lands = _gather_wait(ssem, rsem, thru, lands, after, name=f"weights_gather_wait_{gi}")
            lands = _gather_forward(lands, name=f"weights_gather_forward_{gi}")
        out = {}
        for key, land in zip(group_keys[gi], lands):
            full = lax.dynamic_update_slice(land, shard_of[key][None], (chip, 0, 0, 0))
            if key == "small":
                parts = [_unpack(full[p], sh_shapes) for p in range(4)]
                for i, k in enumerate(SMALL_SHARDED):
                    out[k] = jnp.concatenate([parts[p][i] for p in range(4)], axis=-1)
            else:
                out[key] = to_weight(full, kind_of[key])
        return out

    replicated = dict(lam_re=s5_lam_re[0], lam_im=s5_lam_im[0], log_dt=s5_log_dt[0], b_re=s5_b_re[0], b_im=s5_b_im[0],
                      c_re=s5_c_re[0], c_im=s5_c_im[0], rel_bias=rel_bias, conv_b=ffn_conv_b,
                      d=s5_d[0] + token[0, 0])
    group_of = {k: gi for gi, g in enumerate(group_keys) for k in g if k != "small"}
    group_of.update({k: 0 for k in SMALL_SHARDED})
    group_of.update({k: "replicated" for k in replicated})
    params = _Lazy(group_of, lambda g, after: replicated if g == "replicated" else finish_group(g, after), prepare_group)

    red = [("w_up1", "ffn_w_up", 1), ("w_down1", "ffn_w_down", 1), ("w_ao", "attn_w_out", 0), ("w_kv", "attn_w_kv", 0),
           ("w_q", "attn_w_q", 0), ("w_down0", "ffn_w_down", 0), ("w_up0", "ffn_w_up", 0), ("w_out", "s5_w_out", 0),
           ("w_glu", "s5_w_glu", 0)]
    stages = [red[:5], red[5:7], red[7:]]

    def grad_halves(gw, key, src):
        r, c = w_in[src].shape[-2:]
        return gw[key].reshape(4, 2, r // 2, c)

    sent, early = {}, []

    def on_grads(stage, gw, latest):
        tokens = []
        if stage > 0:
            tag = "abc"[stage - 1]
            ga, theirs = _pair_send_wait(sent.pop(stage - 1), latest, name=f"rs_pair_send_wait_{tag}")
            hf, hx = _pair_sums(ga, [True] * len(ga), cidx, tag, theirs)
            started, tok = _chip_exchange_start(hx, name=f"rs_chip_exchange_start_{tag}")
            early.append((hf, started, tag))
            tokens.append(tok)
        if stage < len(stages):
            ga = [grad_halves(gw, key, src) for key, src, _ in stages[stage]]
            sent[stage], tok = _pair_send_start(ga, name=f"rs_pair_send_start_{'abc'[stage]}")
            tokens.append(tok)
        return sum(tokens[1:], tokens[0])

    loss, grad_x, GW, GS = _local_step(x, loss_target, params, params, on_grads)

    gs_shapes = [GS[k].shape for k in SMALL_ORDER] + [(1,)]
    gs_pack = _pack([GS[k] for k in SMALL_ORDER] + [loss.reshape(1)], 128, 64)
    rs = gs_pack.shape[0] // 8
    gs_halves = [gs_pack.reshape(4, 2, rs, 128)]
    hf_s, hx_s = _pair_sums(gs_halves, [False], cidx, "s")
    started_s, after = _chip_exchange_start(hx_s, name="rs_chip_exchange_start_s")
    mine = []
    for hf, started, tag in early:
        got = _chip_exchange_wait(started, after, name=f"rs_chip_exchange_wait_{tag}")
        mine += _chip_sums(hf, got, chip_idx, tag)
        after = mine[-1]
    mine_s = _chip_sums(hf_s, _chip_exchange_wait(started_s, after, name="rs_chip_exchange_wait_s"), chip_idx, "s")[0]
    other_s = _pair_swap([mine_s], name="rs_pair_swap_small")[0]
    swapping, tok = _pair_send_start(mine, name="rs_pair_swap_start", whole=True, after=other_s)
    small_halves = jnp.where(core == 0, jnp.concatenate([mine_s, other_s]), jnp.concatenate([other_s, mine_s]))
    small_halves = small_halves + tok[0, 0]
    small_all = _gather([small_halves], chip, name="small_grads_all_gather")[0]
    totals = _unpack(small_all, gs_shapes)
    gsmall = dict(zip(SMALL_ORDER, totals))
    loss = totals[-1][0]

    small_w = {"lam_re": s5_lam_re, "lam_im": s5_lam_im, "log_dt": s5_log_dt, "b_re": s5_b_re, "b_im": s5_b_im,
               "c_re": s5_c_re, "c_im": s5_c_im, "d": s5_d, "rel_bias": rel_bias, "conv_b": ffn_conv_b,
               "b_glu": s5_b_glu, "conv_w": ffn_conv_w, "ln_gain": ln_gain, "ln_bias": ln_bias}
    small_name = {"lam_re": "s5_lam_re", "lam_im": "s5_lam_im", "log_dt": "s5_log_dt", "b_re": "s5_b_re", "b_im": "s5_b_im",
                  "c_re": "s5_c_re", "c_im": "s5_c_im", "d": "s5_d", "rel_bias": "rel_bias", "conv_b": "ffn_conv_b",
                  "b_glu": "s5_b_glu", "conv_w": "ffn_conv_w", "ln_gain": "ln_gain", "ln_bias": "ln_bias"}
    sg = {}
    for k in SMALL_ORDER:
        shp = small_w[k].shape
        g = gsmall[k]
        if k in SMALL_SHARDED:
            width = shp[-1]
            g = lax.dynamic_slice_in_dim(g, chip * width, width, axis=g.ndim - 1)
        sg[k] = g.reshape(shp)
    sd, snm, snv = {}, {}, {}
    for k in SMALL_ORDER:
        shp = small_w[k].shape
        flat = lambda t: t.reshape(-1, shp[-1])
        r3 = _adamw(flat(small_w[k]), flat(sg[k]), flat(m_in[small_name[k]]), flat(v_in[small_name[k]]),
                    name=f"adamw_{k}")
        sd[k], snm[k], snv[k] = (t.reshape(shp) for t in r3)

    mine, other = _pair_send_wait(swapping, [sd[k] for k in SMALL_ORDER], name="rs_pair_swap_wait", whole=True)
    big_res = {}
    for (key, src, layer), gm, go in zip(red, mine, other):
        nl = w_in[src].shape[0] if src in ("ffn_w_up", "ffn_w_down") else 1
        r, c = w_in[src].shape[-2:]
        view = lambda t: t.reshape(nl, 2, r // 2, c)
        res4 = _adamw_halves(view(w_in[src]), view(m_in[src]), view(v_in[src]), gm, go, cidx, layer=layer,
                             prev=big_res.get(src), name=f"adamw_{key}")
        big_res[src] = res4
    big_res = {src: tuple(t.reshape(w_in[src].shape) for t in res4) for src, res4 in big_res.items()}

    def big_out(i):
        return {src: big_res[src][i] for _, src, _ in big}

    res = [{}, {}, {}, {}]
    for i in range(4):
        res[i].update(big_out(i))
    for k in SMALL_ORDER:
        res[0][small_name[k]] = sg[k]
        res[1][small_name[k]] = sd[k]
        res[2][small_name[k]] = snm[k]
        res[3][small_name[k]] = snv[k]
    outs = [loss, grad_x]
    for i in range(4):
        outs += [res[i][n] for n in names]
    return tuple(outs)
```

```python
import functools
import math

import numpy as np
import jax
import jax.numpy as jnp
from jax import lax
from jax.experimental import pallas as pl
from jax.experimental.pallas import tpu as pltpu

F32 = jnp.float32
BF16 = jnp.bfloat16
MXU_DTYPE = jnp.bfloat16
V7X_VMEM_LIMIT_BYTES = 52 << 20
MESH = pl.DeviceIdType.MESH

DEPTH = 2
SSM_GROUP = 16
SSM_STATE = 64
GROUPS_PER_CLUSTER = 16
CLUSTER_W = GROUPS_PER_CLUSTER * SSM_GROUP
HEAD_DIM = 64
DILATIONS = (1, 4, 16)
BAND = 128
ATTN_BATCH = (4, 8)
NEG_BIG = -1e30
REL_BUCKETS = 32
REL_MAX_DIST = 2048
DN_ALPHA = (2.0 * DEPTH) ** 0.25
LN_EPS = 1e-5
ADAM_LR, ADAM_B1, ADAM_B2, ADAM_EPS, ADAM_WD, ADAM_STEP = 0.001, 0.9, 0.999, 1e-08, 0.01, 10
GELU_K = math.sqrt(2.0 / math.pi)
GELU_C = 0.044715


def _pallas(body, **kw):
    return pl.pallas_call(body, **kw)


def _params(sem=None):
    return pltpu.CompilerParams(dimension_semantics=sem, vmem_limit_bytes=V7X_VMEM_LIMIT_BYTES)


def _pick(n, cands):
    for c in cands:
        if n % c == 0:
            return c
    return n


def _sigmoid(z):
    return 1.0 / (1.0 + jnp.exp(-z))


def _gelu(y):
    return 0.5 * y * (1.0 + jnp.tanh(GELU_K * (y + GELU_C * y * y * y)))


def _gelu_grad(y):
    t = jnp.tanh(GELU_K * (y + GELU_C * y * y * y))
    return 0.5 * (1.0 + t) + 0.5 * y * (1.0 - t * t) * (GELU_K * (1.0 + 3.0 * GELU_C * y * y))


def _mm_nn(a, w, *, l=0, bias=None, out_dtype=F32, name):
    T, K = a.shape
    P, _, _, Np = w.shape
    tm = _pick(T, (1024, 512, 256, 128))
    tn = _pick(Np, (1408, 1024, 768, 512, 384, 256, 128))
    nj = Np // tn

    def body(*refs):
        if bias is None:
            a_ref, w_ref, o_ref = refs
        else:
            a_ref, w_ref, b_ref, o_ref = refs
        acc = jnp.dot(a_ref[...].astype(MXU_DTYPE), w_ref[...].astype(MXU_DTYPE), preferred_element_type=F32)
        if bias is not None:
            acc = acc + b_ref[...]
        o_ref[...] = acc.astype(o_ref.dtype)

    in_specs = [pl.BlockSpec((tm, K), lambda p, j, i: (i, 0)),
                pl.BlockSpec((None, None, K, tn), lambda p, j, i: (p, l, 0, j))]
    args = [a, w]
    if bias is not None:
        in_specs.append(pl.BlockSpec((1, tn), lambda p, j, i: (0, p * nj + j)))
        args.append(bias)
    return _pallas(
        body, name=name, grid=(P, nj, T // tm), in_specs=in_specs,
        out_specs=pl.BlockSpec((tm, tn), lambda p, j, i: (i, p * nj + j)),
        out_shape=jax.ShapeDtypeStruct((T, P * Np), out_dtype),
        compiler_params=_params(("parallel", "parallel", "parallel")),
    )(*args)


def _mm_nt(a, w, *, l=0, p0=0, pn=None, out_dtype=F32, name):
    T = a.shape[0]
    _, _, K, Np = w.shape
    pn = w.shape[0] if pn is None else pn
    tm = _pick(T, (1024, 512, 256, 128) if K <= 1024 else (512, 256, 128))
    tn = _pick(Np, (1536, 1408, 1024, 768, 512, 384, 256, 128))
    nj = Np // tn
    nred = pn * nj

    def body(a_ref, w_ref, o_ref, acc):
        r = pl.program_id(1)

        @pl.when(r == 0)
        def _():
            acc[...] = jnp.zeros_like(acc)

        acc[...] += lax.dot_general(a_ref[...].astype(MXU_DTYPE), w_ref[...].astype(MXU_DTYPE),
                                    (((1,), (1,)), ((), ())), preferred_element_type=F32)

        @pl.when(r == nred - 1)
        def _():
            o_ref[...] = acc[...].astype(o_ref.dtype)

    return _pallas(
        body, name=name, grid=(T // tm, nred),
        in_specs=[pl.BlockSpec((tm, tn), lambda i, r: (i, r)),
                  pl.BlockSpec((None, None, K, tn), lambda i, r: (p0 + r // nj, l, 0, r % nj))],
        out_specs=pl.BlockSpec((tm, K), lambda i, r: (i, 0)),
        out_shape=jax.ShapeDtypeStruct((T, K), out_dtype),
        scratch_shapes=[pltpu.VMEM((tm, K), F32)],
        compiler_params=_params(("parallel", "arbitrary")),
    )(a, w)


def _tn(a, b, *, ptotal, np_cols, nl=1, l=0, p0=0, prev=None, name):
    T, K = a.shape
    Np = np_cols
    pn = b.shape[1] // Np
    tt = _pick(T, (1024, 512, 256, 128))
    tk = _pick(K, (1408, 1024, 512, 256, 128))
    tn = _pick(Np, (1408, 768, 512, 256, 128))
    if tk * tn > 1408 * 1024:
        tn = _pick(Np, (512, 256, 128))
    nj = Np // tn
    nt = T // tt

    def body(*refs):
        a_ref, b_ref = refs[0], refs[1]
        o_ref, acc = refs[-2], refs[-1]
        t = pl.program_id(3)

        @pl.when(t == 0)
        def _():
            acc[...] = jnp.zeros_like(acc)

        acc[...] += lax.dot_general(a_ref[...].astype(MXU_DTYPE), b_ref[...].astype(MXU_DTYPE),
                                    (((0,), (0,)), ((), ())), preferred_element_type=F32)

        @pl.when(t == nt - 1)
        def _():
            o_ref[...] = acc[...]

    in_specs = [pl.BlockSpec((tt, tk), lambda kb, p, j, t: (t, kb)),
                pl.BlockSpec((tt, tn), lambda kb, p, j, t: (t, p * nj + j))]
    args = [a, b]
    aliases = {}
    if prev is not None:
        in_specs.append(pl.BlockSpec(memory_space=pl.ANY))
        args.append(prev)
        aliases = {2: 0}
    return _pallas(
        body, name=name, grid=(K // tk, pn, nj, nt), in_specs=in_specs,
        out_specs=pl.BlockSpec((None, None, tk, tn), lambda kb, p, j, t: (p0 + p, l, kb, j)),
        out_shape=jax.ShapeDtypeStruct((ptotal, nl, K, Np), F32),
        scratch_shapes=[pltpu.VMEM((tk, tn), F32)],
        input_output_aliases=aliases,
        compiler_params=_params(("parallel", "parallel", "parallel", "arbitrary")),
    )(*args)


def _rows(tm, f):
    return pl.BlockSpec((tm, f), lambda i: (i, 0))


def _whole(shape):
    nd = len(shape)
    return pl.BlockSpec(shape, lambda i: (0,) * nd)


def _ln_fwd(xres, f, gain, bias, *, name):
    T, D = xres.shape
    tm = _pick(T, (256, 128))

    def body(x_ref, f_ref, g_ref, b_ref, y_ref, yb_ref, xh_ref, rs_ref):
        z = DN_ALPHA * x_ref[...] + f_ref[...]
        mu = jnp.mean(z, axis=-1, keepdims=True)
        zc = z - mu
        var = jnp.mean(zc * zc, axis=-1, keepdims=True)
        rstd = lax.rsqrt(var + LN_EPS)
        xh = zc * rstd
        y = xh * g_ref[...] + b_ref[...]
        y_ref[...] = y
        yb_ref[...] = y.astype(yb_ref.dtype)
        xh_ref[...] = xh
        rs_ref[...] = rstd

    return _pallas(
        body, name=name, grid=(T // tm,),
        in_specs=[_rows(tm, D), _rows(tm, D), _whole((1, D)), _whole((1, D))],
        out_specs=[_rows(tm, D), _rows(tm, D), _rows(tm, D), _rows(tm, 1)],
        out_shape=[jax.ShapeDtypeStruct((T, D), F32), jax.ShapeDtypeStruct((T, D), MXU_DTYPE),
                   jax.ShapeDtypeStruct((T, D), F32), jax.ShapeDtypeStruct((T, 1), F32)],
        compiler_params=_params(("parallel",)),
    )(xres, f, gain, bias)


def _ln_bwd(addends, coefs, xhat, rstd, gain, *, name):
    T, D = xhat.shape
    tm = _pick(T, (256, 128))
    n = len(addends)

    def body(*refs):
        adds = refs[:n]
        xh_ref, rs_ref, g_ref, dz_ref, dzb_ref, dg_ref, db_ref = refs[n:]
        dy = coefs[0] * adds[0][...]
        for c, r in zip(coefs[1:], adds[1:]):
            dy = dy + c * r[...]
        xh = xh_ref[...]
        dxh = dy * g_ref[...]
        m1 = jnp.mean(dxh, axis=-1, keepdims=True)
        m2 = jnp.mean(dxh * xh, axis=-1, keepdims=True)
        dz = rs_ref[...] * (dxh - m1 - xh * m2)
        dz_ref[...] = dz
        dzb_ref[...] = dz.astype(dzb_ref.dtype)

        @pl.when(pl.program_id(0) == 0)
        def _():
            dg_ref[...] = jnp.zeros_like(dg_ref)
            db_ref[...] = jnp.zeros_like(db_ref)

        dg_ref[...] += jnp.sum(dy * xh, axis=0, keepdims=True)
        db_ref[...] += jnp.sum(dy, axis=0, keepdims=True)

    return _pallas(
        body, name=name, grid=(T // tm,),
        in_specs=[_rows(tm, D)] * n + [_rows(tm, D), _rows(tm, 1), _whole((1, D))],
        out_specs=[_rows(tm, D), _rows(tm, D), _whole((1, D)), _whole((1, D))],
        out_shape=[jax.ShapeDtypeStruct((T, D), F32), jax.ShapeDtypeStruct((T, D), MXU_DTYPE),
                   jax.ShapeDtypeStruct((1, D), F32), jax.ShapeDtypeStruct((1, D), F32)],
        compiler_params=_params(("arbitrary",)),
    )(*addends, xhat, rstd, gain)


def _loss_grad(y, tgt, *, name):
    T, D = y.shape
    tm = _pick(T, (256, 128))

    def body(y_ref, t_ref, dy_ref, l_ref):
        e = y_ref[...] - t_ref[...]
        dy_ref[...] = e * (1.0 / D)

        @pl.when(pl.program_id(0) == 0)
        def _():
            l_ref[...] = jnp.zeros_like(l_ref)

        l_ref[...] += jnp.zeros_like(l_ref) + jnp.sum(e * e) * (0.5 / D)

    return _pallas(
        body, name=name, grid=(T // tm,),
        in_specs=[_rows(tm, D), _rows(tm, D)],
        out_specs=[_rows(tm, D), _whole((1, 128))],
        out_shape=[jax.ShapeDtypeStruct((T, D), F32), jax.ShapeDtypeStruct((1, 128), F32)],
        compiler_params=_params(("arbitrary",)),
    )(y, tgt)


def _axpy(a, b, ca, *, name):
    T, D = a.shape
    tm = _pick(T, (256, 128))

    def body(a_ref, b_ref, o_ref):
        o_ref[...] = ca * a_ref[...] + b_ref[...]

    return _pallas(
        body, name=name, grid=(T // tm,), in_specs=[_rows(tm, D), _rows(tm, D)], out_specs=_rows(tm, D),
        out_shape=jax.ShapeDtypeStruct((T, D), F32), compiler_params=_params(("parallel",)),
    )(a, b)


def _glu_gate(y, z, *, name):
    T, D = y.shape
    tm = _pick(T, (256, 128))

    def body(y_ref, z_ref, g_ref):
        g_ref[...] = (_gelu(y_ref[...]) * _sigmoid(z_ref[...])).astype(g_ref.dtype)

    return _pallas(
        body, name=name, grid=(T // tm,), in_specs=[_rows(tm, D), _rows(tm, D)], out_specs=_rows(tm, D),
        out_shape=jax.ShapeDtypeStruct((T, D), MXU_DTYPE), compiler_params=_params(("parallel",)),
    )(y, z)


def _glu_bwd(y, z, dg, *, name):
    T, D = y.shape
    tm = _pick(T, (256, 128))

    def body(y_ref, z_ref, dg_ref, dzb_ref, dyg_ref, db_ref):
        s = _sigmoid(z_ref[...])
        dg = dg_ref[...]
        dz = dg * _gelu(y_ref[...]) * s * (1.0 - s)
        dzb_ref[...] = dz.astype(dzb_ref.dtype)
        dyg_ref[...] = dg * s

        @pl.when(pl.program_id(0) == 0)
        def _():
            db_ref[...] = jnp.zeros_like(db_ref)

        db_ref[...] += jnp.sum(dz, axis=0, keepdims=True)

    return _pallas(
        body, name=name, grid=(T // tm,), in_specs=[_rows(tm, D)] * 3,
        out_specs=[_rows(tm, D), _rows(tm, D), _whole((1, D))],
        out_shape=[jax.ShapeDtypeStruct((T, D), MXU_DTYPE), jax.ShapeDtypeStruct((T, D), F32),
                   jax.ShapeDtypeStruct((1, D), F32)],
        compiler_params=_params(("arbitrary",)),
    )(y, z, dg)


def _gelu_bwd(y, d1, d2, *, name):
    T, D = y.shape
    tm = _pick(T, (256, 128))

    def body(y_ref, a_ref, b_ref, o_ref):
        o_ref[...] = (a_ref[...] + b_ref[...]) * _gelu_grad(y_ref[...])

    return _pallas(
        body, name=name, grid=(T // tm,), in_specs=[_rows(tm, D)] * 3, out_specs=_rows(tm, D),
        out_shape=jax.ShapeDtypeStruct((T, D), F32), compiler_params=_params(("parallel",)),
    )(y, d1, d2)


CONV_ROWS = 128
CONV_EDGE = 16


def _row_shifts(x, edge, drop_edge, tm, back):
    keep = jnp.where(drop_edge, 0.0, 1.0).astype(edge.dtype)
    ext = jnp.concatenate([edge * keep, x] if back else [x, edge * keep], axis=0)
    row = lax.broadcasted_iota(jnp.int32, (tm, tm + CONV_EDGE), 0)
    col = lax.broadcasted_iota(jnp.int32, (tm, tm + CONV_EDGE), 1)
    base = row + CONV_EDGE if back else row
    out = []
    for k in (1, 2):
        pick = (col == (base - k if back else base + k)).astype(x.dtype)
        out.append(jnp.dot(pick, ext, preferred_element_type=F32))
    return out


def _conv_specs(T, F2, tm):
    return [_rows(tm, F2),
            pl.BlockSpec((CONV_EDGE, F2), lambda i: (jnp.maximum(i * (tm // CONV_EDGE) - 1, 0), 0))]


def _conv_glu_fwd(hc, conv_w, conv_b, L, *, name):
    T, F2 = hc.shape
    F = F2 // 2
    tm = CONV_ROWS

    def body(x_ref, e_ref, w_ref, b_ref, a_ref):
        at_start = (pl.program_id(0) * tm) % L == 0
        x1, x2 = _row_shifts(x_ref[...], e_ref[...], at_start, tm, True)
        x = x_ref[...].astype(F32)
        c = b_ref[...] + w_ref[0:1, :] * x + w_ref[1:2, :] * x1 + w_ref[2:3, :] * x2
        val, gate = c[:, :F], c[:, F:]
        a_ref[...] = (gate * _sigmoid(gate) * val).astype(a_ref.dtype)

    return _pallas(
        body, name=name, grid=(T // tm,),
        in_specs=_conv_specs(T, F2, tm) + [_whole((3, F2)), _whole((1, F2))],
        out_specs=_rows(tm, F),
        out_shape=jax.ShapeDtypeStruct((T, F), MXU_DTYPE), compiler_params=_params(("parallel",)),
    )(hc, hc, conv_w, conv_b)


def _conv_glu_bwd(hc, da, conv_w, conv_b, L, *, name):
    T, F2 = hc.shape
    F = F2 // 2
    tm = CONV_ROWS

    def body(x_ref, e_ref, da_ref, w_ref, b_ref, dc_ref, dw_ref, db_ref):
        at_start = (pl.program_id(0) * tm) % L == 0
        x1, x2 = _row_shifts(x_ref[...], e_ref[...], at_start, tm, True)
        x = x_ref[...].astype(F32)
        c = b_ref[...] + w_ref[0:1, :] * x + w_ref[1:2, :] * x1 + w_ref[2:3, :] * x2
        val, gate = c[:, :F], c[:, F:]
        s = _sigmoid(gate)
        da = da_ref[...].astype(F32)
        dval = da * (gate * s)
        dgate = da * val * (s * (1.0 + gate * (1.0 - s)))
        dc = jnp.concatenate([dval, dgate], axis=-1)
        dc_ref[...] = dc.astype(dc_ref.dtype)

        @pl.when(pl.program_id(0) == 0)
        def _():
            dw_ref[...] = jnp.zeros_like(dw_ref)
            db_ref[...] = jnp.zeros_like(db_ref)

        dw_ref[0:1, :] += jnp.sum(dc * x, axis=0, keepdims=True)
        dw_ref[1:2, :] += jnp.sum(dc * x1, axis=0, keepdims=True)
        dw_ref[2:3, :] += jnp.sum(dc * x2, axis=0, keepdims=True)
        db_ref[...] += jnp.sum(dc, axis=0, keepdims=True)

    return _pallas(
        body, name=name, grid=(T // tm,),
        in_specs=_conv_specs(T, F2, tm) + [_rows(tm, F), _whole((3, F2)), _whole((1, F2))],
        out_specs=[_rows(tm, F2), _whole((3, F2)), _whole((1, F2))],
        out_shape=[jax.ShapeDtypeStruct((T, F2), MXU_DTYPE), jax.ShapeDtypeStruct((3, F2), F32),
                   jax.ShapeDtypeStruct((1, F2), F32)],
        compiler_params=_params(("arbitrary",)),
    )(hc, hc, da, conv_w, conv_b)


def _conv_bwd_input(dc, conv_w, L, *, name):
    T, F2 = dc.shape
    tm = CONV_ROWS
    edge = CONV_EDGE
    last_blk = T // edge - 1

    def body(x_ref, e_ref, w_ref, o_ref):
        at_end = ((pl.program_id(0) + 1) * tm) % L == 0
        x1, x2 = _row_shifts(x_ref[...], e_ref[...], at_end, tm, False)
        x = x_ref[...].astype(F32)
        o_ref[...] = (w_ref[0:1, :] * x + w_ref[1:2, :] * x1 + w_ref[2:3, :] * x2).astype(o_ref.dtype)

    return _pallas(
        body, name=name, grid=(T // tm,),
        in_specs=[_rows(tm, F2),
                  pl.BlockSpec((edge, F2), lambda i: (jnp.minimum((i + 1) * (tm // edge), last_blk), 0)),
                  _whole((3, F2))],
        out_specs=_rows(tm, F2),
        out_shape=jax.ShapeDtypeStruct((T, F2), MXU_DTYPE), compiler_params=_params(("parallel",)),
    )(dc, dc, conv_w)


S5_CHUNK = 128
S5_CHUNK_FWD = 256
LANES = 128


def _slab_rows(c, n, ncl):
    return pl.ds(c, n) if ncl == 1 else pl.ds(c, n, stride=ncl)


def _slab_put(ref, c, n, ncl, val):
    for s in range(val.shape[1] // LANES):
        ref[s, _slab_rows(c, n, ncl), :] = val[:, s * LANES:(s + 1) * LANES]


def _slab_get(ref, c, n, ncl):
    return jnp.concatenate([ref[s, _slab_rows(c, n, ncl), :] for s in range(ref.shape[0])], axis=-1)


def _slabs(n_slab, rows):
    return pl.BlockSpec((n_slab, rows, LANES), lambda i: (0, i, 0))


def _s5_fwd(xi, wb, wc, a_r, a_i, d_row, B, *, name):
    T, D = xi.shape
    ncl = wb.shape[0]
    cs = wb.shape[2] // 2
    ns = cs // LANES
    R = B * ncl
    Q = S5_CHUNK_FWD
    QR = Q * ncl
    nsteps = Q // B

    def body(x_ref, wb_ref, wc_ref, ar_ref, ai_ref, d_ref, y_ref, yg_ref, hr_ref, hi_ref, bur, bui, cr, ci):
        @pl.when(pl.program_id(0) == 0)
        def _():
            cr[...] = jnp.zeros_like(cr)
            ci[...] = jnp.zeros_like(ci)

        x = x_ref[...]
        xb = x.astype(MXU_DTYPE)
        for c in range(ncl):
            bu = jnp.dot(xb[:, c * CLUSTER_W:(c + 1) * CLUSTER_W], wb_ref[c], preferred_element_type=F32)
            _slab_put(bur, c, Q, ncl, bu[:, :cs])
            _slab_put(bui, c, Q, ncl, bu[:, cs:])
        ar = ar_ref[...]
        ai = ai_ref[...]

        def step(k, carry):
            hr, hi = carry
            sl = pl.ds(pl.multiple_of(k * R, R), R)
            nr = ar * hr - ai * hi + bur[:, sl, :]
            ni = ar * hi + ai * hr + bui[:, sl, :]
            hr_ref[:, sl, :] = nr
            hi_ref[:, sl, :] = ni
            return nr, ni

        hr, hi = lax.fori_loop(0, nsteps, step, (cr[...], ci[...]), unroll=4)
        cr[...] = hr
        ci[...] = hi
        parts = []
        for c in range(ncl):
            hrc = _slab_get(hr_ref, c, Q, ncl).astype(MXU_DTYPE)
            hic = _slab_get(hi_ref, c, Q, ncl).astype(MXU_DTYPE)
            parts.append(jnp.dot(hrc, wc_ref[c, :cs, :], preferred_element_type=F32)
                         + jnp.dot(hic, wc_ref[c, cs:, :], preferred_element_type=F32))
        y = d_ref[...] * x + (parts[0] if ncl == 1 else jnp.concatenate(parts, axis=-1))
        y_ref[...] = y
        yg_ref[...] = _gelu(y).astype(yg_ref.dtype)

    return _pallas(
        body, name=name, grid=(T // Q,),
        in_specs=[_rows(Q, D), _whole(wb.shape), _whole(wc.shape), _whole((ns, R, LANES)), _whole((ns, R, LANES)),
                  _whole((1, D))],
        out_specs=[_rows(Q, D), _rows(Q, D), _slabs(ns, QR), _slabs(ns, QR)],
        out_shape=[jax.ShapeDtypeStruct((T, D), F32), jax.ShapeDtypeStruct((T, D), MXU_DTYPE),
                   jax.ShapeDtypeStruct((ns, T * ncl, LANES), F32), jax.ShapeDtypeStruct((ns, T * ncl, LANES), F32)],
        scratch_shapes=[pltpu.VMEM((ns, QR, LANES), F32), pltpu.VMEM((ns, QR, LANES), F32),
                        pltpu.VMEM((ns, R, LANES), F32), pltpu.VMEM((ns, R, LANES), F32)],
        compiler_params=_params(("arbitrary",)),
    )(xi, wb, wc, a_r, a_i, d_row)


def _s5_bwd(dy, xi, h_r, h_i, wb, wc, a_r, a_i, d_row, B, *, name):
    T, D = dy.shape
    ncl = wb.shape[0]
    cs = wb.shape[2] // 2
    ns = cs // LANES
    R = B * ncl
    Q = S5_CHUNK
    nsteps = Q // B
    nchunk = T // Q
    QR = Q * ncl

    def rev(i):
        return nchunk - 1 - i

    def body(dy_ref, x_ref, hr_ref, hi_ref, pr_ref, pi_ref, wb_ref, wc_ref, ar_ref, ai_ref, d_ref,
             du_ref, gr_ref, gi_ref, dar_ref, dai_ref, dd_ref, dhr, dhi, cr, ci):
        i = pl.program_id(0)

        @pl.when(i == 0)
        def _():
            cr[...] = jnp.zeros_like(cr)
            ci[...] = jnp.zeros_like(ci)
            dar_ref[...] = jnp.zeros_like(dar_ref)
            dai_ref[...] = jnp.zeros_like(dai_ref)
            dd_ref[...] = jnp.zeros_like(dd_ref)

        dyv = dy_ref[...]
        dyb = dyv.astype(MXU_DTYPE)
        for c in range(ncl):
            dh = lax.dot_general(dyb[:, c * CLUSTER_W:(c + 1) * CLUSTER_W], wc_ref[c],
                                 (((1,), (1,)), ((), ())), preferred_element_type=F32)
            _slab_put(dhr, c, Q, ncl, dh[:, :cs])
            _slab_put(dhi, c, Q, ncl, dh[:, cs:])
        ar = ar_ref[...]
        ai = ai_ref[...]

        def step(j, carry):
            gr, gi = carry
            k = nsteps - 1 - j
            sl = pl.ds(pl.multiple_of(k * R, R), R)
            ngr = dhr[:, sl, :] + ar * gr + ai * gi
            ngi = dhi[:, sl, :] - ai * gr + ar * gi
            gr_ref[:, sl, :] = ngr
            gi_ref[:, sl, :] = ngi
            return ngr, ngi

        gr, gi = lax.fori_loop(0, nsteps, step, (cr[...], ci[...]), unroll=4)
        cr[...] = gr
        ci[...] = gi
        keep = jnp.where(i == nchunk - 1, 0.0, 1.0)
        hpr = jnp.concatenate([pr_ref[:, 8 - R:8, :] * keep, hr_ref[:, 0:QR - R, :]], axis=1)
        hpi = jnp.concatenate([pi_ref[:, 8 - R:8, :] * keep, hi_ref[:, 0:QR - R, :]], axis=1)
        gra, gia = gr_ref[...], gi_ref[...]
        steps = lambda t: jnp.sum(t.reshape(ns, nsteps, R, LANES), axis=1)
        dar_ref[...] += steps(gra * hpr + gia * hpi)
        dai_ref[...] += steps(gia * hpr - gra * hpi)
        parts = []
        for c in range(ncl):
            grc = _slab_get(gr_ref, c, Q, ncl).astype(MXU_DTYPE)
            gic = _slab_get(gi_ref, c, Q, ncl).astype(MXU_DTYPE)
            parts.append(lax.dot_general(grc, wb_ref[c, :, :cs], (((1,), (1,)), ((), ())), preferred_element_type=F32)
                         + lax.dot_general(gic, wb_ref[c, :, cs:], (((1,), (1,)), ((), ())), preferred_element_type=F32))
        du_ref[...] = d_ref[...] * dyv + (parts[0] if ncl == 1 else jnp.concatenate(parts, axis=-1))
        dd_ref[...] += jnp.sum(dyv * x_ref[...], axis=0, keepdims=True)

    tok = pl.BlockSpec((Q, D), lambda i: (rev(i), 0))
    st = pl.BlockSpec((ns, QR, LANES), lambda i: (0, rev(i), 0))
    before = pl.BlockSpec((ns, 8, LANES), lambda i: (0, jnp.maximum(rev(i) * (QR // 8) - 1, 0), 0))
    acc = _whole((ns, R, LANES))
    return _pallas(
        body, name=name, grid=(nchunk,),
        in_specs=[tok, tok, st, st, before, before, _whole(wb.shape), _whole(wc.shape), acc, acc, _whole((1, D))],
        out_specs=[tok, st, st, acc, acc, _whole((1, D))],
        out_shape=[jax.ShapeDtypeStruct((T, D), F32),
                   jax.ShapeDtypeStruct((ns, T * ncl, LANES), F32), jax.ShapeDtypeStruct((ns, T * ncl, LANES), F32),
                   jax.ShapeDtypeStruct((ns, R, LANES), F32), jax.ShapeDtypeStruct((ns, R, LANES), F32),
                   jax.ShapeDtypeStruct((1, D), F32)],
        scratch_shapes=[pltpu.VMEM((ns, QR, LANES), F32)] * 2 + [pltpu.VMEM((ns, R, LANES), F32)] * 2,
        compiler_params=_params(("arbitrary",)),
    )(dy, xi, h_r, h_i, h_r, h_i, wb, wc, a_r, a_i, d_row)


def _cluster_tn(tok, st, ncl, *, tok_left, name, after=None):
    T = tok.shape[0]
    ns = st.shape[0]
    cs = ns * LANES
    tt = _pick(T, (512, 256, 128))
    nt = T // tt
    oshape = (ncl, CLUSTER_W, cs) if tok_left else (ncl, cs, CLUSTER_W)

    def body(tok_ref, st_ref, *rest):
        o_ref, acc = rest[-2:]
        t = pl.program_id(0)

        @pl.when(t == 0)
        def _():
            acc[...] = jnp.zeros_like(acc)

        tk = tok_ref[...].astype(MXU_DTYPE)
        for c in range(ncl):
            tc = tk[:, c * CLUSTER_W:(c + 1) * CLUSTER_W]
            sc = _slab_get(st_ref, c, tt, ncl).astype(MXU_DTYPE)
            lhs, rhs = (tc, sc) if tok_left else (sc, tc)
            acc[c] += lax.dot_general(lhs, rhs, (((0,), (0,)), ((), ())), preferred_element_type=F32)

        @pl.when(t == nt - 1)
        def _():
            o_ref[...] = acc[...]

    return _pallas(
        body, name=name, grid=(nt,),
        in_specs=[_rows(tt, tok.shape[1]), _slabs(ns, tt * ncl)] + ([] if after is None else [_ANY]),
        out_specs=_whole(oshape),
        out_shape=jax.ShapeDtypeStruct(oshape, F32),
        scratch_shapes=[pltpu.VMEM(oshape, F32)],
        compiler_params=_params(("arbitrary",)),
    )(tok, st, *([] if after is None else [after]))


def _s5_discretize(lam_re, lam_im, log_dt, b_re, b_im):
    dt = jnp.exp(log_dt)[:, None]
    mag = jnp.exp(lam_re * dt)
    ab_r, ab_i = mag * jnp.cos(lam_im * dt), mag * jnp.sin(lam_im * dt)
    den = lam_re * lam_re + lam_im * lam_im
    nr = ab_r - 1.0
    co_r = (nr * lam_re + ab_i * lam_im) / den
    co_i = (ab_i * lam_re - nr * lam_im) / den
    bb_r = co_r[..., None] * b_re - co_i[..., None] * b_im
    bb_i = co_r[..., None] * b_im + co_i[..., None] * b_re
    return ab_r, ab_i, bb_r, bb_i


def _blockdiag(m):
    G, r, k = m.shape
    ncl = G // GROUPS_PER_CLUSTER
    m4 = m.reshape(ncl, GROUPS_PER_CLUSTER, r, k)
    eye = jnp.eye(GROUPS_PER_CLUSTER, dtype=m.dtype)
    return jnp.einsum('cgrk,gh->cgrhk', m4, eye).reshape(ncl, GROUPS_PER_CLUSTER * r, GROUPS_PER_CLUSTER * k)


def _unblockdiag(m, r, k):
    ncl = m.shape[0]
    m5 = m.reshape(ncl, GROUPS_PER_CLUSTER, r, GROUPS_PER_CLUSTER, k)
    eye = jnp.eye(GROUPS_PER_CLUSTER, dtype=m.dtype)
    return jnp.einsum('cgrhk,gh->cgrk', m5, eye).reshape(ncl * GROUPS_PER_CLUSTER, r, k)


def _t5_bucket(dist):
    exact = REL_BUCKETS // 2
    d = np.maximum(dist, 1).astype(np.float32)
    large = exact + (np.log(d / exact) / math.log(REL_MAX_DIST / exact) * (REL_BUCKETS - exact)).astype(np.int64)
    large = np.minimum(large, REL_BUCKETS - 1)
    return np.where(dist < exact, dist, large).astype(np.int32)


def _band_tables(dil):
    steps = np.arange(BAND)[:, None] + BAND - np.arange(2 * BAND)[None, :]
    bucket = _t5_bucket(np.maximum(steps, 0) * dil)
    in_band = (steps >= 0) & (steps <= BAND)
    return bucket, in_band


def _attn_bias(rel_bias, hpg):
    out = []
    for g, dil in enumerate(DILATIONS):
        bucket, in_band = _band_tables(dil)
        cols = rel_bias[:, g * hpg:(g + 1) * hpg].astype(F32)
        onehot = jnp.asarray((bucket.reshape(-1, 1) == np.arange(REL_BUCKETS)[None, :]).astype(np.float32))
        bias = jnp.dot(onehot, cols, precision=lax.Precision.HIGHEST).T.reshape(hpg, BAND, 2 * BAND)
        out.append(jnp.where(jnp.asarray(in_band)[None], bias, NEG_BIG))
    return jnp.concatenate(out, axis=0)


def _attn_blocks(dil, L):
    M = L // dil
    return M, M // BAND


def _row_sel(r, M, dil):
    return pl.ds(r, M) if dil == 1 else pl.ds(r, M, stride=dil)


def _attn_fwd(q, kv, bias, L, hpg, *, name):
    T = q.shape[0]
    nb_ = T // L
    HP = hpg // 2
    W3 = 3 * hpg * HEAD_DIM
    mmax = L

    def group_body(dil, q_ref, k_ref, v_ref, b_ref, o_ref, l_ref, os, ls):
        M, NB = _attn_blocks(dil, L)
        first = lax.broadcasted_iota(jnp.int32, (1, 2 * HEAD_DIM), 1) < HEAD_DIM
        loaded = {}

        def load(r):
            rows = _row_sel(r, M, dil)
            qf = q_ref[rows, :] * 0.125
            qm = [jnp.where(first, qf, 0.0).astype(MXU_DTYPE), jnp.where(first, 0.0, qf).astype(MXU_DTYPE)]
            kr = k_ref[rows, :].astype(MXU_DTYPE)
            va = jnp.concatenate([v_ref[rows, :].astype(MXU_DTYPE), jnp.ones((M, 2 * HEAD_DIM), MXU_DTYPE)], axis=-1)
            return qm, kr, va

        tasks = [(r, n) for r in range(dil) for n in range(NB)]
        nbatch = ATTN_BATCH[NB == 1]
        for t0 in range(0, len(tasks), nbatch):
            batch = tasks[t0:t0 + nbatch]
            for r, _ in batch:
                if r not in loaded:
                    loaded[r] = load(r)
            chains = [(r, n, hh) for r, n in batch for hh in range(2)]
            ks = lambda n: slice(0, BAND) if n == 0 else slice((n - 1) * BAND, (n + 1) * BAND)
            s = [lax.dot_general(loaded[r][0][hh][n * BAND:(n + 1) * BAND, :], loaded[r][1][ks(n), :],
                                 (((1,), (1,)), ((), ())), preferred_element_type=F32)
                 + (b_ref[hh, :, BAND:] if n == 0 else b_ref[hh]) for r, n, hh in chains]
            m = [jnp.max(t, axis=-1, keepdims=True) for t in s]
            p = [jnp.exp(t - mm) for t, mm in zip(s, m)]
            pv = [jnp.dot(t.astype(MXU_DTYPE), loaded[r][2][ks(n), :], preferred_element_type=F32)
                  for t, (r, n, hh) in zip(p, chains)]
            l = [t[:, 2 * HEAD_DIM:] for t in pv]
            o_h = [t[:, :2 * HEAD_DIM] / ll for t, ll in zip(pv, l)]
            l_h = [mm + jnp.log(ll) for mm, ll in zip(m, l)]
            for i, (r, n) in enumerate(batch):
                os[r * M + n * BAND:r * M + (n + 1) * BAND, :] = jnp.where(first, o_h[2 * i], o_h[2 * i + 1])
                ls[r * M + n * BAND:r * M + (n + 1) * BAND, :] = jnp.where(first, l_h[2 * i], l_h[2 * i + 1])
                if n == NB - 1:
                    rows = _row_sel(r, M, dil)
                    o_ref[rows, :] = os[r * M:(r + 1) * M, :]
                    l_ref[rows, :] = ls[r * M:(r + 1) * M, :]

    def body(q_ref, k_ref, v_ref, b_ref, o_ref, l_ref, os, ls):
        g = pl.program_id(0)
        for gi, dil in enumerate(DILATIONS):
            pl.when(g == gi)(functools.partial(group_body, dil, q_ref, k_ref, v_ref, b_ref, o_ref, l_ref, os, ls))

    blk = (L, 2 * HEAD_DIM)
    return _pallas(
        body, name=name, grid=(3, nb_, HP),
        in_specs=[pl.BlockSpec(blk, lambda g, b, h: (b, g * HP + h)),
                  pl.BlockSpec(blk, lambda g, b, h: (b, g * HP + h)),
                  pl.BlockSpec(blk, lambda g, b, h: (b, 3 * HP + g * HP + h)),
                  pl.BlockSpec((2, BAND, 2 * BAND), lambda g, b, h: (g * HP + h, 0, 0))],
        out_specs=[pl.BlockSpec(blk, lambda g, b, h: (b, g * HP + h)),
                   pl.BlockSpec(blk, lambda g, b, h: (b, g * HP + h))],
        out_shape=[jax.ShapeDtypeStruct((T, W3), F32), jax.ShapeDtypeStruct((T, W3), F32)],
        scratch_shapes=[pltpu.VMEM((mmax, 2 * HEAD_DIM), F32), pltpu.VMEM((mmax, 2 * HEAD_DIM), F32)],
        compiler_params=_params(("arbitrary", "arbitrary", "arbitrary")),
    )(q, kv, kv, bias)


def _attn_merge(o3, l3, hw, *, name):
    T = o3.shape[0]
    tm = _pick(T, (256, 128))

    def body(o0, o1, o2, l0, l1, l2, o_ref, ob_ref, lse_ref):
        a0, a1, a2 = l0[...], l1[...], l2[...]
        m = jnp.maximum(jnp.maximum(a0, a1), a2)
        e0, e1, e2 = jnp.exp(a0 - m), jnp.exp(a1 - m), jnp.exp(a2 - m)
        z = e0 + e1 + e2
        o = (e0 * o0[...] + e1 * o1[...] + e2 * o2[...]) / z
        o_ref[...] = o
        ob_ref[...] = o.astype(ob_ref.dtype)
        lse_ref[...] = m + jnp.log(z)

    def col(g):
        return pl.BlockSpec((tm, hw), lambda i: (i, g))

    return _pallas(
        body, name=name, grid=(T // tm,),
        in_specs=[col(0), col(1), col(2), col(0), col(1), col(2)],
        out_specs=[_rows(tm, hw)] * 3,
        out_shape=[jax.ShapeDtypeStruct((T, hw), F32), jax.ShapeDtypeStruct((T, hw), MXU_DTYPE),
                   jax.ShapeDtypeStruct((T, hw), F32)],
        compiler_params=_params(("parallel",)),
    )(o3, o3, o3, l3, l3, l3)


def _attn_bwd(q, kv, do, o, lse, bias, L, hpg, *, name):
    T = q.shape[0]
    nb_ = T // L
    HP = hpg // 2
    W3 = 3 * hpg * HEAD_DIM
    mmax = L

    def group_body(dil, q_ref, k_ref, v_ref, do_ref, o_ref, l_ref, b_ref, dq_ref, dk_ref, dv_ref, ds_ref,
                   dqs, dks, dvs):
        M, NB = _attn_blocks(dil, L)
        first = lax.broadcasted_iota(jnp.int32, (1, 2 * HEAD_DIM), 1) < HEAD_DIM
        loaded = {}

        def load(r):
            rows = _row_sel(r, M, dil)
            qf = q_ref[rows, :] * 0.125
            qm = [jnp.where(first, qf, 0.0).astype(MXU_DTYPE), jnp.where(first, 0.0, qf).astype(MXU_DTYPE)]
            kr = k_ref[rows, :].astype(MXU_DTYPE)
            vr = v_ref[rows, :].astype(MXU_DTYPE)
            dof = do_ref[rows, :]
            dom = [jnp.where(first, dof, 0.0).astype(MXU_DTYPE), jnp.where(first, 0.0, dof).astype(MXU_DTYPE)]
            dod = dof * o_ref[rows, :]
            delta = [jnp.sum(jnp.where(first, dod, 0.0), axis=-1, keepdims=True),
                     jnp.sum(jnp.where(first, 0.0, dod), axis=-1, keepdims=True)]
            lr = l_ref[rows, :]
            lse = [lr[:, 0:1], lr[:, HEAD_DIM:HEAD_DIM + 1]]
            if NB > 1:
                dks[r * M:(r + 1) * M, :] = jnp.zeros((M, 2 * HEAD_DIM), F32)
                dvs[r * M:(r + 1) * M, :] = jnp.zeros((M, 2 * HEAD_DIM), F32)
            return qm, kr, vr, dom, delta, lse

        nt = (((1,), (1,)), ((), ()))
        tn = (((0,), (0,)), ((), ()))
        tasks = [(r, n) for r in range(dil) for n in range(NB)]
        nbatch = ATTN_BATCH[NB == 1]
        for t0 in range(0, len(tasks), nbatch):
            batch = tasks[t0:t0 + nbatch]
            for r, _ in batch:
                if r not in loaded:
                    loaded[r] = load(r)
            chains = [(r, n, hh) for r, n in batch for hh in range(2)]
            qs = lambda n: slice(n * BAND, (n + 1) * BAND)
            ks = lambda n: slice(0, BAND) if n == 0 else slice((n - 1) * BAND, (n + 1) * BAND)
            qb = [loaded[r][0][hh][qs(n), :] for r, n, hh in chains]
            dob = [loaded[r][3][hh][qs(n), :] for r, n, hh in chains]
            kb = [loaded[r][1][ks(n), :] for r, n, hh in chains]
            s = [lax.dot_general(a, b, nt, preferred_element_type=F32) + (b_ref[hh, :, BAND:] if n == 0 else b_ref[hh])
                 for a, b, (r, n, hh) in zip(qb, kb, chains)]
            dp = [lax.dot_general(a, loaded[r][2][ks(n), :], nt, preferred_element_type=F32)
                  for a, (r, n, hh) in zip(dob, chains)]
            p = [jnp.exp(t - loaded[r][5][hh][qs(n), :]) for t, (r, n, hh) in zip(s, chains)]
            ds = [a * (b - loaded[r][4][hh][qs(n), :]) for a, b, (r, n, hh) in zip(p, dp, chains)]
            for t, (r, n, hh) in zip(ds, chains):
                if n == 0:
                    ds_ref[hh, :, BAND:] += t
                else:
                    ds_ref[hh] += t
            dsm = [t.astype(MXU_DTYPE) for t in ds]
            dq = [jnp.dot(a, b, preferred_element_type=F32) for a, b in zip(dsm, kb)]
            dk = [lax.dot_general(a, b, tn, preferred_element_type=F32) for a, b in zip(dsm, qb)]
            dv = [lax.dot_general(a.astype(MXU_DTYPE), b, tn, preferred_element_type=F32) for a, b in zip(p, dob)]
            for i, (r, n) in enumerate(batch):
                dqs[r * M + n * BAND:r * M + (n + 1) * BAND, :] = jnp.where(first, dq[2 * i], dq[2 * i + 1]) * 0.125
                ksm = slice(r * M + ks(n).start, r * M + ks(n).stop)
                if NB > 1:
                    dks[ksm, :] += dk[2 * i] + dk[2 * i + 1]
                    dvs[ksm, :] += dv[2 * i] + dv[2 * i + 1]
                else:
                    dks[ksm, :] = dk[2 * i] + dk[2 * i + 1]
                    dvs[ksm, :] = dv[2 * i] + dv[2 * i + 1]
                if n == NB - 1:
                    rows = _row_sel(r, M, dil)
                    dq_ref[rows, :] = dqs[r * M:(r + 1) * M, :]
                    dk_ref[rows, :] = dks[r * M:(r + 1) * M, :]
                    dv_ref[rows, :] = dvs[r * M:(r + 1) * M, :]

    def body(q_ref, k_ref, v_ref, do_ref, o_ref, l_ref, b_ref, dq_ref, dk_ref, dv_ref, ds_ref, dqs, dks, dvs):
        g = pl.program_id(0)

        @pl.when(pl.program_id(2) == 0)
        def _():
            ds_ref[...] = jnp.zeros_like(ds_ref)

        for gi, dil in enumerate(DILATIONS):
            pl.when(g == gi)(functools.partial(group_body, dil, q_ref, k_ref, v_ref, do_ref, o_ref, l_ref, b_ref,
                                               dq_ref, dk_ref, dv_ref, ds_ref, dqs, dks, dvs))

    blk = (L, 2 * HEAD_DIM)
    gcol = lambda g, h, b: (b, g * HP + h)
    hcol = lambda g, h, b: (b, h)
    return _pallas(
        body, name=name, grid=(3, HP, nb_),
        in_specs=[pl.BlockSpec(blk, gcol), pl.BlockSpec(blk, gcol),
                  pl.BlockSpec(blk, lambda g, h, b: (b, 3 * HP + g * HP + h)),
                  pl.BlockSpec(blk, hcol), pl.BlockSpec(blk, hcol), pl.BlockSpec(blk, hcol),
                  pl.BlockSpec((2, BAND, 2 * BAND), lambda g, h, b: (g * HP + h, 0, 0))],
        out_specs=[pl.BlockSpec(blk, gcol), pl.BlockSpec(blk, gcol), pl.BlockSpec(blk, gcol),
                   pl.BlockSpec((2, BAND, 2 * BAND), lambda g, h, b: (g * HP + h, 0, 0))],
        out_shape=[jax.ShapeDtypeStruct((T, W3), F32), jax.ShapeDtypeStruct((T, W3), F32),
                   jax.ShapeDtypeStruct((T, W3), F32), jax.ShapeDtypeStruct((3 * hpg, BAND, 2 * BAND), F32)],
        scratch_shapes=[pltpu.VMEM((mmax, 2 * HEAD_DIM), F32)] * 3,
        compiler_params=_params(("arbitrary", "arbitrary", "arbitrary")),
    )(q, kv, kv, do, o, lse, bias)


def _bias_grad(ds_sum, hpg, *, name):
    nh = ds_sum.shape[0]
    idx = np.stack([np.where(_band_tables(dil)[1], _band_tables(dil)[0], -1) for dil in DILATIONS]).astype(np.int32)

    def body(ds_ref, idx_ref, o_ref):
        d = ds_ref[...]
        ix = idx_ref[...]
        lane = lax.broadcasted_iota(jnp.int32, (8, 128), 1)
        row = jnp.zeros((8, 128), F32)
        for b in range(REL_BUCKETS):
            row = row + jnp.where(lane == b, jnp.sum(jnp.where(ix == b, d, 0.0)), 0.0)
        o_ref[...] = row

    out = _pallas(
        body, name=name, grid=(nh,),
        in_specs=[pl.BlockSpec((None, BAND, 2 * BAND), lambda h: (h, 0, 0)),
                  pl.BlockSpec((None, BAND, 2 * BAND), lambda h: (h // hpg, 0, 0))],
        out_specs=pl.BlockSpec((None, 8, 128), lambda h: (h, 0, 0)),
        out_shape=jax.ShapeDtypeStruct((nh, 8, 128), F32),
        compiler_params=_params(("parallel",)),
    )(ds_sum, jnp.asarray(idx))
    return out[:, 0, :REL_BUCKETS].T


def _adamw(w, g, m, v, *, name):
    Rw, C = w.shape
    tm = _pick(Rw, (512, 352, 256, 128, 64, 32, 16, 8))

    def body(w_ref, g_ref, m_ref, v_ref, d_ref, nm_ref, nv_ref):
        gg = g_ref[...]
        nm = ADAM_B1 * m_ref[...] + (1.0 - ADAM_B1) * gg
        nv = ADAM_B2 * v_ref[...] + (1.0 - ADAM_B2) * (gg * gg)
        m_hat = nm / (1.0 - ADAM_B1 ** ADAM_STEP)
        v_hat = nv / (1.0 - ADAM_B2 ** ADAM_STEP)
        d_ref[...] = -ADAM_LR * (m_hat / (jnp.sqrt(v_hat) + ADAM_EPS) + ADAM_WD * w_ref[...])
        nm_ref[...] = nm
        nv_ref[...] = nv

    return _pallas(
        body, name=name, grid=(Rw // tm,), in_specs=[_rows(tm, C)] * 4, out_specs=[_rows(tm, C)] * 3,
        out_shape=[jax.ShapeDtypeStruct((Rw, C), F32)] * 3, compiler_params=_params(("parallel",)),
    )(w, g, m, v)


ROW_TILE_ELEMS = 256 * 1024


def _tile_rows(r, c):
    best = 8
    for t in range(8, r + 1, 8):
        if r % t == 0 and t * c <= ROW_TILE_ELEMS:
            best = t
    return best


def _adamw_halves(w, m, v, mine, other, cidx, *, layer=0, prev=None, name):
    NL, _, r, c = w.shape
    tm = _tile_rows(r, c)

    def body(c_ref, w_ref, m_ref, v_ref, a_ref, b_ref, *rest):
        g_ref, d_ref, nm_ref, nv_ref = rest[-4:]
        gg = jnp.where(pl.program_id(0) == c_ref[0], a_ref[...], b_ref[...])
        nm = ADAM_B1 * m_ref[...] + (1.0 - ADAM_B1) * gg
        nv = ADAM_B2 * v_ref[...] + (1.0 - ADAM_B2) * (gg * gg)
        m_hat = nm / (1.0 - ADAM_B1 ** ADAM_STEP)
        v_hat = nv / (1.0 - ADAM_B2 ** ADAM_STEP)
        g_ref[...] = gg
        d_ref[...] = -ADAM_LR * (m_hat / (jnp.sqrt(v_hat) + ADAM_EPS) + ADAM_WD * w_ref[...])
        nm_ref[...] = nm
        nv_ref[...] = nv

    half = pl.BlockSpec((None, None, tm, c), lambda h, i, cr: (layer, h, i, 0))
    one = pl.BlockSpec((None, tm, c), lambda h, i, cr: (0, i, 0))
    in_specs = [half, half, half, one, one]
    args = [cidx, w, m, v, mine, other]
    aliases = {}
    if prev is not None:
        in_specs += [_ANY] * 4
        args += list(prev)
        aliases = {6 + k: k for k in range(4)}
    spec = pltpu.PrefetchScalarGridSpec(num_scalar_prefetch=1, grid=(2, r // tm), in_specs=in_specs, out_specs=[half] * 4)
    return _pallas(
        body, name=name, grid_spec=spec, out_shape=[jax.ShapeDtypeStruct((NL, 2, r, c), F32)] * 4,
        input_output_aliases=aliases, compiler_params=_params(("parallel", "parallel")),
    )(*args)


def _pair_sum(g, theirs, cidx, *, cast, name):
    _, _, r, c = g.shape
    tm = _tile_rows(r, c)

    def body(c_ref, g_ref, t_ref, *outs):
        s = g_ref[...] + t_ref[...]
        outs[0][...] = s
        if cast:
            outs[1][...] = s.astype(BF16)

    blk = (None, None, tm, c)
    first = pl.BlockSpec(blk, lambda p, i, cr: (p, 0, i, 0))
    shapes = [jax.ShapeDtypeStruct((4, 1, r, c), F32)] + ([jax.ShapeDtypeStruct((4, 1, r, c), BF16)] if cast else [])
    spec = pltpu.PrefetchScalarGridSpec(
        num_scalar_prefetch=1, grid=(4, r // tm),
        in_specs=[pl.BlockSpec(blk, lambda p, i, cr: (p, cr[0], i, 0)), first], out_specs=[first] * len(shapes))
    return _pallas(body, name=name, grid_spec=spec, out_shape=shapes,
                   compiler_params=_params(("parallel", "parallel")))(cidx, g, theirs)


def _chip_sum(hf, got, chip_idx, *, name):
    _, _, r, c = hf.shape
    tm = _tile_rows(r, c)

    def body(p_ref, h_ref, r_ref, o_ref):
        s = h_ref[...]
        for k in range(3):
            s = s + r_ref[k].astype(F32)
        o_ref[...] = s

    spec = pltpu.PrefetchScalarGridSpec(
        num_scalar_prefetch=1, grid=(r // tm,),
        in_specs=[pl.BlockSpec((None, None, tm, c), lambda i, pr: (pr[0], 0, i, 0)),
                  pl.BlockSpec((3, None, tm, c), lambda i, pr: (0, 0, i, 0))],
        out_specs=pl.BlockSpec((None, tm, c), lambda i, pr: (0, i, 0)))
    return _pallas(body, name=name, grid_spec=spec, out_shape=jax.ShapeDtypeStruct((1, r, c), F32),
                   compiler_params=_params(("parallel",)))(chip_idx, hf, got)


def _place():
    x, y, c = lax.axis_index("x"), lax.axis_index("y"), lax.axis_index("c")
    chips = [(1 - x, y), (x, 1 - y), (1 - x, 1 - y)]
    return x, y, c, chips


_ANY = pl.BlockSpec(memory_space=pl.ANY)


def _comm_call(body, ins, out_shapes, n_remote, *, name, aliases=None):
    sems = [pltpu.SemaphoreType.DMA((n,)) for n in n_remote]
    return _pallas(
        body, name=name, in_specs=[_ANY] * len(ins), out_specs=[_ANY] * len(out_shapes), out_shape=out_shapes,
        scratch_shapes=sems, input_output_aliases=aliases or {},
        compiler_params=pltpu.CompilerParams(has_side_effects=True),
    )(*ins)


_HBM_SPEC = pl.BlockSpec(memory_space=pltpu.HBM)
_SEM_SPEC = pl.BlockSpec(memory_space=pltpu.SEMAPHORE)
_DATAFLOW = pltpu.SideEffectType.DATAFLOW_SIDE_EFFECTING


def _in_hbm(a):
    return pltpu.with_memory_space_constraint(a, pltpu.HBM)


def _gather_start(groups, *, name):
    flat = [s for g in groups for s in g]
    n, ng = len(flat), len(groups)

    def body(*refs):
        ins, lands = refs[:n], refs[n:2 * n]
        sems = refs[2 * n:2 * n + 2 * ng]
        token = refs[-1]
        x, y, c, chips = _place()
        me = 2 * x + y
        a = 0
        for gi, g in enumerate(groups):
            for j in range(len(g)):
                for k, (tx, ty) in enumerate(chips):
                    _rcopy(ins[a].at[c], lands[a].at[me, c], sems[2 * gi].at[3 * j + k], sems[2 * gi + 1].at[3 * j + k],
                           (tx, ty, c)).start()
                a += 1
        token[...] = jnp.zeros_like(token)

    land_shapes = [(4,) + s.shape for s in flat]
    out_shape = ([pltpu.SemaphoreType.DMA((3 * len(g),)) for g in groups for _ in range(2)]
                 + [pltpu.HBM(s.shape, s.dtype) for s in flat]
                 + [pltpu.HBM(ls, s.dtype) for ls, s in zip(land_shapes, flat)]
                 + [jax.ShapeDtypeStruct((8, 128), F32)])
    outs = _pallas(
        body, name=name, in_specs=[_HBM_SPEC] * (2 * n),
        out_specs=[_SEM_SPEC] * (2 * ng) + [_HBM_SPEC] * (2 * n) + [pl.BlockSpec(memory_space=pltpu.VMEM)],
        out_shape=out_shape, input_output_aliases={i: 2 * ng + i for i in range(2 * n)},
        compiler_params=pltpu.CompilerParams(has_side_effects=_DATAFLOW),
    )(*[_in_hbm(s) for s in flat], *[_in_hbm(lax.empty(ls, s.dtype)) for ls, s in zip(land_shapes, flat)])
    sems, thru, lands, token = outs[:2 * ng], outs[2 * ng:2 * ng + n], outs[2 * ng + n:2 * ng + 2 * n], outs[-1]
    res, a = [], 0
    for gi, g in enumerate(groups):
        res.append((sems[2 * gi], sems[2 * gi + 1], thru[a:a + len(g)], lands[a:a + len(g)]))
        a += len(g)
    return res, token


def _gather_wait(ssem, rsem, shards, lands, after, *, name):
    m = len(shards)

    def body(*refs):
        ins, lnd = refs[:m], refs[m:2 * m]
        ss, rs = refs[2 * m], refs[2 * m + 1]
        x, y, c, chips = _place()
        for j in range(m):
            for k, (tx, ty) in enumerate(chips):
                cp = _rcopy(ins[j].at[c], lnd[j].at[2 * tx + ty, c], ss.at[3 * j + k], rs.at[3 * j + k], (tx, ty, c))
                cp.wait_send()
                cp.wait_recv()

    outs = _pallas(
        body, name=name, in_specs=[_HBM_SPEC] * (2 * m) + [_SEM_SPEC, _SEM_SPEC, _ANY],
        out_specs=[_HBM_SPEC] * (2 * m),
        out_shape=[pltpu.HBM(s.shape, s.dtype) for s in shards] + [pltpu.HBM(l.shape, l.dtype) for l in lands],
        input_output_aliases={i: i for i in range(2 * m)},
        compiler_params=pltpu.CompilerParams(has_side_effects=_DATAFLOW),
    )(*shards, *lands, ssem, rsem, after)
    return outs[m:]


def _gather_forward(lands, *, name):
    n = len(lands)

    def body(*refs):
        outs = refs[n:2 * n]
        ssem, rsem = refs[2 * n:]
        x, y, c, chips = _place()
        sib = (x, y, 1 - c)
        cps = []
        for a in range(n):
            for k, (tx, ty) in enumerate(chips):
                pk = 2 * tx + ty
                cp = _rcopy(outs[a].at[pk, c], outs[a].at[pk, c], ssem.at[3 * a + k], rsem.at[3 * a + k], sib)
                cp.start()
                cps.append(cp)
        for a in range(n):
            for k, (tx, ty) in enumerate(chips):
                pk = 2 * tx + ty
                _rcopy(outs[a].at[pk, c], outs[a].at[pk, 1 - c], ssem.at[3 * a + k], rsem.at[3 * a + k], sib).wait_recv()
        for cp in cps:
            cp.wait_send()

    shapes = [jax.ShapeDtypeStruct(l.shape, l.dtype) for l in lands]
    return _comm_call(body, lands, shapes, [3 * n, 3 * n], name=name, aliases={i: i for i in range(n)})


def _gather_forward_start(lands, *, name):
    n = len(lands)

    def body(*refs):
        ssem, rsem = refs[n], refs[n + 1]
        outs = refs[n + 2:2 * n + 2]
        token = refs[-1]
        x, y, c, chips = _place()
        for a in range(n):
            for k, (tx, ty) in enumerate(chips):
                pk = 2 * tx + ty
                _rcopy(outs[a].at[pk, c], outs[a].at[pk, c], ssem.at[3 * a + k], rsem.at[3 * a + k], (x, y, 1 - c)).start()
        token[...] = jnp.zeros_like(token)

    outs = _pallas(
        body, name=name, in_specs=[_HBM_SPEC] * n,
        out_specs=[_SEM_SPEC] * 2 + [_HBM_SPEC] * n + [pl.BlockSpec(memory_space=pltpu.VMEM)],
        out_shape=([pltpu.SemaphoreType.DMA((3 * n,))] * 2 + [pltpu.HBM(l.shape, l.dtype) for l in lands]
                   + [jax.ShapeDtypeStruct((8, 128), F32)]),
        input_output_aliases={i: 2 + i for i in range(n)},
        compiler_params=pltpu.CompilerParams(has_side_effects=_DATAFLOW),
    )(*lands)
    return (outs[0], outs[1], outs[2:2 + n]), outs[-1]


def _gather_forward_wait(started, after, *, name):
    ssem, rsem, lands = started
    n = len(lands)

    def body(*refs):
        lnd = refs[:n]
        ss, rs = refs[n], refs[n + 1]
        x, y, c, chips = _place()
        sib = (x, y, 1 - c)
        for a in range(n):
            for k, (tx, ty) in enumerate(chips):
                pk = 2 * tx + ty
                _rcopy(lnd[a].at[pk, c], lnd[a].at[pk, 1 - c], ss.at[3 * a + k], rs.at[3 * a + k], sib).wait_recv()
                _rcopy(lnd[a].at[pk, c], lnd[a].at[pk, c], ss.at[3 * a + k], rs.at[3 * a + k], sib).wait_send()

    return _pallas(
        body, name=name, in_specs=[_HBM_SPEC] * n + [_SEM_SPEC, _SEM_SPEC, _ANY], out_specs=[_HBM_SPEC] * n,
        out_shape=[pltpu.HBM(l.shape, l.dtype) for l in lands], input_output_aliases={i: i for i in range(n)},
        compiler_params=pltpu.CompilerParams(has_side_effects=_DATAFLOW),
    )(*lands, ssem, rsem, after)


class _Lazy:
    def __init__(self, group_of, make, prepare):
        self._group_of, self._make, self._prepare, self._done, self._anchor = group_of, make, prepare, {}, None

    def anchor(self, value):
        self._anchor = value

    def prepare(self, key, value):
        return self._prepare(self._group_of[key], value)

    def __getitem__(self, key):
        g = self._group_of[key]
        if g not in self._done:
            self._done[g] = self._make(g, self._anchor)
        return self._done[g][key]


def _anchor(mapping, value):
    if isinstance(mapping, _Lazy):
        mapping.anchor(value)


def _prepare(mapping, key, value):
    return mapping.prepare(key, value)[0, 0] if isinstance(mapping, _Lazy) else 0.0


def _rcopy(src, dst, ssem, rsem, dev):
    return pltpu.make_async_remote_copy(src_ref=src, dst_ref=dst, send_sem=ssem, recv_sem=rsem,
                                        device_id=dev, device_id_type=MESH)


def _all_gather(shards, *, name):
    n = len(shards)

    def body(*refs):
        ins, outs = refs[:n], refs[n:2 * n]
        s_ici, r_ici, s_d2d, r_d2d = refs[2 * n:]
        x, y, c, chips = _place()
        me = 2 * x + y
        sib = (x, y, 1 - c)
        sends = []
        for a in range(n):
            for k, (tx, ty) in enumerate(chips):
                cp = _rcopy(ins[a].at[c], outs[a].at[me, c], s_ici.at[3 * a + k], r_ici.at[3 * a + k], (tx, ty, c))
                cp.start()
                sends.append(cp)
        for a in range(n):
            for k, (tx, ty) in enumerate(chips):
                pk = 2 * tx + ty
                _rcopy(ins[a].at[c], outs[a].at[pk, c], s_ici.at[3 * a + k], r_ici.at[3 * a + k], (tx, ty, c)).wait_recv()
                fw = _rcopy(outs[a].at[pk, c], outs[a].at[pk, c], s_d2d.at[3 * a + k], r_d2d.at[3 * a + k], sib)
                fw.start()
                sends.append(fw)
        for a in range(n):
            for k, (tx, ty) in enumerate(chips):
                pk = 2 * tx + ty
                _rcopy(ins[a].at[c], outs[a].at[pk, 1 - c], s_d2d.at[3 * a + k], r_d2d.at[3 * a + k], sib).wait_recv()
        for cp in sends:
            cp.wait_send()

    shapes = [jax.ShapeDtypeStruct((4,) + s.shape, s.dtype) for s in shards]
    return _comm_call(body, shards, shapes, [3 * n] * 4, name=name)


def _gather(shards, chip, *, name):
    outs = _all_gather(shards, name=name)
    return [lax.dynamic_update_slice(o, s[None], (chip, 0, 0, 0)) for o, s in zip(outs, shards)]


def _pair_send(gs, *, name):
    n = len(gs)

    def body(*refs):
        ins, theirs = refs[:n], refs[n:2 * n]
        ssem, rsem = refs[2 * n:]
        x, y, c, _ = _place()
        sib = (x, y, 1 - c)
        cps = []
        for a in range(n):
            cp = _rcopy(ins[a].at[:, pl.ds(1 - c, 1)], theirs[a], ssem.at[a], rsem.at[a], sib)
            cp.start()
            cps.append(cp)
        for cp in cps:
            cp.wait_send()
            cp.wait_recv()

    shapes = [jax.ShapeDtypeStruct((4, 1) + g.shape[2:], g.dtype) for g in gs]
    return _comm_call(body, gs, shapes, [n, n], name=name)


def _chip_exchange(hx, *, name):
    n = len(hx)

    def body(*refs):
        hxr, got = refs[:n], refs[n:2 * n]
        ssem, rsem = refs[2 * n:]
        x, y, c, chips = _place()
        cps = []
        for a in range(n):
            for k, (tx, ty) in enumerate(chips):
                cp = _rcopy(hxr[a].at[2 * tx + ty], got[a].at[k], ssem.at[3 * a + k], rsem.at[3 * a + k], (tx, ty, c))
                cp.start()
                cps.append(cp)
        for cp in cps:
            cp.wait_send()
            cp.wait_recv()

    shapes = [jax.ShapeDtypeStruct((3,) + h.shape[1:], h.dtype) for h in hx]
    return _comm_call(body, hx, shapes, [3 * n, 3 * n], name=name)


def _pair_swap(fs, *, name):
    n = len(fs)

    def body(*refs):
        ins, outs = refs[:n], refs[n:2 * n]
        ssem, rsem = refs[2 * n:]
        x, y, c, _ = _place()
        cps = []
        for a in range(n):
            cp = _rcopy(ins[a], outs[a], ssem.at[a], rsem.at[a], (x, y, 1 - c))
            cp.start()
            cps.append(cp)
        for cp in cps:
            cp.wait_send()
            cp.wait_recv()

    shapes = [jax.ShapeDtypeStruct(f.shape, f.dtype) for f in fs]
    return _comm_call(body, fs, shapes, [n, n], name=name)


def _chip_exchange_start(hx, *, name):
    n = len(hx)

    def body(*refs):
        ins, gots = refs[:n], refs[n:2 * n]
        ssem, rsem = refs[2 * n], refs[2 * n + 1]
        token = refs[-1]
        x, y, c, chips = _place()
        for a in range(n):
            for k, (tx, ty) in enumerate(chips):
                _rcopy(ins[a].at[2 * tx + ty], gots[a].at[k], ssem.at[3 * a + k], rsem.at[3 * a + k], (tx, ty, c)).start()
        token[...] = jnp.zeros_like(token)

    got_shapes = [(3,) + h.shape[1:] for h in hx]
    outs = _pallas(
        body, name=name, in_specs=[_HBM_SPEC] * (2 * n),
        out_specs=[_SEM_SPEC] * 2 + [_HBM_SPEC] * (2 * n) + [pl.BlockSpec(memory_space=pltpu.VMEM)],
        out_shape=([pltpu.SemaphoreType.DMA((3 * n,))] * 2 + [pltpu.HBM(h.shape, h.dtype) for h in hx]
                   + [pltpu.HBM(gs, h.dtype) for gs, h in zip(got_shapes, hx)] + [jax.ShapeDtypeStruct((8, 128), F32)]),
        input_output_aliases={i: 2 + i for i in range(2 * n)},
        compiler_params=pltpu.CompilerParams(has_side_effects=_DATAFLOW),
    )(*[_in_hbm(h) for h in hx], *[_in_hbm(lax.empty(gs, h.dtype)) for gs, h in zip(got_shapes, hx)])
    return (outs[0], outs[1], outs[2:2 + n], outs[2 + n:2 + 2 * n]), outs[-1]


def _chip_exchange_wait(started, after, *, name):
    ssem, rsem, hx, gots = started
    n = len(hx)

    def body(*refs):
        ins, gts = refs[:n], refs[n:2 * n]
        ss, rs = refs[2 * n], refs[2 * n + 1]
        x, y, c, chips = _place()
        for a in range(n):
            for k, (tx, ty) in enumerate(chips):
                cp = _rcopy(ins[a].at[2 * tx + ty], gts[a].at[k], ss.at[3 * a + k], rs.at[3 * a + k], (tx, ty, c))
                cp.wait_send()
                cp.wait_recv()

    outs = _pallas(
        body, name=name, in_specs=[_HBM_SPEC] * (2 * n) + [_SEM_SPEC, _SEM_SPEC, _ANY],
        out_specs=[_HBM_SPEC] * (2 * n),
        out_shape=[pltpu.HBM(h.shape, h.dtype) for h in hx] + [pltpu.HBM(g.shape, g.dtype) for g in gots],
        input_output_aliases={i: i for i in range(2 * n)},
        compiler_params=pltpu.CompilerParams(has_side_effects=_DATAFLOW),
    )(*hx, *gots, ssem, rsem, after)
    return outs[n:]


def _sent_part(ref, c, whole):
    return ref if whole else ref.at[:, pl.ds(1 - c, 1)]


def _pair_send_start(gs, *, name, whole=False, after=None):
    n = len(gs)
    afters = [] if after is None else [after]

    def body(*refs):
        ins, lands = refs[:n], refs[n:2 * n]
        ssem, rsem = refs[2 * n + len(afters)], refs[2 * n + len(afters) + 1]
        token = refs[-1]
        x, y, c, _ = _place()
        for a in range(n):
            _rcopy(_sent_part(ins[a], c, whole), lands[a], ssem.at[a], rsem.at[a], (x, y, 1 - c)).start()
        token[...] = jnp.zeros_like(token)

    land_shapes = [g.shape if whole else (4, 1) + g.shape[2:] for g in gs]
    outs = _pallas(
        body, name=name, in_specs=[_HBM_SPEC] * (2 * n) + [_ANY] * len(afters),
        out_specs=[_SEM_SPEC] * 2 + [_HBM_SPEC] * (2 * n) + [pl.BlockSpec(memory_space=pltpu.VMEM)],
        out_shape=([pltpu.SemaphoreType.DMA((n,))] * 2 + [pltpu.HBM(g.shape, g.dtype) for g in gs]
                   + [pltpu.HBM(ls, g.dtype) for ls, g in zip(land_shapes, gs)] + [jax.ShapeDtypeStruct((8, 128), F32)]),
        input_output_aliases={i: 2 + i for i in range(2 * n)},
        compiler_params=pltpu.CompilerParams(has_side_effects=_DATAFLOW),
    )(*[_in_hbm(g) for g in gs], *[_in_hbm(lax.empty(ls, g.dtype)) for ls, g in zip(land_shapes, gs)], *afters)
    return (outs[0], outs[1], outs[2:2 + n], outs[2 + n:2 + 2 * n]), outs[-1]


def _pair_send_wait(started, after, *, name, whole=False):
    ssem, rsem, gs, lands = started
    n = len(gs)
    afters = list(after) if isinstance(after, (list, tuple)) else [after]

    def body(*refs):
        ins, lnd = refs[:n], refs[n:2 * n]
        ss, rs = refs[2 * n], refs[2 * n + 1]
        x, y, c, _ = _place()
        for a in range(n):
            cp = _rcopy(_sent_part(ins[a], c, whole), lnd[a], ss.at[a], rs.at[a], (x, y, 1 - c))
            cp.wait_send()
            cp.wait_recv()

    outs = _pallas(
        body, name=name, in_specs=[_HBM_SPEC] * (2 * n) + [_SEM_SPEC, _SEM_SPEC] + [_ANY] * len(afters),
        out_specs=[_HBM_SPEC] * (2 * n),
        out_shape=[pltpu.HBM(g.shape, g.dtype) for g in gs] + [pltpu.HBM(l.shape, l.dtype) for l in lands],
        input_output_aliases={i: i for i in range(2 * n)},
        compiler_params=pltpu.CompilerParams(has_side_effects=_DATAFLOW),
    )(*gs, *lands, ssem, rsem, *afters)
    return list(outs[:n]), list(outs[n:])


def _pair_sums(grads, exch_bf16, cidx, tag, theirs=None):
    if theirs is None:
        theirs = _pair_send(grads, name=f"rs_pair_send_{tag}")
    hf, hx = [], []
    for a in range(len(grads)):
        res = _pair_sum(grads[a], theirs[a], cidx, cast=exch_bf16[a], name=f"rs_pair_sum_{tag}{a}")
        hf.append(res[0])
        hx.append(res[1] if exch_bf16[a] else res[0])
    return hf, hx


def _chip_sums(hf, got, chip_idx, tag):
    return [_chip_sum(hf[a], got[a], chip_idx, name=f"rs_chip_sum_{tag}{a}") for a in range(len(hf))]


def _interleave(a, B, L):
    return a.reshape(B, L, -1).transpose(1, 0, 2).reshape(B * L, -1)


def _deinterleave(a, B, L):
    return a.reshape(L, B, -1).transpose(1, 0, 2).reshape(B * L, -1)


def _local_step(x, tgt, W, S, on_grads=None):
    B, L, D = x.shape
    T = B * L
    G = D // SSM_GROUP
    Pst = SSM_STATE
    hpg = D // HEAD_DIM
    HW = hpg * HEAD_DIM
    ncl = G // GROUPS_PER_CLUSTER
    x2 = x.reshape(T, D)
    tgt2 = tgt.reshape(T, D)

    disc = lambda *p: _s5_discretize(*p)
    (ab_r, ab_i, bb_r, bb_i), disc_vjp = jax.vjp(disc, S["lam_re"], S["lam_im"], S["log_dt"], S["b_re"], S["b_im"])
    wb = jnp.concatenate([_blockdiag(jnp.transpose(bb_r, (0, 2, 1))), _blockdiag(jnp.transpose(bb_i, (0, 2, 1)))],
                         axis=-1).astype(MXU_DTYPE)
    wc = jnp.concatenate([_blockdiag(jnp.transpose(S["c_re"], (0, 2, 1))), _blockdiag(-jnp.transpose(S["c_im"], (0, 2, 1)))],
                         axis=1).astype(MXU_DTYPE)
    cs = GROUPS_PER_CLUSTER * Pst
    slab = lambda ab: jnp.tile(jnp.transpose(ab.reshape(ncl, cs // LANES, LANES), (1, 0, 2)), (1, B, 1))
    a_r, a_i = slab(ab_r), slab(ab_i)
    d_row = S["d"].reshape(1, D)

    xi = _interleave(x2, B, L)
    y, yg, h_r, h_i = _s5_fwd(xi, wb, wc, a_r, a_i, d_row, B, name="s5_fwd")
    _anchor(W, yg)
    z = _mm_nn(yg, W["w_glu"], bias=S["b_glu"].reshape(1, D), name="glu_z")
    gate = _glu_gate(y, z, name="glu_gate")
    mix_i = _mm_nn(gate, W["w_out"], name="s5_out")
    tok = _prepare(W, "w_up", mix_i)
    mix = _deinterleave(mix_i, B, L)
    h1, h1b, xh1, rs1 = _ln_fwd(x2, mix, S["ln_gain"][0, 0][None] + tok, S["ln_bias"][0, 0][None], name="ln_fwd_0a")

    def ffn_fwd(hb, l, prepare=None):
        hc = _mm_nn(hb, W["w_up"], l=l, out_dtype=MXU_DTYPE, name=f"ffn_up_{l}")
        tok = _prepare(W, prepare, hc) if prepare else 0.0
        a = _conv_glu_fwd(hc, S["conv_w"][l], S["conv_b"][l][None] + tok, L, name=f"ffn_conv_{l}")
        f = _mm_nn(a, W["w_down"], l=l, name=f"ffn_down_{l}")
        return hc, a, f

    _anchor(W, h1b)
    hc0, a0, f0 = ffn_fwd(h1b, 0, prepare="w_kv")
    h2, h2b, xh2, rs2 = _ln_fwd(h1, f0, S["ln_gain"][0, 1][None], S["ln_bias"][0, 1][None], name="ln_fwd_0b")

    _anchor(W, h2b)
    kv = _mm_nn(h2b, W["w_kv"], name="attn_kv")
    q = _mm_nn(h2b, W["w_q"], name="attn_q")
    bias = _attn_bias(S["rel_bias"], hpg)
    o3, l3 = _attn_fwd(q, kv, bias, L, hpg, name="attn_fwd")
    o, ob, lse = _attn_merge(o3, l3, HW, name="attn_merge")
    att = _mm_nn(ob, W["w_ao"], name="attn_out")
    h3, h3b, xh3, rs3 = _ln_fwd(h2, att, S["ln_gain"][1, 0][None], S["ln_bias"][1, 0][None], name="ln_fwd_1a")
    hc1, a1, f1 = ffn_fwd(h3b, 1)
    h4, _, xh4, rs4 = _ln_fwd(h3, f1, S["ln_gain"][1, 1][None], S["ln_bias"][1, 1][None], name="ln_fwd_1b")

    dh4, lrow = _loss_grad(h4, tgt2, name="loss")
    loss = lrow[0, 0]

    GW, GS = {}, {}

    def ffn_bwd(dzb, hb, hc, a, l):
        da = _mm_nt(dzb, W["w_down"], l=l, out_dtype=MXU_DTYPE, name=f"ffn_down_bwd_x_{l}")
        GW[f"w_down{l}"] = _tn(a, dzb, ptotal=1, np_cols=D, name=f"ffn_down_bwd_w_{l}")
        dc, dcw, dcb = _conv_glu_bwd(hc, da, S["conv_w"][l], S["conv_b"][l][None], L, name=f"ffn_conv_bwd_{l}")
        dhc = _conv_bwd_input(dc, S["conv_w"][l], L, name=f"ffn_conv_bwd_x_{l}")
        dh = _mm_nt(dhc, W["w_up"], l=l, name=f"ffn_up_bwd_x_{l}")
        GW[f"w_up{l}"] = _tn(hb, dhc, ptotal=W["w_up"].shape[0], np_cols=W["w_up"].shape[3], name=f"ffn_up_bwd_w_{l}")
        return dh, dcw, dcb

    dz4, dz4b, dg4, db4 = _ln_bwd([dh4], [1.0], xh4, rs4, S["ln_gain"][1, 1][None], name="ln_bwd_1b")
    dh3f, dcw1, dcb1 = ffn_bwd(dz4b, h3b, hc1, a1, 1)
    dz3, dz3b, dg3, db3 = _ln_bwd([dz4, dh3f], [DN_ALPHA, 1.0], xh3, rs3, S["ln_gain"][1, 0][None], name="ln_bwd_1a")
    do = _mm_nt(dz3b, W["w_ao"], name="attn_out_bwd_x")
    GW["w_ao"] = _tn(ob, dz3b, ptotal=1, np_cols=D, name="attn_out_bwd_w")
    dq, dk, dv, ds_sum = _attn_bwd(q, kv, do, o, lse, bias, L, hpg, name="attn_bwd")
    GS["rel_bias"] = _bias_grad(ds_sum, hpg, name="attn_bias_grad")
    GW["w_q"] = _tn(h2b, dq, ptotal=W["w_q"].shape[0], np_cols=W["w_q"].shape[3], name="attn_q_bwd_w")
    pkv, npkv = W["w_kv"].shape[0], W["w_kv"].shape[3]
    gkv = _tn(h2b, dk, ptotal=pkv, np_cols=npkv, p0=0, name="attn_k_bwd_w")
    GW["w_kv"] = _tn(h2b, dv, ptotal=pkv, np_cols=npkv, p0=pkv // 2, prev=gkv, name="attn_v_bwd_w")
    dh2q = _mm_nt(dq, W["w_q"], name="attn_q_bwd_x")
    dh2k = _mm_nt(dk, W["w_kv"], p0=0, pn=pkv // 2, name="attn_k_bwd_x")
    dh2v = _mm_nt(dv, W["w_kv"], p0=pkv // 2, pn=pkv // 2, name="attn_v_bwd_x")

    gain_0b = S["ln_gain"][0, 1][None]
    if on_grads is not None:
        gain_0b = gain_0b + on_grads(0, GW, dh2v)[0, 0]

    dz2, dz2b, dg2, db2 = _ln_bwd([dz3, dh2q, dh2k, dh2v], [DN_ALPHA, 1.0, 1.0, 1.0], xh2, rs2, gain_0b,
                                  name="ln_bwd_0b")
    dh1f, dcw0, dcb0 = ffn_bwd(dz2b, h1b, hc0, a0, 0)
    gain_0a = S["ln_gain"][0, 0][None]
    if on_grads is not None:
        gain_0a = gain_0a + on_grads(1, GW, GW["w_up0"])[0, 0]
    dz1, dz1b, dg1, db1 = _ln_bwd([dz2, dh1f], [DN_ALPHA, 1.0], xh1, rs1, gain_0a, name="ln_bwd_0a")
    dmix_i = _interleave(dz1b, B, L)
    dgate = _mm_nt(dmix_i, W["w_out"], name="s5_out_bwd_x")
    GW["w_out"] = _tn(gate, dmix_i, ptotal=1, np_cols=D, name="s5_out_bwd_w")
    dzg, dyg1, dbglu = _glu_bwd(y, z, dgate, name="glu_bwd")
    dyg2 = _mm_nt(dzg, W["w_glu"], name="glu_z_bwd_x")
    GW["w_glu"] = _tn(yg, dzg, ptotal=1, np_cols=D, name="glu_z_bwd_w")
    dy = _gelu_bwd(y, dyg1, dyg2, name="gelu_bwd")
    if on_grads is not None:
        d_row = d_row + on_grads(2, GW, GW["w_glu"])[0, 0]
    du_i, g_r, g_i, dar, dai, dd = _s5_bwd(dy, xi, h_r, h_i, wb, wc, a_r, a_i, d_row, B, name="s5_bwd")
    started = on_grads(3, GW, du_i) if on_grads is not None else None
    dwb_r = _cluster_tn(xi, g_r, ncl, tok_left=True, name="s5_b_grad_re", after=started)
    dwb_i = _cluster_tn(xi, g_i, ncl, tok_left=True, name="s5_b_grad_im")
    dwc_r = _cluster_tn(dy, h_r, ncl, tok_left=False, name="s5_c_grad_re")
    dwc_i = _cluster_tn(dy, h_i, ncl, tok_left=False, name="s5_c_grad_im")
    grad_x = _axpy(dz1, _deinterleave(du_i, B, L), DN_ALPHA, name="grad_x")

    dbb_r = jnp.transpose(_unblockdiag(dwb_r, SSM_GROUP, Pst), (0, 2, 1))
    dbb_i = jnp.transpose(_unblockdiag(dwb_i, SSM_GROUP, Pst), (0, 2, 1))
    unslab = lambda da: jnp.transpose(da.reshape(cs // LANES, B, ncl, LANES).sum(1), (1, 0, 2)).reshape(G, Pst)
    dab_r, dab_i = unslab(dar), unslab(dai)
    GS["lam_re"], GS["lam_im"], GS["log_dt"], GS["b_re"], GS["b_im"] = disc_vjp((dab_r, dab_i, dbb_r, dbb_i))
    GS["c_re"] = jnp.transpose(_unblockdiag(dwc_r, Pst, SSM_GROUP), (0, 2, 1))
    GS["c_im"] = -jnp.transpose(_unblockdiag(dwc_i, Pst, SSM_GROUP), (0, 2, 1))
    GS["d"] = dd.reshape(G, SSM_GROUP)
    GS["b_glu"] = dbglu.reshape(D)
    GS["conv_w"] = jnp.stack([dcw0, dcw1])
    GS["conv_b"] = jnp.stack([dcb0[0], dcb1[0]])
    GS["ln_gain"] = jnp.stack([jnp.stack([dg1[0], dg2[0]]), jnp.stack([dg3[0], dg4[0]])])
    GS["ln_bias"] = jnp.stack([jnp.stack([db1[0], db2[0]]), jnp.stack([db3[0], db4[0]])])
    return loss, grad_x.reshape(B, L, D), GW, GS


SMALL_REPLICATED = ("lam_re", "lam_im", "log_dt", "b_re", "b_im", "c_re", "c_im", "d", "rel_bias", "conv_b")
SMALL_SHARDED = ("b_glu", "conv_w", "ln_gain", "ln_bias")
SMALL_ORDER = SMALL_REPLICATED + SMALL_SHARDED


def _pack(arrs, lanes, row_mult):
    flat = jnp.concatenate([a.reshape(-1).astype(F32) for a in arrs])
    rows = -(-flat.shape[0] // lanes)
    rows = -(-rows // row_mult) * row_mult
    return jnp.pad(flat, (0, rows * lanes - flat.shape[0])).reshape(rows, lanes)


def _unpack(packed, shapes):
    flat = packed.reshape(-1)
    out, off = [], 0
    for s in shapes:
        n = int(np.prod(s))
        out.append(flat[off:off + n].reshape(s))
        off += n
    return out


def kernel(x, s5_lam_re, s5_lam_im, s5_log_dt, s5_b_re, s5_b_im, s5_c_re, s5_c_im, s5_d, s5_w_glu, s5_b_glu, s5_w_out, attn_w_kv, attn_w_q, attn_w_out, rel_bias, ffn_w_up, ffn_conv_w, ffn_conv_b, ffn_w_down, ln_gain, ln_bias, loss_target, m_s5_lam_re, m_s5_lam_im, m_s5_log_dt, m_s5_b_re, m_s5_b_im, m_s5_c_re, m_s5_c_im, m_s5_d, m_s5_w_glu, m_s5_b_glu, m_s5_w_out, m_attn_w_kv, m_attn_w_q, m_attn_w_out, m_rel_bias, m_ffn_w_up, m_ffn_conv_w, m_ffn_conv_b, m_ffn_w_down, m_ln_gain, m_ln_bias, v_s5_lam_re, v_s5_lam_im, v_s5_log_dt, v_s5_b_re, v_s5_b_im, v_s5_c_re, v_s5_c_im, v_s5_d, v_s5_w_glu, v_s5_b_glu, v_s5_w_out, v_attn_w_kv, v_attn_w_q, v_attn_w_out, v_rel_bias, v_ffn_w_up, v_ffn_conv_w, v_ffn_conv_b, v_ffn_w_down, v_ln_gain, v_ln_bias):
    names = ["s5_lam_re", "s5_lam_im", "s5_log_dt", "s5_b_re", "s5_b_im", "s5_c_re", "s5_c_im", "s5_d", "s5_w_glu",
             "s5_b_glu", "s5_w_out", "attn_w_kv", "attn_w_q", "attn_w_out", "rel_bias", "ffn_w_up", "ffn_conv_w",
             "ffn_conv_b", "ffn_w_down", "ln_gain", "ln_bias"]
    loc = locals()
    w_in = {n: loc[n] for n in names}
    m_in = {n: loc["m_" + n] for n in names}
    v_in = {n: loc["v_" + n] for n in names}
    chip = 2 * lax.axis_index("x") + lax.axis_index("y")
    core = lax.axis_index("c")
    chip_idx = jnp.reshape(chip, (1,)).astype(jnp.int32)
    cidx = jnp.reshape(core, (1,)).astype(jnp.int32)

    big = [("w_glu", "s5_w_glu", "rows"), ("w_out", "s5_w_out", "rows"), ("w_ao", "attn_w_out", "rows"),
           ("w_kv", "attn_w_kv", "cols"), ("w_q", "attn_w_q", "cols"),
           ("w_up", "ffn_w_up", "layer_cols"), ("w_down", "ffn_w_down", "layer_rows")]

    def halves(t, kind):
        if kind.startswith("layer"):
            return t
        r, c = t.shape[-2:]
        return t.reshape(2, r // 2, c)

    def to_weight(g, kind):
        _, _, r, c = g.shape
        if kind == "rows":
            return g.reshape(1, 1, 8 * r, c)
        if kind == "cols":
            return g.reshape(4, 1, 2 * r, c)
        if kind == "layer_cols":
            return g
        return jnp.transpose(g, (1, 0, 2, 3)).reshape(1, 2, 4 * r, c)

    small_sh = {"b_glu": s5_b_glu[0], "conv_w": ffn_conv_w, "ln_gain": ln_gain, "ln_bias": ln_bias}
    sh_shapes = [small_sh[k].shape for k in SMALL_SHARDED]
    sh_pack = _pack([small_sh[k] for k in SMALL_SHARDED], 128, 16)

    shards = [halves(w_in[src].astype(MXU_DTYPE), kind) for _, src, kind in big]
    shards.append(sh_pack.reshape(2, sh_pack.shape[0] // 2, 128))
    shard_of = {key: s for (key, _, _), s in zip(big, shards)}
    shard_of["small"] = shards[-1]
    kind_of = {key: kind for key, _, kind in big}

    group_keys = [["w_glu", "w_out", "small"], ["w_up", "w_down"], ["w_kv", "w_q", "w_ao"]]
    started, token = _gather_start([[shard_of[k] for k in g] for g in group_keys], name="weights_gather_start")

    forwarding = {}

    def prepare_group(gi, after):
        ssem, rsem, thru, lands = started[gi]
        lands = _gather_wait(ssem, rsem, thru, lands, after, name=f"weights_gather_wait_{gi}")
        forwarding[gi], tok = _gather_forward_start(lands, name=f"weights_gather_forward_start_{gi}")
        return tok

    def finish_group(gi, after):
        if gi in forwarding:
            lands = _gather_forward_wait(forwarding.pop(gi), after, name=f"weights_gather_forward_wait_{gi}")
        else:
            ssem, rsem, thru, lands = started[gi]
            lands = _gather_wait(ssem, rsem, thru, lands, after, name=f"weights_gather_wait_{gi}")
            lands = _gather_forward(lands, name=f"weights_gather_forward_{gi}")
        out = {}
        for key, land in zip(group_keys[gi], lands):
            full = lax.dynamic_update_slice(land, shard_of[key][None], (chip, 0, 0, 0))
            if key == "small":
                parts = [_unpack(full[p], sh_shapes) for p in range(4)]
                for i, k in enumerate(SMALL_SHARDED):
                    out[k] = jnp.concatenate([parts[p][i] for p in range(4)], axis=-1)
            else:
                out[key] = to_weight(full, kind_of[key])
        return out

    replicated = dict(lam_re=s5_lam_re[0], lam_im=s5_lam_im[0], log_dt=s5_log_dt[0], b_re=s5_b_re[0], b_im=s5_b_im[0],
                      c_re=s5_c_re[0], c_im=s5_c_im[0], rel_bias=rel_bias, conv_b=ffn_conv_b,
                      d=s5_d[0] + token[0, 0])
    group_of = {k: gi for gi, g in enumerate(group_keys) for k in g if k != "small"}
    group_of.update({k: 0 for k in SMALL_SHARDED})
    group_of.update({k: "replicated" for k in replicated})
    params = _Lazy(group_of, lambda g, after: replicated if g == "replicated" else finish_group(g, after), prepare_group)

    red = [("w_up1", "ffn_w_up", 1), ("w_down1", "ffn_w_down", 1), ("w_ao", "attn_w_out", 0), ("w_kv", "attn_w_kv", 0),
           ("w_q", "attn_w_q", 0), ("w_down0", "ffn_w_down", 0), ("w_up0", "ffn_w_up", 0), ("w_out", "s5_w_out", 0),
           ("w_glu", "s5_w_glu", 0)]
    stages = [red[:5], red[5:7], red[7:]]

    def grad_halves(gw, key, src):
        r, c = w_in[src].shape[-2:]
        return gw[key].reshape(4, 2, r // 2, c)

    sent, early = {}, []

    def on_grads(stage, gw, latest):
        tokens = []
        if stage > 0:
            tag = "abc"[stage - 1]
            ga, theirs = _pair_send_wait(sent.pop(stage - 1), latest, name=f"rs_pair_send_wait_{tag}")
            hf, hx = _pair_sums(ga, [True] * len(ga), cidx, tag, theirs)
            started, tok = _chip_exchange_start(hx, name=f"rs_chip_exchange_start_{tag}")
            early.append((hf, started, tag))
            tokens.append(tok)
        if stage < len(stages):
            ga = [grad_halves(gw, key, src) for key, src, _ in stages[stage]]
            sent[stage], tok = _pair_send_start(ga, name=f"rs_pair_send_start_{'abc'[stage]}")
            tokens.append(tok)
        return sum(tokens[1:], tokens[0])

    loss, grad_x, GW, GS = _local_step(x, loss_target, params, params, on_grads)

    gs_shapes = [GS[k].shape for k in SMALL_ORDER] + [(1,)]
    gs_pack = _pack([GS[k] for k in SMALL_ORDER] + [loss.reshape(1)], 128, 64)
    rs = gs_pack.shape[0] // 8
    gs_halves = [gs_pack.reshape(4, 2, rs, 128)]
    hf_s, hx_s = _pair_sums(gs_halves, [False], cidx, "s")
    started_s, after = _chip_exchange_start(hx_s, name="rs_chip_exchange_start_s")
    mine = []
    for hf, started, tag in early:
        got = _chip_exchange_wait(started, after, name=f"rs_chip_exchange_wait_{tag}")
        mine += _chip_sums(hf, got, chip_idx, tag)
        after = mine[-1]
    mine_s = _chip_sums(hf_s, _chip_exchange_wait(started_s, after, name="rs_chip_exchange_wait_s"), chip_idx, "s")[0]
    other_s = _pair_swap([mine_s], name="rs_pair_swap_small")[0]
    swapping, tok = _pair_send_start(mine, name="rs_pair_swap_start", whole=True, after=other_s)
    small_halves = jnp.where(core == 0, jnp.concatenate([mine_s, other_s]), jnp.concatenate([other_s, mine_s]))
    small_halves = small_halves + tok[0, 0]
    small_all = _gather([small_halves], chip, name="small_grads_all_gather")[0]
    totals = _unpack(small_all, gs_shapes)
    gsmall = dict(zip(SMALL_ORDER, totals))
    loss = totals[-1][0]

    small_w = {"lam_re": s5_lam_re, "lam_im": s5_lam_im, "log_dt": s5_log_dt, "b_re": s5_b_re, "b_im": s5_b_im,
               "c_re": s5_c_re, "c_im": s5_c_im, "d": s5_d, "rel_bias": rel_bias, "conv_b": ffn_conv_b,
               "b_glu": s5_b_glu, "conv_w": ffn_conv_w, "ln_gain": ln_gain, "ln_bias": ln_bias}
    small_name = {"lam_re": "s5_lam_re", "lam_im": "s5_lam_im", "log_dt": "s5_log_dt", "b_re": "s5_b_re", "b_im": "s5_b_im",
                  "c_re": "s5_c_re", "c_im": "s5_c_im", "d": "s5_d", "rel_bias": "rel_bias", "conv_b": "ffn_conv_b",
                  "b_glu": "s5_b_glu", "conv_w": "ffn_conv_w", "ln_gain": "ln_gain", "ln_bias": "ln_bias"}
    sg = {}
    for k in SMALL_ORDER:
        shp = small_w[k].shape
        g = gsmall[k]
        if k in SMALL_SHARDED:
            width = shp[-1]
            g = lax.dynamic_slice_in_dim(g, chip * width, width, axis=g.ndim - 1)
        sg[k] = g.reshape(shp)
    sd, snm, snv = {}, {}, {}
    for k in SMALL_ORDER:
        shp = small_w[k].shape
        flat = lambda t: t.reshape(-1, shp[-1])
        r3 = _adamw(flat(small_w[k]), flat(sg[k]), flat(m_in[small_name[k]]), flat(v_in[small_name[k]]),
                    name=f"adamw_{k}")
        sd[k], snm[k], snv[k] = (t.reshape(shp) for t in r3)

    mine, other = _pair_send_wait(swapping, [sd[k] for k in SMALL_ORDER], name="rs_pair_swap_wait", whole=True)
    big_res = {}
    for (key, src, layer), gm, go in zip(red, mine, other):
        nl = w_in[src].shape[0] if src in ("ffn_w_up", "ffn_w_down") else 1
        r, c = w_in[src].shape[-2:]
        view = lambda t: t.reshape(nl, 2, r // 2, c)
        res4 = _adamw_halves(view(w_in[src]), view(m_in[src]), view(v_in[src]), gm, go, cidx, layer=layer,
                             prev=big_res.get(src), name=f"adamw_{key}")
        big_res[src] = res4
    big_res = {src: tuple(t.reshape(w_in[src].shape) for t in res4) for src, res4 in big_res.items()}

    def big_out(i):
        return {src: big_res[src][i] for _, src, _ in big}

    res = [{}, {}, {}, {}]
    for i in range(4):
        res[i].update(big_out(i))
    for k in SMALL_ORDER:
        res[0][small_name[k]] = sg[k]
        res[1][small_name[k]] = sd[k]
        res[2][small_name[k]] = snm[k]
        res[3][small_name[k]] = snv[k]
    outs = [loss, grad_x]
    for i in range(4):
        outs += [res[i][n] for n in names]
    return tuple(outs)
```
